```python
import jax, jax.numpy as jnp
from jax import lax
import numpy as np

D_MODEL = 1024
BATCH = 16
SEQ = 2048
DEPTH = 2

CHUNK = 64
N_A_LAYERS = DEPTH // 2
N_B_LAYERS = DEPTH - N_A_LAYERS
GMLP_BLOCK = 128
GMLP_WIDTH = D_MODEL
GMLP_GROUPS = 8
GMLP_GROUP_DIM = GMLP_WIDTH // GMLP_GROUPS
N_HEADS = 16
HEAD_DIM = D_MODEL // N_HEADS
LEFT_CHUNKS = 8
BAND = (LEFT_CHUNKS + 1) * CHUNK
PAD = LEFT_CHUNKS * CHUNK
REL_CLIP = 128
REL_SIZE = 2 * REL_CLIP + 1
D_FF = ((8 * D_MODEL // 3 + 127) // 128) * 128
CONV_WIDTH = 3
EPS = 1e-6
NEG_INF = -1e30

kernel_name = "hybrid_gmlp_chunkattn_yoco_convffn"


def rmsnorm(x, g):
    xf = x.astype(jnp.float32)
    y = xf * lax.rsqrt(jnp.mean(xf * xf, axis=-1, keepdims=True) + EPS)
    return (y * g.astype(jnp.float32)).astype(x.dtype)


def gmlp_mixer(h, w_in, v_norm_g, w_s, b_s, w_out):
    B, S, _ = h.shape
    z = jax.nn.gelu(h @ w_in)
    u, v = jnp.split(z, 2, axis=-1)
    v = rmsnorm(v, v_norm_g)
    pos_chunk = jnp.arange(GMLP_BLOCK) // CHUNK
    mask = pos_chunk[:, None] >= pos_chunk[None, :]
    w = jnp.where(mask[None], w_s, 0)
    v = v.reshape(B, S // GMLP_BLOCK, GMLP_BLOCK, GMLP_GROUPS, GMLP_GROUP_DIM)
    s = jnp.einsum('gij,bnjgc->bnigc', w, v) + b_s.T[None, None, :, :, None]
    out = u * s.reshape(B, S, GMLP_WIDTH)
    return out @ w_out


def chunk_attention(h, k, v, w_q, rel_bias, w_o):
    B, S, _ = h.shape
    nc = S // CHUNK
    scale = HEAD_DIM ** -0.5
    q = (h @ w_q).reshape(B, nc, CHUNK, N_HEADS, HEAD_DIM) * scale
    qc = jnp.moveaxis(q, 1, 0)
    kp = jnp.pad(k, ((0, 0), (PAD, 0), (0, 0), (0, 0)))
    vp = jnp.pad(v, ((0, 0), (PAD, 0), (0, 0), (0, 0)))
    qi = jnp.arange(CHUNK)[:, None]
    kj = jnp.arange(BAND)[None, :]
    rel_idx = jnp.clip(qi - kj + PAD, -REL_CLIP, REL_CLIP) + REL_CLIP
    bias = rel_bias[:, rel_idx].astype(jnp.float32)

    def one_chunk(args):
        c, qb = args
        start = c * CHUNK
        kb = lax.dynamic_slice_in_dim(kp, start, BAND, axis=1)
        vb = lax.dynamic_slice_in_dim(vp, start, BAND, axis=1)
        sc = jnp.einsum('bqhd,bkhd->bhqk', qb, kb).astype(jnp.float32) + bias
        valid = (start - PAD + jnp.arange(BAND)) >= 0
        sc = jnp.where(valid[None, None, None, :], sc, NEG_INF)
        p = jax.nn.softmax(sc, axis=-1).astype(vb.dtype)
        return jnp.einsum('bhqk,bkhd->bqhd', p, vb)

    o = lax.map(one_chunk, (jnp.arange(nc), qc))
    o = jnp.moveaxis(o, 0, 1).reshape(B, S, N_HEADS * HEAD_DIM)
    return o @ w_o


def conv_ffn(h, w_in, conv_w, conv_b, w_down):
    a = h @ w_in
    C = a.shape[-1]
    a = lax.conv_general_dilated(
        a, conv_w[:, None, :].astype(a.dtype), window_strides=(1,),
        padding=[(CONV_WIDTH - 1, 0)], dimension_numbers=('NWC', 'WIO', 'NWC'),
        feature_group_count=C) + conv_b
    up, gate = jnp.split(a, 2, axis=-1)
    return (jax.nn.silu(gate) * up) @ w_down


def _fwd_setup_inputs(seed: int = 0) -> dict:
    key = jax.random.key(seed)
    ks = jax.random.split(key, 20)
    nrm = lambda k, shape, s: jax.random.normal(k, shape, jnp.float32) * s
    gain = lambda k, shape: 1.0 + nrm(k, shape, 0.02)
    HD = N_HEADS * HEAD_DIM
    return {
        "x": nrm(ks[0], (BATCH, SEQ, D_MODEL), 1.0),
        "a_norm_g": gain(ks[1], (N_A_LAYERS, D_MODEL)),
        "a_w_in": nrm(ks[2], (N_A_LAYERS, D_MODEL, 2 * GMLP_WIDTH), D_MODEL ** -0.5),
        "a_v_norm_g": gain(ks[3], (N_A_LAYERS, GMLP_WIDTH)),
        "a_w_s": nrm(ks[4], (N_A_LAYERS, GMLP_GROUPS, GMLP_BLOCK, GMLP_BLOCK), GMLP_BLOCK ** -0.5),
        "a_b_s": 1.0 + nrm(ks[5], (N_A_LAYERS, GMLP_GROUPS, GMLP_BLOCK), 0.01),
        "a_w_out": nrm(ks[6], (N_A_LAYERS, GMLP_WIDTH, D_MODEL), GMLP_WIDTH ** -0.5),
        "kv_norm_g": gain(ks[7], (D_MODEL,)),
        "w_kv": nrm(ks[8], (D_MODEL, 2 * HD), D_MODEL ** -0.5),
        "b_norm_g": gain(ks[9], (N_B_LAYERS, D_MODEL)),
        "b_w_q": nrm(ks[10], (N_B_LAYERS, D_MODEL, HD), D_MODEL ** -0.5),
        "b_rel_bias": nrm(ks[11], (N_B_LAYERS, N_HEADS, REL_SIZE), 0.5),
        "b_w_o": nrm(ks[12], (N_B_LAYERS, HD, D_MODEL), HD ** -0.5),
        "f_norm_g": gain(ks[13], (DEPTH, D_MODEL)),
        "f_w_in": nrm(ks[14], (DEPTH, D_MODEL, 2 * D_FF), D_MODEL ** -0.5),
        "f_conv_w": nrm(ks[15], (DEPTH, CONV_WIDTH, 2 * D_FF), CONV_WIDTH ** -0.5),
        "f_conv_b": nrm(ks[16], (DEPTH, 2 * D_FF), 0.01),
        "f_w_down": nrm(ks[17], (DEPTH, D_FF, D_MODEL), D_FF ** -0.5),
        "final_norm_g": gain(ks[18], (D_MODEL,)),
    }


def _fwd_reference(x, a_norm_g, a_w_in, a_v_norm_g, a_w_s, a_b_s, a_w_out,
              kv_norm_g, w_kv, b_norm_g, b_w_q, b_rel_bias, b_w_o,
              f_norm_g, f_w_in, f_conv_w, f_conv_b, f_w_down, final_norm_g):
    B, S, _ = x.shape
    h = x
    k_shared = v_shared = None
    for l in range(DEPTH):
        if l < N_A_LAYERS:
            h = h + gmlp_mixer(rmsnorm(h, a_norm_g[l]), a_w_in[l], a_v_norm_g[l],
                               a_w_s[l], a_b_s[l], a_w_out[l])
        else:
            if l == N_A_LAYERS:
                kv = rmsnorm(h, kv_norm_g) @ w_kv
                k_shared, v_shared = jnp.split(kv, 2, axis=-1)
                k_shared = k_shared.reshape(B, S, N_HEADS, HEAD_DIM)
                v_shared = v_shared.reshape(B, S, N_HEADS, HEAD_DIM)
            j = l - N_A_LAYERS
            h = h + chunk_attention(rmsnorm(h, b_norm_g[j]), k_shared, v_shared,
                                    b_w_q[j], b_rel_bias[j], b_w_o[j])
        h = h + conv_ffn(rmsnorm(h, f_norm_g[l]), f_w_in[l], f_conv_w[l],
                         f_conv_b[l], f_w_down[l])
    return rmsnorm(h, final_norm_g)


import jax as _jax
import jax.numpy as _jnp

TWIN_FORMAT = 'train_step'
FWD_PARAMS = ['x', 'a_norm_g', 'a_w_in', 'a_v_norm_g', 'a_w_s', 'a_b_s', 'a_w_out', 'kv_norm_g', 'w_kv', 'b_norm_g', 'b_w_q', 'b_rel_bias', 'b_w_o', 'f_norm_g', 'f_w_in', 'f_conv_w', 'f_conv_b', 'f_w_down', 'final_norm_g']
TWIN_WEIGHTS = ['a_norm_g', 'a_w_in', 'a_v_norm_g', 'a_w_s', 'a_b_s', 'a_w_out', 'kv_norm_g', 'w_kv', 'b_norm_g', 'b_w_q', 'b_rel_bias', 'b_w_o', 'f_norm_g', 'f_w_in', 'f_conv_w', 'f_conv_b', 'f_w_down', 'final_norm_g']
TWIN_DIFF_INPUT = 'x'
TWIN_INPUTS = ['x', 'a_norm_g', 'a_w_in', 'a_v_norm_g', 'a_w_s', 'a_b_s', 'a_w_out', 'kv_norm_g', 'w_kv', 'b_norm_g', 'b_w_q', 'b_rel_bias', 'b_w_o', 'f_norm_g', 'f_w_in', 'f_conv_w', 'f_conv_b', 'f_w_down', 'final_norm_g', 'loss_target', 'm_a_norm_g', 'm_a_w_in', 'm_a_v_norm_g', 'm_a_w_s', 'm_a_b_s', 'm_a_w_out', 'm_kv_norm_g', 'm_w_kv', 'm_b_norm_g', 'm_b_w_q', 'm_b_rel_bias', 'm_b_w_o', 'm_f_norm_g', 'm_f_w_in', 'm_f_conv_w', 'm_f_conv_b', 'm_f_w_down', 'm_final_norm_g', 'v_a_norm_g', 'v_a_w_in', 'v_a_v_norm_g', 'v_a_w_s', 'v_a_b_s', 'v_a_w_out', 'v_kv_norm_g', 'v_w_kv', 'v_b_norm_g', 'v_b_w_q', 'v_b_rel_bias', 'v_b_w_o', 'v_f_norm_g', 'v_f_w_in', 'v_f_conv_w', 'v_f_conv_b', 'v_f_w_down', 'v_final_norm_g']
TWIN_OUTPUTS = ['loss', 'grad_x', 'grad_a_norm_g', 'grad_a_w_in', 'grad_a_v_norm_g', 'grad_a_w_s', 'grad_a_b_s', 'grad_a_w_out', 'grad_kv_norm_g', 'grad_w_kv', 'grad_b_norm_g', 'grad_b_w_q', 'grad_b_rel_bias', 'grad_b_w_o', 'grad_f_norm_g', 'grad_f_w_in', 'grad_f_conv_w', 'grad_f_conv_b', 'grad_f_w_down', 'grad_final_norm_g', 'delta_a_norm_g', 'delta_a_w_in', 'delta_a_v_norm_g', 'delta_a_w_s', 'delta_a_b_s', 'delta_a_w_out', 'delta_kv_norm_g', 'delta_w_kv', 'delta_b_norm_g', 'delta_b_w_q', 'delta_b_rel_bias', 'delta_b_w_o', 'delta_f_norm_g', 'delta_f_w_in', 'delta_f_conv_w', 'delta_f_conv_b', 'delta_f_w_down', 'delta_final_norm_g', 'new_m_a_norm_g', 'new_m_a_w_in', 'new_m_a_v_norm_g', 'new_m_a_w_s', 'new_m_a_b_s', 'new_m_a_w_out', 'new_m_kv_norm_g', 'new_m_w_kv', 'new_m_b_norm_g', 'new_m_b_w_q', 'new_m_b_rel_bias', 'new_m_b_w_o', 'new_m_f_norm_g', 'new_m_f_w_in', 'new_m_f_conv_w', 'new_m_f_conv_b', 'new_m_f_w_down', 'new_m_final_norm_g', 'new_v_a_norm_g', 'new_v_a_w_in', 'new_v_a_v_norm_g', 'new_v_a_w_s', 'new_v_a_b_s', 'new_v_a_w_out', 'new_v_kv_norm_g', 'new_v_w_kv', 'new_v_b_norm_g', 'new_v_b_w_q', 'new_v_b_rel_bias', 'new_v_b_w_o', 'new_v_f_norm_g', 'new_v_f_w_in', 'new_v_f_conv_w', 'new_v_f_conv_b', 'new_v_f_w_down', 'new_v_final_norm_g']
TWIN_LEAF_KINDS = {'loss': 'loss', 'grad_x': 'grad_x', 'grad_a_norm_g': 'grad_w', 'grad_a_w_in': 'grad_w', 'grad_a_v_norm_g': 'grad_w', 'grad_a_w_s': 'grad_w', 'grad_a_b_s': 'grad_w', 'grad_a_w_out': 'grad_w', 'grad_kv_norm_g': 'grad_w', 'grad_w_kv': 'grad_w', 'grad_b_norm_g': 'grad_w', 'grad_b_w_q': 'grad_w', 'grad_b_rel_bias': 'grad_w', 'grad_b_w_o': 'grad_w', 'grad_f_norm_g': 'grad_w', 'grad_f_w_in': 'grad_w', 'grad_f_conv_w': 'grad_w', 'grad_f_conv_b': 'grad_w', 'grad_f_w_down': 'grad_w', 'grad_final_norm_g': 'grad_w', 'delta_a_norm_g': 'delta_w', 'delta_a_w_in': 'delta_w', 'delta_a_v_norm_g': 'delta_w', 'delta_a_w_s': 'delta_w', 'delta_a_b_s': 'delta_w', 'delta_a_w_out': 'delta_w', 'delta_kv_norm_g': 'delta_w', 'delta_w_kv': 'delta_w', 'delta_b_norm_g': 'delta_w', 'delta_b_w_q': 'delta_w', 'delta_b_rel_bias': 'delta_w', 'delta_b_w_o': 'delta_w', 'delta_f_norm_g': 'delta_w', 'delta_f_w_in': 'delta_w', 'delta_f_conv_w': 'delta_w', 'delta_f_conv_b': 'delta_w', 'delta_f_w_down': 'delta_w', 'delta_final_norm_g': 'delta_w', 'new_m_a_norm_g': 'new_m', 'new_m_a_w_in': 'new_m', 'new_m_a_v_norm_g': 'new_m', 'new_m_a_w_s': 'new_m', 'new_m_a_b_s': 'new_m', 'new_m_a_w_out': 'new_m', 'new_m_kv_norm_g': 'new_m', 'new_m_w_kv': 'new_m', 'new_m_b_norm_g': 'new_m', 'new_m_b_w_q': 'new_m', 'new_m_b_rel_bias': 'new_m', 'new_m_b_w_o': 'new_m', 'new_m_f_norm_g': 'new_m', 'new_m_f_w_in': 'new_m', 'new_m_f_conv_w': 'new_m', 'new_m_f_conv_b': 'new_m', 'new_m_f_w_down': 'new_m', 'new_m_final_norm_g': 'new_m', 'new_v_a_norm_g': 'new_v', 'new_v_a_w_in': 'new_v', 'new_v_a_v_norm_g': 'new_v', 'new_v_a_w_s': 'new_v', 'new_v_a_b_s': 'new_v', 'new_v_a_w_out': 'new_v', 'new_v_kv_norm_g': 'new_v', 'new_v_w_kv': 'new_v', 'new_v_b_norm_g': 'new_v', 'new_v_b_w_q': 'new_v', 'new_v_b_rel_bias': 'new_v', 'new_v_b_w_o': 'new_v', 'new_v_f_norm_g': 'new_v', 'new_v_f_w_in': 'new_v', 'new_v_f_conv_w': 'new_v', 'new_v_f_conv_b': 'new_v', 'new_v_f_w_down': 'new_v', 'new_v_final_norm_g': 'new_v'}


def _forward(args):
    return _fwd_reference(*[args[k] for k in FWD_PARAMS])


def _output_shape():
    out = _jax.eval_shape(lambda: _forward(_fwd_setup_inputs(0)))
    return out.shape, out.dtype

N_MICROBATCH = 1
ADAM_LR = 0.001
ADAM_B1 = 0.9
ADAM_B2 = 0.999
ADAM_EPS = 1e-08
ADAM_WD = 0.01
ADAM_STEP = 10
PER_EXAMPLE_BATCH_AXIS = {'x': 0, 'loss_target': 0}
SHARED_INPUTS = []
_WEIGHT_DTYPES = {'a_norm_g': _jnp.float32, 'a_w_in': _jnp.float32, 'a_v_norm_g': _jnp.float32, 'a_w_s': _jnp.float32, 'a_b_s': _jnp.float32, 'a_w_out': _jnp.float32, 'kv_norm_g': _jnp.float32, 'w_kv': _jnp.float32, 'b_norm_g': _jnp.float32, 'b_w_q': _jnp.float32, 'b_rel_bias': _jnp.float32, 'b_w_o': _jnp.float32, 'f_norm_g': _jnp.float32, 'f_w_in': _jnp.float32, 'f_conv_w': _jnp.float32, 'f_conv_b': _jnp.float32, 'f_w_down': _jnp.float32, 'final_norm_g': _jnp.float32}
MOMENT_SCALE = {'a_norm_g': 1.865145e-01, 'a_w_in': 1.179092e-01, 'a_v_norm_g': 8.726923e-02, 'a_w_s': 8.729403e-02, 'a_b_s': 1.023345e-01, 'a_w_out': 1.436196e-01, 'kv_norm_g': 4.138902e-02, 'w_kv': 2.822940e-02, 'b_norm_g': 1.822273e-02, 'b_w_q': 1.853727e-02, 'b_rel_bias': 1.001365e-02, 'b_w_o': 3.637827e-02, 'f_norm_g': 1.106663e-01, 'f_w_in': 4.563683e-02, 'f_conv_w': 4.487603e-02, 'f_conv_b': 4.845484e-02, 'f_w_down': 7.465428e-02, 'final_norm_g': 3.208656e+01}


def _to_microbatches(a, axis):
    t = _jnp.moveaxis(a, axis, 0)
    t = t.reshape((N_MICROBATCH, t.shape[0] // N_MICROBATCH) + t.shape[1:])
    return _jnp.moveaxis(t, 1, axis + 1)


def setup_inputs(seed: int = 0) -> dict:
    inp = _fwd_setup_inputs(seed)
    key = _jax.random.fold_in(_jax.random.key(seed), 7919)
    shape, _ = _output_shape()
    out = dict(inp)
    out["loss_target"] = _jax.random.normal(_jax.random.fold_in(key, 0), shape, _jnp.float32)
    for i, name in enumerate(TWIN_WEIGHTS):
        w = inp[name].astype(_jnp.float32)
        if MOMENT_SCALE is None:
            s = _jnp.sqrt(_jnp.mean(_jnp.square(w)) + 1e-30)
        else:
            s = MOMENT_SCALE[name]
        km, kv = _jax.random.split(_jax.random.fold_in(key, i + 1))
        out[name] = w
        out["m_" + name] = s * _jax.random.normal(km, w.shape, _jnp.float32)
        out["v_" + name] = (s * s) * _jax.random.uniform(kv, w.shape, _jnp.float32, 0.5, 1.5)
    if N_MICROBATCH > 1:
        for name, axis in PER_EXAMPLE_BATCH_AXIS.items():
            out[name] = _to_microbatches(out[name], axis)
    return {'x': out['x'], 'a_norm_g': out['a_norm_g'], 'a_w_in': out['a_w_in'], 'a_v_norm_g': out['a_v_norm_g'], 'a_w_s': out['a_w_s'], 'a_b_s': out['a_b_s'], 'a_w_out': out['a_w_out'], 'kv_norm_g': out['kv_norm_g'], 'w_kv': out['w_kv'], 'b_norm_g': out['b_norm_g'], 'b_w_q': out['b_w_q'], 'b_rel_bias': out['b_rel_bias'], 'b_w_o': out['b_w_o'], 'f_norm_g': out['f_norm_g'], 'f_w_in': out['f_w_in'], 'f_conv_w': out['f_conv_w'], 'f_conv_b': out['f_conv_b'], 'f_w_down': out['f_w_down'], 'final_norm_g': out['final_norm_g'], 'loss_target': out['loss_target'], 'm_a_norm_g': out['m_a_norm_g'], 'm_a_w_in': out['m_a_w_in'], 'm_a_v_norm_g': out['m_a_v_norm_g'], 'm_a_w_s': out['m_a_w_s'], 'm_a_b_s': out['m_a_b_s'], 'm_a_w_out': out['m_a_w_out'], 'm_kv_norm_g': out['m_kv_norm_g'], 'm_w_kv': out['m_w_kv'], 'm_b_norm_g': out['m_b_norm_g'], 'm_b_w_q': out['m_b_w_q'], 'm_b_rel_bias': out['m_b_rel_bias'], 'm_b_w_o': out['m_b_w_o'], 'm_f_norm_g': out['m_f_norm_g'], 'm_f_w_in': out['m_f_w_in'], 'm_f_conv_w': out['m_f_conv_w'], 'm_f_conv_b': out['m_f_conv_b'], 'm_f_w_down': out['m_f_w_down'], 'm_final_norm_g': out['m_final_norm_g'], 'v_a_norm_g': out['v_a_norm_g'], 'v_a_w_in': out['v_a_w_in'], 'v_a_v_norm_g': out['v_a_v_norm_g'], 'v_a_w_s': out['v_a_w_s'], 'v_a_b_s': out['v_a_b_s'], 'v_a_w_out': out['v_a_w_out'], 'v_kv_norm_g': out['v_kv_norm_g'], 'v_w_kv': out['v_w_kv'], 'v_b_norm_g': out['v_b_norm_g'], 'v_b_w_q': out['v_b_w_q'], 'v_b_rel_bias': out['v_b_rel_bias'], 'v_b_w_o': out['v_b_w_o'], 'v_f_norm_g': out['v_f_norm_g'], 'v_f_w_in': out['v_f_w_in'], 'v_f_conv_w': out['v_f_conv_w'], 'v_f_conv_b': out['v_f_conv_b'], 'v_f_w_down': out['v_f_w_down'], 'v_final_norm_g': out['v_final_norm_g']}


def _loss(weights, diff, rest, loss_target):
    with _jax.named_scope("forward"):
        args = {**rest, TWIN_DIFF_INPUT: diff, **{k: w.astype(_WEIGHT_DTYPES[k]) for k, w in weights.items()}}
        y = _forward(args)
    with _jax.named_scope("loss_head"):
        err = _jnp.square(y.astype(_jnp.float32) - loss_target)
        return 0.5 * _jnp.sum(_jnp.mean(err, axis=-1)) if err.ndim else 0.5 * err


def _adamw(w, g, m, v):
    m = ADAM_B1 * m + (1.0 - ADAM_B1) * g
    v = ADAM_B2 * v + (1.0 - ADAM_B2) * _jnp.square(g)
    m_hat = m / (1.0 - ADAM_B1 ** ADAM_STEP)
    v_hat = v / (1.0 - ADAM_B2 ** ADAM_STEP)
    delta = -ADAM_LR * (m_hat / (_jnp.sqrt(v_hat) + ADAM_EPS) + ADAM_WD * w)
    return delta, m, v


def reference(x, a_norm_g, a_w_in, a_v_norm_g, a_w_s, a_b_s, a_w_out, kv_norm_g, w_kv, b_norm_g, b_w_q, b_rel_bias, b_w_o, f_norm_g, f_w_in, f_conv_w, f_conv_b, f_w_down, final_norm_g, loss_target, m_a_norm_g, m_a_w_in, m_a_v_norm_g, m_a_w_s, m_a_b_s, m_a_w_out, m_kv_norm_g, m_w_kv, m_b_norm_g, m_b_w_q, m_b_rel_bias, m_b_w_o, m_f_norm_g, m_f_w_in, m_f_conv_w, m_f_conv_b, m_f_w_down, m_final_norm_g, v_a_norm_g, v_a_w_in, v_a_v_norm_g, v_a_w_s, v_a_b_s, v_a_w_out, v_kv_norm_g, v_w_kv, v_b_norm_g, v_b_w_q, v_b_rel_bias, v_b_w_o, v_f_norm_g, v_f_w_in, v_f_conv_w, v_f_conv_b, v_f_w_down, v_final_norm_g):
    given = dict(x=x, a_norm_g=a_norm_g, a_w_in=a_w_in, a_v_norm_g=a_v_norm_g, a_w_s=a_w_s, a_b_s=a_b_s, a_w_out=a_w_out, kv_norm_g=kv_norm_g, w_kv=w_kv, b_norm_g=b_norm_g, b_w_q=b_w_q, b_rel_bias=b_rel_bias, b_w_o=b_w_o, f_norm_g=f_norm_g, f_w_in=f_w_in, f_conv_w=f_conv_w, f_conv_b=f_conv_b, f_w_down=f_w_down, final_norm_g=final_norm_g, loss_target=loss_target, m_a_norm_g=m_a_norm_g, m_a_w_in=m_a_w_in, m_a_v_norm_g=m_a_v_norm_g, m_a_w_s=m_a_w_s, m_a_b_s=m_a_b_s, m_a_w_out=m_a_w_out, m_kv_norm_g=m_kv_norm_g, m_w_kv=m_w_kv, m_b_norm_g=m_b_norm_g, m_b_w_q=m_b_w_q, m_b_rel_bias=m_b_rel_bias, m_b_w_o=m_b_w_o, m_f_norm_g=m_f_norm_g, m_f_w_in=m_f_w_in, m_f_conv_w=m_f_conv_w, m_f_conv_b=m_f_conv_b, m_f_w_down=m_f_w_down, m_final_norm_g=m_final_norm_g, v_a_norm_g=v_a_norm_g, v_a_w_in=v_a_w_in, v_a_v_norm_g=v_a_v_norm_g, v_a_w_s=v_a_w_s, v_a_b_s=v_a_b_s, v_a_w_out=v_a_w_out, v_kv_norm_g=v_kv_norm_g, v_w_kv=v_w_kv, v_b_norm_g=v_b_norm_g, v_b_w_q=v_b_w_q, v_b_rel_bias=v_b_rel_bias, v_b_w_o=v_b_w_o, v_f_norm_g=v_f_norm_g, v_f_w_in=v_f_w_in, v_f_conv_w=v_f_conv_w, v_f_conv_b=v_f_conv_b, v_f_w_down=v_f_w_down, v_final_norm_g=v_final_norm_g)
    weights = {n: given[n] for n in TWIN_WEIGHTS}
    shared = {n: given[n] for n in SHARED_INPUTS}
    per_example = {n: given[n] for n in ['x']}
    grad_fn = _jax.value_and_grad(_loss, argnums=(0, 1))

    def one_microbatch(ex, loss_target):
        ex = dict(ex)
        diff = ex.pop(TWIN_DIFF_INPUT)
        return grad_fn(weights, diff, {**shared, **ex}, loss_target)

    if N_MICROBATCH == 1:
        loss, (grad_w, grad_x) = one_microbatch(per_example, given["loss_target"])
    else:
        def body(carry, xs):
            loss_sum, grad_sum = carry
            l_k, (gw_k, gx_k) = one_microbatch(xs[0], xs[1])
            with _jax.named_scope("update"):
                return (loss_sum + l_k, _jax.tree.map(_jnp.add, grad_sum, gw_k)), gx_k

        init = (_jnp.zeros((), _jnp.float32), _jax.tree.map(_jnp.zeros_like, weights))
        (loss, grad_w), grad_x = _jax.lax.scan(body, init, (per_example, given["loss_target"]))
    with _jax.named_scope("update"):
        delta_w, new_m, new_v = {}, {}, {}
        for n in TWIN_WEIGHTS:
            delta_w[n], new_m[n], new_v[n] = _adamw(weights[n], grad_w[n], given["m_" + n], given["v_" + n])
    return (loss, grad_x, *[grad_w[n] for n in TWIN_WEIGHTS], *[delta_w[n] for n in TWIN_WEIGHTS],
            *[new_m[n] for n in TWIN_WEIGHTS], *[new_v[n] for n in TWIN_WEIGHTS])
```

```python
import functools
import math

import numpy as np
import jax
import jax.numpy as jnp
from jax import lax
from jax.experimental import pallas as pl
from jax.experimental.pallas import tpu as pltpu

F32 = jnp.float32
BF16 = jnp.bfloat16
MESH = pl.DeviceIdType.MESH

EPS = 1e-6
NEG_INF = -1e30
CHUNK = 64
GMLP_BLOCK = 128
GROUP_DIM = 128
HEAD_DIM = 64
LEFT_CHUNKS = 8
PAD = LEFT_CHUNKS * CHUNK
REL_CLIP = 128
Q_BLOCK = 128
K_SPAN = PAD + Q_BLOCK
F_LEN = 768
HEADS_PER_STEP = 4
N_CHIPS = 4

ADAM_LR = 0.001
ADAM_B1 = 0.9
ADAM_B2 = 0.999
ADAM_EPS = 1e-08
ADAM_WD = 0.01
ADAM_STEP = 10

VMEM_LIMIT = 56 * 1024 * 1024


def _params(sem=None, **kw):
    if sem is not None:
        kw["dimension_semantics"] = sem
    return pltpu.CompilerParams(vmem_limit_bytes=VMEM_LIMIT, **kw)


def _rms(xf, g):
    r = lax.rsqrt(jnp.mean(xf * xf, axis=-1, keepdims=True) + EPS)
    return xf * r, r


def _gelu(x):
    c = math.sqrt(2.0 / math.pi)
    return 0.5 * x * (1.0 + jnp.tanh(c * (x + 0.044715 * x * x * x)))


def _gelu_grad(x):
    c = math.sqrt(2.0 / math.pi)
    t = jnp.tanh(c * (x + 0.044715 * x * x * x))
    return 0.5 * (1.0 + t) + 0.5 * x * (1.0 - t * t) * c * (1.0 + 3.0 * 0.044715 * x * x)


def _col_tile(n):
    if n <= 1024:
        return n
    for t in (1408, 1024, 512):
        if n % t == 0:
            return t
    raise ValueError(n)


def _row_tile(t, want):
    while t % want:
        want //= 2
    return want


def _mm(x, w, *, name, trans_w=False, norm_g=None, res=None, scale=None, out_dtype=F32, bwd=None, tm=512):
    T, K = x.shape
    N = w.shape[0] if trans_w else w.shape[1]
    tn = N if bwd is not None else _col_tile(N)
    tm = _row_tile(T, tm)
    nn, nm = N // tn, T // tm
    has_norm, has_res, has_bwd = norm_g is not None, res is not None, bwd is not None

    def body(*refs):
        it = iter(refs)
        x_ref, w_ref = next(it), next(it)
        g_ref = next(it) if has_norm else None
        res_ref = next(it) if has_res else None
        if has_bwd:
            h_ref, bg_ref, dh_ref = next(it), next(it), next(it)
        o_ref = next(it)
        xv = x_ref[...]
        if has_norm:
            xv = _rms(xv.astype(F32), None)[0] * g_ref[...]
        xb = xv.astype(BF16)
        if trans_w:
            acc = lax.dot_general(xb, w_ref[...], (((1,), (1,)), ((), ())), preferred_element_type=F32)
        else:
            acc = jnp.dot(xb, w_ref[...], preferred_element_type=F32)
        if scale is not None:
            acc = acc * scale
        if has_res:
            acc = acc + res_ref[...]
        if has_bwd:
            dg_ref = next(it)
            n, r = _rms(h_ref[...], None)

            @pl.when(pl.program_id(1) == 0)
            def _():
                dg_ref[...] = jnp.zeros_like(dg_ref)

            dg_ref[...] += jnp.sum(acc * n, axis=0, keepdims=True)
            t = acc * bg_ref[...]
            o_ref[...] = dh_ref[...] + r * (t - n * jnp.mean(t * n, axis=-1, keepdims=True))
        else:
            o_ref[...] = acc.astype(out_dtype)

    ins = [x, w]
    in_specs = [pl.BlockSpec((tm, K), lambda n, m: (m, 0)),
                pl.BlockSpec((tn, K), lambda n, m: (n, 0)) if trans_w else pl.BlockSpec((K, tn), lambda n, m: (0, n))]
    if has_norm:
        ins.append(norm_g.reshape(1, K))
        in_specs.append(pl.BlockSpec((1, K), lambda n, m: (0, 0)))
    if has_res:
        ins.append(res)
        in_specs.append(pl.BlockSpec((tm, tn), lambda n, m: (m, n)))
    out_shape = [jax.ShapeDtypeStruct((T, N), F32 if has_bwd else out_dtype)]
    out_specs = [pl.BlockSpec((tm, tn), lambda n, m: (m, n))]
    if has_bwd:
        h, g, dh = bwd
        ins += [h, g.reshape(1, N), dh]
        in_specs += [pl.BlockSpec((tm, N), lambda n, m: (m, 0)), pl.BlockSpec((1, N), lambda n, m: (0, 0)),
                     pl.BlockSpec((tm, N), lambda n, m: (m, 0))]
        out_shape.append(jax.ShapeDtypeStruct((1, N), F32))
        out_specs.append(pl.BlockSpec((1, N), lambda n, m: (0, 0)))
    out = pl.pallas_call(body, name=name, grid=(nn, nm), in_specs=in_specs, out_specs=out_specs, out_shape=out_shape,
                         compiler_params=_params(("arbitrary", "arbitrary")))(*ins)
    return out if has_bwd else out[0]


def _mm_tn(x, dy, *, name, norm_g=None, x_cols=None, y_cols=None, rows_are_shards=False, tt=512):
    T, K = x.shape
    xs, kx = x_cols if x_cols is not None else (0, K)
    ys, ny = y_cols if y_cols is not None else (0, dy.shape[1])
    if rows_are_shards:
        tn = ny if ny <= 512 else 512
        R, C = kx // N_CHIPS, ny
    else:
        tn = ny // N_CHIPS
        R, C = kx, tn
    nn = ny // tn
    tt = _row_tile(T, tt)
    nt = T // tt
    y0 = ys // tn
    has_norm = norm_g is not None

    def body(*refs):
        it = iter(refs)
        x_ref, y_ref = next(it), next(it)
        g_ref = next(it) if has_norm else None
        o_ref, acc_ref = next(it), next(it)
        t = pl.program_id(1)

        @pl.when(t == 0)
        def _():
            acc_ref[...] = jnp.zeros_like(acc_ref)

        xv = x_ref[...]
        if has_norm:
            xv = _rms(xv.astype(F32), None)[0] * g_ref[...]
        xb = xv.astype(BF16)
        if kx != K:
            xb = xb[:, xs:xs + kx]
        acc_ref[...] += lax.dot_general(xb, y_ref[...].astype(BF16), (((0,), (0,)), ((), ())),
                                        preferred_element_type=F32)

        @pl.when(t == nt - 1)
        def _():
            a = acc_ref[...].astype(BF16)
            o_ref[...] = a.reshape(N_CHIPS, R, tn) if rows_are_shards else a

    ins = [x, dy]
    in_specs = [pl.BlockSpec((tt, K), lambda n, t: (t, 0)), pl.BlockSpec((tt, tn), lambda n, t: (t, y0 + n))]
    if has_norm:
        ins.append(norm_g.reshape(1, K))
        in_specs.append(pl.BlockSpec((1, K), lambda n, t: (0, 0)))
    if rows_are_shards:
        out_spec = pl.BlockSpec((N_CHIPS, R, tn), lambda n, t: (0, 0, n))
    else:
        out_spec = pl.BlockSpec((None, R, C), lambda n, t: (n, 0, 0))
    return pl.pallas_call(body, name=name, grid=(nn, nt), in_specs=in_specs, out_specs=out_spec,
                          out_shape=jax.ShapeDtypeStruct((N_CHIPS, R, C), BF16),
                          scratch_shapes=[pltpu.VMEM((kx, tn), F32)],
                          compiler_params=_params(("arbitrary", "arbitrary")))(*ins)


def _chunk_mask():
    i = lax.broadcasted_iota(jnp.int32, (GMLP_BLOCK, GMLP_BLOCK), 0) // CHUNK
    j = lax.broadcasted_iota(jnp.int32, (GMLP_BLOCK, GMLP_BLOCK), 1) // CHUNK
    return i >= j


def _gate_fwd(zp, gv, ws, bs_tile, *, tm=256):
    T, W2 = zp.shape
    W = W2 // 2
    G = W // GROUP_DIM
    tm = _row_tile(T, tm)

    def body(zp_ref, gv_ref, ws_ref, bs_ref, o_ref):
        z = _gelu(zp_ref[...])
        u, v = z[:, :W], z[:, W:]
        vn = _rms(v, None)[0] * gv_ref[...]
        mask = _chunk_mask()
        for g in range(G):
            cs = slice(g * GROUP_DIM, (g + 1) * GROUP_DIM)
            wg = jnp.where(mask, ws_ref[g], 0.0).astype(BF16)
            for b in range(tm // GMLP_BLOCK):
                rs = slice(b * GMLP_BLOCK, (b + 1) * GMLP_BLOCK)
                s = jnp.dot(wg, vn[rs, cs].astype(BF16), preferred_element_type=F32) + bs_ref[:, cs]
                o_ref[rs, cs] = (u[rs, cs] * s).astype(BF16)

    return pl.pallas_call(
        body, name="gate_fwd", grid=(T // tm,),
        in_specs=[pl.BlockSpec((tm, W2), lambda i: (i, 0)), pl.BlockSpec((1, W), lambda i: (0, 0)),
                  pl.BlockSpec((G, GMLP_BLOCK, GMLP_BLOCK), lambda i: (0, 0, 0)),
                  pl.BlockSpec((GMLP_BLOCK, W), lambda i: (0, 0))],
        out_specs=pl.BlockSpec((tm, W), lambda i: (i, 0)), out_shape=jax.ShapeDtypeStruct((T, W), BF16),
        compiler_params=_params(("arbitrary",)))(zp, gv, ws, bs_tile)


def _gate_bwd(zp, d_out, gv, ws, bs_tile, *, tm=256):
    T, W2 = zp.shape
    W = W2 // 2
    G = W // GROUP_DIM
    tm = _row_tile(T, tm)
    nm = T // tm

    def body(zp_ref, do_ref, gv_ref, ws_ref, bs_ref, dzp_ref, dws_ref, dbs_ref, dgv_ref, du_scr, dvn_scr, dsum_scr):
        i = pl.program_id(0)

        @pl.when(i == 0)
        def _():
            dws_ref[...] = jnp.zeros_like(dws_ref)
            dgv_ref[...] = jnp.zeros_like(dgv_ref)
            dsum_scr[...] = jnp.zeros_like(dsum_scr)

        zp = zp_ref[...]
        z = _gelu(zp)
        u, v = z[:, :W], z[:, W:]
        n, r = _rms(v, None)
        gv = gv_ref[...]
        vn = n * gv
        d_out = do_ref[...].astype(F32)
        mask = _chunk_mask()
        for g in range(G):
            cs = slice(g * GROUP_DIM, (g + 1) * GROUP_DIM)
            wg = jnp.where(mask, ws_ref[g], 0.0).astype(BF16)
            dw = jnp.zeros((GMLP_BLOCK, GMLP_BLOCK), F32)
            for b in range(tm // GMLP_BLOCK):
                rs = slice(b * GMLP_BLOCK, (b + 1) * GMLP_BLOCK)
                vb = vn[rs, cs].astype(BF16)
                s = jnp.dot(wg, vb, preferred_element_type=F32) + bs_ref[:, cs]
                du_scr[rs, cs] = d_out[rs, cs] * s
                ds = d_out[rs, cs] * u[rs, cs]
                dsb = ds.astype(BF16)
                dvn_scr[rs, cs] = lax.dot_general(wg, dsb, (((0,), (0,)), ((), ())), preferred_element_type=F32)
                dw = dw + lax.dot_general(dsb, vb, (((1,), (1,)), ((), ())), preferred_element_type=F32)
                dsum_scr[:, cs] += ds
            dws_ref[g] += jnp.where(mask, dw, 0.0)
        dvn = dvn_scr[...]
        dgv_ref[...] += jnp.sum(dvn * n, axis=0, keepdims=True)
        t = dvn * gv
        dv = r * (t - n * jnp.mean(t * n, axis=-1, keepdims=True))
        dzp_ref[:, :W] = (du_scr[...] * _gelu_grad(zp[:, :W])).astype(BF16)
        dzp_ref[:, W:] = (dv * _gelu_grad(zp[:, W:])).astype(BF16)

        @pl.when(i == nm - 1)
        def _():
            sel = (lax.broadcasted_iota(jnp.int32, (G, W), 1) // GROUP_DIM
                   == lax.broadcasted_iota(jnp.int32, (G, W), 0)).astype(F32)
            dbs_ref[...] = lax.dot_general(sel, dsum_scr[...], (((1,), (1,)), ((), ())),
                                           precision=lax.Precision.HIGHEST, preferred_element_type=F32)

    return pl.pallas_call(
        body, name="gate_bwd", grid=(nm,),
        in_specs=[pl.BlockSpec((tm, W2), lambda i: (i, 0)), pl.BlockSpec((tm, W), lambda i: (i, 0)),
                  pl.BlockSpec((1, W), lambda i: (0, 0)),
                  pl.BlockSpec((G, GMLP_BLOCK, GMLP_BLOCK), lambda i: (0, 0, 0)),
                  pl.BlockSpec((GMLP_BLOCK, W), lambda i: (0, 0))],
        out_specs=[pl.BlockSpec((tm, W2), lambda i: (i, 0)),
                   pl.BlockSpec((G, GMLP_BLOCK, GMLP_BLOCK), lambda i: (0, 0, 0)),
                   pl.BlockSpec((G, GMLP_BLOCK), lambda i: (0, 0)), pl.BlockSpec((1, W), lambda i: (0, 0))],
        out_shape=[jax.ShapeDtypeStruct((T, W2), BF16), jax.ShapeDtypeStruct((G, GMLP_BLOCK, GMLP_BLOCK), F32),
                   jax.ShapeDtypeStruct((G, GMLP_BLOCK), F32), jax.ShapeDtypeStruct((1, W), F32)],
        scratch_shapes=[pltpu.VMEM((tm, W), F32), pltpu.VMEM((tm, W), F32), pltpu.VMEM((GMLP_BLOCK, W), F32)],
        compiler_params=_params(("arbitrary",)))(zp, d_out, gv, ws, bs_tile)


HALO = 8


def _shift_down(a, prev, k):
    rows = lax.broadcasted_iota(jnp.int32, a.shape, 0)
    out = pltpu.roll(a, k, 0)
    for j in range(k):
        out = jnp.where(rows == j, prev[HALO - k + j:HALO - k + j + 1, :], out)
    return out


def _conv(a, prev, w_ref, b_ref):
    return w_ref[2:3, :] * a + w_ref[1:2, :] * _shift_down(a, prev, 1) + w_ref[0:1, :] * _shift_down(a, prev, 2) \
        + b_ref[...]


def _conv_specs(tm, tc, S, nct, a_cols):
    hb = tm // HALO
    cur = lambda off: pl.BlockSpec((tm, tc), lambda j, i: (i, j + off))
    prev = lambda off: pl.BlockSpec((HALO, tc), lambda j, i: (jnp.maximum(i * hb - 1, 0), j + off))
    return cur, prev


def _conv_fwd(a, cw, cb, S, *, tm=256):
    T, F2 = a.shape
    F = F2 // 2
    tc = _col_tile(F)
    nct = F // tc
    tm = _row_tile(S, tm)
    cur, prev = _conv_specs(tm, tc, S, nct, F2)
    wspec = lambda off: pl.BlockSpec((3, tc), lambda j, i: (0, j + off))
    bspec = lambda off: pl.BlockSpec((1, tc), lambda j, i: (0, j + off))

    def body(au_ref, pu_ref, ag_ref, pg_ref, wu_ref, wg_ref, bu_ref, bg_ref, o_ref):
        first = (pl.program_id(1) * tm) % S == 0
        keep = jnp.where(first, 0.0, 1.0)
        up = _conv(au_ref[...], pu_ref[...] * keep, wu_ref, bu_ref)
        gate = _conv(ag_ref[...], pg_ref[...] * keep, wg_ref, bg_ref)
        o_ref[...] = (gate * jax.nn.sigmoid(gate) * up).astype(BF16)

    return pl.pallas_call(
        body, name="conv_fwd", grid=(nct, T // tm),
        in_specs=[cur(0), prev(0), cur(nct), prev(nct), wspec(0), wspec(nct), bspec(0), bspec(nct)],
        out_specs=pl.BlockSpec((tm, tc), lambda j, i: (i, j)), out_shape=jax.ShapeDtypeStruct((T, F), BF16),
        compiler_params=_params(("arbitrary", "arbitrary")))(a, a, a, a, cw, cw, cb, cb)


def _conv_bwd(a, dy, cw, cb, S, *, tm=256):
    T, F2 = a.shape
    F = F2 // 2
    tc = _col_tile(F)
    nct = F // tc
    tm = _row_tile(S, tm)
    nm = T // tm
    hb = tm // HALO
    cur, prev = _conv_specs(tm, tc, S, nct, F2)
    nxt = lambda off: pl.BlockSpec((HALO, tc), lambda j, i: (jnp.minimum((i + 1) * hb, T // HALO - 1), j + off))
    wspec = lambda off: pl.BlockSpec((3, tc), lambda j, i: (0, j + off))
    bspec = lambda off: pl.BlockSpec((1, tc), lambda j, i: (0, j + off))
    TE = tm + HALO

    def body(au_ref, pu_ref, nu_ref, ag_ref, pg_ref, ng_ref, dy_ref, ndy_ref, wu_ref, wg_ref, bu_ref, bg_ref,
             dau_ref, dag_ref, dwu_ref, dwg_ref, dbu_ref, dbg_ref):
        i = pl.program_id(1)
        first = (i * tm) % S == 0
        last = ((i + 1) * tm) % S == 0
        keep_p = jnp.where(first, 0.0, 1.0)
        keep_n = jnp.where(last, 0.0, 1.0)
        rows = lax.broadcasted_iota(jnp.int32, (TE, tc), 0)
        dyf = jnp.concatenate([dy_ref[...].astype(F32), ndy_ref[...].astype(F32) * keep_n], axis=0)

        def pre(a_ref, p_ref, n_ref, w_ref, b_ref):
            a = jnp.concatenate([a_ref[...], n_ref[...]], axis=0)
            p = p_ref[...] * keep_p
            a1, a2 = _shift_down(a, p, 1), _shift_down(a, p, 2)
            return a, a1, a2, w_ref[2:3, :] * a + w_ref[1:2, :] * a1 + w_ref[0:1, :] * a2 + b_ref[...]

        au, au1, au2, up = pre(au_ref, pu_ref, nu_ref, wu_ref, bu_ref)
        ag, ag1, ag2, gate = pre(ag_ref, pg_ref, ng_ref, wg_ref, bg_ref)
        sg = jax.nn.sigmoid(gate)
        d_up = dyf * (gate * sg)
        d_gate = dyf * up * (sg * (1.0 + gate * (1.0 - sg)))

        @pl.when(i == 0)
        def _():
            for r in (dwu_ref, dwg_ref, dbu_ref, dbg_ref):
                r[...] = jnp.zeros_like(r)

        def back(d, a, a1, a2, w_ref, da_ref, dw_ref, db_ref):
            own = jnp.where(rows < tm, d, 0.0)
            db_ref[...] += jnp.sum(own, axis=0, keepdims=True)
            dw_ref[2:3, :] += jnp.sum(own * a, axis=0, keepdims=True)
            dw_ref[1:2, :] += jnp.sum(own * a1, axis=0, keepdims=True)
            dw_ref[0:1, :] += jnp.sum(own * a2, axis=0, keepdims=True)
            da = w_ref[2:3, :] * d + w_ref[1:2, :] * pltpu.roll(d, TE - 1, 0) + w_ref[0:1, :] * pltpu.roll(d, TE - 2, 0)
            da_ref[...] = da[:tm].astype(BF16)

        back(d_up, au, au1, au2, wu_ref, dau_ref, dwu_ref, dbu_ref)
        back(d_gate, ag, ag1, ag2, wg_ref, dag_ref, dwg_ref, dbg_ref)

    dyspec = pl.BlockSpec((tm, tc), lambda j, i: (i, j))
    ndyspec = pl.BlockSpec((HALO, tc), lambda j, i: (jnp.minimum((i + 1) * hb, T // HALO - 1), j))
    outs = pl.pallas_call(
        body, name="conv_bwd", grid=(nct, nm),
        in_specs=[cur(0), prev(0), nxt(0), cur(nct), prev(nct), nxt(nct), dyspec, ndyspec,
                  wspec(0), wspec(nct), bspec(0), bspec(nct)],
        out_specs=[pl.BlockSpec((tm, tc), lambda j, i: (i, j)), pl.BlockSpec((tm, tc), lambda j, i: (i, j)),
                   pl.BlockSpec((3, tc), lambda j, i: (0, j)), pl.BlockSpec((3, tc), lambda j, i: (0, j)),
                   pl.BlockSpec((1, tc), lambda j, i: (0, j)), pl.BlockSpec((1, tc), lambda j, i: (0, j))],
        out_shape=[jax.ShapeDtypeStruct((T, F), BF16), jax.ShapeDtypeStruct((T, F), BF16),
                   jax.ShapeDtypeStruct((3, F), F32), jax.ShapeDtypeStruct((3, F), F32),
                   jax.ShapeDtypeStruct((1, F), F32), jax.ShapeDtypeStruct((1, F), F32)],
        compiler_params=_params(("arbitrary", "arbitrary")))(a, a, a, a, a, a, dy, dy, cw, cw, cb, cb)
    dau, dag, dwu, dwg, dbu, dbg = outs
    return dau, dag, jnp.concatenate([dwu, dwg], axis=1), jnp.concatenate([dbu, dbg], axis=1)


def _bias_index():
    idx = np.arange(F_LEN)
    d = np.where(idx < K_SPAN, idx, idx - F_LEN)
    return np.clip(PAD - d, -REL_CLIP, REL_CLIP) + REL_CLIP


def _roll_rows(x, sign):
    rows = lax.broadcasted_iota(jnp.int32, x.shape, 0)
    step = 1
    while step < Q_BLOCK:
        shift = step if sign > 0 else F_LEN - step
        x = jnp.where((rows & step) != 0, pltpu.roll(x, shift, 1), x)
        step *= 2
    return x


def _bias_expand(frow):
    H = frow.shape[0]

    def body(f_ref, o_ref):
        x = jnp.broadcast_to(f_ref[...], (Q_BLOCK, F_LEN))
        o_ref[...] = _roll_rows(x, 1)[:, :K_SPAN]

    return pl.pallas_call(
        body, name="bias_expand", grid=(H,),
        in_specs=[pl.BlockSpec((None, 1, F_LEN), lambda h: (h, 0, 0))],
        out_specs=pl.BlockSpec((None, Q_BLOCK, K_SPAN), lambda h: (h, 0, 0)),
        out_shape=jax.ShapeDtypeStruct((H, Q_BLOCK, K_SPAN), F32), compiler_params=_params(("arbitrary",)))(frow)


def _bias_reduce(dbias, n_rel):
    H = dbias.shape[0]
    onehot = jnp.asarray((_bias_index()[:, None] == np.arange(n_rel)[None, :]).astype(np.float32))

    def body(d_ref, oh_ref, o_ref):
        x = jnp.concatenate([d_ref[...], jnp.zeros((Q_BLOCK, F_LEN - K_SPAN), F32)], axis=1)
        row = jnp.sum(_roll_rows(x, -1), axis=0, keepdims=True)
        row8 = jnp.broadcast_to(row, (8, F_LEN))
        o_ref[...] = jnp.dot(row8, oh_ref[...], precision=lax.Precision.HIGHEST, preferred_element_type=F32)[0:1]

    return pl.pallas_call(
        body, name="bias_reduce", grid=(H,),
        in_specs=[pl.BlockSpec((None, Q_BLOCK, K_SPAN), lambda h: (h, 0, 0)),
                  pl.BlockSpec((F_LEN, n_rel), lambda h: (0, 0))],
        out_specs=pl.BlockSpec((None, 1, n_rel), lambda h: (h, 0, 0)),
        out_shape=jax.ShapeDtypeStruct((H, 1, n_rel), F32), compiler_params=_params(("arbitrary",)))(dbias, onehot)


def _band_mask(q0):
    qc = lax.broadcasted_iota(jnp.int32, (Q_BLOCK, K_SPAN), 0) // CHUNK * CHUNK
    kj = lax.broadcasted_iota(jnp.int32, (Q_BLOCK, K_SPAN), 1)
    return (kj >= qc) & (kj < qc + PAD + CHUNK) & (q0 + kj >= PAD)


def _attn_specs(S, HD):
    hw = HEADS_PER_STEP * HEAD_DIM
    qspec = pl.BlockSpec((None, Q_BLOCK, hw), lambda g, b, i: (b, i, g))
    kspec = pl.BlockSpec((None, S + PAD, hw), lambda g, b, i: (b, 0, g))
    vspec = pl.BlockSpec((None, S + PAD, hw), lambda g, b, i: (b, 0, HD // hw + g))
    bspec = pl.BlockSpec((HEADS_PER_STEP, Q_BLOCK, K_SPAN), lambda g, b, i: (g, 0, 0))
    return hw, qspec, kspec, vspec, bspec


def _attn_probs(q_ref, k_ref, b_ref, h, q0, mask):
    hs = slice(h * HEAD_DIM, (h + 1) * HEAD_DIM)
    kh = k_ref[pl.ds(q0, K_SPAN), hs]
    s = lax.dot_general(q_ref[:, hs], kh, (((1,), (1,)), ((), ())), preferred_element_type=F32) + b_ref[h]
    s = jnp.where(mask, s, NEG_INF)
    p = jnp.exp(s - jnp.max(s, axis=-1, keepdims=True))
    return p / jnp.sum(p, axis=-1, keepdims=True), kh


def _attn_fwd(q, kvp, bias, B, S):
    HD = q.shape[-1]
    hw, qspec, kspec, vspec, bspec = _attn_specs(S, HD)

    def body(q_ref, k_ref, v_ref, b_ref, o_ref):
        q0 = pl.multiple_of(pl.program_id(2) * Q_BLOCK, Q_BLOCK)
        mask = _band_mask(q0)
        for h in range(HEADS_PER_STEP):
            hs = slice(h * HEAD_DIM, (h + 1) * HEAD_DIM)
            p, _ = _attn_probs(q_ref, k_ref, b_ref, h, q0, mask)
            o = jnp.dot(p.astype(BF16), v_ref[pl.ds(q0, K_SPAN), hs], preferred_element_type=F32)
            o_ref[:, hs] = o.astype(BF16)

    return pl.pallas_call(
        body, name="attn_fwd", grid=(HD // hw, B, S // Q_BLOCK), in_specs=[qspec, kspec, vspec, bspec],
        out_specs=qspec, out_shape=jax.ShapeDtypeStruct((B, S, HD), BF16),
        compiler_params=_params(("arbitrary", "arbitrary", "arbitrary")))(q, kvp, kvp, bias)


def _attn_bwd(q, kvp, bias, do, B, S):
    HD = q.shape[-1]
    H = HD // HEAD_DIM
    hw, qspec, kspec, vspec, bspec = _attn_specs(S, HD)
    scale = HEAD_DIM ** -0.5

    def body(q_ref, k_ref, v_ref, b_ref, do_ref, dq_ref, dk_ref, dv_ref, db_ref):
        b, i = pl.program_id(1), pl.program_id(2)
        q0 = pl.multiple_of(i * Q_BLOCK, Q_BLOCK)

        @pl.when(i == 0)
        def _():
            dk_ref[...] = jnp.zeros_like(dk_ref)
            dv_ref[...] = jnp.zeros_like(dv_ref)

        @pl.when((i == 0) & (b == 0))
        def _():
            db_ref[...] = jnp.zeros_like(db_ref)

        mask = _band_mask(q0)
        for h in range(HEADS_PER_STEP):
            hs = slice(h * HEAD_DIM, (h + 1) * HEAD_DIM)
            p, kh = _attn_probs(q_ref, k_ref, b_ref, h, q0, mask)
            doh = do_ref[:, hs]
            dp = lax.dot_general(doh, v_ref[pl.ds(q0, K_SPAN), hs], (((1,), (1,)), ((), ())),
                                 preferred_element_type=F32)
            ds = p * (dp - jnp.sum(p * dp, axis=-1, keepdims=True))
            db_ref[h] += ds
            dsb = ds.astype(BF16)
            dq_ref[:, hs] = (jnp.dot(dsb, kh, preferred_element_type=F32) * scale).astype(BF16)
            dk_ref[pl.ds(q0, K_SPAN), hs] += lax.dot_general(dsb, q_ref[:, hs], (((0,), (0,)), ((), ())),
                                                              preferred_element_type=F32)
            dv_ref[pl.ds(q0, K_SPAN), hs] += lax.dot_general(p.astype(BF16), doh, (((0,), (0,)), ((), ())),
                                                              preferred_element_type=F32)

    dkspec = pl.BlockSpec((None, S + PAD, hw), lambda g, b, i: (b, 0, g))
    return pl.pallas_call(
        body, name="attn_bwd", grid=(HD // hw, B, S // Q_BLOCK), in_specs=[qspec, kspec, vspec, bspec, qspec],
        out_specs=[qspec, dkspec, dkspec, bspec],
        out_shape=[jax.ShapeDtypeStruct((B, S, HD), BF16), jax.ShapeDtypeStruct((B, S + PAD, HD), F32),
                   jax.ShapeDtypeStruct((B, S + PAD, HD), F32), jax.ShapeDtypeStruct((H, Q_BLOCK, K_SPAN), F32)],
        compiler_params=_params(("arbitrary", "arbitrary", "arbitrary")))(q, kvp, kvp, bias, do)


def _loss_head(h, g, target, *, tm=512):
    T, D = h.shape
    tm = _row_tile(T, tm)

    def body(h_ref, g_ref, t_ref, dh_ref, loss_ref, dg_ref):
        @pl.when(pl.program_id(0) == 0)
        def _():
            loss_ref[...] = jnp.zeros_like(loss_ref)
            dg_ref[...] = jnp.zeros_like(dg_ref)

        n, r = _rms(h_ref[...], None)
        g = g_ref[...]
        e = n * g - t_ref[...]
        loss_ref[...] += 0.5 * jnp.sum(jnp.mean(e * e, axis=-1, keepdims=True), axis=0, keepdims=True)
        dy = e * (1.0 / D)
        dg_ref[...] += jnp.sum(dy * n, axis=0, keepdims=True)
        t = dy * g
        dh_ref[...] = r * (t - n * jnp.mean(t * n, axis=-1, keepdims=True))

    row = pl.BlockSpec((tm, D), lambda i: (i, 0))
    return pl.pallas_call(
        body, name="loss_head", grid=(T // tm,), in_specs=[row, pl.BlockSpec((1, D), lambda i: (0, 0)), row],
        out_specs=[row, pl.BlockSpec((8, 128), lambda i: (0, 0)), pl.BlockSpec((1, D), lambda i: (0, 0))],
        out_shape=[jax.ShapeDtypeStruct((T, D), F32), jax.ShapeDtypeStruct((8, 128), F32),
                   jax.ShapeDtypeStruct((1, D), F32)],
        compiler_params=_params(("arbitrary",)))(h, g.reshape(1, D), target)


def _adamw(w, g, m, v, *, name):
    R, C = w.shape
    tr = R
    for cand in (256, 352, 128, 64, 8):
        if R % cand == 0 and R > cand:
            tr = cand
            break

    def body(w_ref, g_ref, m_ref, v_ref, d_ref, nm_ref, nv_ref):
        g = g_ref[...]
        m = ADAM_B1 * m_ref[...] + (1.0 - ADAM_B1) * g
        v = ADAM_B2 * v_ref[...] + (1.0 - ADAM_B2) * (g * g)
        m_hat = m / (1.0 - ADAM_B1 ** ADAM_STEP)
        v_hat = v / (1.0 - ADAM_B2 ** ADAM_STEP)
        d_ref[...] = -ADAM_LR * (m_hat / (jnp.sqrt(v_hat) + ADAM_EPS) + ADAM_WD * w_ref[...])
        nm_ref[...] = m
        nv_ref[...] = v

    spec = pl.BlockSpec((tr, C), lambda i: (i, 0))
    return pl.pallas_call(body, name=name, grid=(R // tr,), in_specs=[spec] * 4, out_specs=[spec] * 3,
                          out_shape=[jax.ShapeDtypeStruct((R, C), F32)] * 3,
                          compiler_params=_params(("arbitrary",)))(w, g, m, v)


def _add_pair(u0, u1, got, core, *, name):
    n4, R, C = got.shape
    rows = n4 * R
    tr = 512 if rows % 512 == 0 else R
    flat = lambda a: a.reshape(rows, C)

    def body(c_ref, u0_ref, u1_ref, got_ref, o_ref):
        mine = jnp.where(c_ref[0] == 0, u0_ref[...].astype(F32), u1_ref[...].astype(F32))
        o_ref[...] = (mine + got_ref[...].astype(F32)).astype(BF16)

    spec = pl.BlockSpec((tr, C), lambda i, c: (i, 0))
    grid_spec = pltpu.PrefetchScalarGridSpec(num_scalar_prefetch=1, grid=(rows // tr,), in_specs=[spec] * 3,
                                             out_specs=spec)
    out = pl.pallas_call(body, name=name, grid_spec=grid_spec, out_shape=jax.ShapeDtypeStruct((rows, C), BF16),
                         compiler_params=_params(("arbitrary",)))(core.reshape(1), flat(u0), flat(u1), flat(got))
    return out.reshape(n4, R, C)


def _sum_chips(own, got, j_me, *, name):
    _, R, C = own.shape
    tr = R
    for cand in (256, 352, 128):
        if R % cand == 0 and R > cand:
            tr = cand
            break

    def body(j_ref, own_ref, got_ref, o_ref):
        o_ref[...] = (own_ref[...].astype(F32) + got_ref[0].astype(F32) + got_ref[1].astype(F32)
                      + got_ref[2].astype(F32))

    grid_spec = pltpu.PrefetchScalarGridSpec(
        num_scalar_prefetch=1, grid=(R // tr,),
        in_specs=[pl.BlockSpec((None, tr, C), lambda i, j: (j[0], i, 0)),
                  pl.BlockSpec((3, tr, C), lambda i, j: (0, i, 0))],
        out_specs=pl.BlockSpec((tr, C), lambda i, j: (i, 0)))
    return pl.pallas_call(body, name=name, grid_spec=grid_spec, out_shape=jax.ShapeDtypeStruct((R, C), F32),
                          compiler_params=_params(("arbitrary",)))(j_me.reshape(1), own, got)


def _mesh_pos():
    x, y, c = lax.axis_index("x"), lax.axis_index("y"), lax.axis_index("c")
    return x, y, c


def _other_chips(x, y):
    return [(1 - x, y), (x, 1 - y), (1 - x, 1 - y)]


ANY = pl.BlockSpec(memory_space=pl.ANY)


class _W:
    def __init__(self, name, shard, row_sharded):
        self.name = name
        self.L, ks, ns = shard.shape
        self.row_sharded = row_sharded
        self.K, self.N = (ks * N_CHIPS, ns) if row_sharded else (ks, ns * N_CHIPS)
        self.ks, self.ns = ks, ns

    def shard_of(self, full, j):
        if self.row_sharded:
            return full.at[:, pl.ds(j * self.ks, self.ks), :]
        return full.at[:, :, pl.ds(j * self.ns, self.ns)]

    def half_of(self, shard, c):
        if self.L == 2:
            return shard.at[pl.ds(c, 1)]
        if self.row_sharded:
            return shard.at[:, :, pl.ds(c * (self.ns // 2), self.ns // 2)]
        return shard.at[:, pl.ds(c * (self.ks // 2), self.ks // 2), :]


def _gather_weights(ws, shards):
    nw = len(ws)

    def body(*refs):
        src, dst = refs[:nw], refs[nw:2 * nw]
        send_sems, recv_sems, local_sems = refs[2 * nw:]
        x, y, c = _mesh_pos()
        me = 2 * x + y
        sibling = (x, y, 1 - c)
        chips = _other_chips(x, y)

        def remote(k, s, d, to):
            return pltpu.make_async_remote_copy(src_ref=s, dst_ref=d, send_sem=send_sems.at[k],
                                                recv_sem=recv_sems.at[k], device_id=to, device_id_type=MESH)

        local = [pltpu.make_async_copy(src[i], w.shard_of(dst[i], me), local_sems.at[i]) for i, w in enumerate(ws)]
        for cp in local:
            cp.start()
        sends = []
        for i, w in enumerate(ws):
            for f, (px, py) in enumerate(chips):
                sends.append(remote(6 * i + f, w.half_of(src[i], c), w.half_of(w.shard_of(dst[i], me), c),
                                    (px, py, c)))
        for cp in sends:
            cp.start()
        for i, w in enumerate(ws):
            for f, (px, py) in enumerate(chips):
                landed = w.half_of(w.shard_of(dst[i], 2 * px + py), c)
                remote(6 * i + f, landed, landed, (px, py, c)).wait_recv()
                fwd = remote(6 * i + 3 + f, landed, landed, sibling)
                fwd.start()
                sends.append(fwd)
        for i, w in enumerate(ws):
            for f, (px, py) in enumerate(chips):
                landed = w.half_of(w.shard_of(dst[i], 2 * px + py), 1 - c)
                remote(6 * i + 3 + f, landed, landed, sibling).wait_recv()
        for cp in sends:
            cp.wait_send()
        for cp in local:
            cp.wait()

    return pl.pallas_call(
        body, name="gather_weights", in_specs=[ANY] * nw, out_specs=[ANY] * nw,
        out_shape=[jax.ShapeDtypeStruct((w.L, w.K, w.N), BF16) for w in ws],
        scratch_shapes=[pltpu.SemaphoreType.DMA((6 * nw,)), pltpu.SemaphoreType.DMA((6 * nw,)),
                        pltpu.SemaphoreType.DMA((nw,))],
        compiler_params=_params(has_side_effects=True))(*shards)


def _swap_units(units):
    nw = len(units)

    def body(*refs):
        u0, u1, got = refs[:nw], refs[nw:2 * nw], refs[2 * nw:3 * nw]
        send_sems, recv_sems = refs[3 * nw:]
        x, y, c = _mesh_pos()
        sibling = (x, y, 1 - c)
        copies = []
        for i in range(nw):
            for cc in (0, 1):
                @pl.when(c == cc)
                def _(i=i, cc=cc):
                    src = (u1, u0)[cc][i]
                    pltpu.make_async_remote_copy(src_ref=src, dst_ref=got[i], send_sem=send_sems.at[i],
                                                 recv_sem=recv_sems.at[i], device_id=sibling,
                                                 device_id_type=MESH).start()
        for i in range(nw):
            pltpu.make_async_remote_copy(src_ref=u0[i], dst_ref=got[i], send_sem=send_sems.at[i],
                                         recv_sem=recv_sems.at[i], device_id=sibling, device_id_type=MESH).wait()

    flat0 = [u[0] for u in units]
    flat1 = [u[1] for u in units]
    return pl.pallas_call(
        body, name="swap_units", in_specs=[ANY] * (2 * nw), out_specs=[ANY] * nw,
        out_shape=[jax.ShapeDtypeStruct(u[0].shape, BF16) for u in units],
        scratch_shapes=[pltpu.SemaphoreType.DMA((nw,)), pltpu.SemaphoreType.DMA((nw,))],
        compiler_params=_params(has_side_effects=True))(*flat0, *flat1)


def _scatter_units(sums):
    nw = len(sums)

    def body(*refs):
        src, got = refs[:nw], refs[nw:2 * nw]
        send_sems, recv_sems = refs[2 * nw:]
        x, y, c = _mesh_pos()
        chips = _other_chips(x, y)
        copies = []
        for i in range(nw):
            for f, (px, py) in enumerate(chips):
                cp = pltpu.make_async_remote_copy(src_ref=src[i].at[2 * px + py], dst_ref=got[i].at[f],
                                                  send_sem=send_sems.at[3 * i + f], recv_sem=recv_sems.at[3 * i + f],
                                                  device_id=(px, py, c), device_id_type=MESH)
                cp.start()
                copies.append(cp)
        for cp in copies:
            cp.wait()

    return pl.pallas_call(
        body, name="scatter_units", in_specs=[ANY] * nw, out_specs=[ANY] * nw,
        out_shape=[jax.ShapeDtypeStruct((3,) + s.shape[1:], BF16) for s in sums],
        scratch_shapes=[pltpu.SemaphoreType.DMA((3 * nw,)), pltpu.SemaphoreType.DMA((3 * nw,))],
        compiler_params=_params(has_side_effects=True))(*sums)


def _join_halves(ws, halves):
    nw = len(ws)

    def body(*refs):
        src, dst = refs[:nw], refs[nw:2 * nw]
        send_sems, recv_sems, local_sems = refs[2 * nw:]
        x, y, c = _mesh_pos()
        sibling = (x, y, 1 - c)
        copies = []
        for i, w in enumerate(ws):
            mine = w.half_of(dst[i], c)
            cp = pltpu.make_async_copy(src[i], mine, local_sems.at[i])
            cp.start()
            copies.append(cp)
            rc = pltpu.make_async_remote_copy(src_ref=src[i], dst_ref=mine, send_sem=send_sems.at[i],
                                              recv_sem=recv_sems.at[i], device_id=sibling, device_id_type=MESH)
            rc.start()
        for i, w in enumerate(ws):
            theirs = w.half_of(dst[i], 1 - c)
            pltpu.make_async_remote_copy(src_ref=src[i], dst_ref=theirs, send_sem=send_sems.at[i],
                                         recv_sem=recv_sems.at[i], device_id=sibling, device_id_type=MESH).wait()
        for cp in copies:
            cp.wait()

    return pl.pallas_call(
        body, name="join_halves", in_specs=[ANY] * nw, out_specs=[ANY] * nw,
        out_shape=[jax.ShapeDtypeStruct((w.L, w.ks, w.ns), F32) for w in ws],
        scratch_shapes=[pltpu.SemaphoreType.DMA((nw,)), pltpu.SemaphoreType.DMA((nw,)),
                        pltpu.SemaphoreType.DMA((nw,))],
        compiler_params=_params(has_side_effects=True))(*halves)


def _allreduce_small(vec):
    R = vec.shape[0]

    def body(x_ref, o_ref, buf, send_sems, recv_sems):
        x, y, c = _mesh_pos()
        me, sibling = (x, y, c), (x, y, 1 - c)
        chips = _other_chips(x, y)

        def slot(px, py, pc):
            return buf.at[4 * px + 2 * py + pc]

        def copy(k, block, to, src=None):
            return pltpu.make_async_remote_copy(src_ref=slot(*block) if src is None else src, dst_ref=slot(*block),
                                                send_sem=send_sems.at[k], recv_sem=recv_sems.at[k], device_id=to,
                                                device_id_type=MESH)

        first = [copy(0, me, sibling, src=x_ref)] + [copy(1 + f, me, (*chip, c), src=x_ref)
                                                     for f, chip in enumerate(chips)]
        for cp in first:
            cp.start()
        passed = [copy(4 + f, (*chip, c), sibling) for f, chip in enumerate(chips)]
        for f, chip in enumerate(chips):
            copy(1 + f, (*chip, c), me).wait_recv()
            passed[f].start()
        copy(0, sibling, me).wait_recv()
        for f, chip in enumerate(chips):
            copy(4 + f, (*chip, 1 - c), me).wait_recv()
        for cp in first + passed:
            cp.wait_send()
        slot(*me)[...] = x_ref[...]
        acc = buf[0]
        for d in range(1, 8):
            acc = acc + buf[d]
        o_ref[...] = acc

    return pl.pallas_call(
        body, name="allreduce_small", in_specs=[pl.BlockSpec(memory_space=pltpu.VMEM)],
        out_specs=pl.BlockSpec(memory_space=pltpu.VMEM), out_shape=jax.ShapeDtypeStruct((R, 128), F32),
        scratch_shapes=[pltpu.VMEM((8, R, 128), F32), pltpu.SemaphoreType.DMA((7,)), pltpu.SemaphoreType.DMA((7,))],
        compiler_params=_params())(vec)


def _pack(parts):
    flat = jnp.concatenate([p.reshape(-1).astype(F32) for p in parts])
    n = flat.shape[0]
    pad = (-n) % (64 * 128)
    return jnp.pad(flat, (0, pad)).reshape(-1, 128)


def _unpack(vec, shapes):
    flat = vec.reshape(-1)
    out, off = [], 0
    for s in shapes:
        n = int(np.prod(s))
        out.append(flat[off:off + n].reshape(s))
        off += n
    return out


def kernel(x, a_norm_g, a_w_in, a_v_norm_g, a_w_s, a_b_s, a_w_out, kv_norm_g, w_kv, b_norm_g, b_w_q, b_rel_bias, b_w_o, f_norm_g, f_w_in, f_conv_w, f_conv_b, f_w_down, final_norm_g, loss_target, m_a_norm_g, m_a_w_in, m_a_v_norm_g, m_a_w_s, m_a_b_s, m_a_w_out, m_kv_norm_g, m_w_kv, m_b_norm_g, m_b_w_q, m_b_rel_bias, m_b_w_o, m_f_norm_g, m_f_w_in, m_f_conv_w, m_f_conv_b, m_f_w_down, m_final_norm_g, v_a_norm_g, v_a_w_in, v_a_v_norm_g, v_a_w_s, v_a_b_s, v_a_w_out, v_kv_norm_g, v_w_kv, v_b_norm_g, v_b_w_q, v_b_rel_bias, v_b_w_o, v_f_norm_g, v_f_w_in, v_f_conv_w, v_f_conv_b, v_f_w_down, v_final_norm_g):
    B, S, D = x.shape
    T = B * S
    xi, yi, ci = lax.axis_index("x"), lax.axis_index("y"), lax.axis_index("c")
    j_me = (2 * xi + yi).astype(jnp.int32)

    w_shards = {"a_w_in": (a_w_in, False), "a_w_out": (a_w_out, True), "w_kv": (w_kv[None], False),
                "b_w_q": (b_w_q, True), "b_w_o": (b_w_o, True), "f_w_in": (f_w_in, False), "f_w_down": (f_w_down, True)}
    names = list(w_shards)
    ws = [_W(n, w_shards[n][0], w_shards[n][1]) for n in names]
    full = dict(zip(names, _gather_weights(ws, [w_shards[n][0].astype(BF16) for n in names])))
    W_in_a, W_out_a, W_kv = full["a_w_in"][0], full["a_w_out"][0], full["w_kv"][0]
    W_q, W_o = full["b_w_q"][0], full["b_w_o"][0]
    F_in, F_down = full["f_w_in"], full["f_w_down"]

    a_g_sh, a_vg_sh, cw_sh = a_norm_g, a_v_norm_g, f_conv_w
    Wd = a_w_in.shape[1]
    GW = a_v_norm_g.shape[1] * N_CHIPS
    F2 = f_conv_w.shape[2] * N_CHIPS
    nsd, nsg, nsf = a_norm_g.shape[1], a_v_norm_g.shape[1], f_conv_w.shape[2]
    own = (ci == 0).astype(F32)
    place = lambda sh, width, n: lax.dynamic_update_slice_in_dim(
        jnp.zeros(sh.shape[:-1] + (width,), F32), sh * own, j_me * n, axis=sh.ndim - 1)
    gathered = _allreduce_small(_pack([place(a_g_sh, Wd, nsd), place(a_vg_sh, GW, nsg), place(cw_sh, F2, nsf)]))
    a_g, a_vg, conv_w = _unpack(gathered, [(1, Wd), (1, GW), (2, 3, F2)])

    h0 = x.reshape(T, D)
    target = loss_target.reshape(T, D)
    G = a_w_s.shape[1]
    bs_tile = jnp.repeat(a_b_s[0].T, GROUP_DIM, axis=1)
    ws_a = a_w_s[0]
    scale = HEAD_DIM ** -0.5
    HD = W_q.shape[1]
    H = HD // HEAD_DIM
    n_rel = b_rel_bias.shape[-1]
    frow = b_rel_bias[0][:, _bias_index()].reshape(H, 1, F_LEN)
    bias = _bias_expand(frow)

    def ffn_fwd(h, l):
        a = _mm(h, F_in[l], norm_g=f_norm_g[l], name=f"ffn{l}_in")
        yff = _conv_fwd(a, conv_w[l], f_conv_b[l][None], S)
        return _mm(yff, F_down[l], res=h, name=f"ffn{l}_down"), a, yff

    zp = _mm(h0, W_in_a, norm_g=a_g[0], name="a_in")
    out_a = _gate_fwd(zp, a_vg, ws_a, bs_tile)
    h1 = _mm(out_a, W_out_a, res=h0, name="a_out")
    h2, a0, yff0 = ffn_fwd(h1, 0)
    kv = _mm(h2, W_kv, norm_g=kv_norm_g, out_dtype=BF16, name="kv")
    q = _mm(h2, W_q, norm_g=b_norm_g[0], scale=scale, out_dtype=BF16, name="q")
    kvp = jnp.pad(kv.reshape(B, S, 2 * HD), ((0, 0), (PAD, 0), (0, 0)))
    o = _attn_fwd(q.reshape(B, S, HD), kvp, bias, B, S).reshape(T, HD)
    h3 = _mm(o, W_o, res=h2, name="attn_out")
    h4, a1, yff1 = ffn_fwd(h3, 1)
    dh, loss8, dg_final = _loss_head(h4, final_norm_g, target)

    units = {}
    Fh = F2 // 2

    def ffn_bwd(dh, h, a, yff, l):
        dyff = _mm(dh, F_down[l], trans_w=True, out_dtype=BF16, name=f"ffn{l}_down_dx")
        units[("f_w_down", l)] = _mm_tn(yff, dh, rows_are_shards=True, name=f"ffn{l}_down_dw")
        dau, dag, dcw, dcb = _conv_bwd(a, dyff, conv_w[l], f_conv_b[l][None], S)
        da = jnp.concatenate([dau, dag], axis=1)
        units[("f_w_in", l)] = _mm_tn(h, da, norm_g=f_norm_g[l], name=f"ffn{l}_in_dw")
        dh, dg = _mm(da, F_in[l], trans_w=True, bwd=(h, f_norm_g[l], dh), tm=256, name=f"ffn{l}_in_dx")
        return dh, dg, dcw, dcb

    def halves_tn(key, xx, dy, norm_g, row_sharded):
        for hc in (0, 1):
            if row_sharded:
                half = dy.shape[1] // 2
                units[(key, hc)] = _mm_tn(xx, dy, norm_g=norm_g, y_cols=(hc * half, half), rows_are_shards=True,
                                          name=f"{key}_dw{hc}")
            else:
                half = xx.shape[1] // 2
                units[(key, hc)] = _mm_tn(xx, dy, norm_g=norm_g, x_cols=(hc * half, half), name=f"{key}_dw{hc}")

    dh, dg_f1, dcw1, dcb1 = ffn_bwd(dh, h3, a1, yff1, 1)
    do = _mm(dh, W_o, trans_w=True, out_dtype=BF16, name="attn_out_dx")
    halves_tn("b_w_o", o, dh, None, True)
    dq, dkp, dvp, dbias = _attn_bwd(q.reshape(B, S, HD), kvp, bias, do.reshape(B, S, HD), B, S)
    d_rel = _bias_reduce(dbias, n_rel).reshape(1, H, n_rel)
    dq = dq.reshape(T, HD)
    dkv = jnp.concatenate([dkp[:, PAD:], dvp[:, PAD:]], axis=-1).astype(BF16).reshape(T, 2 * HD)
    halves_tn("b_w_q", h2, dq, b_norm_g[0], True)
    dh, dg_b = _mm(dq, W_q, trans_w=True, bwd=(h2, b_norm_g[0], dh), name="q_dx")
    halves_tn("w_kv", h2, dkv, kv_norm_g, False)
    dh, dg_kv = _mm(dkv, W_kv, trans_w=True, bwd=(h2, kv_norm_g, dh), name="kv_dx")
    dh, dg_f0, dcw0, dcb0 = ffn_bwd(dh, h1, a0, yff0, 0)
    d_out = _mm(dh, W_out_a, trans_w=True, out_dtype=BF16, name="a_out_dx")
    halves_tn("a_w_out", out_a, dh, None, True)
    dzp, dws, dbs, dgv = _gate_bwd(zp, d_out, a_vg, ws_a, bs_tile)
    halves_tn("a_w_in", h0, dzp, a_g[0], False)
    grad_x, dg_a = _mm(dzp, W_in_a, trans_w=True, bwd=(h0, a_g[0], dh), name="a_in_dx")

    pairs = [(units[(n, 0)], units[(n, 1)]) for n in names]
    got = _swap_units(pairs)
    core = ci.astype(jnp.int32)
    sums = [_add_pair(p[0], p[1], g_, core, name=f"pair_{n}") for n, p, g_ in zip(names, pairs, got)]
    recv = _scatter_units(sums)
    halves = []
    for n, s_, r_ in zip(names, sums, recv):
        halves.append(_sum_chips(s_, r_, j_me, name=f"chips_{n}")[None])
    g_big = dict(zip(names, _join_halves(ws, halves)))
    g_big["w_kv"] = g_big["w_kv"][0]

    small = {"a_norm_g": dg_a, "a_v_norm_g": dgv, "a_w_s": dws[None], "a_b_s": dbs[None], "kv_norm_g": dg_kv[0],
             "b_norm_g": dg_b, "b_rel_bias": d_rel, "f_norm_g": jnp.concatenate([dg_f0, dg_f1], axis=0),
             "f_conv_w": jnp.stack([dcw0, dcw1]), "f_conv_b": jnp.concatenate([dcb0, dcb1], axis=0),
             "final_norm_g": dg_final[0]}
    snames = list(small)
    red = _allreduce_small(_pack([small[n] for n in snames] + [loss8[0:1, 0:1]]))
    parts = _unpack(red, [small[n].shape for n in snames] + [(1,)])
    g_small = dict(zip(snames, parts[:-1]))
    loss = parts[-1][0]
    g_small["a_norm_g"] = lax.dynamic_slice_in_dim(g_small["a_norm_g"], j_me * nsd, nsd, axis=1)
    g_small["a_v_norm_g"] = lax.dynamic_slice_in_dim(g_small["a_v_norm_g"], j_me * nsg, nsg, axis=1)
    g_small["f_conv_w"] = lax.dynamic_slice_in_dim(g_small["f_conv_w"], j_me * nsf, nsf, axis=2)

    given = dict(a_norm_g=(a_norm_g, m_a_norm_g, v_a_norm_g), a_w_in=(a_w_in, m_a_w_in, v_a_w_in),
                 a_v_norm_g=(a_v_norm_g, m_a_v_norm_g, v_a_v_norm_g), a_w_s=(a_w_s, m_a_w_s, v_a_w_s),
                 a_b_s=(a_b_s, m_a_b_s, v_a_b_s), a_w_out=(a_w_out, m_a_w_out, v_a_w_out),
                 kv_norm_g=(kv_norm_g, m_kv_norm_g, v_kv_norm_g), w_kv=(w_kv, m_w_kv, v_w_kv),
                 b_norm_g=(b_norm_g, m_b_norm_g, v_b_norm_g), b_w_q=(b_w_q, m_b_w_q, v_b_w_q),
                 b_rel_bias=(b_rel_bias, m_b_rel_bias, v_b_rel_bias), b_w_o=(b_w_o, m_b_w_o, v_b_w_o),
                 f_norm_g=(f_norm_g, m_f_norm_g, v_f_norm_g), f_w_in=(f_w_in, m_f_w_in, v_f_w_in),
                 f_conv_w=(f_conv_w, m_f_conv_w, v_f_conv_w), f_conv_b=(f_conv_b, m_f_conv_b, v_f_conv_b),
                 f_w_down=(f_w_down, m_f_w_down, v_f_w_down), final_norm_g=(final_norm_g, m_final_norm_g, v_final_norm_g))
    order = list(given)
    grads, deltas, new_m, new_v = {}, {}, {}, {}
    for n in names:
        w_, m_, v_ = given[n]
        g_ = g_big[n]
        C = w_.shape[-1]
        d2, m2, v2 = _adamw(w_.reshape(-1, C), g_.reshape(-1, C), m_.reshape(-1, C), v_.reshape(-1, C),
                            name=f"adamw_{n}")
        grads[n], deltas[n], new_m[n], new_v[n] = g_.reshape(w_.shape), d2.reshape(w_.shape), m2.reshape(w_.shape), \
            v2.reshape(w_.shape)
    sm = [n for n in order if n not in names]
    d2, m2, v2 = _adamw(_pack([given[n][0] for n in sm]), _pack([g_small[n].reshape(given[n][0].shape) for n in sm]),
                        _pack([given[n][1] for n in sm]), _pack([given[n][2] for n in sm]), name="adamw_small")
    shapes = [given[n][0].shape for n in sm]
    for n, d_, m_, v_ in zip(sm, _unpack(d2, shapes), _unpack(m2, shapes), _unpack(v2, shapes)):
        grads[n], deltas[n], new_m[n], new_v[n] = g_small[n].reshape(given[n][0].shape), d_, m_, v_

    return (loss, grad_x.reshape(B, S, D), *[grads[n] for n in order], *[deltas[n] for n in order],
            *[new_m[n] for n in order], *[new_v[n] for n in order])
```

```python
import math

import numpy as np
import jax
import jax.numpy as jnp
from jax import lax
from jax.experimental import pallas as pl
from jax.experimental.pallas import tpu as pltpu

F32 = jnp.float32
BF16 = jnp.bfloat16
MESH = pl.DeviceIdType.MESH

EPS = 1e-6
NEG_INF = -1e30
CHUNK = 64
GMLP_BLOCK = 128
GROUP_DIM = 128
HEAD_DIM = 64
LEFT_CHUNKS = 8
PAD = LEFT_CHUNKS * CHUNK
REL_CLIP = 128
Q_BLOCK = 128
K_SPAN = PAD + Q_BLOCK
F_LEN = 768
HEADS_PER_STEP = 4
N_CHIPS = 4

ADAM_LR = 0.001
ADAM_B1 = 0.9
ADAM_B2 = 0.999
ADAM_EPS = 1e-08
ADAM_WD = 0.01
ADAM_STEP = 10

VMEM_LIMIT = 56 * 1024 * 1024


def _params(sem=None, **kw):
    if sem is not None:
        kw["dimension_semantics"] = sem
    return pltpu.CompilerParams(vmem_limit_bytes=VMEM_LIMIT, **kw)


def _rms(xf):
    r = lax.rsqrt(jnp.mean(xf * xf, axis=-1, keepdims=True) + EPS)
    return xf * r, r


def _gelu(x):
    c = math.sqrt(2.0 / math.pi)
    return 0.5 * x * (1.0 + jnp.tanh(c * (x + 0.044715 * x * x * x)))


def _gelu_grad(x):
    c = math.sqrt(2.0 / math.pi)
    t = jnp.tanh(c * (x + 0.044715 * x * x * x))
    return 0.5 * (1.0 + t) + 0.5 * x * (1.0 - t * t) * c * (1.0 + 3.0 * 0.044715 * x * x)


def _col_tile(n):
    if n <= 1024:
        return n
    for t in (1408, 1024, 512):
        if n % t == 0:
            return t
    raise ValueError(n)


def _row_tile(t, want):
    while t % want:
        want //= 2
    return want


def _mm(x, w, *, name, layer=None, trans_w=False, norm_g=None, res=None, scale=None, out_dtype=F32, bwd=None,
        split_out=False, split_x=False, tm=512):
    T = x.shape[-2]
    K = 2 * x.shape[-1] if split_x else x.shape[-1]
    N = w.shape[-2] if trans_w else w.shape[-1]
    tn = N if bwd is not None else _col_tile(N // 2 if split_out else N)
    tm = _row_tile(T, tm)
    nn, nm = N // tn, T // tm
    has_norm, has_res, has_bwd = norm_g is not None, res is not None, bwd is not None
    dims = (((1,), (1,)), ((), ())) if trans_w else (((1,), (0,)), ((), ()))

    def body(*refs):
        it = iter(refs)
        x_ref, w_ref = next(it), next(it)
        g_ref = next(it) if has_norm else None
        res_ref = next(it) if has_res else None
        if has_bwd:
            h_ref, bg_ref, dh_ref = next(it), next(it), next(it)
        o_ref = next(it)
        if split_x:
            kh = K // 2
            acc = lax.dot_general(x_ref[0].astype(BF16), w_ref[:, :kh] if trans_w else w_ref[:kh, :], dims,
                                  preferred_element_type=F32)
            acc = acc + lax.dot_general(x_ref[1].astype(BF16), w_ref[:, kh:] if trans_w else w_ref[kh:, :], dims,
                                        preferred_element_type=F32)
        else:
            xv = x_ref[...]
            if has_norm:
                xv = _rms(xv.astype(F32))[0] * g_ref[...]
            acc = lax.dot_general(xv.astype(BF16), w_ref[...], dims, preferred_element_type=F32)
        if scale is not None:
            acc = acc * scale
        if has_res:
            acc = acc + res_ref[...]
        if has_bwd:
            dg_ref = next(it)
            n, r = _rms(h_ref[...])

            @pl.when(pl.program_id(1) == 0)
            def _():
                dg_ref[...] = jnp.zeros_like(dg_ref)

            dg_ref[...] += jnp.sum(acc * n, axis=0, keepdims=True)
            t = acc * bg_ref[...]
            o_ref[...] = dh_ref[...] + r * (t - n * jnp.mean(t * n, axis=-1, keepdims=True))
        else:
            o_ref[...] = acc.astype(out_dtype)

    lead = () if layer is None else (None,)
    lidx = () if layer is None else (layer,)
    ins = [x, w]
    xspec = (pl.BlockSpec((2, tm, K // 2), lambda n, m: (0, m, 0)) if split_x
             else pl.BlockSpec((tm, K), lambda n, m: (m, 0)))
    wspec = (pl.BlockSpec(lead + (tn, K), lambda n, m: lidx + (n, 0)) if trans_w
             else pl.BlockSpec(lead + (K, tn), lambda n, m: lidx + (0, n)))
    in_specs = [xspec, wspec]
    if has_norm:
        ins.append(norm_g.reshape(1, K))
        in_specs.append(pl.BlockSpec((1, K), lambda n, m: (0, 0)))
    if has_res:
        ins.append(res)
        in_specs.append(pl.BlockSpec((tm, tn), lambda n, m: (m, n)))
    if split_out:
        per = nn // 2
        out_shape = [jax.ShapeDtypeStruct((2, T, N // 2), out_dtype)]
        out_specs = [pl.BlockSpec((None, tm, tn), lambda n, m: (n // per, m, n % per))]
    else:
        out_shape = [jax.ShapeDtypeStruct((T, N), F32 if has_bwd else out_dtype)]
        out_specs = [pl.BlockSpec((tm, tn), lambda n, m: (m, n))]
    if has_bwd:
        h, g, dh = bwd
        ins += [h, g.reshape(1, N), dh]
        in_specs += [pl.BlockSpec((tm, N), lambda n, m: (m, 0)), pl.BlockSpec((1, N), lambda n, m: (0, 0)),
                     pl.BlockSpec((tm, N), lambda n, m: (m, 0))]
        out_shape.append(jax.ShapeDtypeStruct((1, N), F32))
        out_specs.append(pl.BlockSpec((1, N), lambda n, m: (0, 0)))
    out = pl.pallas_call(body, name=name, grid=(nn, nm), in_specs=in_specs, out_specs=out_specs, out_shape=out_shape,
                         compiler_params=_params(("arbitrary", "arbitrary")))(*ins)
    return out if has_bwd else out[0]


def _mm_tn(x, dy, *, name, norm_g=None, x_cols=None, y_cols=None, rows_are_shards=False, split_y=False, tt=512):
    T, K = x.shape
    N = 2 * dy.shape[-1] if split_y else dy.shape[-1]
    xs, kx = x_cols if x_cols is not None else (0, K)
    ys, ny = y_cols if y_cols is not None else (0, N)
    if rows_are_shards:
        tn = ny if ny <= 512 else 512
        R, C = kx // N_CHIPS, ny
    else:
        tn = ny // N_CHIPS
        R, C = kx, tn
    nn = ny // tn
    tt = _row_tile(T, tt)
    nt = T // tt
    y0 = ys // tn
    has_norm = norm_g is not None

    def body(*refs):
        it = iter(refs)
        x_ref, y_ref = next(it), next(it)
        g_ref = next(it) if has_norm else None
        o_ref, acc_ref = next(it), next(it)
        t = pl.program_id(1)

        @pl.when(t == 0)
        def _():
            acc_ref[...] = jnp.zeros_like(acc_ref)

        xv = x_ref[...]
        if has_norm:
            xv = _rms(xv.astype(F32))[0] * g_ref[...]
        xb = xv.astype(BF16)
        if kx != K:
            xb = xb[:, xs:xs + kx]
        acc_ref[...] += lax.dot_general(xb, y_ref[...].astype(BF16), (((0,), (0,)), ((), ())),
                                        preferred_element_type=F32)

        @pl.when(t == nt - 1)
        def _():
            a = acc_ref[...].astype(BF16)
            o_ref[...] = a.reshape(N_CHIPS, R, tn) if rows_are_shards else a

    ins = [x, dy]
    if split_y:
        per = (N // 2) // tn
        yspec = pl.BlockSpec((None, tt, tn), lambda n, t: ((y0 + n) // per, t, (y0 + n) % per))
    else:
        yspec = pl.BlockSpec((tt, tn), lambda n, t: (t, y0 + n))
    in_specs = [pl.BlockSpec((tt, K), lambda n, t: (t, 0)), yspec]
    if has_norm:
        ins.append(norm_g.reshape(1, K))
        in_specs.append(pl.BlockSpec((1, K), lambda n, t: (0, 0)))
    if rows_are_shards:
        out_spec = pl.BlockSpec((N_CHIPS, R, tn), lambda n, t: (0, 0, n))
    else:
        out_spec = pl.BlockSpec((None, R, C), lambda n, t: (n, 0, 0))
    return pl.pallas_call(body, name=name, grid=(nn, nt), in_specs=in_specs, out_specs=out_spec,
                          out_shape=jax.ShapeDtypeStruct((N_CHIPS, R, C), BF16),
                          scratch_shapes=[pltpu.VMEM((kx, tn), F32)],
                          compiler_params=_params(("arbitrary", "arbitrary")))(*ins)


def _chunk_mask():
    i = lax.broadcasted_iota(jnp.int32, (GMLP_BLOCK, GMLP_BLOCK), 0) // CHUNK
    j = lax.broadcasted_iota(jnp.int32, (GMLP_BLOCK, GMLP_BLOCK), 1) // CHUNK
    return i >= j


def _gate_fwd(zp, gv, ws, bs_tile, *, tm=256):
    T, W2 = zp.shape
    W = W2 // 2
    G = W // GROUP_DIM
    tm = _row_tile(T, tm)

    def body(zp_ref, gv_ref, ws_ref, bs_ref, o_ref):
        z = _gelu(zp_ref[...])
        u, v = z[:, :W], z[:, W:]
        vn = _rms(v)[0] * gv_ref[...]
        mask = _chunk_mask()
        for g in range(G):
            cs = slice(g * GROUP_DIM, (g + 1) * GROUP_DIM)
            wg = jnp.where(mask, ws_ref[g], 0.0).astype(BF16)
            for b in range(tm // GMLP_BLOCK):
                rs = slice(b * GMLP_BLOCK, (b + 1) * GMLP_BLOCK)
                s = jnp.dot(wg, vn[rs, cs].astype(BF16), preferred_element_type=F32) + bs_ref[:, cs]
                o_ref[rs, cs] = (u[rs, cs] * s).astype(BF16)

    return pl.pallas_call(
        body, name="gate_fwd", grid=(T // tm,),
        in_specs=[pl.BlockSpec((tm, W2), lambda i: (i, 0)), pl.BlockSpec((1, W), lambda i: (0, 0)),
                  pl.BlockSpec((G, GMLP_BLOCK, GMLP_BLOCK), lambda i: (0, 0, 0)),
                  pl.BlockSpec((GMLP_BLOCK, W), lambda i: (0, 0))],
        out_specs=pl.BlockSpec((tm, W), lambda i: (i, 0)), out_shape=jax.ShapeDtypeStruct((T, W), BF16),
        compiler_params=_params(("arbitrary",)))(zp, gv, ws, bs_tile)


def _gate_bwd(zp, d_out, gv, ws, bs_tile, *, tm=256):
    T, W2 = zp.shape
    W = W2 // 2
    G = W // GROUP_DIM
    tm = _row_tile(T, tm)
    nm = T // tm

    def body(zp_ref, do_ref, gv_ref, ws_ref, bs_ref, dzp_ref, dws_ref, dbs_ref, dgv_ref, du_scr, dvn_scr, dsum_scr):
        i = pl.program_id(0)

        @pl.when(i == 0)
        def _():
            dws_ref[...] = jnp.zeros_like(dws_ref)
            dgv_ref[...] = jnp.zeros_like(dgv_ref)
            dsum_scr[...] = jnp.zeros_like(dsum_scr)

        zp = zp_ref[...]
        z = _gelu(zp)
        u, v = z[:, :W], z[:, W:]
        n, r = _rms(v)
        gv = gv_ref[...]
        vn = n * gv
        d_out = do_ref[...].astype(F32)
        mask = _chunk_mask()
        for g in range(G):
            cs = slice(g * GROUP_DIM, (g + 1) * GROUP_DIM)
            wg = jnp.where(mask, ws_ref[g], 0.0).astype(BF16)
            dw = jnp.zeros((GMLP_BLOCK, GMLP_BLOCK), F32)
            for b in range(tm // GMLP_BLOCK):
                rs = slice(b * GMLP_BLOCK, (b + 1) * GMLP_BLOCK)
                vb = vn[rs, cs].astype(BF16)
                s = jnp.dot(wg, vb, preferred_element_type=F32) + bs_ref[:, cs]
                du_scr[rs, cs] = d_out[rs, cs] * s
                ds = d_out[rs, cs] * u[rs, cs]
                dsb = ds.astype(BF16)
                dvn_scr[rs, cs] = lax.dot_general(wg, dsb, (((0,), (0,)), ((), ())), preferred_element_type=F32)
                dw = dw + lax.dot_general(dsb, vb, (((1,), (1,)), ((), ())), preferred_element_type=F32)
                dsum_scr[:, cs] += ds
            dws_ref[g] += jnp.where(mask, dw, 0.0)
        dvn = dvn_scr[...]
        dgv_ref[...] += jnp.sum(dvn * n, axis=0, keepdims=True)
        t = dvn * gv
        dv = r * (t - n * jnp.mean(t * n, axis=-1, keepdims=True))
        dzp_ref[:, :W] = (du_scr[...] * _gelu_grad(zp[:, :W])).astype(BF16)
        dzp_ref[:, W:] = (dv * _gelu_grad(zp[:, W:])).astype(BF16)

        @pl.when(i == nm - 1)
        def _():
            sel = (lax.broadcasted_iota(jnp.int32, (G, W), 1) // GROUP_DIM
                   == lax.broadcasted_iota(jnp.int32, (G, W), 0)).astype(F32)
            dbs_ref[...] = lax.dot_general(sel, dsum_scr[...], (((1,), (1,)), ((), ())),
                                           precision=lax.Precision.HIGHEST, preferred_element_type=F32)

    return pl.pallas_call(
        body, name="gate_bwd", grid=(nm,),
        in_specs=[pl.BlockSpec((tm, W2), lambda i: (i, 0)), pl.BlockSpec((tm, W), lambda i: (i, 0)),
                  pl.BlockSpec((1, W), lambda i: (0, 0)),
                  pl.BlockSpec((G, GMLP_BLOCK, GMLP_BLOCK), lambda i: (0, 0, 0)),
                  pl.BlockSpec((GMLP_BLOCK, W), lambda i: (0, 0))],
        out_specs=[pl.BlockSpec((tm, W2), lambda i: (i, 0)),
                   pl.BlockSpec((G, GMLP_BLOCK, GMLP_BLOCK), lambda i: (0, 0, 0)),
                   pl.BlockSpec((G, GMLP_BLOCK), lambda i: (0, 0)), pl.BlockSpec((1, W), lambda i: (0, 0))],
        out_shape=[jax.ShapeDtypeStruct((T, W2), BF16), jax.ShapeDtypeStruct((G, GMLP_BLOCK, GMLP_BLOCK), F32),
                   jax.ShapeDtypeStruct((G, GMLP_BLOCK), F32), jax.ShapeDtypeStruct((1, W), F32)],
        scratch_shapes=[pltpu.VMEM((tm, W), F32), pltpu.VMEM((tm, W), F32), pltpu.VMEM((GMLP_BLOCK, W), F32)],
        compiler_params=_params(("arbitrary",)))(zp, d_out, gv, ws, bs_tile)


HALO = 8


def _shift_down(a, prev, k):
    rows = lax.broadcasted_iota(jnp.int32, a.shape, 0)
    out = pltpu.roll(a, k, 0)
    for j in range(k):
        out = jnp.where(rows == j, prev[HALO - k + j:HALO - k + j + 1, :], out)
    return out


def _conv_fwd(a, cw, cb, S, *, tm=256):
    _, T, F = a.shape
    tc = _col_tile(F)
    tm = _row_tile(S, tm)
    hb = tm // HALO

    def body(a_ref, p_ref, w_ref, b_ref, o_ref):
        first = (pl.program_id(1) * tm) % S == 0
        keep = jnp.where(first, 0.0, 1.0)

        def conv(s):
            a, p, w = a_ref[s], p_ref[s] * keep, w_ref[s]
            return w[2:3] * a + w[1:2] * _shift_down(a, p, 1) + w[0:1] * _shift_down(a, p, 2) + b_ref[s:s + 1, :]

        up, gate = conv(0), conv(1)
        o_ref[...] = (gate * jax.nn.sigmoid(gate) * up).astype(BF16)

    return pl.pallas_call(
        body, name="conv_fwd", grid=(F // tc, T // tm),
        in_specs=[pl.BlockSpec((2, tm, tc), lambda j, i: (0, i, j)),
                  pl.BlockSpec((2, HALO, tc), lambda j, i: (0, jnp.maximum(i * hb - 1, 0), j)),
                  pl.BlockSpec((2, 3, tc), lambda j, i: (0, 0, j)), pl.BlockSpec((2, tc), lambda j, i: (0, j))],
        out_specs=pl.BlockSpec((tm, tc), lambda j, i: (i, j)), out_shape=jax.ShapeDtypeStruct((T, F), BF16),
        compiler_params=_params(("arbitrary", "arbitrary")))(a, a, cw, cb)


def _conv_bwd(a, dy, cw, cb, S, *, tm=256):
    _, T, F = a.shape
    tc = _col_tile(F)
    tm = _row_tile(S, tm)
    nm = T // tm
    hb = tm // HALO
    TE = tm + HALO
    nxt = lambda j, i: jnp.minimum((i + 1) * hb, T // HALO - 1)

    def body(a_ref, p_ref, n_ref, dy_ref, ndy_ref, w_ref, b_ref, da_ref, dw_ref, db_ref):
        i = pl.program_id(1)
        first = (i * tm) % S == 0
        last = ((i + 1) * tm) % S == 0
        keep_p = jnp.where(first, 0.0, 1.0)
        keep_n = jnp.where(last, 0.0, 1.0)
        rows = lax.broadcasted_iota(jnp.int32, (TE, tc), 0)
        dyf = jnp.concatenate([dy_ref[...].astype(F32), ndy_ref[...].astype(F32) * keep_n], axis=0)

        def pre(s):
            a = jnp.concatenate([a_ref[s], n_ref[s]], axis=0)
            p, w = p_ref[s] * keep_p, w_ref[s]
            a1, a2 = _shift_down(a, p, 1), _shift_down(a, p, 2)
            return a, a1, a2, w[2:3] * a + w[1:2] * a1 + w[0:1] * a2 + b_ref[s:s + 1, :]

        au, au1, au2, up = pre(0)
        ag, ag1, ag2, gate = pre(1)
        sg = jax.nn.sigmoid(gate)
        d_up = dyf * (gate * sg)
        d_gate = dyf * up * (sg * (1.0 + gate * (1.0 - sg)))

        @pl.when(i == 0)
        def _():
            dw_ref[...] = jnp.zeros_like(dw_ref)
            db_ref[...] = jnp.zeros_like(db_ref)

        def back(s, d, a, a1, a2):
            own = jnp.where(rows < tm, d, 0.0)
            w = w_ref[s]
            db_ref[s:s + 1, :] += jnp.sum(own, axis=0, keepdims=True)
            dw_ref[s, 2:3, :] += jnp.sum(own * a, axis=0, keepdims=True)
            dw_ref[s, 1:2, :] += jnp.sum(own * a1, axis=0, keepdims=True)
            dw_ref[s, 0:1, :] += jnp.sum(own * a2, axis=0, keepdims=True)
            da = w[2:3] * d + w[1:2] * pltpu.roll(d, TE - 1, 0) + w[0:1] * pltpu.roll(d, TE - 2, 0)
            da_ref[s] = da[:tm].astype(BF16)

        back(0, d_up, au, au1, au2)
        back(1, d_gate, ag, ag1, ag2)

    return pl.pallas_call(
        body, name="conv_bwd", grid=(F // tc, nm),
        in_specs=[pl.BlockSpec((2, tm, tc), lambda j, i: (0, i, j)),
                  pl.BlockSpec((2, HALO, tc), lambda j, i: (0, jnp.maximum(i * hb - 1, 0), j)),
                  pl.BlockSpec((2, HALO, tc), lambda j, i: (0, nxt(j, i), j)),
                  pl.BlockSpec((tm, tc), lambda j, i: (i, j)), pl.BlockSpec((HALO, tc), lambda j, i: (nxt(j, i), j)),
                  pl.BlockSpec((2, 3, tc), lambda j, i: (0, 0, j)), pl.BlockSpec((2, tc), lambda j, i: (0, j))],
        out_specs=[pl.BlockSpec((2, tm, tc), lambda j, i: (0, i, j)), pl.BlockSpec((2, 3, tc), lambda j, i: (0, 0, j)),
                   pl.BlockSpec((2, tc), lambda j, i: (0, j))],
        out_shape=[jax.ShapeDtypeStruct((2, T, F), BF16), jax.ShapeDtypeStruct((2, 3, F), F32),
                   jax.ShapeDtypeStruct((2, F), F32)],
        compiler_params=_params(("arbitrary", "arbitrary")))(a, a, a, dy, dy, cw, cb)


def _bias_index():
    idx = np.arange(F_LEN)
    d = np.where(idx < K_SPAN, idx, idx - F_LEN)
    return np.clip(PAD - d, -REL_CLIP, REL_CLIP) + REL_CLIP


def _roll_rows(x, sign):
    rows = lax.broadcasted_iota(jnp.int32, x.shape, 0)
    step = 1
    while step < Q_BLOCK:
        shift = step if sign > 0 else F_LEN - step
        x = jnp.where((rows & step) != 0, pltpu.roll(x, shift, 1), x)
        step *= 2
    return x


def _bias_expand(frow):
    H = frow.shape[0]

    def body(f_ref, o_ref):
        x = _roll_rows(jnp.broadcast_to(f_ref[...], (Q_BLOCK, F_LEN)), 1)[:, :K_SPAN]
        qc = lax.broadcasted_iota(jnp.int32, (Q_BLOCK, K_SPAN), 0) // CHUNK * CHUNK
        kj = lax.broadcasted_iota(jnp.int32, (Q_BLOCK, K_SPAN), 1)
        o_ref[...] = jnp.where((kj >= qc) & (kj < qc + PAD + CHUNK), x, NEG_INF)

    return pl.pallas_call(
        body, name="bias_expand", grid=(H,),
        in_specs=[pl.BlockSpec((None, 1, F_LEN), lambda h: (h, 0, 0))],
        out_specs=pl.BlockSpec((None, Q_BLOCK, K_SPAN), lambda h: (h, 0, 0)),
        out_shape=jax.ShapeDtypeStruct((H, Q_BLOCK, K_SPAN), F32), compiler_params=_params(("arbitrary",)))(frow)


def _bias_reduce(dbias, n_rel):
    H = dbias.shape[0]
    onehot = jnp.asarray((_bias_index()[:, None] == np.arange(n_rel)[None, :]).astype(np.float32))

    def body(d_ref, oh_ref, o_ref):
        x = jnp.concatenate([d_ref[...], jnp.zeros((Q_BLOCK, F_LEN - K_SPAN), F32)], axis=1)
        row = jnp.sum(_roll_rows(x, -1), axis=0, keepdims=True)
        row8 = jnp.broadcast_to(row, (8, F_LEN))
        o_ref[...] = jnp.dot(row8, oh_ref[...], precision=lax.Precision.HIGHEST, preferred_element_type=F32)[0:1]

    return pl.pallas_call(
        body, name="bias_reduce", grid=(H,),
        in_specs=[pl.BlockSpec((None, Q_BLOCK, K_SPAN), lambda h: (h, 0, 0)),
                  pl.BlockSpec((F_LEN, n_rel), lambda h: (0, 0))],
        out_specs=pl.BlockSpec((None, 1, n_rel), lambda h: (h, 0, 0)),
        out_shape=jax.ShapeDtypeStruct((H, 1, n_rel), F32), compiler_params=_params(("arbitrary",)))(dbias, onehot)


def _attn_specs(S):
    hw = HEADS_PER_STEP * HEAD_DIM
    qspec = pl.BlockSpec((None, Q_BLOCK, hw), lambda g, b, i: (b, i, g))
    kspec = pl.BlockSpec((None, None, S, hw), lambda g, b, i: (0, b, 0, g))
    vspec = pl.BlockSpec((None, None, S, hw), lambda g, b, i: (1, b, 0, g))
    bspec = pl.BlockSpec((HEADS_PER_STEP, Q_BLOCK, K_SPAN), lambda g, b, i: (g, 0, 0))
    return hw, qspec, kspec, vspec, bspec


def _load_padded(k_ref, v_ref, kp, vp):
    kp[:PAD, :] = jnp.zeros((PAD, kp.shape[1]), BF16)
    vp[:PAD, :] = jnp.zeros((PAD, vp.shape[1]), BF16)
    kp[PAD:, :] = k_ref[...]
    vp[PAD:, :] = v_ref[...]


def _attn_exp(q_ref, kp, b_ref, h, q0, before):
    hs = slice(h * HEAD_DIM, (h + 1) * HEAD_DIM)
    kh = kp[pl.ds(q0, K_SPAN), hs]
    s = lax.dot_general(q_ref[:, hs], kh, (((1,), (1,)), ((), ())), preferred_element_type=F32) + b_ref[h] + before
    p = jnp.exp(s - jnp.max(s, axis=-1, keepdims=True))
    return p, 1.0 / jnp.sum(p, axis=-1, keepdims=True), kh


def _before_start(q0):
    kj = lax.broadcasted_iota(jnp.int32, (1, K_SPAN), 1)
    return jnp.where(q0 + kj >= PAD, 0.0, NEG_INF)


def _attn_fwd(q, kv, bias, B, S):
    HD = q.shape[-1]
    hw, qspec, kspec, vspec, bspec = _attn_specs(S)

    def body(q_ref, k_ref, v_ref, b_ref, o_ref, kp, vp):
        i = pl.program_id(2)

        @pl.when(i == 0)
        def _():
            _load_padded(k_ref, v_ref, kp, vp)

        q0 = pl.multiple_of(i * Q_BLOCK, Q_BLOCK)
        before = _before_start(q0)
        for h in range(HEADS_PER_STEP):
            hs = slice(h * HEAD_DIM, (h + 1) * HEAD_DIM)
            p, inv, _ = _attn_exp(q_ref, kp, b_ref, h, q0, before)
            o = jnp.dot(p.astype(BF16), vp[pl.ds(q0, K_SPAN), hs], preferred_element_type=F32) * inv
            o_ref[:, hs] = o.astype(BF16)

    return pl.pallas_call(
        body, name="attn_fwd", grid=(HD // hw, B, S // Q_BLOCK), in_specs=[qspec, kspec, vspec, bspec],
        out_specs=qspec, out_shape=jax.ShapeDtypeStruct((B, S, HD), BF16),
        scratch_shapes=[pltpu.VMEM((S + PAD, hw), BF16), pltpu.VMEM((S + PAD, hw), BF16)],
        compiler_params=_params(("arbitrary", "arbitrary", "arbitrary")))(q, kv, kv, bias)


def _attn_bwd(q, kv, bias, do, B, S):
    HD = q.shape[-1]
    H = HD // HEAD_DIM
    hw, qspec, kspec, vspec, bspec = _attn_specs(S)
    scale = HEAD_DIM ** -0.5
    nq = S // Q_BLOCK

    def body(q_ref, k_ref, v_ref, b_ref, do_ref, dq_ref, dkv_ref, db_ref, kp, vp, dk_acc, dv_acc):
        b, i = pl.program_id(1), pl.program_id(2)
        q0 = pl.multiple_of(i * Q_BLOCK, Q_BLOCK)

        @pl.when(i == 0)
        def _():
            _load_padded(k_ref, v_ref, kp, vp)
            dk_acc[...] = jnp.zeros_like(dk_acc)
            dv_acc[...] = jnp.zeros_like(dv_acc)

        @pl.when((i == 0) & (b == 0))
        def _():
            db_ref[...] = jnp.zeros_like(db_ref)

        before = _before_start(q0)
        for h in range(HEADS_PER_STEP):
            hs = slice(h * HEAD_DIM, (h + 1) * HEAD_DIM)
            p, inv, kh = _attn_exp(q_ref, kp, b_ref, h, q0, before)
            p = p * inv
            doh = do_ref[:, hs]
            dp = lax.dot_general(doh, vp[pl.ds(q0, K_SPAN), hs], (((1,), (1,)), ((), ())),
                                 preferred_element_type=F32)
            ds = p * (dp - jnp.sum(p * dp, axis=-1, keepdims=True))
            db_ref[h] += ds
            dsb = ds.astype(BF16)
            dq_ref[:, hs] = (jnp.dot(dsb, kh, preferred_element_type=F32) * scale).astype(BF16)
            dk_acc[pl.ds(q0, K_SPAN), hs] += lax.dot_general(dsb, q_ref[:, hs], (((0,), (0,)), ((), ())),
                                                              preferred_element_type=F32)
            dv_acc[pl.ds(q0, K_SPAN), hs] += lax.dot_general(p.astype(BF16), doh, (((0,), (0,)), ((), ())),
                                                              preferred_element_type=F32)

        @pl.when(i == nq - 1)
        def _():
            dkv_ref[0] = dk_acc[PAD:, :].astype(BF16)
            dkv_ref[1] = dv_acc[PAD:, :].astype(BF16)

    return pl.pallas_call(
        body, name="attn_bwd", grid=(HD // hw, B, nq), in_specs=[qspec, kspec, vspec, bspec, qspec],
        out_specs=[qspec, pl.BlockSpec((2, None, S, hw), lambda g, b, i: (0, b, 0, g)), bspec],
        out_shape=[jax.ShapeDtypeStruct((B, S, HD), BF16), jax.ShapeDtypeStruct((2, B, S, HD), BF16),
                   jax.ShapeDtypeStruct((H, Q_BLOCK, K_SPAN), F32)],
        scratch_shapes=[pltpu.VMEM((S + PAD, hw), BF16), pltpu.VMEM((S + PAD, hw), BF16),
                        pltpu.VMEM((S + PAD, hw), F32), pltpu.VMEM((S + PAD, hw), F32)],
        compiler_params=_params(("arbitrary", "arbitrary", "arbitrary")))(q, kv, kv, bias, do)


def _loss_head(h, g, target, *, tm=512):
    T, D = h.shape
    tm = _row_tile(T, tm)

    def body(h_ref, g_ref, t_ref, dh_ref, loss_ref, dg_ref):
        @pl.when(pl.program_id(0) == 0)
        def _():
            loss_ref[...] = jnp.zeros_like(loss_ref)
            dg_ref[...] = jnp.zeros_like(dg_ref)

        n, r = _rms(h_ref[...])
        g = g_ref[...]
        e = n * g - t_ref[...]
        loss_ref[...] += 0.5 * jnp.sum(jnp.mean(e * e, axis=-1, keepdims=True), axis=0, keepdims=True)
        dy = e * (1.0 / D)
        dg_ref[...] += jnp.sum(dy * n, axis=0, keepdims=True)
        t = dy * g
        dh_ref[...] = r * (t - n * jnp.mean(t * n, axis=-1, keepdims=True))

    row = pl.BlockSpec((tm, D), lambda i: (i, 0))
    return pl.pallas_call(
        body, name="loss_head", grid=(T // tm,), in_specs=[row, pl.BlockSpec((1, D), lambda i: (0, 0)), row],
        out_specs=[row, pl.BlockSpec((8, 128), lambda i: (0, 0)), pl.BlockSpec((1, D), lambda i: (0, 0))],
        out_shape=[jax.ShapeDtypeStruct((T, D), F32), jax.ShapeDtypeStruct((8, 128), F32),
                   jax.ShapeDtypeStruct((1, D), F32)],
        compiler_params=_params(("arbitrary",)))(h, g.reshape(1, D), target)


def _sub_rows(R):
    for cand in (256, 352, 128, 64, 8):
        if R % cand == 0 and R > cand:
            return cand
    return R


def _adamw(w, g, m, v, *, name):
    R, C = w.shape
    tr = _sub_rows(R)

    def body(w_ref, g_ref, m_ref, v_ref, d_ref, nm_ref, nv_ref):
        g = g_ref[...]
        m = ADAM_B1 * m_ref[...] + (1.0 - ADAM_B1) * g
        v = ADAM_B2 * v_ref[...] + (1.0 - ADAM_B2) * (g * g)
        m_hat = m / (1.0 - ADAM_B1 ** ADAM_STEP)
        v_hat = v / (1.0 - ADAM_B2 ** ADAM_STEP)
        d_ref[...] = -ADAM_LR * (m_hat / (jnp.sqrt(v_hat) + ADAM_EPS) + ADAM_WD * w_ref[...])
        nm_ref[...] = m
        nv_ref[...] = v

    spec = pl.BlockSpec((tr, C), lambda i: (i, 0))
    return pl.pallas_call(body, name=name, grid=(R // tr,), in_specs=[spec] * 4, out_specs=[spec] * 3,
                          out_shape=[jax.ShapeDtypeStruct((R, C), F32)] * 3,
                          compiler_params=_params(("arbitrary",)))(w, g, m, v)


def _add_pair(u0, u1, got, core, *, name):
    n4, R, C = got.shape
    rows = n4 * R
    tr = 512 if rows % 512 == 0 else R
    flat = lambda a: a.reshape(rows, C)

    def body(c_ref, u0_ref, u1_ref, got_ref, o_ref):
        mine = jnp.where(c_ref[0] == 0, u0_ref[...].astype(F32), u1_ref[...].astype(F32))
        o_ref[...] = (mine + got_ref[...].astype(F32)).astype(BF16)

    spec = pl.BlockSpec((tr, C), lambda i, c: (i, 0))
    grid_spec = pltpu.PrefetchScalarGridSpec(num_scalar_prefetch=1, grid=(rows // tr,), in_specs=[spec] * 3,
                                             out_specs=spec)
    out = pl.pallas_call(body, name=name, grid_spec=grid_spec, out_shape=jax.ShapeDtypeStruct((rows, C), BF16),
                         compiler_params=_params(("arbitrary",)))(core.reshape(1), flat(u0), flat(u1), flat(got))
    return out.reshape(n4, R, C)


def _sum_chips(w, own, got, pos, *, name):
    _, R, C = own.shape
    tr = _sub_rows(R)
    nr = R // tr

    def body(p_ref, own_ref, got_ref, o_ref):
        o_ref[...] = (own_ref[...].astype(F32) + got_ref[0].astype(F32) + got_ref[1].astype(F32)
                      + got_ref[2].astype(F32))

    if w.L == 2:
        out_map = lambda i, p: (p[1], i, 0)
    elif w.row_sharded:
        out_map = lambda i, p: (0, i, p[1])
    else:
        out_map = lambda i, p: (0, p[1] * nr + i, 0)
    grid_spec = pltpu.PrefetchScalarGridSpec(
        num_scalar_prefetch=1, grid=(nr,),
        in_specs=[pl.BlockSpec((None, tr, C), lambda i, p: (p[0], i, 0)),
                  pl.BlockSpec((3, tr, C), lambda i, p: (0, i, 0))],
        out_specs=pl.BlockSpec((None, tr, C), out_map))
    return pl.pallas_call(body, name=name, grid_spec=grid_spec,
                          out_shape=jax.ShapeDtypeStruct((w.L, w.ks, w.ns), F32),
                          compiler_params=_params(("arbitrary",)))(pos, own, got)


def _mesh_pos():
    return lax.axis_index("x"), lax.axis_index("y"), lax.axis_index("c")


def _other_chips(x, y):
    return [(1 - x, y), (x, 1 - y), (1 - x, 1 - y)]


ANY = pl.BlockSpec(memory_space=pl.ANY)


class _W:
    def __init__(self, name, shard, row_sharded):
        self.name = name
        self.L, ks, ns = shard.shape
        self.row_sharded = row_sharded
        self.K, self.N = (ks * N_CHIPS, ns) if row_sharded else (ks, ns * N_CHIPS)
        self.ks, self.ns = ks, ns

    def shard_of(self, full, j):
        if self.row_sharded:
            return full.at[:, pl.ds(j * self.ks, self.ks), :]
        return full.at[:, :, pl.ds(j * self.ns, self.ns)]

    def half_of(self, shard, c):
        if self.L == 2:
            return shard.at[pl.ds(c, 1)]
        if self.row_sharded:
            return shard.at[:, :, pl.ds(c * (self.ns // 2), self.ns // 2)]
        return shard.at[:, pl.ds(c * (self.ks // 2), self.ks // 2), :]


def _gather_weights(ws, shards):
    nw = len(ws)

    def body(*refs):
        src, dst, stage = refs[:nw], refs[nw:2 * nw], refs[2 * nw:3 * nw]
        send_sems, recv_sems, load_sems, store_sems = refs[3 * nw:]
        x, y, c = _mesh_pos()
        me = 2 * x + y
        sibling = (x, y, 1 - c)
        chips = _other_chips(x, y)

        def remote(k, s, d, to):
            return pltpu.make_async_remote_copy(src_ref=s, dst_ref=d, send_sem=send_sems.at[k],
                                                recv_sem=recv_sems.at[k], device_id=to, device_id_type=MESH)

        loads = [pltpu.make_async_copy(src[i], stage[i], load_sems.at[i]) for i in range(nw)]
        for cp in loads:
            cp.start()
        sends = []
        for i, w in enumerate(ws):
            for f, (px, py) in enumerate(chips):
                sends.append(remote(6 * i + f, w.half_of(src[i], c), w.half_of(w.shard_of(dst[i], me), c),
                                    (px, py, c)))
        for cp in sends:
            cp.start()
        stores = [pltpu.make_async_copy(stage[i], w.shard_of(dst[i], me), store_sems.at[i])
                  for i, w in enumerate(ws)]
        for ld, st in zip(loads, stores):
            ld.wait()
            st.start()
        for i, w in enumerate(ws):
            for f, (px, py) in enumerate(chips):
                landed = w.half_of(w.shard_of(dst[i], 2 * px + py), c)
                remote(6 * i + f, landed, landed, (px, py, c)).wait_recv()
                fwd = remote(6 * i + 3 + f, landed, landed, sibling)
                fwd.start()
                sends.append(fwd)
        for i, w in enumerate(ws):
            for f, (px, py) in enumerate(chips):
                landed = w.half_of(w.shard_of(dst[i], 2 * px + py), 1 - c)
                remote(6 * i + 3 + f, landed, landed, sibling).wait_recv()
        for cp in sends:
            cp.wait_send()
        for cp in stores:
            cp.wait()

    return pl.pallas_call(
        body, name="gather_weights", in_specs=[ANY] * nw, out_specs=[ANY] * nw,
        out_shape=[jax.ShapeDtypeStruct((w.L, w.K, w.N), BF16) for w in ws],
        scratch_shapes=[pltpu.VMEM((w.L, w.ks, w.ns), BF16) for w in ws]
        + [pltpu.SemaphoreType.DMA((6 * nw,)), pltpu.SemaphoreType.DMA((6 * nw,)), pltpu.SemaphoreType.DMA((nw,)),
           pltpu.SemaphoreType.DMA((nw,))],
        compiler_params=_params(has_side_effects=True))(*shards)


def _swap_units(units):
    nw = len(units)

    def body(*refs):
        u0, u1, got = refs[:nw], refs[nw:2 * nw], refs[2 * nw:3 * nw]
        send_sems, recv_sems = refs[3 * nw:]
        x, y, c = _mesh_pos()
        sibling = (x, y, 1 - c)
        for i in range(nw):
            for cc in (0, 1):
                @pl.when(c == cc)
                def _(i=i, cc=cc):
                    src = (u1, u0)[cc][i]
                    pltpu.make_async_remote_copy(src_ref=src, dst_ref=got[i], send_sem=send_sems.at[i],
                                                 recv_sem=recv_sems.at[i], device_id=sibling,
                                                 device_id_type=MESH).start()
        for i in range(nw):
            pltpu.make_async_remote_copy(src_ref=u0[i], dst_ref=got[i], send_sem=send_sems.at[i],
                                         recv_sem=recv_sems.at[i], device_id=sibling, device_id_type=MESH).wait()

    flat0 = [u[0] for u in units]
    flat1 = [u[1] for u in units]
    return pl.pallas_call(
        body, name="swap_units", in_specs=[ANY] * (2 * nw), out_specs=[ANY] * nw,
        out_shape=[jax.ShapeDtypeStruct(u[0].shape, BF16) for u in units],
        scratch_shapes=[pltpu.SemaphoreType.DMA((nw,)), pltpu.SemaphoreType.DMA((nw,))],
        compiler_params=_params(has_side_effects=True))(*flat0, *flat1)


def _scatter_units(sums):
    nw = len(sums)

    def body(*refs):
        src, got = refs[:nw], refs[nw:2 * nw]
        send_sems, recv_sems = refs[2 * nw:]
        x, y, c = _mesh_pos()
        chips = _other_chips(x, y)
        copies = []
        for i in range(nw):
            for f, (px, py) in enumerate(chips):
                cp = pltpu.make_async_remote_copy(src_ref=src[i].at[2 * px + py], dst_ref=got[i].at[f],
                                                  send_sem=send_sems.at[3 * i + f], recv_sem=recv_sems.at[3 * i + f],
                                                  device_id=(px, py, c), device_id_type=MESH)
                cp.start()
                copies.append(cp)
        for cp in copies:
            cp.wait()

    return pl.pallas_call(
        body, name="scatter_units", in_specs=[ANY] * nw, out_specs=[ANY] * nw,
        out_shape=[jax.ShapeDtypeStruct((3,) + s.shape[1:], BF16) for s in sums],
        scratch_shapes=[pltpu.SemaphoreType.DMA((3 * nw,)), pltpu.SemaphoreType.DMA((3 * nw,))],
        compiler_params=_params(has_side_effects=True))(*sums)


def _join_halves(ws, shards):
    nw = len(ws)

    def body(*refs):
        buf = refs[nw:2 * nw]
        send_sems, recv_sems = refs[2 * nw:]
        x, y, c = _mesh_pos()
        sibling = (x, y, 1 - c)

        def copy(i, w, half):
            region = w.half_of(buf[i], half)
            return pltpu.make_async_remote_copy(src_ref=region, dst_ref=region, send_sem=send_sems.at[i],
                                                recv_sem=recv_sems.at[i], device_id=sibling, device_id_type=MESH)

        sends = [copy(i, w, c) for i, w in enumerate(ws)]
        for cp in sends:
            cp.start()
        for i, w in enumerate(ws):
            copy(i, w, 1 - c).wait_recv()
        for cp in sends:
            cp.wait_send()

    return pl.pallas_call(
        body, name="join_halves", in_specs=[ANY] * nw, out_specs=[ANY] * nw,
        out_shape=[jax.ShapeDtypeStruct((w.L, w.ks, w.ns), F32) for w in ws],
        input_output_aliases={i: i for i in range(nw)},
        scratch_shapes=[pltpu.SemaphoreType.DMA((nw,)), pltpu.SemaphoreType.DMA((nw,))],
        compiler_params=_params(has_side_effects=True))(*shards)


def _allreduce_small(vec):
    R = vec.shape[0]

    def body(x_ref, o_ref, buf, send_sems, recv_sems):
        x, y, c = _mesh_pos()
        me, sibling = (x, y, c), (x, y, 1 - c)
        chips = _other_chips(x, y)

        def slot(px, py, pc):
            return buf.at[4 * px + 2 * py + pc]

        def copy(k, block, to, src=None):
            return pltpu.make_async_remote_copy(src_ref=slot(*block) if src is None else src, dst_ref=slot(*block),
                                                send_sem=send_sems.at[k], recv_sem=recv_sems.at[k], device_id=to,
                                                device_id_type=MESH)

        first = [copy(0, me, sibling, src=x_ref)] + [copy(1 + f, me, (*chip, c), src=x_ref)
                                                     for f, chip in enumerate(chips)]
        for cp in first:
            cp.start()
        passed = [copy(4 + f, (*chip, c), sibling) for f, chip in enumerate(chips)]
        for f, chip in enumerate(chips):
            copy(1 + f, (*chip, c), me).wait_recv()
            passed[f].start()
        copy(0, sibling, me).wait_recv()
        for f, chip in enumerate(chips):
            copy(4 + f, (*chip, 1 - c), me).wait_recv()
        for cp in first + passed:
            cp.wait_send()
        slot(*me)[...] = x_ref[...]
        acc = buf[0]
        for d in range(1, 8):
            acc = acc + buf[d]
        o_ref[...] = acc

    return pl.pallas_call(
        body, name="allreduce_small", in_specs=[pl.BlockSpec(memory_space=pltpu.VMEM)],
        out_specs=pl.BlockSpec(memory_space=pltpu.VMEM), out_shape=jax.ShapeDtypeStruct((R, 128), F32),
        scratch_shapes=[pltpu.VMEM((8, R, 128), F32), pltpu.SemaphoreType.DMA((7,)), pltpu.SemaphoreType.DMA((7,))],
        compiler_params=_params())(vec)


def _pack(parts):
    flat = jnp.concatenate([p.reshape(-1).astype(F32) for p in parts])
    n = flat.shape[0]
    pad = (-n) % (64 * 128)
    return jnp.pad(flat, (0, pad)).reshape(-1, 128)


def _unpack(vec, shapes):
    flat = vec.reshape(-1)
    out, off = [], 0
    for s in shapes:
        n = int(np.prod(s))
        out.append(flat[off:off + n].reshape(s))
        off += n
    return out


def kernel(x, a_norm_g, a_w_in, a_v_norm_g, a_w_s, a_b_s, a_w_out, kv_norm_g, w_kv, b_norm_g, b_w_q, b_rel_bias, b_w_o, f_norm_g, f_w_in, f_conv_w, f_conv_b, f_w_down, final_norm_g, loss_target, m_a_norm_g, m_a_w_in, m_a_v_norm_g, m_a_w_s, m_a_b_s, m_a_w_out, m_kv_norm_g, m_w_kv, m_b_norm_g, m_b_w_q, m_b_rel_bias, m_b_w_o, m_f_norm_g, m_f_w_in, m_f_conv_w, m_f_conv_b, m_f_w_down, m_final_norm_g, v_a_norm_g, v_a_w_in, v_a_v_norm_g, v_a_w_s, v_a_b_s, v_a_w_out, v_kv_norm_g, v_w_kv, v_b_norm_g, v_b_w_q, v_b_rel_bias, v_b_w_o, v_f_norm_g, v_f_w_in, v_f_conv_w, v_f_conv_b, v_f_w_down, v_final_norm_g):
    B, S, D = x.shape
    T = B * S
    xi, yi, ci = lax.axis_index("x"), lax.axis_index("y"), lax.axis_index("c")
    j_me = (2 * xi + yi).astype(jnp.int32)
    core = ci.astype(jnp.int32)
    pos = jnp.stack([j_me, core])

    w_shards = {"a_w_in": (a_w_in, False), "a_w_out": (a_w_out, True), "w_kv": (w_kv[None], False),
                "b_w_q": (b_w_q, True), "b_w_o": (b_w_o, True), "f_w_in": (f_w_in, False), "f_w_down": (f_w_down, True)}
    names = list(w_shards)
    ws = [_W(n, w_shards[n][0], w_shards[n][1]) for n in names]
    full = dict(zip(names, _gather_weights(ws, [w_shards[n][0].astype(BF16) for n in names])))
    F_in, F_down = full["f_w_in"], full["f_w_down"]

    Wd = a_w_in.shape[1]
    GW = a_v_norm_g.shape[1] * N_CHIPS
    F2 = f_conv_w.shape[2] * N_CHIPS
    Fh = F2 // 2
    nsd, nsg, nsf = a_norm_g.shape[1], a_v_norm_g.shape[1], f_conv_w.shape[2]
    own = (ci == 0).astype(F32)
    place = lambda sh, width, n: lax.dynamic_update_slice_in_dim(
        jnp.zeros(sh.shape[:-1] + (width,), F32), sh * own, j_me * n, axis=sh.ndim - 1)
    gathered = _allreduce_small(_pack([place(a_norm_g, Wd, nsd), place(a_v_norm_g, GW, nsg),
                                       place(f_conv_w, F2, nsf)]))
    a_g, a_vg, conv_w = _unpack(gathered, [(1, Wd), (1, GW), (2, 3, F2)])
    conv_w2 = conv_w.reshape(2, 3, 2, Fh).transpose(0, 2, 1, 3)
    conv_b2 = f_conv_b.reshape(2, 2, Fh)

    h0 = x.reshape(T, D)
    target = loss_target.reshape(T, D)
    bs_tile = jnp.repeat(a_b_s[0].T, GROUP_DIM, axis=1)
    ws_a = a_w_s[0]
    scale = HEAD_DIM ** -0.5
    HD = b_w_q.shape[2]
    H = HD // HEAD_DIM
    n_rel = b_rel_bias.shape[-1]
    frow = b_rel_bias[0][:, _bias_index()].reshape(H, 1, F_LEN)
    bias = _bias_expand(frow)

    def ffn_fwd(h, l):
        a = _mm(h, F_in, layer=l, norm_g=f_norm_g[l], split_out=True, name=f"ffn{l}_in")
        yff = _conv_fwd(a, conv_w2[l], conv_b2[l], S)
        return _mm(yff, F_down, layer=l, res=h, name=f"ffn{l}_down"), a, yff

    zp = _mm(h0, full["a_w_in"], layer=0, norm_g=a_g[0], name="a_in")
    out_a = _gate_fwd(zp, a_vg, ws_a, bs_tile)
    h1 = _mm(out_a, full["a_w_out"], layer=0, res=h0, name="a_out")
    h2, a0, yff0 = ffn_fwd(h1, 0)
    kv = _mm(h2, full["w_kv"], layer=0, norm_g=kv_norm_g, out_dtype=BF16, split_out=True, name="kv")
    q = _mm(h2, full["b_w_q"], layer=0, norm_g=b_norm_g[0], scale=scale, out_dtype=BF16, name="q")
    kv4, q3 = kv.reshape(2, B, S, HD), q.reshape(B, S, HD)
    o = _attn_fwd(q3, kv4, bias, B, S).reshape(T, HD)
    h3 = _mm(o, full["b_w_o"], layer=0, res=h2, name="attn_out")
    h4, a1, yff1 = ffn_fwd(h3, 1)
    dh, loss8, dg_final = _loss_head(h4, final_norm_g, target)

    units = {}

    def ffn_bwd(dh, h, a, yff, l):
        dyff = _mm(dh, F_down, layer=l, trans_w=True, out_dtype=BF16, name=f"ffn{l}_down_dx")
        units[("f_w_down", l)] = _mm_tn(yff, dh, rows_are_shards=True, name=f"ffn{l}_down_dw")
        da, dcw, dcb = _conv_bwd(a, dyff, conv_w2[l], conv_b2[l], S)
        units[("f_w_in", l)] = _mm_tn(h, da, norm_g=f_norm_g[l], split_y=True, name=f"ffn{l}_in_dw")
        dh, dg = _mm(da, F_in, layer=l, trans_w=True, split_x=True, bwd=(h, f_norm_g[l], dh), tm=256,
                     name=f"ffn{l}_in_dx")
        return dh, dg, dcw, dcb

    def halves_tn(key, xx, dy, norm_g, row_sharded, split_y=False):
        for hc in (0, 1):
            if row_sharded:
                half = dy.shape[1] // 2
                units[(key, hc)] = _mm_tn(xx, dy, norm_g=norm_g, y_cols=(hc * half, half), rows_are_shards=True,
                                          name=f"{key}_dw{hc}")
            else:
                half = xx.shape[1] // 2
                units[(key, hc)] = _mm_tn(xx, dy, norm_g=norm_g, x_cols=(hc * half, half), split_y=split_y,
                                          name=f"{key}_dw{hc}")

    dh, dg_f1, dcw1, dcb1 = ffn_bwd(dh, h3, a1, yff1, 1)
    do = _mm(dh, full["b_w_o"], layer=0, trans_w=True, out_dtype=BF16, name="attn_out_dx")
    halves_tn("b_w_o", o, dh, None, True)
    dq, dkv, dbias = _attn_bwd(q3, kv4, bias, do.reshape(B, S, HD), B, S)
    d_rel = _bias_reduce(dbias, n_rel).reshape(1, H, n_rel)
    dq, dkv = dq.reshape(T, HD), dkv.reshape(2, T, HD)
    halves_tn("b_w_q", h2, dq, b_norm_g[0], True)
    dh, dg_b = _mm(dq, full["b_w_q"], layer=0, trans_w=True, bwd=(h2, b_norm_g[0], dh), name="q_dx")
    halves_tn("w_kv", h2, dkv, kv_norm_g, False, split_y=True)
    dh, dg_kv = _mm(dkv, full["w_kv"], layer=0, trans_w=True, split_x=True, bwd=(h2, kv_norm_g, dh), name="kv_dx")
    dh, dg_f0, dcw0, dcb0 = ffn_bwd(dh, h1, a0, yff0, 0)
    d_out = _mm(dh, full["a_w_out"], layer=0, trans_w=True, out_dtype=BF16, name="a_out_dx")
    halves_tn("a_w_out", out_a, dh, None, True)
    dzp, dws, dbs, dgv = _gate_bwd(zp, d_out, a_vg, ws_a, bs_tile)
    halves_tn("a_w_in", h0, dzp, a_g[0], False)
    grad_x, dg_a = _mm(dzp, full["a_w_in"], layer=0, trans_w=True, bwd=(h0, a_g[0], dh), name="a_in_dx")

    pairs = [(units[(n, 0)], units[(n, 1)]) for n in names]
    got = _swap_units(pairs)
    sums = [_add_pair(p[0], p[1], g_, core, name=f"pair_{n}") for n, p, g_ in zip(names, pairs, got)]
    recv = _scatter_units(sums)
    halves = [_sum_chips(w, s_, r_, pos, name=f"chips_{w.name}") for w, s_, r_ in zip(ws, sums, recv)]
    g_big = dict(zip(names, _join_halves(ws, halves)))
    g_big["w_kv"] = g_big["w_kv"][0]

    to_flat = lambda d: d.transpose(1, 0, 2).reshape(3, F2)
    small = {"a_norm_g": dg_a, "a_v_norm_g": dgv, "a_w_s": dws[None], "a_b_s": dbs[None], "kv_norm_g": dg_kv[0],
             "b_norm_g": dg_b, "b_rel_bias": d_rel, "f_norm_g": jnp.concatenate([dg_f0, dg_f1], axis=0),
             "f_conv_w": jnp.stack([to_flat(dcw0), to_flat(dcw1)]),
             "f_conv_b": jnp.stack([dcb0.reshape(F2), dcb1.reshape(F2)]), "final_norm_g": dg_final[0]}
    snames = list(small)
    red = _allreduce_small(_pack([small[n] for n in snames] + [loss8[0:1, 0:1]]))
    parts = _unpack(red, [small[n].shape for n in snames] + [(1,)])
    g_small = dict(zip(snames, parts[:-1]))
    loss = parts[-1][0]
    g_small["a_norm_g"] = lax.dynamic_slice_in_dim(g_small["a_norm_g"], j_me * nsd, nsd, axis=1)
    g_small["a_v_norm_g"] = lax.dynamic_slice_in_dim(g_small["a_v_norm_g"], j_me * nsg, nsg, axis=1)
    g_small["f_conv_w"] = lax.dynamic_slice_in_dim(g_small["f_conv_w"], j_me * nsf, nsf, axis=2)

    given = dict(a_norm_g=(a_norm_g, m_a_norm_g, v_a_norm_g), a_w_in=(a_w_in, m_a_w_in, v_a_w_in),
                 a_v_norm_g=(a_v_norm_g, m_a_v_norm_g, v_a_v_norm_g), a_w_s=(a_w_s, m_a_w_s, v_a_w_s),
                 a_b_s=(a_b_s, m_a_b_s, v_a_b_s), a_w_out=(a_w_out, m_a_w_out, v_a_w_out),
                 kv_norm_g=(kv_norm_g, m_kv_norm_g, v_kv_norm_g), w_kv=(w_kv, m_w_kv, v_w_kv),
                 b_norm_g=(b_norm_g, m_b_norm_g, v_b_norm_g), b_w_q=(b_w_q, m_b_w_q, v_b_w_q),
                 b_rel_bias=(b_rel_bias, m_b_rel_bias, v_b_rel_bias), b_w_o=(b_w_o, m_b_w_o, v_b_w_o),
                 f_norm_g=(f_norm_g, m_f_norm_g, v_f_norm_g), f_w_in=(f_w_in, m_f_w_in, v_f_w_in),
                 f_conv_w=(f_conv_w, m_f_conv_w, v_f_conv_w), f_conv_b=(f_conv_b, m_f_conv_b, v_f_conv_b),
                 f_w_down=(f_w_down, m_f_w_down, v_f_w_down), final_norm_g=(final_norm_g, m_final_norm_g, v_final_norm_g))
    order = list(given)
    grads, deltas, new_m, new_v = {}, {}, {}, {}
    for n in names:
        w_, m_, v_ = given[n]
        g_ = g_big[n]
        C = w_.shape[-1]
        d2, m2, v2 = _adamw(w_.reshape(-1, C), g_.reshape(-1, C), m_.reshape(-1, C), v_.reshape(-1, C),
                            name=f"adamw_{n}")
        grads[n], deltas[n], new_m[n], new_v[n] = g_.reshape(w_.shape), d2.reshape(w_.shape), m2.reshape(w_.shape), \
            v2.reshape(w_.shape)
    sm = [n for n in order if n not in names]
    d2, m2, v2 = _adamw(_pack([given[n][0] for n in sm]), _pack([g_small[n].reshape(given[n][0].shape) for n in sm]),
                        _pack([given[n][1] for n in sm]), _pack([given[n][2] for n in sm]), name="adamw_small")
    shapes = [given[n][0].shape for n in sm]
    for n, d_, m_, v_ in zip(sm, _unpack(d2, shapes), _unpack(m2, shapes), _unpack(v2, shapes)):
        grads[n], deltas[n], new_m[n], new_v[n] = g_small[n].reshape(given[n][0].shape), d_, m_, v_

    return (loss, grad_x.reshape(B, S, D), *[grads[n] for n in order], *[deltas[n] for n in order],
            *[new_m[n] for n in order], *[new_v[n] for n in order])
```

```python
import math

import numpy as np
import jax
import jax.numpy as jnp
from jax import lax
from jax.experimental import pallas as pl
from jax.experimental.pallas import tpu as pltpu

F32 = jnp.float32
BF16 = jnp.bfloat16
MESH = pl.DeviceIdType.MESH

EPS = 1e-6
NEG_INF = -1e30
CHUNK = 64
GMLP_BLOCK = 128
GROUP_DIM = 128
HEAD_DIM = 64
LEFT_CHUNKS = 8
PAD = LEFT_CHUNKS * CHUNK
REL_CLIP = 128
Q_BLOCK = 128
K_SPAN = PAD + Q_BLOCK
F_LEN = 768
HEADS_PER_STEP = 4
N_CHIPS = 4

ADAM_LR = 0.001
ADAM_B1 = 0.9
ADAM_B2 = 0.999
ADAM_EPS = 1e-08
ADAM_WD = 0.01
ADAM_STEP = 10

VMEM_LIMIT = 56 * 1024 * 1024


def _params(sem=None, **kw):
    if sem is not None:
        kw["dimension_semantics"] = sem
    return pltpu.CompilerParams(vmem_limit_bytes=VMEM_LIMIT, **kw)


def _rms(xf):
    r = lax.rsqrt(jnp.mean(xf * xf, axis=-1, keepdims=True) + EPS)
    return xf * r, r


def _gelu(x):
    c = math.sqrt(2.0 / math.pi)
    return 0.5 * x * (1.0 + jnp.tanh(c * (x + 0.044715 * x * x * x)))


def _gelu_grad(x):
    c = math.sqrt(2.0 / math.pi)
    t = jnp.tanh(c * (x + 0.044715 * x * x * x))
    return 0.5 * (1.0 + t) + 0.5 * x * (1.0 - t * t) * c * (1.0 + 3.0 * 0.044715 * x * x)


def _col_tile(n):
    if n <= 1024:
        return n
    for t in (1408, 1024, 512):
        if n % t == 0:
            return t
    raise ValueError(n)


def _row_tile(t, want):
    while t % want:
        want //= 2
    return want


def _mm(x, w, *, name, layer=None, trans_w=False, norm_g=None, res=None, scale=None, out_dtype=F32, bwd=None,
        split_out=False, split_x=False, tm=512):
    T = x.shape[-2]
    K = 2 * x.shape[-1] if split_x else x.shape[-1]
    N = w.shape[-2] if trans_w else w.shape[-1]
    tn = N if bwd is not None else _col_tile(N // 2 if split_out else N)
    tm = _row_tile(T, tm)
    nn, nm = N // tn, T // tm
    has_norm, has_res, has_bwd = norm_g is not None, res is not None, bwd is not None
    dims = (((1,), (1,)), ((), ())) if trans_w else (((1,), (0,)), ((), ()))

    def body(*refs):
        it = iter(refs)
        x_ref, w_ref = next(it), next(it)
        g_ref = next(it) if has_norm else None
        res_ref = next(it) if has_res else None
        if has_bwd:
            h_ref, bg_ref, dh_ref = next(it), next(it), next(it)
        o_ref = next(it)
        if split_x:
            kh = K // 2
            acc = lax.dot_general(x_ref[0].astype(BF16), w_ref[:, :kh] if trans_w else w_ref[:kh, :], dims,
                                  preferred_element_type=F32)
            acc = acc + lax.dot_general(x_ref[1].astype(BF16), w_ref[:, kh:] if trans_w else w_ref[kh:, :], dims,
                                        preferred_element_type=F32)
        else:
            xv = x_ref[...]
            if has_norm:
                xv = _rms(xv.astype(F32))[0] * g_ref[...]
            acc = lax.dot_general(xv.astype(BF16), w_ref[...], dims, preferred_element_type=F32)
        if scale is not None:
            acc = acc * scale
        if has_res:
            acc = acc + res_ref[...]
        if has_bwd:
            dg_ref = next(it)
            n, r = _rms(h_ref[...])

            @pl.when(pl.program_id(1) == 0)
            def _():
                dg_ref[...] = jnp.zeros_like(dg_ref)

            dg_ref[...] += jnp.sum(acc * n, axis=0, keepdims=True)
            t = acc * bg_ref[...]
            o_ref[...] = dh_ref[...] + r * (t - n * jnp.mean(t * n, axis=-1, keepdims=True))
        else:
            o_ref[...] = acc.astype(out_dtype)

    lead = () if layer is None else (None,)
    lidx = () if layer is None else (layer,)
    ins = [x, w]
    xspec = (pl.BlockSpec((2, tm, K // 2), lambda n, m: (0, m, 0)) if split_x
             else pl.BlockSpec((tm, K), lambda n, m: (m, 0)))
    wspec = (pl.BlockSpec(lead + (tn, K), lambda n, m: lidx + (n, 0)) if trans_w
             else pl.BlockSpec(lead + (K, tn), lambda n, m: lidx + (0, n)))
    in_specs = [xspec, wspec]
    if has_norm:
        ins.append(norm_g.reshape(1, K))
        in_specs.append(pl.BlockSpec((1, K), lambda n, m: (0, 0)))
    if has_res:
        ins.append(res)
        in_specs.append(pl.BlockSpec((tm, tn), lambda n, m: (m, n)))
    if split_out:
        per = nn // 2
        out_shape = [jax.ShapeDtypeStruct((2, T, N // 2), out_dtype)]
        out_specs = [pl.BlockSpec((None, tm, tn), lambda n, m: (n // per, m, n % per))]
    else:
        out_shape = [jax.ShapeDtypeStruct((T, N), F32 if has_bwd else out_dtype)]
        out_specs = [pl.BlockSpec((tm, tn), lambda n, m: (m, n))]
    if has_bwd:
        h, g, dh = bwd
        ins += [h, g.reshape(1, N), dh]
        in_specs += [pl.BlockSpec((tm, N), lambda n, m: (m, 0)), pl.BlockSpec((1, N), lambda n, m: (0, 0)),
                     pl.BlockSpec((tm, N), lambda n, m: (m, 0))]
        out_shape.append(jax.ShapeDtypeStruct((1, N), F32))
        out_specs.append(pl.BlockSpec((1, N), lambda n, m: (0, 0)))
    out = pl.pallas_call(body, name=name, grid=(nn, nm), in_specs=in_specs, out_specs=out_specs, out_shape=out_shape,
                         compiler_params=_params(("arbitrary", "arbitrary")))(*ins)
    return out if has_bwd else out[0]


def _mm_tn(x, dy, *, name, norm_g=None, rows_are_shards=False, split_y=False, tt=512):
    T, K = x.shape
    N = 2 * dy.shape[-1] if split_y else dy.shape[-1]
    if rows_are_shards:
        tn, nn = N // 2, 2
        R, C = K // N_CHIPS, tn
    else:
        tn, nn = N // N_CHIPS, N_CHIPS
        R, C = K // 2, tn
    tt = _row_tile(T, tt)
    nt = T // tt
    has_norm = norm_g is not None

    def body(*refs):
        it = iter(refs)
        x_ref, y_ref = next(it), next(it)
        g_ref = next(it) if has_norm else None
        o_ref, acc_ref = next(it), next(it)
        t = pl.program_id(1)

        @pl.when(t == 0)
        def _():
            acc_ref[...] = jnp.zeros_like(acc_ref)

        xv = x_ref[...]
        if has_norm:
            xv = _rms(xv.astype(F32))[0] * g_ref[...]
        acc_ref[...] += lax.dot_general(xv.astype(BF16), y_ref[...].astype(BF16), (((0,), (0,)), ((), ())),
                                        preferred_element_type=F32)

        @pl.when(t == nt - 1)
        def _():
            a = acc_ref[...].astype(BF16)
            o_ref[...] = a.reshape(N_CHIPS, R, tn) if rows_are_shards else a.reshape(2, R, tn)

    ins = [x, dy]
    if split_y:
        per = (N // 2) // tn
        yspec = pl.BlockSpec((None, tt, tn), lambda n, t: (n // per, t, n % per))
    else:
        yspec = pl.BlockSpec((tt, tn), lambda n, t: (t, n))
    in_specs = [pl.BlockSpec((tt, K), lambda n, t: (t, 0)), yspec]
    if has_norm:
        ins.append(norm_g.reshape(1, K))
        in_specs.append(pl.BlockSpec((1, K), lambda n, t: (0, 0)))
    if rows_are_shards:
        out_spec = pl.BlockSpec((None, N_CHIPS, R, C), lambda n, t: (n, 0, 0, 0))
    else:
        out_spec = pl.BlockSpec((2, None, R, C), lambda n, t: (0, n, 0, 0))
    return pl.pallas_call(body, name=name, grid=(nn, nt), in_specs=in_specs, out_specs=out_spec,
                          out_shape=jax.ShapeDtypeStruct((2, N_CHIPS, R, C), BF16),
                          scratch_shapes=[pltpu.VMEM((K, tn), F32)],
                          compiler_params=_params(("arbitrary", "arbitrary")))(*ins)


def _chunk_mask():
    i = lax.broadcasted_iota(jnp.int32, (GMLP_BLOCK, GMLP_BLOCK), 0) // CHUNK
    j = lax.broadcasted_iota(jnp.int32, (GMLP_BLOCK, GMLP_BLOCK), 1) // CHUNK
    return i >= j


def _gate_fwd(zp, gv, ws, bs_tile, *, tm=256):
    T, W2 = zp.shape
    W = W2 // 2
    G = W // GROUP_DIM
    tm = _row_tile(T, tm)

    def body(zp_ref, gv_ref, ws_ref, bs_ref, o_ref):
        z = _gelu(zp_ref[...])
        u, v = z[:, :W], z[:, W:]
        vn = _rms(v)[0] * gv_ref[...]
        mask = _chunk_mask()
        for g in range(G):
            cs = slice(g * GROUP_DIM, (g + 1) * GROUP_DIM)
            wg = jnp.where(mask, ws_ref[g], 0.0).astype(BF16)
            for b in range(tm // GMLP_BLOCK):
                rs = slice(b * GMLP_BLOCK, (b + 1) * GMLP_BLOCK)
                s = jnp.dot(wg, vn[rs, cs].astype(BF16), preferred_element_type=F32) + bs_ref[:, cs]
                o_ref[rs, cs] = (u[rs, cs] * s).astype(BF16)

    return pl.pallas_call(
        body, name="gate_fwd", grid=(T // tm,),
        in_specs=[pl.BlockSpec((tm, W2), lambda i: (i, 0)), pl.BlockSpec((1, W), lambda i: (0, 0)),
                  pl.BlockSpec((G, GMLP_BLOCK, GMLP_BLOCK), lambda i: (0, 0, 0)),
                  pl.BlockSpec((GMLP_BLOCK, W), lambda i: (0, 0))],
        out_specs=pl.BlockSpec((tm, W), lambda i: (i, 0)), out_shape=jax.ShapeDtypeStruct((T, W), BF16),
        compiler_params=_params(("arbitrary",)))(zp, gv, ws, bs_tile)


def _gate_bwd(zp, d_out, gv, ws, bs_tile, *, tm=256):
    T, W2 = zp.shape
    W = W2 // 2
    G = W // GROUP_DIM
    tm = _row_tile(T, tm)
    nm = T // tm

    def body(zp_ref, do_ref, gv_ref, ws_ref, bs_ref, dzp_ref, dws_ref, dbs_ref, dgv_ref, du_scr, dvn_scr, dsum_scr):
        i = pl.program_id(0)

        @pl.when(i == 0)
        def _():
            dws_ref[...] = jnp.zeros_like(dws_ref)
            dgv_ref[...] = jnp.zeros_like(dgv_ref)
            dsum_scr[...] = jnp.zeros_like(dsum_scr)

        zp = zp_ref[...]
        z = _gelu(zp)
        u, v = z[:, :W], z[:, W:]
        n, r = _rms(v)
        gv = gv_ref[...]
        vn = n * gv
        d_out = do_ref[...].astype(F32)
        mask = _chunk_mask()
        for g in range(G):
            cs = slice(g * GROUP_DIM, (g + 1) * GROUP_DIM)
            wg = jnp.where(mask, ws_ref[g], 0.0).astype(BF16)
            dw = jnp.zeros((GMLP_BLOCK, GMLP_BLOCK), F32)
            for b in range(tm // GMLP_BLOCK):
                rs = slice(b * GMLP_BLOCK, (b + 1) * GMLP_BLOCK)
                vb = vn[rs, cs].astype(BF16)
                s = jnp.dot(wg, vb, preferred_element_type=F32) + bs_ref[:, cs]
                du_scr[rs, cs] = d_out[rs, cs] * s
                ds = d_out[rs, cs] * u[rs, cs]
                dsb = ds.astype(BF16)
                dvn_scr[rs, cs] = lax.dot_general(wg, dsb, (((0,), (0,)), ((), ())), preferred_element_type=F32)
                dw = dw + lax.dot_general(dsb, vb, (((1,), (1,)), ((), ())), preferred_element_type=F32)
                dsum_scr[:, cs] += ds
            dws_ref[g] += jnp.where(mask, dw, 0.0)
        dvn = dvn_scr[...]
        dgv_ref[...] += jnp.sum(dvn * n, axis=0, keepdims=True)
        t = dvn * gv
        dv = r * (t - n * jnp.mean(t * n, axis=-1, keepdims=True))
        dzp_ref[:, :W] = (du_scr[...] * _gelu_grad(zp[:, :W])).astype(BF16)
        dzp_ref[:, W:] = (dv * _gelu_grad(zp[:, W:])).astype(BF16)

        @pl.when(i == nm - 1)
        def _():
            sel = (lax.broadcasted_iota(jnp.int32, (G, W), 1) // GROUP_DIM
                   == lax.broadcasted_iota(jnp.int32, (G, W), 0)).astype(F32)
            dbs_ref[...] = lax.dot_general(sel, dsum_scr[...], (((1,), (1,)), ((), ())),
                                           precision=lax.Precision.HIGHEST, preferred_element_type=F32)

    return pl.pallas_call(
        body, name="gate_bwd", grid=(nm,),
        in_specs=[pl.BlockSpec((tm, W2), lambda i: (i, 0)), pl.BlockSpec((tm, W), lambda i: (i, 0)),
                  pl.BlockSpec((1, W), lambda i: (0, 0)),
                  pl.BlockSpec((G, GMLP_BLOCK, GMLP_BLOCK), lambda i: (0, 0, 0)),
                  pl.BlockSpec((GMLP_BLOCK, W), lambda i: (0, 0))],
        out_specs=[pl.BlockSpec((tm, W2), lambda i: (i, 0)),
                   pl.BlockSpec((G, GMLP_BLOCK, GMLP_BLOCK), lambda i: (0, 0, 0)),
                   pl.BlockSpec((G, GMLP_BLOCK), lambda i: (0, 0)), pl.BlockSpec((1, W), lambda i: (0, 0))],
        out_shape=[jax.ShapeDtypeStruct((T, W2), BF16), jax.ShapeDtypeStruct((G, GMLP_BLOCK, GMLP_BLOCK), F32),
                   jax.ShapeDtypeStruct((G, GMLP_BLOCK), F32), jax.ShapeDtypeStruct((1, W), F32)],
        scratch_shapes=[pltpu.VMEM((tm, W), F32), pltpu.VMEM((tm, W), F32), pltpu.VMEM((GMLP_BLOCK, W), F32)],
        compiler_params=_params(("arbitrary",)))(zp, d_out, gv, ws, bs_tile)


HALO = 8


def _shift_down(a, prev, k):
    rows = lax.broadcasted_iota(jnp.int32, a.shape, 0)
    out = pltpu.roll(a, k, 0)
    for j in range(k):
        out = jnp.where(rows == j, prev[HALO - k + j:HALO - k + j + 1, :], out)
    return out


def _conv_fwd(a, cw, cb, S, *, tm=256):
    _, T, F = a.shape
    tc = _col_tile(F)
    tm = _row_tile(S, tm)
    hb = tm // HALO

    def body(a_ref, p_ref, w_ref, b_ref, o_ref):
        first = (pl.program_id(1) * tm) % S == 0
        keep = jnp.where(first, 0.0, 1.0)

        def conv(s):
            a, p, w = a_ref[s], p_ref[s] * keep, w_ref[s]
            return w[2:3] * a + w[1:2] * _shift_down(a, p, 1) + w[0:1] * _shift_down(a, p, 2) + b_ref[s:s + 1, :]

        up, gate = conv(0), conv(1)
        o_ref[...] = (gate * jax.nn.sigmoid(gate) * up).astype(BF16)

    return pl.pallas_call(
        body, name="conv_fwd", grid=(F // tc, T // tm),
        in_specs=[pl.BlockSpec((2, tm, tc), lambda j, i: (0, i, j)),
                  pl.BlockSpec((2, HALO, tc), lambda j, i: (0, jnp.maximum(i * hb - 1, 0), j)),
                  pl.BlockSpec((2, 3, tc), lambda j, i: (0, 0, j)), pl.BlockSpec((2, tc), lambda j, i: (0, j))],
        out_specs=pl.BlockSpec((tm, tc), lambda j, i: (i, j)), out_shape=jax.ShapeDtypeStruct((T, F), BF16),
        compiler_params=_params(("arbitrary", "arbitrary")))(a, a, cw, cb)


def _conv_bwd(a, dy, cw, cb, S, *, tm=256):
    _, T, F = a.shape
    tc = _col_tile(F)
    tm = _row_tile(S, tm)
    nm = T // tm
    hb = tm // HALO
    TE = tm + HALO
    nxt = lambda j, i: jnp.minimum((i + 1) * hb, T // HALO - 1)

    def body(a_ref, p_ref, n_ref, dy_ref, ndy_ref, w_ref, b_ref, da_ref, dw_ref, db_ref):
        i = pl.program_id(1)
        first = (i * tm) % S == 0
        last = ((i + 1) * tm) % S == 0
        keep_p = jnp.where(first, 0.0, 1.0)
        keep_n = jnp.where(last, 0.0, 1.0)
        rows = lax.broadcasted_iota(jnp.int32, (TE, tc), 0)
        dyf = jnp.concatenate([dy_ref[...].astype(F32), ndy_ref[...].astype(F32) * keep_n], axis=0)

        def pre(s):
            a = jnp.concatenate([a_ref[s], n_ref[s]], axis=0)
            p, w = p_ref[s] * keep_p, w_ref[s]
            a1, a2 = _shift_down(a, p, 1), _shift_down(a, p, 2)
            return a, a1, a2, w[2:3] * a + w[1:2] * a1 + w[0:1] * a2 + b_ref[s:s + 1, :]

        au, au1, au2, up = pre(0)
        ag, ag1, ag2, gate = pre(1)
        sg = jax.nn.sigmoid(gate)
        d_up = dyf * (gate * sg)
        d_gate = dyf * up * (sg * (1.0 + gate * (1.0 - sg)))

        @pl.when(i == 0)
        def _():
            dw_ref[...] = jnp.zeros_like(dw_ref)
            db_ref[...] = jnp.zeros_like(db_ref)

        def back(s, d, a, a1, a2):
            own = jnp.where(rows < tm, d, 0.0)
            w = w_ref[s]
            db_ref[s:s + 1, :] += jnp.sum(own, axis=0, keepdims=True)
            dw_ref[s, 2:3, :] += jnp.sum(own * a, axis=0, keepdims=True)
            dw_ref[s, 1:2, :] += jnp.sum(own * a1, axis=0, keepdims=True)
            dw_ref[s, 0:1, :] += jnp.sum(own * a2, axis=0, keepdims=True)
            da = w[2:3] * d + w[1:2] * pltpu.roll(d, TE - 1, 0) + w[0:1] * pltpu.roll(d, TE - 2, 0)
            da_ref[s] = da[:tm].astype(BF16)

        back(0, d_up, au, au1, au2)
        back(1, d_gate, ag, ag1, ag2)

    return pl.pallas_call(
        body, name="conv_bwd", grid=(F // tc, nm),
        in_specs=[pl.BlockSpec((2, tm, tc), lambda j, i: (0, i, j)),
                  pl.BlockSpec((2, HALO, tc), lambda j, i: (0, jnp.maximum(i * hb - 1, 0), j)),
                  pl.BlockSpec((2, HALO, tc), lambda j, i: (0, nxt(j, i), j)),
                  pl.BlockSpec((tm, tc), lambda j, i: (i, j)), pl.BlockSpec((HALO, tc), lambda j, i: (nxt(j, i), j)),
                  pl.BlockSpec((2, 3, tc), lambda j, i: (0, 0, j)), pl.BlockSpec((2, tc), lambda j, i: (0, j))],
        out_specs=[pl.BlockSpec((2, tm, tc), lambda j, i: (0, i, j)), pl.BlockSpec((2, 3, tc), lambda j, i: (0, 0, j)),
                   pl.BlockSpec((2, tc), lambda j, i: (0, j))],
        out_shape=[jax.ShapeDtypeStruct((2, T, F), BF16), jax.ShapeDtypeStruct((2, 3, F), F32),
                   jax.ShapeDtypeStruct((2, F), F32)],
        compiler_params=_params(("arbitrary", "arbitrary")))(a, a, a, dy, dy, cw, cb)


def _bias_index():
    idx = np.arange(F_LEN)
    d = np.where(idx < K_SPAN, idx, idx - F_LEN)
    return np.clip(PAD - d, -REL_CLIP, REL_CLIP) + REL_CLIP


def _roll_rows(x, sign):
    rows = lax.broadcasted_iota(jnp.int32, x.shape, 0)
    step = 1
    while step < Q_BLOCK:
        shift = step if sign > 0 else F_LEN - step
        x = jnp.where((rows & step) != 0, pltpu.roll(x, shift, 1), x)
        step *= 2
    return x


def _bias_expand(frow):
    H = frow.shape[0]

    def body(f_ref, o_ref):
        x = _roll_rows(jnp.broadcast_to(f_ref[...], (Q_BLOCK, F_LEN)), 1)[:, :K_SPAN]
        qc = lax.broadcasted_iota(jnp.int32, (Q_BLOCK, K_SPAN), 0) // CHUNK * CHUNK
        kj = lax.broadcasted_iota(jnp.int32, (Q_BLOCK, K_SPAN), 1)
        o_ref[...] = jnp.where((kj >= qc) & (kj < qc + PAD + CHUNK), x, NEG_INF)

    return pl.pallas_call(
        body, name="bias_expand", grid=(H,),
        in_specs=[pl.BlockSpec((None, 1, F_LEN), lambda h: (h, 0, 0))],
        out_specs=pl.BlockSpec((None, Q_BLOCK, K_SPAN), lambda h: (h, 0, 0)),
        out_shape=jax.ShapeDtypeStruct((H, Q_BLOCK, K_SPAN), F32), compiler_params=_params(("arbitrary",)))(frow)


def _bias_reduce(dbias, n_rel):
    H = dbias.shape[0]
    onehot = jnp.asarray((_bias_index()[:, None] == np.arange(n_rel)[None, :]).astype(np.float32))

    def body(d_ref, oh_ref, o_ref):
        x = jnp.concatenate([d_ref[...], jnp.zeros((Q_BLOCK, F_LEN - K_SPAN), F32)], axis=1)
        row = jnp.sum(_roll_rows(x, -1), axis=0, keepdims=True)
        row8 = jnp.broadcast_to(row, (8, F_LEN))
        o_ref[...] = jnp.dot(row8, oh_ref[...], precision=lax.Precision.HIGHEST, preferred_element_type=F32)[0:1]

    return pl.pallas_call(
        body, name="bias_reduce", grid=(H,),
        in_specs=[pl.BlockSpec((None, Q_BLOCK, K_SPAN), lambda h: (h, 0, 0)),
                  pl.BlockSpec((F_LEN, n_rel), lambda h: (0, 0))],
        out_specs=pl.BlockSpec((None, 1, n_rel), lambda h: (h, 0, 0)),
        out_shape=jax.ShapeDtypeStruct((H, 1, n_rel), F32), compiler_params=_params(("arbitrary",)))(dbias, onehot)


def _attn_specs(S):
    hw = HEADS_PER_STEP * HEAD_DIM
    qspec = pl.BlockSpec((None, Q_BLOCK, hw), lambda g, b, i: (b, i, g))
    kspec = pl.BlockSpec((None, None, S, hw), lambda g, b, i: (0, b, 0, g))
    vspec = pl.BlockSpec((None, None, S, hw), lambda g, b, i: (1, b, 0, g))
    bspec = pl.BlockSpec((HEADS_PER_STEP, Q_BLOCK, K_SPAN), lambda g, b, i: (g, 0, 0))
    return hw, qspec, kspec, vspec, bspec


def _load_padded(k_ref, v_ref, kp, vp):
    kp[:PAD, :] = jnp.zeros((PAD, kp.shape[1]), BF16)
    vp[:PAD, :] = jnp.zeros((PAD, vp.shape[1]), BF16)
    kp[PAD:, :] = k_ref[...]
    vp[PAD:, :] = v_ref[...]


def _attn_exp(q_ref, kp, b_ref, h, q0, before):
    hs = slice(h * HEAD_DIM, (h + 1) * HEAD_DIM)
    kh = kp[pl.ds(q0, K_SPAN), hs]
    s = lax.dot_general(q_ref[:, hs], kh, (((1,), (1,)), ((), ())), preferred_element_type=F32) + b_ref[h] + before
    p = jnp.exp(s - jnp.max(s, axis=-1, keepdims=True))
    return p, 1.0 / jnp.sum(p, axis=-1, keepdims=True), kh


def _before_start(q0):
    kj = lax.broadcasted_iota(jnp.int32, (1, K_SPAN), 1)
    return jnp.where(q0 + kj >= PAD, 0.0, NEG_INF)


def _attn_fwd(q, kv, bias, B, S):
    HD = q.shape[-1]
    hw, qspec, kspec, vspec, bspec = _attn_specs(S)

    def body(q_ref, k_ref, v_ref, b_ref, o_ref, kp, vp):
        i = pl.program_id(2)

        @pl.when(i == 0)
        def _():
            _load_padded(k_ref, v_ref, kp, vp)

        q0 = pl.multiple_of(i * Q_BLOCK, Q_BLOCK)
        before = _before_start(q0)
        for h in range(HEADS_PER_STEP):
            hs = slice(h * HEAD_DIM, (h + 1) * HEAD_DIM)
            p, inv, _ = _attn_exp(q_ref, kp, b_ref, h, q0, before)
            o = jnp.dot(p.astype(BF16), vp[pl.ds(q0, K_SPAN), hs], preferred_element_type=F32) * inv
            o_ref[:, hs] = o.astype(BF16)

    return pl.pallas_call(
        body, name="attn_fwd", grid=(HD // hw, B, S // Q_BLOCK), in_specs=[qspec, kspec, vspec, bspec],
        out_specs=qspec, out_shape=jax.ShapeDtypeStruct((B, S, HD), BF16),
        scratch_shapes=[pltpu.VMEM((S + PAD, hw), BF16), pltpu.VMEM((S + PAD, hw), BF16)],
        compiler_params=_params(("arbitrary", "arbitrary", "arbitrary")))(q, kv, kv, bias)


def _attn_bwd(q, kv, bias, do, B, S):
    HD = q.shape[-1]
    H = HD // HEAD_DIM
    hw, qspec, kspec, vspec, bspec = _attn_specs(S)
    scale = HEAD_DIM ** -0.5
    nq = S // Q_BLOCK

    def body(q_ref, k_ref, v_ref, b_ref, do_ref, dq_ref, dkv_ref, db_ref, kp, vp, dk_acc, dv_acc):
        b, i = pl.program_id(1), pl.program_id(2)
        q0 = pl.multiple_of(i * Q_BLOCK, Q_BLOCK)

        @pl.when(i == 0)
        def _():
            _load_padded(k_ref, v_ref, kp, vp)
            dk_acc[...] = jnp.zeros_like(dk_acc)
            dv_acc[...] = jnp.zeros_like(dv_acc)

        @pl.when((i == 0) & (b == 0))
        def _():
            db_ref[...] = jnp.zeros_like(db_ref)

        before = _before_start(q0)
        for h in range(HEADS_PER_STEP):
            hs = slice(h * HEAD_DIM, (h + 1) * HEAD_DIM)
            p, inv, kh = _attn_exp(q_ref, kp, b_ref, h, q0, before)
            p = p * inv
            doh = do_ref[:, hs]
            dp = lax.dot_general(doh, vp[pl.ds(q0, K_SPAN), hs], (((1,), (1,)), ((), ())),
                                 preferred_element_type=F32)
            ds = p * (dp - jnp.sum(p * dp, axis=-1, keepdims=True))
            db_ref[h] += ds
            dsb = ds.astype(BF16)
            dq_ref[:, hs] = (jnp.dot(dsb, kh, preferred_element_type=F32) * scale).astype(BF16)
            dk_acc[pl.ds(q0, K_SPAN), hs] += lax.dot_general(dsb, q_ref[:, hs], (((0,), (0,)), ((), ())),
                                                              preferred_element_type=F32)
            dv_acc[pl.ds(q0, K_SPAN), hs] += lax.dot_general(p.astype(BF16), doh, (((0,), (0,)), ((), ())),
                                                              preferred_element_type=F32)

        @pl.when(i == nq - 1)
        def _():
            dkv_ref[0] = dk_acc[PAD:, :].astype(BF16)
            dkv_ref[1] = dv_acc[PAD:, :].astype(BF16)

    return pl.pallas_call(
        body, name="attn_bwd", grid=(HD // hw, B, nq), in_specs=[qspec, kspec, vspec, bspec, qspec],
        out_specs=[qspec, pl.BlockSpec((2, None, S, hw), lambda g, b, i: (0, b, 0, g)), bspec],
        out_shape=[jax.ShapeDtypeStruct((B, S, HD), BF16), jax.ShapeDtypeStruct((2, B, S, HD), BF16),
                   jax.ShapeDtypeStruct((H, Q_BLOCK, K_SPAN), F32)],
        scratch_shapes=[pltpu.VMEM((S + PAD, hw), BF16), pltpu.VMEM((S + PAD, hw), BF16),
                        pltpu.VMEM((S + PAD, hw), F32), pltpu.VMEM((S + PAD, hw), F32)],
        compiler_params=_params(("arbitrary", "arbitrary", "arbitrary")))(q, kv, kv, bias, do)


def _loss_head(h, g, target, *, tm=512):
    T, D = h.shape
    tm = _row_tile(T, tm)

    def body(h_ref, g_ref, t_ref, dh_ref, loss_ref, dg_ref):
        @pl.when(pl.program_id(0) == 0)
        def _():
            loss_ref[...] = jnp.zeros_like(loss_ref)
            dg_ref[...] = jnp.zeros_like(dg_ref)

        n, r = _rms(h_ref[...])
        g = g_ref[...]
        e = n * g - t_ref[...]
        loss_ref[...] += 0.5 * jnp.sum(jnp.mean(e * e, axis=-1, keepdims=True), axis=0, keepdims=True)
        dy = e * (1.0 / D)
        dg_ref[...] += jnp.sum(dy * n, axis=0, keepdims=True)
        t = dy * g
        dh_ref[...] = r * (t - n * jnp.mean(t * n, axis=-1, keepdims=True))

    row = pl.BlockSpec((tm, D), lambda i: (i, 0))
    return pl.pallas_call(
        body, name="loss_head", grid=(T // tm,), in_specs=[row, pl.BlockSpec((1, D), lambda i: (0, 0)), row],
        out_specs=[row, pl.BlockSpec((8, 128), lambda i: (0, 0)), pl.BlockSpec((1, D), lambda i: (0, 0))],
        out_shape=[jax.ShapeDtypeStruct((T, D), F32), jax.ShapeDtypeStruct((8, 128), F32),
                   jax.ShapeDtypeStruct((1, D), F32)],
        compiler_params=_params(("arbitrary",)))(h, g.reshape(1, D), target)


def _sub_rows(R):
    for cand in (256, 352, 128, 64, 8):
        if R % cand == 0 and R > cand:
            return cand
    return R


def _adamw(w, g, m, v, *, name):
    R, C = w.shape
    tr = _sub_rows(R)

    def body(w_ref, g_ref, m_ref, v_ref, d_ref, nm_ref, nv_ref):
        g = g_ref[...]
        m = ADAM_B1 * m_ref[...] + (1.0 - ADAM_B1) * g
        v = ADAM_B2 * v_ref[...] + (1.0 - ADAM_B2) * (g * g)
        m_hat = m / (1.0 - ADAM_B1 ** ADAM_STEP)
        v_hat = v / (1.0 - ADAM_B2 ** ADAM_STEP)
        d_ref[...] = -ADAM_LR * (m_hat / (jnp.sqrt(v_hat) + ADAM_EPS) + ADAM_WD * w_ref[...])
        nm_ref[...] = m
        nv_ref[...] = v

    spec = pl.BlockSpec((tr, C), lambda i: (i, 0))
    return pl.pallas_call(body, name=name, grid=(R // tr,), in_specs=[spec] * 4, out_specs=[spec] * 3,
                          out_shape=[jax.ShapeDtypeStruct((R, C), F32)] * 3,
                          compiler_params=_params(("arbitrary",)))(w, g, m, v)


def _add_pair(units, got, core, *, name):
    n4, R, C = got.shape
    rows = n4 * R
    tr = 512 if rows % 512 == 0 else R

    def body(c_ref, u_ref, got_ref, o_ref):
        o_ref[...] = (u_ref[...].astype(F32) + got_ref[...].astype(F32)).astype(BF16)

    spec = pl.BlockSpec((tr, C), lambda i, c: (i, 0))
    grid_spec = pltpu.PrefetchScalarGridSpec(
        num_scalar_prefetch=1, grid=(rows // tr,),
        in_specs=[pl.BlockSpec((None, tr, C), lambda i, c: (c[0], i, 0)), spec], out_specs=spec)
    out = pl.pallas_call(body, name=name, grid_spec=grid_spec, out_shape=jax.ShapeDtypeStruct((rows, C), BF16),
                         compiler_params=_params(("arbitrary",)))(core.reshape(1), units.reshape(2, rows, C),
                                                                   got.reshape(rows, C))
    return out.reshape(n4, R, C)


def _sum_chips(w, own, got, pos, *, name, layer=0, into=None):
    _, R, C = own.shape
    tr = _sub_rows(R)
    nr = R // tr

    def body(p_ref, own_ref, got_ref, *rest):
        o_ref = rest[-1]
        o_ref[...] = (own_ref[...].astype(F32) + got_ref[0].astype(F32) + got_ref[1].astype(F32)
                      + got_ref[2].astype(F32))

    if w.row_sharded:
        out_map = lambda i, p: (layer, i, p[1])
    else:
        out_map = lambda i, p: (layer, p[1] * nr + i, 0)
    ins = [pos, own, got]
    in_specs = [pl.BlockSpec((None, tr, C), lambda i, p: (p[0], i, 0)),
                pl.BlockSpec((3, tr, C), lambda i, p: (0, i, 0))]
    alias = {}
    if into is not None:
        ins.append(into)
        in_specs.append(ANY)
        alias = {3: 0}
    grid_spec = pltpu.PrefetchScalarGridSpec(num_scalar_prefetch=1, grid=(nr,), in_specs=in_specs,
                                             out_specs=pl.BlockSpec((None, tr, C), out_map))
    return pl.pallas_call(body, name=name, grid_spec=grid_spec, input_output_aliases=alias,
                          out_shape=jax.ShapeDtypeStruct((w.L, w.ks, w.ns), F32),
                          compiler_params=_params(("arbitrary",)))(*ins)


def _mesh_pos():
    return lax.axis_index("x"), lax.axis_index("y"), lax.axis_index("c")


def _other_chips(x, y):
    return [(1 - x, y), (x, 1 - y), (1 - x, 1 - y)]


ANY = pl.BlockSpec(memory_space=pl.ANY)


class _W:
    def __init__(self, name, shard, row_sharded):
        self.name = name
        self.L, ks, ns = shard.shape
        self.row_sharded = row_sharded
        self.K, self.N = (ks * N_CHIPS, ns) if row_sharded else (ks, ns * N_CHIPS)
        self.ks, self.ns = ks, ns

    def shard_of(self, full, j):
        if self.row_sharded:
            return full.at[:, pl.ds(j * self.ks, self.ks), :]
        return full.at[:, :, pl.ds(j * self.ns, self.ns)]

    def half_of(self, shard, c):
        if self.row_sharded:
            return shard.at[:, :, pl.ds(c * (self.ns // 2), self.ns // 2)]
        return shard.at[:, pl.ds(c * (self.ks // 2), self.ks // 2), :]


HBM = pl.BlockSpec(memory_space=pltpu.HBM)
SEM = pl.BlockSpec(memory_space=pltpu.SEMAPHORE)
IN_FLIGHT = pltpu.SideEffectType.DATAFLOW_SIDE_EFFECTING


def _in_hbm(a):
    return pltpu.with_memory_space_constraint(a, pltpu.HBM)


def _gather_start(ws, shards):
    nw = len(ws)

    def body(*refs):
        src, dst = refs[:nw], refs[nw:2 * nw]
        send, recv = refs[2 * nw:3 * nw], refs[3 * nw:4 * nw]
        x, y, c = _mesh_pos()
        me = 2 * x + y
        for i, w in enumerate(ws):
            for f, (px, py) in enumerate(_other_chips(x, y)):
                pltpu.make_async_remote_copy(src_ref=w.half_of(src[i], c), dst_ref=w.half_of(w.shard_of(dst[i], me), c),
                                             send_sem=send[i].at[f], recv_sem=recv[i].at[f], device_id=(px, py, c),
                                             device_id_type=MESH).start()

    fulls = [lax.empty((w.L, w.K, w.N), BF16) for w in ws]
    out = pl.pallas_call(
        body, name="gather_start", in_specs=[HBM] * (2 * nw),
        out_specs=[SEM] * (2 * nw) + [HBM] * (2 * nw),
        out_shape=[pltpu.SemaphoreType.DMA((3,))] * (2 * nw)
        + [pltpu.HBM(s.shape, BF16) for s in shards] + [pltpu.HBM(f.shape, BF16) for f in fulls],
        input_output_aliases={i: 2 * nw + i for i in range(2 * nw)},
        compiler_params=pltpu.CompilerParams(has_side_effects=IN_FLIGHT))(
            *[_in_hbm(s) for s in shards], *[_in_hbm(f) for f in fulls])
    return [(out[i], out[nw + i], out[2 * nw + i], out[3 * nw + i]) for i in range(nw)]


def _gather_wait(ws, flight, after, *, name):
    nw = len(ws)

    def body(*refs):
        src, dst = refs[:nw], refs[nw:2 * nw]
        send, recv = refs[2 * nw:3 * nw], refs[3 * nw:4 * nw]
        x, y, c = _mesh_pos()
        for i, w in enumerate(ws):
            for f, (px, py) in enumerate(_other_chips(x, y)):
                landed = w.half_of(w.shard_of(dst[i], 2 * px + py), c)
                cp = pltpu.make_async_remote_copy(src_ref=w.half_of(src[i], c), dst_ref=landed, send_sem=send[i].at[f],
                                                  recv_sem=recv[i].at[f], device_id=(px, py, c), device_id_type=MESH)
                cp.wait_send()
                cp.wait_recv()

    shards, fulls = [fl[2] for fl in flight], [fl[3] for fl in flight]
    out = pl.pallas_call(
        body, name=name, in_specs=[HBM] * (2 * nw) + [SEM] * (2 * nw) + [ANY],
        out_specs=[HBM] * (2 * nw),
        out_shape=[pltpu.HBM(s.shape, BF16) for s in shards] + [pltpu.HBM(f.shape, BF16) for f in fulls],
        input_output_aliases={i: i for i in range(2 * nw)},
        compiler_params=pltpu.CompilerParams(has_side_effects=IN_FLIGHT))(
            *shards, *fulls, *[fl[0] for fl in flight], *[fl[1] for fl in flight], after)
    return out[:nw], out[nw:]


def _gather_finish(ws, shards, fulls, *, name):
    nw = len(ws)

    def body(*refs):
        src, dst, stage = refs[:nw], refs[3 * nw:4 * nw], refs[4 * nw:5 * nw]
        send_sems, recv_sems, load_sems, store_sems = refs[5 * nw:]
        x, y, c = _mesh_pos()
        me = 2 * x + y
        sibling = (x, y, 1 - c)
        chips = _other_chips(x, y)

        def fwd(i, w, f, half):
            px, py = chips[f]
            landed = w.half_of(w.shard_of(dst[i], 2 * px + py), half)
            return pltpu.make_async_remote_copy(src_ref=landed, dst_ref=landed, send_sem=send_sems.at[3 * i + f],
                                                recv_sem=recv_sems.at[3 * i + f], device_id=sibling,
                                                device_id_type=MESH)

        loads = [pltpu.make_async_copy(src[i], stage[i], load_sems.at[i]) for i in range(nw)]
        for cp in loads:
            cp.start()
        sends = [fwd(i, w, f, c) for i, w in enumerate(ws) for f in range(3)]
        for cp in sends:
            cp.start()
        stores = [pltpu.make_async_copy(stage[i], w.shard_of(dst[i], me), store_sems.at[i])
                  for i, w in enumerate(ws)]
        for ld, st in zip(loads, stores):
            ld.wait()
            st.start()
        for i, w in enumerate(ws):
            for f in range(3):
                fwd(i, w, f, 1 - c).wait_recv()
        for cp in sends:
            cp.wait_send()
        for cp in stores:
            cp.wait()

    out = pl.pallas_call(
        body, name=name, in_specs=[ANY] * (2 * nw), out_specs=[ANY] * (2 * nw),
        out_shape=[jax.ShapeDtypeStruct(s.shape, BF16) for s in shards]
        + [jax.ShapeDtypeStruct(f.shape, BF16) for f in fulls],
        input_output_aliases={i: i for i in range(2 * nw)},
        scratch_shapes=[pltpu.VMEM((w.L, w.ks, w.ns), BF16) for w in ws]
        + [pltpu.SemaphoreType.DMA((3 * nw,)), pltpu.SemaphoreType.DMA((3 * nw,)), pltpu.SemaphoreType.DMA((nw,)),
           pltpu.SemaphoreType.DMA((nw,))],
        compiler_params=_params(has_side_effects=True))(*shards, *fulls)
    return out[nw:]


def _swap_units(units, *, name):
    nw = len(units)

    def body(*refs):
        src, got = refs[:nw], refs[nw:2 * nw]
        send_sems, recv_sems = refs[2 * nw:]
        x, y, c = _mesh_pos()
        copies = [pltpu.make_async_remote_copy(src_ref=src[i].at[1 - c], dst_ref=got[i], send_sem=send_sems.at[i],
                                               recv_sem=recv_sems.at[i], device_id=(x, y, 1 - c),
                                               device_id_type=MESH) for i in range(nw)]
        for cp in copies:
            cp.start()
        for cp in copies:
            cp.wait()

    return pl.pallas_call(
        body, name=name, in_specs=[ANY] * nw, out_specs=[ANY] * nw,
        out_shape=[jax.ShapeDtypeStruct(u.shape[1:], BF16) for u in units],
        scratch_shapes=[pltpu.SemaphoreType.DMA((nw,)), pltpu.SemaphoreType.DMA((nw,))],
        compiler_params=_params(has_side_effects=True))(*units)


def _scatter_copy(src, got, send, recv, f, chip, c):
    px, py = chip
    return pltpu.make_async_remote_copy(src_ref=src.at[2 * px + py], dst_ref=got.at[f], send_sem=send.at[f],
                                        recv_sem=recv.at[f], device_id=(px, py, c), device_id_type=MESH)


def _scatter_start(sums, *, name):
    nw = len(sums)

    def body(*refs):
        src, got = refs[:nw], refs[nw:2 * nw]
        send, recv = refs[2 * nw:3 * nw], refs[3 * nw:4 * nw]
        x, y, c = _mesh_pos()
        for i in range(nw):
            for f, chip in enumerate(_other_chips(x, y)):
                _scatter_copy(src[i], got[i], send[i], recv[i], f, chip, c).start()

    lands = [lax.empty((3,) + s.shape[1:], BF16) for s in sums]
    out = pl.pallas_call(
        body, name=name, in_specs=[HBM] * (2 * nw), out_specs=[SEM] * (2 * nw) + [HBM] * (2 * nw),
        out_shape=[pltpu.SemaphoreType.DMA((3,))] * (2 * nw)
        + [pltpu.HBM(s.shape, BF16) for s in sums] + [pltpu.HBM(l.shape, BF16) for l in lands],
        input_output_aliases={i: 2 * nw + i for i in range(2 * nw)},
        compiler_params=pltpu.CompilerParams(has_side_effects=IN_FLIGHT))(
            *[_in_hbm(s) for s in sums], *[_in_hbm(l) for l in lands])
    return [(out[i], out[nw + i], out[2 * nw + i], out[3 * nw + i]) for i in range(nw)]


def _scatter_wait(flight, after):
    nw = len(flight)

    def body(*refs):
        src, got = refs[:nw], refs[nw:2 * nw]
        send, recv = refs[2 * nw:3 * nw], refs[3 * nw:4 * nw]
        x, y, c = _mesh_pos()
        for i in range(nw):
            for f, chip in enumerate(_other_chips(x, y)):
                cp = _scatter_copy(src[i], got[i], send[i], recv[i], f, chip, c)
                cp.wait_send()
                cp.wait_recv()

    sums, lands = [fl[2] for fl in flight], [fl[3] for fl in flight]
    out = pl.pallas_call(
        body, name="scatter_wait", in_specs=[HBM] * (2 * nw) + [SEM] * (2 * nw) + [ANY], out_specs=[HBM] * (2 * nw),
        out_shape=[pltpu.HBM(s.shape, BF16) for s in sums] + [pltpu.HBM(l.shape, BF16) for l in lands],
        input_output_aliases={i: i for i in range(2 * nw)},
        compiler_params=pltpu.CompilerParams(has_side_effects=IN_FLIGHT))(
            *sums, *lands, *[fl[0] for fl in flight], *[fl[1] for fl in flight], after)
    return out[:nw], out[nw:]


def _join_halves(ws, shards):
    nw = len(ws)

    def body(*refs):
        buf = refs[nw:2 * nw]
        send_sems, recv_sems = refs[2 * nw:]
        x, y, c = _mesh_pos()
        sibling = (x, y, 1 - c)

        def copy(i, w, half):
            region = w.half_of(buf[i], half)
            return pltpu.make_async_remote_copy(src_ref=region, dst_ref=region, send_sem=send_sems.at[i],
                                                recv_sem=recv_sems.at[i], device_id=sibling, device_id_type=MESH)

        sends = [copy(i, w, c) for i, w in enumerate(ws)]
        for cp in sends:
            cp.start()
        for i, w in enumerate(ws):
            copy(i, w, 1 - c).wait_recv()
        for cp in sends:
            cp.wait_send()

    return pl.pallas_call(
        body, name="join_halves", in_specs=[ANY] * nw, out_specs=[ANY] * nw,
        out_shape=[jax.ShapeDtypeStruct((w.L, w.ks, w.ns), F32) for w in ws],
        input_output_aliases={i: i for i in range(nw)},
        scratch_shapes=[pltpu.SemaphoreType.DMA((nw,)), pltpu.SemaphoreType.DMA((nw,))],
        compiler_params=_params(has_side_effects=True))(*shards)


def _allreduce_small(vec):
    R = vec.shape[0]

    def body(x_ref, o_ref, buf, send_sems, recv_sems):
        x, y, c = _mesh_pos()
        me, sibling = (x, y, c), (x, y, 1 - c)
        chips = _other_chips(x, y)

        def slot(px, py, pc):
            return buf.at[4 * px + 2 * py + pc]

        def copy(k, block, to, src=None):
            return pltpu.make_async_remote_copy(src_ref=slot(*block) if src is None else src, dst_ref=slot(*block),
                                                send_sem=send_sems.at[k], recv_sem=recv_sems.at[k], device_id=to,
                                                device_id_type=MESH)

        first = [copy(0, me, sibling, src=x_ref)] + [copy(1 + f, me, (*chip, c), src=x_ref)
                                                     for f, chip in enumerate(chips)]
        for cp in first:
            cp.start()
        passed = [copy(4 + f, (*chip, c), sibling) for f, chip in enumerate(chips)]
        for f, chip in enumerate(chips):
            copy(1 + f, (*chip, c), me).wait_recv()
            passed[f].start()
        copy(0, sibling, me).wait_recv()
        for f, chip in enumerate(chips):
            copy(4 + f, (*chip, 1 - c), me).wait_recv()
        for cp in first + passed:
            cp.wait_send()
        slot(*me)[...] = x_ref[...]
        acc = buf[0]
        for d in range(1, 8):
            acc = acc + buf[d]
        o_ref[...] = acc

    return pl.pallas_call(
        body, name="allreduce_small", in_specs=[pl.BlockSpec(memory_space=pltpu.VMEM)],
        out_specs=pl.BlockSpec(memory_space=pltpu.VMEM), out_shape=jax.ShapeDtypeStruct((R, 128), F32),
        scratch_shapes=[pltpu.VMEM((8, R, 128), F32), pltpu.SemaphoreType.DMA((7,)), pltpu.SemaphoreType.DMA((7,))],
        compiler_params=_params())(vec)


def _pack(parts):
    flat = jnp.concatenate([p.reshape(-1).astype(F32) for p in parts])
    n = flat.shape[0]
    pad = (-n) % (64 * 128)
    return jnp.pad(flat, (0, pad)).reshape(-1, 128)


def _unpack(vec, shapes):
    flat = vec.reshape(-1)
    out, off = [], 0
    for s in shapes:
        n = int(np.prod(s))
        out.append(flat[off:off + n].reshape(s))
        off += n
    return out


def kernel(x, a_norm_g, a_w_in, a_v_norm_g, a_w_s, a_b_s, a_w_out, kv_norm_g, w_kv, b_norm_g, b_w_q, b_rel_bias, b_w_o, f_norm_g, f_w_in, f_conv_w, f_conv_b, f_w_down, final_norm_g, loss_target, m_a_norm_g, m_a_w_in, m_a_v_norm_g, m_a_w_s, m_a_b_s, m_a_w_out, m_kv_norm_g, m_w_kv, m_b_norm_g, m_b_w_q, m_b_rel_bias, m_b_w_o, m_f_norm_g, m_f_w_in, m_f_conv_w, m_f_conv_b, m_f_w_down, m_final_norm_g, v_a_norm_g, v_a_w_in, v_a_v_norm_g, v_a_w_s, v_a_b_s, v_a_w_out, v_kv_norm_g, v_w_kv, v_b_norm_g, v_b_w_q, v_b_rel_bias, v_b_w_o, v_f_norm_g, v_f_w_in, v_f_conv_w, v_f_conv_b, v_f_w_down, v_final_norm_g):
    B, S, D = x.shape
    T = B * S
    xi, yi, ci = lax.axis_index("x"), lax.axis_index("y"), lax.axis_index("c")
    j_me = (2 * xi + yi).astype(jnp.int32)
    core = ci.astype(jnp.int32)
    pos = jnp.stack([j_me, core])

    w_shards = {"a_w_in": (a_w_in, False), "a_w_out": (a_w_out, True), "w_kv": (w_kv[None], False),
                "b_w_q": (b_w_q, True), "b_w_o": (b_w_o, True), "f_w_in": (f_w_in, False), "f_w_down": (f_w_down, True)}
    names = list(w_shards)
    ws = [_W(n, w_shards[n][0], w_shards[n][1]) for n in names]
    g_shards = {"a_w_in": (a_w_in, False), "a_w_out": (a_w_out, True),
                "f_w_in0": (f_w_in[0:1], False), "f_w_down0": (f_w_down[0:1], True),
                "w_kv": (w_kv[None], False), "b_w_q": (b_w_q, True), "b_w_o": (b_w_o, True),
                "f_w_in1": (f_w_in[1:2], False), "f_w_down1": (f_w_down[1:2], True)}
    g_names = list(g_shards)
    g_ws = {n: _W(n, *g_shards[n]) for n in g_names}
    flight = dict(zip(g_names, _gather_start([g_ws[n] for n in g_names],
                                             [g_shards[n][0].astype(BF16) for n in g_names])))
    full = {}

    def arrive(group, after, tag):
        gw = [g_ws[n] for n in group]
        sh, fu = _gather_wait(gw, [flight[n] for n in group], after, name=f"gather_wait_{tag}")
        full.update(zip(group, _gather_finish(gw, sh, fu, name=f"gather_finish_{tag}")))

    Wd = a_w_in.shape[1]
    GW = a_v_norm_g.shape[1] * N_CHIPS
    F2 = f_conv_w.shape[2] * N_CHIPS
    Fh = F2 // 2
    nsd, nsg, nsf = a_norm_g.shape[1], a_v_norm_g.shape[1], f_conv_w.shape[2]
    own = (ci == 0).astype(F32)
    place = lambda sh, width, n: lax.dynamic_update_slice_in_dim(
        jnp.zeros(sh.shape[:-1] + (width,), F32), sh * own, j_me * n, axis=sh.ndim - 1)
    gathered = _allreduce_small(_pack([place(a_norm_g, Wd, nsd), place(a_v_norm_g, GW, nsg),
                                       place(f_conv_w, F2, nsf)]))
    a_g, a_vg, conv_w = _unpack(gathered, [(1, Wd), (1, GW), (2, 3, F2)])
    conv_w2 = conv_w.reshape(2, 3, 2, Fh).transpose(0, 2, 1, 3)
    conv_b2 = f_conv_b.reshape(2, 2, Fh)

    h0 = x.reshape(T, D)
    target = loss_target.reshape(T, D)
    bs_tile = jnp.repeat(a_b_s[0].T, GROUP_DIM, axis=1)
    ws_a = a_w_s[0]
    scale = HEAD_DIM ** -0.5
    HD = b_w_q.shape[2]
    H = HD // HEAD_DIM
    n_rel = b_rel_bias.shape[-1]
    frow = b_rel_bias[0][:, _bias_index()].reshape(H, 1, F_LEN)
    bias = _bias_expand(frow)

    def ffn_fwd(h, l):
        a = _mm(h, full[f"f_w_in{l}"], layer=0, norm_g=f_norm_g[l], split_out=True, name=f"ffn{l}_in")
        yff = _conv_fwd(a, conv_w2[l], conv_b2[l], S)
        return _mm(yff, full[f"f_w_down{l}"], layer=0, res=h, name=f"ffn{l}_down"), a, yff

    arrive(["a_w_in", "a_w_out"], h0, "a")
    zp = _mm(h0, full["a_w_in"], layer=0, norm_g=a_g[0], name="a_in")
    out_a = _gate_fwd(zp, a_vg, ws_a, bs_tile)
    h1 = _mm(out_a, full["a_w_out"], layer=0, res=h0, name="a_out")
    arrive(["f_w_in0", "f_w_down0"], h1, "f0")
    h2, a0, yff0 = ffn_fwd(h1, 0)
    arrive(["w_kv", "b_w_q", "b_w_o"], h2, "b")
    arrive(["f_w_in1", "f_w_down1"], h2, "f1")
    kv = _mm(h2, full["w_kv"], layer=0, norm_g=kv_norm_g, out_dtype=BF16, split_out=True, name="kv")
    q = _mm(h2, full["b_w_q"], layer=0, norm_g=b_norm_g[0], scale=scale, out_dtype=BF16, name="q")
    kv4, q3 = kv.reshape(2, B, S, HD), q.reshape(B, S, HD)
    o = _attn_fwd(q3, kv4, bias, B, S).reshape(T, HD)
    h3 = _mm(o, full["b_w_o"], layer=0, res=h2, name="attn_out")
    h4, a1, yff1 = ffn_fwd(h3, 1)
    dh, loss8, dg_final = _loss_head(h4, final_norm_g, target)

    units = {}

    def ffn_bwd(dh, h, a, yff, l):
        dyff = _mm(dh, full[f"f_w_down{l}"], layer=0, trans_w=True, out_dtype=BF16, name=f"ffn{l}_down_dx")
        units[f"f_w_down{l}"] = _mm_tn(yff, dh, rows_are_shards=True, name=f"ffn{l}_down_dw")
        da, dcw, dcb = _conv_bwd(a, dyff, conv_w2[l], conv_b2[l], S)
        units[f"f_w_in{l}"] = _mm_tn(h, da, norm_g=f_norm_g[l], split_y=True, name=f"ffn{l}_in_dw")
        dh, dg = _mm(da, full[f"f_w_in{l}"], layer=0, trans_w=True, split_x=True, bwd=(h, f_norm_g[l], dh), tm=256,
                     name=f"ffn{l}_in_dx")
        return dh, dg, dcw, dcb

    in_flight = {}

    def reduce_start(group, tag):
        got = _swap_units([units[n] for n in group], name=f"swap_{tag}")
        sums = [_add_pair(units[n], g_, core, name=f"pair_{n}") for n, g_ in zip(group, got)]
        in_flight.update(zip(group, _scatter_start(sums, name=f"scatter_start_{tag}")))

    dh, dg_f1, dcw1, dcb1 = ffn_bwd(dh, h3, a1, yff1, 1)
    reduce_start(["f_w_down1", "f_w_in1"], "f1")
    do = _mm(dh, full["b_w_o"], layer=0, trans_w=True, out_dtype=BF16, name="attn_out_dx")
    units["b_w_o"] = _mm_tn(o, dh, rows_are_shards=True, name="b_w_o_dw")
    dq, dkv, dbias = _attn_bwd(q3, kv4, bias, do.reshape(B, S, HD), B, S)
    d_rel = _bias_reduce(dbias, n_rel).reshape(1, H, n_rel)
    dq, dkv = dq.reshape(T, HD), dkv.reshape(2, T, HD)
    units["b_w_q"] = _mm_tn(h2, dq, norm_g=b_norm_g[0], rows_are_shards=True, name="b_w_q_dw")
    dh, dg_b = _mm(dq, full["b_w_q"], layer=0, trans_w=True, bwd=(h2, b_norm_g[0], dh), name="q_dx")
    units["w_kv"] = _mm_tn(h2, dkv, norm_g=kv_norm_g, split_y=True, name="w_kv_dw")
    dh, dg_kv = _mm(dkv, full["w_kv"], layer=0, trans_w=True, split_x=True, bwd=(h2, kv_norm_g, dh), name="kv_dx")
    reduce_start(["b_w_o", "b_w_q", "w_kv"], "b")
    dh, dg_f0, dcw0, dcb0 = ffn_bwd(dh, h1, a0, yff0, 0)
    reduce_start(["f_w_down0", "f_w_in0"], "f0")
    d_out = _mm(dh, full["a_w_out"], layer=0, trans_w=True, out_dtype=BF16, name="a_out_dx")
    units["a_w_out"] = _mm_tn(out_a, dh, rows_are_shards=True, name="a_w_out_dw")
    dzp, dws, dbs, dgv = _gate_bwd(zp, d_out, a_vg, ws_a, bs_tile)
    units["a_w_in"] = _mm_tn(h0, dzp, norm_g=a_g[0], name="a_w_in_dw")
    reduce_start(["a_w_out", "a_w_in"], "a")
    grad_x, dg_a = _mm(dzp, full["a_w_in"], layer=0, trans_w=True, bwd=(h0, a_g[0], dh), name="a_in_dx")

    sums, recv = _scatter_wait([in_flight[n] for n in g_names], grad_x)
    sums, recv = dict(zip(g_names, sums)), dict(zip(g_names, recv))
    halves = []
    for n, w in zip(names, ws):
        if w.L == 1:
            halves.append(_sum_chips(w, sums[n], recv[n], pos, name=f"chips_{n}"))
        else:
            first = _sum_chips(w, sums[n + "0"], recv[n + "0"], pos, name=f"chips_{n}0")
            halves.append(_sum_chips(w, sums[n + "1"], recv[n + "1"], pos, layer=1, into=first, name=f"chips_{n}1"))
    g_big = dict(zip(names, _join_halves(ws, halves)))
    g_big["w_kv"] = g_big["w_kv"][0]

    to_flat = lambda d: d.transpose(1, 0, 2).reshape(3, F2)
    small = {"a_norm_g": dg_a, "a_v_norm_g": dgv, "a_w_s": dws[None], "a_b_s": dbs[None], "kv_norm_g": dg_kv[0],
             "b_norm_g": dg_b, "b_rel_bias": d_rel, "f_norm_g": jnp.concatenate([dg_f0, dg_f1], axis=0),
             "f_conv_w": jnp.stack([to_flat(dcw0), to_flat(dcw1)]),
             "f_conv_b": jnp.stack([dcb0.reshape(F2), dcb1.reshape(F2)]), "final_norm_g": dg_final[0]}
    snames = list(small)
    red = _allreduce_small(_pack([small[n] for n in snames] + [loss8[0:1, 0:1]]))
    parts = _unpack(red, [small[n].shape for n in snames] + [(1,)])
    g_small = dict(zip(snames, parts[:-1]))
    loss = parts[-1][0]
    g_small["a_norm_g"] = lax.dynamic_slice_in_dim(g_small["a_norm_g"], j_me * nsd, nsd, axis=1)
    g_small["a_v_norm_g"] = lax.dynamic_slice_in_dim(g_small["a_v_norm_g"], j_me * nsg, nsg, axis=1)
    g_small["f_conv_w"] = lax.dynamic_slice_in_dim(g_small["f_conv_w"], j_me * nsf, nsf, axis=2)

    given = dict(a_norm_g=(a_norm_g, m_a_norm_g, v_a_norm_g), a_w_in=(a_w_in, m_a_w_in, v_a_w_in),
                 a_v_norm_g=(a_v_norm_g, m_a_v_norm_g, v_a_v_norm_g), a_w_s=(a_w_s, m_a_w_s, v_a_w_s),
                 a_b_s=(a_b_s, m_a_b_s, v_a_b_s), a_w_out=(a_w_out, m_a_w_out, v_a_w_out),
                 kv_norm_g=(kv_norm_g, m_kv_norm_g, v_kv_norm_g), w_kv=(w_kv, m_w_kv, v_w_kv),
                 b_norm_g=(b_norm_g, m_b_norm_g, v_b_norm_g), b_w_q=(b_w_q, m_b_w_q, v_b_w_q),
                 b_rel_bias=(b_rel_bias, m_b_rel_bias, v_b_rel_bias), b_w_o=(b_w_o, m_b_w_o, v_b_w_o),
                 f_norm_g=(f_norm_g, m_f_norm_g, v_f_norm_g), f_w_in=(f_w_in, m_f_w_in, v_f_w_in),
                 f_conv_w=(f_conv_w, m_f_conv_w, v_f_conv_w), f_conv_b=(f_conv_b, m_f_conv_b, v_f_conv_b),
                 f_w_down=(f_w_down, m_f_w_down, v_f_w_down), final_norm_g=(final_norm_g, m_final_norm_g, v_final_norm_g))
    order = list(given)
    grads, deltas, new_m, new_v = {}, {}, {}, {}
    for n in names:
        w_, m_, v_ = given[n]
        g_ = g_big[n]
        C = w_.shape[-1]
        d2, m2, v2 = _adamw(w_.reshape(-1, C), g_.reshape(-1, C), m_.reshape(-1, C), v_.reshape(-1, C),
                            name=f"adamw_{n}")
        grads[n], deltas[n], new_m[n], new_v[n] = g_.reshape(w_.shape), d2.reshape(w_.shape), m2.reshape(w_.shape), \
            v2.reshape(w_.shape)
    sm = [n for n in order if n not in names]
    d2, m2, v2 = _adamw(_pack([given[n][0] for n in sm]), _pack([g_small[n].reshape(given[n][0].shape) for n in sm]),
                        _pack([given[n][1] for n in sm]), _pack([given[n][2] for n in sm]), name="adamw_small")
    shapes = [given[n][0].shape for n in sm]
    for n, d_, m_, v_ in zip(sm, _unpack(d2, shapes), _unpack(m2, shapes), _unpack(v2, shapes)):
        grads[n], deltas[n], new_m[n], new_v[n] = g_small[n].reshape(given[n][0].shape), d_, m_, v_

    return (loss, grad_x.reshape(B, S, D), *[grads[n] for n in order], *[deltas[n] for n in order],
            *[new_m[n] for n in order], *[new_v[n] for n in order])
```

```python
import math

import numpy as np
import jax
import jax.numpy as jnp
from jax import lax
from jax.experimental import pallas as pl
from jax.experimental.pallas import tpu as pltpu

F32 = jnp.float32
BF16 = jnp.bfloat16
MESH = pl.DeviceIdType.MESH

EPS = 1e-6
NEG_INF = -1e30
CHUNK = 64
GMLP_BLOCK = 128
GROUP_DIM = 128
HEAD_DIM = 64
LEFT_CHUNKS = 8
PAD = LEFT_CHUNKS * CHUNK
REL_CLIP = 128
Q_BLOCK = 256
K_SPAN = PAD + Q_BLOCK
F_LEN = K_SPAN + Q_BLOCK
HEADS_PER_STEP = 4
N_CHIPS = 4

ADAM_LR = 0.001
ADAM_B1 = 0.9
ADAM_B2 = 0.999
ADAM_EPS = 1e-08
ADAM_WD = 0.01
ADAM_STEP = 10

VMEM_LIMIT = 56 * 1024 * 1024


def _params(sem=None, **kw):
    if sem is not None:
        kw["dimension_semantics"] = sem
    return pltpu.CompilerParams(vmem_limit_bytes=VMEM_LIMIT, **kw)


def _rms(xf):
    r = lax.rsqrt(jnp.mean(xf * xf, axis=-1, keepdims=True) + EPS)
    return xf * r, r


def _gelu(x):
    c = math.sqrt(2.0 / math.pi)
    return 0.5 * x * (1.0 + jnp.tanh(c * (x + 0.044715 * x * x * x)))


def _gelu_grad(x):
    c = math.sqrt(2.0 / math.pi)
    t = jnp.tanh(c * (x + 0.044715 * x * x * x))
    return 0.5 * (1.0 + t) + 0.5 * x * (1.0 - t * t) * c * (1.0 + 3.0 * 0.044715 * x * x)


def _col_tile(n):
    if n <= 1024:
        return n
    for t in (1408, 1024, 512):
        if n % t == 0:
            return t
    raise ValueError(n)


def _row_tile(t, want):
    while t % want:
        want //= 2
    return want


def _mm(x, w, *, name, layer=None, trans_w=False, norm_g=None, res=None, scale=None, out_dtype=F32, bwd=None,
        split_out=False, split_x=False, tm=512):
    T = x.shape[-2]
    K = 2 * x.shape[-1] if split_x else x.shape[-1]
    N = w.shape[-2] if trans_w else w.shape[-1]
    tn = N
    tm = _row_tile(T, 256 if N > 4096 else tm)
    nn, nm = N // tn, T // tm
    has_norm, has_res, has_bwd = norm_g is not None, res is not None, bwd is not None
    dims = (((1,), (1,)), ((), ())) if trans_w else (((1,), (0,)), ((), ()))

    def body(*refs):
        it = iter(refs)
        x_ref, w_ref = next(it), next(it)
        g_ref = next(it) if has_norm else None
        res_ref = next(it) if has_res else None
        if has_bwd:
            h_ref, bg_ref, dh_ref = next(it), next(it), next(it)
        o_ref = next(it)
        if split_x:
            kh = K // 2
            acc = lax.dot_general(x_ref[0].astype(BF16), w_ref[:, :kh] if trans_w else w_ref[:kh, :], dims,
                                  preferred_element_type=F32)
            acc = acc + lax.dot_general(x_ref[1].astype(BF16), w_ref[:, kh:] if trans_w else w_ref[kh:, :], dims,
                                        preferred_element_type=F32)
        else:
            xv = x_ref[...]
            if has_norm:
                xv = _rms(xv.astype(F32))[0] * g_ref[...]
            acc = lax.dot_general(xv.astype(BF16), w_ref[...], dims, preferred_element_type=F32)
        if scale is not None:
            acc = acc * scale
        if has_res:
            acc = acc + res_ref[...]
        if has_bwd:
            dg_ref = next(it)
            n, r = _rms(h_ref[...])

            @pl.when(pl.program_id(1) == 0)
            def _():
                dg_ref[...] = jnp.zeros_like(dg_ref)

            dg_ref[...] += jnp.sum(acc * n, axis=0, keepdims=True)
            t = acc * bg_ref[...]
            o_ref[...] = dh_ref[...] + r * (t - n * jnp.mean(t * n, axis=-1, keepdims=True))
        elif split_out:
            o_ref[0] = acc[:, :N // 2].astype(out_dtype)
            o_ref[1] = acc[:, N // 2:].astype(out_dtype)
        else:
            o_ref[...] = acc.astype(out_dtype)

    lead = () if layer is None else (None,)
    lidx = () if layer is None else (layer,)
    ins = [x, w]
    xspec = (pl.BlockSpec((2, tm, K // 2), lambda n, m: (0, m, 0)) if split_x
             else pl.BlockSpec((tm, K), lambda n, m: (m, 0)))
    wspec = (pl.BlockSpec(lead + (tn, K), lambda n, m: lidx + (n, 0)) if trans_w
             else pl.BlockSpec(lead + (K, tn), lambda n, m: lidx + (0, n)))
    in_specs = [xspec, wspec]
    if has_norm:
        ins.append(norm_g.reshape(1, K))
        in_specs.append(pl.BlockSpec((1, K), lambda n, m: (0, 0)))
    if has_res:
        ins.append(res)
        in_specs.append(pl.BlockSpec((tm, tn), lambda n, m: (m, n)))
    if split_out:
        out_shape = [jax.ShapeDtypeStruct((2, T, N // 2), out_dtype)]
        out_specs = [pl.BlockSpec((2, tm, N // 2), lambda n, m: (0, m, 0))]
    else:
        out_shape = [jax.ShapeDtypeStruct((T, N), F32 if has_bwd else out_dtype)]
        out_specs = [pl.BlockSpec((tm, tn), lambda n, m: (m, n))]
    if has_bwd:
        h, g, dh = bwd
        ins += [h, g.reshape(1, N), dh]
        in_specs += [pl.BlockSpec((tm, N), lambda n, m: (m, 0)), pl.BlockSpec((1, N), lambda n, m: (0, 0)),
                     pl.BlockSpec((tm, N), lambda n, m: (m, 0))]
        out_shape.append(jax.ShapeDtypeStruct((1, N), F32))
        out_specs.append(pl.BlockSpec((1, N), lambda n, m: (0, 0)))
    out = pl.pallas_call(body, name=name, grid=(nn, nm), in_specs=in_specs, out_specs=out_specs, out_shape=out_shape,
                         compiler_params=_params(("arbitrary", "arbitrary")))(*ins)
    return out if has_bwd else out[0]


def _mm_tn(x, dy, *, name, norm_g=None, rows_are_shards=False, split_y=False, tt=512):
    T, K = x.shape
    N = 2 * dy.shape[-1] if split_y else dy.shape[-1]
    if rows_are_shards:
        tn, nn = N // 2, 2
        R, C = K // N_CHIPS, tn
    else:
        tn, nn = N // N_CHIPS, N_CHIPS
        R, C = K // 2, tn
    tt = _row_tile(T, tt)
    nt = T // tt
    has_norm = norm_g is not None

    def body(*refs):
        it = iter(refs)
        x_ref, y_ref = next(it), next(it)
        g_ref = next(it) if has_norm else None
        o_ref, acc_ref = next(it), next(it)
        t = pl.program_id(1)

        @pl.when(t == 0)
        def _():
            acc_ref[...] = jnp.zeros_like(acc_ref)

        xv = x_ref[...]
        if has_norm:
            xv = _rms(xv.astype(F32))[0] * g_ref[...]
        acc_ref[...] += lax.dot_general(xv.astype(BF16), y_ref[...].astype(BF16), (((0,), (0,)), ((), ())),
                                        preferred_element_type=F32)

        @pl.when(t == nt - 1)
        def _():
            a = acc_ref[...].astype(BF16)
            o_ref[...] = a.reshape(N_CHIPS, R, tn) if rows_are_shards else a.reshape(2, R, tn)

    ins = [x, dy]
    if split_y:
        per = (N // 2) // tn
        yspec = pl.BlockSpec((None, tt, tn), lambda n, t: (n // per, t, n % per))
    else:
        yspec = pl.BlockSpec((tt, tn), lambda n, t: (t, n))
    in_specs = [pl.BlockSpec((tt, K), lambda n, t: (t, 0)), yspec]
    if has_norm:
        ins.append(norm_g.reshape(1, K))
        in_specs.append(pl.BlockSpec((1, K), lambda n, t: (0, 0)))
    if rows_are_shards:
        out_spec = pl.BlockSpec((None, N_CHIPS, R, C), lambda n, t: (n, 0, 0, 0))
    else:
        out_spec = pl.BlockSpec((2, None, R, C), lambda n, t: (0, n, 0, 0))
    return pl.pallas_call(body, name=name, grid=(nn, nt), in_specs=in_specs, out_specs=out_spec,
                          out_shape=jax.ShapeDtypeStruct((2, N_CHIPS, R, C), BF16),
                          scratch_shapes=[pltpu.VMEM((K, tn), F32)],
                          compiler_params=_params(("arbitrary", "arbitrary")))(*ins)


def _chunk_mask():
    i = lax.broadcasted_iota(jnp.int32, (GMLP_BLOCK, GMLP_BLOCK), 0) // CHUNK
    j = lax.broadcasted_iota(jnp.int32, (GMLP_BLOCK, GMLP_BLOCK), 1) // CHUNK
    return i >= j


def _gate_fwd(zp, gv, ws, bs_tile, *, tm=256):
    T, W2 = zp.shape
    W = W2 // 2
    G = W // GROUP_DIM
    tm = _row_tile(T, tm)

    def body(zp_ref, gv_ref, ws_ref, bs_ref, o_ref):
        z = _gelu(zp_ref[...])
        u, v = z[:, :W], z[:, W:]
        vn = _rms(v)[0] * gv_ref[...]
        mask = _chunk_mask()
        for g in range(G):
            cs = slice(g * GROUP_DIM, (g + 1) * GROUP_DIM)
            wg = jnp.where(mask, ws_ref[g], 0.0).astype(BF16)
            for b in range(tm // GMLP_BLOCK):
                rs = slice(b * GMLP_BLOCK, (b + 1) * GMLP_BLOCK)
                s = jnp.dot(wg, vn[rs, cs].astype(BF16), preferred_element_type=F32) + bs_ref[:, cs]
                o_ref[rs, cs] = (u[rs, cs] * s).astype(BF16)

    return pl.pallas_call(
        body, name="gate_fwd", grid=(T // tm,),
        in_specs=[pl.BlockSpec((tm, W2), lambda i: (i, 0)), pl.BlockSpec((1, W), lambda i: (0, 0)),
                  pl.BlockSpec((G, GMLP_BLOCK, GMLP_BLOCK), lambda i: (0, 0, 0)),
                  pl.BlockSpec((GMLP_BLOCK, W), lambda i: (0, 0))],
        out_specs=pl.BlockSpec((tm, W), lambda i: (i, 0)), out_shape=jax.ShapeDtypeStruct((T, W), BF16),
        compiler_params=_params(("arbitrary",)))(zp, gv, ws, bs_tile)


def _gate_bwd(zp, d_out, gv, ws, bs_tile, *, tm=256):
    T, W2 = zp.shape
    W = W2 // 2
    G = W // GROUP_DIM
    tm = _row_tile(T, tm)
    nm = T // tm

    def body(zp_ref, do_ref, gv_ref, ws_ref, bs_ref, dzp_ref, dws_ref, dbs_ref, dgv_ref, du_scr, dvn_scr, dsum_scr):
        i = pl.program_id(0)

        @pl.when(i == 0)
        def _():
            dws_ref[...] = jnp.zeros_like(dws_ref)
            dgv_ref[...] = jnp.zeros_like(dgv_ref)
            dsum_scr[...] = jnp.zeros_like(dsum_scr)

        zp = zp_ref[...]
        z = _gelu(zp)
        u, v = z[:, :W], z[:, W:]
        n, r = _rms(v)
        gv = gv_ref[...]
        vn = n * gv
        d_out = do_ref[...].astype(F32)
        mask = _chunk_mask()
        for g in range(G):
            cs = slice(g * GROUP_DIM, (g + 1) * GROUP_DIM)
            wg = jnp.where(mask, ws_ref[g], 0.0).astype(BF16)
            dw = jnp.zeros((GMLP_BLOCK, GMLP_BLOCK), F32)
            for b in range(tm // GMLP_BLOCK):
                rs = slice(b * GMLP_BLOCK, (b + 1) * GMLP_BLOCK)
                vb = vn[rs, cs].astype(BF16)
                s = jnp.dot(wg, vb, preferred_element_type=F32) + bs_ref[:, cs]
                du_scr[rs, cs] = d_out[rs, cs] * s
                ds = d_out[rs, cs] * u[rs, cs]
                dsb = ds.astype(BF16)
                dvn_scr[rs, cs] = lax.dot_general(wg, dsb, (((0,), (0,)), ((), ())), preferred_element_type=F32)
                dw = dw + lax.dot_general(dsb, vb, (((1,), (1,)), ((), ())), preferred_element_type=F32)
                dsum_scr[:, cs] += ds
            dws_ref[g] += jnp.where(mask, dw, 0.0)
        dvn = dvn_scr[...]
        dgv_ref[...] += jnp.sum(dvn * n, axis=0, keepdims=True)
        t = dvn * gv
        dv = r * (t - n * jnp.mean(t * n, axis=-1, keepdims=True))
        dzp_ref[:, :W] = (du_scr[...] * _gelu_grad(zp[:, :W])).astype(BF16)
        dzp_ref[:, W:] = (dv * _gelu_grad(zp[:, W:])).astype(BF16)

        @pl.when(i == nm - 1)
        def _():
            sel = (lax.broadcasted_iota(jnp.int32, (G, W), 1) // GROUP_DIM
                   == lax.broadcasted_iota(jnp.int32, (G, W), 0)).astype(F32)
            dbs_ref[...] = lax.dot_general(sel, dsum_scr[...], (((1,), (1,)), ((), ())),
                                           precision=lax.Precision.HIGHEST, preferred_element_type=F32)

    return pl.pallas_call(
        body, name="gate_bwd", grid=(nm,),
        in_specs=[pl.BlockSpec((tm, W2), lambda i: (i, 0)), pl.BlockSpec((tm, W), lambda i: (i, 0)),
                  pl.BlockSpec((1, W), lambda i: (0, 0)),
                  pl.BlockSpec((G, GMLP_BLOCK, GMLP_BLOCK), lambda i: (0, 0, 0)),
                  pl.BlockSpec((GMLP_BLOCK, W), lambda i: (0, 0))],
        out_specs=[pl.BlockSpec((tm, W2), lambda i: (i, 0)),
                   pl.BlockSpec((G, GMLP_BLOCK, GMLP_BLOCK), lambda i: (0, 0, 0)),
                   pl.BlockSpec((G, GMLP_BLOCK), lambda i: (0, 0)), pl.BlockSpec((1, W), lambda i: (0, 0))],
        out_shape=[jax.ShapeDtypeStruct((T, W2), BF16), jax.ShapeDtypeStruct((G, GMLP_BLOCK, GMLP_BLOCK), F32),
                   jax.ShapeDtypeStruct((G, GMLP_BLOCK), F32), jax.ShapeDtypeStruct((1, W), F32)],
        scratch_shapes=[pltpu.VMEM((tm, W), F32), pltpu.VMEM((tm, W), F32), pltpu.VMEM((GMLP_BLOCK, W), F32)],
        compiler_params=_params(("arbitrary",)))(zp, d_out, gv, ws, bs_tile)


HALO = 16


def _taps(ext, w, b):
    a, a1, a2 = ext[HALO:], pltpu.roll(ext, 1, 0)[HALO:], pltpu.roll(ext, 2, 0)[HALO:]
    return w[2:3] * a + w[1:2] * a1 + w[0:1] * a2 + b, a, a1, a2


def _conv_fwd(a, cw, cb, S, *, tm=256):
    _, T, F = a.shape
    tc = _col_tile(F)
    tm = _row_tile(S, tm)
    hb = tm // HALO

    def body(a_ref, p_ref, w_ref, b_ref, o_ref):
        first = (pl.program_id(1) * tm) % S == 0
        keep = jnp.where(first, 0.0, 1.0)

        def conv(s):
            ext = jnp.concatenate([p_ref[s].astype(F32) * keep, a_ref[s].astype(F32)], axis=0)
            return _taps(ext, w_ref[s], b_ref[s:s + 1, :])[0]

        up, gate = conv(0), conv(1)
        o_ref[...] = (gate * jax.nn.sigmoid(gate) * up).astype(BF16)

    return pl.pallas_call(
        body, name="conv_fwd", grid=(F // tc, T // tm),
        in_specs=[pl.BlockSpec((2, tm, tc), lambda j, i: (0, i, j)),
                  pl.BlockSpec((2, HALO, tc), lambda j, i: (0, jnp.maximum(i * hb - 1, 0), j)),
                  pl.BlockSpec((2, 3, tc), lambda j, i: (0, 0, j)), pl.BlockSpec((2, tc), lambda j, i: (0, j))],
        out_specs=pl.BlockSpec((tm, tc), lambda j, i: (i, j)), out_shape=jax.ShapeDtypeStruct((T, F), BF16),
        compiler_params=_params(("arbitrary", "arbitrary")))(a, a, cw, cb)


def _conv_bwd(a, dy, cw, cb, S, *, tm=256):
    _, T, F = a.shape
    tc = _col_tile(F)
    tm = _row_tile(S, tm)
    nm = T // tm
    hb = tm // HALO
    TE = tm + HALO
    nxt = lambda j, i: jnp.minimum((i + 1) * hb, T // HALO - 1)

    def body(a_ref, p_ref, n_ref, dy_ref, ndy_ref, w_ref, b_ref, da_ref, dw_ref, db_ref):
        i = pl.program_id(1)
        first = (i * tm) % S == 0
        last = ((i + 1) * tm) % S == 0
        keep_p = jnp.where(first, 0.0, 1.0)
        keep_n = jnp.where(last, 0.0, 1.0)
        dyf = jnp.concatenate([dy_ref[...].astype(F32), ndy_ref[...].astype(F32) * keep_n], axis=0)

        def pre(s):
            ext = jnp.concatenate([p_ref[s].astype(F32) * keep_p, a_ref[s].astype(F32), n_ref[s].astype(F32)],
                                  axis=0)
            return _taps(ext, w_ref[s], b_ref[s:s + 1, :])

        up, au, au1, au2 = pre(0)
        gate, ag, ag1, ag2 = pre(1)
        sg = jax.nn.sigmoid(gate)
        d_up = dyf * (gate * sg)
        d_gate = dyf * up * (sg * (1.0 + gate * (1.0 - sg)))

        @pl.when(i == 0)
        def _():
            dw_ref[...] = jnp.zeros_like(dw_ref)
            db_ref[...] = jnp.zeros_like(db_ref)

        def back(s, d, a, a1, a2):
            own = d[:tm]
            w = w_ref[s]
            db_ref[s:s + 1, :] += jnp.sum(own, axis=0, keepdims=True)
            dw_ref[s, 2:3, :] += jnp.sum(own * a[:tm], axis=0, keepdims=True)
            dw_ref[s, 1:2, :] += jnp.sum(own * a1[:tm], axis=0, keepdims=True)
            dw_ref[s, 0:1, :] += jnp.sum(own * a2[:tm], axis=0, keepdims=True)
            da = w[2:3] * d + w[1:2] * pltpu.roll(d, TE - 1, 0) + w[0:1] * pltpu.roll(d, TE - 2, 0)
            da_ref[s] = da[:tm].astype(BF16)

        back(0, d_up, au, au1, au2)
        back(1, d_gate, ag, ag1, ag2)

    return pl.pallas_call(
        body, name="conv_bwd", grid=(F // tc, nm),
        in_specs=[pl.BlockSpec((2, tm, tc), lambda j, i: (0, i, j)),
                  pl.BlockSpec((2, HALO, tc), lambda j, i: (0, jnp.maximum(i * hb - 1, 0), j)),
                  pl.BlockSpec((2, HALO, tc), lambda j, i: (0, nxt(j, i), j)),
                  pl.BlockSpec((tm, tc), lambda j, i: (i, j)), pl.BlockSpec((HALO, tc), lambda j, i: (nxt(j, i), j)),
                  pl.BlockSpec((2, 3, tc), lambda j, i: (0, 0, j)), pl.BlockSpec((2, tc), lambda j, i: (0, j))],
        out_specs=[pl.BlockSpec((2, tm, tc), lambda j, i: (0, i, j)), pl.BlockSpec((2, 3, tc), lambda j, i: (0, 0, j)),
                   pl.BlockSpec((2, tc), lambda j, i: (0, j))],
        out_shape=[jax.ShapeDtypeStruct((2, T, F), BF16), jax.ShapeDtypeStruct((2, 3, F), F32),
                   jax.ShapeDtypeStruct((2, F), F32)],
        compiler_params=_params(("arbitrary", "arbitrary")))(a, a, a, dy, dy, cw, cb)


def _bias_index():
    idx = np.arange(F_LEN)
    d = np.where(idx < K_SPAN, idx, idx - F_LEN)
    return np.clip(PAD - d, -REL_CLIP, REL_CLIP) + REL_CLIP


def _roll_rows(x, sign):
    rows = lax.broadcasted_iota(jnp.int32, x.shape, 0)
    step = 1
    while step < Q_BLOCK:
        shift = step if sign > 0 else F_LEN - step
        x = jnp.where((rows & step) != 0, pltpu.roll(x, shift, 1), x)
        step *= 2
    return x


def _bias_expand(frow):
    H = frow.shape[0]

    def body(f_ref, o_ref):
        x = _roll_rows(jnp.broadcast_to(f_ref[...], (Q_BLOCK, F_LEN)), 1)[:, :K_SPAN]
        qc = lax.broadcasted_iota(jnp.int32, (Q_BLOCK, K_SPAN), 0) // CHUNK * CHUNK
        kj = lax.broadcasted_iota(jnp.int32, (Q_BLOCK, K_SPAN), 1)
        o_ref[...] = jnp.where((kj >= qc) & (kj < qc + PAD + CHUNK), x, NEG_INF)

    return pl.pallas_call(
        body, name="bias_expand", grid=(H,),
        in_specs=[pl.BlockSpec((None, 1, F_LEN), lambda h: (h, 0, 0))],
        out_specs=pl.BlockSpec((None, Q_BLOCK, K_SPAN), lambda h: (h, 0, 0)),
        out_shape=jax.ShapeDtypeStruct((H, Q_BLOCK, K_SPAN), F32), compiler_params=_params(("arbitrary",)))(frow)


def _bias_reduce(dbias, n_rel):
    H = dbias.shape[0]
    onehot = jnp.asarray((_bias_index()[:, None] == np.arange(n_rel)[None, :]).astype(np.float32))

    def body(d_ref, oh_ref, o_ref):
        x = jnp.concatenate([d_ref[...], jnp.zeros((Q_BLOCK, F_LEN - K_SPAN), F32)], axis=1)
        row = jnp.sum(_roll_rows(x, -1), axis=0, keepdims=True)
        row8 = jnp.broadcast_to(row, (8, F_LEN))
        o_ref[...] = jnp.dot(row8, oh_ref[...], precision=lax.Precision.HIGHEST, preferred_element_type=F32)[0:1]

    return pl.pallas_call(
        body, name="bias_reduce", grid=(H,),
        in_specs=[pl.BlockSpec((None, Q_BLOCK, K_SPAN), lambda h: (h, 0, 0)),
                  pl.BlockSpec((F_LEN, n_rel), lambda h: (0, 0))],
        out_specs=pl.BlockSpec((None, 1, n_rel), lambda h: (h, 0, 0)),
        out_shape=jax.ShapeDtypeStruct((H, 1, n_rel), F32), compiler_params=_params(("arbitrary",)))(dbias, onehot)


def _attn_specs(S):
    hw = HEADS_PER_STEP * HEAD_DIM
    qspec = pl.BlockSpec((None, Q_BLOCK, hw), lambda g, b, i: (b, i, g))
    kspec = pl.BlockSpec((None, None, S, hw), lambda g, b, i: (0, b, 0, g))
    vspec = pl.BlockSpec((None, None, S, hw), lambda g, b, i: (1, b, 0, g))
    bspec = pl.BlockSpec((HEADS_PER_STEP, Q_BLOCK, K_SPAN), lambda g, b, i: (g, 0, 0))
    return hw, qspec, kspec, vspec, bspec


def _load_padded(k_ref, v_ref, kp, vp):
    kp[:PAD, :] = jnp.zeros((PAD, kp.shape[1]), BF16)
    vp[:PAD, :] = jnp.zeros((PAD, vp.shape[1]), BF16)
    kp[PAD:, :] = k_ref[...]
    vp[PAD:, :] = v_ref[...]


def _attn_exp(q_ref, kp, b_ref, h, q0, before):
    hs = slice(h * HEAD_DIM, (h + 1) * HEAD_DIM)
    kh = kp[pl.ds(q0, K_SPAN), hs]
    s = lax.dot_general(q_ref[:, hs], kh, (((1,), (1,)), ((), ())), preferred_element_type=F32) + b_ref[h] + before
    p = jnp.exp(s - jnp.max(s, axis=-1, keepdims=True))
    return p, 1.0 / jnp.sum(p, axis=-1, keepdims=True), kh


def _before_start(q0):
    kj = lax.broadcasted_iota(jnp.int32, (1, K_SPAN), 1)
    return jnp.where(q0 + kj >= PAD, 0.0, NEG_INF)


def _attn_fwd(q, kv, bias, B, S):
    HD = q.shape[-1]
    hw, qspec, kspec, vspec, bspec = _attn_specs(S)

    def body(q_ref, k_ref, v_ref, b_ref, o_ref, kp, vp):
        i = pl.program_id(2)

        @pl.when(i == 0)
        def _():
            _load_padded(k_ref, v_ref, kp, vp)

        q0 = pl.multiple_of(i * Q_BLOCK, Q_BLOCK)
        before = _before_start(q0)
        outs = []
        for h in range(HEADS_PER_STEP):
            hs = slice(h * HEAD_DIM, (h + 1) * HEAD_DIM)
            p, inv, _ = _attn_exp(q_ref, kp, b_ref, h, q0, before)
            outs.append(jnp.dot(p.astype(BF16), vp[pl.ds(q0, K_SPAN), hs], preferred_element_type=F32) * inv)
        o_ref[...] = jnp.concatenate(outs, axis=1).astype(BF16)

    return pl.pallas_call(
        body, name="attn_fwd", grid=(HD // hw, B, S // Q_BLOCK), in_specs=[qspec, kspec, vspec, bspec],
        out_specs=qspec, out_shape=jax.ShapeDtypeStruct((B, S, HD), BF16),
        scratch_shapes=[pltpu.VMEM((S + PAD, hw), BF16), pltpu.VMEM((S + PAD, hw), BF16)],
        compiler_params=_params(("arbitrary", "arbitrary", "arbitrary")))(q, kv, kv, bias)


def _attn_bwd(q, kv, bias, do, B, S):
    HD = q.shape[-1]
    H = HD // HEAD_DIM
    hw, qspec, kspec, vspec, bspec = _attn_specs(S)
    scale = HEAD_DIM ** -0.5
    nq = S // Q_BLOCK

    def body(q_ref, k_ref, v_ref, b_ref, do_ref, dq_ref, dkv_ref, db_ref, kp, vp, dk_acc, dv_acc):
        b, i = pl.program_id(1), pl.program_id(2)
        q0 = pl.multiple_of(i * Q_BLOCK, Q_BLOCK)

        @pl.when(i == 0)
        def _():
            _load_padded(k_ref, v_ref, kp, vp)
            dk_acc[...] = jnp.zeros_like(dk_acc)
            dv_acc[...] = jnp.zeros_like(dv_acc)

        @pl.when((i == 0) & (b == 0))
        def _():
            db_ref[...] = jnp.zeros_like(db_ref)

        before = _before_start(q0)
        for h in range(HEADS_PER_STEP):
            hs = slice(h * HEAD_DIM, (h + 1) * HEAD_DIM)
            p, inv, kh = _attn_exp(q_ref, kp, b_ref, h, q0, before)
            p = p * inv
            doh = do_ref[:, hs]
            dp = lax.dot_general(doh, vp[pl.ds(q0, K_SPAN), hs], (((1,), (1,)), ((), ())),
                                 preferred_element_type=F32)
            ds = p * (dp - jnp.sum(p * dp, axis=-1, keepdims=True))
            db_ref[h] += ds
            dsb = ds.astype(BF16)
            dq_ref[:, hs] = (jnp.dot(dsb, kh, preferred_element_type=F32) * scale).astype(BF16)
            dk_acc[pl.ds(q0, K_SPAN), hs] += lax.dot_general(dsb, q_ref[:, hs], (((0,), (0,)), ((), ())),
                                                              preferred_element_type=F32)
            dv_acc[pl.ds(q0, K_SPAN), hs] += lax.dot_general(p.astype(BF16), doh, (((0,), (0,)), ((), ())),
                                                              preferred_element_type=F32)

        @pl.when(i == nq - 1)
        def _():
            dkv_ref[0] = dk_acc[PAD:, :].astype(BF16)
            dkv_ref[1] = dv_acc[PAD:, :].astype(BF16)

    return pl.pallas_call(
        body, name="attn_bwd", grid=(HD // hw, B, nq), in_specs=[qspec, kspec, vspec, bspec, qspec],
        out_specs=[qspec, pl.BlockSpec((2, None, S, hw), lambda g, b, i: (0, b, 0, g)), bspec],
        out_shape=[jax.ShapeDtypeStruct((B, S, HD), BF16), jax.ShapeDtypeStruct((2, B, S, HD), BF16),
                   jax.ShapeDtypeStruct((H, Q_BLOCK, K_SPAN), F32)],
        scratch_shapes=[pltpu.VMEM((S + PAD, hw), BF16), pltpu.VMEM((S + PAD, hw), BF16),
                        pltpu.VMEM((S + PAD, hw), F32), pltpu.VMEM((S + PAD, hw), F32)],
        compiler_params=_params(("arbitrary", "arbitrary", "arbitrary")))(q, kv, kv, bias, do)


def _loss_head(h, g, target, *, tm=512):
    T, D = h.shape
    tm = _row_tile(T, tm)

    def body(h_ref, g_ref, t_ref, dh_ref, loss_ref, dg_ref):
        @pl.when(pl.program_id(0) == 0)
        def _():
            loss_ref[...] = jnp.zeros_like(loss_ref)
            dg_ref[...] = jnp.zeros_like(dg_ref)

        n, r = _rms(h_ref[...])
        g = g_ref[...]
        e = n * g - t_ref[...]
        loss_ref[...] += 0.5 * jnp.sum(jnp.mean(e * e, axis=-1, keepdims=True), axis=0, keepdims=True)
        dy = e * (1.0 / D)
        dg_ref[...] += jnp.sum(dy * n, axis=0, keepdims=True)
        t = dy * g
        dh_ref[...] = r * (t - n * jnp.mean(t * n, axis=-1, keepdims=True))

    row = pl.BlockSpec((tm, D), lambda i: (i, 0))
    return pl.pallas_call(
        body, name="loss_head", grid=(T // tm,), in_specs=[row, pl.BlockSpec((1, D), lambda i: (0, 0)), row],
        out_specs=[row, pl.BlockSpec((8, 128), lambda i: (0, 0)), pl.BlockSpec((1, D), lambda i: (0, 0))],
        out_shape=[jax.ShapeDtypeStruct((T, D), F32), jax.ShapeDtypeStruct((8, 128), F32),
                   jax.ShapeDtypeStruct((1, D), F32)],
        compiler_params=_params(("arbitrary",)))(h, g.reshape(1, D), target)


def _sub_rows(R):
    for cand in (256, 352, 128, 64, 8):
        if R % cand == 0 and R > cand:
            return cand
    return R


def _adamw(w, g, m, v, *, name):
    R, C = w.shape
    tr = _sub_rows(R)

    def body(w_ref, g_ref, m_ref, v_ref, d_ref, nm_ref, nv_ref):
        g = g_ref[...]
        m = ADAM_B1 * m_ref[...] + (1.0 - ADAM_B1) * g
        v = ADAM_B2 * v_ref[...] + (1.0 - ADAM_B2) * (g * g)
        m_hat = m / (1.0 - ADAM_B1 ** ADAM_STEP)
        v_hat = v / (1.0 - ADAM_B2 ** ADAM_STEP)
        d_ref[...] = -ADAM_LR * (m_hat / (jnp.sqrt(v_hat) + ADAM_EPS) + ADAM_WD * w_ref[...])
        nm_ref[...] = m
        nv_ref[...] = v

    spec = pl.BlockSpec((tr, C), lambda i: (i, 0))
    return pl.pallas_call(body, name=name, grid=(R // tr,), in_specs=[spec] * 4, out_specs=[spec] * 3,
                          out_shape=[jax.ShapeDtypeStruct((R, C), F32)] * 3,
                          compiler_params=_params(("arbitrary",)))(w, g, m, v)


def _add_pair(units, got, core, *, name):
    n4, R, C = got.shape
    rows = n4 * R
    tr = 512 if rows % 512 == 0 else R

    def body(c_ref, u_ref, got_ref, o_ref):
        o_ref[...] = (u_ref[...].astype(F32) + got_ref[...].astype(F32)).astype(BF16)

    spec = pl.BlockSpec((tr, C), lambda i, c: (i, 0))
    grid_spec = pltpu.PrefetchScalarGridSpec(
        num_scalar_prefetch=1, grid=(rows // tr,),
        in_specs=[pl.BlockSpec((None, tr, C), lambda i, c: (c[0], i, 0)), spec], out_specs=spec)
    out = pl.pallas_call(body, name=name, grid_spec=grid_spec, out_shape=jax.ShapeDtypeStruct((rows, C), BF16),
                         compiler_params=_params(("arbitrary",)))(core.reshape(1), units.reshape(2, rows, C),
                                                                   got.reshape(rows, C))
    return out.reshape(n4, R, C)


def _sum_chips(w, own, got, pos, *, name, layer=0, into=None):
    _, R, C = own.shape
    tr = _sub_rows(R)
    nr = R // tr

    def body(p_ref, own_ref, got_ref, *rest):
        o_ref = rest[-1]
        o_ref[...] = (own_ref[...].astype(F32) + got_ref[0].astype(F32) + got_ref[1].astype(F32)
                      + got_ref[2].astype(F32))

    if w.row_sharded:
        out_map = lambda i, p: (layer, i, p[1])
    else:
        out_map = lambda i, p: (layer, p[1] * nr + i, 0)
    ins = [pos, own, got]
    in_specs = [pl.BlockSpec((None, tr, C), lambda i, p: (p[0], i, 0)),
                pl.BlockSpec((3, tr, C), lambda i, p: (0, i, 0))]
    alias = {}
    if into is not None:
        ins.append(into)
        in_specs.append(ANY)
        alias = {3: 0}
    grid_spec = pltpu.PrefetchScalarGridSpec(num_scalar_prefetch=1, grid=(nr,), in_specs=in_specs,
                                             out_specs=pl.BlockSpec((None, tr, C), out_map))
    return pl.pallas_call(body, name=name, grid_spec=grid_spec, input_output_aliases=alias,
                          out_shape=jax.ShapeDtypeStruct((w.L, w.ks, w.ns), F32),
                          compiler_params=_params(("arbitrary",)))(*ins)


def _mesh_pos():
    return lax.axis_index("x"), lax.axis_index("y"), lax.axis_index("c")


def _other_chips(x, y):
    return [(1 - x, y), (x, 1 - y), (1 - x, 1 - y)]


ANY = pl.BlockSpec(memory_space=pl.ANY)


class _W:
    def __init__(self, name, shard, row_sharded):
        self.name = name
        self.L, ks, ns = shard.shape
        self.row_sharded = row_sharded
        self.K, self.N = (ks * N_CHIPS, ns) if row_sharded else (ks, ns * N_CHIPS)
        self.ks, self.ns = ks, ns

    def shard_of(self, full, j):
        if self.row_sharded:
            return full.at[:, pl.ds(j * self.ks, self.ks), :]
        return full.at[:, :, pl.ds(j * self.ns, self.ns)]

    def half_of(self, shard, c):
        if self.row_sharded:
            return shard.at[:, :, pl.ds(c * (self.ns // 2), self.ns // 2)]
        return shard.at[:, pl.ds(c * (self.ks // 2), self.ks // 2), :]


HBM = pl.BlockSpec(memory_space=pltpu.HBM)
SEM = pl.BlockSpec(memory_space=pltpu.SEMAPHORE)
IN_FLIGHT = pltpu.SideEffectType.DATAFLOW_SIDE_EFFECTING


def _in_hbm(a):
    return pltpu.with_memory_space_constraint(a, pltpu.HBM)


def _gather_start(ws, shards, after):
    nw = len(ws)

    def body(*refs):
        src, dst = refs[:nw], refs[nw:2 * nw]
        send, recv = refs[2 * nw + 1:3 * nw + 1], refs[3 * nw + 1:4 * nw + 1]
        x, y, c = _mesh_pos()
        me = 2 * x + y
        for i, w in enumerate(ws):
            for f, (px, py) in enumerate(_other_chips(x, y)):
                pltpu.make_async_remote_copy(src_ref=w.half_of(src[i], c), dst_ref=w.half_of(w.shard_of(dst[i], me), c),
                                             send_sem=send[i].at[f], recv_sem=recv[i].at[f], device_id=(px, py, c),
                                             device_id_type=MESH).start()

    fulls = [lax.empty((w.L, w.K, w.N), BF16) for w in ws]
    out = pl.pallas_call(
        body, name="gather_start", in_specs=[HBM] * (2 * nw) + [ANY],
        out_specs=[SEM] * (2 * nw) + [HBM] * (2 * nw),
        out_shape=[pltpu.SemaphoreType.DMA((3,))] * (2 * nw)
        + [pltpu.HBM(s.shape, BF16) for s in shards] + [pltpu.HBM(f.shape, BF16) for f in fulls],
        input_output_aliases={i: 2 * nw + i for i in range(2 * nw)},
        compiler_params=pltpu.CompilerParams(has_side_effects=IN_FLIGHT))(
            *[_in_hbm(s) for s in shards], *[_in_hbm(f) for f in fulls], after)
    return [(out[i], out[nw + i], out[2 * nw + i], out[3 * nw + i]) for i in range(nw)]


def _gather_wait(ws, flight, after, *, name):
    nw = len(ws)

    def body(*refs):
        src, dst = refs[:nw], refs[nw:2 * nw]
        send, recv = refs[2 * nw:3 * nw], refs[3 * nw:4 * nw]
        x, y, c = _mesh_pos()
        for i, w in enumerate(ws):
            for f, (px, py) in enumerate(_other_chips(x, y)):
                landed = w.half_of(w.shard_of(dst[i], 2 * px + py), c)
                cp = pltpu.make_async_remote_copy(src_ref=w.half_of(src[i], c), dst_ref=landed, send_sem=send[i].at[f],
                                                  recv_sem=recv[i].at[f], device_id=(px, py, c), device_id_type=MESH)
                cp.wait_send()
                cp.wait_recv()

    shards, fulls = [fl[2] for fl in flight], [fl[3] for fl in flight]
    out = pl.pallas_call(
        body, name=name, in_specs=[HBM] * (2 * nw) + [SEM] * (2 * nw) + [ANY],
        out_specs=[HBM] * (2 * nw),
        out_shape=[pltpu.HBM(s.shape, BF16) for s in shards] + [pltpu.HBM(f.shape, BF16) for f in fulls],
        input_output_aliases={i: i for i in range(2 * nw)},
        compiler_params=pltpu.CompilerParams(has_side_effects=IN_FLIGHT))(
            *shards, *fulls, *[fl[0] for fl in flight], *[fl[1] for fl in flight], after)
    return out[:nw], out[nw:]


def _gather_finish(ws, shards, fulls, *, name):
    nw = len(ws)

    def body(*refs):
        src, dst, stage = refs[:nw], refs[3 * nw:4 * nw], refs[4 * nw:5 * nw]
        send_sems, recv_sems, load_sems, store_sems = refs[5 * nw:]
        x, y, c = _mesh_pos()
        me = 2 * x + y
        sibling = (x, y, 1 - c)
        chips = _other_chips(x, y)

        def fwd(i, w, f, half):
            px, py = chips[f]
            landed = w.half_of(w.shard_of(dst[i], 2 * px + py), half)
            return pltpu.make_async_remote_copy(src_ref=landed, dst_ref=landed, send_sem=send_sems.at[3 * i + f],
                                                recv_sem=recv_sems.at[3 * i + f], device_id=sibling,
                                                device_id_type=MESH)

        loads = [pltpu.make_async_copy(src[i], stage[i], load_sems.at[i]) for i in range(nw)]
        for cp in loads:
            cp.start()
        sends = [fwd(i, w, f, c) for i, w in enumerate(ws) for f in range(3)]
        for cp in sends:
            cp.start()
        stores = [pltpu.make_async_copy(stage[i], w.shard_of(dst[i], me), store_sems.at[i])
                  for i, w in enumerate(ws)]
        for ld, st in zip(loads, stores):
            ld.wait()
            st.start()
        for i, w in enumerate(ws):
            for f in range(3):
                fwd(i, w, f, 1 - c).wait_recv()
        for cp in sends:
            cp.wait_send()
        for cp in stores:
            cp.wait()

    out = pl.pallas_call(
        body, name=name, in_specs=[ANY] * (2 * nw), out_specs=[ANY] * (2 * nw),
        out_shape=[jax.ShapeDtypeStruct(s.shape, BF16) for s in shards]
        + [jax.ShapeDtypeStruct(f.shape, BF16) for f in fulls],
        input_output_aliases={i: i for i in range(2 * nw)},
        scratch_shapes=[pltpu.VMEM((w.L, w.ks, w.ns), BF16) for w in ws]
        + [pltpu.SemaphoreType.DMA((3 * nw,)), pltpu.SemaphoreType.DMA((3 * nw,)), pltpu.SemaphoreType.DMA((nw,)),
           pltpu.SemaphoreType.DMA((nw,))],
        compiler_params=_params(has_side_effects=True))(*shards, *fulls)
    return out[nw:]


def _swap_units(units, *, name):
    nw = len(units)

    def body(*refs):
        src, got = refs[:nw], refs[nw:2 * nw]
        send_sems, recv_sems = refs[2 * nw:]
        x, y, c = _mesh_pos()
        copies = [pltpu.make_async_remote_copy(src_ref=src[i].at[1 - c], dst_ref=got[i], send_sem=send_sems.at[i],
                                               recv_sem=recv_sems.at[i], device_id=(x, y, 1 - c),
                                               device_id_type=MESH) for i in range(nw)]
        for cp in copies:
            cp.start()
        for cp in copies:
            cp.wait()

    return pl.pallas_call(
        body, name=name, in_specs=[ANY] * nw, out_specs=[ANY] * nw,
        out_shape=[jax.ShapeDtypeStruct(u.shape[1:], BF16) for u in units],
        scratch_shapes=[pltpu.SemaphoreType.DMA((nw,)), pltpu.SemaphoreType.DMA((nw,))],
        compiler_params=_params(has_side_effects=True))(*units)


def _scatter_copy(src, got, send, recv, f, chip, c):
    px, py = chip
    return pltpu.make_async_remote_copy(src_ref=src.at[2 * px + py], dst_ref=got.at[f], send_sem=send.at[f],
                                        recv_sem=recv.at[f], device_id=(px, py, c), device_id_type=MESH)


def _scatter_start(sums, *, name):
    nw = len(sums)

    def body(*refs):
        src, got = refs[:nw], refs[nw:2 * nw]
        send, recv = refs[2 * nw:3 * nw], refs[3 * nw:4 * nw]
        x, y, c = _mesh_pos()
        for i in range(nw):
            for f, chip in enumerate(_other_chips(x, y)):
                _scatter_copy(src[i], got[i], send[i], recv[i], f, chip, c).start()

    lands = [lax.empty((3,) + s.shape[1:], BF16) for s in sums]
    out = pl.pallas_call(
        body, name=name, in_specs=[HBM] * (2 * nw), out_specs=[SEM] * (2 * nw) + [HBM] * (2 * nw),
        out_shape=[pltpu.SemaphoreType.DMA((3,))] * (2 * nw)
        + [pltpu.HBM(s.shape, BF16) for s in sums] + [pltpu.HBM(l.shape, BF16) for l in lands],
        input_output_aliases={i: 2 * nw + i for i in range(2 * nw)},
        compiler_params=pltpu.CompilerParams(has_side_effects=IN_FLIGHT))(
            *[_in_hbm(s) for s in sums], *[_in_hbm(l) for l in lands])
    return [(out[i], out[nw + i], out[2 * nw + i], out[3 * nw + i]) for i in range(nw)]


def _scatter_wait(flight, after):
    nw = len(flight)

    def body(*refs):
        src, got = refs[:nw], refs[nw:2 * nw]
        send, recv = refs[2 * nw:3 * nw], refs[3 * nw:4 * nw]
        x, y, c = _mesh_pos()
        for i in range(nw):
            for f, chip in enumerate(_other_chips(x, y)):
                cp = _scatter_copy(src[i], got[i], send[i], recv[i], f, chip, c)
                cp.wait_send()
                cp.wait_recv()

    sums, lands = [fl[2] for fl in flight], [fl[3] for fl in flight]
    out = pl.pallas_call(
        body, name="scatter_wait", in_specs=[HBM] * (2 * nw) + [SEM] * (2 * nw) + [ANY], out_specs=[HBM] * (2 * nw),
        out_shape=[pltpu.HBM(s.shape, BF16) for s in sums] + [pltpu.HBM(l.shape, BF16) for l in lands],
        input_output_aliases={i: i for i in range(2 * nw)},
        compiler_params=pltpu.CompilerParams(has_side_effects=IN_FLIGHT))(
            *sums, *lands, *[fl[0] for fl in flight], *[fl[1] for fl in flight], after)
    return out[:nw], out[nw:]


def _join_halves(ws, shards):
    nw = len(ws)

    def body(*refs):
        buf = refs[nw:2 * nw]
        send_sems, recv_sems = refs[2 * nw:]
        x, y, c = _mesh_pos()
        sibling = (x, y, 1 - c)

        def copy(i, w, half):
            region = w.half_of(buf[i], half)
            return pltpu.make_async_remote_copy(src_ref=region, dst_ref=region, send_sem=send_sems.at[i],
                                                recv_sem=recv_sems.at[i], device_id=sibling, device_id_type=MESH)

        sends = [copy(i, w, c) for i, w in enumerate(ws)]
        for cp in sends:
            cp.start()
        for i, w in enumerate(ws):
            copy(i, w, 1 - c).wait_recv()
        for cp in sends:
            cp.wait_send()

    return pl.pallas_call(
        body, name="join_halves", in_specs=[ANY] * nw, out_specs=[ANY] * nw,
        out_shape=[jax.ShapeDtypeStruct((w.L, w.ks, w.ns), F32) for w in ws],
        input_output_aliases={i: i for i in range(nw)},
        scratch_shapes=[pltpu.SemaphoreType.DMA((nw,)), pltpu.SemaphoreType.DMA((nw,))],
        compiler_params=_params(has_side_effects=True))(*shards)


def _allreduce_small(vec):
    R = vec.shape[0]

    def body(x_ref, o_ref, buf, send_sems, recv_sems):
        x, y, c = _mesh_pos()
        me, sibling = (x, y, c), (x, y, 1 - c)
        chips = _other_chips(x, y)

        def slot(px, py, pc):
            return buf.at[4 * px + 2 * py + pc]

        def copy(k, block, to, src=None):
            return pltpu.make_async_remote_copy(src_ref=slot(*block) if src is None else src, dst_ref=slot(*block),
                                                send_sem=send_sems.at[k], recv_sem=recv_sems.at[k], device_id=to,
                                                device_id_type=MESH)

        first = [copy(0, me, sibling, src=x_ref)] + [copy(1 + f, me, (*chip, c), src=x_ref)
                                                     for f, chip in enumerate(chips)]
        for cp in first:
            cp.start()
        passed = [copy(4 + f, (*chip, c), sibling) for f, chip in enumerate(chips)]
        for f, chip in enumerate(chips):
            copy(1 + f, (*chip, c), me).wait_recv()
            passed[f].start()
        copy(0, sibling, me).wait_recv()
        for f, chip in enumerate(chips):
            copy(4 + f, (*chip, 1 - c), me).wait_recv()
        for cp in first + passed:
            cp.wait_send()
        slot(*me)[...] = x_ref[...]
        acc = buf[0]
        for d in range(1, 8):
            acc = acc + buf[d]
        o_ref[...] = acc

    return pl.pallas_call(
        body, name="allreduce_small", in_specs=[pl.BlockSpec(memory_space=pltpu.VMEM)],
        out_specs=pl.BlockSpec(memory_space=pltpu.VMEM), out_shape=jax.ShapeDtypeStruct((R, 128), F32),
        scratch_shapes=[pltpu.VMEM((8, R, 128), F32), pltpu.SemaphoreType.DMA((7,)), pltpu.SemaphoreType.DMA((7,))],
        compiler_params=_params())(vec)


def _pack(parts):
    flat = jnp.concatenate([p.reshape(-1).astype(F32) for p in parts])
    n = flat.shape[0]
    pad = (-n) % (64 * 128)
    return jnp.pad(flat, (0, pad)).reshape(-1, 128)


def _unpack(vec, shapes):
    flat = vec.reshape(-1)
    out, off = [], 0
    for s in shapes:
        n = int(np.prod(s))
        out.append(flat[off:off + n].reshape(s))
        off += n
    return out


def kernel(x, a_norm_g, a_w_in, a_v_norm_g, a_w_s, a_b_s, a_w_out, kv_norm_g, w_kv, b_norm_g, b_w_q, b_rel_bias, b_w_o, f_norm_g, f_w_in, f_conv_w, f_conv_b, f_w_down, final_norm_g, loss_target, m_a_norm_g, m_a_w_in, m_a_v_norm_g, m_a_w_s, m_a_b_s, m_a_w_out, m_kv_norm_g, m_w_kv, m_b_norm_g, m_b_w_q, m_b_rel_bias, m_b_w_o, m_f_norm_g, m_f_w_in, m_f_conv_w, m_f_conv_b, m_f_w_down, m_final_norm_g, v_a_norm_g, v_a_w_in, v_a_v_norm_g, v_a_w_s, v_a_b_s, v_a_w_out, v_kv_norm_g, v_w_kv, v_b_norm_g, v_b_w_q, v_b_rel_bias, v_b_w_o, v_f_norm_g, v_f_w_in, v_f_conv_w, v_f_conv_b, v_f_w_down, v_final_norm_g):
    B, S, D = x.shape
    T = B * S
    xi, yi, ci = lax.axis_index("x"), lax.axis_index("y"), lax.axis_index("c")
    j_me = (2 * xi + yi).astype(jnp.int32)
    core = ci.astype(jnp.int32)
    pos = jnp.stack([j_me, core])

    w_shards = {"a_w_in": (a_w_in, False), "a_w_out": (a_w_out, True), "w_kv": (w_kv[None], False),
                "b_w_q": (b_w_q, True), "b_w_o": (b_w_o, True), "f_w_in": (f_w_in, False), "f_w_down": (f_w_down, True)}
    names = list(w_shards)
    ws = [_W(n, w_shards[n][0], w_shards[n][1]) for n in names]
    g_shards = {"a_w_in": (a_w_in, False), "a_w_out": (a_w_out, True),
                "f_w_in0": (f_w_in[0:1], False), "f_w_down0": (f_w_down[0:1], True),
                "w_kv": (w_kv[None], False), "b_w_q": (b_w_q, True), "b_w_o": (b_w_o, True),
                "f_w_in1": (f_w_in[1:2], False), "f_w_down1": (f_w_down[1:2], True)}
    g_names = list(g_shards)
    g_ws = {n: _W(n, *g_shards[n]) for n in g_names}

    Wd = a_w_in.shape[1]
    GW = a_v_norm_g.shape[1] * N_CHIPS
    F2 = f_conv_w.shape[2] * N_CHIPS
    Fh = F2 // 2
    nsd, nsg, nsf = a_norm_g.shape[1], a_v_norm_g.shape[1], f_conv_w.shape[2]
    own = (ci == 0).astype(F32)
    place = lambda sh, width, n: lax.dynamic_update_slice_in_dim(
        jnp.zeros(sh.shape[:-1] + (width,), F32), sh * own, j_me * n, axis=sh.ndim - 1)
    gathered = _allreduce_small(_pack([place(a_norm_g, Wd, nsd), place(a_v_norm_g, GW, nsg),
                                       place(f_conv_w, F2, nsf)]))
    a_g, a_vg, conv_w = _unpack(gathered, [(1, Wd), (1, GW), (2, 3, F2)])

    flight = dict(zip(g_names, _gather_start([g_ws[n] for n in g_names],
                                             [g_shards[n][0].astype(BF16) for n in g_names], gathered)))
    full = {}

    def arrive(group, after, tag):
        gw = [g_ws[n] for n in group]
        sh, fu = _gather_wait(gw, [flight[n] for n in group], after, name=f"gather_wait_{tag}")
        full.update(zip(group, _gather_finish(gw, sh, fu, name=f"gather_finish_{tag}")))
    conv_w2 = conv_w.reshape(2, 3, 2, Fh).transpose(0, 2, 1, 3)
    conv_b2 = f_conv_b.reshape(2, 2, Fh)

    h0 = x.reshape(T, D)
    target = loss_target.reshape(T, D)
    bs_tile = jnp.repeat(a_b_s[0].T, GROUP_DIM, axis=1)
    ws_a = a_w_s[0]
    scale = HEAD_DIM ** -0.5
    HD = b_w_q.shape[2]
    H = HD // HEAD_DIM
    n_rel = b_rel_bias.shape[-1]
    frow = b_rel_bias[0][:, _bias_index()].reshape(H, 1, F_LEN)
    bias = _bias_expand(frow)

    def ffn_fwd(h, l):
        a = _mm(h, full[f"f_w_in{l}"], layer=0, norm_g=f_norm_g[l], split_out=True, out_dtype=BF16,
                name=f"ffn{l}_in")
        yff = _conv_fwd(a, conv_w2[l], conv_b2[l], S)
        return _mm(yff, full[f"f_w_down{l}"], layer=0, res=h, name=f"ffn{l}_down"), a, yff

    arrive(["a_w_in", "a_w_out"], h0, "a")
    zp = _mm(h0, full["a_w_in"], layer=0, norm_g=a_g[0], name="a_in")
    out_a = _gate_fwd(zp, a_vg, ws_a, bs_tile)
    h1 = _mm(out_a, full["a_w_out"], layer=0, res=h0, name="a_out")
    arrive(["f_w_in0", "f_w_down0"], h1, "f0")
    h2, a0, yff0 = ffn_fwd(h1, 0)
    arrive(["w_kv", "b_w_q", "b_w_o"], h2, "b")
    arrive(["f_w_in1", "f_w_down1"], h2, "f1")
    kv = _mm(h2, full["w_kv"], layer=0, norm_g=kv_norm_g, out_dtype=BF16, split_out=True, name="kv")
    q = _mm(h2, full["b_w_q"], layer=0, norm_g=b_norm_g[0], scale=scale, out_dtype=BF16, name="q")
    kv4, q3 = kv.reshape(2, B, S, HD), q.reshape(B, S, HD)
    o = _attn_fwd(q3, kv4, bias, B, S).reshape(T, HD)
    h3 = _mm(o, full["b_w_o"], layer=0, res=h2, name="attn_out")
    h4, a1, yff1 = ffn_fwd(h3, 1)
    dh, loss8, dg_final = _loss_head(h4, final_norm_g, target)

    units = {}

    def ffn_bwd(dh, h, a, yff, l):
        dyff = _mm(dh, full[f"f_w_down{l}"], layer=0, trans_w=True, out_dtype=BF16, name=f"ffn{l}_down_dx")
        units[f"f_w_down{l}"] = _mm_tn(yff, dh, rows_are_shards=True, name=f"ffn{l}_down_dw")
        da, dcw, dcb = _conv_bwd(a, dyff, conv_w2[l], conv_b2[l], S)
        units[f"f_w_in{l}"] = _mm_tn(h, da, norm_g=f_norm_g[l], split_y=True, name=f"ffn{l}_in_dw")
        dh, dg = _mm(da, full[f"f_w_in{l}"], layer=0, trans_w=True, split_x=True, bwd=(h, f_norm_g[l], dh), tm=256,
                     name=f"ffn{l}_in_dx")
        return dh, dg, dcw, dcb

    in_flight = {}

    def reduce_start(group, tag):
        got = _swap_units([units[n] for n in group], name=f"swap_{tag}")
        sums = [_add_pair(units[n], g_, core, name=f"pair_{n}") for n, g_ in zip(group, got)]
        in_flight.update(zip(group, _scatter_start(sums, name=f"scatter_start_{tag}")))

    dh, dg_f1, dcw1, dcb1 = ffn_bwd(dh, h3, a1, yff1, 1)
    reduce_start(["f_w_down1", "f_w_in1"], "f1")
    do = _mm(dh, full["b_w_o"], layer=0, trans_w=True, out_dtype=BF16, name="attn_out_dx")
    units["b_w_o"] = _mm_tn(o, dh, rows_are_shards=True, name="b_w_o_dw")
    dq, dkv, dbias = _attn_bwd(q3, kv4, bias, do.reshape(B, S, HD), B, S)
    d_rel = _bias_reduce(dbias, n_rel).reshape(1, H, n_rel)
    dq, dkv = dq.reshape(T, HD), dkv.reshape(2, T, HD)
    units["b_w_q"] = _mm_tn(h2, dq, norm_g=b_norm_g[0], rows_are_shards=True, name="b_w_q_dw")
    dh, dg_b = _mm(dq, full["b_w_q"], layer=0, trans_w=True, bwd=(h2, b_norm_g[0], dh), name="q_dx")
    units["w_kv"] = _mm_tn(h2, dkv, norm_g=kv_norm_g, split_y=True, name="w_kv_dw")
    dh, dg_kv = _mm(dkv, full["w_kv"], layer=0, trans_w=True, split_x=True, bwd=(h2, kv_norm_g, dh), name="kv_dx")
    reduce_start(["b_w_o", "b_w_q", "w_kv"], "b")
    dh, dg_f0, dcw0, dcb0 = ffn_bwd(dh, h1, a0, yff0, 0)
    reduce_start(["f_w_down0", "f_w_in0"], "f0")
    d_out = _mm(dh, full["a_w_out"], layer=0, trans_w=True, out_dtype=BF16, name="a_out_dx")
    units["a_w_out"] = _mm_tn(out_a, dh, rows_are_shards=True, name="a_w_out_dw")
    dzp, dws, dbs, dgv = _gate_bwd(zp, d_out, a_vg, ws_a, bs_tile)
    units["a_w_in"] = _mm_tn(h0, dzp, norm_g=a_g[0], name="a_w_in_dw")
    reduce_start(["a_w_out", "a_w_in"], "a")
    grad_x, dg_a = _mm(dzp, full["a_w_in"], layer=0, trans_w=True, bwd=(h0, a_g[0], dh), name="a_in_dx")

    sums, recv = _scatter_wait([in_flight[n] for n in g_names], grad_x)
    sums, recv = dict(zip(g_names, sums)), dict(zip(g_names, recv))
    halves = []
    for n, w in zip(names, ws):
        if w.L == 1:
            halves.append(_sum_chips(w, sums[n], recv[n], pos, name=f"chips_{n}"))
        else:
            first = _sum_chips(w, sums[n + "0"], recv[n + "0"], pos, name=f"chips_{n}0")
            halves.append(_sum_chips(w, sums[n + "1"], recv[n + "1"], pos, layer=1, into=first, name=f"chips_{n}1"))
    g_big = dict(zip(names, _join_halves(ws, halves)))
    g_big["w_kv"] = g_big["w_kv"][0]

    to_flat = lambda d: d.transpose(1, 0, 2).reshape(3, F2)
    small = {"a_norm_g": dg_a, "a_v_norm_g": dgv, "a_w_s": dws[None], "a_b_s": dbs[None], "kv_norm_g": dg_kv[0],
             "b_norm_g": dg_b, "b_rel_bias": d_rel, "f_norm_g": jnp.concatenate([dg_f0, dg_f1], axis=0),
             "f_conv_w": jnp.stack([to_flat(dcw0), to_flat(dcw1)]),
             "f_conv_b": jnp.stack([dcb0.reshape(F2), dcb1.reshape(F2)]), "final_norm_g": dg_final[0]}
    snames = list(small)
    red = _allreduce_small(_pack([small[n] for n in snames] + [loss8[0:1, 0:1]]))
    parts = _unpack(red, [small[n].shape for n in snames] + [(1,)])
    g_small = dict(zip(snames, parts[:-1]))
    loss = parts[-1][0]
    g_small["a_norm_g"] = lax.dynamic_slice_in_dim(g_small["a_norm_g"], j_me * nsd, nsd, axis=1)
    g_small["a_v_norm_g"] = lax.dynamic_slice_in_dim(g_small["a_v_norm_g"], j_me * nsg, nsg, axis=1)
    g_small["f_conv_w"] = lax.dynamic_slice_in_dim(g_small["f_conv_w"], j_me * nsf, nsf, axis=2)

    given = dict(a_norm_g=(a_norm_g, m_a_norm_g, v_a_norm_g), a_w_in=(a_w_in, m_a_w_in, v_a_w_in),
                 a_v_norm_g=(a_v_norm_g, m_a_v_norm_g, v_a_v_norm_g), a_w_s=(a_w_s, m_a_w_s, v_a_w_s),
                 a_b_s=(a_b_s, m_a_b_s, v_a_b_s), a_w_out=(a_w_out, m_a_w_out, v_a_w_out),
                 kv_norm_g=(kv_norm_g, m_kv_norm_g, v_kv_norm_g), w_kv=(w_kv, m_w_kv, v_w_kv),
                 b_norm_g=(b_norm_g, m_b_norm_g, v_b_norm_g), b_w_q=(b_w_q, m_b_w_q, v_b_w_q),
                 b_rel_bias=(b_rel_bias, m_b_rel_bias, v_b_rel_bias), b_w_o=(b_w_o, m_b_w_o, v_b_w_o),
                 f_norm_g=(f_norm_g, m_f_norm_g, v_f_norm_g), f_w_in=(f_w_in, m_f_w_in, v_f_w_in),
                 f_conv_w=(f_conv_w, m_f_conv_w, v_f_conv_w), f_conv_b=(f_conv_b, m_f_conv_b, v_f_conv_b),
                 f_w_down=(f_w_down, m_f_w_down, v_f_w_down), final_norm_g=(final_norm_g, m_final_norm_g, v_final_norm_g))
    order = list(given)
    grads, deltas, new_m, new_v = {}, {}, {}, {}
    for n in names:
        w_, m_, v_ = given[n]
        g_ = g_big[n]
        C = w_.shape[-1]
        d2, m2, v2 = _adamw(w_.reshape(-1, C), g_.reshape(-1, C), m_.reshape(-1, C), v_.reshape(-1, C),
                            name=f"adamw_{n}")
        grads[n], deltas[n], new_m[n], new_v[n] = g_.reshape(w_.shape), d2.reshape(w_.shape), m2.reshape(w_.shape), \
            v2.reshape(w_.shape)
    sm = [n for n in order if n not in names]
    d2, m2, v2 = _adamw(_pack([given[n][0] for n in sm]), _pack([g_small[n].reshape(given[n][0].shape) for n in sm]),
                        _pack([given[n][1] for n in sm]), _pack([given[n][2] for n in sm]), name="adamw_small")
    shapes = [given[n][0].shape for n in sm]
    for n, d_, m_, v_ in zip(sm, _unpack(d2, shapes), _unpack(m2, shapes), _unpack(v2, shapes)):
        grads[n], deltas[n], new_m[n], new_v[n] = g_small[n].reshape(given[n][0].shape), d_, m_, v_

    return (loss, grad_x.reshape(B, S, D), *[grads[n] for n in order], *[deltas[n] for n in order],
            *[new_m[n] for n in order], *[new_v[n] for n in order])
```

```python
import math

import numpy as np
import jax
import jax.numpy as jnp
from jax import lax
from jax.experimental import pallas as pl
from jax.experimental.pallas import tpu as pltpu

F32 = jnp.float32
BF16 = jnp.bfloat16
MESH = pl.DeviceIdType.MESH

EPS = 1e-6
NEG_INF = -1e30
CHUNK = 64
GMLP_BLOCK = 128
GROUP_DIM = 128
HEAD_DIM = 64
LEFT_CHUNKS = 8
PAD = LEFT_CHUNKS * CHUNK
REL_CLIP = 128
Q_BLOCK = 256
K_SPAN = PAD + Q_BLOCK
F_LEN = K_SPAN + Q_BLOCK
HEADS_PER_STEP = 4
N_CHIPS = 4

ADAM_LR = 0.001
ADAM_B1 = 0.9
ADAM_B2 = 0.999
ADAM_EPS = 1e-08
ADAM_WD = 0.01
ADAM_STEP = 10

VMEM_LIMIT = 56 * 1024 * 1024


def _params(sem=None, **kw):
    if sem is not None:
        kw["dimension_semantics"] = sem
    return pltpu.CompilerParams(vmem_limit_bytes=VMEM_LIMIT, **kw)


def _rms(xf):
    r = lax.rsqrt(jnp.mean(xf * xf, axis=-1, keepdims=True) + EPS)
    return xf * r, r


def _gelu(x):
    c = math.sqrt(2.0 / math.pi)
    return 0.5 * x * (1.0 + jnp.tanh(c * (x + 0.044715 * x * x * x)))


def _gelu_grad(x):
    c = math.sqrt(2.0 / math.pi)
    t = jnp.tanh(c * (x + 0.044715 * x * x * x))
    return 0.5 * (1.0 + t) + 0.5 * x * (1.0 - t * t) * c * (1.0 + 3.0 * 0.044715 * x * x)


def _col_tile(n):
    if n <= 1024:
        return n
    for t in (1408, 1024, 512):
        if n % t == 0:
            return t
    raise ValueError(n)


def _row_tile(t, want):
    while t % want:
        want //= 2
    return want


def _mm(x, w, *, name, layer=None, trans_w=False, norm_g=None, res=None, scale=None, out_dtype=F32, bwd=None,
        split_out=False, split_x=False, emit_norm=False, tm=512):
    T = x.shape[-2]
    K = 2 * x.shape[-1] if split_x else x.shape[-1]
    N = w.shape[-2] if trans_w else w.shape[-1]
    tn = N
    tm = _row_tile(T, 256 if N > 4096 else tm)
    nn, nm = N // tn, T // tm
    has_norm, has_res, has_bwd = norm_g is not None, res is not None, bwd is not None
    dims = (((1,), (1,)), ((), ())) if trans_w else (((1,), (0,)), ((), ()))

    def body(*refs):
        it = iter(refs)
        x_ref, w_ref = next(it), next(it)
        g_ref = next(it) if has_norm else None
        res_ref = next(it) if has_res else None
        if has_bwd:
            h_ref, bg_ref, dh_ref = next(it), next(it), next(it)
        o_ref = next(it)
        if split_x:
            kh = K // 2
            acc = lax.dot_general(x_ref[0].astype(BF16), w_ref[:, :kh] if trans_w else w_ref[:kh, :], dims,
                                  preferred_element_type=F32)
            acc = acc + lax.dot_general(x_ref[1].astype(BF16), w_ref[:, kh:] if trans_w else w_ref[kh:, :], dims,
                                        preferred_element_type=F32)
        else:
            xv = x_ref[...]
            if has_norm:
                xv = _rms(xv.astype(F32))[0] * g_ref[...]
            xb = xv.astype(BF16)
            if emit_norm:
                refs[-1][...] = xb
            acc = lax.dot_general(xb, w_ref[...], dims, preferred_element_type=F32)
        if scale is not None:
            acc = acc * scale
        if has_res:
            acc = acc + res_ref[...]
        if has_bwd:
            dg_ref = next(it)
            n, r = _rms(h_ref[...])

            @pl.when(pl.program_id(1) == 0)
            def _():
                dg_ref[...] = jnp.zeros_like(dg_ref)

            dg_ref[...] += jnp.sum(acc * n, axis=0, keepdims=True)
            t = acc * bg_ref[...]
            o_ref[...] = dh_ref[...] + r * (t - n * jnp.mean(t * n, axis=-1, keepdims=True))
        elif split_out:
            o_ref[0] = acc[:, :N // 2].astype(out_dtype)
            o_ref[1] = acc[:, N // 2:].astype(out_dtype)
        else:
            o_ref[...] = acc.astype(out_dtype)

    lead = () if layer is None else (None,)
    lidx = () if layer is None else (layer,)
    ins = [x, w]
    xspec = (pl.BlockSpec((2, tm, K // 2), lambda n, m: (0, m, 0)) if split_x
             else pl.BlockSpec((tm, K), lambda n, m: (m, 0)))
    wspec = (pl.BlockSpec(lead + (tn, K), lambda n, m: lidx + (n, 0)) if trans_w
             else pl.BlockSpec(lead + (K, tn), lambda n, m: lidx + (0, n)))
    in_specs = [xspec, wspec]
    if has_norm:
        ins.append(norm_g.reshape(1, K))
        in_specs.append(pl.BlockSpec((1, K), lambda n, m: (0, 0)))
    if has_res:
        ins.append(res)
        in_specs.append(pl.BlockSpec((tm, tn), lambda n, m: (m, n)))
    if split_out:
        out_shape = [jax.ShapeDtypeStruct((2, T, N // 2), out_dtype)]
        out_specs = [pl.BlockSpec((2, tm, N // 2), lambda n, m: (0, m, 0))]
    else:
        out_shape = [jax.ShapeDtypeStruct((T, N), F32 if has_bwd else out_dtype)]
        out_specs = [pl.BlockSpec((tm, tn), lambda n, m: (m, n))]
    if has_bwd:
        h, g, dh = bwd
        ins += [h, g.reshape(1, N), dh]
        in_specs += [pl.BlockSpec((tm, N), lambda n, m: (m, 0)), pl.BlockSpec((1, N), lambda n, m: (0, 0)),
                     pl.BlockSpec((tm, N), lambda n, m: (m, 0))]
        out_shape.append(jax.ShapeDtypeStruct((1, N), F32))
        out_specs.append(pl.BlockSpec((1, N), lambda n, m: (0, 0)))
    if emit_norm:
        out_shape.append(jax.ShapeDtypeStruct((T, K), BF16))
        out_specs.append(pl.BlockSpec((tm, K), lambda n, m: (m, 0)))
    out = pl.pallas_call(body, name=name, grid=(nn, nm), in_specs=in_specs, out_specs=out_specs, out_shape=out_shape,
                         compiler_params=_params(("arbitrary", "arbitrary")))(*ins)
    return out if has_bwd or emit_norm else out[0]


def _mm_tn(x, dy, *, name, rows_are_shards=False, split_y=False, tt=512):
    T, K = x.shape
    N = 2 * dy.shape[-1] if split_y else dy.shape[-1]
    R, C = (K // N_CHIPS, N // 2) if rows_are_shards else (K // 2, N // N_CHIPS)
    nn = 2 if split_y else 1
    tn = N // nn
    per = N_CHIPS // nn
    assert not (rows_are_shards and split_y)
    tt = _row_tile(T, tt)
    nt = T // tt

    def body(x_ref, y_ref, o_ref, acc_ref):
        t = pl.program_id(1)

        @pl.when(t == 0)
        def _():
            acc_ref[...] = jnp.zeros_like(acc_ref)

        acc_ref[...] += lax.dot_general(x_ref[...], y_ref[...].astype(BF16), (((0,), (0,)), ((), ())),
                                        preferred_element_type=F32)

        @pl.when(t == nt - 1)
        def _():
            if rows_are_shards:
                for h in range(2):
                    o_ref[h] = acc_ref[:, h * C:(h + 1) * C].astype(BF16).reshape(N_CHIPS, R, C)
            else:
                for j in range(per):
                    o_ref[:, j] = acc_ref[:, j * C:(j + 1) * C].astype(BF16).reshape(2, R, C)

    if split_y:
        yspec = pl.BlockSpec((None, tt, tn), lambda n, t: (n, t, 0))
    else:
        yspec = pl.BlockSpec((tt, tn), lambda n, t: (t, 0))
    if rows_are_shards:
        out_spec = pl.BlockSpec((2, N_CHIPS, R, C), lambda n, t: (0, 0, 0, 0))
    else:
        out_spec = pl.BlockSpec((2, per, R, C), lambda n, t: (0, n, 0, 0))
    return pl.pallas_call(body, name=name, grid=(nn, nt),
                          in_specs=[pl.BlockSpec((tt, K), lambda n, t: (t, 0)), yspec], out_specs=out_spec,
                          out_shape=jax.ShapeDtypeStruct((2, N_CHIPS, R, C), BF16),
                          scratch_shapes=[pltpu.VMEM((K, tn), F32)],
                          compiler_params=_params(("arbitrary", "arbitrary")))(x, dy)


def _chunk_mask():
    i = lax.broadcasted_iota(jnp.int32, (GMLP_BLOCK, GMLP_BLOCK), 0) // CHUNK
    j = lax.broadcasted_iota(jnp.int32, (GMLP_BLOCK, GMLP_BLOCK), 1) // CHUNK
    return i >= j


def _gate_fwd(zp, gv, ws, bs_tile, *, tm=256):
    T, W2 = zp.shape
    W = W2 // 2
    G = W // GROUP_DIM
    tm = _row_tile(T, tm)

    def body(zp_ref, gv_ref, ws_ref, bs_ref, o_ref):
        z = _gelu(zp_ref[...])
        u, v = z[:, :W], z[:, W:]
        vn = _rms(v)[0] * gv_ref[...]
        mask = _chunk_mask()
        for g in range(G):
            cs = slice(g * GROUP_DIM, (g + 1) * GROUP_DIM)
            wg = jnp.where(mask, ws_ref[g], 0.0).astype(BF16)
            for b in range(tm // GMLP_BLOCK):
                rs = slice(b * GMLP_BLOCK, (b + 1) * GMLP_BLOCK)
                s = jnp.dot(wg, vn[rs, cs].astype(BF16), preferred_element_type=F32) + bs_ref[:, cs]
                o_ref[rs, cs] = (u[rs, cs] * s).astype(BF16)

    return pl.pallas_call(
        body, name="gate_fwd", grid=(T // tm,),
        in_specs=[pl.BlockSpec((tm, W2), lambda i: (i, 0)), pl.BlockSpec((1, W), lambda i: (0, 0)),
                  pl.BlockSpec((G, GMLP_BLOCK, GMLP_BLOCK), lambda i: (0, 0, 0)),
                  pl.BlockSpec((GMLP_BLOCK, W), lambda i: (0, 0))],
        out_specs=pl.BlockSpec((tm, W), lambda i: (i, 0)), out_shape=jax.ShapeDtypeStruct((T, W), BF16),
        compiler_params=_params(("arbitrary",)))(zp, gv, ws, bs_tile)


def _gate_bwd(zp, d_out, gv, ws, bs_tile, *, tm=256):
    T, W2 = zp.shape
    W = W2 // 2
    G = W // GROUP_DIM
    tm = _row_tile(T, tm)
    nm = T // tm

    def body(zp_ref, do_ref, gv_ref, ws_ref, bs_ref, dzp_ref, dws_ref, dbs_ref, dgv_ref, du_scr, dvn_scr, dsum_scr):
        i = pl.program_id(0)

        @pl.when(i == 0)
        def _():
            dws_ref[...] = jnp.zeros_like(dws_ref)
            dgv_ref[...] = jnp.zeros_like(dgv_ref)
            dsum_scr[...] = jnp.zeros_like(dsum_scr)

        zp = zp_ref[...]
        z = _gelu(zp)
        u, v = z[:, :W], z[:, W:]
        n, r = _rms(v)
        gv = gv_ref[...]
        vn = n * gv
        d_out = do_ref[...].astype(F32)
        mask = _chunk_mask()
        for g in range(G):
            cs = slice(g * GROUP_DIM, (g + 1) * GROUP_DIM)
            wg = jnp.where(mask, ws_ref[g], 0.0).astype(BF16)
            dw = jnp.zeros((GMLP_BLOCK, GMLP_BLOCK), F32)
            for b in range(tm // GMLP_BLOCK):
                rs = slice(b * GMLP_BLOCK, (b + 1) * GMLP_BLOCK)
                vb = vn[rs, cs].astype(BF16)
                s = jnp.dot(wg, vb, preferred_element_type=F32) + bs_ref[:, cs]
                du_scr[rs, cs] = d_out[rs, cs] * s
                ds = d_out[rs, cs] * u[rs, cs]
                dsb = ds.astype(BF16)
                dvn_scr[rs, cs] = lax.dot_general(wg, dsb, (((0,), (0,)), ((), ())), preferred_element_type=F32)
                dw = dw + lax.dot_general(dsb, vb, (((1,), (1,)), ((), ())), preferred_element_type=F32)
                dsum_scr[:, cs] += ds
            dws_ref[g] += jnp.where(mask, dw, 0.0)
        dvn = dvn_scr[...]
        dgv_ref[...] += jnp.sum(dvn * n, axis=0, keepdims=True)
        t = dvn * gv
        dv = r * (t - n * jnp.mean(t * n, axis=-1, keepdims=True))
        dzp_ref[:, :W] = (du_scr[...] * _gelu_grad(zp[:, :W])).astype(BF16)
        dzp_ref[:, W:] = (dv * _gelu_grad(zp[:, W:])).astype(BF16)

        @pl.when(i == nm - 1)
        def _():
            sel = (lax.broadcasted_iota(jnp.int32, (G, W), 1) // GROUP_DIM
                   == lax.broadcasted_iota(jnp.int32, (G, W), 0)).astype(F32)
            dbs_ref[...] = lax.dot_general(sel, dsum_scr[...], (((1,), (1,)), ((), ())),
                                           precision=lax.Precision.HIGHEST, preferred_element_type=F32)

    return pl.pallas_call(
        body, name="gate_bwd", grid=(nm,),
        in_specs=[pl.BlockSpec((tm, W2), lambda i: (i, 0)), pl.BlockSpec((tm, W), lambda i: (i, 0)),
                  pl.BlockSpec((1, W), lambda i: (0, 0)),
                  pl.BlockSpec((G, GMLP_BLOCK, GMLP_BLOCK), lambda i: (0, 0, 0)),
                  pl.BlockSpec((GMLP_BLOCK, W), lambda i: (0, 0))],
        out_specs=[pl.BlockSpec((tm, W2), lambda i: (i, 0)),
                   pl.BlockSpec((G, GMLP_BLOCK, GMLP_BLOCK), lambda i: (0, 0, 0)),
                   pl.BlockSpec((G, GMLP_BLOCK), lambda i: (0, 0)), pl.BlockSpec((1, W), lambda i: (0, 0))],
        out_shape=[jax.ShapeDtypeStruct((T, W2), BF16), jax.ShapeDtypeStruct((G, GMLP_BLOCK, GMLP_BLOCK), F32),
                   jax.ShapeDtypeStruct((G, GMLP_BLOCK), F32), jax.ShapeDtypeStruct((1, W), F32)],
        scratch_shapes=[pltpu.VMEM((tm, W), F32), pltpu.VMEM((tm, W), F32), pltpu.VMEM((GMLP_BLOCK, W), F32)],
        compiler_params=_params(("arbitrary",)))(zp, d_out, gv, ws, bs_tile)


HALO = 16


def _taps(ext, w, b):
    a, a1, a2 = ext[HALO:], pltpu.roll(ext, 1, 0)[HALO:], pltpu.roll(ext, 2, 0)[HALO:]
    return w[2:3] * a + w[1:2] * a1 + w[0:1] * a2 + b, a, a1, a2


def _conv_fwd(a, cw, cb, S, *, tm=256):
    _, T, F = a.shape
    tc = _col_tile(F)
    tm = _row_tile(S, tm)
    hb = tm // HALO

    def body(a_ref, p_ref, w_ref, b_ref, o_ref):
        first = (pl.program_id(1) * tm) % S == 0
        keep = jnp.where(first, 0.0, 1.0)

        def conv(s):
            ext = jnp.concatenate([p_ref[s].astype(F32) * keep, a_ref[s].astype(F32)], axis=0)
            return _taps(ext, w_ref[s], b_ref[s:s + 1, :])[0]

        up, gate = conv(0), conv(1)
        o_ref[...] = (gate * jax.nn.sigmoid(gate) * up).astype(BF16)

    return pl.pallas_call(
        body, name="conv_fwd", grid=(F // tc, T // tm),
        in_specs=[pl.BlockSpec((2, tm, tc), lambda j, i: (0, i, j)),
                  pl.BlockSpec((2, HALO, tc), lambda j, i: (0, jnp.maximum(i * hb - 1, 0), j)),
                  pl.BlockSpec((2, 3, tc), lambda j, i: (0, 0, j)), pl.BlockSpec((2, tc), lambda j, i: (0, j))],
        out_specs=pl.BlockSpec((tm, tc), lambda j, i: (i, j)), out_shape=jax.ShapeDtypeStruct((T, F), BF16),
        compiler_params=_params(("arbitrary", "arbitrary")))(a, a, cw, cb)


def _conv_bwd(a, dy, cw, cb, S, *, tm=256):
    _, T, F = a.shape
    tc = _col_tile(F)
    tm = _row_tile(S, tm)
    nm = T // tm
    hb = tm // HALO
    TE = tm + HALO
    nxt = lambda j, i: jnp.minimum((i + 1) * hb, T // HALO - 1)

    def body(a_ref, p_ref, n_ref, dy_ref, ndy_ref, w_ref, b_ref, da_ref, dw_ref, db_ref):
        i = pl.program_id(1)
        first = (i * tm) % S == 0
        last = ((i + 1) * tm) % S == 0
        keep_p = jnp.where(first, 0.0, 1.0)
        keep_n = jnp.where(last, 0.0, 1.0)
        dyf = jnp.concatenate([dy_ref[...].astype(F32), ndy_ref[...].astype(F32) * keep_n], axis=0)

        def pre(s):
            ext = jnp.concatenate([p_ref[s].astype(F32) * keep_p, a_ref[s].astype(F32), n_ref[s].astype(F32)],
                                  axis=0)
            return _taps(ext, w_ref[s], b_ref[s:s + 1, :])

        up, au, au1, au2 = pre(0)
        gate, ag, ag1, ag2 = pre(1)
        sg = jax.nn.sigmoid(gate)
        d_up = dyf * (gate * sg)
        d_gate = dyf * up * (sg * (1.0 + gate * (1.0 - sg)))

        @pl.when(i == 0)
        def _():
            dw_ref[...] = jnp.zeros_like(dw_ref)
            db_ref[...] = jnp.zeros_like(db_ref)

        def back(s, d, a, a1, a2):
            own = d[:tm]
            w = w_ref[s]
            db_ref[s:s + 1, :] += jnp.sum(own, axis=0, keepdims=True)
            dw_ref[s, 2:3, :] += jnp.sum(own * a[:tm], axis=0, keepdims=True)
            dw_ref[s, 1:2, :] += jnp.sum(own * a1[:tm], axis=0, keepdims=True)
            dw_ref[s, 0:1, :] += jnp.sum(own * a2[:tm], axis=0, keepdims=True)
            da = w[2:3] * d + w[1:2] * pltpu.roll(d, TE - 1, 0) + w[0:1] * pltpu.roll(d, TE - 2, 0)
            da_ref[s] = da[:tm].astype(BF16)

        back(0, d_up, au, au1, au2)
        back(1, d_gate, ag, ag1, ag2)

    return pl.pallas_call(
        body, name="conv_bwd", grid=(F // tc, nm),
        in_specs=[pl.BlockSpec((2, tm, tc), lambda j, i: (0, i, j)),
                  pl.BlockSpec((2, HALO, tc), lambda j, i: (0, jnp.maximum(i * hb - 1, 0), j)),
                  pl.BlockSpec((2, HALO, tc), lambda j, i: (0, nxt(j, i), j)),
                  pl.BlockSpec((tm, tc), lambda j, i: (i, j)), pl.BlockSpec((HALO, tc), lambda j, i: (nxt(j, i), j)),
                  pl.BlockSpec((2, 3, tc), lambda j, i: (0, 0, j)), pl.BlockSpec((2, tc), lambda j, i: (0, j))],
        out_specs=[pl.BlockSpec((2, tm, tc), lambda j, i: (0, i, j)), pl.BlockSpec((2, 3, tc), lambda j, i: (0, 0, j)),
                   pl.BlockSpec((2, tc), lambda j, i: (0, j))],
        out_shape=[jax.ShapeDtypeStruct((2, T, F), BF16), jax.ShapeDtypeStruct((2, 3, F), F32),
                   jax.ShapeDtypeStruct((2, F), F32)],
        compiler_params=_params(("arbitrary", "arbitrary")))(a, a, a, dy, dy, cw, cb)


def _bias_index():
    idx = np.arange(F_LEN)
    d = np.where(idx < K_SPAN, idx, idx - F_LEN)
    return np.clip(PAD - d, -REL_CLIP, REL_CLIP) + REL_CLIP


def _roll_rows(x, sign):
    rows = lax.broadcasted_iota(jnp.int32, x.shape, 0)
    step = 1
    while step < Q_BLOCK:
        shift = step if sign > 0 else F_LEN - step
        x = jnp.where((rows & step) != 0, pltpu.roll(x, shift, 1), x)
        step *= 2
    return x


def _bias_expand(frow):
    H = frow.shape[0]

    def body(f_ref, o_ref):
        x = _roll_rows(jnp.broadcast_to(f_ref[...], (Q_BLOCK, F_LEN)), 1)[:, :K_SPAN]
        qc = lax.broadcasted_iota(jnp.int32, (Q_BLOCK, K_SPAN), 0) // CHUNK * CHUNK
        kj = lax.broadcasted_iota(jnp.int32, (Q_BLOCK, K_SPAN), 1)
        o_ref[...] = jnp.where((kj >= qc) & (kj < qc + PAD + CHUNK), x, NEG_INF)

    return pl.pallas_call(
        body, name="bias_expand", grid=(H,),
        in_specs=[pl.BlockSpec((None, 1, F_LEN), lambda h: (h, 0, 0))],
        out_specs=pl.BlockSpec((None, Q_BLOCK, K_SPAN), lambda h: (h, 0, 0)),
        out_shape=jax.ShapeDtypeStruct((H, Q_BLOCK, K_SPAN), F32), compiler_params=_params(("arbitrary",)))(frow)


def _bias_reduce(dbias, n_rel):
    H = dbias.shape[0]
    onehot = jnp.asarray((_bias_index()[:, None] == np.arange(n_rel)[None, :]).astype(np.float32))

    def body(d_ref, oh_ref, o_ref):
        x = jnp.concatenate([d_ref[...], jnp.zeros((Q_BLOCK, F_LEN - K_SPAN), F32)], axis=1)
        row = jnp.sum(_roll_rows(x, -1), axis=0, keepdims=True)
        row8 = jnp.broadcast_to(row, (8, F_LEN))
        o_ref[...] = jnp.dot(row8, oh_ref[...], precision=lax.Precision.HIGHEST, preferred_element_type=F32)[0:1]

    return pl.pallas_call(
        body, name="bias_reduce", grid=(H,),
        in_specs=[pl.BlockSpec((None, Q_BLOCK, K_SPAN), lambda h: (h, 0, 0)),
                  pl.BlockSpec((F_LEN, n_rel), lambda h: (0, 0))],
        out_specs=pl.BlockSpec((None, 1, n_rel), lambda h: (h, 0, 0)),
        out_shape=jax.ShapeDtypeStruct((H, 1, n_rel), F32), compiler_params=_params(("arbitrary",)))(dbias, onehot)


def _attn_specs(S):
    hw = HEADS_PER_STEP * HEAD_DIM
    qspec = pl.BlockSpec((None, Q_BLOCK, hw), lambda g, b, i: (b, i, g))
    kspec = pl.BlockSpec((None, None, S, hw), lambda g, b, i: (0, b, 0, g))
    vspec = pl.BlockSpec((None, None, S, hw), lambda g, b, i: (1, b, 0, g))
    bspec = pl.BlockSpec((HEADS_PER_STEP, Q_BLOCK, K_SPAN), lambda g, b, i: (g, 0, 0))
    return hw, qspec, kspec, vspec, bspec


def _load_padded(k_ref, v_ref, kp, vp):
    kp[:PAD, :] = jnp.zeros((PAD, kp.shape[1]), BF16)
    vp[:PAD, :] = jnp.zeros((PAD, vp.shape[1]), BF16)
    kp[PAD:, :] = k_ref[...]
    vp[PAD:, :] = v_ref[...]


def _attn_exp(q_ref, kp, b_ref, h, q0, before):
    hs = slice(h * HEAD_DIM, (h + 1) * HEAD_DIM)
    kh = kp[pl.ds(q0, K_SPAN), hs]
    s = lax.dot_general(q_ref[:, hs], kh, (((1,), (1,)), ((), ())), preferred_element_type=F32) + b_ref[h] + before
    p = jnp.exp(s - jnp.max(s, axis=-1, keepdims=True))
    return p, 1.0 / jnp.sum(p, axis=-1, keepdims=True), kh


def _before_start(q0):
    kj = lax.broadcasted_iota(jnp.int32, (1, K_SPAN), 1)
    return jnp.where(q0 + kj >= PAD, 0.0, NEG_INF)


def _attn_fwd(q, kv, bias, B, S):
    HD = q.shape[-1]
    hw, qspec, kspec, vspec, bspec = _attn_specs(S)

    def body(q_ref, k_ref, v_ref, b_ref, o_ref, kp, vp):
        i = pl.program_id(2)

        @pl.when(i == 0)
        def _():
            _load_padded(k_ref, v_ref, kp, vp)

        q0 = pl.multiple_of(i * Q_BLOCK, Q_BLOCK)
        before = _before_start(q0)
        outs = []
        for h in range(HEADS_PER_STEP):
            hs = slice(h * HEAD_DIM, (h + 1) * HEAD_DIM)
            p, inv, _ = _attn_exp(q_ref, kp, b_ref, h, q0, before)
            outs.append(jnp.dot(p.astype(BF16), vp[pl.ds(q0, K_SPAN), hs], preferred_element_type=F32) * inv)
        o_ref[...] = jnp.concatenate(outs, axis=1).astype(BF16)

    return pl.pallas_call(
        body, name="attn_fwd", grid=(HD // hw, B, S // Q_BLOCK), in_specs=[qspec, kspec, vspec, bspec],
        out_specs=qspec, out_shape=jax.ShapeDtypeStruct((B, S, HD), BF16),
        scratch_shapes=[pltpu.VMEM((S + PAD, hw), BF16), pltpu.VMEM((S + PAD, hw), BF16)],
        compiler_params=_params(("arbitrary", "arbitrary", "arbitrary")))(q, kv, kv, bias)


def _attn_bwd(q, kv, bias, do, B, S):
    HD = q.shape[-1]
    H = HD // HEAD_DIM
    hw, qspec, kspec, vspec, bspec = _attn_specs(S)
    scale = HEAD_DIM ** -0.5
    nq = S // Q_BLOCK

    def body(q_ref, k_ref, v_ref, b_ref, do_ref, dq_ref, dkv_ref, db_ref, kp, vp, dk_acc, dv_acc):
        b, i = pl.program_id(1), pl.program_id(2)
        q0 = pl.multiple_of(i * Q_BLOCK, Q_BLOCK)

        @pl.when(i == 0)
        def _():
            _load_padded(k_ref, v_ref, kp, vp)
            dk_acc[...] = jnp.zeros_like(dk_acc)
            dv_acc[...] = jnp.zeros_like(dv_acc)

        @pl.when((i == 0) & (b == 0))
        def _():
            db_ref[...] = jnp.zeros_like(db_ref)

        before = _before_start(q0)
        for h in range(HEADS_PER_STEP):
            hs = slice(h * HEAD_DIM, (h + 1) * HEAD_DIM)
            p, inv, kh = _attn_exp(q_ref, kp, b_ref, h, q0, before)
            p = p * inv
            doh = do_ref[:, hs]
            dp = lax.dot_general(doh, vp[pl.ds(q0, K_SPAN), hs], (((1,), (1,)), ((), ())),
                                 preferred_element_type=F32)
            ds = p * (dp - jnp.sum(p * dp, axis=-1, keepdims=True))
            db_ref[h] += ds
            dsb = ds.astype(BF16)
            dq_ref[:, hs] = (jnp.dot(dsb, kh, preferred_element_type=F32) * scale).astype(BF16)
            dk_acc[pl.ds(q0, K_SPAN), hs] += lax.dot_general(dsb, q_ref[:, hs], (((0,), (0,)), ((), ())),
                                                              preferred_element_type=F32)
            dv_acc[pl.ds(q0, K_SPAN), hs] += lax.dot_general(p.astype(BF16), doh, (((0,), (0,)), ((), ())),
                                                              preferred_element_type=F32)

        @pl.when(i == nq - 1)
        def _():
            dkv_ref[0] = dk_acc[PAD:, :].astype(BF16)
            dkv_ref[1] = dv_acc[PAD:, :].astype(BF16)

    return pl.pallas_call(
        body, name="attn_bwd", grid=(HD // hw, B, nq), in_specs=[qspec, kspec, vspec, bspec, qspec],
        out_specs=[qspec, pl.BlockSpec((2, None, S, hw), lambda g, b, i: (0, b, 0, g)), bspec],
        out_shape=[jax.ShapeDtypeStruct((B, S, HD), BF16), jax.ShapeDtypeStruct((2, B, S, HD), BF16),
                   jax.ShapeDtypeStruct((H, Q_BLOCK, K_SPAN), F32)],
        scratch_shapes=[pltpu.VMEM((S + PAD, hw), BF16), pltpu.VMEM((S + PAD, hw), BF16),
                        pltpu.VMEM((S + PAD, hw), F32), pltpu.VMEM((S + PAD, hw), F32)],
        compiler_params=_params(("arbitrary", "arbitrary", "arbitrary")))(q, kv, kv, bias, do)


def _loss_head(h, g, target, *, tm=512):
    T, D = h.shape
    tm = _row_tile(T, tm)

    def body(h_ref, g_ref, t_ref, dh_ref, loss_ref, dg_ref):
        @pl.when(pl.program_id(0) == 0)
        def _():
            loss_ref[...] = jnp.zeros_like(loss_ref)
            dg_ref[...] = jnp.zeros_like(dg_ref)

        n, r = _rms(h_ref[...])
        g = g_ref[...]
        e = n * g - t_ref[...]
        loss_ref[...] += 0.5 * jnp.sum(jnp.mean(e * e, axis=-1, keepdims=True), axis=0, keepdims=True)
        dy = e * (1.0 / D)
        dg_ref[...] += jnp.sum(dy * n, axis=0, keepdims=True)
        t = dy * g
        dh_ref[...] = r * (t - n * jnp.mean(t * n, axis=-1, keepdims=True))

    row = pl.BlockSpec((tm, D), lambda i: (i, 0))
    return pl.pallas_call(
        body, name="loss_head", grid=(T // tm,), in_specs=[row, pl.BlockSpec((1, D), lambda i: (0, 0)), row],
        out_specs=[row, pl.BlockSpec((8, 128), lambda i: (0, 0)), pl.BlockSpec((1, D), lambda i: (0, 0))],
        out_shape=[jax.ShapeDtypeStruct((T, D), F32), jax.ShapeDtypeStruct((8, 128), F32),
                   jax.ShapeDtypeStruct((1, D), F32)],
        compiler_params=_params(("arbitrary",)))(h, g.reshape(1, D), target)


def _sub_rows(R):
    for cand in (256, 352, 128, 64, 8):
        if R % cand == 0 and R > cand:
            return cand
    return R


def _adamw(w, g, m, v, *, name):
    R, C = w.shape
    tr = _sub_rows(R)

    def body(w_ref, g_ref, m_ref, v_ref, d_ref, nm_ref, nv_ref):
        g = g_ref[...]
        m = ADAM_B1 * m_ref[...] + (1.0 - ADAM_B1) * g
        v = ADAM_B2 * v_ref[...] + (1.0 - ADAM_B2) * (g * g)
        m_hat = m / (1.0 - ADAM_B1 ** ADAM_STEP)
        v_hat = v / (1.0 - ADAM_B2 ** ADAM_STEP)
        d_ref[...] = -ADAM_LR * (m_hat / (jnp.sqrt(v_hat) + ADAM_EPS) + ADAM_WD * w_ref[...])
        nm_ref[...] = m
        nv_ref[...] = v

    spec = pl.BlockSpec((tr, C), lambda i: (i, 0))
    return pl.pallas_call(body, name=name, grid=(R // tr,), in_specs=[spec] * 4, out_specs=[spec] * 3,
                          out_shape=[jax.ShapeDtypeStruct((R, C), F32)] * 3,
                          compiler_params=_params(("arbitrary",)))(w, g, m, v)


def _add_pair(units, got, core, *, name):
    n4, R, C = got.shape
    rows = n4 * R
    tr = 512 if rows % 512 == 0 else R

    def body(c_ref, u_ref, got_ref, o_ref):
        o_ref[...] = (u_ref[...].astype(F32) + got_ref[...].astype(F32)).astype(BF16)

    spec = pl.BlockSpec((tr, C), lambda i, c: (i, 0))
    grid_spec = pltpu.PrefetchScalarGridSpec(
        num_scalar_prefetch=1, grid=(rows // tr,),
        in_specs=[pl.BlockSpec((None, tr, C), lambda i, c: (c[0], i, 0)), spec], out_specs=spec)
    out = pl.pallas_call(body, name=name, grid_spec=grid_spec, out_shape=jax.ShapeDtypeStruct((rows, C), BF16),
                         compiler_params=_params(("arbitrary",)))(core.reshape(1), units.reshape(2, rows, C),
                                                                   got.reshape(rows, C))
    return out.reshape(n4, R, C)


def _sum_chips(w, own, got, pos, *, name, layer=0, into=None):
    _, R, C = own.shape
    tr = _sub_rows(R)
    nr = R // tr

    def body(p_ref, own_ref, got_ref, *rest):
        o_ref = rest[-1]
        o_ref[...] = (own_ref[...].astype(F32) + got_ref[0].astype(F32) + got_ref[1].astype(F32)
                      + got_ref[2].astype(F32))

    if w.row_sharded:
        out_map = lambda i, p: (layer, i, p[1])
    else:
        out_map = lambda i, p: (layer, p[1] * nr + i, 0)
    ins = [pos, own, got]
    in_specs = [pl.BlockSpec((None, tr, C), lambda i, p: (p[0], i, 0)),
                pl.BlockSpec((3, tr, C), lambda i, p: (0, i, 0))]
    alias = {}
    if into is not None:
        ins.append(into)
        in_specs.append(ANY)
        alias = {3: 0}
    grid_spec = pltpu.PrefetchScalarGridSpec(num_scalar_prefetch=1, grid=(nr,), in_specs=in_specs,
                                             out_specs=pl.BlockSpec((None, tr, C), out_map))
    return pl.pallas_call(body, name=name, grid_spec=grid_spec, input_output_aliases=alias,
                          out_shape=jax.ShapeDtypeStruct((w.L, w.ks, w.ns), F32),
                          compiler_params=_params(("arbitrary",)))(*ins)


def _mesh_pos():
    return lax.axis_index("x"), lax.axis_index("y"), lax.axis_index("c")


def _other_chips(x, y):
    return [(1 - x, y), (x, 1 - y), (1 - x, 1 - y)]


ANY = pl.BlockSpec(memory_space=pl.ANY)


class _W:
    def __init__(self, name, shard, row_sharded):
        self.name = name
        self.L, ks, ns = shard.shape
        self.row_sharded = row_sharded
        self.K, self.N = (ks * N_CHIPS, ns) if row_sharded else (ks, ns * N_CHIPS)
        self.ks, self.ns = ks, ns

    def shard_of(self, full, j):
        if self.row_sharded:
            return full.at[:, pl.ds(j * self.ks, self.ks), :]
        return full.at[:, :, pl.ds(j * self.ns, self.ns)]

    def half_of(self, shard, c):
        if self.row_sharded:
            return shard.at[:, :, pl.ds(c * (self.ns // 2), self.ns // 2)]
        return shard.at[:, pl.ds(c * (self.ks // 2), self.ks // 2), :]


HBM = pl.BlockSpec(memory_space=pltpu.HBM)
SEM = pl.BlockSpec(memory_space=pltpu.SEMAPHORE)
IN_FLIGHT = pltpu.SideEffectType.DATAFLOW_SIDE_EFFECTING


def _in_hbm(a):
    return pltpu.with_memory_space_constraint(a, pltpu.HBM)


def _gather_start(ws, shards, after):
    nw = len(ws)

    def body(*refs):
        src, dst = refs[:nw], refs[nw:2 * nw]
        send, recv = refs[2 * nw + 1:3 * nw + 1], refs[3 * nw + 1:4 * nw + 1]
        x, y, c = _mesh_pos()
        me = 2 * x + y
        for i, w in enumerate(ws):
            for f, (px, py) in enumerate(_other_chips(x, y)):
                pltpu.make_async_remote_copy(src_ref=w.half_of(src[i], c), dst_ref=w.half_of(w.shard_of(dst[i], me), c),
                                             send_sem=send[i].at[f], recv_sem=recv[i].at[f], device_id=(px, py, c),
                                             device_id_type=MESH).start()

    fulls = [lax.empty((w.L, w.K, w.N), BF16) for w in ws]
    out = pl.pallas_call(
        body, name="gather_start", in_specs=[HBM] * (2 * nw) + [ANY],
        out_specs=[SEM] * (2 * nw) + [HBM] * (2 * nw),
        out_shape=[pltpu.SemaphoreType.DMA((3,))] * (2 * nw)
        + [pltpu.HBM(s.shape, BF16) for s in shards] + [pltpu.HBM(f.shape, BF16) for f in fulls],
        input_output_aliases={i: 2 * nw + i for i in range(2 * nw)},
        compiler_params=pltpu.CompilerParams(has_side_effects=IN_FLIGHT))(
            *[_in_hbm(s) for s in shards], *[_in_hbm(f) for f in fulls], after)
    return [(out[i], out[nw + i], out[2 * nw + i], out[3 * nw + i]) for i in range(nw)]


def _gather_wait(ws, flight, after, *, name):
    nw = len(ws)

    def body(*refs):
        src, dst = refs[:nw], refs[nw:2 * nw]
        send, recv = refs[2 * nw:3 * nw], refs[3 * nw:4 * nw]
        x, y, c = _mesh_pos()
        for i, w in enumerate(ws):
            for f, (px, py) in enumerate(_other_chips(x, y)):
                landed = w.half_of(w.shard_of(dst[i], 2 * px + py), c)
                cp = pltpu.make_async_remote_copy(src_ref=w.half_of(src[i], c), dst_ref=landed, send_sem=send[i].at[f],
                                                  recv_sem=recv[i].at[f], device_id=(px, py, c), device_id_type=MESH)
                cp.wait_send()
                cp.wait_recv()

    shards, fulls = [fl[2] for fl in flight], [fl[3] for fl in flight]
    out = pl.pallas_call(
        body, name=name, in_specs=[HBM] * (2 * nw) + [SEM] * (2 * nw) + [ANY],
        out_specs=[HBM] * (2 * nw),
        out_shape=[pltpu.HBM(s.shape, BF16) for s in shards] + [pltpu.HBM(f.shape, BF16) for f in fulls],
        input_output_aliases={i: i for i in range(2 * nw)},
        compiler_params=pltpu.CompilerParams(has_side_effects=IN_FLIGHT))(
            *shards, *fulls, *[fl[0] for fl in flight], *[fl[1] for fl in flight], after)
    return out[:nw], out[nw:]


def _gather_finish(ws, shards, fulls, *, name):
    nw = len(ws)

    def body(*refs):
        src, dst, stage = refs[:nw], refs[3 * nw:4 * nw], refs[4 * nw:5 * nw]
        send_sems, recv_sems, load_sems, store_sems = refs[5 * nw:]
        x, y, c = _mesh_pos()
        me = 2 * x + y
        sibling = (x, y, 1 - c)
        chips = _other_chips(x, y)

        def fwd(i, w, f, half):
            px, py = chips[f]
            landed = w.half_of(w.shard_of(dst[i], 2 * px + py), half)
            return pltpu.make_async_remote_copy(src_ref=landed, dst_ref=landed, send_sem=send_sems.at[3 * i + f],
                                                recv_sem=recv_sems.at[3 * i + f], device_id=sibling,
                                                device_id_type=MESH)

        loads = [pltpu.make_async_copy(src[i], stage[i], load_sems.at[i]) for i in range(nw)]
        for cp in loads:
            cp.start()
        sends = [fwd(i, w, f, c) for i, w in enumerate(ws) for f in range(3)]
        for cp in sends:
            cp.start()
        stores = [pltpu.make_async_copy(stage[i], w.shard_of(dst[i], me), store_sems.at[i])
                  for i, w in enumerate(ws)]
        for ld, st in zip(loads, stores):
            ld.wait()
            st.start()
        for i, w in enumerate(ws):
            for f in range(3):
                fwd(i, w, f, 1 - c).wait_recv()
        for cp in sends:
            cp.wait_send()
        for cp in stores:
            cp.wait()

    out = pl.pallas_call(
        body, name=name, in_specs=[ANY] * (2 * nw), out_specs=[ANY] * (2 * nw),
        out_shape=[jax.ShapeDtypeStruct(s.shape, BF16) for s in shards]
        + [jax.ShapeDtypeStruct(f.shape, BF16) for f in fulls],
        input_output_aliases={i: i for i in range(2 * nw)},
        scratch_shapes=[pltpu.VMEM((w.L, w.ks, w.ns), BF16) for w in ws]
        + [pltpu.SemaphoreType.DMA((3 * nw,)), pltpu.SemaphoreType.DMA((3 * nw,)), pltpu.SemaphoreType.DMA((nw,)),
           pltpu.SemaphoreType.DMA((nw,))],
        compiler_params=_params(has_side_effects=True))(*shards, *fulls)
    return out[nw:]


def _swap_units(units, *, name):
    nw = len(units)

    def body(*refs):
        src, got = refs[:nw], refs[nw:2 * nw]
        send_sems, recv_sems = refs[2 * nw:]
        x, y, c = _mesh_pos()
        copies = [pltpu.make_async_remote_copy(src_ref=src[i].at[1 - c], dst_ref=got[i], send_sem=send_sems.at[i],
                                               recv_sem=recv_sems.at[i], device_id=(x, y, 1 - c),
                                               device_id_type=MESH) for i in range(nw)]
        for cp in copies:
            cp.start()
        for cp in copies:
            cp.wait()

    return pl.pallas_call(
        body, name=name, in_specs=[ANY] * nw, out_specs=[ANY] * nw,
        out_shape=[jax.ShapeDtypeStruct(u.shape[1:], BF16) for u in units],
        scratch_shapes=[pltpu.SemaphoreType.DMA((nw,)), pltpu.SemaphoreType.DMA((nw,))],
        compiler_params=_params(has_side_effects=True))(*units)


def _scatter_copy(src, got, send, recv, f, chip, c):
    px, py = chip
    return pltpu.make_async_remote_copy(src_ref=src.at[2 * px + py], dst_ref=got.at[f], send_sem=send.at[f],
                                        recv_sem=recv.at[f], device_id=(px, py, c), device_id_type=MESH)


def _scatter_start(sums, *, name):
    nw = len(sums)

    def body(*refs):
        src, got = refs[:nw], refs[nw:2 * nw]
        send, recv = refs[2 * nw:3 * nw], refs[3 * nw:4 * nw]
        x, y, c = _mesh_pos()
        for i in range(nw):
            for f, chip in enumerate(_other_chips(x, y)):
                _scatter_copy(src[i], got[i], send[i], recv[i], f, chip, c).start()

    lands = [lax.empty((3,) + s.shape[1:], BF16) for s in sums]
    out = pl.pallas_call(
        body, name=name, in_specs=[HBM] * (2 * nw), out_specs=[SEM] * (2 * nw) + [HBM] * (2 * nw),
        out_shape=[pltpu.SemaphoreType.DMA((3,))] * (2 * nw)
        + [pltpu.HBM(s.shape, BF16) for s in sums] + [pltpu.HBM(l.shape, BF16) for l in lands],
        input_output_aliases={i: 2 * nw + i for i in range(2 * nw)},
        compiler_params=pltpu.CompilerParams(has_side_effects=IN_FLIGHT))(
            *[_in_hbm(s) for s in sums], *[_in_hbm(l) for l in lands])
    return [(out[i], out[nw + i], out[2 * nw + i], out[3 * nw + i]) for i in range(nw)]


def _scatter_wait(flight, after):
    nw = len(flight)

    def body(*refs):
        src, got = refs[:nw], refs[nw:2 * nw]
        send, recv = refs[2 * nw:3 * nw], refs[3 * nw:4 * nw]
        x, y, c = _mesh_pos()
        for i in range(nw):
            for f, chip in enumerate(_other_chips(x, y)):
                cp = _scatter_copy(src[i], got[i], send[i], recv[i], f, chip, c)
                cp.wait_send()
                cp.wait_recv()

    sums, lands = [fl[2] for fl in flight], [fl[3] for fl in flight]
    out = pl.pallas_call(
        body, name="scatter_wait", in_specs=[HBM] * (2 * nw) + [SEM] * (2 * nw) + [ANY], out_specs=[HBM] * (2 * nw),
        out_shape=[pltpu.HBM(s.shape, BF16) for s in sums] + [pltpu.HBM(l.shape, BF16) for l in lands],
        input_output_aliases={i: i for i in range(2 * nw)},
        compiler_params=pltpu.CompilerParams(has_side_effects=IN_FLIGHT))(
            *sums, *lands, *[fl[0] for fl in flight], *[fl[1] for fl in flight], after)
    return out[:nw], out[nw:]


def _join_halves(ws, shards):
    nw = len(ws)

    def body(*refs):
        buf = refs[nw:2 * nw]
        send_sems, recv_sems = refs[2 * nw:]
        x, y, c = _mesh_pos()
        sibling = (x, y, 1 - c)

        def copy(i, w, half):
            region = w.half_of(buf[i], half)
            return pltpu.make_async_remote_copy(src_ref=region, dst_ref=region, send_sem=send_sems.at[i],
                                                recv_sem=recv_sems.at[i], device_id=sibling, device_id_type=MESH)

        sends = [copy(i, w, c) for i, w in enumerate(ws)]
        for cp in sends:
            cp.start()
        for i, w in enumerate(ws):
            copy(i, w, 1 - c).wait_recv()
        for cp in sends:
            cp.wait_send()

    return pl.pallas_call(
        body, name="join_halves", in_specs=[ANY] * nw, out_specs=[ANY] * nw,
        out_shape=[jax.ShapeDtypeStruct((w.L, w.ks, w.ns), F32) for w in ws],
        input_output_aliases={i: i for i in range(nw)},
        scratch_shapes=[pltpu.SemaphoreType.DMA((nw,)), pltpu.SemaphoreType.DMA((nw,))],
        compiler_params=_params(has_side_effects=True))(*shards)


def _allreduce_small(vec):
    R = vec.shape[0]

    def body(x_ref, o_ref, buf, send_sems, recv_sems):
        x, y, c = _mesh_pos()
        me, sibling = (x, y, c), (x, y, 1 - c)
        chips = _other_chips(x, y)

        def slot(px, py, pc):
            return buf.at[4 * px + 2 * py + pc]

        def copy(k, block, to, src=None):
            return pltpu.make_async_remote_copy(src_ref=slot(*block) if src is None else src, dst_ref=slot(*block),
                                                send_sem=send_sems.at[k], recv_sem=recv_sems.at[k], device_id=to,
                                                device_id_type=MESH)

        first = [copy(0, me, sibling, src=x_ref)] + [copy(1 + f, me, (*chip, c), src=x_ref)
                                                     for f, chip in enumerate(chips)]
        for cp in first:
            cp.start()
        passed = [copy(4 + f, (*chip, c), sibling) for f, chip in enumerate(chips)]
        for f, chip in enumerate(chips):
            copy(1 + f, (*chip, c), me).wait_recv()
            passed[f].start()
        copy(0, sibling, me).wait_recv()
        for f, chip in enumerate(chips):
            copy(4 + f, (*chip, 1 - c), me).wait_recv()
        for cp in first + passed:
            cp.wait_send()
        slot(*me)[...] = x_ref[...]
        acc = buf[0]
        for d in range(1, 8):
            acc = acc + buf[d]
        o_ref[...] = acc

    return pl.pallas_call(
        body, name="allreduce_small", in_specs=[pl.BlockSpec(memory_space=pltpu.VMEM)],
        out_specs=pl.BlockSpec(memory_space=pltpu.VMEM), out_shape=jax.ShapeDtypeStruct((R, 128), F32),
        scratch_shapes=[pltpu.VMEM((8, R, 128), F32), pltpu.SemaphoreType.DMA((7,)), pltpu.SemaphoreType.DMA((7,))],
        compiler_params=_params())(vec)


def _pack(parts):
    flat = jnp.concatenate([p.reshape(-1).astype(F32) for p in parts])
    n = flat.shape[0]
    pad = (-n) % (64 * 128)
    return jnp.pad(flat, (0, pad)).reshape(-1, 128)


def _unpack(vec, shapes):
    flat = vec.reshape(-1)
    out, off = [], 0
    for s in shapes:
        n = int(np.prod(s))
        out.append(flat[off:off + n].reshape(s))
        off += n
    return out


def kernel(x, a_norm_g, a_w_in, a_v_norm_g, a_w_s, a_b_s, a_w_out, kv_norm_g, w_kv, b_norm_g, b_w_q, b_rel_bias, b_w_o, f_norm_g, f_w_in, f_conv_w, f_conv_b, f_w_down, final_norm_g, loss_target, m_a_norm_g, m_a_w_in, m_a_v_norm_g, m_a_w_s, m_a_b_s, m_a_w_out, m_kv_norm_g, m_w_kv, m_b_norm_g, m_b_w_q, m_b_rel_bias, m_b_w_o, m_f_norm_g, m_f_w_in, m_f_conv_w, m_f_conv_b, m_f_w_down, m_final_norm_g, v_a_norm_g, v_a_w_in, v_a_v_norm_g, v_a_w_s, v_a_b_s, v_a_w_out, v_kv_norm_g, v_w_kv, v_b_norm_g, v_b_w_q, v_b_rel_bias, v_b_w_o, v_f_norm_g, v_f_w_in, v_f_conv_w, v_f_conv_b, v_f_w_down, v_final_norm_g):
    B, S, D = x.shape
    T = B * S
    xi, yi, ci = lax.axis_index("x"), lax.axis_index("y"), lax.axis_index("c")
    j_me = (2 * xi + yi).astype(jnp.int32)
    core = ci.astype(jnp.int32)
    pos = jnp.stack([j_me, core])

    w_shards = {"a_w_in": (a_w_in, False), "a_w_out": (a_w_out, True), "w_kv": (w_kv[None], False),
                "b_w_q": (b_w_q, True), "b_w_o": (b_w_o, True), "f_w_in": (f_w_in, False), "f_w_down": (f_w_down, True)}
    names = list(w_shards)
    ws = [_W(n, w_shards[n][0], w_shards[n][1]) for n in names]
    g_shards = {"a_w_in": (a_w_in, False), "a_w_out": (a_w_out, True),
                "f_w_in0": (f_w_in[0:1], False), "f_w_down0": (f_w_down[0:1], True),
                "w_kv": (w_kv[None], False), "b_w_q": (b_w_q, True), "b_w_o": (b_w_o, True),
                "f_w_in1": (f_w_in[1:2], False), "f_w_down1": (f_w_down[1:2], True)}
    g_names = list(g_shards)
    g_ws = {n: _W(n, *g_shards[n]) for n in g_names}

    Wd = a_w_in.shape[1]
    GW = a_v_norm_g.shape[1] * N_CHIPS
    F2 = f_conv_w.shape[2] * N_CHIPS
    Fh = F2 // 2
    nsd, nsg, nsf = a_norm_g.shape[1], a_v_norm_g.shape[1], f_conv_w.shape[2]
    own = (ci == 0).astype(F32)
    place = lambda sh, width, n: lax.dynamic_update_slice_in_dim(
        jnp.zeros(sh.shape[:-1] + (width,), F32), sh * own, j_me * n, axis=sh.ndim - 1)
    gathered = _allreduce_small(_pack([place(a_norm_g, Wd, nsd), place(a_v_norm_g, GW, nsg),
                                       place(f_conv_w, F2, nsf)]))
    a_g, a_vg, conv_w = _unpack(gathered, [(1, Wd), (1, GW), (2, 3, F2)])

    flight = dict(zip(g_names, _gather_start([g_ws[n] for n in g_names],
                                             [g_shards[n][0].astype(BF16) for n in g_names], gathered)))
    full = {}

    def arrive(group, after, tag):
        gw = [g_ws[n] for n in group]
        sh, fu = _gather_wait(gw, [flight[n] for n in group], after, name=f"gather_wait_{tag}")
        full.update(zip(group, _gather_finish(gw, sh, fu, name=f"gather_finish_{tag}")))
    conv_w2 = conv_w.reshape(2, 3, 2, Fh).transpose(0, 2, 1, 3)
    conv_b2 = f_conv_b.reshape(2, 2, Fh)

    h0 = x.reshape(T, D)
    target = loss_target.reshape(T, D)
    bs_tile = jnp.repeat(a_b_s[0].T, GROUP_DIM, axis=1)
    ws_a = a_w_s[0]
    scale = HEAD_DIM ** -0.5
    HD = b_w_q.shape[2]
    H = HD // HEAD_DIM
    n_rel = b_rel_bias.shape[-1]
    frow = b_rel_bias[0][:, _bias_index()].reshape(H, 1, F_LEN)
    bias = _bias_expand(frow)

    def ffn_fwd(h, l):
        a, n = _mm(h, full[f"f_w_in{l}"], layer=0, norm_g=f_norm_g[l], split_out=True, out_dtype=BF16,
                   emit_norm=True, name=f"ffn{l}_in")
        yff = _conv_fwd(a, conv_w2[l], conv_b2[l], S)
        return _mm(yff, full[f"f_w_down{l}"], layer=0, res=h, name=f"ffn{l}_down"), (a, n, yff)

    arrive(["a_w_in", "a_w_out"], h0, "a")
    zp, n_a = _mm(h0, full["a_w_in"], layer=0, norm_g=a_g[0], emit_norm=True, name="a_in")
    out_a = _gate_fwd(zp, a_vg, ws_a, bs_tile)
    h1 = _mm(out_a, full["a_w_out"], layer=0, res=h0, name="a_out")
    arrive(["f_w_in0", "f_w_down0"], h1, "f0")
    h2, saved0 = ffn_fwd(h1, 0)
    arrive(["w_kv", "b_w_q", "b_w_o"], h2, "b")
    arrive(["f_w_in1", "f_w_down1"], h2, "f1")
    kv, n_kv = _mm(h2, full["w_kv"], layer=0, norm_g=kv_norm_g, out_dtype=BF16, split_out=True, emit_norm=True,
                   name="kv")
    q, n_q = _mm(h2, full["b_w_q"], layer=0, norm_g=b_norm_g[0], scale=scale, out_dtype=BF16, emit_norm=True,
                 name="q")
    kv4, q3 = kv.reshape(2, B, S, HD), q.reshape(B, S, HD)
    o = _attn_fwd(q3, kv4, bias, B, S).reshape(T, HD)
    h3 = _mm(o, full["b_w_o"], layer=0, res=h2, name="attn_out")
    h4, saved1 = ffn_fwd(h3, 1)
    dh, loss8, dg_final = _loss_head(h4, final_norm_g, target)

    units = {}

    def ffn_bwd(dh, h, saved, l):
        a, n, yff = saved
        dyff = _mm(dh, full[f"f_w_down{l}"], layer=0, trans_w=True, out_dtype=BF16, name=f"ffn{l}_down_dx")
        units[f"f_w_down{l}"] = _mm_tn(yff, dh, rows_are_shards=True, name=f"ffn{l}_down_dw")
        da, dcw, dcb = _conv_bwd(a, dyff, conv_w2[l], conv_b2[l], S)
        units[f"f_w_in{l}"] = _mm_tn(n, da, split_y=True, name=f"ffn{l}_in_dw")
        dh, dg = _mm(da, full[f"f_w_in{l}"], layer=0, trans_w=True, split_x=True, bwd=(h, f_norm_g[l], dh), tm=256,
                     name=f"ffn{l}_in_dx")
        return dh, dg, dcw, dcb

    in_flight = {}

    def reduce_start(group, tag):
        got = _swap_units([units[n] for n in group], name=f"swap_{tag}")
        sums = [_add_pair(units[n], g_, core, name=f"pair_{n}") for n, g_ in zip(group, got)]
        in_flight.update(zip(group, _scatter_start(sums, name=f"scatter_start_{tag}")))

    dh, dg_f1, dcw1, dcb1 = ffn_bwd(dh, h3, saved1, 1)
    reduce_start(["f_w_down1", "f_w_in1"], "f1")
    do = _mm(dh, full["b_w_o"], layer=0, trans_w=True, out_dtype=BF16, name="attn_out_dx")
    units["b_w_o"] = _mm_tn(o, dh, rows_are_shards=True, name="b_w_o_dw")
    dq, dkv, dbias = _attn_bwd(q3, kv4, bias, do.reshape(B, S, HD), B, S)
    d_rel = _bias_reduce(dbias, n_rel).reshape(1, H, n_rel)
    dq, dkv = dq.reshape(T, HD), dkv.reshape(2, T, HD)
    units["b_w_q"] = _mm_tn(n_q, dq, rows_are_shards=True, name="b_w_q_dw")
    dh, dg_b = _mm(dq, full["b_w_q"], layer=0, trans_w=True, bwd=(h2, b_norm_g[0], dh), name="q_dx")
    units["w_kv"] = _mm_tn(n_kv, dkv, split_y=True, name="w_kv_dw")
    dh, dg_kv = _mm(dkv, full["w_kv"], layer=0, trans_w=True, split_x=True, bwd=(h2, kv_norm_g, dh), name="kv_dx")
    reduce_start(["b_w_o", "b_w_q", "w_kv"], "b")
    dh, dg_f0, dcw0, dcb0 = ffn_bwd(dh, h1, saved0, 0)
    reduce_start(["f_w_down0", "f_w_in0"], "f0")
    d_out = _mm(dh, full["a_w_out"], layer=0, trans_w=True, out_dtype=BF16, name="a_out_dx")
    units["a_w_out"] = _mm_tn(out_a, dh, rows_are_shards=True, name="a_w_out_dw")
    dzp, dws, dbs, dgv = _gate_bwd(zp, d_out, a_vg, ws_a, bs_tile)
    units["a_w_in"] = _mm_tn(n_a, dzp, name="a_w_in_dw")
    reduce_start(["a_w_out", "a_w_in"], "a")
    grad_x, dg_a = _mm(dzp, full["a_w_in"], layer=0, trans_w=True, bwd=(h0, a_g[0], dh), name="a_in_dx")

    sums, recv = _scatter_wait([in_flight[n] for n in g_names], grad_x)
    sums, recv = dict(zip(g_names, sums)), dict(zip(g_names, recv))
    halves = []
    for n, w in zip(names, ws):
        if w.L == 1:
            halves.append(_sum_chips(w, sums[n], recv[n], pos, name=f"chips_{n}"))
        else:
            first = _sum_chips(w, sums[n + "0"], recv[n + "0"], pos, name=f"chips_{n}0")
            halves.append(_sum_chips(w, sums[n + "1"], recv[n + "1"], pos, layer=1, into=first, name=f"chips_{n}1"))
    g_big = dict(zip(names, _join_halves(ws, halves)))
    g_big["w_kv"] = g_big["w_kv"][0]

    to_flat = lambda d: d.transpose(1, 0, 2).reshape(3, F2)
    small = {"a_norm_g": dg_a, "a_v_norm_g": dgv, "a_w_s": dws[None], "a_b_s": dbs[None], "kv_norm_g": dg_kv[0],
             "b_norm_g": dg_b, "b_rel_bias": d_rel, "f_norm_g": jnp.concatenate([dg_f0, dg_f1], axis=0),
             "f_conv_w": jnp.stack([to_flat(dcw0), to_flat(dcw1)]),
             "f_conv_b": jnp.stack([dcb0.reshape(F2), dcb1.reshape(F2)]), "final_norm_g": dg_final[0]}
    snames = list(small)
    red = _allreduce_small(_pack([small[n] for n in snames] + [loss8[0:1, 0:1]]))
    parts = _unpack(red, [small[n].shape for n in snames] + [(1,)])
    g_small = dict(zip(snames, parts[:-1]))
    loss = parts[-1][0]
    g_small["a_norm_g"] = lax.dynamic_slice_in_dim(g_small["a_norm_g"], j_me * nsd, nsd, axis=1)
    g_small["a_v_norm_g"] = lax.dynamic_slice_in_dim(g_small["a_v_norm_g"], j_me * nsg, nsg, axis=1)
    g_small["f_conv_w"] = lax.dynamic_slice_in_dim(g_small["f_conv_w"], j_me * nsf, nsf, axis=2)

    given = dict(a_norm_g=(a_norm_g, m_a_norm_g, v_a_norm_g), a_w_in=(a_w_in, m_a_w_in, v_a_w_in),
                 a_v_norm_g=(a_v_norm_g, m_a_v_norm_g, v_a_v_norm_g), a_w_s=(a_w_s, m_a_w_s, v_a_w_s),
                 a_b_s=(a_b_s, m_a_b_s, v_a_b_s), a_w_out=(a_w_out, m_a_w_out, v_a_w_out),
                 kv_norm_g=(kv_norm_g, m_kv_norm_g, v_kv_norm_g), w_kv=(w_kv, m_w_kv, v_w_kv),
                 b_norm_g=(b_norm_g, m_b_norm_g, v_b_norm_g), b_w_q=(b_w_q, m_b_w_q, v_b_w_q),
                 b_rel_bias=(b_rel_bias, m_b_rel_bias, v_b_rel_bias), b_w_o=(b_w_o, m_b_w_o, v_b_w_o),
                 f_norm_g=(f_norm_g, m_f_norm_g, v_f_norm_g), f_w_in=(f_w_in, m_f_w_in, v_f_w_in),
                 f_conv_w=(f_conv_w, m_f_conv_w, v_f_conv_w), f_conv_b=(f_conv_b, m_f_conv_b, v_f_conv_b),
                 f_w_down=(f_w_down, m_f_w_down, v_f_w_down), final_norm_g=(final_norm_g, m_final_norm_g, v_final_norm_g))
    order = list(given)
    grads, deltas, new_m, new_v = {}, {}, {}, {}
    for n in names:
        w_, m_, v_ = given[n]
        g_ = g_big[n]
        C = w_.shape[-1]
        d2, m2, v2 = _adamw(w_.reshape(-1, C), g_.reshape(-1, C), m_.reshape(-1, C), v_.reshape(-1, C),
                            name=f"adamw_{n}")
        grads[n], deltas[n], new_m[n], new_v[n] = g_.reshape(w_.shape), d2.reshape(w_.shape), m2.reshape(w_.shape), \
            v2.reshape(w_.shape)
    sm = [n for n in order if n not in names]
    d2, m2, v2 = _adamw(_pack([given[n][0] for n in sm]), _pack([g_small[n].reshape(given[n][0].shape) for n in sm]),
                        _pack([given[n][1] for n in sm]), _pack([given[n][2] for n in sm]), name="adamw_small")
    shapes = [given[n][0].shape for n in sm]
    for n, d_, m_, v_ in zip(sm, _unpack(d2, shapes), _unpack(m2, shapes), _unpack(v2, shapes)):
        grads[n], deltas[n], new_m[n], new_v[n] = g_small[n].reshape(given[n][0].shape), d_, m_, v_

    return (loss, grad_x.reshape(B, S, D), *[grads[n] for n in order], *[deltas[n] for n in order],
            *[new_m[n] for n in order], *[new_v[n] for n in order])
```

```python
import math

import numpy as np
import jax
import jax.numpy as jnp
from jax import lax
from jax.experimental import pallas as pl
from jax.experimental.pallas import tpu as pltpu

F32 = jnp.float32
BF16 = jnp.bfloat16
MESH = pl.DeviceIdType.MESH

EPS = 1e-6
NEG_INF = -1e30
CHUNK = 64
GMLP_BLOCK = 128
GROUP_DIM = 128
HEAD_DIM = 64
LEFT_CHUNKS = 8
PAD = LEFT_CHUNKS * CHUNK
REL_CLIP = 128
Q_BLOCK = 256
K_SPAN = PAD + Q_BLOCK
F_LEN = K_SPAN + Q_BLOCK
HEADS_PER_STEP = 4
N_CHIPS = 4

ADAM_LR = 0.001
ADAM_B1 = 0.9
ADAM_B2 = 0.999
ADAM_EPS = 1e-08
ADAM_WD = 0.01
ADAM_STEP = 10

VMEM_LIMIT = 56 * 1024 * 1024


def _params(sem=None, **kw):
    if sem is not None:
        kw["dimension_semantics"] = sem
    return pltpu.CompilerParams(vmem_limit_bytes=VMEM_LIMIT, **kw)


def _rms(xf):
    r = lax.rsqrt(jnp.mean(xf * xf, axis=-1, keepdims=True) + EPS)
    return xf * r, r


def _gelu(x):
    c = math.sqrt(2.0 / math.pi)
    return 0.5 * x * (1.0 + jnp.tanh(c * (x + 0.044715 * x * x * x)))


def _gelu_grad(x):
    c = math.sqrt(2.0 / math.pi)
    t = jnp.tanh(c * (x + 0.044715 * x * x * x))
    return 0.5 * (1.0 + t) + 0.5 * x * (1.0 - t * t) * c * (1.0 + 3.0 * 0.044715 * x * x)


def _col_tile(n):
    if n <= 1024:
        return n
    for t in (1408, 1024, 512):
        if n % t == 0:
            return t
    raise ValueError(n)


def _row_tile(t, want):
    while t % want:
        want //= 2
    return want


def _mm(x, w, *, name, layer=None, trans_w=False, norm_g=None, res=None, scale=None, out_dtype=F32, bwd=None,
        split_out=False, split_x=False, emit_norm=False, tm=512):
    T = x.shape[-2]
    K = 2 * x.shape[-1] if split_x else x.shape[-1]
    N = w.shape[-2] if trans_w else w.shape[-1]
    tn = N
    tm = _row_tile(T, 256 if N > 4096 else tm)
    nn, nm = N // tn, T // tm
    has_norm, has_res, has_bwd = norm_g is not None, res is not None, bwd is not None
    dims = (((1,), (1,)), ((), ())) if trans_w else (((1,), (0,)), ((), ()))

    def body(*refs):
        it = iter(refs)
        x_ref, w_ref = next(it), next(it)
        g_ref = next(it) if has_norm else None
        res_ref = next(it) if has_res else None
        if has_bwd:
            h_ref, bg_ref, dh_ref = next(it), next(it), next(it)
        o_ref = next(it)
        if split_x:
            kh = K // 2
            acc = lax.dot_general(x_ref[0].astype(BF16), w_ref[:, :kh] if trans_w else w_ref[:kh, :], dims,
                                  preferred_element_type=F32)
            acc = acc + lax.dot_general(x_ref[1].astype(BF16), w_ref[:, kh:] if trans_w else w_ref[kh:, :], dims,
                                        preferred_element_type=F32)
        else:
            xv = x_ref[...]
            if has_norm:
                xv = _rms(xv.astype(F32))[0] * g_ref[...]
            xb = xv.astype(BF16)
            if emit_norm:
                refs[-1][...] = xb
            acc = lax.dot_general(xb, w_ref[...], dims, preferred_element_type=F32)
        if scale is not None:
            acc = acc * scale
        if has_res:
            acc = acc + res_ref[...]
        if has_bwd:
            dg_ref = next(it)
            n, r = _rms(h_ref[...])

            @pl.when(pl.program_id(1) == 0)
            def _():
                dg_ref[...] = jnp.zeros_like(dg_ref)

            dg_ref[...] += jnp.sum(acc * n, axis=0, keepdims=True)
            t = acc * bg_ref[...]
            o_ref[...] = dh_ref[...] + r * (t - n * jnp.mean(t * n, axis=-1, keepdims=True))
        elif split_out:
            o_ref[0] = acc[:, :N // 2].astype(out_dtype)
            o_ref[1] = acc[:, N // 2:].astype(out_dtype)
        else:
            o_ref[...] = acc.astype(out_dtype)

    lead = () if layer is None else (None,)
    lidx = () if layer is None else (layer,)
    ins = [x, w]
    xspec = (pl.BlockSpec((2, tm, K // 2), lambda n, m: (0, m, 0)) if split_x
             else pl.BlockSpec((tm, K), lambda n, m: (m, 0)))
    wspec = (pl.BlockSpec(lead + (tn, K), lambda n, m: lidx + (n, 0)) if trans_w
             else pl.BlockSpec(lead + (K, tn), lambda n, m: lidx + (0, n)))
    in_specs = [xspec, wspec]
    if has_norm:
        ins.append(norm_g.reshape(1, K))
        in_specs.append(pl.BlockSpec((1, K), lambda n, m: (0, 0)))
    if has_res:
        ins.append(res)
        in_specs.append(pl.BlockSpec((tm, tn), lambda n, m: (m, n)))
    if split_out:
        out_shape = [jax.ShapeDtypeStruct((2, T, N // 2), out_dtype)]
        out_specs = [pl.BlockSpec((2, tm, N // 2), lambda n, m: (0, m, 0))]
    else:
        out_shape = [jax.ShapeDtypeStruct((T, N), F32 if has_bwd else out_dtype)]
        out_specs = [pl.BlockSpec((tm, tn), lambda n, m: (m, n))]
    if has_bwd:
        h, g, dh = bwd
        ins += [h, g.reshape(1, N), dh]
        in_specs += [pl.BlockSpec((tm, N), lambda n, m: (m, 0)), pl.BlockSpec((1, N), lambda n, m: (0, 0)),
                     pl.BlockSpec((tm, N), lambda n, m: (m, 0))]
        out_shape.append(jax.ShapeDtypeStruct((1, N), F32))
        out_specs.append(pl.BlockSpec((1, N), lambda n, m: (0, 0)))
    if emit_norm:
        out_shape.append(jax.ShapeDtypeStruct((T, K), BF16))
        out_specs.append(pl.BlockSpec((tm, K), lambda n, m: (m, 0)))
    out = pl.pallas_call(body, name=name, grid=(nn, nm), in_specs=in_specs, out_specs=out_specs, out_shape=out_shape,
                         compiler_params=_params(("arbitrary", "arbitrary")))(*ins)
    return out if has_bwd or emit_norm else out[0]


def _mm_tn(x, dy, *, name, rows_are_shards=False, split_y=False, tt=512):
    T, K = x.shape
    N = 2 * dy.shape[-1] if split_y else dy.shape[-1]
    R, C = (K // N_CHIPS, N // 2) if rows_are_shards else (K // 2, N // N_CHIPS)
    nn = 2 if split_y else 1
    tn = N // nn
    per = N_CHIPS // nn
    assert not (rows_are_shards and split_y)
    tt = _row_tile(T, tt)
    nt = T // tt

    def body(x_ref, y_ref, o_ref, acc_ref):
        t = pl.program_id(1)

        @pl.when(t == 0)
        def _():
            acc_ref[...] = jnp.zeros_like(acc_ref)

        acc_ref[...] += lax.dot_general(x_ref[...], y_ref[...].astype(BF16), (((0,), (0,)), ((), ())),
                                        preferred_element_type=F32)

        @pl.when(t == nt - 1)
        def _():
            if rows_are_shards:
                for h in range(2):
                    o_ref[h] = acc_ref[:, h * C:(h + 1) * C].astype(BF16).reshape(N_CHIPS, R, C)
            else:
                for j in range(per):
                    o_ref[:, j] = acc_ref[:, j * C:(j + 1) * C].astype(BF16).reshape(2, R, C)

    if split_y:
        yspec = pl.BlockSpec((None, tt, tn), lambda n, t: (n, t, 0))
    else:
        yspec = pl.BlockSpec((tt, tn), lambda n, t: (t, 0))
    if rows_are_shards:
        out_spec = pl.BlockSpec((2, N_CHIPS, R, C), lambda n, t: (0, 0, 0, 0))
    else:
        out_spec = pl.BlockSpec((2, per, R, C), lambda n, t: (0, n, 0, 0))
    return pl.pallas_call(body, name=name, grid=(nn, nt),
                          in_specs=[pl.BlockSpec((tt, K), lambda n, t: (t, 0)), yspec], out_specs=out_spec,
                          out_shape=jax.ShapeDtypeStruct((2, N_CHIPS, R, C), BF16),
                          scratch_shapes=[pltpu.VMEM((K, tn), F32)],
                          compiler_params=_params(("arbitrary", "arbitrary")))(x, dy)


def _chunk_mask():
    i = lax.broadcasted_iota(jnp.int32, (GMLP_BLOCK, GMLP_BLOCK), 0) // CHUNK
    j = lax.broadcasted_iota(jnp.int32, (GMLP_BLOCK, GMLP_BLOCK), 1) // CHUNK
    return i >= j


def _gate_fwd(zp, gv, ws, bs_tile, *, tm=256):
    T, W2 = zp.shape
    W = W2 // 2
    G = W // GROUP_DIM
    tm = _row_tile(T, tm)

    def body(zp_ref, gv_ref, ws_ref, bs_ref, o_ref):
        z = _gelu(zp_ref[...])
        u, v = z[:, :W], z[:, W:]
        vn = _rms(v)[0] * gv_ref[...]
        mask = _chunk_mask()
        for g in range(G):
            cs = slice(g * GROUP_DIM, (g + 1) * GROUP_DIM)
            wg = jnp.where(mask, ws_ref[g], 0.0).astype(BF16)
            for b in range(tm // GMLP_BLOCK):
                rs = slice(b * GMLP_BLOCK, (b + 1) * GMLP_BLOCK)
                s = jnp.dot(wg, vn[rs, cs].astype(BF16), preferred_element_type=F32) + bs_ref[:, cs]
                o_ref[rs, cs] = (u[rs, cs] * s).astype(BF16)

    return pl.pallas_call(
        body, name="gate_fwd", grid=(T // tm,),
        in_specs=[pl.BlockSpec((tm, W2), lambda i: (i, 0)), pl.BlockSpec((1, W), lambda i: (0, 0)),
                  pl.BlockSpec((G, GMLP_BLOCK, GMLP_BLOCK), lambda i: (0, 0, 0)),
                  pl.BlockSpec((GMLP_BLOCK, W), lambda i: (0, 0))],
        out_specs=pl.BlockSpec((tm, W), lambda i: (i, 0)), out_shape=jax.ShapeDtypeStruct((T, W), BF16),
        compiler_params=_params(("arbitrary",)))(zp, gv, ws, bs_tile)


def _gate_bwd(zp, d_out, gv, ws, bs_tile, *, tm=256):
    T, W2 = zp.shape
    W = W2 // 2
    G = W // GROUP_DIM
    tm = _row_tile(T, tm)
    nm = T // tm

    def body(zp_ref, do_ref, gv_ref, ws_ref, bs_ref, dzp_ref, dws_ref, dbs_ref, dgv_ref, du_scr, dvn_scr, dsum_scr):
        i = pl.program_id(0)

        @pl.when(i == 0)
        def _():
            dws_ref[...] = jnp.zeros_like(dws_ref)
            dgv_ref[...] = jnp.zeros_like(dgv_ref)
            dsum_scr[...] = jnp.zeros_like(dsum_scr)

        zp = zp_ref[...]
        z = _gelu(zp)
        u, v = z[:, :W], z[:, W:]
        n, r = _rms(v)
        gv = gv_ref[...]
        vn = n * gv
        d_out = do_ref[...].astype(F32)
        mask = _chunk_mask()
        for g in range(G):
            cs = slice(g * GROUP_DIM, (g + 1) * GROUP_DIM)
            wg = jnp.where(mask, ws_ref[g], 0.0).astype(BF16)
            dw = jnp.zeros((GMLP_BLOCK, GMLP_BLOCK), F32)
            for b in range(tm // GMLP_BLOCK):
                rs = slice(b * GMLP_BLOCK, (b + 1) * GMLP_BLOCK)
                vb = vn[rs, cs].astype(BF16)
                s = jnp.dot(wg, vb, preferred_element_type=F32) + bs_ref[:, cs]
                du_scr[rs, cs] = d_out[rs, cs] * s
                ds = d_out[rs, cs] * u[rs, cs]
                dsb = ds.astype(BF16)
                dvn_scr[rs, cs] = lax.dot_general(wg, dsb, (((0,), (0,)), ((), ())), preferred_element_type=F32)
                dw = dw + lax.dot_general(dsb, vb, (((1,), (1,)), ((), ())), preferred_element_type=F32)
                dsum_scr[:, cs] += ds
            dws_ref[g] += jnp.where(mask, dw, 0.0)
        dvn = dvn_scr[...]
        dgv_ref[...] += jnp.sum(dvn * n, axis=0, keepdims=True)
        t = dvn * gv
        dv = r * (t - n * jnp.mean(t * n, axis=-1, keepdims=True))
        dzp_ref[:, :W] = (du_scr[...] * _gelu_grad(zp[:, :W])).astype(BF16)
        dzp_ref[:, W:] = (dv * _gelu_grad(zp[:, W:])).astype(BF16)

        @pl.when(i == nm - 1)
        def _():
            sel = (lax.broadcasted_iota(jnp.int32, (G, W), 1) // GROUP_DIM
                   == lax.broadcasted_iota(jnp.int32, (G, W), 0)).astype(F32)
            dbs_ref[...] = lax.dot_general(sel, dsum_scr[...], (((1,), (1,)), ((), ())),
                                           precision=lax.Precision.HIGHEST, preferred_element_type=F32)

    return pl.pallas_call(
        body, name="gate_bwd", grid=(nm,),
        in_specs=[pl.BlockSpec((tm, W2), lambda i: (i, 0)), pl.BlockSpec((tm, W), lambda i: (i, 0)),
                  pl.BlockSpec((1, W), lambda i: (0, 0)),
                  pl.BlockSpec((G, GMLP_BLOCK, GMLP_BLOCK), lambda i: (0, 0, 0)),
                  pl.BlockSpec((GMLP_BLOCK, W), lambda i: (0, 0))],
        out_specs=[pl.BlockSpec((tm, W2), lambda i: (i, 0)),
                   pl.BlockSpec((G, GMLP_BLOCK, GMLP_BLOCK), lambda i: (0, 0, 0)),
                   pl.BlockSpec((G, GMLP_BLOCK), lambda i: (0, 0)), pl.BlockSpec((1, W), lambda i: (0, 0))],
        out_shape=[jax.ShapeDtypeStruct((T, W2), BF16), jax.ShapeDtypeStruct((G, GMLP_BLOCK, GMLP_BLOCK), F32),
                   jax.ShapeDtypeStruct((G, GMLP_BLOCK), F32), jax.ShapeDtypeStruct((1, W), F32)],
        scratch_shapes=[pltpu.VMEM((tm, W), F32), pltpu.VMEM((tm, W), F32), pltpu.VMEM((GMLP_BLOCK, W), F32)],
        compiler_params=_params(("arbitrary",)))(zp, d_out, gv, ws, bs_tile)


HALO = 16


def _taps(ext, w, b):
    a, a1, a2 = ext[HALO:], pltpu.roll(ext, 1, 0)[HALO:], pltpu.roll(ext, 2, 0)[HALO:]
    return w[2:3] * a + w[1:2] * a1 + w[0:1] * a2 + b, a, a1, a2


def _conv_fwd(a, cw, cb, S, *, tm=256):
    _, T, F = a.shape
    tc = _col_tile(F)
    tm = _row_tile(S, tm)
    hb = tm // HALO

    def body(a_ref, p_ref, w_ref, b_ref, o_ref, c_ref):
        first = (pl.program_id(1) * tm) % S == 0
        keep = jnp.where(first, 0.0, 1.0)

        def conv(s):
            ext = jnp.concatenate([p_ref[s].astype(F32) * keep, a_ref[s].astype(F32)], axis=0)
            c = _taps(ext, w_ref[s], b_ref[s:s + 1, :])[0].astype(BF16)
            c_ref[s] = c
            return c.astype(F32)

        up, gate = conv(0), conv(1)
        o_ref[...] = (gate * jax.nn.sigmoid(gate) * up).astype(BF16)

    return pl.pallas_call(
        body, name="conv_fwd", grid=(F // tc, T // tm),
        in_specs=[pl.BlockSpec((2, tm, tc), lambda j, i: (0, i, j)),
                  pl.BlockSpec((2, HALO, tc), lambda j, i: (0, jnp.maximum(i * hb - 1, 0), j)),
                  pl.BlockSpec((2, 3, tc), lambda j, i: (0, 0, j)), pl.BlockSpec((2, tc), lambda j, i: (0, j))],
        out_specs=[pl.BlockSpec((tm, tc), lambda j, i: (i, j)), pl.BlockSpec((2, tm, tc), lambda j, i: (0, i, j))],
        out_shape=[jax.ShapeDtypeStruct((T, F), BF16), jax.ShapeDtypeStruct((2, T, F), BF16)],
        compiler_params=_params(("arbitrary", "arbitrary")))(a, a, cw, cb)


def _conv_bwd(a, c, dy, cw, S, *, tm=256):
    _, T, F = a.shape
    tc = _col_tile(F)
    tm = _row_tile(S, tm)
    nm = T // tm
    hb = tm // HALO
    TE = tm + HALO
    nxt = lambda j, i: jnp.minimum((i + 1) * hb, T // HALO - 1)

    def body(a_ref, c_ref, nc_ref, dy_ref, ndy_ref, w_ref, da_ref, dw_ref, db_ref):
        i = pl.program_id(1)
        last = ((i + 1) * tm) % S == 0
        keep_n = jnp.where(last, 0.0, 1.0)
        dyf = jnp.concatenate([dy_ref[...].astype(F32), ndy_ref[...].astype(F32) * keep_n], axis=0)
        up = jnp.concatenate([c_ref[0].astype(F32), nc_ref[0].astype(F32)], axis=0)
        gate = jnp.concatenate([c_ref[1].astype(F32), nc_ref[1].astype(F32)], axis=0)
        sg = jax.nn.sigmoid(gate)
        d_up = dyf * (gate * sg)
        d_gate = dyf * up * (sg * (1.0 + gate * (1.0 - sg)))

        @pl.when(i == 0)
        def _():
            dw_ref[...] = jnp.zeros_like(dw_ref)
            db_ref[...] = jnp.zeros_like(db_ref)

        def back(s, d):
            a = a_ref[s].astype(F32)
            w = w_ref[s]
            u1, u2 = pltpu.roll(d, TE - 1, 0), pltpu.roll(d, TE - 2, 0)
            db_ref[s:s + 1, :] += jnp.sum(d[:tm], axis=0, keepdims=True)
            dw_ref[s, 2:3, :] += jnp.sum(d[:tm] * a, axis=0, keepdims=True)
            dw_ref[s, 1:2, :] += jnp.sum(u1[:tm] * a, axis=0, keepdims=True)
            dw_ref[s, 0:1, :] += jnp.sum(u2[:tm] * a, axis=0, keepdims=True)
            da_ref[s] = (w[2:3] * d + w[1:2] * u1 + w[0:1] * u2)[:tm].astype(BF16)

        back(0, d_up)
        back(1, d_gate)

    cur = pl.BlockSpec((2, tm, tc), lambda j, i: (0, i, j))
    return pl.pallas_call(
        body, name="conv_bwd", grid=(F // tc, nm),
        in_specs=[cur, cur, pl.BlockSpec((2, HALO, tc), lambda j, i: (0, nxt(j, i), j)),
                  pl.BlockSpec((tm, tc), lambda j, i: (i, j)), pl.BlockSpec((HALO, tc), lambda j, i: (nxt(j, i), j)),
                  pl.BlockSpec((2, 3, tc), lambda j, i: (0, 0, j))],
        out_specs=[cur, pl.BlockSpec((2, 3, tc), lambda j, i: (0, 0, j)), pl.BlockSpec((2, tc), lambda j, i: (0, j))],
        out_shape=[jax.ShapeDtypeStruct((2, T, F), BF16), jax.ShapeDtypeStruct((2, 3, F), F32),
                   jax.ShapeDtypeStruct((2, F), F32)],
        compiler_params=_params(("arbitrary", "arbitrary")))(a, c, c, dy, dy, cw)


def _bias_index():
    idx = np.arange(F_LEN)
    d = np.where(idx < K_SPAN, idx, idx - F_LEN)
    return np.clip(PAD - d, -REL_CLIP, REL_CLIP) + REL_CLIP


def _roll_rows(x, sign):
    rows = lax.broadcasted_iota(jnp.int32, x.shape, 0)
    step = 1
    while step < Q_BLOCK:
        shift = step if sign > 0 else F_LEN - step
        x = jnp.where((rows & step) != 0, pltpu.roll(x, shift, 1), x)
        step *= 2
    return x


def _bias_expand(frow):
    H = frow.shape[0]

    def body(f_ref, o_ref):
        x = _roll_rows(jnp.broadcast_to(f_ref[...], (Q_BLOCK, F_LEN)), 1)[:, :K_SPAN]
        qc = lax.broadcasted_iota(jnp.int32, (Q_BLOCK, K_SPAN), 0) // CHUNK * CHUNK
        kj = lax.broadcasted_iota(jnp.int32, (Q_BLOCK, K_SPAN), 1)
        o_ref[...] = jnp.where((kj >= qc) & (kj < qc + PAD + CHUNK), x, NEG_INF)

    return pl.pallas_call(
        body, name="bias_expand", grid=(H,),
        in_specs=[pl.BlockSpec((None, 1, F_LEN), lambda h: (h, 0, 0))],
        out_specs=pl.BlockSpec((None, Q_BLOCK, K_SPAN), lambda h: (h, 0, 0)),
        out_shape=jax.ShapeDtypeStruct((H, Q_BLOCK, K_SPAN), F32), compiler_params=_params(("arbitrary",)))(frow)


def _bias_reduce(dbias, n_rel):
    H = dbias.shape[0]
    onehot = jnp.asarray((_bias_index()[:, None] == np.arange(n_rel)[None, :]).astype(np.float32))

    def body(d_ref, oh_ref, o_ref):
        x = jnp.concatenate([d_ref[...], jnp.zeros((Q_BLOCK, F_LEN - K_SPAN), F32)], axis=1)
        row = jnp.sum(_roll_rows(x, -1), axis=0, keepdims=True)
        row8 = jnp.broadcast_to(row, (8, F_LEN))
        o_ref[...] = jnp.dot(row8, oh_ref[...], precision=lax.Precision.HIGHEST, preferred_element_type=F32)[0:1]

    return pl.pallas_call(
        body, name="bias_reduce", grid=(H,),
        in_specs=[pl.BlockSpec((None, Q_BLOCK, K_SPAN), lambda h: (h, 0, 0)),
                  pl.BlockSpec((F_LEN, n_rel), lambda h: (0, 0))],
        out_specs=pl.BlockSpec((None, 1, n_rel), lambda h: (h, 0, 0)),
        out_shape=jax.ShapeDtypeStruct((H, 1, n_rel), F32), compiler_params=_params(("arbitrary",)))(dbias, onehot)


def _attn_specs(S):
    hw = HEADS_PER_STEP * HEAD_DIM
    qspec = pl.BlockSpec((None, Q_BLOCK, hw), lambda g, b, i: (b, i, g))
    kspec = pl.BlockSpec((None, None, S, hw), lambda g, b, i: (0, b, 0, g))
    vspec = pl.BlockSpec((None, None, S, hw), lambda g, b, i: (1, b, 0, g))
    bspec = pl.BlockSpec((HEADS_PER_STEP, Q_BLOCK, K_SPAN), lambda g, b, i: (g, 0, 0))
    return hw, qspec, kspec, vspec, bspec


def _load_padded(k_ref, v_ref, kp, vp):
    kp[:PAD, :] = jnp.zeros((PAD, kp.shape[1]), BF16)
    vp[:PAD, :] = jnp.zeros((PAD, vp.shape[1]), BF16)
    kp[PAD:, :] = k_ref[...]
    vp[PAD:, :] = v_ref[...]


def _attn_exp(q_ref, kp, b_ref, h, q0, before):
    hs = slice(h * HEAD_DIM, (h + 1) * HEAD_DIM)
    kh = kp[pl.ds(q0, K_SPAN), hs]
    s = lax.dot_general(q_ref[:, hs], kh, (((1,), (1,)), ((), ())), preferred_element_type=F32) + b_ref[h] + before
    p = jnp.exp(s - jnp.max(s, axis=-1, keepdims=True))
    return p, 1.0 / jnp.sum(p, axis=-1, keepdims=True), kh


def _before_start(q0):
    kj = lax.broadcasted_iota(jnp.int32, (1, K_SPAN), 1)
    return jnp.where(q0 + kj >= PAD, 0.0, NEG_INF)


def _attn_fwd(q, kv, bias, B, S):
    HD = q.shape[-1]
    hw, qspec, kspec, vspec, bspec = _attn_specs(S)

    def body(q_ref, k_ref, v_ref, b_ref, o_ref, kp, vp):
        i = pl.program_id(2)

        @pl.when(i == 0)
        def _():
            _load_padded(k_ref, v_ref, kp, vp)

        q0 = pl.multiple_of(i * Q_BLOCK, Q_BLOCK)
        before = _before_start(q0)
        outs = []
        for h in range(HEADS_PER_STEP):
            hs = slice(h * HEAD_DIM, (h + 1) * HEAD_DIM)
            p, inv, _ = _attn_exp(q_ref, kp, b_ref, h, q0, before)
            outs.append(jnp.dot(p.astype(BF16), vp[pl.ds(q0, K_SPAN), hs], preferred_element_type=F32) * inv)
        o_ref[...] = jnp.concatenate(outs, axis=1).astype(BF16)

    return pl.pallas_call(
        body, name="attn_fwd", grid=(HD // hw, B, S // Q_BLOCK), in_specs=[qspec, kspec, vspec, bspec],
        out_specs=qspec, out_shape=jax.ShapeDtypeStruct((B, S, HD), BF16),
        scratch_shapes=[pltpu.VMEM((S + PAD, hw), BF16), pltpu.VMEM((S + PAD, hw), BF16)],
        compiler_params=_params(("arbitrary", "arbitrary", "arbitrary")))(q, kv, kv, bias)


def _attn_bwd(q, kv, bias, do, B, S):
    HD = q.shape[-1]
    H = HD // HEAD_DIM
    hw, qspec, kspec, vspec, bspec = _attn_specs(S)
    scale = HEAD_DIM ** -0.5
    nq = S // Q_BLOCK

    def body(q_ref, k_ref, v_ref, b_ref, do_ref, dq_ref, dkv_ref, db_ref, kp, vp, dk_acc, dv_acc):
        b, i = pl.program_id(1), pl.program_id(2)
        q0 = pl.multiple_of(i * Q_BLOCK, Q_BLOCK)

        @pl.when(i == 0)
        def _():
            _load_padded(k_ref, v_ref, kp, vp)
            dk_acc[...] = jnp.zeros_like(dk_acc)
            dv_acc[...] = jnp.zeros_like(dv_acc)

        @pl.when((i == 0) & (b == 0))
        def _():
            db_ref[...] = jnp.zeros_like(db_ref)

        before = _before_start(q0)
        for h in range(HEADS_PER_STEP):
            hs = slice(h * HEAD_DIM, (h + 1) * HEAD_DIM)
            p, inv, kh = _attn_exp(q_ref, kp, b_ref, h, q0, before)
            p = p * inv
            doh = do_ref[:, hs]
            dp = lax.dot_general(doh, vp[pl.ds(q0, K_SPAN), hs], (((1,), (1,)), ((), ())),
                                 preferred_element_type=F32)
            ds = p * (dp - jnp.sum(p * dp, axis=-1, keepdims=True))
            db_ref[h] += ds
            dsb = ds.astype(BF16)
            dq_ref[:, hs] = (jnp.dot(dsb, kh, preferred_element_type=F32) * scale).astype(BF16)
            dk_acc[pl.ds(q0, K_SPAN), hs] += lax.dot_general(dsb, q_ref[:, hs], (((0,), (0,)), ((), ())),
                                                              preferred_element_type=F32)
            dv_acc[pl.ds(q0, K_SPAN), hs] += lax.dot_general(p.astype(BF16), doh, (((0,), (0,)), ((), ())),
                                                              preferred_element_type=F32)

        @pl.when(i == nq - 1)
        def _():
            dkv_ref[0] = dk_acc[PAD:, :].astype(BF16)
            dkv_ref[1] = dv_acc[PAD:, :].astype(BF16)

    return pl.pallas_call(
        body, name="attn_bwd", grid=(HD // hw, B, nq), in_specs=[qspec, kspec, vspec, bspec, qspec],
        out_specs=[qspec, pl.BlockSpec((2, None, S, hw), lambda g, b, i: (0, b, 0, g)), bspec],
        out_shape=[jax.ShapeDtypeStruct((B, S, HD), BF16), jax.ShapeDtypeStruct((2, B, S, HD), BF16),
                   jax.ShapeDtypeStruct((H, Q_BLOCK, K_SPAN), F32)],
        scratch_shapes=[pltpu.VMEM((S + PAD, hw), BF16), pltpu.VMEM((S + PAD, hw), BF16),
                        pltpu.VMEM((S + PAD, hw), F32), pltpu.VMEM((S + PAD, hw), F32)],
        compiler_params=_params(("arbitrary", "arbitrary", "arbitrary")))(q, kv, kv, bias, do)


def _loss_head(h, g, target, *, tm=512):
    T, D = h.shape
    tm = _row_tile(T, tm)

    def body(h_ref, g_ref, t_ref, dh_ref, loss_ref, dg_ref):
        @pl.when(pl.program_id(0) == 0)
        def _():
            loss_ref[...] = jnp.zeros_like(loss_ref)
            dg_ref[...] = jnp.zeros_like(dg_ref)

        n, r = _rms(h_ref[...])
        g = g_ref[...]
        e = n * g - t_ref[...]
        loss_ref[...] += 0.5 * jnp.sum(jnp.mean(e * e, axis=-1, keepdims=True), axis=0, keepdims=True)
        dy = e * (1.0 / D)
        dg_ref[...] += jnp.sum(dy * n, axis=0, keepdims=True)
        t = dy * g
        dh_ref[...] = r * (t - n * jnp.mean(t * n, axis=-1, keepdims=True))

    row = pl.BlockSpec((tm, D), lambda i: (i, 0))
    return pl.pallas_call(
        body, name="loss_head", grid=(T // tm,), in_specs=[row, pl.BlockSpec((1, D), lambda i: (0, 0)), row],
        out_specs=[row, pl.BlockSpec((8, 128), lambda i: (0, 0)), pl.BlockSpec((1, D), lambda i: (0, 0))],
        out_shape=[jax.ShapeDtypeStruct((T, D), F32), jax.ShapeDtypeStruct((8, 128), F32),
                   jax.ShapeDtypeStruct((1, D), F32)],
        compiler_params=_params(("arbitrary",)))(h, g.reshape(1, D), target)


def _sub_rows(R):
    for cand in (256, 352, 128, 64, 8):
        if R % cand == 0 and R > cand:
            return cand
    return R


def _adamw(w, g, m, v, *, name):
    R, C = w.shape
    tr = _sub_rows(R)

    def body(w_ref, g_ref, m_ref, v_ref, d_ref, nm_ref, nv_ref):
        g = g_ref[...]
        m = ADAM_B1 * m_ref[...] + (1.0 - ADAM_B1) * g
        v = ADAM_B2 * v_ref[...] + (1.0 - ADAM_B2) * (g * g)
        m_hat = m / (1.0 - ADAM_B1 ** ADAM_STEP)
        v_hat = v / (1.0 - ADAM_B2 ** ADAM_STEP)
        d_ref[...] = -ADAM_LR * (m_hat / (jnp.sqrt(v_hat) + ADAM_EPS) + ADAM_WD * w_ref[...])
        nm_ref[...] = m
        nv_ref[...] = v

    spec = pl.BlockSpec((tr, C), lambda i: (i, 0))
    return pl.pallas_call(body, name=name, grid=(R // tr,), in_specs=[spec] * 4, out_specs=[spec] * 3,
                          out_shape=[jax.ShapeDtypeStruct((R, C), F32)] * 3,
                          compiler_params=_params(("arbitrary",)))(w, g, m, v)


def _add_pair(units, got, core, *, name):
    n4, R, C = got.shape
    rows = n4 * R
    tr = 512 if rows % 512 == 0 else R

    def body(c_ref, u_ref, got_ref, o_ref):
        o_ref[...] = (u_ref[...].astype(F32) + got_ref[...].astype(F32)).astype(BF16)

    spec = pl.BlockSpec((tr, C), lambda i, c: (i, 0))
    grid_spec = pltpu.PrefetchScalarGridSpec(
        num_scalar_prefetch=1, grid=(rows // tr,),
        in_specs=[pl.BlockSpec((None, tr, C), lambda i, c: (c[0], i, 0)), spec], out_specs=spec)
    out = pl.pallas_call(body, name=name, grid_spec=grid_spec, out_shape=jax.ShapeDtypeStruct((rows, C), BF16),
                         compiler_params=_params(("arbitrary",)))(core.reshape(1), units.reshape(2, rows, C),
                                                                   got.reshape(rows, C))
    return out.reshape(n4, R, C)


def _sum_chips(w, own, got, pos, *, name, layer=0, into=None):
    _, R, C = own.shape
    tr = _sub_rows(R)
    nr = R // tr

    def body(p_ref, own_ref, got_ref, *rest):
        o_ref = rest[-1]
        o_ref[...] = (own_ref[...].astype(F32) + got_ref[0].astype(F32) + got_ref[1].astype(F32)
                      + got_ref[2].astype(F32))

    if w.row_sharded:
        out_map = lambda i, p: (layer, i, p[1])
    else:
        out_map = lambda i, p: (layer, p[1] * nr + i, 0)
    ins = [pos, own, got]
    in_specs = [pl.BlockSpec((None, tr, C), lambda i, p: (p[0], i, 0)),
                pl.BlockSpec((3, tr, C), lambda i, p: (0, i, 0))]
    alias = {}
    if into is not None:
        ins.append(into)
        in_specs.append(ANY)
        alias = {3: 0}
    grid_spec = pltpu.PrefetchScalarGridSpec(num_scalar_prefetch=1, grid=(nr,), in_specs=in_specs,
                                             out_specs=pl.BlockSpec((None, tr, C), out_map))
    return pl.pallas_call(body, name=name, grid_spec=grid_spec, input_output_aliases=alias,
                          out_shape=jax.ShapeDtypeStruct((w.L, w.ks, w.ns), F32),
                          compiler_params=_params(("arbitrary",)))(*ins)


def _mesh_pos():
    return lax.axis_index("x"), lax.axis_index("y"), lax.axis_index("c")


def _other_chips(x, y):
    return [(1 - x, y), (x, 1 - y), (1 - x, 1 - y)]


ANY = pl.BlockSpec(memory_space=pl.ANY)


class _W:
    def __init__(self, name, shard, row_sharded):
        self.name = name
        self.L, ks, ns = shard.shape
        self.row_sharded = row_sharded
        self.K, self.N = (ks * N_CHIPS, ns) if row_sharded else (ks, ns * N_CHIPS)
        self.ks, self.ns = ks, ns

    def shard_of(self, full, j):
        if self.row_sharded:
            return full.at[:, pl.ds(j * self.ks, self.ks), :]
        return full.at[:, :, pl.ds(j * self.ns, self.ns)]

    def half_of(self, shard, c):
        if self.row_sharded:
            return shard.at[:, :, pl.ds(c * (self.ns // 2), self.ns // 2)]
        return shard.at[:, pl.ds(c * (self.ks // 2), self.ks // 2), :]


HBM = pl.BlockSpec(memory_space=pltpu.HBM)
SEM = pl.BlockSpec(memory_space=pltpu.SEMAPHORE)
IN_FLIGHT = pltpu.SideEffectType.DATAFLOW_SIDE_EFFECTING


def _in_hbm(a):
    return pltpu.with_memory_space_constraint(a, pltpu.HBM)


def _gather_start(ws, shards, after):
    nw = len(ws)

    def body(*refs):
        src, dst = refs[:nw], refs[nw:2 * nw]
        send, recv = refs[2 * nw + 1:3 * nw + 1], refs[3 * nw + 1:4 * nw + 1]
        x, y, c = _mesh_pos()
        me = 2 * x + y
        for i, w in enumerate(ws):
            for f, (px, py) in enumerate(_other_chips(x, y)):
                pltpu.make_async_remote_copy(src_ref=w.half_of(src[i], c), dst_ref=w.half_of(w.shard_of(dst[i], me), c),
                                             send_sem=send[i].at[f], recv_sem=recv[i].at[f], device_id=(px, py, c),
                                             device_id_type=MESH).start()

    fulls = [lax.empty((w.L, w.K, w.N), BF16) for w in ws]
    out = pl.pallas_call(
        body, name="gather_start", in_specs=[HBM] * (2 * nw) + [ANY],
        out_specs=[SEM] * (2 * nw) + [HBM] * (2 * nw),
        out_shape=[pltpu.SemaphoreType.DMA((3,))] * (2 * nw)
        + [pltpu.HBM(s.shape, BF16) for s in shards] + [pltpu.HBM(f.shape, BF16) for f in fulls],
        input_output_aliases={i: 2 * nw + i for i in range(2 * nw)},
        compiler_params=pltpu.CompilerParams(has_side_effects=IN_FLIGHT))(
            *[_in_hbm(s) for s in shards], *[_in_hbm(f) for f in fulls], after)
    return [(out[i], out[nw + i], out[2 * nw + i], out[3 * nw + i]) for i in range(nw)]


def _gather_wait(ws, flight, after, *, name):
    nw = len(ws)

    def body(*refs):
        src, dst = refs[:nw], refs[nw:2 * nw]
        send, recv = refs[2 * nw:3 * nw], refs[3 * nw:4 * nw]
        x, y, c = _mesh_pos()
        for i, w in enumerate(ws):
            for f, (px, py) in enumerate(_other_chips(x, y)):
                landed = w.half_of(w.shard_of(dst[i], 2 * px + py), c)
                cp = pltpu.make_async_remote_copy(src_ref=w.half_of(src[i], c), dst_ref=landed, send_sem=send[i].at[f],
                                                  recv_sem=recv[i].at[f], device_id=(px, py, c), device_id_type=MESH)
                cp.wait_send()
                cp.wait_recv()

    shards, fulls = [fl[2] for fl in flight], [fl[3] for fl in flight]
    out = pl.pallas_call(
        body, name=name, in_specs=[HBM] * (2 * nw) + [SEM] * (2 * nw) + [ANY],
        out_specs=[HBM] * (2 * nw),
        out_shape=[pltpu.HBM(s.shape, BF16) for s in shards] + [pltpu.HBM(f.shape, BF16) for f in fulls],
        input_output_aliases={i: i for i in range(2 * nw)},
        compiler_params=pltpu.CompilerParams(has_side_effects=IN_FLIGHT))(
            *shards, *fulls, *[fl[0] for fl in flight], *[fl[1] for fl in flight], after)
    return out[:nw], out[nw:]


def _gather_finish(ws, shards, fulls, *, name):
    nw = len(ws)

    def body(*refs):
        src, dst, stage = refs[:nw], refs[3 * nw:4 * nw], refs[4 * nw:5 * nw]
        send_sems, recv_sems, load_sems, store_sems = refs[5 * nw:]
        x, y, c = _mesh_pos()
        me = 2 * x + y
        sibling = (x, y, 1 - c)
        chips = _other_chips(x, y)

        def fwd(i, w, f, half):
            px, py = chips[f]
            landed = w.half_of(w.shard_of(dst[i], 2 * px + py), half)
            return pltpu.make_async_remote_copy(src_ref=landed, dst_ref=landed, send_sem=send_sems.at[3 * i + f],
                                                recv_sem=recv_sems.at[3 * i + f], device_id=sibling,
                                                device_id_type=MESH)

        loads = [pltpu.make_async_copy(src[i], stage[i], load_sems.at[i]) for i in range(nw)]
        for cp in loads:
            cp.start()
        sends = [fwd(i, w, f, c) for i, w in enumerate(ws) for f in range(3)]
        for cp in sends:
            cp.start()
        stores = [pltpu.make_async_copy(stage[i], w.shard_of(dst[i], me), store_sems.at[i])
                  for i, w in enumerate(ws)]
        for ld, st in zip(loads, stores):
            ld.wait()
            st.start()
        for i, w in enumerate(ws):
            for f in range(3):
                fwd(i, w, f, 1 - c).wait_recv()
        for cp in sends:
            cp.wait_send()
        for cp in stores:
            cp.wait()

    out = pl.pallas_call(
        body, name=name, in_specs=[ANY] * (2 * nw), out_specs=[ANY] * (2 * nw),
        out_shape=[jax.ShapeDtypeStruct(s.shape, BF16) for s in shards]
        + [jax.ShapeDtypeStruct(f.shape, BF16) for f in fulls],
        input_output_aliases={i: i for i in range(2 * nw)},
        scratch_shapes=[pltpu.VMEM((w.L, w.ks, w.ns), BF16) for w in ws]
        + [pltpu.SemaphoreType.DMA((3 * nw,)), pltpu.SemaphoreType.DMA((3 * nw,)), pltpu.SemaphoreType.DMA((nw,)),
           pltpu.SemaphoreType.DMA((nw,))],
        compiler_params=_params(has_side_effects=True))(*shards, *fulls)
    return out[nw:]


def _swap_units(units, *, name):
    nw = len(units)

    def body(*refs):
        src, got = refs[:nw], refs[nw:2 * nw]
        send_sems, recv_sems = refs[2 * nw:]
        x, y, c = _mesh_pos()
        copies = [pltpu.make_async_remote_copy(src_ref=src[i].at[1 - c], dst_ref=got[i], send_sem=send_sems.at[i],
                                               recv_sem=recv_sems.at[i], device_id=(x, y, 1 - c),
                                               device_id_type=MESH) for i in range(nw)]
        for cp in copies:
            cp.start()
        for cp in copies:
            cp.wait()

    return pl.pallas_call(
        body, name=name, in_specs=[ANY] * nw, out_specs=[ANY] * nw,
        out_shape=[jax.ShapeDtypeStruct(u.shape[1:], BF16) for u in units],
        scratch_shapes=[pltpu.SemaphoreType.DMA((nw,)), pltpu.SemaphoreType.DMA((nw,))],
        compiler_params=_params(has_side_effects=True))(*units)


def _scatter_copy(src, got, send, recv, f, chip, c):
    px, py = chip
    return pltpu.make_async_remote_copy(src_ref=src.at[2 * px + py], dst_ref=got.at[f], send_sem=send.at[f],
                                        recv_sem=recv.at[f], device_id=(px, py, c), device_id_type=MESH)


def _scatter_start(sums, *, name):
    nw = len(sums)

    def body(*refs):
        src, got = refs[:nw], refs[nw:2 * nw]
        send, recv = refs[2 * nw:3 * nw], refs[3 * nw:4 * nw]
        x, y, c = _mesh_pos()
        for i in range(nw):
            for f, chip in enumerate(_other_chips(x, y)):
                _scatter_copy(src[i], got[i], send[i], recv[i], f, chip, c).start()

    lands = [lax.empty((3,) + s.shape[1:], BF16) for s in sums]
    out = pl.pallas_call(
        body, name=name, in_specs=[HBM] * (2 * nw), out_specs=[SEM] * (2 * nw) + [HBM] * (2 * nw),
        out_shape=[pltpu.SemaphoreType.DMA((3,))] * (2 * nw)
        + [pltpu.HBM(s.shape, BF16) for s in sums] + [pltpu.HBM(l.shape, BF16) for l in lands],
        input_output_aliases={i: 2 * nw + i for i in range(2 * nw)},
        compiler_params=pltpu.CompilerParams(has_side_effects=IN_FLIGHT))(
            *[_in_hbm(s) for s in sums], *[_in_hbm(l) for l in lands])
    return [(out[i], out[nw + i], out[2 * nw + i], out[3 * nw + i]) for i in range(nw)]


def _scatter_wait(flight, after):
    nw = len(flight)

    def body(*refs):
        src, got = refs[:nw], refs[nw:2 * nw]
        send, recv = refs[2 * nw:3 * nw], refs[3 * nw:4 * nw]
        x, y, c = _mesh_pos()
        for i in range(nw):
            for f, chip in enumerate(_other_chips(x, y)):
                cp = _scatter_copy(src[i], got[i], send[i], recv[i], f, chip, c)
                cp.wait_send()
                cp.wait_recv()

    sums, lands = [fl[2] for fl in flight], [fl[3] for fl in flight]
    out = pl.pallas_call(
        body, name="scatter_wait", in_specs=[HBM] * (2 * nw) + [SEM] * (2 * nw) + [ANY], out_specs=[HBM] * (2 * nw),
        out_shape=[pltpu.HBM(s.shape, BF16) for s in sums] + [pltpu.HBM(l.shape, BF16) for l in lands],
        input_output_aliases={i: i for i in range(2 * nw)},
        compiler_params=pltpu.CompilerParams(has_side_effects=IN_FLIGHT))(
            *sums, *lands, *[fl[0] for fl in flight], *[fl[1] for fl in flight], after)
    return out[:nw], out[nw:]


def _join_halves(ws, shards):
    nw = len(ws)

    def body(*refs):
        buf = refs[nw:2 * nw]
        send_sems, recv_sems = refs[2 * nw:]
        x, y, c = _mesh_pos()
        sibling = (x, y, 1 - c)

        def copy(i, w, half):
            region = w.half_of(buf[i], half)
            return pltpu.make_async_remote_copy(src_ref=region, dst_ref=region, send_sem=send_sems.at[i],
                                                recv_sem=recv_sems.at[i], device_id=sibling, device_id_type=MESH)

        sends = [copy(i, w, c) for i, w in enumerate(ws)]
        for cp in sends:
            cp.start()
        for i, w in enumerate(ws):
            copy(i, w, 1 - c).wait_recv()
        for cp in sends:
            cp.wait_send()

    return pl.pallas_call(
        body, name="join_halves", in_specs=[ANY] * nw, out_specs=[ANY] * nw,
        out_shape=[jax.ShapeDtypeStruct((w.L, w.ks, w.ns), F32) for w in ws],
        input_output_aliases={i: i for i in range(nw)},
        scratch_shapes=[pltpu.SemaphoreType.DMA((nw,)), pltpu.SemaphoreType.DMA((nw,))],
        compiler_params=_params(has_side_effects=True))(*shards)


def _allreduce_small(vec):
    R = vec.shape[0]

    def body(x_ref, o_ref, buf, send_sems, recv_sems):
        x, y, c = _mesh_pos()
        me, sibling = (x, y, c), (x, y, 1 - c)
        chips = _other_chips(x, y)

        def slot(px, py, pc):
            return buf.at[4 * px + 2 * py + pc]

        def copy(k, block, to, src=None):
            return pltpu.make_async_remote_copy(src_ref=slot(*block) if src is None else src, dst_ref=slot(*block),
                                                send_sem=send_sems.at[k], recv_sem=recv_sems.at[k], device_id=to,
                                                device_id_type=MESH)

        first = [copy(0, me, sibling, src=x_ref)] + [copy(1 + f, me, (*chip, c), src=x_ref)
                                                     for f, chip in enumerate(chips)]
        for cp in first:
            cp.start()
        passed = [copy(4 + f, (*chip, c), sibling) for f, chip in enumerate(chips)]
        for f, chip in enumerate(chips):
            copy(1 + f, (*chip, c), me).wait_recv()
            passed[f].start()
        copy(0, sibling, me).wait_recv()
        for f, chip in enumerate(chips):
            copy(4 + f, (*chip, 1 - c), me).wait_recv()
        for cp in first + passed:
            cp.wait_send()
        slot(*me)[...] = x_ref[...]
        acc = buf[0]
        for d in range(1, 8):
            acc = acc + buf[d]
        o_ref[...] = acc

    return pl.pallas_call(
        body, name="allreduce_small", in_specs=[pl.BlockSpec(memory_space=pltpu.VMEM)],
        out_specs=pl.BlockSpec(memory_space=pltpu.VMEM), out_shape=jax.ShapeDtypeStruct((R, 128), F32),
        scratch_shapes=[pltpu.VMEM((8, R, 128), F32), pltpu.SemaphoreType.DMA((7,)), pltpu.SemaphoreType.DMA((7,))],
        compiler_params=_params())(vec)


def _pack(parts):
    flat = jnp.concatenate([p.reshape(-1).astype(F32) for p in parts])
    n = flat.shape[0]
    pad = (-n) % (64 * 128)
    return jnp.pad(flat, (0, pad)).reshape(-1, 128)


def _unpack(vec, shapes):
    flat = vec.reshape(-1)
    out, off = [], 0
    for s in shapes:
        n = int(np.prod(s))
        out.append(flat[off:off + n].reshape(s))
        off += n
    return out


def kernel(x, a_norm_g, a_w_in, a_v_norm_g, a_w_s, a_b_s, a_w_out, kv_norm_g, w_kv, b_norm_g, b_w_q, b_rel_bias, b_w_o, f_norm_g, f_w_in, f_conv_w, f_conv_b, f_w_down, final_norm_g, loss_target, m_a_norm_g, m_a_w_in, m_a_v_norm_g, m_a_w_s, m_a_b_s, m_a_w_out, m_kv_norm_g, m_w_kv, m_b_norm_g, m_b_w_q, m_b_rel_bias, m_b_w_o, m_f_norm_g, m_f_w_in, m_f_conv_w, m_f_conv_b, m_f_w_down, m_final_norm_g, v_a_norm_g, v_a_w_in, v_a_v_norm_g, v_a_w_s, v_a_b_s, v_a_w_out, v_kv_norm_g, v_w_kv, v_b_norm_g, v_b_w_q, v_b_rel_bias, v_b_w_o, v_f_norm_g, v_f_w_in, v_f_conv_w, v_f_conv_b, v_f_w_down, v_final_norm_g):
    B, S, D = x.shape
    T = B * S
    xi, yi, ci = lax.axis_index("x"), lax.axis_index("y"), lax.axis_index("c")
    j_me = (2 * xi + yi).astype(jnp.int32)
    core = ci.astype(jnp.int32)
    pos = jnp.stack([j_me, core])

    w_shards = {"a_w_in": (a_w_in, False), "a_w_out": (a_w_out, True), "w_kv": (w_kv[None], False),
                "b_w_q": (b_w_q, True), "b_w_o": (b_w_o, True), "f_w_in": (f_w_in, False), "f_w_down": (f_w_down, True)}
    names = list(w_shards)
    ws = [_W(n, w_shards[n][0], w_shards[n][1]) for n in names]
    g_shards = {"a_w_in": (a_w_in, False), "a_w_out": (a_w_out, True),
                "f_w_in0": (f_w_in[0:1], False), "f_w_down0": (f_w_down[0:1], True),
                "w_kv": (w_kv[None], False), "b_w_q": (b_w_q, True), "b_w_o": (b_w_o, True),
                "f_w_in1": (f_w_in[1:2], False), "f_w_down1": (f_w_down[1:2], True)}
    g_names = list(g_shards)
    g_ws = {n: _W(n, *g_shards[n]) for n in g_names}

    Wd = a_w_in.shape[1]
    GW = a_v_norm_g.shape[1] * N_CHIPS
    F2 = f_conv_w.shape[2] * N_CHIPS
    Fh = F2 // 2
    nsd, nsg, nsf = a_norm_g.shape[1], a_v_norm_g.shape[1], f_conv_w.shape[2]
    own = (ci == 0).astype(F32)
    place = lambda sh, width, n: lax.dynamic_update_slice_in_dim(
        jnp.zeros(sh.shape[:-1] + (width,), F32), sh * own, j_me * n, axis=sh.ndim - 1)
    gathered = _allreduce_small(_pack([place(a_norm_g, Wd, nsd), place(a_v_norm_g, GW, nsg),
                                       place(f_conv_w, F2, nsf)]))
    a_g, a_vg, conv_w = _unpack(gathered, [(1, Wd), (1, GW), (2, 3, F2)])

    flight = dict(zip(g_names, _gather_start([g_ws[n] for n in g_names],
                                             [g_shards[n][0].astype(BF16) for n in g_names], gathered)))
    full = {}

    def arrive(group, after, tag):
        gw = [g_ws[n] for n in group]
        sh, fu = _gather_wait(gw, [flight[n] for n in group], after, name=f"gather_wait_{tag}")
        full.update(zip(group, _gather_finish(gw, sh, fu, name=f"gather_finish_{tag}")))
    conv_w2 = conv_w.reshape(2, 3, 2, Fh).transpose(0, 2, 1, 3)
    conv_b2 = f_conv_b.reshape(2, 2, Fh)

    h0 = x.reshape(T, D)
    target = loss_target.reshape(T, D)
    bs_tile = jnp.repeat(a_b_s[0].T, GROUP_DIM, axis=1)
    ws_a = a_w_s[0]
    scale = HEAD_DIM ** -0.5
    HD = b_w_q.shape[2]
    H = HD // HEAD_DIM
    n_rel = b_rel_bias.shape[-1]
    frow = b_rel_bias[0][:, _bias_index()].reshape(H, 1, F_LEN)
    bias = _bias_expand(frow)

    def ffn_fwd(h, l):
        a, n = _mm(h, full[f"f_w_in{l}"], layer=0, norm_g=f_norm_g[l], split_out=True, out_dtype=BF16,
                   emit_norm=True, name=f"ffn{l}_in")
        yff, c = _conv_fwd(a, conv_w2[l], conv_b2[l], S)
        return _mm(yff, full[f"f_w_down{l}"], layer=0, res=h, name=f"ffn{l}_down"), (a, c, n, yff)

    arrive(["a_w_in", "a_w_out"], h0, "a")
    zp, n_a = _mm(h0, full["a_w_in"], layer=0, norm_g=a_g[0], emit_norm=True, name="a_in")
    out_a = _gate_fwd(zp, a_vg, ws_a, bs_tile)
    h1 = _mm(out_a, full["a_w_out"], layer=0, res=h0, name="a_out")
    arrive(["f_w_in0", "f_w_down0"], h1, "f0")
    h2, saved0 = ffn_fwd(h1, 0)
    arrive(["w_kv", "b_w_q", "b_w_o"], h2, "b")
    arrive(["f_w_in1", "f_w_down1"], h2, "f1")
    kv, n_kv = _mm(h2, full["w_kv"], layer=0, norm_g=kv_norm_g, out_dtype=BF16, split_out=True, emit_norm=True,
                   name="kv")
    q, n_q = _mm(h2, full["b_w_q"], layer=0, norm_g=b_norm_g[0], scale=scale, out_dtype=BF16, emit_norm=True,
                 name="q")
    kv4, q3 = kv.reshape(2, B, S, HD), q.reshape(B, S, HD)
    o = _attn_fwd(q3, kv4, bias, B, S).reshape(T, HD)
    h3 = _mm(o, full["b_w_o"], layer=0, res=h2, name="attn_out")
    h4, saved1 = ffn_fwd(h3, 1)
    dh, loss8, dg_final = _loss_head(h4, final_norm_g, target)

    units = {}

    def ffn_bwd(dh, h, saved, l):
        a, c, n, yff = saved
        dyff = _mm(dh, full[f"f_w_down{l}"], layer=0, trans_w=True, out_dtype=BF16, name=f"ffn{l}_down_dx")
        units[f"f_w_down{l}"] = _mm_tn(yff, dh, rows_are_shards=True, name=f"ffn{l}_down_dw")
        da, dcw, dcb = _conv_bwd(a, c, dyff, conv_w2[l], S)
        units[f"f_w_in{l}"] = _mm_tn(n, da, split_y=True, name=f"ffn{l}_in_dw")
        dh, dg = _mm(da, full[f"f_w_in{l}"], layer=0, trans_w=True, split_x=True, bwd=(h, f_norm_g[l], dh), tm=256,
                     name=f"ffn{l}_in_dx")
        return dh, dg, dcw, dcb

    in_flight = {}

    def reduce_start(group, tag):
        got = _swap_units([units[n] for n in group], name=f"swap_{tag}")
        sums = [_add_pair(units[n], g_, core, name=f"pair_{n}") for n, g_ in zip(group, got)]
        in_flight.update(zip(group, _scatter_start(sums, name=f"scatter_start_{tag}")))

    dh, dg_f1, dcw1, dcb1 = ffn_bwd(dh, h3, saved1, 1)
    reduce_start(["f_w_down1", "f_w_in1"], "f1")
    do = _mm(dh, full["b_w_o"], layer=0, trans_w=True, out_dtype=BF16, name="attn_out_dx")
    units["b_w_o"] = _mm_tn(o, dh, rows_are_shards=True, name="b_w_o_dw")
    dq, dkv, dbias = _attn_bwd(q3, kv4, bias, do.reshape(B, S, HD), B, S)
    d_rel = _bias_reduce(dbias, n_rel).reshape(1, H, n_rel)
    dq, dkv = dq.reshape(T, HD), dkv.reshape(2, T, HD)
    units["b_w_q"] = _mm_tn(n_q, dq, rows_are_shards=True, name="b_w_q_dw")
    dh, dg_b = _mm(dq, full["b_w_q"], layer=0, trans_w=True, bwd=(h2, b_norm_g[0], dh), name="q_dx")
    units["w_kv"] = _mm_tn(n_kv, dkv, split_y=True, name="w_kv_dw")
    dh, dg_kv = _mm(dkv, full["w_kv"], layer=0, trans_w=True, split_x=True, bwd=(h2, kv_norm_g, dh), name="kv_dx")
    reduce_start(["b_w_o", "b_w_q", "w_kv"], "b")
    dh, dg_f0, dcw0, dcb0 = ffn_bwd(dh, h1, saved0, 0)
    reduce_start(["f_w_down0", "f_w_in0"], "f0")
    d_out = _mm(dh, full["a_w_out"], layer=0, trans_w=True, out_dtype=BF16, name="a_out_dx")
    units["a_w_out"] = _mm_tn(out_a, dh, rows_are_shards=True, name="a_w_out_dw")
    dzp, dws, dbs, dgv = _gate_bwd(zp, d_out, a_vg, ws_a, bs_tile)
    units["a_w_in"] = _mm_tn(n_a, dzp, name="a_w_in_dw")
    reduce_start(["a_w_out", "a_w_in"], "a")
    grad_x, dg_a = _mm(dzp, full["a_w_in"], layer=0, trans_w=True, bwd=(h0, a_g[0], dh), name="a_in_dx")

    sums, recv = _scatter_wait([in_flight[n] for n in g_names], grad_x)
    sums, recv = dict(zip(g_names, sums)), dict(zip(g_names, recv))
    halves = []
    for n, w in zip(names, ws):
        if w.L == 1:
            halves.append(_sum_chips(w, sums[n], recv[n], pos, name=f"chips_{n}"))
        else:
            first = _sum_chips(w, sums[n + "0"], recv[n + "0"], pos, name=f"chips_{n}0")
            halves.append(_sum_chips(w, sums[n + "1"], recv[n + "1"], pos, layer=1, into=first, name=f"chips_{n}1"))
    g_big = dict(zip(names, _join_halves(ws, halves)))
    g_big["w_kv"] = g_big["w_kv"][0]

    to_flat = lambda d: d.transpose(1, 0, 2).reshape(3, F2)
    small = {"a_norm_g": dg_a, "a_v_norm_g": dgv, "a_w_s": dws[None], "a_b_s": dbs[None], "kv_norm_g": dg_kv[0],
             "b_norm_g": dg_b, "b_rel_bias": d_rel, "f_norm_g": jnp.concatenate([dg_f0, dg_f1], axis=0),
             "f_conv_w": jnp.stack([to_flat(dcw0), to_flat(dcw1)]),
             "f_conv_b": jnp.stack([dcb0.reshape(F2), dcb1.reshape(F2)]), "final_norm_g": dg_final[0]}
    snames = list(small)
    red = _allreduce_small(_pack([small[n] for n in snames] + [loss8[0:1, 0:1]]))
    parts = _unpack(red, [small[n].shape for n in snames] + [(1,)])
    g_small = dict(zip(snames, parts[:-1]))
    loss = parts[-1][0]
    g_small["a_norm_g"] = lax.dynamic_slice_in_dim(g_small["a_norm_g"], j_me * nsd, nsd, axis=1)
    g_small["a_v_norm_g"] = lax.dynamic_slice_in_dim(g_small["a_v_norm_g"], j_me * nsg, nsg, axis=1)
    g_small["f_conv_w"] = lax.dynamic_slice_in_dim(g_small["f_conv_w"], j_me * nsf, nsf, axis=2)

    given = dict(a_norm_g=(a_norm_g, m_a_norm_g, v_a_norm_g), a_w_in=(a_w_in, m_a_w_in, v_a_w_in),
                 a_v_norm_g=(a_v_norm_g, m_a_v_norm_g, v_a_v_norm_g), a_w_s=(a_w_s, m_a_w_s, v_a_w_s),
                 a_b_s=(a_b_s, m_a_b_s, v_a_b_s), a_w_out=(a_w_out, m_a_w_out, v_a_w_out),
                 kv_norm_g=(kv_norm_g, m_kv_norm_g, v_kv_norm_g), w_kv=(w_kv, m_w_kv, v_w_kv),
                 b_norm_g=(b_norm_g, m_b_norm_g, v_b_norm_g), b_w_q=(b_w_q, m_b_w_q, v_b_w_q),
                 b_rel_bias=(b_rel_bias, m_b_rel_bias, v_b_rel_bias), b_w_o=(b_w_o, m_b_w_o, v_b_w_o),
                 f_norm_g=(f_norm_g, m_f_norm_g, v_f_norm_g), f_w_in=(f_w_in, m_f_w_in, v_f_w_in),
                 f_conv_w=(f_conv_w, m_f_conv_w, v_f_conv_w), f_conv_b=(f_conv_b, m_f_conv_b, v_f_conv_b),
                 f_w_down=(f_w_down, m_f_w_down, v_f_w_down), final_norm_g=(final_norm_g, m_final_norm_g, v_final_norm_g))
    order = list(given)
    grads, deltas, new_m, new_v = {}, {}, {}, {}
    for n in names:
        w_, m_, v_ = given[n]
        g_ = g_big[n]
        C = w_.shape[-1]
        d2, m2, v2 = _adamw(w_.reshape(-1, C), g_.reshape(-1, C), m_.reshape(-1, C), v_.reshape(-1, C),
                            name=f"adamw_{n}")
        grads[n], deltas[n], new_m[n], new_v[n] = g_.reshape(w_.shape), d2.reshape(w_.shape), m2.reshape(w_.shape), \
            v2.reshape(w_.shape)
    sm = [n for n in order if n not in names]
    d2, m2, v2 = _adamw(_pack([given[n][0] for n in sm]), _pack([g_small[n].reshape(given[n][0].shape) for n in sm]),
                        _pack([given[n][1] for n in sm]), _pack([given[n][2] for n in sm]), name="adamw_small")
    shapes = [given[n][0].shape for n in sm]
    for n, d_, m_, v_ in zip(sm, _unpack(d2, shapes), _unpack(m2, shapes), _unpack(v2, shapes)):
        grads[n], deltas[n], new_m[n], new_v[n] = g_small[n].reshape(given[n][0].shape), d_, m_, v_

    return (loss, grad_x.reshape(B, S, D), *[grads[n] for n in order], *[deltas[n] for n in order],
            *[new_m[n] for n in order], *[new_v[n] for n in order])
```

```python
import math

import numpy as np
import jax
import jax.numpy as jnp
from jax import lax
from jax.experimental import pallas as pl
from jax.experimental.pallas import tpu as pltpu

F32 = jnp.float32
BF16 = jnp.bfloat16
MESH = pl.DeviceIdType.MESH

EPS = 1e-6
NEG_INF = -1e30
CHUNK = 64
GMLP_BLOCK = 128
GROUP_DIM = 128
HEAD_DIM = 64
LEFT_CHUNKS = 8
PAD = LEFT_CHUNKS * CHUNK
REL_CLIP = 128
Q_BLOCK = 256
K_SPAN = PAD + Q_BLOCK
F_LEN = K_SPAN + Q_BLOCK
HEADS_PER_STEP = 4
N_CHIPS = 4

ADAM_LR = 0.001
ADAM_B1 = 0.9
ADAM_B2 = 0.999
ADAM_EPS = 1e-08
ADAM_WD = 0.01
ADAM_STEP = 10

VMEM_LIMIT = 56 * 1024 * 1024


def _params(sem=None, **kw):
    if sem is not None:
        kw["dimension_semantics"] = sem
    return pltpu.CompilerParams(vmem_limit_bytes=VMEM_LIMIT, **kw)


def _rms(xf):
    r = lax.rsqrt(jnp.mean(xf * xf, axis=-1, keepdims=True) + EPS)
    return xf * r, r


def _gelu(x):
    c = math.sqrt(2.0 / math.pi)
    return 0.5 * x * (1.0 + jnp.tanh(c * (x + 0.044715 * x * x * x)))


def _gelu_grad(x):
    c = math.sqrt(2.0 / math.pi)
    t = jnp.tanh(c * (x + 0.044715 * x * x * x))
    return 0.5 * (1.0 + t) + 0.5 * x * (1.0 - t * t) * c * (1.0 + 3.0 * 0.044715 * x * x)


def _col_tile(n):
    if n <= 1024:
        return n
    for t in (1408, 1024, 512):
        if n % t == 0:
            return t
    raise ValueError(n)


def _row_tile(t, want):
    while t % want:
        want //= 2
    return want


def _mm(x, w, *, name, layer=None, trans_w=False, norm_g=None, res=None, scale=None, out_dtype=F32, bwd=None,
        split_out=False, split_x=False, emit_norm=False, tm=512):
    T = x.shape[-2]
    K = 2 * x.shape[-1] if split_x else x.shape[-1]
    N = w.shape[-2] if trans_w else w.shape[-1]
    tn = N
    tm = _row_tile(T, 256 if N > 4096 else tm)
    nn, nm = N // tn, T // tm
    has_norm, has_res, has_bwd = norm_g is not None, res is not None, bwd is not None
    dims = (((1,), (1,)), ((), ())) if trans_w else (((1,), (0,)), ((), ()))

    def body(*refs):
        it = iter(refs)
        x_ref, w_ref = next(it), next(it)
        g_ref = next(it) if has_norm else None
        res_ref = next(it) if has_res else None
        if has_bwd:
            h_ref, bg_ref, dh_ref = next(it), next(it), next(it)
        o_ref = next(it)
        if split_x:
            kh = K // 2
            acc = lax.dot_general(x_ref[0].astype(BF16), w_ref[:, :kh] if trans_w else w_ref[:kh, :], dims,
                                  preferred_element_type=F32)
            acc = acc + lax.dot_general(x_ref[1].astype(BF16), w_ref[:, kh:] if trans_w else w_ref[kh:, :], dims,
                                        preferred_element_type=F32)
        else:
            xv = x_ref[...]
            if has_norm:
                xv = _rms(xv.astype(F32))[0] * g_ref[...]
            xb = xv.astype(BF16)
            if emit_norm:
                refs[-1][...] = xb
            acc = lax.dot_general(xb, w_ref[...], dims, preferred_element_type=F32)
        if scale is not None:
            acc = acc * scale
        if has_res:
            acc = acc + res_ref[...]
        if has_bwd:
            dg_ref = next(it)
            n, r = _rms(h_ref[...])

            @pl.when(pl.program_id(1) == 0)
            def _():
                dg_ref[...] = jnp.zeros_like(dg_ref)

            dg_ref[...] += jnp.sum(acc * n, axis=0, keepdims=True)
            t = acc * bg_ref[...]
            o_ref[...] = dh_ref[...] + r * (t - n * jnp.mean(t * n, axis=-1, keepdims=True))
        elif split_out:
            o_ref[0] = acc[:, :N // 2].astype(out_dtype)
            o_ref[1] = acc[:, N // 2:].astype(out_dtype)
        else:
            o_ref[...] = acc.astype(out_dtype)

    lead = () if layer is None else (None,)
    lidx = () if layer is None else (layer,)
    ins = [x, w]
    xspec = (pl.BlockSpec((2, tm, K // 2), lambda n, m: (0, m, 0)) if split_x
             else pl.BlockSpec((tm, K), lambda n, m: (m, 0)))
    wspec = (pl.BlockSpec(lead + (tn, K), lambda n, m: lidx + (n, 0)) if trans_w
             else pl.BlockSpec(lead + (K, tn), lambda n, m: lidx + (0, n)))
    in_specs = [xspec, wspec]
    if has_norm:
        ins.append(norm_g.reshape(1, K))
        in_specs.append(pl.BlockSpec((1, K), lambda n, m: (0, 0)))
    if has_res:
        ins.append(res)
        in_specs.append(pl.BlockSpec((tm, tn), lambda n, m: (m, n)))
    if split_out:
        out_shape = [jax.ShapeDtypeStruct((2, T, N // 2), out_dtype)]
        out_specs = [pl.BlockSpec((2, tm, N // 2), lambda n, m: (0, m, 0))]
    else:
        out_shape = [jax.ShapeDtypeStruct((T, N), F32 if has_bwd else out_dtype)]
        out_specs = [pl.BlockSpec((tm, tn), lambda n, m: (m, n))]
    if has_bwd:
        h, g, dh = bwd
        ins += [h, g.reshape(1, N), dh]
        in_specs += [pl.BlockSpec((tm, N), lambda n, m: (m, 0)), pl.BlockSpec((1, N), lambda n, m: (0, 0)),
                     pl.BlockSpec((tm, N), lambda n, m: (m, 0))]
        out_shape.append(jax.ShapeDtypeStruct((1, N), F32))
        out_specs.append(pl.BlockSpec((1, N), lambda n, m: (0, 0)))
    if emit_norm:
        out_shape.append(jax.ShapeDtypeStruct((T, K), BF16))
        out_specs.append(pl.BlockSpec((tm, K), lambda n, m: (m, 0)))
    out = pl.pallas_call(body, name=name, grid=(nn, nm), in_specs=in_specs, out_specs=out_specs, out_shape=out_shape,
                         compiler_params=_params(("arbitrary", "arbitrary")))(*ins)
    return out if has_bwd or emit_norm else out[0]


def _mm_tn(x, dy, *, name, rows_are_shards=False, split_y=False, tt=512):
    T, K = x.shape
    N = 2 * dy.shape[-1] if split_y else dy.shape[-1]
    R, C = (K // N_CHIPS, N // 2) if rows_are_shards else (K // 2, N // N_CHIPS)
    nn = 2 if split_y else 1
    tn = N // nn
    per = N_CHIPS // nn
    assert not (rows_are_shards and split_y)
    tt = _row_tile(T, tt)
    nt = T // tt

    def body(x_ref, y_ref, o_ref, acc_ref):
        t = pl.program_id(1)

        @pl.when(t == 0)
        def _():
            acc_ref[...] = jnp.zeros_like(acc_ref)

        acc_ref[...] += lax.dot_general(x_ref[...], y_ref[...].astype(BF16), (((0,), (0,)), ((), ())),
                                        preferred_element_type=F32)

        @pl.when(t == nt - 1)
        def _():
            if rows_are_shards:
                for h in range(2):
                    o_ref[h] = acc_ref[:, h * C:(h + 1) * C].astype(BF16).reshape(N_CHIPS, R, C)
            else:
                for j in range(per):
                    o_ref[:, j] = acc_ref[:, j * C:(j + 1) * C].astype(BF16).reshape(2, R, C)

    if split_y:
        yspec = pl.BlockSpec((None, tt, tn), lambda n, t: (n, t, 0))
    else:
        yspec = pl.BlockSpec((tt, tn), lambda n, t: (t, 0))
    if rows_are_shards:
        out_spec = pl.BlockSpec((2, N_CHIPS, R, C), lambda n, t: (0, 0, 0, 0))
    else:
        out_spec = pl.BlockSpec((2, per, R, C), lambda n, t: (0, n, 0, 0))
    return pl.pallas_call(body, name=name, grid=(nn, nt),
                          in_specs=[pl.BlockSpec((tt, K), lambda n, t: (t, 0)), yspec], out_specs=out_spec,
                          out_shape=jax.ShapeDtypeStruct((2, N_CHIPS, R, C), BF16),
                          scratch_shapes=[pltpu.VMEM((K, tn), F32)],
                          compiler_params=_params(("arbitrary", "arbitrary")))(x, dy)


def _chunk_mask():
    i = lax.broadcasted_iota(jnp.int32, (GMLP_BLOCK, GMLP_BLOCK), 0) // CHUNK
    j = lax.broadcasted_iota(jnp.int32, (GMLP_BLOCK, GMLP_BLOCK), 1) // CHUNK
    return i >= j


def _gate_fwd(zp, gv, ws, bs_tile, *, tm=256):
    T, W2 = zp.shape
    W = W2 // 2
    G = W // GROUP_DIM
    tm = _row_tile(T, tm)

    def body(zp_ref, gv_ref, ws_ref, bs_ref, o_ref):
        z = _gelu(zp_ref[...])
        u, v = z[:, :W], z[:, W:]
        vn = _rms(v)[0] * gv_ref[...]
        mask = _chunk_mask()
        for g in range(G):
            cs = slice(g * GROUP_DIM, (g + 1) * GROUP_DIM)
            wg = jnp.where(mask, ws_ref[g], 0.0).astype(BF16)
            for b in range(tm // GMLP_BLOCK):
                rs = slice(b * GMLP_BLOCK, (b + 1) * GMLP_BLOCK)
                s = jnp.dot(wg, vn[rs, cs].astype(BF16), preferred_element_type=F32) + bs_ref[:, cs]
                o_ref[rs, cs] = (u[rs, cs] * s).astype(BF16)

    return pl.pallas_call(
        body, name="gate_fwd", grid=(T // tm,),
        in_specs=[pl.BlockSpec((tm, W2), lambda i: (i, 0)), pl.BlockSpec((1, W), lambda i: (0, 0)),
                  pl.BlockSpec((G, GMLP_BLOCK, GMLP_BLOCK), lambda i: (0, 0, 0)),
                  pl.BlockSpec((GMLP_BLOCK, W), lambda i: (0, 0))],
        out_specs=pl.BlockSpec((tm, W), lambda i: (i, 0)), out_shape=jax.ShapeDtypeStruct((T, W), BF16),
        compiler_params=_params(("arbitrary",)))(zp, gv, ws, bs_tile)


def _gate_bwd(zp, d_out, gv, ws, bs_tile, *, tm=256):
    T, W2 = zp.shape
    W = W2 // 2
    G = W // GROUP_DIM
    tm = _row_tile(T, tm)
    nm = T // tm

    def body(zp_ref, do_ref, gv_ref, ws_ref, bs_ref, dzp_ref, dws_ref, dbs_ref, dgv_ref, du_scr, dvn_scr, dsum_scr):
        i = pl.program_id(0)

        @pl.when(i == 0)
        def _():
            dws_ref[...] = jnp.zeros_like(dws_ref)
            dgv_ref[...] = jnp.zeros_like(dgv_ref)
            dsum_scr[...] = jnp.zeros_like(dsum_scr)

        zp = zp_ref[...]
        z = _gelu(zp)
        u, v = z[:, :W], z[:, W:]
        n, r = _rms(v)
        gv = gv_ref[...]
        vn = n * gv
        d_out = do_ref[...].astype(F32)
        mask = _chunk_mask()
        for g in range(G):
            cs = slice(g * GROUP_DIM, (g + 1) * GROUP_DIM)
            wg = jnp.where(mask, ws_ref[g], 0.0).astype(BF16)
            dw = jnp.zeros((GMLP_BLOCK, GMLP_BLOCK), F32)
            for b in range(tm // GMLP_BLOCK):
                rs = slice(b * GMLP_BLOCK, (b + 1) * GMLP_BLOCK)
                vb = vn[rs, cs].astype(BF16)
                s = jnp.dot(wg, vb, preferred_element_type=F32) + bs_ref[:, cs]
                du_scr[rs, cs] = d_out[rs, cs] * s
                ds = d_out[rs, cs] * u[rs, cs]
                dsb = ds.astype(BF16)
                dvn_scr[rs, cs] = lax.dot_general(wg, dsb, (((0,), (0,)), ((), ())), preferred_element_type=F32)
                dw = dw + lax.dot_general(dsb, vb, (((1,), (1,)), ((), ())), preferred_element_type=F32)
                dsum_scr[:, cs] += ds
            dws_ref[g] += jnp.where(mask, dw, 0.0)
        dvn = dvn_scr[...]
        dgv_ref[...] += jnp.sum(dvn * n, axis=0, keepdims=True)
        t = dvn * gv
        dv = r * (t - n * jnp.mean(t * n, axis=-1, keepdims=True))
        dzp_ref[:, :W] = (du_scr[...] * _gelu_grad(zp[:, :W])).astype(BF16)
        dzp_ref[:, W:] = (dv * _gelu_grad(zp[:, W:])).astype(BF16)

        @pl.when(i == nm - 1)
        def _():
            sel = (lax.broadcasted_iota(jnp.int32, (G, W), 1) // GROUP_DIM
                   == lax.broadcasted_iota(jnp.int32, (G, W), 0)).astype(F32)
            dbs_ref[...] = lax.dot_general(sel, dsum_scr[...], (((1,), (1,)), ((), ())),
                                           precision=lax.Precision.HIGHEST, preferred_element_type=F32)

    return pl.pallas_call(
        body, name="gate_bwd", grid=(nm,),
        in_specs=[pl.BlockSpec((tm, W2), lambda i: (i, 0)), pl.BlockSpec((tm, W), lambda i: (i, 0)),
                  pl.BlockSpec((1, W), lambda i: (0, 0)),
                  pl.BlockSpec((G, GMLP_BLOCK, GMLP_BLOCK), lambda i: (0, 0, 0)),
                  pl.BlockSpec((GMLP_BLOCK, W), lambda i: (0, 0))],
        out_specs=[pl.BlockSpec((tm, W2), lambda i: (i, 0)),
                   pl.BlockSpec((G, GMLP_BLOCK, GMLP_BLOCK), lambda i: (0, 0, 0)),
                   pl.BlockSpec((G, GMLP_BLOCK), lambda i: (0, 0)), pl.BlockSpec((1, W), lambda i: (0, 0))],
        out_shape=[jax.ShapeDtypeStruct((T, W2), BF16), jax.ShapeDtypeStruct((G, GMLP_BLOCK, GMLP_BLOCK), F32),
                   jax.ShapeDtypeStruct((G, GMLP_BLOCK), F32), jax.ShapeDtypeStruct((1, W), F32)],
        scratch_shapes=[pltpu.VMEM((tm, W), F32), pltpu.VMEM((tm, W), F32), pltpu.VMEM((GMLP_BLOCK, W), F32)],
        compiler_params=_params(("arbitrary",)))(zp, d_out, gv, ws, bs_tile)


HALO = 16


def _taps(ext, w, b):
    a, a1, a2 = ext[HALO:], pltpu.roll(ext, 1, 0)[HALO:], pltpu.roll(ext, 2, 0)[HALO:]
    return w[2:3] * a + w[1:2] * a1 + w[0:1] * a2 + b, a, a1, a2


def _conv_fwd(a, cw, cb, S, *, tm=256):
    _, T, F = a.shape
    tc = _col_tile(F)
    tm = _row_tile(S, tm)
    hb = tm // HALO

    def body(a_ref, p_ref, w_ref, b_ref, o_ref, c_ref):
        first = (pl.program_id(1) * tm) % S == 0
        keep = jnp.where(first, 0.0, 1.0)

        def conv(s):
            ext = jnp.concatenate([p_ref[s].astype(F32) * keep, a_ref[s].astype(F32)], axis=0)
            c = _taps(ext, w_ref[s], b_ref[s:s + 1, :])[0].astype(BF16)
            c_ref[s] = c
            return c.astype(F32)

        up, gate = conv(0), conv(1)
        o_ref[...] = (gate * jax.nn.sigmoid(gate) * up).astype(BF16)

    return pl.pallas_call(
        body, name="conv_fwd", grid=(F // tc, T // tm),
        in_specs=[pl.BlockSpec((2, tm, tc), lambda j, i: (0, i, j)),
                  pl.BlockSpec((2, HALO, tc), lambda j, i: (0, jnp.maximum(i * hb - 1, 0), j)),
                  pl.BlockSpec((2, 3, tc), lambda j, i: (0, 0, j)), pl.BlockSpec((2, tc), lambda j, i: (0, j))],
        out_specs=[pl.BlockSpec((tm, tc), lambda j, i: (i, j)), pl.BlockSpec((2, tm, tc), lambda j, i: (0, i, j))],
        out_shape=[jax.ShapeDtypeStruct((T, F), BF16), jax.ShapeDtypeStruct((2, T, F), BF16)],
        compiler_params=_params(("arbitrary", "arbitrary")))(a, a, cw, cb)


def _ffn_in_conv(h, w, g, cw, cb, S, *, name, tm=256):
    T, D = h.shape
    F = w.shape[-1] // 2
    tc = _col_tile(F)
    tm = _row_tile(S, tm)

    def body(h_ref, w_ref, g_ref, cw_ref, cb_ref, y_ref, a_ref, c_ref, n_ref, tail):
        first = (pl.program_id(0) * tm) % S == 0
        nb = (_rms(h_ref[...])[0] * g_ref[...]).astype(BF16)
        n_ref[...] = nb
        for j in range(F // tc):
            cs = slice(j * tc, (j + 1) * tc)
            conv = []
            for s in range(2):
                acc = jnp.dot(nb, w_ref[:, s * F + j * tc:s * F + (j + 1) * tc], preferred_element_type=F32)
                ab = acc.astype(BF16)
                a_ref[s, :, cs] = ab
                af = ab.astype(F32)
                ext = jnp.concatenate([jnp.where(first, 0.0, tail[s, :, cs]), af], axis=0)
                tail[s, :, cs] = af[tm - HALO:, :]
                cv = _taps(ext, cw_ref[s, :, cs], cb_ref[s:s + 1, cs])[0].astype(BF16)
                c_ref[s, :, cs] = cv
                conv.append(cv.astype(F32))
            up, gate = conv
            y_ref[:, cs] = (gate * jax.nn.sigmoid(gate) * up).astype(BF16)

    row = lambda width: pl.BlockSpec((tm, width), lambda i: (i, 0))
    wide = pl.BlockSpec((2, tm, F), lambda i: (0, i, 0))
    return pl.pallas_call(
        body, name=name, grid=(T // tm,),
        in_specs=[row(D), pl.BlockSpec((None, D, 2 * F), lambda i: (0, 0, 0)), pl.BlockSpec((1, D), lambda i: (0, 0)),
                  pl.BlockSpec((2, 3, F), lambda i: (0, 0, 0)), pl.BlockSpec((2, F), lambda i: (0, 0))],
        out_specs=[row(F), wide, wide, row(D)],
        out_shape=[jax.ShapeDtypeStruct((T, F), BF16), jax.ShapeDtypeStruct((2, T, F), BF16),
                   jax.ShapeDtypeStruct((2, T, F), BF16), jax.ShapeDtypeStruct((T, D), BF16)],
        scratch_shapes=[pltpu.VMEM((2, HALO, F), F32)],
        compiler_params=_params(("arbitrary",)))(h, w, g.reshape(1, D), cw, cb)


def _conv_bwd(a, c, dy, cw, S, *, tm=256):
    _, T, F = a.shape
    tc = _col_tile(F)
    tm = _row_tile(S, tm)
    nm = T // tm
    hb = tm // HALO
    TE = tm + HALO
    nxt = lambda j, i: jnp.minimum((i + 1) * hb, T // HALO - 1)

    def body(a_ref, c_ref, nc_ref, dy_ref, ndy_ref, w_ref, da_ref, dw_ref, db_ref):
        i = pl.program_id(1)
        last = ((i + 1) * tm) % S == 0
        keep_n = jnp.where(last, 0.0, 1.0)
        dyf = jnp.concatenate([dy_ref[...].astype(F32), ndy_ref[...].astype(F32) * keep_n], axis=0)
        up = jnp.concatenate([c_ref[0].astype(F32), nc_ref[0].astype(F32)], axis=0)
        gate = jnp.concatenate([c_ref[1].astype(F32), nc_ref[1].astype(F32)], axis=0)
        sg = jax.nn.sigmoid(gate)
        d_up = dyf * (gate * sg)
        d_gate = dyf * up * (sg * (1.0 + gate * (1.0 - sg)))

        @pl.when(i == 0)
        def _():
            dw_ref[...] = jnp.zeros_like(dw_ref)
            db_ref[...] = jnp.zeros_like(db_ref)

        def back(s, d):
            a = a_ref[s].astype(F32)
            w = w_ref[s]
            u1, u2 = pltpu.roll(d, TE - 1, 0), pltpu.roll(d, TE - 2, 0)
            db_ref[s:s + 1, :] += jnp.sum(d[:tm], axis=0, keepdims=True)
            dw_ref[s, 2:3, :] += jnp.sum(d[:tm] * a, axis=0, keepdims=True)
            dw_ref[s, 1:2, :] += jnp.sum(u1[:tm] * a, axis=0, keepdims=True)
            dw_ref[s, 0:1, :] += jnp.sum(u2[:tm] * a, axis=0, keepdims=True)
            da_ref[s] = (w[2:3] * d + w[1:2] * u1 + w[0:1] * u2)[:tm].astype(BF16)

        back(0, d_up)
        back(1, d_gate)

    cur = pl.BlockSpec((2, tm, tc), lambda j, i: (0, i, j))
    return pl.pallas_call(
        body, name="conv_bwd", grid=(F // tc, nm),
        in_specs=[cur, cur, pl.BlockSpec((2, HALO, tc), lambda j, i: (0, nxt(j, i), j)),
                  pl.BlockSpec((tm, tc), lambda j, i: (i, j)), pl.BlockSpec((HALO, tc), lambda j, i: (nxt(j, i), j)),
                  pl.BlockSpec((2, 3, tc), lambda j, i: (0, 0, j))],
        out_specs=[cur, pl.BlockSpec((2, 3, tc), lambda j, i: (0, 0, j)), pl.BlockSpec((2, tc), lambda j, i: (0, j))],
        out_shape=[jax.ShapeDtypeStruct((2, T, F), BF16), jax.ShapeDtypeStruct((2, 3, F), F32),
                   jax.ShapeDtypeStruct((2, F), F32)],
        compiler_params=_params(("arbitrary", "arbitrary")))(a, c, c, dy, dy, cw)


def _bias_index():
    idx = np.arange(F_LEN)
    d = np.where(idx < K_SPAN, idx, idx - F_LEN)
    return np.clip(PAD - d, -REL_CLIP, REL_CLIP) + REL_CLIP


def _roll_rows(x, sign):
    rows = lax.broadcasted_iota(jnp.int32, x.shape, 0)
    step = 1
    while step < Q_BLOCK:
        shift = step if sign > 0 else F_LEN - step
        x = jnp.where((rows & step) != 0, pltpu.roll(x, shift, 1), x)
        step *= 2
    return x


def _bias_expand(frow):
    H = frow.shape[0]

    def body(f_ref, o_ref):
        x = _roll_rows(jnp.broadcast_to(f_ref[...], (Q_BLOCK, F_LEN)), 1)[:, :K_SPAN]
        qc = lax.broadcasted_iota(jnp.int32, (Q_BLOCK, K_SPAN), 0) // CHUNK * CHUNK
        kj = lax.broadcasted_iota(jnp.int32, (Q_BLOCK, K_SPAN), 1)
        o_ref[...] = jnp.where((kj >= qc) & (kj < qc + PAD + CHUNK), x, NEG_INF)

    return pl.pallas_call(
        body, name="bias_expand", grid=(H,),
        in_specs=[pl.BlockSpec((None, 1, F_LEN), lambda h: (h, 0, 0))],
        out_specs=pl.BlockSpec((None, Q_BLOCK, K_SPAN), lambda h: (h, 0, 0)),
        out_shape=jax.ShapeDtypeStruct((H, Q_BLOCK, K_SPAN), F32), compiler_params=_params(("arbitrary",)))(frow)


def _bias_reduce(dbias, n_rel):
    H = dbias.shape[0]
    onehot = jnp.asarray((_bias_index()[:, None] == np.arange(n_rel)[None, :]).astype(np.float32))

    def body(d_ref, oh_ref, o_ref):
        x = jnp.concatenate([d_ref[...], jnp.zeros((Q_BLOCK, F_LEN - K_SPAN), F32)], axis=1)
        row = jnp.sum(_roll_rows(x, -1), axis=0, keepdims=True)
        row8 = jnp.broadcast_to(row, (8, F_LEN))
        o_ref[...] = jnp.dot(row8, oh_ref[...], precision=lax.Precision.HIGHEST, preferred_element_type=F32)[0:1]

    return pl.pallas_call(
        body, name="bias_reduce", grid=(H,),
        in_specs=[pl.BlockSpec((None, Q_BLOCK, K_SPAN), lambda h: (h, 0, 0)),
                  pl.BlockSpec((F_LEN, n_rel), lambda h: (0, 0))],
        out_specs=pl.BlockSpec((None, 1, n_rel), lambda h: (h, 0, 0)),
        out_shape=jax.ShapeDtypeStruct((H, 1, n_rel), F32), compiler_params=_params(("arbitrary",)))(dbias, onehot)


def _attn_specs(S):
    hw = HEADS_PER_STEP * HEAD_DIM
    qspec = pl.BlockSpec((None, Q_BLOCK, hw), lambda g, b, i: (b, i, g))
    kspec = pl.BlockSpec((None, None, S, hw), lambda g, b, i: (0, b, 0, g))
    vspec = pl.BlockSpec((None, None, S, hw), lambda g, b, i: (1, b, 0, g))
    bspec = pl.BlockSpec((HEADS_PER_STEP, Q_BLOCK, K_SPAN), lambda g, b, i: (g, 0, 0))
    return hw, qspec, kspec, vspec, bspec


def _load_padded(k_ref, v_ref, kp, vp):
    kp[:PAD, :] = jnp.zeros((PAD, kp.shape[1]), BF16)
    vp[:PAD, :] = jnp.zeros((PAD, vp.shape[1]), BF16)
    kp[PAD:, :] = k_ref[...]
    vp[PAD:, :] = v_ref[...]


def _attn_exp(q_ref, kp, b_ref, h, q0, before):
    hs = slice(h * HEAD_DIM, (h + 1) * HEAD_DIM)
    kh = kp[pl.ds(q0, K_SPAN), hs]
    s = lax.dot_general(q_ref[:, hs], kh, (((1,), (1,)), ((), ())), preferred_element_type=F32) + b_ref[h] + before
    p = jnp.exp(s - jnp.max(s, axis=-1, keepdims=True))
    return p, 1.0 / jnp.sum(p, axis=-1, keepdims=True), kh


def _before_start(q0):
    kj = lax.broadcasted_iota(jnp.int32, (1, K_SPAN), 1)
    return jnp.where(q0 + kj >= PAD, 0.0, NEG_INF)


def _attn_fwd(q, kv, bias, B, S):
    HD = q.shape[-1]
    hw, qspec, kspec, vspec, bspec = _attn_specs(S)

    def body(q_ref, k_ref, v_ref, b_ref, o_ref, kp, vp):
        i = pl.program_id(2)

        @pl.when(i == 0)
        def _():
            _load_padded(k_ref, v_ref, kp, vp)

        q0 = pl.multiple_of(i * Q_BLOCK, Q_BLOCK)
        before = _before_start(q0)
        outs = []
        for h in range(HEADS_PER_STEP):
            hs = slice(h * HEAD_DIM, (h + 1) * HEAD_DIM)
            p, inv, _ = _attn_exp(q_ref, kp, b_ref, h, q0, before)
            outs.append(jnp.dot(p.astype(BF16), vp[pl.ds(q0, K_SPAN), hs], preferred_element_type=F32) * inv)
        o_ref[...] = jnp.concatenate(outs, axis=1).astype(BF16)

    return pl.pallas_call(
        body, name="attn_fwd", grid=(HD // hw, B, S // Q_BLOCK), in_specs=[qspec, kspec, vspec, bspec],
        out_specs=qspec, out_shape=jax.ShapeDtypeStruct((B, S, HD), BF16),
        scratch_shapes=[pltpu.VMEM((S + PAD, hw), BF16), pltpu.VMEM((S + PAD, hw), BF16)],
        compiler_params=_params(("arbitrary", "arbitrary", "arbitrary")))(q, kv, kv, bias)


def _attn_bwd(q, kv, bias, do, B, S):
    HD = q.shape[-1]
    H = HD // HEAD_DIM
    hw, qspec, kspec, vspec, bspec = _attn_specs(S)
    scale = HEAD_DIM ** -0.5
    nq = S // Q_BLOCK

    def body(q_ref, k_ref, v_ref, b_ref, do_ref, dq_ref, dkv_ref, db_ref, kp, vp, dk_acc, dv_acc):
        b, i = pl.program_id(1), pl.program_id(2)
        q0 = pl.multiple_of(i * Q_BLOCK, Q_BLOCK)

        @pl.when(i == 0)
        def _():
            _load_padded(k_ref, v_ref, kp, vp)
            dk_acc[...] = jnp.zeros_like(dk_acc)
            dv_acc[...] = jnp.zeros_like(dv_acc)

        @pl.when((i == 0) & (b == 0))
        def _():
            db_ref[...] = jnp.zeros_like(db_ref)

        before = _before_start(q0)
        for h in range(HEADS_PER_STEP):
            hs = slice(h * HEAD_DIM, (h + 1) * HEAD_DIM)
            p, inv, kh = _attn_exp(q_ref, kp, b_ref, h, q0, before)
            p = p * inv
            doh = do_ref[:, hs]
            dp = lax.dot_general(doh, vp[pl.ds(q0, K_SPAN), hs], (((1,), (1,)), ((), ())),
                                 preferred_element_type=F32)
            ds = p * (dp - jnp.sum(p * dp, axis=-1, keepdims=True))
            db_ref[h] += ds
            dsb = ds.astype(BF16)
            dq_ref[:, hs] = (jnp.dot(dsb, kh, preferred_element_type=F32) * scale).astype(BF16)
            dk_acc[pl.ds(q0, K_SPAN), hs] += lax.dot_general(dsb, q_ref[:, hs], (((0,), (0,)), ((), ())),
                                                              preferred_element_type=F32)
            dv_acc[pl.ds(q0, K_SPAN), hs] += lax.dot_general(p.astype(BF16), doh, (((0,), (0,)), ((), ())),
                                                              preferred_element_type=F32)

        @pl.when(i == nq - 1)
        def _():
            dkv_ref[0] = dk_acc[PAD:, :].astype(BF16)
            dkv_ref[1] = dv_acc[PAD:, :].astype(BF16)

    return pl.pallas_call(
        body, name="attn_bwd", grid=(HD // hw, B, nq), in_specs=[qspec, kspec, vspec, bspec, qspec],
        out_specs=[qspec, pl.BlockSpec((2, None, S, hw), lambda g, b, i: (0, b, 0, g)), bspec],
        out_shape=[jax.ShapeDtypeStruct((B, S, HD), BF16), jax.ShapeDtypeStruct((2, B, S, HD), BF16),
                   jax.ShapeDtypeStruct((H, Q_BLOCK, K_SPAN), F32)],
        scratch_shapes=[pltpu.VMEM((S + PAD, hw), BF16), pltpu.VMEM((S + PAD, hw), BF16),
                        pltpu.VMEM((S + PAD, hw), F32), pltpu.VMEM((S + PAD, hw), F32)],
        compiler_params=_params(("arbitrary", "arbitrary", "arbitrary")))(q, kv, kv, bias, do)


def _loss_head(h, g, target, *, tm=512):
    T, D = h.shape
    tm = _row_tile(T, tm)

    def body(h_ref, g_ref, t_ref, dh_ref, loss_ref, dg_ref):
        @pl.when(pl.program_id(0) == 0)
        def _():
            loss_ref[...] = jnp.zeros_like(loss_ref)
            dg_ref[...] = jnp.zeros_like(dg_ref)

        n, r = _rms(h_ref[...])
        g = g_ref[...]
        e = n * g - t_ref[...]
        loss_ref[...] += 0.5 * jnp.sum(jnp.mean(e * e, axis=-1, keepdims=True), axis=0, keepdims=True)
        dy = e * (1.0 / D)
        dg_ref[...] += jnp.sum(dy * n, axis=0, keepdims=True)
        t = dy * g
        dh_ref[...] = r * (t - n * jnp.mean(t * n, axis=-1, keepdims=True))

    row = pl.BlockSpec((tm, D), lambda i: (i, 0))
    return pl.pallas_call(
        body, name="loss_head", grid=(T // tm,), in_specs=[row, pl.BlockSpec((1, D), lambda i: (0, 0)), row],
        out_specs=[row, pl.BlockSpec((8, 128), lambda i: (0, 0)), pl.BlockSpec((1, D), lambda i: (0, 0))],
        out_shape=[jax.ShapeDtypeStruct((T, D), F32), jax.ShapeDtypeStruct((8, 128), F32),
                   jax.ShapeDtypeStruct((1, D), F32)],
        compiler_params=_params(("arbitrary",)))(h, g.reshape(1, D), target)


def _sub_rows(R):
    for cand in (256, 352, 128, 64, 8):
        if R % cand == 0 and R > cand:
            return cand
    return R


def _adamw(w, g, m, v, *, name):
    R, C = w.shape
    tr = _sub_rows(R)

    def body(w_ref, g_ref, m_ref, v_ref, d_ref, nm_ref, nv_ref):
        g = g_ref[...]
        m = ADAM_B1 * m_ref[...] + (1.0 - ADAM_B1) * g
        v = ADAM_B2 * v_ref[...] + (1.0 - ADAM_B2) * (g * g)
        m_hat = m / (1.0 - ADAM_B1 ** ADAM_STEP)
        v_hat = v / (1.0 - ADAM_B2 ** ADAM_STEP)
        d_ref[...] = -ADAM_LR * (m_hat / (jnp.sqrt(v_hat) + ADAM_EPS) + ADAM_WD * w_ref[...])
        nm_ref[...] = m
        nv_ref[...] = v

    spec = pl.BlockSpec((tr, C), lambda i: (i, 0))
    return pl.pallas_call(body, name=name, grid=(R // tr,), in_specs=[spec] * 4, out_specs=[spec] * 3,
                          out_shape=[jax.ShapeDtypeStruct((R, C), F32)] * 3,
                          compiler_params=_params(("arbitrary",)))(w, g, m, v)


def _add_pair(units, got, core, *, name):
    n4, R, C = got.shape
    rows = n4 * R
    tr = 512 if rows % 512 == 0 else R

    def body(c_ref, u_ref, got_ref, o_ref):
        o_ref[...] = (u_ref[...].astype(F32) + got_ref[...].astype(F32)).astype(BF16)

    spec = pl.BlockSpec((tr, C), lambda i, c: (i, 0))
    grid_spec = pltpu.PrefetchScalarGridSpec(
        num_scalar_prefetch=1, grid=(rows // tr,),
        in_specs=[pl.BlockSpec((None, tr, C), lambda i, c: (c[0], i, 0)), spec], out_specs=spec)
    out = pl.pallas_call(body, name=name, grid_spec=grid_spec, out_shape=jax.ShapeDtypeStruct((rows, C), BF16),
                         compiler_params=_params(("arbitrary",)))(core.reshape(1), units.reshape(2, rows, C),
                                                                   got.reshape(rows, C))
    return out.reshape(n4, R, C)


def _sum_chips(w, own, got, pos, *, name, layer=0, into=None):
    _, R, C = own.shape
    tr = _sub_rows(R)
    nr = R // tr

    def body(p_ref, own_ref, got_ref, *rest):
        o_ref = rest[-1]
        o_ref[...] = (own_ref[...].astype(F32) + got_ref[0].astype(F32) + got_ref[1].astype(F32)
                      + got_ref[2].astype(F32))

    if w.row_sharded:
        out_map = lambda i, p: (layer, i, p[1])
    else:
        out_map = lambda i, p: (layer, p[1] * nr + i, 0)
    ins = [pos, own, got]
    in_specs = [pl.BlockSpec((None, tr, C), lambda i, p: (p[0], i, 0)),
                pl.BlockSpec((3, tr, C), lambda i, p: (0, i, 0))]
    alias = {}
    if into is not None:
        ins.append(into)
        in_specs.append(ANY)
        alias = {3: 0}
    grid_spec = pltpu.PrefetchScalarGridSpec(num_scalar_prefetch=1, grid=(nr,), in_specs=in_specs,
                                             out_specs=pl.BlockSpec((None, tr, C), out_map))
    return pl.pallas_call(body, name=name, grid_spec=grid_spec, input_output_aliases=alias,
                          out_shape=jax.ShapeDtypeStruct((w.L, w.ks, w.ns), F32),
                          compiler_params=_params(("arbitrary",)))(*ins)


def _mesh_pos():
    return lax.axis_index("x"), lax.axis_index("y"), lax.axis_index("c")


def _other_chips(x, y):
    return [(1 - x, y), (x, 1 - y), (1 - x, 1 - y)]


ANY = pl.BlockSpec(memory_space=pl.ANY)


class _W:
    def __init__(self, name, shard, row_sharded):
        self.name = name
        self.L, ks, ns = shard.shape
        self.row_sharded = row_sharded
        self.K, self.N = (ks * N_CHIPS, ns) if row_sharded else (ks, ns * N_CHIPS)
        self.ks, self.ns = ks, ns

    def shard_of(self, full, j):
        if self.row_sharded:
            return full.at[:, pl.ds(j * self.ks, self.ks), :]
        return full.at[:, :, pl.ds(j * self.ns, self.ns)]

    def half_of(self, shard, c):
        if self.row_sharded:
            return shard.at[:, :, pl.ds(c * (self.ns // 2), self.ns // 2)]
        return shard.at[:, pl.ds(c * (self.ks // 2), self.ks // 2), :]


HBM = pl.BlockSpec(memory_space=pltpu.HBM)
SEM = pl.BlockSpec(memory_space=pltpu.SEMAPHORE)
IN_FLIGHT = pltpu.SideEffectType.DATAFLOW_SIDE_EFFECTING


def _in_hbm(a):
    return pltpu.with_memory_space_constraint(a, pltpu.HBM)


def _gather_start(ws, shards, after):
    nw = len(ws)

    def body(*refs):
        src, dst = refs[:nw], refs[nw:2 * nw]
        send, recv = refs[2 * nw + 1:3 * nw + 1], refs[3 * nw + 1:4 * nw + 1]
        x, y, c = _mesh_pos()
        me = 2 * x + y
        for i, w in enumerate(ws):
            for f, (px, py) in enumerate(_other_chips(x, y)):
                pltpu.make_async_remote_copy(src_ref=w.half_of(src[i], c), dst_ref=w.half_of(w.shard_of(dst[i], me), c),
                                             send_sem=send[i].at[f], recv_sem=recv[i].at[f], device_id=(px, py, c),
                                             device_id_type=MESH).start()

    fulls = [lax.empty((w.L, w.K, w.N), BF16) for w in ws]
    out = pl.pallas_call(
        body, name="gather_start", in_specs=[HBM] * (2 * nw) + [ANY],
        out_specs=[SEM] * (2 * nw) + [HBM] * (2 * nw),
        out_shape=[pltpu.SemaphoreType.DMA((3,))] * (2 * nw)
        + [pltpu.HBM(s.shape, BF16) for s in shards] + [pltpu.HBM(f.shape, BF16) for f in fulls],
        input_output_aliases={i: 2 * nw + i for i in range(2 * nw)},
        compiler_params=pltpu.CompilerParams(has_side_effects=IN_FLIGHT))(
            *[_in_hbm(s) for s in shards], *[_in_hbm(f) for f in fulls], after)
    return [(out[i], out[nw + i], out[2 * nw + i], out[3 * nw + i]) for i in range(nw)]


def _gather_wait(ws, flight, after, *, name):
    nw = len(ws)

    def body(*refs):
        src, dst = refs[:nw], refs[nw:2 * nw]
        send, recv = refs[2 * nw:3 * nw], refs[3 * nw:4 * nw]
        x, y, c = _mesh_pos()
        for i, w in enumerate(ws):
            for f, (px, py) in enumerate(_other_chips(x, y)):
                landed = w.half_of(w.shard_of(dst[i], 2 * px + py), c)
                cp = pltpu.make_async_remote_copy(src_ref=w.half_of(src[i], c), dst_ref=landed, send_sem=send[i].at[f],
                                                  recv_sem=recv[i].at[f], device_id=(px, py, c), device_id_type=MESH)
                cp.wait_send()
                cp.wait_recv()

    shards, fulls = [fl[2] for fl in flight], [fl[3] for fl in flight]
    out = pl.pallas_call(
        body, name=name, in_specs=[HBM] * (2 * nw) + [SEM] * (2 * nw) + [ANY],
        out_specs=[HBM] * (2 * nw),
        out_shape=[pltpu.HBM(s.shape, BF16) for s in shards] + [pltpu.HBM(f.shape, BF16) for f in fulls],
        input_output_aliases={i: i for i in range(2 * nw)},
        compiler_params=pltpu.CompilerParams(has_side_effects=IN_FLIGHT))(
            *shards, *fulls, *[fl[0] for fl in flight], *[fl[1] for fl in flight], after)
    return out[:nw], out[nw:]


def _gather_finish(ws, shards, fulls, *, name):
    nw = len(ws)

    def body(*refs):
        src, dst, stage = refs[:nw], refs[3 * nw:4 * nw], refs[4 * nw:5 * nw]
        send_sems, recv_sems, load_sems, store_sems = refs[5 * nw:]
        x, y, c = _mesh_pos()
        me = 2 * x + y
        sibling = (x, y, 1 - c)
        chips = _other_chips(x, y)

        def fwd(i, w, f, half):
            px, py = chips[f]
            landed = w.half_of(w.shard_of(dst[i], 2 * px + py), half)
            return pltpu.make_async_remote_copy(src_ref=landed, dst_ref=landed, send_sem=send_sems.at[3 * i + f],
                                                recv_sem=recv_sems.at[3 * i + f], device_id=sibling,
                                                device_id_type=MESH)

        loads = [pltpu.make_async_copy(src[i], stage[i], load_sems.at[i]) for i in range(nw)]
        for cp in loads:
            cp.start()
        sends = [fwd(i, w, f, c) for i, w in enumerate(ws) for f in range(3)]
        for cp in sends:
            cp.start()
        stores = [pltpu.make_async_copy(stage[i], w.shard_of(dst[i], me), store_sems.at[i])
                  for i, w in enumerate(ws)]
        for ld, st in zip(loads, stores):
            ld.wait()
            st.start()
        for i, w in enumerate(ws):
            for f in range(3):
                fwd(i, w, f, 1 - c).wait_recv()
        for cp in sends:
            cp.wait_send()
        for cp in stores:
            cp.wait()

    out = pl.pallas_call(
        body, name=name, in_specs=[ANY] * (2 * nw), out_specs=[ANY] * (2 * nw),
        out_shape=[jax.ShapeDtypeStruct(s.shape, BF16) for s in shards]
        + [jax.ShapeDtypeStruct(f.shape, BF16) for f in fulls],
        input_output_aliases={i: i for i in range(2 * nw)},
        scratch_shapes=[pltpu.VMEM((w.L, w.ks, w.ns), BF16) for w in ws]
        + [pltpu.SemaphoreType.DMA((3 * nw,)), pltpu.SemaphoreType.DMA((3 * nw,)), pltpu.SemaphoreType.DMA((nw,)),
           pltpu.SemaphoreType.DMA((nw,))],
        compiler_params=_params(has_side_effects=True))(*shards, *fulls)
    return out[nw:]


def _swap_units(units, *, name):
    nw = len(units)

    def body(*refs):
        src, got = refs[:nw], refs[nw:2 * nw]
        send_sems, recv_sems = refs[2 * nw:]
        x, y, c = _mesh_pos()
        copies = [pltpu.make_async_remote_copy(src_ref=src[i].at[1 - c], dst_ref=got[i], send_sem=send_sems.at[i],
                                               recv_sem=recv_sems.at[i], device_id=(x, y, 1 - c),
                                               device_id_type=MESH) for i in range(nw)]
        for cp in copies:
            cp.start()
        for cp in copies:
            cp.wait()

    return pl.pallas_call(
        body, name=name, in_specs=[ANY] * nw, out_specs=[ANY] * nw,
        out_shape=[jax.ShapeDtypeStruct(u.shape[1:], BF16) for u in units],
        scratch_shapes=[pltpu.SemaphoreType.DMA((nw,)), pltpu.SemaphoreType.DMA((nw,))],
        compiler_params=_params(has_side_effects=True))(*units)


def _scatter_copy(src, got, send, recv, f, chip, c):
    px, py = chip
    return pltpu.make_async_remote_copy(src_ref=src.at[2 * px + py], dst_ref=got.at[f], send_sem=send.at[f],
                                        recv_sem=recv.at[f], device_id=(px, py, c), device_id_type=MESH)


def _scatter_start(sums, *, name):
    nw = len(sums)

    def body(*refs):
        src, got = refs[:nw], refs[nw:2 * nw]
        send, recv = refs[2 * nw:3 * nw], refs[3 * nw:4 * nw]
        x, y, c = _mesh_pos()
        for i in range(nw):
            for f, chip in enumerate(_other_chips(x, y)):
                _scatter_copy(src[i], got[i], send[i], recv[i], f, chip, c).start()

    lands = [lax.empty((3,) + s.shape[1:], BF16) for s in sums]
    out = pl.pallas_call(
        body, name=name, in_specs=[HBM] * (2 * nw), out_specs=[SEM] * (2 * nw) + [HBM] * (2 * nw),
        out_shape=[pltpu.SemaphoreType.DMA((3,))] * (2 * nw)
        + [pltpu.HBM(s.shape, BF16) for s in sums] + [pltpu.HBM(l.shape, BF16) for l in lands],
        input_output_aliases={i: 2 * nw + i for i in range(2 * nw)},
        compiler_params=pltpu.CompilerParams(has_side_effects=IN_FLIGHT))(
            *[_in_hbm(s) for s in sums], *[_in_hbm(l) for l in lands])
    return [(out[i], out[nw + i], out[2 * nw + i], out[3 * nw + i]) for i in range(nw)]


def _scatter_wait(flight, after):
    nw = len(flight)

    def body(*refs):
        src, got = refs[:nw], refs[nw:2 * nw]
        send, recv = refs[2 * nw:3 * nw], refs[3 * nw:4 * nw]
        x, y, c = _mesh_pos()
        for i in range(nw):
            for f, chip in enumerate(_other_chips(x, y)):
                cp = _scatter_copy(src[i], got[i], send[i], recv[i], f, chip, c)
                cp.wait_send()
                cp.wait_recv()

    sums, lands = [fl[2] for fl in flight], [fl[3] for fl in flight]
    out = pl.pallas_call(
        body, name="scatter_wait", in_specs=[HBM] * (2 * nw) + [SEM] * (2 * nw) + [ANY], out_specs=[HBM] * (2 * nw),
        out_shape=[pltpu.HBM(s.shape, BF16) for s in sums] + [pltpu.HBM(l.shape, BF16) for l in lands],
        input_output_aliases={i: i for i in range(2 * nw)},
        compiler_params=pltpu.CompilerParams(has_side_effects=IN_FLIGHT))(
            *sums, *lands, *[fl[0] for fl in flight], *[fl[1] for fl in flight], after)
    return out[:nw], out[nw:]


def _join_halves(ws, shards):
    nw = len(ws)

    def body(*refs):
        buf = refs[nw:2 * nw]
        send_sems, recv_sems = refs[2 * nw:]
        x, y, c = _mesh_pos()
        sibling = (x, y, 1 - c)

        def copy(i, w, half):
            region = w.half_of(buf[i], half)
            return pltpu.make_async_remote_copy(src_ref=region, dst_ref=region, send_sem=send_sems.at[i],
                                                recv_sem=recv_sems.at[i], device_id=sibling, device_id_type=MESH)

        sends = [copy(i, w, c) for i, w in enumerate(ws)]
        for cp in sends:
            cp.start()
        for i, w in enumerate(ws):
            copy(i, w, 1 - c).wait_recv()
        for cp in sends:
            cp.wait_send()

    return pl.pallas_call(
        body, name="join_halves", in_specs=[ANY] * nw, out_specs=[ANY] * nw,
        out_shape=[jax.ShapeDtypeStruct((w.L, w.ks, w.ns), F32) for w in ws],
        input_output_aliases={i: i for i in range(nw)},
        scratch_shapes=[pltpu.SemaphoreType.DMA((nw,)), pltpu.SemaphoreType.DMA((nw,))],
        compiler_params=_params(has_side_effects=True))(*shards)


def _allreduce_small(vec):
    R = vec.shape[0]

    def body(x_ref, o_ref, buf, send_sems, recv_sems):
        x, y, c = _mesh_pos()
        me, sibling = (x, y, c), (x, y, 1 - c)
        chips = _other_chips(x, y)

        def slot(px, py, pc):
            return buf.at[4 * px + 2 * py + pc]

        def copy(k, block, to, src=None):
            return pltpu.make_async_remote_copy(src_ref=slot(*block) if src is None else src, dst_ref=slot(*block),
                                                send_sem=send_sems.at[k], recv_sem=recv_sems.at[k], device_id=to,
                                                device_id_type=MESH)

        first = [copy(0, me, sibling, src=x_ref)] + [copy(1 + f, me, (*chip, c), src=x_ref)
                                                     for f, chip in enumerate(chips)]
        for cp in first:
            cp.start()
        passed = [copy(4 + f, (*chip, c), sibling) for f, chip in enumerate(chips)]
        for f, chip in enumerate(chips):
            copy(1 + f, (*chip, c), me).wait_recv()
            passed[f].start()
        copy(0, sibling, me).wait_recv()
        for f, chip in enumerate(chips):
            copy(4 + f, (*chip, 1 - c), me).wait_recv()
        for cp in first + passed:
            cp.wait_send()
        slot(*me)[...] = x_ref[...]
        acc = buf[0]
        for d in range(1, 8):
            acc = acc + buf[d]
        o_ref[...] = acc

    return pl.pallas_call(
        body, name="allreduce_small", in_specs=[pl.BlockSpec(memory_space=pltpu.VMEM)],
        out_specs=pl.BlockSpec(memory_space=pltpu.VMEM), out_shape=jax.ShapeDtypeStruct((R, 128), F32),
        scratch_shapes=[pltpu.VMEM((8, R, 128), F32), pltpu.SemaphoreType.DMA((7,)), pltpu.SemaphoreType.DMA((7,))],
        compiler_params=_params())(vec)


def _pack(parts):
    flat = jnp.concatenate([p.reshape(-1).astype(F32) for p in parts])
    n = flat.shape[0]
    pad = (-n) % (64 * 128)
    return jnp.pad(flat, (0, pad)).reshape(-1, 128)


def _unpack(vec, shapes):
    flat = vec.reshape(-1)
    out, off = [], 0
    for s in shapes:
        n = int(np.prod(s))
        out.append(flat[off:off + n].reshape(s))
        off += n
    return out


def kernel(x, a_norm_g, a_w_in, a_v_norm_g, a_w_s, a_b_s, a_w_out, kv_norm_g, w_kv, b_norm_g, b_w_q, b_rel_bias, b_w_o, f_norm_g, f_w_in, f_conv_w, f_conv_b, f_w_down, final_norm_g, loss_target, m_a_norm_g, m_a_w_in, m_a_v_norm_g, m_a_w_s, m_a_b_s, m_a_w_out, m_kv_norm_g, m_w_kv, m_b_norm_g, m_b_w_q, m_b_rel_bias, m_b_w_o, m_f_norm_g, m_f_w_in, m_f_conv_w, m_f_conv_b, m_f_w_down, m_final_norm_g, v_a_norm_g, v_a_w_in, v_a_v_norm_g, v_a_w_s, v_a_b_s, v_a_w_out, v_kv_norm_g, v_w_kv, v_b_norm_g, v_b_w_q, v_b_rel_bias, v_b_w_o, v_f_norm_g, v_f_w_in, v_f_conv_w, v_f_conv_b, v_f_w_down, v_final_norm_g):
    B, S, D = x.shape
    T = B * S
    xi, yi, ci = lax.axis_index("x"), lax.axis_index("y"), lax.axis_index("c")
    j_me = (2 * xi + yi).astype(jnp.int32)
    core = ci.astype(jnp.int32)
    pos = jnp.stack([j_me, core])

    w_shards = {"a_w_in": (a_w_in, False), "a_w_out": (a_w_out, True), "w_kv": (w_kv[None], False),
                "b_w_q": (b_w_q, True), "b_w_o": (b_w_o, True), "f_w_in": (f_w_in, False), "f_w_down": (f_w_down, True)}
    names = list(w_shards)
    ws = [_W(n, w_shards[n][0], w_shards[n][1]) for n in names]
    g_shards = {"a_w_in": (a_w_in, False), "a_w_out": (a_w_out, True),
                "f_w_in0": (f_w_in[0:1], False), "f_w_down0": (f_w_down[0:1], True),
                "w_kv": (w_kv[None], False), "b_w_q": (b_w_q, True), "b_w_o": (b_w_o, True),
                "f_w_in1": (f_w_in[1:2], False), "f_w_down1": (f_w_down[1:2], True)}
    g_names = list(g_shards)
    g_ws = {n: _W(n, *g_shards[n]) for n in g_names}

    Wd = a_w_in.shape[1]
    GW = a_v_norm_g.shape[1] * N_CHIPS
    F2 = f_conv_w.shape[2] * N_CHIPS
    Fh = F2 // 2
    nsd, nsg, nsf = a_norm_g.shape[1], a_v_norm_g.shape[1], f_conv_w.shape[2]
    own = (ci == 0).astype(F32)
    place = lambda sh, width, n: lax.dynamic_update_slice_in_dim(
        jnp.zeros(sh.shape[:-1] + (width,), F32), sh * own, j_me * n, axis=sh.ndim - 1)
    gathered = _allreduce_small(_pack([place(a_norm_g, Wd, nsd), place(a_v_norm_g, GW, nsg),
                                       place(f_conv_w, F2, nsf)]))
    a_g, a_vg, conv_w = _unpack(gathered, [(1, Wd), (1, GW), (2, 3, F2)])

    flight = dict(zip(g_names, _gather_start([g_ws[n] for n in g_names],
                                             [g_shards[n][0].astype(BF16) for n in g_names], gathered)))
    full = {}

    def arrive(group, after, tag):
        gw = [g_ws[n] for n in group]
        sh, fu = _gather_wait(gw, [flight[n] for n in group], after, name=f"gather_wait_{tag}")
        full.update(zip(group, _gather_finish(gw, sh, fu, name=f"gather_finish_{tag}")))
    conv_w2 = conv_w.reshape(2, 3, 2, Fh).transpose(0, 2, 1, 3)
    conv_b2 = f_conv_b.reshape(2, 2, Fh)

    h0 = x.reshape(T, D)
    target = loss_target.reshape(T, D)
    bs_tile = jnp.repeat(a_b_s[0].T, GROUP_DIM, axis=1)
    ws_a = a_w_s[0]
    scale = HEAD_DIM ** -0.5
    HD = b_w_q.shape[2]
    H = HD // HEAD_DIM
    n_rel = b_rel_bias.shape[-1]
    frow = b_rel_bias[0][:, _bias_index()].reshape(H, 1, F_LEN)
    bias = _bias_expand(frow)

    def ffn_fwd(h, l):
        yff, a, c, n = _ffn_in_conv(h, full[f"f_w_in{l}"], f_norm_g[l], conv_w2[l], conv_b2[l], S, name=f"ffn{l}_in")
        return _mm(yff, full[f"f_w_down{l}"], layer=0, res=h, name=f"ffn{l}_down"), (a, c, n, yff)

    arrive(["a_w_in", "a_w_out"], h0, "a")
    zp, n_a = _mm(h0, full["a_w_in"], layer=0, norm_g=a_g[0], emit_norm=True, name="a_in")
    out_a = _gate_fwd(zp, a_vg, ws_a, bs_tile)
    h1 = _mm(out_a, full["a_w_out"], layer=0, res=h0, name="a_out")
    arrive(["f_w_in0", "f_w_down0"], h1, "f0")
    h2, saved0 = ffn_fwd(h1, 0)
    arrive(["w_kv", "b_w_q", "b_w_o"], h2, "b")
    arrive(["f_w_in1", "f_w_down1"], h2, "f1")
    kv, n_kv = _mm(h2, full["w_kv"], layer=0, norm_g=kv_norm_g, out_dtype=BF16, split_out=True, emit_norm=True,
                   name="kv")
    q, n_q = _mm(h2, full["b_w_q"], layer=0, norm_g=b_norm_g[0], scale=scale, out_dtype=BF16, emit_norm=True,
                 name="q")
    kv4, q3 = kv.reshape(2, B, S, HD), q.reshape(B, S, HD)
    o = _attn_fwd(q3, kv4, bias, B, S).reshape(T, HD)
    h3 = _mm(o, full["b_w_o"], layer=0, res=h2, name="attn_out")
    h4, saved1 = ffn_fwd(h3, 1)
    dh, loss8, dg_final = _loss_head(h4, final_norm_g, target)

    units = {}

    in_flight = {}

    def reduce_start(group, tag):
        got = _swap_units([units[n] for n in group], name=f"swap_{tag}")
        sums = [_add_pair(units[n], g_, core, name=f"pair_{n}") for n, g_ in zip(group, got)]
        in_flight.update(zip(group, _scatter_start(sums, name=f"scatter_start_{tag}")))

    def ffn_bwd(dh, h, saved, l, early):
        a, c, n, yff = saved
        units[f"f_w_down{l}"] = _mm_tn(yff, dh, rows_are_shards=True, name=f"ffn{l}_down_dw")
        if early:
            reduce_start([f"f_w_down{l}"], f"fd{l}")
        dyff = _mm(dh, full[f"f_w_down{l}"], layer=0, trans_w=True, out_dtype=BF16, name=f"ffn{l}_down_dx")
        da, dcw, dcb = _conv_bwd(a, c, dyff, conv_w2[l], S)
        units[f"f_w_in{l}"] = _mm_tn(n, da, split_y=True, name=f"ffn{l}_in_dw")
        reduce_start([f"f_w_in{l}"] if early else [f"f_w_down{l}", f"f_w_in{l}"], f"f{l}")
        dh, dg = _mm(da, full[f"f_w_in{l}"], layer=0, trans_w=True, split_x=True, bwd=(h, f_norm_g[l], dh), tm=256,
                     name=f"ffn{l}_in_dx")
        return dh, dg, dcw, dcb

    dh, dg_f1, dcw1, dcb1 = ffn_bwd(dh, h3, saved1, 1, False)
    do = _mm(dh, full["b_w_o"], layer=0, trans_w=True, out_dtype=BF16, name="attn_out_dx")
    units["b_w_o"] = _mm_tn(o, dh, rows_are_shards=True, name="b_w_o_dw")
    dq, dkv, dbias = _attn_bwd(q3, kv4, bias, do.reshape(B, S, HD), B, S)
    d_rel = _bias_reduce(dbias, n_rel).reshape(1, H, n_rel)
    dq, dkv = dq.reshape(T, HD), dkv.reshape(2, T, HD)
    units["b_w_q"] = _mm_tn(n_q, dq, rows_are_shards=True, name="b_w_q_dw")
    dh, dg_b = _mm(dq, full["b_w_q"], layer=0, trans_w=True, bwd=(h2, b_norm_g[0], dh), name="q_dx")
    units["w_kv"] = _mm_tn(n_kv, dkv, split_y=True, name="w_kv_dw")
    dh, dg_kv = _mm(dkv, full["w_kv"], layer=0, trans_w=True, split_x=True, bwd=(h2, kv_norm_g, dh), name="kv_dx")
    reduce_start(["b_w_o", "b_w_q", "w_kv"], "b")
    dh, dg_f0, dcw0, dcb0 = ffn_bwd(dh, h1, saved0, 0, True)
    units["a_w_out"] = _mm_tn(out_a, dh, rows_are_shards=True, name="a_w_out_dw")
    reduce_start(["a_w_out"], "ao")
    d_out = _mm(dh, full["a_w_out"], layer=0, trans_w=True, out_dtype=BF16, name="a_out_dx")
    dzp, dws, dbs, dgv = _gate_bwd(zp, d_out, a_vg, ws_a, bs_tile)
    units["a_w_in"] = _mm_tn(n_a, dzp, name="a_w_in_dw")
    reduce_start(["a_w_in"], "ai")
    grad_x, dg_a = _mm(dzp, full["a_w_in"], layer=0, trans_w=True, bwd=(h0, a_g[0], dh), name="a_in_dx")

    sums, recv = _scatter_wait([in_flight[n] for n in g_names], grad_x)
    sums, recv = dict(zip(g_names, sums)), dict(zip(g_names, recv))
    halves = []
    for n, w in zip(names, ws):
        if w.L == 1:
            halves.append(_sum_chips(w, sums[n], recv[n], pos, name=f"chips_{n}"))
        else:
            first = _sum_chips(w, sums[n + "0"], recv[n + "0"], pos, name=f"chips_{n}0")
            halves.append(_sum_chips(w, sums[n + "1"], recv[n + "1"], pos, layer=1, into=first, name=f"chips_{n}1"))
    g_big = dict(zip(names, _join_halves(ws, halves)))
    g_big["w_kv"] = g_big["w_kv"][0]

    to_flat = lambda d: d.transpose(1, 0, 2).reshape(3, F2)
    small = {"a_norm_g": dg_a, "a_v_norm_g": dgv, "a_w_s": dws[None], "a_b_s": dbs[None], "kv_norm_g": dg_kv[0],
             "b_norm_g": dg_b, "b_rel_bias": d_rel, "f_norm_g": jnp.concatenate([dg_f0, dg_f1], axis=0),
             "f_conv_w": jnp.stack([to_flat(dcw0), to_flat(dcw1)]),
             "f_conv_b": jnp.stack([dcb0.reshape(F2), dcb1.reshape(F2)]), "final_norm_g": dg_final[0]}
    snames = list(small)
    red = _allreduce_small(_pack([small[n] for n in snames] + [loss8[0:1, 0:1]]))
    parts = _unpack(red, [small[n].shape for n in snames] + [(1,)])
    g_small = dict(zip(snames, parts[:-1]))
    loss = parts[-1][0]
    g_small["a_norm_g"] = lax.dynamic_slice_in_dim(g_small["a_norm_g"], j_me * nsd, nsd, axis=1)
    g_small["a_v_norm_g"] = lax.dynamic_slice_in_dim(g_small["a_v_norm_g"], j_me * nsg, nsg, axis=1)
    g_small["f_conv_w"] = lax.dynamic_slice_in_dim(g_small["f_conv_w"], j_me * nsf, nsf, axis=2)

    given = dict(a_norm_g=(a_norm_g, m_a_norm_g, v_a_norm_g), a_w_in=(a_w_in, m_a_w_in, v_a_w_in),
                 a_v_norm_g=(a_v_norm_g, m_a_v_norm_g, v_a_v_norm_g), a_w_s=(a_w_s, m_a_w_s, v_a_w_s),
                 a_b_s=(a_b_s, m_a_b_s, v_a_b_s), a_w_out=(a_w_out, m_a_w_out, v_a_w_out),
                 kv_norm_g=(kv_norm_g, m_kv_norm_g, v_kv_norm_g), w_kv=(w_kv, m_w_kv, v_w_kv),
                 b_norm_g=(b_norm_g, m_b_norm_g, v_b_norm_g), b_w_q=(b_w_q, m_b_w_q, v_b_w_q),
                 b_rel_bias=(b_rel_bias, m_b_rel_bias, v_b_rel_bias), b_w_o=(b_w_o, m_b_w_o, v_b_w_o),
                 f_norm_g=(f_norm_g, m_f_norm_g, v_f_norm_g), f_w_in=(f_w_in, m_f_w_in, v_f_w_in),
                 f_conv_w=(f_conv_w, m_f_conv_w, v_f_conv_w), f_conv_b=(f_conv_b, m_f_conv_b, v_f_conv_b),
                 f_w_down=(f_w_down, m_f_w_down, v_f_w_down), final_norm_g=(final_norm_g, m_final_norm_g, v_final_norm_g))
    order = list(given)
    grads, deltas, new_m, new_v = {}, {}, {}, {}
    for n in names:
        w_, m_, v_ = given[n]
        g_ = g_big[n]
        C = w_.shape[-1]
        d2, m2, v2 = _adamw(w_.reshape(-1, C), g_.reshape(-1, C), m_.reshape(-1, C), v_.reshape(-1, C),
                            name=f"adamw_{n}")
        grads[n], deltas[n], new_m[n], new_v[n] = g_.reshape(w_.shape), d2.reshape(w_.shape), m2.reshape(w_.shape), \
            v2.reshape(w_.shape)
    sm = [n for n in order if n not in names]
    d2, m2, v2 = _adamw(_pack([given[n][0] for n in sm]), _pack([g_small[n].reshape(given[n][0].shape) for n in sm]),
                        _pack([given[n][1] for n in sm]), _pack([given[n][2] for n in sm]), name="adamw_small")
    shapes = [given[n][0].shape for n in sm]
    for n, d_, m_, v_ in zip(sm, _unpack(d2, shapes), _unpack(m2, shapes), _unpack(v2, shapes)):
        grads[n], deltas[n], new_m[n], new_v[n] = g_small[n].reshape(given[n][0].shape), d_, m_, v_

    return (loss, grad_x.reshape(B, S, D), *[grads[n] for n in order], *[deltas[n] for n in order],
            *[new_m[n] for n in order], *[new_v[n] for n in order])
```

```python
import math

import numpy as np
import jax
import jax.numpy as jnp
from jax import lax
from jax.experimental import pallas as pl
from jax.experimental.pallas import tpu as pltpu

F32 = jnp.float32
BF16 = jnp.bfloat16
MESH = pl.DeviceIdType.MESH

EPS = 1e-6
NEG_INF = -1e30
CHUNK = 64
GMLP_BLOCK = 128
GROUP_DIM = 128
HEAD_DIM = 64
LEFT_CHUNKS = 8
PAD = LEFT_CHUNKS * CHUNK
REL_CLIP = 128
Q_BLOCK = 256
K_SPAN = PAD + Q_BLOCK
F_LEN = K_SPAN + Q_BLOCK
HEADS_PER_STEP = 4
N_CHIPS = 4

ADAM_LR = 0.001
ADAM_B1 = 0.9
ADAM_B2 = 0.999
ADAM_EPS = 1e-08
ADAM_WD = 0.01
ADAM_STEP = 10

VMEM_LIMIT = 56 * 1024 * 1024


def _params(sem=None, **kw):
    if sem is not None:
        kw["dimension_semantics"] = sem
    return pltpu.CompilerParams(vmem_limit_bytes=VMEM_LIMIT, **kw)


def _rms(xf):
    r = lax.rsqrt(jnp.mean(xf * xf, axis=-1, keepdims=True) + EPS)
    return xf * r, r


def _gelu(x):
    c = math.sqrt(2.0 / math.pi)
    return 0.5 * x * (1.0 + jnp.tanh(c * (x + 0.044715 * x * x * x)))


def _gelu_grad(x):
    c = math.sqrt(2.0 / math.pi)
    t = jnp.tanh(c * (x + 0.044715 * x * x * x))
    return 0.5 * (1.0 + t) + 0.5 * x * (1.0 - t * t) * c * (1.0 + 3.0 * 0.044715 * x * x)


def _col_tile(n):
    if n <= 1024:
        return n
    for t in (1408, 1024, 512):
        if n % t == 0:
            return t
    raise ValueError(n)


def _row_tile(t, want):
    while t % want:
        want //= 2
    return want


def _mm(x, w, *, name, layer=None, trans_w=False, norm_g=None, res=None, scale=None, out_dtype=F32, bwd=None,
        split_out=False, split_x=False, emit_norm=False, tm=512):
    T = x.shape[-2]
    K = 2 * x.shape[-1] if split_x else x.shape[-1]
    N = w.shape[-2] if trans_w else w.shape[-1]
    tn = N
    tm = _row_tile(T, 256 if N > 4096 else tm)
    nn, nm = N // tn, T // tm
    has_norm, has_res, has_bwd = norm_g is not None, res is not None, bwd is not None
    dims = (((1,), (1,)), ((), ())) if trans_w else (((1,), (0,)), ((), ()))

    def body(*refs):
        it = iter(refs)
        x_ref, w_ref = next(it), next(it)
        g_ref = next(it) if has_norm else None
        res_ref = next(it) if has_res else None
        if has_bwd:
            h_ref, bg_ref, dh_ref = next(it), next(it), next(it)
        o_ref = next(it)
        if split_x:
            kh = K // 2
            acc = lax.dot_general(x_ref[0].astype(BF16), w_ref[:, :kh] if trans_w else w_ref[:kh, :], dims,
                                  preferred_element_type=F32)
            acc = acc + lax.dot_general(x_ref[1].astype(BF16), w_ref[:, kh:] if trans_w else w_ref[kh:, :], dims,
                                        preferred_element_type=F32)
        else:
            xv = x_ref[...]
            if has_norm:
                xv = _rms(xv.astype(F32))[0] * g_ref[...]
            xb = xv.astype(BF16)
            if emit_norm:
                refs[-1][...] = xb
            acc = lax.dot_general(xb, w_ref[...], dims, preferred_element_type=F32)
        if scale is not None:
            acc = acc * scale
        if has_res:
            acc = acc + res_ref[...]
        if has_bwd:
            dg_ref = next(it)
            n, r = _rms(h_ref[...])

            @pl.when(pl.program_id(1) == 0)
            def _():
                dg_ref[...] = jnp.zeros_like(dg_ref)

            dg_ref[...] += jnp.sum(acc * n, axis=0, keepdims=True)
            t = acc * bg_ref[...]
            o_ref[...] = dh_ref[...] + r * (t - n * jnp.mean(t * n, axis=-1, keepdims=True))
        elif split_out:
            o_ref[0] = acc[:, :N // 2].astype(out_dtype)
            o_ref[1] = acc[:, N // 2:].astype(out_dtype)
        else:
            o_ref[...] = acc.astype(out_dtype)

    lead = () if layer is None else (None,)
    lidx = () if layer is None else (layer,)
    ins = [x, w]
    xspec = (pl.BlockSpec((2, tm, K // 2), lambda n, m: (0, m, 0)) if split_x
             else pl.BlockSpec((tm, K), lambda n, m: (m, 0)))
    wspec = (pl.BlockSpec(lead + (tn, K), lambda n, m: lidx + (n, 0)) if trans_w
             else pl.BlockSpec(lead + (K, tn), lambda n, m: lidx + (0, n)))
    in_specs = [xspec, wspec]
    if has_norm:
        ins.append(norm_g.reshape(1, K))
        in_specs.append(pl.BlockSpec((1, K), lambda n, m: (0, 0)))
    if has_res:
        ins.append(res)
        in_specs.append(pl.BlockSpec((tm, tn), lambda n, m: (m, n)))
    if split_out:
        out_shape = [jax.ShapeDtypeStruct((2, T, N // 2), out_dtype)]
        out_specs = [pl.BlockSpec((2, tm, N // 2), lambda n, m: (0, m, 0))]
    else:
        out_shape = [jax.ShapeDtypeStruct((T, N), F32 if has_bwd else out_dtype)]
        out_specs = [pl.BlockSpec((tm, tn), lambda n, m: (m, n))]
    if has_bwd:
        h, g, dh = bwd
        ins += [h, g.reshape(1, N), dh]
        in_specs += [pl.BlockSpec((tm, N), lambda n, m: (m, 0)), pl.BlockSpec((1, N), lambda n, m: (0, 0)),
                     pl.BlockSpec((tm, N), lambda n, m: (m, 0))]
        out_shape.append(jax.ShapeDtypeStruct((1, N), F32))
        out_specs.append(pl.BlockSpec((1, N), lambda n, m: (0, 0)))
    if emit_norm:
        out_shape.append(jax.ShapeDtypeStruct((T, K), BF16))
        out_specs.append(pl.BlockSpec((tm, K), lambda n, m: (m, 0)))
    out = pl.pallas_call(body, name=name, grid=(nn, nm), in_specs=in_specs, out_specs=out_specs, out_shape=out_shape,
                         compiler_params=_params(("arbitrary", "arbitrary")))(*ins)
    return out if has_bwd or emit_norm else out[0]


def _mm_tn(x, dy, *, name, rows_are_shards=False, split_y=False, tt=512):
    T, K = x.shape
    N = 2 * dy.shape[-1] if split_y else dy.shape[-1]
    R, C = (K // N_CHIPS, N // 2) if rows_are_shards else (K // 2, N // N_CHIPS)
    nn = 2 if split_y else 1
    tn = N // nn
    per = N_CHIPS // nn
    assert not (rows_are_shards and split_y)
    tt = _row_tile(T, tt)
    nt = T // tt

    def body(x_ref, y_ref, o_ref, acc_ref):
        t = pl.program_id(1)

        @pl.when(t == 0)
        def _():
            acc_ref[...] = jnp.zeros_like(acc_ref)

        acc_ref[...] += lax.dot_general(x_ref[...], y_ref[...].astype(BF16), (((0,), (0,)), ((), ())),
                                        preferred_element_type=F32)

        @pl.when(t == nt - 1)
        def _():
            if rows_are_shards:
                for h in range(2):
                    o_ref[h] = acc_ref[:, h * C:(h + 1) * C].astype(BF16).reshape(N_CHIPS, R, C)
            else:
                for j in range(per):
                    o_ref[:, j] = acc_ref[:, j * C:(j + 1) * C].astype(BF16).reshape(2, R, C)

    if split_y:
        yspec = pl.BlockSpec((None, tt, tn), lambda n, t: (n, t, 0))
    else:
        yspec = pl.BlockSpec((tt, tn), lambda n, t: (t, 0))
    if rows_are_shards:
        out_spec = pl.BlockSpec((2, N_CHIPS, R, C), lambda n, t: (0, 0, 0, 0))
    else:
        out_spec = pl.BlockSpec((2, per, R, C), lambda n, t: (0, n, 0, 0))
    return pl.pallas_call(body, name=name, grid=(nn, nt),
                          in_specs=[pl.BlockSpec((tt, K), lambda n, t: (t, 0)), yspec], out_specs=out_spec,
                          out_shape=jax.ShapeDtypeStruct((2, N_CHIPS, R, C), BF16),
                          scratch_shapes=[pltpu.VMEM((K, tn), F32)],
                          compiler_params=_params(("arbitrary", "arbitrary")))(x, dy)


def _chunk_mask():
    i = lax.broadcasted_iota(jnp.int32, (GMLP_BLOCK, GMLP_BLOCK), 0) // CHUNK
    j = lax.broadcasted_iota(jnp.int32, (GMLP_BLOCK, GMLP_BLOCK), 1) // CHUNK
    return i >= j


def _gate_fwd(zp, gv, ws, bs_tile, *, tm=256):
    T, W2 = zp.shape
    W = W2 // 2
    G = W // GROUP_DIM
    tm = _row_tile(T, tm)

    def body(zp_ref, gv_ref, ws_ref, bs_ref, o_ref):
        z = _gelu(zp_ref[...])
        u, v = z[:, :W], z[:, W:]
        vn = _rms(v)[0] * gv_ref[...]
        mask = _chunk_mask()
        for g in range(G):
            cs = slice(g * GROUP_DIM, (g + 1) * GROUP_DIM)
            wg = jnp.where(mask, ws_ref[g], 0.0).astype(BF16)
            for b in range(tm // GMLP_BLOCK):
                rs = slice(b * GMLP_BLOCK, (b + 1) * GMLP_BLOCK)
                s = jnp.dot(wg, vn[rs, cs].astype(BF16), preferred_element_type=F32) + bs_ref[:, cs]
                o_ref[rs, cs] = (u[rs, cs] * s).astype(BF16)

    return pl.pallas_call(
        body, name="gate_fwd", grid=(T // tm,),
        in_specs=[pl.BlockSpec((tm, W2), lambda i: (i, 0)), pl.BlockSpec((1, W), lambda i: (0, 0)),
                  pl.BlockSpec((G, GMLP_BLOCK, GMLP_BLOCK), lambda i: (0, 0, 0)),
                  pl.BlockSpec((GMLP_BLOCK, W), lambda i: (0, 0))],
        out_specs=pl.BlockSpec((tm, W), lambda i: (i, 0)), out_shape=jax.ShapeDtypeStruct((T, W), BF16),
        compiler_params=_params(("arbitrary",)))(zp, gv, ws, bs_tile)


def _gate_bwd(zp, d_out, gv, ws, bs_tile, *, tm=256):
    T, W2 = zp.shape
    W = W2 // 2
    G = W // GROUP_DIM
    tm = _row_tile(T, tm)
    nm = T // tm

    def body(zp_ref, do_ref, gv_ref, ws_ref, bs_ref, dzp_ref, dws_ref, dbs_ref, dgv_ref, du_scr, dvn_scr, dsum_scr):
        i = pl.program_id(0)

        @pl.when(i == 0)
        def _():
            dws_ref[...] = jnp.zeros_like(dws_ref)
            dgv_ref[...] = jnp.zeros_like(dgv_ref)
            dsum_scr[...] = jnp.zeros_like(dsum_scr)

        zp = zp_ref[...]
        z = _gelu(zp)
        u, v = z[:, :W], z[:, W:]
        n, r = _rms(v)
        gv = gv_ref[...]
        vn = n * gv
        d_out = do_ref[...].astype(F32)
        mask = _chunk_mask()
        for g in range(G):
            cs = slice(g * GROUP_DIM, (g + 1) * GROUP_DIM)
            wg = jnp.where(mask, ws_ref[g], 0.0).astype(BF16)
            dw = jnp.zeros((GMLP_BLOCK, GMLP_BLOCK), F32)
            for b in range(tm // GMLP_BLOCK):
                rs = slice(b * GMLP_BLOCK, (b + 1) * GMLP_BLOCK)
                vb = vn[rs, cs].astype(BF16)
                s = jnp.dot(wg, vb, preferred_element_type=F32) + bs_ref[:, cs]
                du_scr[rs, cs] = d_out[rs, cs] * s
                ds = d_out[rs, cs] * u[rs, cs]
                dsb = ds.astype(BF16)
                dvn_scr[rs, cs] = lax.dot_general(wg, dsb, (((0,), (0,)), ((), ())), preferred_element_type=F32)
                dw = dw + lax.dot_general(dsb, vb, (((1,), (1,)), ((), ())), preferred_element_type=F32)
                dsum_scr[:, cs] += ds
            dws_ref[g] += jnp.where(mask, dw, 0.0)
        dvn = dvn_scr[...]
        dgv_ref[...] += jnp.sum(dvn * n, axis=0, keepdims=True)
        t = dvn * gv
        dv = r * (t - n * jnp.mean(t * n, axis=-1, keepdims=True))
        dzp_ref[:, :W] = (du_scr[...] * _gelu_grad(zp[:, :W])).astype(BF16)
        dzp_ref[:, W:] = (dv * _gelu_grad(zp[:, W:])).astype(BF16)

        @pl.when(i == nm - 1)
        def _():
            sel = (lax.broadcasted_iota(jnp.int32, (G, W), 1) // GROUP_DIM
                   == lax.broadcasted_iota(jnp.int32, (G, W), 0)).astype(F32)
            dbs_ref[...] = lax.dot_general(sel, dsum_scr[...], (((1,), (1,)), ((), ())),
                                           precision=lax.Precision.HIGHEST, preferred_element_type=F32)

    return pl.pallas_call(
        body, name="gate_bwd", grid=(nm,),
        in_specs=[pl.BlockSpec((tm, W2), lambda i: (i, 0)), pl.BlockSpec((tm, W), lambda i: (i, 0)),
                  pl.BlockSpec((1, W), lambda i: (0, 0)),
                  pl.BlockSpec((G, GMLP_BLOCK, GMLP_BLOCK), lambda i: (0, 0, 0)),
                  pl.BlockSpec((GMLP_BLOCK, W), lambda i: (0, 0))],
        out_specs=[pl.BlockSpec((tm, W2), lambda i: (i, 0)),
                   pl.BlockSpec((G, GMLP_BLOCK, GMLP_BLOCK), lambda i: (0, 0, 0)),
                   pl.BlockSpec((G, GMLP_BLOCK), lambda i: (0, 0)), pl.BlockSpec((1, W), lambda i: (0, 0))],
        out_shape=[jax.ShapeDtypeStruct((T, W2), BF16), jax.ShapeDtypeStruct((G, GMLP_BLOCK, GMLP_BLOCK), F32),
                   jax.ShapeDtypeStruct((G, GMLP_BLOCK), F32), jax.ShapeDtypeStruct((1, W), F32)],
        scratch_shapes=[pltpu.VMEM((tm, W), F32), pltpu.VMEM((tm, W), F32), pltpu.VMEM((GMLP_BLOCK, W), F32)],
        compiler_params=_params(("arbitrary",)))(zp, d_out, gv, ws, bs_tile)


HALO = 16


def _taps(ext, w, b):
    a, a1, a2 = ext[HALO:], pltpu.roll(ext, 1, 0)[HALO:], pltpu.roll(ext, 2, 0)[HALO:]
    return w[2:3] * a + w[1:2] * a1 + w[0:1] * a2 + b, a, a1, a2


def _conv_fwd(a, cw, cb, S, *, tm=256):
    _, T, F = a.shape
    tc = _col_tile(F)
    tm = _row_tile(S, tm)
    hb = tm // HALO

    def body(a_ref, p_ref, w_ref, b_ref, o_ref, c_ref):
        first = (pl.program_id(1) * tm) % S == 0
        keep = jnp.where(first, 0.0, 1.0)

        def conv(s):
            ext = jnp.concatenate([p_ref[s].astype(F32) * keep, a_ref[s].astype(F32)], axis=0)
            c = _taps(ext, w_ref[s], b_ref[s:s + 1, :])[0].astype(BF16)
            c_ref[s] = c
            return c.astype(F32)

        up, gate = conv(0), conv(1)
        o_ref[...] = (gate * jax.nn.sigmoid(gate) * up).astype(BF16)

    return pl.pallas_call(
        body, name="conv_fwd", grid=(F // tc, T // tm),
        in_specs=[pl.BlockSpec((2, tm, tc), lambda j, i: (0, i, j)),
                  pl.BlockSpec((2, HALO, tc), lambda j, i: (0, jnp.maximum(i * hb - 1, 0), j)),
                  pl.BlockSpec((2, 3, tc), lambda j, i: (0, 0, j)), pl.BlockSpec((2, tc), lambda j, i: (0, j))],
        out_specs=[pl.BlockSpec((tm, tc), lambda j, i: (i, j)), pl.BlockSpec((2, tm, tc), lambda j, i: (0, i, j))],
        out_shape=[jax.ShapeDtypeStruct((T, F), BF16), jax.ShapeDtypeStruct((2, T, F), BF16)],
        compiler_params=_params(("arbitrary", "arbitrary")))(a, a, cw, cb)


def _ffn_in_conv(h, w, g, cw, cb, S, *, name, tm=256):
    T, D = h.shape
    F = w.shape[-1] // 2
    tc = _col_tile(F)
    tm = _row_tile(S, tm)

    def body(h_ref, w_ref, g_ref, cw_ref, cb_ref, y_ref, a_ref, c_ref, n_ref, tail):
        first = (pl.program_id(0) * tm) % S == 0
        nb = (_rms(h_ref[...])[0] * g_ref[...]).astype(BF16)
        n_ref[...] = nb
        for j in range(F // tc):
            cs = slice(j * tc, (j + 1) * tc)
            conv = []
            for s in range(2):
                acc = jnp.dot(nb, w_ref[:, s * F + j * tc:s * F + (j + 1) * tc], preferred_element_type=F32)
                ab = acc.astype(BF16)
                a_ref[s, :, cs] = ab
                af = ab.astype(F32)
                ext = jnp.concatenate([jnp.where(first, 0.0, tail[s, :, cs]), af], axis=0)
                tail[s, :, cs] = af[tm - HALO:, :]
                cv = _taps(ext, cw_ref[s, :, cs], cb_ref[s:s + 1, cs])[0].astype(BF16)
                c_ref[s, :, cs] = cv
                conv.append(cv.astype(F32))
            up, gate = conv
            y_ref[:, cs] = (gate * jax.nn.sigmoid(gate) * up).astype(BF16)

    row = lambda width: pl.BlockSpec((tm, width), lambda i: (i, 0))
    wide = pl.BlockSpec((2, tm, F), lambda i: (0, i, 0))
    return pl.pallas_call(
        body, name=name, grid=(T // tm,),
        in_specs=[row(D), pl.BlockSpec((None, D, 2 * F), lambda i: (0, 0, 0)), pl.BlockSpec((1, D), lambda i: (0, 0)),
                  pl.BlockSpec((2, 3, F), lambda i: (0, 0, 0)), pl.BlockSpec((2, F), lambda i: (0, 0))],
        out_specs=[row(F), wide, wide, row(D)],
        out_shape=[jax.ShapeDtypeStruct((T, F), BF16), jax.ShapeDtypeStruct((2, T, F), BF16),
                   jax.ShapeDtypeStruct((2, T, F), BF16), jax.ShapeDtypeStruct((T, D), BF16)],
        scratch_shapes=[pltpu.VMEM((2, HALO, F), F32)],
        compiler_params=_params(("arbitrary",)))(h, w, g.reshape(1, D), cw, cb)


def _conv_bwd(a, c, dy, cw, S, *, tm=256):
    _, T, F = a.shape
    tc = _col_tile(F)
    tm = _row_tile(S, tm)
    nm = T // tm
    hb = tm // HALO
    TE = tm + HALO
    nxt = lambda j, i: jnp.minimum((i + 1) * hb, T // HALO - 1)

    def body(a_ref, c_ref, nc_ref, dy_ref, ndy_ref, w_ref, da_ref, dw_ref, db_ref):
        i = pl.program_id(1)
        last = ((i + 1) * tm) % S == 0
        keep_n = jnp.where(last, 0.0, 1.0)
        dyf = jnp.concatenate([dy_ref[...].astype(F32), ndy_ref[...].astype(F32) * keep_n], axis=0)
        up = jnp.concatenate([c_ref[0].astype(F32), nc_ref[0].astype(F32)], axis=0)
        gate = jnp.concatenate([c_ref[1].astype(F32), nc_ref[1].astype(F32)], axis=0)
        sg = jax.nn.sigmoid(gate)
        d_up = dyf * (gate * sg)
        d_gate = dyf * up * (sg * (1.0 + gate * (1.0 - sg)))

        @pl.when(i == 0)
        def _():
            dw_ref[...] = jnp.zeros_like(dw_ref)
            db_ref[...] = jnp.zeros_like(db_ref)

        def back(s, d):
            a = a_ref[s].astype(F32)
            w = w_ref[s]
            u1, u2 = pltpu.roll(d, TE - 1, 0), pltpu.roll(d, TE - 2, 0)
            db_ref[s:s + 1, :] += jnp.sum(d[:tm], axis=0, keepdims=True)
            dw_ref[s, 2:3, :] += jnp.sum(d[:tm] * a, axis=0, keepdims=True)
            dw_ref[s, 1:2, :] += jnp.sum(u1[:tm] * a, axis=0, keepdims=True)
            dw_ref[s, 0:1, :] += jnp.sum(u2[:tm] * a, axis=0, keepdims=True)
            da_ref[s] = (w[2:3] * d + w[1:2] * u1 + w[0:1] * u2)[:tm].astype(BF16)

        back(0, d_up)
        back(1, d_gate)

    cur = pl.BlockSpec((2, tm, tc), lambda j, i: (0, i, j))
    return pl.pallas_call(
        body, name="conv_bwd", grid=(F // tc, nm),
        in_specs=[cur, cur, pl.BlockSpec((2, HALO, tc), lambda j, i: (0, nxt(j, i), j)),
                  pl.BlockSpec((tm, tc), lambda j, i: (i, j)), pl.BlockSpec((HALO, tc), lambda j, i: (nxt(j, i), j)),
                  pl.BlockSpec((2, 3, tc), lambda j, i: (0, 0, j))],
        out_specs=[cur, pl.BlockSpec((2, 3, tc), lambda j, i: (0, 0, j)), pl.BlockSpec((2, tc), lambda j, i: (0, j))],
        out_shape=[jax.ShapeDtypeStruct((2, T, F), BF16), jax.ShapeDtypeStruct((2, 3, F), F32),
                   jax.ShapeDtypeStruct((2, F), F32)],
        compiler_params=_params(("arbitrary", "arbitrary")))(a, c, c, dy, dy, cw)


def _bias_index():
    idx = np.arange(F_LEN)
    d = np.where(idx < K_SPAN, idx, idx - F_LEN)
    return np.clip(PAD - d, -REL_CLIP, REL_CLIP) + REL_CLIP


def _roll_rows(x, sign):
    rows = lax.broadcasted_iota(jnp.int32, x.shape, 0)
    step = 1
    while step < Q_BLOCK:
        shift = step if sign > 0 else F_LEN - step
        x = jnp.where((rows & step) != 0, pltpu.roll(x, shift, 1), x)
        step *= 2
    return x


def _bias_expand(frow):
    H = frow.shape[0]

    def body(f_ref, o_ref):
        x = _roll_rows(jnp.broadcast_to(f_ref[...], (Q_BLOCK, F_LEN)), 1)[:, :K_SPAN]
        qc = lax.broadcasted_iota(jnp.int32, (Q_BLOCK, K_SPAN), 0) // CHUNK * CHUNK
        kj = lax.broadcasted_iota(jnp.int32, (Q_BLOCK, K_SPAN), 1)
        o_ref[...] = jnp.where((kj >= qc) & (kj < qc + PAD + CHUNK), x, NEG_INF)

    return pl.pallas_call(
        body, name="bias_expand", grid=(H,),
        in_specs=[pl.BlockSpec((None, 1, F_LEN), lambda h: (h, 0, 0))],
        out_specs=pl.BlockSpec((None, Q_BLOCK, K_SPAN), lambda h: (h, 0, 0)),
        out_shape=jax.ShapeDtypeStruct((H, Q_BLOCK, K_SPAN), F32), compiler_params=_params(("arbitrary",)))(frow)


def _bias_reduce(dbias, n_rel):
    H = dbias.shape[0]
    onehot = jnp.asarray((_bias_index()[:, None] == np.arange(n_rel)[None, :]).astype(np.float32))

    def body(d_ref, oh_ref, o_ref):
        x = jnp.concatenate([d_ref[...], jnp.zeros((Q_BLOCK, F_LEN - K_SPAN), F32)], axis=1)
        row = jnp.sum(_roll_rows(x, -1), axis=0, keepdims=True)
        row8 = jnp.broadcast_to(row, (8, F_LEN))
        o_ref[...] = jnp.dot(row8, oh_ref[...], precision=lax.Precision.HIGHEST, preferred_element_type=F32)[0:1]

    return pl.pallas_call(
        body, name="bias_reduce", grid=(H,),
        in_specs=[pl.BlockSpec((None, Q_BLOCK, K_SPAN), lambda h: (h, 0, 0)),
                  pl.BlockSpec((F_LEN, n_rel), lambda h: (0, 0))],
        out_specs=pl.BlockSpec((None, 1, n_rel), lambda h: (h, 0, 0)),
        out_shape=jax.ShapeDtypeStruct((H, 1, n_rel), F32), compiler_params=_params(("arbitrary",)))(dbias, onehot)


def _attn_specs(S):
    hw = HEADS_PER_STEP * HEAD_DIM
    qspec = pl.BlockSpec((None, Q_BLOCK, hw), lambda g, b, i: (b, i, g))
    kspec = pl.BlockSpec((None, None, S, hw), lambda g, b, i: (0, b, 0, g))
    vspec = pl.BlockSpec((None, None, S, hw), lambda g, b, i: (1, b, 0, g))
    bspec = pl.BlockSpec((HEADS_PER_STEP, Q_BLOCK, K_SPAN), lambda g, b, i: (g, 0, 0))
    return hw, qspec, kspec, vspec, bspec


def _load_padded(k_ref, v_ref, kp, vp):
    kp[:PAD, :] = jnp.zeros((PAD, kp.shape[1]), BF16)
    vp[:PAD, :] = jnp.zeros((PAD, vp.shape[1]), BF16)
    kp[PAD:, :] = k_ref[...]
    vp[PAD:, :] = v_ref[...]


def _attn_exp(q_ref, kp, b_ref, h, q0, before):
    hs = slice(h * HEAD_DIM, (h + 1) * HEAD_DIM)
    kh = kp[pl.ds(q0, K_SPAN), hs]
    s = lax.dot_general(q_ref[:, hs], kh, (((1,), (1,)), ((), ())), preferred_element_type=F32) + b_ref[h] + before
    p = jnp.exp(s - jnp.max(s, axis=-1, keepdims=True))
    return p, 1.0 / jnp.sum(p, axis=-1, keepdims=True), kh


def _before_start(q0):
    kj = lax.broadcasted_iota(jnp.int32, (1, K_SPAN), 1)
    return jnp.where(q0 + kj >= PAD, 0.0, NEG_INF)


def _attn_fwd(q, kv, bias, B, S):
    HD = q.shape[-1]
    hw, qspec, kspec, vspec, bspec = _attn_specs(S)

    def body(q_ref, k_ref, v_ref, b_ref, o_ref, kp, vp):
        i = pl.program_id(2)

        @pl.when(i == 0)
        def _():
            _load_padded(k_ref, v_ref, kp, vp)

        q0 = pl.multiple_of(i * Q_BLOCK, Q_BLOCK)
        before = _before_start(q0)
        outs = []
        for h in range(HEADS_PER_STEP):
            hs = slice(h * HEAD_DIM, (h + 1) * HEAD_DIM)
            p, inv, _ = _attn_exp(q_ref, kp, b_ref, h, q0, before)
            outs.append(jnp.dot(p.astype(BF16), vp[pl.ds(q0, K_SPAN), hs], preferred_element_type=F32) * inv)
        o_ref[...] = jnp.concatenate(outs, axis=1).astype(BF16)

    return pl.pallas_call(
        body, name="attn_fwd", grid=(HD // hw, B, S // Q_BLOCK), in_specs=[qspec, kspec, vspec, bspec],
        out_specs=qspec, out_shape=jax.ShapeDtypeStruct((B, S, HD), BF16),
        scratch_shapes=[pltpu.VMEM((S + PAD, hw), BF16), pltpu.VMEM((S + PAD, hw), BF16)],
        compiler_params=_params(("arbitrary", "arbitrary", "arbitrary")))(q, kv, kv, bias)


def _attn_bwd(q, kv, bias, do, B, S):
    HD = q.shape[-1]
    H = HD // HEAD_DIM
    hw, qspec, kspec, vspec, bspec = _attn_specs(S)
    scale = HEAD_DIM ** -0.5
    nq = S // Q_BLOCK

    def body(q_ref, k_ref, v_ref, b_ref, do_ref, dq_ref, dkv_ref, db_ref, kp, vp, dk_acc, dv_acc):
        b, i = pl.program_id(1), pl.program_id(2)
        q0 = pl.multiple_of(i * Q_BLOCK, Q_BLOCK)

        @pl.when(i == 0)
        def _():
            _load_padded(k_ref, v_ref, kp, vp)
            dk_acc[...] = jnp.zeros_like(dk_acc)
            dv_acc[...] = jnp.zeros_like(dv_acc)

        @pl.when((i == 0) & (b == 0))
        def _():
            db_ref[...] = jnp.zeros_like(db_ref)

        before = _before_start(q0)
        for h in range(HEADS_PER_STEP):
            hs = slice(h * HEAD_DIM, (h + 1) * HEAD_DIM)
            p, inv, kh = _attn_exp(q_ref, kp, b_ref, h, q0, before)
            p = p * inv
            doh = do_ref[:, hs]
            dp = lax.dot_general(doh, vp[pl.ds(q0, K_SPAN), hs], (((1,), (1,)), ((), ())),
                                 preferred_element_type=F32)
            ds = p * (dp - jnp.sum(p * dp, axis=-1, keepdims=True))
            db_ref[h] += ds
            dsb = ds.astype(BF16)
            dq_ref[:, hs] = (jnp.dot(dsb, kh, preferred_element_type=F32) * scale).astype(BF16)
            dk_acc[pl.ds(q0, K_SPAN), hs] += lax.dot_general(dsb, q_ref[:, hs], (((0,), (0,)), ((), ())),
                                                              preferred_element_type=F32)
            dv_acc[pl.ds(q0, K_SPAN), hs] += lax.dot_general(p.astype(BF16), doh, (((0,), (0,)), ((), ())),
                                                              preferred_element_type=F32)

        @pl.when(i == nq - 1)
        def _():
            dkv_ref[0] = dk_acc[PAD:, :].astype(BF16)
            dkv_ref[1] = dv_acc[PAD:, :].astype(BF16)

    return pl.pallas_call(
        body, name="attn_bwd", grid=(HD // hw, B, nq), in_specs=[qspec, kspec, vspec, bspec, qspec],
        out_specs=[qspec, pl.BlockSpec((2, None, S, hw), lambda g, b, i: (0, b, 0, g)), bspec],
        out_shape=[jax.ShapeDtypeStruct((B, S, HD), BF16), jax.ShapeDtypeStruct((2, B, S, HD), BF16),
                   jax.ShapeDtypeStruct((H, Q_BLOCK, K_SPAN), F32)],
        scratch_shapes=[pltpu.VMEM((S + PAD, hw), BF16), pltpu.VMEM((S + PAD, hw), BF16),
                        pltpu.VMEM((S + PAD, hw), F32), pltpu.VMEM((S + PAD, hw), F32)],
        compiler_params=_params(("arbitrary", "arbitrary", "arbitrary")))(q, kv, kv, bias, do)


def _loss_head(h, g, target, *, tm=512):
    T, D = h.shape
    tm = _row_tile(T, tm)

    def body(h_ref, g_ref, t_ref, dh_ref, loss_ref, dg_ref):
        @pl.when(pl.program_id(0) == 0)
        def _():
            loss_ref[...] = jnp.zeros_like(loss_ref)
            dg_ref[...] = jnp.zeros_like(dg_ref)

        n, r = _rms(h_ref[...])
        g = g_ref[...]
        e = n * g - t_ref[...]
        loss_ref[...] += 0.5 * jnp.sum(jnp.mean(e * e, axis=-1, keepdims=True), axis=0, keepdims=True)
        dy = e * (1.0 / D)
        dg_ref[...] += jnp.sum(dy * n, axis=0, keepdims=True)
        t = dy * g
        dh_ref[...] = r * (t - n * jnp.mean(t * n, axis=-1, keepdims=True))

    row = pl.BlockSpec((tm, D), lambda i: (i, 0))
    return pl.pallas_call(
        body, name="loss_head", grid=(T // tm,), in_specs=[row, pl.BlockSpec((1, D), lambda i: (0, 0)), row],
        out_specs=[row, pl.BlockSpec((8, 128), lambda i: (0, 0)), pl.BlockSpec((1, D), lambda i: (0, 0))],
        out_shape=[jax.ShapeDtypeStruct((T, D), F32), jax.ShapeDtypeStruct((8, 128), F32),
                   jax.ShapeDtypeStruct((1, D), F32)],
        compiler_params=_params(("arbitrary",)))(h, g.reshape(1, D), target)


def _sub_rows(R):
    for cand in (256, 352, 128, 64, 8):
        if R % cand == 0 and R > cand:
            return cand
    return R


def _adamw(w, g, m, v, *, name):
    R, C = w.shape
    tr = _sub_rows(R)

    def body(w_ref, g_ref, m_ref, v_ref, d_ref, nm_ref, nv_ref):
        g = g_ref[...]
        m = ADAM_B1 * m_ref[...] + (1.0 - ADAM_B1) * g
        v = ADAM_B2 * v_ref[...] + (1.0 - ADAM_B2) * (g * g)
        m_hat = m / (1.0 - ADAM_B1 ** ADAM_STEP)
        v_hat = v / (1.0 - ADAM_B2 ** ADAM_STEP)
        d_ref[...] = -ADAM_LR * (m_hat / (jnp.sqrt(v_hat) + ADAM_EPS) + ADAM_WD * w_ref[...])
        nm_ref[...] = m
        nv_ref[...] = v

    spec = pl.BlockSpec((tr, C), lambda i: (i, 0))
    return pl.pallas_call(body, name=name, grid=(R // tr,), in_specs=[spec] * 4, out_specs=[spec] * 3,
                          out_shape=[jax.ShapeDtypeStruct((R, C), F32)] * 3,
                          compiler_params=_params(("arbitrary",)))(w, g, m, v)


def _add_pair(units, got, core, *, name):
    n4, R, C = got.shape
    rows = n4 * R
    tr = 512 if rows % 512 == 0 else R

    def body(c_ref, u_ref, got_ref, o_ref):
        o_ref[...] = (u_ref[...].astype(F32) + got_ref[...].astype(F32)).astype(BF16)

    spec = pl.BlockSpec((tr, C), lambda i, c: (i, 0))
    grid_spec = pltpu.PrefetchScalarGridSpec(
        num_scalar_prefetch=1, grid=(rows // tr,),
        in_specs=[pl.BlockSpec((None, tr, C), lambda i, c: (c[0], i, 0)), spec], out_specs=spec)
    out = pl.pallas_call(body, name=name, grid_spec=grid_spec, out_shape=jax.ShapeDtypeStruct((rows, C), BF16),
                         compiler_params=_params(("arbitrary",)))(core.reshape(1), units.reshape(2, rows, C),
                                                                   got.reshape(rows, C))
    return out.reshape(n4, R, C)


def _sum_chips(w, own, got, pos, *, name, layer=0, into=None):
    _, R, C = own.shape
    tr = _sub_rows(R)
    nr = R // tr

    def body(p_ref, own_ref, got_ref, *rest):
        o_ref = rest[-1]
        o_ref[...] = (own_ref[...].astype(F32) + got_ref[0].astype(F32) + got_ref[1].astype(F32)
                      + got_ref[2].astype(F32))

    if w.row_sharded:
        out_map = lambda i, p: (layer, i, p[1])
    else:
        out_map = lambda i, p: (layer, p[1] * nr + i, 0)
    ins = [pos, own, got]
    in_specs = [pl.BlockSpec((None, tr, C), lambda i, p: (p[0], i, 0)),
                pl.BlockSpec((3, tr, C), lambda i, p: (0, i, 0))]
    alias = {}
    if into is not None:
        ins.append(into)
        in_specs.append(ANY)
        alias = {3: 0}
    grid_spec = pltpu.PrefetchScalarGridSpec(num_scalar_prefetch=1, grid=(nr,), in_specs=in_specs,
                                             out_specs=pl.BlockSpec((None, tr, C), out_map))
    return pl.pallas_call(body, name=name, grid_spec=grid_spec, input_output_aliases=alias,
                          out_shape=jax.ShapeDtypeStruct((w.L, w.ks, w.ns), F32),
                          compiler_params=_params(("arbitrary",)))(*ins)


def _mesh_pos():
    return lax.axis_index("x"), lax.axis_index("y"), lax.axis_index("c")


def _other_chips(x, y):
    return [(1 - x, y), (x, 1 - y), (1 - x, 1 - y)]


ANY = pl.BlockSpec(memory_space=pl.ANY)


class _W:
    def __init__(self, name, shard, row_sharded):
        self.name = name
        self.L, ks, ns = shard.shape
        self.row_sharded = row_sharded
        self.K, self.N = (ks * N_CHIPS, ns) if row_sharded else (ks, ns * N_CHIPS)
        self.ks, self.ns = ks, ns

    def shard_of(self, full, j):
        if self.row_sharded:
            return full.at[:, pl.ds(j * self.ks, self.ks), :]
        return full.at[:, :, pl.ds(j * self.ns, self.ns)]

    def half_of(self, shard, c):
        if self.row_sharded:
            return shard.at[:, :, pl.ds(c * (self.ns // 2), self.ns // 2)]
        return shard.at[:, pl.ds(c * (self.ks // 2), self.ks // 2), :]


HBM = pl.BlockSpec(memory_space=pltpu.HBM)
SEM = pl.BlockSpec(memory_space=pltpu.SEMAPHORE)
IN_FLIGHT = pltpu.SideEffectType.DATAFLOW_SIDE_EFFECTING


def _in_hbm(a):
    return pltpu.with_memory_space_constraint(a, pltpu.HBM)


def _gather_start(ws, shards, after):
    nw = len(ws)

    def body(*refs):
        src, dst = refs[:nw], refs[nw:2 * nw]
        send, recv = refs[2 * nw + 1:3 * nw + 1], refs[3 * nw + 1:4 * nw + 1]
        x, y, c = _mesh_pos()
        me = 2 * x + y
        for i, w in enumerate(ws):
            for f, (px, py) in enumerate(_other_chips(x, y)):
                pltpu.make_async_remote_copy(src_ref=w.half_of(src[i], c), dst_ref=w.half_of(w.shard_of(dst[i], me), c),
                                             send_sem=send[i].at[f], recv_sem=recv[i].at[f], device_id=(px, py, c),
                                             device_id_type=MESH).start()

    fulls = [lax.empty((w.L, w.K, w.N), BF16) for w in ws]
    out = pl.pallas_call(
        body, name="gather_start", in_specs=[HBM] * (2 * nw) + [ANY],
        out_specs=[SEM] * (2 * nw) + [HBM] * (2 * nw),
        out_shape=[pltpu.SemaphoreType.DMA((3,))] * (2 * nw)
        + [pltpu.HBM(s.shape, BF16) for s in shards] + [pltpu.HBM(f.shape, BF16) for f in fulls],
        input_output_aliases={i: 2 * nw + i for i in range(2 * nw)},
        compiler_params=pltpu.CompilerParams(has_side_effects=IN_FLIGHT))(
            *[_in_hbm(s) for s in shards], *[_in_hbm(f) for f in fulls], after)
    return [(out[i], out[nw + i], out[2 * nw + i], out[3 * nw + i]) for i in range(nw)]


def _gather_wait(ws, flight, after, *, name):
    nw = len(ws)

    def body(*refs):
        src, dst = refs[:nw], refs[nw:2 * nw]
        send, recv = refs[2 * nw:3 * nw], refs[3 * nw:4 * nw]
        x, y, c = _mesh_pos()
        for i, w in enumerate(ws):
            for f, (px, py) in enumerate(_other_chips(x, y)):
                landed = w.half_of(w.shard_of(dst[i], 2 * px + py), c)
                cp = pltpu.make_async_remote_copy(src_ref=w.half_of(src[i], c), dst_ref=landed, send_sem=send[i].at[f],
                                                  recv_sem=recv[i].at[f], device_id=(px, py, c), device_id_type=MESH)
                cp.wait_send()
                cp.wait_recv()

    shards, fulls = [fl[2] for fl in flight], [fl[3] for fl in flight]
    out = pl.pallas_call(
        body, name=name, in_specs=[HBM] * (2 * nw) + [SEM] * (2 * nw) + [ANY],
        out_specs=[HBM] * (2 * nw),
        out_shape=[pltpu.HBM(s.shape, BF16) for s in shards] + [pltpu.HBM(f.shape, BF16) for f in fulls],
        input_output_aliases={i: i for i in range(2 * nw)},
        compiler_params=pltpu.CompilerParams(has_side_effects=IN_FLIGHT))(
            *shards, *fulls, *[fl[0] for fl in flight], *[fl[1] for fl in flight], after)
    return out[:nw], out[nw:]


def _gather_finish(ws, shards, fulls, *, name):
    nw = len(ws)

    def body(*refs):
        src, dst, stage = refs[:nw], refs[3 * nw:4 * nw], refs[4 * nw:5 * nw]
        send_sems, recv_sems, load_sems, store_sems = refs[5 * nw:]
        x, y, c = _mesh_pos()
        me = 2 * x + y
        sibling = (x, y, 1 - c)
        chips = _other_chips(x, y)

        def fwd(i, w, f, half):
            px, py = chips[f]
            landed = w.half_of(w.shard_of(dst[i], 2 * px + py), half)
            return pltpu.make_async_remote_copy(src_ref=landed, dst_ref=landed, send_sem=send_sems.at[3 * i + f],
                                                recv_sem=recv_sems.at[3 * i + f], device_id=sibling,
                                                device_id_type=MESH)

        loads = [pltpu.make_async_copy(src[i], stage[i], load_sems.at[i]) for i in range(nw)]
        for cp in loads:
            cp.start()
        sends = [fwd(i, w, f, c) for i, w in enumerate(ws) for f in range(3)]
        for cp in sends:
            cp.start()
        stores = [pltpu.make_async_copy(stage[i], w.shard_of(dst[i], me), store_sems.at[i])
                  for i, w in enumerate(ws)]
        for ld, st in zip(loads, stores):
            ld.wait()
            st.start()
        for i, w in enumerate(ws):
            for f in range(3):
                fwd(i, w, f, 1 - c).wait_recv()
        for cp in sends:
            cp.wait_send()
        for cp in stores:
            cp.wait()

    out = pl.pallas_call(
        body, name=name, in_specs=[ANY] * (2 * nw), out_specs=[ANY] * (2 * nw),
        out_shape=[jax.ShapeDtypeStruct(s.shape, BF16) for s in shards]
        + [jax.ShapeDtypeStruct(f.shape, BF16) for f in fulls],
        input_output_aliases={i: i for i in range(2 * nw)},
        scratch_shapes=[pltpu.VMEM((w.L, w.ks, w.ns), BF16) for w in ws]
        + [pltpu.SemaphoreType.DMA((3 * nw,)), pltpu.SemaphoreType.DMA((3 * nw,)), pltpu.SemaphoreType.DMA((nw,)),
           pltpu.SemaphoreType.DMA((nw,))],
        compiler_params=_params(has_side_effects=True))(*shards, *fulls)
    return out[nw:]


def _split_copies(name, srcs, lands, n_sems, copies_of, *, flight=None, after=None):
    n = len(srcs)
    starting = flight is None

    def body(*refs):
        src, land = refs[:n], refs[n:2 * n]
        sems = refs[2 * n + 1:4 * n + 1] if starting else refs[2 * n:4 * n]
        for i in range(n):
            for cp in copies_of(i, src[i], land[i], sems[i], sems[n + i]):
                if starting:
                    cp.start()
                else:
                    cp.wait_send()
                    cp.wait_recv()

    thru = [pltpu.HBM(a.shape, a.dtype) for a in list(srcs) + list(lands)]
    if starting:
        out = pl.pallas_call(
            body, name=name, in_specs=[HBM] * (2 * n) + [ANY], out_specs=[SEM] * (2 * n) + [HBM] * (2 * n),
            out_shape=[pltpu.SemaphoreType.DMA((n_sems,))] * (2 * n) + thru,
            input_output_aliases={i: 2 * n + i for i in range(2 * n)},
            compiler_params=pltpu.CompilerParams(has_side_effects=IN_FLIGHT))(
                *[_in_hbm(a) for a in srcs], *[_in_hbm(a) for a in lands], after)
        return [(out[i], out[n + i], out[2 * n + i], out[3 * n + i]) for i in range(n)]
    out = pl.pallas_call(
        body, name=name, in_specs=[HBM] * (2 * n) + [SEM] * (2 * n) + [ANY], out_specs=[HBM] * (2 * n),
        out_shape=thru, input_output_aliases={i: i for i in range(2 * n)},
        compiler_params=pltpu.CompilerParams(has_side_effects=IN_FLIGHT))(
            *srcs, *lands, *[fl[0] for fl in flight], *[fl[1] for fl in flight], after)
    return out[:n], out[n:]


def _sum8(land, vec, me):
    R = vec.shape[0]

    def body(me_ref, land_ref, vec_ref, o_ref):
        acc = jnp.zeros((R, 128), F32)
        for d in range(8):
            acc = acc + jnp.where(me_ref[0] == d, vec_ref[...], land_ref[d])
        o_ref[...] = acc

    grid_spec = pltpu.PrefetchScalarGridSpec(
        num_scalar_prefetch=1, grid=(1,),
        in_specs=[pl.BlockSpec((8, R, 128), lambda i, m: (0, 0, 0)), pl.BlockSpec((R, 128), lambda i, m: (0, 0))],
        out_specs=pl.BlockSpec((R, 128), lambda i, m: (0, 0)))
    return pl.pallas_call(body, name="sum8", grid_spec=grid_spec, out_shape=jax.ShapeDtypeStruct((R, 128), F32),
                          compiler_params=_params(("arbitrary",)))(me.reshape(1), land, vec)


def _swap_copies(i, src, got, send, recv):
    x, y, c = _mesh_pos()
    return [pltpu.make_async_remote_copy(src_ref=src.at[1 - c], dst_ref=got, send_sem=send.at[0], recv_sem=recv.at[0],
                                         device_id=(x, y, 1 - c), device_id_type=MESH)]


def _gather8_copies(i, src, land, send, recv):
    x, y, c = _mesh_pos()
    me = 4 * x + 2 * y + c
    peers = [(x, y, 1 - c)] + [(px, py, pc) for px, py in _other_chips(x, y) for pc in (c, 1 - c)]
    return [pltpu.make_async_remote_copy(src_ref=src, dst_ref=land.at[me], send_sem=send.at[k], recv_sem=recv.at[k],
                                         device_id=peer, device_id_type=MESH) for k, peer in enumerate(peers)]


def _scatter_copy(src, got, send, recv, f, chip, c):
    px, py = chip
    return pltpu.make_async_remote_copy(src_ref=src.at[2 * px + py], dst_ref=got.at[f], send_sem=send.at[f],
                                        recv_sem=recv.at[f], device_id=(px, py, c), device_id_type=MESH)


def _scatter_start(sums, *, name):
    nw = len(sums)

    def body(*refs):
        src, got = refs[:nw], refs[nw:2 * nw]
        send, recv = refs[2 * nw:3 * nw], refs[3 * nw:4 * nw]
        x, y, c = _mesh_pos()
        for i in range(nw):
            for f, chip in enumerate(_other_chips(x, y)):
                _scatter_copy(src[i], got[i], send[i], recv[i], f, chip, c).start()

    lands = [lax.empty((3,) + s.shape[1:], BF16) for s in sums]
    out = pl.pallas_call(
        body, name=name, in_specs=[HBM] * (2 * nw), out_specs=[SEM] * (2 * nw) + [HBM] * (2 * nw),
        out_shape=[pltpu.SemaphoreType.DMA((3,))] * (2 * nw)
        + [pltpu.HBM(s.shape, BF16) for s in sums] + [pltpu.HBM(l.shape, BF16) for l in lands],
        input_output_aliases={i: 2 * nw + i for i in range(2 * nw)},
        compiler_params=pltpu.CompilerParams(has_side_effects=IN_FLIGHT))(
            *[_in_hbm(s) for s in sums], *[_in_hbm(l) for l in lands])
    return [(out[i], out[nw + i], out[2 * nw + i], out[3 * nw + i]) for i in range(nw)]


def _scatter_wait(flight, after):
    nw = len(flight)

    def body(*refs):
        src, got = refs[:nw], refs[nw:2 * nw]
        send, recv = refs[2 * nw:3 * nw], refs[3 * nw:4 * nw]
        x, y, c = _mesh_pos()
        for i in range(nw):
            for f, chip in enumerate(_other_chips(x, y)):
                cp = _scatter_copy(src[i], got[i], send[i], recv[i], f, chip, c)
                cp.wait_send()
                cp.wait_recv()

    sums, lands = [fl[2] for fl in flight], [fl[3] for fl in flight]
    out = pl.pallas_call(
        body, name="scatter_wait", in_specs=[HBM] * (2 * nw) + [SEM] * (2 * nw) + [ANY], out_specs=[HBM] * (2 * nw),
        out_shape=[pltpu.HBM(s.shape, BF16) for s in sums] + [pltpu.HBM(l.shape, BF16) for l in lands],
        input_output_aliases={i: i for i in range(2 * nw)},
        compiler_params=pltpu.CompilerParams(has_side_effects=IN_FLIGHT))(
            *sums, *lands, *[fl[0] for fl in flight], *[fl[1] for fl in flight], after)
    return out[:nw], out[nw:]


def _join_halves(ws, shards):
    nw = len(ws)

    def body(*refs):
        buf = refs[nw:2 * nw]
        send_sems, recv_sems = refs[2 * nw:]
        x, y, c = _mesh_pos()
        sibling = (x, y, 1 - c)

        def copy(i, w, half):
            region = w.half_of(buf[i], half)
            return pltpu.make_async_remote_copy(src_ref=region, dst_ref=region, send_sem=send_sems.at[i],
                                                recv_sem=recv_sems.at[i], device_id=sibling, device_id_type=MESH)

        sends = [copy(i, w, c) for i, w in enumerate(ws)]
        for cp in sends:
            cp.start()
        for i, w in enumerate(ws):
            copy(i, w, 1 - c).wait_recv()
        for cp in sends:
            cp.wait_send()

    return pl.pallas_call(
        body, name="join_halves", in_specs=[ANY] * nw, out_specs=[ANY] * nw,
        out_shape=[jax.ShapeDtypeStruct((w.L, w.ks, w.ns), F32) for w in ws],
        input_output_aliases={i: i for i in range(nw)},
        scratch_shapes=[pltpu.SemaphoreType.DMA((nw,)), pltpu.SemaphoreType.DMA((nw,))],
        compiler_params=_params(has_side_effects=True))(*shards)


def _allreduce_small(vec):
    R = vec.shape[0]

    def body(x_ref, o_ref, buf, send_sems, recv_sems):
        x, y, c = _mesh_pos()
        me, sibling = (x, y, c), (x, y, 1 - c)
        chips = _other_chips(x, y)

        def slot(px, py, pc):
            return buf.at[4 * px + 2 * py + pc]

        def copy(k, block, to, src=None):
            return pltpu.make_async_remote_copy(src_ref=slot(*block) if src is None else src, dst_ref=slot(*block),
                                                send_sem=send_sems.at[k], recv_sem=recv_sems.at[k], device_id=to,
                                                device_id_type=MESH)

        first = [copy(0, me, sibling, src=x_ref)] + [copy(1 + f, me, (*chip, c), src=x_ref)
                                                     for f, chip in enumerate(chips)]
        for cp in first:
            cp.start()
        passed = [copy(4 + f, (*chip, c), sibling) for f, chip in enumerate(chips)]
        for f, chip in enumerate(chips):
            copy(1 + f, (*chip, c), me).wait_recv()
            passed[f].start()
        copy(0, sibling, me).wait_recv()
        for f, chip in enumerate(chips):
            copy(4 + f, (*chip, 1 - c), me).wait_recv()
        for cp in first + passed:
            cp.wait_send()
        slot(*me)[...] = x_ref[...]
        acc = buf[0]
        for d in range(1, 8):
            acc = acc + buf[d]
        o_ref[...] = acc

    return pl.pallas_call(
        body, name="allreduce_small", in_specs=[pl.BlockSpec(memory_space=pltpu.VMEM)],
        out_specs=pl.BlockSpec(memory_space=pltpu.VMEM), out_shape=jax.ShapeDtypeStruct((R, 128), F32),
        scratch_shapes=[pltpu.VMEM((8, R, 128), F32), pltpu.SemaphoreType.DMA((7,)), pltpu.SemaphoreType.DMA((7,))],
        compiler_params=_params())(vec)


def _pack(parts):
    flat = jnp.concatenate([p.reshape(-1).astype(F32) for p in parts])
    n = flat.shape[0]
    pad = (-n) % (64 * 128)
    return jnp.pad(flat, (0, pad)).reshape(-1, 128)


def _unpack(vec, shapes):
    flat = vec.reshape(-1)
    out, off = [], 0
    for s in shapes:
        n = int(np.prod(s))
        out.append(flat[off:off + n].reshape(s))
        off += n
    return out


def kernel(x, a_norm_g, a_w_in, a_v_norm_g, a_w_s, a_b_s, a_w_out, kv_norm_g, w_kv, b_norm_g, b_w_q, b_rel_bias, b_w_o, f_norm_g, f_w_in, f_conv_w, f_conv_b, f_w_down, final_norm_g, loss_target, m_a_norm_g, m_a_w_in, m_a_v_norm_g, m_a_w_s, m_a_b_s, m_a_w_out, m_kv_norm_g, m_w_kv, m_b_norm_g, m_b_w_q, m_b_rel_bias, m_b_w_o, m_f_norm_g, m_f_w_in, m_f_conv_w, m_f_conv_b, m_f_w_down, m_final_norm_g, v_a_norm_g, v_a_w_in, v_a_v_norm_g, v_a_w_s, v_a_b_s, v_a_w_out, v_kv_norm_g, v_w_kv, v_b_norm_g, v_b_w_q, v_b_rel_bias, v_b_w_o, v_f_norm_g, v_f_w_in, v_f_conv_w, v_f_conv_b, v_f_w_down, v_final_norm_g):
    B, S, D = x.shape
    T = B * S
    xi, yi, ci = lax.axis_index("x"), lax.axis_index("y"), lax.axis_index("c")
    j_me = (2 * xi + yi).astype(jnp.int32)
    core = ci.astype(jnp.int32)
    pos = jnp.stack([j_me, core])

    w_shards = {"a_w_in": (a_w_in, False), "a_w_out": (a_w_out, True), "w_kv": (w_kv[None], False),
                "b_w_q": (b_w_q, True), "b_w_o": (b_w_o, True), "f_w_in": (f_w_in, False), "f_w_down": (f_w_down, True)}
    names = list(w_shards)
    ws = [_W(n, w_shards[n][0], w_shards[n][1]) for n in names]
    g_shards = {"a_w_in": (a_w_in, False), "a_w_out": (a_w_out, True),
                "f_w_in0": (f_w_in[0:1], False), "f_w_down0": (f_w_down[0:1], True),
                "w_kv": (w_kv[None], False), "b_w_q": (b_w_q, True), "b_w_o": (b_w_o, True),
                "f_w_in1": (f_w_in[1:2], False), "f_w_down1": (f_w_down[1:2], True)}
    g_names = list(g_shards)
    g_ws = {n: _W(n, *g_shards[n]) for n in g_names}

    Wd = a_w_in.shape[1]
    GW = a_v_norm_g.shape[1] * N_CHIPS
    F2 = f_conv_w.shape[2] * N_CHIPS
    Fh = F2 // 2
    nsd, nsg, nsf = a_norm_g.shape[1], a_v_norm_g.shape[1], f_conv_w.shape[2]
    own = (ci == 0).astype(F32)
    place = lambda sh, width, n: lax.dynamic_update_slice_in_dim(
        jnp.zeros(sh.shape[:-1] + (width,), F32), sh * own, j_me * n, axis=sh.ndim - 1)
    gathered = _allreduce_small(_pack([place(a_norm_g, Wd, nsd), place(a_v_norm_g, GW, nsg),
                                       place(f_conv_w, F2, nsf)]))
    a_g, a_vg, conv_w = _unpack(gathered, [(1, Wd), (1, GW), (2, 3, F2)])

    flight = dict(zip(g_names, _gather_start([g_ws[n] for n in g_names],
                                             [g_shards[n][0].astype(BF16) for n in g_names], gathered)))
    full = {}

    def arrive(group, after, tag):
        gw = [g_ws[n] for n in group]
        sh, fu = _gather_wait(gw, [flight[n] for n in group], after, name=f"gather_wait_{tag}")
        full.update(zip(group, _gather_finish(gw, sh, fu, name=f"gather_finish_{tag}")))
    conv_w2 = conv_w.reshape(2, 3, 2, Fh).transpose(0, 2, 1, 3)
    conv_b2 = f_conv_b.reshape(2, 2, Fh)

    h0 = x.reshape(T, D)
    target = loss_target.reshape(T, D)
    bs_tile = jnp.repeat(a_b_s[0].T, GROUP_DIM, axis=1)
    ws_a = a_w_s[0]
    scale = HEAD_DIM ** -0.5
    HD = b_w_q.shape[2]
    H = HD // HEAD_DIM
    n_rel = b_rel_bias.shape[-1]
    frow = b_rel_bias[0][:, _bias_index()].reshape(H, 1, F_LEN)
    bias = _bias_expand(frow)

    def ffn_fwd(h, l):
        yff, a, c, n = _ffn_in_conv(h, full[f"f_w_in{l}"], f_norm_g[l], conv_w2[l], conv_b2[l], S, name=f"ffn{l}_in")
        return _mm(yff, full[f"f_w_down{l}"], layer=0, res=h, name=f"ffn{l}_down"), (a, c, n, yff)

    arrive(["a_w_in", "a_w_out"], h0, "a")
    zp, n_a = _mm(h0, full["a_w_in"], layer=0, norm_g=a_g[0], emit_norm=True, name="a_in")
    out_a = _gate_fwd(zp, a_vg, ws_a, bs_tile)
    h1 = _mm(out_a, full["a_w_out"], layer=0, res=h0, name="a_out")
    arrive(["f_w_in0", "f_w_down0"], h1, "f0")
    h2, saved0 = ffn_fwd(h1, 0)
    arrive(["w_kv", "b_w_q", "b_w_o"], h2, "b")
    arrive(["f_w_in1", "f_w_down1"], h2, "f1")
    kv, n_kv = _mm(h2, full["w_kv"], layer=0, norm_g=kv_norm_g, out_dtype=BF16, split_out=True, emit_norm=True,
                   name="kv")
    q, n_q = _mm(h2, full["b_w_q"], layer=0, norm_g=b_norm_g[0], scale=scale, out_dtype=BF16, emit_norm=True,
                 name="q")
    kv4, q3 = kv.reshape(2, B, S, HD), q.reshape(B, S, HD)
    o = _attn_fwd(q3, kv4, bias, B, S).reshape(T, HD)
    h3 = _mm(o, full["b_w_o"], layer=0, res=h2, name="attn_out")
    h4, saved1 = ffn_fwd(h3, 1)
    dh, loss8, dg_final = _loss_head(h4, final_norm_g, target)

    units = {}

    in_flight = {}

    def swap_start(group, tag):
        us = [units[n] for n in group]
        lands = [lax.empty(u.shape[1:], BF16) for u in us]
        return group, tag, _split_copies(f"swap_start_{tag}", us, lands, 1, _swap_copies, after=us[0])

    def reduce_start(swap, after):
        group, tag, flight = swap
        us, got = _split_copies(f"swap_wait_{tag}", [fl[2] for fl in flight], [fl[3] for fl in flight], 1,
                                _swap_copies, flight=flight, after=after)
        sums = [_add_pair(u, g_, core, name=f"pair_{n}") for n, u, g_ in zip(group, us, got)]
        in_flight.update(zip(group, _scatter_start(sums, name=f"scatter_start_{tag}")))

    def ffn_bwd(dh, h, saved, l, early):
        a, c, n, yff = saved
        units[f"f_w_down{l}"] = _mm_tn(yff, dh, rows_are_shards=True, name=f"ffn{l}_down_dw")
        if early:
            sw = swap_start([f"f_w_down{l}"], f"fd{l}")
        dyff = _mm(dh, full[f"f_w_down{l}"], layer=0, trans_w=True, out_dtype=BF16, name=f"ffn{l}_down_dx")
        if early:
            reduce_start(sw, dyff)
        da, dcw, dcb = _conv_bwd(a, c, dyff, conv_w2[l], S)
        units[f"f_w_in{l}"] = _mm_tn(n, da, split_y=True, name=f"ffn{l}_in_dw")
        sw = swap_start([f"f_w_in{l}"] if early else [f"f_w_down{l}", f"f_w_in{l}"], f"f{l}")
        dh, dg = _mm(da, full[f"f_w_in{l}"], layer=0, trans_w=True, split_x=True, bwd=(h, f_norm_g[l], dh), tm=256,
                     name=f"ffn{l}_in_dx")
        reduce_start(sw, dh)
        return dh, dg, dcw, dcb

    dh, dg_f1, dcw1, dcb1 = ffn_bwd(dh, h3, saved1, 1, False)
    do = _mm(dh, full["b_w_o"], layer=0, trans_w=True, out_dtype=BF16, name="attn_out_dx")
    units["b_w_o"] = _mm_tn(o, dh, rows_are_shards=True, name="b_w_o_dw")
    dq, dkv, dbias = _attn_bwd(q3, kv4, bias, do.reshape(B, S, HD), B, S)
    d_rel = _bias_reduce(dbias, n_rel).reshape(1, H, n_rel)
    dq, dkv = dq.reshape(T, HD), dkv.reshape(2, T, HD)
    units["b_w_q"] = _mm_tn(n_q, dq, rows_are_shards=True, name="b_w_q_dw")
    dh, dg_b = _mm(dq, full["b_w_q"], layer=0, trans_w=True, bwd=(h2, b_norm_g[0], dh), name="q_dx")
    units["w_kv"] = _mm_tn(n_kv, dkv, split_y=True, name="w_kv_dw")
    sw = swap_start(["b_w_o", "b_w_q", "w_kv"], "b")
    dh, dg_kv = _mm(dkv, full["w_kv"], layer=0, trans_w=True, split_x=True, bwd=(h2, kv_norm_g, dh), name="kv_dx")
    reduce_start(sw, dh)
    dh, dg_f0, dcw0, dcb0 = ffn_bwd(dh, h1, saved0, 0, True)
    units["a_w_out"] = _mm_tn(out_a, dh, rows_are_shards=True, name="a_w_out_dw")
    sw = swap_start(["a_w_out"], "ao")
    d_out = _mm(dh, full["a_w_out"], layer=0, trans_w=True, out_dtype=BF16, name="a_out_dx")
    reduce_start(sw, d_out)
    dzp, dws, dbs, dgv = _gate_bwd(zp, d_out, a_vg, ws_a, bs_tile)
    units["a_w_in"] = _mm_tn(n_a, dzp, name="a_w_in_dw")
    sw = swap_start(["a_w_in"], "ai")
    grad_x, dg_a = _mm(dzp, full["a_w_in"], layer=0, trans_w=True, bwd=(h0, a_g[0], dh), name="a_in_dx")
    reduce_start(sw, grad_x)

    to_flat = lambda d: d.transpose(1, 0, 2).reshape(3, F2)
    small = {"a_norm_g": dg_a, "a_v_norm_g": dgv, "a_w_s": dws[None], "a_b_s": dbs[None], "kv_norm_g": dg_kv[0],
             "b_norm_g": dg_b, "b_rel_bias": d_rel, "f_norm_g": jnp.concatenate([dg_f0, dg_f1], axis=0),
             "f_conv_w": jnp.stack([to_flat(dcw0), to_flat(dcw1)]),
             "f_conv_b": jnp.stack([dcb0.reshape(F2), dcb1.reshape(F2)]), "final_norm_g": dg_final[0]}
    snames = list(small)
    small_vec = _pack([small[n] for n in snames] + [loss8[0:1, 0:1]])
    small_flight = _split_copies("small_start", [small_vec], [lax.empty((8,) + small_vec.shape, F32)], 7,
                                 _gather8_copies, after=small_vec)

    sums, recv = _scatter_wait([in_flight[n] for n in g_names], grad_x)
    sums, recv = dict(zip(g_names, sums)), dict(zip(g_names, recv))
    halves = []
    for n, w in zip(names, ws):
        if w.L == 1:
            halves.append(_sum_chips(w, sums[n], recv[n], pos, name=f"chips_{n}"))
        else:
            first = _sum_chips(w, sums[n + "0"], recv[n + "0"], pos, name=f"chips_{n}0")
            halves.append(_sum_chips(w, sums[n + "1"], recv[n + "1"], pos, layer=1, into=first, name=f"chips_{n}1"))
    g_big = dict(zip(names, _join_halves(ws, halves)))
    g_big["w_kv"] = g_big["w_kv"][0]

    given = dict(a_norm_g=(a_norm_g, m_a_norm_g, v_a_norm_g), a_w_in=(a_w_in, m_a_w_in, v_a_w_in),
                 a_v_norm_g=(a_v_norm_g, m_a_v_norm_g, v_a_v_norm_g), a_w_s=(a_w_s, m_a_w_s, v_a_w_s),
                 a_b_s=(a_b_s, m_a_b_s, v_a_b_s), a_w_out=(a_w_out, m_a_w_out, v_a_w_out),
                 kv_norm_g=(kv_norm_g, m_kv_norm_g, v_kv_norm_g), w_kv=(w_kv, m_w_kv, v_w_kv),
                 b_norm_g=(b_norm_g, m_b_norm_g, v_b_norm_g), b_w_q=(b_w_q, m_b_w_q, v_b_w_q),
                 b_rel_bias=(b_rel_bias, m_b_rel_bias, v_b_rel_bias), b_w_o=(b_w_o, m_b_w_o, v_b_w_o),
                 f_norm_g=(f_norm_g, m_f_norm_g, v_f_norm_g), f_w_in=(f_w_in, m_f_w_in, v_f_w_in),
                 f_conv_w=(f_conv_w, m_f_conv_w, v_f_conv_w), f_conv_b=(f_conv_b, m_f_conv_b, v_f_conv_b),
                 f_w_down=(f_w_down, m_f_w_down, v_f_w_down), final_norm_g=(final_norm_g, m_final_norm_g, v_final_norm_g))
    order = list(given)
    grads, deltas, new_m, new_v = {}, {}, {}, {}
    for n in names:
        w_, m_, v_ = given[n]
        g_ = g_big[n]
        C = w_.shape[-1]
        d2, m2, v2 = _adamw(w_.reshape(-1, C), g_.reshape(-1, C), m_.reshape(-1, C), v_.reshape(-1, C),
                            name=f"adamw_{n}")
        grads[n], deltas[n], new_m[n], new_v[n] = g_.reshape(w_.shape), d2.reshape(w_.shape), m2.reshape(w_.shape), \
            v2.reshape(w_.shape)
    vecs, lands = _split_copies("small_wait", [small_flight[0][2]], [small_flight[0][3]], 7, _gather8_copies,
                                flight=small_flight, after=deltas[names[-1]])
    red = _sum8(lands[0], vecs[0], (4 * xi + 2 * yi + ci).astype(jnp.int32))
    parts = _unpack(red, [small[n].shape for n in snames] + [(1,)])
    g_small = dict(zip(snames, parts[:-1]))
    loss = parts[-1][0]
    g_small["a_norm_g"] = lax.dynamic_slice_in_dim(g_small["a_norm_g"], j_me * nsd, nsd, axis=1)
    g_small["a_v_norm_g"] = lax.dynamic_slice_in_dim(g_small["a_v_norm_g"], j_me * nsg, nsg, axis=1)
    g_small["f_conv_w"] = lax.dynamic_slice_in_dim(g_small["f_conv_w"], j_me * nsf, nsf, axis=2)

    sm = [n for n in order if n not in names]
    d2, m2, v2 = _adamw(_pack([given[n][0] for n in sm]), _pack([g_small[n].reshape(given[n][0].shape) for n in sm]),
                        _pack([given[n][1] for n in sm]), _pack([given[n][2] for n in sm]), name="adamw_small")
    shapes = [given[n][0].shape for n in sm]
    for n, d_, m_, v_ in zip(sm, _unpack(d2, shapes), _unpack(m2, shapes), _unpack(v2, shapes)):
        grads[n], deltas[n], new_m[n], new_v[n] = g_small[n].reshape(given[n][0].shape), d_, m_, v_

    return (loss, grad_x.reshape(B, S, D), *[grads[n] for n in order], *[deltas[n] for n in order],
            *[new_m[n] for n in order], *[new_v[n] for n in order])
```

```python
import math

import numpy as np
import jax
import jax.numpy as jnp
from jax import lax
from jax.experimental import pallas as pl
from jax.experimental.pallas import tpu as pltpu

F32 = jnp.float32
BF16 = jnp.bfloat16
MESH = pl.DeviceIdType.MESH

EPS = 1e-6
NEG_INF = -1e30
CHUNK = 64
GMLP_BLOCK = 128
GROUP_DIM = 128
HEAD_DIM = 64
LEFT_CHUNKS = 8
PAD = LEFT_CHUNKS * CHUNK
REL_CLIP = 128
Q_BLOCK = 256
K_SPAN = PAD + Q_BLOCK
F_LEN = K_SPAN + Q_BLOCK
HEADS_PER_STEP = 4
N_CHIPS = 4

ADAM_LR = 0.001
ADAM_B1 = 0.9
ADAM_B2 = 0.999
ADAM_EPS = 1e-08
ADAM_WD = 0.01
ADAM_STEP = 10

VMEM_LIMIT = 56 * 1024 * 1024


def _params(sem=None, **kw):
    if sem is not None:
        kw["dimension_semantics"] = sem
    return pltpu.CompilerParams(vmem_limit_bytes=VMEM_LIMIT, **kw)


def _rms(xf):
    r = lax.rsqrt(jnp.mean(xf * xf, axis=-1, keepdims=True) + EPS)
    return xf * r, r


def _gelu(x):
    c = math.sqrt(2.0 / math.pi)
    return 0.5 * x * (1.0 + jnp.tanh(c * (x + 0.044715 * x * x * x)))


def _gelu_grad(x):
    c = math.sqrt(2.0 / math.pi)
    t = jnp.tanh(c * (x + 0.044715 * x * x * x))
    return 0.5 * (1.0 + t) + 0.5 * x * (1.0 - t * t) * c * (1.0 + 3.0 * 0.044715 * x * x)


def _col_tile(n):
    if n <= 1024:
        return n
    for t in (1408, 1024, 512):
        if n % t == 0:
            return t
    raise ValueError(n)


def _row_tile(t, want):
    while t % want:
        want //= 2
    return want


def _mm(x, w, *, name, layer=None, trans_w=False, norm_g=None, res=None, scale=None, out_dtype=F32, bwd=None,
        split_out=False, split_x=False, emit_norm=False, tm=512):
    T = x.shape[-2]
    K = 2 * x.shape[-1] if split_x else x.shape[-1]
    N = w.shape[-2] if trans_w else w.shape[-1]
    tn = N
    tm = _row_tile(T, 256 if N > 4096 else tm)
    nn, nm = N // tn, T // tm
    has_norm, has_res, has_bwd = norm_g is not None, res is not None, bwd is not None
    dims = (((1,), (1,)), ((), ())) if trans_w else (((1,), (0,)), ((), ()))

    def body(*refs):
        it = iter(refs)
        x_ref, w_ref = next(it), next(it)
        g_ref = next(it) if has_norm else None
        res_ref = next(it) if has_res else None
        if has_bwd:
            h_ref, bg_ref, dh_ref = next(it), next(it), next(it)
        o_ref = next(it)
        if split_x:
            kh = K // 2
            acc = lax.dot_general(x_ref[0].astype(BF16), w_ref[:, :kh] if trans_w else w_ref[:kh, :], dims,
                                  preferred_element_type=F32)
            acc = acc + lax.dot_general(x_ref[1].astype(BF16), w_ref[:, kh:] if trans_w else w_ref[kh:, :], dims,
                                        preferred_element_type=F32)
        else:
            xv = x_ref[...]
            if has_norm:
                xv = _rms(xv.astype(F32))[0] * g_ref[...]
            xb = xv.astype(BF16)
            if emit_norm:
                refs[-1][...] = xb
            acc = lax.dot_general(xb, w_ref[...], dims, preferred_element_type=F32)
        if scale is not None:
            acc = acc * scale
        if has_res:
            acc = acc + res_ref[...]
        if has_bwd:
            dg_ref = next(it)
            n, r = _rms(h_ref[...])

            @pl.when(pl.program_id(1) == 0)
            def _():
                dg_ref[...] = jnp.zeros_like(dg_ref)

            dg_ref[...] += jnp.sum(acc * n, axis=0, keepdims=True)
            t = acc * bg_ref[...]
            o_ref[...] = dh_ref[...] + r * (t - n * jnp.mean(t * n, axis=-1, keepdims=True))
        elif split_out:
            o_ref[0] = acc[:, :N // 2].astype(out_dtype)
            o_ref[1] = acc[:, N // 2:].astype(out_dtype)
        else:
            o_ref[...] = acc.astype(out_dtype)

    lead = () if layer is None else (None,)
    lidx = () if layer is None else (layer,)
    ins = [x, w]
    xspec = (pl.BlockSpec((2, tm, K // 2), lambda n, m: (0, m, 0)) if split_x
             else pl.BlockSpec((tm, K), lambda n, m: (m, 0)))
    wspec = (pl.BlockSpec(lead + (tn, K), lambda n, m: lidx + (n, 0)) if trans_w
             else pl.BlockSpec(lead + (K, tn), lambda n, m: lidx + (0, n)))
    in_specs = [xspec, wspec]
    if has_norm:
        ins.append(norm_g.reshape(1, K))
        in_specs.append(pl.BlockSpec((1, K), lambda n, m: (0, 0)))
    if has_res:
        ins.append(res)
        in_specs.append(pl.BlockSpec((tm, tn), lambda n, m: (m, n)))
    if split_out:
        out_shape = [jax.ShapeDtypeStruct((2, T, N // 2), out_dtype)]
        out_specs = [pl.BlockSpec((2, tm, N // 2), lambda n, m: (0, m, 0))]
    else:
        out_shape = [jax.ShapeDtypeStruct((T, N), F32 if has_bwd else out_dtype)]
        out_specs = [pl.BlockSpec((tm, tn), lambda n, m: (m, n))]
    if has_bwd:
        h, g, dh = bwd
        ins += [h, g.reshape(1, N), dh]
        in_specs += [pl.BlockSpec((tm, N), lambda n, m: (m, 0)), pl.BlockSpec((1, N), lambda n, m: (0, 0)),
                     pl.BlockSpec((tm, N), lambda n, m: (m, 0))]
        out_shape.append(jax.ShapeDtypeStruct((1, N), F32))
        out_specs.append(pl.BlockSpec((1, N), lambda n, m: (0, 0)))
    if emit_norm:
        out_shape.append(jax.ShapeDtypeStruct((T, K), BF16))
        out_specs.append(pl.BlockSpec((tm, K), lambda n, m: (m, 0)))
    out = pl.pallas_call(body, name=name, grid=(nn, nm), in_specs=in_specs, out_specs=out_specs, out_shape=out_shape,
                         compiler_params=_params(("arbitrary", "arbitrary")))(*ins)
    return out if has_bwd or emit_norm else out[0]


def _mm_tn(x, dy, *, name, rows_are_shards=False, split_y=False, tt=512):
    T, K = x.shape
    N = 2 * dy.shape[-1] if split_y else dy.shape[-1]
    R, C = (K // N_CHIPS, N // 2) if rows_are_shards else (K // 2, N // N_CHIPS)
    nn = 2 if split_y else 1
    tn = N // nn
    per = N_CHIPS // nn
    assert not (rows_are_shards and split_y)
    tt = _row_tile(T, tt)
    nt = T // tt

    def body(x_ref, y_ref, o_ref, acc_ref):
        t = pl.program_id(1)

        @pl.when(t == 0)
        def _():
            acc_ref[...] = jnp.zeros_like(acc_ref)

        acc_ref[...] += lax.dot_general(x_ref[...], y_ref[...].astype(BF16), (((0,), (0,)), ((), ())),
                                        preferred_element_type=F32)

        @pl.when(t == nt - 1)
        def _():
            if rows_are_shards:
                for h in range(2):
                    o_ref[h] = acc_ref[:, h * C:(h + 1) * C].astype(BF16).reshape(N_CHIPS, R, C)
            else:
                for j in range(per):
                    o_ref[:, j] = acc_ref[:, j * C:(j + 1) * C].astype(BF16).reshape(2, R, C)

    if split_y:
        yspec = pl.BlockSpec((None, tt, tn), lambda n, t: (n, t, 0))
    else:
        yspec = pl.BlockSpec((tt, tn), lambda n, t: (t, 0))
    if rows_are_shards:
        out_spec = pl.BlockSpec((2, N_CHIPS, R, C), lambda n, t: (0, 0, 0, 0))
    else:
        out_spec = pl.BlockSpec((2, per, R, C), lambda n, t: (0, n, 0, 0))
    return pl.pallas_call(body, name=name, grid=(nn, nt),
                          in_specs=[pl.BlockSpec((tt, K), lambda n, t: (t, 0)), yspec], out_specs=out_spec,
                          out_shape=jax.ShapeDtypeStruct((2, N_CHIPS, R, C), BF16),
                          scratch_shapes=[pltpu.VMEM((K, tn), F32)],
                          compiler_params=_params(("arbitrary", "arbitrary")))(x, dy)


def _chunk_mask():
    i = lax.broadcasted_iota(jnp.int32, (GMLP_BLOCK, GMLP_BLOCK), 0) // CHUNK
    j = lax.broadcasted_iota(jnp.int32, (GMLP_BLOCK, GMLP_BLOCK), 1) // CHUNK
    return i >= j


def _gate_fwd(zp, gv, ws, bs_tile, *, tm=256):
    T, W2 = zp.shape
    W = W2 // 2
    G = W // GROUP_DIM
    tm = _row_tile(T, tm)

    def body(zp_ref, gv_ref, ws_ref, bs_ref, o_ref):
        z = _gelu(zp_ref[...])
        u, v = z[:, :W], z[:, W:]
        vn = _rms(v)[0] * gv_ref[...]
        mask = _chunk_mask()
        for g in range(G):
            cs = slice(g * GROUP_DIM, (g + 1) * GROUP_DIM)
            wg = jnp.where(mask, ws_ref[g], 0.0).astype(BF16)
            for b in range(tm // GMLP_BLOCK):
                rs = slice(b * GMLP_BLOCK, (b + 1) * GMLP_BLOCK)
                s = jnp.dot(wg, vn[rs, cs].astype(BF16), preferred_element_type=F32) + bs_ref[:, cs]
                o_ref[rs, cs] = (u[rs, cs] * s).astype(BF16)

    return pl.pallas_call(
        body, name="gate_fwd", grid=(T // tm,),
        in_specs=[pl.BlockSpec((tm, W2), lambda i: (i, 0)), pl.BlockSpec((1, W), lambda i: (0, 0)),
                  pl.BlockSpec((G, GMLP_BLOCK, GMLP_BLOCK), lambda i: (0, 0, 0)),
                  pl.BlockSpec((GMLP_BLOCK, W), lambda i: (0, 0))],
        out_specs=pl.BlockSpec((tm, W), lambda i: (i, 0)), out_shape=jax.ShapeDtypeStruct((T, W), BF16),
        compiler_params=_params(("arbitrary",)))(zp, gv, ws, bs_tile)


def _gate_bwd(zp, d_out, gv, ws, bs_tile, *, tm=256):
    T, W2 = zp.shape
    W = W2 // 2
    G = W // GROUP_DIM
    tm = _row_tile(T, tm)
    nm = T // tm

    def body(zp_ref, do_ref, gv_ref, ws_ref, bs_ref, dzp_ref, dws_ref, dbs_ref, dgv_ref, du_scr, dvn_scr, dsum_scr):
        i = pl.program_id(0)

        @pl.when(i == 0)
        def _():
            dws_ref[...] = jnp.zeros_like(dws_ref)
            dgv_ref[...] = jnp.zeros_like(dgv_ref)
            dsum_scr[...] = jnp.zeros_like(dsum_scr)

        zp = zp_ref[...]
        z = _gelu(zp)
        u, v = z[:, :W], z[:, W:]
        n, r = _rms(v)
        gv = gv_ref[...]
        vn = n * gv
        d_out = do_ref[...].astype(F32)
        mask = _chunk_mask()
        for g in range(G):
            cs = slice(g * GROUP_DIM, (g + 1) * GROUP_DIM)
            wg = jnp.where(mask, ws_ref[g], 0.0).astype(BF16)
            dw = jnp.zeros((GMLP_BLOCK, GMLP_BLOCK), F32)
            for b in range(tm // GMLP_BLOCK):
                rs = slice(b * GMLP_BLOCK, (b + 1) * GMLP_BLOCK)
                vb = vn[rs, cs].astype(BF16)
                s = jnp.dot(wg, vb, preferred_element_type=F32) + bs_ref[:, cs]
                du_scr[rs, cs] = d_out[rs, cs] * s
                ds = d_out[rs, cs] * u[rs, cs]
                dsb = ds.astype(BF16)
                dvn_scr[rs, cs] = lax.dot_general(wg, dsb, (((0,), (0,)), ((), ())), preferred_element_type=F32)
                dw = dw + lax.dot_general(dsb, vb, (((1,), (1,)), ((), ())), preferred_element_type=F32)
                dsum_scr[:, cs] += ds
            dws_ref[g] += jnp.where(mask, dw, 0.0)
        dvn = dvn_scr[...]
        dgv_ref[...] += jnp.sum(dvn * n, axis=0, keepdims=True)
        t = dvn * gv
        dv = r * (t - n * jnp.mean(t * n, axis=-1, keepdims=True))
        dzp_ref[:, :W] = (du_scr[...] * _gelu_grad(zp[:, :W])).astype(BF16)
        dzp_ref[:, W:] = (dv * _gelu_grad(zp[:, W:])).astype(BF16)

        @pl.when(i == nm - 1)
        def _():
            sel = (lax.broadcasted_iota(jnp.int32, (G, W), 1) // GROUP_DIM
                   == lax.broadcasted_iota(jnp.int32, (G, W), 0)).astype(F32)
            dbs_ref[...] = lax.dot_general(sel, dsum_scr[...], (((1,), (1,)), ((), ())),
                                           precision=lax.Precision.HIGHEST, preferred_element_type=F32)

    return pl.pallas_call(
        body, name="gate_bwd", grid=(nm,),
        in_specs=[pl.BlockSpec((tm, W2), lambda i: (i, 0)), pl.BlockSpec((tm, W), lambda i: (i, 0)),
                  pl.BlockSpec((1, W), lambda i: (0, 0)),
                  pl.BlockSpec((G, GMLP_BLOCK, GMLP_BLOCK), lambda i: (0, 0, 0)),
                  pl.BlockSpec((GMLP_BLOCK, W), lambda i: (0, 0))],
        out_specs=[pl.BlockSpec((tm, W2), lambda i: (i, 0)),
                   pl.BlockSpec((G, GMLP_BLOCK, GMLP_BLOCK), lambda i: (0, 0, 0)),
                   pl.BlockSpec((G, GMLP_BLOCK), lambda i: (0, 0)), pl.BlockSpec((1, W), lambda i: (0, 0))],
        out_shape=[jax.ShapeDtypeStruct((T, W2), BF16), jax.ShapeDtypeStruct((G, GMLP_BLOCK, GMLP_BLOCK), F32),
                   jax.ShapeDtypeStruct((G, GMLP_BLOCK), F32), jax.ShapeDtypeStruct((1, W), F32)],
        scratch_shapes=[pltpu.VMEM((tm, W), F32), pltpu.VMEM((tm, W), F32), pltpu.VMEM((GMLP_BLOCK, W), F32)],
        compiler_params=_params(("arbitrary",)))(zp, d_out, gv, ws, bs_tile)


HALO = 16


def _taps(ext, w, b):
    a, a1, a2 = ext[HALO:], pltpu.roll(ext, 1, 0)[HALO:], pltpu.roll(ext, 2, 0)[HALO:]
    return w[2:3] * a + w[1:2] * a1 + w[0:1] * a2 + b, a, a1, a2


def _conv_fwd(a, cw, cb, S, *, tm=256):
    _, T, F = a.shape
    tc = _col_tile(F)
    tm = _row_tile(S, tm)
    hb = tm // HALO

    def body(a_ref, p_ref, w_ref, b_ref, o_ref, c_ref):
        first = (pl.program_id(1) * tm) % S == 0
        keep = jnp.where(first, 0.0, 1.0)

        def conv(s):
            ext = jnp.concatenate([p_ref[s].astype(F32) * keep, a_ref[s].astype(F32)], axis=0)
            c = _taps(ext, w_ref[s], b_ref[s:s + 1, :])[0].astype(BF16)
            c_ref[s] = c
            return c.astype(F32)

        up, gate = conv(0), conv(1)
        o_ref[...] = (gate * jax.nn.sigmoid(gate) * up).astype(BF16)

    return pl.pallas_call(
        body, name="conv_fwd", grid=(F // tc, T // tm),
        in_specs=[pl.BlockSpec((2, tm, tc), lambda j, i: (0, i, j)),
                  pl.BlockSpec((2, HALO, tc), lambda j, i: (0, jnp.maximum(i * hb - 1, 0), j)),
                  pl.BlockSpec((2, 3, tc), lambda j, i: (0, 0, j)), pl.BlockSpec((2, tc), lambda j, i: (0, j))],
        out_specs=[pl.BlockSpec((tm, tc), lambda j, i: (i, j)), pl.BlockSpec((2, tm, tc), lambda j, i: (0, i, j))],
        out_shape=[jax.ShapeDtypeStruct((T, F), BF16), jax.ShapeDtypeStruct((2, T, F), BF16)],
        compiler_params=_params(("arbitrary", "arbitrary")))(a, a, cw, cb)


def _ffn_in_conv(h, w, g, cw, cb, S, *, name, tm=256):
    T, D = h.shape
    F = w.shape[-1] // 2
    tc = _col_tile(F)
    tm = _row_tile(S, tm)

    def body(h_ref, w_ref, g_ref, cw_ref, cb_ref, y_ref, a_ref, c_ref, n_ref, tail):
        first = (pl.program_id(0) * tm) % S == 0
        nb = (_rms(h_ref[...])[0] * g_ref[...]).astype(BF16)
        n_ref[...] = nb
        for j in range(F // tc):
            cs = slice(j * tc, (j + 1) * tc)
            conv = []
            for s in range(2):
                acc = jnp.dot(nb, w_ref[:, s * F + j * tc:s * F + (j + 1) * tc], preferred_element_type=F32)
                ab = acc.astype(BF16)
                a_ref[s, :, cs] = ab
                af = ab.astype(F32)
                ext = jnp.concatenate([jnp.where(first, 0.0, tail[s, :, cs]), af], axis=0)
                tail[s, :, cs] = af[tm - HALO:, :]
                cv = _taps(ext, cw_ref[s, :, cs], cb_ref[s:s + 1, cs])[0].astype(BF16)
                c_ref[s, :, cs] = cv
                conv.append(cv.astype(F32))
            up, gate = conv
            y_ref[:, cs] = (gate * jax.nn.sigmoid(gate) * up).astype(BF16)

    row = lambda width: pl.BlockSpec((tm, width), lambda i: (i, 0))
    wide = pl.BlockSpec((2, tm, F), lambda i: (0, i, 0))
    return pl.pallas_call(
        body, name=name, grid=(T // tm,),
        in_specs=[row(D), pl.BlockSpec((None, D, 2 * F), lambda i: (0, 0, 0)), pl.BlockSpec((1, D), lambda i: (0, 0)),
                  pl.BlockSpec((2, 3, F), lambda i: (0, 0, 0)), pl.BlockSpec((2, F), lambda i: (0, 0))],
        out_specs=[row(F), wide, wide, row(D)],
        out_shape=[jax.ShapeDtypeStruct((T, F), BF16), jax.ShapeDtypeStruct((2, T, F), BF16),
                   jax.ShapeDtypeStruct((2, T, F), BF16), jax.ShapeDtypeStruct((T, D), BF16)],
        scratch_shapes=[pltpu.VMEM((2, HALO, F), F32)],
        compiler_params=_params(("arbitrary",)))(h, w, g.reshape(1, D), cw, cb)


def _conv_bwd(a, c, dy, cw, S, *, tm=256):
    _, T, F = a.shape
    tc = _col_tile(F)
    tm = _row_tile(S, tm)
    nm = T // tm
    hb = tm // HALO
    TE = tm + HALO
    nxt = lambda j, i: jnp.minimum((i + 1) * hb, T // HALO - 1)

    def body(a_ref, c_ref, nc_ref, dy_ref, ndy_ref, w_ref, da_ref, dw_ref, db_ref):
        i = pl.program_id(1)
        last = ((i + 1) * tm) % S == 0
        keep_n = jnp.where(last, 0.0, 1.0)
        dyf = jnp.concatenate([dy_ref[...].astype(F32), ndy_ref[...].astype(F32) * keep_n], axis=0)
        up = jnp.concatenate([c_ref[0].astype(F32), nc_ref[0].astype(F32)], axis=0)
        gate = jnp.concatenate([c_ref[1].astype(F32), nc_ref[1].astype(F32)], axis=0)
        sg = jax.nn.sigmoid(gate)
        d_up = dyf * (gate * sg)
        d_gate = dyf * up * (sg * (1.0 + gate * (1.0 - sg)))

        @pl.when(i == 0)
        def _():
            dw_ref[...] = jnp.zeros_like(dw_ref)
            db_ref[...] = jnp.zeros_like(db_ref)

        def back(s, d):
            a = a_ref[s].astype(F32)
            w = w_ref[s]
            u1, u2 = pltpu.roll(d, TE - 1, 0), pltpu.roll(d, TE - 2, 0)
            db_ref[s:s + 1, :] += jnp.sum(d[:tm], axis=0, keepdims=True)
            dw_ref[s, 2:3, :] += jnp.sum(d[:tm] * a, axis=0, keepdims=True)
            dw_ref[s, 1:2, :] += jnp.sum(u1[:tm] * a, axis=0, keepdims=True)
            dw_ref[s, 0:1, :] += jnp.sum(u2[:tm] * a, axis=0, keepdims=True)
            da_ref[s] = (w[2:3] * d + w[1:2] * u1 + w[0:1] * u2)[:tm].astype(BF16)

        back(0, d_up)
        back(1, d_gate)

    cur = pl.BlockSpec((2, tm, tc), lambda j, i: (0, i, j))
    return pl.pallas_call(
        body, name="conv_bwd", grid=(F // tc, nm),
        in_specs=[cur, cur, pl.BlockSpec((2, HALO, tc), lambda j, i: (0, nxt(j, i), j)),
                  pl.BlockSpec((tm, tc), lambda j, i: (i, j)), pl.BlockSpec((HALO, tc), lambda j, i: (nxt(j, i), j)),
                  pl.BlockSpec((2, 3, tc), lambda j, i: (0, 0, j))],
        out_specs=[cur, pl.BlockSpec((2, 3, tc), lambda j, i: (0, 0, j)), pl.BlockSpec((2, tc), lambda j, i: (0, j))],
        out_shape=[jax.ShapeDtypeStruct((2, T, F), BF16), jax.ShapeDtypeStruct((2, 3, F), F32),
                   jax.ShapeDtypeStruct((2, F), F32)],
        compiler_params=_params(("arbitrary", "arbitrary")))(a, c, c, dy, dy, cw)


def _bias_index():
    idx = np.arange(F_LEN)
    d = np.where(idx < K_SPAN, idx, idx - F_LEN)
    return np.clip(PAD - d, -REL_CLIP, REL_CLIP) + REL_CLIP


def _roll_rows(x, sign):
    rows = lax.broadcasted_iota(jnp.int32, x.shape, 0)
    step = 1
    while step < Q_BLOCK:
        shift = step if sign > 0 else F_LEN - step
        x = jnp.where((rows & step) != 0, pltpu.roll(x, shift, 1), x)
        step *= 2
    return x


def _bias_expand(frow):
    H = frow.shape[0]

    def body(f_ref, o_ref):
        x = _roll_rows(jnp.broadcast_to(f_ref[...], (Q_BLOCK, F_LEN)), 1)[:, :K_SPAN]
        qc = lax.broadcasted_iota(jnp.int32, (Q_BLOCK, K_SPAN), 0) // CHUNK * CHUNK
        kj = lax.broadcasted_iota(jnp.int32, (Q_BLOCK, K_SPAN), 1)
        o_ref[...] = jnp.where((kj >= qc) & (kj < qc + PAD + CHUNK), x, NEG_INF)

    return pl.pallas_call(
        body, name="bias_expand", grid=(H,),
        in_specs=[pl.BlockSpec((None, 1, F_LEN), lambda h: (h, 0, 0))],
        out_specs=pl.BlockSpec((None, Q_BLOCK, K_SPAN), lambda h: (h, 0, 0)),
        out_shape=jax.ShapeDtypeStruct((H, Q_BLOCK, K_SPAN), F32), compiler_params=_params(("arbitrary",)))(frow)


def _bias_reduce(dbias, n_rel):
    H = dbias.shape[0]
    onehot = jnp.asarray((_bias_index()[:, None] == np.arange(n_rel)[None, :]).astype(np.float32))

    def body(d_ref, oh_ref, o_ref):
        x = jnp.concatenate([d_ref[...], jnp.zeros((Q_BLOCK, F_LEN - K_SPAN), F32)], axis=1)
        row = jnp.sum(_roll_rows(x, -1), axis=0, keepdims=True)
        row8 = jnp.broadcast_to(row, (8, F_LEN))
        o_ref[...] = jnp.dot(row8, oh_ref[...], precision=lax.Precision.HIGHEST, preferred_element_type=F32)[0:1]

    return pl.pallas_call(
        body, name="bias_reduce", grid=(H,),
        in_specs=[pl.BlockSpec((None, Q_BLOCK, K_SPAN), lambda h: (h, 0, 0)),
                  pl.BlockSpec((F_LEN, n_rel), lambda h: (0, 0))],
        out_specs=pl.BlockSpec((None, 1, n_rel), lambda h: (h, 0, 0)),
        out_shape=jax.ShapeDtypeStruct((H, 1, n_rel), F32), compiler_params=_params(("arbitrary",)))(dbias, onehot)


def _attn_specs(S):
    hw = HEADS_PER_STEP * HEAD_DIM
    qspec = pl.BlockSpec((None, Q_BLOCK, hw), lambda g, b, i: (b, i, g))
    kspec = pl.BlockSpec((None, None, S, hw), lambda g, b, i: (0, b, 0, g))
    vspec = pl.BlockSpec((None, None, S, hw), lambda g, b, i: (1, b, 0, g))
    bspec = pl.BlockSpec((HEADS_PER_STEP, Q_BLOCK, K_SPAN), lambda g, b, i: (g, 0, 0))
    return hw, qspec, kspec, vspec, bspec


def _load_padded(k_ref, v_ref, kp, vp):
    kp[:PAD, :] = jnp.zeros((PAD, kp.shape[1]), BF16)
    vp[:PAD, :] = jnp.zeros((PAD, vp.shape[1]), BF16)
    kp[PAD:, :] = k_ref[...]
    vp[PAD:, :] = v_ref[...]


def _attn_exp(q_ref, kp, b_ref, h, q0, before):
    hs = slice(h * HEAD_DIM, (h + 1) * HEAD_DIM)
    kh = kp[pl.ds(q0, K_SPAN), hs]
    s = lax.dot_general(q_ref[:, hs], kh, (((1,), (1,)), ((), ())), preferred_element_type=F32) + b_ref[h] + before
    p = jnp.exp(s - jnp.max(s, axis=-1, keepdims=True))
    return p, 1.0 / jnp.sum(p, axis=-1, keepdims=True), kh


def _before_start(q0):
    kj = lax.broadcasted_iota(jnp.int32, (1, K_SPAN), 1)
    return jnp.where(q0 + kj >= PAD, 0.0, NEG_INF)


def _attn_fwd(q, kv, bias, B, S):
    HD = q.shape[-1]
    hw, qspec, kspec, vspec, bspec = _attn_specs(S)

    def body(q_ref, k_ref, v_ref, b_ref, o_ref, kp, vp):
        i = pl.program_id(2)

        @pl.when(i == 0)
        def _():
            _load_padded(k_ref, v_ref, kp, vp)

        q0 = pl.multiple_of(i * Q_BLOCK, Q_BLOCK)
        before = _before_start(q0)
        outs = []
        for h in range(HEADS_PER_STEP):
            hs = slice(h * HEAD_DIM, (h + 1) * HEAD_DIM)
            p, inv, _ = _attn_exp(q_ref, kp, b_ref, h, q0, before)
            outs.append(jnp.dot(p.astype(BF16), vp[pl.ds(q0, K_SPAN), hs], preferred_element_type=F32) * inv)
        o_ref[...] = jnp.concatenate(outs, axis=1).astype(BF16)

    return pl.pallas_call(
        body, name="attn_fwd", grid=(HD // hw, B, S // Q_BLOCK), in_specs=[qspec, kspec, vspec, bspec],
        out_specs=qspec, out_shape=jax.ShapeDtypeStruct((B, S, HD), BF16),
        scratch_shapes=[pltpu.VMEM((S + PAD, hw), BF16), pltpu.VMEM((S + PAD, hw), BF16)],
        compiler_params=_params(("arbitrary", "arbitrary", "arbitrary")))(q, kv, kv, bias)


def _attn_bwd(q, kv, bias, do, B, S):
    HD = q.shape[-1]
    H = HD // HEAD_DIM
    hw, qspec, kspec, vspec, bspec = _attn_specs(S)
    scale = HEAD_DIM ** -0.5
    nq = S // Q_BLOCK

    def body(q_ref, k_ref, v_ref, b_ref, do_ref, dq_ref, dkv_ref, db_ref, kp, vp, dk_acc, dv_acc):
        b, i = pl.program_id(1), pl.program_id(2)
        q0 = pl.multiple_of(i * Q_BLOCK, Q_BLOCK)

        @pl.when(i == 0)
        def _():
            _load_padded(k_ref, v_ref, kp, vp)
            dk_acc[...] = jnp.zeros_like(dk_acc)
            dv_acc[...] = jnp.zeros_like(dv_acc)

        @pl.when((i == 0) & (b == 0))
        def _():
            db_ref[...] = jnp.zeros_like(db_ref)

        before = _before_start(q0)
        for h in range(HEADS_PER_STEP):
            hs = slice(h * HEAD_DIM, (h + 1) * HEAD_DIM)
            p, inv, kh = _attn_exp(q_ref, kp, b_ref, h, q0, before)
            p = p * inv
            doh = do_ref[:, hs]
            dp = lax.dot_general(doh, vp[pl.ds(q0, K_SPAN), hs], (((1,), (1,)), ((), ())),
                                 preferred_element_type=F32)
            ds = p * (dp - jnp.sum(p * dp, axis=-1, keepdims=True))
            db_ref[h] += ds
            dsb = ds.astype(BF16)
            dq_ref[:, hs] = (jnp.dot(dsb, kh, preferred_element_type=F32) * scale).astype(BF16)
            dk_acc[pl.ds(q0, K_SPAN), hs] += lax.dot_general(dsb, q_ref[:, hs], (((0,), (0,)), ((), ())),
                                                              preferred_element_type=F32)
            dv_acc[pl.ds(q0, K_SPAN), hs] += lax.dot_general(p.astype(BF16), doh, (((0,), (0,)), ((), ())),
                                                              preferred_element_type=F32)

        @pl.when(i == nq - 1)
        def _():
            dkv_ref[0] = dk_acc[PAD:, :].astype(BF16)
            dkv_ref[1] = dv_acc[PAD:, :].astype(BF16)

    return pl.pallas_call(
        body, name="attn_bwd", grid=(HD // hw, B, nq), in_specs=[qspec, kspec, vspec, bspec, qspec],
        out_specs=[qspec, pl.BlockSpec((2, None, S, hw), lambda g, b, i: (0, b, 0, g)), bspec],
        out_shape=[jax.ShapeDtypeStruct((B, S, HD), BF16), jax.ShapeDtypeStruct((2, B, S, HD), BF16),
                   jax.ShapeDtypeStruct((H, Q_BLOCK, K_SPAN), F32)],
        scratch_shapes=[pltpu.VMEM((S + PAD, hw), BF16), pltpu.VMEM((S + PAD, hw), BF16),
                        pltpu.VMEM((S + PAD, hw), F32), pltpu.VMEM((S + PAD, hw), F32)],
        compiler_params=_params(("arbitrary", "arbitrary", "arbitrary")))(q, kv, kv, bias, do)


def _loss_head(h, g, target, *, tm=512):
    T, D = h.shape
    tm = _row_tile(T, tm)

    def body(h_ref, g_ref, t_ref, dh_ref, loss_ref, dg_ref):
        @pl.when(pl.program_id(0) == 0)
        def _():
            loss_ref[...] = jnp.zeros_like(loss_ref)
            dg_ref[...] = jnp.zeros_like(dg_ref)

        n, r = _rms(h_ref[...])
        g = g_ref[...]
        e = n * g - t_ref[...]
        loss_ref[...] += 0.5 * jnp.sum(jnp.mean(e * e, axis=-1, keepdims=True), axis=0, keepdims=True)
        dy = e * (1.0 / D)
        dg_ref[...] += jnp.sum(dy * n, axis=0, keepdims=True)
        t = dy * g
        dh_ref[...] = r * (t - n * jnp.mean(t * n, axis=-1, keepdims=True))

    row = pl.BlockSpec((tm, D), lambda i: (i, 0))
    return pl.pallas_call(
        body, name="loss_head", grid=(T // tm,), in_specs=[row, pl.BlockSpec((1, D), lambda i: (0, 0)), row],
        out_specs=[row, pl.BlockSpec((8, 128), lambda i: (0, 0)), pl.BlockSpec((1, D), lambda i: (0, 0))],
        out_shape=[jax.ShapeDtypeStruct((T, D), F32), jax.ShapeDtypeStruct((8, 128), F32),
                   jax.ShapeDtypeStruct((1, D), F32)],
        compiler_params=_params(("arbitrary",)))(h, g.reshape(1, D), target)


def _sub_rows(R):
    for cand in (256, 352, 128, 64, 8):
        if R % cand == 0 and R > cand:
            return cand
    return R


def _adamw(w, g, m, v, *, name):
    R, C = w.shape
    tr = _sub_rows(R)

    def body(w_ref, g_ref, m_ref, v_ref, d_ref, nm_ref, nv_ref):
        g = g_ref[...]
        m = ADAM_B1 * m_ref[...] + (1.0 - ADAM_B1) * g
        v = ADAM_B2 * v_ref[...] + (1.0 - ADAM_B2) * (g * g)
        m_hat = m / (1.0 - ADAM_B1 ** ADAM_STEP)
        v_hat = v / (1.0 - ADAM_B2 ** ADAM_STEP)
        d_ref[...] = -ADAM_LR * (m_hat / (jnp.sqrt(v_hat) + ADAM_EPS) + ADAM_WD * w_ref[...])
        nm_ref[...] = m
        nv_ref[...] = v

    spec = pl.BlockSpec((tr, C), lambda i: (i, 0))
    return pl.pallas_call(body, name=name, grid=(R // tr,), in_specs=[spec] * 4, out_specs=[spec] * 3,
                          out_shape=[jax.ShapeDtypeStruct((R, C), F32)] * 3,
                          compiler_params=_params(("arbitrary",)))(w, g, m, v)


def _add_pair(units, got, core, *, name):
    n4, R, C = got.shape
    rows = n4 * R
    tr = 512 if rows % 512 == 0 else R

    def body(c_ref, u_ref, got_ref, o_ref):
        o_ref[...] = (u_ref[...].astype(F32) + got_ref[...].astype(F32)).astype(BF16)

    spec = pl.BlockSpec((tr, C), lambda i, c: (i, 0))
    grid_spec = pltpu.PrefetchScalarGridSpec(
        num_scalar_prefetch=1, grid=(rows // tr,),
        in_specs=[pl.BlockSpec((None, tr, C), lambda i, c: (c[0], i, 0)), spec], out_specs=spec)
    out = pl.pallas_call(body, name=name, grid_spec=grid_spec, out_shape=jax.ShapeDtypeStruct((rows, C), BF16),
                         compiler_params=_params(("arbitrary",)))(core.reshape(1), units.reshape(2, rows, C),
                                                                   got.reshape(rows, C))
    return out.reshape(n4, R, C)


def _sum_chips(w, own, got, pos, *, name, layer=0, into=None):
    _, R, C = own.shape
    tr = _sub_rows(R)
    nr = R // tr

    def body(p_ref, own_ref, got_ref, *rest):
        o_ref = rest[-1]
        o_ref[...] = (own_ref[...].astype(F32) + got_ref[0].astype(F32) + got_ref[1].astype(F32)
                      + got_ref[2].astype(F32))

    if w.row_sharded:
        out_map = lambda i, p: (layer, i, p[1])
    else:
        out_map = lambda i, p: (layer, p[1] * nr + i, 0)
    ins = [pos, own, got]
    in_specs = [pl.BlockSpec((None, tr, C), lambda i, p: (p[0], i, 0)),
                pl.BlockSpec((3, tr, C), lambda i, p: (0, i, 0))]
    alias = {}
    if into is not None:
        ins.append(into)
        in_specs.append(ANY)
        alias = {3: 0}
    grid_spec = pltpu.PrefetchScalarGridSpec(num_scalar_prefetch=1, grid=(nr,), in_specs=in_specs,
                                             out_specs=pl.BlockSpec((None, tr, C), out_map))
    return pl.pallas_call(body, name=name, grid_spec=grid_spec, input_output_aliases=alias,
                          out_shape=jax.ShapeDtypeStruct((w.L, w.ks, w.ns), F32),
                          compiler_params=_params(("arbitrary",)))(*ins)


def _mesh_pos():
    return lax.axis_index("x"), lax.axis_index("y"), lax.axis_index("c")


def _other_chips(x, y):
    return [(1 - x, y), (x, 1 - y), (1 - x, 1 - y)]


ANY = pl.BlockSpec(memory_space=pl.ANY)


class _W:
    def __init__(self, name, shard, row_sharded):
        self.name = name
        self.L, ks, ns = shard.shape
        self.row_sharded = row_sharded
        self.K, self.N = (ks * N_CHIPS, ns) if row_sharded else (ks, ns * N_CHIPS)
        self.ks, self.ns = ks, ns

    def shard_of(self, full, j):
        if self.row_sharded:
            return full.at[:, pl.ds(j * self.ks, self.ks), :]
        return full.at[:, :, pl.ds(j * self.ns, self.ns)]

    def half_of(self, shard, c):
        if self.row_sharded:
            return shard.at[:, :, pl.ds(c * (self.ns // 2), self.ns // 2)]
        return shard.at[:, pl.ds(c * (self.ks // 2), self.ks // 2), :]


HBM = pl.BlockSpec(memory_space=pltpu.HBM)
SEM = pl.BlockSpec(memory_space=pltpu.SEMAPHORE)
IN_FLIGHT = pltpu.SideEffectType.DATAFLOW_SIDE_EFFECTING


def _in_hbm(a):
    return pltpu.with_memory_space_constraint(a, pltpu.HBM)


def _gather_start(ws, shards, after):
    nw = len(ws)

    def body(*refs):
        src, dst = refs[:nw], refs[nw:2 * nw]
        send, recv = refs[2 * nw + 1:3 * nw + 1], refs[3 * nw + 1:4 * nw + 1]
        x, y, c = _mesh_pos()
        me = 2 * x + y
        for i, w in enumerate(ws):
            for f, (px, py) in enumerate(_other_chips(x, y)):
                pltpu.make_async_remote_copy(src_ref=w.half_of(src[i], c), dst_ref=w.half_of(w.shard_of(dst[i], me), c),
                                             send_sem=send[i].at[f], recv_sem=recv[i].at[f], device_id=(px, py, c),
                                             device_id_type=MESH).start()

    fulls = [lax.empty((w.L, w.K, w.N), BF16) for w in ws]
    out = pl.pallas_call(
        body, name="gather_start", in_specs=[HBM] * (2 * nw) + [ANY],
        out_specs=[SEM] * (2 * nw) + [HBM] * (2 * nw),
        out_shape=[pltpu.SemaphoreType.DMA((3,))] * (2 * nw)
        + [pltpu.HBM(s.shape, BF16) for s in shards] + [pltpu.HBM(f.shape, BF16) for f in fulls],
        input_output_aliases={i: 2 * nw + i for i in range(2 * nw)},
        compiler_params=pltpu.CompilerParams(has_side_effects=IN_FLIGHT))(
            *[_in_hbm(s) for s in shards], *[_in_hbm(f) for f in fulls], after)
    return [(out[i], out[nw + i], out[2 * nw + i], out[3 * nw + i]) for i in range(nw)]


def _gather_wait(ws, flight, after, *, name):
    nw = len(ws)

    def body(*refs):
        src, dst = refs[:nw], refs[nw:2 * nw]
        send, recv = refs[2 * nw:3 * nw], refs[3 * nw:4 * nw]
        x, y, c = _mesh_pos()
        for i, w in enumerate(ws):
            for f, (px, py) in enumerate(_other_chips(x, y)):
                landed = w.half_of(w.shard_of(dst[i], 2 * px + py), c)
                cp = pltpu.make_async_remote_copy(src_ref=w.half_of(src[i], c), dst_ref=landed, send_sem=send[i].at[f],
                                                  recv_sem=recv[i].at[f], device_id=(px, py, c), device_id_type=MESH)
                cp.wait_send()
                cp.wait_recv()

    shards, fulls = [fl[2] for fl in flight], [fl[3] for fl in flight]
    out = pl.pallas_call(
        body, name=name, in_specs=[HBM] * (2 * nw) + [SEM] * (2 * nw) + [ANY],
        out_specs=[HBM] * (2 * nw),
        out_shape=[pltpu.HBM(s.shape, BF16) for s in shards] + [pltpu.HBM(f.shape, BF16) for f in fulls],
        input_output_aliases={i: i for i in range(2 * nw)},
        compiler_params=pltpu.CompilerParams(has_side_effects=IN_FLIGHT))(
            *shards, *fulls, *[fl[0] for fl in flight], *[fl[1] for fl in flight], after)
    return out[:nw], out[nw:]


def _gather_finish(ws, shards, fulls, *, name):
    nw = len(ws)

    def body(*refs):
        src, dst, stage = refs[:nw], refs[3 * nw:4 * nw], refs[4 * nw:5 * nw]
        send_sems, recv_sems, load_sems, store_sems = refs[5 * nw:]
        x, y, c = _mesh_pos()
        me = 2 * x + y
        sibling = (x, y, 1 - c)
        chips = _other_chips(x, y)

        def fwd(i, w, f, half):
            px, py = chips[f]
            landed = w.half_of(w.shard_of(dst[i], 2 * px + py), half)
            return pltpu.make_async_remote_copy(src_ref=landed, dst_ref=landed, send_sem=send_sems.at[3 * i + f],
                                                recv_sem=recv_sems.at[3 * i + f], device_id=sibling,
                                                device_id_type=MESH)

        loads = [pltpu.make_async_copy(src[i], stage[i], load_sems.at[i]) for i in range(nw)]
        for cp in loads:
            cp.start()
        sends = [fwd(i, w, f, c) for i, w in enumerate(ws) for f in range(3)]
        for cp in sends:
            cp.start()
        stores = [pltpu.make_async_copy(stage[i], w.shard_of(dst[i], me), store_sems.at[i])
                  for i, w in enumerate(ws)]
        for ld, st in zip(loads, stores):
            ld.wait()
            st.start()
        for i, w in enumerate(ws):
            for f in range(3):
                fwd(i, w, f, 1 - c).wait_recv()
        for cp in sends:
            cp.wait_send()
        for cp in stores:
            cp.wait()

    out = pl.pallas_call(
        body, name=name, in_specs=[ANY] * (2 * nw), out_specs=[ANY] * (2 * nw),
        out_shape=[jax.ShapeDtypeStruct(s.shape, BF16) for s in shards]
        + [jax.ShapeDtypeStruct(f.shape, BF16) for f in fulls],
        input_output_aliases={i: i for i in range(2 * nw)},
        scratch_shapes=[pltpu.VMEM((w.L, w.ks, w.ns), BF16) for w in ws]
        + [pltpu.SemaphoreType.DMA((3 * nw,)), pltpu.SemaphoreType.DMA((3 * nw,)), pltpu.SemaphoreType.DMA((nw,)),
           pltpu.SemaphoreType.DMA((nw,))],
        compiler_params=_params(has_side_effects=True))(*shards, *fulls)
    return out[nw:]


def _split_copies(name, srcs, lands, n_sems, copies_of, *, flight=None, after=None):
    n = len(srcs)
    starting = flight is None

    def body(*refs):
        src, land = refs[:n], refs[n:2 * n]
        sems = refs[2 * n + 1:4 * n + 1] if starting else refs[2 * n:4 * n]
        for i in range(n):
            for cp in copies_of(i, src[i], land[i], sems[i], sems[n + i]):
                if starting:
                    cp.start()
                else:
                    cp.wait_send()
                    cp.wait_recv()

    thru = [pltpu.HBM(a.shape, a.dtype) for a in list(srcs) + list(lands)]
    if starting:
        out = pl.pallas_call(
            body, name=name, in_specs=[HBM] * (2 * n) + [ANY], out_specs=[SEM] * (2 * n) + [HBM] * (2 * n),
            out_shape=[pltpu.SemaphoreType.DMA((n_sems,))] * (2 * n) + thru,
            input_output_aliases={i: 2 * n + i for i in range(2 * n)},
            compiler_params=pltpu.CompilerParams(has_side_effects=IN_FLIGHT))(
                *[_in_hbm(a) for a in srcs], *[_in_hbm(a) for a in lands], after)
        return [(out[i], out[n + i], out[2 * n + i], out[3 * n + i]) for i in range(n)]
    out = pl.pallas_call(
        body, name=name, in_specs=[HBM] * (2 * n) + [SEM] * (2 * n) + [ANY], out_specs=[HBM] * (2 * n),
        out_shape=thru, input_output_aliases={i: i for i in range(2 * n)},
        compiler_params=pltpu.CompilerParams(has_side_effects=IN_FLIGHT))(
            *srcs, *lands, *[fl[0] for fl in flight], *[fl[1] for fl in flight], after)
    return out[:n], out[n:]


def _sum8(land, vec, me):
    R = vec.shape[0]

    def body(me_ref, land_ref, vec_ref, o_ref):
        acc = jnp.zeros((R, 128), F32)
        for d in range(8):
            acc = acc + jnp.where(me_ref[0] == d, vec_ref[...], land_ref[d])
        o_ref[...] = acc

    grid_spec = pltpu.PrefetchScalarGridSpec(
        num_scalar_prefetch=1, grid=(1,),
        in_specs=[pl.BlockSpec((8, R, 128), lambda i, m: (0, 0, 0)), pl.BlockSpec((R, 128), lambda i, m: (0, 0))],
        out_specs=pl.BlockSpec((R, 128), lambda i, m: (0, 0)))
    return pl.pallas_call(body, name="sum8", grid_spec=grid_spec, out_shape=jax.ShapeDtypeStruct((R, 128), F32),
                          compiler_params=_params(("arbitrary",)))(me.reshape(1), land, vec)


def _swap_copies(i, src, got, send, recv):
    x, y, c = _mesh_pos()
    return [pltpu.make_async_remote_copy(src_ref=src.at[1 - c], dst_ref=got, send_sem=send.at[0], recv_sem=recv.at[0],
                                         device_id=(x, y, 1 - c), device_id_type=MESH)]


def _gather8_copies(i, src, land, send, recv):
    x, y, c = _mesh_pos()
    me = 4 * x + 2 * y + c
    peers = [(x, y, 1 - c)] + [(px, py, pc) for px, py in _other_chips(x, y) for pc in (c, 1 - c)]
    return [pltpu.make_async_remote_copy(src_ref=src, dst_ref=land.at[me], send_sem=send.at[k], recv_sem=recv.at[k],
                                         device_id=peer, device_id_type=MESH) for k, peer in enumerate(peers)]


def _scatter_copy(src, got, send, recv, f, chip, c):
    px, py = chip
    return pltpu.make_async_remote_copy(src_ref=src.at[2 * px + py], dst_ref=got.at[f], send_sem=send.at[f],
                                        recv_sem=recv.at[f], device_id=(px, py, c), device_id_type=MESH)


def _scatter_start(sums, *, name):
    nw = len(sums)

    def body(*refs):
        src, got = refs[:nw], refs[nw:2 * nw]
        send, recv = refs[2 * nw:3 * nw], refs[3 * nw:4 * nw]
        x, y, c = _mesh_pos()
        for i in range(nw):
            for f, chip in enumerate(_other_chips(x, y)):
                _scatter_copy(src[i], got[i], send[i], recv[i], f, chip, c).start()

    lands = [lax.empty((3,) + s.shape[1:], BF16) for s in sums]
    out = pl.pallas_call(
        body, name=name, in_specs=[HBM] * (2 * nw), out_specs=[SEM] * (2 * nw) + [HBM] * (2 * nw),
        out_shape=[pltpu.SemaphoreType.DMA((3,))] * (2 * nw)
        + [pltpu.HBM(s.shape, BF16) for s in sums] + [pltpu.HBM(l.shape, BF16) for l in lands],
        input_output_aliases={i: 2 * nw + i for i in range(2 * nw)},
        compiler_params=pltpu.CompilerParams(has_side_effects=IN_FLIGHT))(
            *[_in_hbm(s) for s in sums], *[_in_hbm(l) for l in lands])
    return [(out[i], out[nw + i], out[2 * nw + i], out[3 * nw + i]) for i in range(nw)]


def _scatter_wait(flight, after):
    nw = len(flight)

    def body(*refs):
        src, got = refs[:nw], refs[nw:2 * nw]
        send, recv = refs[2 * nw:3 * nw], refs[3 * nw:4 * nw]
        x, y, c = _mesh_pos()
        for i in range(nw):
            for f, chip in enumerate(_other_chips(x, y)):
                cp = _scatter_copy(src[i], got[i], send[i], recv[i], f, chip, c)
                cp.wait_send()
                cp.wait_recv()

    sums, lands = [fl[2] for fl in flight], [fl[3] for fl in flight]
    out = pl.pallas_call(
        body, name="scatter_wait", in_specs=[HBM] * (2 * nw) + [SEM] * (2 * nw) + [ANY], out_specs=[HBM] * (2 * nw),
        out_shape=[pltpu.HBM(s.shape, BF16) for s in sums] + [pltpu.HBM(l.shape, BF16) for l in lands],
        input_output_aliases={i: i for i in range(2 * nw)},
        compiler_params=pltpu.CompilerParams(has_side_effects=IN_FLIGHT))(
            *sums, *lands, *[fl[0] for fl in flight], *[fl[1] for fl in flight], after)
    return out[:nw], out[nw:]


def _join_halves(ws, shards):
    nw = len(ws)

    def body(*refs):
        buf = refs[nw:2 * nw]
        send_sems, recv_sems = refs[2 * nw:]
        x, y, c = _mesh_pos()
        sibling = (x, y, 1 - c)

        def copy(i, w, half):
            region = w.half_of(buf[i], half)
            return pltpu.make_async_remote_copy(src_ref=region, dst_ref=region, send_sem=send_sems.at[i],
                                                recv_sem=recv_sems.at[i], device_id=sibling, device_id_type=MESH)

        sends = [copy(i, w, c) for i, w in enumerate(ws)]
        for cp in sends:
            cp.start()
        for i, w in enumerate(ws):
            copy(i, w, 1 - c).wait_recv()
        for cp in sends:
            cp.wait_send()

    return pl.pallas_call(
        body, name="join_halves", in_specs=[ANY] * nw, out_specs=[ANY] * nw,
        out_shape=[jax.ShapeDtypeStruct((w.L, w.ks, w.ns), F32) for w in ws],
        input_output_aliases={i: i for i in range(nw)},
        scratch_shapes=[pltpu.SemaphoreType.DMA((nw,)), pltpu.SemaphoreType.DMA((nw,))],
        compiler_params=_params(has_side_effects=True))(*shards)


def _allreduce_small(vec):
    R = vec.shape[0]

    def body(x_ref, o_ref, buf, send_sems, recv_sems):
        x, y, c = _mesh_pos()
        me, sibling = (x, y, c), (x, y, 1 - c)
        chips = _other_chips(x, y)

        def slot(px, py, pc):
            return buf.at[4 * px + 2 * py + pc]

        def copy(k, block, to, src=None):
            return pltpu.make_async_remote_copy(src_ref=slot(*block) if src is None else src, dst_ref=slot(*block),
                                                send_sem=send_sems.at[k], recv_sem=recv_sems.at[k], device_id=to,
                                                device_id_type=MESH)

        first = [copy(0, me, sibling, src=x_ref)] + [copy(1 + f, me, (*chip, c), src=x_ref)
                                                     for f, chip in enumerate(chips)]
        for cp in first:
            cp.start()
        passed = [copy(4 + f, (*chip, c), sibling) for f, chip in enumerate(chips)]
        for f, chip in enumerate(chips):
            copy(1 + f, (*chip, c), me).wait_recv()
            passed[f].start()
        copy(0, sibling, me).wait_recv()
        for f, chip in enumerate(chips):
            copy(4 + f, (*chip, 1 - c), me).wait_recv()
        for cp in first + passed:
            cp.wait_send()
        slot(*me)[...] = x_ref[...]
        acc = buf[0]
        for d in range(1, 8):
            acc = acc + buf[d]
        o_ref[...] = acc

    return pl.pallas_call(
        body, name="allreduce_small", in_specs=[pl.BlockSpec(memory_space=pltpu.VMEM)],
        out_specs=pl.BlockSpec(memory_space=pltpu.VMEM), out_shape=jax.ShapeDtypeStruct((R, 128), F32),
        scratch_shapes=[pltpu.VMEM((8, R, 128), F32), pltpu.SemaphoreType.DMA((7,)), pltpu.SemaphoreType.DMA((7,))],
        compiler_params=_params())(vec)


def _pack(parts):
    flat = jnp.concatenate([p.reshape(-1).astype(F32) for p in parts])
    n = flat.shape[0]
    pad = (-n) % (64 * 128)
    return jnp.pad(flat, (0, pad)).reshape(-1, 128)


def _unpack(vec, shapes):
    flat = vec.reshape(-1)
    out, off = [], 0
    for s in shapes:
        n = int(np.prod(s))
        out.append(flat[off:off + n].reshape(s))
        off += n
    return out


def kernel(x, a_norm_g, a_w_in, a_v_norm_g, a_w_s, a_b_s, a_w_out, kv_norm_g, w_kv, b_norm_g, b_w_q, b_rel_bias, b_w_o, f_norm_g, f_w_in, f_conv_w, f_conv_b, f_w_down, final_norm_g, loss_target, m_a_norm_g, m_a_w_in, m_a_v_norm_g, m_a_w_s, m_a_b_s, m_a_w_out, m_kv_norm_g, m_w_kv, m_b_norm_g, m_b_w_q, m_b_rel_bias, m_b_w_o, m_f_norm_g, m_f_w_in, m_f_conv_w, m_f_conv_b, m_f_w_down, m_final_norm_g, v_a_norm_g, v_a_w_in, v_a_v_norm_g, v_a_w_s, v_a_b_s, v_a_w_out, v_kv_norm_g, v_w_kv, v_b_norm_g, v_b_w_q, v_b_rel_bias, v_b_w_o, v_f_norm_g, v_f_w_in, v_f_conv_w, v_f_conv_b, v_f_w_down, v_final_norm_g):
    B, S, D = x.shape
    T = B * S
    xi, yi, ci = lax.axis_index("x"), lax.axis_index("y"), lax.axis_index("c")
    j_me = (2 * xi + yi).astype(jnp.int32)
    core = ci.astype(jnp.int32)
    pos = jnp.stack([j_me, core])

    w_shards = {"a_w_in": (a_w_in, False), "a_w_out": (a_w_out, True), "w_kv": (w_kv[None], False),
                "b_w_q": (b_w_q, True), "b_w_o": (b_w_o, True), "f_w_in": (f_w_in, False), "f_w_down": (f_w_down, True)}
    names = list(w_shards)
    ws = [_W(n, w_shards[n][0], w_shards[n][1]) for n in names]
    g_shards = {"a_w_in": (a_w_in, False), "a_w_out": (a_w_out, True),
                "f_w_in0": (f_w_in[0:1], False), "f_w_down0": (f_w_down[0:1], True),
                "w_kv": (w_kv[None], False), "b_w_q": (b_w_q, True), "b_w_o": (b_w_o, True),
                "f_w_in1": (f_w_in[1:2], False), "f_w_down1": (f_w_down[1:2], True)}
    g_names = list(g_shards)
    g_ws = {n: _W(n, *g_shards[n]) for n in g_names}

    Wd = a_w_in.shape[1]
    GW = a_v_norm_g.shape[1] * N_CHIPS
    F2 = f_conv_w.shape[2] * N_CHIPS
    Fh = F2 // 2
    nsd, nsg, nsf = a_norm_g.shape[1], a_v_norm_g.shape[1], f_conv_w.shape[2]
    own = (ci == 0).astype(F32)
    place = lambda sh, width, n: lax.dynamic_update_slice_in_dim(
        jnp.zeros(sh.shape[:-1] + (width,), F32), sh * own, j_me * n, axis=sh.ndim - 1)
    gathered = _allreduce_small(_pack([place(a_norm_g, Wd, nsd), place(a_v_norm_g, GW, nsg),
                                       place(f_conv_w, F2, nsf)]))
    a_g, a_vg, conv_w = _unpack(gathered, [(1, Wd), (1, GW), (2, 3, F2)])

    flight = dict(zip(g_names, _gather_start([g_ws[n] for n in g_names],
                                             [g_shards[n][0].astype(BF16) for n in g_names], gathered)))
    full = {}

    def arrive(group, after, tag):
        gw = [g_ws[n] for n in group]
        sh, fu = _gather_wait(gw, [flight[n] for n in group], after, name=f"gather_wait_{tag}")
        full.update(zip(group, _gather_finish(gw, sh, fu, name=f"gather_finish_{tag}")))
    conv_w2 = conv_w.reshape(2, 3, 2, Fh).transpose(0, 2, 1, 3)
    conv_b2 = f_conv_b.reshape(2, 2, Fh)

    h0 = x.reshape(T, D)
    target = loss_target.reshape(T, D)
    bs_tile = jnp.repeat(a_b_s[0].T, GROUP_DIM, axis=1)
    ws_a = a_w_s[0]
    scale = HEAD_DIM ** -0.5
    HD = b_w_q.shape[2]
    H = HD // HEAD_DIM
    n_rel = b_rel_bias.shape[-1]
    frow = b_rel_bias[0][:, _bias_index()].reshape(H, 1, F_LEN)
    bias = _bias_expand(frow)

    def ffn_fwd(h, l):
        yff, a, c, n = _ffn_in_conv(h, full[f"f_w_in{l}"], f_norm_g[l], conv_w2[l], conv_b2[l], S, name=f"ffn{l}_in")
        return _mm(yff, full[f"f_w_down{l}"], layer=0, res=h, name=f"ffn{l}_down"), (a, c, n, yff)

    arrive(["a_w_in", "a_w_out"], h0, "a")
    zp, n_a = _mm(h0, full["a_w_in"], layer=0, norm_g=a_g[0], emit_norm=True, name="a_in")
    out_a = _gate_fwd(zp, a_vg, ws_a, bs_tile)
    h1 = _mm(out_a, full["a_w_out"], layer=0, res=h0, name="a_out")
    arrive(["f_w_in0", "f_w_down0"], h1, "f0")
    h2, saved0 = ffn_fwd(h1, 0)
    arrive(["w_kv", "b_w_q", "b_w_o"], h2, "b")
    arrive(["f_w_in1", "f_w_down1"], h2, "f1")
    kv, n_kv = _mm(h2, full["w_kv"], layer=0, norm_g=kv_norm_g, out_dtype=BF16, split_out=True, emit_norm=True,
                   name="kv")
    q, n_q = _mm(h2, full["b_w_q"], layer=0, norm_g=b_norm_g[0], scale=scale, out_dtype=BF16, emit_norm=True,
                 name="q")
    kv4, q3 = kv.reshape(2, B, S, HD), q.reshape(B, S, HD)
    o = _attn_fwd(q3, kv4, bias, B, S).reshape(T, HD)
    h3 = _mm(o, full["b_w_o"], layer=0, res=h2, name="attn_out")
    h4, saved1 = ffn_fwd(h3, 1)
    dh, loss8, dg_final = _loss_head(h4, final_norm_g, target)

    units = {}

    in_flight = {}

    def tied(x, flight):
        x, thru = lax.optimization_barrier((x, flight[0][2]))
        return x, [flight[0][:2] + (thru,) + flight[0][3:]] + flight[1:]

    def swap_start(group, tag, carry):
        us = [units[n] for n in group]
        lands = [lax.empty(u.shape[1:], BF16) for u in us]
        carry, flight = tied(carry, _split_copies(f"swap_start_{tag}", us, lands, 1, _swap_copies, after=us[0]))
        return (group, tag, flight), carry

    def reduce_start(swap, after):
        group, tag, flight = swap
        us, got = _split_copies(f"swap_wait_{tag}", [fl[2] for fl in flight], [fl[3] for fl in flight], 1,
                                _swap_copies, flight=flight, after=after)
        sums = [_add_pair(u, g_, core, name=f"pair_{n}") for n, u, g_ in zip(group, us, got)]
        after, flight = tied(after, _scatter_start(sums, name=f"scatter_start_{tag}"))
        in_flight.update(zip(group, flight))
        return after

    def ffn_bwd(dh, h, saved, l, early):
        a, c, n, yff = saved
        units[f"f_w_down{l}"] = _mm_tn(yff, dh, rows_are_shards=True, name=f"ffn{l}_down_dw")
        dh_in = dh
        if early:
            sw, dh_in = swap_start([f"f_w_down{l}"], f"fd{l}", dh)
        dyff = _mm(dh_in, full[f"f_w_down{l}"], layer=0, trans_w=True, out_dtype=BF16, name=f"ffn{l}_down_dx")
        if early:
            dyff = reduce_start(sw, dyff)
        da, dcw, dcb = _conv_bwd(a, c, dyff, conv_w2[l], S)
        units[f"f_w_in{l}"] = _mm_tn(n, da, split_y=True, name=f"ffn{l}_in_dw")
        sw, da = swap_start([f"f_w_in{l}"] if early else [f"f_w_down{l}", f"f_w_in{l}"], f"f{l}", da)
        dh, dg = _mm(da, full[f"f_w_in{l}"], layer=0, trans_w=True, split_x=True, bwd=(h, f_norm_g[l], dh), tm=256,
                     name=f"ffn{l}_in_dx")
        return reduce_start(sw, dh), dg, dcw, dcb

    dh, dg_f1, dcw1, dcb1 = ffn_bwd(dh, h3, saved1, 1, False)
    do = _mm(dh, full["b_w_o"], layer=0, trans_w=True, out_dtype=BF16, name="attn_out_dx")
    units["b_w_o"] = _mm_tn(o, dh, rows_are_shards=True, name="b_w_o_dw")
    dq, dkv, dbias = _attn_bwd(q3, kv4, bias, do.reshape(B, S, HD), B, S)
    d_rel = _bias_reduce(dbias, n_rel).reshape(1, H, n_rel)
    dq, dkv = dq.reshape(T, HD), dkv.reshape(2, T, HD)
    units["b_w_q"] = _mm_tn(n_q, dq, rows_are_shards=True, name="b_w_q_dw")
    dh, dg_b = _mm(dq, full["b_w_q"], layer=0, trans_w=True, bwd=(h2, b_norm_g[0], dh), name="q_dx")
    units["w_kv"] = _mm_tn(n_kv, dkv, split_y=True, name="w_kv_dw")
    sw, dkv = swap_start(["b_w_o", "b_w_q", "w_kv"], "b", dkv)
    dh, dg_kv = _mm(dkv, full["w_kv"], layer=0, trans_w=True, split_x=True, bwd=(h2, kv_norm_g, dh), name="kv_dx")
    dh = reduce_start(sw, dh)
    dh, dg_f0, dcw0, dcb0 = ffn_bwd(dh, h1, saved0, 0, True)
    units["a_w_out"] = _mm_tn(out_a, dh, rows_are_shards=True, name="a_w_out_dw")
    sw, dh_in = swap_start(["a_w_out"], "ao", dh)
    d_out = _mm(dh_in, full["a_w_out"], layer=0, trans_w=True, out_dtype=BF16, name="a_out_dx")
    d_out = reduce_start(sw, d_out)
    dzp, dws, dbs, dgv = _gate_bwd(zp, d_out, a_vg, ws_a, bs_tile)
    units["a_w_in"] = _mm_tn(n_a, dzp, name="a_w_in_dw")
    sw, dzp_in = swap_start(["a_w_in"], "ai", dzp)
    grad_x, dg_a = _mm(dzp_in, full["a_w_in"], layer=0, trans_w=True, bwd=(h0, a_g[0], dh), name="a_in_dx")
    grad_x = reduce_start(sw, grad_x)

    to_flat = lambda d: d.transpose(1, 0, 2).reshape(3, F2)
    small = {"a_norm_g": dg_a, "a_v_norm_g": dgv, "a_w_s": dws[None], "a_b_s": dbs[None], "kv_norm_g": dg_kv[0],
             "b_norm_g": dg_b, "b_rel_bias": d_rel, "f_norm_g": jnp.concatenate([dg_f0, dg_f1], axis=0),
             "f_conv_w": jnp.stack([to_flat(dcw0), to_flat(dcw1)]),
             "f_conv_b": jnp.stack([dcb0.reshape(F2), dcb1.reshape(F2)]), "final_norm_g": dg_final[0]}
    snames = list(small)
    small_vec = _pack([small[n] for n in snames] + [loss8[0:1, 0:1]])
    grad_x, small_flight = tied(grad_x, _split_copies("small_start", [small_vec],
                                                      [lax.empty((8,) + small_vec.shape, F32)], 7, _gather8_copies,
                                                      after=small_vec))

    sums, recv = _scatter_wait([in_flight[n] for n in g_names], grad_x)
    sums, recv = dict(zip(g_names, sums)), dict(zip(g_names, recv))
    halves = []
    for n, w in zip(names, ws):
        if w.L == 1:
            halves.append(_sum_chips(w, sums[n], recv[n], pos, name=f"chips_{n}"))
        else:
            first = _sum_chips(w, sums[n + "0"], recv[n + "0"], pos, name=f"chips_{n}0")
            halves.append(_sum_chips(w, sums[n + "1"], recv[n + "1"], pos, layer=1, into=first, name=f"chips_{n}1"))
    g_big = dict(zip(names, _join_halves(ws, halves)))
    g_big["w_kv"] = g_big["w_kv"][0]

    given = dict(a_norm_g=(a_norm_g, m_a_norm_g, v_a_norm_g), a_w_in=(a_w_in, m_a_w_in, v_a_w_in),
                 a_v_norm_g=(a_v_norm_g, m_a_v_norm_g, v_a_v_norm_g), a_w_s=(a_w_s, m_a_w_s, v_a_w_s),
                 a_b_s=(a_b_s, m_a_b_s, v_a_b_s), a_w_out=(a_w_out, m_a_w_out, v_a_w_out),
                 kv_norm_g=(kv_norm_g, m_kv_norm_g, v_kv_norm_g), w_kv=(w_kv, m_w_kv, v_w_kv),
                 b_norm_g=(b_norm_g, m_b_norm_g, v_b_norm_g), b_w_q=(b_w_q, m_b_w_q, v_b_w_q),
                 b_rel_bias=(b_rel_bias, m_b_rel_bias, v_b_rel_bias), b_w_o=(b_w_o, m_b_w_o, v_b_w_o),
                 f_norm_g=(f_norm_g, m_f_norm_g, v_f_norm_g), f_w_in=(f_w_in, m_f_w_in, v_f_w_in),
                 f_conv_w=(f_conv_w, m_f_conv_w, v_f_conv_w), f_conv_b=(f_conv_b, m_f_conv_b, v_f_conv_b),
                 f_w_down=(f_w_down, m_f_w_down, v_f_w_down), final_norm_g=(final_norm_g, m_final_norm_g, v_final_norm_g))
    order = list(given)
    grads, deltas, new_m, new_v = {}, {}, {}, {}
    for n in names:
        w_, m_, v_ = given[n]
        g_ = g_big[n]
        C = w_.shape[-1]
        d2, m2, v2 = _adamw(w_.reshape(-1, C), g_.reshape(-1, C), m_.reshape(-1, C), v_.reshape(-1, C),
                            name=f"adamw_{n}")
        grads[n], deltas[n], new_m[n], new_v[n] = g_.reshape(w_.shape), d2.reshape(w_.shape), m2.reshape(w_.shape), \
            v2.reshape(w_.shape)
    vecs, lands = _split_copies("small_wait", [small_flight[0][2]], [small_flight[0][3]], 7, _gather8_copies,
                                flight=small_flight, after=deltas[names[-1]])
    red = _sum8(lands[0], vecs[0], (4 * xi + 2 * yi + ci).astype(jnp.int32))
    parts = _unpack(red, [small[n].shape for n in snames] + [(1,)])
    g_small = dict(zip(snames, parts[:-1]))
    loss = parts[-1][0]
    g_small["a_norm_g"] = lax.dynamic_slice_in_dim(g_small["a_norm_g"], j_me * nsd, nsd, axis=1)
    g_small["a_v_norm_g"] = lax.dynamic_slice_in_dim(g_small["a_v_norm_g"], j_me * nsg, nsg, axis=1)
    g_small["f_conv_w"] = lax.dynamic_slice_in_dim(g_small["f_conv_w"], j_me * nsf, nsf, axis=2)

    sm = [n for n in order if n not in names]
    d2, m2, v2 = _adamw(_pack([given[n][0] for n in sm]), _pack([g_small[n].reshape(given[n][0].shape) for n in sm]),
                        _pack([given[n][1] for n in sm]), _pack([given[n][2] for n in sm]), name="adamw_small")
    shapes = [given[n][0].shape for n in sm]
    for n, d_, m_, v_ in zip(sm, _unpack(d2, shapes), _unpack(m2, shapes), _unpack(v2, shapes)):
        grads[n], deltas[n], new_m[n], new_v[n] = g_small[n].reshape(given[n][0].shape), d_, m_, v_

    return (loss, grad_x.reshape(B, S, D), *[grads[n] for n in order], *[deltas[n] for n in order],
            *[new_m[n] for n in order], *[new_v[n] for n in order])
```

```python
import math

import numpy as np
import jax
import jax.numpy as jnp
from jax import lax
from jax.experimental import pallas as pl
from jax.experimental.pallas import tpu as pltpu

F32 = jnp.float32
BF16 = jnp.bfloat16
MESH = pl.DeviceIdType.MESH

EPS = 1e-6
NEG_INF = -1e30
CHUNK = 64
GMLP_BLOCK = 128
GROUP_DIM = 128
HEAD_DIM = 64
LEFT_CHUNKS = 8
PAD = LEFT_CHUNKS * CHUNK
REL_CLIP = 128
Q_BLOCK = 256
K_SPAN = PAD + Q_BLOCK
F_LEN = K_SPAN + Q_BLOCK
HEADS_PER_STEP = 4
N_CHIPS = 4

ADAM_LR = 0.001
ADAM_B1 = 0.9
ADAM_B2 = 0.999
ADAM_EPS = 1e-08
ADAM_WD = 0.01
ADAM_STEP = 10

VMEM_LIMIT = 56 * 1024 * 1024


def _params(sem=None, **kw):
    if sem is not None:
        kw["dimension_semantics"] = sem
    return pltpu.CompilerParams(vmem_limit_bytes=VMEM_LIMIT, **kw)


def _rms(xf):
    r = lax.rsqrt(jnp.mean(xf * xf, axis=-1, keepdims=True) + EPS)
    return xf * r, r


def _gelu(x):
    c = math.sqrt(2.0 / math.pi)
    return 0.5 * x * (1.0 + jnp.tanh(c * (x + 0.044715 * x * x * x)))


def _gelu_grad(x):
    c = math.sqrt(2.0 / math.pi)
    t = jnp.tanh(c * (x + 0.044715 * x * x * x))
    return 0.5 * (1.0 + t) + 0.5 * x * (1.0 - t * t) * c * (1.0 + 3.0 * 0.044715 * x * x)


def _col_tile(n):
    if n <= 1024:
        return n
    for t in (1408, 1024, 512):
        if n % t == 0:
            return t
    raise ValueError(n)


def _row_tile(t, want):
    while t % want:
        want //= 2
    return want


def _mm(x, w, *, name, layer=None, trans_w=False, norm_g=None, res=None, scale=None, out_dtype=F32, bwd=None,
        split_out=False, split_x=False, emit_norm=False, tm=512):
    T = x.shape[-2]
    K = 2 * x.shape[-1] if split_x else x.shape[-1]
    N = w.shape[-2] if trans_w else w.shape[-1]
    tn = N
    tm = _row_tile(T, 256 if N > 4096 else tm)
    nn, nm = N // tn, T // tm
    has_norm, has_res, has_bwd = norm_g is not None, res is not None, bwd is not None
    dims = (((1,), (1,)), ((), ())) if trans_w else (((1,), (0,)), ((), ()))

    def body(*refs):
        it = iter(refs)
        x_ref, w_ref = next(it), next(it)
        g_ref = next(it) if has_norm else None
        res_ref = next(it) if has_res else None
        if has_bwd:
            h_ref, bg_ref, dh_ref = next(it), next(it), next(it)
        o_ref = next(it)
        if split_x:
            kh = K // 2
            acc = lax.dot_general(x_ref[0].astype(BF16), w_ref[:, :kh] if trans_w else w_ref[:kh, :], dims,
                                  preferred_element_type=F32)
            acc = acc + lax.dot_general(x_ref[1].astype(BF16), w_ref[:, kh:] if trans_w else w_ref[kh:, :], dims,
                                        preferred_element_type=F32)
        else:
            xv = x_ref[...]
            if has_norm:
                xv = _rms(xv.astype(F32))[0] * g_ref[...]
            xb = xv.astype(BF16)
            if emit_norm:
                refs[-1][...] = xb
            acc = lax.dot_general(xb, w_ref[...], dims, preferred_element_type=F32)
        if scale is not None:
            acc = acc * scale
        if has_res:
            acc = acc + res_ref[...]
        if has_bwd:
            dg_ref = next(it)
            n, r = _rms(h_ref[...])

            @pl.when(pl.program_id(1) == 0)
            def _():
                dg_ref[...] = jnp.zeros_like(dg_ref)

            dg_ref[...] += jnp.sum(acc * n, axis=0, keepdims=True)
            t = acc * bg_ref[...]
            o_ref[...] = dh_ref[...] + r * (t - n * jnp.mean(t * n, axis=-1, keepdims=True))
        elif split_out:
            o_ref[0] = acc[:, :N // 2].astype(out_dtype)
            o_ref[1] = acc[:, N // 2:].astype(out_dtype)
        else:
            o_ref[...] = acc.astype(out_dtype)

    lead = () if layer is None else (None,)
    lidx = () if layer is None else (layer,)
    ins = [x, w]
    xspec = (pl.BlockSpec((2, tm, K // 2), lambda n, m: (0, m, 0)) if split_x
             else pl.BlockSpec((tm, K), lambda n, m: (m, 0)))
    wspec = (pl.BlockSpec(lead + (tn, K), lambda n, m: lidx + (n, 0)) if trans_w
             else pl.BlockSpec(lead + (K, tn), lambda n, m: lidx + (0, n)))
    in_specs = [xspec, wspec]
    if has_norm:
        ins.append(norm_g.reshape(1, K))
        in_specs.append(pl.BlockSpec((1, K), lambda n, m: (0, 0)))
    if has_res:
        ins.append(res)
        in_specs.append(pl.BlockSpec((tm, tn), lambda n, m: (m, n)))
    if split_out:
        out_shape = [jax.ShapeDtypeStruct((2, T, N // 2), out_dtype)]
        out_specs = [pl.BlockSpec((2, tm, N // 2), lambda n, m: (0, m, 0))]
    else:
        out_shape = [jax.ShapeDtypeStruct((T, N), F32 if has_bwd else out_dtype)]
        out_specs = [pl.BlockSpec((tm, tn), lambda n, m: (m, n))]
    if has_bwd:
        h, g, dh = bwd
        ins += [h, g.reshape(1, N), dh]
        in_specs += [pl.BlockSpec((tm, N), lambda n, m: (m, 0)), pl.BlockSpec((1, N), lambda n, m: (0, 0)),
                     pl.BlockSpec((tm, N), lambda n, m: (m, 0))]
        out_shape.append(jax.ShapeDtypeStruct((1, N), F32))
        out_specs.append(pl.BlockSpec((1, N), lambda n, m: (0, 0)))
    if emit_norm:
        out_shape.append(jax.ShapeDtypeStruct((T, K), BF16))
        out_specs.append(pl.BlockSpec((tm, K), lambda n, m: (m, 0)))
    out = pl.pallas_call(body, name=name, grid=(nn, nm), in_specs=in_specs, out_specs=out_specs, out_shape=out_shape,
                         compiler_params=_params(("arbitrary", "arbitrary")))(*ins)
    return out if has_bwd or emit_norm else out[0]


def _mm_tn(x, dy, *, name, rows_are_shards=False, split_y=False, tt=512):
    T, K = x.shape
    N = 2 * dy.shape[-1] if split_y else dy.shape[-1]
    R, C = (K // N_CHIPS, N // 2) if rows_are_shards else (K // 2, N // N_CHIPS)
    nn = 2 if split_y else 1
    tn = N // nn
    per = N_CHIPS // nn
    assert not (rows_are_shards and split_y)
    tt = _row_tile(T, tt)
    nt = T // tt

    def body(x_ref, y_ref, o_ref, acc_ref):
        t = pl.program_id(1)

        @pl.when(t == 0)
        def _():
            acc_ref[...] = jnp.zeros_like(acc_ref)

        acc_ref[...] += lax.dot_general(x_ref[...], y_ref[...].astype(BF16), (((0,), (0,)), ((), ())),
                                        preferred_element_type=F32)

        @pl.when(t == nt - 1)
        def _():
            if rows_are_shards:
                for h in range(2):
                    o_ref[h] = acc_ref[:, h * C:(h + 1) * C].astype(BF16).reshape(N_CHIPS, R, C)
            else:
                for j in range(per):
                    o_ref[:, j] = acc_ref[:, j * C:(j + 1) * C].astype(BF16).reshape(2, R, C)

    if split_y:
        yspec = pl.BlockSpec((None, tt, tn), lambda n, t: (n, t, 0))
    else:
        yspec = pl.BlockSpec((tt, tn), lambda n, t: (t, 0))
    if rows_are_shards:
        out_spec = pl.BlockSpec((2, N_CHIPS, R, C), lambda n, t: (0, 0, 0, 0))
    else:
        out_spec = pl.BlockSpec((2, per, R, C), lambda n, t: (0, n, 0, 0))
    return pl.pallas_call(body, name=name, grid=(nn, nt),
                          in_specs=[pl.BlockSpec((tt, K), lambda n, t: (t, 0)), yspec], out_specs=out_spec,
                          out_shape=jax.ShapeDtypeStruct((2, N_CHIPS, R, C), BF16),
                          scratch_shapes=[pltpu.VMEM((K, tn), F32)],
                          compiler_params=_params(("arbitrary", "arbitrary")))(x, dy)


def _chunk_mask():
    i = lax.broadcasted_iota(jnp.int32, (GMLP_BLOCK, GMLP_BLOCK), 0) // CHUNK
    j = lax.broadcasted_iota(jnp.int32, (GMLP_BLOCK, GMLP_BLOCK), 1) // CHUNK
    return i >= j


def _gate_fwd(zp, gv, ws, bs_tile, *, tm=256):
    T, W2 = zp.shape
    W = W2 // 2
    G = W // GROUP_DIM
    tm = _row_tile(T, tm)

    def body(zp_ref, gv_ref, ws_ref, bs_ref, o_ref):
        z = _gelu(zp_ref[...])
        u, v = z[:, :W], z[:, W:]
        vn = _rms(v)[0] * gv_ref[...]
        mask = _chunk_mask()
        for g in range(G):
            cs = slice(g * GROUP_DIM, (g + 1) * GROUP_DIM)
            wg = jnp.where(mask, ws_ref[g], 0.0).astype(BF16)
            for b in range(tm // GMLP_BLOCK):
                rs = slice(b * GMLP_BLOCK, (b + 1) * GMLP_BLOCK)
                s = jnp.dot(wg, vn[rs, cs].astype(BF16), preferred_element_type=F32) + bs_ref[:, cs]
                o_ref[rs, cs] = (u[rs, cs] * s).astype(BF16)

    return pl.pallas_call(
        body, name="gate_fwd", grid=(T // tm,),
        in_specs=[pl.BlockSpec((tm, W2), lambda i: (i, 0)), pl.BlockSpec((1, W), lambda i: (0, 0)),
                  pl.BlockSpec((G, GMLP_BLOCK, GMLP_BLOCK), lambda i: (0, 0, 0)),
                  pl.BlockSpec((GMLP_BLOCK, W), lambda i: (0, 0))],
        out_specs=pl.BlockSpec((tm, W), lambda i: (i, 0)), out_shape=jax.ShapeDtypeStruct((T, W), BF16),
        compiler_params=_params(("arbitrary",)))(zp, gv, ws, bs_tile)


def _gate_bwd(zp, d_out, gv, ws, bs_tile, *, tm=256):
    T, W2 = zp.shape
    W = W2 // 2
    G = W // GROUP_DIM
    tm = _row_tile(T, tm)
    nm = T // tm

    def body(zp_ref, do_ref, gv_ref, ws_ref, bs_ref, dzp_ref, dws_ref, dbs_ref, dgv_ref, du_scr, dvn_scr, dsum_scr):
        i = pl.program_id(0)

        @pl.when(i == 0)
        def _():
            dws_ref[...] = jnp.zeros_like(dws_ref)
            dgv_ref[...] = jnp.zeros_like(dgv_ref)
            dsum_scr[...] = jnp.zeros_like(dsum_scr)

        zp = zp_ref[...]
        z = _gelu(zp)
        u, v = z[:, :W], z[:, W:]
        n, r = _rms(v)
        gv = gv_ref[...]
        vn = n * gv
        d_out = do_ref[...].astype(F32)
        mask = _chunk_mask()
        for g in range(G):
            cs = slice(g * GROUP_DIM, (g + 1) * GROUP_DIM)
            wg = jnp.where(mask, ws_ref[g], 0.0).astype(BF16)
            dw = jnp.zeros((GMLP_BLOCK, GMLP_BLOCK), F32)
            for b in range(tm // GMLP_BLOCK):
                rs = slice(b * GMLP_BLOCK, (b + 1) * GMLP_BLOCK)
                vb = vn[rs, cs].astype(BF16)
                s = jnp.dot(wg, vb, preferred_element_type=F32) + bs_ref[:, cs]
                du_scr[rs, cs] = d_out[rs, cs] * s
                ds = d_out[rs, cs] * u[rs, cs]
                dsb = ds.astype(BF16)
                dvn_scr[rs, cs] = lax.dot_general(wg, dsb, (((0,), (0,)), ((), ())), preferred_element_type=F32)
                dw = dw + lax.dot_general(dsb, vb, (((1,), (1,)), ((), ())), preferred_element_type=F32)
                dsum_scr[:, cs] += ds
            dws_ref[g] += jnp.where(mask, dw, 0.0)
        dvn = dvn_scr[...]
        dgv_ref[...] += jnp.sum(dvn * n, axis=0, keepdims=True)
        t = dvn * gv
        dv = r * (t - n * jnp.mean(t * n, axis=-1, keepdims=True))
        dzp_ref[:, :W] = (du_scr[...] * _gelu_grad(zp[:, :W])).astype(BF16)
        dzp_ref[:, W:] = (dv * _gelu_grad(zp[:, W:])).astype(BF16)

        @pl.when(i == nm - 1)
        def _():
            sel = (lax.broadcasted_iota(jnp.int32, (G, W), 1) // GROUP_DIM
                   == lax.broadcasted_iota(jnp.int32, (G, W), 0)).astype(F32)
            dbs_ref[...] = lax.dot_general(sel, dsum_scr[...], (((1,), (1,)), ((), ())),
                                           precision=lax.Precision.HIGHEST, preferred_element_type=F32)

    return pl.pallas_call(
        body, name="gate_bwd", grid=(nm,),
        in_specs=[pl.BlockSpec((tm, W2), lambda i: (i, 0)), pl.BlockSpec((tm, W), lambda i: (i, 0)),
                  pl.BlockSpec((1, W), lambda i: (0, 0)),
                  pl.BlockSpec((G, GMLP_BLOCK, GMLP_BLOCK), lambda i: (0, 0, 0)),
                  pl.BlockSpec((GMLP_BLOCK, W), lambda i: (0, 0))],
        out_specs=[pl.BlockSpec((tm, W2), lambda i: (i, 0)),
                   pl.BlockSpec((G, GMLP_BLOCK, GMLP_BLOCK), lambda i: (0, 0, 0)),
                   pl.BlockSpec((G, GMLP_BLOCK), lambda i: (0, 0)), pl.BlockSpec((1, W), lambda i: (0, 0))],
        out_shape=[jax.ShapeDtypeStruct((T, W2), BF16), jax.ShapeDtypeStruct((G, GMLP_BLOCK, GMLP_BLOCK), F32),
                   jax.ShapeDtypeStruct((G, GMLP_BLOCK), F32), jax.ShapeDtypeStruct((1, W), F32)],
        scratch_shapes=[pltpu.VMEM((tm, W), F32), pltpu.VMEM((tm, W), F32), pltpu.VMEM((GMLP_BLOCK, W), F32)],
        compiler_params=_params(("arbitrary",)))(zp, d_out, gv, ws, bs_tile)


HALO = 16


def _taps(ext, w, b):
    a, a1, a2 = ext[HALO:], pltpu.roll(ext, 1, 0)[HALO:], pltpu.roll(ext, 2, 0)[HALO:]
    return w[2:3] * a + w[1:2] * a1 + w[0:1] * a2 + b, a, a1, a2


def _conv_fwd(a, cw, cb, S, *, tm=256):
    _, T, F = a.shape
    tc = _col_tile(F)
    tm = _row_tile(S, tm)
    hb = tm // HALO

    def body(a_ref, p_ref, w_ref, b_ref, o_ref, c_ref):
        first = (pl.program_id(1) * tm) % S == 0
        keep = jnp.where(first, 0.0, 1.0)

        def conv(s):
            ext = jnp.concatenate([p_ref[s].astype(F32) * keep, a_ref[s].astype(F32)], axis=0)
            c = _taps(ext, w_ref[s], b_ref[s:s + 1, :])[0].astype(BF16)
            c_ref[s] = c
            return c.astype(F32)

        up, gate = conv(0), conv(1)
        o_ref[...] = (gate * jax.nn.sigmoid(gate) * up).astype(BF16)

    return pl.pallas_call(
        body, name="conv_fwd", grid=(F // tc, T // tm),
        in_specs=[pl.BlockSpec((2, tm, tc), lambda j, i: (0, i, j)),
                  pl.BlockSpec((2, HALO, tc), lambda j, i: (0, jnp.maximum(i * hb - 1, 0), j)),
                  pl.BlockSpec((2, 3, tc), lambda j, i: (0, 0, j)), pl.BlockSpec((2, tc), lambda j, i: (0, j))],
        out_specs=[pl.BlockSpec((tm, tc), lambda j, i: (i, j)), pl.BlockSpec((2, tm, tc), lambda j, i: (0, i, j))],
        out_shape=[jax.ShapeDtypeStruct((T, F), BF16), jax.ShapeDtypeStruct((2, T, F), BF16)],
        compiler_params=_params(("arbitrary", "arbitrary")))(a, a, cw, cb)


def _ffn_in_conv(h, w, g, cw, cb, S, *, name, tm=256):
    T, D = h.shape
    F = w.shape[-1] // 2
    tc = _col_tile(F)
    tm = _row_tile(S, tm)

    def body(h_ref, w_ref, g_ref, cw_ref, cb_ref, y_ref, a_ref, c_ref, n_ref, tail):
        first = (pl.program_id(0) * tm) % S == 0
        nb = (_rms(h_ref[...])[0] * g_ref[...]).astype(BF16)
        n_ref[...] = nb
        for j in range(F // tc):
            cs = slice(j * tc, (j + 1) * tc)
            conv = []
            for s in range(2):
                acc = jnp.dot(nb, w_ref[:, s * F + j * tc:s * F + (j + 1) * tc], preferred_element_type=F32)
                ab = acc.astype(BF16)
                a_ref[s, :, cs] = ab
                af = ab.astype(F32)
                ext = jnp.concatenate([jnp.where(first, 0.0, tail[s, :, cs]), af], axis=0)
                tail[s, :, cs] = af[tm - HALO:, :]
                cv = _taps(ext, cw_ref[s, :, cs], cb_ref[s:s + 1, cs])[0].astype(BF16)
                c_ref[s, :, cs] = cv
                conv.append(cv.astype(F32))
            up, gate = conv
            y_ref[:, cs] = (gate * jax.nn.sigmoid(gate) * up).astype(BF16)

    row = lambda width: pl.BlockSpec((tm, width), lambda i: (i, 0))
    wide = pl.BlockSpec((2, tm, F), lambda i: (0, i, 0))
    return pl.pallas_call(
        body, name=name, grid=(T // tm,),
        in_specs=[row(D), pl.BlockSpec((None, D, 2 * F), lambda i: (0, 0, 0)), pl.BlockSpec((1, D), lambda i: (0, 0)),
                  pl.BlockSpec((2, 3, F), lambda i: (0, 0, 0)), pl.BlockSpec((2, F), lambda i: (0, 0))],
        out_specs=[row(F), wide, wide, row(D)],
        out_shape=[jax.ShapeDtypeStruct((T, F), BF16), jax.ShapeDtypeStruct((2, T, F), BF16),
                   jax.ShapeDtypeStruct((2, T, F), BF16), jax.ShapeDtypeStruct((T, D), BF16)],
        scratch_shapes=[pltpu.VMEM((2, HALO, F), F32)],
        compiler_params=_params(("arbitrary",)))(h, w, g.reshape(1, D), cw, cb)


def _conv_bwd(a, c, dy, cw, S, *, tm=256):
    _, T, F = a.shape
    tc = _col_tile(F)
    tm = _row_tile(S, tm)
    nm = T // tm
    hb = tm // HALO
    TE = tm + HALO
    nxt = lambda j, i: jnp.minimum((i + 1) * hb, T // HALO - 1)

    def body(a_ref, c_ref, nc_ref, dy_ref, ndy_ref, w_ref, da_ref, dw_ref, db_ref):
        i = pl.program_id(1)
        last = ((i + 1) * tm) % S == 0
        keep_n = jnp.where(last, 0.0, 1.0)
        dyf = jnp.concatenate([dy_ref[...].astype(F32), ndy_ref[...].astype(F32) * keep_n], axis=0)
        up = jnp.concatenate([c_ref[0].astype(F32), nc_ref[0].astype(F32)], axis=0)
        gate = jnp.concatenate([c_ref[1].astype(F32), nc_ref[1].astype(F32)], axis=0)
        sg = jax.nn.sigmoid(gate)
        d_up = dyf * (gate * sg)
        d_gate = dyf * up * (sg * (1.0 + gate * (1.0 - sg)))

        @pl.when(i == 0)
        def _():
            dw_ref[...] = jnp.zeros_like(dw_ref)
            db_ref[...] = jnp.zeros_like(db_ref)

        def back(s, d):
            a = a_ref[s].astype(F32)
            w = w_ref[s]
            u1, u2 = pltpu.roll(d, TE - 1, 0), pltpu.roll(d, TE - 2, 0)
            db_ref[s:s + 1, :] += jnp.sum(d[:tm], axis=0, keepdims=True)
            dw_ref[s, 2:3, :] += jnp.sum(d[:tm] * a, axis=0, keepdims=True)
            dw_ref[s, 1:2, :] += jnp.sum(u1[:tm] * a, axis=0, keepdims=True)
            dw_ref[s, 0:1, :] += jnp.sum(u2[:tm] * a, axis=0, keepdims=True)
            da_ref[s] = (w[2:3] * d + w[1:2] * u1 + w[0:1] * u2)[:tm].astype(BF16)

        back(0, d_up)
        back(1, d_gate)

    cur = pl.BlockSpec((2, tm, tc), lambda j, i: (0, i, j))
    return pl.pallas_call(
        body, name="conv_bwd", grid=(F // tc, nm),
        in_specs=[cur, cur, pl.BlockSpec((2, HALO, tc), lambda j, i: (0, nxt(j, i), j)),
                  pl.BlockSpec((tm, tc), lambda j, i: (i, j)), pl.BlockSpec((HALO, tc), lambda j, i: (nxt(j, i), j)),
                  pl.BlockSpec((2, 3, tc), lambda j, i: (0, 0, j))],
        out_specs=[cur, pl.BlockSpec((2, 3, tc), lambda j, i: (0, 0, j)), pl.BlockSpec((2, tc), lambda j, i: (0, j))],
        out_shape=[jax.ShapeDtypeStruct((2, T, F), BF16), jax.ShapeDtypeStruct((2, 3, F), F32),
                   jax.ShapeDtypeStruct((2, F), F32)],
        compiler_params=_params(("arbitrary", "arbitrary")))(a, c, c, dy, dy, cw)


def _bias_index():
    idx = np.arange(F_LEN)
    d = np.where(idx < K_SPAN, idx, idx - F_LEN)
    return np.clip(PAD - d, -REL_CLIP, REL_CLIP) + REL_CLIP


ROW_GROUP = 16


def _roll_rows(x, sign, unit, steps):
    rows = lax.broadcasted_iota(jnp.int32, x.shape, 0)
    step = 1
    while step < steps:
        shift = unit * step if sign > 0 else F_LEN - unit * step
        x = jnp.where((rows & step) != 0, pltpu.roll(x, shift, 1), x)
        step *= 2
    return x


def _bias_expand(frow):
    H = frow.shape[0]
    groups = Q_BLOCK // ROW_GROUP

    def body(f_ref, o_ref):
        coarse = _roll_rows(jnp.broadcast_to(f_ref[...], (groups, F_LEN)), 1, ROW_GROUP, groups)
        x = jnp.concatenate([jnp.broadcast_to(coarse[a:a + 1], (ROW_GROUP, F_LEN)) for a in range(groups)], axis=0)
        x = _roll_rows(x, 1, 1, ROW_GROUP)[:, :K_SPAN]
        qc = lax.broadcasted_iota(jnp.int32, (Q_BLOCK, K_SPAN), 0) // CHUNK * CHUNK
        kj = lax.broadcasted_iota(jnp.int32, (Q_BLOCK, K_SPAN), 1)
        o_ref[...] = jnp.where((kj >= qc) & (kj < qc + PAD + CHUNK), x, NEG_INF)

    return pl.pallas_call(
        body, name="bias_expand", grid=(H,),
        in_specs=[pl.BlockSpec((None, 1, F_LEN), lambda h: (h, 0, 0))],
        out_specs=pl.BlockSpec((None, Q_BLOCK, K_SPAN), lambda h: (h, 0, 0)),
        out_shape=jax.ShapeDtypeStruct((H, Q_BLOCK, K_SPAN), F32), compiler_params=_params(("arbitrary",)))(frow)


def _bias_reduce(dbias, n_rel):
    H = dbias.shape[0]
    onehot = jnp.asarray((_bias_index()[:, None] == np.arange(n_rel)[None, :]).astype(np.float32))

    def body(d_ref, oh_ref, o_ref):
        x = jnp.concatenate([d_ref[...], jnp.zeros((Q_BLOCK, F_LEN - K_SPAN), F32)], axis=1)
        fine = _roll_rows(x, -1, 1, ROW_GROUP).reshape(Q_BLOCK // ROW_GROUP, ROW_GROUP, F_LEN)
        coarse = _roll_rows(jnp.sum(fine, axis=1), -1, ROW_GROUP, Q_BLOCK // ROW_GROUP)
        row = jnp.sum(coarse, axis=0, keepdims=True)
        row8 = jnp.broadcast_to(row, (8, F_LEN))
        o_ref[...] = jnp.dot(row8, oh_ref[...], precision=lax.Precision.HIGHEST, preferred_element_type=F32)[0:1]

    return pl.pallas_call(
        body, name="bias_reduce", grid=(H,),
        in_specs=[pl.BlockSpec((None, Q_BLOCK, K_SPAN), lambda h: (h, 0, 0)),
                  pl.BlockSpec((F_LEN, n_rel), lambda h: (0, 0))],
        out_specs=pl.BlockSpec((None, 1, n_rel), lambda h: (h, 0, 0)),
        out_shape=jax.ShapeDtypeStruct((H, 1, n_rel), F32), compiler_params=_params(("arbitrary",)))(dbias, onehot)


def _attn_specs(S):
    hw = HEADS_PER_STEP * HEAD_DIM
    qspec = pl.BlockSpec((None, Q_BLOCK, hw), lambda g, b, i: (b, i, g))
    kspec = pl.BlockSpec((None, None, S, hw), lambda g, b, i: (0, b, 0, g))
    vspec = pl.BlockSpec((None, None, S, hw), lambda g, b, i: (1, b, 0, g))
    bspec = pl.BlockSpec((HEADS_PER_STEP, Q_BLOCK, K_SPAN), lambda g, b, i: (g, 0, 0))
    return hw, qspec, kspec, vspec, bspec


def _load_padded(k_ref, v_ref, kp, vp):
    kp[:PAD, :] = jnp.zeros((PAD, kp.shape[1]), BF16)
    vp[:PAD, :] = jnp.zeros((PAD, vp.shape[1]), BF16)
    kp[PAD:, :] = k_ref[...]
    vp[PAD:, :] = v_ref[...]


def _attn_exp(q_ref, kp, b_ref, h, q0, before):
    hs = slice(h * HEAD_DIM, (h + 1) * HEAD_DIM)
    kh = kp[pl.ds(q0, K_SPAN), hs]
    s = lax.dot_general(q_ref[:, hs], kh, (((1,), (1,)), ((), ())), preferred_element_type=F32) + b_ref[h] + before
    p = jnp.exp(s - jnp.max(s, axis=-1, keepdims=True))
    return p, 1.0 / jnp.sum(p, axis=-1, keepdims=True), kh


def _before_start(q0):
    kj = lax.broadcasted_iota(jnp.int32, (1, K_SPAN), 1)
    return jnp.where(q0 + kj >= PAD, 0.0, NEG_INF)


def _attn_fwd(q, kv, bias, B, S):
    HD = q.shape[-1]
    hw, qspec, kspec, vspec, bspec = _attn_specs(S)

    def body(q_ref, k_ref, v_ref, b_ref, o_ref, kp, vp):
        i = pl.program_id(2)

        @pl.when(i == 0)
        def _():
            _load_padded(k_ref, v_ref, kp, vp)

        q0 = pl.multiple_of(i * Q_BLOCK, Q_BLOCK)
        before = _before_start(q0)
        outs = []
        for h in range(HEADS_PER_STEP):
            hs = slice(h * HEAD_DIM, (h + 1) * HEAD_DIM)
            p, inv, _ = _attn_exp(q_ref, kp, b_ref, h, q0, before)
            outs.append(jnp.dot(p.astype(BF16), vp[pl.ds(q0, K_SPAN), hs], preferred_element_type=F32) * inv)
        o_ref[...] = jnp.concatenate(outs, axis=1).astype(BF16)

    return pl.pallas_call(
        body, name="attn_fwd", grid=(HD // hw, B, S // Q_BLOCK), in_specs=[qspec, kspec, vspec, bspec],
        out_specs=qspec, out_shape=jax.ShapeDtypeStruct((B, S, HD), BF16),
        scratch_shapes=[pltpu.VMEM((S + PAD, hw), BF16), pltpu.VMEM((S + PAD, hw), BF16)],
        compiler_params=_params(("arbitrary", "arbitrary", "arbitrary")))(q, kv, kv, bias)


def _attn_bwd(q, kv, bias, do, B, S):
    HD = q.shape[-1]
    H = HD // HEAD_DIM
    hw, qspec, kspec, vspec, bspec = _attn_specs(S)
    scale = HEAD_DIM ** -0.5
    nq = S // Q_BLOCK

    def body(q_ref, k_ref, v_ref, b_ref, do_ref, dq_ref, dkv_ref, db_ref, kp, vp, dk_acc, dv_acc):
        b, i = pl.program_id(1), pl.program_id(2)
        q0 = pl.multiple_of(i * Q_BLOCK, Q_BLOCK)

        @pl.when(i == 0)
        def _():
            _load_padded(k_ref, v_ref, kp, vp)
            dk_acc[...] = jnp.zeros_like(dk_acc)
            dv_acc[...] = jnp.zeros_like(dv_acc)

        @pl.when((i == 0) & (b == 0))
        def _():
            db_ref[...] = jnp.zeros_like(db_ref)

        before = _before_start(q0)
        for h in range(HEADS_PER_STEP):
            hs = slice(h * HEAD_DIM, (h + 1) * HEAD_DIM)
            p, inv, kh = _attn_exp(q_ref, kp, b_ref, h, q0, before)
            p = p * inv
            doh = do_ref[:, hs]
            dp = lax.dot_general(doh, vp[pl.ds(q0, K_SPAN), hs], (((1,), (1,)), ((), ())),
                                 preferred_element_type=F32)
            ds = p * (dp - jnp.sum(p * dp, axis=-1, keepdims=True))
            db_ref[h] += ds
            dsb = ds.astype(BF16)
            dq_ref[:, hs] = (jnp.dot(dsb, kh, preferred_element_type=F32) * scale).astype(BF16)
            dk_acc[pl.ds(q0, K_SPAN), hs] += lax.dot_general(dsb, q_ref[:, hs], (((0,), (0,)), ((), ())),
                                                              preferred_element_type=F32)
            dv_acc[pl.ds(q0, K_SPAN), hs] += lax.dot_general(p.astype(BF16), doh, (((0,), (0,)), ((), ())),
                                                              preferred_element_type=F32)

        @pl.when(i == nq - 1)
        def _():
            dkv_ref[0] = dk_acc[PAD:, :].astype(BF16)
            dkv_ref[1] = dv_acc[PAD:, :].astype(BF16)

    return pl.pallas_call(
        body, name="attn_bwd", grid=(HD // hw, B, nq), in_specs=[qspec, kspec, vspec, bspec, qspec],
        out_specs=[qspec, pl.BlockSpec((2, None, S, hw), lambda g, b, i: (0, b, 0, g)), bspec],
        out_shape=[jax.ShapeDtypeStruct((B, S, HD), BF16), jax.ShapeDtypeStruct((2, B, S, HD), BF16),
                   jax.ShapeDtypeStruct((H, Q_BLOCK, K_SPAN), F32)],
        scratch_shapes=[pltpu.VMEM((S + PAD, hw), BF16), pltpu.VMEM((S + PAD, hw), BF16),
                        pltpu.VMEM((S + PAD, hw), F32), pltpu.VMEM((S + PAD, hw), F32)],
        compiler_params=_params(("arbitrary", "arbitrary", "arbitrary")))(q, kv, kv, bias, do)


def _loss_head(h, g, target, *, tm=512):
    T, D = h.shape
    tm = _row_tile(T, tm)

    def body(h_ref, g_ref, t_ref, dh_ref, loss_ref, dg_ref):
        @pl.when(pl.program_id(0) == 0)
        def _():
            loss_ref[...] = jnp.zeros_like(loss_ref)
            dg_ref[...] = jnp.zeros_like(dg_ref)

        n, r = _rms(h_ref[...])
        g = g_ref[...]
        e = n * g - t_ref[...]
        loss_ref[...] += 0.5 * jnp.sum(jnp.mean(e * e, axis=-1, keepdims=True), axis=0, keepdims=True)
        dy = e * (1.0 / D)
        dg_ref[...] += jnp.sum(dy * n, axis=0, keepdims=True)
        t = dy * g
        dh_ref[...] = r * (t - n * jnp.mean(t * n, axis=-1, keepdims=True))

    row = pl.BlockSpec((tm, D), lambda i: (i, 0))
    return pl.pallas_call(
        body, name="loss_head", grid=(T // tm,), in_specs=[row, pl.BlockSpec((1, D), lambda i: (0, 0)), row],
        out_specs=[row, pl.BlockSpec((8, 128), lambda i: (0, 0)), pl.BlockSpec((1, D), lambda i: (0, 0))],
        out_shape=[jax.ShapeDtypeStruct((T, D), F32), jax.ShapeDtypeStruct((8, 128), F32),
                   jax.ShapeDtypeStruct((1, D), F32)],
        compiler_params=_params(("arbitrary",)))(h, g.reshape(1, D), target)


def _sub_rows(R):
    for cand in (256, 352, 128, 64, 8):
        if R % cand == 0 and R > cand:
            return cand
    return R


def _adamw(w, g, m, v, *, name):
    R, C = w.shape
    tr = _sub_rows(R)

    def body(w_ref, g_ref, m_ref, v_ref, d_ref, nm_ref, nv_ref):
        g = g_ref[...]
        m = ADAM_B1 * m_ref[...] + (1.0 - ADAM_B1) * g
        v = ADAM_B2 * v_ref[...] + (1.0 - ADAM_B2) * (g * g)
        m_hat = m / (1.0 - ADAM_B1 ** ADAM_STEP)
        v_hat = v / (1.0 - ADAM_B2 ** ADAM_STEP)
        d_ref[...] = -ADAM_LR * (m_hat / (jnp.sqrt(v_hat) + ADAM_EPS) + ADAM_WD * w_ref[...])
        nm_ref[...] = m
        nv_ref[...] = v

    spec = pl.BlockSpec((tr, C), lambda i: (i, 0))
    return pl.pallas_call(body, name=name, grid=(R // tr,), in_specs=[spec] * 4, out_specs=[spec] * 3,
                          out_shape=[jax.ShapeDtypeStruct((R, C), F32)] * 3,
                          compiler_params=_params(("arbitrary",)))(w, g, m, v)


def _add_pair(units, got, core, *, name):
    n4, R, C = got.shape
    rows = n4 * R
    tr = 512 if rows % 512 == 0 else R

    def body(c_ref, u_ref, got_ref, o_ref):
        o_ref[...] = (u_ref[...].astype(F32) + got_ref[...].astype(F32)).astype(BF16)

    spec = pl.BlockSpec((tr, C), lambda i, c: (i, 0))
    grid_spec = pltpu.PrefetchScalarGridSpec(
        num_scalar_prefetch=1, grid=(rows // tr,),
        in_specs=[pl.BlockSpec((None, tr, C), lambda i, c: (c[0], i, 0)), spec], out_specs=spec)
    out = pl.pallas_call(body, name=name, grid_spec=grid_spec, out_shape=jax.ShapeDtypeStruct((rows, C), BF16),
                         compiler_params=_params(("arbitrary",)))(core.reshape(1), units.reshape(2, rows, C),
                                                                   got.reshape(rows, C))
    return out.reshape(n4, R, C)


def _sum_chips(w, own, got, pos, *, name, layer=0, into=None):
    _, R, C = own.shape
    tr = _sub_rows(R)
    nr = R // tr

    def body(p_ref, own_ref, got_ref, *rest):
        o_ref = rest[-1]
        o_ref[...] = (own_ref[...].astype(F32) + got_ref[0].astype(F32) + got_ref[1].astype(F32)
                      + got_ref[2].astype(F32))

    if w.row_sharded:
        out_map = lambda i, p: (layer, i, p[1])
    else:
        out_map = lambda i, p: (layer, p[1] * nr + i, 0)
    ins = [pos, own, got]
    in_specs = [pl.BlockSpec((None, tr, C), lambda i, p: (p[0], i, 0)),
                pl.BlockSpec((3, tr, C), lambda i, p: (0, i, 0))]
    alias = {}
    if into is not None:
        ins.append(into)
        in_specs.append(ANY)
        alias = {3: 0}
    grid_spec = pltpu.PrefetchScalarGridSpec(num_scalar_prefetch=1, grid=(nr,), in_specs=in_specs,
                                             out_specs=pl.BlockSpec((None, tr, C), out_map))
    return pl.pallas_call(body, name=name, grid_spec=grid_spec, input_output_aliases=alias,
                          out_shape=jax.ShapeDtypeStruct((w.L, w.ks, w.ns), F32),
                          compiler_params=_params(("arbitrary",)))(*ins)


def _mesh_pos():
    return lax.axis_index("x"), lax.axis_index("y"), lax.axis_index("c")


def _other_chips(x, y):
    return [(1 - x, y), (x, 1 - y), (1 - x, 1 - y)]


ANY = pl.BlockSpec(memory_space=pl.ANY)


class _W:
    def __init__(self, name, shard, row_sharded):
        self.name = name
        self.L, ks, ns = shard.shape
        self.row_sharded = row_sharded
        self.K, self.N = (ks * N_CHIPS, ns) if row_sharded else (ks, ns * N_CHIPS)
        self.ks, self.ns = ks, ns

    def shard_of(self, full, j):
        if self.row_sharded:
            return full.at[:, pl.ds(j * self.ks, self.ks), :]
        return full.at[:, :, pl.ds(j * self.ns, self.ns)]

    def half_of(self, shard, c):
        if self.row_sharded:
            return shard.at[:, :, pl.ds(c * (self.ns // 2), self.ns // 2)]
        return shard.at[:, pl.ds(c * (self.ks // 2), self.ks // 2), :]


HBM = pl.BlockSpec(memory_space=pltpu.HBM)
SEM = pl.BlockSpec(memory_space=pltpu.SEMAPHORE)
IN_FLIGHT = pltpu.SideEffectType.DATAFLOW_SIDE_EFFECTING


def _in_hbm(a):
    return pltpu.with_memory_space_constraint(a, pltpu.HBM)


def _gather_start(ws, shards, after):
    nw = len(ws)

    def body(*refs):
        src, dst = refs[:nw], refs[nw:2 * nw]
        send, recv = refs[2 * nw + 1:3 * nw + 1], refs[3 * nw + 1:4 * nw + 1]
        x, y, c = _mesh_pos()
        me = 2 * x + y
        for i, w in enumerate(ws):
            for f, (px, py) in enumerate(_other_chips(x, y)):
                pltpu.make_async_remote_copy(src_ref=w.half_of(src[i], c), dst_ref=w.half_of(w.shard_of(dst[i], me), c),
                                             send_sem=send[i].at[f], recv_sem=recv[i].at[f], device_id=(px, py, c),
                                             device_id_type=MESH).start()

    fulls = [lax.empty((w.L, w.K, w.N), BF16) for w in ws]
    out = pl.pallas_call(
        body, name="gather_start", in_specs=[HBM] * (2 * nw) + [ANY],
        out_specs=[SEM] * (2 * nw) + [HBM] * (2 * nw),
        out_shape=[pltpu.SemaphoreType.DMA((3,))] * (2 * nw)
        + [pltpu.HBM(s.shape, BF16) for s in shards] + [pltpu.HBM(f.shape, BF16) for f in fulls],
        input_output_aliases={i: 2 * nw + i for i in range(2 * nw)},
        compiler_params=pltpu.CompilerParams(has_side_effects=IN_FLIGHT))(
            *[_in_hbm(s) for s in shards], *[_in_hbm(f) for f in fulls], after)
    return [(out[i], out[nw + i], out[2 * nw + i], out[3 * nw + i]) for i in range(nw)]


def _gather_wait(ws, flight, after, *, name):
    nw = len(ws)

    def body(*refs):
        src, dst = refs[:nw], refs[nw:2 * nw]
        send, recv = refs[2 * nw:3 * nw], refs[3 * nw:4 * nw]
        x, y, c = _mesh_pos()
        for i, w in enumerate(ws):
            for f, (px, py) in enumerate(_other_chips(x, y)):
                landed = w.half_of(w.shard_of(dst[i], 2 * px + py), c)
                cp = pltpu.make_async_remote_copy(src_ref=w.half_of(src[i], c), dst_ref=landed, send_sem=send[i].at[f],
                                                  recv_sem=recv[i].at[f], device_id=(px, py, c), device_id_type=MESH)
                cp.wait_send()
                cp.wait_recv()

    shards, fulls = [fl[2] for fl in flight], [fl[3] for fl in flight]
    out = pl.pallas_call(
        body, name=name, in_specs=[HBM] * (2 * nw) + [SEM] * (2 * nw) + [ANY],
        out_specs=[HBM] * (2 * nw),
        out_shape=[pltpu.HBM(s.shape, BF16) for s in shards] + [pltpu.HBM(f.shape, BF16) for f in fulls],
        input_output_aliases={i: i for i in range(2 * nw)},
        compiler_params=pltpu.CompilerParams(has_side_effects=IN_FLIGHT))(
            *shards, *fulls, *[fl[0] for fl in flight], *[fl[1] for fl in flight], after)
    return out[:nw], out[nw:]


def _gather_finish(ws, shards, fulls, *, name):
    nw = len(ws)

    def body(*refs):
        src, dst, stage = refs[:nw], refs[3 * nw:4 * nw], refs[4 * nw:5 * nw]
        send_sems, recv_sems, load_sems, store_sems = refs[5 * nw:]
        x, y, c = _mesh_pos()
        me = 2 * x + y
        sibling = (x, y, 1 - c)
        chips = _other_chips(x, y)

        def fwd(i, w, f, half):
            px, py = chips[f]
            landed = w.half_of(w.shard_of(dst[i], 2 * px + py), half)
            return pltpu.make_async_remote_copy(src_ref=landed, dst_ref=landed, send_sem=send_sems.at[3 * i + f],
                                                recv_sem=recv_sems.at[3 * i + f], device_id=sibling,
                                                device_id_type=MESH)

        loads = [pltpu.make_async_copy(src[i], stage[i], load_sems.at[i]) for i in range(nw)]
        for cp in loads:
            cp.start()
        sends = [fwd(i, w, f, c) for i, w in enumerate(ws) for f in range(3)]
        for cp in sends:
            cp.start()
        stores = [pltpu.make_async_copy(stage[i], w.shard_of(dst[i], me), store_sems.at[i])
                  for i, w in enumerate(ws)]
        for ld, st in zip(loads, stores):
            ld.wait()
            st.start()
        for i, w in enumerate(ws):
            for f in range(3):
                fwd(i, w, f, 1 - c).wait_recv()
        for cp in sends:
            cp.wait_send()
        for cp in stores:
            cp.wait()

    out = pl.pallas_call(
        body, name=name, in_specs=[ANY] * (2 * nw), out_specs=[ANY] * (2 * nw),
        out_shape=[jax.ShapeDtypeStruct(s.shape, BF16) for s in shards]
        + [jax.ShapeDtypeStruct(f.shape, BF16) for f in fulls],
        input_output_aliases={i: i for i in range(2 * nw)},
        scratch_shapes=[pltpu.VMEM((w.L, w.ks, w.ns), BF16) for w in ws]
        + [pltpu.SemaphoreType.DMA((3 * nw,)), pltpu.SemaphoreType.DMA((3 * nw,)), pltpu.SemaphoreType.DMA((nw,)),
           pltpu.SemaphoreType.DMA((nw,))],
        compiler_params=_params(has_side_effects=True))(*shards, *fulls)
    return out[nw:]


def _split_copies(name, srcs, lands, n_sems, copies_of, *, flight=None, after=None):
    n = len(srcs)
    starting = flight is None

    def body(*refs):
        src, land = refs[:n], refs[n:2 * n]
        sems = refs[2 * n + 1:4 * n + 1] if starting else refs[2 * n:4 * n]
        for i in range(n):
            for cp in copies_of(i, src[i], land[i], sems[i], sems[n + i]):
                if starting:
                    cp.start()
                else:
                    cp.wait_send()
                    cp.wait_recv()

    thru = [pltpu.HBM(a.shape, a.dtype) for a in list(srcs) + list(lands)]
    if starting:
        out = pl.pallas_call(
            body, name=name, in_specs=[HBM] * (2 * n) + [ANY], out_specs=[SEM] * (2 * n) + [HBM] * (2 * n),
            out_shape=[pltpu.SemaphoreType.DMA((n_sems,))] * (2 * n) + thru,
            input_output_aliases={i: 2 * n + i for i in range(2 * n)},
            compiler_params=pltpu.CompilerParams(has_side_effects=IN_FLIGHT))(
                *[_in_hbm(a) for a in srcs], *[_in_hbm(a) for a in lands], after)
        return [(out[i], out[n + i], out[2 * n + i], out[3 * n + i]) for i in range(n)]
    out = pl.pallas_call(
        body, name=name, in_specs=[HBM] * (2 * n) + [SEM] * (2 * n) + [ANY], out_specs=[HBM] * (2 * n),
        out_shape=thru, input_output_aliases={i: i for i in range(2 * n)},
        compiler_params=pltpu.CompilerParams(has_side_effects=IN_FLIGHT))(
            *srcs, *lands, *[fl[0] for fl in flight], *[fl[1] for fl in flight], after)
    return out[:n], out[n:]


def _sum8(land, vec, me):
    R = vec.shape[0]

    def body(me_ref, land_ref, vec_ref, o_ref):
        acc = jnp.zeros((R, 128), F32)
        for d in range(8):
            acc = acc + jnp.where(me_ref[0] == d, vec_ref[...], land_ref[d])
        o_ref[...] = acc

    grid_spec = pltpu.PrefetchScalarGridSpec(
        num_scalar_prefetch=1, grid=(1,),
        in_specs=[pl.BlockSpec((8, R, 128), lambda i, m: (0, 0, 0)), pl.BlockSpec((R, 128), lambda i, m: (0, 0))],
        out_specs=pl.BlockSpec((R, 128), lambda i, m: (0, 0)))
    return pl.pallas_call(body, name="sum8", grid_spec=grid_spec, out_shape=jax.ShapeDtypeStruct((R, 128), F32),
                          compiler_params=_params(("arbitrary",)))(me.reshape(1), land, vec)


def _swap_copies(i, src, got, send, recv):
    x, y, c = _mesh_pos()
    return [pltpu.make_async_remote_copy(src_ref=src.at[1 - c], dst_ref=got, send_sem=send.at[0], recv_sem=recv.at[0],
                                         device_id=(x, y, 1 - c), device_id_type=MESH)]


def _gather8_copies(i, src, land, send, recv):
    x, y, c = _mesh_pos()
    me = 4 * x + 2 * y + c
    peers = [(x, y, 1 - c)] + [(px, py, pc) for px, py in _other_chips(x, y) for pc in (c, 1 - c)]
    return [pltpu.make_async_remote_copy(src_ref=src, dst_ref=land.at[me], send_sem=send.at[k], recv_sem=recv.at[k],
                                         device_id=peer, device_id_type=MESH) for k, peer in enumerate(peers)]


def _scatter_copy(src, got, send, recv, f, chip, c):
    px, py = chip
    return pltpu.make_async_remote_copy(src_ref=src.at[2 * px + py], dst_ref=got.at[f], send_sem=send.at[f],
                                        recv_sem=recv.at[f], device_id=(px, py, c), device_id_type=MESH)


def _scatter_start(sums, *, name):
    nw = len(sums)

    def body(*refs):
        src, got = refs[:nw], refs[nw:2 * nw]
        send, recv = refs[2 * nw:3 * nw], refs[3 * nw:4 * nw]
        x, y, c = _mesh_pos()
        for i in range(nw):
            for f, chip in enumerate(_other_chips(x, y)):
                _scatter_copy(src[i], got[i], send[i], recv[i], f, chip, c).start()

    lands = [lax.empty((3,) + s.shape[1:], BF16) for s in sums]
    out = pl.pallas_call(
        body, name=name, in_specs=[HBM] * (2 * nw), out_specs=[SEM] * (2 * nw) + [HBM] * (2 * nw),
        out_shape=[pltpu.SemaphoreType.DMA((3,))] * (2 * nw)
        + [pltpu.HBM(s.shape, BF16) for s in sums] + [pltpu.HBM(l.shape, BF16) for l in lands],
        input_output_aliases={i: 2 * nw + i for i in range(2 * nw)},
        compiler_params=pltpu.CompilerParams(has_side_effects=IN_FLIGHT))(
            *[_in_hbm(s) for s in sums], *[_in_hbm(l) for l in lands])
    return [(out[i], out[nw + i], out[2 * nw + i], out[3 * nw + i]) for i in range(nw)]


def _scatter_wait(flight, after):
    nw = len(flight)

    def body(*refs):
        src, got = refs[:nw], refs[nw:2 * nw]
        send, recv = refs[2 * nw:3 * nw], refs[3 * nw:4 * nw]
        x, y, c = _mesh_pos()
        for i in range(nw):
            for f, chip in enumerate(_other_chips(x, y)):
                cp = _scatter_copy(src[i], got[i], send[i], recv[i], f, chip, c)
                cp.wait_send()
                cp.wait_recv()

    sums, lands = [fl[2] for fl in flight], [fl[3] for fl in flight]
    out = pl.pallas_call(
        body, name="scatter_wait", in_specs=[HBM] * (2 * nw) + [SEM] * (2 * nw) + [ANY], out_specs=[HBM] * (2 * nw),
        out_shape=[pltpu.HBM(s.shape, BF16) for s in sums] + [pltpu.HBM(l.shape, BF16) for l in lands],
        input_output_aliases={i: i for i in range(2 * nw)},
        compiler_params=pltpu.CompilerParams(has_side_effects=IN_FLIGHT))(
            *sums, *lands, *[fl[0] for fl in flight], *[fl[1] for fl in flight], after)
    return out[:nw], out[nw:]


def _join_halves(ws, shards):
    nw = len(ws)

    def body(*refs):
        buf = refs[nw:2 * nw]
        send_sems, recv_sems = refs[2 * nw:]
        x, y, c = _mesh_pos()
        sibling = (x, y, 1 - c)

        def copy(i, w, half):
            region = w.half_of(buf[i], half)
            return pltpu.make_async_remote_copy(src_ref=region, dst_ref=region, send_sem=send_sems.at[i],
                                                recv_sem=recv_sems.at[i], device_id=sibling, device_id_type=MESH)

        sends = [copy(i, w, c) for i, w in enumerate(ws)]
        for cp in sends:
            cp.start()
        for i, w in enumerate(ws):
            copy(i, w, 1 - c).wait_recv()
        for cp in sends:
            cp.wait_send()

    return pl.pallas_call(
        body, name="join_halves", in_specs=[ANY] * nw, out_specs=[ANY] * nw,
        out_shape=[jax.ShapeDtypeStruct((w.L, w.ks, w.ns), F32) for w in ws],
        input_output_aliases={i: i for i in range(nw)},
        scratch_shapes=[pltpu.SemaphoreType.DMA((nw,)), pltpu.SemaphoreType.DMA((nw,))],
        compiler_params=_params(has_side_effects=True))(*shards)


def _allreduce_small(vec):
    R = vec.shape[0]

    def body(x_ref, o_ref, buf, send_sems, recv_sems):
        x, y, c = _mesh_pos()
        me, sibling = (x, y, c), (x, y, 1 - c)
        chips = _other_chips(x, y)

        def slot(px, py, pc):
            return buf.at[4 * px + 2 * py + pc]

        def copy(k, block, to, src=None):
            return pltpu.make_async_remote_copy(src_ref=slot(*block) if src is None else src, dst_ref=slot(*block),
                                                send_sem=send_sems.at[k], recv_sem=recv_sems.at[k], device_id=to,
                                                device_id_type=MESH)

        first = [copy(0, me, sibling, src=x_ref)] + [copy(1 + f, me, (*chip, c), src=x_ref)
                                                     for f, chip in enumerate(chips)]
        for cp in first:
            cp.start()
        passed = [copy(4 + f, (*chip, c), sibling) for f, chip in enumerate(chips)]
        for f, chip in enumerate(chips):
            copy(1 + f, (*chip, c), me).wait_recv()
            passed[f].start()
        copy(0, sibling, me).wait_recv()
        for f, chip in enumerate(chips):
            copy(4 + f, (*chip, 1 - c), me).wait_recv()
        for cp in first + passed:
            cp.wait_send()
        slot(*me)[...] = x_ref[...]
        acc = buf[0]
        for d in range(1, 8):
            acc = acc + buf[d]
        o_ref[...] = acc

    return pl.pallas_call(
        body, name="allreduce_small", in_specs=[pl.BlockSpec(memory_space=pltpu.VMEM)],
        out_specs=pl.BlockSpec(memory_space=pltpu.VMEM), out_shape=jax.ShapeDtypeStruct((R, 128), F32),
        scratch_shapes=[pltpu.VMEM((8, R, 128), F32), pltpu.SemaphoreType.DMA((7,)), pltpu.SemaphoreType.DMA((7,))],
        compiler_params=_params())(vec)


def _pack(parts):
    flat = jnp.concatenate([p.reshape(-1).astype(F32) for p in parts])
    n = flat.shape[0]
    pad = (-n) % (64 * 128)
    return jnp.pad(flat, (0, pad)).reshape(-1, 128)


def _unpack(vec, shapes):
    flat = vec.reshape(-1)
    out, off = [], 0
    for s in shapes:
        n = int(np.prod(s))
        out.append(flat[off:off + n].reshape(s))
        off += n
    return out


def kernel(x, a_norm_g, a_w_in, a_v_norm_g, a_w_s, a_b_s, a_w_out, kv_norm_g, w_kv, b_norm_g, b_w_q, b_rel_bias, b_w_o, f_norm_g, f_w_in, f_conv_w, f_conv_b, f_w_down, final_norm_g, loss_target, m_a_norm_g, m_a_w_in, m_a_v_norm_g, m_a_w_s, m_a_b_s, m_a_w_out, m_kv_norm_g, m_w_kv, m_b_norm_g, m_b_w_q, m_b_rel_bias, m_b_w_o, m_f_norm_g, m_f_w_in, m_f_conv_w, m_f_conv_b, m_f_w_down, m_final_norm_g, v_a_norm_g, v_a_w_in, v_a_v_norm_g, v_a_w_s, v_a_b_s, v_a_w_out, v_kv_norm_g, v_w_kv, v_b_norm_g, v_b_w_q, v_b_rel_bias, v_b_w_o, v_f_norm_g, v_f_w_in, v_f_conv_w, v_f_conv_b, v_f_w_down, v_final_norm_g):
    B, S, D = x.shape
    T = B * S
    xi, yi, ci = lax.axis_index("x"), lax.axis_index("y"), lax.axis_index("c")
    j_me = (2 * xi + yi).astype(jnp.int32)
    core = ci.astype(jnp.int32)
    pos = jnp.stack([j_me, core])

    w_shards = {"a_w_in": (a_w_in, False), "a_w_out": (a_w_out, True), "w_kv": (w_kv[None], False),
                "b_w_q": (b_w_q, True), "b_w_o": (b_w_o, True), "f_w_in": (f_w_in, False), "f_w_down": (f_w_down, True)}
    names = list(w_shards)
    ws = [_W(n, w_shards[n][0], w_shards[n][1]) for n in names]
    g_shards = {"a_w_in": (a_w_in, False), "a_w_out": (a_w_out, True),
                "f_w_in0": (f_w_in[0:1], False), "f_w_down0": (f_w_down[0:1], True),
                "w_kv": (w_kv[None], False), "b_w_q": (b_w_q, True), "b_w_o": (b_w_o, True),
                "f_w_in1": (f_w_in[1:2], False), "f_w_down1": (f_w_down[1:2], True)}
    g_names = list(g_shards)
    g_ws = {n: _W(n, *g_shards[n]) for n in g_names}

    Wd = a_w_in.shape[1]
    GW = a_v_norm_g.shape[1] * N_CHIPS
    F2 = f_conv_w.shape[2] * N_CHIPS
    Fh = F2 // 2
    nsd, nsg, nsf = a_norm_g.shape[1], a_v_norm_g.shape[1], f_conv_w.shape[2]
    own = (ci == 0).astype(F32)
    place = lambda sh, width, n: lax.dynamic_update_slice_in_dim(
        jnp.zeros(sh.shape[:-1] + (width,), F32), sh * own, j_me * n, axis=sh.ndim - 1)
    gathered = _allreduce_small(_pack([place(a_norm_g, Wd, nsd), place(a_v_norm_g, GW, nsg),
                                       place(f_conv_w, F2, nsf)]))
    a_g, a_vg, conv_w = _unpack(gathered, [(1, Wd), (1, GW), (2, 3, F2)])

    flight = dict(zip(g_names, _gather_start([g_ws[n] for n in g_names],
                                             [g_shards[n][0].astype(BF16) for n in g_names], gathered)))
    full = {}

    def tied(x, flight):
        x, thru = lax.optimization_barrier((x, flight[0][2]))
        return x, [flight[0][:2] + (thru,) + flight[0][3:]] + flight[1:]

    def arrive(group, after, tag):
        gw = [g_ws[n] for n in group]
        sh, fu = _gather_wait(gw, [flight[n] for n in group], after, name=f"gather_wait_{tag}")
        full.update(zip(group, _gather_finish(gw, sh, fu, name=f"gather_finish_{tag}")))
    conv_w2 = conv_w.reshape(2, 3, 2, Fh).transpose(0, 2, 1, 3)
    conv_b2 = f_conv_b.reshape(2, 2, Fh)

    h0 = x.reshape(T, D)
    target = loss_target.reshape(T, D)
    bs_tile = jnp.repeat(a_b_s[0].T, GROUP_DIM, axis=1)
    ws_a = a_w_s[0]
    scale = HEAD_DIM ** -0.5
    HD = b_w_q.shape[2]
    H = HD // HEAD_DIM
    n_rel = b_rel_bias.shape[-1]
    frow, (flight["a_w_in"],) = tied(b_rel_bias[0][:, _bias_index()].reshape(H, 1, F_LEN), [flight["a_w_in"]])
    bias = _bias_expand(frow)

    def ffn_fwd(h, l):
        yff, a, c, n = _ffn_in_conv(h, full[f"f_w_in{l}"], f_norm_g[l], conv_w2[l], conv_b2[l], S, name=f"ffn{l}_in")
        return _mm(yff, full[f"f_w_down{l}"], layer=0, res=h, name=f"ffn{l}_down"), (a, c, n, yff)

    arrive(["a_w_in", "a_w_out"], bias, "a")
    zp, n_a = _mm(h0, full["a_w_in"], layer=0, norm_g=a_g[0], emit_norm=True, name="a_in")
    out_a = _gate_fwd(zp, a_vg, ws_a, bs_tile)
    h1 = _mm(out_a, full["a_w_out"], layer=0, res=h0, name="a_out")
    arrive(["f_w_in0", "f_w_down0"], h1, "f0")
    h2, saved0 = ffn_fwd(h1, 0)
    arrive(["w_kv", "b_w_q", "b_w_o"], h2, "b")
    arrive(["f_w_in1", "f_w_down1"], h2, "f1")
    kv, n_kv = _mm(h2, full["w_kv"], layer=0, norm_g=kv_norm_g, out_dtype=BF16, split_out=True, emit_norm=True,
                   name="kv")
    q, n_q = _mm(h2, full["b_w_q"], layer=0, norm_g=b_norm_g[0], scale=scale, out_dtype=BF16, emit_norm=True,
                 name="q")
    kv4, q3 = kv.reshape(2, B, S, HD), q.reshape(B, S, HD)
    o = _attn_fwd(q3, kv4, bias, B, S).reshape(T, HD)
    h3 = _mm(o, full["b_w_o"], layer=0, res=h2, name="attn_out")
    h4, saved1 = ffn_fwd(h3, 1)
    dh, loss8, dg_final = _loss_head(h4, final_norm_g, target)

    units = {}

    in_flight = {}

    def swap_start(group, tag, carry):
        us = [units[n] for n in group]
        lands = [lax.empty(u.shape[1:], BF16) for u in us]
        carry, flight = tied(carry, _split_copies(f"swap_start_{tag}", us, lands, 1, _swap_copies, after=us[0]))
        return (group, tag, flight), carry

    def reduce_start(swap, after):
        group, tag, flight = swap
        us, got = _split_copies(f"swap_wait_{tag}", [fl[2] for fl in flight], [fl[3] for fl in flight], 1,
                                _swap_copies, flight=flight, after=after)
        sums = [_add_pair(u, g_, core, name=f"pair_{n}") for n, u, g_ in zip(group, us, got)]
        after, flight = tied(after, _scatter_start(sums, name=f"scatter_start_{tag}"))
        in_flight.update(zip(group, flight))
        return after

    def ffn_bwd(dh, h, saved, l, early):
        a, c, n, yff = saved
        units[f"f_w_down{l}"] = _mm_tn(yff, dh, rows_are_shards=True, name=f"ffn{l}_down_dw")
        dh_in = dh
        if early:
            sw, dh_in = swap_start([f"f_w_down{l}"], f"fd{l}", dh)
        dyff = _mm(dh_in, full[f"f_w_down{l}"], layer=0, trans_w=True, out_dtype=BF16, name=f"ffn{l}_down_dx")
        if early:
            dyff = reduce_start(sw, dyff)
        da, dcw, dcb = _conv_bwd(a, c, dyff, conv_w2[l], S)
        units[f"f_w_in{l}"] = _mm_tn(n, da, split_y=True, name=f"ffn{l}_in_dw")
        sw, da = swap_start([f"f_w_in{l}"] if early else [f"f_w_down{l}", f"f_w_in{l}"], f"f{l}", da)
        dh, dg = _mm(da, full[f"f_w_in{l}"], layer=0, trans_w=True, split_x=True, bwd=(h, f_norm_g[l], dh), tm=256,
                     name=f"ffn{l}_in_dx")
        return reduce_start(sw, dh), dg, dcw, dcb

    dh, dg_f1, dcw1, dcb1 = ffn_bwd(dh, h3, saved1, 1, False)
    do = _mm(dh, full["b_w_o"], layer=0, trans_w=True, out_dtype=BF16, name="attn_out_dx")
    units["b_w_o"] = _mm_tn(o, dh, rows_are_shards=True, name="b_w_o_dw")
    dq, dkv, dbias = _attn_bwd(q3, kv4, bias, do.reshape(B, S, HD), B, S)
    dq, d_rel = lax.optimization_barrier((dq, _bias_reduce(dbias, n_rel)))
    d_rel = d_rel.reshape(1, H, n_rel)
    dq, dkv = dq.reshape(T, HD), dkv.reshape(2, T, HD)
    units["b_w_q"] = _mm_tn(n_q, dq, rows_are_shards=True, name="b_w_q_dw")
    dh, dg_b = _mm(dq, full["b_w_q"], layer=0, trans_w=True, bwd=(h2, b_norm_g[0], dh), name="q_dx")
    units["w_kv"] = _mm_tn(n_kv, dkv, split_y=True, name="w_kv_dw")
    sw, dkv = swap_start(["b_w_o", "b_w_q", "w_kv"], "b", dkv)
    dh, dg_kv = _mm(dkv, full["w_kv"], layer=0, trans_w=True, split_x=True, bwd=(h2, kv_norm_g, dh), name="kv_dx")
    dh = reduce_start(sw, dh)
    dh, dg_f0, dcw0, dcb0 = ffn_bwd(dh, h1, saved0, 0, True)
    units["a_w_out"] = _mm_tn(out_a, dh, rows_are_shards=True, name="a_w_out_dw")
    sw, dh_in = swap_start(["a_w_out"], "ao", dh)
    d_out = _mm(dh_in, full["a_w_out"], layer=0, trans_w=True, out_dtype=BF16, name="a_out_dx")
    d_out = reduce_start(sw, d_out)
    dzp, dws, dbs, dgv = _gate_bwd(zp, d_out, a_vg, ws_a, bs_tile)
    units["a_w_in"] = _mm_tn(n_a, dzp, name="a_w_in_dw")
    sw, dzp_in = swap_start(["a_w_in"], "ai", dzp)
    grad_x, dg_a = _mm(dzp_in, full["a_w_in"], layer=0, trans_w=True, bwd=(h0, a_g[0], dh), name="a_in_dx")
    grad_x = reduce_start(sw, grad_x)

    to_flat = lambda d: d.transpose(1, 0, 2).reshape(3, F2)
    small = {"a_norm_g": dg_a, "a_v_norm_g": dgv, "a_w_s": dws[None], "a_b_s": dbs[None], "kv_norm_g": dg_kv[0],
             "b_norm_g": dg_b, "b_rel_bias": d_rel, "f_norm_g": jnp.concatenate([dg_f0, dg_f1], axis=0),
             "f_conv_w": jnp.stack([to_flat(dcw0), to_flat(dcw1)]),
             "f_conv_b": jnp.stack([dcb0.reshape(F2), dcb1.reshape(F2)]), "final_norm_g": dg_final[0]}
    snames = list(small)
    small_vec = _pack([small[n] for n in snames] + [loss8[0:1, 0:1]])
    grad_x, small_flight = tied(grad_x, _split_copies("small_start", [small_vec],
                                                      [lax.empty((8,) + small_vec.shape, F32)], 7, _gather8_copies,
                                                      after=small_vec))

    sums, recv = _scatter_wait([in_flight[n] for n in g_names], grad_x)
    sums, recv = dict(zip(g_names, sums)), dict(zip(g_names, recv))
    halves = []
    for n, w in zip(names, ws):
        if w.L == 1:
            halves.append(_sum_chips(w, sums[n], recv[n], pos, name=f"chips_{n}"))
        else:
            first = _sum_chips(w, sums[n + "0"], recv[n + "0"], pos, name=f"chips_{n}0")
            halves.append(_sum_chips(w, sums[n + "1"], recv[n + "1"], pos, layer=1, into=first, name=f"chips_{n}1"))
    g_big = dict(zip(names, _join_halves(ws, halves)))
    g_big["w_kv"] = g_big["w_kv"][0]

    given = dict(a_norm_g=(a_norm_g, m_a_norm_g, v_a_norm_g), a_w_in=(a_w_in, m_a_w_in, v_a_w_in),
                 a_v_norm_g=(a_v_norm_g, m_a_v_norm_g, v_a_v_norm_g), a_w_s=(a_w_s, m_a_w_s, v_a_w_s),
                 a_b_s=(a_b_s, m_a_b_s, v_a_b_s), a_w_out=(a_w_out, m_a_w_out, v_a_w_out),
                 kv_norm_g=(kv_norm_g, m_kv_norm_g, v_kv_norm_g), w_kv=(w_kv, m_w_kv, v_w_kv),
                 b_norm_g=(b_norm_g, m_b_norm_g, v_b_norm_g), b_w_q=(b_w_q, m_b_w_q, v_b_w_q),
                 b_rel_bias=(b_rel_bias, m_b_rel_bias, v_b_rel_bias), b_w_o=(b_w_o, m_b_w_o, v_b_w_o),
                 f_norm_g=(f_norm_g, m_f_norm_g, v_f_norm_g), f_w_in=(f_w_in, m_f_w_in, v_f_w_in),
                 f_conv_w=(f_conv_w, m_f_conv_w, v_f_conv_w), f_conv_b=(f_conv_b, m_f_conv_b, v_f_conv_b),
                 f_w_down=(f_w_down, m_f_w_down, v_f_w_down), final_norm_g=(final_norm_g, m_final_norm_g, v_final_norm_g))
    order = list(given)
    grads, deltas, new_m, new_v = {}, {}, {}, {}
    for n in names:
        w_, m_, v_ = given[n]
        g_ = g_big[n]
        C = w_.shape[-1]
        d2, m2, v2 = _adamw(w_.reshape(-1, C), g_.reshape(-1, C), m_.reshape(-1, C), v_.reshape(-1, C),
                            name=f"adamw_{n}")
        grads[n], deltas[n], new_m[n], new_v[n] = g_.reshape(w_.shape), d2.reshape(w_.shape), m2.reshape(w_.shape), \
            v2.reshape(w_.shape)
    vecs, lands = _split_copies("small_wait", [small_flight[0][2]], [small_flight[0][3]], 7, _gather8_copies,
                                flight=small_flight, after=deltas[names[-1]])
    red = _sum8(lands[0], vecs[0], (4 * xi + 2 * yi + ci).astype(jnp.int32))
    parts = _unpack(red, [small[n].shape for n in snames] + [(1,)])
    g_small = dict(zip(snames, parts[:-1]))
    loss = parts[-1][0]
    g_small["a_norm_g"] = lax.dynamic_slice_in_dim(g_small["a_norm_g"], j_me * nsd, nsd, axis=1)
    g_small["a_v_norm_g"] = lax.dynamic_slice_in_dim(g_small["a_v_norm_g"], j_me * nsg, nsg, axis=1)
    g_small["f_conv_w"] = lax.dynamic_slice_in_dim(g_small["f_conv_w"], j_me * nsf, nsf, axis=2)

    sm = [n for n in order if n not in names]
    d2, m2, v2 = _adamw(_pack([given[n][0] for n in sm]), _pack([g_small[n].reshape(given[n][0].shape) for n in sm]),
                        _pack([given[n][1] for n in sm]), _pack([given[n][2] for n in sm]), name="adamw_small")
    shapes = [given[n][0].shape for n in sm]
    for n, d_, m_, v_ in zip(sm, _unpack(d2, shapes), _unpack(m2, shapes), _unpack(v2, shapes)):
        grads[n], deltas[n], new_m[n], new_v[n] = g_small[n].reshape(given[n][0].shape), d_, m_, v_

    return (loss, grad_x.reshape(B, S, D), *[grads[n] for n in order], *[deltas[n] for n in order],
            *[new_m[n] for n in order], *[new_v[n] for n in order])
```

```python
import math

import numpy as np
import jax
import jax.numpy as jnp
from jax import lax
from jax.experimental import pallas as pl
from jax.experimental.pallas import tpu as pltpu

F32 = jnp.float32
BF16 = jnp.bfloat16
MESH = pl.DeviceIdType.MESH

EPS = 1e-6
NEG_INF = -1e30
CHUNK = 64
GMLP_BLOCK = 128
GROUP_DIM = 128
HEAD_DIM = 64
LEFT_CHUNKS = 8
PAD = LEFT_CHUNKS * CHUNK
REL_CLIP = 128
Q_BLOCK = 256
K_SPAN = PAD + Q_BLOCK
F_LEN = K_SPAN + Q_BLOCK
HEADS_PER_STEP = 4
N_CHIPS = 4

ADAM_LR = 0.001
ADAM_B1 = 0.9
ADAM_B2 = 0.999
ADAM_EPS = 1e-08
ADAM_WD = 0.01
ADAM_STEP = 10

VMEM_LIMIT = 56 * 1024 * 1024


def _params(sem=None, **kw):
    if sem is not None:
        kw["dimension_semantics"] = sem
    return pltpu.CompilerParams(vmem_limit_bytes=VMEM_LIMIT, **kw)


def _rms(xf):
    r = lax.rsqrt(jnp.mean(xf * xf, axis=-1, keepdims=True) + EPS)
    return xf * r, r


def _gelu(x):
    c = math.sqrt(2.0 / math.pi)
    return 0.5 * x * (1.0 + jnp.tanh(c * (x + 0.044715 * x * x * x)))


def _gelu_grad(x):
    c = math.sqrt(2.0 / math.pi)
    t = jnp.tanh(c * (x + 0.044715 * x * x * x))
    return 0.5 * (1.0 + t) + 0.5 * x * (1.0 - t * t) * c * (1.0 + 3.0 * 0.044715 * x * x)


def _col_tile(n):
    if n <= 1024:
        return n
    for t in (1408, 1024, 512):
        if n % t == 0:
            return t
    raise ValueError(n)


def _row_tile(t, want):
    while t % want:
        want //= 2
    return want


def _mm(x, w, *, name, layer=None, trans_w=False, norm_g=None, res=None, scale=None, out_dtype=F32, bwd=None,
        split_out=False, split_x=False, emit_norm=False, loss=None, tm=512):
    T = x.shape[-2]
    K = 2 * x.shape[-1] if split_x else x.shape[-1]
    N = w.shape[-2] if trans_w else w.shape[-1]
    tn = N
    tm = _row_tile(T, 256 if N > 4096 else tm)
    nn, nm = N // tn, T // tm
    has_norm, has_res, has_bwd, has_loss = norm_g is not None, res is not None, bwd is not None, loss is not None
    dims = (((1,), (1,)), ((), ())) if trans_w else (((1,), (0,)), ((), ()))

    def body(*refs):
        it = iter(refs)
        x_ref, w_ref = next(it), next(it)
        g_ref = next(it) if has_norm else None
        res_ref = next(it) if has_res else None
        if has_bwd:
            h_ref, bg_ref, dh_ref = next(it), next(it), next(it)
        if has_loss:
            lg_ref, t_ref = next(it), next(it)
        o_ref = next(it)
        if split_x:
            kh = K // 2
            acc = lax.dot_general(x_ref[0].astype(BF16), w_ref[:, :kh] if trans_w else w_ref[:kh, :], dims,
                                  preferred_element_type=F32)
            acc = acc + lax.dot_general(x_ref[1].astype(BF16), w_ref[:, kh:] if trans_w else w_ref[kh:, :], dims,
                                        preferred_element_type=F32)
        else:
            xv = x_ref[...]
            if has_norm:
                xv = _rms(xv.astype(F32))[0] * g_ref[...]
            xb = xv.astype(BF16)
            if emit_norm:
                refs[-1][...] = xb
            acc = lax.dot_general(xb, w_ref[...], dims, preferred_element_type=F32)
        if scale is not None:
            acc = acc * scale
        if has_res:
            acc = acc + res_ref[...]
        if has_bwd:
            dg_ref = next(it)
            n, r = _rms(h_ref[...])

            @pl.when(pl.program_id(1) == 0)
            def _():
                dg_ref[...] = jnp.zeros_like(dg_ref)

            dg_ref[...] += jnp.sum(acc * n, axis=0, keepdims=True)
            t = acc * bg_ref[...]
            o_ref[...] = dh_ref[...] + r * (t - n * jnp.mean(t * n, axis=-1, keepdims=True))
        elif has_loss:
            loss_ref, dg_ref = refs[-2], refs[-1]

            @pl.when(pl.program_id(1) == 0)
            def _():
                loss_ref[...] = jnp.zeros_like(loss_ref)
                dg_ref[...] = jnp.zeros_like(dg_ref)

            n, r = _rms(acc)
            g = lg_ref[...]
            e = n * g - t_ref[...]
            loss_ref[...] += 0.5 * jnp.sum(jnp.mean(e * e, axis=-1, keepdims=True), axis=0, keepdims=True)
            dy = e * (1.0 / N)
            dg_ref[...] += jnp.sum(dy * n, axis=0, keepdims=True)
            t = dy * g
            o_ref[...] = r * (t - n * jnp.mean(t * n, axis=-1, keepdims=True))
        elif split_out:
            o_ref[0] = acc[:, :N // 2].astype(out_dtype)
            o_ref[1] = acc[:, N // 2:].astype(out_dtype)
        else:
            o_ref[...] = acc.astype(out_dtype)

    lead = () if layer is None else (None,)
    lidx = () if layer is None else (layer,)
    ins = [x, w]
    xspec = (pl.BlockSpec((2, tm, K // 2), lambda n, m: (0, m, 0)) if split_x
             else pl.BlockSpec((tm, K), lambda n, m: (m, 0)))
    wspec = (pl.BlockSpec(lead + (tn, K), lambda n, m: lidx + (n, 0)) if trans_w
             else pl.BlockSpec(lead + (K, tn), lambda n, m: lidx + (0, n)))
    in_specs = [xspec, wspec]
    if has_norm:
        ins.append(norm_g.reshape(1, K))
        in_specs.append(pl.BlockSpec((1, K), lambda n, m: (0, 0)))
    if has_res:
        ins.append(res)
        in_specs.append(pl.BlockSpec((tm, tn), lambda n, m: (m, n)))
    if split_out:
        out_shape = [jax.ShapeDtypeStruct((2, T, N // 2), out_dtype)]
        out_specs = [pl.BlockSpec((2, tm, N // 2), lambda n, m: (0, m, 0))]
    else:
        out_shape = [jax.ShapeDtypeStruct((T, N), F32 if has_bwd else out_dtype)]
        out_specs = [pl.BlockSpec((tm, tn), lambda n, m: (m, n))]
    if has_bwd:
        h, g, dh = bwd
        ins += [h, g.reshape(1, N), dh]
        in_specs += [pl.BlockSpec((tm, N), lambda n, m: (m, 0)), pl.BlockSpec((1, N), lambda n, m: (0, 0)),
                     pl.BlockSpec((tm, N), lambda n, m: (m, 0))]
        out_shape.append(jax.ShapeDtypeStruct((1, N), F32))
        out_specs.append(pl.BlockSpec((1, N), lambda n, m: (0, 0)))
    if emit_norm:
        out_shape.append(jax.ShapeDtypeStruct((T, K), BF16))
        out_specs.append(pl.BlockSpec((tm, K), lambda n, m: (m, 0)))
    if has_loss:
        ins += [loss[0].reshape(1, N), loss[1]]
        in_specs += [pl.BlockSpec((1, N), lambda n, m: (0, 0)), pl.BlockSpec((tm, N), lambda n, m: (m, 0))]
        out_shape += [jax.ShapeDtypeStruct((8, 128), F32), jax.ShapeDtypeStruct((1, N), F32)]
        out_specs += [pl.BlockSpec((8, 128), lambda n, m: (0, 0)), pl.BlockSpec((1, N), lambda n, m: (0, 0))]
    out = pl.pallas_call(body, name=name, grid=(nn, nm), in_specs=in_specs, out_specs=out_specs, out_shape=out_shape,
                         compiler_params=_params(("arbitrary", "arbitrary")))(*ins)
    return out if has_bwd or emit_norm or has_loss else out[0]


def _mm_tn(x, dy, *, name, rows_are_shards=False, split_y=False, tt=512):
    T, K = x.shape
    N = 2 * dy.shape[-1] if split_y else dy.shape[-1]
    R, C = (K // N_CHIPS, N // 2) if rows_are_shards else (K // 2, N // N_CHIPS)
    nn = 2 if split_y else 1
    tn = N // nn
    per = N_CHIPS // nn
    assert not (rows_are_shards and split_y)
    tt = _row_tile(T, tt)
    nt = T // tt

    def body(x_ref, y_ref, o_ref, acc_ref):
        t = pl.program_id(1)

        @pl.when(t == 0)
        def _():
            acc_ref[...] = jnp.zeros_like(acc_ref)

        acc_ref[...] += lax.dot_general(x_ref[...], y_ref[...].astype(BF16), (((0,), (0,)), ((), ())),
                                        preferred_element_type=F32)

        @pl.when(t == nt - 1)
        def _():
            if rows_are_shards:
                for h in range(2):
                    o_ref[h] = acc_ref[:, h * C:(h + 1) * C].astype(BF16).reshape(N_CHIPS, R, C)
            else:
                for j in range(per):
                    o_ref[:, j] = acc_ref[:, j * C:(j + 1) * C].astype(BF16).reshape(2, R, C)

    if split_y:
        yspec = pl.BlockSpec((None, tt, tn), lambda n, t: (n, t, 0))
    else:
        yspec = pl.BlockSpec((tt, tn), lambda n, t: (t, 0))
    if rows_are_shards:
        out_spec = pl.BlockSpec((2, N_CHIPS, R, C), lambda n, t: (0, 0, 0, 0))
    else:
        out_spec = pl.BlockSpec((2, per, R, C), lambda n, t: (0, n, 0, 0))
    return pl.pallas_call(body, name=name, grid=(nn, nt),
                          in_specs=[pl.BlockSpec((tt, K), lambda n, t: (t, 0)), yspec], out_specs=out_spec,
                          out_shape=jax.ShapeDtypeStruct((2, N_CHIPS, R, C), BF16),
                          scratch_shapes=[pltpu.VMEM((K, tn), F32)],
                          compiler_params=_params(("arbitrary", "arbitrary")))(x, dy)


def _chunk_mask():
    i = lax.broadcasted_iota(jnp.int32, (GMLP_BLOCK, GMLP_BLOCK), 0) // CHUNK
    j = lax.broadcasted_iota(jnp.int32, (GMLP_BLOCK, GMLP_BLOCK), 1) // CHUNK
    return i >= j


def _gate_fwd(zp, gv, ws, bs_tile, *, tm=256):
    T, W2 = zp.shape
    W = W2 // 2
    G = W // GROUP_DIM
    tm = _row_tile(T, tm)

    def body(zp_ref, gv_ref, ws_ref, bs_ref, o_ref):
        z = _gelu(zp_ref[...].astype(F32))
        u, v = z[:, :W], z[:, W:]
        vn = _rms(v)[0] * gv_ref[...]
        mask = _chunk_mask()
        for g in range(G):
            cs = slice(g * GROUP_DIM, (g + 1) * GROUP_DIM)
            wg = jnp.where(mask, ws_ref[g], 0.0).astype(BF16)
            for b in range(tm // GMLP_BLOCK):
                rs = slice(b * GMLP_BLOCK, (b + 1) * GMLP_BLOCK)
                s = jnp.dot(wg, vn[rs, cs].astype(BF16), preferred_element_type=F32) + bs_ref[:, cs]
                o_ref[rs, cs] = (u[rs, cs] * s).astype(BF16)

    return pl.pallas_call(
        body, name="gate_fwd", grid=(T // tm,),
        in_specs=[pl.BlockSpec((tm, W2), lambda i: (i, 0)), pl.BlockSpec((1, W), lambda i: (0, 0)),
                  pl.BlockSpec((G, GMLP_BLOCK, GMLP_BLOCK), lambda i: (0, 0, 0)),
                  pl.BlockSpec((GMLP_BLOCK, W), lambda i: (0, 0))],
        out_specs=pl.BlockSpec((tm, W), lambda i: (i, 0)), out_shape=jax.ShapeDtypeStruct((T, W), BF16),
        compiler_params=_params(("arbitrary",)))(zp, gv, ws, bs_tile)


def _gate_bwd(zp, d_out, gv, ws, bs_tile, *, tm=256):
    T, W2 = zp.shape
    W = W2 // 2
    G = W // GROUP_DIM
    tm = _row_tile(T, tm)
    nm = T // tm

    def body(zp_ref, do_ref, gv_ref, ws_ref, bs_ref, dzp_ref, dws_ref, dbs_ref, dgv_ref, du_scr, dvn_scr, dsum_scr):
        i = pl.program_id(0)

        @pl.when(i == 0)
        def _():
            dws_ref[...] = jnp.zeros_like(dws_ref)
            dgv_ref[...] = jnp.zeros_like(dgv_ref)
            dsum_scr[...] = jnp.zeros_like(dsum_scr)

        zp = zp_ref[...].astype(F32)
        z = _gelu(zp)
        u, v = z[:, :W], z[:, W:]
        n, r = _rms(v)
        gv = gv_ref[...]
        vn = n * gv
        d_out = do_ref[...].astype(F32)
        mask = _chunk_mask()
        for g in range(G):
            cs = slice(g * GROUP_DIM, (g + 1) * GROUP_DIM)
            wg = jnp.where(mask, ws_ref[g], 0.0).astype(BF16)
            dw = jnp.zeros((GMLP_BLOCK, GMLP_BLOCK), F32)
            for b in range(tm // GMLP_BLOCK):
                rs = slice(b * GMLP_BLOCK, (b + 1) * GMLP_BLOCK)
                vb = vn[rs, cs].astype(BF16)
                s = jnp.dot(wg, vb, preferred_element_type=F32) + bs_ref[:, cs]
                du_scr[rs, cs] = d_out[rs, cs] * s
                ds = d_out[rs, cs] * u[rs, cs]
                dsb = ds.astype(BF16)
                dvn_scr[rs, cs] = lax.dot_general(wg, dsb, (((0,), (0,)), ((), ())), preferred_element_type=F32)
                dw = dw + lax.dot_general(dsb, vb, (((1,), (1,)), ((), ())), preferred_element_type=F32)
                dsum_scr[:, cs] += ds
            dws_ref[g] += jnp.where(mask, dw, 0.0)
        dvn = dvn_scr[...]
        dgv_ref[...] += jnp.sum(dvn * n, axis=0, keepdims=True)
        t = dvn * gv
        dv = r * (t - n * jnp.mean(t * n, axis=-1, keepdims=True))
        dzp_ref[:, :W] = (du_scr[...] * _gelu_grad(zp[:, :W])).astype(BF16)
        dzp_ref[:, W:] = (dv * _gelu_grad(zp[:, W:])).astype(BF16)

        @pl.when(i == nm - 1)
        def _():
            sel = (lax.broadcasted_iota(jnp.int32, (G, W), 1) // GROUP_DIM
                   == lax.broadcasted_iota(jnp.int32, (G, W), 0)).astype(F32)
            dbs_ref[...] = lax.dot_general(sel, dsum_scr[...], (((1,), (1,)), ((), ())),
                                           precision=lax.Precision.HIGHEST, preferred_element_type=F32)

    return pl.pallas_call(
        body, name="gate_bwd", grid=(nm,),
        in_specs=[pl.BlockSpec((tm, W2), lambda i: (i, 0)), pl.BlockSpec((tm, W), lambda i: (i, 0)),
                  pl.BlockSpec((1, W), lambda i: (0, 0)),
                  pl.BlockSpec((G, GMLP_BLOCK, GMLP_BLOCK), lambda i: (0, 0, 0)),
                  pl.BlockSpec((GMLP_BLOCK, W), lambda i: (0, 0))],
        out_specs=[pl.BlockSpec((tm, W2), lambda i: (i, 0)),
                   pl.BlockSpec((G, GMLP_BLOCK, GMLP_BLOCK), lambda i: (0, 0, 0)),
                   pl.BlockSpec((G, GMLP_BLOCK), lambda i: (0, 0)), pl.BlockSpec((1, W), lambda i: (0, 0))],
        out_shape=[jax.ShapeDtypeStruct((T, W2), BF16), jax.ShapeDtypeStruct((G, GMLP_BLOCK, GMLP_BLOCK), F32),
                   jax.ShapeDtypeStruct((G, GMLP_BLOCK), F32), jax.ShapeDtypeStruct((1, W), F32)],
        scratch_shapes=[pltpu.VMEM((tm, W), F32), pltpu.VMEM((tm, W), F32), pltpu.VMEM((GMLP_BLOCK, W), F32)],
        compiler_params=_params(("arbitrary",)))(zp, d_out, gv, ws, bs_tile)


HALO = 16


def _taps(ext, w, b):
    a, a1, a2 = ext[HALO:], pltpu.roll(ext, 1, 0)[HALO:], pltpu.roll(ext, 2, 0)[HALO:]
    return w[2:3] * a + w[1:2] * a1 + w[0:1] * a2 + b, a, a1, a2


def _conv_fwd(a, cw, cb, S, *, tm=256):
    _, T, F = a.shape
    tc = _col_tile(F)
    tm = _row_tile(S, tm)
    hb = tm // HALO

    def body(a_ref, p_ref, w_ref, b_ref, o_ref, c_ref):
        first = (pl.program_id(1) * tm) % S == 0
        keep = jnp.where(first, 0.0, 1.0)

        def conv(s):
            ext = jnp.concatenate([p_ref[s].astype(F32) * keep, a_ref[s].astype(F32)], axis=0)
            c = _taps(ext, w_ref[s], b_ref[s:s + 1, :])[0].astype(BF16)
            c_ref[s] = c
            return c.astype(F32)

        up, gate = conv(0), conv(1)
        o_ref[...] = (gate * jax.nn.sigmoid(gate) * up).astype(BF16)

    return pl.pallas_call(
        body, name="conv_fwd", grid=(F // tc, T // tm),
        in_specs=[pl.BlockSpec((2, tm, tc), lambda j, i: (0, i, j)),
                  pl.BlockSpec((2, HALO, tc), lambda j, i: (0, jnp.maximum(i * hb - 1, 0), j)),
                  pl.BlockSpec((2, 3, tc), lambda j, i: (0, 0, j)), pl.BlockSpec((2, tc), lambda j, i: (0, j))],
        out_specs=[pl.BlockSpec((tm, tc), lambda j, i: (i, j)), pl.BlockSpec((2, tm, tc), lambda j, i: (0, i, j))],
        out_shape=[jax.ShapeDtypeStruct((T, F), BF16), jax.ShapeDtypeStruct((2, T, F), BF16)],
        compiler_params=_params(("arbitrary", "arbitrary")))(a, a, cw, cb)


def _ffn_in_conv(h, w, g, cw, cb, S, *, name, tm=256):
    T, D = h.shape
    F = w.shape[-1] // 2
    tc = _col_tile(F)
    tm = _row_tile(S, tm)

    def body(h_ref, w_ref, g_ref, cw_ref, cb_ref, y_ref, a_ref, c_ref, n_ref, tail):
        first = (pl.program_id(0) * tm) % S == 0
        nb = (_rms(h_ref[...])[0] * g_ref[...]).astype(BF16)
        n_ref[...] = nb
        for j in range(F // tc):
            cs = slice(j * tc, (j + 1) * tc)
            conv = []
            for s in range(2):
                acc = jnp.dot(nb, w_ref[:, s * F + j * tc:s * F + (j + 1) * tc], preferred_element_type=F32)
                ab = acc.astype(BF16)
                a_ref[s, :, cs] = ab
                af = ab.astype(F32)
                ext = jnp.concatenate([jnp.where(first, 0.0, tail[s, :, cs]), af], axis=0)
                tail[s, :, cs] = af[tm - HALO:, :]
                cv = _taps(ext, cw_ref[s, :, cs], cb_ref[s:s + 1, cs])[0].astype(BF16)
                c_ref[s, :, cs] = cv
                conv.append(cv.astype(F32))
            up, gate = conv
            y_ref[:, cs] = (gate * jax.nn.sigmoid(gate) * up).astype(BF16)

    row = lambda width: pl.BlockSpec((tm, width), lambda i: (i, 0))
    wide = pl.BlockSpec((2, tm, F), lambda i: (0, i, 0))
    return pl.pallas_call(
        body, name=name, grid=(T // tm,),
        in_specs=[row(D), pl.BlockSpec((None, D, 2 * F), lambda i: (0, 0, 0)), pl.BlockSpec((1, D), lambda i: (0, 0)),
                  pl.BlockSpec((2, 3, F), lambda i: (0, 0, 0)), pl.BlockSpec((2, F), lambda i: (0, 0))],
        out_specs=[row(F), wide, wide, row(D)],
        out_shape=[jax.ShapeDtypeStruct((T, F), BF16), jax.ShapeDtypeStruct((2, T, F), BF16),
                   jax.ShapeDtypeStruct((2, T, F), BF16), jax.ShapeDtypeStruct((T, D), BF16)],
        scratch_shapes=[pltpu.VMEM((2, HALO, F), F32)],
        compiler_params=_params(("arbitrary",)))(h, w, g.reshape(1, D), cw, cb)


def _conv_bwd(a, c, dy, cw, S, *, tm=256):
    _, T, F = a.shape
    tc = _col_tile(F)
    tm = _row_tile(S, tm)
    nm = T // tm
    hb = tm // HALO
    TE = tm + HALO
    nxt = lambda j, i: jnp.minimum((i + 1) * hb, T // HALO - 1)

    def body(a_ref, c_ref, nc_ref, dy_ref, ndy_ref, w_ref, da_ref, dw_ref, db_ref):
        i = pl.program_id(1)
        last = ((i + 1) * tm) % S == 0
        keep_n = jnp.where(last, 0.0, 1.0)
        dyf = jnp.concatenate([dy_ref[...].astype(F32), ndy_ref[...].astype(F32) * keep_n], axis=0)
        up = jnp.concatenate([c_ref[0].astype(F32), nc_ref[0].astype(F32)], axis=0)
        gate = jnp.concatenate([c_ref[1].astype(F32), nc_ref[1].astype(F32)], axis=0)
        sg = jax.nn.sigmoid(gate)
        d_up = dyf * (gate * sg)
        d_gate = dyf * up * (sg * (1.0 + gate * (1.0 - sg)))

        @pl.when(i == 0)
        def _():
            dw_ref[...] = jnp.zeros_like(dw_ref)
            db_ref[...] = jnp.zeros_like(db_ref)

        def back(s, d):
            a = a_ref[s].astype(F32)
            w = w_ref[s]
            u1, u2 = pltpu.roll(d, TE - 1, 0), pltpu.roll(d, TE - 2, 0)
            db_ref[s:s + 1, :] += jnp.sum(d[:tm], axis=0, keepdims=True)
            dw_ref[s, 2:3, :] += jnp.sum(d[:tm] * a, axis=0, keepdims=True)
            dw_ref[s, 1:2, :] += jnp.sum(u1[:tm] * a, axis=0, keepdims=True)
            dw_ref[s, 0:1, :] += jnp.sum(u2[:tm] * a, axis=0, keepdims=True)
            da_ref[s] = (w[2:3] * d + w[1:2] * u1 + w[0:1] * u2)[:tm].astype(BF16)

        back(0, d_up)
        back(1, d_gate)

    cur = pl.BlockSpec((2, tm, tc), lambda j, i: (0, i, j))
    return pl.pallas_call(
        body, name="conv_bwd", grid=(F // tc, nm),
        in_specs=[cur, cur, pl.BlockSpec((2, HALO, tc), lambda j, i: (0, nxt(j, i), j)),
                  pl.BlockSpec((tm, tc), lambda j, i: (i, j)), pl.BlockSpec((HALO, tc), lambda j, i: (nxt(j, i), j)),
                  pl.BlockSpec((2, 3, tc), lambda j, i: (0, 0, j))],
        out_specs=[cur, pl.BlockSpec((2, 3, tc), lambda j, i: (0, 0, j)), pl.BlockSpec((2, tc), lambda j, i: (0, j))],
        out_shape=[jax.ShapeDtypeStruct((2, T, F), BF16), jax.ShapeDtypeStruct((2, 3, F), F32),
                   jax.ShapeDtypeStruct((2, F), F32)],
        compiler_params=_params(("arbitrary", "arbitrary")))(a, c, c, dy, dy, cw)


def _bias_index():
    idx = np.arange(F_LEN)
    d = np.where(idx < K_SPAN, idx, idx - F_LEN)
    return np.clip(PAD - d, -REL_CLIP, REL_CLIP) + REL_CLIP


ROW_GROUP = 16


def _roll_rows(x, sign, unit, steps):
    rows = lax.broadcasted_iota(jnp.int32, x.shape, 0)
    step = 1
    while step < steps:
        shift = unit * step if sign > 0 else F_LEN - unit * step
        x = jnp.where((rows & step) != 0, pltpu.roll(x, shift, 1), x)
        step *= 2
    return x


def _bias_expand(frow):
    H = frow.shape[0]
    groups = Q_BLOCK // ROW_GROUP

    def body(f_ref, o_ref):
        coarse = _roll_rows(jnp.broadcast_to(f_ref[...], (groups, F_LEN)), 1, ROW_GROUP, groups)
        x = jnp.concatenate([jnp.broadcast_to(coarse[a:a + 1], (ROW_GROUP, F_LEN)) for a in range(groups)], axis=0)
        x = _roll_rows(x, 1, 1, ROW_GROUP)[:, :K_SPAN]
        qc = lax.broadcasted_iota(jnp.int32, (Q_BLOCK, K_SPAN), 0) // CHUNK * CHUNK
        kj = lax.broadcasted_iota(jnp.int32, (Q_BLOCK, K_SPAN), 1)
        o_ref[...] = jnp.where((kj >= qc) & (kj < qc + PAD + CHUNK), x, NEG_INF)

    return pl.pallas_call(
        body, name="bias_expand", grid=(H,),
        in_specs=[pl.BlockSpec((None, 1, F_LEN), lambda h: (h, 0, 0))],
        out_specs=pl.BlockSpec((None, Q_BLOCK, K_SPAN), lambda h: (h, 0, 0)),
        out_shape=jax.ShapeDtypeStruct((H, Q_BLOCK, K_SPAN), F32), compiler_params=_params(("arbitrary",)))(frow)


def _bias_reduce(dbias, n_rel):
    H = dbias.shape[0]
    onehot = jnp.asarray((_bias_index()[:, None] == np.arange(n_rel)[None, :]).astype(np.float32), dtype=BF16)

    def body(d_ref, oh_ref, o_ref):
        x = jnp.concatenate([d_ref[...], jnp.zeros((Q_BLOCK, F_LEN - K_SPAN), F32)], axis=1)
        fine = _roll_rows(x, -1, 1, ROW_GROUP).reshape(Q_BLOCK // ROW_GROUP, ROW_GROUP, F_LEN)
        coarse = _roll_rows(jnp.sum(fine, axis=1), -1, ROW_GROUP, Q_BLOCK // ROW_GROUP)
        row = jnp.broadcast_to(jnp.sum(coarse, axis=0, keepdims=True), (8, F_LEN))
        acc = jnp.zeros((8, n_rel), F32)
        for _ in range(3):
            piece = row.astype(BF16)
            acc = acc + jnp.dot(piece, oh_ref[...], preferred_element_type=F32)
            row = row - piece.astype(F32)
        o_ref[...] = acc[0:1]

    return pl.pallas_call(
        body, name="bias_reduce", grid=(H,),
        in_specs=[pl.BlockSpec((None, Q_BLOCK, K_SPAN), lambda h: (h, 0, 0)),
                  pl.BlockSpec((F_LEN, n_rel), lambda h: (0, 0))],
        out_specs=pl.BlockSpec((None, 1, n_rel), lambda h: (h, 0, 0)),
        out_shape=jax.ShapeDtypeStruct((H, 1, n_rel), F32), compiler_params=_params(("arbitrary",)))(dbias, onehot)


def _attn_specs(S):
    hw = HEADS_PER_STEP * HEAD_DIM
    qspec = pl.BlockSpec((None, Q_BLOCK, hw), lambda g, b, i: (b, i, g))
    kspec = pl.BlockSpec((None, None, S, hw), lambda g, b, i: (0, b, 0, g))
    vspec = pl.BlockSpec((None, None, S, hw), lambda g, b, i: (1, b, 0, g))
    bspec = pl.BlockSpec((HEADS_PER_STEP, Q_BLOCK, K_SPAN), lambda g, b, i: (g, 0, 0))
    return hw, qspec, kspec, vspec, bspec


def _load_padded(k_ref, v_ref, kp, vp):
    kp[:PAD, :] = jnp.zeros((PAD, kp.shape[1]), BF16)
    vp[:PAD, :] = jnp.zeros((PAD, vp.shape[1]), BF16)
    kp[PAD:, :] = k_ref[...]
    vp[PAD:, :] = v_ref[...]


def _attn_exp(q_ref, kp, b_ref, h, q0, before):
    hs = slice(h * HEAD_DIM, (h + 1) * HEAD_DIM)
    kh = kp[pl.ds(q0, K_SPAN), hs]
    s = lax.dot_general(q_ref[:, hs], kh, (((1,), (1,)), ((), ())), preferred_element_type=F32) + b_ref[h] + before
    p = jnp.exp(s - jnp.max(s, axis=-1, keepdims=True))
    return p, 1.0 / jnp.sum(p, axis=-1, keepdims=True), kh


def _before_start(q0):
    kj = lax.broadcasted_iota(jnp.int32, (1, K_SPAN), 1)
    return jnp.where(q0 + kj >= PAD, 0.0, NEG_INF)


def _attn_fwd(q, kv, bias, B, S):
    HD = q.shape[-1]
    hw, qspec, kspec, vspec, bspec = _attn_specs(S)

    def body(q_ref, k_ref, v_ref, b_ref, o_ref, kp, vp):
        i = pl.program_id(2)

        @pl.when(i == 0)
        def _():
            _load_padded(k_ref, v_ref, kp, vp)

        q0 = pl.multiple_of(i * Q_BLOCK, Q_BLOCK)
        before = _before_start(q0)
        outs = []
        for h in range(HEADS_PER_STEP):
            hs = slice(h * HEAD_DIM, (h + 1) * HEAD_DIM)
            p, inv, _ = _attn_exp(q_ref, kp, b_ref, h, q0, before)
            outs.append(jnp.dot(p.astype(BF16), vp[pl.ds(q0, K_SPAN), hs], preferred_element_type=F32) * inv)
        o_ref[...] = jnp.concatenate(outs, axis=1).astype(BF16)

    return pl.pallas_call(
        body, name="attn_fwd", grid=(HD // hw, B, S // Q_BLOCK), in_specs=[qspec, kspec, vspec, bspec],
        out_specs=qspec, out_shape=jax.ShapeDtypeStruct((B, S, HD), BF16),
        scratch_shapes=[pltpu.VMEM((S + PAD, hw), BF16), pltpu.VMEM((S + PAD, hw), BF16)],
        compiler_params=_params(("arbitrary", "arbitrary", "arbitrary")))(q, kv, kv, bias)


def _attn_bwd(q, kv, bias, do, B, S):
    HD = q.shape[-1]
    H = HD // HEAD_DIM
    hw, qspec, kspec, vspec, bspec = _attn_specs(S)
    scale = HEAD_DIM ** -0.5
    nq = S // Q_BLOCK

    def body(q_ref, k_ref, v_ref, b_ref, do_ref, dq_ref, dkv_ref, db_ref, kp, vp, dk_acc, dv_acc):
        b, i = pl.program_id(1), pl.program_id(2)
        q0 = pl.multiple_of(i * Q_BLOCK, Q_BLOCK)

        @pl.when(i == 0)
        def _():
            _load_padded(k_ref, v_ref, kp, vp)
            dk_acc[...] = jnp.zeros_like(dk_acc)
            dv_acc[...] = jnp.zeros_like(dv_acc)

        @pl.when((i == 0) & (b == 0))
        def _():
            db_ref[...] = jnp.zeros_like(db_ref)

        before = _before_start(q0)
        for h in range(HEADS_PER_STEP):
            hs = slice(h * HEAD_DIM, (h + 1) * HEAD_DIM)
            p, inv, kh = _attn_exp(q_ref, kp, b_ref, h, q0, before)
            p = p * inv
            doh = do_ref[:, hs]
            dp = lax.dot_general(doh, vp[pl.ds(q0, K_SPAN), hs], (((1,), (1,)), ((), ())),
                                 preferred_element_type=F32)
            ds = p * (dp - jnp.sum(p * dp, axis=-1, keepdims=True))
            db_ref[h] += ds
            dsb = ds.astype(BF16)
            dq_ref[:, hs] = (jnp.dot(dsb, kh, preferred_element_type=F32) * scale).astype(BF16)
            dk_acc[pl.ds(q0, K_SPAN), hs] += lax.dot_general(dsb, q_ref[:, hs], (((0,), (0,)), ((), ())),
                                                              preferred_element_type=F32)
            dv_acc[pl.ds(q0, K_SPAN), hs] += lax.dot_general(p.astype(BF16), doh, (((0,), (0,)), ((), ())),
                                                              preferred_element_type=F32)

        @pl.when(i == nq - 1)
        def _():
            dkv_ref[0] = dk_acc[PAD:, :].astype(BF16)
            dkv_ref[1] = dv_acc[PAD:, :].astype(BF16)

    return pl.pallas_call(
        body, name="attn_bwd", grid=(HD // hw, B, nq), in_specs=[qspec, kspec, vspec, bspec, qspec],
        out_specs=[qspec, pl.BlockSpec((2, None, S, hw), lambda g, b, i: (0, b, 0, g)), bspec],
        out_shape=[jax.ShapeDtypeStruct((B, S, HD), BF16), jax.ShapeDtypeStruct((2, B, S, HD), BF16),
                   jax.ShapeDtypeStruct((H, Q_BLOCK, K_SPAN), F32)],
        scratch_shapes=[pltpu.VMEM((S + PAD, hw), BF16), pltpu.VMEM((S + PAD, hw), BF16),
                        pltpu.VMEM((S + PAD, hw), F32), pltpu.VMEM((S + PAD, hw), F32)],
        compiler_params=_params(("arbitrary", "arbitrary", "arbitrary")))(q, kv, kv, bias, do)


def _loss_head(h, g, target, *, tm=512):
    T, D = h.shape
    tm = _row_tile(T, tm)

    def body(h_ref, g_ref, t_ref, dh_ref, loss_ref, dg_ref):
        @pl.when(pl.program_id(0) == 0)
        def _():
            loss_ref[...] = jnp.zeros_like(loss_ref)
            dg_ref[...] = jnp.zeros_like(dg_ref)

        n, r = _rms(h_ref[...])
        g = g_ref[...]
        e = n * g - t_ref[...]
        loss_ref[...] += 0.5 * jnp.sum(jnp.mean(e * e, axis=-1, keepdims=True), axis=0, keepdims=True)
        dy = e * (1.0 / D)
        dg_ref[...] += jnp.sum(dy * n, axis=0, keepdims=True)
        t = dy * g
        dh_ref[...] = r * (t - n * jnp.mean(t * n, axis=-1, keepdims=True))

    row = pl.BlockSpec((tm, D), lambda i: (i, 0))
    return pl.pallas_call(
        body, name="loss_head", grid=(T // tm,), in_specs=[row, pl.BlockSpec((1, D), lambda i: (0, 0)), row],
        out_specs=[row, pl.BlockSpec((8, 128), lambda i: (0, 0)), pl.BlockSpec((1, D), lambda i: (0, 0))],
        out_shape=[jax.ShapeDtypeStruct((T, D), F32), jax.ShapeDtypeStruct((8, 128), F32),
                   jax.ShapeDtypeStruct((1, D), F32)],
        compiler_params=_params(("arbitrary",)))(h, g.reshape(1, D), target)


def _sub_rows(R):
    for cand in (256, 352, 128, 64, 8):
        if R % cand == 0 and R > cand:
            return cand
    return R


def _adamw(w, g, m, v, *, name):
    R, C = w.shape
    tr = _sub_rows(R)

    def body(w_ref, g_ref, m_ref, v_ref, d_ref, nm_ref, nv_ref):
        g = g_ref[...]
        m = ADAM_B1 * m_ref[...] + (1.0 - ADAM_B1) * g
        v = ADAM_B2 * v_ref[...] + (1.0 - ADAM_B2) * (g * g)
        m_hat = m / (1.0 - ADAM_B1 ** ADAM_STEP)
        v_hat = v / (1.0 - ADAM_B2 ** ADAM_STEP)
        d_ref[...] = -ADAM_LR * (m_hat / (jnp.sqrt(v_hat) + ADAM_EPS) + ADAM_WD * w_ref[...])
        nm_ref[...] = m
        nv_ref[...] = v

    spec = pl.BlockSpec((tr, C), lambda i: (i, 0))
    return pl.pallas_call(body, name=name, grid=(R // tr,), in_specs=[spec] * 4, out_specs=[spec] * 3,
                          out_shape=[jax.ShapeDtypeStruct((R, C), F32)] * 3,
                          compiler_params=_params(("arbitrary",)))(w, g, m, v)


def _add_pair(units, got, core, *, name):
    n4, R, C = got.shape
    rows = n4 * R
    tr = 512 if rows % 512 == 0 else R

    def body(c_ref, u_ref, got_ref, o_ref):
        o_ref[...] = (u_ref[...].astype(F32) + got_ref[...].astype(F32)).astype(BF16)

    spec = pl.BlockSpec((tr, C), lambda i, c: (i, 0))
    grid_spec = pltpu.PrefetchScalarGridSpec(
        num_scalar_prefetch=1, grid=(rows // tr,),
        in_specs=[pl.BlockSpec((None, tr, C), lambda i, c: (c[0], i, 0)), spec], out_specs=spec)
    out = pl.pallas_call(body, name=name, grid_spec=grid_spec, out_shape=jax.ShapeDtypeStruct((rows, C), BF16),
                         compiler_params=_params(("arbitrary",)))(core.reshape(1), units.reshape(2, rows, C),
                                                                   got.reshape(rows, C))
    return out.reshape(n4, R, C)


def _sum_chips(w, own, got, pos, *, name, layer=0, into=None):
    _, R, C = own.shape
    tr = _sub_rows(R)
    nr = R // tr

    def body(p_ref, own_ref, got_ref, *rest):
        o_ref = rest[-1]
        o_ref[...] = (own_ref[...].astype(F32) + got_ref[0].astype(F32) + got_ref[1].astype(F32)
                      + got_ref[2].astype(F32))

    if w.row_sharded:
        out_map = lambda i, p: (layer, i, p[1])
    else:
        out_map = lambda i, p: (layer, p[1] * nr + i, 0)
    ins = [pos, own, got]
    in_specs = [pl.BlockSpec((None, tr, C), lambda i, p: (p[0], i, 0)),
                pl.BlockSpec((3, tr, C), lambda i, p: (0, i, 0))]
    alias = {}
    if into is not None:
        ins.append(into)
        in_specs.append(ANY)
        alias = {3: 0}
    grid_spec = pltpu.PrefetchScalarGridSpec(num_scalar_prefetch=1, grid=(nr,), in_specs=in_specs,
                                             out_specs=pl.BlockSpec((None, tr, C), out_map))
    return pl.pallas_call(body, name=name, grid_spec=grid_spec, input_output_aliases=alias,
                          out_shape=jax.ShapeDtypeStruct((w.L, w.ks, w.ns), F32),
                          compiler_params=_params(("arbitrary",)))(*ins)


def _mesh_pos():
    return lax.axis_index("x"), lax.axis_index("y"), lax.axis_index("c")


def _other_chips(x, y):
    return [(1 - x, y), (x, 1 - y), (1 - x, 1 - y)]


ANY = pl.BlockSpec(memory_space=pl.ANY)


class _W:
    def __init__(self, name, shard, row_sharded):
        self.name = name
        self.L, ks, ns = shard.shape
        self.row_sharded = row_sharded
        self.K, self.N = (ks * N_CHIPS, ns) if row_sharded else (ks, ns * N_CHIPS)
        self.ks, self.ns = ks, ns

    def shard_of(self, full, j):
        if self.row_sharded:
            return full.at[:, pl.ds(j * self.ks, self.ks), :]
        return full.at[:, :, pl.ds(j * self.ns, self.ns)]

    def half_of(self, shard, c):
        if self.row_sharded:
            return shard.at[:, :, pl.ds(c * (self.ns // 2), self.ns // 2)]
        return shard.at[:, pl.ds(c * (self.ks // 2), self.ks // 2), :]


HBM = pl.BlockSpec(memory_space=pltpu.HBM)
SEM = pl.BlockSpec(memory_space=pltpu.SEMAPHORE)
IN_FLIGHT = pltpu.SideEffectType.DATAFLOW_SIDE_EFFECTING


def _in_hbm(a):
    return pltpu.with_memory_space_constraint(a, pltpu.HBM)


def _gather_start(ws, shards, after):
    nw = len(ws)

    def body(*refs):
        src, dst = refs[:nw], refs[nw:2 * nw]
        send, recv = refs[2 * nw + 1:3 * nw + 1], refs[3 * nw + 1:4 * nw + 1]
        x, y, c = _mesh_pos()
        me = 2 * x + y
        for i, w in enumerate(ws):
            for f, (px, py) in enumerate(_other_chips(x, y)):
                pltpu.make_async_remote_copy(src_ref=w.half_of(src[i], c), dst_ref=w.half_of(w.shard_of(dst[i], me), c),
                                             send_sem=send[i].at[f], recv_sem=recv[i].at[f], device_id=(px, py, c),
                                             device_id_type=MESH).start()

    fulls = [lax.empty((w.L, w.K, w.N), BF16) for w in ws]
    out = pl.pallas_call(
        body, name="gather_start", in_specs=[HBM] * (2 * nw) + [ANY],
        out_specs=[SEM] * (2 * nw) + [HBM] * (2 * nw),
        out_shape=[pltpu.SemaphoreType.DMA((3,))] * (2 * nw)
        + [pltpu.HBM(s.shape, BF16) for s in shards] + [pltpu.HBM(f.shape, BF16) for f in fulls],
        input_output_aliases={i: 2 * nw + i for i in range(2 * nw)},
        compiler_params=pltpu.CompilerParams(has_side_effects=IN_FLIGHT))(
            *[_in_hbm(s) for s in shards], *[_in_hbm(f) for f in fulls], after)
    return [(out[i], out[nw + i], out[2 * nw + i], out[3 * nw + i]) for i in range(nw)]


def _gather_wait(ws, flight, after, *, name):
    nw = len(ws)

    def body(*refs):
        src, dst = refs[:nw], refs[nw:2 * nw]
        send, recv = refs[2 * nw:3 * nw], refs[3 * nw:4 * nw]
        x, y, c = _mesh_pos()
        for i, w in enumerate(ws):
            for f, (px, py) in enumerate(_other_chips(x, y)):
                landed = w.half_of(w.shard_of(dst[i], 2 * px + py), c)
                cp = pltpu.make_async_remote_copy(src_ref=w.half_of(src[i], c), dst_ref=landed, send_sem=send[i].at[f],
                                                  recv_sem=recv[i].at[f], device_id=(px, py, c), device_id_type=MESH)
                cp.wait_send()
                cp.wait_recv()

    shards, fulls = [fl[2] for fl in flight], [fl[3] for fl in flight]
    out = pl.pallas_call(
        body, name=name, in_specs=[HBM] * (2 * nw) + [SEM] * (2 * nw) + [ANY],
        out_specs=[HBM] * (2 * nw),
        out_shape=[pltpu.HBM(s.shape, BF16) for s in shards] + [pltpu.HBM(f.shape, BF16) for f in fulls],
        input_output_aliases={i: i for i in range(2 * nw)},
        compiler_params=pltpu.CompilerParams(has_side_effects=IN_FLIGHT))(
            *shards, *fulls, *[fl[0] for fl in flight], *[fl[1] for fl in flight], after)
    return out[:nw], out[nw:]


def _gather_finish(ws, shards, fulls, *, name):
    nw = len(ws)

    def body(*refs):
        src, dst, stage = refs[:nw], refs[3 * nw:4 * nw], refs[4 * nw:5 * nw]
        send_sems, recv_sems, load_sems, store_sems = refs[5 * nw:]
        x, y, c = _mesh_pos()
        me = 2 * x + y
        sibling = (x, y, 1 - c)
        chips = _other_chips(x, y)

        def fwd(i, w, f, half):
            px, py = chips[f]
            landed = w.half_of(w.shard_of(dst[i], 2 * px + py), half)
            return pltpu.make_async_remote_copy(src_ref=landed, dst_ref=landed, send_sem=send_sems.at[3 * i + f],
                                                recv_sem=recv_sems.at[3 * i + f], device_id=sibling,
                                                device_id_type=MESH)

        loads = [pltpu.make_async_copy(src[i], stage[i], load_sems.at[i]) for i in range(nw)]
        for cp in loads:
            cp.start()
        sends = [fwd(i, w, f, c) for i, w in enumerate(ws) for f in range(3)]
        for cp in sends:
            cp.start()
        stores = [pltpu.make_async_copy(stage[i], w.shard_of(dst[i], me), store_sems.at[i])
                  for i, w in enumerate(ws)]
        for ld, st in zip(loads, stores):
            ld.wait()
            st.start()
        for i, w in enumerate(ws):
            for f in range(3):
                fwd(i, w, f, 1 - c).wait_recv()
        for cp in sends:
            cp.wait_send()
        for cp in stores:
            cp.wait()

    out = pl.pallas_call(
        body, name=name, in_specs=[ANY] * (2 * nw), out_specs=[ANY] * (2 * nw),
        out_shape=[jax.ShapeDtypeStruct(s.shape, BF16) for s in shards]
        + [jax.ShapeDtypeStruct(f.shape, BF16) for f in fulls],
        input_output_aliases={i: i for i in range(2 * nw)},
        scratch_shapes=[pltpu.VMEM((w.L, w.ks, w.ns), BF16) for w in ws]
        + [pltpu.SemaphoreType.DMA((3 * nw,)), pltpu.SemaphoreType.DMA((3 * nw,)), pltpu.SemaphoreType.DMA((nw,)),
           pltpu.SemaphoreType.DMA((nw,))],
        compiler_params=_params(has_side_effects=True))(*shards, *fulls)
    return out[nw:]


def _split_copies(name, srcs, lands, n_sems, copies_of, *, flight=None, after=None):
    n = len(srcs)
    starting = flight is None

    def body(*refs):
        src, land = refs[:n], refs[n:2 * n]
        sems = refs[2 * n + 1:4 * n + 1] if starting else refs[2 * n:4 * n]
        for i in range(n):
            for cp in copies_of(i, src[i], land[i], sems[i], sems[n + i]):
                if starting:
                    cp.start()
                else:
                    cp.wait_send()
                    cp.wait_recv()

    thru = [pltpu.HBM(a.shape, a.dtype) for a in list(srcs) + list(lands)]
    if starting:
        out = pl.pallas_call(
            body, name=name, in_specs=[HBM] * (2 * n) + [ANY], out_specs=[SEM] * (2 * n) + [HBM] * (2 * n),
            out_shape=[pltpu.SemaphoreType.DMA((n_sems,))] * (2 * n) + thru,
            input_output_aliases={i: 2 * n + i for i in range(2 * n)},
            compiler_params=pltpu.CompilerParams(has_side_effects=IN_FLIGHT))(
                *[_in_hbm(a) for a in srcs], *[_in_hbm(a) for a in lands], after)
        return [(out[i], out[n + i], out[2 * n + i], out[3 * n + i]) for i in range(n)]
    out = pl.pallas_call(
        body, name=name, in_specs=[HBM] * (2 * n) + [SEM] * (2 * n) + [ANY], out_specs=[HBM] * (2 * n),
        out_shape=thru, input_output_aliases={i: i for i in range(2 * n)},
        compiler_params=pltpu.CompilerParams(has_side_effects=IN_FLIGHT))(
            *srcs, *lands, *[fl[0] for fl in flight], *[fl[1] for fl in flight], after)
    return out[:n], out[n:]


def _sum8(land, vec, me):
    R = vec.shape[0]

    def body(me_ref, land_ref, vec_ref, o_ref):
        acc = jnp.zeros((R, 128), F32)
        for d in range(8):
            acc = acc + jnp.where(me_ref[0] == d, vec_ref[...], land_ref[d])
        o_ref[...] = acc

    grid_spec = pltpu.PrefetchScalarGridSpec(
        num_scalar_prefetch=1, grid=(1,),
        in_specs=[pl.BlockSpec((8, R, 128), lambda i, m: (0, 0, 0)), pl.BlockSpec((R, 128), lambda i, m: (0, 0))],
        out_specs=pl.BlockSpec((R, 128), lambda i, m: (0, 0)))
    return pl.pallas_call(body, name="sum8", grid_spec=grid_spec, out_shape=jax.ShapeDtypeStruct((R, 128), F32),
                          compiler_params=_params(("arbitrary",)))(me.reshape(1), land, vec)


def _swap_copies(i, src, got, send, recv):
    x, y, c = _mesh_pos()
    return [pltpu.make_async_remote_copy(src_ref=src.at[1 - c], dst_ref=got, send_sem=send.at[0], recv_sem=recv.at[0],
                                         device_id=(x, y, 1 - c), device_id_type=MESH)]


def _gather8_copies(i, src, land, send, recv):
    x, y, c = _mesh_pos()
    me = 4 * x + 2 * y + c
    peers = [(x, y, 1 - c)] + [(px, py, pc) for px, py in _other_chips(x, y) for pc in (c, 1 - c)]
    return [pltpu.make_async_remote_copy(src_ref=src, dst_ref=land.at[me], send_sem=send.at[k], recv_sem=recv.at[k],
                                         device_id=peer, device_id_type=MESH) for k, peer in enumerate(peers)]


def _scatter_copy(src, got, send, recv, f, chip, c):
    px, py = chip
    return pltpu.make_async_remote_copy(src_ref=src.at[2 * px + py], dst_ref=got.at[f], send_sem=send.at[f],
                                        recv_sem=recv.at[f], device_id=(px, py, c), device_id_type=MESH)


def _scatter_start(sums, *, name):
    nw = len(sums)

    def body(*refs):
        src, got = refs[:nw], refs[nw:2 * nw]
        send, recv = refs[2 * nw:3 * nw], refs[3 * nw:4 * nw]
        x, y, c = _mesh_pos()
        for i in range(nw):
            for f, chip in enumerate(_other_chips(x, y)):
                _scatter_copy(src[i], got[i], send[i], recv[i], f, chip, c).start()

    lands = [lax.empty((3,) + s.shape[1:], BF16) for s in sums]
    out = pl.pallas_call(
        body, name=name, in_specs=[HBM] * (2 * nw), out_specs=[SEM] * (2 * nw) + [HBM] * (2 * nw),
        out_shape=[pltpu.SemaphoreType.DMA((3,))] * (2 * nw)
        + [pltpu.HBM(s.shape, BF16) for s in sums] + [pltpu.HBM(l.shape, BF16) for l in lands],
        input_output_aliases={i: 2 * nw + i for i in range(2 * nw)},
        compiler_params=pltpu.CompilerParams(has_side_effects=IN_FLIGHT))(
            *[_in_hbm(s) for s in sums], *[_in_hbm(l) for l in lands])
    return [(out[i], out[nw + i], out[2 * nw + i], out[3 * nw + i]) for i in range(nw)]


def _scatter_wait(flight, after):
    nw = len(flight)

    def body(*refs):
        src, got = refs[:nw], refs[nw:2 * nw]
        send, recv = refs[2 * nw:3 * nw], refs[3 * nw:4 * nw]
        x, y, c = _mesh_pos()
        for i in range(nw):
            for f, chip in enumerate(_other_chips(x, y)):
                cp = _scatter_copy(src[i], got[i], send[i], recv[i], f, chip, c)
                cp.wait_send()
                cp.wait_recv()

    sums, lands = [fl[2] for fl in flight], [fl[3] for fl in flight]
    out = pl.pallas_call(
        body, name="scatter_wait", in_specs=[HBM] * (2 * nw) + [SEM] * (2 * nw) + [ANY], out_specs=[HBM] * (2 * nw),
        out_shape=[pltpu.HBM(s.shape, BF16) for s in sums] + [pltpu.HBM(l.shape, BF16) for l in lands],
        input_output_aliases={i: i for i in range(2 * nw)},
        compiler_params=pltpu.CompilerParams(has_side_effects=IN_FLIGHT))(
            *sums, *lands, *[fl[0] for fl in flight], *[fl[1] for fl in flight], after)
    return out[:nw], out[nw:]


def _join_halves(ws, shards):
    nw = len(ws)

    def body(*refs):
        buf = refs[nw:2 * nw]
        send_sems, recv_sems = refs[2 * nw:]
        x, y, c = _mesh_pos()
        sibling = (x, y, 1 - c)

        def copy(i, w, half):
            region = w.half_of(buf[i], half)
            return pltpu.make_async_remote_copy(src_ref=region, dst_ref=region, send_sem=send_sems.at[i],
                                                recv_sem=recv_sems.at[i], device_id=sibling, device_id_type=MESH)

        sends = [copy(i, w, c) for i, w in enumerate(ws)]
        for cp in sends:
            cp.start()
        for i, w in enumerate(ws):
            copy(i, w, 1 - c).wait_recv()
        for cp in sends:
            cp.wait_send()

    return pl.pallas_call(
        body, name="join_halves", in_specs=[ANY] * nw, out_specs=[ANY] * nw,
        out_shape=[jax.ShapeDtypeStruct((w.L, w.ks, w.ns), F32) for w in ws],
        input_output_aliases={i: i for i in range(nw)},
        scratch_shapes=[pltpu.SemaphoreType.DMA((nw,)), pltpu.SemaphoreType.DMA((nw,))],
        compiler_params=_params(has_side_effects=True))(*shards)


def _allreduce_small(vec):
    R = vec.shape[0]

    def body(x_ref, o_ref, buf, send_sems, recv_sems):
        x, y, c = _mesh_pos()
        me, sibling = (x, y, c), (x, y, 1 - c)
        chips = _other_chips(x, y)

        def slot(px, py, pc):
            return buf.at[4 * px + 2 * py + pc]

        def copy(k, block, to, src=None):
            return pltpu.make_async_remote_copy(src_ref=slot(*block) if src is None else src, dst_ref=slot(*block),
                                                send_sem=send_sems.at[k], recv_sem=recv_sems.at[k], device_id=to,
                                                device_id_type=MESH)

        first = [copy(0, me, sibling, src=x_ref)] + [copy(1 + f, me, (*chip, c), src=x_ref)
                                                     for f, chip in enumerate(chips)]
        for cp in first:
            cp.start()
        passed = [copy(4 + f, (*chip, c), sibling) for f, chip in enumerate(chips)]
        for f, chip in enumerate(chips):
            copy(1 + f, (*chip, c), me).wait_recv()
            passed[f].start()
        copy(0, sibling, me).wait_recv()
        for f, chip in enumerate(chips):
            copy(4 + f, (*chip, 1 - c), me).wait_recv()
        for cp in first + passed:
            cp.wait_send()
        slot(*me)[...] = x_ref[...]
        acc = buf[0]
        for d in range(1, 8):
            acc = acc + buf[d]
        o_ref[...] = acc

    return pl.pallas_call(
        body, name="allreduce_small", in_specs=[pl.BlockSpec(memory_space=pltpu.VMEM)],
        out_specs=pl.BlockSpec(memory_space=pltpu.VMEM), out_shape=jax.ShapeDtypeStruct((R, 128), F32),
        scratch_shapes=[pltpu.VMEM((8, R, 128), F32), pltpu.SemaphoreType.DMA((7,)), pltpu.SemaphoreType.DMA((7,))],
        compiler_params=_params())(vec)


def _pack(parts):
    flat = jnp.concatenate([p.reshape(-1).astype(F32) for p in parts])
    n = flat.shape[0]
    pad = (-n) % (64 * 128)
    return jnp.pad(flat, (0, pad)).reshape(-1, 128)


def _unpack(vec, shapes):
    flat = vec.reshape(-1)
    out, off = [], 0
    for s in shapes:
        n = int(np.prod(s))
        out.append(flat[off:off + n].reshape(s))
        off += n
    return out


def kernel(x, a_norm_g, a_w_in, a_v_norm_g, a_w_s, a_b_s, a_w_out, kv_norm_g, w_kv, b_norm_g, b_w_q, b_rel_bias, b_w_o, f_norm_g, f_w_in, f_conv_w, f_conv_b, f_w_down, final_norm_g, loss_target, m_a_norm_g, m_a_w_in, m_a_v_norm_g, m_a_w_s, m_a_b_s, m_a_w_out, m_kv_norm_g, m_w_kv, m_b_norm_g, m_b_w_q, m_b_rel_bias, m_b_w_o, m_f_norm_g, m_f_w_in, m_f_conv_w, m_f_conv_b, m_f_w_down, m_final_norm_g, v_a_norm_g, v_a_w_in, v_a_v_norm_g, v_a_w_s, v_a_b_s, v_a_w_out, v_kv_norm_g, v_w_kv, v_b_norm_g, v_b_w_q, v_b_rel_bias, v_b_w_o, v_f_norm_g, v_f_w_in, v_f_conv_w, v_f_conv_b, v_f_w_down, v_final_norm_g):
    B, S, D = x.shape
    T = B * S
    xi, yi, ci = lax.axis_index("x"), lax.axis_index("y"), lax.axis_index("c")
    j_me = (2 * xi + yi).astype(jnp.int32)
    core = ci.astype(jnp.int32)
    pos = jnp.stack([j_me, core])

    w_shards = {"a_w_in": (a_w_in, False), "a_w_out": (a_w_out, True), "w_kv": (w_kv[None], False),
                "b_w_q": (b_w_q, True), "b_w_o": (b_w_o, True), "f_w_in": (f_w_in, False), "f_w_down": (f_w_down, True)}
    names = list(w_shards)
    ws = [_W(n, w_shards[n][0], w_shards[n][1]) for n in names]
    g_shards = {"a_w_in": (a_w_in, False), "a_w_out": (a_w_out, True),
                "f_w_in0": (f_w_in[0:1], False), "f_w_down0": (f_w_down[0:1], True),
                "w_kv": (w_kv[None], False), "b_w_q": (b_w_q, True), "b_w_o": (b_w_o, True),
                "f_w_in1": (f_w_in[1:2], False), "f_w_down1": (f_w_down[1:2], True)}
    g_names = list(g_shards)
    g_ws = {n: _W(n, *g_shards[n]) for n in g_names}

    Wd = a_w_in.shape[1]
    GW = a_v_norm_g.shape[1] * N_CHIPS
    F2 = f_conv_w.shape[2] * N_CHIPS
    Fh = F2 // 2
    nsd, nsg, nsf = a_norm_g.shape[1], a_v_norm_g.shape[1], f_conv_w.shape[2]
    own = (ci == 0).astype(F32)
    place = lambda sh, width, n: lax.dynamic_update_slice_in_dim(
        jnp.zeros(sh.shape[:-1] + (width,), F32), sh * own, j_me * n, axis=sh.ndim - 1)
    gathered = _allreduce_small(_pack([place(a_norm_g, Wd, nsd), place(a_v_norm_g, GW, nsg),
                                       place(f_conv_w, F2, nsf)]))
    a_g, a_vg, conv_w = _unpack(gathered, [(1, Wd), (1, GW), (2, 3, F2)])

    flight = dict(zip(g_names, _gather_start([g_ws[n] for n in g_names],
                                             [g_shards[n][0].astype(BF16) for n in g_names], gathered)))
    full = {}

    def tied(x, flight):
        x, thru = lax.optimization_barrier((x, flight[0][2]))
        return x, [flight[0][:2] + (thru,) + flight[0][3:]] + flight[1:]

    def arrive(group, after, tag):
        gw = [g_ws[n] for n in group]
        sh, fu = _gather_wait(gw, [flight[n] for n in group], after, name=f"gather_wait_{tag}")
        full.update(zip(group, _gather_finish(gw, sh, fu, name=f"gather_finish_{tag}")))
    conv_w2 = conv_w.reshape(2, 3, 2, Fh).transpose(0, 2, 1, 3)
    conv_b2 = f_conv_b.reshape(2, 2, Fh)

    h0 = x.reshape(T, D)
    target = loss_target.reshape(T, D)
    bs_tile = jnp.repeat(a_b_s[0].T, GROUP_DIM, axis=1)
    ws_a = a_w_s[0]
    scale = HEAD_DIM ** -0.5
    HD = b_w_q.shape[2]
    H = HD // HEAD_DIM
    n_rel = b_rel_bias.shape[-1]
    frow, (flight["a_w_in"],) = tied(b_rel_bias[0][:, _bias_index()].reshape(H, 1, F_LEN), [flight["a_w_in"]])
    bias = _bias_expand(frow)

    def ffn_fwd(h, l, loss=None):
        yff, a, c, n = _ffn_in_conv(h, full[f"f_w_in{l}"], f_norm_g[l], conv_w2[l], conv_b2[l], S, name=f"ffn{l}_in")
        return _mm(yff, full[f"f_w_down{l}"], layer=0, res=h, loss=loss, name=f"ffn{l}_down"), (a, c, n, yff)

    arrive(["a_w_in", "a_w_out"], bias, "a")
    zp, n_a = _mm(h0, full["a_w_in"], layer=0, norm_g=a_g[0], out_dtype=BF16, emit_norm=True, name="a_in")
    out_a = _gate_fwd(zp, a_vg, ws_a, bs_tile)
    h1 = _mm(out_a, full["a_w_out"], layer=0, res=h0, name="a_out")
    arrive(["f_w_in0", "f_w_down0"], h1, "f0")
    h2, saved0 = ffn_fwd(h1, 0)
    arrive(["w_kv", "b_w_q", "b_w_o"], h2, "b")
    arrive(["f_w_in1", "f_w_down1"], h2, "f1")
    kv, n_kv = _mm(h2, full["w_kv"], layer=0, norm_g=kv_norm_g, out_dtype=BF16, split_out=True, emit_norm=True,
                   name="kv")
    q, n_q = _mm(h2, full["b_w_q"], layer=0, norm_g=b_norm_g[0], scale=scale, out_dtype=BF16, emit_norm=True,
                 name="q")
    kv4, q3 = kv.reshape(2, B, S, HD), q.reshape(B, S, HD)
    o = _attn_fwd(q3, kv4, bias, B, S).reshape(T, HD)
    h3 = _mm(o, full["b_w_o"], layer=0, res=h2, name="attn_out")
    (dh, loss8, dg_final), saved1 = ffn_fwd(h3, 1, loss=(final_norm_g, target))

    units = {}

    in_flight = {}

    def swap_start(group, tag, carry):
        us = [units[n] for n in group]
        lands = [lax.empty(u.shape[1:], BF16) for u in us]
        carry, flight = tied(carry, _split_copies(f"swap_start_{tag}", us, lands, 1, _swap_copies, after=us[0]))
        return (group, tag, flight), carry

    def reduce_start(swap, after):
        group, tag, flight = swap
        us, got = _split_copies(f"swap_wait_{tag}", [fl[2] for fl in flight], [fl[3] for fl in flight], 1,
                                _swap_copies, flight=flight, after=after)
        sums = [_add_pair(u, g_, core, name=f"pair_{n}") for n, u, g_ in zip(group, us, got)]
        after, flight = tied(after, _scatter_start(sums, name=f"scatter_start_{tag}"))
        in_flight.update(zip(group, flight))
        return after

    def ffn_bwd(dh, h, saved, l, early):
        a, c, n, yff = saved
        units[f"f_w_down{l}"] = _mm_tn(yff, dh, rows_are_shards=True, name=f"ffn{l}_down_dw")
        dh_in = dh
        if early:
            sw, dh_in = swap_start([f"f_w_down{l}"], f"fd{l}", dh)
        dyff = _mm(dh_in, full[f"f_w_down{l}"], layer=0, trans_w=True, out_dtype=BF16, name=f"ffn{l}_down_dx")
        if early:
            dyff = reduce_start(sw, dyff)
        da, dcw, dcb = _conv_bwd(a, c, dyff, conv_w2[l], S)
        units[f"f_w_in{l}"] = _mm_tn(n, da, split_y=True, name=f"ffn{l}_in_dw")
        sw, da = swap_start([f"f_w_in{l}"] if early else [f"f_w_down{l}", f"f_w_in{l}"], f"f{l}", da)
        dh, dg = _mm(da, full[f"f_w_in{l}"], layer=0, trans_w=True, split_x=True, bwd=(h, f_norm_g[l], dh), tm=256,
                     name=f"ffn{l}_in_dx")
        return reduce_start(sw, dh), dg, dcw, dcb

    dh, dg_f1, dcw1, dcb1 = ffn_bwd(dh, h3, saved1, 1, False)
    do = _mm(dh, full["b_w_o"], layer=0, trans_w=True, out_dtype=BF16, name="attn_out_dx")
    units["b_w_o"] = _mm_tn(o, dh, rows_are_shards=True, name="b_w_o_dw")
    dq, dkv, dbias = _attn_bwd(q3, kv4, bias, do.reshape(B, S, HD), B, S)
    dq, d_rel = lax.optimization_barrier((dq, _bias_reduce(dbias, n_rel)))
    d_rel = d_rel.reshape(1, H, n_rel)
    dq, dkv = dq.reshape(T, HD), dkv.reshape(2, T, HD)
    units["b_w_q"] = _mm_tn(n_q, dq, rows_are_shards=True, name="b_w_q_dw")
    dh, dg_b = _mm(dq, full["b_w_q"], layer=0, trans_w=True, bwd=(h2, b_norm_g[0], dh), name="q_dx")
    units["w_kv"] = _mm_tn(n_kv, dkv, split_y=True, name="w_kv_dw")
    sw, dkv = swap_start(["b_w_o", "b_w_q", "w_kv"], "b", dkv)
    dh, dg_kv = _mm(dkv, full["w_kv"], layer=0, trans_w=True, split_x=True, bwd=(h2, kv_norm_g, dh), name="kv_dx")
    dh = reduce_start(sw, dh)
    dh, dg_f0, dcw0, dcb0 = ffn_bwd(dh, h1, saved0, 0, True)
    units["a_w_out"] = _mm_tn(out_a, dh, rows_are_shards=True, name="a_w_out_dw")
    sw, dh_in = swap_start(["a_w_out"], "ao", dh)
    d_out = _mm(dh_in, full["a_w_out"], layer=0, trans_w=True, out_dtype=BF16, name="a_out_dx")
    d_out = reduce_start(sw, d_out)
    dzp, dws, dbs, dgv = _gate_bwd(zp, d_out, a_vg, ws_a, bs_tile)
    units["a_w_in"] = _mm_tn(n_a, dzp, name="a_w_in_dw")
    sw, dzp_in = swap_start(["a_w_in"], "ai", dzp)
    grad_x, dg_a = _mm(dzp_in, full["a_w_in"], layer=0, trans_w=True, bwd=(h0, a_g[0], dh), name="a_in_dx")
    grad_x = reduce_start(sw, grad_x)

    to_flat = lambda d: d.transpose(1, 0, 2).reshape(3, F2)
    small = {"a_norm_g": dg_a, "a_v_norm_g": dgv, "a_w_s": dws[None], "a_b_s": dbs[None], "kv_norm_g": dg_kv[0],
             "b_norm_g": dg_b, "b_rel_bias": d_rel, "f_norm_g": jnp.concatenate([dg_f0, dg_f1], axis=0),
             "f_conv_w": jnp.stack([to_flat(dcw0), to_flat(dcw1)]),
             "f_conv_b": jnp.stack([dcb0.reshape(F2), dcb1.reshape(F2)]), "final_norm_g": dg_final[0]}
    snames = list(small)
    small_vec = _pack([small[n] for n in snames] + [loss8[0:1, 0:1]])
    grad_x, small_flight = tied(grad_x, _split_copies("small_start", [small_vec],
                                                      [lax.empty((8,) + small_vec.shape, F32)], 7, _gather8_copies,
                                                      after=small_vec))

    sums, recv = _scatter_wait([in_flight[n] for n in g_names], grad_x)
    sums, recv = dict(zip(g_names, sums)), dict(zip(g_names, recv))
    halves = []
    for n, w in zip(names, ws):
        if w.L == 1:
            halves.append(_sum_chips(w, sums[n], recv[n], pos, name=f"chips_{n}"))
        else:
            first = _sum_chips(w, sums[n + "0"], recv[n + "0"], pos, name=f"chips_{n}0")
            halves.append(_sum_chips(w, sums[n + "1"], recv[n + "1"], pos, layer=1, into=first, name=f"chips_{n}1"))
    g_big = dict(zip(names, _join_halves(ws, halves)))
    g_big["w_kv"] = g_big["w_kv"][0]

    given = dict(a_norm_g=(a_norm_g, m_a_norm_g, v_a_norm_g), a_w_in=(a_w_in, m_a_w_in, v_a_w_in),
                 a_v_norm_g=(a_v_norm_g, m_a_v_norm_g, v_a_v_norm_g), a_w_s=(a_w_s, m_a_w_s, v_a_w_s),
                 a_b_s=(a_b_s, m_a_b_s, v_a_b_s), a_w_out=(a_w_out, m_a_w_out, v_a_w_out),
                 kv_norm_g=(kv_norm_g, m_kv_norm_g, v_kv_norm_g), w_kv=(w_kv, m_w_kv, v_w_kv),
                 b_norm_g=(b_norm_g, m_b_norm_g, v_b_norm_g), b_w_q=(b_w_q, m_b_w_q, v_b_w_q),
                 b_rel_bias=(b_rel_bias, m_b_rel_bias, v_b_rel_bias), b_w_o=(b_w_o, m_b_w_o, v_b_w_o),
                 f_norm_g=(f_norm_g, m_f_norm_g, v_f_norm_g), f_w_in=(f_w_in, m_f_w_in, v_f_w_in),
                 f_conv_w=(f_conv_w, m_f_conv_w, v_f_conv_w), f_conv_b=(f_conv_b, m_f_conv_b, v_f_conv_b),
                 f_w_down=(f_w_down, m_f_w_down, v_f_w_down), final_norm_g=(final_norm_g, m_final_norm_g, v_final_norm_g))
    order = list(given)
    grads, deltas, new_m, new_v = {}, {}, {}, {}
    for n in names:
        w_, m_, v_ = given[n]
        g_ = g_big[n]
        C = w_.shape[-1]
        d2, m2, v2 = _adamw(w_.reshape(-1, C), g_.reshape(-1, C), m_.reshape(-1, C), v_.reshape(-1, C),
                            name=f"adamw_{n}")
        grads[n], deltas[n], new_m[n], new_v[n] = g_.reshape(w_.shape), d2.reshape(w_.shape), m2.reshape(w_.shape), \
            v2.reshape(w_.shape)
    vecs, lands = _split_copies("small_wait", [small_flight[0][2]], [small_flight[0][3]], 7, _gather8_copies,
                                flight=small_flight, after=deltas[names[-1]])
    red = _sum8(lands[0], vecs[0], (4 * xi + 2 * yi + ci).astype(jnp.int32))
    parts = _unpack(red, [small[n].shape for n in snames] + [(1,)])
    g_small = dict(zip(snames, parts[:-1]))
    loss = parts[-1][0]
    g_small["a_norm_g"] = lax.dynamic_slice_in_dim(g_small["a_norm_g"], j_me * nsd, nsd, axis=1)
    g_small["a_v_norm_g"] = lax.dynamic_slice_in_dim(g_small["a_v_norm_g"], j_me * nsg, nsg, axis=1)
    g_small["f_conv_w"] = lax.dynamic_slice_in_dim(g_small["f_conv_w"], j_me * nsf, nsf, axis=2)

    sm = [n for n in order if n not in names]
    d2, m2, v2 = _adamw(_pack([given[n][0] for n in sm]), _pack([g_small[n].reshape(given[n][0].shape) for n in sm]),
                        _pack([given[n][1] for n in sm]), _pack([given[n][2] for n in sm]), name="adamw_small")
    shapes = [given[n][0].shape for n in sm]
    for n, d_, m_, v_ in zip(sm, _unpack(d2, shapes), _unpack(m2, shapes), _unpack(v2, shapes)):
        grads[n], deltas[n], new_m[n], new_v[n] = g_small[n].reshape(given[n][0].shape), d_, m_, v_

    return (loss, grad_x.reshape(B, S, D), *[grads[n] for n in order], *[deltas[n] for n in order],
            *[new_m[n] for n in order], *[new_v[n] for n in order])
```

```python
import functools
import math

import numpy as np
import jax
import jax.numpy as jnp
from jax import lax
from jax.experimental import pallas as pl
from jax.experimental.pallas import tpu as pltpu

F32 = jnp.float32
BF16 = jnp.bfloat16
MESH = pl.DeviceIdType.MESH

EPS = 1e-6
NEG_INF = -1e30
CHUNK = 64
GMLP_BLOCK = 128
GROUP_DIM = 128
HEAD_DIM = 64
LEFT_CHUNKS = 8
PAD = LEFT_CHUNKS * CHUNK
REL_CLIP = 128
Q_BLOCK = 256
K_SPAN = PAD + Q_BLOCK
F_LEN = K_SPAN + Q_BLOCK
HEADS_PER_STEP = 4
N_CHIPS = 4

ADAM_LR = 0.001
ADAM_B1 = 0.9
ADAM_B2 = 0.999
ADAM_EPS = 1e-08
ADAM_WD = 0.01
ADAM_STEP = 10

VMEM_LIMIT = 56 * 1024 * 1024


def _params(sem=None, **kw):
    if sem is not None:
        kw["dimension_semantics"] = sem
    return pltpu.CompilerParams(vmem_limit_bytes=VMEM_LIMIT, **kw)


def _rms(xf):
    r = lax.rsqrt(jnp.mean(xf * xf, axis=-1, keepdims=True) + EPS)
    return xf * r, r


def _gelu(x):
    c = math.sqrt(2.0 / math.pi)
    return 0.5 * x * (1.0 + jnp.tanh(c * (x + 0.044715 * x * x * x)))


def _gelu_grad(x):
    c = math.sqrt(2.0 / math.pi)
    t = jnp.tanh(c * (x + 0.044715 * x * x * x))
    return 0.5 * (1.0 + t) + 0.5 * x * (1.0 - t * t) * c * (1.0 + 3.0 * 0.044715 * x * x)


def _col_tile(n):
    if n <= 1024:
        return n
    for t in (1408, 1024, 512):
        if n % t == 0:
            return t
    raise ValueError(n)


def _row_tile(t, want):
    while t % want:
        want //= 2
    return want


def _mm(x, w, *, name, layer=None, trans_w=False, norm_g=None, res=None, scale=None, out_dtype=F32, bwd=None,
        split_out=False, split_x=False, emit_norm=False, loss=None, tm=512):
    T = x.shape[-2]
    K = 2 * x.shape[-1] if split_x else x.shape[-1]
    N = w.shape[-2] if trans_w else w.shape[-1]
    tn = N
    tm = _row_tile(T, 256 if N > 4096 else tm)
    nn, nm = N // tn, T // tm
    has_norm, has_res, has_bwd, has_loss = norm_g is not None, res is not None, bwd is not None, loss is not None
    dims = (((1,), (1,)), ((), ())) if trans_w else (((1,), (0,)), ((), ()))

    def body(*refs):
        it = iter(refs)
        x_ref, w_ref = next(it), next(it)
        g_ref = next(it) if has_norm else None
        res_ref = next(it) if has_res else None
        if has_bwd:
            h_ref, bg_ref, dh_ref = next(it), next(it), next(it)
        if has_loss:
            lg_ref, t_ref = next(it), next(it)
        o_ref = next(it)
        if split_x:
            kh = K // 2
            acc = lax.dot_general(x_ref[0].astype(BF16), w_ref[:, :kh] if trans_w else w_ref[:kh, :], dims,
                                  preferred_element_type=F32)
            acc = acc + lax.dot_general(x_ref[1].astype(BF16), w_ref[:, kh:] if trans_w else w_ref[kh:, :], dims,
                                        preferred_element_type=F32)
        else:
            xv = x_ref[...]
            if has_norm:
                xv = _rms(xv.astype(F32))[0] * g_ref[...]
            xb = xv.astype(BF16)
            if emit_norm:
                refs[-1][...] = xb
            acc = lax.dot_general(xb, w_ref[...], dims, preferred_element_type=F32)
        if scale is not None:
            acc = acc * scale
        if has_res:
            acc = acc + res_ref[...]
        if has_bwd:
            dg_ref = next(it)
            n, r = _rms(h_ref[...])

            @pl.when(pl.program_id(1) == 0)
            def _():
                dg_ref[...] = jnp.zeros_like(dg_ref)

            dg_ref[...] += jnp.sum(acc * n, axis=0, keepdims=True)
            t = acc * bg_ref[...]
            o_ref[...] = dh_ref[...] + r * (t - n * jnp.mean(t * n, axis=-1, keepdims=True))
        elif has_loss:
            loss_ref, dg_ref = refs[-2], refs[-1]

            @pl.when(pl.program_id(1) == 0)
            def _():
                loss_ref[...] = jnp.zeros_like(loss_ref)
                dg_ref[...] = jnp.zeros_like(dg_ref)

            n, r = _rms(acc)
            g = lg_ref[...]
            e = n * g - t_ref[...]
            loss_ref[...] += 0.5 * jnp.sum(jnp.mean(e * e, axis=-1, keepdims=True), axis=0, keepdims=True)
            dy = e * (1.0 / N)
            dg_ref[...] += jnp.sum(dy * n, axis=0, keepdims=True)
            t = dy * g
            o_ref[...] = r * (t - n * jnp.mean(t * n, axis=-1, keepdims=True))
        elif split_out:
            o_ref[0] = acc[:, :N // 2].astype(out_dtype)
            o_ref[1] = acc[:, N // 2:].astype(out_dtype)
        else:
            o_ref[...] = acc.astype(out_dtype)

    lead = () if layer is None else (None,)
    lidx = () if layer is None else (layer,)
    ins = [x, w]
    xspec = (pl.BlockSpec((2, tm, K // 2), lambda n, m: (0, m, 0)) if split_x
             else pl.BlockSpec((tm, K), lambda n, m: (m, 0)))
    wspec = (pl.BlockSpec(lead + (tn, K), lambda n, m: lidx + (n, 0)) if trans_w
             else pl.BlockSpec(lead + (K, tn), lambda n, m: lidx + (0, n)))
    in_specs = [xspec, wspec]
    if has_norm:
        ins.append(norm_g.reshape(1, K))
        in_specs.append(pl.BlockSpec((1, K), lambda n, m: (0, 0)))
    if has_res:
        ins.append(res)
        in_specs.append(pl.BlockSpec((tm, tn), lambda n, m: (m, n)))
    if split_out:
        out_shape = [jax.ShapeDtypeStruct((2, T, N // 2), out_dtype)]
        out_specs = [pl.BlockSpec((2, tm, N // 2), lambda n, m: (0, m, 0))]
    else:
        out_shape = [jax.ShapeDtypeStruct((T, N), F32 if has_bwd else out_dtype)]
        out_specs = [pl.BlockSpec((tm, tn), lambda n, m: (m, n))]
    if has_bwd:
        h, g, dh = bwd
        ins += [h, g.reshape(1, N), dh]
        in_specs += [pl.BlockSpec((tm, N), lambda n, m: (m, 0)), pl.BlockSpec((1, N), lambda n, m: (0, 0)),
                     pl.BlockSpec((tm, N), lambda n, m: (m, 0))]
        out_shape.append(jax.ShapeDtypeStruct((1, N), F32))
        out_specs.append(pl.BlockSpec((1, N), lambda n, m: (0, 0)))
    if emit_norm:
        out_shape.append(jax.ShapeDtypeStruct((T, K), BF16))
        out_specs.append(pl.BlockSpec((tm, K), lambda n, m: (m, 0)))
    if has_loss:
        ins += [loss[0].reshape(1, N), loss[1]]
        in_specs += [pl.BlockSpec((1, N), lambda n, m: (0, 0)), pl.BlockSpec((tm, N), lambda n, m: (m, 0))]
        out_shape += [jax.ShapeDtypeStruct((8, 128), F32), jax.ShapeDtypeStruct((1, N), F32)]
        out_specs += [pl.BlockSpec((8, 128), lambda n, m: (0, 0)), pl.BlockSpec((1, N), lambda n, m: (0, 0))]
    out = pl.pallas_call(body, name=name, grid=(nn, nm), in_specs=in_specs, out_specs=out_specs, out_shape=out_shape,
                         compiler_params=_params(("arbitrary", "arbitrary")))(*ins)
    return out if has_bwd or emit_norm or has_loss else out[0]


def _mm_tn(x, dy, *, name, rows_are_shards=False, split_y=False, tt=512):
    T, K = x.shape
    N = 2 * dy.shape[-1] if split_y else dy.shape[-1]
    R, C = (K // N_CHIPS, N // 2) if rows_are_shards else (K // 2, N // N_CHIPS)
    nn = 2 if split_y else 1
    tn = N // nn
    per = N_CHIPS // nn
    assert not (rows_are_shards and split_y)
    tt = _row_tile(T, tt)
    nt = T // tt

    def body(x_ref, y_ref, o_ref, acc_ref):
        t = pl.program_id(1)

        @pl.when(t == 0)
        def _():
            acc_ref[...] = jnp.zeros_like(acc_ref)

        acc_ref[...] += lax.dot_general(x_ref[...], y_ref[...].astype(BF16), (((0,), (0,)), ((), ())),
                                        preferred_element_type=F32)

        @pl.when(t == nt - 1)
        def _():
            if rows_are_shards:
                for h in range(2):
                    o_ref[h] = acc_ref[:, h * C:(h + 1) * C].astype(BF16).reshape(N_CHIPS, R, C)
            else:
                for j in range(per):
                    o_ref[:, j] = acc_ref[:, j * C:(j + 1) * C].astype(BF16).reshape(2, R, C)

    if split_y:
        yspec = pl.BlockSpec((None, tt, tn), lambda n, t: (n, t, 0))
    else:
        yspec = pl.BlockSpec((tt, tn), lambda n, t: (t, 0))
    if rows_are_shards:
        out_spec = pl.BlockSpec((2, N_CHIPS, R, C), lambda n, t: (0, 0, 0, 0))
    else:
        out_spec = pl.BlockSpec((2, per, R, C), lambda n, t: (0, n, 0, 0))
    return pl.pallas_call(body, name=name, grid=(nn, nt),
                          in_specs=[pl.BlockSpec((tt, K), lambda n, t: (t, 0)), yspec], out_specs=out_spec,
                          out_shape=jax.ShapeDtypeStruct((2, N_CHIPS, R, C), BF16),
                          scratch_shapes=[pltpu.VMEM((K, tn), F32)],
                          compiler_params=_params(("arbitrary", "arbitrary")))(x, dy)


def _chunk_mask():
    i = lax.broadcasted_iota(jnp.int32, (GMLP_BLOCK, GMLP_BLOCK), 0) // CHUNK
    j = lax.broadcasted_iota(jnp.int32, (GMLP_BLOCK, GMLP_BLOCK), 1) // CHUNK
    return i >= j


def _gate_fwd(zp, gv, ws, bs_tile, *, tm=256):
    T, W2 = zp.shape
    W = W2 // 2
    G = W // GROUP_DIM
    tm = _row_tile(T, tm)

    def body(zp_ref, gv_ref, ws_ref, bs_ref, o_ref):
        z = _gelu(zp_ref[...].astype(F32))
        u, v = z[:, :W], z[:, W:]
        vn = _rms(v)[0] * gv_ref[...]
        mask = _chunk_mask()
        for g in range(G):
            cs = slice(g * GROUP_DIM, (g + 1) * GROUP_DIM)
            wg = jnp.where(mask, ws_ref[g], 0.0).astype(BF16)
            for b in range(tm // GMLP_BLOCK):
                rs = slice(b * GMLP_BLOCK, (b + 1) * GMLP_BLOCK)
                s = jnp.dot(wg, vn[rs, cs].astype(BF16), preferred_element_type=F32) + bs_ref[:, cs]
                o_ref[rs, cs] = (u[rs, cs] * s).astype(BF16)

    return pl.pallas_call(
        body, name="gate_fwd", grid=(T // tm,),
        in_specs=[pl.BlockSpec((tm, W2), lambda i: (i, 0)), pl.BlockSpec((1, W), lambda i: (0, 0)),
                  pl.BlockSpec((G, GMLP_BLOCK, GMLP_BLOCK), lambda i: (0, 0, 0)),
                  pl.BlockSpec((GMLP_BLOCK, W), lambda i: (0, 0))],
        out_specs=pl.BlockSpec((tm, W), lambda i: (i, 0)), out_shape=jax.ShapeDtypeStruct((T, W), BF16),
        compiler_params=_params(("arbitrary",)))(zp, gv, ws, bs_tile)


def _gate_bwd(zp, d_out, gv, ws, bs_tile, *, tm=256):
    T, W2 = zp.shape
    W = W2 // 2
    G = W // GROUP_DIM
    tm = _row_tile(T, tm)
    nm = T // tm

    def body(zp_ref, do_ref, gv_ref, ws_ref, bs_ref, dzp_ref, dws_ref, dbs_ref, dgv_ref, du_scr, dvn_scr, dsum_scr):
        i = pl.program_id(0)

        @pl.when(i == 0)
        def _():
            dws_ref[...] = jnp.zeros_like(dws_ref)
            dgv_ref[...] = jnp.zeros_like(dgv_ref)
            dsum_scr[...] = jnp.zeros_like(dsum_scr)

        zp = zp_ref[...].astype(F32)
        z = _gelu(zp)
        u, v = z[:, :W], z[:, W:]
        n, r = _rms(v)
        gv = gv_ref[...]
        vn = n * gv
        d_out = do_ref[...].astype(F32)
        mask = _chunk_mask()
        for g in range(G):
            cs = slice(g * GROUP_DIM, (g + 1) * GROUP_DIM)
            wg = jnp.where(mask, ws_ref[g], 0.0).astype(BF16)
            dw = jnp.zeros((GMLP_BLOCK, GMLP_BLOCK), F32)
            for b in range(tm // GMLP_BLOCK):
                rs = slice(b * GMLP_BLOCK, (b + 1) * GMLP_BLOCK)
                vb = vn[rs, cs].astype(BF16)
                s = jnp.dot(wg, vb, preferred_element_type=F32) + bs_ref[:, cs]
                du_scr[rs, cs] = d_out[rs, cs] * s
                ds = d_out[rs, cs] * u[rs, cs]
                dsb = ds.astype(BF16)
                dvn_scr[rs, cs] = lax.dot_general(wg, dsb, (((0,), (0,)), ((), ())), preferred_element_type=F32)
                dw = dw + lax.dot_general(dsb, vb, (((1,), (1,)), ((), ())), preferred_element_type=F32)
                dsum_scr[:, cs] += ds
            dws_ref[g] += jnp.where(mask, dw, 0.0)
        dvn = dvn_scr[...]
        dgv_ref[...] += jnp.sum(dvn * n, axis=0, keepdims=True)
        t = dvn * gv
        dv = r * (t - n * jnp.mean(t * n, axis=-1, keepdims=True))
        dzp_ref[:, :W] = (du_scr[...] * _gelu_grad(zp[:, :W])).astype(BF16)
        dzp_ref[:, W:] = (dv * _gelu_grad(zp[:, W:])).astype(BF16)

        @pl.when(i == nm - 1)
        def _():
            sel = (lax.broadcasted_iota(jnp.int32, (G, W), 1) // GROUP_DIM
                   == lax.broadcasted_iota(jnp.int32, (G, W), 0)).astype(F32)
            dbs_ref[...] = lax.dot_general(sel, dsum_scr[...], (((1,), (1,)), ((), ())),
                                           precision=lax.Precision.HIGHEST, preferred_element_type=F32)

    return pl.pallas_call(
        body, name="gate_bwd", grid=(nm,),
        in_specs=[pl.BlockSpec((tm, W2), lambda i: (i, 0)), pl.BlockSpec((tm, W), lambda i: (i, 0)),
                  pl.BlockSpec((1, W), lambda i: (0, 0)),
                  pl.BlockSpec((G, GMLP_BLOCK, GMLP_BLOCK), lambda i: (0, 0, 0)),
                  pl.BlockSpec((GMLP_BLOCK, W), lambda i: (0, 0))],
        out_specs=[pl.BlockSpec((tm, W2), lambda i: (i, 0)),
                   pl.BlockSpec((G, GMLP_BLOCK, GMLP_BLOCK), lambda i: (0, 0, 0)),
                   pl.BlockSpec((G, GMLP_BLOCK), lambda i: (0, 0)), pl.BlockSpec((1, W), lambda i: (0, 0))],
        out_shape=[jax.ShapeDtypeStruct((T, W2), BF16), jax.ShapeDtypeStruct((G, GMLP_BLOCK, GMLP_BLOCK), F32),
                   jax.ShapeDtypeStruct((G, GMLP_BLOCK), F32), jax.ShapeDtypeStruct((1, W), F32)],
        scratch_shapes=[pltpu.VMEM((tm, W), F32), pltpu.VMEM((tm, W), F32), pltpu.VMEM((GMLP_BLOCK, W), F32)],
        compiler_params=_params(("arbitrary",)))(zp, d_out, gv, ws, bs_tile)


HALO = 16


def _taps(ext, w, b):
    a, a1, a2 = ext[HALO:], pltpu.roll(ext, 1, 0)[HALO:], pltpu.roll(ext, 2, 0)[HALO:]
    return w[2:3] * a + w[1:2] * a1 + w[0:1] * a2 + b, a, a1, a2


def _conv_fwd(a, cw, cb, S, *, tm=256):
    _, T, F = a.shape
    tc = _col_tile(F)
    tm = _row_tile(S, tm)
    hb = tm // HALO

    def body(a_ref, p_ref, w_ref, b_ref, o_ref, c_ref):
        first = (pl.program_id(1) * tm) % S == 0
        keep = jnp.where(first, 0.0, 1.0)

        def conv(s):
            ext = jnp.concatenate([p_ref[s].astype(F32) * keep, a_ref[s].astype(F32)], axis=0)
            c = _taps(ext, w_ref[s], b_ref[s:s + 1, :])[0].astype(BF16)
            c_ref[s] = c
            return c.astype(F32)

        up, gate = conv(0), conv(1)
        o_ref[...] = (gate * jax.nn.sigmoid(gate) * up).astype(BF16)

    return pl.pallas_call(
        body, name="conv_fwd", grid=(F // tc, T // tm),
        in_specs=[pl.BlockSpec((2, tm, tc), lambda j, i: (0, i, j)),
                  pl.BlockSpec((2, HALO, tc), lambda j, i: (0, jnp.maximum(i * hb - 1, 0), j)),
                  pl.BlockSpec((2, 3, tc), lambda j, i: (0, 0, j)), pl.BlockSpec((2, tc), lambda j, i: (0, j))],
        out_specs=[pl.BlockSpec((tm, tc), lambda j, i: (i, j)), pl.BlockSpec((2, tm, tc), lambda j, i: (0, i, j))],
        out_shape=[jax.ShapeDtypeStruct((T, F), BF16), jax.ShapeDtypeStruct((2, T, F), BF16)],
        compiler_params=_params(("arbitrary", "arbitrary")))(a, a, cw, cb)


def _ffn_in_conv(h, w, g, cw, cb, S, *, name, tm=256):
    T, D = h.shape
    F = w.shape[-1] // 2
    tc = _col_tile(F)
    tm = _row_tile(S, tm)

    def body(h_ref, w_ref, g_ref, cw_ref, cb_ref, y_ref, a_ref, c_ref, n_ref, tail):
        first = (pl.program_id(0) * tm) % S == 0
        nb = (_rms(h_ref[...])[0] * g_ref[...]).astype(BF16)
        n_ref[...] = nb
        for j in range(F // tc):
            cs = slice(j * tc, (j + 1) * tc)
            conv = []
            for s in range(2):
                acc = jnp.dot(nb, w_ref[:, s * F + j * tc:s * F + (j + 1) * tc], preferred_element_type=F32)
                ab = acc.astype(BF16)
                a_ref[s, :, cs] = ab
                af = ab.astype(F32)
                ext = jnp.concatenate([jnp.where(first, 0.0, tail[s, :, cs]), af], axis=0)
                tail[s, :, cs] = af[tm - HALO:, :]
                cv = _taps(ext, cw_ref[s, :, cs], cb_ref[s:s + 1, cs])[0].astype(BF16)
                c_ref[s, :, cs] = cv
                conv.append(cv.astype(F32))
            up, gate = conv
            y_ref[:, cs] = (gate * jax.nn.sigmoid(gate) * up).astype(BF16)

    row = lambda width: pl.BlockSpec((tm, width), lambda i: (i, 0))
    wide = pl.BlockSpec((2, tm, F), lambda i: (0, i, 0))
    return pl.pallas_call(
        body, name=name, grid=(T // tm,),
        in_specs=[row(D), pl.BlockSpec((None, D, 2 * F), lambda i: (0, 0, 0)), pl.BlockSpec((1, D), lambda i: (0, 0)),
                  pl.BlockSpec((2, 3, F), lambda i: (0, 0, 0)), pl.BlockSpec((2, F), lambda i: (0, 0))],
        out_specs=[row(F), wide, wide, row(D)],
        out_shape=[jax.ShapeDtypeStruct((T, F), BF16), jax.ShapeDtypeStruct((2, T, F), BF16),
                   jax.ShapeDtypeStruct((2, T, F), BF16), jax.ShapeDtypeStruct((T, D), BF16)],
        scratch_shapes=[pltpu.VMEM((2, HALO, F), F32)],
        compiler_params=_params(("arbitrary",)))(h, w, g.reshape(1, D), cw, cb)


def _conv_bwd(a, c, dy, cw, S, *, tm=256):
    _, T, F = a.shape
    tc = _col_tile(F)
    tm = _row_tile(S, tm)
    nm = T // tm
    hb = tm // HALO
    TE = tm + HALO
    nxt = lambda j, i: jnp.minimum((i + 1) * hb, T // HALO - 1)

    def body(a_ref, c_ref, nc_ref, dy_ref, ndy_ref, w_ref, da_ref, dw_ref, db_ref):
        i = pl.program_id(1)
        last = ((i + 1) * tm) % S == 0
        keep_n = jnp.where(last, 0.0, 1.0)
        dyf = jnp.concatenate([dy_ref[...].astype(F32), ndy_ref[...].astype(F32) * keep_n], axis=0)
        up = jnp.concatenate([c_ref[0].astype(F32), nc_ref[0].astype(F32)], axis=0)
        gate = jnp.concatenate([c_ref[1].astype(F32), nc_ref[1].astype(F32)], axis=0)
        sg = jax.nn.sigmoid(gate)
        d_up = dyf * (gate * sg)
        d_gate = dyf * up * (sg * (1.0 + gate * (1.0 - sg)))

        @pl.when(i == 0)
        def _():
            dw_ref[...] = jnp.zeros_like(dw_ref)
            db_ref[...] = jnp.zeros_like(db_ref)

        def back(s, d):
            a = a_ref[s].astype(F32)
            w = w_ref[s]
            u1, u2 = pltpu.roll(d, TE - 1, 0), pltpu.roll(d, TE - 2, 0)
            db_ref[s:s + 1, :] += jnp.sum(d[:tm], axis=0, keepdims=True)
            dw_ref[s, 2:3, :] += jnp.sum(d[:tm] * a, axis=0, keepdims=True)
            dw_ref[s, 1:2, :] += jnp.sum(u1[:tm] * a, axis=0, keepdims=True)
            dw_ref[s, 0:1, :] += jnp.sum(u2[:tm] * a, axis=0, keepdims=True)
            da_ref[s] = (w[2:3] * d + w[1:2] * u1 + w[0:1] * u2)[:tm].astype(BF16)

        back(0, d_up)
        back(1, d_gate)

    cur = pl.BlockSpec((2, tm, tc), lambda j, i: (0, i, j))
    return pl.pallas_call(
        body, name="conv_bwd", grid=(F // tc, nm),
        in_specs=[cur, cur, pl.BlockSpec((2, HALO, tc), lambda j, i: (0, nxt(j, i), j)),
                  pl.BlockSpec((tm, tc), lambda j, i: (i, j)), pl.BlockSpec((HALO, tc), lambda j, i: (nxt(j, i), j)),
                  pl.BlockSpec((2, 3, tc), lambda j, i: (0, 0, j))],
        out_specs=[cur, pl.BlockSpec((2, 3, tc), lambda j, i: (0, 0, j)), pl.BlockSpec((2, tc), lambda j, i: (0, j))],
        out_shape=[jax.ShapeDtypeStruct((2, T, F), BF16), jax.ShapeDtypeStruct((2, 3, F), F32),
                   jax.ShapeDtypeStruct((2, F), F32)],
        compiler_params=_params(("arbitrary", "arbitrary")))(a, c, c, dy, dy, cw)


def _bias_index():
    idx = np.arange(F_LEN)
    d = np.where(idx < K_SPAN, idx, idx - F_LEN)
    return np.clip(PAD - d, -REL_CLIP, REL_CLIP) + REL_CLIP


ROW_GROUP = 16


def _roll_rows(x, sign, unit, steps):
    rows = lax.broadcasted_iota(jnp.int32, x.shape, 0)
    step = 1
    while step < steps:
        shift = unit * step if sign > 0 else F_LEN - unit * step
        x = jnp.where((rows & step) != 0, pltpu.roll(x, shift, 1), x)
        step *= 2
    return x


def _bias_expand(frow):
    H = frow.shape[0]
    groups = Q_BLOCK // ROW_GROUP

    def body(f_ref, o_ref):
        coarse = _roll_rows(jnp.broadcast_to(f_ref[...], (groups, F_LEN)), 1, ROW_GROUP, groups)
        x = jnp.concatenate([jnp.broadcast_to(coarse[a:a + 1], (ROW_GROUP, F_LEN)) for a in range(groups)], axis=0)
        x = _roll_rows(x, 1, 1, ROW_GROUP)[:, :K_SPAN]
        qc = lax.broadcasted_iota(jnp.int32, (Q_BLOCK, K_SPAN), 0) // CHUNK * CHUNK
        kj = lax.broadcasted_iota(jnp.int32, (Q_BLOCK, K_SPAN), 1)
        o_ref[...] = jnp.where((kj >= qc) & (kj < qc + PAD + CHUNK), x, NEG_INF)

    return pl.pallas_call(
        body, name="bias_expand", grid=(H,),
        in_specs=[pl.BlockSpec((None, 1, F_LEN), lambda h: (h, 0, 0))],
        out_specs=pl.BlockSpec((None, Q_BLOCK, K_SPAN), lambda h: (h, 0, 0)),
        out_shape=jax.ShapeDtypeStruct((H, Q_BLOCK, K_SPAN), F32), compiler_params=_params(("arbitrary",)))(frow)


def _bias_reduce(dbias, n_rel):
    H = dbias.shape[0]
    onehot = jnp.asarray((_bias_index()[:, None] == np.arange(n_rel)[None, :]).astype(np.float32), dtype=BF16)

    def body(d_ref, oh_ref, o_ref):
        x = jnp.concatenate([d_ref[...], jnp.zeros((Q_BLOCK, F_LEN - K_SPAN), F32)], axis=1)
        fine = _roll_rows(x, -1, 1, ROW_GROUP).reshape(Q_BLOCK // ROW_GROUP, ROW_GROUP, F_LEN)
        coarse = _roll_rows(jnp.sum(fine, axis=1), -1, ROW_GROUP, Q_BLOCK // ROW_GROUP)
        row = jnp.broadcast_to(jnp.sum(coarse, axis=0, keepdims=True), (8, F_LEN))
        acc = jnp.zeros((8, n_rel), F32)
        for _ in range(3):
            piece = row.astype(BF16)
            acc = acc + jnp.dot(piece, oh_ref[...], preferred_element_type=F32)
            row = row - piece.astype(F32)
        o_ref[...] = acc[0:1]

    return pl.pallas_call(
        body, name="bias_reduce", grid=(H,),
        in_specs=[pl.BlockSpec((None, Q_BLOCK, K_SPAN), lambda h: (h, 0, 0)),
                  pl.BlockSpec((F_LEN, n_rel), lambda h: (0, 0))],
        out_specs=pl.BlockSpec((None, 1, n_rel), lambda h: (h, 0, 0)),
        out_shape=jax.ShapeDtypeStruct((H, 1, n_rel), F32), compiler_params=_params(("arbitrary",)))(dbias, onehot)


def _attn_specs(S):
    hw = HEADS_PER_STEP * HEAD_DIM
    qspec = pl.BlockSpec((None, Q_BLOCK, hw), lambda g, b, i: (b, i, g))
    kspec = pl.BlockSpec((None, None, S, hw), lambda g, b, i: (0, b, 0, g))
    vspec = pl.BlockSpec((None, None, S, hw), lambda g, b, i: (1, b, 0, g))
    bspec = pl.BlockSpec((HEADS_PER_STEP, Q_BLOCK, K_SPAN), lambda g, b, i: (g, 0, 0))
    return hw, qspec, kspec, vspec, bspec


def _span_cases(i, fn):
    short = PAD // Q_BLOCK
    for j in range(short):
        pl.when(i == j)(functools.partial(fn, PAD - j * Q_BLOCK))
    pl.when(i >= short)(functools.partial(fn, 0))


def _key_start(i, off):
    return 0 if off else pl.multiple_of(i * Q_BLOCK - PAD, Q_BLOCK)


def _attn_exp(q_ref, k_ref, b_ref, h, k0, off):
    hs = slice(h * HEAD_DIM, (h + 1) * HEAD_DIM)
    kh = k_ref[pl.ds(k0, K_SPAN - off), hs]
    s = lax.dot_general(q_ref[:, hs], kh, (((1,), (1,)), ((), ())), preferred_element_type=F32) + b_ref[h, :, off:]
    p = jnp.exp(s - jnp.max(s, axis=-1, keepdims=True))
    return p, 1.0 / jnp.sum(p, axis=-1, keepdims=True), kh


def _attn_fwd(q, kv, bias, B, S):
    HD = q.shape[-1]
    hw, qspec, kspec, vspec, bspec = _attn_specs(S)

    def body(q_ref, k_ref, v_ref, b_ref, o_ref):
        i = pl.program_id(2)

        def block(off):
            k0 = _key_start(i, off)
            outs = []
            for h in range(HEADS_PER_STEP):
                hs = slice(h * HEAD_DIM, (h + 1) * HEAD_DIM)
                p, inv, _ = _attn_exp(q_ref, k_ref, b_ref, h, k0, off)
                outs.append(jnp.dot(p.astype(BF16), v_ref[pl.ds(k0, K_SPAN - off), hs],
                                    preferred_element_type=F32) * inv)
            o_ref[...] = jnp.concatenate(outs, axis=1).astype(BF16)

        _span_cases(i, block)

    return pl.pallas_call(
        body, name="attn_fwd", grid=(HD // hw, B, S // Q_BLOCK), in_specs=[qspec, kspec, vspec, bspec],
        out_specs=qspec, out_shape=jax.ShapeDtypeStruct((B, S, HD), BF16),
        compiler_params=_params(("arbitrary", "arbitrary", "arbitrary")))(q, kv, kv, bias)


def _attn_bwd(q, kv, bias, do, B, S):
    HD = q.shape[-1]
    H = HD // HEAD_DIM
    hw, qspec, kspec, vspec, bspec = _attn_specs(S)
    scale = HEAD_DIM ** -0.5
    nq = S // Q_BLOCK

    def body(q_ref, k_ref, v_ref, b_ref, do_ref, dq_ref, dkv_ref, db_ref, dk_acc, dv_acc):
        b, i = pl.program_id(1), pl.program_id(2)

        @pl.when(i == 0)
        def _():
            dk_acc[...] = jnp.zeros_like(dk_acc)
            dv_acc[...] = jnp.zeros_like(dv_acc)

        @pl.when((i == 0) & (b == 0))
        def _():
            db_ref[...] = jnp.zeros_like(db_ref)

        def block(off):
            k0 = _key_start(i, off)
            keys = pl.ds(k0, K_SPAN - off)
            for h in range(HEADS_PER_STEP):
                hs = slice(h * HEAD_DIM, (h + 1) * HEAD_DIM)
                p, inv, kh = _attn_exp(q_ref, k_ref, b_ref, h, k0, off)
                p = p * inv
                doh = do_ref[:, hs]
                dp = lax.dot_general(doh, v_ref[keys, hs], (((1,), (1,)), ((), ())), preferred_element_type=F32)
                ds = p * (dp - jnp.sum(p * dp, axis=-1, keepdims=True))
                db_ref[h, :, off:] += ds
                dsb = ds.astype(BF16)
                dq_ref[:, hs] = (jnp.dot(dsb, kh, preferred_element_type=F32) * scale).astype(BF16)
                dk_acc[hs, keys] += lax.dot_general(q_ref[:, hs], dsb, (((0,), (0,)), ((), ())),
                                                     preferred_element_type=F32)
                dv_acc[hs, keys] += lax.dot_general(doh, p.astype(BF16), (((0,), (0,)), ((), ())),
                                                     preferred_element_type=F32)

        _span_cases(i, block)

        @pl.when(i == nq - 1)
        def _():
            dkv_ref[0] = dk_acc[...].T.astype(BF16)
            dkv_ref[1] = dv_acc[...].T.astype(BF16)

    return pl.pallas_call(
        body, name="attn_bwd", grid=(HD // hw, B, nq), in_specs=[qspec, kspec, vspec, bspec, qspec],
        out_specs=[qspec, pl.BlockSpec((2, None, S, hw), lambda g, b, i: (0, b, 0, g)), bspec],
        out_shape=[jax.ShapeDtypeStruct((B, S, HD), BF16), jax.ShapeDtypeStruct((2, B, S, HD), BF16),
                   jax.ShapeDtypeStruct((H, Q_BLOCK, K_SPAN), F32)],
        scratch_shapes=[pltpu.VMEM((hw, S), F32), pltpu.VMEM((hw, S), F32)],
        compiler_params=_params(("arbitrary", "arbitrary", "arbitrary")))(q, kv, kv, bias, do)


def _loss_head(h, g, target, *, tm=512):
    T, D = h.shape
    tm = _row_tile(T, tm)

    def body(h_ref, g_ref, t_ref, dh_ref, loss_ref, dg_ref):
        @pl.when(pl.program_id(0) == 0)
        def _():
            loss_ref[...] = jnp.zeros_like(loss_ref)
            dg_ref[...] = jnp.zeros_like(dg_ref)

        n, r = _rms(h_ref[...])
        g = g_ref[...]
        e = n * g - t_ref[...]
        loss_ref[...] += 0.5 * jnp.sum(jnp.mean(e * e, axis=-1, keepdims=True), axis=0, keepdims=True)
        dy = e * (1.0 / D)
        dg_ref[...] += jnp.sum(dy * n, axis=0, keepdims=True)
        t = dy * g
        dh_ref[...] = r * (t - n * jnp.mean(t * n, axis=-1, keepdims=True))

    row = pl.BlockSpec((tm, D), lambda i: (i, 0))
    return pl.pallas_call(
        body, name="loss_head", grid=(T // tm,), in_specs=[row, pl.BlockSpec((1, D), lambda i: (0, 0)), row],
        out_specs=[row, pl.BlockSpec((8, 128), lambda i: (0, 0)), pl.BlockSpec((1, D), lambda i: (0, 0))],
        out_shape=[jax.ShapeDtypeStruct((T, D), F32), jax.ShapeDtypeStruct((8, 128), F32),
                   jax.ShapeDtypeStruct((1, D), F32)],
        compiler_params=_params(("arbitrary",)))(h, g.reshape(1, D), target)


def _sub_rows(R):
    for cand in (256, 352, 128, 64, 8):
        if R % cand == 0 and R > cand:
            return cand
    return R


def _adamw(w, g, m, v, *, name):
    R, C = w.shape
    tr = _sub_rows(R)

    def body(w_ref, g_ref, m_ref, v_ref, d_ref, nm_ref, nv_ref):
        g = g_ref[...]
        m = ADAM_B1 * m_ref[...] + (1.0 - ADAM_B1) * g
        v = ADAM_B2 * v_ref[...] + (1.0 - ADAM_B2) * (g * g)
        m_hat = m / (1.0 - ADAM_B1 ** ADAM_STEP)
        v_hat = v / (1.0 - ADAM_B2 ** ADAM_STEP)
        d_ref[...] = -ADAM_LR * (m_hat / (jnp.sqrt(v_hat) + ADAM_EPS) + ADAM_WD * w_ref[...])
        nm_ref[...] = m
        nv_ref[...] = v

    spec = pl.BlockSpec((tr, C), lambda i: (i, 0))
    return pl.pallas_call(body, name=name, grid=(R // tr,), in_specs=[spec] * 4, out_specs=[spec] * 3,
                          out_shape=[jax.ShapeDtypeStruct((R, C), F32)] * 3,
                          compiler_params=_params(("arbitrary",)))(w, g, m, v)


def _add_pair(units, got, core, *, name):
    n4, R, C = got.shape
    rows = n4 * R
    tr = 512 if rows % 512 == 0 else R

    def body(c_ref, u_ref, got_ref, o_ref):
        o_ref[...] = (u_ref[...].astype(F32) + got_ref[...].astype(F32)).astype(BF16)

    spec = pl.BlockSpec((tr, C), lambda i, c: (i, 0))
    grid_spec = pltpu.PrefetchScalarGridSpec(
        num_scalar_prefetch=1, grid=(rows // tr,),
        in_specs=[pl.BlockSpec((None, tr, C), lambda i, c: (c[0], i, 0)), spec], out_specs=spec)
    out = pl.pallas_call(body, name=name, grid_spec=grid_spec, out_shape=jax.ShapeDtypeStruct((rows, C), BF16),
                         compiler_params=_params(("arbitrary",)))(core.reshape(1), units.reshape(2, rows, C),
                                                                   got.reshape(rows, C))
    return out.reshape(n4, R, C)


def _sum_chips(w, own, got, pos, *, name, layer=0, into=None):
    _, R, C = own.shape
    tr = _sub_rows(R)
    nr = R // tr

    def body(p_ref, own_ref, got_ref, *rest):
        o_ref = rest[-1]
        o_ref[...] = (own_ref[...].astype(F32) + got_ref[0].astype(F32) + got_ref[1].astype(F32)
                      + got_ref[2].astype(F32))

    if w.row_sharded:
        out_map = lambda i, p: (layer, i, p[1])
    else:
        out_map = lambda i, p: (layer, p[1] * nr + i, 0)
    ins = [pos, own, got]
    in_specs = [pl.BlockSpec((None, tr, C), lambda i, p: (p[0], i, 0)),
                pl.BlockSpec((3, tr, C), lambda i, p: (0, i, 0))]
    alias = {}
    if into is not None:
        ins.append(into)
        in_specs.append(ANY)
        alias = {3: 0}
    grid_spec = pltpu.PrefetchScalarGridSpec(num_scalar_prefetch=1, grid=(nr,), in_specs=in_specs,
                                             out_specs=pl.BlockSpec((None, tr, C), out_map))
    return pl.pallas_call(body, name=name, grid_spec=grid_spec, input_output_aliases=alias,
                          out_shape=jax.ShapeDtypeStruct((w.L, w.ks, w.ns), F32),
                          compiler_params=_params(("arbitrary",)))(*ins)


def _mesh_pos():
    return lax.axis_index("x"), lax.axis_index("y"), lax.axis_index("c")


def _other_chips(x, y):
    return [(1 - x, y), (x, 1 - y), (1 - x, 1 - y)]


ANY = pl.BlockSpec(memory_space=pl.ANY)


class _W:
    def __init__(self, name, shard, row_sharded):
        self.name = name
        self.L, ks, ns = shard.shape
        self.row_sharded = row_sharded
        self.K, self.N = (ks * N_CHIPS, ns) if row_sharded else (ks, ns * N_CHIPS)
        self.ks, self.ns = ks, ns

    def shard_of(self, full, j):
        if self.row_sharded:
            return full.at[:, pl.ds(j * self.ks, self.ks), :]
        return full.at[:, :, pl.ds(j * self.ns, self.ns)]

    def half_of(self, shard, c):
        if self.row_sharded:
            return shard.at[:, :, pl.ds(c * (self.ns // 2), self.ns // 2)]
        return shard.at[:, pl.ds(c * (self.ks // 2), self.ks // 2), :]


HBM = pl.BlockSpec(memory_space=pltpu.HBM)
SEM = pl.BlockSpec(memory_space=pltpu.SEMAPHORE)
IN_FLIGHT = pltpu.SideEffectType.DATAFLOW_SIDE_EFFECTING


def _in_hbm(a):
    return pltpu.with_memory_space_constraint(a, pltpu.HBM)


def _gather_start(ws, shards, after):
    nw = len(ws)

    def body(*refs):
        src, dst = refs[:nw], refs[nw:2 * nw]
        send, recv = refs[2 * nw + 1:3 * nw + 1], refs[3 * nw + 1:4 * nw + 1]
        x, y, c = _mesh_pos()
        me = 2 * x + y
        for i, w in enumerate(ws):
            for f, (px, py) in enumerate(_other_chips(x, y)):
                pltpu.make_async_remote_copy(src_ref=w.half_of(src[i], c), dst_ref=w.half_of(w.shard_of(dst[i], me), c),
                                             send_sem=send[i].at[f], recv_sem=recv[i].at[f], device_id=(px, py, c),
                                             device_id_type=MESH).start()

    fulls = [lax.empty((w.L, w.K, w.N), BF16) for w in ws]
    out = pl.pallas_call(
        body, name="gather_start", in_specs=[HBM] * (2 * nw) + [ANY],
        out_specs=[SEM] * (2 * nw) + [HBM] * (2 * nw),
        out_shape=[pltpu.SemaphoreType.DMA((3,))] * (2 * nw)
        + [pltpu.HBM(s.shape, BF16) for s in shards] + [pltpu.HBM(f.shape, BF16) for f in fulls],
        input_output_aliases={i: 2 * nw + i for i in range(2 * nw)},
        compiler_params=pltpu.CompilerParams(has_side_effects=IN_FLIGHT))(
            *[_in_hbm(s) for s in shards], *[_in_hbm(f) for f in fulls], after)
    return [(out[i], out[nw + i], out[2 * nw + i], out[3 * nw + i]) for i in range(nw)]


def _gather_wait(ws, flight, after, *, name):
    nw = len(ws)

    def body(*refs):
        src, dst = refs[:nw], refs[nw:2 * nw]
        send, recv = refs[2 * nw:3 * nw], refs[3 * nw:4 * nw]
        x, y, c = _mesh_pos()
        for i, w in enumerate(ws):
            for f, (px, py) in enumerate(_other_chips(x, y)):
                landed = w.half_of(w.shard_of(dst[i], 2 * px + py), c)
                cp = pltpu.make_async_remote_copy(src_ref=w.half_of(src[i], c), dst_ref=landed, send_sem=send[i].at[f],
                                                  recv_sem=recv[i].at[f], device_id=(px, py, c), device_id_type=MESH)
                cp.wait_send()
                cp.wait_recv()

    shards, fulls = [fl[2] for fl in flight], [fl[3] for fl in flight]
    out = pl.pallas_call(
        body, name=name, in_specs=[HBM] * (2 * nw) + [SEM] * (2 * nw) + [ANY],
        out_specs=[HBM] * (2 * nw),
        out_shape=[pltpu.HBM(s.shape, BF16) for s in shards] + [pltpu.HBM(f.shape, BF16) for f in fulls],
        input_output_aliases={i: i for i in range(2 * nw)},
        compiler_params=pltpu.CompilerParams(has_side_effects=IN_FLIGHT))(
            *shards, *fulls, *[fl[0] for fl in flight], *[fl[1] for fl in flight], after)
    return out[:nw], out[nw:]


def _gather_finish(ws, shards, fulls, *, name):
    nw = len(ws)

    def body(*refs):
        src, dst, stage = refs[:nw], refs[3 * nw:4 * nw], refs[4 * nw:5 * nw]
        send_sems, recv_sems, load_sems, store_sems = refs[5 * nw:]
        x, y, c = _mesh_pos()
        me = 2 * x + y
        sibling = (x, y, 1 - c)
        chips = _other_chips(x, y)

        def fwd(i, w, f, half):
            px, py = chips[f]
            landed = w.half_of(w.shard_of(dst[i], 2 * px + py), half)
            return pltpu.make_async_remote_copy(src_ref=landed, dst_ref=landed, send_sem=send_sems.at[3 * i + f],
                                                recv_sem=recv_sems.at[3 * i + f], device_id=sibling,
                                                device_id_type=MESH)

        loads = [pltpu.make_async_copy(src[i], stage[i], load_sems.at[i]) for i in range(nw)]
        for cp in loads:
            cp.start()
        sends = [fwd(i, w, f, c) for i, w in enumerate(ws) for f in range(3)]
        for cp in sends:
            cp.start()
        stores = [pltpu.make_async_copy(stage[i], w.shard_of(dst[i], me), store_sems.at[i])
                  for i, w in enumerate(ws)]
        for ld, st in zip(loads, stores):
            ld.wait()
            st.start()
        for i, w in enumerate(ws):
            for f in range(3):
                fwd(i, w, f, 1 - c).wait_recv()
        for cp in sends:
            cp.wait_send()
        for cp in stores:
            cp.wait()

    out = pl.pallas_call(
        body, name=name, in_specs=[ANY] * (2 * nw), out_specs=[ANY] * (2 * nw),
        out_shape=[jax.ShapeDtypeStruct(s.shape, BF16) for s in shards]
        + [jax.ShapeDtypeStruct(f.shape, BF16) for f in fulls],
        input_output_aliases={i: i for i in range(2 * nw)},
        scratch_shapes=[pltpu.VMEM((w.L, w.ks, w.ns), BF16) for w in ws]
        + [pltpu.SemaphoreType.DMA((3 * nw,)), pltpu.SemaphoreType.DMA((3 * nw,)), pltpu.SemaphoreType.DMA((nw,)),
           pltpu.SemaphoreType.DMA((nw,))],
        compiler_params=_params(has_side_effects=True))(*shards, *fulls)
    return out[nw:]


def _split_copies(name, srcs, lands, n_sems, copies_of, *, flight=None, after=None):
    n = len(srcs)
    starting = flight is None

    def body(*refs):
        src, land = refs[:n], refs[n:2 * n]
        sems = refs[2 * n + 1:4 * n + 1] if starting else refs[2 * n:4 * n]
        for i in range(n):
            for cp in copies_of(i, src[i], land[i], sems[i], sems[n + i]):
                if starting:
                    cp.start()
                else:
                    cp.wait_send()
                    cp.wait_recv()

    thru = [pltpu.HBM(a.shape, a.dtype) for a in list(srcs) + list(lands)]
    if starting:
        out = pl.pallas_call(
            body, name=name, in_specs=[HBM] * (2 * n) + [ANY], out_specs=[SEM] * (2 * n) + [HBM] * (2 * n),
            out_shape=[pltpu.SemaphoreType.DMA((n_sems,))] * (2 * n) + thru,
            input_output_aliases={i: 2 * n + i for i in range(2 * n)},
            compiler_params=pltpu.CompilerParams(has_side_effects=IN_FLIGHT))(
                *[_in_hbm(a) for a in srcs], *[_in_hbm(a) for a in lands], after)
        return [(out[i], out[n + i], out[2 * n + i], out[3 * n + i]) for i in range(n)]
    out = pl.pallas_call(
        body, name=name, in_specs=[HBM] * (2 * n) + [SEM] * (2 * n) + [ANY], out_specs=[HBM] * (2 * n),
        out_shape=thru, input_output_aliases={i: i for i in range(2 * n)},
        compiler_params=pltpu.CompilerParams(has_side_effects=IN_FLIGHT))(
            *srcs, *lands, *[fl[0] for fl in flight], *[fl[1] for fl in flight], after)
    return out[:n], out[n:]


def _sum8(land, vec, me):
    R = vec.shape[0]

    def body(me_ref, land_ref, vec_ref, o_ref):
        acc = jnp.zeros((R, 128), F32)
        for d in range(8):
            acc = acc + jnp.where(me_ref[0] == d, vec_ref[...], land_ref[d])
        o_ref[...] = acc

    grid_spec = pltpu.PrefetchScalarGridSpec(
        num_scalar_prefetch=1, grid=(1,),
        in_specs=[pl.BlockSpec((8, R, 128), lambda i, m: (0, 0, 0)), pl.BlockSpec((R, 128), lambda i, m: (0, 0))],
        out_specs=pl.BlockSpec((R, 128), lambda i, m: (0, 0)))
    return pl.pallas_call(body, name="sum8", grid_spec=grid_spec, out_shape=jax.ShapeDtypeStruct((R, 128), F32),
                          compiler_params=_params(("arbitrary",)))(me.reshape(1), land, vec)


def _swap_copies(i, src, got, send, recv):
    x, y, c = _mesh_pos()
    return [pltpu.make_async_remote_copy(src_ref=src.at[1 - c], dst_ref=got, send_sem=send.at[0], recv_sem=recv.at[0],
                                         device_id=(x, y, 1 - c), device_id_type=MESH)]


def _gather8_copies(i, src, land, send, recv):
    x, y, c = _mesh_pos()
    me = 4 * x + 2 * y + c
    peers = [(x, y, 1 - c)] + [(px, py, pc) for px, py in _other_chips(x, y) for pc in (c, 1 - c)]
    return [pltpu.make_async_remote_copy(src_ref=src, dst_ref=land.at[me], send_sem=send.at[k], recv_sem=recv.at[k],
                                         device_id=peer, device_id_type=MESH) for k, peer in enumerate(peers)]


def _scatter_copy(src, got, send, recv, f, chip, c):
    px, py = chip
    return pltpu.make_async_remote_copy(src_ref=src.at[2 * px + py], dst_ref=got.at[f], send_sem=send.at[f],
                                        recv_sem=recv.at[f], device_id=(px, py, c), device_id_type=MESH)


def _scatter_start(sums, *, name):
    nw = len(sums)

    def body(*refs):
        src, got = refs[:nw], refs[nw:2 * nw]
        send, recv = refs[2 * nw:3 * nw], refs[3 * nw:4 * nw]
        x, y, c = _mesh_pos()
        for i in range(nw):
            for f, chip in enumerate(_other_chips(x, y)):
                _scatter_copy(src[i], got[i], send[i], recv[i], f, chip, c).start()

    lands = [lax.empty((3,) + s.shape[1:], BF16) for s in sums]
    out = pl.pallas_call(
        body, name=name, in_specs=[HBM] * (2 * nw), out_specs=[SEM] * (2 * nw) + [HBM] * (2 * nw),
        out_shape=[pltpu.SemaphoreType.DMA((3,))] * (2 * nw)
        + [pltpu.HBM(s.shape, BF16) for s in sums] + [pltpu.HBM(l.shape, BF16) for l in lands],
        input_output_aliases={i: 2 * nw + i for i in range(2 * nw)},
        compiler_params=pltpu.CompilerParams(has_side_effects=IN_FLIGHT))(
            *[_in_hbm(s) for s in sums], *[_in_hbm(l) for l in lands])
    return [(out[i], out[nw + i], out[2 * nw + i], out[3 * nw + i]) for i in range(nw)]


def _scatter_wait(flight, after):
    nw = len(flight)

    def body(*refs):
        src, got = refs[:nw], refs[nw:2 * nw]
        send, recv = refs[2 * nw:3 * nw], refs[3 * nw:4 * nw]
        x, y, c = _mesh_pos()
        for i in range(nw):
            for f, chip in enumerate(_other_chips(x, y)):
                cp = _scatter_copy(src[i], got[i], send[i], recv[i], f, chip, c)
                cp.wait_send()
                cp.wait_recv()

    sums, lands = [fl[2] for fl in flight], [fl[3] for fl in flight]
    out = pl.pallas_call(
        body, name="scatter_wait", in_specs=[HBM] * (2 * nw) + [SEM] * (2 * nw) + [ANY], out_specs=[HBM] * (2 * nw),
        out_shape=[pltpu.HBM(s.shape, BF16) for s in sums] + [pltpu.HBM(l.shape, BF16) for l in lands],
        input_output_aliases={i: i for i in range(2 * nw)},
        compiler_params=pltpu.CompilerParams(has_side_effects=IN_FLIGHT))(
            *sums, *lands, *[fl[0] for fl in flight], *[fl[1] for fl in flight], after)
    return out[:nw], out[nw:]


def _join_halves(ws, shards):
    nw = len(ws)

    def body(*refs):
        buf = refs[nw:2 * nw]
        send_sems, recv_sems = refs[2 * nw:]
        x, y, c = _mesh_pos()
        sibling = (x, y, 1 - c)

        def copy(i, w, half):
            region = w.half_of(buf[i], half)
            return pltpu.make_async_remote_copy(src_ref=region, dst_ref=region, send_sem=send_sems.at[i],
                                                recv_sem=recv_sems.at[i], device_id=sibling, device_id_type=MESH)

        sends = [copy(i, w, c) for i, w in enumerate(ws)]
        for cp in sends:
            cp.start()
        for i, w in enumerate(ws):
            copy(i, w, 1 - c).wait_recv()
        for cp in sends:
            cp.wait_send()

    return pl.pallas_call(
        body, name="join_halves", in_specs=[ANY] * nw, out_specs=[ANY] * nw,
        out_shape=[jax.ShapeDtypeStruct((w.L, w.ks, w.ns), F32) for w in ws],
        input_output_aliases={i: i for i in range(nw)},
        scratch_shapes=[pltpu.SemaphoreType.DMA((nw,)), pltpu.SemaphoreType.DMA((nw,))],
        compiler_params=_params(has_side_effects=True))(*shards)


def _allreduce_small(vec):
    R = vec.shape[0]

    def body(x_ref, o_ref, buf, send_sems, recv_sems):
        x, y, c = _mesh_pos()
        me, sibling = (x, y, c), (x, y, 1 - c)
        chips = _other_chips(x, y)

        def slot(px, py, pc):
            return buf.at[4 * px + 2 * py + pc]

        def copy(k, block, to, src=None):
            return pltpu.make_async_remote_copy(src_ref=slot(*block) if src is None else src, dst_ref=slot(*block),
                                                send_sem=send_sems.at[k], recv_sem=recv_sems.at[k], device_id=to,
                                                device_id_type=MESH)

        first = [copy(0, me, sibling, src=x_ref)] + [copy(1 + f, me, (*chip, c), src=x_ref)
                                                     for f, chip in enumerate(chips)]
        for cp in first:
            cp.start()
        passed = [copy(4 + f, (*chip, c), sibling) for f, chip in enumerate(chips)]
        for f, chip in enumerate(chips):
            copy(1 + f, (*chip, c), me).wait_recv()
            passed[f].start()
        copy(0, sibling, me).wait_recv()
        for f, chip in enumerate(chips):
            copy(4 + f, (*chip, 1 - c), me).wait_recv()
        for cp in first + passed:
            cp.wait_send()
        slot(*me)[...] = x_ref[...]
        acc = buf[0]
        for d in range(1, 8):
            acc = acc + buf[d]
        o_ref[...] = acc

    return pl.pallas_call(
        body, name="allreduce_small", in_specs=[pl.BlockSpec(memory_space=pltpu.VMEM)],
        out_specs=pl.BlockSpec(memory_space=pltpu.VMEM), out_shape=jax.ShapeDtypeStruct((R, 128), F32),
        scratch_shapes=[pltpu.VMEM((8, R, 128), F32), pltpu.SemaphoreType.DMA((7,)), pltpu.SemaphoreType.DMA((7,))],
        compiler_params=_params())(vec)


def _pack(parts):
    flat = jnp.concatenate([p.reshape(-1).astype(F32) for p in parts])
    n = flat.shape[0]
    pad = (-n) % (64 * 128)
    return jnp.pad(flat, (0, pad)).reshape(-1, 128)


def _unpack(vec, shapes):
    flat = vec.reshape(-1)
    out, off = [], 0
    for s in shapes:
        n = int(np.prod(s))
        out.append(flat[off:off + n].reshape(s))
        off += n
    return out


def kernel(x, a_norm_g, a_w_in, a_v_norm_g, a_w_s, a_b_s, a_w_out, kv_norm_g, w_kv, b_norm_g, b_w_q, b_rel_bias, b_w_o, f_norm_g, f_w_in, f_conv_w, f_conv_b, f_w_down, final_norm_g, loss_target, m_a_norm_g, m_a_w_in, m_a_v_norm_g, m_a_w_s, m_a_b_s, m_a_w_out, m_kv_norm_g, m_w_kv, m_b_norm_g, m_b_w_q, m_b_rel_bias, m_b_w_o, m_f_norm_g, m_f_w_in, m_f_conv_w, m_f_conv_b, m_f_w_down, m_final_norm_g, v_a_norm_g, v_a_w_in, v_a_v_norm_g, v_a_w_s, v_a_b_s, v_a_w_out, v_kv_norm_g, v_w_kv, v_b_norm_g, v_b_w_q, v_b_rel_bias, v_b_w_o, v_f_norm_g, v_f_w_in, v_f_conv_w, v_f_conv_b, v_f_w_down, v_final_norm_g):
    B, S, D = x.shape
    T = B * S
    xi, yi, ci = lax.axis_index("x"), lax.axis_index("y"), lax.axis_index("c")
    j_me = (2 * xi + yi).astype(jnp.int32)
    core = ci.astype(jnp.int32)
    pos = jnp.stack([j_me, core])

    w_shards = {"a_w_in": (a_w_in, False), "a_w_out": (a_w_out, True), "w_kv": (w_kv[None], False),
                "b_w_q": (b_w_q, True), "b_w_o": (b_w_o, True), "f_w_in": (f_w_in, False), "f_w_down": (f_w_down, True)}
    names = list(w_shards)
    ws = [_W(n, w_shards[n][0], w_shards[n][1]) for n in names]
    g_shards = {"a_w_in": (a_w_in, False), "a_w_out": (a_w_out, True),
                "f_w_in0": (f_w_in[0:1], False), "f_w_down0": (f_w_down[0:1], True),
                "w_kv": (w_kv[None], False), "b_w_q": (b_w_q, True), "b_w_o": (b_w_o, True),
                "f_w_in1": (f_w_in[1:2], False), "f_w_down1": (f_w_down[1:2], True)}
    g_names = list(g_shards)
    g_ws = {n: _W(n, *g_shards[n]) for n in g_names}

    Wd = a_w_in.shape[1]
    GW = a_v_norm_g.shape[1] * N_CHIPS
    F2 = f_conv_w.shape[2] * N_CHIPS
    Fh = F2 // 2
    nsd, nsg, nsf = a_norm_g.shape[1], a_v_norm_g.shape[1], f_conv_w.shape[2]
    own = (ci == 0).astype(F32)
    place = lambda sh, width, n: lax.dynamic_update_slice_in_dim(
        jnp.zeros(sh.shape[:-1] + (width,), F32), sh * own, j_me * n, axis=sh.ndim - 1)
    gathered = _allreduce_small(_pack([place(a_norm_g, Wd, nsd), place(a_v_norm_g, GW, nsg),
                                       place(f_conv_w, F2, nsf)]))
    a_g, a_vg, conv_w = _unpack(gathered, [(1, Wd), (1, GW), (2, 3, F2)])

    flight = dict(zip(g_names, _gather_start([g_ws[n] for n in g_names],
                                             [g_shards[n][0].astype(BF16) for n in g_names], gathered)))
    full = {}

    def tied(x, flight):
        x, thru = lax.optimization_barrier((x, flight[0][2]))
        return x, [flight[0][:2] + (thru,) + flight[0][3:]] + flight[1:]

    def arrive(group, after, tag):
        gw = [g_ws[n] for n in group]
        sh, fu = _gather_wait(gw, [flight[n] for n in group], after, name=f"gather_wait_{tag}")
        full.update(zip(group, _gather_finish(gw, sh, fu, name=f"gather_finish_{tag}")))
    conv_w2 = conv_w.reshape(2, 3, 2, Fh).transpose(0, 2, 1, 3)
    conv_b2 = f_conv_b.reshape(2, 2, Fh)

    h0 = x.reshape(T, D)
    target = loss_target.reshape(T, D)
    bs_tile = jnp.repeat(a_b_s[0].T, GROUP_DIM, axis=1)
    ws_a = a_w_s[0]
    scale = HEAD_DIM ** -0.5
    HD = b_w_q.shape[2]
    H = HD // HEAD_DIM
    n_rel = b_rel_bias.shape[-1]
    frow, (flight["a_w_in"],) = tied(b_rel_bias[0][:, _bias_index()].reshape(H, 1, F_LEN), [flight["a_w_in"]])
    bias = _bias_expand(frow)

    def ffn_fwd(h, l, loss=None):
        yff, a, c, n = _ffn_in_conv(h, full[f"f_w_in{l}"], f_norm_g[l], conv_w2[l], conv_b2[l], S, name=f"ffn{l}_in")
        return _mm(yff, full[f"f_w_down{l}"], layer=0, res=h, loss=loss, name=f"ffn{l}_down"), (a, c, n, yff)

    arrive(["a_w_in", "a_w_out"], bias, "a")
    zp, n_a = _mm(h0, full["a_w_in"], layer=0, norm_g=a_g[0], out_dtype=BF16, emit_norm=True, name="a_in")
    out_a = _gate_fwd(zp, a_vg, ws_a, bs_tile)
    h1 = _mm(out_a, full["a_w_out"], layer=0, res=h0, name="a_out")
    arrive(["f_w_in0", "f_w_down0"], h1, "f0")
    h2, saved0 = ffn_fwd(h1, 0)
    arrive(["w_kv", "b_w_q", "b_w_o"], h2, "b")
    arrive(["f_w_in1", "f_w_down1"], h2, "f1")
    kv, n_kv = _mm(h2, full["w_kv"], layer=0, norm_g=kv_norm_g, out_dtype=BF16, split_out=True, emit_norm=True,
                   name="kv")
    q, n_q = _mm(h2, full["b_w_q"], layer=0, norm_g=b_norm_g[0], scale=scale, out_dtype=BF16, emit_norm=True,
                 name="q")
    kv4, q3 = kv.reshape(2, B, S, HD), q.reshape(B, S, HD)
    o = _attn_fwd(q3, kv4, bias, B, S).reshape(T, HD)
    h3 = _mm(o, full["b_w_o"], layer=0, res=h2, name="attn_out")
    (dh, loss8, dg_final), saved1 = ffn_fwd(h3, 1, loss=(final_norm_g, target))

    units = {}

    in_flight = {}

    def swap_start(group, tag, carry):
        us = [units[n] for n in group]
        lands = [lax.empty(u.shape[1:], BF16) for u in us]
        carry, flight = tied(carry, _split_copies(f"swap_start_{tag}", us, lands, 1, _swap_copies, after=us[0]))
        return (group, tag, flight), carry

    def reduce_start(swap, after):
        group, tag, flight = swap
        us, got = _split_copies(f"swap_wait_{tag}", [fl[2] for fl in flight], [fl[3] for fl in flight], 1,
                                _swap_copies, flight=flight, after=after)
        sums = [_add_pair(u, g_, core, name=f"pair_{n}") for n, u, g_ in zip(group, us, got)]
        after, flight = tied(after, _scatter_start(sums, name=f"scatter_start_{tag}"))
        in_flight.update(zip(group, flight))
        return after

    def ffn_bwd(dh, h, saved, l, early):
        a, c, n, yff = saved
        units[f"f_w_down{l}"] = _mm_tn(yff, dh, rows_are_shards=True, name=f"ffn{l}_down_dw")
        dh_in = dh
        if early:
            sw, dh_in = swap_start([f"f_w_down{l}"], f"fd{l}", dh)
        dyff = _mm(dh_in, full[f"f_w_down{l}"], layer=0, trans_w=True, out_dtype=BF16, name=f"ffn{l}_down_dx")
        if early:
            dyff = reduce_start(sw, dyff)
        da, dcw, dcb = _conv_bwd(a, c, dyff, conv_w2[l], S)
        units[f"f_w_in{l}"] = _mm_tn(n, da, split_y=True, name=f"ffn{l}_in_dw")
        sw, da = swap_start([f"f_w_in{l}"] if early else [f"f_w_down{l}", f"f_w_in{l}"], f"f{l}", da)
        dh, dg = _mm(da, full[f"f_w_in{l}"], layer=0, trans_w=True, split_x=True, bwd=(h, f_norm_g[l], dh), tm=256,
                     name=f"ffn{l}_in_dx")
        return reduce_start(sw, dh), dg, dcw, dcb

    dh, dg_f1, dcw1, dcb1 = ffn_bwd(dh, h3, saved1, 1, False)
    do = _mm(dh, full["b_w_o"], layer=0, trans_w=True, out_dtype=BF16, name="attn_out_dx")
    units["b_w_o"] = _mm_tn(o, dh, rows_are_shards=True, name="b_w_o_dw")
    dq, dkv, dbias = _attn_bwd(q3, kv4, bias, do.reshape(B, S, HD), B, S)
    dq, d_rel = lax.optimization_barrier((dq, _bias_reduce(dbias, n_rel)))
    d_rel = d_rel.reshape(1, H, n_rel)
    dq, dkv = dq.reshape(T, HD), dkv.reshape(2, T, HD)
    units["b_w_q"] = _mm_tn(n_q, dq, rows_are_shards=True, name="b_w_q_dw")
    dh, dg_b = _mm(dq, full["b_w_q"], layer=0, trans_w=True, bwd=(h2, b_norm_g[0], dh), name="q_dx")
    units["w_kv"] = _mm_tn(n_kv, dkv, split_y=True, name="w_kv_dw")
    sw, dkv = swap_start(["b_w_o", "b_w_q", "w_kv"], "b", dkv)
    dh, dg_kv = _mm(dkv, full["w_kv"], layer=0, trans_w=True, split_x=True, bwd=(h2, kv_norm_g, dh), name="kv_dx")
    dh = reduce_start(sw, dh)
    dh, dg_f0, dcw0, dcb0 = ffn_bwd(dh, h1, saved0, 0, True)
    units["a_w_out"] = _mm_tn(out_a, dh, rows_are_shards=True, name="a_w_out_dw")
    sw, dh_in = swap_start(["a_w_out"], "ao", dh)
    d_out = _mm(dh_in, full["a_w_out"], layer=0, trans_w=True, out_dtype=BF16, name="a_out_dx")
    d_out = reduce_start(sw, d_out)
    dzp, dws, dbs, dgv = _gate_bwd(zp, d_out, a_vg, ws_a, bs_tile)
    units["a_w_in"] = _mm_tn(n_a, dzp, name="a_w_in_dw")
    sw, dzp_in = swap_start(["a_w_in"], "ai", dzp)
    grad_x, dg_a = _mm(dzp_in, full["a_w_in"], layer=0, trans_w=True, bwd=(h0, a_g[0], dh), name="a_in_dx")
    grad_x = reduce_start(sw, grad_x)

    to_flat = lambda d: d.transpose(1, 0, 2).reshape(3, F2)
    small = {"a_norm_g": dg_a, "a_v_norm_g": dgv, "a_w_s": dws[None], "a_b_s": dbs[None], "kv_norm_g": dg_kv[0],
             "b_norm_g": dg_b, "b_rel_bias": d_rel, "f_norm_g": jnp.concatenate([dg_f0, dg_f1], axis=0),
             "f_conv_w": jnp.stack([to_flat(dcw0), to_flat(dcw1)]),
             "f_conv_b": jnp.stack([dcb0.reshape(F2), dcb1.reshape(F2)]), "final_norm_g": dg_final[0]}
    snames = list(small)
    small_vec = _pack([small[n] for n in snames] + [loss8[0:1, 0:1]])
    grad_x, small_flight = tied(grad_x, _split_copies("small_start", [small_vec],
                                                      [lax.empty((8,) + small_vec.shape, F32)], 7, _gather8_copies,
                                                      after=small_vec))

    sums, recv = _scatter_wait([in_flight[n] for n in g_names], grad_x)
    sums, recv = dict(zip(g_names, sums)), dict(zip(g_names, recv))
    halves = []
    for n, w in zip(names, ws):
        if w.L == 1:
            halves.append(_sum_chips(w, sums[n], recv[n], pos, name=f"chips_{n}"))
        else:
            first = _sum_chips(w, sums[n + "0"], recv[n + "0"], pos, name=f"chips_{n}0")
            halves.append(_sum_chips(w, sums[n + "1"], recv[n + "1"], pos, layer=1, into=first, name=f"chips_{n}1"))
    g_big = dict(zip(names, _join_halves(ws, halves)))
    g_big["w_kv"] = g_big["w_kv"][0]

    given = dict(a_norm_g=(a_norm_g, m_a_norm_g, v_a_norm_g), a_w_in=(a_w_in, m_a_w_in, v_a_w_in),
                 a_v_norm_g=(a_v_norm_g, m_a_v_norm_g, v_a_v_norm_g), a_w_s=(a_w_s, m_a_w_s, v_a_w_s),
                 a_b_s=(a_b_s, m_a_b_s, v_a_b_s), a_w_out=(a_w_out, m_a_w_out, v_a_w_out),
                 kv_norm_g=(kv_norm_g, m_kv_norm_g, v_kv_norm_g), w_kv=(w_kv, m_w_kv, v_w_kv),
                 b_norm_g=(b_norm_g, m_b_norm_g, v_b_norm_g), b_w_q=(b_w_q, m_b_w_q, v_b_w_q),
                 b_rel_bias=(b_rel_bias, m_b_rel_bias, v_b_rel_bias), b_w_o=(b_w_o, m_b_w_o, v_b_w_o),
                 f_norm_g=(f_norm_g, m_f_norm_g, v_f_norm_g), f_w_in=(f_w_in, m_f_w_in, v_f_w_in),
                 f_conv_w=(f_conv_w, m_f_conv_w, v_f_conv_w), f_conv_b=(f_conv_b, m_f_conv_b, v_f_conv_b),
                 f_w_down=(f_w_down, m_f_w_down, v_f_w_down), final_norm_g=(final_norm_g, m_final_norm_g, v_final_norm_g))
    order = list(given)
    grads, deltas, new_m, new_v = {}, {}, {}, {}
    for n in names:
        w_, m_, v_ = given[n]
        g_ = g_big[n]
        C = w_.shape[-1]
        d2, m2, v2 = _adamw(w_.reshape(-1, C), g_.reshape(-1, C), m_.reshape(-1, C), v_.reshape(-1, C),
                            name=f"adamw_{n}")
        grads[n], deltas[n], new_m[n], new_v[n] = g_.reshape(w_.shape), d2.reshape(w_.shape), m2.reshape(w_.shape), \
            v2.reshape(w_.shape)
    vecs, lands = _split_copies("small_wait", [small_flight[0][2]], [small_flight[0][3]], 7, _gather8_copies,
                                flight=small_flight, after=deltas[names[-1]])
    red = _sum8(lands[0], vecs[0], (4 * xi + 2 * yi + ci).astype(jnp.int32))
    parts = _unpack(red, [small[n].shape for n in snames] + [(1,)])
    g_small = dict(zip(snames, parts[:-1]))
    loss = parts[-1][0]
    g_small["a_norm_g"] = lax.dynamic_slice_in_dim(g_small["a_norm_g"], j_me * nsd, nsd, axis=1)
    g_small["a_v_norm_g"] = lax.dynamic_slice_in_dim(g_small["a_v_norm_g"], j_me * nsg, nsg, axis=1)
    g_small["f_conv_w"] = lax.dynamic_slice_in_dim(g_small["f_conv_w"], j_me * nsf, nsf, axis=2)

    sm = [n for n in order if n not in names]
    d2, m2, v2 = _adamw(_pack([given[n][0] for n in sm]), _pack([g_small[n].reshape(given[n][0].shape) for n in sm]),
                        _pack([given[n][1] for n in sm]), _pack([given[n][2] for n in sm]), name="adamw_small")
    shapes = [given[n][0].shape for n in sm]
    for n, d_, m_, v_ in zip(sm, _unpack(d2, shapes), _unpack(m2, shapes), _unpack(v2, shapes)):
        grads[n], deltas[n], new_m[n], new_v[n] = g_small[n].reshape(given[n][0].shape), d_, m_, v_

    return (loss, grad_x.reshape(B, S, D), *[grads[n] for n in order], *[deltas[n] for n in order],
            *[new_m[n] for n in order], *[new_v[n] for n in order])
```

```python
import functools
import math

import numpy as np
import jax
import jax.numpy as jnp
from jax import lax
from jax.experimental import pallas as pl
from jax.experimental.pallas import tpu as pltpu

F32 = jnp.float32
BF16 = jnp.bfloat16
MESH = pl.DeviceIdType.MESH

EPS = 1e-6
NEG_INF = -1e30
CHUNK = 64
GMLP_BLOCK = 128
GROUP_DIM = 128
HEAD_DIM = 64
LEFT_CHUNKS = 8
PAD = LEFT_CHUNKS * CHUNK
REL_CLIP = 128
Q_BLOCK = 256
K_SPAN = PAD + Q_BLOCK
F_LEN = K_SPAN + Q_BLOCK
HEADS_PER_STEP = 4
N_CHIPS = 4

ADAM_LR = 0.001
ADAM_B1 = 0.9
ADAM_B2 = 0.999
ADAM_EPS = 1e-08
ADAM_WD = 0.01
ADAM_STEP = 10

VMEM_LIMIT = 56 * 1024 * 1024


def _params(sem=None, **kw):
    if sem is not None:
        kw["dimension_semantics"] = sem
    return pltpu.CompilerParams(vmem_limit_bytes=VMEM_LIMIT, **kw)


def _rms(xf):
    r = lax.rsqrt(jnp.mean(xf * xf, axis=-1, keepdims=True) + EPS)
    return xf * r, r


def _gelu(x, with_grad=False):
    c = math.sqrt(2.0 / math.pi)
    x2 = x * x
    t = jnp.tanh(c * x * (1.0 + 0.044715 * x2))
    half = 0.5 * (1.0 + t)
    if not with_grad:
        return x * half
    return x * half, half + 0.5 * x * (1.0 - t * t) * c * (1.0 + 3.0 * 0.044715 * x2)


def _col_tile(n):
    if n <= 1024:
        return n
    for t in (1408, 1024, 512):
        if n % t == 0:
            return t
    raise ValueError(n)


def _row_tile(t, want):
    while t % want:
        want //= 2
    return want


def _mm(x, w, *, name, layer=None, trans_w=False, norm_g=None, res=None, scale=None, out_dtype=F32, bwd=None,
        split_out=False, split_x=False, emit_norm=False, loss=None, tm=512):
    T = x.shape[-2]
    K = 2 * x.shape[-1] if split_x else x.shape[-1]
    N = w.shape[-2] if trans_w else w.shape[-1]
    tn = N
    tm = _row_tile(T, 256 if N > 4096 else tm)
    nn, nm = N // tn, T // tm
    has_norm, has_res, has_bwd, has_loss = norm_g is not None, res is not None, bwd is not None, loss is not None
    dims = (((1,), (1,)), ((), ())) if trans_w else (((1,), (0,)), ((), ()))

    def body(*refs):
        it = iter(refs)
        x_ref, w_ref = next(it), next(it)
        g_ref = next(it) if has_norm else None
        res_ref = next(it) if has_res else None
        if has_bwd:
            h_ref, bg_ref, dh_ref = next(it), next(it), next(it)
        if has_loss:
            lg_ref, t_ref = next(it), next(it)
        o_ref = next(it)
        if split_x:
            kh = K // 2
            acc = lax.dot_general(x_ref[0].astype(BF16), w_ref[:, :kh] if trans_w else w_ref[:kh, :], dims,
                                  preferred_element_type=F32)
            acc = acc + lax.dot_general(x_ref[1].astype(BF16), w_ref[:, kh:] if trans_w else w_ref[kh:, :], dims,
                                        preferred_element_type=F32)
        else:
            xv = x_ref[...]
            if has_norm:
                xv = _rms(xv.astype(F32))[0] * g_ref[...]
            xb = xv.astype(BF16)
            if emit_norm:
                refs[-1][...] = xb
            acc = lax.dot_general(xb, w_ref[...], dims, preferred_element_type=F32)
        if scale is not None:
            acc = acc * scale
        if has_res:
            acc = acc + res_ref[...]
        if has_bwd:
            dg_ref = next(it)
            n, r = _rms(h_ref[...])

            @pl.when(pl.program_id(1) == 0)
            def _():
                dg_ref[...] = jnp.zeros_like(dg_ref)

            dg_ref[...] += jnp.sum(acc * n, axis=0, keepdims=True)
            t = acc * bg_ref[...]
            o_ref[...] = dh_ref[...] + r * (t - n * jnp.mean(t * n, axis=-1, keepdims=True))
        elif has_loss:
            loss_ref, dg_ref = refs[-2], refs[-1]

            @pl.when(pl.program_id(1) == 0)
            def _():
                loss_ref[...] = jnp.zeros_like(loss_ref)
                dg_ref[...] = jnp.zeros_like(dg_ref)

            n, r = _rms(acc)
            g = lg_ref[...]
            e = n * g - t_ref[...]
            loss_ref[...] += 0.5 * jnp.sum(jnp.mean(e * e, axis=-1, keepdims=True), axis=0, keepdims=True)
            dy = e * (1.0 / N)
            dg_ref[...] += jnp.sum(dy * n, axis=0, keepdims=True)
            t = dy * g
            o_ref[...] = r * (t - n * jnp.mean(t * n, axis=-1, keepdims=True))
        elif split_out:
            o_ref[0] = acc[:, :N // 2].astype(out_dtype)
            o_ref[1] = acc[:, N // 2:].astype(out_dtype)
        else:
            o_ref[...] = acc.astype(out_dtype)

    lead = () if layer is None else (None,)
    lidx = () if layer is None else (layer,)
    ins = [x, w]
    xspec = (pl.BlockSpec((2, tm, K // 2), lambda n, m: (0, m, 0)) if split_x
             else pl.BlockSpec((tm, K), lambda n, m: (m, 0)))
    once = pl.Buffered(1)
    wspec = (pl.BlockSpec(lead + (tn, K), lambda n, m: lidx + (n, 0), pipeline_mode=once) if trans_w
             else pl.BlockSpec(lead + (K, tn), lambda n, m: lidx + (0, n), pipeline_mode=once))
    in_specs = [xspec, wspec]
    if has_norm:
        ins.append(norm_g.reshape(1, K))
        in_specs.append(pl.BlockSpec((1, K), lambda n, m: (0, 0)))
    if has_res:
        ins.append(res)
        in_specs.append(pl.BlockSpec((tm, tn), lambda n, m: (m, n)))
    if split_out:
        out_shape = [jax.ShapeDtypeStruct((2, T, N // 2), out_dtype)]
        out_specs = [pl.BlockSpec((2, tm, N // 2), lambda n, m: (0, m, 0))]
    else:
        out_shape = [jax.ShapeDtypeStruct((T, N), F32 if has_bwd else out_dtype)]
        out_specs = [pl.BlockSpec((tm, tn), lambda n, m: (m, n))]
    if has_bwd:
        h, g, dh = bwd
        ins += [h, g.reshape(1, N), dh]
        in_specs += [pl.BlockSpec((tm, N), lambda n, m: (m, 0)), pl.BlockSpec((1, N), lambda n, m: (0, 0)),
                     pl.BlockSpec((tm, N), lambda n, m: (m, 0))]
        out_shape.append(jax.ShapeDtypeStruct((1, N), F32))
        out_specs.append(pl.BlockSpec((1, N), lambda n, m: (0, 0)))
    if emit_norm:
        out_shape.append(jax.ShapeDtypeStruct((T, K), BF16))
        out_specs.append(pl.BlockSpec((tm, K), lambda n, m: (m, 0)))
    if has_loss:
        ins += [loss[0].reshape(1, N), loss[1]]
        in_specs += [pl.BlockSpec((1, N), lambda n, m: (0, 0)), pl.BlockSpec((tm, N), lambda n, m: (m, 0))]
        out_shape += [jax.ShapeDtypeStruct((8, 128), F32), jax.ShapeDtypeStruct((1, N), F32)]
        out_specs += [pl.BlockSpec((8, 128), lambda n, m: (0, 0)), pl.BlockSpec((1, N), lambda n, m: (0, 0))]
    out = pl.pallas_call(body, name=name, grid=(nn, nm), in_specs=in_specs, out_specs=out_specs, out_shape=out_shape,
                         compiler_params=_params(("arbitrary", "arbitrary")))(*ins)
    return out if has_bwd or emit_norm or has_loss else out[0]


def _mm_tn(x, dy, *, name, rows_are_shards=False, split_y=False, tt=512):
    T, K = x.shape
    N = 2 * dy.shape[-1] if split_y else dy.shape[-1]
    R, C = (K // N_CHIPS, N // 2) if rows_are_shards else (K // 2, N // N_CHIPS)
    nn = 2 if split_y else 1
    tn = N // nn
    per = N_CHIPS // nn
    assert not (rows_are_shards and split_y)
    tt = _row_tile(T, tt)
    nt = T // tt

    def body(x_ref, y_ref, o_ref, acc_ref):
        t = pl.program_id(1)

        @pl.when(t == 0)
        def _():
            acc_ref[...] = jnp.zeros_like(acc_ref)

        acc_ref[...] += lax.dot_general(x_ref[...], y_ref[...].astype(BF16), (((0,), (0,)), ((), ())),
                                        preferred_element_type=F32)

        @pl.when(t == nt - 1)
        def _():
            if rows_are_shards:
                for h in range(2):
                    o_ref[h] = acc_ref[:, h * C:(h + 1) * C].astype(BF16).reshape(N_CHIPS, R, C)
            else:
                for j in range(per):
                    o_ref[:, j] = acc_ref[:, j * C:(j + 1) * C].astype(BF16).reshape(2, R, C)

    if split_y:
        yspec = pl.BlockSpec((None, tt, tn), lambda n, t: (n, t, 0))
    else:
        yspec = pl.BlockSpec((tt, tn), lambda n, t: (t, 0))
    if rows_are_shards:
        out_spec = pl.BlockSpec((2, N_CHIPS, R, C), lambda n, t: (0, 0, 0, 0))
    else:
        out_spec = pl.BlockSpec((2, per, R, C), lambda n, t: (0, n, 0, 0))
    return pl.pallas_call(body, name=name, grid=(nn, nt),
                          in_specs=[pl.BlockSpec((tt, K), lambda n, t: (t, 0)), yspec], out_specs=out_spec,
                          out_shape=jax.ShapeDtypeStruct((2, N_CHIPS, R, C), BF16),
                          scratch_shapes=[pltpu.VMEM((K, tn), F32)],
                          compiler_params=_params(("arbitrary", "arbitrary")))(x, dy)


def _chunk_mask():
    i = lax.broadcasted_iota(jnp.int32, (GMLP_BLOCK, GMLP_BLOCK), 0) // CHUNK
    j = lax.broadcasted_iota(jnp.int32, (GMLP_BLOCK, GMLP_BLOCK), 1) // CHUNK
    return i >= j


def _gate_fwd(zp, gv, ws, bs_tile, *, tm=256):
    T, W2 = zp.shape
    W = W2 // 2
    G = W // GROUP_DIM
    tm = _row_tile(T, tm)

    def body(zp_ref, gv_ref, ws_ref, bs_ref, o_ref):
        z = _gelu(zp_ref[...].astype(F32))
        u, v = z[:, :W], z[:, W:]
        vn = _rms(v)[0] * gv_ref[...]
        mask = _chunk_mask()
        for g in range(G):
            cs = slice(g * GROUP_DIM, (g + 1) * GROUP_DIM)
            wg = jnp.where(mask, ws_ref[g], 0.0).astype(BF16)
            for b in range(tm // GMLP_BLOCK):
                rs = slice(b * GMLP_BLOCK, (b + 1) * GMLP_BLOCK)
                s = jnp.dot(wg, vn[rs, cs].astype(BF16), preferred_element_type=F32) + bs_ref[:, cs]
                o_ref[rs, cs] = (u[rs, cs] * s).astype(BF16)

    return pl.pallas_call(
        body, name="gate_fwd", grid=(T // tm,),
        in_specs=[pl.BlockSpec((tm, W2), lambda i: (i, 0)), pl.BlockSpec((1, W), lambda i: (0, 0)),
                  pl.BlockSpec((G, GMLP_BLOCK, GMLP_BLOCK), lambda i: (0, 0, 0)),
                  pl.BlockSpec((GMLP_BLOCK, W), lambda i: (0, 0))],
        out_specs=pl.BlockSpec((tm, W), lambda i: (i, 0)), out_shape=jax.ShapeDtypeStruct((T, W), BF16),
        compiler_params=_params(("arbitrary",)))(zp, gv, ws, bs_tile)


def _gate_bwd(zp, d_out, gv, ws, bs_tile, *, tm=256):
    T, W2 = zp.shape
    W = W2 // 2
    G = W // GROUP_DIM
    tm = _row_tile(T, tm)
    nm = T // tm

    def body(zp_ref, do_ref, gv_ref, ws_ref, bs_ref, dzp_ref, dws_ref, dbs_ref, dgv_ref, du_scr, dvn_scr, dsum_scr):
        i = pl.program_id(0)

        @pl.when(i == 0)
        def _():
            dws_ref[...] = jnp.zeros_like(dws_ref)
            dgv_ref[...] = jnp.zeros_like(dgv_ref)
            dsum_scr[...] = jnp.zeros_like(dsum_scr)

        zp = zp_ref[...].astype(F32)
        z, dz = _gelu(zp, with_grad=True)
        u, v = z[:, :W], z[:, W:]
        n, r = _rms(v)
        gv = gv_ref[...]
        vn = n * gv
        d_out = do_ref[...].astype(F32)
        mask = _chunk_mask()
        for g in range(G):
            cs = slice(g * GROUP_DIM, (g + 1) * GROUP_DIM)
            wg = jnp.where(mask, ws_ref[g], 0.0).astype(BF16)
            dw = jnp.zeros((GMLP_BLOCK, GMLP_BLOCK), F32)
            for b in range(tm // GMLP_BLOCK):
                rs = slice(b * GMLP_BLOCK, (b + 1) * GMLP_BLOCK)
                vb = vn[rs, cs].astype(BF16)
                s = jnp.dot(wg, vb, preferred_element_type=F32) + bs_ref[:, cs]
                du_scr[rs, cs] = d_out[rs, cs] * s
                ds = d_out[rs, cs] * u[rs, cs]
                dsb = ds.astype(BF16)
                dvn_scr[rs, cs] = lax.dot_general(wg, dsb, (((0,), (0,)), ((), ())), preferred_element_type=F32)
                dw = dw + lax.dot_general(dsb, vb, (((1,), (1,)), ((), ())), preferred_element_type=F32)
                dsum_scr[:, cs] += ds
            dws_ref[g] += jnp.where(mask, dw, 0.0)
        dvn = dvn_scr[...]
        dgv_ref[...] += jnp.sum(dvn * n, axis=0, keepdims=True)
        t = dvn * gv
        dv = r * (t - n * jnp.mean(t * n, axis=-1, keepdims=True))
        dzp_ref[:, :W] = (du_scr[...] * dz[:, :W]).astype(BF16)
        dzp_ref[:, W:] = (dv * dz[:, W:]).astype(BF16)

        @pl.when(i == nm - 1)
        def _():
            sel = (lax.broadcasted_iota(jnp.int32, (G, W), 1) // GROUP_DIM
                   == lax.broadcasted_iota(jnp.int32, (G, W), 0)).astype(F32)
            dbs_ref[...] = lax.dot_general(sel, dsum_scr[...], (((1,), (1,)), ((), ())),
                                           precision=lax.Precision.HIGHEST, preferred_element_type=F32)

    return pl.pallas_call(
        body, name="gate_bwd", grid=(nm,),
        in_specs=[pl.BlockSpec((tm, W2), lambda i: (i, 0)), pl.BlockSpec((tm, W), lambda i: (i, 0)),
                  pl.BlockSpec((1, W), lambda i: (0, 0)),
                  pl.BlockSpec((G, GMLP_BLOCK, GMLP_BLOCK), lambda i: (0, 0, 0)),
                  pl.BlockSpec((GMLP_BLOCK, W), lambda i: (0, 0))],
        out_specs=[pl.BlockSpec((tm, W2), lambda i: (i, 0)),
                   pl.BlockSpec((G, GMLP_BLOCK, GMLP_BLOCK), lambda i: (0, 0, 0)),
                   pl.BlockSpec((G, GMLP_BLOCK), lambda i: (0, 0)), pl.BlockSpec((1, W), lambda i: (0, 0))],
        out_shape=[jax.ShapeDtypeStruct((T, W2), BF16), jax.ShapeDtypeStruct((G, GMLP_BLOCK, GMLP_BLOCK), F32),
                   jax.ShapeDtypeStruct((G, GMLP_BLOCK), F32), jax.ShapeDtypeStruct((1, W), F32)],
        scratch_shapes=[pltpu.VMEM((tm, W), F32), pltpu.VMEM((tm, W), F32), pltpu.VMEM((GMLP_BLOCK, W), F32)],
        compiler_params=_params(("arbitrary",)))(zp, d_out, gv, ws, bs_tile)


HALO = 16


def _taps(ext, w, b):
    return w[2:3] * ext[HALO:] + w[1:2] * pltpu.roll(ext, 1, 0)[HALO:] + w[0:1] * pltpu.roll(ext, 2, 0)[HALO:] + b


def _ffn_in_conv(h, w, g, cw, cb, S, *, name, tm=256):
    T, D = h.shape
    F = w.shape[-1] // 2
    tc = _col_tile(F)
    tm = _row_tile(S, tm)

    def body(h_ref, w_ref, g_ref, cw_ref, cb_ref, y_ref, a_ref, c_ref, n_ref, tail):
        first = (pl.program_id(0) * tm) % S == 0
        nb = (_rms(h_ref[...])[0] * g_ref[...]).astype(BF16)
        n_ref[...] = nb
        for j in range(F // tc):
            cs = slice(j * tc, (j + 1) * tc)
            conv = []
            for s in range(2):
                acc = jnp.dot(nb, w_ref[:, s * F + j * tc:s * F + (j + 1) * tc], preferred_element_type=F32)
                ab = acc.astype(BF16)
                a_ref[s, :, cs] = ab
                af = ab.astype(F32)
                ext = jnp.concatenate([jnp.where(first, 0.0, tail[s, :, cs]), af], axis=0)
                tail[s, :, cs] = af[tm - HALO:, :]
                cv = _taps(ext, cw_ref[s, :, cs], cb_ref[s:s + 1, cs]).astype(BF16)
                c_ref[s, :, cs] = cv
                conv.append(cv.astype(F32))
            up, gate = conv
            y_ref[:, cs] = (gate * jax.nn.sigmoid(gate) * up).astype(BF16)

    row = lambda width: pl.BlockSpec((tm, width), lambda i: (i, 0))
    wide = pl.BlockSpec((2, tm, F), lambda i: (0, i, 0))
    return pl.pallas_call(
        body, name=name, grid=(T // tm,),
        in_specs=[row(D), pl.BlockSpec((None, D, 2 * F), lambda i: (0, 0, 0), pipeline_mode=pl.Buffered(1)),
                  pl.BlockSpec((1, D), lambda i: (0, 0)),
                  pl.BlockSpec((2, 3, F), lambda i: (0, 0, 0)), pl.BlockSpec((2, F), lambda i: (0, 0))],
        out_specs=[row(F), wide, wide, row(D)],
        out_shape=[jax.ShapeDtypeStruct((T, F), BF16), jax.ShapeDtypeStruct((2, T, F), BF16),
                   jax.ShapeDtypeStruct((2, T, F), BF16), jax.ShapeDtypeStruct((T, D), BF16)],
        scratch_shapes=[pltpu.VMEM((2, HALO, F), F32)],
        compiler_params=_params(("arbitrary",)))(h, w, g.reshape(1, D), cw, cb)


def _conv_bwd(a, c, dy, cw, S, *, tm=256):
    _, T, F = a.shape
    tc = _col_tile(F)
    tm = _row_tile(S, tm)
    nm = T // tm
    hb = tm // HALO
    TE = tm + HALO
    nxt = lambda j, i: jnp.minimum((i + 1) * hb, T // HALO - 1)

    def body(a_ref, c_ref, nc_ref, dy_ref, ndy_ref, w_ref, da_ref, dw_ref, db_ref):
        i = pl.program_id(1)
        last = ((i + 1) * tm) % S == 0
        keep_n = jnp.where(last, 0.0, 1.0)
        dyf = jnp.concatenate([dy_ref[...].astype(F32), ndy_ref[...].astype(F32) * keep_n], axis=0)
        up = jnp.concatenate([c_ref[0].astype(F32), nc_ref[0].astype(F32)], axis=0)
        gate = jnp.concatenate([c_ref[1].astype(F32), nc_ref[1].astype(F32)], axis=0)
        sg = jax.nn.sigmoid(gate)
        d_up = dyf * (gate * sg)
        d_gate = dyf * up * (sg * (1.0 + gate * (1.0 - sg)))

        @pl.when(i == 0)
        def _():
            dw_ref[...] = jnp.zeros_like(dw_ref)
            db_ref[...] = jnp.zeros_like(db_ref)

        def back(s, d):
            a = a_ref[s].astype(F32)
            w = w_ref[s]
            u1, u2 = pltpu.roll(d, TE - 1, 0), pltpu.roll(d, TE - 2, 0)
            db_ref[s:s + 1, :] += jnp.sum(d[:tm], axis=0, keepdims=True)
            dw_ref[s, 2:3, :] += jnp.sum(d[:tm] * a, axis=0, keepdims=True)
            dw_ref[s, 1:2, :] += jnp.sum(u1[:tm] * a, axis=0, keepdims=True)
            dw_ref[s, 0:1, :] += jnp.sum(u2[:tm] * a, axis=0, keepdims=True)
            da_ref[s] = (w[2:3] * d + w[1:2] * u1 + w[0:1] * u2)[:tm].astype(BF16)

        back(0, d_up)
        back(1, d_gate)

    cur = pl.BlockSpec((2, tm, tc), lambda j, i: (0, i, j))
    return pl.pallas_call(
        body, name="conv_bwd", grid=(F // tc, nm),
        in_specs=[cur, cur, pl.BlockSpec((2, HALO, tc), lambda j, i: (0, nxt(j, i), j)),
                  pl.BlockSpec((tm, tc), lambda j, i: (i, j)), pl.BlockSpec((HALO, tc), lambda j, i: (nxt(j, i), j)),
                  pl.BlockSpec((2, 3, tc), lambda j, i: (0, 0, j))],
        out_specs=[cur, pl.BlockSpec((2, 3, tc), lambda j, i: (0, 0, j)), pl.BlockSpec((2, tc), lambda j, i: (0, j))],
        out_shape=[jax.ShapeDtypeStruct((2, T, F), BF16), jax.ShapeDtypeStruct((2, 3, F), F32),
                   jax.ShapeDtypeStruct((2, F), F32)],
        compiler_params=_params(("arbitrary", "arbitrary")))(a, c, c, dy, dy, cw)


def _bias_index():
    idx = np.arange(F_LEN)
    d = np.where(idx < K_SPAN, idx, idx - F_LEN)
    return np.clip(PAD - d, -REL_CLIP, REL_CLIP) + REL_CLIP


ROW_GROUP = 16


def _roll_rows(x, sign, unit, steps):
    rows = lax.broadcasted_iota(jnp.int32, x.shape, 0)
    step = 1
    while step < steps:
        shift = unit * step if sign > 0 else F_LEN - unit * step
        x = jnp.where((rows & step) != 0, pltpu.roll(x, shift, 1), x)
        step *= 2
    return x


def _bias_expand(frow):
    H = frow.shape[0]
    groups = Q_BLOCK // ROW_GROUP

    def body(f_ref, o_ref):
        coarse = _roll_rows(jnp.broadcast_to(f_ref[...], (groups, F_LEN)), 1, ROW_GROUP, groups)
        x = jnp.concatenate([jnp.broadcast_to(coarse[a:a + 1], (ROW_GROUP, F_LEN)) for a in range(groups)], axis=0)
        x = _roll_rows(x, 1, 1, ROW_GROUP)[:, :K_SPAN]
        qc = lax.broadcasted_iota(jnp.int32, (Q_BLOCK, K_SPAN), 0) // CHUNK * CHUNK
        kj = lax.broadcasted_iota(jnp.int32, (Q_BLOCK, K_SPAN), 1)
        o_ref[...] = jnp.where((kj >= qc) & (kj < qc + PAD + CHUNK), x, NEG_INF)

    return pl.pallas_call(
        body, name="bias_expand", grid=(H,),
        in_specs=[pl.BlockSpec((None, 1, F_LEN), lambda h: (h, 0, 0))],
        out_specs=pl.BlockSpec((None, Q_BLOCK, K_SPAN), lambda h: (h, 0, 0)),
        out_shape=jax.ShapeDtypeStruct((H, Q_BLOCK, K_SPAN), F32), compiler_params=_params(("arbitrary",)))(frow)


def _bias_reduce(dbias, n_rel):
    H = dbias.shape[0]
    onehot = jnp.asarray((_bias_index()[:, None] == np.arange(n_rel)[None, :]).astype(np.float32), dtype=BF16)

    def body(d_ref, oh_ref, o_ref):
        x = jnp.concatenate([d_ref[...], jnp.zeros((Q_BLOCK, F_LEN - K_SPAN), F32)], axis=1)
        fine = _roll_rows(x, -1, 1, ROW_GROUP).reshape(Q_BLOCK // ROW_GROUP, ROW_GROUP, F_LEN)
        coarse = _roll_rows(jnp.sum(fine, axis=1), -1, ROW_GROUP, Q_BLOCK // ROW_GROUP)
        row = jnp.broadcast_to(jnp.sum(coarse, axis=0, keepdims=True), (8, F_LEN))
        acc = jnp.zeros((8, n_rel), F32)
        for _ in range(3):
            piece = row.astype(BF16)
            acc = acc + jnp.dot(piece, oh_ref[...], preferred_element_type=F32)
            row = row - piece.astype(F32)
        o_ref[...] = acc[0:1]

    return pl.pallas_call(
        body, name="bias_reduce", grid=(H,),
        in_specs=[pl.BlockSpec((None, Q_BLOCK, K_SPAN), lambda h: (h, 0, 0)),
                  pl.BlockSpec((F_LEN, n_rel), lambda h: (0, 0))],
        out_specs=pl.BlockSpec((None, 1, n_rel), lambda h: (h, 0, 0)),
        out_shape=jax.ShapeDtypeStruct((H, 1, n_rel), F32), compiler_params=_params(("arbitrary",)))(dbias, onehot)


def _attn_specs(S):
    hw = HEADS_PER_STEP * HEAD_DIM
    qspec = pl.BlockSpec((None, Q_BLOCK, hw), lambda g, b, i: (b, i, g))
    kspec = pl.BlockSpec((None, None, S, hw), lambda g, b, i: (0, b, 0, g))
    vspec = pl.BlockSpec((None, None, S, hw), lambda g, b, i: (1, b, 0, g))
    bspec = pl.BlockSpec((HEADS_PER_STEP, Q_BLOCK, K_SPAN), lambda g, b, i: (g, 0, 0))
    return hw, qspec, kspec, vspec, bspec


def _span_cases(i, fn):
    short = PAD // Q_BLOCK
    for j in range(short):
        pl.when(i == j)(functools.partial(fn, PAD - j * Q_BLOCK))
    pl.when(i >= short)(functools.partial(fn, 0))


def _key_start(i, off):
    return 0 if off else pl.multiple_of(i * Q_BLOCK - PAD, Q_BLOCK)


def _attn_exp(q_ref, k_ref, b_ref, h, k0, off):
    hs = slice(h * HEAD_DIM, (h + 1) * HEAD_DIM)
    kh = k_ref[pl.ds(k0, K_SPAN - off), hs]
    s = lax.dot_general(q_ref[:, hs], kh, (((1,), (1,)), ((), ())), preferred_element_type=F32) + b_ref[h, :, off:]
    p = jnp.exp(s - jnp.max(s, axis=-1, keepdims=True))
    return p, 1.0 / jnp.sum(p, axis=-1, keepdims=True), kh


def _attn_fwd(q, kv, bias, B, S):
    HD = q.shape[-1]
    hw, qspec, kspec, vspec, bspec = _attn_specs(S)

    def body(q_ref, k_ref, v_ref, b_ref, o_ref):
        i = pl.program_id(2)

        def block(off):
            k0 = _key_start(i, off)
            outs = []
            for h in range(HEADS_PER_STEP):
                hs = slice(h * HEAD_DIM, (h + 1) * HEAD_DIM)
                p, inv, _ = _attn_exp(q_ref, k_ref, b_ref, h, k0, off)
                outs.append(jnp.dot(p.astype(BF16), v_ref[pl.ds(k0, K_SPAN - off), hs],
                                    preferred_element_type=F32) * inv)
            o_ref[...] = jnp.concatenate(outs, axis=1).astype(BF16)

        _span_cases(i, block)

    return pl.pallas_call(
        body, name="attn_fwd", grid=(HD // hw, B, S // Q_BLOCK), in_specs=[qspec, kspec, vspec, bspec],
        out_specs=qspec, out_shape=jax.ShapeDtypeStruct((B, S, HD), BF16),
        compiler_params=_params(("arbitrary", "arbitrary", "arbitrary")))(q, kv, kv, bias)


def _attn_bwd(q, kv, bias, do, B, S):
    HD = q.shape[-1]
    H = HD // HEAD_DIM
    hw, qspec, kspec, vspec, bspec = _attn_specs(S)
    scale = HEAD_DIM ** -0.5
    nq = S // Q_BLOCK

    def body(q_ref, k_ref, v_ref, b_ref, do_ref, dq_ref, dkv_ref, db_ref, dk_acc, dv_acc):
        b, i = pl.program_id(1), pl.program_id(2)

        @pl.when(i == 0)
        def _():
            dk_acc[...] = jnp.zeros_like(dk_acc)
            dv_acc[...] = jnp.zeros_like(dv_acc)

        @pl.when((i == 0) & (b == 0))
        def _():
            db_ref[...] = jnp.zeros_like(db_ref)

        def block(off):
            k0 = _key_start(i, off)
            keys = pl.ds(k0, K_SPAN - off)
            for h in range(HEADS_PER_STEP):
                hs = slice(h * HEAD_DIM, (h + 1) * HEAD_DIM)
                p, inv, kh = _attn_exp(q_ref, k_ref, b_ref, h, k0, off)
                p = p * inv
                doh = do_ref[:, hs]
                dp = lax.dot_general(doh, v_ref[keys, hs], (((1,), (1,)), ((), ())), preferred_element_type=F32)
                ds = p * (dp - jnp.sum(p * dp, axis=-1, keepdims=True))
                db_ref[h, :, off:] += ds
                dsb = ds.astype(BF16)
                dq_ref[:, hs] = (jnp.dot(dsb, kh, preferred_element_type=F32) * scale).astype(BF16)
                dk_acc[hs, keys] += lax.dot_general(q_ref[:, hs], dsb, (((0,), (0,)), ((), ())),
                                                     preferred_element_type=F32)
                dv_acc[hs, keys] += lax.dot_general(doh, p.astype(BF16), (((0,), (0,)), ((), ())),
                                                     preferred_element_type=F32)

        _span_cases(i, block)

        @pl.when(i == nq - 1)
        def _():
            dkv_ref[0] = dk_acc[...].T.astype(BF16)
            dkv_ref[1] = dv_acc[...].T.astype(BF16)

    return pl.pallas_call(
        body, name="attn_bwd", grid=(HD // hw, B, nq), in_specs=[qspec, kspec, vspec, bspec, qspec],
        out_specs=[qspec, pl.BlockSpec((2, None, S, hw), lambda g, b, i: (0, b, 0, g)), bspec],
        out_shape=[jax.ShapeDtypeStruct((B, S, HD), BF16), jax.ShapeDtypeStruct((2, B, S, HD), BF16),
                   jax.ShapeDtypeStruct((H, Q_BLOCK, K_SPAN), F32)],
        scratch_shapes=[pltpu.VMEM((hw, S), F32), pltpu.VMEM((hw, S), F32)],
        compiler_params=_params(("arbitrary", "arbitrary", "arbitrary")))(q, kv, kv, bias, do)


def _sub_rows(R):
    for cand in (256, 352, 128, 64, 8):
        if R % cand == 0 and R > cand:
            return cand
    return R


def _adamw(w, g, m, v, *, name):
    R, C = w.shape
    tr = _sub_rows(R)

    def body(w_ref, g_ref, m_ref, v_ref, d_ref, nm_ref, nv_ref):
        g = g_ref[...]
        m = ADAM_B1 * m_ref[...] + (1.0 - ADAM_B1) * g
        v = ADAM_B2 * v_ref[...] + (1.0 - ADAM_B2) * (g * g)
        m_hat = m / (1.0 - ADAM_B1 ** ADAM_STEP)
        v_hat = v / (1.0 - ADAM_B2 ** ADAM_STEP)
        d_ref[...] = -ADAM_LR * (m_hat / (jnp.sqrt(v_hat) + ADAM_EPS) + ADAM_WD * w_ref[...])
        nm_ref[...] = m
        nv_ref[...] = v

    spec = pl.BlockSpec((tr, C), lambda i: (i, 0))
    return pl.pallas_call(body, name=name, grid=(R // tr,), in_specs=[spec] * 4, out_specs=[spec] * 3,
                          out_shape=[jax.ShapeDtypeStruct((R, C), F32)] * 3,
                          compiler_params=_params(("arbitrary",)))(w, g, m, v)


def _add_pair(units, got, core, *, name):
    n4, R, C = got.shape
    rows = n4 * R
    tr = 512 if rows % 512 == 0 else R

    def body(c_ref, u_ref, got_ref, o_ref):
        o_ref[...] = (u_ref[...].astype(F32) + got_ref[...].astype(F32)).astype(BF16)

    spec = pl.BlockSpec((tr, C), lambda i, c: (i, 0))
    grid_spec = pltpu.PrefetchScalarGridSpec(
        num_scalar_prefetch=1, grid=(rows // tr,),
        in_specs=[pl.BlockSpec((None, tr, C), lambda i, c: (c[0], i, 0)), spec], out_specs=spec)
    out = pl.pallas_call(body, name=name, grid_spec=grid_spec, out_shape=jax.ShapeDtypeStruct((rows, C), BF16),
                         compiler_params=_params(("arbitrary",)))(core.reshape(1), units.reshape(2, rows, C),
                                                                   got.reshape(rows, C))
    return out.reshape(n4, R, C)


def _sum_chips(w, own, got, pos, *, name, layer=0, into=None):
    _, R, C = own.shape
    tr = _sub_rows(R)
    nr = R // tr

    def body(p_ref, own_ref, got_ref, *rest):
        o_ref = rest[-1]
        o_ref[...] = (own_ref[...].astype(F32) + got_ref[0].astype(F32) + got_ref[1].astype(F32)
                      + got_ref[2].astype(F32))

    if w.row_sharded:
        out_map = lambda i, p: (layer, i, p[1])
    else:
        out_map = lambda i, p: (layer, p[1] * nr + i, 0)
    ins = [pos, own, got]
    in_specs = [pl.BlockSpec((None, tr, C), lambda i, p: (p[0], i, 0)),
                pl.BlockSpec((3, tr, C), lambda i, p: (0, i, 0))]
    alias = {}
    if into is not None:
        ins.append(into)
        in_specs.append(ANY)
        alias = {3: 0}
    grid_spec = pltpu.PrefetchScalarGridSpec(num_scalar_prefetch=1, grid=(nr,), in_specs=in_specs,
                                             out_specs=pl.BlockSpec((None, tr, C), out_map))
    return pl.pallas_call(body, name=name, grid_spec=grid_spec, input_output_aliases=alias,
                          out_shape=jax.ShapeDtypeStruct((w.L, w.ks, w.ns), F32),
                          compiler_params=_params(("arbitrary",)))(*ins)


def _mesh_pos():
    return lax.axis_index("x"), lax.axis_index("y"), lax.axis_index("c")


def _other_chips(x, y):
    return [(1 - x, y), (x, 1 - y), (1 - x, 1 - y)]


ANY = pl.BlockSpec(memory_space=pl.ANY)


class _W:
    def __init__(self, name, shard, row_sharded):
        self.name = name
        self.L, ks, ns = shard.shape
        self.row_sharded = row_sharded
        self.K, self.N = (ks * N_CHIPS, ns) if row_sharded else (ks, ns * N_CHIPS)
        self.ks, self.ns = ks, ns

    def shard_of(self, full, j):
        if self.row_sharded:
            return full.at[:, pl.ds(j * self.ks, self.ks), :]
        return full.at[:, :, pl.ds(j * self.ns, self.ns)]

    def half_of(self, shard, c):
        if self.row_sharded:
            return shard.at[:, :, pl.ds(c * (self.ns // 2), self.ns // 2)]
        return shard.at[:, pl.ds(c * (self.ks // 2), self.ks // 2), :]


HBM = pl.BlockSpec(memory_space=pltpu.HBM)
SEM = pl.BlockSpec(memory_space=pltpu.SEMAPHORE)
IN_FLIGHT = pltpu.SideEffectType.DATAFLOW_SIDE_EFFECTING


def _in_hbm(a):
    return pltpu.with_memory_space_constraint(a, pltpu.HBM)


def _gather_start(ws, shards, after):
    nw = len(ws)

    def body(*refs):
        src, dst = refs[:nw], refs[nw:2 * nw]
        send, recv = refs[2 * nw + 1:3 * nw + 1], refs[3 * nw + 1:4 * nw + 1]
        x, y, c = _mesh_pos()
        me = 2 * x + y
        for i, w in enumerate(ws):
            for f, (px, py) in enumerate(_other_chips(x, y)):
                pltpu.make_async_remote_copy(src_ref=w.half_of(src[i], c), dst_ref=w.half_of(w.shard_of(dst[i], me), c),
                                             send_sem=send[i].at[f], recv_sem=recv[i].at[f], device_id=(px, py, c),
                                             device_id_type=MESH).start()

    fulls = [lax.empty((w.L, w.K, w.N), BF16) for w in ws]
    out = pl.pallas_call(
        body, name="gather_start", in_specs=[HBM] * (2 * nw) + [ANY],
        out_specs=[SEM] * (2 * nw) + [HBM] * (2 * nw),
        out_shape=[pltpu.SemaphoreType.DMA((3,))] * (2 * nw)
        + [pltpu.HBM(s.shape, BF16) for s in shards] + [pltpu.HBM(f.shape, BF16) for f in fulls],
        input_output_aliases={i: 2 * nw + i for i in range(2 * nw)},
        compiler_params=pltpu.CompilerParams(has_side_effects=IN_FLIGHT))(
            *[_in_hbm(s) for s in shards], *[_in_hbm(f) for f in fulls], after)
    return [(out[i], out[nw + i], out[2 * nw + i], out[3 * nw + i]) for i in range(nw)]


def _gather_wait(ws, flight, after, *, name):
    nw = len(ws)

    def body(*refs):
        src, dst = refs[:nw], refs[nw:2 * nw]
        send, recv = refs[2 * nw:3 * nw], refs[3 * nw:4 * nw]
        x, y, c = _mesh_pos()
        for i, w in enumerate(ws):
            for f, (px, py) in enumerate(_other_chips(x, y)):
                landed = w.half_of(w.shard_of(dst[i], 2 * px + py), c)
                cp = pltpu.make_async_remote_copy(src_ref=w.half_of(src[i], c), dst_ref=landed, send_sem=send[i].at[f],
                                                  recv_sem=recv[i].at[f], device_id=(px, py, c), device_id_type=MESH)
                cp.wait_send()
                cp.wait_recv()

    shards, fulls = [fl[2] for fl in flight], [fl[3] for fl in flight]
    out = pl.pallas_call(
        body, name=name, in_specs=[HBM] * (2 * nw) + [SEM] * (2 * nw) + [ANY],
        out_specs=[HBM] * (2 * nw),
        out_shape=[pltpu.HBM(s.shape, BF16) for s in shards] + [pltpu.HBM(f.shape, BF16) for f in fulls],
        input_output_aliases={i: i for i in range(2 * nw)},
        compiler_params=pltpu.CompilerParams(has_side_effects=IN_FLIGHT))(
            *shards, *fulls, *[fl[0] for fl in flight], *[fl[1] for fl in flight], after)
    return out[:nw], out[nw:]


def _gather_finish(ws, shards, fulls, *, name):
    nw = len(ws)

    def body(*refs):
        src, dst, stage = refs[:nw], refs[3 * nw:4 * nw], refs[4 * nw:5 * nw]
        send_sems, recv_sems, load_sems, store_sems = refs[5 * nw:]
        x, y, c = _mesh_pos()
        me = 2 * x + y
        sibling = (x, y, 1 - c)
        chips = _other_chips(x, y)

        def fwd(i, w, f, half):
            px, py = chips[f]
            landed = w.half_of(w.shard_of(dst[i], 2 * px + py), half)
            return pltpu.make_async_remote_copy(src_ref=landed, dst_ref=landed, send_sem=send_sems.at[3 * i + f],
                                                recv_sem=recv_sems.at[3 * i + f], device_id=sibling,
                                                device_id_type=MESH)

        loads = [pltpu.make_async_copy(src[i], stage[i], load_sems.at[i]) for i in range(nw)]
        for cp in loads:
            cp.start()
        sends = [fwd(i, w, f, c) for i, w in enumerate(ws) for f in range(3)]
        for cp in sends:
            cp.start()
        stores = [pltpu.make_async_copy(stage[i], w.shard_of(dst[i], me), store_sems.at[i])
                  for i, w in enumerate(ws)]
        for ld, st in zip(loads, stores):
            ld.wait()
            st.start()
        for i, w in enumerate(ws):
            for f in range(3):
                fwd(i, w, f, 1 - c).wait_recv()
        for cp in sends:
            cp.wait_send()
        for cp in stores:
            cp.wait()

    out = pl.pallas_call(
        body, name=name, in_specs=[ANY] * (2 * nw), out_specs=[ANY] * (2 * nw),
        out_shape=[jax.ShapeDtypeStruct(s.shape, BF16) for s in shards]
        + [jax.ShapeDtypeStruct(f.shape, BF16) for f in fulls],
        input_output_aliases={i: i for i in range(2 * nw)},
        scratch_shapes=[pltpu.VMEM((w.L, w.ks, w.ns), BF16) for w in ws]
        + [pltpu.SemaphoreType.DMA((3 * nw,)), pltpu.SemaphoreType.DMA((3 * nw,)), pltpu.SemaphoreType.DMA((nw,)),
           pltpu.SemaphoreType.DMA((nw,))],
        compiler_params=_params(has_side_effects=True))(*shards, *fulls)
    return out[nw:]


def _split_copies(name, srcs, lands, n_sems, copies_of, *, flight=None, after=None):
    n = len(srcs)
    starting = flight is None

    def body(*refs):
        src, land = refs[:n], refs[n:2 * n]
        sems = refs[2 * n + 1:4 * n + 1] if starting else refs[2 * n:4 * n]
        for i in range(n):
            for cp in copies_of(i, src[i], land[i], sems[i], sems[n + i]):
                if starting:
                    cp.start()
                else:
                    cp.wait_send()
                    cp.wait_recv()

    thru = [pltpu.HBM(a.shape, a.dtype) for a in list(srcs) + list(lands)]
    if starting:
        out = pl.pallas_call(
            body, name=name, in_specs=[HBM] * (2 * n) + [ANY], out_specs=[SEM] * (2 * n) + [HBM] * (2 * n),
            out_shape=[pltpu.SemaphoreType.DMA((n_sems,))] * (2 * n) + thru,
            input_output_aliases={i: 2 * n + i for i in range(2 * n)},
            compiler_params=pltpu.CompilerParams(has_side_effects=IN_FLIGHT))(
                *[_in_hbm(a) for a in srcs], *[_in_hbm(a) for a in lands], after)
        return [(out[i], out[n + i], out[2 * n + i], out[3 * n + i]) for i in range(n)]
    out = pl.pallas_call(
        body, name=name, in_specs=[HBM] * (2 * n) + [SEM] * (2 * n) + [ANY], out_specs=[HBM] * (2 * n),
        out_shape=thru, input_output_aliases={i: i for i in range(2 * n)},
        compiler_params=pltpu.CompilerParams(has_side_effects=IN_FLIGHT))(
            *srcs, *lands, *[fl[0] for fl in flight], *[fl[1] for fl in flight], after)
    return out[:n], out[n:]


def _sum8(land, vec, me):
    R = vec.shape[0]

    def body(me_ref, land_ref, vec_ref, o_ref):
        acc = jnp.zeros((R, 128), F32)
        for d in range(8):
            acc = acc + jnp.where(me_ref[0] == d, vec_ref[...], land_ref[d])
        o_ref[...] = acc

    grid_spec = pltpu.PrefetchScalarGridSpec(
        num_scalar_prefetch=1, grid=(1,),
        in_specs=[pl.BlockSpec((8, R, 128), lambda i, m: (0, 0, 0)), pl.BlockSpec((R, 128), lambda i, m: (0, 0))],
        out_specs=pl.BlockSpec((R, 128), lambda i, m: (0, 0)))
    return pl.pallas_call(body, name="sum8", grid_spec=grid_spec, out_shape=jax.ShapeDtypeStruct((R, 128), F32),
                          compiler_params=_params(("arbitrary",)))(me.reshape(1), land, vec)


def _swap_copies(i, src, got, send, recv):
    x, y, c = _mesh_pos()
    return [pltpu.make_async_remote_copy(src_ref=src.at[1 - c], dst_ref=got, send_sem=send.at[0], recv_sem=recv.at[0],
                                         device_id=(x, y, 1 - c), device_id_type=MESH)]


def _gather8_copies(i, src, land, send, recv):
    x, y, c = _mesh_pos()
    me = 4 * x + 2 * y + c
    peers = [(x, y, 1 - c)] + [(px, py, pc) for px, py in _other_chips(x, y) for pc in (c, 1 - c)]
    return [pltpu.make_async_remote_copy(src_ref=src, dst_ref=land.at[me], send_sem=send.at[k], recv_sem=recv.at[k],
                                         device_id=peer, device_id_type=MESH) for k, peer in enumerate(peers)]


def _scatter_copy(src, got, send, recv, f, chip, c):
    px, py = chip
    return pltpu.make_async_remote_copy(src_ref=src.at[2 * px + py], dst_ref=got.at[f], send_sem=send.at[f],
                                        recv_sem=recv.at[f], device_id=(px, py, c), device_id_type=MESH)


def _scatter_start(sums, *, name):
    nw = len(sums)

    def body(*refs):
        src, got = refs[:nw], refs[nw:2 * nw]
        send, recv = refs[2 * nw:3 * nw], refs[3 * nw:4 * nw]
        x, y, c = _mesh_pos()
        for i in range(nw):
            for f, chip in enumerate(_other_chips(x, y)):
                _scatter_copy(src[i], got[i], send[i], recv[i], f, chip, c).start()

    lands = [lax.empty((3,) + s.shape[1:], BF16) for s in sums]
    out = pl.pallas_call(
        body, name=name, in_specs=[HBM] * (2 * nw), out_specs=[SEM] * (2 * nw) + [HBM] * (2 * nw),
        out_shape=[pltpu.SemaphoreType.DMA((3,))] * (2 * nw)
        + [pltpu.HBM(s.shape, BF16) for s in sums] + [pltpu.HBM(l.shape, BF16) for l in lands],
        input_output_aliases={i: 2 * nw + i for i in range(2 * nw)},
        compiler_params=pltpu.CompilerParams(has_side_effects=IN_FLIGHT))(
            *[_in_hbm(s) for s in sums], *[_in_hbm(l) for l in lands])
    return [(out[i], out[nw + i], out[2 * nw + i], out[3 * nw + i]) for i in range(nw)]


def _scatter_wait(flight, after):
    nw = len(flight)

    def body(*refs):
        src, got = refs[:nw], refs[nw:2 * nw]
        send, recv = refs[2 * nw:3 * nw], refs[3 * nw:4 * nw]
        x, y, c = _mesh_pos()
        for i in range(nw):
            for f, chip in enumerate(_other_chips(x, y)):
                cp = _scatter_copy(src[i], got[i], send[i], recv[i], f, chip, c)
                cp.wait_send()
                cp.wait_recv()

    sums, lands = [fl[2] for fl in flight], [fl[3] for fl in flight]
    out = pl.pallas_call(
        body, name="scatter_wait", in_specs=[HBM] * (2 * nw) + [SEM] * (2 * nw) + [ANY], out_specs=[HBM] * (2 * nw),
        out_shape=[pltpu.HBM(s.shape, BF16) for s in sums] + [pltpu.HBM(l.shape, BF16) for l in lands],
        input_output_aliases={i: i for i in range(2 * nw)},
        compiler_params=pltpu.CompilerParams(has_side_effects=IN_FLIGHT))(
            *sums, *lands, *[fl[0] for fl in flight], *[fl[1] for fl in flight], after)
    return out[:nw], out[nw:]


def _join_halves(ws, shards):
    nw = len(ws)

    def body(*refs):
        buf = refs[nw:2 * nw]
        send_sems, recv_sems = refs[2 * nw:]
        x, y, c = _mesh_pos()
        sibling = (x, y, 1 - c)

        def copy(i, w, half):
            region = w.half_of(buf[i], half)
            return pltpu.make_async_remote_copy(src_ref=region, dst_ref=region, send_sem=send_sems.at[i],
                                                recv_sem=recv_sems.at[i], device_id=sibling, device_id_type=MESH)

        sends = [copy(i, w, c) for i, w in enumerate(ws)]
        for cp in sends:
            cp.start()
        for i, w in enumerate(ws):
            copy(i, w, 1 - c).wait_recv()
        for cp in sends:
            cp.wait_send()

    return pl.pallas_call(
        body, name="join_halves", in_specs=[ANY] * nw, out_specs=[ANY] * nw,
        out_shape=[jax.ShapeDtypeStruct((w.L, w.ks, w.ns), F32) for w in ws],
        input_output_aliases={i: i for i in range(nw)},
        scratch_shapes=[pltpu.SemaphoreType.DMA((nw,)), pltpu.SemaphoreType.DMA((nw,))],
        compiler_params=_params(has_side_effects=True))(*shards)


def _allreduce_small(vec):
    R = vec.shape[0]

    def body(x_ref, o_ref, buf, send_sems, recv_sems):
        x, y, c = _mesh_pos()
        me, sibling = (x, y, c), (x, y, 1 - c)
        chips = _other_chips(x, y)

        def slot(px, py, pc):
            return buf.at[4 * px + 2 * py + pc]

        def copy(k, block, to, src=None):
            return pltpu.make_async_remote_copy(src_ref=slot(*block) if src is None else src, dst_ref=slot(*block),
                                                send_sem=send_sems.at[k], recv_sem=recv_sems.at[k], device_id=to,
                                                device_id_type=MESH)

        first = [copy(0, me, sibling, src=x_ref)] + [copy(1 + f, me, (*chip, c), src=x_ref)
                                                     for f, chip in enumerate(chips)]
        for cp in first:
            cp.start()
        passed = [copy(4 + f, (*chip, c), sibling) for f, chip in enumerate(chips)]
        for f, chip in enumerate(chips):
            copy(1 + f, (*chip, c), me).wait_recv()
            passed[f].start()
        copy(0, sibling, me).wait_recv()
        for f, chip in enumerate(chips):
            copy(4 + f, (*chip, 1 - c), me).wait_recv()
        for cp in first + passed:
            cp.wait_send()
        slot(*me)[...] = x_ref[...]
        acc = buf[0]
        for d in range(1, 8):
            acc = acc + buf[d]
        o_ref[...] = acc

    return pl.pallas_call(
        body, name="allreduce_small", in_specs=[pl.BlockSpec(memory_space=pltpu.VMEM)],
        out_specs=pl.BlockSpec(memory_space=pltpu.VMEM), out_shape=jax.ShapeDtypeStruct((R, 128), F32),
        scratch_shapes=[pltpu.VMEM((8, R, 128), F32), pltpu.SemaphoreType.DMA((7,)), pltpu.SemaphoreType.DMA((7,))],
        compiler_params=_params())(vec)


def _pack(parts):
    flat = jnp.concatenate([p.reshape(-1).astype(F32) for p in parts])
    n = flat.shape[0]
    pad = (-n) % (64 * 128)
    return jnp.pad(flat, (0, pad)).reshape(-1, 128)


def _unpack(vec, shapes):
    flat = vec.reshape(-1)
    out, off = [], 0
    for s in shapes:
        n = int(np.prod(s))
        out.append(flat[off:off + n].reshape(s))
        off += n
    return out


def kernel(x, a_norm_g, a_w_in, a_v_norm_g, a_w_s, a_b_s, a_w_out, kv_norm_g, w_kv, b_norm_g, b_w_q, b_rel_bias, b_w_o, f_norm_g, f_w_in, f_conv_w, f_conv_b, f_w_down, final_norm_g, loss_target, m_a_norm_g, m_a_w_in, m_a_v_norm_g, m_a_w_s, m_a_b_s, m_a_w_out, m_kv_norm_g, m_w_kv, m_b_norm_g, m_b_w_q, m_b_rel_bias, m_b_w_o, m_f_norm_g, m_f_w_in, m_f_conv_w, m_f_conv_b, m_f_w_down, m_final_norm_g, v_a_norm_g, v_a_w_in, v_a_v_norm_g, v_a_w_s, v_a_b_s, v_a_w_out, v_kv_norm_g, v_w_kv, v_b_norm_g, v_b_w_q, v_b_rel_bias, v_b_w_o, v_f_norm_g, v_f_w_in, v_f_conv_w, v_f_conv_b, v_f_w_down, v_final_norm_g):
    B, S, D = x.shape
    T = B * S
    xi, yi, ci = lax.axis_index("x"), lax.axis_index("y"), lax.axis_index("c")
    j_me = (2 * xi + yi).astype(jnp.int32)
    core = ci.astype(jnp.int32)
    pos = jnp.stack([j_me, core])

    w_shards = {"a_w_in": (a_w_in, False), "a_w_out": (a_w_out, True), "w_kv": (w_kv[None], False),
                "b_w_q": (b_w_q, True), "b_w_o": (b_w_o, True), "f_w_in": (f_w_in, False), "f_w_down": (f_w_down, True)}
    names = list(w_shards)
    ws = [_W(n, w_shards[n][0], w_shards[n][1]) for n in names]
    g_shards = {"a_w_in": (a_w_in, False), "a_w_out": (a_w_out, True),
                "f_w_in0": (f_w_in[0:1], False), "f_w_down0": (f_w_down[0:1], True),
                "w_kv": (w_kv[None], False), "b_w_q": (b_w_q, True), "b_w_o": (b_w_o, True),
                "f_w_in1": (f_w_in[1:2], False), "f_w_down1": (f_w_down[1:2], True)}
    g_names = list(g_shards)
    g_ws = {n: _W(n, *g_shards[n]) for n in g_names}

    Wd = a_w_in.shape[1]
    GW = a_v_norm_g.shape[1] * N_CHIPS
    F2 = f_conv_w.shape[2] * N_CHIPS
    Fh = F2 // 2
    nsd, nsg, nsf = a_norm_g.shape[1], a_v_norm_g.shape[1], f_conv_w.shape[2]
    own = (ci == 0).astype(F32)
    place = lambda sh, width, n: lax.dynamic_update_slice_in_dim(
        jnp.zeros(sh.shape[:-1] + (width,), F32), sh * own, j_me * n, axis=sh.ndim - 1)
    gathered = _allreduce_small(_pack([place(a_norm_g, Wd, nsd), place(a_v_norm_g, GW, nsg),
                                       place(f_conv_w, F2, nsf)]))
    a_g, a_vg, conv_w = _unpack(gathered, [(1, Wd), (1, GW), (2, 3, F2)])

    flight = dict(zip(g_names, _gather_start([g_ws[n] for n in g_names],
                                             [g_shards[n][0].astype(BF16) for n in g_names], gathered)))
    full = {}

    def tied(x, flight):
        x, thru = lax.optimization_barrier((x, flight[0][2]))
        return x, [flight[0][:2] + (thru,) + flight[0][3:]] + flight[1:]

    def arrive(group, after, tag):
        gw = [g_ws[n] for n in group]
        sh, fu = _gather_wait(gw, [flight[n] for n in group], after, name=f"gather_wait_{tag}")
        full.update(zip(group, _gather_finish(gw, sh, fu, name=f"gather_finish_{tag}")))
    conv_w2 = conv_w.reshape(2, 3, 2, Fh).transpose(0, 2, 1, 3)
    conv_b2 = f_conv_b.reshape(2, 2, Fh)

    h0 = x.reshape(T, D)
    target = loss_target.reshape(T, D)
    bs_tile = jnp.repeat(a_b_s[0].T, GROUP_DIM, axis=1)
    ws_a = a_w_s[0]
    scale = HEAD_DIM ** -0.5
    HD = b_w_q.shape[2]
    H = HD // HEAD_DIM
    n_rel = b_rel_bias.shape[-1]
    frow, (flight["a_w_in"],) = tied(b_rel_bias[0][:, _bias_index()].reshape(H, 1, F_LEN), [flight["a_w_in"]])
    bias = _bias_expand(frow)

    def ffn_fwd(h, l, loss=None):
        yff, a, c, n = _ffn_in_conv(h, full[f"f_w_in{l}"], f_norm_g[l], conv_w2[l], conv_b2[l], S, name=f"ffn{l}_in")
        return _mm(yff, full[f"f_w_down{l}"], layer=0, res=h, loss=loss, name=f"ffn{l}_down"), (a, c, n, yff)

    arrive(["a_w_in", "a_w_out"], bias, "a")
    zp, n_a = _mm(h0, full["a_w_in"], layer=0, norm_g=a_g[0], out_dtype=BF16, emit_norm=True, name="a_in")
    out_a = _gate_fwd(zp, a_vg, ws_a, bs_tile)
    h1 = _mm(out_a, full["a_w_out"], layer=0, res=h0, name="a_out")
    arrive(["f_w_in0", "f_w_down0"], h1, "f0")
    h2, saved0 = ffn_fwd(h1, 0)
    arrive(["w_kv", "b_w_q", "b_w_o"], h2, "b")
    arrive(["f_w_in1", "f_w_down1"], h2, "f1")
    kv, n_kv = _mm(h2, full["w_kv"], layer=0, norm_g=kv_norm_g, out_dtype=BF16, split_out=True, emit_norm=True,
                   name="kv")
    q, n_q = _mm(h2, full["b_w_q"], layer=0, norm_g=b_norm_g[0], scale=scale, out_dtype=BF16, emit_norm=True,
                 name="q")
    kv4, q3 = kv.reshape(2, B, S, HD), q.reshape(B, S, HD)
    o = _attn_fwd(q3, kv4, bias, B, S).reshape(T, HD)
    h3 = _mm(o, full["b_w_o"], layer=0, res=h2, name="attn_out")
    (dh, loss8, dg_final), saved1 = ffn_fwd(h3, 1, loss=(final_norm_g, target))

    units = {}

    in_flight = {}

    def swap_start(group, tag, carry):
        us = [units[n] for n in group]
        lands = [lax.empty(u.shape[1:], BF16) for u in us]
        carry, flight = tied(carry, _split_copies(f"swap_start_{tag}", us, lands, 1, _swap_copies, after=carry))
        return (group, tag, flight), carry

    def reduce_start(swap, after):
        group, tag, flight = swap
        us, got = _split_copies(f"swap_wait_{tag}", [fl[2] for fl in flight], [fl[3] for fl in flight], 1,
                                _swap_copies, flight=flight, after=after)
        sums = [_add_pair(u, g_, core, name=f"pair_{n}") for n, u, g_ in zip(group, us, got)]
        after, flight = tied(after, _scatter_start(sums, name=f"scatter_start_{tag}"))
        in_flight.update(zip(group, flight))
        return after

    def ffn_bwd(dh, h, saved, l, early):
        a, c, n, yff = saved
        units[f"f_w_down{l}"] = _mm_tn(yff, dh, rows_are_shards=True, name=f"ffn{l}_down_dw")
        dh_in = dh
        if early:
            sw, dh_in = swap_start([f"f_w_down{l}"], f"fd{l}", dh)
        dyff = _mm(dh_in, full[f"f_w_down{l}"], layer=0, trans_w=True, out_dtype=BF16, name=f"ffn{l}_down_dx")
        if early:
            dyff = reduce_start(sw, dyff)
        da, dcw, dcb = _conv_bwd(a, c, dyff, conv_w2[l], S)
        units[f"f_w_in{l}"] = _mm_tn(n, da, split_y=True, name=f"ffn{l}_in_dw")
        sw, da = swap_start([f"f_w_in{l}"] if early else [f"f_w_down{l}", f"f_w_in{l}"], f"f{l}", da)
        dh, dg = _mm(da, full[f"f_w_in{l}"], layer=0, trans_w=True, split_x=True, bwd=(h, f_norm_g[l], dh),
                     name=f"ffn{l}_in_dx")
        return reduce_start(sw, dh), dg, dcw, dcb

    dh, dg_f1, dcw1, dcb1 = ffn_bwd(dh, h3, saved1, 1, False)
    do = _mm(dh, full["b_w_o"], layer=0, trans_w=True, out_dtype=BF16, name="attn_out_dx")
    units["b_w_o"] = _mm_tn(o, dh, rows_are_shards=True, name="b_w_o_dw")
    dq, dkv, dbias = _attn_bwd(q3, kv4, bias, do.reshape(B, S, HD), B, S)
    dq, d_rel = lax.optimization_barrier((dq, _bias_reduce(dbias, n_rel)))
    d_rel = d_rel.reshape(1, H, n_rel)
    dq, dkv = dq.reshape(T, HD), dkv.reshape(2, T, HD)
    units["b_w_q"] = _mm_tn(n_q, dq, rows_are_shards=True, name="b_w_q_dw")
    dh, dg_b = _mm(dq, full["b_w_q"], layer=0, trans_w=True, bwd=(h2, b_norm_g[0], dh), name="q_dx")
    units["w_kv"] = _mm_tn(n_kv, dkv, split_y=True, name="w_kv_dw")
    sw, dkv = swap_start(["b_w_o", "b_w_q", "w_kv"], "b", dkv)
    dh, dg_kv = _mm(dkv, full["w_kv"], layer=0, trans_w=True, split_x=True, bwd=(h2, kv_norm_g, dh), name="kv_dx")
    dh = reduce_start(sw, dh)
    dh, dg_f0, dcw0, dcb0 = ffn_bwd(dh, h1, saved0, 0, True)
    units["a_w_out"] = _mm_tn(out_a, dh, rows_are_shards=True, name="a_w_out_dw")
    sw, dh_in = swap_start(["a_w_out"], "ao", dh)
    d_out = _mm(dh_in, full["a_w_out"], layer=0, trans_w=True, out_dtype=BF16, name="a_out_dx")
    d_out = reduce_start(sw, d_out)
    dzp, dws, dbs, dgv = _gate_bwd(zp, d_out, a_vg, ws_a, bs_tile)
    units["a_w_in"] = _mm_tn(n_a, dzp, name="a_w_in_dw")
    sw, dzp_in = swap_start(["a_w_in"], "ai", dzp)
    grad_x, dg_a = _mm(dzp_in, full["a_w_in"], layer=0, trans_w=True, bwd=(h0, a_g[0], dh), name="a_in_dx")
    grad_x = reduce_start(sw, grad_x)

    to_flat = lambda d: d.transpose(1, 0, 2).reshape(3, F2)
    small = {"a_norm_g": dg_a, "a_v_norm_g": dgv, "a_w_s": dws[None], "a_b_s": dbs[None], "kv_norm_g": dg_kv[0],
             "b_norm_g": dg_b, "b_rel_bias": d_rel, "f_norm_g": jnp.concatenate([dg_f0, dg_f1], axis=0),
             "f_conv_w": jnp.stack([to_flat(dcw0), to_flat(dcw1)]),
             "f_conv_b": jnp.stack([dcb0.reshape(F2), dcb1.reshape(F2)]), "final_norm_g": dg_final[0]}
    snames = list(small)
    small_vec = _pack([small[n] for n in snames] + [loss8[0:1, 0:1]])
    grad_x, small_flight = tied(grad_x, _split_copies("small_start", [small_vec],
                                                      [lax.empty((8,) + small_vec.shape, F32)], 7, _gather8_copies,
                                                      after=grad_x))

    sums, recv = _scatter_wait([in_flight[n] for n in g_names], grad_x)
    sums, recv = dict(zip(g_names, sums)), dict(zip(g_names, recv))
    halves = []
    for n, w in zip(names, ws):
        if w.L == 1:
            halves.append(_sum_chips(w, sums[n], recv[n], pos, name=f"chips_{n}"))
        else:
            first = _sum_chips(w, sums[n + "0"], recv[n + "0"], pos, name=f"chips_{n}0")
            halves.append(_sum_chips(w, sums[n + "1"], recv[n + "1"], pos, layer=1, into=first, name=f"chips_{n}1"))
    g_big = dict(zip(names, _join_halves(ws, halves)))
    g_big["w_kv"] = g_big["w_kv"][0]

    given = dict(a_norm_g=(a_norm_g, m_a_norm_g, v_a_norm_g), a_w_in=(a_w_in, m_a_w_in, v_a_w_in),
                 a_v_norm_g=(a_v_norm_g, m_a_v_norm_g, v_a_v_norm_g), a_w_s=(a_w_s, m_a_w_s, v_a_w_s),
                 a_b_s=(a_b_s, m_a_b_s, v_a_b_s), a_w_out=(a_w_out, m_a_w_out, v_a_w_out),
                 kv_norm_g=(kv_norm_g, m_kv_norm_g, v_kv_norm_g), w_kv=(w_kv, m_w_kv, v_w_kv),
                 b_norm_g=(b_norm_g, m_b_norm_g, v_b_norm_g), b_w_q=(b_w_q, m_b_w_q, v_b_w_q),
                 b_rel_bias=(b_rel_bias, m_b_rel_bias, v_b_rel_bias), b_w_o=(b_w_o, m_b_w_o, v_b_w_o),
                 f_norm_g=(f_norm_g, m_f_norm_g, v_f_norm_g), f_w_in=(f_w_in, m_f_w_in, v_f_w_in),
                 f_conv_w=(f_conv_w, m_f_conv_w, v_f_conv_w), f_conv_b=(f_conv_b, m_f_conv_b, v_f_conv_b),
                 f_w_down=(f_w_down, m_f_w_down, v_f_w_down), final_norm_g=(final_norm_g, m_final_norm_g, v_final_norm_g))
    order = list(given)
    grads, deltas, new_m, new_v = {}, {}, {}, {}
    for n in names:
        w_, m_, v_ = given[n]
        g_ = g_big[n]
        C = w_.shape[-1]
        d2, m2, v2 = _adamw(w_.reshape(-1, C), g_.reshape(-1, C), m_.reshape(-1, C), v_.reshape(-1, C),
                            name=f"adamw_{n}")
        grads[n], deltas[n], new_m[n], new_v[n] = g_.reshape(w_.shape), d2.reshape(w_.shape), m2.reshape(w_.shape), \
            v2.reshape(w_.shape)
    vecs, lands = _split_copies("small_wait", [small_flight[0][2]], [small_flight[0][3]], 7, _gather8_copies,
                                flight=small_flight, after=deltas[names[-1]])
    red = _sum8(lands[0], vecs[0], (4 * xi + 2 * yi + ci).astype(jnp.int32))
    parts = _unpack(red, [small[n].shape for n in snames] + [(1,)])
    g_small = dict(zip(snames, parts[:-1]))
    loss = parts[-1][0]
    g_small["a_norm_g"] = lax.dynamic_slice_in_dim(g_small["a_norm_g"], j_me * nsd, nsd, axis=1)
    g_small["a_v_norm_g"] = lax.dynamic_slice_in_dim(g_small["a_v_norm_g"], j_me * nsg, nsg, axis=1)
    g_small["f_conv_w"] = lax.dynamic_slice_in_dim(g_small["f_conv_w"], j_me * nsf, nsf, axis=2)

    sm = [n for n in order if n not in names]
    d2, m2, v2 = _adamw(_pack([given[n][0] for n in sm]), _pack([g_small[n].reshape(given[n][0].shape) for n in sm]),
                        _pack([given[n][1] for n in sm]), _pack([given[n][2] for n in sm]), name="adamw_small")
    shapes = [given[n][0].shape for n in sm]
    for n, d_, m_, v_ in zip(sm, _unpack(d2, shapes), _unpack(m2, shapes), _unpack(v2, shapes)):
        grads[n], deltas[n], new_m[n], new_v[n] = g_small[n].reshape(given[n][0].shape), d_, m_, v_

    return (loss, grad_x.reshape(B, S, D), *[grads[n] for n in order], *[deltas[n] for n in order],
            *[new_m[n] for n in order], *[new_v[n] for n in order])
```

```python
import functools
import math

import numpy as np
import jax
import jax.numpy as jnp
from jax import lax
from jax.experimental import pallas as pl
from jax.experimental.pallas import tpu as pltpu

F32 = jnp.float32
BF16 = jnp.bfloat16
MESH = pl.DeviceIdType.MESH

EPS = 1e-6
NEG_INF = -1e30
CHUNK = 64
GMLP_BLOCK = 128
GROUP_DIM = 128
HEAD_DIM = 64
LEFT_CHUNKS = 8
PAD = LEFT_CHUNKS * CHUNK
REL_CLIP = 128
Q_BLOCK = 256
K_SPAN = PAD + Q_BLOCK
F_LEN = K_SPAN + Q_BLOCK
HEADS_PER_STEP = 4
N_CHIPS = 4

ADAM_LR = 0.001
ADAM_B1 = 0.9
ADAM_B2 = 0.999
ADAM_EPS = 1e-08
ADAM_WD = 0.01
ADAM_STEP = 10

VMEM_LIMIT = 56 * 1024 * 1024


def _params(sem=None, **kw):
    if sem is not None:
        kw["dimension_semantics"] = sem
    return pltpu.CompilerParams(vmem_limit_bytes=VMEM_LIMIT, **kw)


def _rms(xf):
    r = lax.rsqrt(jnp.mean(xf * xf, axis=-1, keepdims=True) + EPS)
    return xf * r, r


def _gelu(x, with_grad=False):
    c = math.sqrt(2.0 / math.pi)
    x2 = x * x
    t = jnp.tanh(c * x * (1.0 + 0.044715 * x2))
    half = 0.5 * (1.0 + t)
    if not with_grad:
        return x * half
    return x * half, half + 0.5 * x * (1.0 - t * t) * c * (1.0 + 3.0 * 0.044715 * x2)


def _col_tile(n):
    if n <= 1024:
        return n
    for t in (1408, 1024, 512):
        if n % t == 0:
            return t
    raise ValueError(n)


def _row_tile(t, want):
    while t % want:
        want //= 2
    return want


def _mm(x, w, *, name, layer=None, trans_w=False, norm_g=None, res=None, scale=None, out_dtype=F32, bwd=None,
        split_out=False, split_x=False, emit_norm=False, loss=None, tm=512):
    T = x.shape[-2]
    K = 2 * x.shape[-1] if split_x else x.shape[-1]
    N = w.shape[-2] if trans_w else w.shape[-1]
    tn = N
    tm = _row_tile(T, 2 * tm if max(K, N) <= 2048 else tm)
    nn, nm = N // tn, T // tm
    has_norm, has_res, has_bwd, has_loss = norm_g is not None, res is not None, bwd is not None, loss is not None
    dims = (((1,), (1,)), ((), ())) if trans_w else (((1,), (0,)), ((), ()))

    def body(*refs):
        it = iter(refs)
        x_ref, w_ref = next(it), next(it)
        g_ref = next(it) if has_norm else None
        res_ref = next(it) if has_res else None
        if has_bwd:
            h_ref, bg_ref, dh_ref = next(it), next(it), next(it)
        if has_loss:
            lg_ref, t_ref = next(it), next(it)
        o_ref = next(it)
        if split_x:
            kh = K // 2
            acc = lax.dot_general(x_ref[0].astype(BF16), w_ref[:, :kh] if trans_w else w_ref[:kh, :], dims,
                                  preferred_element_type=F32)
            acc = acc + lax.dot_general(x_ref[1].astype(BF16), w_ref[:, kh:] if trans_w else w_ref[kh:, :], dims,
                                        preferred_element_type=F32)
        else:
            xv = x_ref[...]
            if has_norm:
                xv = _rms(xv.astype(F32))[0] * g_ref[...]
            xb = xv.astype(BF16)
            if emit_norm:
                refs[-1][...] = xb
            acc = lax.dot_general(xb, w_ref[...], dims, preferred_element_type=F32)
        if scale is not None:
            acc = acc * scale
        if has_res:
            acc = acc + res_ref[...]
        if has_bwd:
            dg_ref = next(it)
            n, r = _rms(h_ref[...])

            @pl.when(pl.program_id(1) == 0)
            def _():
                dg_ref[...] = jnp.zeros_like(dg_ref)

            dg_ref[...] += jnp.sum(acc * n, axis=0, keepdims=True)
            t = acc * bg_ref[...]
            o_ref[...] = dh_ref[...] + r * (t - n * jnp.mean(t * n, axis=-1, keepdims=True))
        elif has_loss:
            loss_ref, dg_ref = refs[-2], refs[-1]

            @pl.when(pl.program_id(1) == 0)
            def _():
                loss_ref[...] = jnp.zeros_like(loss_ref)
                dg_ref[...] = jnp.zeros_like(dg_ref)

            n, r = _rms(acc)
            g = lg_ref[...]
            e = n * g - t_ref[...]
            loss_ref[...] += 0.5 * jnp.sum(jnp.mean(e * e, axis=-1, keepdims=True), axis=0, keepdims=True)
            dy = e * (1.0 / N)
            dg_ref[...] += jnp.sum(dy * n, axis=0, keepdims=True)
            t = dy * g
            o_ref[...] = r * (t - n * jnp.mean(t * n, axis=-1, keepdims=True))
        elif split_out:
            o_ref[0] = acc[:, :N // 2].astype(out_dtype)
            o_ref[1] = acc[:, N // 2:].astype(out_dtype)
        else:
            o_ref[...] = acc.astype(out_dtype)

    lead = () if layer is None else (None,)
    lidx = () if layer is None else (layer,)
    ins = [x, w]
    xspec = (pl.BlockSpec((2, tm, K // 2), lambda n, m: (0, m, 0)) if split_x
             else pl.BlockSpec((tm, K), lambda n, m: (m, 0)))
    once = pl.Buffered(1)
    wspec = (pl.BlockSpec(lead + (tn, K), lambda n, m: lidx + (n, 0), pipeline_mode=once) if trans_w
             else pl.BlockSpec(lead + (K, tn), lambda n, m: lidx + (0, n), pipeline_mode=once))
    in_specs = [xspec, wspec]
    if has_norm:
        ins.append(norm_g.reshape(1, K))
        in_specs.append(pl.BlockSpec((1, K), lambda n, m: (0, 0)))
    if has_res:
        ins.append(res)
        in_specs.append(pl.BlockSpec((tm, tn), lambda n, m: (m, n)))
    if split_out:
        out_shape = [jax.ShapeDtypeStruct((2, T, N // 2), out_dtype)]
        out_specs = [pl.BlockSpec((2, tm, N // 2), lambda n, m: (0, m, 0))]
    else:
        out_shape = [jax.ShapeDtypeStruct((T, N), F32 if has_bwd else out_dtype)]
        out_specs = [pl.BlockSpec((tm, tn), lambda n, m: (m, n))]
    if has_bwd:
        h, g, dh = bwd
        ins += [h, g.reshape(1, N), dh]
        in_specs += [pl.BlockSpec((tm, N), lambda n, m: (m, 0)), pl.BlockSpec((1, N), lambda n, m: (0, 0)),
                     pl.BlockSpec((tm, N), lambda n, m: (m, 0))]
        out_shape.append(jax.ShapeDtypeStruct((1, N), F32))
        out_specs.append(pl.BlockSpec((1, N), lambda n, m: (0, 0)))
    if emit_norm:
        out_shape.append(jax.ShapeDtypeStruct((T, K), BF16))
        out_specs.append(pl.BlockSpec((tm, K), lambda n, m: (m, 0)))
    if has_loss:
        ins += [loss[0].reshape(1, N), loss[1]]
        in_specs += [pl.BlockSpec((1, N), lambda n, m: (0, 0)), pl.BlockSpec((tm, N), lambda n, m: (m, 0))]
        out_shape += [jax.ShapeDtypeStruct((8, 128), F32), jax.ShapeDtypeStruct((1, N), F32)]
        out_specs += [pl.BlockSpec((8, 128), lambda n, m: (0, 0)), pl.BlockSpec((1, N), lambda n, m: (0, 0))]
    out = pl.pallas_call(body, name=name, grid=(nn, nm), in_specs=in_specs, out_specs=out_specs, out_shape=out_shape,
                         compiler_params=_params(("arbitrary", "arbitrary")))(*ins)
    return out if has_bwd or emit_norm or has_loss else out[0]


def _mm_tn(x, dy, *, name, rows_are_shards=False, split_y=False, tt=1024):
    T, K = x.shape
    N = 2 * dy.shape[-1] if split_y else dy.shape[-1]
    R, C = (K // N_CHIPS, N // 2) if rows_are_shards else (K // 2, N // N_CHIPS)
    nn = 2 if split_y else 1
    tn = N // nn
    per = N_CHIPS // nn
    assert not (rows_are_shards and split_y)
    tt = _row_tile(T, tt)
    nt = T // tt

    def body(x_ref, y_ref, o_ref, acc_ref):
        t = pl.program_id(1)

        @pl.when(t == 0)
        def _():
            acc_ref[...] = jnp.zeros_like(acc_ref)

        acc_ref[...] += lax.dot_general(x_ref[...], y_ref[...].astype(BF16), (((0,), (0,)), ((), ())),
                                        preferred_element_type=F32)

        @pl.when(t == nt - 1)
        def _():
            if rows_are_shards:
                for h in range(2):
                    o_ref[h] = acc_ref[:, h * C:(h + 1) * C].astype(BF16).reshape(N_CHIPS, R, C)
            else:
                for j in range(per):
                    o_ref[:, j] = acc_ref[:, j * C:(j + 1) * C].astype(BF16).reshape(2, R, C)

    if split_y:
        yspec = pl.BlockSpec((None, tt, tn), lambda n, t: (n, t, 0))
    else:
        yspec = pl.BlockSpec((tt, tn), lambda n, t: (t, 0))
    if rows_are_shards:
        out_spec = pl.BlockSpec((2, N_CHIPS, R, C), lambda n, t: (0, 0, 0, 0))
    else:
        out_spec = pl.BlockSpec((2, per, R, C), lambda n, t: (0, n, 0, 0))
    return pl.pallas_call(body, name=name, grid=(nn, nt),
                          in_specs=[pl.BlockSpec((tt, K), lambda n, t: (t, 0)), yspec], out_specs=out_spec,
                          out_shape=jax.ShapeDtypeStruct((2, N_CHIPS, R, C), BF16),
                          scratch_shapes=[pltpu.VMEM((K, tn), F32)],
                          compiler_params=_params(("arbitrary", "arbitrary")))(x, dy)


def _chunk_mask():
    i = lax.broadcasted_iota(jnp.int32, (GMLP_BLOCK, GMLP_BLOCK), 0) // CHUNK
    j = lax.broadcasted_iota(jnp.int32, (GMLP_BLOCK, GMLP_BLOCK), 1) // CHUNK
    return i >= j


def _gate_fwd(zp, gv, ws, bs_tile, *, tm=256):
    T, W2 = zp.shape
    W = W2 // 2
    G = W // GROUP_DIM
    tm = _row_tile(T, tm)

    def body(zp_ref, gv_ref, ws_ref, bs_ref, o_ref):
        z = _gelu(zp_ref[...].astype(F32))
        u, v = z[:, :W], z[:, W:]
        vn = _rms(v)[0] * gv_ref[...]
        mask = _chunk_mask()
        for g in range(G):
            cs = slice(g * GROUP_DIM, (g + 1) * GROUP_DIM)
            wg = jnp.where(mask, ws_ref[g], 0.0).astype(BF16)
            for b in range(tm // GMLP_BLOCK):
                rs = slice(b * GMLP_BLOCK, (b + 1) * GMLP_BLOCK)
                s = jnp.dot(wg, vn[rs, cs].astype(BF16), preferred_element_type=F32) + bs_ref[:, cs]
                o_ref[rs, cs] = (u[rs, cs] * s).astype(BF16)

    return pl.pallas_call(
        body, name="gate_fwd", grid=(T // tm,),
        in_specs=[pl.BlockSpec((tm, W2), lambda i: (i, 0)), pl.BlockSpec((1, W), lambda i: (0, 0)),
                  pl.BlockSpec((G, GMLP_BLOCK, GMLP_BLOCK), lambda i: (0, 0, 0)),
                  pl.BlockSpec((GMLP_BLOCK, W), lambda i: (0, 0))],
        out_specs=pl.BlockSpec((tm, W), lambda i: (i, 0)), out_shape=jax.ShapeDtypeStruct((T, W), BF16),
        compiler_params=_params(("arbitrary",)))(zp, gv, ws, bs_tile)


def _gate_bwd(zp, d_out, gv, ws, bs_tile, *, tm=256):
    T, W2 = zp.shape
    W = W2 // 2
    G = W // GROUP_DIM
    tm = _row_tile(T, tm)
    nm = T // tm

    def body(zp_ref, do_ref, gv_ref, ws_ref, bs_ref, dzp_ref, dws_ref, dbs_ref, dgv_ref, du_scr, dvn_scr, dsum_scr):
        i = pl.program_id(0)

        @pl.when(i == 0)
        def _():
            dws_ref[...] = jnp.zeros_like(dws_ref)
            dgv_ref[...] = jnp.zeros_like(dgv_ref)
            dsum_scr[...] = jnp.zeros_like(dsum_scr)

        zp = zp_ref[...].astype(F32)
        z, dz = _gelu(zp, with_grad=True)
        u, v = z[:, :W], z[:, W:]
        n, r = _rms(v)
        gv = gv_ref[...]
        vn = n * gv
        d_out = do_ref[...].astype(F32)
        mask = _chunk_mask()
        for g in range(G):
            cs = slice(g * GROUP_DIM, (g + 1) * GROUP_DIM)
            wg = jnp.where(mask, ws_ref[g], 0.0).astype(BF16)
            dw = jnp.zeros((GMLP_BLOCK, GMLP_BLOCK), F32)
            for b in range(tm // GMLP_BLOCK):
                rs = slice(b * GMLP_BLOCK, (b + 1) * GMLP_BLOCK)
                vb = vn[rs, cs].astype(BF16)
                s = jnp.dot(wg, vb, preferred_element_type=F32) + bs_ref[:, cs]
                du_scr[rs, cs] = d_out[rs, cs] * s
                ds = d_out[rs, cs] * u[rs, cs]
                dsb = ds.astype(BF16)
                dvn_scr[rs, cs] = lax.dot_general(wg, dsb, (((0,), (0,)), ((), ())), preferred_element_type=F32)
                dw = dw + lax.dot_general(dsb, vb, (((1,), (1,)), ((), ())), preferred_element_type=F32)
                dsum_scr[:, cs] += ds
            dws_ref[g] += jnp.where(mask, dw, 0.0)
        dvn = dvn_scr[...]
        dgv_ref[...] += jnp.sum(dvn * n, axis=0, keepdims=True)
        t = dvn * gv
        dv = r * (t - n * jnp.mean(t * n, axis=-1, keepdims=True))
        dzp_ref[:, :W] = (du_scr[...] * dz[:, :W]).astype(BF16)
        dzp_ref[:, W:] = (dv * dz[:, W:]).astype(BF16)

        @pl.when(i == nm - 1)
        def _():
            sel = (lax.broadcasted_iota(jnp.int32, (G, W), 1) // GROUP_DIM
                   == lax.broadcasted_iota(jnp.int32, (G, W), 0)).astype(F32)
            dbs_ref[...] = lax.dot_general(sel, dsum_scr[...], (((1,), (1,)), ((), ())),
                                           precision=lax.Precision.HIGHEST, preferred_element_type=F32)

    return pl.pallas_call(
        body, name="gate_bwd", grid=(nm,),
        in_specs=[pl.BlockSpec((tm, W2), lambda i: (i, 0)), pl.BlockSpec((tm, W), lambda i: (i, 0)),
                  pl.BlockSpec((1, W), lambda i: (0, 0)),
                  pl.BlockSpec((G, GMLP_BLOCK, GMLP_BLOCK), lambda i: (0, 0, 0)),
                  pl.BlockSpec((GMLP_BLOCK, W), lambda i: (0, 0))],
        out_specs=[pl.BlockSpec((tm, W2), lambda i: (i, 0)),
                   pl.BlockSpec((G, GMLP_BLOCK, GMLP_BLOCK), lambda i: (0, 0, 0)),
                   pl.BlockSpec((G, GMLP_BLOCK), lambda i: (0, 0)), pl.BlockSpec((1, W), lambda i: (0, 0))],
        out_shape=[jax.ShapeDtypeStruct((T, W2), BF16), jax.ShapeDtypeStruct((G, GMLP_BLOCK, GMLP_BLOCK), F32),
                   jax.ShapeDtypeStruct((G, GMLP_BLOCK), F32), jax.ShapeDtypeStruct((1, W), F32)],
        scratch_shapes=[pltpu.VMEM((tm, W), F32), pltpu.VMEM((tm, W), F32), pltpu.VMEM((GMLP_BLOCK, W), F32)],
        compiler_params=_params(("arbitrary",)))(zp, d_out, gv, ws, bs_tile)


HALO = 16


def _taps(ext, w, b):
    return w[2:3] * ext[HALO:] + w[1:2] * pltpu.roll(ext, 1, 0)[HALO:] + w[0:1] * pltpu.roll(ext, 2, 0)[HALO:] + b


def _ffn_in_conv(h, w, g, cw, cb, S, *, name, tm=256):
    T, D = h.shape
    F = w.shape[-1] // 2
    tc = _col_tile(F)
    tm = _row_tile(S, tm)

    def body(h_ref, w_ref, g_ref, cw_ref, cb_ref, y_ref, a_ref, c_ref, n_ref, tail):
        first = (pl.program_id(0) * tm) % S == 0
        nb = (_rms(h_ref[...])[0] * g_ref[...]).astype(BF16)
        n_ref[...] = nb
        for j in range(F // tc):
            cs = slice(j * tc, (j + 1) * tc)
            conv = []
            for s in range(2):
                acc = jnp.dot(nb, w_ref[:, s * F + j * tc:s * F + (j + 1) * tc], preferred_element_type=F32)
                ab = acc.astype(BF16)
                a_ref[s, :, cs] = ab
                af = ab.astype(F32)
                ext = jnp.concatenate([jnp.where(first, 0.0, tail[s, :, cs]), af], axis=0)
                tail[s, :, cs] = af[tm - HALO:, :]
                cv = _taps(ext, cw_ref[s, :, cs], cb_ref[s:s + 1, cs]).astype(BF16)
                c_ref[s, :, cs] = cv
                conv.append(cv.astype(F32))
            up, gate = conv
            y_ref[:, cs] = (gate * jax.nn.sigmoid(gate) * up).astype(BF16)

    row = lambda width: pl.BlockSpec((tm, width), lambda i: (i, 0))
    wide = pl.BlockSpec((2, tm, F), lambda i: (0, i, 0))
    return pl.pallas_call(
        body, name=name, grid=(T // tm,),
        in_specs=[row(D), pl.BlockSpec((None, D, 2 * F), lambda i: (0, 0, 0), pipeline_mode=pl.Buffered(1)),
                  pl.BlockSpec((1, D), lambda i: (0, 0)),
                  pl.BlockSpec((2, 3, F), lambda i: (0, 0, 0)), pl.BlockSpec((2, F), lambda i: (0, 0))],
        out_specs=[row(F), wide, wide, row(D)],
        out_shape=[jax.ShapeDtypeStruct((T, F), BF16), jax.ShapeDtypeStruct((2, T, F), BF16),
                   jax.ShapeDtypeStruct((2, T, F), BF16), jax.ShapeDtypeStruct((T, D), BF16)],
        scratch_shapes=[pltpu.VMEM((2, HALO, F), F32)],
        compiler_params=_params(("arbitrary",)))(h, w, g.reshape(1, D), cw, cb)


def _conv_bwd(a, c, dy, cw, S, *, tm=256):
    _, T, F = a.shape
    tc = _col_tile(F)
    tm = _row_tile(S, tm)
    nm = T // tm
    hb = tm // HALO
    TE = tm + HALO
    nxt = lambda j, i: jnp.minimum((i + 1) * hb, T // HALO - 1)

    def body(a_ref, c_ref, nc_ref, dy_ref, ndy_ref, w_ref, da_ref, dw_ref, db_ref):
        i = pl.program_id(1)
        last = ((i + 1) * tm) % S == 0
        keep_n = jnp.where(last, 0.0, 1.0)
        dyf = jnp.concatenate([dy_ref[...].astype(F32), ndy_ref[...].astype(F32) * keep_n], axis=0)
        up = jnp.concatenate([c_ref[0].astype(F32), nc_ref[0].astype(F32)], axis=0)
        gate = jnp.concatenate([c_ref[1].astype(F32), nc_ref[1].astype(F32)], axis=0)
        sg = jax.nn.sigmoid(gate)
        d_up = dyf * (gate * sg)
        d_gate = dyf * up * (sg * (1.0 + gate * (1.0 - sg)))

        @pl.when(i == 0)
        def _():
            dw_ref[...] = jnp.zeros_like(dw_ref)
            db_ref[...] = jnp.zeros_like(db_ref)

        def back(s, d):
            a = a_ref[s].astype(F32)
            w = w_ref[s]
            u1, u2 = pltpu.roll(d, TE - 1, 0), pltpu.roll(d, TE - 2, 0)
            db_ref[s:s + 1, :] += jnp.sum(d[:tm], axis=0, keepdims=True)
            dw_ref[s, 2:3, :] += jnp.sum(d[:tm] * a, axis=0, keepdims=True)
            dw_ref[s, 1:2, :] += jnp.sum(u1[:tm] * a, axis=0, keepdims=True)
            dw_ref[s, 0:1, :] += jnp.sum(u2[:tm] * a, axis=0, keepdims=True)
            da_ref[s] = (w[2:3] * d + w[1:2] * u1 + w[0:1] * u2)[:tm].astype(BF16)

        back(0, d_up)
        back(1, d_gate)

    cur = pl.BlockSpec((2, tm, tc), lambda j, i: (0, i, j))
    return pl.pallas_call(
        body, name="conv_bwd", grid=(F // tc, nm),
        in_specs=[cur, cur, pl.BlockSpec((2, HALO, tc), lambda j, i: (0, nxt(j, i), j)),
                  pl.BlockSpec((tm, tc), lambda j, i: (i, j)), pl.BlockSpec((HALO, tc), lambda j, i: (nxt(j, i), j)),
                  pl.BlockSpec((2, 3, tc), lambda j, i: (0, 0, j))],
        out_specs=[cur, pl.BlockSpec((2, 3, tc), lambda j, i: (0, 0, j)), pl.BlockSpec((2, tc), lambda j, i: (0, j))],
        out_shape=[jax.ShapeDtypeStruct((2, T, F), BF16), jax.ShapeDtypeStruct((2, 3, F), F32),
                   jax.ShapeDtypeStruct((2, F), F32)],
        compiler_params=_params(("arbitrary", "arbitrary")))(a, c, c, dy, dy, cw)


def _bias_index():
    idx = np.arange(F_LEN)
    d = np.where(idx < K_SPAN, idx, idx - F_LEN)
    return np.clip(PAD - d, -REL_CLIP, REL_CLIP) + REL_CLIP


ROW_GROUP = 16


def _roll_rows(x, sign, unit, steps):
    rows = lax.broadcasted_iota(jnp.int32, x.shape, 0)
    step = 1
    while step < steps:
        shift = unit * step if sign > 0 else F_LEN - unit * step
        x = jnp.where((rows & step) != 0, pltpu.roll(x, shift, 1), x)
        step *= 2
    return x


def _bias_expand(frow):
    H = frow.shape[0]
    groups = Q_BLOCK // ROW_GROUP

    def body(f_ref, o_ref):
        coarse = _roll_rows(jnp.broadcast_to(f_ref[...], (groups, F_LEN)), 1, ROW_GROUP, groups)
        x = jnp.concatenate([jnp.broadcast_to(coarse[a:a + 1], (ROW_GROUP, F_LEN)) for a in range(groups)], axis=0)
        x = _roll_rows(x, 1, 1, ROW_GROUP)[:, :K_SPAN]
        qc = lax.broadcasted_iota(jnp.int32, (Q_BLOCK, K_SPAN), 0) // CHUNK * CHUNK
        kj = lax.broadcasted_iota(jnp.int32, (Q_BLOCK, K_SPAN), 1)
        o_ref[...] = jnp.where((kj >= qc) & (kj < qc + PAD + CHUNK), x, NEG_INF)

    return pl.pallas_call(
        body, name="bias_expand", grid=(H,),
        in_specs=[pl.BlockSpec((None, 1, F_LEN), lambda h: (h, 0, 0))],
        out_specs=pl.BlockSpec((None, Q_BLOCK, K_SPAN), lambda h: (h, 0, 0)),
        out_shape=jax.ShapeDtypeStruct((H, Q_BLOCK, K_SPAN), F32), compiler_params=_params(("arbitrary",)))(frow)


def _bias_reduce(dbias, n_rel):
    H = dbias.shape[0]
    onehot = jnp.asarray((_bias_index()[:, None] == np.arange(n_rel)[None, :]).astype(np.float32), dtype=BF16)

    def body(d_ref, oh_ref, o_ref):
        x = jnp.concatenate([d_ref[...], jnp.zeros((Q_BLOCK, F_LEN - K_SPAN), F32)], axis=1)
        fine = _roll_rows(x, -1, 1, ROW_GROUP).reshape(Q_BLOCK // ROW_GROUP, ROW_GROUP, F_LEN)
        coarse = _roll_rows(jnp.sum(fine, axis=1), -1, ROW_GROUP, Q_BLOCK // ROW_GROUP)
        row = jnp.broadcast_to(jnp.sum(coarse, axis=0, keepdims=True), (8, F_LEN))
        acc = jnp.zeros((8, n_rel), F32)
        for _ in range(3):
            piece = row.astype(BF16)
            acc = acc + jnp.dot(piece, oh_ref[...], preferred_element_type=F32)
            row = row - piece.astype(F32)
        o_ref[...] = acc[0:1]

    return pl.pallas_call(
        body, name="bias_reduce", grid=(H,),
        in_specs=[pl.BlockSpec((None, Q_BLOCK, K_SPAN), lambda h: (h, 0, 0)),
                  pl.BlockSpec((F_LEN, n_rel), lambda h: (0, 0))],
        out_specs=pl.BlockSpec((None, 1, n_rel), lambda h: (h, 0, 0)),
        out_shape=jax.ShapeDtypeStruct((H, 1, n_rel), F32), compiler_params=_params(("arbitrary",)))(dbias, onehot)


def _attn_specs(S):
    hw = HEADS_PER_STEP * HEAD_DIM
    qspec = pl.BlockSpec((None, Q_BLOCK, hw), lambda g, b, i: (b, i, g))
    kspec = pl.BlockSpec((None, None, S, hw), lambda g, b, i: (0, b, 0, g))
    vspec = pl.BlockSpec((None, None, S, hw), lambda g, b, i: (1, b, 0, g))
    bspec = pl.BlockSpec((HEADS_PER_STEP, Q_BLOCK, K_SPAN), lambda g, b, i: (g, 0, 0))
    return hw, qspec, kspec, vspec, bspec


def _span_cases(i, fn):
    short = PAD // Q_BLOCK
    for j in range(short):
        pl.when(i == j)(functools.partial(fn, PAD - j * Q_BLOCK))
    pl.when(i >= short)(functools.partial(fn, 0))


def _key_start(i, off):
    return 0 if off else pl.multiple_of(i * Q_BLOCK - PAD, Q_BLOCK)


def _attn_exp(q_ref, k_ref, b_ref, h, k0, off):
    hs = slice(h * HEAD_DIM, (h + 1) * HEAD_DIM)
    kh = k_ref[pl.ds(k0, K_SPAN - off), hs]
    s = lax.dot_general(q_ref[:, hs], kh, (((1,), (1,)), ((), ())), preferred_element_type=F32) + b_ref[h, :, off:]
    p = jnp.exp(s - jnp.max(s, axis=-1, keepdims=True))
    return p, 1.0 / jnp.sum(p, axis=-1, keepdims=True), kh


def _attn_fwd(q, kv, bias, B, S):
    HD = q.shape[-1]
    hw, qspec, kspec, vspec, bspec = _attn_specs(S)

    def body(q_ref, k_ref, v_ref, b_ref, o_ref):
        i = pl.program_id(2)

        def block(off):
            k0 = _key_start(i, off)
            outs = []
            for h in range(HEADS_PER_STEP):
                hs = slice(h * HEAD_DIM, (h + 1) * HEAD_DIM)
                p, inv, _ = _attn_exp(q_ref, k_ref, b_ref, h, k0, off)
                outs.append(jnp.dot(p.astype(BF16), v_ref[pl.ds(k0, K_SPAN - off), hs],
                                    preferred_element_type=F32) * inv)
            o_ref[...] = jnp.concatenate(outs, axis=1).astype(BF16)

        _span_cases(i, block)

    return pl.pallas_call(
        body, name="attn_fwd", grid=(HD // hw, B, S // Q_BLOCK), in_specs=[qspec, kspec, vspec, bspec],
        out_specs=qspec, out_shape=jax.ShapeDtypeStruct((B, S, HD), BF16),
        compiler_params=_params(("arbitrary", "arbitrary", "arbitrary")))(q, kv, kv, bias)


def _attn_bwd(q, kv, bias, do, B, S):
    HD = q.shape[-1]
    H = HD // HEAD_DIM
    hw, qspec, kspec, vspec, bspec = _attn_specs(S)
    scale = HEAD_DIM ** -0.5
    nq = S // Q_BLOCK

    def body(q_ref, k_ref, v_ref, b_ref, do_ref, dq_ref, dkv_ref, db_ref, dk_acc, dv_acc):
        b, i = pl.program_id(1), pl.program_id(2)

        @pl.when(i == 0)
        def _():
            dk_acc[...] = jnp.zeros_like(dk_acc)
            dv_acc[...] = jnp.zeros_like(dv_acc)

        @pl.when((i == 0) & (b == 0))
        def _():
            db_ref[...] = jnp.zeros_like(db_ref)

        def block(off):
            k0 = _key_start(i, off)
            keys = pl.ds(k0, K_SPAN - off)
            for h in range(HEADS_PER_STEP):
                hs = slice(h * HEAD_DIM, (h + 1) * HEAD_DIM)
                p, inv, kh = _attn_exp(q_ref, k_ref, b_ref, h, k0, off)
                p = p * inv
                doh = do_ref[:, hs]
                dp = lax.dot_general(doh, v_ref[keys, hs], (((1,), (1,)), ((), ())), preferred_element_type=F32)
                ds = p * (dp - jnp.sum(p * dp, axis=-1, keepdims=True))
                db_ref[h, :, off:] += ds
                dsb = ds.astype(BF16)
                dq_ref[:, hs] = (jnp.dot(dsb, kh, preferred_element_type=F32) * scale).astype(BF16)
                dk_acc[hs, keys] += lax.dot_general(q_ref[:, hs], dsb, (((0,), (0,)), ((), ())),
                                                     preferred_element_type=F32)
                dv_acc[hs, keys] += lax.dot_general(doh, p.astype(BF16), (((0,), (0,)), ((), ())),
                                                     preferred_element_type=F32)

        _span_cases(i, block)

        @pl.when(i == nq - 1)
        def _():
            dkv_ref[0] = dk_acc[...].T.astype(BF16)
            dkv_ref[1] = dv_acc[...].T.astype(BF16)

    return pl.pallas_call(
        body, name="attn_bwd", grid=(HD // hw, B, nq), in_specs=[qspec, kspec, vspec, bspec, qspec],
        out_specs=[qspec, pl.BlockSpec((2, None, S, hw), lambda g, b, i: (0, b, 0, g)), bspec],
        out_shape=[jax.ShapeDtypeStruct((B, S, HD), BF16), jax.ShapeDtypeStruct((2, B, S, HD), BF16),
                   jax.ShapeDtypeStruct((H, Q_BLOCK, K_SPAN), F32)],
        scratch_shapes=[pltpu.VMEM((hw, S), F32), pltpu.VMEM((hw, S), F32)],
        compiler_params=_params(("arbitrary", "arbitrary", "arbitrary")))(q, kv, kv, bias, do)


def _sub_rows(R):
    for cand in (256, 352, 128, 64, 8):
        if R % cand == 0 and R > cand:
            return cand
    return R


def _adamw(w, g, m, v, *, name):
    R, C = w.shape
    tr = _sub_rows(R)

    def body(w_ref, g_ref, m_ref, v_ref, d_ref, nm_ref, nv_ref):
        g = g_ref[...]
        m = ADAM_B1 * m_ref[...] + (1.0 - ADAM_B1) * g
        v = ADAM_B2 * v_ref[...] + (1.0 - ADAM_B2) * (g * g)
        m_hat = m / (1.0 - ADAM_B1 ** ADAM_STEP)
        v_hat = v / (1.0 - ADAM_B2 ** ADAM_STEP)
        d_ref[...] = -ADAM_LR * (m_hat / (jnp.sqrt(v_hat) + ADAM_EPS) + ADAM_WD * w_ref[...])
        nm_ref[...] = m
        nv_ref[...] = v

    spec = pl.BlockSpec((tr, C), lambda i: (i, 0))
    return pl.pallas_call(body, name=name, grid=(R // tr,), in_specs=[spec] * 4, out_specs=[spec] * 3,
                          out_shape=[jax.ShapeDtypeStruct((R, C), F32)] * 3,
                          compiler_params=_params(("arbitrary",)))(w, g, m, v)


def _add_pair(units, got, core, *, name):
    n4, R, C = got.shape
    rows = n4 * R
    tr = 512 if rows % 512 == 0 else R

    def body(c_ref, u_ref, got_ref, o_ref):
        o_ref[...] = (u_ref[...].astype(F32) + got_ref[...].astype(F32)).astype(BF16)

    spec = pl.BlockSpec((tr, C), lambda i, c: (i, 0))
    grid_spec = pltpu.PrefetchScalarGridSpec(
        num_scalar_prefetch=1, grid=(rows // tr,),
        in_specs=[pl.BlockSpec((None, tr, C), lambda i, c: (c[0], i, 0)), spec], out_specs=spec)
    out = pl.pallas_call(body, name=name, grid_spec=grid_spec, out_shape=jax.ShapeDtypeStruct((rows, C), BF16),
                         compiler_params=_params(("arbitrary",)))(core.reshape(1), units.reshape(2, rows, C),
                                                                   got.reshape(rows, C))
    return out.reshape(n4, R, C)


def _sum_chips(w, own, got, pos, *, name, layer=0, into=None):
    _, R, C = own.shape
    tr = _sub_rows(R)
    nr = R // tr

    def body(p_ref, own_ref, got_ref, *rest):
        o_ref = rest[-1]
        o_ref[...] = (own_ref[...].astype(F32) + got_ref[0].astype(F32) + got_ref[1].astype(F32)
                      + got_ref[2].astype(F32))

    if w.row_sharded:
        out_map = lambda i, p: (layer, i, p[1])
    else:
        out_map = lambda i, p: (layer, p[1] * nr + i, 0)
    ins = [pos, own, got]
    in_specs = [pl.BlockSpec((None, tr, C), lambda i, p: (p[0], i, 0)),
                pl.BlockSpec((3, tr, C), lambda i, p: (0, i, 0))]
    alias = {}
    if into is not None:
        ins.append(into)
        in_specs.append(ANY)
        alias = {3: 0}
    grid_spec = pltpu.PrefetchScalarGridSpec(num_scalar_prefetch=1, grid=(nr,), in_specs=in_specs,
                                             out_specs=pl.BlockSpec((None, tr, C), out_map))
    return pl.pallas_call(body, name=name, grid_spec=grid_spec, input_output_aliases=alias,
                          out_shape=jax.ShapeDtypeStruct((w.L, w.ks, w.ns), F32),
                          compiler_params=_params(("arbitrary",)))(*ins)


def _mesh_pos():
    return lax.axis_index("x"), lax.axis_index("y"), lax.axis_index("c")


def _other_chips(x, y):
    return [(1 - x, y), (x, 1 - y), (1 - x, 1 - y)]


ANY = pl.BlockSpec(memory_space=pl.ANY)


class _W:
    def __init__(self, name, shard, row_sharded):
        self.name = name
        self.L, ks, ns = shard.shape
        self.row_sharded = row_sharded
        self.K, self.N = (ks * N_CHIPS, ns) if row_sharded else (ks, ns * N_CHIPS)
        self.ks, self.ns = ks, ns

    def shard_of(self, full, j):
        if self.row_sharded:
            return full.at[:, pl.ds(j * self.ks, self.ks), :]
        return full.at[:, :, pl.ds(j * self.ns, self.ns)]

    def half_of(self, shard, c):
        if self.row_sharded:
            return shard.at[:, :, pl.ds(c * (self.ns // 2), self.ns // 2)]
        return shard.at[:, pl.ds(c * (self.ks // 2), self.ks // 2), :]


HBM = pl.BlockSpec(memory_space=pltpu.HBM)
SEM = pl.BlockSpec(memory_space=pltpu.SEMAPHORE)
IN_FLIGHT = pltpu.SideEffectType.DATAFLOW_SIDE_EFFECTING


def _in_hbm(a):
    return pltpu.with_memory_space_constraint(a, pltpu.HBM)


def _gather_start(ws, shards, after):
    nw = len(ws)

    def body(*refs):
        src, dst = refs[:nw], refs[nw:2 * nw]
        send, recv = refs[2 * nw + 1:3 * nw + 1], refs[3 * nw + 1:4 * nw + 1]
        x, y, c = _mesh_pos()
        me = 2 * x + y
        for i, w in enumerate(ws):
            for f, (px, py) in enumerate(_other_chips(x, y)):
                pltpu.make_async_remote_copy(src_ref=w.half_of(src[i], c), dst_ref=w.half_of(w.shard_of(dst[i], me), c),
                                             send_sem=send[i].at[f], recv_sem=recv[i].at[f], device_id=(px, py, c),
                                             device_id_type=MESH).start()

    fulls = [lax.empty((w.L, w.K, w.N), BF16) for w in ws]
    out = pl.pallas_call(
        body, name="gather_start", in_specs=[HBM] * (2 * nw) + [ANY],
        out_specs=[SEM] * (2 * nw) + [HBM] * (2 * nw),
        out_shape=[pltpu.SemaphoreType.DMA((3,))] * (2 * nw)
        + [pltpu.HBM(s.shape, BF16) for s in shards] + [pltpu.HBM(f.shape, BF16) for f in fulls],
        input_output_aliases={i: 2 * nw + i for i in range(2 * nw)},
        compiler_params=pltpu.CompilerParams(has_side_effects=IN_FLIGHT))(
            *[_in_hbm(s) for s in shards], *[_in_hbm(f) for f in fulls], after)
    return [(out[i], out[nw + i], out[2 * nw + i], out[3 * nw + i]) for i in range(nw)]


def _gather_wait(ws, flight, after, *, name):
    nw = len(ws)

    def body(*refs):
        src, dst = refs[:nw], refs[nw:2 * nw]
        send, recv = refs[2 * nw:3 * nw], refs[3 * nw:4 * nw]
        x, y, c = _mesh_pos()
        for i, w in enumerate(ws):
            for f, (px, py) in enumerate(_other_chips(x, y)):
                landed = w.half_of(w.shard_of(dst[i], 2 * px + py), c)
                cp = pltpu.make_async_remote_copy(src_ref=w.half_of(src[i], c), dst_ref=landed, send_sem=send[i].at[f],
                                                  recv_sem=recv[i].at[f], device_id=(px, py, c), device_id_type=MESH)
                cp.wait_send()
                cp.wait_recv()

    shards, fulls = [fl[2] for fl in flight], [fl[3] for fl in flight]
    out = pl.pallas_call(
        body, name=name, in_specs=[HBM] * (2 * nw) + [SEM] * (2 * nw) + [ANY],
        out_specs=[HBM] * (2 * nw),
        out_shape=[pltpu.HBM(s.shape, BF16) for s in shards] + [pltpu.HBM(f.shape, BF16) for f in fulls],
        input_output_aliases={i: i for i in range(2 * nw)},
        compiler_params=pltpu.CompilerParams(has_side_effects=IN_FLIGHT))(
            *shards, *fulls, *[fl[0] for fl in flight], *[fl[1] for fl in flight], after)
    return out[:nw], out[nw:]


def _gather_finish(ws, shards, fulls, *, name):
    nw = len(ws)

    def body(*refs):
        src, dst, stage = refs[:nw], refs[3 * nw:4 * nw], refs[4 * nw:5 * nw]
        send_sems, recv_sems, load_sems, store_sems = refs[5 * nw:]
        x, y, c = _mesh_pos()
        me = 2 * x + y
        sibling = (x, y, 1 - c)
        chips = _other_chips(x, y)

        def fwd(i, w, f, half):
            px, py = chips[f]
            landed = w.half_of(w.shard_of(dst[i], 2 * px + py), half)
            return pltpu.make_async_remote_copy(src_ref=landed, dst_ref=landed, send_sem=send_sems.at[3 * i + f],
                                                recv_sem=recv_sems.at[3 * i + f], device_id=sibling,
                                                device_id_type=MESH)

        loads = [pltpu.make_async_copy(src[i], stage[i], load_sems.at[i]) for i in range(nw)]
        for cp in loads:
            cp.start()
        sends = [fwd(i, w, f, c) for i, w in enumerate(ws) for f in range(3)]
        for cp in sends:
            cp.start()
        stores = [pltpu.make_async_copy(stage[i], w.shard_of(dst[i], me), store_sems.at[i])
                  for i, w in enumerate(ws)]
        for ld, st in zip(loads, stores):
            ld.wait()
            st.start()
        for i, w in enumerate(ws):
            for f in range(3):
                fwd(i, w, f, 1 - c).wait_recv()
        for cp in sends:
            cp.wait_send()
        for cp in stores:
            cp.wait()

    out = pl.pallas_call(
        body, name=name, in_specs=[ANY] * (2 * nw), out_specs=[ANY] * (2 * nw),
        out_shape=[jax.ShapeDtypeStruct(s.shape, BF16) for s in shards]
        + [jax.ShapeDtypeStruct(f.shape, BF16) for f in fulls],
        input_output_aliases={i: i for i in range(2 * nw)},
        scratch_shapes=[pltpu.VMEM((w.L, w.ks, w.ns), BF16) for w in ws]
        + [pltpu.SemaphoreType.DMA((3 * nw,)), pltpu.SemaphoreType.DMA((3 * nw,)), pltpu.SemaphoreType.DMA((nw,)),
           pltpu.SemaphoreType.DMA((nw,))],
        compiler_params=_params(has_side_effects=True))(*shards, *fulls)
    return out[nw:]


def _split_copies(name, srcs, lands, n_sems, copies_of, *, flight=None, after=None):
    n = len(srcs)
    starting = flight is None

    def body(*refs):
        src, land = refs[:n], refs[n:2 * n]
        sems = refs[2 * n + 1:4 * n + 1] if starting else refs[2 * n:4 * n]
        for i in range(n):
            for cp in copies_of(i, src[i], land[i], sems[i], sems[n + i]):
                if starting:
                    cp.start()
                else:
                    cp.wait_send()
                    cp.wait_recv()

    thru = [pltpu.HBM(a.shape, a.dtype) for a in list(srcs) + list(lands)]
    if starting:
        out = pl.pallas_call(
            body, name=name, in_specs=[HBM] * (2 * n) + [ANY], out_specs=[SEM] * (2 * n) + [HBM] * (2 * n),
            out_shape=[pltpu.SemaphoreType.DMA((n_sems,))] * (2 * n) + thru,
            input_output_aliases={i: 2 * n + i for i in range(2 * n)},
            compiler_params=pltpu.CompilerParams(has_side_effects=IN_FLIGHT))(
                *[_in_hbm(a) for a in srcs], *[_in_hbm(a) for a in lands], after)
        return [(out[i], out[n + i], out[2 * n + i], out[3 * n + i]) for i in range(n)]
    out = pl.pallas_call(
        body, name=name, in_specs=[HBM] * (2 * n) + [SEM] * (2 * n) + [ANY], out_specs=[HBM] * (2 * n),
        out_shape=thru, input_output_aliases={i: i for i in range(2 * n)},
        compiler_params=pltpu.CompilerParams(has_side_effects=IN_FLIGHT))(
            *srcs, *lands, *[fl[0] for fl in flight], *[fl[1] for fl in flight], after)
    return out[:n], out[n:]


def _sum8(land, vec, me):
    R = vec.shape[0]

    def body(me_ref, land_ref, vec_ref, o_ref):
        acc = jnp.zeros((R, 128), F32)
        for d in range(8):
            acc = acc + jnp.where(me_ref[0] == d, vec_ref[...], land_ref[d])
        o_ref[...] = acc

    grid_spec = pltpu.PrefetchScalarGridSpec(
        num_scalar_prefetch=1, grid=(1,),
        in_specs=[pl.BlockSpec((8, R, 128), lambda i, m: (0, 0, 0)), pl.BlockSpec((R, 128), lambda i, m: (0, 0))],
        out_specs=pl.BlockSpec((R, 128), lambda i, m: (0, 0)))
    return pl.pallas_call(body, name="sum8", grid_spec=grid_spec, out_shape=jax.ShapeDtypeStruct((R, 128), F32),
                          compiler_params=_params(("arbitrary",)))(me.reshape(1), land, vec)


def _swap_copies(i, src, got, send, recv):
    x, y, c = _mesh_pos()
    return [pltpu.make_async_remote_copy(src_ref=src.at[1 - c], dst_ref=got, send_sem=send.at[0], recv_sem=recv.at[0],
                                         device_id=(x, y, 1 - c), device_id_type=MESH)]


def _gather8_copies(i, src, land, send, recv):
    x, y, c = _mesh_pos()
    me = 4 * x + 2 * y + c
    peers = [(x, y, 1 - c)] + [(px, py, pc) for px, py in _other_chips(x, y) for pc in (c, 1 - c)]
    return [pltpu.make_async_remote_copy(src_ref=src, dst_ref=land.at[me], send_sem=send.at[k], recv_sem=recv.at[k],
                                         device_id=peer, device_id_type=MESH) for k, peer in enumerate(peers)]


def _scatter_copy(src, got, send, recv, f, chip, c):
    px, py = chip
    return pltpu.make_async_remote_copy(src_ref=src.at[2 * px + py], dst_ref=got.at[f], send_sem=send.at[f],
                                        recv_sem=recv.at[f], device_id=(px, py, c), device_id_type=MESH)


def _scatter_start(sums, *, name):
    nw = len(sums)

    def body(*refs):
        src, got = refs[:nw], refs[nw:2 * nw]
        send, recv = refs[2 * nw:3 * nw], refs[3 * nw:4 * nw]
        x, y, c = _mesh_pos()
        for i in range(nw):
            for f, chip in enumerate(_other_chips(x, y)):
                _scatter_copy(src[i], got[i], send[i], recv[i], f, chip, c).start()

    lands = [lax.empty((3,) + s.shape[1:], BF16) for s in sums]
    out = pl.pallas_call(
        body, name=name, in_specs=[HBM] * (2 * nw), out_specs=[SEM] * (2 * nw) + [HBM] * (2 * nw),
        out_shape=[pltpu.SemaphoreType.DMA((3,))] * (2 * nw)
        + [pltpu.HBM(s.shape, BF16) for s in sums] + [pltpu.HBM(l.shape, BF16) for l in lands],
        input_output_aliases={i: 2 * nw + i for i in range(2 * nw)},
        compiler_params=pltpu.CompilerParams(has_side_effects=IN_FLIGHT))(
            *[_in_hbm(s) for s in sums], *[_in_hbm(l) for l in lands])
    return [(out[i], out[nw + i], out[2 * nw + i], out[3 * nw + i]) for i in range(nw)]


def _scatter_wait(flight, after):
    nw = len(flight)

    def body(*refs):
        src, got = refs[:nw], refs[nw:2 * nw]
        send, recv = refs[2 * nw:3 * nw], refs[3 * nw:4 * nw]
        x, y, c = _mesh_pos()
        for i in range(nw):
            for f, chip in enumerate(_other_chips(x, y)):
                cp = _scatter_copy(src[i], got[i], send[i], recv[i], f, chip, c)
                cp.wait_send()
                cp.wait_recv()

    sums, lands = [fl[2] for fl in flight], [fl[3] for fl in flight]
    out = pl.pallas_call(
        body, name="scatter_wait", in_specs=[HBM] * (2 * nw) + [SEM] * (2 * nw) + [ANY], out_specs=[HBM] * (2 * nw),
        out_shape=[pltpu.HBM(s.shape, BF16) for s in sums] + [pltpu.HBM(l.shape, BF16) for l in lands],
        input_output_aliases={i: i for i in range(2 * nw)},
        compiler_params=pltpu.CompilerParams(has_side_effects=IN_FLIGHT))(
            *sums, *lands, *[fl[0] for fl in flight], *[fl[1] for fl in flight], after)
    return out[:nw], out[nw:]


def _join_halves(ws, shards):
    nw = len(ws)

    def body(*refs):
        buf = refs[nw:2 * nw]
        send_sems, recv_sems = refs[2 * nw:]
        x, y, c = _mesh_pos()
        sibling = (x, y, 1 - c)

        def copy(i, w, half):
            region = w.half_of(buf[i], half)
            return pltpu.make_async_remote_copy(src_ref=region, dst_ref=region, send_sem=send_sems.at[i],
                                                recv_sem=recv_sems.at[i], device_id=sibling, device_id_type=MESH)

        sends = [copy(i, w, c) for i, w in enumerate(ws)]
        for cp in sends:
            cp.start()
        for i, w in enumerate(ws):
            copy(i, w, 1 - c).wait_recv()
        for cp in sends:
            cp.wait_send()

    return pl.pallas_call(
        body, name="join_halves", in_specs=[ANY] * nw, out_specs=[ANY] * nw,
        out_shape=[jax.ShapeDtypeStruct((w.L, w.ks, w.ns), F32) for w in ws],
        input_output_aliases={i: i for i in range(nw)},
        scratch_shapes=[pltpu.SemaphoreType.DMA((nw,)), pltpu.SemaphoreType.DMA((nw,))],
        compiler_params=_params(has_side_effects=True))(*shards)


def _allreduce_small(vec):
    R = vec.shape[0]

    def body(x_ref, o_ref, buf, send_sems, recv_sems):
        x, y, c = _mesh_pos()
        me, sibling = (x, y, c), (x, y, 1 - c)
        chips = _other_chips(x, y)

        def slot(px, py, pc):
            return buf.at[4 * px + 2 * py + pc]

        def copy(k, block, to, src=None):
            return pltpu.make_async_remote_copy(src_ref=slot(*block) if src is None else src, dst_ref=slot(*block),
                                                send_sem=send_sems.at[k], recv_sem=recv_sems.at[k], device_id=to,
                                                device_id_type=MESH)

        first = [copy(0, me, sibling, src=x_ref)] + [copy(1 + f, me, (*chip, c), src=x_ref)
                                                     for f, chip in enumerate(chips)]
        for cp in first:
            cp.start()
        passed = [copy(4 + f, (*chip, c), sibling) for f, chip in enumerate(chips)]
        for f, chip in enumerate(chips):
            copy(1 + f, (*chip, c), me).wait_recv()
            passed[f].start()
        copy(0, sibling, me).wait_recv()
        for f, chip in enumerate(chips):
            copy(4 + f, (*chip, 1 - c), me).wait_recv()
        for cp in first + passed:
            cp.wait_send()
        slot(*me)[...] = x_ref[...]
        acc = buf[0]
        for d in range(1, 8):
            acc = acc + buf[d]
        o_ref[...] = acc

    return pl.pallas_call(
        body, name="allreduce_small", in_specs=[pl.BlockSpec(memory_space=pltpu.VMEM)],
        out_specs=pl.BlockSpec(memory_space=pltpu.VMEM), out_shape=jax.ShapeDtypeStruct((R, 128), F32),
        scratch_shapes=[pltpu.VMEM((8, R, 128), F32), pltpu.SemaphoreType.DMA((7,)), pltpu.SemaphoreType.DMA((7,))],
        compiler_params=_params())(vec)


def _pack(parts):
    flat = jnp.concatenate([p.reshape(-1).astype(F32) for p in parts])
    n = flat.shape[0]
    pad = (-n) % (64 * 128)
    return jnp.pad(flat, (0, pad)).reshape(-1, 128)


def _unpack(vec, shapes):
    flat = vec.reshape(-1)
    out, off = [], 0
    for s in shapes:
        n = int(np.prod(s))
        out.append(flat[off:off + n].reshape(s))
        off += n
    return out


def kernel(x, a_norm_g, a_w_in, a_v_norm_g, a_w_s, a_b_s, a_w_out, kv_norm_g, w_kv, b_norm_g, b_w_q, b_rel_bias, b_w_o, f_norm_g, f_w_in, f_conv_w, f_conv_b, f_w_down, final_norm_g, loss_target, m_a_norm_g, m_a_w_in, m_a_v_norm_g, m_a_w_s, m_a_b_s, m_a_w_out, m_kv_norm_g, m_w_kv, m_b_norm_g, m_b_w_q, m_b_rel_bias, m_b_w_o, m_f_norm_g, m_f_w_in, m_f_conv_w, m_f_conv_b, m_f_w_down, m_final_norm_g, v_a_norm_g, v_a_w_in, v_a_v_norm_g, v_a_w_s, v_a_b_s, v_a_w_out, v_kv_norm_g, v_w_kv, v_b_norm_g, v_b_w_q, v_b_rel_bias, v_b_w_o, v_f_norm_g, v_f_w_in, v_f_conv_w, v_f_conv_b, v_f_w_down, v_final_norm_g):
    B, S, D = x.shape
    T = B * S
    xi, yi, ci = lax.axis_index("x"), lax.axis_index("y"), lax.axis_index("c")
    j_me = (2 * xi + yi).astype(jnp.int32)
    core = ci.astype(jnp.int32)
    pos = jnp.stack([j_me, core])

    w_shards = {"a_w_in": (a_w_in, False), "a_w_out": (a_w_out, True), "w_kv": (w_kv[None], False),
                "b_w_q": (b_w_q, True), "b_w_o": (b_w_o, True), "f_w_in": (f_w_in, False), "f_w_down": (f_w_down, True)}
    names = list(w_shards)
    ws = [_W(n, w_shards[n][0], w_shards[n][1]) for n in names]
    g_shards = {"a_w_in": (a_w_in, False), "a_w_out": (a_w_out, True),
                "f_w_in0": (f_w_in[0:1], False), "f_w_down0": (f_w_down[0:1], True),
                "w_kv": (w_kv[None], False), "b_w_q": (b_w_q, True), "b_w_o": (b_w_o, True),
                "f_w_in1": (f_w_in[1:2], False), "f_w_down1": (f_w_down[1:2], True)}
    g_names = list(g_shards)
    g_ws = {n: _W(n, *g_shards[n]) for n in g_names}

    Wd = a_w_in.shape[1]
    GW = a_v_norm_g.shape[1] * N_CHIPS
    F2 = f_conv_w.shape[2] * N_CHIPS
    Fh = F2 // 2
    nsd, nsg, nsf = a_norm_g.shape[1], a_v_norm_g.shape[1], f_conv_w.shape[2]
    own = (ci == 0).astype(F32)
    place = lambda sh, width, n: lax.dynamic_update_slice_in_dim(
        jnp.zeros(sh.shape[:-1] + (width,), F32), sh * own, j_me * n, axis=sh.ndim - 1)
    gathered = _allreduce_small(_pack([place(a_norm_g, Wd, nsd), place(a_v_norm_g, GW, nsg),
                                       place(f_conv_w, F2, nsf)]))
    a_g, a_vg, conv_w = _unpack(gathered, [(1, Wd), (1, GW), (2, 3, F2)])

    flight = dict(zip(g_names, _gather_start([g_ws[n] for n in g_names],
                                             [g_shards[n][0].astype(BF16) for n in g_names], gathered)))
    full = {}

    def tied(x, flight):
        x, thru = lax.optimization_barrier((x, flight[0][2]))
        return x, [flight[0][:2] + (thru,) + flight[0][3:]] + flight[1:]

    def arrive(group, after, tag):
        gw = [g_ws[n] for n in group]
        sh, fu = _gather_wait(gw, [flight[n] for n in group], after, name=f"gather_wait_{tag}")
        full.update(zip(group, _gather_finish(gw, sh, fu, name=f"gather_finish_{tag}")))
    conv_w2 = conv_w.reshape(2, 3, 2, Fh).transpose(0, 2, 1, 3)
    conv_b2 = f_conv_b.reshape(2, 2, Fh)

    h0 = x.reshape(T, D)
    target = loss_target.reshape(T, D)
    bs_tile = jnp.repeat(a_b_s[0].T, GROUP_DIM, axis=1)
    ws_a = a_w_s[0]
    scale = HEAD_DIM ** -0.5
    HD = b_w_q.shape[2]
    H = HD // HEAD_DIM
    n_rel = b_rel_bias.shape[-1]
    frow, (flight["a_w_in"],) = tied(b_rel_bias[0][:, _bias_index()].reshape(H, 1, F_LEN), [flight["a_w_in"]])
    bias = _bias_expand(frow)

    def ffn_fwd(h, l, loss=None):
        yff, a, c, n = _ffn_in_conv(h, full[f"f_w_in{l}"], f_norm_g[l], conv_w2[l], conv_b2[l], S, name=f"ffn{l}_in")
        return _mm(yff, full[f"f_w_down{l}"], layer=0, res=h, loss=loss, name=f"ffn{l}_down"), (a, c, n, yff)

    arrive(["a_w_in", "a_w_out"], bias, "a")
    zp, n_a = _mm(h0, full["a_w_in"], layer=0, norm_g=a_g[0], out_dtype=BF16, emit_norm=True, name="a_in")
    out_a = _gate_fwd(zp, a_vg, ws_a, bs_tile)
    h1 = _mm(out_a, full["a_w_out"], layer=0, res=h0, name="a_out")
    arrive(["f_w_in0", "f_w_down0"], h1, "f0")
    h2, saved0 = ffn_fwd(h1, 0)
    arrive(["w_kv", "b_w_q", "b_w_o"], h2, "b")
    arrive(["f_w_in1", "f_w_down1"], h2, "f1")
    kv, n_kv = _mm(h2, full["w_kv"], layer=0, norm_g=kv_norm_g, out_dtype=BF16, split_out=True, emit_norm=True,
                   name="kv")
    q, n_q = _mm(h2, full["b_w_q"], layer=0, norm_g=b_norm_g[0], scale=scale, out_dtype=BF16, emit_norm=True,
                 name="q")
    kv4, q3 = kv.reshape(2, B, S, HD), q.reshape(B, S, HD)
    o = _attn_fwd(q3, kv4, bias, B, S).reshape(T, HD)
    h3 = _mm(o, full["b_w_o"], layer=0, res=h2, name="attn_out")
    (dh, loss8, dg_final), saved1 = ffn_fwd(h3, 1, loss=(final_norm_g, target))

    units = {}

    in_flight = {}

    def swap_start(group, tag, carry):
        us = [units[n] for n in group]
        lands = [lax.empty(u.shape[1:], BF16) for u in us]
        carry, flight = tied(carry, _split_copies(f"swap_start_{tag}", us, lands, 1, _swap_copies, after=carry))
        return (group, tag, flight), carry

    def reduce_start(swap, after):
        group, tag, flight = swap
        us, got = _split_copies(f"swap_wait_{tag}", [fl[2] for fl in flight], [fl[3] for fl in flight], 1,
                                _swap_copies, flight=flight, after=after)
        sums = [_add_pair(u, g_, core, name=f"pair_{n}") for n, u, g_ in zip(group, us, got)]
        after, flight = tied(after, _scatter_start(sums, name=f"scatter_start_{tag}"))
        in_flight.update(zip(group, flight))
        return after

    def ffn_bwd(dh, h, saved, l, early):
        a, c, n, yff = saved
        units[f"f_w_down{l}"] = _mm_tn(yff, dh, rows_are_shards=True, name=f"ffn{l}_down_dw")
        dh_in = dh
        if early:
            sw, dh_in = swap_start([f"f_w_down{l}"], f"fd{l}", dh)
        dyff = _mm(dh_in, full[f"f_w_down{l}"], layer=0, trans_w=True, out_dtype=BF16, name=f"ffn{l}_down_dx")
        if early:
            dyff = reduce_start(sw, dyff)
        da, dcw, dcb = _conv_bwd(a, c, dyff, conv_w2[l], S)
        units[f"f_w_in{l}"] = _mm_tn(n, da, split_y=True, name=f"ffn{l}_in_dw")
        sw, da = swap_start([f"f_w_in{l}"] if early else [f"f_w_down{l}", f"f_w_in{l}"], f"f{l}", da)
        dh, dg = _mm(da, full[f"f_w_in{l}"], layer=0, trans_w=True, split_x=True, bwd=(h, f_norm_g[l], dh),
                     name=f"ffn{l}_in_dx")
        return reduce_start(sw, dh), dg, dcw, dcb

    dh, dg_f1, dcw1, dcb1 = ffn_bwd(dh, h3, saved1, 1, False)
    do = _mm(dh, full["b_w_o"], layer=0, trans_w=True, out_dtype=BF16, name="attn_out_dx")
    units["b_w_o"] = _mm_tn(o, dh, rows_are_shards=True, name="b_w_o_dw")
    dq, dkv, dbias = _attn_bwd(q3, kv4, bias, do.reshape(B, S, HD), B, S)
    dq, d_rel = lax.optimization_barrier((dq, _bias_reduce(dbias, n_rel)))
    d_rel = d_rel.reshape(1, H, n_rel)
    dq, dkv = dq.reshape(T, HD), dkv.reshape(2, T, HD)
    units["b_w_q"] = _mm_tn(n_q, dq, rows_are_shards=True, name="b_w_q_dw")
    dh, dg_b = _mm(dq, full["b_w_q"], layer=0, trans_w=True, bwd=(h2, b_norm_g[0], dh), name="q_dx")
    units["w_kv"] = _mm_tn(n_kv, dkv, split_y=True, name="w_kv_dw")
    sw, dkv = swap_start(["b_w_o", "b_w_q", "w_kv"], "b", dkv)
    dh, dg_kv = _mm(dkv, full["w_kv"], layer=0, trans_w=True, split_x=True, bwd=(h2, kv_norm_g, dh), name="kv_dx")
    dh = reduce_start(sw, dh)
    dh, dg_f0, dcw0, dcb0 = ffn_bwd(dh, h1, saved0, 0, True)
    units["a_w_out"] = _mm_tn(out_a, dh, rows_are_shards=True, name="a_w_out_dw")
    sw, dh_in = swap_start(["a_w_out"], "ao", dh)
    d_out = _mm(dh_in, full["a_w_out"], layer=0, trans_w=True, out_dtype=BF16, name="a_out_dx")
    d_out = reduce_start(sw, d_out)
    dzp, dws, dbs, dgv = _gate_bwd(zp, d_out, a_vg, ws_a, bs_tile)
    units["a_w_in"] = _mm_tn(n_a, dzp, name="a_w_in_dw")
    sw, dzp_in = swap_start(["a_w_in"], "ai", dzp)
    grad_x, dg_a = _mm(dzp_in, full["a_w_in"], layer=0, trans_w=True, bwd=(h0, a_g[0], dh), name="a_in_dx")
    grad_x = reduce_start(sw, grad_x)

    to_flat = lambda d: d.transpose(1, 0, 2).reshape(3, F2)
    small = {"a_norm_g": dg_a, "a_v_norm_g": dgv, "a_w_s": dws[None], "a_b_s": dbs[None], "kv_norm_g": dg_kv[0],
             "b_norm_g": dg_b, "b_rel_bias": d_rel, "f_norm_g": jnp.concatenate([dg_f0, dg_f1], axis=0),
             "f_conv_w": jnp.stack([to_flat(dcw0), to_flat(dcw1)]),
             "f_conv_b": jnp.stack([dcb0.reshape(F2), dcb1.reshape(F2)]), "final_norm_g": dg_final[0]}
    snames = list(small)
    small_vec = _pack([small[n] for n in snames] + [loss8[0:1, 0:1]])
    grad_x, small_flight = tied(grad_x, _split_copies("small_start", [small_vec],
                                                      [lax.empty((8,) + small_vec.shape, F32)], 7, _gather8_copies,
                                                      after=grad_x))

    sums, recv = _scatter_wait([in_flight[n] for n in g_names], grad_x)
    sums, recv = dict(zip(g_names, sums)), dict(zip(g_names, recv))
    halves = []
    for n, w in zip(names, ws):
        if w.L == 1:
            halves.append(_sum_chips(w, sums[n], recv[n], pos, name=f"chips_{n}"))
        else:
            first = _sum_chips(w, sums[n + "0"], recv[n + "0"], pos, name=f"chips_{n}0")
            halves.append(_sum_chips(w, sums[n + "1"], recv[n + "1"], pos, layer=1, into=first, name=f"chips_{n}1"))
    g_big = dict(zip(names, _join_halves(ws, halves)))
    g_big["w_kv"] = g_big["w_kv"][0]

    given = dict(a_norm_g=(a_norm_g, m_a_norm_g, v_a_norm_g), a_w_in=(a_w_in, m_a_w_in, v_a_w_in),
                 a_v_norm_g=(a_v_norm_g, m_a_v_norm_g, v_a_v_norm_g), a_w_s=(a_w_s, m_a_w_s, v_a_w_s),
                 a_b_s=(a_b_s, m_a_b_s, v_a_b_s), a_w_out=(a_w_out, m_a_w_out, v_a_w_out),
                 kv_norm_g=(kv_norm_g, m_kv_norm_g, v_kv_norm_g), w_kv=(w_kv, m_w_kv, v_w_kv),
                 b_norm_g=(b_norm_g, m_b_norm_g, v_b_norm_g), b_w_q=(b_w_q, m_b_w_q, v_b_w_q),
                 b_rel_bias=(b_rel_bias, m_b_rel_bias, v_b_rel_bias), b_w_o=(b_w_o, m_b_w_o, v_b_w_o),
                 f_norm_g=(f_norm_g, m_f_norm_g, v_f_norm_g), f_w_in=(f_w_in, m_f_w_in, v_f_w_in),
                 f_conv_w=(f_conv_w, m_f_conv_w, v_f_conv_w), f_conv_b=(f_conv_b, m_f_conv_b, v_f_conv_b),
                 f_w_down=(f_w_down, m_f_w_down, v_f_w_down), final_norm_g=(final_norm_g, m_final_norm_g, v_final_norm_g))
    order = list(given)
    grads, deltas, new_m, new_v = {}, {}, {}, {}
    for n in names:
        w_, m_, v_ = given[n]
        g_ = g_big[n]
        C = w_.shape[-1]
        d2, m2, v2 = _adamw(w_.reshape(-1, C), g_.reshape(-1, C), m_.reshape(-1, C), v_.reshape(-1, C),
                            name=f"adamw_{n}")
        grads[n], deltas[n], new_m[n], new_v[n] = g_.reshape(w_.shape), d2.reshape(w_.shape), m2.reshape(w_.shape), \
            v2.reshape(w_.shape)
    vecs, lands = _split_copies("small_wait", [small_flight[0][2]], [small_flight[0][3]], 7, _gather8_copies,
                                flight=small_flight, after=deltas[names[-1]])
    red = _sum8(lands[0], vecs[0], (4 * xi + 2 * yi + ci).astype(jnp.int32))
    parts = _unpack(red, [small[n].shape for n in snames] + [(1,)])
    g_small = dict(zip(snames, parts[:-1]))
    loss = parts[-1][0]
    g_small["a_norm_g"] = lax.dynamic_slice_in_dim(g_small["a_norm_g"], j_me * nsd, nsd, axis=1)
    g_small["a_v_norm_g"] = lax.dynamic_slice_in_dim(g_small["a_v_norm_g"], j_me * nsg, nsg, axis=1)
    g_small["f_conv_w"] = lax.dynamic_slice_in_dim(g_small["f_conv_w"], j_me * nsf, nsf, axis=2)

    sm = [n for n in order if n not in names]
    d2, m2, v2 = _adamw(_pack([given[n][0] for n in sm]), _pack([g_small[n].reshape(given[n][0].shape) for n in sm]),
                        _pack([given[n][1] for n in sm]), _pack([given[n][2] for n in sm]), name="adamw_small")
    shapes = [given[n][0].shape for n in sm]
    for n, d_, m_, v_ in zip(sm, _unpack(d2, shapes), _unpack(m2, shapes), _unpack(v2, shapes)):
        grads[n], deltas[n], new_m[n], new_v[n] = g_small[n].reshape(given[n][0].shape), d_, m_, v_

    return (loss, grad_x.reshape(B, S, D), *[grads[n] for n in order], *[deltas[n] for n in order],
            *[new_m[n] for n in order], *[new_v[n] for n in order])
```

```python
import functools
import math

import numpy as np
import jax
import jax.numpy as jnp
from jax import lax
from jax.experimental import pallas as pl
from jax.experimental.pallas import tpu as pltpu

F32 = jnp.float32
BF16 = jnp.bfloat16
MESH = pl.DeviceIdType.MESH

EPS = 1e-6
NEG_INF = -1e30
CHUNK = 64
GMLP_BLOCK = 128
GROUP_DIM = 128
HEAD_DIM = 64
LEFT_CHUNKS = 8
PAD = LEFT_CHUNKS * CHUNK
REL_CLIP = 128
Q_BLOCK = 256
K_SPAN = PAD + Q_BLOCK
F_LEN = K_SPAN + Q_BLOCK
HEADS_PER_STEP = 4
N_CHIPS = 4

ADAM_LR = 0.001
ADAM_B1 = 0.9
ADAM_B2 = 0.999
ADAM_EPS = 1e-08
ADAM_WD = 0.01
ADAM_STEP = 10

VMEM_LIMIT = 56 * 1024 * 1024


def _params(sem=None, **kw):
    if sem is not None:
        kw["dimension_semantics"] = sem
    return pltpu.CompilerParams(vmem_limit_bytes=VMEM_LIMIT, **kw)


def _rms(xf):
    r = lax.rsqrt(jnp.mean(xf * xf, axis=-1, keepdims=True) + EPS)
    return xf * r, r


def _gelu(x, with_grad=False):
    c = math.sqrt(2.0 / math.pi)
    x2 = x * x
    t = jnp.tanh(c * x * (1.0 + 0.044715 * x2))
    half = 0.5 * (1.0 + t)
    if not with_grad:
        return x * half
    return x * half, half + 0.5 * x * (1.0 - t * t) * c * (1.0 + 3.0 * 0.044715 * x2)


def _col_tile(n):
    if n <= 1024:
        return n
    for t in (1408, 1024, 512):
        if n % t == 0:
            return t
    raise ValueError(n)


def _row_tile(t, want):
    while t % want:
        want //= 2
    return want


def _mm(x, w, *, name, layer=None, trans_w=False, norm_g=None, res=None, scale=None, out_dtype=F32, bwd=None,
        split_out=False, split_x=False, emit_norm=False, loss=None, tm=512):
    T = x.shape[-2]
    K = 2 * x.shape[-1] if split_x else x.shape[-1]
    N = w.shape[-2] if trans_w else w.shape[-1]
    tn = N
    tm = _row_tile(T, 2 * tm if max(K, N) <= 2048 else tm)
    nn, nm = N // tn, T // tm
    has_norm, has_res, has_bwd, has_loss = norm_g is not None, res is not None, bwd is not None, loss is not None
    dims = (((1,), (1,)), ((), ())) if trans_w else (((1,), (0,)), ((), ()))

    def body(*refs):
        it = iter(refs)
        x_ref, w_ref = next(it), next(it)
        g_ref = next(it) if has_norm else None
        res_ref = next(it) if has_res else None
        if has_bwd:
            h_ref, bg_ref, dh_ref = next(it), next(it), next(it)
        if has_loss:
            lg_ref, t_ref = next(it), next(it)
        o_ref = next(it)
        if split_x:
            kh = K // 2
            acc = lax.dot_general(x_ref[0].astype(BF16), w_ref[:, :kh] if trans_w else w_ref[:kh, :], dims,
                                  preferred_element_type=F32)
            acc = acc + lax.dot_general(x_ref[1].astype(BF16), w_ref[:, kh:] if trans_w else w_ref[kh:, :], dims,
                                        preferred_element_type=F32)
        else:
            xv = x_ref[...]
            if has_norm:
                xv = _rms(xv.astype(F32))[0] * g_ref[...]
            xb = xv.astype(BF16)
            if emit_norm:
                refs[-1][...] = xb
            acc = lax.dot_general(xb, w_ref[...], dims, preferred_element_type=F32)
        if scale is not None:
            acc = acc * scale
        if has_res:
            acc = acc + res_ref[...]
        if has_bwd:
            dg_ref = next(it)
            n, r = _rms(h_ref[...])

            @pl.when(pl.program_id(1) == 0)
            def _():
                dg_ref[...] = jnp.zeros_like(dg_ref)

            dg_ref[...] += jnp.sum(acc * n, axis=0, keepdims=True)
            t = acc * bg_ref[...]
            o_ref[...] = dh_ref[...] + r * (t - n * jnp.mean(t * n, axis=-1, keepdims=True))
        elif has_loss:
            loss_ref, dg_ref = refs[-2], refs[-1]

            @pl.when(pl.program_id(1) == 0)
            def _():
                loss_ref[...] = jnp.zeros_like(loss_ref)
                dg_ref[...] = jnp.zeros_like(dg_ref)

            n, r = _rms(acc)
            g = lg_ref[...]
            e = n * g - t_ref[...]
            loss_ref[...] += 0.5 * jnp.sum(jnp.mean(e * e, axis=-1, keepdims=True), axis=0, keepdims=True)
            dy = e * (1.0 / N)
            dg_ref[...] += jnp.sum(dy * n, axis=0, keepdims=True)
            t = dy * g
            o_ref[...] = r * (t - n * jnp.mean(t * n, axis=-1, keepdims=True))
        elif split_out:
            o_ref[0] = acc[:, :N // 2].astype(out_dtype)
            o_ref[1] = acc[:, N // 2:].astype(out_dtype)
        else:
            o_ref[...] = acc.astype(out_dtype)

    lead = () if layer is None else (None,)
    lidx = () if layer is None else (layer,)
    ins = [x, w]
    xspec = (pl.BlockSpec((2, tm, K // 2), lambda n, m: (0, m, 0)) if split_x
             else pl.BlockSpec((tm, K), lambda n, m: (m, 0)))
    once = pl.Buffered(1)
    wspec = (pl.BlockSpec(lead + (tn, K), lambda n, m: lidx + (n, 0), pipeline_mode=once) if trans_w
             else pl.BlockSpec(lead + (K, tn), lambda n, m: lidx + (0, n), pipeline_mode=once))
    in_specs = [xspec, wspec]
    if has_norm:
        ins.append(norm_g.reshape(1, K))
        in_specs.append(pl.BlockSpec((1, K), lambda n, m: (0, 0)))
    if has_res:
        ins.append(res)
        in_specs.append(pl.BlockSpec((tm, tn), lambda n, m: (m, n)))
    if split_out:
        out_shape = [jax.ShapeDtypeStruct((2, T, N // 2), out_dtype)]
        out_specs = [pl.BlockSpec((2, tm, N // 2), lambda n, m: (0, m, 0))]
    else:
        out_shape = [jax.ShapeDtypeStruct((T, N), F32 if has_bwd else out_dtype)]
        out_specs = [pl.BlockSpec((tm, tn), lambda n, m: (m, n))]
    if has_bwd:
        h, g, dh = bwd
        ins += [h, g.reshape(1, N), dh]
        in_specs += [pl.BlockSpec((tm, N), lambda n, m: (m, 0)), pl.BlockSpec((1, N), lambda n, m: (0, 0)),
                     pl.BlockSpec((tm, N), lambda n, m: (m, 0))]
        out_shape.append(jax.ShapeDtypeStruct((1, N), F32))
        out_specs.append(pl.BlockSpec((1, N), lambda n, m: (0, 0)))
    if emit_norm:
        out_shape.append(jax.ShapeDtypeStruct((T, K), BF16))
        out_specs.append(pl.BlockSpec((tm, K), lambda n, m: (m, 0)))
    if has_loss:
        ins += [loss[0].reshape(1, N), loss[1]]
        in_specs += [pl.BlockSpec((1, N), lambda n, m: (0, 0)), pl.BlockSpec((tm, N), lambda n, m: (m, 0))]
        out_shape += [jax.ShapeDtypeStruct((8, 128), F32), jax.ShapeDtypeStruct((1, N), F32)]
        out_specs += [pl.BlockSpec((8, 128), lambda n, m: (0, 0)), pl.BlockSpec((1, N), lambda n, m: (0, 0))]
    out = pl.pallas_call(body, name=name, grid=(nn, nm), in_specs=in_specs, out_specs=out_specs, out_shape=out_shape,
                         compiler_params=_params(("arbitrary", "arbitrary")))(*ins)
    return out if has_bwd or emit_norm or has_loss else out[0]


def _mm_tn(x, dy, *, name, rows_are_shards=False, split_y=False, tt=1024):
    T, K = x.shape
    N = 2 * dy.shape[-1] if split_y else dy.shape[-1]
    R, C = (K // N_CHIPS, N // 2) if rows_are_shards else (K // 2, N // N_CHIPS)
    nn = 2 if split_y else 1
    tn = N // nn
    per = N_CHIPS // nn
    assert not (rows_are_shards and split_y)
    tt = _row_tile(T, tt)
    nt = T // tt

    def body(x_ref, y_ref, o_ref, acc_ref):
        t = pl.program_id(1)

        @pl.when(t == 0)
        def _():
            acc_ref[...] = jnp.zeros_like(acc_ref)

        acc_ref[...] += lax.dot_general(x_ref[...], y_ref[...].astype(BF16), (((0,), (0,)), ((), ())),
                                        preferred_element_type=F32)

        @pl.when(t == nt - 1)
        def _():
            if rows_are_shards:
                for h in range(2):
                    o_ref[h] = acc_ref[:, h * C:(h + 1) * C].astype(BF16).reshape(N_CHIPS, R, C)
            else:
                for j in range(per):
                    o_ref[:, j] = acc_ref[:, j * C:(j + 1) * C].astype(BF16).reshape(2, R, C)

    if split_y:
        yspec = pl.BlockSpec((None, tt, tn), lambda n, t: (n, t, 0))
    else:
        yspec = pl.BlockSpec((tt, tn), lambda n, t: (t, 0))
    if rows_are_shards:
        out_spec = pl.BlockSpec((2, N_CHIPS, R, C), lambda n, t: (0, 0, 0, 0))
    else:
        out_spec = pl.BlockSpec((2, per, R, C), lambda n, t: (0, n, 0, 0))
    return pl.pallas_call(body, name=name, grid=(nn, nt),
                          in_specs=[pl.BlockSpec((tt, K), lambda n, t: (t, 0)), yspec], out_specs=out_spec,
                          out_shape=jax.ShapeDtypeStruct((2, N_CHIPS, R, C), BF16),
                          scratch_shapes=[pltpu.VMEM((K, tn), F32)],
                          compiler_params=_params(("arbitrary", "arbitrary")))(x, dy)


def _chunk_mask():
    i = lax.broadcasted_iota(jnp.int32, (GMLP_BLOCK, GMLP_BLOCK), 0) // CHUNK
    j = lax.broadcasted_iota(jnp.int32, (GMLP_BLOCK, GMLP_BLOCK), 1) // CHUNK
    return i >= j


def _gate_fwd(zp, gv, ws, bs_tile, *, tm=256):
    T, W2 = zp.shape
    W = W2 // 2
    G = W // GROUP_DIM
    tm = _row_tile(T, tm)

    def body(zp_ref, gv_ref, ws_ref, bs_ref, o_ref):
        z = _gelu(zp_ref[...].astype(F32))
        u, v = z[:, :W], z[:, W:]
        vn = _rms(v)[0] * gv_ref[...]
        mask = _chunk_mask()
        for g in range(G):
            cs = slice(g * GROUP_DIM, (g + 1) * GROUP_DIM)
            wg = jnp.where(mask, ws_ref[g], 0.0).astype(BF16)
            for b in range(tm // GMLP_BLOCK):
                rs = slice(b * GMLP_BLOCK, (b + 1) * GMLP_BLOCK)
                s = jnp.dot(wg, vn[rs, cs].astype(BF16), preferred_element_type=F32) + bs_ref[:, cs]
                o_ref[rs, cs] = (u[rs, cs] * s).astype(BF16)

    return pl.pallas_call(
        body, name="gate_fwd", grid=(T // tm,),
        in_specs=[pl.BlockSpec((tm, W2), lambda i: (i, 0)), pl.BlockSpec((1, W), lambda i: (0, 0)),
                  pl.BlockSpec((G, GMLP_BLOCK, GMLP_BLOCK), lambda i: (0, 0, 0)),
                  pl.BlockSpec((GMLP_BLOCK, W), lambda i: (0, 0))],
        out_specs=pl.BlockSpec((tm, W), lambda i: (i, 0)), out_shape=jax.ShapeDtypeStruct((T, W), BF16),
        compiler_params=_params(("arbitrary",)))(zp, gv, ws, bs_tile)


def _gate_bwd(zp, d_out, gv, ws, bs_tile, *, tm=256):
    T, W2 = zp.shape
    W = W2 // 2
    G = W // GROUP_DIM
    tm = _row_tile(T, tm)
    nm = T // tm

    def body(zp_ref, do_ref, gv_ref, ws_ref, bs_ref, dzp_ref, dws_ref, dbs_ref, dgv_ref, du_scr, dvn_scr, dsum_scr):
        i = pl.program_id(0)

        @pl.when(i == 0)
        def _():
            dws_ref[...] = jnp.zeros_like(dws_ref)
            dgv_ref[...] = jnp.zeros_like(dgv_ref)
            dsum_scr[...] = jnp.zeros_like(dsum_scr)

        zp = zp_ref[...].astype(F32)
        z, dz = _gelu(zp, with_grad=True)
        u, v = z[:, :W], z[:, W:]
        n, r = _rms(v)
        gv = gv_ref[...]
        vn = n * gv
        d_out = do_ref[...].astype(F32)
        mask = _chunk_mask()
        for g in range(G):
            cs = slice(g * GROUP_DIM, (g + 1) * GROUP_DIM)
            wg = jnp.where(mask, ws_ref[g], 0.0).astype(BF16)
            dw = jnp.zeros((GMLP_BLOCK, GMLP_BLOCK), F32)
            for b in range(tm // GMLP_BLOCK):
                rs = slice(b * GMLP_BLOCK, (b + 1) * GMLP_BLOCK)
                vb = vn[rs, cs].astype(BF16)
                s = jnp.dot(wg, vb, preferred_element_type=F32) + bs_ref[:, cs]
                du_scr[rs, cs] = d_out[rs, cs] * s
                ds = d_out[rs, cs] * u[rs, cs]
                dsb = ds.astype(BF16)
                dvn_scr[rs, cs] = lax.dot_general(wg, dsb, (((0,), (0,)), ((), ())), preferred_element_type=F32)
                dw = dw + lax.dot_general(dsb, vb, (((1,), (1,)), ((), ())), preferred_element_type=F32)
                dsum_scr[:, cs] += ds
            dws_ref[g] += jnp.where(mask, dw, 0.0)
        dvn = dvn_scr[...]
        dgv_ref[...] += jnp.sum(dvn * n, axis=0, keepdims=True)
        t = dvn * gv
        dv = r * (t - n * jnp.mean(t * n, axis=-1, keepdims=True))
        dzp_ref[:, :W] = (du_scr[...] * dz[:, :W]).astype(BF16)
        dzp_ref[:, W:] = (dv * dz[:, W:]).astype(BF16)

        @pl.when(i == nm - 1)
        def _():
            sel = (lax.broadcasted_iota(jnp.int32, (G, W), 1) // GROUP_DIM
                   == lax.broadcasted_iota(jnp.int32, (G, W), 0)).astype(F32)
            dbs_ref[...] = lax.dot_general(sel, dsum_scr[...], (((1,), (1,)), ((), ())),
                                           precision=lax.Precision.HIGHEST, preferred_element_type=F32)

    return pl.pallas_call(
        body, name="gate_bwd", grid=(nm,),
        in_specs=[pl.BlockSpec((tm, W2), lambda i: (i, 0)), pl.BlockSpec((tm, W), lambda i: (i, 0)),
                  pl.BlockSpec((1, W), lambda i: (0, 0)),
                  pl.BlockSpec((G, GMLP_BLOCK, GMLP_BLOCK), lambda i: (0, 0, 0)),
                  pl.BlockSpec((GMLP_BLOCK, W), lambda i: (0, 0))],
        out_specs=[pl.BlockSpec((tm, W2), lambda i: (i, 0)),
                   pl.BlockSpec((G, GMLP_BLOCK, GMLP_BLOCK), lambda i: (0, 0, 0)),
                   pl.BlockSpec((G, GMLP_BLOCK), lambda i: (0, 0)), pl.BlockSpec((1, W), lambda i: (0, 0))],
        out_shape=[jax.ShapeDtypeStruct((T, W2), BF16), jax.ShapeDtypeStruct((G, GMLP_BLOCK, GMLP_BLOCK), F32),
                   jax.ShapeDtypeStruct((G, GMLP_BLOCK), F32), jax.ShapeDtypeStruct((1, W), F32)],
        scratch_shapes=[pltpu.VMEM((tm, W), F32), pltpu.VMEM((tm, W), F32), pltpu.VMEM((GMLP_BLOCK, W), F32)],
        compiler_params=_params(("arbitrary",)))(zp, d_out, gv, ws, bs_tile)


LANES = 128
HALO = 16


def _taps(ext, w, b):
    return w[2:3] * ext[HALO:] + w[1:2] * pltpu.roll(ext, 1, 0)[HALO:] + w[0:1] * pltpu.roll(ext, 2, 0)[HALO:] + b


def _ffn_in_conv(h, w, g, cw, cb, S, *, name, tm=256):
    T, D = h.shape
    F = w.shape[-1] // 2
    tc = _col_tile(F)
    tm = _row_tile(S, tm)

    def body(h_ref, w_ref, g_ref, cw_ref, cb_ref, y_ref, a_ref, c_ref, n_ref, tail):
        first = (pl.program_id(0) * tm) % S == 0
        nb = (_rms(h_ref[...])[0] * g_ref[...]).astype(BF16)
        n_ref[...] = nb
        for j in range(F // tc):
            cs = slice(j * tc, (j + 1) * tc)
            conv = []
            for s in range(2):
                acc = jnp.dot(nb, w_ref[:, s * F + j * tc:s * F + (j + 1) * tc], preferred_element_type=F32)
                ab = acc.astype(BF16)
                a_ref[s, :, cs] = ab
                af = ab.astype(F32)
                ext = jnp.concatenate([jnp.where(first, 0.0, tail[s, :, cs]), af], axis=0)
                tail[s, :, cs] = af[tm - HALO:, :]
                cv = _taps(ext, cw_ref[s, :, cs], cb_ref[s:s + 1, cs]).astype(BF16)
                c_ref[s, :, cs] = cv
                conv.append(cv.astype(F32))
            up, gate = conv
            y_ref[:, cs] = (gate * jax.nn.sigmoid(gate) * up).astype(BF16)

    row = lambda width: pl.BlockSpec((tm, width), lambda i: (i, 0))
    wide = pl.BlockSpec((2, tm, F), lambda i: (0, i, 0))
    return pl.pallas_call(
        body, name=name, grid=(T // tm,),
        in_specs=[row(D), pl.BlockSpec((None, D, 2 * F), lambda i: (0, 0, 0), pipeline_mode=pl.Buffered(1)),
                  pl.BlockSpec((1, D), lambda i: (0, 0)),
                  pl.BlockSpec((2, 3, F), lambda i: (0, 0, 0)), pl.BlockSpec((2, F), lambda i: (0, 0))],
        out_specs=[row(F), wide, wide, row(D)],
        out_shape=[jax.ShapeDtypeStruct((T, F), BF16), jax.ShapeDtypeStruct((2, T, F), BF16),
                   jax.ShapeDtypeStruct((2, T, F), BF16), jax.ShapeDtypeStruct((T, D), BF16)],
        scratch_shapes=[pltpu.VMEM((2, HALO, F), F32)],
        compiler_params=_params(("arbitrary",)))(h, w, g.reshape(1, D), cw, cb)


def _conv_bwd(a, c, dy, cw, S, *, tm=256):
    _, T, F = a.shape
    tc = _col_tile(F)
    tm = _row_tile(S, tm)
    nm = T // tm
    hb = tm // HALO
    TE = tm + HALO
    nxt = lambda j, i: jnp.minimum((i + 1) * hb, T // HALO - 1)

    def body(a_ref, c_ref, nc_ref, dy_ref, ndy_ref, w_ref, da_ref, dw_ref, db_ref):
        i = pl.program_id(1)
        last = ((i + 1) * tm) % S == 0
        keep_n = jnp.where(last, 0.0, 1.0)

        @pl.when(i == 0)
        def _():
            dw_ref[...] = jnp.zeros_like(dw_ref)
            db_ref[...] = jnp.zeros_like(db_ref)

        for j in range(tc // LANES):
            cs = slice(j * LANES, (j + 1) * LANES)
            dyf = jnp.concatenate([dy_ref[:, cs].astype(F32), ndy_ref[:, cs].astype(F32) * keep_n], axis=0)
            up = jnp.concatenate([c_ref[0, :, cs].astype(F32), nc_ref[0, :, cs].astype(F32)], axis=0)
            gate = jnp.concatenate([c_ref[1, :, cs].astype(F32), nc_ref[1, :, cs].astype(F32)], axis=0)
            sg = jax.nn.sigmoid(gate)
            for s, d in ((0, dyf * (gate * sg)), (1, dyf * up * (sg * (1.0 + gate * (1.0 - sg))))):
                a = a_ref[s, :, cs].astype(F32)
                w = w_ref[s, :, cs]
                u1, u2 = pltpu.roll(d, TE - 1, 0), pltpu.roll(d, TE - 2, 0)
                db_ref[s:s + 1, cs] += jnp.sum(d[:tm], axis=0, keepdims=True)
                dw_ref[s, 2:3, cs] += jnp.sum(d[:tm] * a, axis=0, keepdims=True)
                dw_ref[s, 1:2, cs] += jnp.sum(u1[:tm] * a, axis=0, keepdims=True)
                dw_ref[s, 0:1, cs] += jnp.sum(u2[:tm] * a, axis=0, keepdims=True)
                da_ref[s, :, cs] = (w[2:3] * d + w[1:2] * u1 + w[0:1] * u2)[:tm].astype(BF16)

    cur = pl.BlockSpec((2, tm, tc), lambda j, i: (0, i, j))
    return pl.pallas_call(
        body, name="conv_bwd", grid=(F // tc, nm),
        in_specs=[cur, cur, pl.BlockSpec((2, HALO, tc), lambda j, i: (0, nxt(j, i), j)),
                  pl.BlockSpec((tm, tc), lambda j, i: (i, j)), pl.BlockSpec((HALO, tc), lambda j, i: (nxt(j, i), j)),
                  pl.BlockSpec((2, 3, tc), lambda j, i: (0, 0, j))],
        out_specs=[cur, pl.BlockSpec((2, 3, tc), lambda j, i: (0, 0, j)), pl.BlockSpec((2, tc), lambda j, i: (0, j))],
        out_shape=[jax.ShapeDtypeStruct((2, T, F), BF16), jax.ShapeDtypeStruct((2, 3, F), F32),
                   jax.ShapeDtypeStruct((2, F), F32)],
        compiler_params=_params(("arbitrary", "arbitrary")))(a, c, c, dy, dy, cw)


def _bias_index():
    idx = np.arange(F_LEN)
    d = np.where(idx < K_SPAN, idx, idx - F_LEN)
    return np.clip(PAD - d, -REL_CLIP, REL_CLIP) + REL_CLIP


ROW_GROUP = 16


def _roll_rows(x, sign, unit, steps):
    rows = lax.broadcasted_iota(jnp.int32, x.shape, 0)
    step = 1
    while step < steps:
        shift = unit * step if sign > 0 else F_LEN - unit * step
        x = jnp.where((rows & step) != 0, pltpu.roll(x, shift, 1), x)
        step *= 2
    return x


def _bias_expand(frow):
    H = frow.shape[0]
    groups = Q_BLOCK // ROW_GROUP

    def body(f_ref, o_ref):
        coarse = _roll_rows(jnp.broadcast_to(f_ref[...], (groups, F_LEN)), 1, ROW_GROUP, groups)
        x = jnp.concatenate([jnp.broadcast_to(coarse[a:a + 1], (ROW_GROUP, F_LEN)) for a in range(groups)], axis=0)
        x = _roll_rows(x, 1, 1, ROW_GROUP)[:, :K_SPAN]
        qc = lax.broadcasted_iota(jnp.int32, (Q_BLOCK, K_SPAN), 0) // CHUNK * CHUNK
        kj = lax.broadcasted_iota(jnp.int32, (Q_BLOCK, K_SPAN), 1)
        o_ref[...] = jnp.where((kj >= qc) & (kj < qc + PAD + CHUNK), x, NEG_INF)

    return pl.pallas_call(
        body, name="bias_expand", grid=(H,),
        in_specs=[pl.BlockSpec((None, 1, F_LEN), lambda h: (h, 0, 0))],
        out_specs=pl.BlockSpec((None, Q_BLOCK, K_SPAN), lambda h: (h, 0, 0)),
        out_shape=jax.ShapeDtypeStruct((H, Q_BLOCK, K_SPAN), F32), compiler_params=_params(("arbitrary",)))(frow)


def _bias_reduce(dbias, n_rel):
    H = dbias.shape[0]
    onehot = jnp.asarray((_bias_index()[:, None] == np.arange(n_rel)[None, :]).astype(np.float32), dtype=BF16)

    def body(d_ref, oh_ref, o_ref):
        x = jnp.concatenate([d_ref[...], jnp.zeros((Q_BLOCK, F_LEN - K_SPAN), F32)], axis=1)
        fine = _roll_rows(x, -1, 1, ROW_GROUP).reshape(Q_BLOCK // ROW_GROUP, ROW_GROUP, F_LEN)
        coarse = _roll_rows(jnp.sum(fine, axis=1), -1, ROW_GROUP, Q_BLOCK // ROW_GROUP)
        row = jnp.broadcast_to(jnp.sum(coarse, axis=0, keepdims=True), (8, F_LEN))
        acc = jnp.zeros((8, n_rel), F32)
        for _ in range(3):
            piece = row.astype(BF16)
            acc = acc + jnp.dot(piece, oh_ref[...], preferred_element_type=F32)
            row = row - piece.astype(F32)
        o_ref[...] = acc[0:1]

    return pl.pallas_call(
        body, name="bias_reduce", grid=(H,),
        in_specs=[pl.BlockSpec((None, Q_BLOCK, K_SPAN), lambda h: (h, 0, 0)),
                  pl.BlockSpec((F_LEN, n_rel), lambda h: (0, 0))],
        out_specs=pl.BlockSpec((None, 1, n_rel), lambda h: (h, 0, 0)),
        out_shape=jax.ShapeDtypeStruct((H, 1, n_rel), F32), compiler_params=_params(("arbitrary",)))(dbias, onehot)


def _attn_specs(S):
    hw = HEADS_PER_STEP * HEAD_DIM
    qspec = pl.BlockSpec((None, Q_BLOCK, hw), lambda g, b, i: (b, i, g))
    kspec = pl.BlockSpec((None, None, S, hw), lambda g, b, i: (0, b, 0, g))
    vspec = pl.BlockSpec((None, None, S, hw), lambda g, b, i: (1, b, 0, g))
    bspec = pl.BlockSpec((HEADS_PER_STEP, Q_BLOCK, K_SPAN), lambda g, b, i: (g, 0, 0))
    return hw, qspec, kspec, vspec, bspec


def _span_cases(i, fn):
    short = PAD // Q_BLOCK
    for j in range(short):
        pl.when(i == j)(functools.partial(fn, PAD - j * Q_BLOCK))
    pl.when(i >= short)(functools.partial(fn, 0))


def _key_start(i, off):
    return 0 if off else pl.multiple_of(i * Q_BLOCK - PAD, Q_BLOCK)


def _attn_exp(q_ref, k_ref, b_ref, h, k0, off):
    hs = slice(h * HEAD_DIM, (h + 1) * HEAD_DIM)
    kh = k_ref[pl.ds(k0, K_SPAN - off), hs]
    s = lax.dot_general(q_ref[:, hs], kh, (((1,), (1,)), ((), ())), preferred_element_type=F32) + b_ref[h, :, off:]
    p = jnp.exp(s - jnp.max(s, axis=-1, keepdims=True))
    return p, 1.0 / jnp.sum(p, axis=-1, keepdims=True), kh


def _attn_fwd(q, kv, bias, B, S):
    HD = q.shape[-1]
    hw, qspec, kspec, vspec, bspec = _attn_specs(S)

    def body(q_ref, k_ref, v_ref, b_ref, o_ref):
        i = pl.program_id(2)

        def block(off):
            k0 = _key_start(i, off)
            outs = []
            for h in range(HEADS_PER_STEP):
                hs = slice(h * HEAD_DIM, (h + 1) * HEAD_DIM)
                p, inv, _ = _attn_exp(q_ref, k_ref, b_ref, h, k0, off)
                outs.append(jnp.dot(p.astype(BF16), v_ref[pl.ds(k0, K_SPAN - off), hs],
                                    preferred_element_type=F32) * inv)
            o_ref[...] = jnp.concatenate(outs, axis=1).astype(BF16)

        _span_cases(i, block)

    return pl.pallas_call(
        body, name="attn_fwd", grid=(HD // hw, B, S // Q_BLOCK), in_specs=[qspec, kspec, vspec, bspec],
        out_specs=qspec, out_shape=jax.ShapeDtypeStruct((B, S, HD), BF16),
        compiler_params=_params(("arbitrary", "arbitrary", "arbitrary")))(q, kv, kv, bias)


def _attn_bwd(q, kv, bias, do, B, S):
    HD = q.shape[-1]
    H = HD // HEAD_DIM
    hw, qspec, kspec, vspec, bspec = _attn_specs(S)
    scale = HEAD_DIM ** -0.5
    nq = S // Q_BLOCK

    def body(q_ref, k_ref, v_ref, b_ref, do_ref, dq_ref, dkv_ref, db_ref, dk_acc, dv_acc):
        b, i = pl.program_id(1), pl.program_id(2)

        @pl.when(i == 0)
        def _():
            dk_acc[...] = jnp.zeros_like(dk_acc)
            dv_acc[...] = jnp.zeros_like(dv_acc)

        @pl.when((i == 0) & (b == 0))
        def _():
            db_ref[...] = jnp.zeros_like(db_ref)

        def block(off):
            k0 = _key_start(i, off)
            keys = pl.ds(k0, K_SPAN - off)
            for h in range(HEADS_PER_STEP):
                hs = slice(h * HEAD_DIM, (h + 1) * HEAD_DIM)
                p, inv, kh = _attn_exp(q_ref, k_ref, b_ref, h, k0, off)
                p = p * inv
                doh = do_ref[:, hs]
                dp = lax.dot_general(doh, v_ref[keys, hs], (((1,), (1,)), ((), ())), preferred_element_type=F32)
                ds = p * (dp - jnp.sum(p * dp, axis=-1, keepdims=True))
                db_ref[h, :, off:] += ds
                dsb = ds.astype(BF16)
                dq_ref[:, hs] = (jnp.dot(dsb, kh, preferred_element_type=F32) * scale).astype(BF16)
                dk_acc[hs, keys] += lax.dot_general(q_ref[:, hs], dsb, (((0,), (0,)), ((), ())),
                                                     preferred_element_type=F32)
                dv_acc[hs, keys] += lax.dot_general(doh, p.astype(BF16), (((0,), (0,)), ((), ())),
                                                     preferred_element_type=F32)

        _span_cases(i, block)

        @pl.when(i == nq - 1)
        def _():
            dkv_ref[0] = dk_acc[...].T.astype(BF16)
            dkv_ref[1] = dv_acc[...].T.astype(BF16)

    return pl.pallas_call(
        body, name="attn_bwd", grid=(HD // hw, B, nq), in_specs=[qspec, kspec, vspec, bspec, qspec],
        out_specs=[qspec, pl.BlockSpec((2, None, S, hw), lambda g, b, i: (0, b, 0, g)), bspec],
        out_shape=[jax.ShapeDtypeStruct((B, S, HD), BF16), jax.ShapeDtypeStruct((2, B, S, HD), BF16),
                   jax.ShapeDtypeStruct((H, Q_BLOCK, K_SPAN), F32)],
        scratch_shapes=[pltpu.VMEM((hw, S), F32), pltpu.VMEM((hw, S), F32)],
        compiler_params=_params(("arbitrary", "arbitrary", "arbitrary")))(q, kv, kv, bias, do)


def _sub_rows(R):
    for cand in (256, 352, 128, 64, 8):
        if R % cand == 0 and R > cand:
            return cand
    return R


def _adamw(w, g, m, v, *, name):
    R, C = w.shape
    tr = _sub_rows(R)

    def body(w_ref, g_ref, m_ref, v_ref, d_ref, nm_ref, nv_ref):
        g = g_ref[...]
        m = ADAM_B1 * m_ref[...] + (1.0 - ADAM_B1) * g
        v = ADAM_B2 * v_ref[...] + (1.0 - ADAM_B2) * (g * g)
        m_hat = m / (1.0 - ADAM_B1 ** ADAM_STEP)
        v_hat = v / (1.0 - ADAM_B2 ** ADAM_STEP)
        d_ref[...] = -ADAM_LR * (m_hat / (jnp.sqrt(v_hat) + ADAM_EPS) + ADAM_WD * w_ref[...])
        nm_ref[...] = m
        nv_ref[...] = v

    spec = pl.BlockSpec((tr, C), lambda i: (i, 0))
    return pl.pallas_call(body, name=name, grid=(R // tr,), in_specs=[spec] * 4, out_specs=[spec] * 3,
                          out_shape=[jax.ShapeDtypeStruct((R, C), F32)] * 3,
                          compiler_params=_params(("arbitrary",)))(w, g, m, v)


def _add_pair(units, got, core, *, name):
    n4, R, C = got.shape
    rows = n4 * R
    tr = 512 if rows % 512 == 0 else R

    def body(c_ref, u_ref, got_ref, o_ref):
        o_ref[...] = (u_ref[...].astype(F32) + got_ref[...].astype(F32)).astype(BF16)

    spec = pl.BlockSpec((tr, C), lambda i, c: (i, 0))
    grid_spec = pltpu.PrefetchScalarGridSpec(
        num_scalar_prefetch=1, grid=(rows // tr,),
        in_specs=[pl.BlockSpec((None, tr, C), lambda i, c: (c[0], i, 0)), spec], out_specs=spec)
    out = pl.pallas_call(body, name=name, grid_spec=grid_spec, out_shape=jax.ShapeDtypeStruct((rows, C), BF16),
                         compiler_params=_params(("arbitrary",)))(core.reshape(1), units.reshape(2, rows, C),
                                                                   got.reshape(rows, C))
    return out.reshape(n4, R, C)


def _sum_chips(w, own, got, pos, *, name, layer=0, into=None):
    _, R, C = own.shape
    tr = _sub_rows(R)
    nr = R // tr

    def body(p_ref, own_ref, got_ref, *rest):
        o_ref = rest[-1]
        o_ref[...] = (own_ref[...].astype(F32) + got_ref[0].astype(F32) + got_ref[1].astype(F32)
                      + got_ref[2].astype(F32))

    if w.row_sharded:
        out_map = lambda i, p: (layer, i, p[1])
    else:
        out_map = lambda i, p: (layer, p[1] * nr + i, 0)
    ins = [pos, own, got]
    in_specs = [pl.BlockSpec((None, tr, C), lambda i, p: (p[0], i, 0)),
                pl.BlockSpec((3, tr, C), lambda i, p: (0, i, 0))]
    alias = {}
    if into is not None:
        ins.append(into)
        in_specs.append(ANY)
        alias = {3: 0}
    grid_spec = pltpu.PrefetchScalarGridSpec(num_scalar_prefetch=1, grid=(nr,), in_specs=in_specs,
                                             out_specs=pl.BlockSpec((None, tr, C), out_map))
    return pl.pallas_call(body, name=name, grid_spec=grid_spec, input_output_aliases=alias,
                          out_shape=jax.ShapeDtypeStruct((w.L, w.ks, w.ns), F32),
                          compiler_params=_params(("arbitrary",)))(*ins)


def _mesh_pos():
    return lax.axis_index("x"), lax.axis_index("y"), lax.axis_index("c")


def _other_chips(x, y):
    return [(1 - x, y), (x, 1 - y), (1 - x, 1 - y)]


ANY = pl.BlockSpec(memory_space=pl.ANY)


class _W:
    def __init__(self, name, shard, row_sharded):
        self.name = name
        self.L, ks, ns = shard.shape
        self.row_sharded = row_sharded
        self.K, self.N = (ks * N_CHIPS, ns) if row_sharded else (ks, ns * N_CHIPS)
        self.ks, self.ns = ks, ns

    def shard_of(self, full, j):
        if self.row_sharded:
            return full.at[:, pl.ds(j * self.ks, self.ks), :]
        return full.at[:, :, pl.ds(j * self.ns, self.ns)]

    def half_of(self, shard, c):
        if self.row_sharded:
            return shard.at[:, :, pl.ds(c * (self.ns // 2), self.ns // 2)]
        return shard.at[:, pl.ds(c * (self.ks // 2), self.ks // 2), :]


HBM = pl.BlockSpec(memory_space=pltpu.HBM)
SEM = pl.BlockSpec(memory_space=pltpu.SEMAPHORE)
IN_FLIGHT = pltpu.SideEffectType.DATAFLOW_SIDE_EFFECTING


def _in_hbm(a):
    return pltpu.with_memory_space_constraint(a, pltpu.HBM)


def _gather_start(ws, shards, after):
    nw = len(ws)

    def body(*refs):
        src, dst = refs[:nw], refs[nw:2 * nw]
        send, recv = refs[2 * nw + 1:3 * nw + 1], refs[3 * nw + 1:4 * nw + 1]
        x, y, c = _mesh_pos()
        me = 2 * x + y
        for i, w in enumerate(ws):
            for f, (px, py) in enumerate(_other_chips(x, y)):
                pltpu.make_async_remote_copy(src_ref=w.half_of(src[i], c), dst_ref=w.half_of(w.shard_of(dst[i], me), c),
                                             send_sem=send[i].at[f], recv_sem=recv[i].at[f], device_id=(px, py, c),
                                             device_id_type=MESH).start()

    fulls = [lax.empty((w.L, w.K, w.N), BF16) for w in ws]
    out = pl.pallas_call(
        body, name="gather_start", in_specs=[HBM] * (2 * nw) + [ANY],
        out_specs=[SEM] * (2 * nw) + [HBM] * (2 * nw),
        out_shape=[pltpu.SemaphoreType.DMA((3,))] * (2 * nw)
        + [pltpu.HBM(s.shape, BF16) for s in shards] + [pltpu.HBM(f.shape, BF16) for f in fulls],
        input_output_aliases={i: 2 * nw + i for i in range(2 * nw)},
        compiler_params=pltpu.CompilerParams(has_side_effects=IN_FLIGHT))(
            *[_in_hbm(s) for s in shards], *[_in_hbm(f) for f in fulls], after)
    return [(out[i], out[nw + i], out[2 * nw + i], out[3 * nw + i]) for i in range(nw)]


def _gather_wait(ws, flight, after, *, name):
    nw = len(ws)

    def body(*refs):
        src, dst = refs[:nw], refs[nw:2 * nw]
        send, recv = refs[2 * nw:3 * nw], refs[3 * nw:4 * nw]
        x, y, c = _mesh_pos()
        for i, w in enumerate(ws):
            for f, (px, py) in enumerate(_other_chips(x, y)):
                landed = w.half_of(w.shard_of(dst[i], 2 * px + py), c)
                cp = pltpu.make_async_remote_copy(src_ref=w.half_of(src[i], c), dst_ref=landed, send_sem=send[i].at[f],
                                                  recv_sem=recv[i].at[f], device_id=(px, py, c), device_id_type=MESH)
                cp.wait_send()
                cp.wait_recv()

    shards, fulls = [fl[2] for fl in flight], [fl[3] for fl in flight]
    out = pl.pallas_call(
        body, name=name, in_specs=[HBM] * (2 * nw) + [SEM] * (2 * nw) + [ANY],
        out_specs=[HBM] * (2 * nw),
        out_shape=[pltpu.HBM(s.shape, BF16) for s in shards] + [pltpu.HBM(f.shape, BF16) for f in fulls],
        input_output_aliases={i: i for i in range(2 * nw)},
        compiler_params=pltpu.CompilerParams(has_side_effects=IN_FLIGHT))(
            *shards, *fulls, *[fl[0] for fl in flight], *[fl[1] for fl in flight], after)
    return out[:nw], out[nw:]


def _gather_finish(ws, shards, fulls, *, name):
    nw = len(ws)

    def body(*refs):
        src, dst, stage = refs[:nw], refs[3 * nw:4 * nw], refs[4 * nw:5 * nw]
        send_sems, recv_sems, load_sems, store_sems = refs[5 * nw:]
        x, y, c = _mesh_pos()
        me = 2 * x + y
        sibling = (x, y, 1 - c)
        chips = _other_chips(x, y)

        def fwd(i, w, f, half):
            px, py = chips[f]
            landed = w.half_of(w.shard_of(dst[i], 2 * px + py), half)
            return pltpu.make_async_remote_copy(src_ref=landed, dst_ref=landed, send_sem=send_sems.at[3 * i + f],
                                                recv_sem=recv_sems.at[3 * i + f], device_id=sibling,
                                                device_id_type=MESH)

        loads = [pltpu.make_async_copy(src[i], stage[i], load_sems.at[i]) for i in range(nw)]
        for cp in loads:
            cp.start()
        sends = [fwd(i, w, f, c) for i, w in enumerate(ws) for f in range(3)]
        for cp in sends:
            cp.start()
        stores = [pltpu.make_async_copy(stage[i], w.shard_of(dst[i], me), store_sems.at[i])
                  for i, w in enumerate(ws)]
        for ld, st in zip(loads, stores):
            ld.wait()
            st.start()
        for i, w in enumerate(ws):
            for f in range(3):
                fwd(i, w, f, 1 - c).wait_recv()
        for cp in sends:
            cp.wait_send()
        for cp in stores:
            cp.wait()

    out = pl.pallas_call(
        body, name=name, in_specs=[ANY] * (2 * nw), out_specs=[ANY] * (2 * nw),
        out_shape=[jax.ShapeDtypeStruct(s.shape, BF16) for s in shards]
        + [jax.ShapeDtypeStruct(f.shape, BF16) for f in fulls],
        input_output_aliases={i: i for i in range(2 * nw)},
        scratch_shapes=[pltpu.VMEM((w.L, w.ks, w.ns), BF16) for w in ws]
        + [pltpu.SemaphoreType.DMA((3 * nw,)), pltpu.SemaphoreType.DMA((3 * nw,)), pltpu.SemaphoreType.DMA((nw,)),
           pltpu.SemaphoreType.DMA((nw,))],
        compiler_params=_params(has_side_effects=True))(*shards, *fulls)
    return out[nw:]


def _split_copies(name, srcs, lands, n_sems, copies_of, *, flight=None, after=None):
    n = len(srcs)
    starting = flight is None

    def body(*refs):
        src, land = refs[:n], refs[n:2 * n]
        sems = refs[2 * n + 1:4 * n + 1] if starting else refs[2 * n:4 * n]
        for i in range(n):
            for cp in copies_of(i, src[i], land[i], sems[i], sems[n + i]):
                if starting:
                    cp.start()
                else:
                    cp.wait_send()
                    cp.wait_recv()

    thru = [pltpu.HBM(a.shape, a.dtype) for a in list(srcs) + list(lands)]
    if starting:
        out = pl.pallas_call(
            body, name=name, in_specs=[HBM] * (2 * n) + [ANY], out_specs=[SEM] * (2 * n) + [HBM] * (2 * n),
            out_shape=[pltpu.SemaphoreType.DMA((n_sems,))] * (2 * n) + thru,
            input_output_aliases={i: 2 * n + i for i in range(2 * n)},
            compiler_params=pltpu.CompilerParams(has_side_effects=IN_FLIGHT))(
                *[_in_hbm(a) for a in srcs], *[_in_hbm(a) for a in lands], after)
        return [(out[i], out[n + i], out[2 * n + i], out[3 * n + i]) for i in range(n)]
    out = pl.pallas_call(
        body, name=name, in_specs=[HBM] * (2 * n) + [SEM] * (2 * n) + [ANY], out_specs=[HBM] * (2 * n),
        out_shape=thru, input_output_aliases={i: i for i in range(2 * n)},
        compiler_params=pltpu.CompilerParams(has_side_effects=IN_FLIGHT))(
            *srcs, *lands, *[fl[0] for fl in flight], *[fl[1] for fl in flight], after)
    return out[:n], out[n:]


def _sum8(land, vec, me):
    R = vec.shape[0]

    def body(me_ref, land_ref, vec_ref, o_ref):
        acc = jnp.zeros((R, 128), F32)
        for d in range(8):
            acc = acc + jnp.where(me_ref[0] == d, vec_ref[...], land_ref[d])
        o_ref[...] = acc

    grid_spec = pltpu.PrefetchScalarGridSpec(
        num_scalar_prefetch=1, grid=(1,),
        in_specs=[pl.BlockSpec((8, R, 128), lambda i, m: (0, 0, 0)), pl.BlockSpec((R, 128), lambda i, m: (0, 0))],
        out_specs=pl.BlockSpec((R, 128), lambda i, m: (0, 0)))
    return pl.pallas_call(body, name="sum8", grid_spec=grid_spec, out_shape=jax.ShapeDtypeStruct((R, 128), F32),
                          compiler_params=_params(("arbitrary",)))(me.reshape(1), land, vec)


def _swap_copies(i, src, got, send, recv):
    x, y, c = _mesh_pos()
    return [pltpu.make_async_remote_copy(src_ref=src.at[1 - c], dst_ref=got, send_sem=send.at[0], recv_sem=recv.at[0],
                                         device_id=(x, y, 1 - c), device_id_type=MESH)]


def _gather8_copies(i, src, land, send, recv):
    x, y, c = _mesh_pos()
    me = 4 * x + 2 * y + c
    peers = [(x, y, 1 - c)] + [(px, py, pc) for px, py in _other_chips(x, y) for pc in (c, 1 - c)]
    return [pltpu.make_async_remote_copy(src_ref=src, dst_ref=land.at[me], send_sem=send.at[k], recv_sem=recv.at[k],
                                         device_id=peer, device_id_type=MESH) for k, peer in enumerate(peers)]


def _scatter_copy(src, got, send, recv, f, chip, c):
    px, py = chip
    return pltpu.make_async_remote_copy(src_ref=src.at[2 * px + py], dst_ref=got.at[f], send_sem=send.at[f],
                                        recv_sem=recv.at[f], device_id=(px, py, c), device_id_type=MESH)


def _scatter_start(sums, *, name):
    nw = len(sums)

    def body(*refs):
        src, got = refs[:nw], refs[nw:2 * nw]
        send, recv = refs[2 * nw:3 * nw], refs[3 * nw:4 * nw]
        x, y, c = _mesh_pos()
        for i in range(nw):
            for f, chip in enumerate(_other_chips(x, y)):
                _scatter_copy(src[i], got[i], send[i], recv[i], f, chip, c).start()

    lands = [lax.empty((3,) + s.shape[1:], BF16) for s in sums]
    out = pl.pallas_call(
        body, name=name, in_specs=[HBM] * (2 * nw), out_specs=[SEM] * (2 * nw) + [HBM] * (2 * nw),
        out_shape=[pltpu.SemaphoreType.DMA((3,))] * (2 * nw)
        + [pltpu.HBM(s.shape, BF16) for s in sums] + [pltpu.HBM(l.shape, BF16) for l in lands],
        input_output_aliases={i: 2 * nw + i for i in range(2 * nw)},
        compiler_params=pltpu.CompilerParams(has_side_effects=IN_FLIGHT))(
            *[_in_hbm(s) for s in sums], *[_in_hbm(l) for l in lands])
    return [(out[i], out[nw + i], out[2 * nw + i], out[3 * nw + i]) for i in range(nw)]


def _scatter_wait(flight, after):
    nw = len(flight)

    def body(*refs):
        src, got = refs[:nw], refs[nw:2 * nw]
        send, recv = refs[2 * nw:3 * nw], refs[3 * nw:4 * nw]
        x, y, c = _mesh_pos()
        for i in range(nw):
            for f, chip in enumerate(_other_chips(x, y)):
                cp = _scatter_copy(src[i], got[i], send[i], recv[i], f, chip, c)
                cp.wait_send()
                cp.wait_recv()

    sums, lands = [fl[2] for fl in flight], [fl[3] for fl in flight]
    out = pl.pallas_call(
        body, name="scatter_wait", in_specs=[HBM] * (2 * nw) + [SEM] * (2 * nw) + [ANY], out_specs=[HBM] * (2 * nw),
        out_shape=[pltpu.HBM(s.shape, BF16) for s in sums] + [pltpu.HBM(l.shape, BF16) for l in lands],
        input_output_aliases={i: i for i in range(2 * nw)},
        compiler_params=pltpu.CompilerParams(has_side_effects=IN_FLIGHT))(
            *sums, *lands, *[fl[0] for fl in flight], *[fl[1] for fl in flight], after)
    return out[:nw], out[nw:]


def _join_halves(ws, shards):
    nw = len(ws)

    def body(*refs):
        buf = refs[nw:2 * nw]
        send_sems, recv_sems = refs[2 * nw:]
        x, y, c = _mesh_pos()
        sibling = (x, y, 1 - c)

        def copy(i, w, half):
            region = w.half_of(buf[i], half)
            return pltpu.make_async_remote_copy(src_ref=region, dst_ref=region, send_sem=send_sems.at[i],
                                                recv_sem=recv_sems.at[i], device_id=sibling, device_id_type=MESH)

        sends = [copy(i, w, c) for i, w in enumerate(ws)]
        for cp in sends:
            cp.start()
        for i, w in enumerate(ws):
            copy(i, w, 1 - c).wait_recv()
        for cp in sends:
            cp.wait_send()

    return pl.pallas_call(
        body, name="join_halves", in_specs=[ANY] * nw, out_specs=[ANY] * nw,
        out_shape=[jax.ShapeDtypeStruct((w.L, w.ks, w.ns), F32) for w in ws],
        input_output_aliases={i: i for i in range(nw)},
        scratch_shapes=[pltpu.SemaphoreType.DMA((nw,)), pltpu.SemaphoreType.DMA((nw,))],
        compiler_params=_params(has_side_effects=True))(*shards)


def _allreduce_small(vec):
    R = vec.shape[0]

    def body(x_ref, o_ref, buf, send_sems, recv_sems):
        x, y, c = _mesh_pos()
        me, sibling = (x, y, c), (x, y, 1 - c)
        chips = _other_chips(x, y)

        def slot(px, py, pc):
            return buf.at[4 * px + 2 * py + pc]

        def copy(k, block, to, src=None):
            return pltpu.make_async_remote_copy(src_ref=slot(*block) if src is None else src, dst_ref=slot(*block),
                                                send_sem=send_sems.at[k], recv_sem=recv_sems.at[k], device_id=to,
                                                device_id_type=MESH)

        first = [copy(0, me, sibling, src=x_ref)] + [copy(1 + f, me, (*chip, c), src=x_ref)
                                                     for f, chip in enumerate(chips)]
        for cp in first:
            cp.start()
        passed = [copy(4 + f, (*chip, c), sibling) for f, chip in enumerate(chips)]
        for f, chip in enumerate(chips):
            copy(1 + f, (*chip, c), me).wait_recv()
            passed[f].start()
        copy(0, sibling, me).wait_recv()
        for f, chip in enumerate(chips):
            copy(4 + f, (*chip, 1 - c), me).wait_recv()
        for cp in first + passed:
            cp.wait_send()
        slot(*me)[...] = x_ref[...]
        acc = buf[0]
        for d in range(1, 8):
            acc = acc + buf[d]
        o_ref[...] = acc

    return pl.pallas_call(
        body, name="allreduce_small", in_specs=[pl.BlockSpec(memory_space=pltpu.VMEM)],
        out_specs=pl.BlockSpec(memory_space=pltpu.VMEM), out_shape=jax.ShapeDtypeStruct((R, 128), F32),
        scratch_shapes=[pltpu.VMEM((8, R, 128), F32), pltpu.SemaphoreType.DMA((7,)), pltpu.SemaphoreType.DMA((7,))],
        compiler_params=_params())(vec)


def _pack(parts):
    flat = jnp.concatenate([p.reshape(-1).astype(F32) for p in parts])
    n = flat.shape[0]
    pad = (-n) % (64 * 128)
    return jnp.pad(flat, (0, pad)).reshape(-1, 128)


def _unpack(vec, shapes):
    flat = vec.reshape(-1)
    out, off = [], 0
    for s in shapes:
        n = int(np.prod(s))
        out.append(flat[off:off + n].reshape(s))
        off += n
    return out


def kernel(x, a_norm_g, a_w_in, a_v_norm_g, a_w_s, a_b_s, a_w_out, kv_norm_g, w_kv, b_norm_g, b_w_q, b_rel_bias, b_w_o, f_norm_g, f_w_in, f_conv_w, f_conv_b, f_w_down, final_norm_g, loss_target, m_a_norm_g, m_a_w_in, m_a_v_norm_g, m_a_w_s, m_a_b_s, m_a_w_out, m_kv_norm_g, m_w_kv, m_b_norm_g, m_b_w_q, m_b_rel_bias, m_b_w_o, m_f_norm_g, m_f_w_in, m_f_conv_w, m_f_conv_b, m_f_w_down, m_final_norm_g, v_a_norm_g, v_a_w_in, v_a_v_norm_g, v_a_w_s, v_a_b_s, v_a_w_out, v_kv_norm_g, v_w_kv, v_b_norm_g, v_b_w_q, v_b_rel_bias, v_b_w_o, v_f_norm_g, v_f_w_in, v_f_conv_w, v_f_conv_b, v_f_w_down, v_final_norm_g):
    B, S, D = x.shape
    T = B * S
    xi, yi, ci = lax.axis_index("x"), lax.axis_index("y"), lax.axis_index("c")
    j_me = (2 * xi + yi).astype(jnp.int32)
    core = ci.astype(jnp.int32)
    pos = jnp.stack([j_me, core])

    w_shards = {"a_w_in": (a_w_in, False), "a_w_out": (a_w_out, True), "w_kv": (w_kv[None], False),
                "b_w_q": (b_w_q, True), "b_w_o": (b_w_o, True), "f_w_in": (f_w_in, False), "f_w_down": (f_w_down, True)}
    names = list(w_shards)
    ws = [_W(n, w_shards[n][0], w_shards[n][1]) for n in names]
    g_shards = {"a_w_in": (a_w_in, False), "a_w_out": (a_w_out, True),
                "f_w_in0": (f_w_in[0:1], False), "f_w_down0": (f_w_down[0:1], True),
                "w_kv": (w_kv[None], False), "b_w_q": (b_w_q, True), "b_w_o": (b_w_o, True),
                "f_w_in1": (f_w_in[1:2], False), "f_w_down1": (f_w_down[1:2], True)}
    g_names = list(g_shards)
    g_ws = {n: _W(n, *g_shards[n]) for n in g_names}

    Wd = a_w_in.shape[1]
    GW = a_v_norm_g.shape[1] * N_CHIPS
    F2 = f_conv_w.shape[2] * N_CHIPS
    Fh = F2 // 2
    nsd, nsg, nsf = a_norm_g.shape[1], a_v_norm_g.shape[1], f_conv_w.shape[2]
    own = (ci == 0).astype(F32)
    place = lambda sh, width, n: lax.dynamic_update_slice_in_dim(
        jnp.zeros(sh.shape[:-1] + (width,), F32), sh * own, j_me * n, axis=sh.ndim - 1)
    gathered = _allreduce_small(_pack([place(a_norm_g, Wd, nsd), place(a_v_norm_g, GW, nsg),
                                       place(f_conv_w, F2, nsf)]))
    a_g, a_vg, conv_w = _unpack(gathered, [(1, Wd), (1, GW), (2, 3, F2)])

    flight = dict(zip(g_names, _gather_start([g_ws[n] for n in g_names],
                                             [g_shards[n][0].astype(BF16) for n in g_names], gathered)))
    full = {}

    def tied(x, flight):
        x, thru = lax.optimization_barrier((x, flight[0][2]))
        return x, [flight[0][:2] + (thru,) + flight[0][3:]] + flight[1:]

    def arrive(group, after, tag):
        gw = [g_ws[n] for n in group]
        sh, fu = _gather_wait(gw, [flight[n] for n in group], after, name=f"gather_wait_{tag}")
        full.update(zip(group, _gather_finish(gw, sh, fu, name=f"gather_finish_{tag}")))
    conv_w2 = conv_w.reshape(2, 3, 2, Fh).transpose(0, 2, 1, 3)
    conv_b2 = f_conv_b.reshape(2, 2, Fh)

    h0 = x.reshape(T, D)
    target = loss_target.reshape(T, D)
    bs_tile = jnp.repeat(a_b_s[0].T, GROUP_DIM, axis=1)
    ws_a = a_w_s[0]
    scale = HEAD_DIM ** -0.5
    HD = b_w_q.shape[2]
    H = HD // HEAD_DIM
    n_rel = b_rel_bias.shape[-1]
    frow, (flight["a_w_in"],) = tied(b_rel_bias[0][:, _bias_index()].reshape(H, 1, F_LEN), [flight["a_w_in"]])
    bias = _bias_expand(frow)

    def ffn_fwd(h, l, loss=None):
        yff, a, c, n = _ffn_in_conv(h, full[f"f_w_in{l}"], f_norm_g[l], conv_w2[l], conv_b2[l], S, name=f"ffn{l}_in")
        return _mm(yff, full[f"f_w_down{l}"], layer=0, res=h, loss=loss, name=f"ffn{l}_down"), (a, c, n, yff)

    arrive(["a_w_in", "a_w_out"], bias, "a")
    zp, n_a = _mm(h0, full["a_w_in"], layer=0, norm_g=a_g[0], out_dtype=BF16, emit_norm=True, name="a_in")
    out_a = _gate_fwd(zp, a_vg, ws_a, bs_tile)
    h1 = _mm(out_a, full["a_w_out"], layer=0, res=h0, name="a_out")
    arrive(["f_w_in0", "f_w_down0"], h1, "f0")
    h2, saved0 = ffn_fwd(h1, 0)
    arrive(["w_kv", "b_w_q", "b_w_o"], h2, "b")
    arrive(["f_w_in1", "f_w_down1"], h2, "f1")
    kv, n_kv = _mm(h2, full["w_kv"], layer=0, norm_g=kv_norm_g, out_dtype=BF16, split_out=True, emit_norm=True,
                   name="kv")
    q, n_q = _mm(h2, full["b_w_q"], layer=0, norm_g=b_norm_g[0], scale=scale, out_dtype=BF16, emit_norm=True,
                 name="q")
    kv4, q3 = kv.reshape(2, B, S, HD), q.reshape(B, S, HD)
    o = _attn_fwd(q3, kv4, bias, B, S).reshape(T, HD)
    h3 = _mm(o, full["b_w_o"], layer=0, res=h2, name="attn_out")
    (dh, loss8, dg_final), saved1 = ffn_fwd(h3, 1, loss=(final_norm_g, target))

    units = {}

    in_flight = {}

    def swap_start(group, tag, carry):
        us = [units[n] for n in group]
        lands = [lax.empty(u.shape[1:], BF16) for u in us]
        carry, flight = tied(carry, _split_copies(f"swap_start_{tag}", us, lands, 1, _swap_copies, after=carry))
        return (group, tag, flight), carry

    def reduce_start(swap, after):
        group, tag, flight = swap
        us, got = _split_copies(f"swap_wait_{tag}", [fl[2] for fl in flight], [fl[3] for fl in flight], 1,
                                _swap_copies, flight=flight, after=after)
        sums = [_add_pair(u, g_, core, name=f"pair_{n}") for n, u, g_ in zip(group, us, got)]
        after, flight = tied(after, _scatter_start(sums, name=f"scatter_start_{tag}"))
        in_flight.update(zip(group, flight))
        return after

    def ffn_bwd(dh, h, saved, l, early):
        a, c, n, yff = saved
        units[f"f_w_down{l}"] = _mm_tn(yff, dh, rows_are_shards=True, name=f"ffn{l}_down_dw")
        dh_in = dh
        if early:
            sw, dh_in = swap_start([f"f_w_down{l}"], f"fd{l}", dh)
        dyff = _mm(dh_in, full[f"f_w_down{l}"], layer=0, trans_w=True, out_dtype=BF16, name=f"ffn{l}_down_dx")
        if early:
            dyff = reduce_start(sw, dyff)
        da, dcw, dcb = _conv_bwd(a, c, dyff, conv_w2[l], S)
        units[f"f_w_in{l}"] = _mm_tn(n, da, split_y=True, name=f"ffn{l}_in_dw")
        sw, da = swap_start([f"f_w_in{l}"] if early else [f"f_w_down{l}", f"f_w_in{l}"], f"f{l}", da)
        dh, dg = _mm(da, full[f"f_w_in{l}"], layer=0, trans_w=True, split_x=True, bwd=(h, f_norm_g[l], dh),
                     name=f"ffn{l}_in_dx")
        return reduce_start(sw, dh), dg, dcw, dcb

    dh, dg_f1, dcw1, dcb1 = ffn_bwd(dh, h3, saved1, 1, False)
    do = _mm(dh, full["b_w_o"], layer=0, trans_w=True, out_dtype=BF16, name="attn_out_dx")
    units["b_w_o"] = _mm_tn(o, dh, rows_are_shards=True, name="b_w_o_dw")
    dq, dkv, dbias = _attn_bwd(q3, kv4, bias, do.reshape(B, S, HD), B, S)
    dq, d_rel = lax.optimization_barrier((dq, _bias_reduce(dbias, n_rel)))
    d_rel = d_rel.reshape(1, H, n_rel)
    dq, dkv = dq.reshape(T, HD), dkv.reshape(2, T, HD)
    units["b_w_q"] = _mm_tn(n_q, dq, rows_are_shards=True, name="b_w_q_dw")
    dh, dg_b = _mm(dq, full["b_w_q"], layer=0, trans_w=True, bwd=(h2, b_norm_g[0], dh), name="q_dx")
    units["w_kv"] = _mm_tn(n_kv, dkv, split_y=True, name="w_kv_dw")
    sw, dkv = swap_start(["b_w_o", "b_w_q", "w_kv"], "b", dkv)
    dh, dg_kv = _mm(dkv, full["w_kv"], layer=0, trans_w=True, split_x=True, bwd=(h2, kv_norm_g, dh), name="kv_dx")
    dh = reduce_start(sw, dh)
    dh, dg_f0, dcw0, dcb0 = ffn_bwd(dh, h1, saved0, 0, True)
    units["a_w_out"] = _mm_tn(out_a, dh, rows_are_shards=True, name="a_w_out_dw")
    sw, dh_in = swap_start(["a_w_out"], "ao", dh)
    d_out = _mm(dh_in, full["a_w_out"], layer=0, trans_w=True, out_dtype=BF16, name="a_out_dx")
    d_out = reduce_start(sw, d_out)
    dzp, dws, dbs, dgv = _gate_bwd(zp, d_out, a_vg, ws_a, bs_tile)
    units["a_w_in"] = _mm_tn(n_a, dzp, name="a_w_in_dw")
    sw, dzp_in = swap_start(["a_w_in"], "ai", dzp)
    grad_x, dg_a = _mm(dzp_in, full["a_w_in"], layer=0, trans_w=True, bwd=(h0, a_g[0], dh), name="a_in_dx")
    grad_x = reduce_start(sw, grad_x)

    to_flat = lambda d: d.transpose(1, 0, 2).reshape(3, F2)
    small = {"a_norm_g": dg_a, "a_v_norm_g": dgv, "a_w_s": dws[None], "a_b_s": dbs[None], "kv_norm_g": dg_kv[0],
             "b_norm_g": dg_b, "b_rel_bias": d_rel, "f_norm_g": jnp.concatenate([dg_f0, dg_f1], axis=0),
             "f_conv_w": jnp.stack([to_flat(dcw0), to_flat(dcw1)]),
             "f_conv_b": jnp.stack([dcb0.reshape(F2), dcb1.reshape(F2)]), "final_norm_g": dg_final[0]}
    snames = list(small)
    small_vec = _pack([small[n] for n in snames] + [loss8[0:1, 0:1]])
    grad_x, small_flight = tied(grad_x, _split_copies("small_start", [small_vec],
                                                      [lax.empty((8,) + small_vec.shape, F32)], 7, _gather8_copies,
                                                      after=grad_x))

    sums, recv = _scatter_wait([in_flight[n] for n in g_names], grad_x)
    sums, recv = dict(zip(g_names, sums)), dict(zip(g_names, recv))
    halves = []
    for n, w in zip(names, ws):
        if w.L == 1:
            halves.append(_sum_chips(w, sums[n], recv[n], pos, name=f"chips_{n}"))
        else:
            first = _sum_chips(w, sums[n + "0"], recv[n + "0"], pos, name=f"chips_{n}0")
            halves.append(_sum_chips(w, sums[n + "1"], recv[n + "1"], pos, layer=1, into=first, name=f"chips_{n}1"))
    g_big = dict(zip(names, _join_halves(ws, halves)))
    g_big["w_kv"] = g_big["w_kv"][0]

    given = dict(a_norm_g=(a_norm_g, m_a_norm_g, v_a_norm_g), a_w_in=(a_w_in, m_a_w_in, v_a_w_in),
                 a_v_norm_g=(a_v_norm_g, m_a_v_norm_g, v_a_v_norm_g), a_w_s=(a_w_s, m_a_w_s, v_a_w_s),
                 a_b_s=(a_b_s, m_a_b_s, v_a_b_s), a_w_out=(a_w_out, m_a_w_out, v_a_w_out),
                 kv_norm_g=(kv_norm_g, m_kv_norm_g, v_kv_norm_g), w_kv=(w_kv, m_w_kv, v_w_kv),
                 b_norm_g=(b_norm_g, m_b_norm_g, v_b_norm_g), b_w_q=(b_w_q, m_b_w_q, v_b_w_q),
                 b_rel_bias=(b_rel_bias, m_b_rel_bias, v_b_rel_bias), b_w_o=(b_w_o, m_b_w_o, v_b_w_o),
                 f_norm_g=(f_norm_g, m_f_norm_g, v_f_norm_g), f_w_in=(f_w_in, m_f_w_in, v_f_w_in),
                 f_conv_w=(f_conv_w, m_f_conv_w, v_f_conv_w), f_conv_b=(f_conv_b, m_f_conv_b, v_f_conv_b),
                 f_w_down=(f_w_down, m_f_w_down, v_f_w_down), final_norm_g=(final_norm_g, m_final_norm_g, v_final_norm_g))
    order = list(given)
    grads, deltas, new_m, new_v = {}, {}, {}, {}
    for n in names:
        w_, m_, v_ = given[n]
        g_ = g_big[n]
        C = w_.shape[-1]
        d2, m2, v2 = _adamw(w_.reshape(-1, C), g_.reshape(-1, C), m_.reshape(-1, C), v_.reshape(-1, C),
                            name=f"adamw_{n}")
        grads[n], deltas[n], new_m[n], new_v[n] = g_.reshape(w_.shape), d2.reshape(w_.shape), m2.reshape(w_.shape), \
            v2.reshape(w_.shape)
    vecs, lands = _split_copies("small_wait", [small_flight[0][2]], [small_flight[0][3]], 7, _gather8_copies,
                                flight=small_flight, after=deltas[names[-1]])
    red = _sum8(lands[0], vecs[0], (4 * xi + 2 * yi + ci).astype(jnp.int32))
    parts = _unpack(red, [small[n].shape for n in snames] + [(1,)])
    g_small = dict(zip(snames, parts[:-1]))
    loss = parts[-1][0]
    g_small["a_norm_g"] = lax.dynamic_slice_in_dim(g_small["a_norm_g"], j_me * nsd, nsd, axis=1)
    g_small["a_v_norm_g"] = lax.dynamic_slice_in_dim(g_small["a_v_norm_g"], j_me * nsg, nsg, axis=1)
    g_small["f_conv_w"] = lax.dynamic_slice_in_dim(g_small["f_conv_w"], j_me * nsf, nsf, axis=2)

    sm = [n for n in order if n not in names]
    d2, m2, v2 = _adamw(_pack([given[n][0] for n in sm]), _pack([g_small[n].reshape(given[n][0].shape) for n in sm]),
                        _pack([given[n][1] for n in sm]), _pack([given[n][2] for n in sm]), name="adamw_small")
    shapes = [given[n][0].shape for n in sm]
    for n, d_, m_, v_ in zip(sm, _unpack(d2, shapes), _unpack(m2, shapes), _unpack(v2, shapes)):
        grads[n], deltas[n], new_m[n], new_v[n] = g_small[n].reshape(given[n][0].shape), d_, m_, v_

    return (loss, grad_x.reshape(B, S, D), *[grads[n] for n in order], *[deltas[n] for n in order],
            *[new_m[n] for n in order], *[new_v[n] for n in order])
```

```python
import functools
import math

import numpy as np
import jax
import jax.numpy as jnp
from jax import lax
from jax.experimental import pallas as pl
from jax.experimental.pallas import tpu as pltpu

F32 = jnp.float32
BF16 = jnp.bfloat16
MESH = pl.DeviceIdType.MESH

EPS = 1e-6
NEG_INF = -1e30
CHUNK = 64
GMLP_BLOCK = 128
GROUP_DIM = 128
HEAD_DIM = 64
LEFT_CHUNKS = 8
PAD = LEFT_CHUNKS * CHUNK
REL_CLIP = 128
Q_BLOCK = 256
K_SPAN = PAD + Q_BLOCK
F_LEN = K_SPAN + Q_BLOCK
HEADS_PER_STEP = 4
N_CHIPS = 4

ADAM_LR = 0.001
ADAM_B1 = 0.9
ADAM_B2 = 0.999
ADAM_EPS = 1e-08
ADAM_WD = 0.01
ADAM_STEP = 10

VMEM_LIMIT = 56 * 1024 * 1024


def _params(sem=None, **kw):
    if sem is not None:
        kw["dimension_semantics"] = sem
    return pltpu.CompilerParams(vmem_limit_bytes=VMEM_LIMIT, **kw)


def _rms(xf):
    r = lax.rsqrt(jnp.mean(xf * xf, axis=-1, keepdims=True) + EPS)
    return xf * r, r


def _gelu(x, with_grad=False):
    c = math.sqrt(2.0 / math.pi)
    x2 = x * x
    t = jnp.tanh(c * x * (1.0 + 0.044715 * x2))
    half = 0.5 * (1.0 + t)
    if not with_grad:
        return x * half
    return x * half, half + 0.5 * x * (1.0 - t * t) * c * (1.0 + 3.0 * 0.044715 * x2)


def _col_tile(n):
    if n <= 1024:
        return n
    for t in (1408, 1024, 512):
        if n % t == 0:
            return t
    raise ValueError(n)


def _row_tile(t, want):
    while t % want:
        want //= 2
    return want


def _mm(x, w, *, name, layer=None, trans_w=False, norm_g=None, res=None, scale=None, out_dtype=F32, bwd=None,
        split_out=False, split_x=False, emit_norm=False, loss=None, tm=512):
    T = x.shape[-2]
    K = 2 * x.shape[-1] if split_x else x.shape[-1]
    N = w.shape[-2] if trans_w else w.shape[-1]
    tn = N
    tm = _row_tile(T, 2 * tm if max(K, N) <= 2048 else tm)
    nn, nm = N // tn, T // tm
    has_norm, has_res, has_bwd, has_loss = norm_g is not None, res is not None, bwd is not None, loss is not None
    dims = (((1,), (1,)), ((), ())) if trans_w else (((1,), (0,)), ((), ()))

    def body(*refs):
        it = iter(refs)
        x_ref, w_ref = next(it), next(it)
        g_ref = next(it) if has_norm else None
        res_ref = next(it) if has_res else None
        if has_bwd:
            h_ref, bg_ref, dh_ref = next(it), next(it), next(it)
        if has_loss:
            lg_ref, t_ref = next(it), next(it)
        o_ref = next(it)
        if split_x:
            kh = K // 2
            acc = lax.dot_general(x_ref[0].astype(BF16), w_ref[:, :kh] if trans_w else w_ref[:kh, :], dims,
                                  preferred_element_type=F32)
            acc = acc + lax.dot_general(x_ref[1].astype(BF16), w_ref[:, kh:] if trans_w else w_ref[kh:, :], dims,
                                        preferred_element_type=F32)
        else:
            xv = x_ref[...]
            if has_norm:
                xv = _rms(xv.astype(F32))[0] * g_ref[...]
            xb = xv.astype(BF16)
            if emit_norm:
                refs[-1][...] = xb
            acc = lax.dot_general(xb, w_ref[...], dims, preferred_element_type=F32)
        if scale is not None:
            acc = acc * scale
        if has_res:
            acc = acc + res_ref[...]
        if has_bwd:
            dg_ref = next(it)
            n, r = _rms(h_ref[...])

            @pl.when(pl.program_id(1) == 0)
            def _():
                dg_ref[...] = jnp.zeros_like(dg_ref)

            dg_ref[...] += jnp.sum(acc * n, axis=0, keepdims=True)
            t = acc * bg_ref[...]
            o_ref[...] = dh_ref[...] + r * (t - n * jnp.mean(t * n, axis=-1, keepdims=True))
        elif has_loss:
            loss_ref, dg_ref = refs[-2], refs[-1]

            @pl.when(pl.program_id(1) == 0)
            def _():
                loss_ref[...] = jnp.zeros_like(loss_ref)
                dg_ref[...] = jnp.zeros_like(dg_ref)

            n, r = _rms(acc)
            g = lg_ref[...]
            e = n * g - t_ref[...]
            loss_ref[...] += 0.5 * jnp.sum(jnp.mean(e * e, axis=-1, keepdims=True), axis=0, keepdims=True)
            dy = e * (1.0 / N)
            dg_ref[...] += jnp.sum(dy * n, axis=0, keepdims=True)
            t = dy * g
            o_ref[...] = r * (t - n * jnp.mean(t * n, axis=-1, keepdims=True))
        elif split_out:
            o_ref[0] = acc[:, :N // 2].astype(out_dtype)
            o_ref[1] = acc[:, N // 2:].astype(out_dtype)
        else:
            o_ref[...] = acc.astype(out_dtype)

    lead = () if layer is None else (None,)
    lidx = () if layer is None else (layer,)
    ins = [x, w]
    xspec = (pl.BlockSpec((2, tm, K // 2), lambda n, m: (0, m, 0)) if split_x
             else pl.BlockSpec((tm, K), lambda n, m: (m, 0)))
    once = pl.Buffered(1)
    wspec = (pl.BlockSpec(lead + (tn, K), lambda n, m: lidx + (n, 0), pipeline_mode=once) if trans_w
             else pl.BlockSpec(lead + (K, tn), lambda n, m: lidx + (0, n), pipeline_mode=once))
    in_specs = [xspec, wspec]
    if has_norm:
        ins.append(norm_g.reshape(1, K))
        in_specs.append(pl.BlockSpec((1, K), lambda n, m: (0, 0)))
    if has_res:
        ins.append(res)
        in_specs.append(pl.BlockSpec((tm, tn), lambda n, m: (m, n)))
    if split_out:
        out_shape = [jax.ShapeDtypeStruct((2, T, N // 2), out_dtype)]
        out_specs = [pl.BlockSpec((2, tm, N // 2), lambda n, m: (0, m, 0))]
    else:
        out_shape = [jax.ShapeDtypeStruct((T, N), F32 if has_bwd else out_dtype)]
        out_specs = [pl.BlockSpec((tm, tn), lambda n, m: (m, n))]
    if has_bwd:
        h, g, dh = bwd
        ins += [h, g.reshape(1, N), dh]
        in_specs += [pl.BlockSpec((tm, N), lambda n, m: (m, 0)), pl.BlockSpec((1, N), lambda n, m: (0, 0)),
                     pl.BlockSpec((tm, N), lambda n, m: (m, 0))]
        out_shape.append(jax.ShapeDtypeStruct((1, N), F32))
        out_specs.append(pl.BlockSpec((1, N), lambda n, m: (0, 0)))
    if emit_norm:
        out_shape.append(jax.ShapeDtypeStruct((T, K), BF16))
        out_specs.append(pl.BlockSpec((tm, K), lambda n, m: (m, 0)))
    if has_loss:
        ins += [loss[0].reshape(1, N), loss[1]]
        in_specs += [pl.BlockSpec((1, N), lambda n, m: (0, 0)), pl.BlockSpec((tm, N), lambda n, m: (m, 0))]
        out_shape += [jax.ShapeDtypeStruct((8, 128), F32), jax.ShapeDtypeStruct((1, N), F32)]
        out_specs += [pl.BlockSpec((8, 128), lambda n, m: (0, 0)), pl.BlockSpec((1, N), lambda n, m: (0, 0))]
    out = pl.pallas_call(body, name=name, grid=(nn, nm), in_specs=in_specs, out_specs=out_specs, out_shape=out_shape,
                         compiler_params=_params(("arbitrary", "arbitrary")))(*ins)
    return out if has_bwd or emit_norm or has_loss else out[0]


def _mm_tn(x, dy, *, name, rows_are_shards=False, split_y=False, tt=1024):
    T, K = x.shape
    N = 2 * dy.shape[-1] if split_y else dy.shape[-1]
    R, C = (K // N_CHIPS, N // 2) if rows_are_shards else (K // 2, N // N_CHIPS)
    nn = 2 if split_y else 1
    tn = N // nn
    per = N_CHIPS // nn
    assert not (rows_are_shards and split_y)
    tt = _row_tile(T, tt)
    nt = T // tt

    def body(x_ref, y_ref, o_ref, acc_ref):
        t = pl.program_id(1)

        @pl.when(t == 0)
        def _():
            acc_ref[...] = jnp.zeros_like(acc_ref)

        acc_ref[...] += lax.dot_general(x_ref[...], y_ref[...].astype(BF16), (((0,), (0,)), ((), ())),
                                        preferred_element_type=F32)

        @pl.when(t == nt - 1)
        def _():
            if rows_are_shards:
                for h in range(2):
                    o_ref[h] = acc_ref[:, h * C:(h + 1) * C].astype(BF16).reshape(N_CHIPS, R, C)
            else:
                for j in range(per):
                    o_ref[:, j] = acc_ref[:, j * C:(j + 1) * C].astype(BF16).reshape(2, R, C)

    if split_y:
        yspec = pl.BlockSpec((None, tt, tn), lambda n, t: (n, t, 0))
    else:
        yspec = pl.BlockSpec((tt, tn), lambda n, t: (t, 0))
    if rows_are_shards:
        out_spec = pl.BlockSpec((2, N_CHIPS, R, C), lambda n, t: (0, 0, 0, 0))
    else:
        out_spec = pl.BlockSpec((2, per, R, C), lambda n, t: (0, n, 0, 0))
    return pl.pallas_call(body, name=name, grid=(nn, nt),
                          in_specs=[pl.BlockSpec((tt, K), lambda n, t: (t, 0)), yspec], out_specs=out_spec,
                          out_shape=jax.ShapeDtypeStruct((2, N_CHIPS, R, C), BF16),
                          scratch_shapes=[pltpu.VMEM((K, tn), F32)],
                          compiler_params=_params(("arbitrary", "arbitrary")))(x, dy)


def _chunk_mask():
    i = lax.broadcasted_iota(jnp.int32, (GMLP_BLOCK, GMLP_BLOCK), 0) // CHUNK
    j = lax.broadcasted_iota(jnp.int32, (GMLP_BLOCK, GMLP_BLOCK), 1) // CHUNK
    return i >= j


def _gate_fwd(zp, gv, ws, bs_tile, *, tm=256):
    T, W2 = zp.shape
    W = W2 // 2
    G = W // GROUP_DIM
    tm = _row_tile(T, tm)

    def body(zp_ref, gv_ref, ws_ref, bs_ref, o_ref):
        z = _gelu(zp_ref[...].astype(F32))
        u, v = z[:, :W], z[:, W:]
        vn = _rms(v)[0] * gv_ref[...]
        mask = _chunk_mask()
        for g in range(G):
            cs = slice(g * GROUP_DIM, (g + 1) * GROUP_DIM)
            wg = jnp.where(mask, ws_ref[g], 0.0).astype(BF16)
            for b in range(tm // GMLP_BLOCK):
                rs = slice(b * GMLP_BLOCK, (b + 1) * GMLP_BLOCK)
                s = jnp.dot(wg, vn[rs, cs].astype(BF16), preferred_element_type=F32) + bs_ref[:, cs]
                o_ref[rs, cs] = (u[rs, cs] * s).astype(BF16)

    return pl.pallas_call(
        body, name="gate_fwd", grid=(T // tm,),
        in_specs=[pl.BlockSpec((tm, W2), lambda i: (i, 0)), pl.BlockSpec((1, W), lambda i: (0, 0)),
                  pl.BlockSpec((G, GMLP_BLOCK, GMLP_BLOCK), lambda i: (0, 0, 0)),
                  pl.BlockSpec((GMLP_BLOCK, W), lambda i: (0, 0))],
        out_specs=pl.BlockSpec((tm, W), lambda i: (i, 0)), out_shape=jax.ShapeDtypeStruct((T, W), BF16),
        compiler_params=_params(("arbitrary",)))(zp, gv, ws, bs_tile)


def _gate_bwd(zp, d_out, gv, ws, bs_tile, *, tm=256):
    T, W2 = zp.shape
    W = W2 // 2
    G = W // GROUP_DIM
    tm = _row_tile(T, tm)
    nm = T // tm

    def body(zp_ref, do_ref, gv_ref, ws_ref, bs_ref, dzp_ref, dws_ref, dbs_ref, dgv_ref, du_scr, dvn_scr, dsum_scr):
        i = pl.program_id(0)

        @pl.when(i == 0)
        def _():
            dws_ref[...] = jnp.zeros_like(dws_ref)
            dgv_ref[...] = jnp.zeros_like(dgv_ref)
            dsum_scr[...] = jnp.zeros_like(dsum_scr)

        zp = zp_ref[...].astype(F32)
        z, dz = _gelu(zp, with_grad=True)
        u, v = z[:, :W], z[:, W:]
        n, r = _rms(v)
        gv = gv_ref[...]
        vn = n * gv
        d_out = do_ref[...].astype(F32)
        mask = _chunk_mask()
        for g in range(G):
            cs = slice(g * GROUP_DIM, (g + 1) * GROUP_DIM)
            wg = jnp.where(mask, ws_ref[g], 0.0).astype(BF16)
            dw = jnp.zeros((GMLP_BLOCK, GMLP_BLOCK), F32)
            for b in range(tm // GMLP_BLOCK):
                rs = slice(b * GMLP_BLOCK, (b + 1) * GMLP_BLOCK)
                vb = vn[rs, cs].astype(BF16)
                s = jnp.dot(wg, vb, preferred_element_type=F32) + bs_ref[:, cs]
                du_scr[rs, cs] = d_out[rs, cs] * s
                ds = d_out[rs, cs] * u[rs, cs]
                dsb = ds.astype(BF16)
                dvn_scr[rs, cs] = lax.dot_general(wg, dsb, (((0,), (0,)), ((), ())), preferred_element_type=F32)
                dw = dw + lax.dot_general(dsb, vb, (((1,), (1,)), ((), ())), preferred_element_type=F32)
                dsum_scr[:, cs] += ds
            dws_ref[g] += jnp.where(mask, dw, 0.0)
        dvn = dvn_scr[...]
        dgv_ref[...] += jnp.sum(dvn * n, axis=0, keepdims=True)
        t = dvn * gv
        dv = r * (t - n * jnp.mean(t * n, axis=-1, keepdims=True))
        dzp_ref[:, :W] = (du_scr[...] * dz[:, :W]).astype(BF16)
        dzp_ref[:, W:] = (dv * dz[:, W:]).astype(BF16)

        @pl.when(i == nm - 1)
        def _():
            sel = (lax.broadcasted_iota(jnp.int32, (G, W), 1) // GROUP_DIM
                   == lax.broadcasted_iota(jnp.int32, (G, W), 0)).astype(F32)
            dbs_ref[...] = lax.dot_general(sel, dsum_scr[...], (((1,), (1,)), ((), ())),
                                           precision=lax.Precision.HIGHEST, preferred_element_type=F32)

    return pl.pallas_call(
        body, name="gate_bwd", grid=(nm,),
        in_specs=[pl.BlockSpec((tm, W2), lambda i: (i, 0)), pl.BlockSpec((tm, W), lambda i: (i, 0)),
                  pl.BlockSpec((1, W), lambda i: (0, 0)),
                  pl.BlockSpec((G, GMLP_BLOCK, GMLP_BLOCK), lambda i: (0, 0, 0)),
                  pl.BlockSpec((GMLP_BLOCK, W), lambda i: (0, 0))],
        out_specs=[pl.BlockSpec((tm, W2), lambda i: (i, 0)),
                   pl.BlockSpec((G, GMLP_BLOCK, GMLP_BLOCK), lambda i: (0, 0, 0)),
                   pl.BlockSpec((G, GMLP_BLOCK), lambda i: (0, 0)), pl.BlockSpec((1, W), lambda i: (0, 0))],
        out_shape=[jax.ShapeDtypeStruct((T, W2), BF16), jax.ShapeDtypeStruct((G, GMLP_BLOCK, GMLP_BLOCK), F32),
                   jax.ShapeDtypeStruct((G, GMLP_BLOCK), F32), jax.ShapeDtypeStruct((1, W), F32)],
        scratch_shapes=[pltpu.VMEM((tm, W), F32), pltpu.VMEM((tm, W), F32), pltpu.VMEM((GMLP_BLOCK, W), F32)],
        compiler_params=_params(("arbitrary",)))(zp, d_out, gv, ws, bs_tile)


LANES = 128
HALO = 16


def _taps(ext, w, b):
    return w[2:3] * ext[HALO:] + w[1:2] * pltpu.roll(ext, 1, 0)[HALO:] + w[0:1] * pltpu.roll(ext, 2, 0)[HALO:] + b


def _ffn_in_conv(h, w, g, cw, cb, S, *, name, tm=256):
    T, D = h.shape
    F = w.shape[-1] // 2
    tc = _col_tile(F)
    tm = _row_tile(S, tm)

    def body(h_ref, w_ref, g_ref, cw_ref, cb_ref, y_ref, a_ref, c_ref, n_ref, tail):
        first = (pl.program_id(0) * tm) % S == 0
        nb = (_rms(h_ref[...])[0] * g_ref[...]).astype(BF16)
        n_ref[...] = nb
        for j in range(F // tc):
            cs = slice(j * tc, (j + 1) * tc)
            conv = []
            for s in range(2):
                acc = jnp.dot(nb, w_ref[:, s * F + j * tc:s * F + (j + 1) * tc], preferred_element_type=F32)
                ab = acc.astype(BF16)
                a_ref[s, :, cs] = ab
                af = ab.astype(F32)
                ext = jnp.concatenate([jnp.where(first, 0.0, tail[s, :, cs]), af], axis=0)
                tail[s, :, cs] = af[tm - HALO:, :]
                cv = _taps(ext, cw_ref[s, :, cs], cb_ref[s:s + 1, cs]).astype(BF16)
                c_ref[s, :, cs] = cv
                conv.append(cv.astype(F32))
            up, gate = conv
            y_ref[:, cs] = (gate * jax.nn.sigmoid(gate) * up).astype(BF16)

    row = lambda width: pl.BlockSpec((tm, width), lambda i: (i, 0))
    wide = pl.BlockSpec((2, tm, F), lambda i: (0, i, 0))
    return pl.pallas_call(
        body, name=name, grid=(T // tm,),
        in_specs=[row(D), pl.BlockSpec((None, D, 2 * F), lambda i: (0, 0, 0), pipeline_mode=pl.Buffered(1)),
                  pl.BlockSpec((1, D), lambda i: (0, 0)),
                  pl.BlockSpec((2, 3, F), lambda i: (0, 0, 0)), pl.BlockSpec((2, F), lambda i: (0, 0))],
        out_specs=[row(F), wide, wide, row(D)],
        out_shape=[jax.ShapeDtypeStruct((T, F), BF16), jax.ShapeDtypeStruct((2, T, F), BF16),
                   jax.ShapeDtypeStruct((2, T, F), BF16), jax.ShapeDtypeStruct((T, D), BF16)],
        scratch_shapes=[pltpu.VMEM((2, HALO, F), F32)],
        compiler_params=_params(("arbitrary",)))(h, w, g.reshape(1, D), cw, cb)


def _conv_bwd(a, c, dy, cw, S, *, tm=256):
    _, T, F = a.shape
    tc = _col_tile(F)
    tm = _row_tile(S, tm)
    nm = T // tm
    hb = tm // HALO
    TE = tm + HALO
    nxt = lambda j, i: jnp.minimum((i + 1) * hb, T // HALO - 1)

    def body(a_ref, c_ref, nc_ref, dy_ref, ndy_ref, w_ref, da_ref, dw_ref, db_ref):
        i = pl.program_id(1)
        last = ((i + 1) * tm) % S == 0
        keep_n = jnp.where(last, 0.0, 1.0)

        @pl.when(i == 0)
        def _():
            dw_ref[...] = jnp.zeros_like(dw_ref)
            db_ref[...] = jnp.zeros_like(db_ref)

        for j in range(tc // LANES):
            cs = slice(j * LANES, (j + 1) * LANES)
            dyf = jnp.concatenate([dy_ref[:, cs].astype(F32), ndy_ref[:, cs].astype(F32) * keep_n], axis=0)
            up = jnp.concatenate([c_ref[0, :, cs].astype(F32), nc_ref[0, :, cs].astype(F32)], axis=0)
            gate = jnp.concatenate([c_ref[1, :, cs].astype(F32), nc_ref[1, :, cs].astype(F32)], axis=0)
            sg = jax.nn.sigmoid(gate)
            for s, d in ((0, dyf * (gate * sg)), (1, dyf * up * (sg * (1.0 + gate * (1.0 - sg))))):
                a = a_ref[s, :, cs].astype(F32)
                w = w_ref[s, :, cs]
                u1, u2 = pltpu.roll(d, TE - 1, 0), pltpu.roll(d, TE - 2, 0)
                db_ref[s:s + 1, cs] += jnp.sum(d[:tm], axis=0, keepdims=True)
                dw_ref[s, 2:3, cs] += jnp.sum(d[:tm] * a, axis=0, keepdims=True)
                dw_ref[s, 1:2, cs] += jnp.sum(u1[:tm] * a, axis=0, keepdims=True)
                dw_ref[s, 0:1, cs] += jnp.sum(u2[:tm] * a, axis=0, keepdims=True)
                da_ref[s, :, cs] = (w[2:3] * d + w[1:2] * u1 + w[0:1] * u2)[:tm].astype(BF16)

    cur = pl.BlockSpec((2, tm, tc), lambda j, i: (0, i, j))
    return pl.pallas_call(
        body, name="conv_bwd", grid=(F // tc, nm),
        in_specs=[cur, cur, pl.BlockSpec((2, HALO, tc), lambda j, i: (0, nxt(j, i), j)),
                  pl.BlockSpec((tm, tc), lambda j, i: (i, j)), pl.BlockSpec((HALO, tc), lambda j, i: (nxt(j, i), j)),
                  pl.BlockSpec((2, 3, tc), lambda j, i: (0, 0, j))],
        out_specs=[cur, pl.BlockSpec((2, 3, tc), lambda j, i: (0, 0, j)), pl.BlockSpec((2, tc), lambda j, i: (0, j))],
        out_shape=[jax.ShapeDtypeStruct((2, T, F), BF16), jax.ShapeDtypeStruct((2, 3, F), F32),
                   jax.ShapeDtypeStruct((2, F), F32)],
        compiler_params=_params(("arbitrary", "arbitrary")))(a, c, c, dy, dy, cw)


def _bias_index():
    idx = np.arange(F_LEN)
    d = np.where(idx < K_SPAN, idx, idx - F_LEN)
    return np.clip(PAD - d, -REL_CLIP, REL_CLIP) + REL_CLIP


ROW_GROUP = 16


def _roll_rows(x, sign, unit, steps):
    rows = lax.broadcasted_iota(jnp.int32, x.shape, 0)
    step = 1
    while step < steps:
        shift = unit * step if sign > 0 else F_LEN - unit * step
        x = jnp.where((rows & step) != 0, pltpu.roll(x, shift, 1), x)
        step *= 2
    return x


def _bias_expand(frow):
    H = frow.shape[0]
    groups = Q_BLOCK // ROW_GROUP

    def body(f_ref, o_ref):
        coarse = _roll_rows(jnp.broadcast_to(f_ref[...], (groups, F_LEN)), 1, ROW_GROUP, groups)
        x = jnp.concatenate([jnp.broadcast_to(coarse[a:a + 1], (ROW_GROUP, F_LEN)) for a in range(groups)], axis=0)
        x = _roll_rows(x, 1, 1, ROW_GROUP)[:, :K_SPAN]
        qc = lax.broadcasted_iota(jnp.int32, (Q_BLOCK, K_SPAN), 0) // CHUNK * CHUNK
        kj = lax.broadcasted_iota(jnp.int32, (Q_BLOCK, K_SPAN), 1)
        o_ref[...] = jnp.where((kj >= qc) & (kj < qc + PAD + CHUNK), x, NEG_INF)

    return pl.pallas_call(
        body, name="bias_expand", grid=(H,),
        in_specs=[pl.BlockSpec((None, 1, F_LEN), lambda h: (h, 0, 0))],
        out_specs=pl.BlockSpec((None, Q_BLOCK, K_SPAN), lambda h: (h, 0, 0)),
        out_shape=jax.ShapeDtypeStruct((H, Q_BLOCK, K_SPAN), F32), compiler_params=_params(("arbitrary",)))(frow)


def _bias_reduce(dbias, n_rel):
    H = dbias.shape[0]
    onehot = jnp.asarray((_bias_index()[:, None] == np.arange(n_rel)[None, :]).astype(np.float32), dtype=BF16)

    def body(d_ref, oh_ref, o_ref):
        x = jnp.concatenate([d_ref[...], jnp.zeros((Q_BLOCK, F_LEN - K_SPAN), F32)], axis=1)
        fine = _roll_rows(x, -1, 1, ROW_GROUP).reshape(Q_BLOCK // ROW_GROUP, ROW_GROUP, F_LEN)
        coarse = _roll_rows(jnp.sum(fine, axis=1), -1, ROW_GROUP, Q_BLOCK // ROW_GROUP)
        row = jnp.broadcast_to(jnp.sum(coarse, axis=0, keepdims=True), (8, F_LEN))
        acc = jnp.zeros((8, n_rel), F32)
        for _ in range(3):
            piece = row.astype(BF16)
            acc = acc + jnp.dot(piece, oh_ref[...], preferred_element_type=F32)
            row = row - piece.astype(F32)
        o_ref[...] = acc[0:1]

    return pl.pallas_call(
        body, name="bias_reduce", grid=(H,),
        in_specs=[pl.BlockSpec((None, Q_BLOCK, K_SPAN), lambda h: (h, 0, 0)),
                  pl.BlockSpec((F_LEN, n_rel), lambda h: (0, 0))],
        out_specs=pl.BlockSpec((None, 1, n_rel), lambda h: (h, 0, 0)),
        out_shape=jax.ShapeDtypeStruct((H, 1, n_rel), F32), compiler_params=_params(("arbitrary",)))(dbias, onehot)


def _attn_specs(S):
    hw = HEADS_PER_STEP * HEAD_DIM
    qspec = pl.BlockSpec((None, Q_BLOCK, hw), lambda g, b, i: (b, i, g))
    kspec = pl.BlockSpec((None, None, S, hw), lambda g, b, i: (0, b, 0, g))
    vspec = pl.BlockSpec((None, None, S, hw), lambda g, b, i: (1, b, 0, g))
    bspec = pl.BlockSpec((HEADS_PER_STEP, Q_BLOCK, K_SPAN), lambda g, b, i: (g, 0, 0))
    return hw, qspec, kspec, vspec, bspec


def _span_cases(i, fn):
    short = PAD // Q_BLOCK
    for j in range(short):
        pl.when(i == j)(functools.partial(fn, PAD - j * Q_BLOCK))
    pl.when(i >= short)(functools.partial(fn, 0))


def _key_start(i, off):
    return 0 if off else pl.multiple_of(i * Q_BLOCK - PAD, Q_BLOCK)


def _attn_exp(q_ref, k_ref, b_ref, h, k0, off):
    hs = slice(h * HEAD_DIM, (h + 1) * HEAD_DIM)
    kh = k_ref[pl.ds(k0, K_SPAN - off), hs]
    s = lax.dot_general(q_ref[:, hs], kh, (((1,), (1,)), ((), ())), preferred_element_type=F32) + b_ref[h, :, off:]
    p = jnp.exp(s - jnp.max(s, axis=-1, keepdims=True))
    return p, 1.0 / jnp.sum(p, axis=-1, keepdims=True), kh


def _attn_fwd(q, kv, bias, B, S):
    HD = q.shape[-1]
    hw, qspec, kspec, vspec, bspec = _attn_specs(S)

    def body(q_ref, k_ref, v_ref, b_ref, o_ref):
        i = pl.program_id(2)

        def block(off):
            k0 = _key_start(i, off)
            outs = []
            for h in range(HEADS_PER_STEP):
                hs = slice(h * HEAD_DIM, (h + 1) * HEAD_DIM)
                p, inv, _ = _attn_exp(q_ref, k_ref, b_ref, h, k0, off)
                outs.append(jnp.dot(p.astype(BF16), v_ref[pl.ds(k0, K_SPAN - off), hs],
                                    preferred_element_type=F32) * inv)
            o_ref[...] = jnp.concatenate(outs, axis=1).astype(BF16)

        _span_cases(i, block)

    return pl.pallas_call(
        body, name="attn_fwd", grid=(HD // hw, B, S // Q_BLOCK), in_specs=[qspec, kspec, vspec, bspec],
        out_specs=qspec, out_shape=jax.ShapeDtypeStruct((B, S, HD), BF16),
        compiler_params=_params(("arbitrary", "arbitrary", "arbitrary")))(q, kv, kv, bias)


def _attn_bwd(q, kv, bias, do, B, S):
    HD = q.shape[-1]
    H = HD // HEAD_DIM
    hw, qspec, kspec, vspec, bspec = _attn_specs(S)
    scale = HEAD_DIM ** -0.5
    nq = S // Q_BLOCK

    def body(q_ref, k_ref, v_ref, b_ref, do_ref, dq_ref, dkv_ref, db_ref, dk_acc, dv_acc):
        b, i = pl.program_id(1), pl.program_id(2)

        @pl.when(i == 0)
        def _():
            dk_acc[...] = jnp.zeros_like(dk_acc)
            dv_acc[...] = jnp.zeros_like(dv_acc)

        @pl.when((i == 0) & (b == 0))
        def _():
            db_ref[...] = jnp.zeros_like(db_ref)

        def block(off):
            k0 = _key_start(i, off)
            keys = pl.ds(k0, K_SPAN - off)
            for h in range(HEADS_PER_STEP):
                hs = slice(h * HEAD_DIM, (h + 1) * HEAD_DIM)
                p, inv, kh = _attn_exp(q_ref, k_ref, b_ref, h, k0, off)
                p = p * inv
                doh = do_ref[:, hs]
                dp = lax.dot_general(doh, v_ref[keys, hs], (((1,), (1,)), ((), ())), preferred_element_type=F32)
                ds = p * (dp - jnp.sum(p * dp, axis=-1, keepdims=True))
                db_ref[h, :, off:] += ds
                dsb = ds.astype(BF16)
                dq_ref[:, hs] = (jnp.dot(dsb, kh, preferred_element_type=F32) * scale).astype(BF16)
                dk_acc[hs, keys] += lax.dot_general(q_ref[:, hs], dsb, (((0,), (0,)), ((), ())),
                                                     preferred_element_type=F32)
                dv_acc[hs, keys] += lax.dot_general(doh, p.astype(BF16), (((0,), (0,)), ((), ())),
                                                     preferred_element_type=F32)

        _span_cases(i, block)

        @pl.when(i == nq - 1)
        def _():
            dkv_ref[0] = dk_acc[...].T.astype(BF16)
            dkv_ref[1] = dv_acc[...].T.astype(BF16)

    return pl.pallas_call(
        body, name="attn_bwd", grid=(HD // hw, B, nq), in_specs=[qspec, kspec, vspec, bspec, qspec],
        out_specs=[qspec, pl.BlockSpec((2, None, S, hw), lambda g, b, i: (0, b, 0, g)), bspec],
        out_shape=[jax.ShapeDtypeStruct((B, S, HD), BF16), jax.ShapeDtypeStruct((2, B, S, HD), BF16),
                   jax.ShapeDtypeStruct((H, Q_BLOCK, K_SPAN), F32)],
        scratch_shapes=[pltpu.VMEM((hw, S), F32), pltpu.VMEM((hw, S), F32)],
        compiler_params=_params(("arbitrary", "arbitrary", "arbitrary")))(q, kv, kv, bias, do)


def _sub_rows(R):
    for cand in (256, 352, 128, 64, 8):
        if R % cand == 0 and R > cand:
            return cand
    return R


def _adamw(w, g, m, v, *, name):
    R, C = w.shape
    tr = _sub_rows(R)

    def body(w_ref, g_ref, m_ref, v_ref, d_ref, nm_ref, nv_ref):
        g = g_ref[...]
        m = ADAM_B1 * m_ref[...] + (1.0 - ADAM_B1) * g
        v = ADAM_B2 * v_ref[...] + (1.0 - ADAM_B2) * (g * g)
        m_hat = m / (1.0 - ADAM_B1 ** ADAM_STEP)
        v_hat = v / (1.0 - ADAM_B2 ** ADAM_STEP)
        d_ref[...] = -ADAM_LR * (m_hat / (jnp.sqrt(v_hat) + ADAM_EPS) + ADAM_WD * w_ref[...])
        nm_ref[...] = m
        nv_ref[...] = v

    spec = pl.BlockSpec((tr, C), lambda i: (i, 0))
    return pl.pallas_call(body, name=name, grid=(R // tr,), in_specs=[spec] * 4, out_specs=[spec] * 3,
                          out_shape=[jax.ShapeDtypeStruct((R, C), F32)] * 3,
                          compiler_params=_params(("arbitrary",)))(w, g, m, v)


def _add_pair(units, got, core, *, name):
    n4, R, C = got.shape
    rows = n4 * R
    tr = 512 if rows % 512 == 0 else R

    def body(c_ref, u_ref, got_ref, o_ref):
        o_ref[...] = (u_ref[...].astype(F32) + got_ref[...].astype(F32)).astype(BF16)

    spec = pl.BlockSpec((tr, C), lambda i, c: (i, 0))
    grid_spec = pltpu.PrefetchScalarGridSpec(
        num_scalar_prefetch=1, grid=(rows // tr,),
        in_specs=[pl.BlockSpec((None, tr, C), lambda i, c: (c[0], i, 0)), spec], out_specs=spec)
    out = pl.pallas_call(body, name=name, grid_spec=grid_spec, out_shape=jax.ShapeDtypeStruct((rows, C), BF16),
                         compiler_params=_params(("arbitrary",)))(core.reshape(1), units.reshape(2, rows, C),
                                                                   got.reshape(rows, C))
    return out.reshape(n4, R, C)


def _sum_chips(w, own, got, pos, *, name, layer=0, into=None):
    _, R, C = own.shape
    tr = _sub_rows(R)
    nr = R // tr

    def body(p_ref, own_ref, got_ref, *rest):
        o_ref = rest[-1]
        o_ref[...] = (own_ref[...].astype(F32) + got_ref[0].astype(F32) + got_ref[1].astype(F32)
                      + got_ref[2].astype(F32))

    if w.row_sharded:
        out_map = lambda i, p: (layer, i, p[1])
    else:
        out_map = lambda i, p: (layer, p[1] * nr + i, 0)
    ins = [pos, own, got]
    in_specs = [pl.BlockSpec((None, tr, C), lambda i, p: (p[0], i, 0)),
                pl.BlockSpec((3, tr, C), lambda i, p: (0, i, 0))]
    alias = {}
    if into is not None:
        ins.append(into)
        in_specs.append(ANY)
        alias = {3: 0}
    grid_spec = pltpu.PrefetchScalarGridSpec(num_scalar_prefetch=1, grid=(nr,), in_specs=in_specs,
                                             out_specs=pl.BlockSpec((None, tr, C), out_map))
    return pl.pallas_call(body, name=name, grid_spec=grid_spec, input_output_aliases=alias,
                          out_shape=jax.ShapeDtypeStruct((w.L, w.ks, w.ns), F32),
                          compiler_params=_params(("arbitrary",)))(*ins)


def _mesh_pos():
    return lax.axis_index("x"), lax.axis_index("y"), lax.axis_index("c")


def _other_chips(x, y):
    return [(1 - x, y), (x, 1 - y), (1 - x, 1 - y)]


ANY = pl.BlockSpec(memory_space=pl.ANY)


class _W:
    def __init__(self, name, shard, row_sharded):
        self.name = name
        self.L, ks, ns = shard.shape
        self.row_sharded = row_sharded
        self.K, self.N = (ks * N_CHIPS, ns) if row_sharded else (ks, ns * N_CHIPS)
        self.ks, self.ns = ks, ns

    def shard_of(self, full, j):
        if self.row_sharded:
            return full.at[:, pl.ds(j * self.ks, self.ks), :]
        return full.at[:, :, pl.ds(j * self.ns, self.ns)]

    def half_of(self, shard, c):
        if self.row_sharded:
            return shard.at[:, :, pl.ds(c * (self.ns // 2), self.ns // 2)]
        return shard.at[:, pl.ds(c * (self.ks // 2), self.ks // 2), :]


HBM = pl.BlockSpec(memory_space=pltpu.HBM)
SEM = pl.BlockSpec(memory_space=pltpu.SEMAPHORE)
IN_FLIGHT = pltpu.SideEffectType.DATAFLOW_SIDE_EFFECTING


def _in_hbm(a):
    return pltpu.with_memory_space_constraint(a, pltpu.HBM)


def _gather_start(ws, shards, after, *, name):
    nw = len(ws)

    def body(*refs):
        src, dst = refs[:nw], refs[nw:2 * nw]
        send, recv = refs[2 * nw + 1:3 * nw + 1], refs[3 * nw + 1:4 * nw + 1]
        x, y, c = _mesh_pos()
        me = 2 * x + y
        for i, w in enumerate(ws):
            for f, (px, py) in enumerate(_other_chips(x, y)):
                pltpu.make_async_remote_copy(src_ref=w.half_of(src[i], c), dst_ref=w.half_of(w.shard_of(dst[i], me), c),
                                             send_sem=send[i].at[f], recv_sem=recv[i].at[f], device_id=(px, py, c),
                                             device_id_type=MESH).start()

    fulls = [lax.empty((w.L, w.K, w.N), BF16) for w in ws]
    out = pl.pallas_call(
        body, name=name, in_specs=[HBM] * (2 * nw) + [ANY],
        out_specs=[SEM] * (2 * nw) + [HBM] * (2 * nw),
        out_shape=[pltpu.SemaphoreType.DMA((3,))] * (2 * nw)
        + [pltpu.HBM(s.shape, BF16) for s in shards] + [pltpu.HBM(f.shape, BF16) for f in fulls],
        input_output_aliases={i: 2 * nw + i for i in range(2 * nw)},
        compiler_params=pltpu.CompilerParams(has_side_effects=IN_FLIGHT))(
            *[_in_hbm(s) for s in shards], *[_in_hbm(f) for f in fulls], after)
    return [(out[i], out[nw + i], out[2 * nw + i], out[3 * nw + i]) for i in range(nw)]


def _gather_wait(ws, flight, after, *, name):
    nw = len(ws)

    def body(*refs):
        src, dst = refs[:nw], refs[nw:2 * nw]
        send, recv = refs[2 * nw:3 * nw], refs[3 * nw:4 * nw]
        x, y, c = _mesh_pos()
        for i, w in enumerate(ws):
            for f, (px, py) in enumerate(_other_chips(x, y)):
                landed = w.half_of(w.shard_of(dst[i], 2 * px + py), c)
                cp = pltpu.make_async_remote_copy(src_ref=w.half_of(src[i], c), dst_ref=landed, send_sem=send[i].at[f],
                                                  recv_sem=recv[i].at[f], device_id=(px, py, c), device_id_type=MESH)
                cp.wait_send()
                cp.wait_recv()

    shards, fulls = [fl[2] for fl in flight], [fl[3] for fl in flight]
    out = pl.pallas_call(
        body, name=name, in_specs=[HBM] * (2 * nw) + [SEM] * (2 * nw) + [ANY],
        out_specs=[HBM] * (2 * nw),
        out_shape=[pltpu.HBM(s.shape, BF16) for s in shards] + [pltpu.HBM(f.shape, BF16) for f in fulls],
        input_output_aliases={i: i for i in range(2 * nw)},
        compiler_params=pltpu.CompilerParams(has_side_effects=IN_FLIGHT))(
            *shards, *fulls, *[fl[0] for fl in flight], *[fl[1] for fl in flight], after)
    return out[:nw], out[nw:]


def _gather_finish(ws, shards, fulls, *, name):
    nw = len(ws)

    def body(*refs):
        src, dst, stage = refs[:nw], refs[3 * nw:4 * nw], refs[4 * nw:5 * nw]
        send_sems, recv_sems, load_sems, store_sems = refs[5 * nw:]
        x, y, c = _mesh_pos()
        me = 2 * x + y
        sibling = (x, y, 1 - c)
        chips = _other_chips(x, y)

        def fwd(i, w, f, half):
            px, py = chips[f]
            landed = w.half_of(w.shard_of(dst[i], 2 * px + py), half)
            return pltpu.make_async_remote_copy(src_ref=landed, dst_ref=landed, send_sem=send_sems.at[3 * i + f],
                                                recv_sem=recv_sems.at[3 * i + f], device_id=sibling,
                                                device_id_type=MESH)

        loads = [pltpu.make_async_copy(src[i], stage[i], load_sems.at[i]) for i in range(nw)]
        for cp in loads:
            cp.start()
        sends = [fwd(i, w, f, c) for i, w in enumerate(ws) for f in range(3)]
        for cp in sends:
            cp.start()
        stores = [pltpu.make_async_copy(stage[i], w.shard_of(dst[i], me), store_sems.at[i])
                  for i, w in enumerate(ws)]
        for ld, st in zip(loads, stores):
            ld.wait()
            st.start()
        for i, w in enumerate(ws):
            for f in range(3):
                fwd(i, w, f, 1 - c).wait_recv()
        for cp in sends:
            cp.wait_send()
        for cp in stores:
            cp.wait()

    out = pl.pallas_call(
        body, name=name, in_specs=[ANY] * (2 * nw), out_specs=[ANY] * (2 * nw),
        out_shape=[jax.ShapeDtypeStruct(s.shape, BF16) for s in shards]
        + [jax.ShapeDtypeStruct(f.shape, BF16) for f in fulls],
        input_output_aliases={i: i for i in range(2 * nw)},
        scratch_shapes=[pltpu.VMEM((w.L, w.ks, w.ns), BF16) for w in ws]
        + [pltpu.SemaphoreType.DMA((3 * nw,)), pltpu.SemaphoreType.DMA((3 * nw,)), pltpu.SemaphoreType.DMA((nw,)),
           pltpu.SemaphoreType.DMA((nw,))],
        compiler_params=_params(has_side_effects=True))(*shards, *fulls)
    return out[nw:]


def _split_copies(name, srcs, lands, n_sems, copies_of, *, flight=None, after=None):
    n = len(srcs)
    starting = flight is None

    def body(*refs):
        src, land = refs[:n], refs[n:2 * n]
        sems = refs[2 * n + 1:4 * n + 1] if starting else refs[2 * n:4 * n]
        for i in range(n):
            for cp in copies_of(i, src[i], land[i], sems[i], sems[n + i]):
                if starting:
                    cp.start()
                else:
                    cp.wait_send()
                    cp.wait_recv()

    thru = [pltpu.HBM(a.shape, a.dtype) for a in list(srcs) + list(lands)]
    if starting:
        out = pl.pallas_call(
            body, name=name, in_specs=[HBM] * (2 * n) + [ANY], out_specs=[SEM] * (2 * n) + [HBM] * (2 * n),
            out_shape=[pltpu.SemaphoreType.DMA((n_sems,))] * (2 * n) + thru,
            input_output_aliases={i: 2 * n + i for i in range(2 * n)},
            compiler_params=pltpu.CompilerParams(has_side_effects=IN_FLIGHT))(
                *[_in_hbm(a) for a in srcs], *[_in_hbm(a) for a in lands], after)
        return [(out[i], out[n + i], out[2 * n + i], out[3 * n + i]) for i in range(n)]
    out = pl.pallas_call(
        body, name=name, in_specs=[HBM] * (2 * n) + [SEM] * (2 * n) + [ANY], out_specs=[HBM] * (2 * n),
        out_shape=thru, input_output_aliases={i: i for i in range(2 * n)},
        compiler_params=pltpu.CompilerParams(has_side_effects=IN_FLIGHT))(
            *srcs, *lands, *[fl[0] for fl in flight], *[fl[1] for fl in flight], after)
    return out[:n], out[n:]


def _sum8(land, vec, me):
    R = vec.shape[0]

    def body(me_ref, land_ref, vec_ref, o_ref):
        acc = jnp.zeros((R, 128), F32)
        for d in range(8):
            acc = acc + jnp.where(me_ref[0] == d, vec_ref[...], land_ref[d])
        o_ref[...] = acc

    grid_spec = pltpu.PrefetchScalarGridSpec(
        num_scalar_prefetch=1, grid=(1,),
        in_specs=[pl.BlockSpec((8, R, 128), lambda i, m: (0, 0, 0)), pl.BlockSpec((R, 128), lambda i, m: (0, 0))],
        out_specs=pl.BlockSpec((R, 128), lambda i, m: (0, 0)))
    return pl.pallas_call(body, name="sum8", grid_spec=grid_spec, out_shape=jax.ShapeDtypeStruct((R, 128), F32),
                          compiler_params=_params(("arbitrary",)))(me.reshape(1), land, vec)


def _swap_copies(i, src, got, send, recv):
    x, y, c = _mesh_pos()
    return [pltpu.make_async_remote_copy(src_ref=src.at[1 - c], dst_ref=got, send_sem=send.at[0], recv_sem=recv.at[0],
                                         device_id=(x, y, 1 - c), device_id_type=MESH)]


def _gather8_copies(i, src, land, send, recv):
    x, y, c = _mesh_pos()
    me = 4 * x + 2 * y + c
    peers = [(x, y, 1 - c)] + [(px, py, pc) for px, py in _other_chips(x, y) for pc in (c, 1 - c)]
    return [pltpu.make_async_remote_copy(src_ref=src, dst_ref=land.at[me], send_sem=send.at[k], recv_sem=recv.at[k],
                                         device_id=peer, device_id_type=MESH) for k, peer in enumerate(peers)]


def _scatter_copy(src, got, send, recv, f, chip, c):
    px, py = chip
    return pltpu.make_async_remote_copy(src_ref=src.at[2 * px + py], dst_ref=got.at[f], send_sem=send.at[f],
                                        recv_sem=recv.at[f], device_id=(px, py, c), device_id_type=MESH)


def _scatter_start(sums, *, name):
    nw = len(sums)

    def body(*refs):
        src, got = refs[:nw], refs[nw:2 * nw]
        send, recv = refs[2 * nw:3 * nw], refs[3 * nw:4 * nw]
        x, y, c = _mesh_pos()
        for i in range(nw):
            for f, chip in enumerate(_other_chips(x, y)):
                _scatter_copy(src[i], got[i], send[i], recv[i], f, chip, c).start()

    lands = [lax.empty((3,) + s.shape[1:], BF16) for s in sums]
    out = pl.pallas_call(
        body, name=name, in_specs=[HBM] * (2 * nw), out_specs=[SEM] * (2 * nw) + [HBM] * (2 * nw),
        out_shape=[pltpu.SemaphoreType.DMA((3,))] * (2 * nw)
        + [pltpu.HBM(s.shape, BF16) for s in sums] + [pltpu.HBM(l.shape, BF16) for l in lands],
        input_output_aliases={i: 2 * nw + i for i in range(2 * nw)},
        compiler_params=pltpu.CompilerParams(has_side_effects=IN_FLIGHT))(
            *[_in_hbm(s) for s in sums], *[_in_hbm(l) for l in lands])
    return [(out[i], out[nw + i], out[2 * nw + i], out[3 * nw + i]) for i in range(nw)]


def _scatter_wait(flight, after):
    nw = len(flight)

    def body(*refs):
        src, got = refs[:nw], refs[nw:2 * nw]
        send, recv = refs[2 * nw:3 * nw], refs[3 * nw:4 * nw]
        x, y, c = _mesh_pos()
        for i in range(nw):
            for f, chip in enumerate(_other_chips(x, y)):
                cp = _scatter_copy(src[i], got[i], send[i], recv[i], f, chip, c)
                cp.wait_send()
                cp.wait_recv()

    sums, lands = [fl[2] for fl in flight], [fl[3] for fl in flight]
    out = pl.pallas_call(
        body, name="scatter_wait", in_specs=[HBM] * (2 * nw) + [SEM] * (2 * nw) + [ANY], out_specs=[HBM] * (2 * nw),
        out_shape=[pltpu.HBM(s.shape, BF16) for s in sums] + [pltpu.HBM(l.shape, BF16) for l in lands],
        input_output_aliases={i: i for i in range(2 * nw)},
        compiler_params=pltpu.CompilerParams(has_side_effects=IN_FLIGHT))(
            *sums, *lands, *[fl[0] for fl in flight], *[fl[1] for fl in flight], after)
    return out[:nw], out[nw:]


def _join_halves(ws, shards):
    nw = len(ws)

    def body(*refs):
        buf = refs[nw:2 * nw]
        send_sems, recv_sems = refs[2 * nw:]
        x, y, c = _mesh_pos()
        sibling = (x, y, 1 - c)

        def copy(i, w, half):
            region = w.half_of(buf[i], half)
            return pltpu.make_async_remote_copy(src_ref=region, dst_ref=region, send_sem=send_sems.at[i],
                                                recv_sem=recv_sems.at[i], device_id=sibling, device_id_type=MESH)

        sends = [copy(i, w, c) for i, w in enumerate(ws)]
        for cp in sends:
            cp.start()
        for i, w in enumerate(ws):
            copy(i, w, 1 - c).wait_recv()
        for cp in sends:
            cp.wait_send()

    return pl.pallas_call(
        body, name="join_halves", in_specs=[ANY] * nw, out_specs=[ANY] * nw,
        out_shape=[jax.ShapeDtypeStruct((w.L, w.ks, w.ns), F32) for w in ws],
        input_output_aliases={i: i for i in range(nw)},
        scratch_shapes=[pltpu.SemaphoreType.DMA((nw,)), pltpu.SemaphoreType.DMA((nw,))],
        compiler_params=_params(has_side_effects=True))(*shards)


def _allreduce_small(vec):
    R = vec.shape[0]

    def body(x_ref, o_ref, buf, send_sems, recv_sems):
        x, y, c = _mesh_pos()
        me, sibling = (x, y, c), (x, y, 1 - c)
        chips = _other_chips(x, y)

        def slot(px, py, pc):
            return buf.at[4 * px + 2 * py + pc]

        def copy(k, block, to, src=None):
            return pltpu.make_async_remote_copy(src_ref=slot(*block) if src is None else src, dst_ref=slot(*block),
                                                send_sem=send_sems.at[k], recv_sem=recv_sems.at[k], device_id=to,
                                                device_id_type=MESH)

        first = [copy(0, me, sibling, src=x_ref)] + [copy(1 + f, me, (*chip, c), src=x_ref)
                                                     for f, chip in enumerate(chips)]
        for cp in first:
            cp.start()
        passed = [copy(4 + f, (*chip, c), sibling) for f, chip in enumerate(chips)]
        for f, chip in enumerate(chips):
            copy(1 + f, (*chip, c), me).wait_recv()
            passed[f].start()
        copy(0, sibling, me).wait_recv()
        for f, chip in enumerate(chips):
            copy(4 + f, (*chip, 1 - c), me).wait_recv()
        for cp in first + passed:
            cp.wait_send()
        slot(*me)[...] = x_ref[...]
        acc = buf[0]
        for d in range(1, 8):
            acc = acc + buf[d]
        o_ref[...] = acc

    return pl.pallas_call(
        body, name="allreduce_small", in_specs=[pl.BlockSpec(memory_space=pltpu.VMEM)],
        out_specs=pl.BlockSpec(memory_space=pltpu.VMEM), out_shape=jax.ShapeDtypeStruct((R, 128), F32),
        scratch_shapes=[pltpu.VMEM((8, R, 128), F32), pltpu.SemaphoreType.DMA((7,)), pltpu.SemaphoreType.DMA((7,))],
        compiler_params=_params())(vec)


def _pack(parts):
    flat = jnp.concatenate([p.reshape(-1).astype(F32) for p in parts])
    n = flat.shape[0]
    pad = (-n) % (64 * 128)
    return jnp.pad(flat, (0, pad)).reshape(-1, 128)


def _unpack(vec, shapes):
    flat = vec.reshape(-1)
    out, off = [], 0
    for s in shapes:
        n = int(np.prod(s))
        out.append(flat[off:off + n].reshape(s))
        off += n
    return out


def kernel(x, a_norm_g, a_w_in, a_v_norm_g, a_w_s, a_b_s, a_w_out, kv_norm_g, w_kv, b_norm_g, b_w_q, b_rel_bias, b_w_o, f_norm_g, f_w_in, f_conv_w, f_conv_b, f_w_down, final_norm_g, loss_target, m_a_norm_g, m_a_w_in, m_a_v_norm_g, m_a_w_s, m_a_b_s, m_a_w_out, m_kv_norm_g, m_w_kv, m_b_norm_g, m_b_w_q, m_b_rel_bias, m_b_w_o, m_f_norm_g, m_f_w_in, m_f_conv_w, m_f_conv_b, m_f_w_down, m_final_norm_g, v_a_norm_g, v_a_w_in, v_a_v_norm_g, v_a_w_s, v_a_b_s, v_a_w_out, v_kv_norm_g, v_w_kv, v_b_norm_g, v_b_w_q, v_b_rel_bias, v_b_w_o, v_f_norm_g, v_f_w_in, v_f_conv_w, v_f_conv_b, v_f_w_down, v_final_norm_g):
    B, S, D = x.shape
    T = B * S
    xi, yi, ci = lax.axis_index("x"), lax.axis_index("y"), lax.axis_index("c")
    j_me = (2 * xi + yi).astype(jnp.int32)
    core = ci.astype(jnp.int32)
    pos = jnp.stack([j_me, core])

    w_shards = {"a_w_in": (a_w_in, False), "a_w_out": (a_w_out, True), "w_kv": (w_kv[None], False),
                "b_w_q": (b_w_q, True), "b_w_o": (b_w_o, True), "f_w_in": (f_w_in, False), "f_w_down": (f_w_down, True)}
    names = list(w_shards)
    ws = [_W(n, w_shards[n][0], w_shards[n][1]) for n in names]
    g_shards = {"a_w_in": (a_w_in, False), "a_w_out": (a_w_out, True),
                "f_w_in0": (f_w_in[0:1], False), "f_w_down0": (f_w_down[0:1], True),
                "w_kv": (w_kv[None], False), "b_w_q": (b_w_q, True), "b_w_o": (b_w_o, True),
                "f_w_in1": (f_w_in[1:2], False), "f_w_down1": (f_w_down[1:2], True)}
    g_names = list(g_shards)
    g_ws = {n: _W(n, *g_shards[n]) for n in g_names}

    Wd = a_w_in.shape[1]
    GW = a_v_norm_g.shape[1] * N_CHIPS
    F2 = f_conv_w.shape[2] * N_CHIPS
    Fh = F2 // 2
    nsd, nsg, nsf = a_norm_g.shape[1], a_v_norm_g.shape[1], f_conv_w.shape[2]
    own = (ci == 0).astype(F32)
    place = lambda sh, width, n: lax.dynamic_update_slice_in_dim(
        jnp.zeros(sh.shape[:-1] + (width,), F32), sh * own, j_me * n, axis=sh.ndim - 1)
    def tied(x, flight):
        x, thru = lax.optimization_barrier((x, flight[0][2]))
        return x, [flight[0][:2] + (thru,) + flight[0][3:]] + flight[1:]

    first, rest = g_names[:2], g_names[2:]
    flight = dict(zip(first, _gather_start([g_ws[n] for n in first], [g_shards[n][0].astype(BF16) for n in first],
                                           a_norm_g, name="gather_start_a")))
    small_in, (flight[first[0]],) = tied(_pack([place(a_norm_g, Wd, nsd), place(a_v_norm_g, GW, nsg),
                                                place(f_conv_w, F2, nsf)]), [flight[first[0]]])
    gathered = _allreduce_small(small_in)
    a_g, a_vg, conv_w = _unpack(gathered, [(1, Wd), (1, GW), (2, 3, F2)])
    fi, fd, kv_w, qw, ow, after = lax.optimization_barrier((f_w_in, f_w_down, w_kv, b_w_q, b_w_o, gathered))
    late = {"f_w_in0": fi[0:1], "f_w_down0": fd[0:1], "w_kv": kv_w[None], "b_w_q": qw, "b_w_o": ow,
            "f_w_in1": fi[1:2], "f_w_down1": fd[1:2]}
    flight.update(zip(rest, _gather_start([g_ws[n] for n in rest], [late[n].astype(BF16) for n in rest], after,
                                          name="gather_start_rest")))
    full = {}

    def arrive(group, after, tag):
        gw = [g_ws[n] for n in group]
        sh, fu = _gather_wait(gw, [flight[n] for n in group], after, name=f"gather_wait_{tag}")
        full.update(zip(group, _gather_finish(gw, sh, fu, name=f"gather_finish_{tag}")))
    conv_w2 = conv_w.reshape(2, 3, 2, Fh).transpose(0, 2, 1, 3)
    conv_b2 = f_conv_b.reshape(2, 2, Fh)

    h0 = x.reshape(T, D)
    target = loss_target.reshape(T, D)
    bs_tile = jnp.repeat(a_b_s[0].T, GROUP_DIM, axis=1)
    ws_a = a_w_s[0]
    scale = HEAD_DIM ** -0.5
    HD = b_w_q.shape[2]
    H = HD // HEAD_DIM
    n_rel = b_rel_bias.shape[-1]
    frow, (flight["a_w_in"],) = tied(b_rel_bias[0][:, _bias_index()].reshape(H, 1, F_LEN), [flight["a_w_in"]])
    bias = _bias_expand(frow)

    def ffn_fwd(h, l, loss=None):
        yff, a, c, n = _ffn_in_conv(h, full[f"f_w_in{l}"], f_norm_g[l], conv_w2[l], conv_b2[l], S, name=f"ffn{l}_in")
        return _mm(yff, full[f"f_w_down{l}"], layer=0, res=h, loss=loss, name=f"ffn{l}_down"), (a, c, n, yff)

    arrive(["a_w_in", "a_w_out"], bias, "a")
    zp, n_a = _mm(h0, full["a_w_in"], layer=0, norm_g=a_g[0], out_dtype=BF16, emit_norm=True, name="a_in")
    out_a = _gate_fwd(zp, a_vg, ws_a, bs_tile)
    h1 = _mm(out_a, full["a_w_out"], layer=0, res=h0, name="a_out")
    arrive(["f_w_in0", "f_w_down0"], h1, "f0")
    h2, saved0 = ffn_fwd(h1, 0)
    arrive(["w_kv", "b_w_q", "b_w_o"], h2, "b")
    arrive(["f_w_in1", "f_w_down1"], h2, "f1")
    kv, n_kv = _mm(h2, full["w_kv"], layer=0, norm_g=kv_norm_g, out_dtype=BF16, split_out=True, emit_norm=True,
                   name="kv")
    q, n_q = _mm(h2, full["b_w_q"], layer=0, norm_g=b_norm_g[0], scale=scale, out_dtype=BF16, emit_norm=True,
                 name="q")
    kv4, q3 = kv.reshape(2, B, S, HD), q.reshape(B, S, HD)
    o = _attn_fwd(q3, kv4, bias, B, S).reshape(T, HD)
    h3 = _mm(o, full["b_w_o"], layer=0, res=h2, name="attn_out")
    (dh, loss8, dg_final), saved1 = ffn_fwd(h3, 1, loss=(final_norm_g, target))

    units = {}

    in_flight = {}

    def swap_start(group, tag, carry):
        us = [units[n] for n in group]
        lands = [lax.empty(u.shape[1:], BF16) for u in us]
        carry, flight = tied(carry, _split_copies(f"swap_start_{tag}", us, lands, 1, _swap_copies, after=carry))
        return (group, tag, flight), carry

    def reduce_start(swap, after):
        group, tag, flight = swap
        us, got = _split_copies(f"swap_wait_{tag}", [fl[2] for fl in flight], [fl[3] for fl in flight], 1,
                                _swap_copies, flight=flight, after=after)
        sums = [_add_pair(u, g_, core, name=f"pair_{n}") for n, u, g_ in zip(group, us, got)]
        after, flight = tied(after, _scatter_start(sums, name=f"scatter_start_{tag}"))
        in_flight.update(zip(group, flight))
        return after

    def ffn_bwd(dh, h, saved, l, early):
        a, c, n, yff = saved
        units[f"f_w_down{l}"] = _mm_tn(yff, dh, rows_are_shards=True, name=f"ffn{l}_down_dw")
        dh_in = dh
        if early:
            sw, dh_in = swap_start([f"f_w_down{l}"], f"fd{l}", dh)
        dyff = _mm(dh_in, full[f"f_w_down{l}"], layer=0, trans_w=True, out_dtype=BF16, name=f"ffn{l}_down_dx")
        if early:
            dyff = reduce_start(sw, dyff)
        da, dcw, dcb = _conv_bwd(a, c, dyff, conv_w2[l], S)
        units[f"f_w_in{l}"] = _mm_tn(n, da, split_y=True, name=f"ffn{l}_in_dw")
        sw, da = swap_start([f"f_w_in{l}"] if early else [f"f_w_down{l}", f"f_w_in{l}"], f"f{l}", da)
        dh, dg = _mm(da, full[f"f_w_in{l}"], layer=0, trans_w=True, split_x=True, bwd=(h, f_norm_g[l], dh),
                     name=f"ffn{l}_in_dx")
        return reduce_start(sw, dh), dg, dcw, dcb

    dh, dg_f1, dcw1, dcb1 = ffn_bwd(dh, h3, saved1, 1, False)
    do = _mm(dh, full["b_w_o"], layer=0, trans_w=True, out_dtype=BF16, name="attn_out_dx")
    units["b_w_o"] = _mm_tn(o, dh, rows_are_shards=True, name="b_w_o_dw")
    dq, dkv, dbias = _attn_bwd(q3, kv4, bias, do.reshape(B, S, HD), B, S)
    dq, d_rel = lax.optimization_barrier((dq, _bias_reduce(dbias, n_rel)))
    d_rel = d_rel.reshape(1, H, n_rel)
    dq, dkv = dq.reshape(T, HD), dkv.reshape(2, T, HD)
    units["b_w_q"] = _mm_tn(n_q, dq, rows_are_shards=True, name="b_w_q_dw")
    dh, dg_b = _mm(dq, full["b_w_q"], layer=0, trans_w=True, bwd=(h2, b_norm_g[0], dh), name="q_dx")
    units["w_kv"] = _mm_tn(n_kv, dkv, split_y=True, name="w_kv_dw")
    sw, dkv = swap_start(["b_w_o", "b_w_q", "w_kv"], "b", dkv)
    dh, dg_kv = _mm(dkv, full["w_kv"], layer=0, trans_w=True, split_x=True, bwd=(h2, kv_norm_g, dh), name="kv_dx")
    dh = reduce_start(sw, dh)
    dh, dg_f0, dcw0, dcb0 = ffn_bwd(dh, h1, saved0, 0, True)
    units["a_w_out"] = _mm_tn(out_a, dh, rows_are_shards=True, name="a_w_out_dw")
    sw, dh_in = swap_start(["a_w_out"], "ao", dh)
    d_out = _mm(dh_in, full["a_w_out"], layer=0, trans_w=True, out_dtype=BF16, name="a_out_dx")
    d_out = reduce_start(sw, d_out)
    dzp, dws, dbs, dgv = _gate_bwd(zp, d_out, a_vg, ws_a, bs_tile)
    units["a_w_in"] = _mm_tn(n_a, dzp, name="a_w_in_dw")
    sw, dzp_in = swap_start(["a_w_in"], "ai", dzp)
    grad_x, dg_a = _mm(dzp_in, full["a_w_in"], layer=0, trans_w=True, bwd=(h0, a_g[0], dh), name="a_in_dx")
    grad_x = reduce_start(sw, grad_x)

    to_flat = lambda d: d.transpose(1, 0, 2).reshape(3, F2)
    small = {"a_norm_g": dg_a, "a_v_norm_g": dgv, "a_w_s": dws[None], "a_b_s": dbs[None], "kv_norm_g": dg_kv[0],
             "b_norm_g": dg_b, "b_rel_bias": d_rel, "f_norm_g": jnp.concatenate([dg_f0, dg_f1], axis=0),
             "f_conv_w": jnp.stack([to_flat(dcw0), to_flat(dcw1)]),
             "f_conv_b": jnp.stack([dcb0.reshape(F2), dcb1.reshape(F2)]), "final_norm_g": dg_final[0]}
    snames = list(small)
    small_vec = _pack([small[n] for n in snames] + [loss8[0:1, 0:1]])
    grad_x, small_flight = tied(grad_x, _split_copies("small_start", [small_vec],
                                                      [lax.empty((8,) + small_vec.shape, F32)], 7, _gather8_copies,
                                                      after=grad_x))

    sums, recv = _scatter_wait([in_flight[n] for n in g_names], grad_x)
    sums, recv = dict(zip(g_names, sums)), dict(zip(g_names, recv))
    halves = []
    for n, w in zip(names, ws):
        if w.L == 1:
            halves.append(_sum_chips(w, sums[n], recv[n], pos, name=f"chips_{n}"))
        else:
            first = _sum_chips(w, sums[n + "0"], recv[n + "0"], pos, name=f"chips_{n}0")
            halves.append(_sum_chips(w, sums[n + "1"], recv[n + "1"], pos, layer=1, into=first, name=f"chips_{n}1"))
    g_big = dict(zip(names, _join_halves(ws, halves)))
    g_big["w_kv"] = g_big["w_kv"][0]

    given = dict(a_norm_g=(a_norm_g, m_a_norm_g, v_a_norm_g), a_w_in=(a_w_in, m_a_w_in, v_a_w_in),
                 a_v_norm_g=(a_v_norm_g, m_a_v_norm_g, v_a_v_norm_g), a_w_s=(a_w_s, m_a_w_s, v_a_w_s),
                 a_b_s=(a_b_s, m_a_b_s, v_a_b_s), a_w_out=(a_w_out, m_a_w_out, v_a_w_out),
                 kv_norm_g=(kv_norm_g, m_kv_norm_g, v_kv_norm_g), w_kv=(w_kv, m_w_kv, v_w_kv),
                 b_norm_g=(b_norm_g, m_b_norm_g, v_b_norm_g), b_w_q=(b_w_q, m_b_w_q, v_b_w_q),
                 b_rel_bias=(b_rel_bias, m_b_rel_bias, v_b_rel_bias), b_w_o=(b_w_o, m_b_w_o, v_b_w_o),
                 f_norm_g=(f_norm_g, m_f_norm_g, v_f_norm_g), f_w_in=(f_w_in, m_f_w_in, v_f_w_in),
                 f_conv_w=(f_conv_w, m_f_conv_w, v_f_conv_w), f_conv_b=(f_conv_b, m_f_conv_b, v_f_conv_b),
                 f_w_down=(f_w_down, m_f_w_down, v_f_w_down), final_norm_g=(final_norm_g, m_final_norm_g, v_final_norm_g))
    order = list(given)
    grads, deltas, new_m, new_v = {}, {}, {}, {}
    for n in names:
        w_, m_, v_ = given[n]
        g_ = g_big[n]
        C = w_.shape[-1]
        d2, m2, v2 = _adamw(w_.reshape(-1, C), g_.reshape(-1, C), m_.reshape(-1, C), v_.reshape(-1, C),
                            name=f"adamw_{n}")
        grads[n], deltas[n], new_m[n], new_v[n] = g_.reshape(w_.shape), d2.reshape(w_.shape), m2.reshape(w_.shape), \
            v2.reshape(w_.shape)
    vecs, lands = _split_copies("small_wait", [small_flight[0][2]], [small_flight[0][3]], 7, _gather8_copies,
                                flight=small_flight, after=deltas[names[-1]])
    red = _sum8(lands[0], vecs[0], (4 * xi + 2 * yi + ci).astype(jnp.int32))
    parts = _unpack(red, [small[n].shape for n in snames] + [(1,)])
    g_small = dict(zip(snames, parts[:-1]))
    loss = parts[-1][0]
    g_small["a_norm_g"] = lax.dynamic_slice_in_dim(g_small["a_norm_g"], j_me * nsd, nsd, axis=1)
    g_small["a_v_norm_g"] = lax.dynamic_slice_in_dim(g_small["a_v_norm_g"], j_me * nsg, nsg, axis=1)
    g_small["f_conv_w"] = lax.dynamic_slice_in_dim(g_small["f_conv_w"], j_me * nsf, nsf, axis=2)

    sm = [n for n in order if n not in names]
    d2, m2, v2 = _adamw(_pack([given[n][0] for n in sm]), _pack([g_small[n].reshape(given[n][0].shape) for n in sm]),
                        _pack([given[n][1] for n in sm]), _pack([given[n][2] for n in sm]), name="adamw_small")
    shapes = [given[n][0].shape for n in sm]
    for n, d_, m_, v_ in zip(sm, _unpack(d2, shapes), _unpack(m2, shapes), _unpack(v2, shapes)):
        grads[n], deltas[n], new_m[n], new_v[n] = g_small[n].reshape(given[n][0].shape), d_, m_, v_

    return (loss, grad_x.reshape(B, S, D), *[grads[n] for n in order], *[deltas[n] for n in order],
            *[new_m[n] for n in order], *[new_v[n] for n in order])
```

```python
import functools
import math

import numpy as np
import jax
import jax.numpy as jnp
from jax import lax
from jax.experimental import pallas as pl
from jax.experimental.pallas import tpu as pltpu

F32 = jnp.float32
BF16 = jnp.bfloat16
MESH = pl.DeviceIdType.MESH

EPS = 1e-6
NEG_INF = -1e30
CHUNK = 64
GMLP_BLOCK = 128
GROUP_DIM = 128
HEAD_DIM = 64
LEFT_CHUNKS = 8
PAD = LEFT_CHUNKS * CHUNK
REL_CLIP = 128
Q_BLOCK = 256
K_SPAN = PAD + Q_BLOCK
F_LEN = K_SPAN + Q_BLOCK
HEADS_PER_STEP = 4
N_CHIPS = 4

ADAM_LR = 0.001
ADAM_B1 = 0.9
ADAM_B2 = 0.999
ADAM_EPS = 1e-08
ADAM_WD = 0.01
ADAM_STEP = 10

VMEM_LIMIT = 56 * 1024 * 1024


def _params(sem=None, **kw):
    if sem is not None:
        kw["dimension_semantics"] = sem
    return pltpu.CompilerParams(vmem_limit_bytes=VMEM_LIMIT, **kw)


def _rms(xf):
    r = lax.rsqrt(jnp.mean(xf * xf, axis=-1, keepdims=True) + EPS)
    return xf * r, r


def _gelu(x, with_grad=False):
    c = math.sqrt(2.0 / math.pi)
    x2 = x * x
    t = jnp.tanh(c * x * (1.0 + 0.044715 * x2))
    half = 0.5 * (1.0 + t)
    if not with_grad:
        return x * half
    return x * half, half + 0.5 * x * (1.0 - t * t) * c * (1.0 + 3.0 * 0.044715 * x2)


def _col_tile(n):
    if n <= 1024:
        return n
    for t in (1408, 1024, 512):
        if n % t == 0:
            return t
    raise ValueError(n)


def _row_tile(t, want):
    while t % want:
        want //= 2
    return want


def _mm(x, w, *, name, layer=None, trans_w=False, norm_g=None, res=None, scale=None, out_dtype=F32, bwd=None,
        split_out=False, split_x=False, emit_norm=False, loss=None, tm=512):
    T = x.shape[-2]
    K = 2 * x.shape[-1] if split_x else x.shape[-1]
    N = w.shape[-2] if trans_w else w.shape[-1]
    tn = N
    tm = _row_tile(T, 2 * tm if max(K, N) <= 2048 else tm)
    nn, nm = N // tn, T // tm
    has_norm, has_res, has_bwd, has_loss = norm_g is not None, res is not None, bwd is not None, loss is not None
    dims = (((1,), (1,)), ((), ())) if trans_w else (((1,), (0,)), ((), ()))

    def body(*refs):
        it = iter(refs)
        x_ref, w_ref = next(it), next(it)
        g_ref = next(it) if has_norm else None
        res_ref = next(it) if has_res else None
        if has_bwd:
            h_ref, bg_ref, dh_ref = next(it), next(it), next(it)
        if has_loss:
            lg_ref, t_ref = next(it), next(it)
        o_ref = next(it)
        if split_x:
            kh = K // 2
            acc = lax.dot_general(x_ref[0].astype(BF16), w_ref[:, :kh] if trans_w else w_ref[:kh, :], dims,
                                  preferred_element_type=F32)
            acc = acc + lax.dot_general(x_ref[1].astype(BF16), w_ref[:, kh:] if trans_w else w_ref[kh:, :], dims,
                                        preferred_element_type=F32)
        else:
            xv = x_ref[...]
            if has_norm:
                xv = _rms(xv.astype(F32))[0] * g_ref[...]
            xb = xv.astype(BF16)
            if emit_norm:
                refs[-1][...] = xb
            acc = lax.dot_general(xb, w_ref[...], dims, preferred_element_type=F32)
        if scale is not None:
            acc = acc * scale
        if has_res:
            acc = acc + res_ref[...]
        if has_bwd:
            dg_ref = next(it)
            n, r = _rms(h_ref[...])

            @pl.when(pl.program_id(1) == 0)
            def _():
                dg_ref[...] = jnp.zeros_like(dg_ref)

            dg_ref[...] += jnp.sum(acc * n, axis=0, keepdims=True)
            t = acc * bg_ref[...]
            o_ref[...] = dh_ref[...] + r * (t - n * jnp.mean(t * n, axis=-1, keepdims=True))
        elif has_loss:
            loss_ref, dg_ref = refs[-2], refs[-1]

            @pl.when(pl.program_id(1) == 0)
            def _():
                loss_ref[...] = jnp.zeros_like(loss_ref)
                dg_ref[...] = jnp.zeros_like(dg_ref)

            n, r = _rms(acc)
            g = lg_ref[...]
            e = n * g - t_ref[...]
            loss_ref[...] += 0.5 * jnp.sum(jnp.mean(e * e, axis=-1, keepdims=True), axis=0, keepdims=True)
            dy = e * (1.0 / N)
            dg_ref[...] += jnp.sum(dy * n, axis=0, keepdims=True)
            t = dy * g
            o_ref[...] = r * (t - n * jnp.mean(t * n, axis=-1, keepdims=True))
        elif split_out:
            o_ref[0] = acc[:, :N // 2].astype(out_dtype)
            o_ref[1] = acc[:, N // 2:].astype(out_dtype)
        else:
            o_ref[...] = acc.astype(out_dtype)

    lead = () if layer is None else (None,)
    lidx = () if layer is None else (layer,)
    ins = [x, w]
    xspec = (pl.BlockSpec((2, tm, K // 2), lambda n, m: (0, m, 0)) if split_x
             else pl.BlockSpec((tm, K), lambda n, m: (m, 0)))
    once = pl.Buffered(1)
    wspec = (pl.BlockSpec(lead + (tn, K), lambda n, m: lidx + (n, 0), pipeline_mode=once) if trans_w
             else pl.BlockSpec(lead + (K, tn), lambda n, m: lidx + (0, n), pipeline_mode=once))
    in_specs = [xspec, wspec]
    if has_norm:
        ins.append(norm_g.reshape(1, K))
        in_specs.append(pl.BlockSpec((1, K), lambda n, m: (0, 0)))
    if has_res:
        ins.append(res)
        in_specs.append(pl.BlockSpec((tm, tn), lambda n, m: (m, n)))
    if split_out:
        out_shape = [jax.ShapeDtypeStruct((2, T, N // 2), out_dtype)]
        out_specs = [pl.BlockSpec((2, tm, N // 2), lambda n, m: (0, m, 0))]
    else:
        out_shape = [jax.ShapeDtypeStruct((T, N), F32 if has_bwd else out_dtype)]
        out_specs = [pl.BlockSpec((tm, tn), lambda n, m: (m, n))]
    if has_bwd:
        h, g, dh = bwd
        ins += [h, g.reshape(1, N), dh]
        in_specs += [pl.BlockSpec((tm, N), lambda n, m: (m, 0)), pl.BlockSpec((1, N), lambda n, m: (0, 0)),
                     pl.BlockSpec((tm, N), lambda n, m: (m, 0))]
        out_shape.append(jax.ShapeDtypeStruct((1, N), F32))
        out_specs.append(pl.BlockSpec((1, N), lambda n, m: (0, 0)))
    if emit_norm:
        out_shape.append(jax.ShapeDtypeStruct((T, K), BF16))
        out_specs.append(pl.BlockSpec((tm, K), lambda n, m: (m, 0)))
    if has_loss:
        ins += [loss[0].reshape(1, N), loss[1]]
        in_specs += [pl.BlockSpec((1, N), lambda n, m: (0, 0)), pl.BlockSpec((tm, N), lambda n, m: (m, 0))]
        out_shape += [jax.ShapeDtypeStruct((8, 128), F32), jax.ShapeDtypeStruct((1, N), F32)]
        out_specs += [pl.BlockSpec((8, 128), lambda n, m: (0, 0)), pl.BlockSpec((1, N), lambda n, m: (0, 0))]
    out = pl.pallas_call(body, name=name, grid=(nn, nm), in_specs=in_specs, out_specs=out_specs, out_shape=out_shape,
                         compiler_params=_params(("arbitrary", "arbitrary")))(*ins)
    return out if has_bwd or emit_norm or has_loss else out[0]


def _mm_tn(x, dy, *, name, rows_are_shards=False, split_y=False, tt=1024):
    T, K = x.shape
    N = 2 * dy.shape[-1] if split_y else dy.shape[-1]
    R, C = (K // N_CHIPS, N // 2) if rows_are_shards else (K // 2, N // N_CHIPS)
    nn = 2 if split_y else 1
    tn = N // nn
    per = N_CHIPS // nn
    assert not (rows_are_shards and split_y)
    tt = _row_tile(T, tt)
    nt = T // tt

    def body(x_ref, y_ref, o_ref, acc_ref):
        t = pl.program_id(1)

        @pl.when(t == 0)
        def _():
            acc_ref[...] = jnp.zeros_like(acc_ref)

        acc_ref[...] += lax.dot_general(x_ref[...], y_ref[...].astype(BF16), (((0,), (0,)), ((), ())),
                                        preferred_element_type=F32)

        @pl.when(t == nt - 1)
        def _():
            if rows_are_shards:
                for h in range(2):
                    o_ref[h] = acc_ref[:, h * C:(h + 1) * C].astype(BF16).reshape(N_CHIPS, R, C)
            else:
                for j in range(per):
                    o_ref[:, j] = acc_ref[:, j * C:(j + 1) * C].astype(BF16).reshape(2, R, C)

    if split_y:
        yspec = pl.BlockSpec((None, tt, tn), lambda n, t: (n, t, 0))
    else:
        yspec = pl.BlockSpec((tt, tn), lambda n, t: (t, 0))
    if rows_are_shards:
        out_spec = pl.BlockSpec((2, N_CHIPS, R, C), lambda n, t: (0, 0, 0, 0))
    else:
        out_spec = pl.BlockSpec((2, per, R, C), lambda n, t: (0, n, 0, 0))
    return pl.pallas_call(body, name=name, grid=(nn, nt),
                          in_specs=[pl.BlockSpec((tt, K), lambda n, t: (t, 0)), yspec], out_specs=out_spec,
                          out_shape=jax.ShapeDtypeStruct((2, N_CHIPS, R, C), BF16),
                          scratch_shapes=[pltpu.VMEM((K, tn), F32)],
                          compiler_params=_params(("arbitrary", "arbitrary")))(x, dy)


def _chunk_mask():
    i = lax.broadcasted_iota(jnp.int32, (GMLP_BLOCK, GMLP_BLOCK), 0) // CHUNK
    j = lax.broadcasted_iota(jnp.int32, (GMLP_BLOCK, GMLP_BLOCK), 1) // CHUNK
    return i >= j


def _gate_fwd(zp, gv, ws, bs_tile, *, tm=256):
    T, W2 = zp.shape
    W = W2 // 2
    G = W // GROUP_DIM
    tm = _row_tile(T, tm)

    def body(zp_ref, gv_ref, ws_ref, bs_ref, o_ref):
        z = _gelu(zp_ref[...].astype(F32))
        u, v = z[:, :W], z[:, W:]
        vn = _rms(v)[0] * gv_ref[...]
        mask = _chunk_mask()
        for g in range(G):
            cs = slice(g * GROUP_DIM, (g + 1) * GROUP_DIM)
            wg = jnp.where(mask, ws_ref[g], 0.0).astype(BF16)
            for b in range(tm // GMLP_BLOCK):
                rs = slice(b * GMLP_BLOCK, (b + 1) * GMLP_BLOCK)
                s = jnp.dot(wg, vn[rs, cs].astype(BF16), preferred_element_type=F32) + bs_ref[:, cs]
                o_ref[rs, cs] = (u[rs, cs] * s).astype(BF16)

    return pl.pallas_call(
        body, name="gate_fwd", grid=(T // tm,),
        in_specs=[pl.BlockSpec((tm, W2), lambda i: (i, 0)), pl.BlockSpec((1, W), lambda i: (0, 0)),
                  pl.BlockSpec((G, GMLP_BLOCK, GMLP_BLOCK), lambda i: (0, 0, 0)),
                  pl.BlockSpec((GMLP_BLOCK, W), lambda i: (0, 0))],
        out_specs=pl.BlockSpec((tm, W), lambda i: (i, 0)), out_shape=jax.ShapeDtypeStruct((T, W), BF16),
        compiler_params=_params(("arbitrary",)))(zp, gv, ws, bs_tile)


def _gate_bwd(zp, d_out, gv, ws, bs_tile, *, tm=256):
    T, W2 = zp.shape
    W = W2 // 2
    G = W // GROUP_DIM
    tm = _row_tile(T, tm)
    nm = T // tm

    def body(zp_ref, do_ref, gv_ref, ws_ref, bs_ref, dzp_ref, dws_ref, dbs_ref, dgv_ref, du_scr, dvn_scr, dsum_scr):
        i = pl.program_id(0)

        @pl.when(i == 0)
        def _():
            dws_ref[...] = jnp.zeros_like(dws_ref)
            dgv_ref[...] = jnp.zeros_like(dgv_ref)
            dsum_scr[...] = jnp.zeros_like(dsum_scr)

        zp = zp_ref[...].astype(F32)
        z, dz = _gelu(zp, with_grad=True)
        u, v = z[:, :W], z[:, W:]
        n, r = _rms(v)
        gv = gv_ref[...]
        vn = n * gv
        d_out = do_ref[...].astype(F32)
        mask = _chunk_mask()
        for g in range(G):
            cs = slice(g * GROUP_DIM, (g + 1) * GROUP_DIM)
            wg = jnp.where(mask, ws_ref[g], 0.0).astype(BF16)
            dw = jnp.zeros((GMLP_BLOCK, GMLP_BLOCK), F32)
            for b in range(tm // GMLP_BLOCK):
                rs = slice(b * GMLP_BLOCK, (b + 1) * GMLP_BLOCK)
                vb = vn[rs, cs].astype(BF16)
                s = jnp.dot(wg, vb, preferred_element_type=F32) + bs_ref[:, cs]
                du_scr[rs, cs] = d_out[rs, cs] * s
                ds = d_out[rs, cs] * u[rs, cs]
                dsb = ds.astype(BF16)
                dvn_scr[rs, cs] = lax.dot_general(wg, dsb, (((0,), (0,)), ((), ())), preferred_element_type=F32)
                dw = dw + lax.dot_general(dsb, vb, (((1,), (1,)), ((), ())), preferred_element_type=F32)
                dsum_scr[:, cs] += ds
            dws_ref[g] += jnp.where(mask, dw, 0.0)
        dvn = dvn_scr[...]
        dgv_ref[...] += jnp.sum(dvn * n, axis=0, keepdims=True)
        t = dvn * gv
        dv = r * (t - n * jnp.mean(t * n, axis=-1, keepdims=True))
        dzp_ref[:, :W] = (du_scr[...] * dz[:, :W]).astype(BF16)
        dzp_ref[:, W:] = (dv * dz[:, W:]).astype(BF16)

        @pl.when(i == nm - 1)
        def _():
            sel = (lax.broadcasted_iota(jnp.int32, (G, W), 1) // GROUP_DIM
                   == lax.broadcasted_iota(jnp.int32, (G, W), 0)).astype(F32)
            dbs_ref[...] = lax.dot_general(sel, dsum_scr[...], (((1,), (1,)), ((), ())),
                                           precision=lax.Precision.HIGHEST, preferred_element_type=F32)

    return pl.pallas_call(
        body, name="gate_bwd", grid=(nm,),
        in_specs=[pl.BlockSpec((tm, W2), lambda i: (i, 0)), pl.BlockSpec((tm, W), lambda i: (i, 0)),
                  pl.BlockSpec((1, W), lambda i: (0, 0)),
                  pl.BlockSpec((G, GMLP_BLOCK, GMLP_BLOCK), lambda i: (0, 0, 0)),
                  pl.BlockSpec((GMLP_BLOCK, W), lambda i: (0, 0))],
        out_specs=[pl.BlockSpec((tm, W2), lambda i: (i, 0)),
                   pl.BlockSpec((G, GMLP_BLOCK, GMLP_BLOCK), lambda i: (0, 0, 0)),
                   pl.BlockSpec((G, GMLP_BLOCK), lambda i: (0, 0)), pl.BlockSpec((1, W), lambda i: (0, 0))],
        out_shape=[jax.ShapeDtypeStruct((T, W2), BF16), jax.ShapeDtypeStruct((G, GMLP_BLOCK, GMLP_BLOCK), F32),
                   jax.ShapeDtypeStruct((G, GMLP_BLOCK), F32), jax.ShapeDtypeStruct((1, W), F32)],
        scratch_shapes=[pltpu.VMEM((tm, W), F32), pltpu.VMEM((tm, W), F32), pltpu.VMEM((GMLP_BLOCK, W), F32)],
        compiler_params=_params(("arbitrary",)))(zp, d_out, gv, ws, bs_tile)


LANES = 128
HALO = 16


def _taps(ext, w, b):
    return w[2:3] * ext[HALO:] + w[1:2] * pltpu.roll(ext, 1, 0)[HALO:] + w[0:1] * pltpu.roll(ext, 2, 0)[HALO:] + b


def _ffn_in_conv(h, w, g, cw, cb, S, *, name, tm=256):
    T, D = h.shape
    F = w.shape[-1] // 2
    tc = _col_tile(F)
    tm = _row_tile(S, tm)

    def body(h_ref, w_ref, g_ref, cw_ref, cb_ref, y_ref, a_ref, c_ref, n_ref, tail):
        first = (pl.program_id(0) * tm) % S == 0
        nb = (_rms(h_ref[...])[0] * g_ref[...]).astype(BF16)
        n_ref[...] = nb
        for j in range(F // tc):
            cs = slice(j * tc, (j + 1) * tc)
            conv = []
            for s in range(2):
                acc = jnp.dot(nb, w_ref[:, s * F + j * tc:s * F + (j + 1) * tc], preferred_element_type=F32)
                ab = acc.astype(BF16)
                a_ref[s, :, cs] = ab
                af = ab.astype(F32)
                ext = jnp.concatenate([jnp.where(first, 0.0, tail[s, :, cs]), af], axis=0)
                tail[s, :, cs] = af[tm - HALO:, :]
                cv = _taps(ext, cw_ref[s, :, cs], cb_ref[s:s + 1, cs]).astype(BF16)
                c_ref[s, :, cs] = cv
                conv.append(cv.astype(F32))
            up, gate = conv
            y_ref[:, cs] = (gate * jax.nn.sigmoid(gate) * up).astype(BF16)

    row = lambda width: pl.BlockSpec((tm, width), lambda i: (i, 0))
    wide = pl.BlockSpec((2, tm, F), lambda i: (0, i, 0))
    return pl.pallas_call(
        body, name=name, grid=(T // tm,),
        in_specs=[row(D), pl.BlockSpec((None, D, 2 * F), lambda i: (0, 0, 0), pipeline_mode=pl.Buffered(1)),
                  pl.BlockSpec((1, D), lambda i: (0, 0)),
                  pl.BlockSpec((2, 3, F), lambda i: (0, 0, 0)), pl.BlockSpec((2, F), lambda i: (0, 0))],
        out_specs=[row(F), wide, wide, row(D)],
        out_shape=[jax.ShapeDtypeStruct((T, F), BF16), jax.ShapeDtypeStruct((2, T, F), BF16),
                   jax.ShapeDtypeStruct((2, T, F), BF16), jax.ShapeDtypeStruct((T, D), BF16)],
        scratch_shapes=[pltpu.VMEM((2, HALO, F), F32)],
        compiler_params=_params(("arbitrary",)))(h, w, g.reshape(1, D), cw, cb)


def _conv_bwd(a, c, dy, cw, S, *, tm=256):
    _, T, F = a.shape
    tc = _col_tile(F)
    tm = _row_tile(S, tm)
    nm = T // tm
    hb = tm // HALO
    TE = tm + HALO
    nxt = lambda j, i: jnp.minimum((i + 1) * hb, T // HALO - 1)

    def body(a_ref, c_ref, nc_ref, dy_ref, ndy_ref, w_ref, da_ref, dw_ref, db_ref):
        i = pl.program_id(1)
        last = ((i + 1) * tm) % S == 0
        keep_n = jnp.where(last, 0.0, 1.0)

        @pl.when(i == 0)
        def _():
            dw_ref[...] = jnp.zeros_like(dw_ref)
            db_ref[...] = jnp.zeros_like(db_ref)

        for j in range(tc // LANES):
            cs = slice(j * LANES, (j + 1) * LANES)
            dyf = jnp.concatenate([dy_ref[:, cs].astype(F32), ndy_ref[:, cs].astype(F32) * keep_n], axis=0)
            up = jnp.concatenate([c_ref[0, :, cs].astype(F32), nc_ref[0, :, cs].astype(F32)], axis=0)
            gate = jnp.concatenate([c_ref[1, :, cs].astype(F32), nc_ref[1, :, cs].astype(F32)], axis=0)
            sg = jax.nn.sigmoid(gate)
            for s, d in ((0, dyf * (gate * sg)), (1, dyf * up * (sg * (1.0 + gate * (1.0 - sg))))):
                a = a_ref[s, :, cs].astype(F32)
                w = w_ref[s, :, cs]
                u1, u2 = pltpu.roll(d, TE - 1, 0), pltpu.roll(d, TE - 2, 0)
                db_ref[s:s + 1, cs] += jnp.sum(d[:tm], axis=0, keepdims=True)
                dw_ref[s, 2:3, cs] += jnp.sum(d[:tm] * a, axis=0, keepdims=True)
                dw_ref[s, 1:2, cs] += jnp.sum(u1[:tm] * a, axis=0, keepdims=True)
                dw_ref[s, 0:1, cs] += jnp.sum(u2[:tm] * a, axis=0, keepdims=True)
                da_ref[s, :, cs] = (w[2:3] * d + w[1:2] * u1 + w[0:1] * u2)[:tm].astype(BF16)

    cur = pl.BlockSpec((2, tm, tc), lambda j, i: (0, i, j))
    return pl.pallas_call(
        body, name="conv_bwd", grid=(F // tc, nm),
        in_specs=[cur, cur, pl.BlockSpec((2, HALO, tc), lambda j, i: (0, nxt(j, i), j)),
                  pl.BlockSpec((tm, tc), lambda j, i: (i, j)), pl.BlockSpec((HALO, tc), lambda j, i: (nxt(j, i), j)),
                  pl.BlockSpec((2, 3, tc), lambda j, i: (0, 0, j))],
        out_specs=[cur, pl.BlockSpec((2, 3, tc), lambda j, i: (0, 0, j)), pl.BlockSpec((2, tc), lambda j, i: (0, j))],
        out_shape=[jax.ShapeDtypeStruct((2, T, F), BF16), jax.ShapeDtypeStruct((2, 3, F), F32),
                   jax.ShapeDtypeStruct((2, F), F32)],
        compiler_params=_params(("arbitrary", "arbitrary")))(a, c, c, dy, dy, cw)


def _bias_index():
    idx = np.arange(F_LEN)
    d = np.where(idx < K_SPAN, idx, idx - F_LEN)
    return np.clip(PAD - d, -REL_CLIP, REL_CLIP) + REL_CLIP


ROW_GROUP = 16


def _roll_rows(x, sign, unit, steps):
    rows = lax.broadcasted_iota(jnp.int32, x.shape, 0)
    step = 1
    while step < steps:
        shift = unit * step if sign > 0 else F_LEN - unit * step
        x = jnp.where((rows & step) != 0, pltpu.roll(x, shift, 1), x)
        step *= 2
    return x


def _bias_expand(frow):
    H = frow.shape[0]
    groups = Q_BLOCK // ROW_GROUP

    def body(f_ref, o_ref):
        coarse = _roll_rows(jnp.broadcast_to(f_ref[...], (groups, F_LEN)), 1, ROW_GROUP, groups)
        x = jnp.concatenate([jnp.broadcast_to(coarse[a:a + 1], (ROW_GROUP, F_LEN)) for a in range(groups)], axis=0)
        x = _roll_rows(x, 1, 1, ROW_GROUP)[:, :K_SPAN]
        qc = lax.broadcasted_iota(jnp.int32, (Q_BLOCK, K_SPAN), 0) // CHUNK * CHUNK
        kj = lax.broadcasted_iota(jnp.int32, (Q_BLOCK, K_SPAN), 1)
        o_ref[...] = jnp.where((kj >= qc) & (kj < qc + PAD + CHUNK), x, NEG_INF)

    return pl.pallas_call(
        body, name="bias_expand", grid=(H,),
        in_specs=[pl.BlockSpec((None, 1, F_LEN), lambda h: (h, 0, 0))],
        out_specs=pl.BlockSpec((None, Q_BLOCK, K_SPAN), lambda h: (h, 0, 0)),
        out_shape=jax.ShapeDtypeStruct((H, Q_BLOCK, K_SPAN), F32), compiler_params=_params(("arbitrary",)))(frow)


def _bias_reduce(dbias, n_rel):
    H = dbias.shape[0]
    onehot = jnp.asarray((_bias_index()[:, None] == np.arange(n_rel)[None, :]).astype(np.float32), dtype=BF16)

    def body(d_ref, oh_ref, o_ref):
        x = jnp.concatenate([d_ref[...], jnp.zeros((Q_BLOCK, F_LEN - K_SPAN), F32)], axis=1)
        fine = _roll_rows(x, -1, 1, ROW_GROUP).reshape(Q_BLOCK // ROW_GROUP, ROW_GROUP, F_LEN)
        coarse = _roll_rows(jnp.sum(fine, axis=1), -1, ROW_GROUP, Q_BLOCK // ROW_GROUP)
        row = jnp.broadcast_to(jnp.sum(coarse, axis=0, keepdims=True), (8, F_LEN))
        acc = jnp.zeros((8, n_rel), F32)
        for _ in range(3):
            piece = row.astype(BF16)
            acc = acc + jnp.dot(piece, oh_ref[...], preferred_element_type=F32)
            row = row - piece.astype(F32)
        o_ref[...] = acc[0:1]

    return pl.pallas_call(
        body, name="bias_reduce", grid=(H,),
        in_specs=[pl.BlockSpec((None, Q_BLOCK, K_SPAN), lambda h: (h, 0, 0)),
                  pl.BlockSpec((F_LEN, n_rel), lambda h: (0, 0))],
        out_specs=pl.BlockSpec((None, 1, n_rel), lambda h: (h, 0, 0)),
        out_shape=jax.ShapeDtypeStruct((H, 1, n_rel), F32), compiler_params=_params(("arbitrary",)))(dbias, onehot)


def _attn_specs(S):
    hw = HEADS_PER_STEP * HEAD_DIM
    qspec = pl.BlockSpec((None, Q_BLOCK, hw), lambda g, b, i: (b, i, g))
    kspec = pl.BlockSpec((None, None, S, hw), lambda g, b, i: (0, b, 0, g))
    vspec = pl.BlockSpec((None, None, S, hw), lambda g, b, i: (1, b, 0, g))
    bspec = pl.BlockSpec((HEADS_PER_STEP, Q_BLOCK, K_SPAN), lambda g, b, i: (g, 0, 0))
    return hw, qspec, kspec, vspec, bspec


def _span_cases(i, fn):
    short = PAD // Q_BLOCK
    for j in range(short):
        pl.when(i == j)(functools.partial(fn, PAD - j * Q_BLOCK))
    pl.when(i >= short)(functools.partial(fn, 0))


def _key_start(i, off):
    return 0 if off else pl.multiple_of(i * Q_BLOCK - PAD, Q_BLOCK)


def _attn_exp(q_ref, k_ref, b_ref, h, k0, off):
    hs = slice(h * HEAD_DIM, (h + 1) * HEAD_DIM)
    kh = k_ref[pl.ds(k0, K_SPAN - off), hs]
    s = lax.dot_general(q_ref[:, hs], kh, (((1,), (1,)), ((), ())), preferred_element_type=F32) + b_ref[h, :, off:]
    p = jnp.exp(s - jnp.max(s, axis=-1, keepdims=True))
    return p, 1.0 / jnp.sum(p, axis=-1, keepdims=True), kh


def _attn_fwd(q, kv, bias, B, S):
    HD = q.shape[-1]
    hw, qspec, kspec, vspec, bspec = _attn_specs(S)

    def body(q_ref, k_ref, v_ref, b_ref, o_ref):
        i = pl.program_id(2)

        def block(off):
            k0 = _key_start(i, off)
            outs = []
            for h in range(HEADS_PER_STEP):
                hs = slice(h * HEAD_DIM, (h + 1) * HEAD_DIM)
                p, inv, _ = _attn_exp(q_ref, k_ref, b_ref, h, k0, off)
                outs.append(jnp.dot(p.astype(BF16), v_ref[pl.ds(k0, K_SPAN - off), hs],
                                    preferred_element_type=F32) * inv)
            o_ref[...] = jnp.concatenate(outs, axis=1).astype(BF16)

        _span_cases(i, block)

    return pl.pallas_call(
        body, name="attn_fwd", grid=(HD // hw, B, S // Q_BLOCK), in_specs=[qspec, kspec, vspec, bspec],
        out_specs=qspec, out_shape=jax.ShapeDtypeStruct((B, S, HD), BF16),
        compiler_params=_params(("arbitrary", "arbitrary", "arbitrary")))(q, kv, kv, bias)


def _attn_bwd(q, kv, bias, do, B, S):
    HD = q.shape[-1]
    H = HD // HEAD_DIM
    hw, qspec, kspec, vspec, bspec = _attn_specs(S)
    scale = HEAD_DIM ** -0.5
    nq = S // Q_BLOCK

    def body(q_ref, k_ref, v_ref, b_ref, do_ref, dq_ref, dkv_ref, db_ref, dk_acc, dv_acc):
        b, i = pl.program_id(1), pl.program_id(2)

        @pl.when(i == 0)
        def _():
            dk_acc[...] = jnp.zeros_like(dk_acc)
            dv_acc[...] = jnp.zeros_like(dv_acc)

        @pl.when((i == 0) & (b == 0))
        def _():
            db_ref[...] = jnp.zeros_like(db_ref)

        def block(off):
            k0 = _key_start(i, off)
            keys = pl.ds(k0, K_SPAN - off)
            for h in range(HEADS_PER_STEP):
                hs = slice(h * HEAD_DIM, (h + 1) * HEAD_DIM)
                p, inv, kh = _attn_exp(q_ref, k_ref, b_ref, h, k0, off)
                p = p * inv
                doh = do_ref[:, hs]
                dp = lax.dot_general(doh, v_ref[keys, hs], (((1,), (1,)), ((), ())), preferred_element_type=F32)
                ds = p * (dp - jnp.sum(p * dp, axis=-1, keepdims=True))
                db_ref[h, :, off:] += ds
                dsb = ds.astype(BF16)
                dq_ref[:, hs] = (jnp.dot(dsb, kh, preferred_element_type=F32) * scale).astype(BF16)
                dk_acc[hs, keys] += lax.dot_general(q_ref[:, hs], dsb, (((0,), (0,)), ((), ())),
                                                     preferred_element_type=F32)
                dv_acc[hs, keys] += lax.dot_general(doh, p.astype(BF16), (((0,), (0,)), ((), ())),
                                                     preferred_element_type=F32)

        _span_cases(i, block)

        @pl.when(i == nq - 1)
        def _():
            dkv_ref[0] = dk_acc[...].T.astype(BF16)
            dkv_ref[1] = dv_acc[...].T.astype(BF16)

    return pl.pallas_call(
        body, name="attn_bwd", grid=(HD // hw, B, nq), in_specs=[qspec, kspec, vspec, bspec, qspec],
        out_specs=[qspec, pl.BlockSpec((2, None, S, hw), lambda g, b, i: (0, b, 0, g)), bspec],
        out_shape=[jax.ShapeDtypeStruct((B, S, HD), BF16), jax.ShapeDtypeStruct((2, B, S, HD), BF16),
                   jax.ShapeDtypeStruct((H, Q_BLOCK, K_SPAN), F32)],
        scratch_shapes=[pltpu.VMEM((hw, S), F32), pltpu.VMEM((hw, S), F32)],
        compiler_params=_params(("arbitrary", "arbitrary", "arbitrary")))(q, kv, kv, bias, do)


def _sub_rows(R):
    for cand in (256, 352, 128, 64, 8):
        if R % cand == 0 and R > cand:
            return cand
    return R


def _adamw(w, g, m, v, *, name):
    R, C = w.shape
    tr = _sub_rows(R)

    def body(w_ref, g_ref, m_ref, v_ref, d_ref, nm_ref, nv_ref):
        g = g_ref[...]
        m = ADAM_B1 * m_ref[...] + (1.0 - ADAM_B1) * g
        v = ADAM_B2 * v_ref[...] + (1.0 - ADAM_B2) * (g * g)
        m_hat = m / (1.0 - ADAM_B1 ** ADAM_STEP)
        v_hat = v / (1.0 - ADAM_B2 ** ADAM_STEP)
        d_ref[...] = -ADAM_LR * (m_hat / (jnp.sqrt(v_hat) + ADAM_EPS) + ADAM_WD * w_ref[...])
        nm_ref[...] = m
        nv_ref[...] = v

    spec = pl.BlockSpec((tr, C), lambda i: (i, 0))
    return pl.pallas_call(body, name=name, grid=(R // tr,), in_specs=[spec] * 4, out_specs=[spec] * 3,
                          out_shape=[jax.ShapeDtypeStruct((R, C), F32)] * 3,
                          compiler_params=_params(("arbitrary",)))(w, g, m, v)


def _add_pair(units, got, core, *, name):
    n4, R, C = got.shape
    rows = n4 * R
    tr = 512 if rows % 512 == 0 else R

    def body(c_ref, u_ref, got_ref, o_ref):
        o_ref[...] = (u_ref[...].astype(F32) + got_ref[...].astype(F32)).astype(BF16)

    spec = pl.BlockSpec((tr, C), lambda i, c: (i, 0))
    grid_spec = pltpu.PrefetchScalarGridSpec(
        num_scalar_prefetch=1, grid=(rows // tr,),
        in_specs=[pl.BlockSpec((None, tr, C), lambda i, c: (c[0], i, 0)), spec], out_specs=spec)
    out = pl.pallas_call(body, name=name, grid_spec=grid_spec, out_shape=jax.ShapeDtypeStruct((rows, C), BF16),
                         compiler_params=_params(("arbitrary",)))(core.reshape(1), units.reshape(2, rows, C),
                                                                   got.reshape(rows, C))
    return out.reshape(n4, R, C)


def _sum_chips(w, own, got, pos, *, name, layer=0, into=None):
    _, R, C = own.shape
    tr = _sub_rows(R)
    nr = R // tr

    def body(p_ref, own_ref, got_ref, *rest):
        o_ref = rest[-1]
        o_ref[...] = (own_ref[...].astype(F32) + got_ref[0].astype(F32) + got_ref[1].astype(F32)
                      + got_ref[2].astype(F32))

    if w.row_sharded:
        out_map = lambda i, p: (layer, i, p[1])
    else:
        out_map = lambda i, p: (layer, p[1] * nr + i, 0)
    ins = [pos, own, got]
    in_specs = [pl.BlockSpec((None, tr, C), lambda i, p: (p[0], i, 0)),
                pl.BlockSpec((3, tr, C), lambda i, p: (0, i, 0))]
    alias = {}
    if into is not None:
        ins.append(into)
        in_specs.append(ANY)
        alias = {3: 0}
    grid_spec = pltpu.PrefetchScalarGridSpec(num_scalar_prefetch=1, grid=(nr,), in_specs=in_specs,
                                             out_specs=pl.BlockSpec((None, tr, C), out_map))
    return pl.pallas_call(body, name=name, grid_spec=grid_spec, input_output_aliases=alias,
                          out_shape=jax.ShapeDtypeStruct((w.L, w.ks, w.ns), F32),
                          compiler_params=_params(("arbitrary",)))(*ins)


def _mesh_pos():
    return lax.axis_index("x"), lax.axis_index("y"), lax.axis_index("c")


def _other_chips(x, y):
    return [(1 - x, y), (x, 1 - y), (1 - x, 1 - y)]


ANY = pl.BlockSpec(memory_space=pl.ANY)


class _W:
    def __init__(self, name, shard, row_sharded, direct=False):
        self.name = name
        self.direct = direct
        self.L, ks, ns = shard.shape
        self.row_sharded = row_sharded
        self.K, self.N = (ks * N_CHIPS, ns) if row_sharded else (ks, ns * N_CHIPS)
        self.ks, self.ns = ks, ns

    def shard_of(self, full, j):
        if self.row_sharded:
            return full.at[:, pl.ds(j * self.ks, self.ks), :]
        return full.at[:, :, pl.ds(j * self.ns, self.ns)]

    def half_of(self, shard, c):
        if self.row_sharded:
            return shard.at[:, :, pl.ds(c * (self.ns // 2), self.ns // 2)]
        return shard.at[:, pl.ds(c * (self.ks // 2), self.ks // 2), :]


HBM = pl.BlockSpec(memory_space=pltpu.HBM)
SEM = pl.BlockSpec(memory_space=pltpu.SEMAPHORE)
IN_FLIGHT = pltpu.SideEffectType.DATAFLOW_SIDE_EFFECTING


def _in_hbm(a):
    return pltpu.with_memory_space_constraint(a, pltpu.HBM)


def _gather_start(ws, shards, after, *, name):
    nw = len(ws)

    def body(*refs):
        src, dst = refs[:nw], refs[nw:2 * nw]
        send, recv = refs[2 * nw + 1:3 * nw + 1], refs[3 * nw + 1:4 * nw + 1]
        x, y, c = _mesh_pos()
        me = 2 * x + y
        for i, w in enumerate(ws):
            for f, (px, py) in enumerate(_other_chips(x, y)):
                for e in range(2 if w.direct else 1):
                    k = 2 * f + e
                    pltpu.make_async_remote_copy(
                        src_ref=w.half_of(src[i], c), dst_ref=w.half_of(w.shard_of(dst[i], me), c),
                        send_sem=send[i].at[k], recv_sem=recv[i].at[k], device_id=(px, py, c if e == 0 else 1 - c),
                        device_id_type=MESH).start()

    fulls = [lax.empty((w.L, w.K, w.N), BF16) for w in ws]
    out = pl.pallas_call(
        body, name=name, in_specs=[HBM] * (2 * nw) + [ANY],
        out_specs=[SEM] * (2 * nw) + [HBM] * (2 * nw),
        out_shape=[pltpu.SemaphoreType.DMA((6,))] * (2 * nw)
        + [pltpu.HBM(s.shape, BF16) for s in shards] + [pltpu.HBM(f.shape, BF16) for f in fulls],
        input_output_aliases={i: 2 * nw + i for i in range(2 * nw)},
        compiler_params=pltpu.CompilerParams(has_side_effects=IN_FLIGHT))(
            *[_in_hbm(s) for s in shards], *[_in_hbm(f) for f in fulls], after)
    return [(out[i], out[nw + i], out[2 * nw + i], out[3 * nw + i]) for i in range(nw)]


def _gather_wait(ws, flight, after, *, name):
    nw = len(ws)

    def body(*refs):
        src, dst = refs[:nw], refs[nw:2 * nw]
        send, recv = refs[2 * nw:3 * nw], refs[3 * nw:4 * nw]
        x, y, c = _mesh_pos()
        for i, w in enumerate(ws):
            for f, (px, py) in enumerate(_other_chips(x, y)):
                for e in range(2 if w.direct else 1):
                    k = 2 * f + e
                    landed = w.half_of(w.shard_of(dst[i], 2 * px + py), c if e == 0 else 1 - c)
                    cp = pltpu.make_async_remote_copy(
                        src_ref=w.half_of(src[i], c), dst_ref=landed, send_sem=send[i].at[k], recv_sem=recv[i].at[k],
                        device_id=(px, py, c), device_id_type=MESH)
                    cp.wait_send()
                    cp.wait_recv()

    shards, fulls = [fl[2] for fl in flight], [fl[3] for fl in flight]
    out = pl.pallas_call(
        body, name=name, in_specs=[HBM] * (2 * nw) + [SEM] * (2 * nw) + [ANY],
        out_specs=[HBM] * (2 * nw),
        out_shape=[pltpu.HBM(s.shape, BF16) for s in shards] + [pltpu.HBM(f.shape, BF16) for f in fulls],
        input_output_aliases={i: i for i in range(2 * nw)},
        compiler_params=pltpu.CompilerParams(has_side_effects=IN_FLIGHT))(
            *shards, *fulls, *[fl[0] for fl in flight], *[fl[1] for fl in flight], after)
    return out[:nw], out[nw:]


def _gather_finish(ws, shards, fulls, *, name):
    nw = len(ws)
    forward = not ws[0].direct

    def body(*refs):
        src, dst, stage = refs[:nw], refs[3 * nw:4 * nw], refs[4 * nw:5 * nw]
        send_sems, recv_sems, load_sems, store_sems = refs[5 * nw:]
        x, y, c = _mesh_pos()
        me = 2 * x + y
        sibling = (x, y, 1 - c)
        chips = _other_chips(x, y)

        def fwd(i, w, f, half):
            px, py = chips[f]
            landed = w.half_of(w.shard_of(dst[i], 2 * px + py), half)
            return pltpu.make_async_remote_copy(src_ref=landed, dst_ref=landed, send_sem=send_sems.at[3 * i + f],
                                                recv_sem=recv_sems.at[3 * i + f], device_id=sibling,
                                                device_id_type=MESH)

        loads = [pltpu.make_async_copy(src[i], stage[i], load_sems.at[i]) for i in range(nw)]
        for cp in loads:
            cp.start()
        sends = [fwd(i, w, f, c) for i, w in enumerate(ws) for f in range(3)] if forward else []
        for cp in sends:
            cp.start()
        stores = [pltpu.make_async_copy(stage[i], w.shard_of(dst[i], me), store_sems.at[i])
                  for i, w in enumerate(ws)]
        for ld, st in zip(loads, stores):
            ld.wait()
            st.start()
        if forward:
            for i, w in enumerate(ws):
                for f in range(3):
                    fwd(i, w, f, 1 - c).wait_recv()
        for cp in sends:
            cp.wait_send()
        for cp in stores:
            cp.wait()

    out = pl.pallas_call(
        body, name=name, in_specs=[ANY] * (2 * nw), out_specs=[ANY] * (2 * nw),
        out_shape=[jax.ShapeDtypeStruct(s.shape, BF16) for s in shards]
        + [jax.ShapeDtypeStruct(f.shape, BF16) for f in fulls],
        input_output_aliases={i: i for i in range(2 * nw)},
        scratch_shapes=[pltpu.VMEM((w.L, w.ks, w.ns), BF16) for w in ws]
        + [pltpu.SemaphoreType.DMA((3 * nw,)), pltpu.SemaphoreType.DMA((3 * nw,)), pltpu.SemaphoreType.DMA((nw,)),
           pltpu.SemaphoreType.DMA((nw,))],
        compiler_params=_params(has_side_effects=True))(*shards, *fulls)
    return out[nw:]


def _split_copies(name, srcs, lands, n_sems, copies_of, *, flight=None, after=None):
    n = len(srcs)
    starting = flight is None

    def body(*refs):
        src, land = refs[:n], refs[n:2 * n]
        sems = refs[2 * n + 1:4 * n + 1] if starting else refs[2 * n:4 * n]
        for i in range(n):
            for cp in copies_of(i, src[i], land[i], sems[i], sems[n + i]):
                if starting:
                    cp.start()
                else:
                    cp.wait_send()
                    cp.wait_recv()

    thru = [pltpu.HBM(a.shape, a.dtype) for a in list(srcs) + list(lands)]
    if starting:
        out = pl.pallas_call(
            body, name=name, in_specs=[HBM] * (2 * n) + [ANY], out_specs=[SEM] * (2 * n) + [HBM] * (2 * n),
            out_shape=[pltpu.SemaphoreType.DMA((n_sems,))] * (2 * n) + thru,
            input_output_aliases={i: 2 * n + i for i in range(2 * n)},
            compiler_params=pltpu.CompilerParams(has_side_effects=IN_FLIGHT))(
                *[_in_hbm(a) for a in srcs], *[_in_hbm(a) for a in lands], after)
        return [(out[i], out[n + i], out[2 * n + i], out[3 * n + i]) for i in range(n)]
    out = pl.pallas_call(
        body, name=name, in_specs=[HBM] * (2 * n) + [SEM] * (2 * n) + [ANY], out_specs=[HBM] * (2 * n),
        out_shape=thru, input_output_aliases={i: i for i in range(2 * n)},
        compiler_params=pltpu.CompilerParams(has_side_effects=IN_FLIGHT))(
            *srcs, *lands, *[fl[0] for fl in flight], *[fl[1] for fl in flight], after)
    return out[:n], out[n:]


def _sum8(land, vec, me):
    R = vec.shape[0]

    def body(me_ref, land_ref, vec_ref, o_ref):
        acc = jnp.zeros((R, 128), F32)
        for d in range(8):
            acc = acc + jnp.where(me_ref[0] == d, vec_ref[...], land_ref[d])
        o_ref[...] = acc

    grid_spec = pltpu.PrefetchScalarGridSpec(
        num_scalar_prefetch=1, grid=(1,),
        in_specs=[pl.BlockSpec((8, R, 128), lambda i, m: (0, 0, 0)), pl.BlockSpec((R, 128), lambda i, m: (0, 0))],
        out_specs=pl.BlockSpec((R, 128), lambda i, m: (0, 0)))
    return pl.pallas_call(body, name="sum8", grid_spec=grid_spec, out_shape=jax.ShapeDtypeStruct((R, 128), F32),
                          compiler_params=_params(("arbitrary",)))(me.reshape(1), land, vec)


def _swap_copies(i, src, got, send, recv):
    x, y, c = _mesh_pos()
    return [pltpu.make_async_remote_copy(src_ref=src.at[1 - c], dst_ref=got, send_sem=send.at[0], recv_sem=recv.at[0],
                                         device_id=(x, y, 1 - c), device_id_type=MESH)]


def _gather8_copies(i, src, land, send, recv):
    x, y, c = _mesh_pos()
    me = 4 * x + 2 * y + c
    peers = [(x, y, 1 - c)] + [(px, py, pc) for px, py in _other_chips(x, y) for pc in (c, 1 - c)]
    return [pltpu.make_async_remote_copy(src_ref=src, dst_ref=land.at[me], send_sem=send.at[k], recv_sem=recv.at[k],
                                         device_id=peer, device_id_type=MESH) for k, peer in enumerate(peers)]


def _scatter_copy(src, got, send, recv, f, chip, c):
    px, py = chip
    return pltpu.make_async_remote_copy(src_ref=src.at[2 * px + py], dst_ref=got.at[f], send_sem=send.at[f],
                                        recv_sem=recv.at[f], device_id=(px, py, c), device_id_type=MESH)


def _scatter_start(sums, *, name):
    nw = len(sums)

    def body(*refs):
        src, got = refs[:nw], refs[nw:2 * nw]
        send, recv = refs[2 * nw:3 * nw], refs[3 * nw:4 * nw]
        x, y, c = _mesh_pos()
        for i in range(nw):
            for f, chip in enumerate(_other_chips(x, y)):
                _scatter_copy(src[i], got[i], send[i], recv[i], f, chip, c).start()

    lands = [lax.empty((3,) + s.shape[1:], BF16) for s in sums]
    out = pl.pallas_call(
        body, name=name, in_specs=[HBM] * (2 * nw), out_specs=[SEM] * (2 * nw) + [HBM] * (2 * nw),
        out_shape=[pltpu.SemaphoreType.DMA((3,))] * (2 * nw)
        + [pltpu.HBM(s.shape, BF16) for s in sums] + [pltpu.HBM(l.shape, BF16) for l in lands],
        input_output_aliases={i: 2 * nw + i for i in range(2 * nw)},
        compiler_params=pltpu.CompilerParams(has_side_effects=IN_FLIGHT))(
            *[_in_hbm(s) for s in sums], *[_in_hbm(l) for l in lands])
    return [(out[i], out[nw + i], out[2 * nw + i], out[3 * nw + i]) for i in range(nw)]


def _scatter_wait(flight, after):
    nw = len(flight)

    def body(*refs):
        src, got = refs[:nw], refs[nw:2 * nw]
        send, recv = refs[2 * nw:3 * nw], refs[3 * nw:4 * nw]
        x, y, c = _mesh_pos()
        for i in range(nw):
            for f, chip in enumerate(_other_chips(x, y)):
                cp = _scatter_copy(src[i], got[i], send[i], recv[i], f, chip, c)
                cp.wait_send()
                cp.wait_recv()

    sums, lands = [fl[2] for fl in flight], [fl[3] for fl in flight]
    out = pl.pallas_call(
        body, name="scatter_wait", in_specs=[HBM] * (2 * nw) + [SEM] * (2 * nw) + [ANY], out_specs=[HBM] * (2 * nw),
        out_shape=[pltpu.HBM(s.shape, BF16) for s in sums] + [pltpu.HBM(l.shape, BF16) for l in lands],
        input_output_aliases={i: i for i in range(2 * nw)},
        compiler_params=pltpu.CompilerParams(has_side_effects=IN_FLIGHT))(
            *sums, *lands, *[fl[0] for fl in flight], *[fl[1] for fl in flight], after)
    return out[:nw], out[nw:]


def _join_halves(ws, shards):
    nw = len(ws)

    def body(*refs):
        buf = refs[nw:2 * nw]
        send_sems, recv_sems = refs[2 * nw:]
        x, y, c = _mesh_pos()
        sibling = (x, y, 1 - c)

        def copy(i, w, half):
            region = w.half_of(buf[i], half)
            return pltpu.make_async_remote_copy(src_ref=region, dst_ref=region, send_sem=send_sems.at[i],
                                                recv_sem=recv_sems.at[i], device_id=sibling, device_id_type=MESH)

        sends = [copy(i, w, c) for i, w in enumerate(ws)]
        for cp in sends:
            cp.start()
        for i, w in enumerate(ws):
            copy(i, w, 1 - c).wait_recv()
        for cp in sends:
            cp.wait_send()

    return pl.pallas_call(
        body, name="join_halves", in_specs=[ANY] * nw, out_specs=[ANY] * nw,
        out_shape=[jax.ShapeDtypeStruct((w.L, w.ks, w.ns), F32) for w in ws],
        input_output_aliases={i: i for i in range(nw)},
        scratch_shapes=[pltpu.SemaphoreType.DMA((nw,)), pltpu.SemaphoreType.DMA((nw,))],
        compiler_params=_params(has_side_effects=True))(*shards)


def _allreduce_small(vec):
    R = vec.shape[0]

    def body(x_ref, o_ref, buf, send_sems, recv_sems):
        x, y, c = _mesh_pos()
        me, sibling = (x, y, c), (x, y, 1 - c)
        chips = _other_chips(x, y)

        def slot(px, py, pc):
            return buf.at[4 * px + 2 * py + pc]

        def copy(k, block, to, src=None):
            return pltpu.make_async_remote_copy(src_ref=slot(*block) if src is None else src, dst_ref=slot(*block),
                                                send_sem=send_sems.at[k], recv_sem=recv_sems.at[k], device_id=to,
                                                device_id_type=MESH)

        first = [copy(0, me, sibling, src=x_ref)] + [copy(1 + f, me, (*chip, c), src=x_ref)
                                                     for f, chip in enumerate(chips)]
        for cp in first:
            cp.start()
        passed = [copy(4 + f, (*chip, c), sibling) for f, chip in enumerate(chips)]
        for f, chip in enumerate(chips):
            copy(1 + f, (*chip, c), me).wait_recv()
            passed[f].start()
        copy(0, sibling, me).wait_recv()
        for f, chip in enumerate(chips):
            copy(4 + f, (*chip, 1 - c), me).wait_recv()
        for cp in first + passed:
            cp.wait_send()
        slot(*me)[...] = x_ref[...]
        acc = buf[0]
        for d in range(1, 8):
            acc = acc + buf[d]
        o_ref[...] = acc

    return pl.pallas_call(
        body, name="allreduce_small", in_specs=[pl.BlockSpec(memory_space=pltpu.VMEM)],
        out_specs=pl.BlockSpec(memory_space=pltpu.VMEM), out_shape=jax.ShapeDtypeStruct((R, 128), F32),
        scratch_shapes=[pltpu.VMEM((8, R, 128), F32), pltpu.SemaphoreType.DMA((7,)), pltpu.SemaphoreType.DMA((7,))],
        compiler_params=_params())(vec)


def _pack(parts):
    flat = jnp.concatenate([p.reshape(-1).astype(F32) for p in parts])
    n = flat.shape[0]
    pad = (-n) % (64 * 128)
    return jnp.pad(flat, (0, pad)).reshape(-1, 128)


def _unpack(vec, shapes):
    flat = vec.reshape(-1)
    out, off = [], 0
    for s in shapes:
        n = int(np.prod(s))
        out.append(flat[off:off + n].reshape(s))
        off += n
    return out


def kernel(x, a_norm_g, a_w_in, a_v_norm_g, a_w_s, a_b_s, a_w_out, kv_norm_g, w_kv, b_norm_g, b_w_q, b_rel_bias, b_w_o, f_norm_g, f_w_in, f_conv_w, f_conv_b, f_w_down, final_norm_g, loss_target, m_a_norm_g, m_a_w_in, m_a_v_norm_g, m_a_w_s, m_a_b_s, m_a_w_out, m_kv_norm_g, m_w_kv, m_b_norm_g, m_b_w_q, m_b_rel_bias, m_b_w_o, m_f_norm_g, m_f_w_in, m_f_conv_w, m_f_conv_b, m_f_w_down, m_final_norm_g, v_a_norm_g, v_a_w_in, v_a_v_norm_g, v_a_w_s, v_a_b_s, v_a_w_out, v_kv_norm_g, v_w_kv, v_b_norm_g, v_b_w_q, v_b_rel_bias, v_b_w_o, v_f_norm_g, v_f_w_in, v_f_conv_w, v_f_conv_b, v_f_w_down, v_final_norm_g):
    B, S, D = x.shape
    T = B * S
    xi, yi, ci = lax.axis_index("x"), lax.axis_index("y"), lax.axis_index("c")
    j_me = (2 * xi + yi).astype(jnp.int32)
    core = ci.astype(jnp.int32)
    pos = jnp.stack([j_me, core])

    w_shards = {"a_w_in": (a_w_in, False), "a_w_out": (a_w_out, True), "w_kv": (w_kv[None], False),
                "b_w_q": (b_w_q, True), "b_w_o": (b_w_o, True), "f_w_in": (f_w_in, False), "f_w_down": (f_w_down, True)}
    names = list(w_shards)
    ws = [_W(n, w_shards[n][0], w_shards[n][1]) for n in names]
    g_shards = {"a_w_in": (a_w_in, False), "a_w_out": (a_w_out, True),
                "f_w_in0": (f_w_in[0:1], False), "f_w_down0": (f_w_down[0:1], True),
                "w_kv": (w_kv[None], False), "b_w_q": (b_w_q, True), "b_w_o": (b_w_o, True),
                "f_w_in1": (f_w_in[1:2], False), "f_w_down1": (f_w_down[1:2], True)}
    g_names = list(g_shards)
    g_ws = {n: _W(n, *g_shards[n], direct=n in ("w_kv", "b_w_q", "b_w_o", "f_w_in1", "f_w_down1")) for n in g_names}

    Wd = a_w_in.shape[1]
    GW = a_v_norm_g.shape[1] * N_CHIPS
    F2 = f_conv_w.shape[2] * N_CHIPS
    Fh = F2 // 2
    nsd, nsg, nsf = a_norm_g.shape[1], a_v_norm_g.shape[1], f_conv_w.shape[2]
    own = (ci == 0).astype(F32)
    place = lambda sh, width, n: lax.dynamic_update_slice_in_dim(
        jnp.zeros(sh.shape[:-1] + (width,), F32), sh * own, j_me * n, axis=sh.ndim - 1)
    def tied(x, flight):
        x, thru = lax.optimization_barrier((x, flight[0][2]))
        return x, [flight[0][:2] + (thru,) + flight[0][3:]] + flight[1:]

    gathered = _allreduce_small(_pack([place(a_norm_g, Wd, nsd), place(a_v_norm_g, GW, nsg),
                                       place(f_conv_w, F2, nsf)]))
    a_g, a_vg, conv_w = _unpack(gathered, [(1, Wd), (1, GW), (2, 3, F2)])
    flight = dict(zip(g_names, _gather_start([g_ws[n] for n in g_names],
                                             [g_shards[n][0].astype(BF16) for n in g_names], gathered,
                                             name="gather_start")))
    full = {}

    def arrive(group, after, tag):
        gw = [g_ws[n] for n in group]
        sh, fu = _gather_wait(gw, [flight[n] for n in group], after, name=f"gather_wait_{tag}")
        full.update(zip(group, _gather_finish(gw, sh, fu, name=f"gather_finish_{tag}")))
    conv_w2 = conv_w.reshape(2, 3, 2, Fh).transpose(0, 2, 1, 3)
    conv_b2 = f_conv_b.reshape(2, 2, Fh)

    h0 = x.reshape(T, D)
    target = loss_target.reshape(T, D)
    bs_tile = jnp.repeat(a_b_s[0].T, GROUP_DIM, axis=1)
    ws_a = a_w_s[0]
    scale = HEAD_DIM ** -0.5
    HD = b_w_q.shape[2]
    H = HD // HEAD_DIM
    n_rel = b_rel_bias.shape[-1]
    frow, (flight["a_w_in"],) = tied(b_rel_bias[0][:, _bias_index()].reshape(H, 1, F_LEN), [flight["a_w_in"]])
    bias = _bias_expand(frow)

    def ffn_fwd(h, l, loss=None):
        yff, a, c, n = _ffn_in_conv(h, full[f"f_w_in{l}"], f_norm_g[l], conv_w2[l], conv_b2[l], S, name=f"ffn{l}_in")
        return _mm(yff, full[f"f_w_down{l}"], layer=0, res=h, loss=loss, name=f"ffn{l}_down"), (a, c, n, yff)

    arrive(["a_w_in", "a_w_out"], bias, "a")
    zp, n_a = _mm(h0, full["a_w_in"], layer=0, norm_g=a_g[0], out_dtype=BF16, emit_norm=True, name="a_in")
    out_a = _gate_fwd(zp, a_vg, ws_a, bs_tile)
    h1 = _mm(out_a, full["a_w_out"], layer=0, res=h0, name="a_out")
    arrive(["f_w_in0", "f_w_down0"], h1, "f0")
    h2, saved0 = ffn_fwd(h1, 0)
    arrive(["w_kv", "b_w_q", "b_w_o"], h2, "b")
    arrive(["f_w_in1", "f_w_down1"], h2, "f1")
    kv, n_kv = _mm(h2, full["w_kv"], layer=0, norm_g=kv_norm_g, out_dtype=BF16, split_out=True, emit_norm=True,
                   name="kv")
    q, n_q = _mm(h2, full["b_w_q"], layer=0, norm_g=b_norm_g[0], scale=scale, out_dtype=BF16, emit_norm=True,
                 name="q")
    kv4, q3 = kv.reshape(2, B, S, HD), q.reshape(B, S, HD)
    o = _attn_fwd(q3, kv4, bias, B, S).reshape(T, HD)
    h3 = _mm(o, full["b_w_o"], layer=0, res=h2, name="attn_out")
    (dh, loss8, dg_final), saved1 = ffn_fwd(h3, 1, loss=(final_norm_g, target))

    units = {}

    in_flight = {}

    def swap_start(group, tag, carry):
        us = [units[n] for n in group]
        lands = [lax.empty(u.shape[1:], BF16) for u in us]
        carry, flight = tied(carry, _split_copies(f"swap_start_{tag}", us, lands, 1, _swap_copies, after=carry))
        return (group, tag, flight), carry

    def reduce_start(swap, after):
        group, tag, flight = swap
        us, got = _split_copies(f"swap_wait_{tag}", [fl[2] for fl in flight], [fl[3] for fl in flight], 1,
                                _swap_copies, flight=flight, after=after)
        sums = [_add_pair(u, g_, core, name=f"pair_{n}") for n, u, g_ in zip(group, us, got)]
        after, flight = tied(after, _scatter_start(sums, name=f"scatter_start_{tag}"))
        in_flight.update(zip(group, flight))
        return after

    def ffn_bwd(dh, h, saved, l, early):
        a, c, n, yff = saved
        units[f"f_w_down{l}"] = _mm_tn(yff, dh, rows_are_shards=True, name=f"ffn{l}_down_dw")
        dh_in = dh
        if early:
            sw, dh_in = swap_start([f"f_w_down{l}"], f"fd{l}", dh)
        dyff = _mm(dh_in, full[f"f_w_down{l}"], layer=0, trans_w=True, out_dtype=BF16, name=f"ffn{l}_down_dx")
        if early:
            dyff = reduce_start(sw, dyff)
        da, dcw, dcb = _conv_bwd(a, c, dyff, conv_w2[l], S)
        units[f"f_w_in{l}"] = _mm_tn(n, da, split_y=True, name=f"ffn{l}_in_dw")
        sw, da = swap_start([f"f_w_in{l}"] if early else [f"f_w_down{l}", f"f_w_in{l}"], f"f{l}", da)
        dh, dg = _mm(da, full[f"f_w_in{l}"], layer=0, trans_w=True, split_x=True, bwd=(h, f_norm_g[l], dh),
                     name=f"ffn{l}_in_dx")
        return reduce_start(sw, dh), dg, dcw, dcb

    dh, dg_f1, dcw1, dcb1 = ffn_bwd(dh, h3, saved1, 1, False)
    do = _mm(dh, full["b_w_o"], layer=0, trans_w=True, out_dtype=BF16, name="attn_out_dx")
    units["b_w_o"] = _mm_tn(o, dh, rows_are_shards=True, name="b_w_o_dw")
    dq, dkv, dbias = _attn_bwd(q3, kv4, bias, do.reshape(B, S, HD), B, S)
    dq, d_rel = lax.optimization_barrier((dq, _bias_reduce(dbias, n_rel)))
    d_rel = d_rel.reshape(1, H, n_rel)
    dq, dkv = dq.reshape(T, HD), dkv.reshape(2, T, HD)
    units["b_w_q"] = _mm_tn(n_q, dq, rows_are_shards=True, name="b_w_q_dw")
    dh, dg_b = _mm(dq, full["b_w_q"], layer=0, trans_w=True, bwd=(h2, b_norm_g[0], dh), name="q_dx")
    units["w_kv"] = _mm_tn(n_kv, dkv, split_y=True, name="w_kv_dw")
    sw, dkv = swap_start(["b_w_o", "b_w_q", "w_kv"], "b", dkv)
    dh, dg_kv = _mm(dkv, full["w_kv"], layer=0, trans_w=True, split_x=True, bwd=(h2, kv_norm_g, dh), name="kv_dx")
    dh = reduce_start(sw, dh)
    dh, dg_f0, dcw0, dcb0 = ffn_bwd(dh, h1, saved0, 0, True)
    units["a_w_out"] = _mm_tn(out_a, dh, rows_are_shards=True, name="a_w_out_dw")
    sw, dh_in = swap_start(["a_w_out"], "ao", dh)
    d_out = _mm(dh_in, full["a_w_out"], layer=0, trans_w=True, out_dtype=BF16, name="a_out_dx")
    d_out = reduce_start(sw, d_out)
    dzp, dws, dbs, dgv = _gate_bwd(zp, d_out, a_vg, ws_a, bs_tile)
    units["a_w_in"] = _mm_tn(n_a, dzp, name="a_w_in_dw")
    sw, dzp_in = swap_start(["a_w_in"], "ai", dzp)
    grad_x, dg_a = _mm(dzp_in, full["a_w_in"], layer=0, trans_w=True, bwd=(h0, a_g[0], dh), name="a_in_dx")
    grad_x = reduce_start(sw, grad_x)

    to_flat = lambda d: d.transpose(1, 0, 2).reshape(3, F2)
    small = {"a_norm_g": dg_a, "a_v_norm_g": dgv, "a_w_s": dws[None], "a_b_s": dbs[None], "kv_norm_g": dg_kv[0],
             "b_norm_g": dg_b, "b_rel_bias": d_rel, "f_norm_g": jnp.concatenate([dg_f0, dg_f1], axis=0),
             "f_conv_w": jnp.stack([to_flat(dcw0), to_flat(dcw1)]),
             "f_conv_b": jnp.stack([dcb0.reshape(F2), dcb1.reshape(F2)]), "final_norm_g": dg_final[0]}
    snames = list(small)
    small_vec = _pack([small[n] for n in snames] + [loss8[0:1, 0:1]])
    grad_x, small_flight = tied(grad_x, _split_copies("small_start", [small_vec],
                                                      [lax.empty((8,) + small_vec.shape, F32)], 7, _gather8_copies,
                                                      after=grad_x))

    sums, recv = _scatter_wait([in_flight[n] for n in g_names], grad_x)
    sums, recv = dict(zip(g_names, sums)), dict(zip(g_names, recv))
    halves = []
    for n, w in zip(names, ws):
        if w.L == 1:
            halves.append(_sum_chips(w, sums[n], recv[n], pos, name=f"chips_{n}"))
        else:
            first = _sum_chips(w, sums[n + "0"], recv[n + "0"], pos, name=f"chips_{n}0")
            halves.append(_sum_chips(w, sums[n + "1"], recv[n + "1"], pos, layer=1, into=first, name=f"chips_{n}1"))
    g_big = dict(zip(names, _join_halves(ws, halves)))
    g_big["w_kv"] = g_big["w_kv"][0]

    given = dict(a_norm_g=(a_norm_g, m_a_norm_g, v_a_norm_g), a_w_in=(a_w_in, m_a_w_in, v_a_w_in),
                 a_v_norm_g=(a_v_norm_g, m_a_v_norm_g, v_a_v_norm_g), a_w_s=(a_w_s, m_a_w_s, v_a_w_s),
                 a_b_s=(a_b_s, m_a_b_s, v_a_b_s), a_w_out=(a_w_out, m_a_w_out, v_a_w_out),
                 kv_norm_g=(kv_norm_g, m_kv_norm_g, v_kv_norm_g), w_kv=(w_kv, m_w_kv, v_w_kv),
                 b_norm_g=(b_norm_g, m_b_norm_g, v_b_norm_g), b_w_q=(b_w_q, m_b_w_q, v_b_w_q),
                 b_rel_bias=(b_rel_bias, m_b_rel_bias, v_b_rel_bias), b_w_o=(b_w_o, m_b_w_o, v_b_w_o),
                 f_norm_g=(f_norm_g, m_f_norm_g, v_f_norm_g), f_w_in=(f_w_in, m_f_w_in, v_f_w_in),
                 f_conv_w=(f_conv_w, m_f_conv_w, v_f_conv_w), f_conv_b=(f_conv_b, m_f_conv_b, v_f_conv_b),
                 f_w_down=(f_w_down, m_f_w_down, v_f_w_down), final_norm_g=(final_norm_g, m_final_norm_g, v_final_norm_g))
    order = list(given)
    grads, deltas, new_m, new_v = {}, {}, {}, {}
    for n in names:
        w_, m_, v_ = given[n]
        g_ = g_big[n]
        C = w_.shape[-1]
        d2, m2, v2 = _adamw(w_.reshape(-1, C), g_.reshape(-1, C), m_.reshape(-1, C), v_.reshape(-1, C),
                            name=f"adamw_{n}")
        grads[n], deltas[n], new_m[n], new_v[n] = g_.reshape(w_.shape), d2.reshape(w_.shape), m2.reshape(w_.shape), \
            v2.reshape(w_.shape)
    vecs, lands = _split_copies("small_wait", [small_flight[0][2]], [small_flight[0][3]], 7, _gather8_copies,
                                flight=small_flight, after=deltas[names[-1]])
    red = _sum8(lands[0], vecs[0], (4 * xi + 2 * yi + ci).astype(jnp.int32))
    parts = _unpack(red, [small[n].shape for n in snames] + [(1,)])
    g_small = dict(zip(snames, parts[:-1]))
    loss = parts[-1][0]
    g_small["a_norm_g"] = lax.dynamic_slice_in_dim(g_small["a_norm_g"], j_me * nsd, nsd, axis=1)
    g_small["a_v_norm_g"] = lax.dynamic_slice_in_dim(g_small["a_v_norm_g"], j_me * nsg, nsg, axis=1)
    g_small["f_conv_w"] = lax.dynamic_slice_in_dim(g_small["f_conv_w"], j_me * nsf, nsf, axis=2)

    sm = [n for n in order if n not in names]
    d2, m2, v2 = _adamw(_pack([given[n][0] for n in sm]), _pack([g_small[n].reshape(given[n][0].shape) for n in sm]),
                        _pack([given[n][1] for n in sm]), _pack([given[n][2] for n in sm]), name="adamw_small")
    shapes = [given[n][0].shape for n in sm]
    for n, d_, m_, v_ in zip(sm, _unpack(d2, shapes), _unpack(m2, shapes), _unpack(v2, shapes)):
        grads[n], deltas[n], new_m[n], new_v[n] = g_small[n].reshape(given[n][0].shape), d_, m_, v_

    return (loss, grad_x.reshape(B, S, D), *[grads[n] for n in order], *[deltas[n] for n in order],
            *[new_m[n] for n in order], *[new_v[n] for n in order])
```

```python
import functools
import math

import numpy as np
import jax
import jax.numpy as jnp
from jax import lax
from jax.experimental import pallas as pl
from jax.experimental.pallas import tpu as pltpu

F32 = jnp.float32
BF16 = jnp.bfloat16
MESH = pl.DeviceIdType.MESH

EPS = 1e-6
NEG_INF = -1e30
CHUNK = 64
GMLP_BLOCK = 128
GROUP_DIM = 128
HEAD_DIM = 64
LEFT_CHUNKS = 8
PAD = LEFT_CHUNKS * CHUNK
REL_CLIP = 128
Q_BLOCK = 256
K_SPAN = PAD + Q_BLOCK
F_LEN = K_SPAN + Q_BLOCK
HEADS_PER_STEP = 4
N_CHIPS = 4

ADAM_LR = 0.001
ADAM_B1 = 0.9
ADAM_B2 = 0.999
ADAM_EPS = 1e-08
ADAM_WD = 0.01
ADAM_STEP = 10

VMEM_LIMIT = 56 * 1024 * 1024


def _params(sem=None, **kw):
    if sem is not None:
        kw["dimension_semantics"] = sem
    return pltpu.CompilerParams(vmem_limit_bytes=VMEM_LIMIT, **kw)


def _rms(xf):
    r = lax.rsqrt(jnp.mean(xf * xf, axis=-1, keepdims=True) + EPS)
    return xf * r, r


def _gelu(x, with_grad=False):
    c = math.sqrt(2.0 / math.pi)
    x2 = x * x
    t = jnp.tanh(c * x * (1.0 + 0.044715 * x2))
    half = 0.5 * (1.0 + t)
    if not with_grad:
        return x * half
    return x * half, half + 0.5 * x * (1.0 - t * t) * c * (1.0 + 3.0 * 0.044715 * x2)


def _col_tile(n):
    if n <= 1024:
        return n
    for t in (1408, 1024, 512):
        if n % t == 0:
            return t
    raise ValueError(n)


def _row_tile(t, want):
    while t % want:
        want //= 2
    return want


def _loss_epilogue(h, g_ref, t_ref, dh_ref, loss_ref, dg_ref, first):
    @pl.when(first)
    def _():
        loss_ref[...] = jnp.zeros_like(loss_ref)
        dg_ref[...] = jnp.zeros_like(dg_ref)

    n, r = _rms(h)
    g = g_ref[...]
    e = n * g - t_ref[...]
    loss_ref[...] += 0.5 * jnp.sum(jnp.mean(e * e, axis=-1, keepdims=True), axis=0, keepdims=True)
    dy = e * (1.0 / h.shape[-1])
    dg_ref[...] += jnp.sum(dy * n, axis=0, keepdims=True)
    t = dy * g
    dh_ref[...] = r * (t - n * jnp.mean(t * n, axis=-1, keepdims=True))


def _mm(x, w, *, name, layer=None, trans_w=False, norm_g=None, res=None, scale=None, out_dtype=F32, bwd=None,
        split_out=False, split_x=False, emit_norm=False, loss=None, tm=512):
    T = x.shape[-2]
    K = 2 * x.shape[-1] if split_x else x.shape[-1]
    N = w.shape[-2] if trans_w else w.shape[-1]
    tn = N
    tm = _row_tile(T, 2 * tm if max(K, N) <= 2048 else tm)
    nn, nm = N // tn, T // tm
    has_norm, has_res, has_bwd, has_loss = norm_g is not None, res is not None, bwd is not None, loss is not None
    dims = (((1,), (1,)), ((), ())) if trans_w else (((1,), (0,)), ((), ()))

    def body(*refs):
        it = iter(refs)
        x_ref, w_ref = next(it), next(it)
        g_ref = next(it) if has_norm else None
        res_ref = next(it) if has_res else None
        if has_bwd:
            h_ref, bg_ref, dh_ref = next(it), next(it), next(it)
        if has_loss:
            lg_ref, t_ref = next(it), next(it)
        o_ref = next(it)
        if split_x:
            kh = K // 2
            acc = lax.dot_general(x_ref[0].astype(BF16), w_ref[:, :kh] if trans_w else w_ref[:kh, :], dims,
                                  preferred_element_type=F32)
            acc = acc + lax.dot_general(x_ref[1].astype(BF16), w_ref[:, kh:] if trans_w else w_ref[kh:, :], dims,
                                        preferred_element_type=F32)
        else:
            xv = x_ref[...]
            if has_norm:
                xv = _rms(xv.astype(F32))[0] * g_ref[...]
            xb = xv.astype(BF16)
            if emit_norm:
                refs[-1][...] = xb
            acc = lax.dot_general(xb, w_ref[...], dims, preferred_element_type=F32)
        if scale is not None:
            acc = acc * scale
        if has_res:
            acc = acc + res_ref[...]
        if has_bwd:
            dg_ref = next(it)
            n, r = _rms(h_ref[...])

            @pl.when(pl.program_id(1) == 0)
            def _():
                dg_ref[...] = jnp.zeros_like(dg_ref)

            dg_ref[...] += jnp.sum(acc * n, axis=0, keepdims=True)
            t = acc * bg_ref[...]
            o_ref[...] = dh_ref[...] + r * (t - n * jnp.mean(t * n, axis=-1, keepdims=True))
        elif has_loss:
            _loss_epilogue(acc, lg_ref, t_ref, o_ref, refs[-2], refs[-1], pl.program_id(1) == 0)
        elif split_out:
            o_ref[0] = acc[:, :N // 2].astype(out_dtype)
            o_ref[1] = acc[:, N // 2:].astype(out_dtype)
        else:
            o_ref[...] = acc.astype(out_dtype)

    lead = () if layer is None else (None,)
    lidx = () if layer is None else (layer,)
    ins = [x, w]
    xspec = (pl.BlockSpec((2, tm, K // 2), lambda n, m: (0, m, 0)) if split_x
             else pl.BlockSpec((tm, K), lambda n, m: (m, 0)))
    once = pl.Buffered(1)
    wspec = (pl.BlockSpec(lead + (tn, K), lambda n, m: lidx + (n, 0), pipeline_mode=once) if trans_w
             else pl.BlockSpec(lead + (K, tn), lambda n, m: lidx + (0, n), pipeline_mode=once))
    in_specs = [xspec, wspec]
    if has_norm:
        ins.append(norm_g.reshape(1, K))
        in_specs.append(pl.BlockSpec((1, K), lambda n, m: (0, 0)))
    if has_res:
        ins.append(res)
        in_specs.append(pl.BlockSpec((tm, tn), lambda n, m: (m, n)))
    if split_out:
        out_shape = [jax.ShapeDtypeStruct((2, T, N // 2), out_dtype)]
        out_specs = [pl.BlockSpec((2, tm, N // 2), lambda n, m: (0, m, 0))]
    else:
        out_shape = [jax.ShapeDtypeStruct((T, N), F32 if has_bwd else out_dtype)]
        out_specs = [pl.BlockSpec((tm, tn), lambda n, m: (m, n))]
    if has_bwd:
        h, g, dh = bwd
        ins += [h, g.reshape(1, N), dh]
        in_specs += [pl.BlockSpec((tm, N), lambda n, m: (m, 0)), pl.BlockSpec((1, N), lambda n, m: (0, 0)),
                     pl.BlockSpec((tm, N), lambda n, m: (m, 0))]
        out_shape.append(jax.ShapeDtypeStruct((1, N), F32))
        out_specs.append(pl.BlockSpec((1, N), lambda n, m: (0, 0)))
    if emit_norm:
        out_shape.append(jax.ShapeDtypeStruct((T, K), BF16))
        out_specs.append(pl.BlockSpec((tm, K), lambda n, m: (m, 0)))
    if has_loss:
        ins += [loss[0].reshape(1, N), loss[1]]
        in_specs += [pl.BlockSpec((1, N), lambda n, m: (0, 0)), pl.BlockSpec((tm, N), lambda n, m: (m, 0))]
        out_shape += [jax.ShapeDtypeStruct((8, 128), F32), jax.ShapeDtypeStruct((1, N), F32)]
        out_specs += [pl.BlockSpec((8, 128), lambda n, m: (0, 0)), pl.BlockSpec((1, N), lambda n, m: (0, 0))]
    out = pl.pallas_call(body, name=name, grid=(nn, nm), in_specs=in_specs, out_specs=out_specs, out_shape=out_shape,
                         compiler_params=_params(("arbitrary", "arbitrary")))(*ins)
    return out if has_bwd or emit_norm or has_loss else out[0]


def _mm_tn(x, dy, *, name, rows_are_shards=False, split_y=False, tt=1024):
    T, K = x.shape
    N = 2 * dy.shape[-1] if split_y else dy.shape[-1]
    R, C = (K // N_CHIPS, N // 2) if rows_are_shards else (K // 2, N // N_CHIPS)
    nn = 2 if split_y else 1
    tn = N // nn
    per = N_CHIPS // nn
    assert not (rows_are_shards and split_y)
    tt = _row_tile(T, tt)
    nt = T // tt

    def body(x_ref, y_ref, o_ref, acc_ref):
        t = pl.program_id(1)

        @pl.when(t == 0)
        def _():
            acc_ref[...] = jnp.zeros_like(acc_ref)

        acc_ref[...] += lax.dot_general(x_ref[...], y_ref[...].astype(BF16), (((0,), (0,)), ((), ())),
                                        preferred_element_type=F32)

        @pl.when(t == nt - 1)
        def _():
            if rows_are_shards:
                for h in range(2):
                    o_ref[h] = acc_ref[:, h * C:(h + 1) * C].astype(BF16).reshape(N_CHIPS, R, C)
            else:
                for j in range(per):
                    o_ref[:, j] = acc_ref[:, j * C:(j + 1) * C].astype(BF16).reshape(2, R, C)

    if split_y:
        yspec = pl.BlockSpec((None, tt, tn), lambda n, t: (n, t, 0))
    else:
        yspec = pl.BlockSpec((tt, tn), lambda n, t: (t, 0))
    if rows_are_shards:
        out_spec = pl.BlockSpec((2, N_CHIPS, R, C), lambda n, t: (0, 0, 0, 0))
    else:
        out_spec = pl.BlockSpec((2, per, R, C), lambda n, t: (0, n, 0, 0))
    return pl.pallas_call(body, name=name, grid=(nn, nt),
                          in_specs=[pl.BlockSpec((tt, K), lambda n, t: (t, 0)), yspec], out_specs=out_spec,
                          out_shape=jax.ShapeDtypeStruct((2, N_CHIPS, R, C), BF16),
                          scratch_shapes=[pltpu.VMEM((K, tn), F32)],
                          compiler_params=_params(("arbitrary", "arbitrary")))(x, dy)


def _chunk_mask():
    i = lax.broadcasted_iota(jnp.int32, (GMLP_BLOCK, GMLP_BLOCK), 0) // CHUNK
    j = lax.broadcasted_iota(jnp.int32, (GMLP_BLOCK, GMLP_BLOCK), 1) // CHUNK
    return i >= j


def _gate_fwd(zp, gv, ws, bs_tile, *, tm=256):
    T, W2 = zp.shape
    W = W2 // 2
    G = W // GROUP_DIM
    tm = _row_tile(T, tm)

    def body(zp_ref, gv_ref, ws_ref, bs_ref, o_ref):
        z = _gelu(zp_ref[...].astype(F32))
        u, v = z[:, :W], z[:, W:]
        vn = _rms(v)[0] * gv_ref[...]
        mask = _chunk_mask()
        for g in range(G):
            cs = slice(g * GROUP_DIM, (g + 1) * GROUP_DIM)
            wg = jnp.where(mask, ws_ref[g], 0.0).astype(BF16)
            for b in range(tm // GMLP_BLOCK):
                rs = slice(b * GMLP_BLOCK, (b + 1) * GMLP_BLOCK)
                s = jnp.dot(wg, vn[rs, cs].astype(BF16), preferred_element_type=F32) + bs_ref[:, cs]
                o_ref[rs, cs] = (u[rs, cs] * s).astype(BF16)

    return pl.pallas_call(
        body, name="gate_fwd", grid=(T // tm,),
        in_specs=[pl.BlockSpec((tm, W2), lambda i: (i, 0)), pl.BlockSpec((1, W), lambda i: (0, 0)),
                  pl.BlockSpec((G, GMLP_BLOCK, GMLP_BLOCK), lambda i: (0, 0, 0)),
                  pl.BlockSpec((GMLP_BLOCK, W), lambda i: (0, 0))],
        out_specs=pl.BlockSpec((tm, W), lambda i: (i, 0)), out_shape=jax.ShapeDtypeStruct((T, W), BF16),
        compiler_params=_params(("arbitrary",)))(zp, gv, ws, bs_tile)


def _gate_bwd(zp, d_out, gv, ws, bs_tile, *, tm=256):
    T, W2 = zp.shape
    W = W2 // 2
    G = W // GROUP_DIM
    tm = _row_tile(T, tm)
    nm = T // tm

    def body(zp_ref, do_ref, gv_ref, ws_ref, bs_ref, dzp_ref, dws_ref, dbs_ref, dgv_ref, du_scr, dvn_scr, dsum_scr):
        i = pl.program_id(0)

        @pl.when(i == 0)
        def _():
            dws_ref[...] = jnp.zeros_like(dws_ref)
            dgv_ref[...] = jnp.zeros_like(dgv_ref)
            dsum_scr[...] = jnp.zeros_like(dsum_scr)

        zp = zp_ref[...].astype(F32)
        z, dz = _gelu(zp, with_grad=True)
        u, v = z[:, :W], z[:, W:]
        n, r = _rms(v)
        gv = gv_ref[...]
        vn = n * gv
        d_out = do_ref[...].astype(F32)
        mask = _chunk_mask()
        for g in range(G):
            cs = slice(g * GROUP_DIM, (g + 1) * GROUP_DIM)
            wg = jnp.where(mask, ws_ref[g], 0.0).astype(BF16)
            dw = jnp.zeros((GMLP_BLOCK, GMLP_BLOCK), F32)
            for b in range(tm // GMLP_BLOCK):
                rs = slice(b * GMLP_BLOCK, (b + 1) * GMLP_BLOCK)
                vb = vn[rs, cs].astype(BF16)
                s = jnp.dot(wg, vb, preferred_element_type=F32) + bs_ref[:, cs]
                du_scr[rs, cs] = d_out[rs, cs] * s
                ds = d_out[rs, cs] * u[rs, cs]
                dsb = ds.astype(BF16)
                dvn_scr[rs, cs] = lax.dot_general(wg, dsb, (((0,), (0,)), ((), ())), preferred_element_type=F32)
                dw = dw + lax.dot_general(dsb, vb, (((1,), (1,)), ((), ())), preferred_element_type=F32)
                dsum_scr[:, cs] += ds
            dws_ref[g] += jnp.where(mask, dw, 0.0)
        dvn = dvn_scr[...]
        dgv_ref[...] += jnp.sum(dvn * n, axis=0, keepdims=True)
        t = dvn * gv
        dv = r * (t - n * jnp.mean(t * n, axis=-1, keepdims=True))
        dzp_ref[:, :W] = (du_scr[...] * dz[:, :W]).astype(BF16)
        dzp_ref[:, W:] = (dv * dz[:, W:]).astype(BF16)

        @pl.when(i == nm - 1)
        def _():
            sel = (lax.broadcasted_iota(jnp.int32, (G, W), 1) // GROUP_DIM
                   == lax.broadcasted_iota(jnp.int32, (G, W), 0)).astype(F32)
            dbs_ref[...] = lax.dot_general(sel, dsum_scr[...], (((1,), (1,)), ((), ())),
                                           precision=lax.Precision.HIGHEST, preferred_element_type=F32)

    return pl.pallas_call(
        body, name="gate_bwd", grid=(nm,),
        in_specs=[pl.BlockSpec((tm, W2), lambda i: (i, 0)), pl.BlockSpec((tm, W), lambda i: (i, 0)),
                  pl.BlockSpec((1, W), lambda i: (0, 0)),
                  pl.BlockSpec((G, GMLP_BLOCK, GMLP_BLOCK), lambda i: (0, 0, 0)),
                  pl.BlockSpec((GMLP_BLOCK, W), lambda i: (0, 0))],
        out_specs=[pl.BlockSpec((tm, W2), lambda i: (i, 0)),
                   pl.BlockSpec((G, GMLP_BLOCK, GMLP_BLOCK), lambda i: (0, 0, 0)),
                   pl.BlockSpec((G, GMLP_BLOCK), lambda i: (0, 0)), pl.BlockSpec((1, W), lambda i: (0, 0))],
        out_shape=[jax.ShapeDtypeStruct((T, W2), BF16), jax.ShapeDtypeStruct((G, GMLP_BLOCK, GMLP_BLOCK), F32),
                   jax.ShapeDtypeStruct((G, GMLP_BLOCK), F32), jax.ShapeDtypeStruct((1, W), F32)],
        scratch_shapes=[pltpu.VMEM((tm, W), F32), pltpu.VMEM((tm, W), F32), pltpu.VMEM((GMLP_BLOCK, W), F32)],
        compiler_params=_params(("arbitrary",)))(zp, d_out, gv, ws, bs_tile)


LANES = 128
HALO = 16


def _taps(ext, w, b):
    return w[2:3] * ext[HALO:] + w[1:2] * pltpu.roll(ext, 1, 0)[HALO:] + w[0:1] * pltpu.roll(ext, 2, 0)[HALO:] + b


def _ffn_fwd(h, w, g, cw, cb, wd, S, *, name, loss=None, tm=256):
    T, D = h.shape
    F = w.shape[-1] // 2
    tc = _col_tile(F)
    tm = _row_tile(S, tm)
    has_loss = loss is not None

    def body(*refs):
        h_ref, w_ref, g_ref, cw_ref, cb_ref, wd_ref = refs[:6]
        o_ref, y_ref, a_ref, c_ref, n_ref = refs[8:13] if has_loss else refs[6:11]
        tail = refs[-1]
        first = (pl.program_id(0) * tm) % S == 0
        nb = (_rms(h_ref[...])[0] * g_ref[...]).astype(BF16)
        n_ref[...] = nb
        for j in range(F // tc):
            cs = slice(j * tc, (j + 1) * tc)
            conv = []
            for s in range(2):
                acc = jnp.dot(nb, w_ref[:, s * F + j * tc:s * F + (j + 1) * tc], preferred_element_type=F32)
                ab = acc.astype(BF16)
                a_ref[s, :, cs] = ab
                af = ab.astype(F32)
                ext = jnp.concatenate([jnp.where(first, 0.0, tail[s, :, cs]), af], axis=0)
                tail[s, :, cs] = af[tm - HALO:, :]
                cv = _taps(ext, cw_ref[s, :, cs], cb_ref[s:s + 1, cs]).astype(BF16)
                c_ref[s, :, cs] = cv
                conv.append(cv.astype(F32))
            up, gate = conv
            y_ref[:, cs] = (gate * jax.nn.sigmoid(gate) * up).astype(BF16)
        out = h_ref[...] + jnp.dot(y_ref[...], wd_ref[...], preferred_element_type=F32)
        if has_loss:
            _loss_epilogue(out, refs[6], refs[7], o_ref, refs[13], refs[14], pl.program_id(0) == 0)
        else:
            o_ref[...] = out

    row = lambda width: pl.BlockSpec((tm, width), lambda i: (i, 0))
    wide = pl.BlockSpec((2, tm, F), lambda i: (0, i, 0))
    fixed = lambda *shape: pl.BlockSpec(shape, lambda i: (0,) * len(shape))
    once = pl.Buffered(1)
    ins = [h, w, g.reshape(1, D), cw, cb, wd]
    in_specs = [row(D), pl.BlockSpec((None, D, 2 * F), lambda i: (0, 0, 0), pipeline_mode=once), fixed(1, D),
                fixed(2, 3, F), fixed(2, F), pl.BlockSpec((None, F, D), lambda i: (0, 0, 0), pipeline_mode=once)]
    out_specs = [row(D), row(F), wide, wide, row(D)]
    out_shape = [jax.ShapeDtypeStruct((T, D), F32), jax.ShapeDtypeStruct((T, F), BF16),
                 jax.ShapeDtypeStruct((2, T, F), BF16), jax.ShapeDtypeStruct((2, T, F), BF16),
                 jax.ShapeDtypeStruct((T, D), BF16)]
    if has_loss:
        ins += [loss[0].reshape(1, D), loss[1]]
        in_specs += [fixed(1, D), row(D)]
        out_specs += [fixed(8, 128), fixed(1, D)]
        out_shape += [jax.ShapeDtypeStruct((8, 128), F32), jax.ShapeDtypeStruct((1, D), F32)]
    return pl.pallas_call(body, name=name, grid=(T // tm,), in_specs=in_specs, out_specs=out_specs,
                          out_shape=out_shape, scratch_shapes=[pltpu.VMEM((2, HALO, F), F32)],
                          compiler_params=_params(("arbitrary",)))(*ins)


def _conv_bwd(a, c, dy, cw, S, *, tm=256):
    _, T, F = a.shape
    tc = _col_tile(F)
    tm = _row_tile(S, tm)
    nm = T // tm
    hb = tm // HALO
    TE = tm + HALO
    nxt = lambda j, i: jnp.minimum((i + 1) * hb, T // HALO - 1)

    def body(a_ref, c_ref, nc_ref, dy_ref, ndy_ref, w_ref, da_ref, dw_ref, db_ref):
        i = pl.program_id(1)
        last = ((i + 1) * tm) % S == 0
        keep_n = jnp.where(last, 0.0, 1.0)

        @pl.when(i == 0)
        def _():
            dw_ref[...] = jnp.zeros_like(dw_ref)
            db_ref[...] = jnp.zeros_like(db_ref)

        for j in range(tc // LANES):
            cs = slice(j * LANES, (j + 1) * LANES)
            dyf = jnp.concatenate([dy_ref[:, cs].astype(F32), ndy_ref[:, cs].astype(F32) * keep_n], axis=0)
            up = jnp.concatenate([c_ref[0, :, cs].astype(F32), nc_ref[0, :, cs].astype(F32)], axis=0)
            gate = jnp.concatenate([c_ref[1, :, cs].astype(F32), nc_ref[1, :, cs].astype(F32)], axis=0)
            sg = jax.nn.sigmoid(gate)
            for s, d in ((0, dyf * (gate * sg)), (1, dyf * up * (sg * (1.0 + gate * (1.0 - sg))))):
                a = a_ref[s, :, cs].astype(F32)
                w = w_ref[s, :, cs]
                u1, u2 = pltpu.roll(d, TE - 1, 0), pltpu.roll(d, TE - 2, 0)
                db_ref[s:s + 1, cs] += jnp.sum(d[:tm], axis=0, keepdims=True)
                dw_ref[s, 2:3, cs] += jnp.sum(d[:tm] * a, axis=0, keepdims=True)
                dw_ref[s, 1:2, cs] += jnp.sum(u1[:tm] * a, axis=0, keepdims=True)
                dw_ref[s, 0:1, cs] += jnp.sum(u2[:tm] * a, axis=0, keepdims=True)
                da_ref[s, :, cs] = (w[2:3] * d + w[1:2] * u1 + w[0:1] * u2)[:tm].astype(BF16)

    cur = pl.BlockSpec((2, tm, tc), lambda j, i: (0, i, j))
    return pl.pallas_call(
        body, name="conv_bwd", grid=(F // tc, nm),
        in_specs=[cur, cur, pl.BlockSpec((2, HALO, tc), lambda j, i: (0, nxt(j, i), j)),
                  pl.BlockSpec((tm, tc), lambda j, i: (i, j)), pl.BlockSpec((HALO, tc), lambda j, i: (nxt(j, i), j)),
                  pl.BlockSpec((2, 3, tc), lambda j, i: (0, 0, j))],
        out_specs=[cur, pl.BlockSpec((2, 3, tc), lambda j, i: (0, 0, j)), pl.BlockSpec((2, tc), lambda j, i: (0, j))],
        out_shape=[jax.ShapeDtypeStruct((2, T, F), BF16), jax.ShapeDtypeStruct((2, 3, F), F32),
                   jax.ShapeDtypeStruct((2, F), F32)],
        compiler_params=_params(("arbitrary", "arbitrary")))(a, c, c, dy, dy, cw)


def _bias_index():
    idx = np.arange(F_LEN)
    d = np.where(idx < K_SPAN, idx, idx - F_LEN)
    return np.clip(PAD - d, -REL_CLIP, REL_CLIP) + REL_CLIP


ROW_GROUP = 16


def _roll_rows(x, sign, unit, steps):
    rows = lax.broadcasted_iota(jnp.int32, x.shape, 0)
    step = 1
    while step < steps:
        shift = unit * step if sign > 0 else F_LEN - unit * step
        x = jnp.where((rows & step) != 0, pltpu.roll(x, shift, 1), x)
        step *= 2
    return x


def _bias_expand(frow):
    H = frow.shape[0]
    groups = Q_BLOCK // ROW_GROUP

    def body(f_ref, o_ref):
        coarse = _roll_rows(jnp.broadcast_to(f_ref[...], (groups, F_LEN)), 1, ROW_GROUP, groups)
        x = jnp.concatenate([jnp.broadcast_to(coarse[a:a + 1], (ROW_GROUP, F_LEN)) for a in range(groups)], axis=0)
        x = _roll_rows(x, 1, 1, ROW_GROUP)[:, :K_SPAN]
        qc = lax.broadcasted_iota(jnp.int32, (Q_BLOCK, K_SPAN), 0) // CHUNK * CHUNK
        kj = lax.broadcasted_iota(jnp.int32, (Q_BLOCK, K_SPAN), 1)
        o_ref[...] = jnp.where((kj >= qc) & (kj < qc + PAD + CHUNK), x, NEG_INF)

    return pl.pallas_call(
        body, name="bias_expand", grid=(H,),
        in_specs=[pl.BlockSpec((None, 1, F_LEN), lambda h: (h, 0, 0))],
        out_specs=pl.BlockSpec((None, Q_BLOCK, K_SPAN), lambda h: (h, 0, 0)),
        out_shape=jax.ShapeDtypeStruct((H, Q_BLOCK, K_SPAN), F32), compiler_params=_params(("arbitrary",)))(frow)


def _bias_reduce(dbias, n_rel):
    H = dbias.shape[0]
    onehot = jnp.asarray((_bias_index()[:, None] == np.arange(n_rel)[None, :]).astype(np.float32), dtype=BF16)

    def body(d_ref, oh_ref, o_ref):
        x = jnp.concatenate([d_ref[...], jnp.zeros((Q_BLOCK, F_LEN - K_SPAN), F32)], axis=1)
        fine = _roll_rows(x, -1, 1, ROW_GROUP).reshape(Q_BLOCK // ROW_GROUP, ROW_GROUP, F_LEN)
        coarse = _roll_rows(jnp.sum(fine, axis=1), -1, ROW_GROUP, Q_BLOCK // ROW_GROUP)
        row = jnp.broadcast_to(jnp.sum(coarse, axis=0, keepdims=True), (8, F_LEN))
        acc = jnp.zeros((8, n_rel), F32)
        for _ in range(3):
            piece = row.astype(BF16)
            acc = acc + jnp.dot(piece, oh_ref[...], preferred_element_type=F32)
            row = row - piece.astype(F32)
        o_ref[...] = acc[0:1]

    return pl.pallas_call(
        body, name="bias_reduce", grid=(H,),
        in_specs=[pl.BlockSpec((None, Q_BLOCK, K_SPAN), lambda h: (h, 0, 0)),
                  pl.BlockSpec((F_LEN, n_rel), lambda h: (0, 0))],
        out_specs=pl.BlockSpec((None, 1, n_rel), lambda h: (h, 0, 0)),
        out_shape=jax.ShapeDtypeStruct((H, 1, n_rel), F32), compiler_params=_params(("arbitrary",)))(dbias, onehot)


def _attn_specs(S):
    hw = HEADS_PER_STEP * HEAD_DIM
    qspec = pl.BlockSpec((None, Q_BLOCK, hw), lambda g, b, i: (b, i, g))
    kspec = pl.BlockSpec((None, None, S, hw), lambda g, b, i: (0, b, 0, g))
    vspec = pl.BlockSpec((None, None, S, hw), lambda g, b, i: (1, b, 0, g))
    bspec = pl.BlockSpec((HEADS_PER_STEP, Q_BLOCK, K_SPAN), lambda g, b, i: (g, 0, 0))
    return hw, qspec, kspec, vspec, bspec


def _span_cases(i, fn):
    short = PAD // Q_BLOCK
    for j in range(short):
        pl.when(i == j)(functools.partial(fn, PAD - j * Q_BLOCK))
    pl.when(i >= short)(functools.partial(fn, 0))


def _key_start(i, off):
    return 0 if off else pl.multiple_of(i * Q_BLOCK - PAD, Q_BLOCK)


def _attn_exp(q_ref, k_ref, b_ref, h, k0, off):
    hs = slice(h * HEAD_DIM, (h + 1) * HEAD_DIM)
    kh = k_ref[pl.ds(k0, K_SPAN - off), hs]
    s = lax.dot_general(q_ref[:, hs], kh, (((1,), (1,)), ((), ())), preferred_element_type=F32) + b_ref[h, :, off:]
    p = jnp.exp(s - jnp.max(s, axis=-1, keepdims=True))
    return p, 1.0 / jnp.sum(p, axis=-1, keepdims=True), kh


def _attn_fwd(q, kv, bias, B, S):
    HD = q.shape[-1]
    hw, qspec, kspec, vspec, bspec = _attn_specs(S)

    def body(q_ref, k_ref, v_ref, b_ref, o_ref):
        i = pl.program_id(2)

        def block(off):
            k0 = _key_start(i, off)
            outs = []
            for h in range(HEADS_PER_STEP):
                hs = slice(h * HEAD_DIM, (h + 1) * HEAD_DIM)
                p, inv, _ = _attn_exp(q_ref, k_ref, b_ref, h, k0, off)
                outs.append(jnp.dot(p.astype(BF16), v_ref[pl.ds(k0, K_SPAN - off), hs],
                                    preferred_element_type=F32) * inv)
            o_ref[...] = jnp.concatenate(outs, axis=1).astype(BF16)

        _span_cases(i, block)

    return pl.pallas_call(
        body, name="attn_fwd", grid=(HD // hw, B, S // Q_BLOCK), in_specs=[qspec, kspec, vspec, bspec],
        out_specs=qspec, out_shape=jax.ShapeDtypeStruct((B, S, HD), BF16),
        compiler_params=_params(("arbitrary", "arbitrary", "arbitrary")))(q, kv, kv, bias)


def _attn_bwd(q, kv, bias, do, B, S):
    HD = q.shape[-1]
    H = HD // HEAD_DIM
    hw, qspec, kspec, vspec, bspec = _attn_specs(S)
    scale = HEAD_DIM ** -0.5
    nq = S // Q_BLOCK

    def body(q_ref, k_ref, v_ref, b_ref, do_ref, dq_ref, dkv_ref, db_ref, dk_acc, dv_acc):
        b, i = pl.program_id(1), pl.program_id(2)

        @pl.when(i == 0)
        def _():
            dk_acc[...] = jnp.zeros_like(dk_acc)
            dv_acc[...] = jnp.zeros_like(dv_acc)

        @pl.when((i == 0) & (b == 0))
        def _():
            db_ref[...] = jnp.zeros_like(db_ref)

        def block(off):
            k0 = _key_start(i, off)
            keys = pl.ds(k0, K_SPAN - off)
            for h in range(HEADS_PER_STEP):
                hs = slice(h * HEAD_DIM, (h + 1) * HEAD_DIM)
                p, inv, kh = _attn_exp(q_ref, k_ref, b_ref, h, k0, off)
                p = p * inv
                doh = do_ref[:, hs]
                dp = lax.dot_general(doh, v_ref[keys, hs], (((1,), (1,)), ((), ())), preferred_element_type=F32)
                ds = p * (dp - jnp.sum(p * dp, axis=-1, keepdims=True))
                db_ref[h, :, off:] += ds
                dsb = ds.astype(BF16)
                dq_ref[:, hs] = (jnp.dot(dsb, kh, preferred_element_type=F32) * scale).astype(BF16)
                dk_acc[hs, keys] += lax.dot_general(q_ref[:, hs], dsb, (((0,), (0,)), ((), ())),
                                                     preferred_element_type=F32)
                dv_acc[hs, keys] += lax.dot_general(doh, p.astype(BF16), (((0,), (0,)), ((), ())),
                                                     preferred_element_type=F32)

        _span_cases(i, block)

        @pl.when(i == nq - 1)
        def _():
            dkv_ref[0] = dk_acc[...].T.astype(BF16)
            dkv_ref[1] = dv_acc[...].T.astype(BF16)

    return pl.pallas_call(
        body, name="attn_bwd", grid=(HD // hw, B, nq), in_specs=[qspec, kspec, vspec, bspec, qspec],
        out_specs=[qspec, pl.BlockSpec((2, None, S, hw), lambda g, b, i: (0, b, 0, g)), bspec],
        out_shape=[jax.ShapeDtypeStruct((B, S, HD), BF16), jax.ShapeDtypeStruct((2, B, S, HD), BF16),
                   jax.ShapeDtypeStruct((H, Q_BLOCK, K_SPAN), F32)],
        scratch_shapes=[pltpu.VMEM((hw, S), F32), pltpu.VMEM((hw, S), F32)],
        compiler_params=_params(("arbitrary", "arbitrary", "arbitrary")))(q, kv, kv, bias, do)


def _sub_rows(R):
    for cand in (256, 352, 128, 64, 8):
        if R % cand == 0 and R > cand:
            return cand
    return R


def _adamw(w, g, m, v, *, name):
    R, C = w.shape
    tr = _sub_rows(R)

    def body(w_ref, g_ref, m_ref, v_ref, d_ref, nm_ref, nv_ref):
        g = g_ref[...]
        m = ADAM_B1 * m_ref[...] + (1.0 - ADAM_B1) * g
        v = ADAM_B2 * v_ref[...] + (1.0 - ADAM_B2) * (g * g)
        m_hat = m / (1.0 - ADAM_B1 ** ADAM_STEP)
        v_hat = v / (1.0 - ADAM_B2 ** ADAM_STEP)
        d_ref[...] = -ADAM_LR * (m_hat / (jnp.sqrt(v_hat) + ADAM_EPS) + ADAM_WD * w_ref[...])
        nm_ref[...] = m
        nv_ref[...] = v

    spec = pl.BlockSpec((tr, C), lambda i: (i, 0))
    return pl.pallas_call(body, name=name, grid=(R // tr,), in_specs=[spec] * 4, out_specs=[spec] * 3,
                          out_shape=[jax.ShapeDtypeStruct((R, C), F32)] * 3,
                          compiler_params=_params(("arbitrary",)))(w, g, m, v)


def _add_pair(units, got, core, *, name):
    n4, R, C = got.shape
    rows = n4 * R
    tr = 512 if rows % 512 == 0 else R

    def body(c_ref, u_ref, got_ref, o_ref):
        o_ref[...] = (u_ref[...].astype(F32) + got_ref[...].astype(F32)).astype(BF16)

    spec = pl.BlockSpec((tr, C), lambda i, c: (i, 0))
    grid_spec = pltpu.PrefetchScalarGridSpec(
        num_scalar_prefetch=1, grid=(rows // tr,),
        in_specs=[pl.BlockSpec((None, tr, C), lambda i, c: (c[0], i, 0)), spec], out_specs=spec)
    out = pl.pallas_call(body, name=name, grid_spec=grid_spec, out_shape=jax.ShapeDtypeStruct((rows, C), BF16),
                         compiler_params=_params(("arbitrary",)))(core.reshape(1), units.reshape(2, rows, C),
                                                                   got.reshape(rows, C))
    return out.reshape(n4, R, C)


def _sum_chips(w, own, got, pos, *, name, layer=0, into=None):
    _, R, C = own.shape
    tr = _sub_rows(R)
    nr = R // tr

    def body(p_ref, own_ref, got_ref, *rest):
        o_ref = rest[-1]
        o_ref[...] = (own_ref[...].astype(F32) + got_ref[0].astype(F32) + got_ref[1].astype(F32)
                      + got_ref[2].astype(F32))

    if w.row_sharded:
        out_map = lambda i, p: (layer, i, p[1])
    else:
        out_map = lambda i, p: (layer, p[1] * nr + i, 0)
    ins = [pos, own, got]
    in_specs = [pl.BlockSpec((None, tr, C), lambda i, p: (p[0], i, 0)),
                pl.BlockSpec((3, tr, C), lambda i, p: (0, i, 0))]
    alias = {}
    if into is not None:
        ins.append(into)
        in_specs.append(ANY)
        alias = {3: 0}
    grid_spec = pltpu.PrefetchScalarGridSpec(num_scalar_prefetch=1, grid=(nr,), in_specs=in_specs,
                                             out_specs=pl.BlockSpec((None, tr, C), out_map))
    return pl.pallas_call(body, name=name, grid_spec=grid_spec, input_output_aliases=alias,
                          out_shape=jax.ShapeDtypeStruct((w.L, w.ks, w.ns), F32),
                          compiler_params=_params(("arbitrary",)))(*ins)


def _mesh_pos():
    return lax.axis_index("x"), lax.axis_index("y"), lax.axis_index("c")


def _other_chips(x, y):
    return [(1 - x, y), (x, 1 - y), (1 - x, 1 - y)]


ANY = pl.BlockSpec(memory_space=pl.ANY)


class _W:
    def __init__(self, name, shard, row_sharded, direct=False):
        self.name = name
        self.direct = direct
        self.L, ks, ns = shard.shape
        self.row_sharded = row_sharded
        self.K, self.N = (ks * N_CHIPS, ns) if row_sharded else (ks, ns * N_CHIPS)
        self.ks, self.ns = ks, ns

    def shard_of(self, full, j):
        if self.row_sharded:
            return full.at[:, pl.ds(j * self.ks, self.ks), :]
        return full.at[:, :, pl.ds(j * self.ns, self.ns)]

    def half_of(self, shard, c):
        if self.row_sharded:
            return shard.at[:, :, pl.ds(c * (self.ns // 2), self.ns // 2)]
        return shard.at[:, pl.ds(c * (self.ks // 2), self.ks // 2), :]


HBM = pl.BlockSpec(memory_space=pltpu.HBM)
SEM = pl.BlockSpec(memory_space=pltpu.SEMAPHORE)
IN_FLIGHT = pltpu.SideEffectType.DATAFLOW_SIDE_EFFECTING


def _in_hbm(a):
    return pltpu.with_memory_space_constraint(a, pltpu.HBM)


def _gather_start(ws, shards, after, *, name):
    nw = len(ws)

    def body(*refs):
        src, dst = refs[:nw], refs[nw:2 * nw]
        send, recv = refs[2 * nw + 1:3 * nw + 1], refs[3 * nw + 1:4 * nw + 1]
        x, y, c = _mesh_pos()
        me = 2 * x + y
        for i, w in enumerate(ws):
            for f, (px, py) in enumerate(_other_chips(x, y)):
                for e in range(2 if w.direct else 1):
                    k = 2 * f + e
                    pltpu.make_async_remote_copy(
                        src_ref=w.half_of(src[i], c), dst_ref=w.half_of(w.shard_of(dst[i], me), c),
                        send_sem=send[i].at[k], recv_sem=recv[i].at[k], device_id=(px, py, c if e == 0 else 1 - c),
                        device_id_type=MESH).start()

    fulls = [lax.empty((w.L, w.K, w.N), BF16) for w in ws]
    out = pl.pallas_call(
        body, name=name, in_specs=[HBM] * (2 * nw) + [ANY],
        out_specs=[SEM] * (2 * nw) + [HBM] * (2 * nw),
        out_shape=[pltpu.SemaphoreType.DMA((6,))] * (2 * nw)
        + [pltpu.HBM(s.shape, BF16) for s in shards] + [pltpu.HBM(f.shape, BF16) for f in fulls],
        input_output_aliases={i: 2 * nw + i for i in range(2 * nw)},
        compiler_params=pltpu.CompilerParams(has_side_effects=IN_FLIGHT))(
            *[_in_hbm(s) for s in shards], *[_in_hbm(f) for f in fulls], after)
    return [(out[i], out[nw + i], out[2 * nw + i], out[3 * nw + i]) for i in range(nw)]


def _gather_wait(ws, flight, after, *, name):
    nw = len(ws)

    def body(*refs):
        src, dst = refs[:nw], refs[nw:2 * nw]
        send, recv = refs[2 * nw:3 * nw], refs[3 * nw:4 * nw]
        x, y, c = _mesh_pos()
        for i, w in enumerate(ws):
            for f, (px, py) in enumerate(_other_chips(x, y)):
                for e in range(2 if w.direct else 1):
                    k = 2 * f + e
                    landed = w.half_of(w.shard_of(dst[i], 2 * px + py), c if e == 0 else 1 - c)
                    cp = pltpu.make_async_remote_copy(
                        src_ref=w.half_of(src[i], c), dst_ref=landed, send_sem=send[i].at[k], recv_sem=recv[i].at[k],
                        device_id=(px, py, c), device_id_type=MESH)
                    cp.wait_send()
                    cp.wait_recv()

    shards, fulls = [fl[2] for fl in flight], [fl[3] for fl in flight]
    out = pl.pallas_call(
        body, name=name, in_specs=[HBM] * (2 * nw) + [SEM] * (2 * nw) + [ANY],
        out_specs=[HBM] * (2 * nw),
        out_shape=[pltpu.HBM(s.shape, BF16) for s in shards] + [pltpu.HBM(f.shape, BF16) for f in fulls],
        input_output_aliases={i: i for i in range(2 * nw)},
        compiler_params=pltpu.CompilerParams(has_side_effects=IN_FLIGHT))(
            *shards, *fulls, *[fl[0] for fl in flight], *[fl[1] for fl in flight], after)
    return out[:nw], out[nw:]


def _gather_finish(ws, shards, fulls, *, name):
    nw = len(ws)
    forward = not ws[0].direct

    def body(*refs):
        src, dst, stage = refs[:nw], refs[3 * nw:4 * nw], refs[4 * nw:5 * nw]
        send_sems, recv_sems, load_sems, store_sems = refs[5 * nw:]
        x, y, c = _mesh_pos()
        me = 2 * x + y
        sibling = (x, y, 1 - c)
        chips = _other_chips(x, y)

        def fwd(i, w, f, half):
            px, py = chips[f]
            landed = w.half_of(w.shard_of(dst[i], 2 * px + py), half)
            return pltpu.make_async_remote_copy(src_ref=landed, dst_ref=landed, send_sem=send_sems.at[3 * i + f],
                                                recv_sem=recv_sems.at[3 * i + f], device_id=sibling,
                                                device_id_type=MESH)

        loads = [pltpu.make_async_copy(src[i], stage[i], load_sems.at[i]) for i in range(nw)]
        for cp in loads:
            cp.start()
        sends = [fwd(i, w, f, c) for i, w in enumerate(ws) for f in range(3)] if forward else []
        for cp in sends:
            cp.start()
        stores = [pltpu.make_async_copy(stage[i], w.shard_of(dst[i], me), store_sems.at[i])
                  for i, w in enumerate(ws)]
        for ld, st in zip(loads, stores):
            ld.wait()
            st.start()
        if forward:
            for i, w in enumerate(ws):
                for f in range(3):
                    fwd(i, w, f, 1 - c).wait_recv()
        for cp in sends:
            cp.wait_send()
        for cp in stores:
            cp.wait()

    out = pl.pallas_call(
        body, name=name, in_specs=[ANY] * (2 * nw), out_specs=[ANY] * (2 * nw),
        out_shape=[jax.ShapeDtypeStruct(s.shape, BF16) for s in shards]
        + [jax.ShapeDtypeStruct(f.shape, BF16) for f in fulls],
        input_output_aliases={i: i for i in range(2 * nw)},
        scratch_shapes=[pltpu.VMEM((w.L, w.ks, w.ns), BF16) for w in ws]
        + [pltpu.SemaphoreType.DMA((3 * nw,)), pltpu.SemaphoreType.DMA((3 * nw,)), pltpu.SemaphoreType.DMA((nw,)),
           pltpu.SemaphoreType.DMA((nw,))],
        compiler_params=_params(has_side_effects=True))(*shards, *fulls)
    return out[nw:]


def _split_copies(name, srcs, lands, n_sems, copies_of, *, flight=None, after=None):
    n = len(srcs)
    starting = flight is None

    def body(*refs):
        src, land = refs[:n], refs[n:2 * n]
        sems = refs[2 * n + 1:4 * n + 1] if starting else refs[2 * n:4 * n]
        for i in range(n):
            for cp in copies_of(i, src[i], land[i], sems[i], sems[n + i]):
                if starting:
                    cp.start()
                else:
                    cp.wait_send()
                    cp.wait_recv()

    thru = [pltpu.HBM(a.shape, a.dtype) for a in list(srcs) + list(lands)]
    if starting:
        out = pl.pallas_call(
            body, name=name, in_specs=[HBM] * (2 * n) + [ANY], out_specs=[SEM] * (2 * n) + [HBM] * (2 * n),
            out_shape=[pltpu.SemaphoreType.DMA((n_sems,))] * (2 * n) + thru,
            input_output_aliases={i: 2 * n + i for i in range(2 * n)},
            compiler_params=pltpu.CompilerParams(has_side_effects=IN_FLIGHT))(
                *[_in_hbm(a) for a in srcs], *[_in_hbm(a) for a in lands], after)
        return [(out[i], out[n + i], out[2 * n + i], out[3 * n + i]) for i in range(n)]
    out = pl.pallas_call(
        body, name=name, in_specs=[HBM] * (2 * n) + [SEM] * (2 * n) + [ANY], out_specs=[HBM] * (2 * n),
        out_shape=thru, input_output_aliases={i: i for i in range(2 * n)},
        compiler_params=pltpu.CompilerParams(has_side_effects=IN_FLIGHT))(
            *srcs, *lands, *[fl[0] for fl in flight], *[fl[1] for fl in flight], after)
    return out[:n], out[n:]


def _sum8(land, vec, me):
    R = vec.shape[0]

    def body(me_ref, land_ref, vec_ref, o_ref):
        acc = jnp.zeros((R, 128), F32)
        for d in range(8):
            acc = acc + jnp.where(me_ref[0] == d, vec_ref[...], land_ref[d])
        o_ref[...] = acc

    grid_spec = pltpu.PrefetchScalarGridSpec(
        num_scalar_prefetch=1, grid=(1,),
        in_specs=[pl.BlockSpec((8, R, 128), lambda i, m: (0, 0, 0)), pl.BlockSpec((R, 128), lambda i, m: (0, 0))],
        out_specs=pl.BlockSpec((R, 128), lambda i, m: (0, 0)))
    return pl.pallas_call(body, name="sum8", grid_spec=grid_spec, out_shape=jax.ShapeDtypeStruct((R, 128), F32),
                          compiler_params=_params(("arbitrary",)))(me.reshape(1), land, vec)


def _swap_copies(i, src, got, send, recv):
    x, y, c = _mesh_pos()
    return [pltpu.make_async_remote_copy(src_ref=src.at[1 - c], dst_ref=got, send_sem=send.at[0], recv_sem=recv.at[0],
                                         device_id=(x, y, 1 - c), device_id_type=MESH)]


def _gather8_copies(i, src, land, send, recv):
    x, y, c = _mesh_pos()
    me = 4 * x + 2 * y + c
    peers = [(x, y, 1 - c)] + [(px, py, pc) for px, py in _other_chips(x, y) for pc in (c, 1 - c)]
    return [pltpu.make_async_remote_copy(src_ref=src, dst_ref=land.at[me], send_sem=send.at[k], recv_sem=recv.at[k],
                                         device_id=peer, device_id_type=MESH) for k, peer in enumerate(peers)]


def _scatter_copy(src, got, send, recv, f, chip, c):
    px, py = chip
    return pltpu.make_async_remote_copy(src_ref=src.at[2 * px + py], dst_ref=got.at[f], send_sem=send.at[f],
                                        recv_sem=recv.at[f], device_id=(px, py, c), device_id_type=MESH)


def _scatter_start(sums, *, name):
    nw = len(sums)

    def body(*refs):
        src, got = refs[:nw], refs[nw:2 * nw]
        send, recv = refs[2 * nw:3 * nw], refs[3 * nw:4 * nw]
        x, y, c = _mesh_pos()
        for i in range(nw):
            for f, chip in enumerate(_other_chips(x, y)):
                _scatter_copy(src[i], got[i], send[i], recv[i], f, chip, c).start()

    lands = [lax.empty((3,) + s.shape[1:], BF16) for s in sums]
    out = pl.pallas_call(
        body, name=name, in_specs=[HBM] * (2 * nw), out_specs=[SEM] * (2 * nw) + [HBM] * (2 * nw),
        out_shape=[pltpu.SemaphoreType.DMA((3,))] * (2 * nw)
        + [pltpu.HBM(s.shape, BF16) for s in sums] + [pltpu.HBM(l.shape, BF16) for l in lands],
        input_output_aliases={i: 2 * nw + i for i in range(2 * nw)},
        compiler_params=pltpu.CompilerParams(has_side_effects=IN_FLIGHT))(
            *[_in_hbm(s) for s in sums], *[_in_hbm(l) for l in lands])
    return [(out[i], out[nw + i], out[2 * nw + i], out[3 * nw + i]) for i in range(nw)]


def _scatter_wait(flight, after):
    nw = len(flight)

    def body(*refs):
        src, got = refs[:nw], refs[nw:2 * nw]
        send, recv = refs[2 * nw:3 * nw], refs[3 * nw:4 * nw]
        x, y, c = _mesh_pos()
        for i in range(nw):
            for f, chip in enumerate(_other_chips(x, y)):
                cp = _scatter_copy(src[i], got[i], send[i], recv[i], f, chip, c)
                cp.wait_send()
                cp.wait_recv()

    sums, lands = [fl[2] for fl in flight], [fl[3] for fl in flight]
    out = pl.pallas_call(
        body, name="scatter_wait", in_specs=[HBM] * (2 * nw) + [SEM] * (2 * nw) + [ANY], out_specs=[HBM] * (2 * nw),
        out_shape=[pltpu.HBM(s.shape, BF16) for s in sums] + [pltpu.HBM(l.shape, BF16) for l in lands],
        input_output_aliases={i: i for i in range(2 * nw)},
        compiler_params=pltpu.CompilerParams(has_side_effects=IN_FLIGHT))(
            *sums, *lands, *[fl[0] for fl in flight], *[fl[1] for fl in flight], after)
    return out[:nw], out[nw:]


def _join_halves(ws, shards):
    nw = len(ws)

    def body(*refs):
        buf = refs[nw:2 * nw]
        send_sems, recv_sems = refs[2 * nw:]
        x, y, c = _mesh_pos()
        sibling = (x, y, 1 - c)

        def copy(i, w, half):
            region = w.half_of(buf[i], half)
            return pltpu.make_async_remote_copy(src_ref=region, dst_ref=region, send_sem=send_sems.at[i],
                                                recv_sem=recv_sems.at[i], device_id=sibling, device_id_type=MESH)

        sends = [copy(i, w, c) for i, w in enumerate(ws)]
        for cp in sends:
            cp.start()
        for i, w in enumerate(ws):
            copy(i, w, 1 - c).wait_recv()
        for cp in sends:
            cp.wait_send()

    return pl.pallas_call(
        body, name="join_halves", in_specs=[ANY] * nw, out_specs=[ANY] * nw,
        out_shape=[jax.ShapeDtypeStruct((w.L, w.ks, w.ns), F32) for w in ws],
        input_output_aliases={i: i for i in range(nw)},
        scratch_shapes=[pltpu.SemaphoreType.DMA((nw,)), pltpu.SemaphoreType.DMA((nw,))],
        compiler_params=_params(has_side_effects=True))(*shards)


def _allreduce_small(vec):
    R = vec.shape[0]

    def body(x_ref, o_ref, buf, send_sems, recv_sems):
        x, y, c = _mesh_pos()
        me, sibling = (x, y, c), (x, y, 1 - c)
        chips = _other_chips(x, y)

        def slot(px, py, pc):
            return buf.at[4 * px + 2 * py + pc]

        def copy(k, block, to, src=None):
            return pltpu.make_async_remote_copy(src_ref=slot(*block) if src is None else src, dst_ref=slot(*block),
                                                send_sem=send_sems.at[k], recv_sem=recv_sems.at[k], device_id=to,
                                                device_id_type=MESH)

        first = [copy(0, me, sibling, src=x_ref)] + [copy(1 + f, me, (*chip, c), src=x_ref)
                                                     for f, chip in enumerate(chips)]
        for cp in first:
            cp.start()
        passed = [copy(4 + f, (*chip, c), sibling) for f, chip in enumerate(chips)]
        for f, chip in enumerate(chips):
            copy(1 + f, (*chip, c), me).wait_recv()
            passed[f].start()
        copy(0, sibling, me).wait_recv()
        for f, chip in enumerate(chips):
            copy(4 + f, (*chip, 1 - c), me).wait_recv()
        for cp in first + passed:
            cp.wait_send()
        slot(*me)[...] = x_ref[...]
        acc = buf[0]
        for d in range(1, 8):
            acc = acc + buf[d]
        o_ref[...] = acc

    return pl.pallas_call(
        body, name="allreduce_small", in_specs=[pl.BlockSpec(memory_space=pltpu.VMEM)],
        out_specs=pl.BlockSpec(memory_space=pltpu.VMEM), out_shape=jax.ShapeDtypeStruct((R, 128), F32),
        scratch_shapes=[pltpu.VMEM((8, R, 128), F32), pltpu.SemaphoreType.DMA((7,)), pltpu.SemaphoreType.DMA((7,))],
        compiler_params=_params())(vec)


def _pack(parts):
    flat = jnp.concatenate([p.reshape(-1).astype(F32) for p in parts])
    n = flat.shape[0]
    pad = (-n) % (64 * 128)
    return jnp.pad(flat, (0, pad)).reshape(-1, 128)


def _unpack(vec, shapes):
    flat = vec.reshape(-1)
    out, off = [], 0
    for s in shapes:
        n = int(np.prod(s))
        out.append(flat[off:off + n].reshape(s))
        off += n
    return out


def kernel(x, a_norm_g, a_w_in, a_v_norm_g, a_w_s, a_b_s, a_w_out, kv_norm_g, w_kv, b_norm_g, b_w_q, b_rel_bias, b_w_o, f_norm_g, f_w_in, f_conv_w, f_conv_b, f_w_down, final_norm_g, loss_target, m_a_norm_g, m_a_w_in, m_a_v_norm_g, m_a_w_s, m_a_b_s, m_a_w_out, m_kv_norm_g, m_w_kv, m_b_norm_g, m_b_w_q, m_b_rel_bias, m_b_w_o, m_f_norm_g, m_f_w_in, m_f_conv_w, m_f_conv_b, m_f_w_down, m_final_norm_g, v_a_norm_g, v_a_w_in, v_a_v_norm_g, v_a_w_s, v_a_b_s, v_a_w_out, v_kv_norm_g, v_w_kv, v_b_norm_g, v_b_w_q, v_b_rel_bias, v_b_w_o, v_f_norm_g, v_f_w_in, v_f_conv_w, v_f_conv_b, v_f_w_down, v_final_norm_g):
    B, S, D = x.shape
    T = B * S
    xi, yi, ci = lax.axis_index("x"), lax.axis_index("y"), lax.axis_index("c")
    j_me = (2 * xi + yi).astype(jnp.int32)
    core = ci.astype(jnp.int32)
    pos = jnp.stack([j_me, core])

    w_shards = {"a_w_in": (a_w_in, False), "a_w_out": (a_w_out, True), "w_kv": (w_kv[None], False),
                "b_w_q": (b_w_q, True), "b_w_o": (b_w_o, True), "f_w_in": (f_w_in, False), "f_w_down": (f_w_down, True)}
    names = list(w_shards)
    ws = [_W(n, w_shards[n][0], w_shards[n][1]) for n in names]
    g_shards = {"a_w_in": (a_w_in, False), "a_w_out": (a_w_out, True),
                "f_w_in0": (f_w_in[0:1], False), "f_w_down0": (f_w_down[0:1], True),
                "w_kv": (w_kv[None], False), "b_w_q": (b_w_q, True), "b_w_o": (b_w_o, True),
                "f_w_in1": (f_w_in[1:2], False), "f_w_down1": (f_w_down[1:2], True)}
    g_names = list(g_shards)
    g_ws = {n: _W(n, *g_shards[n], direct=n in ("w_kv", "b_w_q", "b_w_o", "f_w_in1", "f_w_down1")) for n in g_names}

    Wd = a_w_in.shape[1]
    GW = a_v_norm_g.shape[1] * N_CHIPS
    F2 = f_conv_w.shape[2] * N_CHIPS
    Fh = F2 // 2
    nsd, nsg, nsf = a_norm_g.shape[1], a_v_norm_g.shape[1], f_conv_w.shape[2]
    own = (ci == 0).astype(F32)
    place = lambda sh, width, n: lax.dynamic_update_slice_in_dim(
        jnp.zeros(sh.shape[:-1] + (width,), F32), sh * own, j_me * n, axis=sh.ndim - 1)
    def tied(x, flight):
        x, thru = lax.optimization_barrier((x, flight[0][2]))
        return x, [flight[0][:2] + (thru,) + flight[0][3:]] + flight[1:]

    gathered = _allreduce_small(_pack([place(a_norm_g, Wd, nsd), place(a_v_norm_g, GW, nsg),
                                       place(f_conv_w, F2, nsf)]))
    a_g, a_vg, conv_w = _unpack(gathered, [(1, Wd), (1, GW), (2, 3, F2)])
    flight = dict(zip(g_names, _gather_start([g_ws[n] for n in g_names],
                                             [g_shards[n][0].astype(BF16) for n in g_names], gathered,
                                             name="gather_start")))
    full = {}

    def arrive(group, after, tag):
        gw = [g_ws[n] for n in group]
        sh, fu = _gather_wait(gw, [flight[n] for n in group], after, name=f"gather_wait_{tag}")
        full.update(zip(group, _gather_finish(gw, sh, fu, name=f"gather_finish_{tag}")))
    conv_w2 = conv_w.reshape(2, 3, 2, Fh).transpose(0, 2, 1, 3)
    conv_b2 = f_conv_b.reshape(2, 2, Fh)

    h0 = x.reshape(T, D)
    target = loss_target.reshape(T, D)
    bs_tile = jnp.repeat(a_b_s[0].T, GROUP_DIM, axis=1)
    ws_a = a_w_s[0]
    scale = HEAD_DIM ** -0.5
    HD = b_w_q.shape[2]
    H = HD // HEAD_DIM
    n_rel = b_rel_bias.shape[-1]
    frow, (flight["a_w_in"],) = tied(b_rel_bias[0][:, _bias_index()].reshape(H, 1, F_LEN), [flight["a_w_in"]])
    bias = _bias_expand(frow)

    def ffn_fwd(h, l, loss=None):
        out = _ffn_fwd(h, full[f"f_w_in{l}"], f_norm_g[l], conv_w2[l], conv_b2[l], full[f"f_w_down{l}"], S,
                       loss=loss, name=f"ffn{l}")
        yff, a, c, n = out[1:5]
        return (out[0] if loss is None else (out[0], out[5], out[6])), (a, c, n, yff)

    arrive(["a_w_in", "a_w_out"], bias, "a")
    zp, n_a = _mm(h0, full["a_w_in"], layer=0, norm_g=a_g[0], out_dtype=BF16, emit_norm=True, name="a_in")
    out_a = _gate_fwd(zp, a_vg, ws_a, bs_tile)
    h1 = _mm(out_a, full["a_w_out"], layer=0, res=h0, name="a_out")
    arrive(["f_w_in0", "f_w_down0"], h1, "f0")
    h2, saved0 = ffn_fwd(h1, 0)
    arrive(["w_kv", "b_w_q", "b_w_o"], h2, "b")
    arrive(["f_w_in1", "f_w_down1"], h2, "f1")
    kv, n_kv = _mm(h2, full["w_kv"], layer=0, norm_g=kv_norm_g, out_dtype=BF16, split_out=True, emit_norm=True,
                   name="kv")
    q, n_q = _mm(h2, full["b_w_q"], layer=0, norm_g=b_norm_g[0], scale=scale, out_dtype=BF16, emit_norm=True,
                 name="q")
    kv4, q3 = kv.reshape(2, B, S, HD), q.reshape(B, S, HD)
    o = _attn_fwd(q3, kv4, bias, B, S).reshape(T, HD)
    h3 = _mm(o, full["b_w_o"], layer=0, res=h2, name="attn_out")
    (dh, loss8, dg_final), saved1 = ffn_fwd(h3, 1, loss=(final_norm_g, target))

    units = {}

    in_flight = {}

    def swap_start(group, tag, carry):
        us = [units[n] for n in group]
        lands = [lax.empty(u.shape[1:], BF16) for u in us]
        carry, flight = tied(carry, _split_copies(f"swap_start_{tag}", us, lands, 1, _swap_copies, after=carry))
        return (group, tag, flight), carry

    def reduce_start(swap, after):
        group, tag, flight = swap
        us, got = _split_copies(f"swap_wait_{tag}", [fl[2] for fl in flight], [fl[3] for fl in flight], 1,
                                _swap_copies, flight=flight, after=after)
        sums = [_add_pair(u, g_, core, name=f"pair_{n}") for n, u, g_ in zip(group, us, got)]
        after, flight = tied(after, _scatter_start(sums, name=f"scatter_start_{tag}"))
        in_flight.update(zip(group, flight))
        return after

    def ffn_bwd(dh, h, saved, l, early):
        a, c, n, yff = saved
        units[f"f_w_down{l}"] = _mm_tn(yff, dh, rows_are_shards=True, name=f"ffn{l}_down_dw")
        dh_in = dh
        if early:
            sw, dh_in = swap_start([f"f_w_down{l}"], f"fd{l}", dh)
        dyff = _mm(dh_in, full[f"f_w_down{l}"], layer=0, trans_w=True, out_dtype=BF16, name=f"ffn{l}_down_dx")
        if early:
            dyff = reduce_start(sw, dyff)
        da, dcw, dcb = _conv_bwd(a, c, dyff, conv_w2[l], S)
        units[f"f_w_in{l}"] = _mm_tn(n, da, split_y=True, name=f"ffn{l}_in_dw")
        sw, da = swap_start([f"f_w_in{l}"] if early else [f"f_w_down{l}", f"f_w_in{l}"], f"f{l}", da)
        dh, dg = _mm(da, full[f"f_w_in{l}"], layer=0, trans_w=True, split_x=True, bwd=(h, f_norm_g[l], dh),
                     name=f"ffn{l}_in_dx")
        return reduce_start(sw, dh), dg, dcw, dcb

    dh, dg_f1, dcw1, dcb1 = ffn_bwd(dh, h3, saved1, 1, False)
    do = _mm(dh, full["b_w_o"], layer=0, trans_w=True, out_dtype=BF16, name="attn_out_dx")
    units["b_w_o"] = _mm_tn(o, dh, rows_are_shards=True, name="b_w_o_dw")
    dq, dkv, dbias = _attn_bwd(q3, kv4, bias, do.reshape(B, S, HD), B, S)
    dq, d_rel = lax.optimization_barrier((dq, _bias_reduce(dbias, n_rel)))
    d_rel = d_rel.reshape(1, H, n_rel)
    dq, dkv = dq.reshape(T, HD), dkv.reshape(2, T, HD)
    units["b_w_q"] = _mm_tn(n_q, dq, rows_are_shards=True, name="b_w_q_dw")
    dh, dg_b = _mm(dq, full["b_w_q"], layer=0, trans_w=True, bwd=(h2, b_norm_g[0], dh), name="q_dx")
    units["w_kv"] = _mm_tn(n_kv, dkv, split_y=True, name="w_kv_dw")
    sw, dkv = swap_start(["b_w_o", "b_w_q", "w_kv"], "b", dkv)
    dh, dg_kv = _mm(dkv, full["w_kv"], layer=0, trans_w=True, split_x=True, bwd=(h2, kv_norm_g, dh), name="kv_dx")
    dh = reduce_start(sw, dh)
    dh, dg_f0, dcw0, dcb0 = ffn_bwd(dh, h1, saved0, 0, True)
    units["a_w_out"] = _mm_tn(out_a, dh, rows_are_shards=True, name="a_w_out_dw")
    sw, dh_in = swap_start(["a_w_out"], "ao", dh)
    d_out = _mm(dh_in, full["a_w_out"], layer=0, trans_w=True, out_dtype=BF16, name="a_out_dx")
    d_out = reduce_start(sw, d_out)
    dzp, dws, dbs, dgv = _gate_bwd(zp, d_out, a_vg, ws_a, bs_tile)
    units["a_w_in"] = _mm_tn(n_a, dzp, name="a_w_in_dw")
    sw, dzp_in = swap_start(["a_w_in"], "ai", dzp)
    grad_x, dg_a = _mm(dzp_in, full["a_w_in"], layer=0, trans_w=True, bwd=(h0, a_g[0], dh), name="a_in_dx")
    grad_x = reduce_start(sw, grad_x)

    to_flat = lambda d: d.transpose(1, 0, 2).reshape(3, F2)
    small = {"a_norm_g": dg_a, "a_v_norm_g": dgv, "a_w_s": dws[None], "a_b_s": dbs[None], "kv_norm_g": dg_kv[0],
             "b_norm_g": dg_b, "b_rel_bias": d_rel, "f_norm_g": jnp.concatenate([dg_f0, dg_f1], axis=0),
             "f_conv_w": jnp.stack([to_flat(dcw0), to_flat(dcw1)]),
             "f_conv_b": jnp.stack([dcb0.reshape(F2), dcb1.reshape(F2)]), "final_norm_g": dg_final[0]}
    snames = list(small)
    small_vec = _pack([small[n] for n in snames] + [loss8[0:1, 0:1]])
    grad_x, small_flight = tied(grad_x, _split_copies("small_start", [small_vec],
                                                      [lax.empty((8,) + small_vec.shape, F32)], 7, _gather8_copies,
                                                      after=grad_x))

    sums, recv = _scatter_wait([in_flight[n] for n in g_names], grad_x)
    sums, recv = dict(zip(g_names, sums)), dict(zip(g_names, recv))
    halves = []
    for n, w in zip(names, ws):
        if w.L == 1:
            halves.append(_sum_chips(w, sums[n], recv[n], pos, name=f"chips_{n}"))
        else:
            first = _sum_chips(w, sums[n + "0"], recv[n + "0"], pos, name=f"chips_{n}0")
            halves.append(_sum_chips(w, sums[n + "1"], recv[n + "1"], pos, layer=1, into=first, name=f"chips_{n}1"))
    g_big = dict(zip(names, _join_halves(ws, halves)))
    g_big["w_kv"] = g_big["w_kv"][0]

    given = dict(a_norm_g=(a_norm_g, m_a_norm_g, v_a_norm_g), a_w_in=(a_w_in, m_a_w_in, v_a_w_in),
                 a_v_norm_g=(a_v_norm_g, m_a_v_norm_g, v_a_v_norm_g), a_w_s=(a_w_s, m_a_w_s, v_a_w_s),
                 a_b_s=(a_b_s, m_a_b_s, v_a_b_s), a_w_out=(a_w_out, m_a_w_out, v_a_w_out),
                 kv_norm_g=(kv_norm_g, m_kv_norm_g, v_kv_norm_g), w_kv=(w_kv, m_w_kv, v_w_kv),
                 b_norm_g=(b_norm_g, m_b_norm_g, v_b_norm_g), b_w_q=(b_w_q, m_b_w_q, v_b_w_q),
                 b_rel_bias=(b_rel_bias, m_b_rel_bias, v_b_rel_bias), b_w_o=(b_w_o, m_b_w_o, v_b_w_o),
                 f_norm_g=(f_norm_g, m_f_norm_g, v_f_norm_g), f_w_in=(f_w_in, m_f_w_in, v_f_w_in),
                 f_conv_w=(f_conv_w, m_f_conv_w, v_f_conv_w), f_conv_b=(f_conv_b, m_f_conv_b, v_f_conv_b),
                 f_w_down=(f_w_down, m_f_w_down, v_f_w_down), final_norm_g=(final_norm_g, m_final_norm_g, v_final_norm_g))
    order = list(given)
    grads, deltas, new_m, new_v = {}, {}, {}, {}
    for n in names:
        w_, m_, v_ = given[n]
        g_ = g_big[n]
        C = w_.shape[-1]
        d2, m2, v2 = _adamw(w_.reshape(-1, C), g_.reshape(-1, C), m_.reshape(-1, C), v_.reshape(-1, C),
                            name=f"adamw_{n}")
        grads[n], deltas[n], new_m[n], new_v[n] = g_.reshape(w_.shape), d2.reshape(w_.shape), m2.reshape(w_.shape), \
            v2.reshape(w_.shape)
    vecs, lands = _split_copies("small_wait", [small_flight[0][2]], [small_flight[0][3]], 7, _gather8_copies,
                                flight=small_flight, after=deltas[names[-1]])
    red = _sum8(lands[0], vecs[0], (4 * xi + 2 * yi + ci).astype(jnp.int32))
    parts = _unpack(red, [small[n].shape for n in snames] + [(1,)])
    g_small = dict(zip(snames, parts[:-1]))
    loss = parts[-1][0]
    g_small["a_norm_g"] = lax.dynamic_slice_in_dim(g_small["a_norm_g"], j_me * nsd, nsd, axis=1)
    g_small["a_v_norm_g"] = lax.dynamic_slice_in_dim(g_small["a_v_norm_g"], j_me * nsg, nsg, axis=1)
    g_small["f_conv_w"] = lax.dynamic_slice_in_dim(g_small["f_conv_w"], j_me * nsf, nsf, axis=2)

    sm = [n for n in order if n not in names]
    d2, m2, v2 = _adamw(_pack([given[n][0] for n in sm]), _pack([g_small[n].reshape(given[n][0].shape) for n in sm]),
                        _pack([given[n][1] for n in sm]), _pack([given[n][2] for n in sm]), name="adamw_small")
    shapes = [given[n][0].shape for n in sm]
    for n, d_, m_, v_ in zip(sm, _unpack(d2, shapes), _unpack(m2, shapes), _unpack(v2, shapes)):
        grads[n], deltas[n], new_m[n], new_v[n] = g_small[n].reshape(given[n][0].shape), d_, m_, v_

    return (loss, grad_x.reshape(B, S, D), *[grads[n] for n in order], *[deltas[n] for n in order],
            *[new_m[n] for n in order], *[new_v[n] for n in order])
```

```python
import functools
import math

import numpy as np
import jax
import jax.numpy as jnp
from jax import lax
from jax.experimental import pallas as pl
from jax.experimental.pallas import tpu as pltpu

F32 = jnp.float32
BF16 = jnp.bfloat16
MESH = pl.DeviceIdType.MESH

EPS = 1e-6
NEG_INF = -1e30
CHUNK = 64
GMLP_BLOCK = 128
GROUP_DIM = 128
HEAD_DIM = 64
LEFT_CHUNKS = 8
PAD = LEFT_CHUNKS * CHUNK
REL_CLIP = 128
Q_BLOCK = 256
K_SPAN = PAD + Q_BLOCK
F_LEN = K_SPAN + Q_BLOCK
HEADS_PER_STEP = 4
N_CHIPS = 4

ADAM_LR = 0.001
ADAM_B1 = 0.9
ADAM_B2 = 0.999
ADAM_EPS = 1e-08
ADAM_WD = 0.01
ADAM_STEP = 10

VMEM_LIMIT = 56 * 1024 * 1024


def _params(sem=None, **kw):
    if sem is not None:
        kw["dimension_semantics"] = sem
    return pltpu.CompilerParams(vmem_limit_bytes=VMEM_LIMIT, **kw)


def _rms(xf):
    r = lax.rsqrt(jnp.mean(xf * xf, axis=-1, keepdims=True) + EPS)
    return xf * r, r


def _gelu(x, with_grad=False):
    c = math.sqrt(2.0 / math.pi)
    x2 = x * x
    t = jnp.tanh(c * x * (1.0 + 0.044715 * x2))
    half = 0.5 * (1.0 + t)
    if not with_grad:
        return x * half
    return x * half, half + 0.5 * x * (1.0 - t * t) * c * (1.0 + 3.0 * 0.044715 * x2)


def _col_tile(n):
    if n <= 1024:
        return n
    for t in (1408, 1024, 512):
        if n % t == 0:
            return t
    raise ValueError(n)


def _row_tile(t, want):
    while t % want:
        want //= 2
    return want


def _loss_epilogue(h, g_ref, t_ref, dh_ref, loss_ref, dg_ref, first):
    @pl.when(first)
    def _():
        loss_ref[...] = jnp.zeros_like(loss_ref)
        dg_ref[...] = jnp.zeros_like(dg_ref)

    n, r = _rms(h)
    g = g_ref[...]
    e = n * g - t_ref[...]
    loss_ref[...] += 0.5 * jnp.sum(jnp.mean(e * e, axis=-1, keepdims=True), axis=0, keepdims=True)
    dy = e * (1.0 / h.shape[-1])
    dg_ref[...] += jnp.sum(dy * n, axis=0, keepdims=True)
    t = dy * g
    dh_ref[...] = r * (t - n * jnp.mean(t * n, axis=-1, keepdims=True))


def _mm(x, w, *, name, layer=None, trans_w=False, norm_g=None, res=None, out_dtype=F32, bwd=None, split_x=False,
        emit_norm=False, tm=512):
    T = x.shape[-2]
    K = 2 * x.shape[-1] if split_x else x.shape[-1]
    N = w.shape[-2] if trans_w else w.shape[-1]
    tn = N
    tm = _row_tile(T, 2 * tm if max(K, N) <= 2048 else tm)
    nn, nm = N // tn, T // tm
    has_norm, has_res, has_bwd = norm_g is not None, res is not None, bwd is not None
    dims = (((1,), (1,)), ((), ())) if trans_w else (((1,), (0,)), ((), ()))

    def body(*refs):
        it = iter(refs)
        x_ref, w_ref = next(it), next(it)
        g_ref = next(it) if has_norm else None
        res_ref = next(it) if has_res else None
        if has_bwd:
            h_ref, bg_ref, dh_ref = next(it), next(it), next(it)
        o_ref = next(it)
        if split_x:
            kh = K // 2
            acc = lax.dot_general(x_ref[0].astype(BF16), w_ref[:, :kh] if trans_w else w_ref[:kh, :], dims,
                                  preferred_element_type=F32)
            acc = acc + lax.dot_general(x_ref[1].astype(BF16), w_ref[:, kh:] if trans_w else w_ref[kh:, :], dims,
                                        preferred_element_type=F32)
        else:
            xv = x_ref[...]
            if has_norm:
                xv = _rms(xv.astype(F32))[0] * g_ref[...]
            xb = xv.astype(BF16)
            if emit_norm:
                refs[-1][...] = xb
            acc = lax.dot_general(xb, w_ref[...], dims, preferred_element_type=F32)
        if has_res:
            acc = acc + res_ref[...]
        if has_bwd:
            dg_ref = next(it)
            n, r = _rms(h_ref[...])

            @pl.when(pl.program_id(1) == 0)
            def _():
                dg_ref[...] = jnp.zeros_like(dg_ref)

            dg_ref[...] += jnp.sum(acc * n, axis=0, keepdims=True)
            t = acc * bg_ref[...]
            o_ref[...] = dh_ref[...] + r * (t - n * jnp.mean(t * n, axis=-1, keepdims=True))
        else:
            o_ref[...] = acc.astype(out_dtype)

    lead = () if layer is None else (None,)
    lidx = () if layer is None else (layer,)
    ins = [x, w]
    xspec = (pl.BlockSpec((2, tm, K // 2), lambda n, m: (0, m, 0)) if split_x
             else pl.BlockSpec((tm, K), lambda n, m: (m, 0)))
    once = pl.Buffered(1)
    wspec = (pl.BlockSpec(lead + (tn, K), lambda n, m: lidx + (n, 0), pipeline_mode=once) if trans_w
             else pl.BlockSpec(lead + (K, tn), lambda n, m: lidx + (0, n), pipeline_mode=once))
    in_specs = [xspec, wspec]
    if has_norm:
        ins.append(norm_g.reshape(1, K))
        in_specs.append(pl.BlockSpec((1, K), lambda n, m: (0, 0)))
    if has_res:
        ins.append(res)
        in_specs.append(pl.BlockSpec((tm, tn), lambda n, m: (m, n)))
    out_shape = [jax.ShapeDtypeStruct((T, N), F32 if has_bwd else out_dtype)]
    out_specs = [pl.BlockSpec((tm, tn), lambda n, m: (m, n))]
    if has_bwd:
        h, g, dh = bwd
        ins += [h, g.reshape(1, N), dh]
        in_specs += [pl.BlockSpec((tm, N), lambda n, m: (m, 0)), pl.BlockSpec((1, N), lambda n, m: (0, 0)),
                     pl.BlockSpec((tm, N), lambda n, m: (m, 0))]
        out_shape.append(jax.ShapeDtypeStruct((1, N), F32))
        out_specs.append(pl.BlockSpec((1, N), lambda n, m: (0, 0)))
    if emit_norm:
        out_shape.append(jax.ShapeDtypeStruct((T, K), BF16))
        out_specs.append(pl.BlockSpec((tm, K), lambda n, m: (m, 0)))
    out = pl.pallas_call(body, name=name, grid=(nn, nm), in_specs=in_specs, out_specs=out_specs, out_shape=out_shape,
                         compiler_params=_params(("arbitrary", "arbitrary")))(*ins)
    return out if has_bwd or emit_norm else out[0]


def _mm_tn(x, dy, *, name, rows_are_shards=False, split_y=False, tt=1024):
    T, K = x.shape
    N = 2 * dy.shape[-1] if split_y else dy.shape[-1]
    R, C = (K // N_CHIPS, N // 2) if rows_are_shards else (K // 2, N // N_CHIPS)
    nn = 2 if split_y else 1
    tn = N // nn
    per = N_CHIPS // nn
    assert not (rows_are_shards and split_y)
    tt = _row_tile(T, tt)
    nt = T // tt

    def body(x_ref, y_ref, o_ref, acc_ref):
        t = pl.program_id(1)

        @pl.when(t == 0)
        def _():
            acc_ref[...] = jnp.zeros_like(acc_ref)

        acc_ref[...] += lax.dot_general(x_ref[...], y_ref[...].astype(BF16), (((0,), (0,)), ((), ())),
                                        preferred_element_type=F32)

        @pl.when(t == nt - 1)
        def _():
            if rows_are_shards:
                for h in range(2):
                    o_ref[h] = acc_ref[:, h * C:(h + 1) * C].astype(BF16).reshape(N_CHIPS, R, C)
            else:
                for j in range(per):
                    o_ref[:, j] = acc_ref[:, j * C:(j + 1) * C].astype(BF16).reshape(2, R, C)

    if split_y:
        yspec = pl.BlockSpec((None, tt, tn), lambda n, t: (n, t, 0))
    else:
        yspec = pl.BlockSpec((tt, tn), lambda n, t: (t, 0))
    if rows_are_shards:
        out_spec = pl.BlockSpec((2, N_CHIPS, R, C), lambda n, t: (0, 0, 0, 0))
    else:
        out_spec = pl.BlockSpec((2, per, R, C), lambda n, t: (0, n, 0, 0))
    return pl.pallas_call(body, name=name, grid=(nn, nt),
                          in_specs=[pl.BlockSpec((tt, K), lambda n, t: (t, 0)), yspec], out_specs=out_spec,
                          out_shape=jax.ShapeDtypeStruct((2, N_CHIPS, R, C), BF16),
                          scratch_shapes=[pltpu.VMEM((K, tn), F32)],
                          compiler_params=_params(("arbitrary", "arbitrary")))(x, dy)


def _qkv_fwd(h, wq, gq, wkv, gkv, scale, *, tm=512):
    T, D = h.shape
    HD = wq.shape[-1]
    tm = _row_tile(T, tm)

    def body(h_ref, wq_ref, gq_ref, wkv_ref, gkv_ref, q_ref, kv_ref, nq_ref, nkv_ref):
        n = _rms(h_ref[...])[0]
        nq = (n * gq_ref[...]).astype(BF16)
        nkv = (n * gkv_ref[...]).astype(BF16)
        nq_ref[...] = nq
        nkv_ref[...] = nkv
        q_ref[...] = (jnp.dot(nq, wq_ref[...], preferred_element_type=F32) * scale).astype(BF16)
        kv = jnp.dot(nkv, wkv_ref[...], preferred_element_type=F32)
        kv_ref[0] = kv[:, :HD].astype(BF16)
        kv_ref[1] = kv[:, HD:].astype(BF16)

    row = lambda width: pl.BlockSpec((tm, width), lambda i: (i, 0))
    fixed = lambda *shape: pl.BlockSpec(shape, lambda i: (0,) * len(shape))
    weight = lambda n: pl.BlockSpec((None, D, n), lambda i: (0, 0, 0), pipeline_mode=pl.Buffered(1))
    return pl.pallas_call(
        body, name="qkv", grid=(T // tm,),
        in_specs=[row(D), weight(HD), fixed(1, D), weight(2 * HD), fixed(1, D)],
        out_specs=[row(HD), pl.BlockSpec((2, tm, HD), lambda i: (0, i, 0)), row(D), row(D)],
        out_shape=[jax.ShapeDtypeStruct((T, HD), BF16), jax.ShapeDtypeStruct((2, T, HD), BF16),
                   jax.ShapeDtypeStruct((T, D), BF16), jax.ShapeDtypeStruct((T, D), BF16)],
        compiler_params=_params(("arbitrary",)))(h, wq, gq.reshape(1, D), wkv, gkv.reshape(1, D))


def _qkv_dx(dq, wq, gq, dkv, wkv, gkv, h, dh, *, tm=512):
    T, D = h.shape
    HD = wq.shape[-1]
    tm = _row_tile(T, tm)
    nt = (((1,), (1,)), ((), ()))

    def body(dq_ref, wq_ref, gq_ref, dkv_ref, wkv_ref, gkv_ref, h_ref, dh_ref, o_ref, dgq_ref, dgkv_ref):
        @pl.when(pl.program_id(0) == 0)
        def _():
            dgq_ref[...] = jnp.zeros_like(dgq_ref)
            dgkv_ref[...] = jnp.zeros_like(dgkv_ref)

        n, r = _rms(h_ref[...])
        dnq = lax.dot_general(dq_ref[...], wq_ref[...], nt, preferred_element_type=F32)
        dnkv = (lax.dot_general(dkv_ref[0], wkv_ref[:, :HD], nt, preferred_element_type=F32)
                + lax.dot_general(dkv_ref[1], wkv_ref[:, HD:], nt, preferred_element_type=F32))
        dgq_ref[...] += jnp.sum(dnq * n, axis=0, keepdims=True)
        dgkv_ref[...] += jnp.sum(dnkv * n, axis=0, keepdims=True)
        t = dnq * gq_ref[...] + dnkv * gkv_ref[...]
        o_ref[...] = dh_ref[...] + r * (t - n * jnp.mean(t * n, axis=-1, keepdims=True))

    row = lambda width: pl.BlockSpec((tm, width), lambda i: (i, 0))
    fixed = lambda *shape: pl.BlockSpec(shape, lambda i: (0,) * len(shape))
    weight = lambda n: pl.BlockSpec((None, D, n), lambda i: (0, 0, 0), pipeline_mode=pl.Buffered(1))
    return pl.pallas_call(
        body, name="qkv_dx", grid=(T // tm,),
        in_specs=[row(HD), weight(HD), fixed(1, D), pl.BlockSpec((2, tm, HD), lambda i: (0, i, 0)), weight(2 * HD),
                  fixed(1, D), row(D), row(D)],
        out_specs=[row(D), fixed(1, D), fixed(1, D)],
        out_shape=[jax.ShapeDtypeStruct((T, D), F32), jax.ShapeDtypeStruct((1, D), F32),
                   jax.ShapeDtypeStruct((1, D), F32)],
        compiler_params=_params(("arbitrary",)))(dq, wq, gq.reshape(1, D), dkv, wkv, gkv.reshape(1, D), h, dh)


def _chunk_mask():
    i = lax.broadcasted_iota(jnp.int32, (GMLP_BLOCK, GMLP_BLOCK), 0) // CHUNK
    j = lax.broadcasted_iota(jnp.int32, (GMLP_BLOCK, GMLP_BLOCK), 1) // CHUNK
    return i >= j


def _gate_fwd(zp, gv, ws, bs_tile, *, tm=256):
    T, W2 = zp.shape
    W = W2 // 2
    G = W // GROUP_DIM
    tm = _row_tile(T, tm)

    def body(zp_ref, gv_ref, ws_ref, bs_ref, o_ref):
        z = _gelu(zp_ref[...].astype(F32))
        u, v = z[:, :W], z[:, W:]
        vn = _rms(v)[0] * gv_ref[...]
        mask = _chunk_mask()
        for g in range(G):
            cs = slice(g * GROUP_DIM, (g + 1) * GROUP_DIM)
            wg = jnp.where(mask, ws_ref[g], 0.0).astype(BF16)
            for b in range(tm // GMLP_BLOCK):
                rs = slice(b * GMLP_BLOCK, (b + 1) * GMLP_BLOCK)
                s = jnp.dot(wg, vn[rs, cs].astype(BF16), preferred_element_type=F32) + bs_ref[:, cs]
                o_ref[rs, cs] = (u[rs, cs] * s).astype(BF16)

    return pl.pallas_call(
        body, name="gate_fwd", grid=(T // tm,),
        in_specs=[pl.BlockSpec((tm, W2), lambda i: (i, 0)), pl.BlockSpec((1, W), lambda i: (0, 0)),
                  pl.BlockSpec((G, GMLP_BLOCK, GMLP_BLOCK), lambda i: (0, 0, 0)),
                  pl.BlockSpec((GMLP_BLOCK, W), lambda i: (0, 0))],
        out_specs=pl.BlockSpec((tm, W), lambda i: (i, 0)), out_shape=jax.ShapeDtypeStruct((T, W), BF16),
        compiler_params=_params(("arbitrary",)))(zp, gv, ws, bs_tile)


def _gate_bwd(zp, d_out, gv, ws, bs_tile, *, tm=256):
    T, W2 = zp.shape
    W = W2 // 2
    G = W // GROUP_DIM
    tm = _row_tile(T, tm)
    nm = T // tm

    def body(zp_ref, do_ref, gv_ref, ws_ref, bs_ref, dzp_ref, dws_ref, dbs_ref, dgv_ref, du_scr, dvn_scr, dsum_scr):
        i = pl.program_id(0)

        @pl.when(i == 0)
        def _():
            dws_ref[...] = jnp.zeros_like(dws_ref)
            dgv_ref[...] = jnp.zeros_like(dgv_ref)
            dsum_scr[...] = jnp.zeros_like(dsum_scr)

        zp = zp_ref[...].astype(F32)
        z, dz = _gelu(zp, with_grad=True)
        u, v = z[:, :W], z[:, W:]
        n, r = _rms(v)
        gv = gv_ref[...]
        vn = n * gv
        d_out = do_ref[...].astype(F32)
        mask = _chunk_mask()
        for g in range(G):
            cs = slice(g * GROUP_DIM, (g + 1) * GROUP_DIM)
            wg = jnp.where(mask, ws_ref[g], 0.0).astype(BF16)
            dw = jnp.zeros((GMLP_BLOCK, GMLP_BLOCK), F32)
            for b in range(tm // GMLP_BLOCK):
                rs = slice(b * GMLP_BLOCK, (b + 1) * GMLP_BLOCK)
                vb = vn[rs, cs].astype(BF16)
                s = jnp.dot(wg, vb, preferred_element_type=F32) + bs_ref[:, cs]
                du_scr[rs, cs] = d_out[rs, cs] * s
                ds = d_out[rs, cs] * u[rs, cs]
                dsb = ds.astype(BF16)
                dvn_scr[rs, cs] = lax.dot_general(wg, dsb, (((0,), (0,)), ((), ())), preferred_element_type=F32)
                dw = dw + lax.dot_general(dsb, vb, (((1,), (1,)), ((), ())), preferred_element_type=F32)
                dsum_scr[:, cs] += ds
            dws_ref[g] += jnp.where(mask, dw, 0.0)
        dvn = dvn_scr[...]
        dgv_ref[...] += jnp.sum(dvn * n, axis=0, keepdims=True)
        t = dvn * gv
        dv = r * (t - n * jnp.mean(t * n, axis=-1, keepdims=True))
        dzp_ref[:, :W] = (du_scr[...] * dz[:, :W]).astype(BF16)
        dzp_ref[:, W:] = (dv * dz[:, W:]).astype(BF16)

        @pl.when(i == nm - 1)
        def _():
            sel = (lax.broadcasted_iota(jnp.int32, (G, W), 1) // GROUP_DIM
                   == lax.broadcasted_iota(jnp.int32, (G, W), 0)).astype(F32)
            dbs_ref[...] = lax.dot_general(sel, dsum_scr[...], (((1,), (1,)), ((), ())),
                                           precision=lax.Precision.HIGHEST, preferred_element_type=F32)

    return pl.pallas_call(
        body, name="gate_bwd", grid=(nm,),
        in_specs=[pl.BlockSpec((tm, W2), lambda i: (i, 0)), pl.BlockSpec((tm, W), lambda i: (i, 0)),
                  pl.BlockSpec((1, W), lambda i: (0, 0)),
                  pl.BlockSpec((G, GMLP_BLOCK, GMLP_BLOCK), lambda i: (0, 0, 0)),
                  pl.BlockSpec((GMLP_BLOCK, W), lambda i: (0, 0))],
        out_specs=[pl.BlockSpec((tm, W2), lambda i: (i, 0)),
                   pl.BlockSpec((G, GMLP_BLOCK, GMLP_BLOCK), lambda i: (0, 0, 0)),
                   pl.BlockSpec((G, GMLP_BLOCK), lambda i: (0, 0)), pl.BlockSpec((1, W), lambda i: (0, 0))],
        out_shape=[jax.ShapeDtypeStruct((T, W2), BF16), jax.ShapeDtypeStruct((G, GMLP_BLOCK, GMLP_BLOCK), F32),
                   jax.ShapeDtypeStruct((G, GMLP_BLOCK), F32), jax.ShapeDtypeStruct((1, W), F32)],
        scratch_shapes=[pltpu.VMEM((tm, W), F32), pltpu.VMEM((tm, W), F32), pltpu.VMEM((GMLP_BLOCK, W), F32)],
        compiler_params=_params(("arbitrary",)))(zp, d_out, gv, ws, bs_tile)


LANES = 128
HALO = 16


def _taps(ext, w, b):
    return w[2:3] * ext[HALO:] + w[1:2] * pltpu.roll(ext, 1, 0)[HALO:] + w[0:1] * pltpu.roll(ext, 2, 0)[HALO:] + b


def _ffn_fwd(h, w, g, cw, cb, wd, S, *, name, loss=None, tm=256):
    T, D = h.shape
    F = w.shape[-1] // 2
    tc = _col_tile(F)
    tm = _row_tile(S, tm)
    has_loss = loss is not None

    def body(*refs):
        h_ref, w_ref, g_ref, cw_ref, cb_ref, wd_ref = refs[:6]
        o_ref, y_ref, a_ref, c_ref, n_ref = refs[8:13] if has_loss else refs[6:11]
        tail = refs[-1]
        first = (pl.program_id(0) * tm) % S == 0
        nb = (_rms(h_ref[...])[0] * g_ref[...]).astype(BF16)
        n_ref[...] = nb
        for j in range(F // tc):
            cs = slice(j * tc, (j + 1) * tc)
            conv = []
            for s in range(2):
                acc = jnp.dot(nb, w_ref[:, s * F + j * tc:s * F + (j + 1) * tc], preferred_element_type=F32)
                ab = acc.astype(BF16)
                a_ref[s, :, cs] = ab
                af = ab.astype(F32)
                ext = jnp.concatenate([jnp.where(first, 0.0, tail[s, :, cs]), af], axis=0)
                tail[s, :, cs] = af[tm - HALO:, :]
                cv = _taps(ext, cw_ref[s, :, cs], cb_ref[s:s + 1, cs]).astype(BF16)
                c_ref[s, :, cs] = cv
                conv.append(cv.astype(F32))
            up, gate = conv
            y_ref[:, cs] = (gate * jax.nn.sigmoid(gate) * up).astype(BF16)
        out = h_ref[...] + jnp.dot(y_ref[...], wd_ref[...], preferred_element_type=F32)
        if has_loss:
            _loss_epilogue(out, refs[6], refs[7], o_ref, refs[13], refs[14], pl.program_id(0) == 0)
        else:
            o_ref[...] = out

    row = lambda width: pl.BlockSpec((tm, width), lambda i: (i, 0))
    wide = pl.BlockSpec((2, tm, F), lambda i: (0, i, 0))
    fixed = lambda *shape: pl.BlockSpec(shape, lambda i: (0,) * len(shape))
    once = pl.Buffered(1)
    ins = [h, w, g.reshape(1, D), cw, cb, wd]
    in_specs = [row(D), pl.BlockSpec((None, D, 2 * F), lambda i: (0, 0, 0), pipeline_mode=once), fixed(1, D),
                fixed(2, 3, F), fixed(2, F), pl.BlockSpec((None, F, D), lambda i: (0, 0, 0), pipeline_mode=once)]
    out_specs = [row(D), row(F), wide, wide, row(D)]
    out_shape = [jax.ShapeDtypeStruct((T, D), F32), jax.ShapeDtypeStruct((T, F), BF16),
                 jax.ShapeDtypeStruct((2, T, F), BF16), jax.ShapeDtypeStruct((2, T, F), BF16),
                 jax.ShapeDtypeStruct((T, D), BF16)]
    if has_loss:
        ins += [loss[0].reshape(1, D), loss[1]]
        in_specs += [fixed(1, D), row(D)]
        out_specs += [fixed(8, 128), fixed(1, D)]
        out_shape += [jax.ShapeDtypeStruct((8, 128), F32), jax.ShapeDtypeStruct((1, D), F32)]
    return pl.pallas_call(body, name=name, grid=(T // tm,), in_specs=in_specs, out_specs=out_specs,
                          out_shape=out_shape, scratch_shapes=[pltpu.VMEM((2, HALO, F), F32)],
                          compiler_params=_params(("arbitrary",)))(*ins)


def _conv_bwd(a, c, dy, cw, S, *, tm=256):
    _, T, F = a.shape
    tc = _col_tile(F)
    tm = _row_tile(S, tm)
    nm = T // tm
    hb = tm // HALO
    TE = tm + HALO
    nxt = lambda j, i: jnp.minimum((i + 1) * hb, T // HALO - 1)

    def body(a_ref, c_ref, nc_ref, dy_ref, ndy_ref, w_ref, da_ref, dw_ref, db_ref):
        i = pl.program_id(1)
        last = ((i + 1) * tm) % S == 0
        keep_n = jnp.where(last, 0.0, 1.0)

        @pl.when(i == 0)
        def _():
            dw_ref[...] = jnp.zeros_like(dw_ref)
            db_ref[...] = jnp.zeros_like(db_ref)

        for j in range(tc // LANES):
            cs = slice(j * LANES, (j + 1) * LANES)
            dyf = jnp.concatenate([dy_ref[:, cs].astype(F32), ndy_ref[:, cs].astype(F32) * keep_n], axis=0)
            up = jnp.concatenate([c_ref[0, :, cs].astype(F32), nc_ref[0, :, cs].astype(F32)], axis=0)
            gate = jnp.concatenate([c_ref[1, :, cs].astype(F32), nc_ref[1, :, cs].astype(F32)], axis=0)
            sg = jax.nn.sigmoid(gate)
            for s, d in ((0, dyf * (gate * sg)), (1, dyf * up * (sg * (1.0 + gate * (1.0 - sg))))):
                a = a_ref[s, :, cs].astype(F32)
                w = w_ref[s, :, cs]
                u1, u2 = pltpu.roll(d, TE - 1, 0), pltpu.roll(d, TE - 2, 0)
                db_ref[s:s + 1, cs] += jnp.sum(d[:tm], axis=0, keepdims=True)
                dw_ref[s, 2:3, cs] += jnp.sum(d[:tm] * a, axis=0, keepdims=True)
                dw_ref[s, 1:2, cs] += jnp.sum(u1[:tm] * a, axis=0, keepdims=True)
                dw_ref[s, 0:1, cs] += jnp.sum(u2[:tm] * a, axis=0, keepdims=True)
                da_ref[s, :, cs] = (w[2:3] * d + w[1:2] * u1 + w[0:1] * u2)[:tm].astype(BF16)

    cur = pl.BlockSpec((2, tm, tc), lambda j, i: (0, i, j))
    return pl.pallas_call(
        body, name="conv_bwd", grid=(F // tc, nm),
        in_specs=[cur, cur, pl.BlockSpec((2, HALO, tc), lambda j, i: (0, nxt(j, i), j)),
                  pl.BlockSpec((tm, tc), lambda j, i: (i, j)), pl.BlockSpec((HALO, tc), lambda j, i: (nxt(j, i), j)),
                  pl.BlockSpec((2, 3, tc), lambda j, i: (0, 0, j))],
        out_specs=[cur, pl.BlockSpec((2, 3, tc), lambda j, i: (0, 0, j)), pl.BlockSpec((2, tc), lambda j, i: (0, j))],
        out_shape=[jax.ShapeDtypeStruct((2, T, F), BF16), jax.ShapeDtypeStruct((2, 3, F), F32),
                   jax.ShapeDtypeStruct((2, F), F32)],
        compiler_params=_params(("arbitrary", "arbitrary")))(a, c, c, dy, dy, cw)


def _bias_index():
    idx = np.arange(F_LEN)
    d = np.where(idx < K_SPAN, idx, idx - F_LEN)
    return np.clip(PAD - d, -REL_CLIP, REL_CLIP) + REL_CLIP


ROW_GROUP = 16


def _roll_rows(x, sign, unit, steps):
    rows = lax.broadcasted_iota(jnp.int32, x.shape, 0)
    step = 1
    while step < steps:
        shift = unit * step if sign > 0 else F_LEN - unit * step
        x = jnp.where((rows & step) != 0, pltpu.roll(x, shift, 1), x)
        step *= 2
    return x


def _bias_expand(frow):
    H = frow.shape[0]
    groups = Q_BLOCK // ROW_GROUP

    def body(f_ref, o_ref):
        coarse = _roll_rows(jnp.broadcast_to(f_ref[...], (groups, F_LEN)), 1, ROW_GROUP, groups)
        x = jnp.concatenate([jnp.broadcast_to(coarse[a:a + 1], (ROW_GROUP, F_LEN)) for a in range(groups)], axis=0)
        x = _roll_rows(x, 1, 1, ROW_GROUP)[:, :K_SPAN]
        qc = lax.broadcasted_iota(jnp.int32, (Q_BLOCK, K_SPAN), 0) // CHUNK * CHUNK
        kj = lax.broadcasted_iota(jnp.int32, (Q_BLOCK, K_SPAN), 1)
        o_ref[...] = jnp.where((kj >= qc) & (kj < qc + PAD + CHUNK), x, NEG_INF)

    return pl.pallas_call(
        body, name="bias_expand", grid=(H,),
        in_specs=[pl.BlockSpec((None, 1, F_LEN), lambda h: (h, 0, 0))],
        out_specs=pl.BlockSpec((None, Q_BLOCK, K_SPAN), lambda h: (h, 0, 0)),
        out_shape=jax.ShapeDtypeStruct((H, Q_BLOCK, K_SPAN), F32), compiler_params=_params(("arbitrary",)))(frow)


def _bias_reduce(dbias, n_rel):
    H = dbias.shape[0]
    onehot = jnp.asarray((_bias_index()[:, None] == np.arange(n_rel)[None, :]).astype(np.float32), dtype=BF16)

    def body(d_ref, oh_ref, o_ref):
        x = jnp.concatenate([d_ref[...], jnp.zeros((Q_BLOCK, F_LEN - K_SPAN), F32)], axis=1)
        fine = _roll_rows(x, -1, 1, ROW_GROUP).reshape(Q_BLOCK // ROW_GROUP, ROW_GROUP, F_LEN)
        coarse = _roll_rows(jnp.sum(fine, axis=1), -1, ROW_GROUP, Q_BLOCK // ROW_GROUP)
        row = jnp.broadcast_to(jnp.sum(coarse, axis=0, keepdims=True), (8, F_LEN))
        acc = jnp.zeros((8, n_rel), F32)
        for _ in range(3):
            piece = row.astype(BF16)
            acc = acc + jnp.dot(piece, oh_ref[...], preferred_element_type=F32)
            row = row - piece.astype(F32)
        o_ref[...] = acc[0:1]

    return pl.pallas_call(
        body, name="bias_reduce", grid=(H,),
        in_specs=[pl.BlockSpec((None, Q_BLOCK, K_SPAN), lambda h: (h, 0, 0)),
                  pl.BlockSpec((F_LEN, n_rel), lambda h: (0, 0))],
        out_specs=pl.BlockSpec((None, 1, n_rel), lambda h: (h, 0, 0)),
        out_shape=jax.ShapeDtypeStruct((H, 1, n_rel), F32), compiler_params=_params(("arbitrary",)))(dbias, onehot)


def _attn_specs(S):
    hw = HEADS_PER_STEP * HEAD_DIM
    qspec = pl.BlockSpec((None, Q_BLOCK, hw), lambda g, b, i: (b, i, g))
    kspec = pl.BlockSpec((None, None, S, hw), lambda g, b, i: (0, b, 0, g))
    vspec = pl.BlockSpec((None, None, S, hw), lambda g, b, i: (1, b, 0, g))
    bspec = pl.BlockSpec((HEADS_PER_STEP, Q_BLOCK, K_SPAN), lambda g, b, i: (g, 0, 0))
    return hw, qspec, kspec, vspec, bspec


def _span_cases(i, fn):
    short = PAD // Q_BLOCK
    for j in range(short):
        pl.when(i == j)(functools.partial(fn, PAD - j * Q_BLOCK))
    pl.when(i >= short)(functools.partial(fn, 0))


def _key_start(i, off):
    return 0 if off else pl.multiple_of(i * Q_BLOCK - PAD, Q_BLOCK)


def _attn_exp(q_ref, k_ref, b_ref, h, k0, off):
    hs = slice(h * HEAD_DIM, (h + 1) * HEAD_DIM)
    kh = k_ref[pl.ds(k0, K_SPAN - off), hs]
    s = lax.dot_general(q_ref[:, hs], kh, (((1,), (1,)), ((), ())), preferred_element_type=F32) + b_ref[h, :, off:]
    p = jnp.exp(s - jnp.max(s, axis=-1, keepdims=True))
    return p, 1.0 / jnp.sum(p, axis=-1, keepdims=True), kh


def _attn_fwd(q, kv, bias, B, S):
    HD = q.shape[-1]
    hw, qspec, kspec, vspec, bspec = _attn_specs(S)

    def body(q_ref, k_ref, v_ref, b_ref, o_ref):
        i = pl.program_id(2)

        def block(off):
            k0 = _key_start(i, off)
            outs = []
            for h in range(HEADS_PER_STEP):
                hs = slice(h * HEAD_DIM, (h + 1) * HEAD_DIM)
                p, inv, _ = _attn_exp(q_ref, k_ref, b_ref, h, k0, off)
                outs.append(jnp.dot(p.astype(BF16), v_ref[pl.ds(k0, K_SPAN - off), hs],
                                    preferred_element_type=F32) * inv)
            o_ref[...] = jnp.concatenate(outs, axis=1).astype(BF16)

        _span_cases(i, block)

    return pl.pallas_call(
        body, name="attn_fwd", grid=(HD // hw, B, S // Q_BLOCK), in_specs=[qspec, kspec, vspec, bspec],
        out_specs=qspec, out_shape=jax.ShapeDtypeStruct((B, S, HD), BF16),
        compiler_params=_params(("arbitrary", "arbitrary", "arbitrary")))(q, kv, kv, bias)


def _attn_bwd(q, kv, bias, do, B, S):
    HD = q.shape[-1]
    H = HD // HEAD_DIM
    hw, qspec, kspec, vspec, bspec = _attn_specs(S)
    scale = HEAD_DIM ** -0.5
    nq = S // Q_BLOCK

    def body(q_ref, k_ref, v_ref, b_ref, do_ref, dq_ref, dkv_ref, db_ref, dk_acc, dv_acc):
        b, i = pl.program_id(1), pl.program_id(2)

        @pl.when(i == 0)
        def _():
            dk_acc[...] = jnp.zeros_like(dk_acc)
            dv_acc[...] = jnp.zeros_like(dv_acc)

        @pl.when((i == 0) & (b == 0))
        def _():
            db_ref[...] = jnp.zeros_like(db_ref)

        def block(off):
            k0 = _key_start(i, off)
            keys = pl.ds(k0, K_SPAN - off)
            for h in range(HEADS_PER_STEP):
                hs = slice(h * HEAD_DIM, (h + 1) * HEAD_DIM)
                p, inv, kh = _attn_exp(q_ref, k_ref, b_ref, h, k0, off)
                p = p * inv
                doh = do_ref[:, hs]
                dp = lax.dot_general(doh, v_ref[keys, hs], (((1,), (1,)), ((), ())), preferred_element_type=F32)
                ds = p * (dp - jnp.sum(p * dp, axis=-1, keepdims=True))
                db_ref[h, :, off:] += ds
                dsb = ds.astype(BF16)
                dq_ref[:, hs] = (jnp.dot(dsb, kh, preferred_element_type=F32) * scale).astype(BF16)
                dk_acc[hs, keys] += lax.dot_general(q_ref[:, hs], dsb, (((0,), (0,)), ((), ())),
                                                     preferred_element_type=F32)
                dv_acc[hs, keys] += lax.dot_general(doh, p.astype(BF16), (((0,), (0,)), ((), ())),
                                                     preferred_element_type=F32)

        _span_cases(i, block)

        @pl.when(i == nq - 1)
        def _():
            dkv_ref[0] = dk_acc[...].T.astype(BF16)
            dkv_ref[1] = dv_acc[...].T.astype(BF16)

    return pl.pallas_call(
        body, name="attn_bwd", grid=(HD // hw, B, nq), in_specs=[qspec, kspec, vspec, bspec, qspec],
        out_specs=[qspec, pl.BlockSpec((2, None, S, hw), lambda g, b, i: (0, b, 0, g)), bspec],
        out_shape=[jax.ShapeDtypeStruct((B, S, HD), BF16), jax.ShapeDtypeStruct((2, B, S, HD), BF16),
                   jax.ShapeDtypeStruct((H, Q_BLOCK, K_SPAN), F32)],
        scratch_shapes=[pltpu.VMEM((hw, S), F32), pltpu.VMEM((hw, S), F32)],
        compiler_params=_params(("arbitrary", "arbitrary", "arbitrary")))(q, kv, kv, bias, do)


def _sub_rows(R):
    for cand in (256, 352, 128, 64, 8):
        if R % cand == 0 and R > cand:
            return cand
    return R


def _adamw(w, g, m, v, *, name):
    R, C = w.shape
    tr = _sub_rows(R)

    def body(w_ref, g_ref, m_ref, v_ref, d_ref, nm_ref, nv_ref):
        g = g_ref[...]
        m = ADAM_B1 * m_ref[...] + (1.0 - ADAM_B1) * g
        v = ADAM_B2 * v_ref[...] + (1.0 - ADAM_B2) * (g * g)
        m_hat = m / (1.0 - ADAM_B1 ** ADAM_STEP)
        v_hat = v / (1.0 - ADAM_B2 ** ADAM_STEP)
        d_ref[...] = -ADAM_LR * (m_hat / (jnp.sqrt(v_hat) + ADAM_EPS) + ADAM_WD * w_ref[...])
        nm_ref[...] = m
        nv_ref[...] = v

    spec = pl.BlockSpec((tr, C), lambda i: (i, 0))
    return pl.pallas_call(body, name=name, grid=(R // tr,), in_specs=[spec] * 4, out_specs=[spec] * 3,
                          out_shape=[jax.ShapeDtypeStruct((R, C), F32)] * 3,
                          compiler_params=_params(("arbitrary",)))(w, g, m, v)


def _add_pair(units, got, core, *, name):
    n4, R, C = got.shape
    rows = n4 * R
    tr = 512 if rows % 512 == 0 else R

    def body(c_ref, u_ref, got_ref, o_ref):
        o_ref[...] = (u_ref[...].astype(F32) + got_ref[...].astype(F32)).astype(BF16)

    spec = pl.BlockSpec((tr, C), lambda i, c: (i, 0))
    grid_spec = pltpu.PrefetchScalarGridSpec(
        num_scalar_prefetch=1, grid=(rows // tr,),
        in_specs=[pl.BlockSpec((None, tr, C), lambda i, c: (c[0], i, 0)), spec], out_specs=spec)
    out = pl.pallas_call(body, name=name, grid_spec=grid_spec, out_shape=jax.ShapeDtypeStruct((rows, C), BF16),
                         compiler_params=_params(("arbitrary",)))(core.reshape(1), units.reshape(2, rows, C),
                                                                   got.reshape(rows, C))
    return out.reshape(n4, R, C)


def _sum_chips(w, own, got, pos, *, name, layer=0, into=None):
    _, R, C = own.shape
    tr = _sub_rows(R)
    nr = R // tr

    def body(p_ref, own_ref, got_ref, *rest):
        o_ref = rest[-1]
        o_ref[...] = (own_ref[...].astype(F32) + got_ref[0].astype(F32) + got_ref[1].astype(F32)
                      + got_ref[2].astype(F32))

    if w.row_sharded:
        out_map = lambda i, p: (layer, i, p[1])
    else:
        out_map = lambda i, p: (layer, p[1] * nr + i, 0)
    ins = [pos, own, got]
    in_specs = [pl.BlockSpec((None, tr, C), lambda i, p: (p[0], i, 0)),
                pl.BlockSpec((3, tr, C), lambda i, p: (0, i, 0))]
    alias = {}
    if into is not None:
        ins.append(into)
        in_specs.append(ANY)
        alias = {3: 0}
    grid_spec = pltpu.PrefetchScalarGridSpec(num_scalar_prefetch=1, grid=(nr,), in_specs=in_specs,
                                             out_specs=pl.BlockSpec((None, tr, C), out_map))
    return pl.pallas_call(body, name=name, grid_spec=grid_spec, input_output_aliases=alias,
                          out_shape=jax.ShapeDtypeStruct((w.L, w.ks, w.ns), F32),
                          compiler_params=_params(("arbitrary",)))(*ins)


def _mesh_pos():
    return lax.axis_index("x"), lax.axis_index("y"), lax.axis_index("c")


def _other_chips(x, y):
    return [(1 - x, y), (x, 1 - y), (1 - x, 1 - y)]


ANY = pl.BlockSpec(memory_space=pl.ANY)


class _W:
    def __init__(self, name, shard, row_sharded, direct=False):
        self.name = name
        self.direct = direct
        self.L, ks, ns = shard.shape
        self.row_sharded = row_sharded
        self.K, self.N = (ks * N_CHIPS, ns) if row_sharded else (ks, ns * N_CHIPS)
        self.ks, self.ns = ks, ns

    def shard_of(self, full, j):
        if self.row_sharded:
            return full.at[:, pl.ds(j * self.ks, self.ks), :]
        return full.at[:, :, pl.ds(j * self.ns, self.ns)]

    def half_of(self, shard, c):
        if self.row_sharded:
            return shard.at[:, :, pl.ds(c * (self.ns // 2), self.ns // 2)]
        return shard.at[:, pl.ds(c * (self.ks // 2), self.ks // 2), :]


HBM = pl.BlockSpec(memory_space=pltpu.HBM)
SEM = pl.BlockSpec(memory_space=pltpu.SEMAPHORE)
IN_FLIGHT = pltpu.SideEffectType.DATAFLOW_SIDE_EFFECTING


def _in_hbm(a):
    return pltpu.with_memory_space_constraint(a, pltpu.HBM)


def _gather_start(ws, shards, after, *, name):
    nw = len(ws)

    def body(*refs):
        src, dst = refs[:nw], refs[nw:2 * nw]
        send, recv = refs[2 * nw + 1:3 * nw + 1], refs[3 * nw + 1:4 * nw + 1]
        x, y, c = _mesh_pos()
        me = 2 * x + y
        for i, w in enumerate(ws):
            for f, (px, py) in enumerate(_other_chips(x, y)):
                for e in range(2 if w.direct else 1):
                    k = 2 * f + e
                    pltpu.make_async_remote_copy(
                        src_ref=w.half_of(src[i], c), dst_ref=w.half_of(w.shard_of(dst[i], me), c),
                        send_sem=send[i].at[k], recv_sem=recv[i].at[k], device_id=(px, py, c if e == 0 else 1 - c),
                        device_id_type=MESH).start()

    fulls = [lax.empty((w.L, w.K, w.N), BF16) for w in ws]
    out = pl.pallas_call(
        body, name=name, in_specs=[HBM] * (2 * nw) + [ANY],
        out_specs=[SEM] * (2 * nw) + [HBM] * (2 * nw),
        out_shape=[pltpu.SemaphoreType.DMA((6,))] * (2 * nw)
        + [pltpu.HBM(s.shape, BF16) for s in shards] + [pltpu.HBM(f.shape, BF16) for f in fulls],
        input_output_aliases={i: 2 * nw + i for i in range(2 * nw)},
        compiler_params=pltpu.CompilerParams(has_side_effects=IN_FLIGHT))(
            *[_in_hbm(s) for s in shards], *[_in_hbm(f) for f in fulls], after)
    return [(out[i], out[nw + i], out[2 * nw + i], out[3 * nw + i]) for i in range(nw)]


def _gather_wait(ws, flight, after, *, name):
    nw = len(ws)

    def body(*refs):
        src, dst = refs[:nw], refs[nw:2 * nw]
        send, recv = refs[2 * nw:3 * nw], refs[3 * nw:4 * nw]
        x, y, c = _mesh_pos()
        for i, w in enumerate(ws):
            for f, (px, py) in enumerate(_other_chips(x, y)):
                for e in range(2 if w.direct else 1):
                    k = 2 * f + e
                    landed = w.half_of(w.shard_of(dst[i], 2 * px + py), c if e == 0 else 1 - c)
                    cp = pltpu.make_async_remote_copy(
                        src_ref=w.half_of(src[i], c), dst_ref=landed, send_sem=send[i].at[k], recv_sem=recv[i].at[k],
                        device_id=(px, py, c), device_id_type=MESH)
                    cp.wait_send()
                    cp.wait_recv()

    shards, fulls = [fl[2] for fl in flight], [fl[3] for fl in flight]
    out = pl.pallas_call(
        body, name=name, in_specs=[HBM] * (2 * nw) + [SEM] * (2 * nw) + [ANY],
        out_specs=[HBM] * (2 * nw),
        out_shape=[pltpu.HBM(s.shape, BF16) for s in shards] + [pltpu.HBM(f.shape, BF16) for f in fulls],
        input_output_aliases={i: i for i in range(2 * nw)},
        compiler_params=pltpu.CompilerParams(has_side_effects=IN_FLIGHT))(
            *shards, *fulls, *[fl[0] for fl in flight], *[fl[1] for fl in flight], after)
    return out[:nw], out[nw:]


def _gather_finish(ws, shards, fulls, *, name):
    nw = len(ws)
    forward = not ws[0].direct

    def body(*refs):
        src, dst, stage = refs[:nw], refs[3 * nw:4 * nw], refs[4 * nw:5 * nw]
        send_sems, recv_sems, load_sems, store_sems = refs[5 * nw:]
        x, y, c = _mesh_pos()
        me = 2 * x + y
        sibling = (x, y, 1 - c)
        chips = _other_chips(x, y)

        def fwd(i, w, f, half):
            px, py = chips[f]
            landed = w.half_of(w.shard_of(dst[i], 2 * px + py), half)
            return pltpu.make_async_remote_copy(src_ref=landed, dst_ref=landed, send_sem=send_sems.at[3 * i + f],
                                                recv_sem=recv_sems.at[3 * i + f], device_id=sibling,
                                                device_id_type=MESH)

        loads = [pltpu.make_async_copy(src[i], stage[i], load_sems.at[i]) for i in range(nw)]
        for cp in loads:
            cp.start()
        sends = [fwd(i, w, f, c) for i, w in enumerate(ws) for f in range(3)] if forward else []
        for cp in sends:
            cp.start()
        stores = [pltpu.make_async_copy(stage[i], w.shard_of(dst[i], me), store_sems.at[i])
                  for i, w in enumerate(ws)]
        for ld, st in zip(loads, stores):
            ld.wait()
            st.start()
        if forward:
            for i, w in enumerate(ws):
                for f in range(3):
                    fwd(i, w, f, 1 - c).wait_recv()
        for cp in sends:
            cp.wait_send()
        for cp in stores:
            cp.wait()

    out = pl.pallas_call(
        body, name=name, in_specs=[ANY] * (2 * nw), out_specs=[ANY] * (2 * nw),
        out_shape=[jax.ShapeDtypeStruct(s.shape, BF16) for s in shards]
        + [jax.ShapeDtypeStruct(f.shape, BF16) for f in fulls],
        input_output_aliases={i: i for i in range(2 * nw)},
        scratch_shapes=[pltpu.VMEM((w.L, w.ks, w.ns), BF16) for w in ws]
        + [pltpu.SemaphoreType.DMA((3 * nw,)), pltpu.SemaphoreType.DMA((3 * nw,)), pltpu.SemaphoreType.DMA((nw,)),
           pltpu.SemaphoreType.DMA((nw,))],
        compiler_params=_params(has_side_effects=True))(*shards, *fulls)
    return out[nw:]


def _split_copies(name, srcs, lands, n_sems, copies_of, *, flight=None, after=None):
    n = len(srcs)
    starting = flight is None

    def body(*refs):
        src, land = refs[:n], refs[n:2 * n]
        sems = refs[2 * n + 1:4 * n + 1] if starting else refs[2 * n:4 * n]
        for i in range(n):
            for cp in copies_of(i, src[i], land[i], sems[i], sems[n + i]):
                if starting:
                    cp.start()
                else:
                    cp.wait_send()
                    cp.wait_recv()

    thru = [pltpu.HBM(a.shape, a.dtype) for a in list(srcs) + list(lands)]
    if starting:
        out = pl.pallas_call(
            body, name=name, in_specs=[HBM] * (2 * n) + [ANY], out_specs=[SEM] * (2 * n) + [HBM] * (2 * n),
            out_shape=[pltpu.SemaphoreType.DMA((n_sems,))] * (2 * n) + thru,
            input_output_aliases={i: 2 * n + i for i in range(2 * n)},
            compiler_params=pltpu.CompilerParams(has_side_effects=IN_FLIGHT))(
                *[_in_hbm(a) for a in srcs], *[_in_hbm(a) for a in lands], after)
        return [(out[i], out[n + i], out[2 * n + i], out[3 * n + i]) for i in range(n)]
    out = pl.pallas_call(
        body, name=name, in_specs=[HBM] * (2 * n) + [SEM] * (2 * n) + [ANY], out_specs=[HBM] * (2 * n),
        out_shape=thru, input_output_aliases={i: i for i in range(2 * n)},
        compiler_params=pltpu.CompilerParams(has_side_effects=IN_FLIGHT))(
            *srcs, *lands, *[fl[0] for fl in flight], *[fl[1] for fl in flight], after)
    return out[:n], out[n:]


def _sum8(land, vec, me):
    R = vec.shape[0]

    def body(me_ref, land_ref, vec_ref, o_ref):
        acc = jnp.zeros((R, 128), F32)
        for d in range(8):
            acc = acc + jnp.where(me_ref[0] == d, vec_ref[...], land_ref[d])
        o_ref[...] = acc

    grid_spec = pltpu.PrefetchScalarGridSpec(
        num_scalar_prefetch=1, grid=(1,),
        in_specs=[pl.BlockSpec((8, R, 128), lambda i, m: (0, 0, 0)), pl.BlockSpec((R, 128), lambda i, m: (0, 0))],
        out_specs=pl.BlockSpec((R, 128), lambda i, m: (0, 0)))
    return pl.pallas_call(body, name="sum8", grid_spec=grid_spec, out_shape=jax.ShapeDtypeStruct((R, 128), F32),
                          compiler_params=_params(("arbitrary",)))(me.reshape(1), land, vec)


def _swap_copies(i, src, got, send, recv):
    x, y, c = _mesh_pos()
    return [pltpu.make_async_remote_copy(src_ref=src.at[1 - c], dst_ref=got, send_sem=send.at[0], recv_sem=recv.at[0],
                                         device_id=(x, y, 1 - c), device_id_type=MESH)]


def _gather8_copies(i, src, land, send, recv):
    x, y, c = _mesh_pos()
    me = 4 * x + 2 * y + c
    peers = [(x, y, 1 - c)] + [(px, py, pc) for px, py in _other_chips(x, y) for pc in (c, 1 - c)]
    return [pltpu.make_async_remote_copy(src_ref=src, dst_ref=land.at[me], send_sem=send.at[k], recv_sem=recv.at[k],
                                         device_id=peer, device_id_type=MESH) for k, peer in enumerate(peers)]


def _scatter_copy(src, got, send, recv, f, chip, c):
    px, py = chip
    return pltpu.make_async_remote_copy(src_ref=src.at[2 * px + py], dst_ref=got.at[f], send_sem=send.at[f],
                                        recv_sem=recv.at[f], device_id=(px, py, c), device_id_type=MESH)


def _scatter_start(sums, *, name):
    nw = len(sums)

    def body(*refs):
        src, got = refs[:nw], refs[nw:2 * nw]
        send, recv = refs[2 * nw:3 * nw], refs[3 * nw:4 * nw]
        x, y, c = _mesh_pos()
        for i in range(nw):
            for f, chip in enumerate(_other_chips(x, y)):
                _scatter_copy(src[i], got[i], send[i], recv[i], f, chip, c).start()

    lands = [lax.empty((3,) + s.shape[1:], BF16) for s in sums]
    out = pl.pallas_call(
        body, name=name, in_specs=[HBM] * (2 * nw), out_specs=[SEM] * (2 * nw) + [HBM] * (2 * nw),
        out_shape=[pltpu.SemaphoreType.DMA((3,))] * (2 * nw)
        + [pltpu.HBM(s.shape, BF16) for s in sums] + [pltpu.HBM(l.shape, BF16) for l in lands],
        input_output_aliases={i: 2 * nw + i for i in range(2 * nw)},
        compiler_params=pltpu.CompilerParams(has_side_effects=IN_FLIGHT))(
            *[_in_hbm(s) for s in sums], *[_in_hbm(l) for l in lands])
    return [(out[i], out[nw + i], out[2 * nw + i], out[3 * nw + i]) for i in range(nw)]


def _scatter_wait(flight, after):
    nw = len(flight)

    def body(*refs):
        src, got = refs[:nw], refs[nw:2 * nw]
        send, recv = refs[2 * nw:3 * nw], refs[3 * nw:4 * nw]
        x, y, c = _mesh_pos()
        for i in range(nw):
            for f, chip in enumerate(_other_chips(x, y)):
                cp = _scatter_copy(src[i], got[i], send[i], recv[i], f, chip, c)
                cp.wait_send()
                cp.wait_recv()

    sums, lands = [fl[2] for fl in flight], [fl[3] for fl in flight]
    out = pl.pallas_call(
        body, name="scatter_wait", in_specs=[HBM] * (2 * nw) + [SEM] * (2 * nw) + [ANY], out_specs=[HBM] * (2 * nw),
        out_shape=[pltpu.HBM(s.shape, BF16) for s in sums] + [pltpu.HBM(l.shape, BF16) for l in lands],
        input_output_aliases={i: i for i in range(2 * nw)},
        compiler_params=pltpu.CompilerParams(has_side_effects=IN_FLIGHT))(
            *sums, *lands, *[fl[0] for fl in flight], *[fl[1] for fl in flight], after)
    return out[:nw], out[nw:]


def _join_halves(ws, shards):
    nw = len(ws)

    def body(*refs):
        buf = refs[nw:2 * nw]
        send_sems, recv_sems = refs[2 * nw:]
        x, y, c = _mesh_pos()
        sibling = (x, y, 1 - c)

        def copy(i, w, half):
            region = w.half_of(buf[i], half)
            return pltpu.make_async_remote_copy(src_ref=region, dst_ref=region, send_sem=send_sems.at[i],
                                                recv_sem=recv_sems.at[i], device_id=sibling, device_id_type=MESH)

        sends = [copy(i, w, c) for i, w in enumerate(ws)]
        for cp in sends:
            cp.start()
        for i, w in enumerate(ws):
            copy(i, w, 1 - c).wait_recv()
        for cp in sends:
            cp.wait_send()

    return pl.pallas_call(
        body, name="join_halves", in_specs=[ANY] * nw, out_specs=[ANY] * nw,
        out_shape=[jax.ShapeDtypeStruct((w.L, w.ks, w.ns), F32) for w in ws],
        input_output_aliases={i: i for i in range(nw)},
        scratch_shapes=[pltpu.SemaphoreType.DMA((nw,)), pltpu.SemaphoreType.DMA((nw,))],
        compiler_params=_params(has_side_effects=True))(*shards)


def _allreduce_small(vec):
    R = vec.shape[0]

    def body(x_ref, o_ref, buf, send_sems, recv_sems):
        x, y, c = _mesh_pos()
        me, sibling = (x, y, c), (x, y, 1 - c)
        chips = _other_chips(x, y)

        def slot(px, py, pc):
            return buf.at[4 * px + 2 * py + pc]

        def copy(k, block, to, src=None):
            return pltpu.make_async_remote_copy(src_ref=slot(*block) if src is None else src, dst_ref=slot(*block),
                                                send_sem=send_sems.at[k], recv_sem=recv_sems.at[k], device_id=to,
                                                device_id_type=MESH)

        first = [copy(0, me, sibling, src=x_ref)] + [copy(1 + f, me, (*chip, c), src=x_ref)
                                                     for f, chip in enumerate(chips)]
        for cp in first:
            cp.start()
        passed = [copy(4 + f, (*chip, c), sibling) for f, chip in enumerate(chips)]
        for f, chip in enumerate(chips):
            copy(1 + f, (*chip, c), me).wait_recv()
            passed[f].start()
        copy(0, sibling, me).wait_recv()
        for f, chip in enumerate(chips):
            copy(4 + f, (*chip, 1 - c), me).wait_recv()
        for cp in first + passed:
            cp.wait_send()
        slot(*me)[...] = x_ref[...]
        acc = buf[0]
        for d in range(1, 8):
            acc = acc + buf[d]
        o_ref[...] = acc

    return pl.pallas_call(
        body, name="allreduce_small", in_specs=[pl.BlockSpec(memory_space=pltpu.VMEM)],
        out_specs=pl.BlockSpec(memory_space=pltpu.VMEM), out_shape=jax.ShapeDtypeStruct((R, 128), F32),
        scratch_shapes=[pltpu.VMEM((8, R, 128), F32), pltpu.SemaphoreType.DMA((7,)), pltpu.SemaphoreType.DMA((7,))],
        compiler_params=_params())(vec)


def _pack(parts):
    flat = jnp.concatenate([p.reshape(-1).astype(F32) for p in parts])
    n = flat.shape[0]
    pad = (-n) % (64 * 128)
    return jnp.pad(flat, (0, pad)).reshape(-1, 128)


def _unpack(vec, shapes):
    flat = vec.reshape(-1)
    out, off = [], 0
    for s in shapes:
        n = int(np.prod(s))
        out.append(flat[off:off + n].reshape(s))
        off += n
    return out


def kernel(x, a_norm_g, a_w_in, a_v_norm_g, a_w_s, a_b_s, a_w_out, kv_norm_g, w_kv, b_norm_g, b_w_q, b_rel_bias, b_w_o, f_norm_g, f_w_in, f_conv_w, f_conv_b, f_w_down, final_norm_g, loss_target, m_a_norm_g, m_a_w_in, m_a_v_norm_g, m_a_w_s, m_a_b_s, m_a_w_out, m_kv_norm_g, m_w_kv, m_b_norm_g, m_b_w_q, m_b_rel_bias, m_b_w_o, m_f_norm_g, m_f_w_in, m_f_conv_w, m_f_conv_b, m_f_w_down, m_final_norm_g, v_a_norm_g, v_a_w_in, v_a_v_norm_g, v_a_w_s, v_a_b_s, v_a_w_out, v_kv_norm_g, v_w_kv, v_b_norm_g, v_b_w_q, v_b_rel_bias, v_b_w_o, v_f_norm_g, v_f_w_in, v_f_conv_w, v_f_conv_b, v_f_w_down, v_final_norm_g):
    B, S, D = x.shape
    T = B * S
    xi, yi, ci = lax.axis_index("x"), lax.axis_index("y"), lax.axis_index("c")
    j_me = (2 * xi + yi).astype(jnp.int32)
    core = ci.astype(jnp.int32)
    pos = jnp.stack([j_me, core])

    w_shards = {"a_w_in": (a_w_in, False), "a_w_out": (a_w_out, True), "w_kv": (w_kv[None], False),
                "b_w_q": (b_w_q, True), "b_w_o": (b_w_o, True), "f_w_in": (f_w_in, False), "f_w_down": (f_w_down, True)}
    names = list(w_shards)
    ws = [_W(n, w_shards[n][0], w_shards[n][1]) for n in names]
    g_shards = {"a_w_in": (a_w_in, False), "a_w_out": (a_w_out, True),
                "f_w_in0": (f_w_in[0:1], False), "f_w_down0": (f_w_down[0:1], True),
                "w_kv": (w_kv[None], False), "b_w_q": (b_w_q, True), "b_w_o": (b_w_o, True),
                "f_w_in1": (f_w_in[1:2], False), "f_w_down1": (f_w_down[1:2], True)}
    g_names = list(g_shards)
    g_ws = {n: _W(n, *g_shards[n], direct=n in ("w_kv", "b_w_q", "b_w_o", "f_w_in1", "f_w_down1")) for n in g_names}

    Wd = a_w_in.shape[1]
    GW = a_v_norm_g.shape[1] * N_CHIPS
    F2 = f_conv_w.shape[2] * N_CHIPS
    Fh = F2 // 2
    nsd, nsg, nsf = a_norm_g.shape[1], a_v_norm_g.shape[1], f_conv_w.shape[2]
    own = (ci == 0).astype(F32)
    place = lambda sh, width, n: lax.dynamic_update_slice_in_dim(
        jnp.zeros(sh.shape[:-1] + (width,), F32), sh * own, j_me * n, axis=sh.ndim - 1)
    def tied(x, flight):
        x, thru = lax.optimization_barrier((x, flight[0][2]))
        return x, [flight[0][:2] + (thru,) + flight[0][3:]] + flight[1:]

    gathered = _allreduce_small(_pack([place(a_norm_g, Wd, nsd), place(a_v_norm_g, GW, nsg),
                                       place(f_conv_w, F2, nsf)]))
    a_g, a_vg, conv_w = _unpack(gathered, [(1, Wd), (1, GW), (2, 3, F2)])
    flight = dict(zip(g_names, _gather_start([g_ws[n] for n in g_names],
                                             [g_shards[n][0].astype(BF16) for n in g_names], gathered,
                                             name="gather_start")))
    full = {}

    def arrive(group, after, tag):
        gw = [g_ws[n] for n in group]
        sh, fu = _gather_wait(gw, [flight[n] for n in group], after, name=f"gather_wait_{tag}")
        full.update(zip(group, _gather_finish(gw, sh, fu, name=f"gather_finish_{tag}")))
    conv_w2 = conv_w.reshape(2, 3, 2, Fh).transpose(0, 2, 1, 3)
    conv_b2 = f_conv_b.reshape(2, 2, Fh)

    h0 = x.reshape(T, D)
    target = loss_target.reshape(T, D)
    bs_tile = jnp.repeat(a_b_s[0].T, GROUP_DIM, axis=1)
    ws_a = a_w_s[0]
    scale = HEAD_DIM ** -0.5
    HD = b_w_q.shape[2]
    H = HD // HEAD_DIM
    n_rel = b_rel_bias.shape[-1]
    frow, (flight["a_w_in"],) = tied(b_rel_bias[0][:, _bias_index()].reshape(H, 1, F_LEN), [flight["a_w_in"]])
    bias = _bias_expand(frow)

    def ffn_fwd(h, l, loss=None):
        out = _ffn_fwd(h, full[f"f_w_in{l}"], f_norm_g[l], conv_w2[l], conv_b2[l], full[f"f_w_down{l}"], S,
                       loss=loss, name=f"ffn{l}")
        yff, a, c, n = out[1:5]
        return (out[0] if loss is None else (out[0], out[5], out[6])), (a, c, n, yff)

    arrive(["a_w_in", "a_w_out"], bias, "a")
    zp, n_a = _mm(h0, full["a_w_in"], layer=0, norm_g=a_g[0], out_dtype=BF16, emit_norm=True, name="a_in")
    out_a = _gate_fwd(zp, a_vg, ws_a, bs_tile)
    h1 = _mm(out_a, full["a_w_out"], layer=0, res=h0, name="a_out")
    arrive(["f_w_in0", "f_w_down0"], h1, "f0")
    h2, saved0 = ffn_fwd(h1, 0)
    arrive(["w_kv", "b_w_q", "b_w_o"], h2, "b")
    arrive(["f_w_in1", "f_w_down1"], h2, "f1")
    q, kv, n_q, n_kv = _qkv_fwd(h2, full["b_w_q"], b_norm_g[0], full["w_kv"], kv_norm_g, scale)
    kv4, q3 = kv.reshape(2, B, S, HD), q.reshape(B, S, HD)
    o = _attn_fwd(q3, kv4, bias, B, S).reshape(T, HD)
    h3 = _mm(o, full["b_w_o"], layer=0, res=h2, name="attn_out")
    (dh, loss8, dg_final), saved1 = ffn_fwd(h3, 1, loss=(final_norm_g, target))

    units = {}

    in_flight = {}

    def swap_start(group, tag, carry):
        us = [units[n] for n in group]
        lands = [lax.empty(u.shape[1:], BF16) for u in us]
        carry, flight = tied(carry, _split_copies(f"swap_start_{tag}", us, lands, 1, _swap_copies, after=carry))
        return (group, tag, flight), carry

    def reduce_start(swap, after):
        group, tag, flight = swap
        us, got = _split_copies(f"swap_wait_{tag}", [fl[2] for fl in flight], [fl[3] for fl in flight], 1,
                                _swap_copies, flight=flight, after=after)
        sums = [_add_pair(u, g_, core, name=f"pair_{n}") for n, u, g_ in zip(group, us, got)]
        after, flight = tied(after, _scatter_start(sums, name=f"scatter_start_{tag}"))
        in_flight.update(zip(group, flight))
        return after

    def ffn_bwd(dh, h, saved, l, early):
        a, c, n, yff = saved
        units[f"f_w_down{l}"] = _mm_tn(yff, dh, rows_are_shards=True, name=f"ffn{l}_down_dw")
        dh_in = dh
        if early:
            sw, dh_in = swap_start([f"f_w_down{l}"], f"fd{l}", dh)
        dyff = _mm(dh_in, full[f"f_w_down{l}"], layer=0, trans_w=True, out_dtype=BF16, name=f"ffn{l}_down_dx")
        if early:
            dyff = reduce_start(sw, dyff)
        da, dcw, dcb = _conv_bwd(a, c, dyff, conv_w2[l], S)
        units[f"f_w_in{l}"] = _mm_tn(n, da, split_y=True, name=f"ffn{l}_in_dw")
        sw, da = swap_start([f"f_w_in{l}"] if early else [f"f_w_down{l}", f"f_w_in{l}"], f"f{l}", da)
        dh, dg = _mm(da, full[f"f_w_in{l}"], layer=0, trans_w=True, split_x=True, bwd=(h, f_norm_g[l], dh),
                     name=f"ffn{l}_in_dx")
        return reduce_start(sw, dh), dg, dcw, dcb

    dh, dg_f1, dcw1, dcb1 = ffn_bwd(dh, h3, saved1, 1, False)
    do = _mm(dh, full["b_w_o"], layer=0, trans_w=True, out_dtype=BF16, name="attn_out_dx")
    units["b_w_o"] = _mm_tn(o, dh, rows_are_shards=True, name="b_w_o_dw")
    dq, dkv, dbias = _attn_bwd(q3, kv4, bias, do.reshape(B, S, HD), B, S)
    dq, d_rel = lax.optimization_barrier((dq, _bias_reduce(dbias, n_rel)))
    d_rel = d_rel.reshape(1, H, n_rel)
    dq, dkv = dq.reshape(T, HD), dkv.reshape(2, T, HD)
    units["b_w_q"] = _mm_tn(n_q, dq, rows_are_shards=True, name="b_w_q_dw")
    units["w_kv"] = _mm_tn(n_kv, dkv, split_y=True, name="w_kv_dw")
    sw, dkv = swap_start(["b_w_o", "b_w_q", "w_kv"], "b", dkv)
    dh, dg_b, dg_kv = _qkv_dx(dq, full["b_w_q"], b_norm_g[0], dkv, full["w_kv"], kv_norm_g, h2, dh)
    dh = reduce_start(sw, dh)
    dh, dg_f0, dcw0, dcb0 = ffn_bwd(dh, h1, saved0, 0, True)
    units["a_w_out"] = _mm_tn(out_a, dh, rows_are_shards=True, name="a_w_out_dw")
    sw, dh_in = swap_start(["a_w_out"], "ao", dh)
    d_out = _mm(dh_in, full["a_w_out"], layer=0, trans_w=True, out_dtype=BF16, name="a_out_dx")
    d_out = reduce_start(sw, d_out)
    dzp, dws, dbs, dgv = _gate_bwd(zp, d_out, a_vg, ws_a, bs_tile)
    units["a_w_in"] = _mm_tn(n_a, dzp, name="a_w_in_dw")
    sw, dzp_in = swap_start(["a_w_in"], "ai", dzp)
    grad_x, dg_a = _mm(dzp_in, full["a_w_in"], layer=0, trans_w=True, bwd=(h0, a_g[0], dh), name="a_in_dx")
    grad_x = reduce_start(sw, grad_x)

    to_flat = lambda d: d.transpose(1, 0, 2).reshape(3, F2)
    small = {"a_norm_g": dg_a, "a_v_norm_g": dgv, "a_w_s": dws[None], "a_b_s": dbs[None], "kv_norm_g": dg_kv[0],
             "b_norm_g": dg_b, "b_rel_bias": d_rel, "f_norm_g": jnp.concatenate([dg_f0, dg_f1], axis=0),
             "f_conv_w": jnp.stack([to_flat(dcw0), to_flat(dcw1)]),
             "f_conv_b": jnp.stack([dcb0.reshape(F2), dcb1.reshape(F2)]), "final_norm_g": dg_final[0]}
    snames = list(small)
    small_vec = _pack([small[n] for n in snames] + [loss8[0:1, 0:1]])
    grad_x, small_flight = tied(grad_x, _split_copies("small_start", [small_vec],
                                                      [lax.empty((8,) + small_vec.shape, F32)], 7, _gather8_copies,
                                                      after=grad_x))

    sums, recv = _scatter_wait([in_flight[n] for n in g_names], grad_x)
    sums, recv = dict(zip(g_names, sums)), dict(zip(g_names, recv))
    halves = []
    for n, w in zip(names, ws):
        if w.L == 1:
            halves.append(_sum_chips(w, sums[n], recv[n], pos, name=f"chips_{n}"))
        else:
            first = _sum_chips(w, sums[n + "0"], recv[n + "0"], pos, name=f"chips_{n}0")
            halves.append(_sum_chips(w, sums[n + "1"], recv[n + "1"], pos, layer=1, into=first, name=f"chips_{n}1"))
    g_big = dict(zip(names, _join_halves(ws, halves)))
    g_big["w_kv"] = g_big["w_kv"][0]

    given = dict(a_norm_g=(a_norm_g, m_a_norm_g, v_a_norm_g), a_w_in=(a_w_in, m_a_w_in, v_a_w_in),
                 a_v_norm_g=(a_v_norm_g, m_a_v_norm_g, v_a_v_norm_g), a_w_s=(a_w_s, m_a_w_s, v_a_w_s),
                 a_b_s=(a_b_s, m_a_b_s, v_a_b_s), a_w_out=(a_w_out, m_a_w_out, v_a_w_out),
                 kv_norm_g=(kv_norm_g, m_kv_norm_g, v_kv_norm_g), w_kv=(w_kv, m_w_kv, v_w_kv),
                 b_norm_g=(b_norm_g, m_b_norm_g, v_b_norm_g), b_w_q=(b_w_q, m_b_w_q, v_b_w_q),
                 b_rel_bias=(b_rel_bias, m_b_rel_bias, v_b_rel_bias), b_w_o=(b_w_o, m_b_w_o, v_b_w_o),
                 f_norm_g=(f_norm_g, m_f_norm_g, v_f_norm_g), f_w_in=(f_w_in, m_f_w_in, v_f_w_in),
                 f_conv_w=(f_conv_w, m_f_conv_w, v_f_conv_w), f_conv_b=(f_conv_b, m_f_conv_b, v_f_conv_b),
                 f_w_down=(f_w_down, m_f_w_down, v_f_w_down), final_norm_g=(final_norm_g, m_final_norm_g, v_final_norm_g))
    order = list(given)
    grads, deltas, new_m, new_v = {}, {}, {}, {}
    for n in names:
        w_, m_, v_ = given[n]
        g_ = g_big[n]
        C = w_.shape[-1]
        d2, m2, v2 = _adamw(w_.reshape(-1, C), g_.reshape(-1, C), m_.reshape(-1, C), v_.reshape(-1, C),
                            name=f"adamw_{n}")
        grads[n], deltas[n], new_m[n], new_v[n] = g_.reshape(w_.shape), d2.reshape(w_.shape), m2.reshape(w_.shape), \
            v2.reshape(w_.shape)
    vecs, lands = _split_copies("small_wait", [small_flight[0][2]], [small_flight[0][3]], 7, _gather8_copies,
                                flight=small_flight, after=deltas[names[-1]])
    red = _sum8(lands[0], vecs[0], (4 * xi + 2 * yi + ci).astype(jnp.int32))
    parts = _unpack(red, [small[n].shape for n in snames] + [(1,)])
    g_small = dict(zip(snames, parts[:-1]))
    loss = parts[-1][0]
    g_small["a_norm_g"] = lax.dynamic_slice_in_dim(g_small["a_norm_g"], j_me * nsd, nsd, axis=1)
    g_small["a_v_norm_g"] = lax.dynamic_slice_in_dim(g_small["a_v_norm_g"], j_me * nsg, nsg, axis=1)
    g_small["f_conv_w"] = lax.dynamic_slice_in_dim(g_small["f_conv_w"], j_me * nsf, nsf, axis=2)

    sm = [n for n in order if n not in names]
    d2, m2, v2 = _adamw(_pack([given[n][0] for n in sm]), _pack([g_small[n].reshape(given[n][0].shape) for n in sm]),
                        _pack([given[n][1] for n in sm]), _pack([given[n][2] for n in sm]), name="adamw_small")
    shapes = [given[n][0].shape for n in sm]
    for n, d_, m_, v_ in zip(sm, _unpack(d2, shapes), _unpack(m2, shapes), _unpack(v2, shapes)):
        grads[n], deltas[n], new_m[n], new_v[n] = g_small[n].reshape(given[n][0].shape), d_, m_, v_

    return (loss, grad_x.reshape(B, S, D), *[grads[n] for n in order], *[deltas[n] for n in order],
            *[new_m[n] for n in order], *[new_v[n] for n in order])
```

```python
import functools
import math

import numpy as np
import jax
import jax.numpy as jnp
from jax import lax
from jax.experimental import pallas as pl
from jax.experimental.pallas import tpu as pltpu

F32 = jnp.float32
BF16 = jnp.bfloat16
MESH = pl.DeviceIdType.MESH

EPS = 1e-6
NEG_INF = -1e30
CHUNK = 64
GMLP_BLOCK = 128
GROUP_DIM = 128
HEAD_DIM = 64
LEFT_CHUNKS = 8
PAD = LEFT_CHUNKS * CHUNK
REL_CLIP = 128
Q_BLOCK = 256
K_SPAN = PAD + Q_BLOCK
F_LEN = K_SPAN + Q_BLOCK
HEADS_PER_STEP = 4
N_CHIPS = 4

ADAM_LR = 0.001
ADAM_B1 = 0.9
ADAM_B2 = 0.999
ADAM_EPS = 1e-08
ADAM_WD = 0.01
ADAM_STEP = 10

VMEM_LIMIT = 56 * 1024 * 1024


def _params(sem=None, **kw):
    if sem is not None:
        kw["dimension_semantics"] = sem
    return pltpu.CompilerParams(vmem_limit_bytes=VMEM_LIMIT, **kw)


def _rms(xf):
    r = lax.rsqrt(jnp.mean(xf * xf, axis=-1, keepdims=True) + EPS)
    return xf * r, r


def _gelu(x, with_grad=False):
    c = math.sqrt(2.0 / math.pi)
    x2 = x * x
    t = jnp.tanh(c * x * (1.0 + 0.044715 * x2))
    half = 0.5 * (1.0 + t)
    if not with_grad:
        return x * half
    return x * half, half + 0.5 * x * (1.0 - t * t) * c * (1.0 + 3.0 * 0.044715 * x2)


def _col_tile(n):
    if n <= 1024:
        return n
    for t in (1408, 1024, 512):
        if n % t == 0:
            return t
    raise ValueError(n)


def _row_tile(t, want):
    while t % want:
        want //= 2
    return want


def _loss_epilogue(h, g_ref, t_ref, dh_ref, loss_ref, dg_ref, first):
    @pl.when(first)
    def _():
        loss_ref[...] = jnp.zeros_like(loss_ref)
        dg_ref[...] = jnp.zeros_like(dg_ref)

    n, r = _rms(h)
    g = g_ref[...]
    e = n * g - t_ref[...]
    loss_ref[...] += 0.5 * jnp.sum(jnp.mean(e * e, axis=-1, keepdims=True), axis=0, keepdims=True)
    dy = e * (1.0 / h.shape[-1])
    dg_ref[...] += jnp.sum(dy * n, axis=0, keepdims=True)
    t = dy * g
    dh_ref[...] = r * (t - n * jnp.mean(t * n, axis=-1, keepdims=True))


def _mm(x, w, *, name, trans_w=False, res=None, out_dtype=F32, bwd=None, split_x=False, tm=512):
    T = x.shape[-2]
    K = 2 * x.shape[-1] if split_x else x.shape[-1]
    N = w.shape[-2] if trans_w else w.shape[-1]
    tm = _row_tile(T, 2 * tm if max(K, N) <= 2048 else tm)
    has_res, has_bwd = res is not None, bwd is not None
    dims = (((1,), (1,)), ((), ())) if trans_w else (((1,), (0,)), ((), ()))

    def body(*refs):
        it = iter(refs)
        x_ref, w_ref = next(it), next(it)
        res_ref = next(it) if has_res else None
        if has_bwd:
            h_ref, bg_ref, dh_ref = next(it), next(it), next(it)
        o_ref = next(it)
        if split_x:
            kh = K // 2
            acc = lax.dot_general(x_ref[0].astype(BF16), w_ref[:, :kh] if trans_w else w_ref[:kh, :], dims,
                                  preferred_element_type=F32)
            acc = acc + lax.dot_general(x_ref[1].astype(BF16), w_ref[:, kh:] if trans_w else w_ref[kh:, :], dims,
                                        preferred_element_type=F32)
        else:
            acc = lax.dot_general(x_ref[...].astype(BF16), w_ref[...], dims, preferred_element_type=F32)
        if has_res:
            acc = acc + res_ref[...]
        if has_bwd:
            dg_ref = next(it)
            n, r = _rms(h_ref[...])

            @pl.when(pl.program_id(0) == 0)
            def _():
                dg_ref[...] = jnp.zeros_like(dg_ref)

            dg_ref[...] += jnp.sum(acc * n, axis=0, keepdims=True)
            t = acc * bg_ref[...]
            o_ref[...] = dh_ref[...] + r * (t - n * jnp.mean(t * n, axis=-1, keepdims=True))
        else:
            o_ref[...] = acc.astype(out_dtype)

    row = lambda width: pl.BlockSpec((tm, width), lambda m: (m, 0))
    ins = [x, w]
    in_specs = [pl.BlockSpec((2, tm, K // 2), lambda m: (0, m, 0)) if split_x else row(K),
                pl.BlockSpec((None,) + w.shape[1:], lambda m: (0, 0, 0), pipeline_mode=pl.Buffered(1))]
    if has_res:
        ins.append(res)
        in_specs.append(row(N))
    out_shape = [jax.ShapeDtypeStruct((T, N), F32 if has_bwd else out_dtype)]
    out_specs = [row(N)]
    if has_bwd:
        h, g, dh = bwd
        ins += [h, g.reshape(1, N), dh]
        in_specs += [row(N), pl.BlockSpec((1, N), lambda m: (0, 0)), row(N)]
        out_shape.append(jax.ShapeDtypeStruct((1, N), F32))
        out_specs.append(pl.BlockSpec((1, N), lambda m: (0, 0)))
    out = pl.pallas_call(body, name=name, grid=(T // tm,), in_specs=in_specs, out_specs=out_specs,
                         out_shape=out_shape, compiler_params=_params(("arbitrary",)))(*ins)
    return out if has_bwd else out[0]


def _mm_tn(x, dy, *, name, rows_are_shards=False, split_y=False, tt=1024):
    T, K = x.shape
    N = 2 * dy.shape[-1] if split_y else dy.shape[-1]
    R, C = (K // N_CHIPS, N // 2) if rows_are_shards else (K // 2, N // N_CHIPS)
    nn = 2 if split_y else 1
    tn = N // nn
    per = N_CHIPS // nn
    assert not (rows_are_shards and split_y)
    tt = _row_tile(T, tt)
    nt = T // tt

    def body(x_ref, y_ref, o_ref, acc_ref):
        t = pl.program_id(1)

        @pl.when(t == 0)
        def _():
            acc_ref[...] = jnp.zeros_like(acc_ref)

        acc_ref[...] += lax.dot_general(x_ref[...], y_ref[...].astype(BF16), (((0,), (0,)), ((), ())),
                                        preferred_element_type=F32)

        @pl.when(t == nt - 1)
        def _():
            if rows_are_shards:
                for h in range(2):
                    o_ref[h] = acc_ref[:, h * C:(h + 1) * C].astype(BF16).reshape(N_CHIPS, R, C)
            else:
                for j in range(per):
                    o_ref[:, j] = acc_ref[:, j * C:(j + 1) * C].astype(BF16).reshape(2, R, C)

    if split_y:
        yspec = pl.BlockSpec((None, tt, tn), lambda n, t: (n, t, 0))
    else:
        yspec = pl.BlockSpec((tt, tn), lambda n, t: (t, 0))
    if rows_are_shards:
        out_spec = pl.BlockSpec((2, N_CHIPS, R, C), lambda n, t: (0, 0, 0, 0))
    else:
        out_spec = pl.BlockSpec((2, per, R, C), lambda n, t: (0, n, 0, 0))
    return pl.pallas_call(body, name=name, grid=(nn, nt),
                          in_specs=[pl.BlockSpec((tt, K), lambda n, t: (t, 0)), yspec], out_specs=out_spec,
                          out_shape=jax.ShapeDtypeStruct((2, N_CHIPS, R, C), BF16),
                          scratch_shapes=[pltpu.VMEM((K, tn), F32)],
                          compiler_params=_params(("arbitrary", "arbitrary")))(x, dy)


def _qkv_fwd(h, wq, gq, wkv, gkv, scale, *, tm=512):
    T, D = h.shape
    HD = wq.shape[-1]
    tm = _row_tile(T, tm)

    def body(h_ref, wq_ref, gq_ref, wkv_ref, gkv_ref, q_ref, kv_ref, nq_ref, nkv_ref):
        n = _rms(h_ref[...])[0]
        nq = (n * gq_ref[...]).astype(BF16)
        nkv = (n * gkv_ref[...]).astype(BF16)
        nq_ref[...] = nq
        nkv_ref[...] = nkv
        q_ref[...] = (jnp.dot(nq, wq_ref[...], preferred_element_type=F32) * scale).astype(BF16)
        kv = jnp.dot(nkv, wkv_ref[...], preferred_element_type=F32)
        kv_ref[0] = kv[:, :HD].astype(BF16)
        kv_ref[1] = kv[:, HD:].astype(BF16)

    row = lambda width: pl.BlockSpec((tm, width), lambda i: (i, 0))
    fixed = lambda *shape: pl.BlockSpec(shape, lambda i: (0,) * len(shape))
    weight = lambda n: pl.BlockSpec((None, D, n), lambda i: (0, 0, 0), pipeline_mode=pl.Buffered(1))
    return pl.pallas_call(
        body, name="qkv", grid=(T // tm,),
        in_specs=[row(D), weight(HD), fixed(1, D), weight(2 * HD), fixed(1, D)],
        out_specs=[row(HD), pl.BlockSpec((2, tm, HD), lambda i: (0, i, 0)), row(D), row(D)],
        out_shape=[jax.ShapeDtypeStruct((T, HD), BF16), jax.ShapeDtypeStruct((2, T, HD), BF16),
                   jax.ShapeDtypeStruct((T, D), BF16), jax.ShapeDtypeStruct((T, D), BF16)],
        compiler_params=_params(("arbitrary",)))(h, wq, gq.reshape(1, D), wkv, gkv.reshape(1, D))


def _qkv_dx(dq, wq, gq, dkv, wkv, gkv, h, dh, *, tm=512):
    T, D = h.shape
    HD = wq.shape[-1]
    tm = _row_tile(T, tm)
    nt = (((1,), (1,)), ((), ()))

    def body(dq_ref, wq_ref, gq_ref, dkv_ref, wkv_ref, gkv_ref, h_ref, dh_ref, o_ref, dgq_ref, dgkv_ref):
        @pl.when(pl.program_id(0) == 0)
        def _():
            dgq_ref[...] = jnp.zeros_like(dgq_ref)
            dgkv_ref[...] = jnp.zeros_like(dgkv_ref)

        n, r = _rms(h_ref[...])
        dnq = lax.dot_general(dq_ref[...], wq_ref[...], nt, preferred_element_type=F32)
        dnkv = (lax.dot_general(dkv_ref[0], wkv_ref[:, :HD], nt, preferred_element_type=F32)
                + lax.dot_general(dkv_ref[1], wkv_ref[:, HD:], nt, preferred_element_type=F32))
        dgq_ref[...] += jnp.sum(dnq * n, axis=0, keepdims=True)
        dgkv_ref[...] += jnp.sum(dnkv * n, axis=0, keepdims=True)
        t = dnq * gq_ref[...] + dnkv * gkv_ref[...]
        o_ref[...] = dh_ref[...] + r * (t - n * jnp.mean(t * n, axis=-1, keepdims=True))

    row = lambda width: pl.BlockSpec((tm, width), lambda i: (i, 0))
    fixed = lambda *shape: pl.BlockSpec(shape, lambda i: (0,) * len(shape))
    weight = lambda n: pl.BlockSpec((None, D, n), lambda i: (0, 0, 0), pipeline_mode=pl.Buffered(1))
    return pl.pallas_call(
        body, name="qkv_dx", grid=(T // tm,),
        in_specs=[row(HD), weight(HD), fixed(1, D), pl.BlockSpec((2, tm, HD), lambda i: (0, i, 0)), weight(2 * HD),
                  fixed(1, D), row(D), row(D)],
        out_specs=[row(D), fixed(1, D), fixed(1, D)],
        out_shape=[jax.ShapeDtypeStruct((T, D), F32), jax.ShapeDtypeStruct((1, D), F32),
                   jax.ShapeDtypeStruct((1, D), F32)],
        compiler_params=_params(("arbitrary",)))(dq, wq, gq.reshape(1, D), dkv, wkv, gkv.reshape(1, D), h, dh)


def _chunk_mask():
    i = lax.broadcasted_iota(jnp.int32, (GMLP_BLOCK, GMLP_BLOCK), 0) // CHUNK
    j = lax.broadcasted_iota(jnp.int32, (GMLP_BLOCK, GMLP_BLOCK), 1) // CHUNK
    return i >= j


def _mixer_a_fwd(h, w_in, g, gv, ws, bs_tile, w_out, *, tm=256):
    T, D = h.shape
    W = w_out.shape[-2]
    G = W // GROUP_DIM
    tm = _row_tile(T, tm)

    def body(h_ref, wi_ref, g_ref, gv_ref, ws_ref, bs_ref, wo_ref, o_ref, zp_ref, ga_ref, n_ref):
        nb = (_rms(h_ref[...])[0] * g_ref[...]).astype(BF16)
        n_ref[...] = nb
        zpb = jnp.dot(nb, wi_ref[...], preferred_element_type=F32).astype(BF16)
        zp_ref[...] = zpb
        z = _gelu(zpb.astype(F32))
        u, v = z[:, :W], z[:, W:]
        vn = _rms(v)[0] * gv_ref[...]
        mask = _chunk_mask()
        for gi in range(G):
            cs = slice(gi * GROUP_DIM, (gi + 1) * GROUP_DIM)
            wg = jnp.where(mask, ws_ref[gi], 0.0).astype(BF16)
            for b in range(tm // GMLP_BLOCK):
                rs = slice(b * GMLP_BLOCK, (b + 1) * GMLP_BLOCK)
                s = jnp.dot(wg, vn[rs, cs].astype(BF16), preferred_element_type=F32) + bs_ref[:, cs]
                ga_ref[rs, cs] = (u[rs, cs] * s).astype(BF16)
        o_ref[...] = h_ref[...] + jnp.dot(ga_ref[...], wo_ref[...], preferred_element_type=F32)

    row = lambda width: pl.BlockSpec((tm, width), lambda i: (i, 0))
    fixed = lambda *shape: pl.BlockSpec(shape, lambda i: (0,) * len(shape))
    weight = lambda k, n: pl.BlockSpec((None, k, n), lambda i: (0, 0, 0), pipeline_mode=pl.Buffered(1))
    return pl.pallas_call(
        body, name="mixer_a", grid=(T // tm,),
        in_specs=[row(D), weight(D, 2 * W), fixed(1, D), fixed(1, W), fixed(G, GMLP_BLOCK, GMLP_BLOCK),
                  fixed(GMLP_BLOCK, W), weight(W, D)],
        out_specs=[row(D), row(2 * W), row(W), row(D)],
        out_shape=[jax.ShapeDtypeStruct((T, D), F32), jax.ShapeDtypeStruct((T, 2 * W), BF16),
                   jax.ShapeDtypeStruct((T, W), BF16), jax.ShapeDtypeStruct((T, D), BF16)],
        compiler_params=_params(("arbitrary",)))(h, w_in, g.reshape(1, D), gv, ws, bs_tile, w_out)


def _gate_bwd(zp, d_out, gv, ws, bs_tile, *, tm=256):
    T, W2 = zp.shape
    W = W2 // 2
    G = W // GROUP_DIM
    tm = _row_tile(T, tm)
    nm = T // tm

    def body(zp_ref, do_ref, gv_ref, ws_ref, bs_ref, dzp_ref, dws_ref, dbs_ref, dgv_ref, du_scr, dvn_scr, dsum_scr):
        i = pl.program_id(0)

        @pl.when(i == 0)
        def _():
            dws_ref[...] = jnp.zeros_like(dws_ref)
            dgv_ref[...] = jnp.zeros_like(dgv_ref)
            dsum_scr[...] = jnp.zeros_like(dsum_scr)

        zp = zp_ref[...].astype(F32)
        z, dz = _gelu(zp, with_grad=True)
        u, v = z[:, :W], z[:, W:]
        n, r = _rms(v)
        gv = gv_ref[...]
        vn = n * gv
        d_out = do_ref[...].astype(F32)
        mask = _chunk_mask()
        for g in range(G):
            cs = slice(g * GROUP_DIM, (g + 1) * GROUP_DIM)
            wg = jnp.where(mask, ws_ref[g], 0.0).astype(BF16)
            dw = jnp.zeros((GMLP_BLOCK, GMLP_BLOCK), F32)
            for b in range(tm // GMLP_BLOCK):
                rs = slice(b * GMLP_BLOCK, (b + 1) * GMLP_BLOCK)
                vb = vn[rs, cs].astype(BF16)
                s = jnp.dot(wg, vb, preferred_element_type=F32) + bs_ref[:, cs]
                du_scr[rs, cs] = d_out[rs, cs] * s
                ds = d_out[rs, cs] * u[rs, cs]
                dsb = ds.astype(BF16)
                dvn_scr[rs, cs] = lax.dot_general(wg, dsb, (((0,), (0,)), ((), ())), preferred_element_type=F32)
                dw = dw + lax.dot_general(dsb, vb, (((1,), (1,)), ((), ())), preferred_element_type=F32)
                dsum_scr[:, cs] += ds
            dws_ref[g] += jnp.where(mask, dw, 0.0)
        dvn = dvn_scr[...]
        dgv_ref[...] += jnp.sum(dvn * n, axis=0, keepdims=True)
        t = dvn * gv
        dv = r * (t - n * jnp.mean(t * n, axis=-1, keepdims=True))
        dzp_ref[:, :W] = (du_scr[...] * dz[:, :W]).astype(BF16)
        dzp_ref[:, W:] = (dv * dz[:, W:]).astype(BF16)

        @pl.when(i == nm - 1)
        def _():
            sel = (lax.broadcasted_iota(jnp.int32, (G, W), 1) // GROUP_DIM
                   == lax.broadcasted_iota(jnp.int32, (G, W), 0)).astype(F32)
            dbs_ref[...] = lax.dot_general(sel, dsum_scr[...], (((1,), (1,)), ((), ())),
                                           precision=lax.Precision.HIGHEST, preferred_element_type=F32)

    return pl.pallas_call(
        body, name="gate_bwd", grid=(nm,),
        in_specs=[pl.BlockSpec((tm, W2), lambda i: (i, 0)), pl.BlockSpec((tm, W), lambda i: (i, 0)),
                  pl.BlockSpec((1, W), lambda i: (0, 0)),
                  pl.BlockSpec((G, GMLP_BLOCK, GMLP_BLOCK), lambda i: (0, 0, 0)),
                  pl.BlockSpec((GMLP_BLOCK, W), lambda i: (0, 0))],
        out_specs=[pl.BlockSpec((tm, W2), lambda i: (i, 0)),
                   pl.BlockSpec((G, GMLP_BLOCK, GMLP_BLOCK), lambda i: (0, 0, 0)),
                   pl.BlockSpec((G, GMLP_BLOCK), lambda i: (0, 0)), pl.BlockSpec((1, W), lambda i: (0, 0))],
        out_shape=[jax.ShapeDtypeStruct((T, W2), BF16), jax.ShapeDtypeStruct((G, GMLP_BLOCK, GMLP_BLOCK), F32),
                   jax.ShapeDtypeStruct((G, GMLP_BLOCK), F32), jax.ShapeDtypeStruct((1, W), F32)],
        scratch_shapes=[pltpu.VMEM((tm, W), F32), pltpu.VMEM((tm, W), F32), pltpu.VMEM((GMLP_BLOCK, W), F32)],
        compiler_params=_params(("arbitrary",)))(zp, d_out, gv, ws, bs_tile)


LANES = 128
HALO = 16


def _taps(ext, w, b):
    return w[2:3] * ext[HALO:] + w[1:2] * pltpu.roll(ext, 1, 0)[HALO:] + w[0:1] * pltpu.roll(ext, 2, 0)[HALO:] + b


def _ffn_fwd(h, w, g, cw, cb, wd, S, *, name, loss=None, tm=256):
    T, D = h.shape
    F = w.shape[-1] // 2
    tc = _col_tile(F)
    tm = _row_tile(S, tm)
    has_loss = loss is not None

    def body(*refs):
        h_ref, w_ref, g_ref, cw_ref, cb_ref, wd_ref = refs[:6]
        o_ref, y_ref, a_ref, c_ref, n_ref = refs[8:13] if has_loss else refs[6:11]
        tail = refs[-1]
        first = (pl.program_id(0) * tm) % S == 0
        nb = (_rms(h_ref[...])[0] * g_ref[...]).astype(BF16)
        n_ref[...] = nb
        for j in range(F // tc):
            cs = slice(j * tc, (j + 1) * tc)
            conv = []
            for s in range(2):
                acc = jnp.dot(nb, w_ref[:, s * F + j * tc:s * F + (j + 1) * tc], preferred_element_type=F32)
                ab = acc.astype(BF16)
                a_ref[s, :, cs] = ab
                af = ab.astype(F32)
                ext = jnp.concatenate([jnp.where(first, 0.0, tail[s, :, cs]), af], axis=0)
                tail[s, :, cs] = af[tm - HALO:, :]
                cv = _taps(ext, cw_ref[s, :, cs], cb_ref[s:s + 1, cs]).astype(BF16)
                c_ref[s, :, cs] = cv
                conv.append(cv.astype(F32))
            up, gate = conv
            y_ref[:, cs] = (gate * jax.nn.sigmoid(gate) * up).astype(BF16)
        out = h_ref[...] + jnp.dot(y_ref[...], wd_ref[...], preferred_element_type=F32)
        if has_loss:
            _loss_epilogue(out, refs[6], refs[7], o_ref, refs[13], refs[14], pl.program_id(0) == 0)
        else:
            o_ref[...] = out

    row = lambda width: pl.BlockSpec((tm, width), lambda i: (i, 0))
    wide = pl.BlockSpec((2, tm, F), lambda i: (0, i, 0))
    fixed = lambda *shape: pl.BlockSpec(shape, lambda i: (0,) * len(shape))
    once = pl.Buffered(1)
    ins = [h, w, g.reshape(1, D), cw, cb, wd]
    in_specs = [row(D), pl.BlockSpec((None, D, 2 * F), lambda i: (0, 0, 0), pipeline_mode=once), fixed(1, D),
                fixed(2, 3, F), fixed(2, F), pl.BlockSpec((None, F, D), lambda i: (0, 0, 0), pipeline_mode=once)]
    out_specs = [row(D), row(F), wide, wide, row(D)]
    out_shape = [jax.ShapeDtypeStruct((T, D), F32), jax.ShapeDtypeStruct((T, F), BF16),
                 jax.ShapeDtypeStruct((2, T, F), BF16), jax.ShapeDtypeStruct((2, T, F), BF16),
                 jax.ShapeDtypeStruct((T, D), BF16)]
    if has_loss:
        ins += [loss[0].reshape(1, D), loss[1]]
        in_specs += [fixed(1, D), row(D)]
        out_specs += [fixed(8, 128), fixed(1, D)]
        out_shape += [jax.ShapeDtypeStruct((8, 128), F32), jax.ShapeDtypeStruct((1, D), F32)]
    return pl.pallas_call(body, name=name, grid=(T // tm,), in_specs=in_specs, out_specs=out_specs,
                          out_shape=out_shape, scratch_shapes=[pltpu.VMEM((2, HALO, F), F32)],
                          compiler_params=_params(("arbitrary",)))(*ins)


def _conv_bwd(a, c, dy, cw, S, *, tm=256):
    _, T, F = a.shape
    tc = _col_tile(F)
    tm = _row_tile(S, tm)
    nm = T // tm
    hb = tm // HALO
    TE = tm + HALO
    nxt = lambda j, i: jnp.minimum((i + 1) * hb, T // HALO - 1)

    def body(a_ref, c_ref, nc_ref, dy_ref, ndy_ref, w_ref, da_ref, dw_ref, db_ref):
        i = pl.program_id(1)
        last = ((i + 1) * tm) % S == 0
        keep_n = jnp.where(last, 0.0, 1.0)

        @pl.when(i == 0)
        def _():
            dw_ref[...] = jnp.zeros_like(dw_ref)
            db_ref[...] = jnp.zeros_like(db_ref)

        for j in range(tc // LANES):
            cs = slice(j * LANES, (j + 1) * LANES)
            dyf = jnp.concatenate([dy_ref[:, cs].astype(F32), ndy_ref[:, cs].astype(F32) * keep_n], axis=0)
            up = jnp.concatenate([c_ref[0, :, cs].astype(F32), nc_ref[0, :, cs].astype(F32)], axis=0)
            gate = jnp.concatenate([c_ref[1, :, cs].astype(F32), nc_ref[1, :, cs].astype(F32)], axis=0)
            sg = jax.nn.sigmoid(gate)
            for s, d in ((0, dyf * (gate * sg)), (1, dyf * up * (sg * (1.0 + gate * (1.0 - sg))))):
                a = a_ref[s, :, cs].astype(F32)
                w = w_ref[s, :, cs]
                u1, u2 = pltpu.roll(d, TE - 1, 0), pltpu.roll(d, TE - 2, 0)
                db_ref[s:s + 1, cs] += jnp.sum(d[:tm], axis=0, keepdims=True)
                dw_ref[s, 2:3, cs] += jnp.sum(d[:tm] * a, axis=0, keepdims=True)
                dw_ref[s, 1:2, cs] += jnp.sum(u1[:tm] * a, axis=0, keepdims=True)
                dw_ref[s, 0:1, cs] += jnp.sum(u2[:tm] * a, axis=0, keepdims=True)
                da_ref[s, :, cs] = (w[2:3] * d + w[1:2] * u1 + w[0:1] * u2)[:tm].astype(BF16)

    cur = pl.BlockSpec((2, tm, tc), lambda j, i: (0, i, j))
    return pl.pallas_call(
        body, name="conv_bwd", grid=(F // tc, nm),
        in_specs=[cur, cur, pl.BlockSpec((2, HALO, tc), lambda j, i: (0, nxt(j, i), j)),
                  pl.BlockSpec((tm, tc), lambda j, i: (i, j)), pl.BlockSpec((HALO, tc), lambda j, i: (nxt(j, i), j)),
                  pl.BlockSpec((2, 3, tc), lambda j, i: (0, 0, j))],
        out_specs=[cur, pl.BlockSpec((2, 3, tc), lambda j, i: (0, 0, j)), pl.BlockSpec((2, tc), lambda j, i: (0, j))],
        out_shape=[jax.ShapeDtypeStruct((2, T, F), BF16), jax.ShapeDtypeStruct((2, 3, F), F32),
                   jax.ShapeDtypeStruct((2, F), F32)],
        compiler_params=_params(("arbitrary", "arbitrary")))(a, c, c, dy, dy, cw)


def _bias_index():
    idx = np.arange(F_LEN)
    d = np.where(idx < K_SPAN, idx, idx - F_LEN)
    return np.clip(PAD - d, -REL_CLIP, REL_CLIP) + REL_CLIP


ROW_GROUP = 16


def _roll_rows(x, sign, unit, steps):
    rows = lax.broadcasted_iota(jnp.int32, x.shape, 0)
    step = 1
    while step < steps:
        shift = unit * step if sign > 0 else F_LEN - unit * step
        x = jnp.where((rows & step) != 0, pltpu.roll(x, shift, 1), x)
        step *= 2
    return x


def _bias_expand(frow):
    H = frow.shape[0]
    groups = Q_BLOCK // ROW_GROUP

    def body(f_ref, o_ref):
        coarse = _roll_rows(jnp.broadcast_to(f_ref[...], (groups, F_LEN)), 1, ROW_GROUP, groups)
        x = jnp.concatenate([jnp.broadcast_to(coarse[a:a + 1], (ROW_GROUP, F_LEN)) for a in range(groups)], axis=0)
        x = _roll_rows(x, 1, 1, ROW_GROUP)[:, :K_SPAN]
        qc = lax.broadcasted_iota(jnp.int32, (Q_BLOCK, K_SPAN), 0) // CHUNK * CHUNK
        kj = lax.broadcasted_iota(jnp.int32, (Q_BLOCK, K_SPAN), 1)
        o_ref[...] = jnp.where((kj >= qc) & (kj < qc + PAD + CHUNK), x, NEG_INF)

    return pl.pallas_call(
        body, name="bias_expand", grid=(H,),
        in_specs=[pl.BlockSpec((None, 1, F_LEN), lambda h: (h, 0, 0))],
        out_specs=pl.BlockSpec((None, Q_BLOCK, K_SPAN), lambda h: (h, 0, 0)),
        out_shape=jax.ShapeDtypeStruct((H, Q_BLOCK, K_SPAN), F32), compiler_params=_params(("arbitrary",)))(frow)


def _bias_reduce(dbias, n_rel):
    H = dbias.shape[0]
    onehot = jnp.asarray((_bias_index()[:, None] == np.arange(n_rel)[None, :]).astype(np.float32), dtype=BF16)

    def body(d_ref, oh_ref, o_ref):
        x = jnp.concatenate([d_ref[...], jnp.zeros((Q_BLOCK, F_LEN - K_SPAN), F32)], axis=1)
        fine = _roll_rows(x, -1, 1, ROW_GROUP).reshape(Q_BLOCK // ROW_GROUP, ROW_GROUP, F_LEN)
        coarse = _roll_rows(jnp.sum(fine, axis=1), -1, ROW_GROUP, Q_BLOCK // ROW_GROUP)
        row = jnp.broadcast_to(jnp.sum(coarse, axis=0, keepdims=True), (8, F_LEN))
        acc = jnp.zeros((8, n_rel), F32)
        for _ in range(3):
            piece = row.astype(BF16)
            acc = acc + jnp.dot(piece, oh_ref[...], preferred_element_type=F32)
            row = row - piece.astype(F32)
        o_ref[...] = acc[0:1]

    return pl.pallas_call(
        body, name="bias_reduce", grid=(H,),
        in_specs=[pl.BlockSpec((None, Q_BLOCK, K_SPAN), lambda h: (h, 0, 0)),
                  pl.BlockSpec((F_LEN, n_rel), lambda h: (0, 0))],
        out_specs=pl.BlockSpec((None, 1, n_rel), lambda h: (h, 0, 0)),
        out_shape=jax.ShapeDtypeStruct((H, 1, n_rel), F32), compiler_params=_params(("arbitrary",)))(dbias, onehot)


def _attn_specs(S):
    hw = HEADS_PER_STEP * HEAD_DIM
    qspec = pl.BlockSpec((None, Q_BLOCK, hw), lambda g, b, i: (b, i, g))
    kspec = pl.BlockSpec((None, None, S, hw), lambda g, b, i: (0, b, 0, g))
    vspec = pl.BlockSpec((None, None, S, hw), lambda g, b, i: (1, b, 0, g))
    bspec = pl.BlockSpec((HEADS_PER_STEP, Q_BLOCK, K_SPAN), lambda g, b, i: (g, 0, 0))
    return hw, qspec, kspec, vspec, bspec


def _span_cases(i, fn):
    short = PAD // Q_BLOCK
    for j in range(short):
        pl.when(i == j)(functools.partial(fn, PAD - j * Q_BLOCK))
    pl.when(i >= short)(functools.partial(fn, 0))


def _key_start(i, off):
    return 0 if off else pl.multiple_of(i * Q_BLOCK - PAD, Q_BLOCK)


def _attn_exp(q_ref, k_ref, b_ref, h, k0, off):
    hs = slice(h * HEAD_DIM, (h + 1) * HEAD_DIM)
    kh = k_ref[pl.ds(k0, K_SPAN - off), hs]
    s = lax.dot_general(q_ref[:, hs], kh, (((1,), (1,)), ((), ())), preferred_element_type=F32) + b_ref[h, :, off:]
    p = jnp.exp(s - jnp.max(s, axis=-1, keepdims=True))
    return p, 1.0 / jnp.sum(p, axis=-1, keepdims=True), kh


def _attn_fwd(q, kv, bias, B, S):
    HD = q.shape[-1]
    hw, qspec, kspec, vspec, bspec = _attn_specs(S)

    def body(q_ref, k_ref, v_ref, b_ref, o_ref):
        i = pl.program_id(2)

        def block(off):
            k0 = _key_start(i, off)
            outs = []
            for h in range(HEADS_PER_STEP):
                hs = slice(h * HEAD_DIM, (h + 1) * HEAD_DIM)
                p, inv, _ = _attn_exp(q_ref, k_ref, b_ref, h, k0, off)
                outs.append(jnp.dot(p.astype(BF16), v_ref[pl.ds(k0, K_SPAN - off), hs],
                                    preferred_element_type=F32) * inv)
            o_ref[...] = jnp.concatenate(outs, axis=1).astype(BF16)

        _span_cases(i, block)

    return pl.pallas_call(
        body, name="attn_fwd", grid=(HD // hw, B, S // Q_BLOCK), in_specs=[qspec, kspec, vspec, bspec],
        out_specs=qspec, out_shape=jax.ShapeDtypeStruct((B, S, HD), BF16),
        compiler_params=_params(("arbitrary", "arbitrary", "arbitrary")))(q, kv, kv, bias)


def _attn_bwd(q, kv, bias, do, B, S):
    HD = q.shape[-1]
    H = HD // HEAD_DIM
    hw, qspec, kspec, vspec, bspec = _attn_specs(S)
    scale = HEAD_DIM ** -0.5
    nq = S // Q_BLOCK

    def body(q_ref, k_ref, v_ref, b_ref, do_ref, dq_ref, dkv_ref, db_ref, dk_acc, dv_acc):
        b, i = pl.program_id(1), pl.program_id(2)

        @pl.when(i == 0)
        def _():
            dk_acc[...] = jnp.zeros_like(dk_acc)
            dv_acc[...] = jnp.zeros_like(dv_acc)

        @pl.when((i == 0) & (b == 0))
        def _():
            db_ref[...] = jnp.zeros_like(db_ref)

        def block(off):
            k0 = _key_start(i, off)
            keys = pl.ds(k0, K_SPAN - off)
            for h in range(HEADS_PER_STEP):
                hs = slice(h * HEAD_DIM, (h + 1) * HEAD_DIM)
                p, inv, kh = _attn_exp(q_ref, k_ref, b_ref, h, k0, off)
                p = p * inv
                doh = do_ref[:, hs]
                dp = lax.dot_general(doh, v_ref[keys, hs], (((1,), (1,)), ((), ())), preferred_element_type=F32)
                ds = p * (dp - jnp.sum(p * dp, axis=-1, keepdims=True))
                db_ref[h, :, off:] += ds
                dsb = ds.astype(BF16)
                dq_ref[:, hs] = (jnp.dot(dsb, kh, preferred_element_type=F32) * scale).astype(BF16)
                dk_acc[hs, keys] += lax.dot_general(q_ref[:, hs], dsb, (((0,), (0,)), ((), ())),
                                                     preferred_element_type=F32)
                dv_acc[hs, keys] += lax.dot_general(doh, p.astype(BF16), (((0,), (0,)), ((), ())),
                                                     preferred_element_type=F32)

        _span_cases(i, block)

        @pl.when(i == nq - 1)
        def _():
            dkv_ref[0] = dk_acc[...].T.astype(BF16)
            dkv_ref[1] = dv_acc[...].T.astype(BF16)

    return pl.pallas_call(
        body, name="attn_bwd", grid=(HD // hw, B, nq), in_specs=[qspec, kspec, vspec, bspec, qspec],
        out_specs=[qspec, pl.BlockSpec((2, None, S, hw), lambda g, b, i: (0, b, 0, g)), bspec],
        out_shape=[jax.ShapeDtypeStruct((B, S, HD), BF16), jax.ShapeDtypeStruct((2, B, S, HD), BF16),
                   jax.ShapeDtypeStruct((H, Q_BLOCK, K_SPAN), F32)],
        scratch_shapes=[pltpu.VMEM((hw, S), F32), pltpu.VMEM((hw, S), F32)],
        compiler_params=_params(("arbitrary", "arbitrary", "arbitrary")))(q, kv, kv, bias, do)


def _sub_rows(R):
    for cand in (256, 352, 128, 64, 8):
        if R % cand == 0 and R > cand:
            return cand
    return R


def _adamw(w, g, m, v, *, name):
    R, C = w.shape
    tr = _sub_rows(R)

    def body(w_ref, g_ref, m_ref, v_ref, d_ref, nm_ref, nv_ref):
        g = g_ref[...]
        m = ADAM_B1 * m_ref[...] + (1.0 - ADAM_B1) * g
        v = ADAM_B2 * v_ref[...] + (1.0 - ADAM_B2) * (g * g)
        m_hat = m / (1.0 - ADAM_B1 ** ADAM_STEP)
        v_hat = v / (1.0 - ADAM_B2 ** ADAM_STEP)
        d_ref[...] = -ADAM_LR * (m_hat / (jnp.sqrt(v_hat) + ADAM_EPS) + ADAM_WD * w_ref[...])
        nm_ref[...] = m
        nv_ref[...] = v

    spec = pl.BlockSpec((tr, C), lambda i: (i, 0))
    return pl.pallas_call(body, name=name, grid=(R // tr,), in_specs=[spec] * 4, out_specs=[spec] * 3,
                          out_shape=[jax.ShapeDtypeStruct((R, C), F32)] * 3,
                          compiler_params=_params(("arbitrary",)))(w, g, m, v)


def _add_pair(units, got, core, *, name):
    n4, R, C = got.shape
    rows = n4 * R
    tr = 512 if rows % 512 == 0 else R

    def body(c_ref, u_ref, got_ref, o_ref):
        o_ref[...] = (u_ref[...].astype(F32) + got_ref[...].astype(F32)).astype(BF16)

    spec = pl.BlockSpec((tr, C), lambda i, c: (i, 0))
    grid_spec = pltpu.PrefetchScalarGridSpec(
        num_scalar_prefetch=1, grid=(rows // tr,),
        in_specs=[pl.BlockSpec((None, tr, C), lambda i, c: (c[0], i, 0)), spec], out_specs=spec)
    out = pl.pallas_call(body, name=name, grid_spec=grid_spec, out_shape=jax.ShapeDtypeStruct((rows, C), BF16),
                         compiler_params=_params(("arbitrary",)))(core.reshape(1), units.reshape(2, rows, C),
                                                                   got.reshape(rows, C))
    return out.reshape(n4, R, C)


def _sum_chips(w, own, got, pos, *, name, layer=0, into=None):
    _, R, C = own.shape
    tr = _sub_rows(R)
    nr = R // tr

    def body(p_ref, own_ref, got_ref, *rest):
        o_ref = rest[-1]
        o_ref[...] = (own_ref[...].astype(F32) + got_ref[0].astype(F32) + got_ref[1].astype(F32)
                      + got_ref[2].astype(F32))

    if w.row_sharded:
        out_map = lambda i, p: (layer, i, p[1])
    else:
        out_map = lambda i, p: (layer, p[1] * nr + i, 0)
    ins = [pos, own, got]
    in_specs = [pl.BlockSpec((None, tr, C), lambda i, p: (p[0], i, 0)),
                pl.BlockSpec((3, tr, C), lambda i, p: (0, i, 0))]
    alias = {}
    if into is not None:
        ins.append(into)
        in_specs.append(ANY)
        alias = {3: 0}
    grid_spec = pltpu.PrefetchScalarGridSpec(num_scalar_prefetch=1, grid=(nr,), in_specs=in_specs,
                                             out_specs=pl.BlockSpec((None, tr, C), out_map))
    return pl.pallas_call(body, name=name, grid_spec=grid_spec, input_output_aliases=alias,
                          out_shape=jax.ShapeDtypeStruct((w.L, w.ks, w.ns), F32),
                          compiler_params=_params(("arbitrary",)))(*ins)


def _mesh_pos():
    return lax.axis_index("x"), lax.axis_index("y"), lax.axis_index("c")


def _other_chips(x, y):
    return [(1 - x, y), (x, 1 - y), (1 - x, 1 - y)]


ANY = pl.BlockSpec(memory_space=pl.ANY)


class _W:
    def __init__(self, name, shard, row_sharded, direct=False):
        self.name = name
        self.direct = direct
        self.L, ks, ns = shard.shape
        self.row_sharded = row_sharded
        self.K, self.N = (ks * N_CHIPS, ns) if row_sharded else (ks, ns * N_CHIPS)
        self.ks, self.ns = ks, ns

    def shard_of(self, full, j):
        if self.row_sharded:
            return full.at[:, pl.ds(j * self.ks, self.ks), :]
        return full.at[:, :, pl.ds(j * self.ns, self.ns)]

    def half_of(self, shard, c):
        if self.row_sharded:
            return shard.at[:, :, pl.ds(c * (self.ns // 2), self.ns // 2)]
        return shard.at[:, pl.ds(c * (self.ks // 2), self.ks // 2), :]


HBM = pl.BlockSpec(memory_space=pltpu.HBM)
SEM = pl.BlockSpec(memory_space=pltpu.SEMAPHORE)
IN_FLIGHT = pltpu.SideEffectType.DATAFLOW_SIDE_EFFECTING


def _in_hbm(a):
    return pltpu.with_memory_space_constraint(a, pltpu.HBM)


def _gather_start(ws, shards, after, *, name):
    nw = len(ws)

    def body(*refs):
        src, dst = refs[:nw], refs[nw:2 * nw]
        send, recv = refs[2 * nw + 1:3 * nw + 1], refs[3 * nw + 1:4 * nw + 1]
        x, y, c = _mesh_pos()
        me = 2 * x + y
        for i, w in enumerate(ws):
            for f, (px, py) in enumerate(_other_chips(x, y)):
                for e in range(2 if w.direct else 1):
                    k = 2 * f + e
                    pltpu.make_async_remote_copy(
                        src_ref=w.half_of(src[i], c), dst_ref=w.half_of(w.shard_of(dst[i], me), c),
                        send_sem=send[i].at[k], recv_sem=recv[i].at[k], device_id=(px, py, c if e == 0 else 1 - c),
                        device_id_type=MESH).start()

    fulls = [lax.empty((w.L, w.K, w.N), BF16) for w in ws]
    out = pl.pallas_call(
        body, name=name, in_specs=[HBM] * (2 * nw) + [ANY],
        out_specs=[SEM] * (2 * nw) + [HBM] * (2 * nw),
        out_shape=[pltpu.SemaphoreType.DMA((6,))] * (2 * nw)
        + [pltpu.HBM(s.shape, BF16) for s in shards] + [pltpu.HBM(f.shape, BF16) for f in fulls],
        input_output_aliases={i: 2 * nw + i for i in range(2 * nw)},
        compiler_params=pltpu.CompilerParams(has_side_effects=IN_FLIGHT))(
            *[_in_hbm(s) for s in shards], *[_in_hbm(f) for f in fulls], after)
    return [(out[i], out[nw + i], out[2 * nw + i], out[3 * nw + i]) for i in range(nw)]


def _gather_wait(ws, flight, after, *, name):
    nw = len(ws)

    def body(*refs):
        src, dst = refs[:nw], refs[nw:2 * nw]
        send, recv = refs[2 * nw:3 * nw], refs[3 * nw:4 * nw]
        x, y, c = _mesh_pos()
        for i, w in enumerate(ws):
            for f, (px, py) in enumerate(_other_chips(x, y)):
                for e in range(2 if w.direct else 1):
                    k = 2 * f + e
                    landed = w.half_of(w.shard_of(dst[i], 2 * px + py), c if e == 0 else 1 - c)
                    cp = pltpu.make_async_remote_copy(
                        src_ref=w.half_of(src[i], c), dst_ref=landed, send_sem=send[i].at[k], recv_sem=recv[i].at[k],
                        device_id=(px, py, c), device_id_type=MESH)
                    cp.wait_send()
                    cp.wait_recv()

    shards, fulls = [fl[2] for fl in flight], [fl[3] for fl in flight]
    out = pl.pallas_call(
        body, name=name, in_specs=[HBM] * (2 * nw) + [SEM] * (2 * nw) + [ANY],
        out_specs=[HBM] * (2 * nw),
        out_shape=[pltpu.HBM(s.shape, BF16) for s in shards] + [pltpu.HBM(f.shape, BF16) for f in fulls],
        input_output_aliases={i: i for i in range(2 * nw)},
        compiler_params=pltpu.CompilerParams(has_side_effects=IN_FLIGHT))(
            *shards, *fulls, *[fl[0] for fl in flight], *[fl[1] for fl in flight], after)
    return out[:nw], out[nw:]


def _gather_finish(ws, shards, fulls, *, name):
    nw = len(ws)
    forward = not ws[0].direct

    def body(*refs):
        src, dst, stage = refs[:nw], refs[3 * nw:4 * nw], refs[4 * nw:5 * nw]
        send_sems, recv_sems, load_sems, store_sems = refs[5 * nw:]
        x, y, c = _mesh_pos()
        me = 2 * x + y
        sibling = (x, y, 1 - c)
        chips = _other_chips(x, y)

        def fwd(i, w, f, half):
            px, py = chips[f]
            landed = w.half_of(w.shard_of(dst[i], 2 * px + py), half)
            return pltpu.make_async_remote_copy(src_ref=landed, dst_ref=landed, send_sem=send_sems.at[3 * i + f],
                                                recv_sem=recv_sems.at[3 * i + f], device_id=sibling,
                                                device_id_type=MESH)

        loads = [pltpu.make_async_copy(src[i], stage[i], load_sems.at[i]) for i in range(nw)]
        for cp in loads:
            cp.start()
        sends = [fwd(i, w, f, c) for i, w in enumerate(ws) for f in range(3)] if forward else []
        for cp in sends:
            cp.start()
        stores = [pltpu.make_async_copy(stage[i], w.shard_of(dst[i], me), store_sems.at[i])
                  for i, w in enumerate(ws)]
        for ld, st in zip(loads, stores):
            ld.wait()
            st.start()
        if forward:
            for i, w in enumerate(ws):
                for f in range(3):
                    fwd(i, w, f, 1 - c).wait_recv()
        for cp in sends:
            cp.wait_send()
        for cp in stores:
            cp.wait()

    out = pl.pallas_call(
        body, name=name, in_specs=[ANY] * (2 * nw), out_specs=[ANY] * (2 * nw),
        out_shape=[jax.ShapeDtypeStruct(s.shape, BF16) for s in shards]
        + [jax.ShapeDtypeStruct(f.shape, BF16) for f in fulls],
        input_output_aliases={i: i for i in range(2 * nw)},
        scratch_shapes=[pltpu.VMEM((w.L, w.ks, w.ns), BF16) for w in ws]
        + [pltpu.SemaphoreType.DMA((3 * nw,)), pltpu.SemaphoreType.DMA((3 * nw,)), pltpu.SemaphoreType.DMA((nw,)),
           pltpu.SemaphoreType.DMA((nw,))],
        compiler_params=_params(has_side_effects=True))(*shards, *fulls)
    return out[nw:]


def _split_copies(name, srcs, lands, n_sems, copies_of, *, flight=None, after=None):
    n = len(srcs)
    starting = flight is None

    def body(*refs):
        src, land = refs[:n], refs[n:2 * n]
        sems = refs[2 * n + 1:4 * n + 1] if starting else refs[2 * n:4 * n]
        for i in range(n):
            for cp in copies_of(i, src[i], land[i], sems[i], sems[n + i]):
                if starting:
                    cp.start()
                else:
                    cp.wait_send()
                    cp.wait_recv()

    thru = [pltpu.HBM(a.shape, a.dtype) for a in list(srcs) + list(lands)]
    if starting:
        out = pl.pallas_call(
            body, name=name, in_specs=[HBM] * (2 * n) + [ANY], out_specs=[SEM] * (2 * n) + [HBM] * (2 * n),
            out_shape=[pltpu.SemaphoreType.DMA((n_sems,))] * (2 * n) + thru,
            input_output_aliases={i: 2 * n + i for i in range(2 * n)},
            compiler_params=pltpu.CompilerParams(has_side_effects=IN_FLIGHT))(
                *[_in_hbm(a) for a in srcs], *[_in_hbm(a) for a in lands], after)
        return [(out[i], out[n + i], out[2 * n + i], out[3 * n + i]) for i in range(n)]
    out = pl.pallas_call(
        body, name=name, in_specs=[HBM] * (2 * n) + [SEM] * (2 * n) + [ANY], out_specs=[HBM] * (2 * n),
        out_shape=thru, input_output_aliases={i: i for i in range(2 * n)},
        compiler_params=pltpu.CompilerParams(has_side_effects=IN_FLIGHT))(
            *srcs, *lands, *[fl[0] for fl in flight], *[fl[1] for fl in flight], after)
    return out[:n], out[n:]


def _sum8(land, vec, me):
    R = vec.shape[0]

    def body(me_ref, land_ref, vec_ref, o_ref):
        acc = jnp.zeros((R, 128), F32)
        for d in range(8):
            acc = acc + jnp.where(me_ref[0] == d, vec_ref[...], land_ref[d])
        o_ref[...] = acc

    grid_spec = pltpu.PrefetchScalarGridSpec(
        num_scalar_prefetch=1, grid=(1,),
        in_specs=[pl.BlockSpec((8, R, 128), lambda i, m: (0, 0, 0)), pl.BlockSpec((R, 128), lambda i, m: (0, 0))],
        out_specs=pl.BlockSpec((R, 128), lambda i, m: (0, 0)))
    return pl.pallas_call(body, name="sum8", grid_spec=grid_spec, out_shape=jax.ShapeDtypeStruct((R, 128), F32),
                          compiler_params=_params(("arbitrary",)))(me.reshape(1), land, vec)


def _swap_copies(i, src, got, send, recv):
    x, y, c = _mesh_pos()
    return [pltpu.make_async_remote_copy(src_ref=src.at[1 - c], dst_ref=got, send_sem=send.at[0], recv_sem=recv.at[0],
                                         device_id=(x, y, 1 - c), device_id_type=MESH)]


def _gather8_copies(i, src, land, send, recv):
    x, y, c = _mesh_pos()
    me = 4 * x + 2 * y + c
    peers = [(x, y, 1 - c)] + [(px, py, pc) for px, py in _other_chips(x, y) for pc in (c, 1 - c)]
    return [pltpu.make_async_remote_copy(src_ref=src, dst_ref=land.at[me], send_sem=send.at[k], recv_sem=recv.at[k],
                                         device_id=peer, device_id_type=MESH) for k, peer in enumerate(peers)]


def _scatter_copy(src, got, send, recv, f, chip, c):
    px, py = chip
    return pltpu.make_async_remote_copy(src_ref=src.at[2 * px + py], dst_ref=got.at[f], send_sem=send.at[f],
                                        recv_sem=recv.at[f], device_id=(px, py, c), device_id_type=MESH)


def _scatter_start(sums, *, name):
    nw = len(sums)

    def body(*refs):
        src, got = refs[:nw], refs[nw:2 * nw]
        send, recv = refs[2 * nw:3 * nw], refs[3 * nw:4 * nw]
        x, y, c = _mesh_pos()
        for i in range(nw):
            for f, chip in enumerate(_other_chips(x, y)):
                _scatter_copy(src[i], got[i], send[i], recv[i], f, chip, c).start()

    lands = [lax.empty((3,) + s.shape[1:], BF16) for s in sums]
    out = pl.pallas_call(
        body, name=name, in_specs=[HBM] * (2 * nw), out_specs=[SEM] * (2 * nw) + [HBM] * (2 * nw),
        out_shape=[pltpu.SemaphoreType.DMA((3,))] * (2 * nw)
        + [pltpu.HBM(s.shape, BF16) for s in sums] + [pltpu.HBM(l.shape, BF16) for l in lands],
        input_output_aliases={i: 2 * nw + i for i in range(2 * nw)},
        compiler_params=pltpu.CompilerParams(has_side_effects=IN_FLIGHT))(
            *[_in_hbm(s) for s in sums], *[_in_hbm(l) for l in lands])
    return [(out[i], out[nw + i], out[2 * nw + i], out[3 * nw + i]) for i in range(nw)]


def _scatter_wait(flight, after):
    nw = len(flight)

    def body(*refs):
        src, got = refs[:nw], refs[nw:2 * nw]
        send, recv = refs[2 * nw:3 * nw], refs[3 * nw:4 * nw]
        x, y, c = _mesh_pos()
        for i in range(nw):
            for f, chip in enumerate(_other_chips(x, y)):
                cp = _scatter_copy(src[i], got[i], send[i], recv[i], f, chip, c)
                cp.wait_send()
                cp.wait_recv()

    sums, lands = [fl[2] for fl in flight], [fl[3] for fl in flight]
    out = pl.pallas_call(
        body, name="scatter_wait", in_specs=[HBM] * (2 * nw) + [SEM] * (2 * nw) + [ANY], out_specs=[HBM] * (2 * nw),
        out_shape=[pltpu.HBM(s.shape, BF16) for s in sums] + [pltpu.HBM(l.shape, BF16) for l in lands],
        input_output_aliases={i: i for i in range(2 * nw)},
        compiler_params=pltpu.CompilerParams(has_side_effects=IN_FLIGHT))(
            *sums, *lands, *[fl[0] for fl in flight], *[fl[1] for fl in flight], after)
    return out[:nw], out[nw:]


def _join_halves(ws, shards):
    nw = len(ws)

    def body(*refs):
        buf = refs[nw:2 * nw]
        send_sems, recv_sems = refs[2 * nw:]
        x, y, c = _mesh_pos()
        sibling = (x, y, 1 - c)

        def copy(i, w, half):
            region = w.half_of(buf[i], half)
            return pltpu.make_async_remote_copy(src_ref=region, dst_ref=region, send_sem=send_sems.at[i],
                                                recv_sem=recv_sems.at[i], device_id=sibling, device_id_type=MESH)

        sends = [copy(i, w, c) for i, w in enumerate(ws)]
        for cp in sends:
            cp.start()
        for i, w in enumerate(ws):
            copy(i, w, 1 - c).wait_recv()
        for cp in sends:
            cp.wait_send()

    return pl.pallas_call(
        body, name="join_halves", in_specs=[ANY] * nw, out_specs=[ANY] * nw,
        out_shape=[jax.ShapeDtypeStruct((w.L, w.ks, w.ns), F32) for w in ws],
        input_output_aliases={i: i for i in range(nw)},
        scratch_shapes=[pltpu.SemaphoreType.DMA((nw,)), pltpu.SemaphoreType.DMA((nw,))],
        compiler_params=_params(has_side_effects=True))(*shards)


def _allreduce_small(vec):
    R = vec.shape[0]

    def body(x_ref, o_ref, buf, send_sems, recv_sems):
        x, y, c = _mesh_pos()
        me, sibling = (x, y, c), (x, y, 1 - c)
        chips = _other_chips(x, y)

        def slot(px, py, pc):
            return buf.at[4 * px + 2 * py + pc]

        def copy(k, block, to, src=None):
            return pltpu.make_async_remote_copy(src_ref=slot(*block) if src is None else src, dst_ref=slot(*block),
                                                send_sem=send_sems.at[k], recv_sem=recv_sems.at[k], device_id=to,
                                                device_id_type=MESH)

        first = [copy(0, me, sibling, src=x_ref)] + [copy(1 + f, me, (*chip, c), src=x_ref)
                                                     for f, chip in enumerate(chips)]
        for cp in first:
            cp.start()
        passed = [copy(4 + f, (*chip, c), sibling) for f, chip in enumerate(chips)]
        for f, chip in enumerate(chips):
            copy(1 + f, (*chip, c), me).wait_recv()
            passed[f].start()
        copy(0, sibling, me).wait_recv()
        for f, chip in enumerate(chips):
            copy(4 + f, (*chip, 1 - c), me).wait_recv()
        for cp in first + passed:
            cp.wait_send()
        slot(*me)[...] = x_ref[...]
        acc = buf[0]
        for d in range(1, 8):
            acc = acc + buf[d]
        o_ref[...] = acc

    return pl.pallas_call(
        body, name="allreduce_small", in_specs=[pl.BlockSpec(memory_space=pltpu.VMEM)],
        out_specs=pl.BlockSpec(memory_space=pltpu.VMEM), out_shape=jax.ShapeDtypeStruct((R, 128), F32),
        scratch_shapes=[pltpu.VMEM((8, R, 128), F32), pltpu.SemaphoreType.DMA((7,)), pltpu.SemaphoreType.DMA((7,))],
        compiler_params=_params())(vec)


def _pack(parts):
    flat = jnp.concatenate([p.reshape(-1).astype(F32) for p in parts])
    n = flat.shape[0]
    pad = (-n) % (64 * 128)
    return jnp.pad(flat, (0, pad)).reshape(-1, 128)


def _unpack(vec, shapes):
    flat = vec.reshape(-1)
    out, off = [], 0
    for s in shapes:
        n = int(np.prod(s))
        out.append(flat[off:off + n].reshape(s))
        off += n
    return out


def kernel(x, a_norm_g, a_w_in, a_v_norm_g, a_w_s, a_b_s, a_w_out, kv_norm_g, w_kv, b_norm_g, b_w_q, b_rel_bias, b_w_o, f_norm_g, f_w_in, f_conv_w, f_conv_b, f_w_down, final_norm_g, loss_target, m_a_norm_g, m_a_w_in, m_a_v_norm_g, m_a_w_s, m_a_b_s, m_a_w_out, m_kv_norm_g, m_w_kv, m_b_norm_g, m_b_w_q, m_b_rel_bias, m_b_w_o, m_f_norm_g, m_f_w_in, m_f_conv_w, m_f_conv_b, m_f_w_down, m_final_norm_g, v_a_norm_g, v_a_w_in, v_a_v_norm_g, v_a_w_s, v_a_b_s, v_a_w_out, v_kv_norm_g, v_w_kv, v_b_norm_g, v_b_w_q, v_b_rel_bias, v_b_w_o, v_f_norm_g, v_f_w_in, v_f_conv_w, v_f_conv_b, v_f_w_down, v_final_norm_g):
    B, S, D = x.shape
    T = B * S
    xi, yi, ci = lax.axis_index("x"), lax.axis_index("y"), lax.axis_index("c")
    j_me = (2 * xi + yi).astype(jnp.int32)
    core = ci.astype(jnp.int32)
    pos = jnp.stack([j_me, core])

    w_shards = {"a_w_in": (a_w_in, False), "a_w_out": (a_w_out, True), "w_kv": (w_kv[None], False),
                "b_w_q": (b_w_q, True), "b_w_o": (b_w_o, True), "f_w_in": (f_w_in, False), "f_w_down": (f_w_down, True)}
    names = list(w_shards)
    ws = [_W(n, w_shards[n][0], w_shards[n][1]) for n in names]
    g_shards = {"a_w_in": (a_w_in, False), "a_w_out": (a_w_out, True),
                "f_w_in0": (f_w_in[0:1], False), "f_w_down0": (f_w_down[0:1], True),
                "w_kv": (w_kv[None], False), "b_w_q": (b_w_q, True), "b_w_o": (b_w_o, True),
                "f_w_in1": (f_w_in[1:2], False), "f_w_down1": (f_w_down[1:2], True)}
    g_names = list(g_shards)
    g_ws = {n: _W(n, *g_shards[n], direct=n in ("w_kv", "b_w_q", "b_w_o", "f_w_in1", "f_w_down1")) for n in g_names}

    Wd = a_w_in.shape[1]
    GW = a_v_norm_g.shape[1] * N_CHIPS
    F2 = f_conv_w.shape[2] * N_CHIPS
    Fh = F2 // 2
    nsd, nsg, nsf = a_norm_g.shape[1], a_v_norm_g.shape[1], f_conv_w.shape[2]
    own = (ci == 0).astype(F32)
    place = lambda sh, width, n: lax.dynamic_update_slice_in_dim(
        jnp.zeros(sh.shape[:-1] + (width,), F32), sh * own, j_me * n, axis=sh.ndim - 1)
    def tied(x, flight):
        x, thru = lax.optimization_barrier((x, flight[0][2]))
        return x, [flight[0][:2] + (thru,) + flight[0][3:]] + flight[1:]

    gathered = _allreduce_small(_pack([place(a_norm_g, Wd, nsd), place(a_v_norm_g, GW, nsg),
                                       place(f_conv_w, F2, nsf)]))
    a_g, a_vg, conv_w = _unpack(gathered, [(1, Wd), (1, GW), (2, 3, F2)])
    flight = dict(zip(g_names, _gather_start([g_ws[n] for n in g_names],
                                             [g_shards[n][0].astype(BF16) for n in g_names], gathered,
                                             name="gather_start")))
    full = {}

    def arrive(group, after, tag):
        gw = [g_ws[n] for n in group]
        sh, fu = _gather_wait(gw, [flight[n] for n in group], after, name=f"gather_wait_{tag}")
        full.update(zip(group, _gather_finish(gw, sh, fu, name=f"gather_finish_{tag}")))
    conv_w2 = conv_w.reshape(2, 3, 2, Fh).transpose(0, 2, 1, 3)
    conv_b2 = f_conv_b.reshape(2, 2, Fh)

    h0 = x.reshape(T, D)
    target = loss_target.reshape(T, D)
    bs_tile = jnp.repeat(a_b_s[0].T, GROUP_DIM, axis=1)
    ws_a = a_w_s[0]
    scale = HEAD_DIM ** -0.5
    HD = b_w_q.shape[2]
    H = HD // HEAD_DIM
    n_rel = b_rel_bias.shape[-1]
    frow, (flight["a_w_in"],) = tied(b_rel_bias[0][:, _bias_index()].reshape(H, 1, F_LEN), [flight["a_w_in"]])
    bias = _bias_expand(frow)

    def ffn_fwd(h, l, loss=None):
        out = _ffn_fwd(h, full[f"f_w_in{l}"], f_norm_g[l], conv_w2[l], conv_b2[l], full[f"f_w_down{l}"], S,
                       loss=loss, name=f"ffn{l}")
        yff, a, c, n = out[1:5]
        return (out[0] if loss is None else (out[0], out[5], out[6])), (a, c, n, yff)

    arrive(["a_w_in", "a_w_out"], bias, "a")
    h1, zp, out_a, n_a = _mixer_a_fwd(h0, full["a_w_in"], a_g[0], a_vg, ws_a, bs_tile, full["a_w_out"])
    arrive(["f_w_in0", "f_w_down0"], h1, "f0")
    h2, saved0 = ffn_fwd(h1, 0)
    arrive(["w_kv", "b_w_q", "b_w_o"], h2, "b")
    arrive(["f_w_in1", "f_w_down1"], h2, "f1")
    q, kv, n_q, n_kv = _qkv_fwd(h2, full["b_w_q"], b_norm_g[0], full["w_kv"], kv_norm_g, scale)
    kv4, q3 = kv.reshape(2, B, S, HD), q.reshape(B, S, HD)
    o = _attn_fwd(q3, kv4, bias, B, S).reshape(T, HD)
    h3 = _mm(o, full["b_w_o"], res=h2, name="attn_out")
    (dh, loss8, dg_final), saved1 = ffn_fwd(h3, 1, loss=(final_norm_g, target))

    units = {}

    in_flight = {}

    def swap_start(group, tag, carry):
        us = [units[n] for n in group]
        lands = [lax.empty(u.shape[1:], BF16) for u in us]
        carry, flight = tied(carry, _split_copies(f"swap_start_{tag}", us, lands, 1, _swap_copies, after=carry))
        return (group, tag, flight), carry

    def reduce_start(swap, after):
        group, tag, flight = swap
        us, got = _split_copies(f"swap_wait_{tag}", [fl[2] for fl in flight], [fl[3] for fl in flight], 1,
                                _swap_copies, flight=flight, after=after)
        sums = [_add_pair(u, g_, core, name=f"pair_{n}") for n, u, g_ in zip(group, us, got)]
        after, flight = tied(after, _scatter_start(sums, name=f"scatter_start_{tag}"))
        in_flight.update(zip(group, flight))
        return after

    def ffn_bwd(dh, h, saved, l, early):
        a, c, n, yff = saved
        units[f"f_w_down{l}"] = _mm_tn(yff, dh, rows_are_shards=True, name=f"ffn{l}_down_dw")
        dh_in = dh
        if early:
            sw, dh_in = swap_start([f"f_w_down{l}"], f"fd{l}", dh)
        dyff = _mm(dh_in, full[f"f_w_down{l}"], trans_w=True, out_dtype=BF16, name=f"ffn{l}_down_dx")
        if early:
            dyff = reduce_start(sw, dyff)
        da, dcw, dcb = _conv_bwd(a, c, dyff, conv_w2[l], S)
        units[f"f_w_in{l}"] = _mm_tn(n, da, split_y=True, name=f"ffn{l}_in_dw")
        sw, da = swap_start([f"f_w_in{l}"] if early else [f"f_w_down{l}", f"f_w_in{l}"], f"f{l}", da)
        dh, dg = _mm(da, full[f"f_w_in{l}"], trans_w=True, split_x=True, bwd=(h, f_norm_g[l], dh),
                     name=f"ffn{l}_in_dx")
        return reduce_start(sw, dh), dg, dcw, dcb

    dh, dg_f1, dcw1, dcb1 = ffn_bwd(dh, h3, saved1, 1, False)
    do = _mm(dh, full["b_w_o"], trans_w=True, out_dtype=BF16, name="attn_out_dx")
    units["b_w_o"] = _mm_tn(o, dh, rows_are_shards=True, name="b_w_o_dw")
    dq, dkv, dbias = _attn_bwd(q3, kv4, bias, do.reshape(B, S, HD), B, S)
    dq, d_rel = lax.optimization_barrier((dq, _bias_reduce(dbias, n_rel)))
    d_rel = d_rel.reshape(1, H, n_rel)
    dq, dkv = dq.reshape(T, HD), dkv.reshape(2, T, HD)
    units["b_w_q"] = _mm_tn(n_q, dq, rows_are_shards=True, name="b_w_q_dw")
    units["w_kv"] = _mm_tn(n_kv, dkv, split_y=True, name="w_kv_dw")
    sw, dkv = swap_start(["b_w_o", "b_w_q", "w_kv"], "b", dkv)
    dh, dg_b, dg_kv = _qkv_dx(dq, full["b_w_q"], b_norm_g[0], dkv, full["w_kv"], kv_norm_g, h2, dh)
    dh = reduce_start(sw, dh)
    dh, dg_f0, dcw0, dcb0 = ffn_bwd(dh, h1, saved0, 0, True)
    units["a_w_out"] = _mm_tn(out_a, dh, rows_are_shards=True, name="a_w_out_dw")
    sw, dh_in = swap_start(["a_w_out"], "ao", dh)
    d_out = _mm(dh_in, full["a_w_out"], trans_w=True, out_dtype=BF16, name="a_out_dx")
    d_out = reduce_start(sw, d_out)
    dzp, dws, dbs, dgv = _gate_bwd(zp, d_out, a_vg, ws_a, bs_tile)
    units["a_w_in"] = _mm_tn(n_a, dzp, name="a_w_in_dw")
    sw, dzp_in = swap_start(["a_w_in"], "ai", dzp)
    grad_x, dg_a = _mm(dzp_in, full["a_w_in"], trans_w=True, bwd=(h0, a_g[0], dh), name="a_in_dx")
    grad_x = reduce_start(sw, grad_x)

    to_flat = lambda d: d.transpose(1, 0, 2).reshape(3, F2)
    small = {"a_norm_g": dg_a, "a_v_norm_g": dgv, "a_w_s": dws[None], "a_b_s": dbs[None], "kv_norm_g": dg_kv[0],
             "b_norm_g": dg_b, "b_rel_bias": d_rel, "f_norm_g": jnp.concatenate([dg_f0, dg_f1], axis=0),
             "f_conv_w": jnp.stack([to_flat(dcw0), to_flat(dcw1)]),
             "f_conv_b": jnp.stack([dcb0.reshape(F2), dcb1.reshape(F2)]), "final_norm_g": dg_final[0]}
    snames = list(small)
    small_vec = _pack([small[n] for n in snames] + [loss8[0:1, 0:1]])
    grad_x, small_flight = tied(grad_x, _split_copies("small_start", [small_vec],
                                                      [lax.empty((8,) + small_vec.shape, F32)], 7, _gather8_copies,
                                                      after=grad_x))

    sums, recv = _scatter_wait([in_flight[n] for n in g_names], grad_x)
    sums, recv = dict(zip(g_names, sums)), dict(zip(g_names, recv))
    halves = []
    for n, w in zip(names, ws):
        if w.L == 1:
            halves.append(_sum_chips(w, sums[n], recv[n], pos, name=f"chips_{n}"))
        else:
            first = _sum_chips(w, sums[n + "0"], recv[n + "0"], pos, name=f"chips_{n}0")
            halves.append(_sum_chips(w, sums[n + "1"], recv[n + "1"], pos, layer=1, into=first, name=f"chips_{n}1"))
    g_big = dict(zip(names, _join_halves(ws, halves)))
    g_big["w_kv"] = g_big["w_kv"][0]

    given = dict(a_norm_g=(a_norm_g, m_a_norm_g, v_a_norm_g), a_w_in=(a_w_in, m_a_w_in, v_a_w_in),
                 a_v_norm_g=(a_v_norm_g, m_a_v_norm_g, v_a_v_norm_g), a_w_s=(a_w_s, m_a_w_s, v_a_w_s),
                 a_b_s=(a_b_s, m_a_b_s, v_a_b_s), a_w_out=(a_w_out, m_a_w_out, v_a_w_out),
                 kv_norm_g=(kv_norm_g, m_kv_norm_g, v_kv_norm_g), w_kv=(w_kv, m_w_kv, v_w_kv),
                 b_norm_g=(b_norm_g, m_b_norm_g, v_b_norm_g), b_w_q=(b_w_q, m_b_w_q, v_b_w_q),
                 b_rel_bias=(b_rel_bias, m_b_rel_bias, v_b_rel_bias), b_w_o=(b_w_o, m_b_w_o, v_b_w_o),
                 f_norm_g=(f_norm_g, m_f_norm_g, v_f_norm_g), f_w_in=(f_w_in, m_f_w_in, v_f_w_in),
                 f_conv_w=(f_conv_w, m_f_conv_w, v_f_conv_w), f_conv_b=(f_conv_b, m_f_conv_b, v_f_conv_b),
                 f_w_down=(f_w_down, m_f_w_down, v_f_w_down), final_norm_g=(final_norm_g, m_final_norm_g, v_final_norm_g))
    order = list(given)
    grads, deltas, new_m, new_v = {}, {}, {}, {}
    for n in names:
        w_, m_, v_ = given[n]
        g_ = g_big[n]
        C = w_.shape[-1]
        d2, m2, v2 = _adamw(w_.reshape(-1, C), g_.reshape(-1, C), m_.reshape(-1, C), v_.reshape(-1, C),
                            name=f"adamw_{n}")
        grads[n], deltas[n], new_m[n], new_v[n] = g_.reshape(w_.shape), d2.reshape(w_.shape), m2.reshape(w_.shape), \
            v2.reshape(w_.shape)
    vecs, lands = _split_copies("small_wait", [small_flight[0][2]], [small_flight[0][3]], 7, _gather8_copies,
                                flight=small_flight, after=deltas[names[-1]])
    red = _sum8(lands[0], vecs[0], (4 * xi + 2 * yi + ci).astype(jnp.int32))
    parts = _unpack(red, [small[n].shape for n in snames] + [(1,)])
    g_small = dict(zip(snames, parts[:-1]))
    loss = parts[-1][0]
    g_small["a_norm_g"] = lax.dynamic_slice_in_dim(g_small["a_norm_g"], j_me * nsd, nsd, axis=1)
    g_small["a_v_norm_g"] = lax.dynamic_slice_in_dim(g_small["a_v_norm_g"], j_me * nsg, nsg, axis=1)
    g_small["f_conv_w"] = lax.dynamic_slice_in_dim(g_small["f_conv_w"], j_me * nsf, nsf, axis=2)

    sm = [n for n in order if n not in names]
    d2, m2, v2 = _adamw(_pack([given[n][0] for n in sm]), _pack([g_small[n].reshape(given[n][0].shape) for n in sm]),
                        _pack([given[n][1] for n in sm]), _pack([given[n][2] for n in sm]), name="adamw_small")
    shapes = [given[n][0].shape for n in sm]
    for n, d_, m_, v_ in zip(sm, _unpack(d2, shapes), _unpack(m2, shapes), _unpack(v2, shapes)):
        grads[n], deltas[n], new_m[n], new_v[n] = g_small[n].reshape(given[n][0].shape), d_, m_, v_

    return (loss, grad_x.reshape(B, S, D), *[grads[n] for n in order], *[deltas[n] for n in order],
            *[new_m[n] for n in order], *[new_v[n] for n in order])
```

```python
import functools
import math

import numpy as np
import jax
import jax.numpy as jnp
from jax import lax
from jax.experimental import pallas as pl
from jax.experimental.pallas import tpu as pltpu

F32 = jnp.float32
BF16 = jnp.bfloat16
MESH = pl.DeviceIdType.MESH

EPS = 1e-6
NEG_INF = -1e30
CHUNK = 64
GMLP_BLOCK = 128
GROUP_DIM = 128
HEAD_DIM = 64
LEFT_CHUNKS = 8
PAD = LEFT_CHUNKS * CHUNK
REL_CLIP = 128
Q_BLOCK = 256
K_SPAN = PAD + Q_BLOCK
F_LEN = K_SPAN + Q_BLOCK
HEADS_PER_STEP = 4
N_CHIPS = 4

ADAM_LR = 0.001
ADAM_B1 = 0.9
ADAM_B2 = 0.999
ADAM_EPS = 1e-08
ADAM_WD = 0.01
ADAM_STEP = 10

VMEM_LIMIT = 56 * 1024 * 1024


def _params(sem=None, **kw):
    if sem is not None:
        kw["dimension_semantics"] = sem
    return pltpu.CompilerParams(vmem_limit_bytes=VMEM_LIMIT, **kw)


def _rms(xf):
    r = lax.rsqrt(jnp.mean(xf * xf, axis=-1, keepdims=True) + EPS)
    return xf * r, r


def _gelu(x, with_grad=False):
    c = math.sqrt(2.0 / math.pi)
    x2 = x * x
    t = jnp.tanh(c * x * (1.0 + 0.044715 * x2))
    half = 0.5 * (1.0 + t)
    if not with_grad:
        return x * half
    return x * half, half + 0.5 * x * (1.0 - t * t) * c * (1.0 + 3.0 * 0.044715 * x2)


def _col_tile(n):
    if n <= 1024:
        return n
    for t in (1408, 1024, 512):
        if n % t == 0:
            return t
    raise ValueError(n)


def _row_tile(t, want):
    while t % want:
        want //= 2
    return want


def _loss_epilogue(h, g_ref, t_ref, dh_ref, loss_ref, dg_ref, first):
    @pl.when(first)
    def _():
        loss_ref[...] = jnp.zeros_like(loss_ref)
        dg_ref[...] = jnp.zeros_like(dg_ref)

    n, r = _rms(h)
    g = g_ref[...]
    e = n * g - t_ref[...]
    loss_ref[...] += 0.5 * jnp.sum(jnp.mean(e * e, axis=-1, keepdims=True), axis=0, keepdims=True)
    dy = e * (1.0 / h.shape[-1])
    dg_ref[...] += jnp.sum(dy * n, axis=0, keepdims=True)
    t = dy * g
    dh_ref[...] = r * (t - n * jnp.mean(t * n, axis=-1, keepdims=True))


def _mm(x, w, *, name, trans_w=False, res=None, out_dtype=F32, bwd=None, split_x=False, tm=512):
    T = x.shape[-2]
    K = 2 * x.shape[-1] if split_x else x.shape[-1]
    N = w.shape[-2] if trans_w else w.shape[-1]
    tm = _row_tile(T, 2 * tm if max(K, N) <= 2048 else tm)
    has_res, has_bwd = res is not None, bwd is not None
    dims = (((1,), (1,)), ((), ())) if trans_w else (((1,), (0,)), ((), ()))

    def body(*refs):
        it = iter(refs)
        x_ref, w_ref = next(it), next(it)
        res_ref = next(it) if has_res else None
        if has_bwd:
            h_ref, bg_ref, dh_ref = next(it), next(it), next(it)
        o_ref = next(it)
        if split_x:
            kh = K // 2
            acc = lax.dot_general(x_ref[0].astype(BF16), w_ref[:, :kh] if trans_w else w_ref[:kh, :], dims,
                                  preferred_element_type=F32)
            acc = acc + lax.dot_general(x_ref[1].astype(BF16), w_ref[:, kh:] if trans_w else w_ref[kh:, :], dims,
                                        preferred_element_type=F32)
        else:
            acc = lax.dot_general(x_ref[...].astype(BF16), w_ref[...], dims, preferred_element_type=F32)
        if has_res:
            acc = acc + res_ref[...]
        if has_bwd:
            dg_ref = next(it)
            n, r = _rms(h_ref[...])

            @pl.when(pl.program_id(0) == 0)
            def _():
                dg_ref[...] = jnp.zeros_like(dg_ref)

            dg_ref[...] += jnp.sum(acc * n, axis=0, keepdims=True)
            t = acc * bg_ref[...]
            o_ref[...] = dh_ref[...] + r * (t - n * jnp.mean(t * n, axis=-1, keepdims=True))
        else:
            o_ref[...] = acc.astype(out_dtype)

    row = lambda width: pl.BlockSpec((tm, width), lambda m: (m, 0))
    ins = [x, w]
    in_specs = [pl.BlockSpec((2, tm, K // 2), lambda m: (0, m, 0)) if split_x else row(K),
                pl.BlockSpec((None,) + w.shape[1:], lambda m: (0, 0, 0), pipeline_mode=pl.Buffered(1))]
    if has_res:
        ins.append(res)
        in_specs.append(row(N))
    out_shape = [jax.ShapeDtypeStruct((T, N), F32 if has_bwd else out_dtype)]
    out_specs = [row(N)]
    if has_bwd:
        h, g, dh = bwd
        ins += [h, g.reshape(1, N), dh]
        in_specs += [row(N), pl.BlockSpec((1, N), lambda m: (0, 0)), row(N)]
        out_shape.append(jax.ShapeDtypeStruct((1, N), F32))
        out_specs.append(pl.BlockSpec((1, N), lambda m: (0, 0)))
    out = pl.pallas_call(body, name=name, grid=(T // tm,), in_specs=in_specs, out_specs=out_specs,
                         out_shape=out_shape, compiler_params=_params(("arbitrary",)))(*ins)
    return out if has_bwd else out[0]


def _mm_tn(x, dy, *, name, rows_are_shards=False, split_y=False, tt=1024):
    T, K = x.shape
    N = 2 * dy.shape[-1] if split_y else dy.shape[-1]
    R, C = (K // N_CHIPS, N // 2) if rows_are_shards else (K // 2, N // N_CHIPS)
    nn = 2 if split_y else 1
    tn = N // nn
    per = N_CHIPS // nn
    assert not (rows_are_shards and split_y)
    tt = _row_tile(T, tt)
    nt = T // tt

    def body(x_ref, y_ref, o_ref, acc_ref):
        t = pl.program_id(1)

        @pl.when(t == 0)
        def _():
            acc_ref[...] = jnp.zeros_like(acc_ref)

        acc_ref[...] += lax.dot_general(x_ref[...], y_ref[...].astype(BF16), (((0,), (0,)), ((), ())),
                                        preferred_element_type=F32)

        @pl.when(t == nt - 1)
        def _():
            if rows_are_shards:
                for h in range(2):
                    o_ref[h] = acc_ref[:, h * C:(h + 1) * C].astype(BF16).reshape(N_CHIPS, R, C)
            else:
                for j in range(per):
                    o_ref[:, j] = acc_ref[:, j * C:(j + 1) * C].astype(BF16).reshape(2, R, C)

    if split_y:
        yspec = pl.BlockSpec((None, tt, tn), lambda n, t: (n, t, 0))
    else:
        yspec = pl.BlockSpec((tt, tn), lambda n, t: (t, 0))
    if rows_are_shards:
        out_spec = pl.BlockSpec((2, N_CHIPS, R, C), lambda n, t: (0, 0, 0, 0))
    else:
        out_spec = pl.BlockSpec((2, per, R, C), lambda n, t: (0, n, 0, 0))
    return pl.pallas_call(body, name=name, grid=(nn, nt),
                          in_specs=[pl.BlockSpec((tt, K), lambda n, t: (t, 0)), yspec], out_specs=out_spec,
                          out_shape=jax.ShapeDtypeStruct((2, N_CHIPS, R, C), BF16),
                          scratch_shapes=[pltpu.VMEM((K, tn), F32)],
                          compiler_params=_params(("arbitrary", "arbitrary")))(x, dy)


def _qkv_fwd(h, wq, gq, wkv, gkv, scale, *, tm=512):
    T, D = h.shape
    HD = wq.shape[-1]
    tm = _row_tile(T, tm)

    def body(h_ref, wq_ref, gq_ref, wkv_ref, gkv_ref, q_ref, kv_ref, nq_ref, nkv_ref):
        n = _rms(h_ref[...])[0]
        nq = (n * gq_ref[...]).astype(BF16)
        nkv = (n * gkv_ref[...]).astype(BF16)
        nq_ref[...] = nq
        nkv_ref[...] = nkv
        q_ref[...] = (jnp.dot(nq, wq_ref[...], preferred_element_type=F32) * scale).astype(BF16)
        kv = jnp.dot(nkv, wkv_ref[...], preferred_element_type=F32)
        kv_ref[0] = kv[:, :HD].astype(BF16)
        kv_ref[1] = kv[:, HD:].astype(BF16)

    row = lambda width: pl.BlockSpec((tm, width), lambda i: (i, 0))
    fixed = lambda *shape: pl.BlockSpec(shape, lambda i: (0,) * len(shape))
    weight = lambda n: pl.BlockSpec((None, D, n), lambda i: (0, 0, 0), pipeline_mode=pl.Buffered(1))
    return pl.pallas_call(
        body, name="qkv", grid=(T // tm,),
        in_specs=[row(D), weight(HD), fixed(1, D), weight(2 * HD), fixed(1, D)],
        out_specs=[row(HD), pl.BlockSpec((2, tm, HD), lambda i: (0, i, 0)), row(D), row(D)],
        out_shape=[jax.ShapeDtypeStruct((T, HD), BF16), jax.ShapeDtypeStruct((2, T, HD), BF16),
                   jax.ShapeDtypeStruct((T, D), BF16), jax.ShapeDtypeStruct((T, D), BF16)],
        compiler_params=_params(("arbitrary",)))(h, wq, gq.reshape(1, D), wkv, gkv.reshape(1, D))


def _qkv_dx(dq, wq, gq, dkv, wkv, gkv, h, dh, *, tm=512):
    T, D = h.shape
    HD = wq.shape[-1]
    tm = _row_tile(T, tm)
    nt = (((1,), (1,)), ((), ()))

    def body(dq_ref, wq_ref, gq_ref, dkv_ref, wkv_ref, gkv_ref, h_ref, dh_ref, o_ref, dgq_ref, dgkv_ref):
        @pl.when(pl.program_id(0) == 0)
        def _():
            dgq_ref[...] = jnp.zeros_like(dgq_ref)
            dgkv_ref[...] = jnp.zeros_like(dgkv_ref)

        n, r = _rms(h_ref[...])
        dnq = lax.dot_general(dq_ref[...], wq_ref[...], nt, preferred_element_type=F32)
        dnkv = (lax.dot_general(dkv_ref[0], wkv_ref[:, :HD], nt, preferred_element_type=F32)
                + lax.dot_general(dkv_ref[1], wkv_ref[:, HD:], nt, preferred_element_type=F32))
        dgq_ref[...] += jnp.sum(dnq * n, axis=0, keepdims=True)
        dgkv_ref[...] += jnp.sum(dnkv * n, axis=0, keepdims=True)
        t = dnq * gq_ref[...] + dnkv * gkv_ref[...]
        o_ref[...] = dh_ref[...] + r * (t - n * jnp.mean(t * n, axis=-1, keepdims=True))

    row = lambda width: pl.BlockSpec((tm, width), lambda i: (i, 0))
    fixed = lambda *shape: pl.BlockSpec(shape, lambda i: (0,) * len(shape))
    weight = lambda n: pl.BlockSpec((None, D, n), lambda i: (0, 0, 0), pipeline_mode=pl.Buffered(1))
    return pl.pallas_call(
        body, name="qkv_dx", grid=(T // tm,),
        in_specs=[row(HD), weight(HD), fixed(1, D), pl.BlockSpec((2, tm, HD), lambda i: (0, i, 0)), weight(2 * HD),
                  fixed(1, D), row(D), row(D)],
        out_specs=[row(D), fixed(1, D), fixed(1, D)],
        out_shape=[jax.ShapeDtypeStruct((T, D), F32), jax.ShapeDtypeStruct((1, D), F32),
                   jax.ShapeDtypeStruct((1, D), F32)],
        compiler_params=_params(("arbitrary",)))(dq, wq, gq.reshape(1, D), dkv, wkv, gkv.reshape(1, D), h, dh)


def _chunk_mask():
    i = lax.broadcasted_iota(jnp.int32, (GMLP_BLOCK, GMLP_BLOCK), 0) // CHUNK
    j = lax.broadcasted_iota(jnp.int32, (GMLP_BLOCK, GMLP_BLOCK), 1) // CHUNK
    return i >= j


def _mixer_a_fwd(h, w_in, g, gv, ws, bs_tile, w_out, *, tm=256):
    T, D = h.shape
    W = w_out.shape[-2]
    G = W // GROUP_DIM
    tm = _row_tile(T, tm)

    def body(h_ref, wi_ref, g_ref, gv_ref, ws_ref, bs_ref, wo_ref, o_ref, zp_ref, ga_ref, n_ref):
        nb = (_rms(h_ref[...])[0] * g_ref[...]).astype(BF16)
        n_ref[...] = nb
        zpb = jnp.dot(nb, wi_ref[...], preferred_element_type=F32).astype(BF16)
        zp_ref[...] = zpb
        z = _gelu(zpb.astype(F32))
        u, v = z[:, :W], z[:, W:]
        vn = _rms(v)[0] * gv_ref[...]
        mask = _chunk_mask()
        for gi in range(G):
            cs = slice(gi * GROUP_DIM, (gi + 1) * GROUP_DIM)
            wg = jnp.where(mask, ws_ref[gi], 0.0).astype(BF16)
            for b in range(tm // GMLP_BLOCK):
                rs = slice(b * GMLP_BLOCK, (b + 1) * GMLP_BLOCK)
                s = jnp.dot(wg, vn[rs, cs].astype(BF16), preferred_element_type=F32) + bs_ref[:, cs]
                ga_ref[rs, cs] = (u[rs, cs] * s).astype(BF16)
        o_ref[...] = h_ref[...] + jnp.dot(ga_ref[...], wo_ref[...], preferred_element_type=F32)

    row = lambda width: pl.BlockSpec((tm, width), lambda i: (i, 0))
    fixed = lambda *shape: pl.BlockSpec(shape, lambda i: (0,) * len(shape))
    weight = lambda k, n: pl.BlockSpec((None, k, n), lambda i: (0, 0, 0), pipeline_mode=pl.Buffered(1))
    return pl.pallas_call(
        body, name="mixer_a", grid=(T // tm,),
        in_specs=[row(D), weight(D, 2 * W), fixed(1, D), fixed(1, W), fixed(G, GMLP_BLOCK, GMLP_BLOCK),
                  fixed(GMLP_BLOCK, W), weight(W, D)],
        out_specs=[row(D), row(2 * W), row(W), row(D)],
        out_shape=[jax.ShapeDtypeStruct((T, D), F32), jax.ShapeDtypeStruct((T, 2 * W), BF16),
                   jax.ShapeDtypeStruct((T, W), BF16), jax.ShapeDtypeStruct((T, D), BF16)],
        compiler_params=_params(("arbitrary",)))(h, w_in, g.reshape(1, D), gv, ws, bs_tile, w_out)


def _gate_bwd(zp, d_out, gv, ws, bs_tile, *, tm=256):
    T, W2 = zp.shape
    W = W2 // 2
    G = W // GROUP_DIM
    tm = _row_tile(T, tm)
    nm = T // tm

    def body(zp_ref, do_ref, gv_ref, ws_ref, bs_ref, dzp_ref, dws_ref, dbs_ref, dgv_ref, du_scr, dvn_scr, dsum_scr):
        i = pl.program_id(0)

        @pl.when(i == 0)
        def _():
            dws_ref[...] = jnp.zeros_like(dws_ref)
            dgv_ref[...] = jnp.zeros_like(dgv_ref)
            dsum_scr[...] = jnp.zeros_like(dsum_scr)

        zp = zp_ref[...].astype(F32)
        z, dz = _gelu(zp, with_grad=True)
        u, v = z[:, :W], z[:, W:]
        n, r = _rms(v)
        gv = gv_ref[...]
        vn = n * gv
        d_out = do_ref[...].astype(F32)
        mask = _chunk_mask()
        for g in range(G):
            cs = slice(g * GROUP_DIM, (g + 1) * GROUP_DIM)
            wg = jnp.where(mask, ws_ref[g], 0.0).astype(BF16)
            dw = jnp.zeros((GMLP_BLOCK, GMLP_BLOCK), F32)
            for b in range(tm // GMLP_BLOCK):
                rs = slice(b * GMLP_BLOCK, (b + 1) * GMLP_BLOCK)
                vb = vn[rs, cs].astype(BF16)
                s = jnp.dot(wg, vb, preferred_element_type=F32) + bs_ref[:, cs]
                du_scr[rs, cs] = d_out[rs, cs] * s
                ds = d_out[rs, cs] * u[rs, cs]
                dsb = ds.astype(BF16)
                dvn_scr[rs, cs] = lax.dot_general(wg, dsb, (((0,), (0,)), ((), ())), preferred_element_type=F32)
                dw = dw + lax.dot_general(dsb, vb, (((1,), (1,)), ((), ())), preferred_element_type=F32)
                dsum_scr[:, cs] += ds
            dws_ref[g] += jnp.where(mask, dw, 0.0)
        dvn = dvn_scr[...]
        dgv_ref[...] += jnp.sum(dvn * n, axis=0, keepdims=True)
        t = dvn * gv
        dv = r * (t - n * jnp.mean(t * n, axis=-1, keepdims=True))
        dzp_ref[:, :W] = (du_scr[...] * dz[:, :W]).astype(BF16)
        dzp_ref[:, W:] = (dv * dz[:, W:]).astype(BF16)

        @pl.when(i == nm - 1)
        def _():
            sel = (lax.broadcasted_iota(jnp.int32, (G, W), 1) // GROUP_DIM
                   == lax.broadcasted_iota(jnp.int32, (G, W), 0)).astype(F32)
            dbs_ref[...] = lax.dot_general(sel, dsum_scr[...], (((1,), (1,)), ((), ())),
                                           precision=lax.Precision.HIGHEST, preferred_element_type=F32)

    return pl.pallas_call(
        body, name="gate_bwd", grid=(nm,),
        in_specs=[pl.BlockSpec((tm, W2), lambda i: (i, 0)), pl.BlockSpec((tm, W), lambda i: (i, 0)),
                  pl.BlockSpec((1, W), lambda i: (0, 0)),
                  pl.BlockSpec((G, GMLP_BLOCK, GMLP_BLOCK), lambda i: (0, 0, 0)),
                  pl.BlockSpec((GMLP_BLOCK, W), lambda i: (0, 0))],
        out_specs=[pl.BlockSpec((tm, W2), lambda i: (i, 0)),
                   pl.BlockSpec((G, GMLP_BLOCK, GMLP_BLOCK), lambda i: (0, 0, 0)),
                   pl.BlockSpec((G, GMLP_BLOCK), lambda i: (0, 0)), pl.BlockSpec((1, W), lambda i: (0, 0))],
        out_shape=[jax.ShapeDtypeStruct((T, W2), BF16), jax.ShapeDtypeStruct((G, GMLP_BLOCK, GMLP_BLOCK), F32),
                   jax.ShapeDtypeStruct((G, GMLP_BLOCK), F32), jax.ShapeDtypeStruct((1, W), F32)],
        scratch_shapes=[pltpu.VMEM((tm, W), F32), pltpu.VMEM((tm, W), F32), pltpu.VMEM((GMLP_BLOCK, W), F32)],
        compiler_params=_params(("arbitrary",)))(zp, d_out, gv, ws, bs_tile)


LANES = 128
HALO = 16


def _taps(ext, w, b):
    return w[2:3] * ext[HALO:] + w[1:2] * pltpu.roll(ext, 1, 0)[HALO:] + w[0:1] * pltpu.roll(ext, 2, 0)[HALO:] + b


def _ffn_fwd(h, w, g, cw, cb, wd, S, *, name, loss=None, tm=256):
    T, D = h.shape
    F = w.shape[-1] // 2
    tc = _col_tile(F)
    tm = _row_tile(S, tm)
    has_loss = loss is not None

    def body(*refs):
        h_ref, w_ref, g_ref, cw_ref, cb_ref, wd_ref = refs[:6]
        o_ref, y_ref, a_ref, c_ref, n_ref = refs[8:13] if has_loss else refs[6:11]
        tail = refs[-1]
        first = (pl.program_id(0) * tm) % S == 0
        nb = (_rms(h_ref[...])[0] * g_ref[...]).astype(BF16)
        n_ref[...] = nb
        for j in range(F // tc):
            cs = slice(j * tc, (j + 1) * tc)
            conv = []
            for s in range(2):
                acc = jnp.dot(nb, w_ref[:, s * F + j * tc:s * F + (j + 1) * tc], preferred_element_type=F32)
                ab = acc.astype(BF16)
                a_ref[s, :, cs] = ab
                af = ab.astype(F32)
                ext = jnp.concatenate([jnp.where(first, 0.0, tail[s, :, cs]), af], axis=0)
                tail[s, :, cs] = af[tm - HALO:, :]
                cv = _taps(ext, cw_ref[s, :, cs], cb_ref[s:s + 1, cs]).astype(BF16)
                c_ref[s, :, cs] = cv
                conv.append(cv.astype(F32))
            up, gate = conv
            y_ref[:, cs] = (gate * jax.nn.sigmoid(gate) * up).astype(BF16)
        out = h_ref[...] + jnp.dot(y_ref[...], wd_ref[...], preferred_element_type=F32)
        if has_loss:
            _loss_epilogue(out, refs[6], refs[7], o_ref, refs[13], refs[14], pl.program_id(0) == 0)
        else:
            o_ref[...] = out

    row = lambda width: pl.BlockSpec((tm, width), lambda i: (i, 0))
    wide = pl.BlockSpec((2, tm, F), lambda i: (0, i, 0))
    fixed = lambda *shape: pl.BlockSpec(shape, lambda i: (0,) * len(shape))
    once = pl.Buffered(1)
    ins = [h, w, g.reshape(1, D), cw, cb, wd]
    in_specs = [row(D), pl.BlockSpec((None, D, 2 * F), lambda i: (0, 0, 0), pipeline_mode=once), fixed(1, D),
                fixed(2, 3, F), fixed(2, F), pl.BlockSpec((None, F, D), lambda i: (0, 0, 0), pipeline_mode=once)]
    out_specs = [row(D), row(F), wide, wide, row(D)]
    out_shape = [jax.ShapeDtypeStruct((T, D), F32), jax.ShapeDtypeStruct((T, F), BF16),
                 jax.ShapeDtypeStruct((2, T, F), BF16), jax.ShapeDtypeStruct((2, T, F), BF16),
                 jax.ShapeDtypeStruct((T, D), BF16)]
    if has_loss:
        ins += [loss[0].reshape(1, D), loss[1]]
        in_specs += [fixed(1, D), row(D)]
        out_specs += [fixed(8, 128), fixed(1, D)]
        out_shape += [jax.ShapeDtypeStruct((8, 128), F32), jax.ShapeDtypeStruct((1, D), F32)]
    return pl.pallas_call(body, name=name, grid=(T // tm,), in_specs=in_specs, out_specs=out_specs,
                          out_shape=out_shape, scratch_shapes=[pltpu.VMEM((2, HALO, F), F32)],
                          compiler_params=_params(("arbitrary",)))(*ins)


def _conv_bwd(a, c, dy, cw, S, *, tm=256):
    _, T, F = a.shape
    tc = _col_tile(F)
    tm = _row_tile(S, tm)
    nm = T // tm
    hb = tm // HALO
    TE = tm + HALO
    nxt = lambda j, i: jnp.minimum((i + 1) * hb, T // HALO - 1)

    def body(a_ref, c_ref, nc_ref, dy_ref, ndy_ref, w_ref, da_ref, dw_ref, db_ref):
        i = pl.program_id(1)
        last = ((i + 1) * tm) % S == 0
        keep_n = jnp.where(last, 0.0, 1.0)

        @pl.when(i == 0)
        def _():
            dw_ref[...] = jnp.zeros_like(dw_ref)
            db_ref[...] = jnp.zeros_like(db_ref)

        for j in range(tc // LANES):
            cs = slice(j * LANES, (j + 1) * LANES)
            dyf = jnp.concatenate([dy_ref[:, cs].astype(F32), ndy_ref[:, cs].astype(F32) * keep_n], axis=0)
            up = jnp.concatenate([c_ref[0, :, cs].astype(F32), nc_ref[0, :, cs].astype(F32)], axis=0)
            gate = jnp.concatenate([c_ref[1, :, cs].astype(F32), nc_ref[1, :, cs].astype(F32)], axis=0)
            sg = jax.nn.sigmoid(gate)
            for s, d in ((0, dyf * (gate * sg)), (1, dyf * up * (sg * (1.0 + gate * (1.0 - sg))))):
                a = a_ref[s, :, cs].astype(F32)
                w = w_ref[s, :, cs]
                u1, u2 = pltpu.roll(d, TE - 1, 0), pltpu.roll(d, TE - 2, 0)
                db_ref[s:s + 1, cs] += jnp.sum(d[:tm], axis=0, keepdims=True)
                dw_ref[s, 2:3, cs] += jnp.sum(d[:tm] * a, axis=0, keepdims=True)
                dw_ref[s, 1:2, cs] += jnp.sum(u1[:tm] * a, axis=0, keepdims=True)
                dw_ref[s, 0:1, cs] += jnp.sum(u2[:tm] * a, axis=0, keepdims=True)
                da_ref[s, :, cs] = (w[2:3] * d + w[1:2] * u1 + w[0:1] * u2)[:tm].astype(BF16)

    cur = pl.BlockSpec((2, tm, tc), lambda j, i: (0, i, j))
    return pl.pallas_call(
        body, name="conv_bwd", grid=(F // tc, nm),
        in_specs=[cur, cur, pl.BlockSpec((2, HALO, tc), lambda j, i: (0, nxt(j, i), j)),
                  pl.BlockSpec((tm, tc), lambda j, i: (i, j)), pl.BlockSpec((HALO, tc), lambda j, i: (nxt(j, i), j)),
                  pl.BlockSpec((2, 3, tc), lambda j, i: (0, 0, j))],
        out_specs=[cur, pl.BlockSpec((2, 3, tc), lambda j, i: (0, 0, j)), pl.BlockSpec((2, tc), lambda j, i: (0, j))],
        out_shape=[jax.ShapeDtypeStruct((2, T, F), BF16), jax.ShapeDtypeStruct((2, 3, F), F32),
                   jax.ShapeDtypeStruct((2, F), F32)],
        compiler_params=_params(("arbitrary", "arbitrary")))(a, c, c, dy, dy, cw)


def _bias_index():
    idx = np.arange(F_LEN)
    d = np.where(idx < K_SPAN, idx, idx - F_LEN)
    return np.clip(PAD - d, -REL_CLIP, REL_CLIP) + REL_CLIP


ROW_GROUP = 16


def _roll_rows(x, sign, unit, steps):
    rows = lax.broadcasted_iota(jnp.int32, x.shape, 0)
    step = 1
    while step < steps:
        shift = unit * step if sign > 0 else F_LEN - unit * step
        x = jnp.where((rows & step) != 0, pltpu.roll(x, shift, 1), x)
        step *= 2
    return x


def _bias_expand(frow):
    H = frow.shape[0]
    groups = Q_BLOCK // ROW_GROUP

    def body(f_ref, o_ref):
        coarse = _roll_rows(jnp.broadcast_to(f_ref[...], (groups, F_LEN)), 1, ROW_GROUP, groups)
        x = jnp.concatenate([jnp.broadcast_to(coarse[a:a + 1], (ROW_GROUP, F_LEN)) for a in range(groups)], axis=0)
        x = _roll_rows(x, 1, 1, ROW_GROUP)[:, :K_SPAN]
        qc = lax.broadcasted_iota(jnp.int32, (Q_BLOCK, K_SPAN), 0) // CHUNK * CHUNK
        kj = lax.broadcasted_iota(jnp.int32, (Q_BLOCK, K_SPAN), 1)
        o_ref[...] = jnp.where((kj >= qc) & (kj < qc + PAD + CHUNK), x, NEG_INF)

    return pl.pallas_call(
        body, name="bias_expand", grid=(H,),
        in_specs=[pl.BlockSpec((None, 1, F_LEN), lambda h: (h, 0, 0))],
        out_specs=pl.BlockSpec((None, Q_BLOCK, K_SPAN), lambda h: (h, 0, 0)),
        out_shape=jax.ShapeDtypeStruct((H, Q_BLOCK, K_SPAN), F32), compiler_params=_params(("arbitrary",)))(frow)


def _bias_reduce(dbias, n_rel):
    H = dbias.shape[0]
    onehot = jnp.asarray((_bias_index()[:, None] == np.arange(n_rel)[None, :]).astype(np.float32), dtype=BF16)

    def body(d_ref, oh_ref, o_ref):
        x = jnp.concatenate([d_ref[...], jnp.zeros((Q_BLOCK, F_LEN - K_SPAN), F32)], axis=1)
        fine = _roll_rows(x, -1, 1, ROW_GROUP).reshape(Q_BLOCK // ROW_GROUP, ROW_GROUP, F_LEN)
        coarse = _roll_rows(jnp.sum(fine, axis=1), -1, ROW_GROUP, Q_BLOCK // ROW_GROUP)
        row = jnp.broadcast_to(jnp.sum(coarse, axis=0, keepdims=True), (8, F_LEN))
        acc = jnp.zeros((8, n_rel), F32)
        for _ in range(3):
            piece = row.astype(BF16)
            acc = acc + jnp.dot(piece, oh_ref[...], preferred_element_type=F32)
            row = row - piece.astype(F32)
        o_ref[...] = acc[0:1]

    return pl.pallas_call(
        body, name="bias_reduce", grid=(H,),
        in_specs=[pl.BlockSpec((None, Q_BLOCK, K_SPAN), lambda h: (h, 0, 0)),
                  pl.BlockSpec((F_LEN, n_rel), lambda h: (0, 0))],
        out_specs=pl.BlockSpec((None, 1, n_rel), lambda h: (h, 0, 0)),
        out_shape=jax.ShapeDtypeStruct((H, 1, n_rel), F32), compiler_params=_params(("arbitrary",)))(dbias, onehot)


def _attn_specs(S):
    hw = HEADS_PER_STEP * HEAD_DIM
    qspec = pl.BlockSpec((None, Q_BLOCK, hw), lambda g, b, i: (b, i, g))
    kspec = pl.BlockSpec((None, None, S, hw), lambda g, b, i: (0, b, 0, g))
    vspec = pl.BlockSpec((None, None, S, hw), lambda g, b, i: (1, b, 0, g))
    bspec = pl.BlockSpec((HEADS_PER_STEP, Q_BLOCK, K_SPAN), lambda g, b, i: (g, 0, 0))
    return hw, qspec, kspec, vspec, bspec


def _span_cases(i, fn):
    short = PAD // Q_BLOCK
    for j in range(short):
        pl.when(i == j)(functools.partial(fn, PAD - j * Q_BLOCK))
    pl.when(i >= short)(functools.partial(fn, 0))


def _key_start(i, off):
    return 0 if off else pl.multiple_of(i * Q_BLOCK - PAD, Q_BLOCK)


def _attn_exp(q_ref, k_ref, b_ref, h, k0, off):
    hs = slice(h * HEAD_DIM, (h + 1) * HEAD_DIM)
    kh = k_ref[pl.ds(k0, K_SPAN - off), hs]
    s = lax.dot_general(q_ref[:, hs], kh, (((1,), (1,)), ((), ())), preferred_element_type=F32) + b_ref[h, :, off:]
    p = jnp.exp(s - jnp.max(s, axis=-1, keepdims=True))
    return p, 1.0 / jnp.sum(p, axis=-1, keepdims=True), kh


def _attn_fwd(q, kv, bias, B, S):
    HD = q.shape[-1]
    hw, qspec, kspec, vspec, bspec = _attn_specs(S)

    def body(q_ref, k_ref, v_ref, b_ref, o_ref):
        i = pl.program_id(2)

        def block(off):
            k0 = _key_start(i, off)
            outs = []
            for h in range(HEADS_PER_STEP):
                hs = slice(h * HEAD_DIM, (h + 1) * HEAD_DIM)
                p, inv, _ = _attn_exp(q_ref, k_ref, b_ref, h, k0, off)
                outs.append(jnp.dot(p.astype(BF16), v_ref[pl.ds(k0, K_SPAN - off), hs],
                                    preferred_element_type=F32) * inv)
            o_ref[...] = jnp.concatenate(outs, axis=1).astype(BF16)

        _span_cases(i, block)

    return pl.pallas_call(
        body, name="attn_fwd", grid=(HD // hw, B, S // Q_BLOCK), in_specs=[qspec, kspec, vspec, bspec],
        out_specs=qspec, out_shape=jax.ShapeDtypeStruct((B, S, HD), BF16),
        compiler_params=_params(("arbitrary", "arbitrary", "arbitrary")))(q, kv, kv, bias)


def _attn_bwd(q, kv, bias, do, B, S):
    HD = q.shape[-1]
    H = HD // HEAD_DIM
    hw, qspec, kspec, vspec, bspec = _attn_specs(S)
    scale = HEAD_DIM ** -0.5
    nq = S // Q_BLOCK

    def body(q_ref, k_ref, v_ref, b_ref, do_ref, dq_ref, dkv_ref, db_ref, dk_acc, dv_acc):
        b, i = pl.program_id(1), pl.program_id(2)

        @pl.when(i == 0)
        def _():
            dk_acc[...] = jnp.zeros_like(dk_acc)
            dv_acc[...] = jnp.zeros_like(dv_acc)

        @pl.when((i == 0) & (b == 0))
        def _():
            db_ref[...] = jnp.zeros_like(db_ref)

        def block(off):
            k0 = _key_start(i, off)
            keys = pl.ds(k0, K_SPAN - off)
            for h in range(HEADS_PER_STEP):
                hs = slice(h * HEAD_DIM, (h + 1) * HEAD_DIM)
                p, inv, kh = _attn_exp(q_ref, k_ref, b_ref, h, k0, off)
                p = p * inv
                doh = do_ref[:, hs]
                dp = lax.dot_general(doh, v_ref[keys, hs], (((1,), (1,)), ((), ())), preferred_element_type=F32)
                ds = p * (dp - jnp.sum(p * dp, axis=-1, keepdims=True))
                db_ref[h, :, off:] += ds
                dsb = ds.astype(BF16)
                dq_ref[:, hs] = (jnp.dot(dsb, kh, preferred_element_type=F32) * scale).astype(BF16)
                dk_acc[hs, keys] += lax.dot_general(q_ref[:, hs], dsb, (((0,), (0,)), ((), ())),
                                                     preferred_element_type=F32)
                dv_acc[hs, keys] += lax.dot_general(doh, p.astype(BF16), (((0,), (0,)), ((), ())),
                                                     preferred_element_type=F32)

        _span_cases(i, block)

        @pl.when(i == nq - 1)
        def _():
            dkv_ref[0] = dk_acc[...].T.astype(BF16)
            dkv_ref[1] = dv_acc[...].T.astype(BF16)

    return pl.pallas_call(
        body, name="attn_bwd", grid=(HD // hw, B, nq), in_specs=[qspec, kspec, vspec, bspec, qspec],
        out_specs=[qspec, pl.BlockSpec((2, None, S, hw), lambda g, b, i: (0, b, 0, g)), bspec],
        out_shape=[jax.ShapeDtypeStruct((B, S, HD), BF16), jax.ShapeDtypeStruct((2, B, S, HD), BF16),
                   jax.ShapeDtypeStruct((H, Q_BLOCK, K_SPAN), F32)],
        scratch_shapes=[pltpu.VMEM((hw, S), F32), pltpu.VMEM((hw, S), F32)],
        compiler_params=_params(("arbitrary", "arbitrary", "arbitrary")))(q, kv, kv, bias, do)


def _sub_rows(R):
    for cand in (256, 352, 128, 64, 8):
        if R % cand == 0 and R > cand:
            return cand
    return R


def _adamw(w, g, m, v, *, name):
    R, C = w.shape
    tr = _sub_rows(R)

    def body(w_ref, g_ref, m_ref, v_ref, d_ref, nm_ref, nv_ref):
        g = g_ref[...]
        m = ADAM_B1 * m_ref[...] + (1.0 - ADAM_B1) * g
        v = ADAM_B2 * v_ref[...] + (1.0 - ADAM_B2) * (g * g)
        m_hat = m / (1.0 - ADAM_B1 ** ADAM_STEP)
        v_hat = v / (1.0 - ADAM_B2 ** ADAM_STEP)
        d_ref[...] = -ADAM_LR * (m_hat / (jnp.sqrt(v_hat) + ADAM_EPS) + ADAM_WD * w_ref[...])
        nm_ref[...] = m
        nv_ref[...] = v

    spec = pl.BlockSpec((tr, C), lambda i: (i, 0))
    return pl.pallas_call(body, name=name, grid=(R // tr,), in_specs=[spec] * 4, out_specs=[spec] * 3,
                          out_shape=[jax.ShapeDtypeStruct((R, C), F32)] * 3,
                          compiler_params=_params(("arbitrary",)))(w, g, m, v)


def _add_pair(units, got, core, *, name):
    n4, R, C = got.shape
    rows = n4 * R
    tr = 512 if rows % 512 == 0 else R

    def body(c_ref, u_ref, got_ref, o_ref):
        o_ref[...] = (u_ref[...].astype(F32) + got_ref[...].astype(F32)).astype(BF16)

    spec = pl.BlockSpec((tr, C), lambda i, c: (i, 0))
    grid_spec = pltpu.PrefetchScalarGridSpec(
        num_scalar_prefetch=1, grid=(rows // tr,),
        in_specs=[pl.BlockSpec((None, tr, C), lambda i, c: (c[0], i, 0)), spec], out_specs=spec)
    out = pl.pallas_call(body, name=name, grid_spec=grid_spec, out_shape=jax.ShapeDtypeStruct((rows, C), BF16),
                         compiler_params=_params(("arbitrary",)))(core.reshape(1), units.reshape(2, rows, C),
                                                                   got.reshape(rows, C))
    return out.reshape(n4, R, C)


def _sum_chips(w, own, got, pos, *, name, layer=0, into=None):
    _, R, C = own.shape
    tr = _sub_rows(R)
    nr = R // tr

    def body(p_ref, own_ref, got_ref, *rest):
        o_ref = rest[-1]
        o_ref[...] = (own_ref[...].astype(F32) + got_ref[0].astype(F32) + got_ref[1].astype(F32)
                      + got_ref[2].astype(F32))

    if w.row_sharded:
        out_map = lambda i, p: (layer, i, p[1])
    else:
        out_map = lambda i, p: (layer, p[1] * nr + i, 0)
    ins = [pos, own, got]
    in_specs = [pl.BlockSpec((None, tr, C), lambda i, p: (p[0], i, 0)),
                pl.BlockSpec((3, tr, C), lambda i, p: (0, i, 0))]
    alias = {}
    if into is not None:
        ins.append(into)
        in_specs.append(ANY)
        alias = {3: 0}
    grid_spec = pltpu.PrefetchScalarGridSpec(num_scalar_prefetch=1, grid=(nr,), in_specs=in_specs,
                                             out_specs=pl.BlockSpec((None, tr, C), out_map))
    return pl.pallas_call(body, name=name, grid_spec=grid_spec, input_output_aliases=alias,
                          out_shape=jax.ShapeDtypeStruct((w.L, w.ks, w.ns), F32),
                          compiler_params=_params(("arbitrary",)))(*ins)


def _mesh_pos():
    return lax.axis_index("x"), lax.axis_index("y"), lax.axis_index("c")


def _other_chips(x, y):
    return [(1 - x, y), (x, 1 - y), (1 - x, 1 - y)]


ANY = pl.BlockSpec(memory_space=pl.ANY)


class _W:
    def __init__(self, name, shard, row_sharded, direct=False):
        self.name = name
        self.direct = direct
        self.L, ks, ns = shard.shape
        self.row_sharded = row_sharded
        self.K, self.N = (ks * N_CHIPS, ns) if row_sharded else (ks, ns * N_CHIPS)
        self.ks, self.ns = ks, ns

    def shard_of(self, full, j):
        if self.row_sharded:
            return full.at[:, pl.ds(j * self.ks, self.ks), :]
        return full.at[:, :, pl.ds(j * self.ns, self.ns)]

    def half_of(self, shard, c):
        if self.row_sharded:
            return shard.at[:, :, pl.ds(c * (self.ns // 2), self.ns // 2)]
        return shard.at[:, pl.ds(c * (self.ks // 2), self.ks // 2), :]


HBM = pl.BlockSpec(memory_space=pltpu.HBM)
SEM = pl.BlockSpec(memory_space=pltpu.SEMAPHORE)
IN_FLIGHT = pltpu.SideEffectType.DATAFLOW_SIDE_EFFECTING


def _in_hbm(a):
    return pltpu.with_memory_space_constraint(a, pltpu.HBM)


def _gather_start(ws, shards, after, *, name):
    nw = len(ws)

    def body(*refs):
        src, dst = refs[:nw], refs[nw:2 * nw]
        send, recv = refs[2 * nw + 1:3 * nw + 1], refs[3 * nw + 1:4 * nw + 1]
        x, y, c = _mesh_pos()
        me = 2 * x + y
        for i, w in enumerate(ws):
            for f, (px, py) in enumerate(_other_chips(x, y)):
                for e in range(2 if w.direct else 1):
                    k = 2 * f + e
                    pltpu.make_async_remote_copy(
                        src_ref=w.half_of(src[i], c), dst_ref=w.half_of(w.shard_of(dst[i], me), c),
                        send_sem=send[i].at[k], recv_sem=recv[i].at[k], device_id=(px, py, c if e == 0 else 1 - c),
                        device_id_type=MESH).start()

    fulls = [lax.empty((w.L, w.K, w.N), BF16) for w in ws]
    out = pl.pallas_call(
        body, name=name, in_specs=[HBM] * (2 * nw) + [ANY],
        out_specs=[SEM] * (2 * nw) + [HBM] * (2 * nw),
        out_shape=[pltpu.SemaphoreType.DMA((6,))] * (2 * nw)
        + [pltpu.HBM(s.shape, BF16) for s in shards] + [pltpu.HBM(f.shape, BF16) for f in fulls],
        input_output_aliases={i: 2 * nw + i for i in range(2 * nw)},
        compiler_params=pltpu.CompilerParams(has_side_effects=IN_FLIGHT))(
            *[_in_hbm(s) for s in shards], *[_in_hbm(f) for f in fulls], after)
    return [(out[i], out[nw + i], out[2 * nw + i], out[3 * nw + i]) for i in range(nw)]


def _gather_wait(ws, flight, after, *, name):
    nw = len(ws)

    def body(*refs):
        src, dst = refs[:nw], refs[nw:2 * nw]
        send, recv = refs[2 * nw:3 * nw], refs[3 * nw:4 * nw]
        x, y, c = _mesh_pos()
        for i, w in enumerate(ws):
            for f, (px, py) in enumerate(_other_chips(x, y)):
                for e in range(2 if w.direct else 1):
                    k = 2 * f + e
                    landed = w.half_of(w.shard_of(dst[i], 2 * px + py), c if e == 0 else 1 - c)
                    cp = pltpu.make_async_remote_copy(
                        src_ref=w.half_of(src[i], c), dst_ref=landed, send_sem=send[i].at[k], recv_sem=recv[i].at[k],
                        device_id=(px, py, c), device_id_type=MESH)
                    cp.wait_send()
                    cp.wait_recv()

    shards, fulls = [fl[2] for fl in flight], [fl[3] for fl in flight]
    out = pl.pallas_call(
        body, name=name, in_specs=[HBM] * (2 * nw) + [SEM] * (2 * nw) + [ANY],
        out_specs=[HBM] * (2 * nw),
        out_shape=[pltpu.HBM(s.shape, BF16) for s in shards] + [pltpu.HBM(f.shape, BF16) for f in fulls],
        input_output_aliases={i: i for i in range(2 * nw)},
        compiler_params=pltpu.CompilerParams(has_side_effects=IN_FLIGHT))(
            *shards, *fulls, *[fl[0] for fl in flight], *[fl[1] for fl in flight], after)
    return out[:nw], out[nw:]


def _gather_finish(ws, shards, fulls, *, name):
    nw = len(ws)
    forward = not ws[0].direct

    def body(*refs):
        src, dst, stage = refs[:nw], refs[3 * nw:4 * nw], refs[4 * nw:5 * nw]
        send_sems, recv_sems, load_sems, store_sems = refs[5 * nw:]
        x, y, c = _mesh_pos()
        me = 2 * x + y
        sibling = (x, y, 1 - c)
        chips = _other_chips(x, y)

        def fwd(i, w, f, half):
            px, py = chips[f]
            landed = w.half_of(w.shard_of(dst[i], 2 * px + py), half)
            return pltpu.make_async_remote_copy(src_ref=landed, dst_ref=landed, send_sem=send_sems.at[3 * i + f],
                                                recv_sem=recv_sems.at[3 * i + f], device_id=sibling,
                                                device_id_type=MESH)

        loads = [pltpu.make_async_copy(src[i], stage[i], load_sems.at[i]) for i in range(nw)]
        for cp in loads:
            cp.start()
        sends = [fwd(i, w, f, c) for i, w in enumerate(ws) for f in range(3)] if forward else []
        for cp in sends:
            cp.start()
        stores = [pltpu.make_async_copy(stage[i], w.shard_of(dst[i], me), store_sems.at[i])
                  for i, w in enumerate(ws)]
        for ld, st in zip(loads, stores):
            ld.wait()
            st.start()
        if forward:
            for i, w in enumerate(ws):
                for f in range(3):
                    fwd(i, w, f, 1 - c).wait_recv()
        for cp in sends:
            cp.wait_send()
        for cp in stores:
            cp.wait()

    out = pl.pallas_call(
        body, name=name, in_specs=[ANY] * (2 * nw), out_specs=[ANY] * (2 * nw),
        out_shape=[jax.ShapeDtypeStruct(s.shape, BF16) for s in shards]
        + [jax.ShapeDtypeStruct(f.shape, BF16) for f in fulls],
        input_output_aliases={i: i for i in range(2 * nw)},
        scratch_shapes=[pltpu.VMEM((w.L, w.ks, w.ns), BF16) for w in ws]
        + [pltpu.SemaphoreType.DMA((3 * nw,)), pltpu.SemaphoreType.DMA((3 * nw,)), pltpu.SemaphoreType.DMA((nw,)),
           pltpu.SemaphoreType.DMA((nw,))],
        compiler_params=_params(has_side_effects=True))(*shards, *fulls)
    return out[nw:]


def _split_copies(name, srcs, lands, n_sems, copies_of, *, flight=None, after=None):
    n = len(srcs)
    starting = flight is None

    def body(*refs):
        src, land = refs[:n], refs[n:2 * n]
        sems = refs[2 * n + 1:4 * n + 1] if starting else refs[2 * n:4 * n]
        for i in range(n):
            for cp in copies_of(i, src[i], land[i], sems[i], sems[n + i]):
                if starting:
                    cp.start()
                else:
                    cp.wait_send()
                    cp.wait_recv()

    thru = [pltpu.HBM(a.shape, a.dtype) for a in list(srcs) + list(lands)]
    if starting:
        out = pl.pallas_call(
            body, name=name, in_specs=[HBM] * (2 * n) + [ANY], out_specs=[SEM] * (2 * n) + [HBM] * (2 * n),
            out_shape=[pltpu.SemaphoreType.DMA((n_sems,))] * (2 * n) + thru,
            input_output_aliases={i: 2 * n + i for i in range(2 * n)},
            compiler_params=pltpu.CompilerParams(has_side_effects=IN_FLIGHT))(
                *[_in_hbm(a) for a in srcs], *[_in_hbm(a) for a in lands], after)
        return [(out[i], out[n + i], out[2 * n + i], out[3 * n + i]) for i in range(n)]
    out = pl.pallas_call(
        body, name=name, in_specs=[HBM] * (2 * n) + [SEM] * (2 * n) + [ANY], out_specs=[HBM] * (2 * n),
        out_shape=thru, input_output_aliases={i: i for i in range(2 * n)},
        compiler_params=pltpu.CompilerParams(has_side_effects=IN_FLIGHT))(
            *srcs, *lands, *[fl[0] for fl in flight], *[fl[1] for fl in flight], after)
    return out[:n], out[n:]


def _sum8(land, vec, me):
    R = vec.shape[0]

    def body(me_ref, land_ref, vec_ref, o_ref):
        acc = jnp.zeros((R, 128), F32)
        for d in range(8):
            acc = acc + jnp.where(me_ref[0] == d, vec_ref[...], land_ref[d])
        o_ref[...] = acc

    grid_spec = pltpu.PrefetchScalarGridSpec(
        num_scalar_prefetch=1, grid=(1,),
        in_specs=[pl.BlockSpec((8, R, 128), lambda i, m: (0, 0, 0)), pl.BlockSpec((R, 128), lambda i, m: (0, 0))],
        out_specs=pl.BlockSpec((R, 128), lambda i, m: (0, 0)))
    return pl.pallas_call(body, name="sum8", grid_spec=grid_spec, out_shape=jax.ShapeDtypeStruct((R, 128), F32),
                          compiler_params=_params(("arbitrary",)))(me.reshape(1), land, vec)


def _swap_copies(i, src, got, send, recv):
    x, y, c = _mesh_pos()
    return [pltpu.make_async_remote_copy(src_ref=src.at[1 - c], dst_ref=got, send_sem=send.at[0], recv_sem=recv.at[0],
                                         device_id=(x, y, 1 - c), device_id_type=MESH)]


def _gather8_copies(i, src, land, send, recv):
    x, y, c = _mesh_pos()
    me = 4 * x + 2 * y + c
    peers = [(x, y, 1 - c)] + [(px, py, pc) for px, py in _other_chips(x, y) for pc in (c, 1 - c)]
    return [pltpu.make_async_remote_copy(src_ref=src, dst_ref=land.at[me], send_sem=send.at[k], recv_sem=recv.at[k],
                                         device_id=peer, device_id_type=MESH) for k, peer in enumerate(peers)]


def _scatter_copy(src, got, send, recv, f, chip, c):
    px, py = chip
    return pltpu.make_async_remote_copy(src_ref=src.at[2 * px + py], dst_ref=got.at[f], send_sem=send.at[f],
                                        recv_sem=recv.at[f], device_id=(px, py, c), device_id_type=MESH)


def _scatter_start(sums, *, name):
    nw = len(sums)

    def body(*refs):
        src, got = refs[:nw], refs[nw:2 * nw]
        send, recv = refs[2 * nw:3 * nw], refs[3 * nw:4 * nw]
        x, y, c = _mesh_pos()
        for i in range(nw):
            for f, chip in enumerate(_other_chips(x, y)):
                _scatter_copy(src[i], got[i], send[i], recv[i], f, chip, c).start()

    lands = [lax.empty((3,) + s.shape[1:], BF16) for s in sums]
    out = pl.pallas_call(
        body, name=name, in_specs=[HBM] * (2 * nw), out_specs=[SEM] * (2 * nw) + [HBM] * (2 * nw),
        out_shape=[pltpu.SemaphoreType.DMA((3,))] * (2 * nw)
        + [pltpu.HBM(s.shape, BF16) for s in sums] + [pltpu.HBM(l.shape, BF16) for l in lands],
        input_output_aliases={i: 2 * nw + i for i in range(2 * nw)},
        compiler_params=pltpu.CompilerParams(has_side_effects=IN_FLIGHT))(
            *[_in_hbm(s) for s in sums], *[_in_hbm(l) for l in lands])
    return [(out[i], out[nw + i], out[2 * nw + i], out[3 * nw + i]) for i in range(nw)]


def _scatter_wait(flight, after):
    nw = len(flight)

    def body(*refs):
        src, got = refs[:nw], refs[nw:2 * nw]
        send, recv = refs[2 * nw:3 * nw], refs[3 * nw:4 * nw]
        x, y, c = _mesh_pos()
        for i in range(nw):
            for f, chip in enumerate(_other_chips(x, y)):
                cp = _scatter_copy(src[i], got[i], send[i], recv[i], f, chip, c)
                cp.wait_send()
                cp.wait_recv()

    sums, lands = [fl[2] for fl in flight], [fl[3] for fl in flight]
    out = pl.pallas_call(
        body, name="scatter_wait", in_specs=[HBM] * (2 * nw) + [SEM] * (2 * nw) + [ANY], out_specs=[HBM] * (2 * nw),
        out_shape=[pltpu.HBM(s.shape, BF16) for s in sums] + [pltpu.HBM(l.shape, BF16) for l in lands],
        input_output_aliases={i: i for i in range(2 * nw)},
        compiler_params=pltpu.CompilerParams(has_side_effects=IN_FLIGHT))(
            *sums, *lands, *[fl[0] for fl in flight], *[fl[1] for fl in flight], after)
    return out[:nw], out[nw:]


def _join_halves(ws, shards):
    nw = len(ws)

    def body(*refs):
        buf = refs[nw:2 * nw]
        send_sems, recv_sems = refs[2 * nw:]
        x, y, c = _mesh_pos()
        sibling = (x, y, 1 - c)

        def copy(i, w, half):
            region = w.half_of(buf[i], half)
            return pltpu.make_async_remote_copy(src_ref=region, dst_ref=region, send_sem=send_sems.at[i],
                                                recv_sem=recv_sems.at[i], device_id=sibling, device_id_type=MESH)

        sends = [copy(i, w, c) for i, w in enumerate(ws)]
        for cp in sends:
            cp.start()
        for i, w in enumerate(ws):
            copy(i, w, 1 - c).wait_recv()
        for cp in sends:
            cp.wait_send()

    return pl.pallas_call(
        body, name="join_halves", in_specs=[ANY] * nw, out_specs=[ANY] * nw,
        out_shape=[jax.ShapeDtypeStruct((w.L, w.ks, w.ns), F32) for w in ws],
        input_output_aliases={i: i for i in range(nw)},
        scratch_shapes=[pltpu.SemaphoreType.DMA((nw,)), pltpu.SemaphoreType.DMA((nw,))],
        compiler_params=_params(has_side_effects=True))(*shards)


def _allreduce_small(vec):
    R = vec.shape[0]

    def body(x_ref, o_ref, buf, send_sems, recv_sems):
        x, y, c = _mesh_pos()
        me, sibling = (x, y, c), (x, y, 1 - c)
        chips = _other_chips(x, y)

        def slot(px, py, pc):
            return buf.at[4 * px + 2 * py + pc]

        def copy(k, block, to, src=None):
            return pltpu.make_async_remote_copy(src_ref=slot(*block) if src is None else src, dst_ref=slot(*block),
                                                send_sem=send_sems.at[k], recv_sem=recv_sems.at[k], device_id=to,
                                                device_id_type=MESH)

        first = [copy(0, me, sibling, src=x_ref)] + [copy(1 + f, me, (*chip, c), src=x_ref)
                                                     for f, chip in enumerate(chips)]
        for cp in first:
            cp.start()
        passed = [copy(4 + f, (*chip, c), sibling) for f, chip in enumerate(chips)]
        for f, chip in enumerate(chips):
            copy(1 + f, (*chip, c), me).wait_recv()
            passed[f].start()
        copy(0, sibling, me).wait_recv()
        for f, chip in enumerate(chips):
            copy(4 + f, (*chip, 1 - c), me).wait_recv()
        for cp in first + passed:
            cp.wait_send()
        slot(*me)[...] = x_ref[...]
        acc = buf[0]
        for d in range(1, 8):
            acc = acc + buf[d]
        o_ref[...] = acc

    return pl.pallas_call(
        body, name="allreduce_small", in_specs=[pl.BlockSpec(memory_space=pltpu.VMEM)],
        out_specs=pl.BlockSpec(memory_space=pltpu.VMEM), out_shape=jax.ShapeDtypeStruct((R, 128), F32),
        scratch_shapes=[pltpu.VMEM((8, R, 128), F32), pltpu.SemaphoreType.DMA((7,)), pltpu.SemaphoreType.DMA((7,))],
        compiler_params=_params())(vec)


def _pack(parts):
    flat = jnp.concatenate([p.reshape(-1).astype(F32) for p in parts])
    n = flat.shape[0]
    pad = (-n) % (64 * 128)
    return jnp.pad(flat, (0, pad)).reshape(-1, 128)


def _unpack(vec, shapes):
    flat = vec.reshape(-1)
    out, off = [], 0
    for s in shapes:
        n = int(np.prod(s))
        out.append(flat[off:off + n].reshape(s))
        off += n
    return out


def kernel(x, a_norm_g, a_w_in, a_v_norm_g, a_w_s, a_b_s, a_w_out, kv_norm_g, w_kv, b_norm_g, b_w_q, b_rel_bias, b_w_o, f_norm_g, f_w_in, f_conv_w, f_conv_b, f_w_down, final_norm_g, loss_target, m_a_norm_g, m_a_w_in, m_a_v_norm_g, m_a_w_s, m_a_b_s, m_a_w_out, m_kv_norm_g, m_w_kv, m_b_norm_g, m_b_w_q, m_b_rel_bias, m_b_w_o, m_f_norm_g, m_f_w_in, m_f_conv_w, m_f_conv_b, m_f_w_down, m_final_norm_g, v_a_norm_g, v_a_w_in, v_a_v_norm_g, v_a_w_s, v_a_b_s, v_a_w_out, v_kv_norm_g, v_w_kv, v_b_norm_g, v_b_w_q, v_b_rel_bias, v_b_w_o, v_f_norm_g, v_f_w_in, v_f_conv_w, v_f_conv_b, v_f_w_down, v_final_norm_g):
    B, S, D = x.shape
    T = B * S
    xi, yi, ci = lax.axis_index("x"), lax.axis_index("y"), lax.axis_index("c")
    j_me = (2 * xi + yi).astype(jnp.int32)
    core = ci.astype(jnp.int32)
    pos = jnp.stack([j_me, core])

    w_shards = {"a_w_in": (a_w_in, False), "a_w_out": (a_w_out, True), "w_kv": (w_kv[None], False),
                "b_w_q": (b_w_q, True), "b_w_o": (b_w_o, True), "f_w_in": (f_w_in, False), "f_w_down": (f_w_down, True)}
    names = list(w_shards)
    ws = [_W(n, w_shards[n][0], w_shards[n][1]) for n in names]
    g_shards = {"a_w_in": (a_w_in, False), "a_w_out": (a_w_out, True),
                "f_w_in0": (f_w_in[0:1], False), "f_w_down0": (f_w_down[0:1], True),
                "w_kv": (w_kv[None], False), "b_w_q": (b_w_q, True), "b_w_o": (b_w_o, True),
                "f_w_in1": (f_w_in[1:2], False), "f_w_down1": (f_w_down[1:2], True)}
    g_names = list(g_shards)
    g_ws = {n: _W(n, *g_shards[n], direct=n in ("w_kv", "b_w_q", "b_w_o", "f_w_in1", "f_w_down1")) for n in g_names}

    Wd = a_w_in.shape[1]
    GW = a_v_norm_g.shape[1] * N_CHIPS
    F2 = f_conv_w.shape[2] * N_CHIPS
    Fh = F2 // 2
    nsd, nsg, nsf = a_norm_g.shape[1], a_v_norm_g.shape[1], f_conv_w.shape[2]
    own = (ci == 0).astype(F32)
    place = lambda sh, width, n: lax.dynamic_update_slice_in_dim(
        jnp.zeros(sh.shape[:-1] + (width,), F32), sh * own, j_me * n, axis=sh.ndim - 1)
    def tied(x, flight):
        x, thru = lax.optimization_barrier((x, flight[0][2]))
        return x, [flight[0][:2] + (thru,) + flight[0][3:]] + flight[1:]

    gathered = _allreduce_small(_pack([place(a_norm_g, Wd, nsd), place(a_v_norm_g, GW, nsg),
                                       place(f_conv_w, F2, nsf)]))
    a_g, a_vg, conv_w = _unpack(gathered, [(1, Wd), (1, GW), (2, 3, F2)])
    flight = dict(zip(g_names, _gather_start([g_ws[n] for n in g_names],
                                             [g_shards[n][0].astype(BF16) for n in g_names], gathered,
                                             name="gather_start")))
    full = {}

    def arrive(group, after, tag):
        gw = [g_ws[n] for n in group]
        sh, fu = _gather_wait(gw, [flight[n] for n in group], after, name=f"gather_wait_{tag}")
        full.update(zip(group, _gather_finish(gw, sh, fu, name=f"gather_finish_{tag}")))
    conv_w2 = conv_w.reshape(2, 3, 2, Fh).transpose(0, 2, 1, 3)
    conv_b2 = f_conv_b.reshape(2, 2, Fh)

    h0 = x.reshape(T, D)
    target = loss_target.reshape(T, D)
    bs_tile = jnp.repeat(a_b_s[0].T, GROUP_DIM, axis=1)
    ws_a = a_w_s[0]
    scale = HEAD_DIM ** -0.5
    HD = b_w_q.shape[2]
    H = HD // HEAD_DIM
    n_rel = b_rel_bias.shape[-1]
    frow, (flight["a_w_in"],) = tied(b_rel_bias[0][:, _bias_index()].reshape(H, 1, F_LEN), [flight["a_w_in"]])
    bias = _bias_expand(frow)

    def ffn_fwd(h, l, loss=None):
        out = _ffn_fwd(h, full[f"f_w_in{l}"], f_norm_g[l], conv_w2[l], conv_b2[l], full[f"f_w_down{l}"], S,
                       loss=loss, name=f"ffn{l}")
        yff, a, c, n = out[1:5]
        return (out[0] if loss is None else (out[0], out[5], out[6])), (a, c, n, yff)

    arrive(["a_w_in", "a_w_out"], bias, "a")
    h1, zp, out_a, n_a = _mixer_a_fwd(h0, full["a_w_in"], a_g[0], a_vg, ws_a, bs_tile, full["a_w_out"])
    arrive(["f_w_in0", "f_w_down0"], h1, "f0")
    h2, saved0 = ffn_fwd(h1, 0)
    arrive(["w_kv", "b_w_q", "b_w_o"], h2, "b")
    arrive(["f_w_in1", "f_w_down1"], h2, "f1")
    q, kv, n_q, n_kv = _qkv_fwd(h2, full["b_w_q"], b_norm_g[0], full["w_kv"], kv_norm_g, scale)
    kv4, q3 = kv.reshape(2, B, S, HD), q.reshape(B, S, HD)
    o = _attn_fwd(q3, kv4, bias, B, S).reshape(T, HD)
    h3 = _mm(o, full["b_w_o"], res=h2, name="attn_out")
    (dh, loss8, dg_final), saved1 = ffn_fwd(h3, 1, loss=(final_norm_g, target))

    units = {}

    in_flight = {}

    def swap_start(group, tag, carry):
        us = [units[n] for n in group]
        lands = [lax.empty(u.shape[1:], BF16) for u in us]
        carry, flight = tied(carry, _split_copies(f"swap_start_{tag}", us, lands, 1, _swap_copies, after=carry))
        return (group, tag, flight), carry

    def reduce_start(swap, after):
        group, tag, flight = swap
        us, got = _split_copies(f"swap_wait_{tag}", [fl[2] for fl in flight], [fl[3] for fl in flight], 1,
                                _swap_copies, flight=flight, after=after)
        sums = [_add_pair(u, g_, core, name=f"pair_{n}") for n, u, g_ in zip(group, us, got)]
        after, flight = tied(after, _scatter_start(sums, name=f"scatter_start_{tag}"))
        in_flight.update(zip(group, flight))
        return after

    def ffn_bwd(dh, h, saved, l, early):
        a, c, n, yff = saved
        units[f"f_w_down{l}"] = _mm_tn(yff, dh, rows_are_shards=True, name=f"ffn{l}_down_dw")
        dh_in = dh
        if early:
            sw, dh_in = swap_start([f"f_w_down{l}"], f"fd{l}", dh)
        dyff = _mm(dh_in, full[f"f_w_down{l}"], trans_w=True, out_dtype=BF16, name=f"ffn{l}_down_dx")
        if early:
            dyff = reduce_start(sw, dyff)
        da, dcw, dcb = _conv_bwd(a, c, dyff, conv_w2[l], S)
        units[f"f_w_in{l}"] = _mm_tn(n, da, split_y=True, name=f"ffn{l}_in_dw")
        sw, da = swap_start([f"f_w_in{l}"] if early else [f"f_w_down{l}", f"f_w_in{l}"], f"f{l}", da)
        dh, dg = _mm(da, full[f"f_w_in{l}"], trans_w=True, split_x=True, bwd=(h, f_norm_g[l], dh),
                     name=f"ffn{l}_in_dx")
        return reduce_start(sw, dh), dg, dcw, dcb

    dh, dg_f1, dcw1, dcb1 = ffn_bwd(dh, h3, saved1, 1, False)
    do = _mm(dh, full["b_w_o"], trans_w=True, out_dtype=BF16, name="attn_out_dx")
    units["b_w_o"] = _mm_tn(o, dh, rows_are_shards=True, name="b_w_o_dw")
    dq, dkv, dbias = _attn_bwd(q3, kv4, bias, do.reshape(B, S, HD), B, S)
    dq, d_rel = lax.optimization_barrier((dq, _bias_reduce(dbias, n_rel)))
    d_rel = d_rel.reshape(1, H, n_rel)
    dq, dkv = dq.reshape(T, HD), dkv.reshape(2, T, HD)
    units["b_w_q"] = _mm_tn(n_q, dq, rows_are_shards=True, name="b_w_q_dw")
    units["w_kv"] = _mm_tn(n_kv, dkv, split_y=True, name="w_kv_dw")
    sw, dkv = swap_start(["b_w_o", "b_w_q", "w_kv"], "b", dkv)
    dh, dg_b, dg_kv = _qkv_dx(dq, full["b_w_q"], b_norm_g[0], dkv, full["w_kv"], kv_norm_g, h2, dh)
    dh = reduce_start(sw, dh)
    dh, dg_f0, dcw0, dcb0 = ffn_bwd(dh, h1, saved0, 0, True)
    units["a_w_out"] = _mm_tn(out_a, dh, rows_are_shards=True, name="a_w_out_dw")
    sw, dh_in = swap_start(["a_w_out"], "ao", dh)
    d_out = _mm(dh_in, full["a_w_out"], trans_w=True, out_dtype=BF16, name="a_out_dx")
    d_out = reduce_start(sw, d_out)
    dzp, dws, dbs, dgv = _gate_bwd(zp, d_out, a_vg, ws_a, bs_tile)
    units["a_w_in"] = _mm_tn(n_a, dzp, name="a_w_in_dw")
    sw, dzp_in = swap_start(["a_w_in"], "ai", dzp)
    dzp_in = reduce_start(sw, dzp_in)
    grad_x, dg_a = _mm(dzp_in, full["a_w_in"], trans_w=True, bwd=(h0, a_g[0], dh), name="a_in_dx")

    to_flat = lambda d: d.transpose(1, 0, 2).reshape(3, F2)
    small = {"a_norm_g": dg_a, "a_v_norm_g": dgv, "a_w_s": dws[None], "a_b_s": dbs[None], "kv_norm_g": dg_kv[0],
             "b_norm_g": dg_b, "b_rel_bias": d_rel, "f_norm_g": jnp.concatenate([dg_f0, dg_f1], axis=0),
             "f_conv_w": jnp.stack([to_flat(dcw0), to_flat(dcw1)]),
             "f_conv_b": jnp.stack([dcb0.reshape(F2), dcb1.reshape(F2)]), "final_norm_g": dg_final[0]}
    snames = list(small)
    small_vec = _pack([small[n] for n in snames] + [loss8[0:1, 0:1]])
    grad_x, small_flight = tied(grad_x, _split_copies("small_start", [small_vec],
                                                      [lax.empty((8,) + small_vec.shape, F32)], 7, _gather8_copies,
                                                      after=grad_x))

    sums, recv = _scatter_wait([in_flight[n] for n in g_names], grad_x)
    sums, recv = dict(zip(g_names, sums)), dict(zip(g_names, recv))
    halves = []
    for n, w in zip(names, ws):
        if w.L == 1:
            halves.append(_sum_chips(w, sums[n], recv[n], pos, name=f"chips_{n}"))
        else:
            first = _sum_chips(w, sums[n + "0"], recv[n + "0"], pos, name=f"chips_{n}0")
            halves.append(_sum_chips(w, sums[n + "1"], recv[n + "1"], pos, layer=1, into=first, name=f"chips_{n}1"))
    g_big = dict(zip(names, _join_halves(ws, halves)))
    g_big["w_kv"] = g_big["w_kv"][0]

    given = dict(a_norm_g=(a_norm_g, m_a_norm_g, v_a_norm_g), a_w_in=(a_w_in, m_a_w_in, v_a_w_in),
                 a_v_norm_g=(a_v_norm_g, m_a_v_norm_g, v_a_v_norm_g), a_w_s=(a_w_s, m_a_w_s, v_a_w_s),
                 a_b_s=(a_b_s, m_a_b_s, v_a_b_s), a_w_out=(a_w_out, m_a_w_out, v_a_w_out),
                 kv_norm_g=(kv_norm_g, m_kv_norm_g, v_kv_norm_g), w_kv=(w_kv, m_w_kv, v_w_kv),
                 b_norm_g=(b_norm_g, m_b_norm_g, v_b_norm_g), b_w_q=(b_w_q, m_b_w_q, v_b_w_q),
                 b_rel_bias=(b_rel_bias, m_b_rel_bias, v_b_rel_bias), b_w_o=(b_w_o, m_b_w_o, v_b_w_o),
                 f_norm_g=(f_norm_g, m_f_norm_g, v_f_norm_g), f_w_in=(f_w_in, m_f_w_in, v_f_w_in),
                 f_conv_w=(f_conv_w, m_f_conv_w, v_f_conv_w), f_conv_b=(f_conv_b, m_f_conv_b, v_f_conv_b),
                 f_w_down=(f_w_down, m_f_w_down, v_f_w_down), final_norm_g=(final_norm_g, m_final_norm_g, v_final_norm_g))
    order = list(given)
    grads, deltas, new_m, new_v = {}, {}, {}, {}
    for n in names:
        w_, m_, v_ = given[n]
        g_ = g_big[n]
        C = w_.shape[-1]
        d2, m2, v2 = _adamw(w_.reshape(-1, C), g_.reshape(-1, C), m_.reshape(-1, C), v_.reshape(-1, C),
                            name=f"adamw_{n}")
        grads[n], deltas[n], new_m[n], new_v[n] = g_.reshape(w_.shape), d2.reshape(w_.shape), m2.reshape(w_.shape), \
            v2.reshape(w_.shape)
    vecs, lands = _split_copies("small_wait", [small_flight[0][2]], [small_flight[0][3]], 7, _gather8_copies,
                                flight=small_flight, after=deltas[names[-1]])
    red = _sum8(lands[0], vecs[0], (4 * xi + 2 * yi + ci).astype(jnp.int32))
    parts = _unpack(red, [small[n].shape for n in snames] + [(1,)])
    g_small = dict(zip(snames, parts[:-1]))
    loss = parts[-1][0]
    g_small["a_norm_g"] = lax.dynamic_slice_in_dim(g_small["a_norm_g"], j_me * nsd, nsd, axis=1)
    g_small["a_v_norm_g"] = lax.dynamic_slice_in_dim(g_small["a_v_norm_g"], j_me * nsg, nsg, axis=1)
    g_small["f_conv_w"] = lax.dynamic_slice_in_dim(g_small["f_conv_w"], j_me * nsf, nsf, axis=2)

    sm = [n for n in order if n not in names]
    d2, m2, v2 = _adamw(_pack([given[n][0] for n in sm]), _pack([g_small[n].reshape(given[n][0].shape) for n in sm]),
                        _pack([given[n][1] for n in sm]), _pack([given[n][2] for n in sm]), name="adamw_small")
    shapes = [given[n][0].shape for n in sm]
    for n, d_, m_, v_ in zip(sm, _unpack(d2, shapes), _unpack(m2, shapes), _unpack(v2, shapes)):
        grads[n], deltas[n], new_m[n], new_v[n] = g_small[n].reshape(given[n][0].shape), d_, m_, v_

    return (loss, grad_x.reshape(B, S, D), *[grads[n] for n in order], *[deltas[n] for n in order],
            *[new_m[n] for n in order], *[new_v[n] for n in order])
```

```python
import functools
import math

import numpy as np
import jax
import jax.numpy as jnp
from jax import lax
from jax.experimental import pallas as pl
from jax.experimental.pallas import tpu as pltpu

F32 = jnp.float32
BF16 = jnp.bfloat16
MESH = pl.DeviceIdType.MESH

EPS = 1e-6
NEG_INF = -1e30
CHUNK = 64
GMLP_BLOCK = 128
GROUP_DIM = 128
HEAD_DIM = 64
LEFT_CHUNKS = 8
PAD = LEFT_CHUNKS * CHUNK
REL_CLIP = 128
Q_BLOCK = 256
K_SPAN = PAD + Q_BLOCK
F_LEN = K_SPAN + Q_BLOCK
HEADS_PER_STEP = 4
N_CHIPS = 4

ADAM_LR = 0.001
ADAM_B1 = 0.9
ADAM_B2 = 0.999
ADAM_EPS = 1e-08
ADAM_WD = 0.01
ADAM_STEP = 10

VMEM_LIMIT = 56 * 1024 * 1024


def _params(sem=None, **kw):
    if sem is not None:
        kw["dimension_semantics"] = sem
    return pltpu.CompilerParams(vmem_limit_bytes=VMEM_LIMIT, **kw)


def _rms(xf):
    r = lax.rsqrt(jnp.mean(xf * xf, axis=-1, keepdims=True) + EPS)
    return xf * r, r


def _gelu(x, with_grad=False):
    c = math.sqrt(2.0 / math.pi)
    x2 = x * x
    t = jnp.tanh(c * x * (1.0 + 0.044715 * x2))
    half = 0.5 * (1.0 + t)
    if not with_grad:
        return x * half
    return x * half, half + 0.5 * x * (1.0 - t * t) * c * (1.0 + 3.0 * 0.044715 * x2)


def _col_tile(n):
    if n <= 1024:
        return n
    for t in (1408, 1024, 512):
        if n % t == 0:
            return t
    raise ValueError(n)


def _row_tile(t, want):
    while t % want:
        want //= 2
    return want


def _loss_epilogue(h, g_ref, t_ref, dh_ref, loss_ref, dg_ref, first):
    @pl.when(first)
    def _():
        loss_ref[...] = jnp.zeros_like(loss_ref)
        dg_ref[...] = jnp.zeros_like(dg_ref)

    n, r = _rms(h)
    g = g_ref[...]
    e = n * g - t_ref[...]
    loss_ref[...] += 0.5 * jnp.sum(jnp.mean(e * e, axis=-1, keepdims=True), axis=0, keepdims=True)
    dy = e * (1.0 / h.shape[-1])
    dg_ref[...] += jnp.sum(dy * n, axis=0, keepdims=True)
    t = dy * g
    dh_ref[...] = r * (t - n * jnp.mean(t * n, axis=-1, keepdims=True))


def _mm(x, w, *, name, trans_w=False, res=None, out_dtype=F32, bwd=None, split_x=False, tm=512):
    T = x.shape[-2]
    K = 2 * x.shape[-1] if split_x else x.shape[-1]
    N = w.shape[-2] if trans_w else w.shape[-1]
    tm = _row_tile(T, 2 * tm if max(K, N) <= 2048 else tm)
    has_res, has_bwd = res is not None, bwd is not None
    dims = (((1,), (1,)), ((), ())) if trans_w else (((1,), (0,)), ((), ()))

    def body(*refs):
        it = iter(refs)
        x_ref, w_ref = next(it), next(it)
        res_ref = next(it) if has_res else None
        if has_bwd:
            h_ref, bg_ref, dh_ref = next(it), next(it), next(it)
        o_ref = next(it)
        if split_x:
            kh = K // 2
            acc = lax.dot_general(x_ref[0].astype(BF16), w_ref[:, :kh] if trans_w else w_ref[:kh, :], dims,
                                  preferred_element_type=F32)
            acc = acc + lax.dot_general(x_ref[1].astype(BF16), w_ref[:, kh:] if trans_w else w_ref[kh:, :], dims,
                                        preferred_element_type=F32)
        else:
            acc = lax.dot_general(x_ref[...].astype(BF16), w_ref[...], dims, preferred_element_type=F32)
        if has_res:
            acc = acc + res_ref[...]
        if has_bwd:
            dg_ref = next(it)
            n, r = _rms(h_ref[...])

            @pl.when(pl.program_id(0) == 0)
            def _():
                dg_ref[...] = jnp.zeros_like(dg_ref)

            dg_ref[...] += jnp.sum(acc * n, axis=0, keepdims=True)
            t = acc * bg_ref[...]
            o_ref[...] = dh_ref[...] + r * (t - n * jnp.mean(t * n, axis=-1, keepdims=True))
        else:
            o_ref[...] = acc.astype(out_dtype)

    row = lambda width: pl.BlockSpec((tm, width), lambda m: (m, 0))
    ins = [x, w]
    in_specs = [pl.BlockSpec((2, tm, K // 2), lambda m: (0, m, 0)) if split_x else row(K),
                pl.BlockSpec((None,) + w.shape[1:], lambda m: (0, 0, 0), pipeline_mode=pl.Buffered(1))]
    if has_res:
        ins.append(res)
        in_specs.append(row(N))
    out_shape = [jax.ShapeDtypeStruct((T, N), F32 if has_bwd else out_dtype)]
    out_specs = [row(N)]
    if has_bwd:
        h, g, dh = bwd
        ins += [h, g.reshape(1, N), dh]
        in_specs += [row(N), pl.BlockSpec((1, N), lambda m: (0, 0)), row(N)]
        out_shape.append(jax.ShapeDtypeStruct((1, N), F32))
        out_specs.append(pl.BlockSpec((1, N), lambda m: (0, 0)))
    out = pl.pallas_call(body, name=name, grid=(T // tm,), in_specs=in_specs, out_specs=out_specs,
                         out_shape=out_shape, compiler_params=_params(("arbitrary",)))(*ins)
    return out if has_bwd else out[0]


def _mm_tn(x, dy, *, name, rows_are_shards=False, split_y=False, tt=1024):
    T, K = x.shape
    N = 2 * dy.shape[-1] if split_y else dy.shape[-1]
    R, C = (K // N_CHIPS, N // 2) if rows_are_shards else (K // 2, N // N_CHIPS)
    nn = 2 if split_y else 1
    tn = N // nn
    per = N_CHIPS // nn
    assert not (rows_are_shards and split_y)
    tt = _row_tile(T, tt)
    nt = T // tt

    def body(x_ref, y_ref, o_ref, acc_ref):
        t = pl.program_id(1)

        @pl.when(t == 0)
        def _():
            acc_ref[...] = jnp.zeros_like(acc_ref)

        acc_ref[...] += lax.dot_general(x_ref[...], y_ref[...].astype(BF16), (((0,), (0,)), ((), ())),
                                        preferred_element_type=F32)

        @pl.when(t == nt - 1)
        def _():
            if rows_are_shards:
                for h in range(2):
                    o_ref[h] = acc_ref[:, h * C:(h + 1) * C].astype(BF16).reshape(N_CHIPS, R, C)
            else:
                for j in range(per):
                    o_ref[:, j] = acc_ref[:, j * C:(j + 1) * C].astype(BF16).reshape(2, R, C)

    if split_y:
        yspec = pl.BlockSpec((None, tt, tn), lambda n, t: (n, t, 0))
    else:
        yspec = pl.BlockSpec((tt, tn), lambda n, t: (t, 0))
    if rows_are_shards:
        out_spec = pl.BlockSpec((2, N_CHIPS, R, C), lambda n, t: (0, 0, 0, 0))
    else:
        out_spec = pl.BlockSpec((2, per, R, C), lambda n, t: (0, n, 0, 0))
    return pl.pallas_call(body, name=name, grid=(nn, nt),
                          in_specs=[pl.BlockSpec((tt, K), lambda n, t: (t, 0)), yspec], out_specs=out_spec,
                          out_shape=jax.ShapeDtypeStruct((2, N_CHIPS, R, C), BF16),
                          scratch_shapes=[pltpu.VMEM((K, tn), F32)],
                          compiler_params=_params(("arbitrary", "arbitrary")))(x, dy)


def _qkv_fwd(h, wq, gq, wkv, gkv, scale, *, tm=512):
    T, D = h.shape
    HD = wq.shape[-1]
    tm = _row_tile(T, tm)

    def body(h_ref, wq_ref, gq_ref, wkv_ref, gkv_ref, q_ref, kv_ref, nq_ref, nkv_ref):
        n = _rms(h_ref[...])[0]
        nq = (n * gq_ref[...]).astype(BF16)
        nkv = (n * gkv_ref[...]).astype(BF16)
        nq_ref[...] = nq
        nkv_ref[...] = nkv
        q_ref[...] = (jnp.dot(nq, wq_ref[...], preferred_element_type=F32) * scale).astype(BF16)
        kv = jnp.dot(nkv, wkv_ref[...], preferred_element_type=F32)
        kv_ref[0] = kv[:, :HD].astype(BF16)
        kv_ref[1] = kv[:, HD:].astype(BF16)

    row = lambda width: pl.BlockSpec((tm, width), lambda i: (i, 0))
    fixed = lambda *shape: pl.BlockSpec(shape, lambda i: (0,) * len(shape))
    weight = lambda n: pl.BlockSpec((None, D, n), lambda i: (0, 0, 0), pipeline_mode=pl.Buffered(1))
    return pl.pallas_call(
        body, name="qkv", grid=(T // tm,),
        in_specs=[row(D), weight(HD), fixed(1, D), weight(2 * HD), fixed(1, D)],
        out_specs=[row(HD), pl.BlockSpec((2, tm, HD), lambda i: (0, i, 0)), row(D), row(D)],
        out_shape=[jax.ShapeDtypeStruct((T, HD), BF16), jax.ShapeDtypeStruct((2, T, HD), BF16),
                   jax.ShapeDtypeStruct((T, D), BF16), jax.ShapeDtypeStruct((T, D), BF16)],
        compiler_params=_params(("arbitrary",)))(h, wq, gq.reshape(1, D), wkv, gkv.reshape(1, D))


def _qkv_dx(dq, wq, gq, dkv, wkv, gkv, h, dh, *, tm=512):
    T, D = h.shape
    HD = wq.shape[-1]
    tm = _row_tile(T, tm)
    nt = (((1,), (1,)), ((), ()))

    def body(dq_ref, wq_ref, gq_ref, dkv_ref, wkv_ref, gkv_ref, h_ref, dh_ref, o_ref, dgq_ref, dgkv_ref):
        @pl.when(pl.program_id(0) == 0)
        def _():
            dgq_ref[...] = jnp.zeros_like(dgq_ref)
            dgkv_ref[...] = jnp.zeros_like(dgkv_ref)

        n, r = _rms(h_ref[...])
        dnq = lax.dot_general(dq_ref[...], wq_ref[...], nt, preferred_element_type=F32)
        dnkv = (lax.dot_general(dkv_ref[0], wkv_ref[:, :HD], nt, preferred_element_type=F32)
                + lax.dot_general(dkv_ref[1], wkv_ref[:, HD:], nt, preferred_element_type=F32))
        dgq_ref[...] += jnp.sum(dnq * n, axis=0, keepdims=True)
        dgkv_ref[...] += jnp.sum(dnkv * n, axis=0, keepdims=True)
        t = dnq * gq_ref[...] + dnkv * gkv_ref[...]
        o_ref[...] = dh_ref[...] + r * (t - n * jnp.mean(t * n, axis=-1, keepdims=True))

    row = lambda width: pl.BlockSpec((tm, width), lambda i: (i, 0))
    fixed = lambda *shape: pl.BlockSpec(shape, lambda i: (0,) * len(shape))
    weight = lambda n: pl.BlockSpec((None, D, n), lambda i: (0, 0, 0), pipeline_mode=pl.Buffered(1))
    return pl.pallas_call(
        body, name="qkv_dx", grid=(T // tm,),
        in_specs=[row(HD), weight(HD), fixed(1, D), pl.BlockSpec((2, tm, HD), lambda i: (0, i, 0)), weight(2 * HD),
                  fixed(1, D), row(D), row(D)],
        out_specs=[row(D), fixed(1, D), fixed(1, D)],
        out_shape=[jax.ShapeDtypeStruct((T, D), F32), jax.ShapeDtypeStruct((1, D), F32),
                   jax.ShapeDtypeStruct((1, D), F32)],
        compiler_params=_params(("arbitrary",)))(dq, wq, gq.reshape(1, D), dkv, wkv, gkv.reshape(1, D), h, dh)


def _chunk_mask():
    i = lax.broadcasted_iota(jnp.int32, (GMLP_BLOCK, GMLP_BLOCK), 0) // CHUNK
    j = lax.broadcasted_iota(jnp.int32, (GMLP_BLOCK, GMLP_BLOCK), 1) // CHUNK
    return i >= j


def _mixer_a_fwd(h, w_in, g, gv, ws, bs_tile, w_out, *, tm=256):
    T, D = h.shape
    W = w_out.shape[-2]
    G = W // GROUP_DIM
    tm = _row_tile(T, tm)

    def body(h_ref, wi_ref, g_ref, gv_ref, ws_ref, bs_ref, wo_ref, o_ref, zp_ref, ga_ref, n_ref):
        nb = (_rms(h_ref[...])[0] * g_ref[...]).astype(BF16)
        n_ref[...] = nb
        zpb = jnp.dot(nb, wi_ref[...], preferred_element_type=F32).astype(BF16)
        zp_ref[...] = zpb
        z = _gelu(zpb.astype(F32))
        u, v = z[:, :W], z[:, W:]
        vn = _rms(v)[0] * gv_ref[...]
        mask = _chunk_mask()
        for gi in range(G):
            cs = slice(gi * GROUP_DIM, (gi + 1) * GROUP_DIM)
            wg = jnp.where(mask, ws_ref[gi], 0.0).astype(BF16)
            for b in range(tm // GMLP_BLOCK):
                rs = slice(b * GMLP_BLOCK, (b + 1) * GMLP_BLOCK)
                s = jnp.dot(wg, vn[rs, cs].astype(BF16), preferred_element_type=F32) + bs_ref[:, cs]
                ga_ref[rs, cs] = (u[rs, cs] * s).astype(BF16)
        o_ref[...] = h_ref[...] + jnp.dot(ga_ref[...], wo_ref[...], preferred_element_type=F32)

    row = lambda width: pl.BlockSpec((tm, width), lambda i: (i, 0))
    fixed = lambda *shape: pl.BlockSpec(shape, lambda i: (0,) * len(shape))
    weight = lambda k, n: pl.BlockSpec((None, k, n), lambda i: (0, 0, 0), pipeline_mode=pl.Buffered(1))
    return pl.pallas_call(
        body, name="mixer_a", grid=(T // tm,),
        in_specs=[row(D), weight(D, 2 * W), fixed(1, D), fixed(1, W), fixed(G, GMLP_BLOCK, GMLP_BLOCK),
                  fixed(GMLP_BLOCK, W), weight(W, D)],
        out_specs=[row(D), row(2 * W), row(W), row(D)],
        out_shape=[jax.ShapeDtypeStruct((T, D), F32), jax.ShapeDtypeStruct((T, 2 * W), BF16),
                   jax.ShapeDtypeStruct((T, W), BF16), jax.ShapeDtypeStruct((T, D), BF16)],
        compiler_params=_params(("arbitrary",)))(h, w_in, g.reshape(1, D), gv, ws, bs_tile, w_out)


def _gate_bwd(zp, d_out, gv, ws, bs_tile, *, tm=256):
    T, W2 = zp.shape
    W = W2 // 2
    G = W // GROUP_DIM
    tm = _row_tile(T, tm)
    nm = T // tm

    def body(zp_ref, do_ref, gv_ref, ws_ref, bs_ref, dzp_ref, dws_ref, dbs_ref, dgv_ref, du_scr, dvn_scr, dsum_scr):
        i = pl.program_id(0)

        @pl.when(i == 0)
        def _():
            dws_ref[...] = jnp.zeros_like(dws_ref)
            dgv_ref[...] = jnp.zeros_like(dgv_ref)
            dsum_scr[...] = jnp.zeros_like(dsum_scr)

        zp = zp_ref[...].astype(F32)
        z, dz = _gelu(zp, with_grad=True)
        u, v = z[:, :W], z[:, W:]
        n, r = _rms(v)
        gv = gv_ref[...]
        vn = n * gv
        d_out = do_ref[...].astype(F32)
        mask = _chunk_mask()
        for g in range(G):
            cs = slice(g * GROUP_DIM, (g + 1) * GROUP_DIM)
            wg = jnp.where(mask, ws_ref[g], 0.0).astype(BF16)
            dw = jnp.zeros((GMLP_BLOCK, GMLP_BLOCK), F32)
            for b in range(tm // GMLP_BLOCK):
                rs = slice(b * GMLP_BLOCK, (b + 1) * GMLP_BLOCK)
                vb = vn[rs, cs].astype(BF16)
                s = jnp.dot(wg, vb, preferred_element_type=F32) + bs_ref[:, cs]
                du_scr[rs, cs] = d_out[rs, cs] * s
                ds = d_out[rs, cs] * u[rs, cs]
                dsb = ds.astype(BF16)
                dvn_scr[rs, cs] = lax.dot_general(wg, dsb, (((0,), (0,)), ((), ())), preferred_element_type=F32)
                dw = dw + lax.dot_general(dsb, vb, (((1,), (1,)), ((), ())), preferred_element_type=F32)
                dsum_scr[:, cs] += ds
            dws_ref[g] += jnp.where(mask, dw, 0.0)
        dvn = dvn_scr[...]
        dgv_ref[...] += jnp.sum(dvn * n, axis=0, keepdims=True)
        t = dvn * gv
        dv = r * (t - n * jnp.mean(t * n, axis=-1, keepdims=True))
        dzp_ref[:, :W] = (du_scr[...] * dz[:, :W]).astype(BF16)
        dzp_ref[:, W:] = (dv * dz[:, W:]).astype(BF16)

        @pl.when(i == nm - 1)
        def _():
            sel = (lax.broadcasted_iota(jnp.int32, (G, W), 1) // GROUP_DIM
                   == lax.broadcasted_iota(jnp.int32, (G, W), 0)).astype(F32)
            dbs_ref[...] = lax.dot_general(sel, dsum_scr[...], (((1,), (1,)), ((), ())),
                                           precision=lax.Precision.HIGHEST, preferred_element_type=F32)

    return pl.pallas_call(
        body, name="gate_bwd", grid=(nm,),
        in_specs=[pl.BlockSpec((tm, W2), lambda i: (i, 0)), pl.BlockSpec((tm, W), lambda i: (i, 0)),
                  pl.BlockSpec((1, W), lambda i: (0, 0)),
                  pl.BlockSpec((G, GMLP_BLOCK, GMLP_BLOCK), lambda i: (0, 0, 0)),
                  pl.BlockSpec((GMLP_BLOCK, W), lambda i: (0, 0))],
        out_specs=[pl.BlockSpec((tm, W2), lambda i: (i, 0)),
                   pl.BlockSpec((G, GMLP_BLOCK, GMLP_BLOCK), lambda i: (0, 0, 0)),
                   pl.BlockSpec((G, GMLP_BLOCK), lambda i: (0, 0)), pl.BlockSpec((1, W), lambda i: (0, 0))],
        out_shape=[jax.ShapeDtypeStruct((T, W2), BF16), jax.ShapeDtypeStruct((G, GMLP_BLOCK, GMLP_BLOCK), F32),
                   jax.ShapeDtypeStruct((G, GMLP_BLOCK), F32), jax.ShapeDtypeStruct((1, W), F32)],
        scratch_shapes=[pltpu.VMEM((tm, W), F32), pltpu.VMEM((tm, W), F32), pltpu.VMEM((GMLP_BLOCK, W), F32)],
        compiler_params=_params(("arbitrary",)))(zp, d_out, gv, ws, bs_tile)


LANES = 128
HALO = 16


def _taps(ext, w, b):
    return w[2:3] * ext[HALO:] + w[1:2] * pltpu.roll(ext, 1, 0)[HALO:] + w[0:1] * pltpu.roll(ext, 2, 0)[HALO:] + b


def _ffn_fwd(h, w, g, cw, cb, wd, S, *, name, loss=None, tm=256):
    T, D = h.shape
    F = w.shape[-1] // 2
    tc = _col_tile(F)
    tm = _row_tile(S, tm)
    has_loss = loss is not None

    def body(*refs):
        h_ref, w_ref, g_ref, cw_ref, cb_ref, wd_ref = refs[:6]
        o_ref, y_ref, a_ref, c_ref, n_ref = refs[8:13] if has_loss else refs[6:11]
        tail = refs[-1]
        first = (pl.program_id(0) * tm) % S == 0
        nb = (_rms(h_ref[...])[0] * g_ref[...]).astype(BF16)
        n_ref[...] = nb
        for j in range(F // tc):
            cs = slice(j * tc, (j + 1) * tc)
            conv = []
            for s in range(2):
                acc = jnp.dot(nb, w_ref[:, s * F + j * tc:s * F + (j + 1) * tc], preferred_element_type=F32)
                ab = acc.astype(BF16)
                a_ref[s, :, cs] = ab
                af = ab.astype(F32)
                ext = jnp.concatenate([jnp.where(first, 0.0, tail[s, :, cs]), af], axis=0)
                tail[s, :, cs] = af[tm - HALO:, :]
                cv = _taps(ext, cw_ref[s, :, cs], cb_ref[s:s + 1, cs]).astype(BF16)
                c_ref[s, :, cs] = cv
                conv.append(cv.astype(F32))
            up, gate = conv
            y_ref[:, cs] = (gate * jax.nn.sigmoid(gate) * up).astype(BF16)
        out = h_ref[...] + jnp.dot(y_ref[...], wd_ref[...], preferred_element_type=F32)
        if has_loss:
            _loss_epilogue(out, refs[6], refs[7], o_ref, refs[13], refs[14], pl.program_id(0) == 0)
        else:
            o_ref[...] = out

    row = lambda width: pl.BlockSpec((tm, width), lambda i: (i, 0))
    wide = pl.BlockSpec((2, tm, F), lambda i: (0, i, 0))
    fixed = lambda *shape: pl.BlockSpec(shape, lambda i: (0,) * len(shape))
    once = pl.Buffered(1)
    ins = [h, w, g.reshape(1, D), cw, cb, wd]
    in_specs = [row(D), pl.BlockSpec((None, D, 2 * F), lambda i: (0, 0, 0), pipeline_mode=once), fixed(1, D),
                fixed(2, 3, F), fixed(2, F), pl.BlockSpec((None, F, D), lambda i: (0, 0, 0), pipeline_mode=once)]
    out_specs = [row(D), row(F), wide, wide, row(D)]
    out_shape = [jax.ShapeDtypeStruct((T, D), F32), jax.ShapeDtypeStruct((T, F), BF16),
                 jax.ShapeDtypeStruct((2, T, F), BF16), jax.ShapeDtypeStruct((2, T, F), BF16),
                 jax.ShapeDtypeStruct((T, D), BF16)]
    if has_loss:
        ins += [loss[0].reshape(1, D), loss[1]]
        in_specs += [fixed(1, D), row(D)]
        out_specs += [fixed(8, 128), fixed(1, D)]
        out_shape += [jax.ShapeDtypeStruct((8, 128), F32), jax.ShapeDtypeStruct((1, D), F32)]
    return pl.pallas_call(body, name=name, grid=(T // tm,), in_specs=in_specs, out_specs=out_specs,
                          out_shape=out_shape, scratch_shapes=[pltpu.VMEM((2, HALO, F), F32)],
                          compiler_params=_params(("arbitrary",)))(*ins)


def _conv_bwd(a, c, dy, cw, S, *, tm=256):
    _, T, F = a.shape
    tc = _col_tile(F)
    tm = _row_tile(S, tm)
    nm = T // tm
    hb = tm // HALO
    TE = tm + HALO
    nxt = lambda j, i: jnp.minimum((i + 1) * hb, T // HALO - 1)

    def body(a_ref, c_ref, nc_ref, dy_ref, ndy_ref, w_ref, da_ref, dw_ref, db_ref):
        i = pl.program_id(1)
        last = ((i + 1) * tm) % S == 0
        keep_n = jnp.where(last, 0.0, 1.0)

        @pl.when(i == 0)
        def _():
            dw_ref[...] = jnp.zeros_like(dw_ref)
            db_ref[...] = jnp.zeros_like(db_ref)

        for j in range(tc // LANES):
            cs = slice(j * LANES, (j + 1) * LANES)
            dyf = jnp.concatenate([dy_ref[:, cs].astype(F32), ndy_ref[:, cs].astype(F32) * keep_n], axis=0)
            up = jnp.concatenate([c_ref[0, :, cs].astype(F32), nc_ref[0, :, cs].astype(F32)], axis=0)
            gate = jnp.concatenate([c_ref[1, :, cs].astype(F32), nc_ref[1, :, cs].astype(F32)], axis=0)
            sg = jax.nn.sigmoid(gate)
            for s, d in ((0, dyf * (gate * sg)), (1, dyf * up * (sg * (1.0 + gate * (1.0 - sg))))):
                a = a_ref[s, :, cs].astype(F32)
                w = w_ref[s, :, cs]
                u1, u2 = pltpu.roll(d, TE - 1, 0), pltpu.roll(d, TE - 2, 0)
                db_ref[s:s + 1, cs] += jnp.sum(d[:tm], axis=0, keepdims=True)
                dw_ref[s, 2:3, cs] += jnp.sum(d[:tm] * a, axis=0, keepdims=True)
                dw_ref[s, 1:2, cs] += jnp.sum(u1[:tm] * a, axis=0, keepdims=True)
                dw_ref[s, 0:1, cs] += jnp.sum(u2[:tm] * a, axis=0, keepdims=True)
                da_ref[s, :, cs] = (w[2:3] * d + w[1:2] * u1 + w[0:1] * u2)[:tm].astype(BF16)

    cur = pl.BlockSpec((2, tm, tc), lambda j, i: (0, i, j))
    return pl.pallas_call(
        body, name="conv_bwd", grid=(F // tc, nm),
        in_specs=[cur, cur, pl.BlockSpec((2, HALO, tc), lambda j, i: (0, nxt(j, i), j)),
                  pl.BlockSpec((tm, tc), lambda j, i: (i, j)), pl.BlockSpec((HALO, tc), lambda j, i: (nxt(j, i), j)),
                  pl.BlockSpec((2, 3, tc), lambda j, i: (0, 0, j))],
        out_specs=[cur, pl.BlockSpec((2, 3, tc), lambda j, i: (0, 0, j)), pl.BlockSpec((2, tc), lambda j, i: (0, j))],
        out_shape=[jax.ShapeDtypeStruct((2, T, F), BF16), jax.ShapeDtypeStruct((2, 3, F), F32),
                   jax.ShapeDtypeStruct((2, F), F32)],
        compiler_params=_params(("arbitrary", "arbitrary")))(a, c, c, dy, dy, cw)


def _bias_index():
    idx = np.arange(F_LEN)
    d = np.where(idx < K_SPAN, idx, idx - F_LEN)
    return np.clip(PAD - d, -REL_CLIP, REL_CLIP) + REL_CLIP


ROW_GROUP = 16


def _roll_rows(x, sign, unit, steps):
    rows = lax.broadcasted_iota(jnp.int32, x.shape, 0)
    step = 1
    while step < steps:
        shift = unit * step if sign > 0 else F_LEN - unit * step
        x = jnp.where((rows & step) != 0, pltpu.roll(x, shift, 1), x)
        step *= 2
    return x


def _bias_expand(frow):
    H = frow.shape[0]
    groups = Q_BLOCK // ROW_GROUP

    def body(f_ref, o_ref):
        coarse = _roll_rows(jnp.broadcast_to(f_ref[...], (groups, F_LEN)), 1, ROW_GROUP, groups)
        x = jnp.concatenate([jnp.broadcast_to(coarse[a:a + 1], (ROW_GROUP, F_LEN)) for a in range(groups)], axis=0)
        x = _roll_rows(x, 1, 1, ROW_GROUP)[:, :K_SPAN]
        qc = lax.broadcasted_iota(jnp.int32, (Q_BLOCK, K_SPAN), 0) // CHUNK * CHUNK
        kj = lax.broadcasted_iota(jnp.int32, (Q_BLOCK, K_SPAN), 1)
        o_ref[...] = jnp.where((kj >= qc) & (kj < qc + PAD + CHUNK), x, NEG_INF)

    return pl.pallas_call(
        body, name="bias_expand", grid=(H,),
        in_specs=[pl.BlockSpec((None, 1, F_LEN), lambda h: (h, 0, 0))],
        out_specs=pl.BlockSpec((None, Q_BLOCK, K_SPAN), lambda h: (h, 0, 0)),
        out_shape=jax.ShapeDtypeStruct((H, Q_BLOCK, K_SPAN), F32), compiler_params=_params(("arbitrary",)))(frow)


def _bias_reduce(dbias, n_rel):
    H = dbias.shape[0]
    onehot = jnp.asarray((_bias_index()[:, None] == np.arange(n_rel)[None, :]).astype(np.float32), dtype=BF16)

    def body(d_ref, oh_ref, o_ref):
        x = jnp.concatenate([d_ref[...], jnp.zeros((Q_BLOCK, F_LEN - K_SPAN), F32)], axis=1)
        fine = _roll_rows(x, -1, 1, ROW_GROUP).reshape(Q_BLOCK // ROW_GROUP, ROW_GROUP, F_LEN)
        coarse = _roll_rows(jnp.sum(fine, axis=1), -1, ROW_GROUP, Q_BLOCK // ROW_GROUP)
        row = jnp.broadcast_to(jnp.sum(coarse, axis=0, keepdims=True), (8, F_LEN))
        acc = jnp.zeros((8, n_rel), F32)
        for _ in range(3):
            piece = row.astype(BF16)
            acc = acc + jnp.dot(piece, oh_ref[...], preferred_element_type=F32)
            row = row - piece.astype(F32)
        o_ref[...] = acc[0:1]

    return pl.pallas_call(
        body, name="bias_reduce", grid=(H,),
        in_specs=[pl.BlockSpec((None, Q_BLOCK, K_SPAN), lambda h: (h, 0, 0)),
                  pl.BlockSpec((F_LEN, n_rel), lambda h: (0, 0))],
        out_specs=pl.BlockSpec((None, 1, n_rel), lambda h: (h, 0, 0)),
        out_shape=jax.ShapeDtypeStruct((H, 1, n_rel), F32), compiler_params=_params(("arbitrary",)))(dbias, onehot)


def _attn_specs(S):
    hw = HEADS_PER_STEP * HEAD_DIM
    qspec = pl.BlockSpec((None, Q_BLOCK, hw), lambda g, b, i: (b, i, g))
    kspec = pl.BlockSpec((None, None, S, hw), lambda g, b, i: (0, b, 0, g))
    vspec = pl.BlockSpec((None, None, S, hw), lambda g, b, i: (1, b, 0, g))
    bspec = pl.BlockSpec((HEADS_PER_STEP, Q_BLOCK, K_SPAN), lambda g, b, i: (g, 0, 0))
    return hw, qspec, kspec, vspec, bspec


def _span_cases(i, fn):
    short = PAD // Q_BLOCK
    for j in range(short):
        pl.when(i == j)(functools.partial(fn, PAD - j * Q_BLOCK))
    pl.when(i >= short)(functools.partial(fn, 0))


def _key_start(i, off):
    return 0 if off else pl.multiple_of(i * Q_BLOCK - PAD, Q_BLOCK)


def _attn_exp(q_ref, k_ref, b_ref, h, k0, off):
    hs = slice(h * HEAD_DIM, (h + 1) * HEAD_DIM)
    kh = k_ref[pl.ds(k0, K_SPAN - off), hs]
    s = lax.dot_general(q_ref[:, hs], kh, (((1,), (1,)), ((), ())), preferred_element_type=F32) + b_ref[h, :, off:]
    p = jnp.exp(s - jnp.max(s, axis=-1, keepdims=True))
    return p, 1.0 / jnp.sum(p, axis=-1, keepdims=True), kh


def _attn_fwd(q, kv, bias, B, S):
    HD = q.shape[-1]
    hw, qspec, kspec, vspec, bspec = _attn_specs(S)

    def body(q_ref, k_ref, v_ref, b_ref, o_ref):
        i = pl.program_id(2)

        def block(off):
            k0 = _key_start(i, off)
            outs = []
            for h in range(HEADS_PER_STEP):
                hs = slice(h * HEAD_DIM, (h + 1) * HEAD_DIM)
                p, inv, _ = _attn_exp(q_ref, k_ref, b_ref, h, k0, off)
                outs.append(jnp.dot(p.astype(BF16), v_ref[pl.ds(k0, K_SPAN - off), hs],
                                    preferred_element_type=F32) * inv)
            o_ref[...] = jnp.concatenate(outs, axis=1).astype(BF16)

        _span_cases(i, block)

    return pl.pallas_call(
        body, name="attn_fwd", grid=(HD // hw, B, S // Q_BLOCK), in_specs=[qspec, kspec, vspec, bspec],
        out_specs=qspec, out_shape=jax.ShapeDtypeStruct((B, S, HD), BF16),
        compiler_params=_params(("arbitrary", "arbitrary", "arbitrary")))(q, kv, kv, bias)


def _attn_bwd(q, kv, bias, do, B, S):
    HD = q.shape[-1]
    H = HD // HEAD_DIM
    hw, qspec, kspec, vspec, bspec = _attn_specs(S)
    scale = HEAD_DIM ** -0.5
    nq = S // Q_BLOCK

    def body(q_ref, k_ref, v_ref, b_ref, do_ref, dq_ref, dkv_ref, db_ref, dk_acc, dv_acc):
        b, i = pl.program_id(1), pl.program_id(2)

        @pl.when(i == 0)
        def _():
            dk_acc[...] = jnp.zeros_like(dk_acc)
            dv_acc[...] = jnp.zeros_like(dv_acc)

        @pl.when((i == 0) & (b == 0))
        def _():
            db_ref[...] = jnp.zeros_like(db_ref)

        def block(off):
            k0 = _key_start(i, off)
            keys = pl.ds(k0, K_SPAN - off)
            for h in range(HEADS_PER_STEP):
                hs = slice(h * HEAD_DIM, (h + 1) * HEAD_DIM)
                p, inv, kh = _attn_exp(q_ref, k_ref, b_ref, h, k0, off)
                p = p * inv
                doh = do_ref[:, hs]
                dp = lax.dot_general(doh, v_ref[keys, hs], (((1,), (1,)), ((), ())), preferred_element_type=F32)
                ds = p * (dp - jnp.sum(p * dp, axis=-1, keepdims=True))
                db_ref[h, :, off:] += ds
                dsb = ds.astype(BF16)
                dq_ref[:, hs] = (jnp.dot(dsb, kh, preferred_element_type=F32) * scale).astype(BF16)
                dk_acc[hs, keys] += lax.dot_general(q_ref[:, hs], dsb, (((0,), (0,)), ((), ())),
                                                     preferred_element_type=F32)
                dv_acc[hs, keys] += lax.dot_general(doh, p.astype(BF16), (((0,), (0,)), ((), ())),
                                                     preferred_element_type=F32)

        _span_cases(i, block)

        @pl.when(i == nq - 1)
        def _():
            dkv_ref[0] = dk_acc[...].T.astype(BF16)
            dkv_ref[1] = dv_acc[...].T.astype(BF16)

    return pl.pallas_call(
        body, name="attn_bwd", grid=(HD // hw, B, nq), in_specs=[qspec, kspec, vspec, bspec, qspec],
        out_specs=[qspec, pl.BlockSpec((2, None, S, hw), lambda g, b, i: (0, b, 0, g)), bspec],
        out_shape=[jax.ShapeDtypeStruct((B, S, HD), BF16), jax.ShapeDtypeStruct((2, B, S, HD), BF16),
                   jax.ShapeDtypeStruct((H, Q_BLOCK, K_SPAN), F32)],
        scratch_shapes=[pltpu.VMEM((hw, S), F32), pltpu.VMEM((hw, S), F32)],
        compiler_params=_params(("arbitrary", "arbitrary", "arbitrary")))(q, kv, kv, bias, do)


def _sub_rows(R):
    for cand in (256, 352, 128, 64, 8):
        if R % cand == 0 and R > cand:
            return cand
    return R


def _adamw(w, g, m, v, *, name):
    R, C = w.shape
    tr = _sub_rows(R)

    def body(w_ref, g_ref, m_ref, v_ref, d_ref, nm_ref, nv_ref):
        g = g_ref[...]
        m = ADAM_B1 * m_ref[...] + (1.0 - ADAM_B1) * g
        v = ADAM_B2 * v_ref[...] + (1.0 - ADAM_B2) * (g * g)
        m_hat = m / (1.0 - ADAM_B1 ** ADAM_STEP)
        v_hat = v / (1.0 - ADAM_B2 ** ADAM_STEP)
        d_ref[...] = -ADAM_LR * (m_hat / (jnp.sqrt(v_hat) + ADAM_EPS) + ADAM_WD * w_ref[...])
        nm_ref[...] = m
        nv_ref[...] = v

    spec = pl.BlockSpec((tr, C), lambda i: (i, 0))
    return pl.pallas_call(body, name=name, grid=(R // tr,), in_specs=[spec] * 4, out_specs=[spec] * 3,
                          out_shape=[jax.ShapeDtypeStruct((R, C), F32)] * 3,
                          compiler_params=_params(("arbitrary",)))(w, g, m, v)


def _add_pair(units, got, core, *, name):
    n4, R, C = got.shape
    rows = n4 * R
    tr = 512 if rows % 512 == 0 else R

    def body(c_ref, u_ref, got_ref, o_ref):
        o_ref[...] = (u_ref[...].astype(F32) + got_ref[...].astype(F32)).astype(BF16)

    spec = pl.BlockSpec((tr, C), lambda i, c: (i, 0))
    grid_spec = pltpu.PrefetchScalarGridSpec(
        num_scalar_prefetch=1, grid=(rows // tr,),
        in_specs=[pl.BlockSpec((None, tr, C), lambda i, c: (c[0], i, 0)), spec], out_specs=spec)
    out = pl.pallas_call(body, name=name, grid_spec=grid_spec, out_shape=jax.ShapeDtypeStruct((rows, C), BF16),
                         compiler_params=_params(("arbitrary",)))(core.reshape(1), units.reshape(2, rows, C),
                                                                   got.reshape(rows, C))
    return out.reshape(n4, R, C)


def _sum_chips(w, own, got, pos, *, name, layer=0, into=None):
    _, R, C = own.shape
    tr = _sub_rows(R)
    nr = R // tr

    def body(p_ref, own_ref, got_ref, *rest):
        o_ref = rest[-1]
        o_ref[...] = (own_ref[...].astype(F32) + got_ref[0].astype(F32) + got_ref[1].astype(F32)
                      + got_ref[2].astype(F32))

    if w.row_sharded:
        out_map = lambda i, p: (layer, i, p[1])
    else:
        out_map = lambda i, p: (layer, p[1] * nr + i, 0)
    ins = [pos, own, got]
    in_specs = [pl.BlockSpec((None, tr, C), lambda i, p: (p[0], i, 0)),
                pl.BlockSpec((3, tr, C), lambda i, p: (0, i, 0))]
    alias = {}
    if into is not None:
        ins.append(into)
        in_specs.append(ANY)
        alias = {3: 0}
    grid_spec = pltpu.PrefetchScalarGridSpec(num_scalar_prefetch=1, grid=(nr,), in_specs=in_specs,
                                             out_specs=pl.BlockSpec((None, tr, C), out_map))
    return pl.pallas_call(body, name=name, grid_spec=grid_spec, input_output_aliases=alias,
                          out_shape=jax.ShapeDtypeStruct((w.L, w.ks, w.ns), F32),
                          compiler_params=_params(("arbitrary",)))(*ins)


def _mesh_pos():
    return lax.axis_index("x"), lax.axis_index("y"), lax.axis_index("c")


def _other_chips(x, y):
    return [(1 - x, y), (x, 1 - y), (1 - x, 1 - y)]


ANY = pl.BlockSpec(memory_space=pl.ANY)


class _W:
    def __init__(self, name, shard, row_sharded, direct=False):
        self.name = name
        self.direct = direct
        self.L, ks, ns = shard.shape
        self.row_sharded = row_sharded
        self.K, self.N = (ks * N_CHIPS, ns) if row_sharded else (ks, ns * N_CHIPS)
        self.ks, self.ns = ks, ns

    def shard_of(self, full, j):
        if self.row_sharded:
            return full.at[:, pl.ds(j * self.ks, self.ks), :]
        return full.at[:, :, pl.ds(j * self.ns, self.ns)]

    def half_of(self, shard, c):
        if self.row_sharded:
            return shard.at[:, :, pl.ds(c * (self.ns // 2), self.ns // 2)]
        return shard.at[:, pl.ds(c * (self.ks // 2), self.ks // 2), :]


HBM = pl.BlockSpec(memory_space=pltpu.HBM)
SEM = pl.BlockSpec(memory_space=pltpu.SEMAPHORE)
IN_FLIGHT = pltpu.SideEffectType.DATAFLOW_SIDE_EFFECTING


def _in_hbm(a):
    return pltpu.with_memory_space_constraint(a, pltpu.HBM)


def _gather_start(ws, shards, after, *, name):
    nw = len(ws)

    def body(*refs):
        src, dst = refs[:nw], refs[nw:2 * nw]
        send, recv = refs[2 * nw + 1:3 * nw + 1], refs[3 * nw + 1:4 * nw + 1]
        x, y, c = _mesh_pos()
        me = 2 * x + y
        for i, w in enumerate(ws):
            for f, (px, py) in enumerate(_other_chips(x, y)):
                for e in range(2 if w.direct else 1):
                    k = 2 * f + e
                    pltpu.make_async_remote_copy(
                        src_ref=w.half_of(src[i], c), dst_ref=w.half_of(w.shard_of(dst[i], me), c),
                        send_sem=send[i].at[k], recv_sem=recv[i].at[k], device_id=(px, py, c if e == 0 else 1 - c),
                        device_id_type=MESH).start()

    fulls = [lax.empty((w.L, w.K, w.N), BF16) for w in ws]
    out = pl.pallas_call(
        body, name=name, in_specs=[HBM] * (2 * nw) + [ANY],
        out_specs=[SEM] * (2 * nw) + [HBM] * (2 * nw),
        out_shape=[pltpu.SemaphoreType.DMA((6,))] * (2 * nw)
        + [pltpu.HBM(s.shape, BF16) for s in shards] + [pltpu.HBM(f.shape, BF16) for f in fulls],
        input_output_aliases={i: 2 * nw + i for i in range(2 * nw)},
        compiler_params=pltpu.CompilerParams(has_side_effects=IN_FLIGHT))(
            *[_in_hbm(s) for s in shards], *[_in_hbm(f) for f in fulls], after)
    return [(out[i], out[nw + i], out[2 * nw + i], out[3 * nw + i]) for i in range(nw)]


def _gather_wait(ws, flight, after, *, name):
    nw = len(ws)

    def body(*refs):
        src, dst = refs[:nw], refs[nw:2 * nw]
        send, recv = refs[2 * nw:3 * nw], refs[3 * nw:4 * nw]
        x, y, c = _mesh_pos()
        for i, w in enumerate(ws):
            for f, (px, py) in enumerate(_other_chips(x, y)):
                for e in range(2 if w.direct else 1):
                    k = 2 * f + e
                    landed = w.half_of(w.shard_of(dst[i], 2 * px + py), c if e == 0 else 1 - c)
                    cp = pltpu.make_async_remote_copy(
                        src_ref=w.half_of(src[i], c), dst_ref=landed, send_sem=send[i].at[k], recv_sem=recv[i].at[k],
                        device_id=(px, py, c), device_id_type=MESH)
                    cp.wait_send()
                    cp.wait_recv()

    shards, fulls = [fl[2] for fl in flight], [fl[3] for fl in flight]
    out = pl.pallas_call(
        body, name=name, in_specs=[HBM] * (2 * nw) + [SEM] * (2 * nw) + [ANY],
        out_specs=[HBM] * (2 * nw),
        out_shape=[pltpu.HBM(s.shape, BF16) for s in shards] + [pltpu.HBM(f.shape, BF16) for f in fulls],
        input_output_aliases={i: i for i in range(2 * nw)},
        compiler_params=pltpu.CompilerParams(has_side_effects=IN_FLIGHT))(
            *shards, *fulls, *[fl[0] for fl in flight], *[fl[1] for fl in flight], after)
    return out[:nw], out[nw:]


def _gather_finish(ws, shards, fulls, *, name):
    nw = len(ws)
    forward = not ws[0].direct

    def body(*refs):
        src, dst, stage = refs[:nw], refs[3 * nw:4 * nw], refs[4 * nw:5 * nw]
        send_sems, recv_sems, load_sems, store_sems = refs[5 * nw:]
        x, y, c = _mesh_pos()
        me = 2 * x + y
        sibling = (x, y, 1 - c)
        chips = _other_chips(x, y)

        def fwd(i, w, f, half):
            px, py = chips[f]
            landed = w.half_of(w.shard_of(dst[i], 2 * px + py), half)
            return pltpu.make_async_remote_copy(src_ref=landed, dst_ref=landed, send_sem=send_sems.at[3 * i + f],
                                                recv_sem=recv_sems.at[3 * i + f], device_id=sibling,
                                                device_id_type=MESH)

        loads = [pltpu.make_async_copy(src[i], stage[i], load_sems.at[i]) for i in range(nw)]
        for cp in loads:
            cp.start()
        sends = [fwd(i, w, f, c) for i, w in enumerate(ws) for f in range(3)] if forward else []
        for cp in sends:
            cp.start()
        stores = [pltpu.make_async_copy(stage[i], w.shard_of(dst[i], me), store_sems.at[i])
                  for i, w in enumerate(ws)]
        for ld, st in zip(loads, stores):
            ld.wait()
            st.start()
        if forward:
            for i, w in enumerate(ws):
                for f in range(3):
                    fwd(i, w, f, 1 - c).wait_recv()
        for cp in sends:
            cp.wait_send()
        for cp in stores:
            cp.wait()

    out = pl.pallas_call(
        body, name=name, in_specs=[ANY] * (2 * nw), out_specs=[ANY] * (2 * nw),
        out_shape=[jax.ShapeDtypeStruct(s.shape, BF16) for s in shards]
        + [jax.ShapeDtypeStruct(f.shape, BF16) for f in fulls],
        input_output_aliases={i: i for i in range(2 * nw)},
        scratch_shapes=[pltpu.VMEM((w.L, w.ks, w.ns), BF16) for w in ws]
        + [pltpu.SemaphoreType.DMA((3 * nw,)), pltpu.SemaphoreType.DMA((3 * nw,)), pltpu.SemaphoreType.DMA((nw,)),
           pltpu.SemaphoreType.DMA((nw,))],
        compiler_params=_params(has_side_effects=True))(*shards, *fulls)
    return out[nw:]


def _split_copies(name, srcs, lands, n_sems, copies_of, *, flight=None, after=None):
    n = len(srcs)
    starting = flight is None

    def body(*refs):
        src, land = refs[:n], refs[n:2 * n]
        sems = refs[2 * n + 1:4 * n + 1] if starting else refs[2 * n:4 * n]
        for i in range(n):
            for cp in copies_of(i, src[i], land[i], sems[i], sems[n + i]):
                if starting:
                    cp.start()
                else:
                    cp.wait_send()
                    cp.wait_recv()

    thru = [pltpu.HBM(a.shape, a.dtype) for a in list(srcs) + list(lands)]
    if starting:
        out = pl.pallas_call(
            body, name=name, in_specs=[HBM] * (2 * n) + [ANY], out_specs=[SEM] * (2 * n) + [HBM] * (2 * n),
            out_shape=[pltpu.SemaphoreType.DMA((n_sems,))] * (2 * n) + thru,
            input_output_aliases={i: 2 * n + i for i in range(2 * n)},
            compiler_params=pltpu.CompilerParams(has_side_effects=IN_FLIGHT))(
                *[_in_hbm(a) for a in srcs], *[_in_hbm(a) for a in lands], after)
        return [(out[i], out[n + i], out[2 * n + i], out[3 * n + i]) for i in range(n)]
    out = pl.pallas_call(
        body, name=name, in_specs=[HBM] * (2 * n) + [SEM] * (2 * n) + [ANY], out_specs=[HBM] * (2 * n),
        out_shape=thru, input_output_aliases={i: i for i in range(2 * n)},
        compiler_params=pltpu.CompilerParams(has_side_effects=IN_FLIGHT))(
            *srcs, *lands, *[fl[0] for fl in flight], *[fl[1] for fl in flight], after)
    return out[:n], out[n:]


def _sum8(land, vec, me):
    R = vec.shape[0]

    def body(me_ref, land_ref, vec_ref, o_ref):
        acc = jnp.zeros((R, 128), F32)
        for d in range(8):
            acc = acc + jnp.where(me_ref[0] == d, vec_ref[...], land_ref[d])
        o_ref[...] = acc

    grid_spec = pltpu.PrefetchScalarGridSpec(
        num_scalar_prefetch=1, grid=(1,),
        in_specs=[pl.BlockSpec((8, R, 128), lambda i, m: (0, 0, 0)), pl.BlockSpec((R, 128), lambda i, m: (0, 0))],
        out_specs=pl.BlockSpec((R, 128), lambda i, m: (0, 0)))
    return pl.pallas_call(body, name="sum8", grid_spec=grid_spec, out_shape=jax.ShapeDtypeStruct((R, 128), F32),
                          compiler_params=_params(("arbitrary",)))(me.reshape(1), land, vec)


def _swap_copies(i, src, got, send, recv):
    x, y, c = _mesh_pos()
    return [pltpu.make_async_remote_copy(src_ref=src.at[1 - c], dst_ref=got, send_sem=send.at[0], recv_sem=recv.at[0],
                                         device_id=(x, y, 1 - c), device_id_type=MESH)]


def _gather8_copies(i, src, land, send, recv):
    x, y, c = _mesh_pos()
    me = 4 * x + 2 * y + c
    peers = [(x, y, 1 - c)] + [(px, py, pc) for px, py in _other_chips(x, y) for pc in (c, 1 - c)]
    return [pltpu.make_async_remote_copy(src_ref=src, dst_ref=land.at[me], send_sem=send.at[k], recv_sem=recv.at[k],
                                         device_id=peer, device_id_type=MESH) for k, peer in enumerate(peers)]


def _scatter_copy(src, got, send, recv, f, chip, c):
    px, py = chip
    return pltpu.make_async_remote_copy(src_ref=src.at[2 * px + py], dst_ref=got.at[f], send_sem=send.at[f],
                                        recv_sem=recv.at[f], device_id=(px, py, c), device_id_type=MESH)


def _scatter_start(sums, *, name):
    nw = len(sums)

    def body(*refs):
        src, got = refs[:nw], refs[nw:2 * nw]
        send, recv = refs[2 * nw:3 * nw], refs[3 * nw:4 * nw]
        x, y, c = _mesh_pos()
        for i in range(nw):
            for f, chip in enumerate(_other_chips(x, y)):
                _scatter_copy(src[i], got[i], send[i], recv[i], f, chip, c).start()

    lands = [lax.empty((3,) + s.shape[1:], BF16) for s in sums]
    out = pl.pallas_call(
        body, name=name, in_specs=[HBM] * (2 * nw), out_specs=[SEM] * (2 * nw) + [HBM] * (2 * nw),
        out_shape=[pltpu.SemaphoreType.DMA((3,))] * (2 * nw)
        + [pltpu.HBM(s.shape, BF16) for s in sums] + [pltpu.HBM(l.shape, BF16) for l in lands],
        input_output_aliases={i: 2 * nw + i for i in range(2 * nw)},
        compiler_params=pltpu.CompilerParams(has_side_effects=IN_FLIGHT))(
            *[_in_hbm(s) for s in sums], *[_in_hbm(l) for l in lands])
    return [(out[i], out[nw + i], out[2 * nw + i], out[3 * nw + i]) for i in range(nw)]


def _scatter_wait(flight, after):
    nw = len(flight)

    def body(*refs):
        src, got = refs[:nw], refs[nw:2 * nw]
        send, recv = refs[2 * nw:3 * nw], refs[3 * nw:4 * nw]
        x, y, c = _mesh_pos()
        for i in range(nw):
            for f, chip in enumerate(_other_chips(x, y)):
                cp = _scatter_copy(src[i], got[i], send[i], recv[i], f, chip, c)
                cp.wait_send()
                cp.wait_recv()

    sums, lands = [fl[2] for fl in flight], [fl[3] for fl in flight]
    out = pl.pallas_call(
        body, name="scatter_wait", in_specs=[HBM] * (2 * nw) + [SEM] * (2 * nw) + [ANY], out_specs=[HBM] * (2 * nw),
        out_shape=[pltpu.HBM(s.shape, BF16) for s in sums] + [pltpu.HBM(l.shape, BF16) for l in lands],
        input_output_aliases={i: i for i in range(2 * nw)},
        compiler_params=pltpu.CompilerParams(has_side_effects=IN_FLIGHT))(
            *sums, *lands, *[fl[0] for fl in flight], *[fl[1] for fl in flight], after)
    return out[:nw], out[nw:]


def _join_halves(ws, shards):
    nw = len(ws)

    def body(*refs):
        buf = refs[nw:2 * nw]
        send_sems, recv_sems = refs[2 * nw:]
        x, y, c = _mesh_pos()
        sibling = (x, y, 1 - c)

        def copy(i, w, half):
            region = w.half_of(buf[i], half)
            return pltpu.make_async_remote_copy(src_ref=region, dst_ref=region, send_sem=send_sems.at[i],
                                                recv_sem=recv_sems.at[i], device_id=sibling, device_id_type=MESH)

        sends = [copy(i, w, c) for i, w in enumerate(ws)]
        for cp in sends:
            cp.start()
        for i, w in enumerate(ws):
            copy(i, w, 1 - c).wait_recv()
        for cp in sends:
            cp.wait_send()

    return pl.pallas_call(
        body, name="join_halves", in_specs=[ANY] * nw, out_specs=[ANY] * nw,
        out_shape=[jax.ShapeDtypeStruct((w.L, w.ks, w.ns), F32) for w in ws],
        input_output_aliases={i: i for i in range(nw)},
        scratch_shapes=[pltpu.SemaphoreType.DMA((nw,)), pltpu.SemaphoreType.DMA((nw,))],
        compiler_params=_params(has_side_effects=True))(*shards)


def _allreduce_small(vec):
    R = vec.shape[0]

    def body(x_ref, o_ref, buf, send_sems, recv_sems):
        x, y, c = _mesh_pos()
        me, sibling = (x, y, c), (x, y, 1 - c)
        chips = _other_chips(x, y)

        def slot(px, py, pc):
            return buf.at[4 * px + 2 * py + pc]

        def copy(k, block, to, src=None):
            return pltpu.make_async_remote_copy(src_ref=slot(*block) if src is None else src, dst_ref=slot(*block),
                                                send_sem=send_sems.at[k], recv_sem=recv_sems.at[k], device_id=to,
                                                device_id_type=MESH)

        first = [copy(0, me, sibling, src=x_ref)] + [copy(1 + f, me, (*chip, c), src=x_ref)
                                                     for f, chip in enumerate(chips)]
        for cp in first:
            cp.start()
        passed = [copy(4 + f, (*chip, c), sibling) for f, chip in enumerate(chips)]
        for f, chip in enumerate(chips):
            copy(1 + f, (*chip, c), me).wait_recv()
            passed[f].start()
        copy(0, sibling, me).wait_recv()
        for f, chip in enumerate(chips):
            copy(4 + f, (*chip, 1 - c), me).wait_recv()
        for cp in first + passed:
            cp.wait_send()
        slot(*me)[...] = x_ref[...]
        acc = buf[0]
        for d in range(1, 8):
            acc = acc + buf[d]
        o_ref[...] = acc

    return pl.pallas_call(
        body, name="allreduce_small", in_specs=[pl.BlockSpec(memory_space=pltpu.VMEM)],
        out_specs=pl.BlockSpec(memory_space=pltpu.VMEM), out_shape=jax.ShapeDtypeStruct((R, 128), F32),
        scratch_shapes=[pltpu.VMEM((8, R, 128), F32), pltpu.SemaphoreType.DMA((7,)), pltpu.SemaphoreType.DMA((7,))],
        compiler_params=_params())(vec)


def _pack(parts):
    flat = jnp.concatenate([p.reshape(-1).astype(F32) for p in parts])
    n = flat.shape[0]
    pad = (-n) % (64 * 128)
    return jnp.pad(flat, (0, pad)).reshape(-1, 128)


def _unpack(vec, shapes):
    flat = vec.reshape(-1)
    out, off = [], 0
    for s in shapes:
        n = int(np.prod(s))
        out.append(flat[off:off + n].reshape(s))
        off += n
    return out


def kernel(x, a_norm_g, a_w_in, a_v_norm_g, a_w_s, a_b_s, a_w_out, kv_norm_g, w_kv, b_norm_g, b_w_q, b_rel_bias, b_w_o, f_norm_g, f_w_in, f_conv_w, f_conv_b, f_w_down, final_norm_g, loss_target, m_a_norm_g, m_a_w_in, m_a_v_norm_g, m_a_w_s, m_a_b_s, m_a_w_out, m_kv_norm_g, m_w_kv, m_b_norm_g, m_b_w_q, m_b_rel_bias, m_b_w_o, m_f_norm_g, m_f_w_in, m_f_conv_w, m_f_conv_b, m_f_w_down, m_final_norm_g, v_a_norm_g, v_a_w_in, v_a_v_norm_g, v_a_w_s, v_a_b_s, v_a_w_out, v_kv_norm_g, v_w_kv, v_b_norm_g, v_b_w_q, v_b_rel_bias, v_b_w_o, v_f_norm_g, v_f_w_in, v_f_conv_w, v_f_conv_b, v_f_w_down, v_final_norm_g):
    B, S, D = x.shape
    T = B * S
    xi, yi, ci = lax.axis_index("x"), lax.axis_index("y"), lax.axis_index("c")
    j_me = (2 * xi + yi).astype(jnp.int32)
    core = ci.astype(jnp.int32)
    pos = jnp.stack([j_me, core])

    w_shards = {"a_w_in": (a_w_in, False), "a_w_out": (a_w_out, True), "w_kv": (w_kv[None], False),
                "b_w_q": (b_w_q, True), "b_w_o": (b_w_o, True), "f_w_in": (f_w_in, False), "f_w_down": (f_w_down, True)}
    names = list(w_shards)
    ws = [_W(n, w_shards[n][0], w_shards[n][1]) for n in names]
    g_shards = {"a_w_in": (a_w_in, False), "a_w_out": (a_w_out, True),
                "f_w_in0": (f_w_in[0:1], False), "f_w_down0": (f_w_down[0:1], True),
                "w_kv": (w_kv[None], False), "b_w_q": (b_w_q, True), "b_w_o": (b_w_o, True),
                "f_w_in1": (f_w_in[1:2], False), "f_w_down1": (f_w_down[1:2], True)}
    g_names = list(g_shards)
    g_ws = {n: _W(n, *g_shards[n], direct=n in ("w_kv", "b_w_q", "b_w_o", "f_w_in1", "f_w_down1")) for n in g_names}

    Wd = a_w_in.shape[1]
    GW = a_v_norm_g.shape[1] * N_CHIPS
    F2 = f_conv_w.shape[2] * N_CHIPS
    Fh = F2 // 2
    nsd, nsg, nsf = a_norm_g.shape[1], a_v_norm_g.shape[1], f_conv_w.shape[2]
    own = (ci == 0).astype(F32)
    place = lambda sh, width, n: lax.dynamic_update_slice_in_dim(
        jnp.zeros(sh.shape[:-1] + (width,), F32), sh * own, j_me * n, axis=sh.ndim - 1)
    def tied(x, flight):
        x, thru = lax.optimization_barrier((x, flight[0][2]))
        return x, [flight[0][:2] + (thru,) + flight[0][3:]] + flight[1:]

    gathered = _allreduce_small(_pack([place(a_norm_g, Wd, nsd), place(a_v_norm_g, GW, nsg),
                                       place(f_conv_w, F2, nsf)]))
    a_g, a_vg, conv_w = _unpack(gathered, [(1, Wd), (1, GW), (2, 3, F2)])
    first, rest = g_names[:4], g_names[4:]
    flight = dict(zip(first, _gather_start([g_ws[n] for n in first], [g_shards[n][0].astype(BF16) for n in first],
                                           gathered, name="gather_start_first")))
    (fi, fd, kv_w, qw, ow), (flight[first[0]],) = tied((f_w_in, f_w_down, w_kv, b_w_q, b_w_o), [flight[first[0]]])
    late = {"w_kv": kv_w[None], "b_w_q": qw, "b_w_o": ow, "f_w_in1": fi[1:2], "f_w_down1": fd[1:2]}
    flight.update(zip(rest, _gather_start([g_ws[n] for n in rest], [late[n].astype(BF16) for n in rest], kv_w,
                                          name="gather_start_rest")))
    full = {}

    def arrive(group, after, tag):
        gw = [g_ws[n] for n in group]
        sh, fu = _gather_wait(gw, [flight[n] for n in group], after, name=f"gather_wait_{tag}")
        full.update(zip(group, _gather_finish(gw, sh, fu, name=f"gather_finish_{tag}")))
    conv_w2 = conv_w.reshape(2, 3, 2, Fh).transpose(0, 2, 1, 3)
    conv_b2 = f_conv_b.reshape(2, 2, Fh)

    h0 = x.reshape(T, D)
    target = loss_target.reshape(T, D)
    bs_tile = jnp.repeat(a_b_s[0].T, GROUP_DIM, axis=1)
    ws_a = a_w_s[0]
    scale = HEAD_DIM ** -0.5
    HD = b_w_q.shape[2]
    H = HD // HEAD_DIM
    n_rel = b_rel_bias.shape[-1]
    frow, (flight["w_kv"],) = tied(b_rel_bias[0][:, _bias_index()].reshape(H, 1, F_LEN), [flight["w_kv"]])
    bias = _bias_expand(frow)

    def ffn_fwd(h, l, loss=None):
        out = _ffn_fwd(h, full[f"f_w_in{l}"], f_norm_g[l], conv_w2[l], conv_b2[l], full[f"f_w_down{l}"], S,
                       loss=loss, name=f"ffn{l}")
        yff, a, c, n = out[1:5]
        return (out[0] if loss is None else (out[0], out[5], out[6])), (a, c, n, yff)

    arrive(["a_w_in", "a_w_out"], bias, "a")
    h1, zp, out_a, n_a = _mixer_a_fwd(h0, full["a_w_in"], a_g[0], a_vg, ws_a, bs_tile, full["a_w_out"])
    arrive(["f_w_in0", "f_w_down0"], h1, "f0")
    h2, saved0 = ffn_fwd(h1, 0)
    arrive(["w_kv", "b_w_q", "b_w_o"], h2, "b")
    arrive(["f_w_in1", "f_w_down1"], h2, "f1")
    q, kv, n_q, n_kv = _qkv_fwd(h2, full["b_w_q"], b_norm_g[0], full["w_kv"], kv_norm_g, scale)
    kv4, q3 = kv.reshape(2, B, S, HD), q.reshape(B, S, HD)
    o = _attn_fwd(q3, kv4, bias, B, S).reshape(T, HD)
    h3 = _mm(o, full["b_w_o"], res=h2, name="attn_out")
    (dh, loss8, dg_final), saved1 = ffn_fwd(h3, 1, loss=(final_norm_g, target))

    units = {}

    in_flight = {}

    def swap_start(group, tag, carry):
        us = [units[n] for n in group]
        lands = [lax.empty(u.shape[1:], BF16) for u in us]
        carry, flight = tied(carry, _split_copies(f"swap_start_{tag}", us, lands, 1, _swap_copies, after=carry))
        return (group, tag, flight), carry

    def reduce_start(swap, after):
        group, tag, flight = swap
        us, got = _split_copies(f"swap_wait_{tag}", [fl[2] for fl in flight], [fl[3] for fl in flight], 1,
                                _swap_copies, flight=flight, after=after)
        sums = [_add_pair(u, g_, core, name=f"pair_{n}") for n, u, g_ in zip(group, us, got)]
        after, flight = tied(after, _scatter_start(sums, name=f"scatter_start_{tag}"))
        in_flight.update(zip(group, flight))
        return after

    def ffn_bwd(dh, h, saved, l, early):
        a, c, n, yff = saved
        units[f"f_w_down{l}"] = _mm_tn(yff, dh, rows_are_shards=True, name=f"ffn{l}_down_dw")
        dh_in = dh
        if early:
            sw, dh_in = swap_start([f"f_w_down{l}"], f"fd{l}", dh)
        dyff = _mm(dh_in, full[f"f_w_down{l}"], trans_w=True, out_dtype=BF16, name=f"ffn{l}_down_dx")
        if early:
            dyff = reduce_start(sw, dyff)
        da, dcw, dcb = _conv_bwd(a, c, dyff, conv_w2[l], S)
        units[f"f_w_in{l}"] = _mm_tn(n, da, split_y=True, name=f"ffn{l}_in_dw")
        sw, da = swap_start([f"f_w_in{l}"] if early else [f"f_w_down{l}", f"f_w_in{l}"], f"f{l}", da)
        dh, dg = _mm(da, full[f"f_w_in{l}"], trans_w=True, split_x=True, bwd=(h, f_norm_g[l], dh),
                     name=f"ffn{l}_in_dx")
        return reduce_start(sw, dh), dg, dcw, dcb

    dh, dg_f1, dcw1, dcb1 = ffn_bwd(dh, h3, saved1, 1, False)
    do = _mm(dh, full["b_w_o"], trans_w=True, out_dtype=BF16, name="attn_out_dx")
    units["b_w_o"] = _mm_tn(o, dh, rows_are_shards=True, name="b_w_o_dw")
    dq, dkv, dbias = _attn_bwd(q3, kv4, bias, do.reshape(B, S, HD), B, S)
    dq, d_rel = lax.optimization_barrier((dq, _bias_reduce(dbias, n_rel)))
    d_rel = d_rel.reshape(1, H, n_rel)
    dq, dkv = dq.reshape(T, HD), dkv.reshape(2, T, HD)
    units["b_w_q"] = _mm_tn(n_q, dq, rows_are_shards=True, name="b_w_q_dw")
    units["w_kv"] = _mm_tn(n_kv, dkv, split_y=True, name="w_kv_dw")
    sw, dkv = swap_start(["b_w_o", "b_w_q", "w_kv"], "b", dkv)
    dh, dg_b, dg_kv = _qkv_dx(dq, full["b_w_q"], b_norm_g[0], dkv, full["w_kv"], kv_norm_g, h2, dh)
    dh = reduce_start(sw, dh)
    dh, dg_f0, dcw0, dcb0 = ffn_bwd(dh, h1, saved0, 0, True)
    units["a_w_out"] = _mm_tn(out_a, dh, rows_are_shards=True, name="a_w_out_dw")
    sw, dh_in = swap_start(["a_w_out"], "ao", dh)
    d_out = _mm(dh_in, full["a_w_out"], trans_w=True, out_dtype=BF16, name="a_out_dx")
    d_out = reduce_start(sw, d_out)
    dzp, dws, dbs, dgv = _gate_bwd(zp, d_out, a_vg, ws_a, bs_tile)
    units["a_w_in"] = _mm_tn(n_a, dzp, name="a_w_in_dw")
    sw, dzp_in = swap_start(["a_w_in"], "ai", dzp)
    dzp_in = reduce_start(sw, dzp_in)
    grad_x, dg_a = _mm(dzp_in, full["a_w_in"], trans_w=True, bwd=(h0, a_g[0], dh), name="a_in_dx")

    to_flat = lambda d: d.transpose(1, 0, 2).reshape(3, F2)
    small = {"a_norm_g": dg_a, "a_v_norm_g": dgv, "a_w_s": dws[None], "a_b_s": dbs[None], "kv_norm_g": dg_kv[0],
             "b_norm_g": dg_b, "b_rel_bias": d_rel, "f_norm_g": jnp.concatenate([dg_f0, dg_f1], axis=0),
             "f_conv_w": jnp.stack([to_flat(dcw0), to_flat(dcw1)]),
             "f_conv_b": jnp.stack([dcb0.reshape(F2), dcb1.reshape(F2)]), "final_norm_g": dg_final[0]}
    snames = list(small)
    small_vec = _pack([small[n] for n in snames] + [loss8[0:1, 0:1]])
    grad_x, small_flight = tied(grad_x, _split_copies("small_start", [small_vec],
                                                      [lax.empty((8,) + small_vec.shape, F32)], 7, _gather8_copies,
                                                      after=grad_x))

    sums, recv = _scatter_wait([in_flight[n] for n in g_names], grad_x)
    sums, recv = dict(zip(g_names, sums)), dict(zip(g_names, recv))
    halves = []
    for n, w in zip(names, ws):
        if w.L == 1:
            halves.append(_sum_chips(w, sums[n], recv[n], pos, name=f"chips_{n}"))
        else:
            first = _sum_chips(w, sums[n + "0"], recv[n + "0"], pos, name=f"chips_{n}0")
            halves.append(_sum_chips(w, sums[n + "1"], recv[n + "1"], pos, layer=1, into=first, name=f"chips_{n}1"))
    g_big = dict(zip(names, _join_halves(ws, halves)))
    g_big["w_kv"] = g_big["w_kv"][0]

    given = dict(a_norm_g=(a_norm_g, m_a_norm_g, v_a_norm_g), a_w_in=(a_w_in, m_a_w_in, v_a_w_in),
                 a_v_norm_g=(a_v_norm_g, m_a_v_norm_g, v_a_v_norm_g), a_w_s=(a_w_s, m_a_w_s, v_a_w_s),
                 a_b_s=(a_b_s, m_a_b_s, v_a_b_s), a_w_out=(a_w_out, m_a_w_out, v_a_w_out),
                 kv_norm_g=(kv_norm_g, m_kv_norm_g, v_kv_norm_g), w_kv=(w_kv, m_w_kv, v_w_kv),
                 b_norm_g=(b_norm_g, m_b_norm_g, v_b_norm_g), b_w_q=(b_w_q, m_b_w_q, v_b_w_q),
                 b_rel_bias=(b_rel_bias, m_b_rel_bias, v_b_rel_bias), b_w_o=(b_w_o, m_b_w_o, v_b_w_o),
                 f_norm_g=(f_norm_g, m_f_norm_g, v_f_norm_g), f_w_in=(f_w_in, m_f_w_in, v_f_w_in),
                 f_conv_w=(f_conv_w, m_f_conv_w, v_f_conv_w), f_conv_b=(f_conv_b, m_f_conv_b, v_f_conv_b),
                 f_w_down=(f_w_down, m_f_w_down, v_f_w_down), final_norm_g=(final_norm_g, m_final_norm_g, v_final_norm_g))
    order = list(given)
    grads, deltas, new_m, new_v = {}, {}, {}, {}
    for n in names:
        w_, m_, v_ = given[n]
        g_ = g_big[n]
        C = w_.shape[-1]
        d2, m2, v2 = _adamw(w_.reshape(-1, C), g_.reshape(-1, C), m_.reshape(-1, C), v_.reshape(-1, C),
                            name=f"adamw_{n}")
        grads[n], deltas[n], new_m[n], new_v[n] = g_.reshape(w_.shape), d2.reshape(w_.shape), m2.reshape(w_.shape), \
            v2.reshape(w_.shape)
    vecs, lands = _split_copies("small_wait", [small_flight[0][2]], [small_flight[0][3]], 7, _gather8_copies,
                                flight=small_flight, after=deltas[names[-1]])
    red = _sum8(lands[0], vecs[0], (4 * xi + 2 * yi + ci).astype(jnp.int32))
    parts = _unpack(red, [small[n].shape for n in snames] + [(1,)])
    g_small = dict(zip(snames, parts[:-1]))
    loss = parts[-1][0]
    g_small["a_norm_g"] = lax.dynamic_slice_in_dim(g_small["a_norm_g"], j_me * nsd, nsd, axis=1)
    g_small["a_v_norm_g"] = lax.dynamic_slice_in_dim(g_small["a_v_norm_g"], j_me * nsg, nsg, axis=1)
    g_small["f_conv_w"] = lax.dynamic_slice_in_dim(g_small["f_conv_w"], j_me * nsf, nsf, axis=2)

    sm = [n for n in order if n not in names]
    d2, m2, v2 = _adamw(_pack([given[n][0] for n in sm]), _pack([g_small[n].reshape(given[n][0].shape) for n in sm]),
                        _pack([given[n][1] for n in sm]), _pack([given[n][2] for n in sm]), name="adamw_small")
    shapes = [given[n][0].shape for n in sm]
    for n, d_, m_, v_ in zip(sm, _unpack(d2, shapes), _unpack(m2, shapes), _unpack(v2, shapes)):
        grads[n], deltas[n], new_m[n], new_v[n] = g_small[n].reshape(given[n][0].shape), d_, m_, v_

    return (loss, grad_x.reshape(B, S, D), *[grads[n] for n in order], *[deltas[n] for n in order],
            *[new_m[n] for n in order], *[new_v[n] for n in order])
```

```python
import functools
import math

import numpy as np
import jax
import jax.numpy as jnp
from jax import lax
from jax.experimental import pallas as pl
from jax.experimental.pallas import tpu as pltpu

F32 = jnp.float32
BF16 = jnp.bfloat16
MESH = pl.DeviceIdType.MESH

EPS = 1e-6
NEG_INF = -1e30
CHUNK = 64
GMLP_BLOCK = 128
GROUP_DIM = 128
HEAD_DIM = 64
LEFT_CHUNKS = 8
PAD = LEFT_CHUNKS * CHUNK
REL_CLIP = 128
Q_BLOCK = 256
K_SPAN = PAD + Q_BLOCK
F_LEN = K_SPAN + Q_BLOCK
HEADS_PER_STEP = 4
N_CHIPS = 4

ADAM_LR = 0.001
ADAM_B1 = 0.9
ADAM_B2 = 0.999
ADAM_EPS = 1e-08
ADAM_WD = 0.01
ADAM_STEP = 10

VMEM_LIMIT = 56 * 1024 * 1024


def _params(sem=None, **kw):
    if sem is not None:
        kw["dimension_semantics"] = sem
    return pltpu.CompilerParams(vmem_limit_bytes=VMEM_LIMIT, **kw)


def _rms(xf):
    r = lax.rsqrt(jnp.mean(xf * xf, axis=-1, keepdims=True) + EPS)
    return xf * r, r


def _gelu(x, with_grad=False):
    c = math.sqrt(2.0 / math.pi)
    x2 = x * x
    t = jnp.tanh(c * x * (1.0 + 0.044715 * x2))
    half = 0.5 * (1.0 + t)
    if not with_grad:
        return x * half
    return x * half, half + 0.5 * x * (1.0 - t * t) * c * (1.0 + 3.0 * 0.044715 * x2)


def _col_tile(n):
    if n <= 1024:
        return n
    for t in (1408, 1024, 512):
        if n % t == 0:
            return t
    raise ValueError(n)


def _row_tile(t, want):
    while t % want:
        want //= 2
    return want


def _loss_epilogue(h, g_ref, t_ref, dh_ref, loss_ref, dg_ref, first):
    @pl.when(first)
    def _():
        loss_ref[...] = jnp.zeros_like(loss_ref)
        dg_ref[...] = jnp.zeros_like(dg_ref)

    n, r = _rms(h)
    g = g_ref[...]
    e = n * g - t_ref[...]
    loss_ref[...] += 0.5 * jnp.sum(jnp.mean(e * e, axis=-1, keepdims=True), axis=0, keepdims=True)
    dy = e * (1.0 / h.shape[-1])
    dg_ref[...] += jnp.sum(dy * n, axis=0, keepdims=True)
    t = dy * g
    dh_ref[...] = r * (t - n * jnp.mean(t * n, axis=-1, keepdims=True))


def _mm(x, w, *, name, trans_w=False, res=None, out_dtype=F32, bwd=None, split_x=False, tm=512):
    T = x.shape[-2]
    K = 2 * x.shape[-1] if split_x else x.shape[-1]
    N = w.shape[-2] if trans_w else w.shape[-1]
    tm = _row_tile(T, 2 * tm if max(K, N) <= 2048 else tm)
    has_res, has_bwd = res is not None, bwd is not None
    dims = (((1,), (1,)), ((), ())) if trans_w else (((1,), (0,)), ((), ()))

    def body(*refs):
        it = iter(refs)
        x_ref, w_ref = next(it), next(it)
        res_ref = next(it) if has_res else None
        if has_bwd:
            h_ref, bg_ref, dh_ref = next(it), next(it), next(it)
        o_ref = next(it)
        if split_x:
            kh = K // 2
            acc = lax.dot_general(x_ref[0].astype(BF16), w_ref[:, :kh] if trans_w else w_ref[:kh, :], dims,
                                  preferred_element_type=F32)
            acc = acc + lax.dot_general(x_ref[1].astype(BF16), w_ref[:, kh:] if trans_w else w_ref[kh:, :], dims,
                                        preferred_element_type=F32)
        else:
            acc = lax.dot_general(x_ref[...].astype(BF16), w_ref[...], dims, preferred_element_type=F32)
        if has_res:
            acc = acc + res_ref[...]
        if has_bwd:
            dg_ref = next(it)
            n, r = _rms(h_ref[...])

            @pl.when(pl.program_id(0) == 0)
            def _():
                dg_ref[...] = jnp.zeros_like(dg_ref)

            dg_ref[...] += jnp.sum(acc * n, axis=0, keepdims=True)
            t = acc * bg_ref[...]
            o_ref[...] = dh_ref[...] + r * (t - n * jnp.mean(t * n, axis=-1, keepdims=True))
        else:
            o_ref[...] = acc.astype(out_dtype)

    row = lambda width: pl.BlockSpec((tm, width), lambda m: (m, 0))
    ins = [x, w]
    in_specs = [pl.BlockSpec((2, tm, K // 2), lambda m: (0, m, 0)) if split_x else row(K),
                pl.BlockSpec((None,) + w.shape[1:], lambda m: (0, 0, 0), pipeline_mode=pl.Buffered(1))]
    if has_res:
        ins.append(res)
        in_specs.append(row(N))
    out_shape = [jax.ShapeDtypeStruct((T, N), F32 if has_bwd else out_dtype)]
    out_specs = [row(N)]
    if has_bwd:
        h, g, dh = bwd
        ins += [h, g.reshape(1, N), dh]
        in_specs += [row(N), pl.BlockSpec((1, N), lambda m: (0, 0)), row(N)]
        out_shape.append(jax.ShapeDtypeStruct((1, N), F32))
        out_specs.append(pl.BlockSpec((1, N), lambda m: (0, 0)))
    out = pl.pallas_call(body, name=name, grid=(T // tm,), in_specs=in_specs, out_specs=out_specs,
                         out_shape=out_shape, compiler_params=_params(("arbitrary",)))(*ins)
    return out if has_bwd else out[0]


def _mm_tn(x, dy, *, name, rows_are_shards=False, split_y=False, tt=1024):
    T, K = x.shape
    N = 2 * dy.shape[-1] if split_y else dy.shape[-1]
    R, C = (K // N_CHIPS, N // 2) if rows_are_shards else (K // 2, N // N_CHIPS)
    nn = 2 if split_y else 1
    tn = N // nn
    per = N_CHIPS // nn
    assert not (rows_are_shards and split_y)
    tt = _row_tile(T, tt)
    nt = T // tt

    def body(x_ref, y_ref, o_ref, acc_ref):
        t = pl.program_id(1)

        @pl.when(t == 0)
        def _():
            acc_ref[...] = jnp.zeros_like(acc_ref)

        acc_ref[...] += lax.dot_general(x_ref[...], y_ref[...].astype(BF16), (((0,), (0,)), ((), ())),
                                        preferred_element_type=F32)

        @pl.when(t == nt - 1)
        def _():
            if rows_are_shards:
                for h in range(2):
                    o_ref[h] = acc_ref[:, h * C:(h + 1) * C].astype(BF16).reshape(N_CHIPS, R, C)
            else:
                for j in range(per):
                    o_ref[:, j] = acc_ref[:, j * C:(j + 1) * C].astype(BF16).reshape(2, R, C)

    if split_y:
        yspec = pl.BlockSpec((None, tt, tn), lambda n, t: (n, t, 0))
    else:
        yspec = pl.BlockSpec((tt, tn), lambda n, t: (t, 0))
    if rows_are_shards:
        out_spec = pl.BlockSpec((2, N_CHIPS, R, C), lambda n, t: (0, 0, 0, 0))
    else:
        out_spec = pl.BlockSpec((2, per, R, C), lambda n, t: (0, n, 0, 0))
    return pl.pallas_call(body, name=name, grid=(nn, nt),
                          in_specs=[pl.BlockSpec((tt, K), lambda n, t: (t, 0)), yspec], out_specs=out_spec,
                          out_shape=jax.ShapeDtypeStruct((2, N_CHIPS, R, C), BF16),
                          scratch_shapes=[pltpu.VMEM((K, tn), F32)],
                          compiler_params=_params(("arbitrary", "arbitrary")))(x, dy)


def _qkv_fwd(h, wq, gq, wkv, gkv, scale, *, tm=512):
    T, D = h.shape
    HD = wq.shape[-1]
    tm = _row_tile(T, tm)

    def body(h_ref, wq_ref, gq_ref, wkv_ref, gkv_ref, q_ref, kv_ref, nq_ref, nkv_ref):
        n = _rms(h_ref[...])[0]
        nq = (n * gq_ref[...]).astype(BF16)
        nkv = (n * gkv_ref[...]).astype(BF16)
        nq_ref[...] = nq
        nkv_ref[...] = nkv
        q_ref[...] = (jnp.dot(nq, wq_ref[...], preferred_element_type=F32) * scale).astype(BF16)
        kv = jnp.dot(nkv, wkv_ref[...], preferred_element_type=F32)
        kv_ref[0] = kv[:, :HD].astype(BF16)
        kv_ref[1] = kv[:, HD:].astype(BF16)

    row = lambda width: pl.BlockSpec((tm, width), lambda i: (i, 0))
    fixed = lambda *shape: pl.BlockSpec(shape, lambda i: (0,) * len(shape))
    weight = lambda n: pl.BlockSpec((None, D, n), lambda i: (0, 0, 0), pipeline_mode=pl.Buffered(1))
    return pl.pallas_call(
        body, name="qkv", grid=(T // tm,),
        in_specs=[row(D), weight(HD), fixed(1, D), weight(2 * HD), fixed(1, D)],
        out_specs=[row(HD), pl.BlockSpec((2, tm, HD), lambda i: (0, i, 0)), row(D), row(D)],
        out_shape=[jax.ShapeDtypeStruct((T, HD), BF16), jax.ShapeDtypeStruct((2, T, HD), BF16),
                   jax.ShapeDtypeStruct((T, D), BF16), jax.ShapeDtypeStruct((T, D), BF16)],
        compiler_params=_params(("arbitrary",)))(h, wq, gq.reshape(1, D), wkv, gkv.reshape(1, D))


def _qkv_dx(dq, wq, gq, dkv, wkv, gkv, h, dh, *, tm=512):
    T, D = h.shape
    HD = wq.shape[-1]
    tm = _row_tile(T, tm)
    nt = (((1,), (1,)), ((), ()))

    def body(dq_ref, wq_ref, gq_ref, dkv_ref, wkv_ref, gkv_ref, h_ref, dh_ref, o_ref, dgq_ref, dgkv_ref):
        @pl.when(pl.program_id(0) == 0)
        def _():
            dgq_ref[...] = jnp.zeros_like(dgq_ref)
            dgkv_ref[...] = jnp.zeros_like(dgkv_ref)

        n, r = _rms(h_ref[...])
        dnq = lax.dot_general(dq_ref[...], wq_ref[...], nt, preferred_element_type=F32)
        dnkv = (lax.dot_general(dkv_ref[0], wkv_ref[:, :HD], nt, preferred_element_type=F32)
                + lax.dot_general(dkv_ref[1], wkv_ref[:, HD:], nt, preferred_element_type=F32))
        dgq_ref[...] += jnp.sum(dnq * n, axis=0, keepdims=True)
        dgkv_ref[...] += jnp.sum(dnkv * n, axis=0, keepdims=True)
        t = dnq * gq_ref[...] + dnkv * gkv_ref[...]
        o_ref[...] = dh_ref[...] + r * (t - n * jnp.mean(t * n, axis=-1, keepdims=True))

    row = lambda width: pl.BlockSpec((tm, width), lambda i: (i, 0))
    fixed = lambda *shape: pl.BlockSpec(shape, lambda i: (0,) * len(shape))
    weight = lambda n: pl.BlockSpec((None, D, n), lambda i: (0, 0, 0), pipeline_mode=pl.Buffered(1))
    return pl.pallas_call(
        body, name="qkv_dx", grid=(T // tm,),
        in_specs=[row(HD), weight(HD), fixed(1, D), pl.BlockSpec((2, tm, HD), lambda i: (0, i, 0)), weight(2 * HD),
                  fixed(1, D), row(D), row(D)],
        out_specs=[row(D), fixed(1, D), fixed(1, D)],
        out_shape=[jax.ShapeDtypeStruct((T, D), F32), jax.ShapeDtypeStruct((1, D), F32),
                   jax.ShapeDtypeStruct((1, D), F32)],
        compiler_params=_params(("arbitrary",)))(dq, wq, gq.reshape(1, D), dkv, wkv, gkv.reshape(1, D), h, dh)


def _chunk_mask():
    i = lax.broadcasted_iota(jnp.int32, (GMLP_BLOCK, GMLP_BLOCK), 0) // CHUNK
    j = lax.broadcasted_iota(jnp.int32, (GMLP_BLOCK, GMLP_BLOCK), 1) // CHUNK
    return i >= j


def _mixer_a_fwd(h, w_in, g, gv, ws, bs_tile, w_out, *, tm=256):
    T, D = h.shape
    W = w_out.shape[-2]
    G = W // GROUP_DIM
    tm = _row_tile(T, tm)

    def body(h_ref, wi_ref, g_ref, gv_ref, ws_ref, bs_ref, wo_ref, o_ref, zp_ref, ga_ref, n_ref):
        nb = (_rms(h_ref[...])[0] * g_ref[...]).astype(BF16)
        n_ref[...] = nb
        zpb = jnp.dot(nb, wi_ref[...], preferred_element_type=F32).astype(BF16)
        zp_ref[...] = zpb
        z = _gelu(zpb.astype(F32))
        u, v = z[:, :W], z[:, W:]
        vn = _rms(v)[0] * gv_ref[...]
        mask = _chunk_mask()
        for gi in range(G):
            cs = slice(gi * GROUP_DIM, (gi + 1) * GROUP_DIM)
            wg = jnp.where(mask, ws_ref[gi], 0.0).astype(BF16)
            for b in range(tm // GMLP_BLOCK):
                rs = slice(b * GMLP_BLOCK, (b + 1) * GMLP_BLOCK)
                s = jnp.dot(wg, vn[rs, cs].astype(BF16), preferred_element_type=F32) + bs_ref[:, cs]
                ga_ref[rs, cs] = (u[rs, cs] * s).astype(BF16)
        o_ref[...] = h_ref[...] + jnp.dot(ga_ref[...], wo_ref[...], preferred_element_type=F32)

    row = lambda width: pl.BlockSpec((tm, width), lambda i: (i, 0))
    fixed = lambda *shape: pl.BlockSpec(shape, lambda i: (0,) * len(shape))
    weight = lambda k, n: pl.BlockSpec((None, k, n), lambda i: (0, 0, 0), pipeline_mode=pl.Buffered(1))
    return pl.pallas_call(
        body, name="mixer_a", grid=(T // tm,),
        in_specs=[row(D), weight(D, 2 * W), fixed(1, D), fixed(1, W), fixed(G, GMLP_BLOCK, GMLP_BLOCK),
                  fixed(GMLP_BLOCK, W), weight(W, D)],
        out_specs=[row(D), row(2 * W), row(W), row(D)],
        out_shape=[jax.ShapeDtypeStruct((T, D), F32), jax.ShapeDtypeStruct((T, 2 * W), BF16),
                   jax.ShapeDtypeStruct((T, W), BF16), jax.ShapeDtypeStruct((T, D), BF16)],
        compiler_params=_params(("arbitrary",)))(h, w_in, g.reshape(1, D), gv, ws, bs_tile, w_out)


def _gate_bwd(zp, d_out, gv, ws, bs_tile, *, tm=256):
    T, W2 = zp.shape
    W = W2 // 2
    G = W // GROUP_DIM
    tm = _row_tile(T, tm)
    nm = T // tm

    def body(zp_ref, do_ref, gv_ref, ws_ref, bs_ref, dzp_ref, dws_ref, dbs_ref, dgv_ref, du_scr, dvn_scr, dsum_scr):
        i = pl.program_id(0)

        @pl.when(i == 0)
        def _():
            dws_ref[...] = jnp.zeros_like(dws_ref)
            dgv_ref[...] = jnp.zeros_like(dgv_ref)
            dsum_scr[...] = jnp.zeros_like(dsum_scr)

        zp = zp_ref[...].astype(F32)
        z, dz = _gelu(zp, with_grad=True)
        u, v = z[:, :W], z[:, W:]
        n, r = _rms(v)
        gv = gv_ref[...]
        vn = n * gv
        d_out = do_ref[...].astype(F32)
        mask = _chunk_mask()
        for g in range(G):
            cs = slice(g * GROUP_DIM, (g + 1) * GROUP_DIM)
            wg = jnp.where(mask, ws_ref[g], 0.0).astype(BF16)
            dw = jnp.zeros((GMLP_BLOCK, GMLP_BLOCK), F32)
            for b in range(tm // GMLP_BLOCK):
                rs = slice(b * GMLP_BLOCK, (b + 1) * GMLP_BLOCK)
                vb = vn[rs, cs].astype(BF16)
                s = jnp.dot(wg, vb, preferred_element_type=F32) + bs_ref[:, cs]
                du_scr[rs, cs] = d_out[rs, cs] * s
                ds = d_out[rs, cs] * u[rs, cs]
                dsb = ds.astype(BF16)
                dvn_scr[rs, cs] = lax.dot_general(wg, dsb, (((0,), (0,)), ((), ())), preferred_element_type=F32)
                dw = dw + lax.dot_general(dsb, vb, (((1,), (1,)), ((), ())), preferred_element_type=F32)
                dsum_scr[:, cs] += ds
            dws_ref[g] += jnp.where(mask, dw, 0.0)
        dvn = dvn_scr[...]
        dgv_ref[...] += jnp.sum(dvn * n, axis=0, keepdims=True)
        t = dvn * gv
        dv = r * (t - n * jnp.mean(t * n, axis=-1, keepdims=True))
        dzp_ref[:, :W] = (du_scr[...] * dz[:, :W]).astype(BF16)
        dzp_ref[:, W:] = (dv * dz[:, W:]).astype(BF16)

        @pl.when(i == nm - 1)
        def _():
            sel = (lax.broadcasted_iota(jnp.int32, (G, W), 1) // GROUP_DIM
                   == lax.broadcasted_iota(jnp.int32, (G, W), 0)).astype(F32)
            dbs_ref[...] = lax.dot_general(sel, dsum_scr[...], (((1,), (1,)), ((), ())),
                                           precision=lax.Precision.HIGHEST, preferred_element_type=F32)

    return pl.pallas_call(
        body, name="gate_bwd", grid=(nm,),
        in_specs=[pl.BlockSpec((tm, W2), lambda i: (i, 0)), pl.BlockSpec((tm, W), lambda i: (i, 0)),
                  pl.BlockSpec((1, W), lambda i: (0, 0)),
                  pl.BlockSpec((G, GMLP_BLOCK, GMLP_BLOCK), lambda i: (0, 0, 0)),
                  pl.BlockSpec((GMLP_BLOCK, W), lambda i: (0, 0))],
        out_specs=[pl.BlockSpec((tm, W2), lambda i: (i, 0)),
                   pl.BlockSpec((G, GMLP_BLOCK, GMLP_BLOCK), lambda i: (0, 0, 0)),
                   pl.BlockSpec((G, GMLP_BLOCK), lambda i: (0, 0)), pl.BlockSpec((1, W), lambda i: (0, 0))],
        out_shape=[jax.ShapeDtypeStruct((T, W2), BF16), jax.ShapeDtypeStruct((G, GMLP_BLOCK, GMLP_BLOCK), F32),
                   jax.ShapeDtypeStruct((G, GMLP_BLOCK), F32), jax.ShapeDtypeStruct((1, W), F32)],
        scratch_shapes=[pltpu.VMEM((tm, W), F32), pltpu.VMEM((tm, W), F32), pltpu.VMEM((GMLP_BLOCK, W), F32)],
        compiler_params=_params(("arbitrary",)))(zp, d_out, gv, ws, bs_tile)


LANES = 128
HALO = 16


def _taps(ext, w, b):
    return w[2:3] * ext[HALO:] + w[1:2] * pltpu.roll(ext, 1, 0)[HALO:] + w[0:1] * pltpu.roll(ext, 2, 0)[HALO:] + b


def _ffn_fwd(h, w, g, cw, cb, wd, S, *, name, loss=None, tm=256):
    T, D = h.shape
    F = w.shape[-1] // 2
    tc = _col_tile(F)
    tm = _row_tile(S, tm)
    has_loss, has_down = loss is not None, wd is not None
    n_in = 5 + has_down + 2 * has_loss

    def body(*refs):
        h_ref, w_ref, g_ref, cw_ref, cb_ref = refs[:5]
        outs, tail = refs[n_in:-1], refs[-1]
        y_ref, a_ref, c_ref, n_ref = outs[has_down:has_down + 4]
        first = (pl.program_id(0) * tm) % S == 0
        nb = (_rms(h_ref[...])[0] * g_ref[...]).astype(BF16)
        n_ref[...] = nb
        for j in range(F // tc):
            cs = slice(j * tc, (j + 1) * tc)
            conv = []
            for s in range(2):
                acc = jnp.dot(nb, w_ref[:, s * F + j * tc:s * F + (j + 1) * tc], preferred_element_type=F32)
                ab = acc.astype(BF16)
                a_ref[s, :, cs] = ab
                af = ab.astype(F32)
                ext = jnp.concatenate([jnp.where(first, 0.0, tail[s, :, cs]), af], axis=0)
                tail[s, :, cs] = af[tm - HALO:, :]
                cv = _taps(ext, cw_ref[s, :, cs], cb_ref[s:s + 1, cs]).astype(BF16)
                c_ref[s, :, cs] = cv
                conv.append(cv.astype(F32))
            up, gate = conv
            y_ref[:, cs] = (gate * jax.nn.sigmoid(gate) * up).astype(BF16)
        if has_down:
            out = h_ref[...] + jnp.dot(y_ref[...], refs[5][...], preferred_element_type=F32)
            if has_loss:
                _loss_epilogue(out, refs[6], refs[7], outs[0], outs[5], outs[6], pl.program_id(0) == 0)
            else:
                outs[0][...] = out

    row = lambda width: pl.BlockSpec((tm, width), lambda i: (i, 0))
    wide = pl.BlockSpec((2, tm, F), lambda i: (0, i, 0))
    fixed = lambda *shape: pl.BlockSpec(shape, lambda i: (0,) * len(shape))
    once = pl.Buffered(1)
    ins = [h, w, g.reshape(1, D), cw, cb]
    in_specs = [row(D), pl.BlockSpec((None, D, 2 * F), lambda i: (0, 0, 0), pipeline_mode=once), fixed(1, D),
                fixed(2, 3, F), fixed(2, F)]
    out_specs = [row(F), wide, wide, row(D)]
    out_shape = [jax.ShapeDtypeStruct((T, F), BF16), jax.ShapeDtypeStruct((2, T, F), BF16),
                 jax.ShapeDtypeStruct((2, T, F), BF16), jax.ShapeDtypeStruct((T, D), BF16)]
    if has_down:
        ins.append(wd)
        in_specs.append(pl.BlockSpec((None, F, D), lambda i: (0, 0, 0), pipeline_mode=once))
        out_specs.insert(0, row(D))
        out_shape.insert(0, jax.ShapeDtypeStruct((T, D), F32))
    if has_loss:
        ins += [loss[0].reshape(1, D), loss[1]]
        in_specs += [fixed(1, D), row(D)]
        out_specs += [fixed(8, 128), fixed(1, D)]
        out_shape += [jax.ShapeDtypeStruct((8, 128), F32), jax.ShapeDtypeStruct((1, D), F32)]
    return pl.pallas_call(body, name=name, grid=(T // tm,), in_specs=in_specs, out_specs=out_specs,
                          out_shape=out_shape, scratch_shapes=[pltpu.VMEM((2, HALO, F), F32)],
                          compiler_params=_params(("arbitrary",)))(*ins)


def _conv_bwd(a, c, dy, cw, S, *, tm=256):
    _, T, F = a.shape
    tc = _col_tile(F)
    tm = _row_tile(S, tm)
    nm = T // tm
    hb = tm // HALO
    TE = tm + HALO
    nxt = lambda j, i: jnp.minimum((i + 1) * hb, T // HALO - 1)

    def body(a_ref, c_ref, nc_ref, dy_ref, ndy_ref, w_ref, da_ref, dw_ref, db_ref):
        i = pl.program_id(1)
        last = ((i + 1) * tm) % S == 0
        keep_n = jnp.where(last, 0.0, 1.0)

        @pl.when(i == 0)
        def _():
            dw_ref[...] = jnp.zeros_like(dw_ref)
            db_ref[...] = jnp.zeros_like(db_ref)

        for j in range(tc // LANES):
            cs = slice(j * LANES, (j + 1) * LANES)
            dyf = jnp.concatenate([dy_ref[:, cs].astype(F32), ndy_ref[:, cs].astype(F32) * keep_n], axis=0)
            up = jnp.concatenate([c_ref[0, :, cs].astype(F32), nc_ref[0, :, cs].astype(F32)], axis=0)
            gate = jnp.concatenate([c_ref[1, :, cs].astype(F32), nc_ref[1, :, cs].astype(F32)], axis=0)
            sg = jax.nn.sigmoid(gate)
            for s, d in ((0, dyf * (gate * sg)), (1, dyf * up * (sg * (1.0 + gate * (1.0 - sg))))):
                a = a_ref[s, :, cs].astype(F32)
                w = w_ref[s, :, cs]
                u1, u2 = pltpu.roll(d, TE - 1, 0), pltpu.roll(d, TE - 2, 0)
                db_ref[s:s + 1, cs] += jnp.sum(d[:tm], axis=0, keepdims=True)
                dw_ref[s, 2:3, cs] += jnp.sum(d[:tm] * a, axis=0, keepdims=True)
                dw_ref[s, 1:2, cs] += jnp.sum(u1[:tm] * a, axis=0, keepdims=True)
                dw_ref[s, 0:1, cs] += jnp.sum(u2[:tm] * a, axis=0, keepdims=True)
                da_ref[s, :, cs] = (w[2:3] * d + w[1:2] * u1 + w[0:1] * u2)[:tm].astype(BF16)

    cur = pl.BlockSpec((2, tm, tc), lambda j, i: (0, i, j))
    return pl.pallas_call(
        body, name="conv_bwd", grid=(F // tc, nm),
        in_specs=[cur, cur, pl.BlockSpec((2, HALO, tc), lambda j, i: (0, nxt(j, i), j)),
                  pl.BlockSpec((tm, tc), lambda j, i: (i, j)), pl.BlockSpec((HALO, tc), lambda j, i: (nxt(j, i), j)),
                  pl.BlockSpec((2, 3, tc), lambda j, i: (0, 0, j))],
        out_specs=[cur, pl.BlockSpec((2, 3, tc), lambda j, i: (0, 0, j)), pl.BlockSpec((2, tc), lambda j, i: (0, j))],
        out_shape=[jax.ShapeDtypeStruct((2, T, F), BF16), jax.ShapeDtypeStruct((2, 3, F), F32),
                   jax.ShapeDtypeStruct((2, F), F32)],
        compiler_params=_params(("arbitrary", "arbitrary")))(a, c, c, dy, dy, cw)


def _bias_index():
    idx = np.arange(F_LEN)
    d = np.where(idx < K_SPAN, idx, idx - F_LEN)
    return np.clip(PAD - d, -REL_CLIP, REL_CLIP) + REL_CLIP


ROW_GROUP = 16


def _roll_rows(x, sign, unit, steps):
    rows = lax.broadcasted_iota(jnp.int32, x.shape, 0)
    step = 1
    while step < steps:
        shift = unit * step if sign > 0 else F_LEN - unit * step
        x = jnp.where((rows & step) != 0, pltpu.roll(x, shift, 1), x)
        step *= 2
    return x


def _bias_expand(frow):
    H = frow.shape[0]
    groups = Q_BLOCK // ROW_GROUP

    def body(f_ref, o_ref):
        coarse = _roll_rows(jnp.broadcast_to(f_ref[...], (groups, F_LEN)), 1, ROW_GROUP, groups)
        x = jnp.concatenate([jnp.broadcast_to(coarse[a:a + 1], (ROW_GROUP, F_LEN)) for a in range(groups)], axis=0)
        x = _roll_rows(x, 1, 1, ROW_GROUP)[:, :K_SPAN]
        qc = lax.broadcasted_iota(jnp.int32, (Q_BLOCK, K_SPAN), 0) // CHUNK * CHUNK
        kj = lax.broadcasted_iota(jnp.int32, (Q_BLOCK, K_SPAN), 1)
        o_ref[...] = jnp.where((kj >= qc) & (kj < qc + PAD + CHUNK), x, NEG_INF)

    return pl.pallas_call(
        body, name="bias_expand", grid=(H,),
        in_specs=[pl.BlockSpec((None, 1, F_LEN), lambda h: (h, 0, 0))],
        out_specs=pl.BlockSpec((None, Q_BLOCK, K_SPAN), lambda h: (h, 0, 0)),
        out_shape=jax.ShapeDtypeStruct((H, Q_BLOCK, K_SPAN), F32), compiler_params=_params(("arbitrary",)))(frow)


def _bias_reduce(dbias, n_rel):
    H = dbias.shape[0]
    onehot = jnp.asarray((_bias_index()[:, None] == np.arange(n_rel)[None, :]).astype(np.float32), dtype=BF16)

    def body(d_ref, oh_ref, o_ref):
        x = jnp.concatenate([d_ref[...], jnp.zeros((Q_BLOCK, F_LEN - K_SPAN), F32)], axis=1)
        fine = _roll_rows(x, -1, 1, ROW_GROUP).reshape(Q_BLOCK // ROW_GROUP, ROW_GROUP, F_LEN)
        coarse = _roll_rows(jnp.sum(fine, axis=1), -1, ROW_GROUP, Q_BLOCK // ROW_GROUP)
        row = jnp.broadcast_to(jnp.sum(coarse, axis=0, keepdims=True), (8, F_LEN))
        acc = jnp.zeros((8, n_rel), F32)
        for _ in range(3):
            piece = row.astype(BF16)
            acc = acc + jnp.dot(piece, oh_ref[...], preferred_element_type=F32)
            row = row - piece.astype(F32)
        o_ref[...] = acc[0:1]

    return pl.pallas_call(
        body, name="bias_reduce", grid=(H,),
        in_specs=[pl.BlockSpec((None, Q_BLOCK, K_SPAN), lambda h: (h, 0, 0)),
                  pl.BlockSpec((F_LEN, n_rel), lambda h: (0, 0))],
        out_specs=pl.BlockSpec((None, 1, n_rel), lambda h: (h, 0, 0)),
        out_shape=jax.ShapeDtypeStruct((H, 1, n_rel), F32), compiler_params=_params(("arbitrary",)))(dbias, onehot)


def _attn_specs(S):
    hw = HEADS_PER_STEP * HEAD_DIM
    qspec = pl.BlockSpec((None, Q_BLOCK, hw), lambda g, b, i: (b, i, g))
    kspec = pl.BlockSpec((None, None, S, hw), lambda g, b, i: (0, b, 0, g))
    vspec = pl.BlockSpec((None, None, S, hw), lambda g, b, i: (1, b, 0, g))
    bspec = pl.BlockSpec((HEADS_PER_STEP, Q_BLOCK, K_SPAN), lambda g, b, i: (g, 0, 0))
    return hw, qspec, kspec, vspec, bspec


def _span_cases(i, fn):
    short = PAD // Q_BLOCK
    for j in range(short):
        pl.when(i == j)(functools.partial(fn, PAD - j * Q_BLOCK))
    pl.when(i >= short)(functools.partial(fn, 0))


def _key_start(i, off):
    return 0 if off else pl.multiple_of(i * Q_BLOCK - PAD, Q_BLOCK)


def _attn_exp(q_ref, k_ref, b_ref, h, k0, off):
    hs = slice(h * HEAD_DIM, (h + 1) * HEAD_DIM)
    kh = k_ref[pl.ds(k0, K_SPAN - off), hs]
    s = lax.dot_general(q_ref[:, hs], kh, (((1,), (1,)), ((), ())), preferred_element_type=F32) + b_ref[h, :, off:]
    p = jnp.exp(s - jnp.max(s, axis=-1, keepdims=True))
    return p, 1.0 / jnp.sum(p, axis=-1, keepdims=True), kh


def _attn_fwd(q, kv, bias, B, S):
    HD = q.shape[-1]
    hw, qspec, kspec, vspec, bspec = _attn_specs(S)

    def body(q_ref, k_ref, v_ref, b_ref, o_ref):
        i = pl.program_id(2)

        def block(off):
            k0 = _key_start(i, off)
            outs = []
            for h in range(HEADS_PER_STEP):
                hs = slice(h * HEAD_DIM, (h + 1) * HEAD_DIM)
                p, inv, _ = _attn_exp(q_ref, k_ref, b_ref, h, k0, off)
                outs.append(jnp.dot(p.astype(BF16), v_ref[pl.ds(k0, K_SPAN - off), hs],
                                    preferred_element_type=F32) * inv)
            o_ref[...] = jnp.concatenate(outs, axis=1).astype(BF16)

        _span_cases(i, block)

    return pl.pallas_call(
        body, name="attn_fwd", grid=(HD // hw, B, S // Q_BLOCK), in_specs=[qspec, kspec, vspec, bspec],
        out_specs=qspec, out_shape=jax.ShapeDtypeStruct((B, S, HD), BF16),
        compiler_params=_params(("arbitrary", "arbitrary", "arbitrary")))(q, kv, kv, bias)


def _attn_bwd(q, kv, bias, do, B, S):
    HD = q.shape[-1]
    H = HD // HEAD_DIM
    hw, qspec, kspec, vspec, bspec = _attn_specs(S)
    scale = HEAD_DIM ** -0.5
    nq = S // Q_BLOCK

    def body(q_ref, k_ref, v_ref, b_ref, do_ref, dq_ref, dkv_ref, db_ref, dk_acc, dv_acc):
        b, i = pl.program_id(1), pl.program_id(2)

        @pl.when(i == 0)
        def _():
            dk_acc[...] = jnp.zeros_like(dk_acc)
            dv_acc[...] = jnp.zeros_like(dv_acc)

        @pl.when((i == 0) & (b == 0))
        def _():
            db_ref[...] = jnp.zeros_like(db_ref)

        def block(off):
            k0 = _key_start(i, off)
            keys = pl.ds(k0, K_SPAN - off)
            for h in range(HEADS_PER_STEP):
                hs = slice(h * HEAD_DIM, (h + 1) * HEAD_DIM)
                p, inv, kh = _attn_exp(q_ref, k_ref, b_ref, h, k0, off)
                p = p * inv
                doh = do_ref[:, hs]
                dp = lax.dot_general(doh, v_ref[keys, hs], (((1,), (1,)), ((), ())), preferred_element_type=F32)
                ds = p * (dp - jnp.sum(p * dp, axis=-1, keepdims=True))
                db_ref[h, :, off:] += ds
                dsb = ds.astype(BF16)
                dq_ref[:, hs] = (jnp.dot(dsb, kh, preferred_element_type=F32) * scale).astype(BF16)
                dk_acc[hs, keys] += lax.dot_general(q_ref[:, hs], dsb, (((0,), (0,)), ((), ())),
                                                     preferred_element_type=F32)
                dv_acc[hs, keys] += lax.dot_general(doh, p.astype(BF16), (((0,), (0,)), ((), ())),
                                                     preferred_element_type=F32)

        _span_cases(i, block)

        @pl.when(i == nq - 1)
        def _():
            dkv_ref[0] = dk_acc[...].T.astype(BF16)
            dkv_ref[1] = dv_acc[...].T.astype(BF16)

    return pl.pallas_call(
        body, name="attn_bwd", grid=(HD // hw, B, nq), in_specs=[qspec, kspec, vspec, bspec, qspec],
        out_specs=[qspec, pl.BlockSpec((2, None, S, hw), lambda g, b, i: (0, b, 0, g)), bspec],
        out_shape=[jax.ShapeDtypeStruct((B, S, HD), BF16), jax.ShapeDtypeStruct((2, B, S, HD), BF16),
                   jax.ShapeDtypeStruct((H, Q_BLOCK, K_SPAN), F32)],
        scratch_shapes=[pltpu.VMEM((hw, S), F32), pltpu.VMEM((hw, S), F32)],
        compiler_params=_params(("arbitrary", "arbitrary", "arbitrary")))(q, kv, kv, bias, do)


def _sub_rows(R):
    for cand in (256, 352, 128, 64, 8):
        if R % cand == 0 and R > cand:
            return cand
    return R


def _adamw(w, g, m, v, *, name):
    R, C = w.shape
    tr = _sub_rows(R)

    def body(w_ref, g_ref, m_ref, v_ref, d_ref, nm_ref, nv_ref):
        g = g_ref[...]
        m = ADAM_B1 * m_ref[...] + (1.0 - ADAM_B1) * g
        v = ADAM_B2 * v_ref[...] + (1.0 - ADAM_B2) * (g * g)
        m_hat = m / (1.0 - ADAM_B1 ** ADAM_STEP)
        v_hat = v / (1.0 - ADAM_B2 ** ADAM_STEP)
        d_ref[...] = -ADAM_LR * (m_hat / (jnp.sqrt(v_hat) + ADAM_EPS) + ADAM_WD * w_ref[...])
        nm_ref[...] = m
        nv_ref[...] = v

    spec = pl.BlockSpec((tr, C), lambda i: (i, 0))
    return pl.pallas_call(body, name=name, grid=(R // tr,), in_specs=[spec] * 4, out_specs=[spec] * 3,
                          out_shape=[jax.ShapeDtypeStruct((R, C), F32)] * 3,
                          compiler_params=_params(("arbitrary",)))(w, g, m, v)


def _add_pair(units, got, core, *, name):
    n4, R, C = got.shape
    rows = n4 * R
    tr = 512 if rows % 512 == 0 else R

    def body(c_ref, u_ref, got_ref, o_ref):
        o_ref[...] = (u_ref[...].astype(F32) + got_ref[...].astype(F32)).astype(BF16)

    spec = pl.BlockSpec((tr, C), lambda i, c: (i, 0))
    grid_spec = pltpu.PrefetchScalarGridSpec(
        num_scalar_prefetch=1, grid=(rows // tr,),
        in_specs=[pl.BlockSpec((None, tr, C), lambda i, c: (c[0], i, 0)), spec], out_specs=spec)
    out = pl.pallas_call(body, name=name, grid_spec=grid_spec, out_shape=jax.ShapeDtypeStruct((rows, C), BF16),
                         compiler_params=_params(("arbitrary",)))(core.reshape(1), units.reshape(2, rows, C),
                                                                   got.reshape(rows, C))
    return out.reshape(n4, R, C)


def _sum_chips(w, own, got, pos, *, name, layer=0, into=None):
    _, R, C = own.shape
    tr = _sub_rows(R)
    nr = R // tr

    def body(p_ref, own_ref, got_ref, *rest):
        o_ref = rest[-1]
        o_ref[...] = (own_ref[...].astype(F32) + got_ref[0].astype(F32) + got_ref[1].astype(F32)
                      + got_ref[2].astype(F32))

    if w.row_sharded:
        out_map = lambda i, p: (layer, i, p[1])
    else:
        out_map = lambda i, p: (layer, p[1] * nr + i, 0)
    ins = [pos, own, got]
    in_specs = [pl.BlockSpec((None, tr, C), lambda i, p: (p[0], i, 0)),
                pl.BlockSpec((3, tr, C), lambda i, p: (0, i, 0))]
    alias = {}
    if into is not None:
        ins.append(into)
        in_specs.append(ANY)
        alias = {3: 0}
    grid_spec = pltpu.PrefetchScalarGridSpec(num_scalar_prefetch=1, grid=(nr,), in_specs=in_specs,
                                             out_specs=pl.BlockSpec((None, tr, C), out_map))
    return pl.pallas_call(body, name=name, grid_spec=grid_spec, input_output_aliases=alias,
                          out_shape=jax.ShapeDtypeStruct((w.L, w.ks, w.ns), F32),
                          compiler_params=_params(("arbitrary",)))(*ins)


def _mesh_pos():
    return lax.axis_index("x"), lax.axis_index("y"), lax.axis_index("c")


def _other_chips(x, y):
    return [(1 - x, y), (x, 1 - y), (1 - x, 1 - y)]


ANY = pl.BlockSpec(memory_space=pl.ANY)


class _W:
    def __init__(self, name, shard, row_sharded, direct=False):
        self.name = name
        self.direct = direct
        self.L, ks, ns = shard.shape
        self.row_sharded = row_sharded
        self.K, self.N = (ks * N_CHIPS, ns) if row_sharded else (ks, ns * N_CHIPS)
        self.ks, self.ns = ks, ns

    def shard_of(self, full, j):
        if self.row_sharded:
            return full.at[:, pl.ds(j * self.ks, self.ks), :]
        return full.at[:, :, pl.ds(j * self.ns, self.ns)]

    def half_of(self, shard, c):
        if self.row_sharded:
            return shard.at[:, :, pl.ds(c * (self.ns // 2), self.ns // 2)]
        return shard.at[:, pl.ds(c * (self.ks // 2), self.ks // 2), :]


HBM = pl.BlockSpec(memory_space=pltpu.HBM)
SEM = pl.BlockSpec(memory_space=pltpu.SEMAPHORE)
IN_FLIGHT = pltpu.SideEffectType.DATAFLOW_SIDE_EFFECTING


def _in_hbm(a):
    return pltpu.with_memory_space_constraint(a, pltpu.HBM)


def _gather_start(ws, shards, after, *, name):
    nw = len(ws)

    def body(*refs):
        src, dst = refs[:nw], refs[nw:2 * nw]
        send, recv = refs[2 * nw + 1:3 * nw + 1], refs[3 * nw + 1:4 * nw + 1]
        x, y, c = _mesh_pos()
        me = 2 * x + y
        for i, w in enumerate(ws):
            for f, (px, py) in enumerate(_other_chips(x, y)):
                for e in range(2 if w.direct else 1):
                    k = 2 * f + e
                    pltpu.make_async_remote_copy(
                        src_ref=w.half_of(src[i], c), dst_ref=w.half_of(w.shard_of(dst[i], me), c),
                        send_sem=send[i].at[k], recv_sem=recv[i].at[k], device_id=(px, py, c if e == 0 else 1 - c),
                        device_id_type=MESH).start()

    fulls = [lax.empty((w.L, w.K, w.N), BF16) for w in ws]
    out = pl.pallas_call(
        body, name=name, in_specs=[HBM] * (2 * nw) + [ANY],
        out_specs=[SEM] * (2 * nw) + [HBM] * (2 * nw),
        out_shape=[pltpu.SemaphoreType.DMA((6,))] * (2 * nw)
        + [pltpu.HBM(s.shape, BF16) for s in shards] + [pltpu.HBM(f.shape, BF16) for f in fulls],
        input_output_aliases={i: 2 * nw + i for i in range(2 * nw)},
        compiler_params=pltpu.CompilerParams(has_side_effects=IN_FLIGHT))(
            *[_in_hbm(s) for s in shards], *[_in_hbm(f) for f in fulls], after)
    return [(out[i], out[nw + i], out[2 * nw + i], out[3 * nw + i]) for i in range(nw)]


def _gather_wait(ws, flight, after, *, name):
    nw = len(ws)

    def body(*refs):
        src, dst = refs[:nw], refs[nw:2 * nw]
        send, recv = refs[2 * nw:3 * nw], refs[3 * nw:4 * nw]
        x, y, c = _mesh_pos()
        for i, w in enumerate(ws):
            for f, (px, py) in enumerate(_other_chips(x, y)):
                for e in range(2 if w.direct else 1):
                    k = 2 * f + e
                    landed = w.half_of(w.shard_of(dst[i], 2 * px + py), c if e == 0 else 1 - c)
                    cp = pltpu.make_async_remote_copy(
                        src_ref=w.half_of(src[i], c), dst_ref=landed, send_sem=send[i].at[k], recv_sem=recv[i].at[k],
                        device_id=(px, py, c), device_id_type=MESH)
                    cp.wait_send()
                    cp.wait_recv()

    shards, fulls = [fl[2] for fl in flight], [fl[3] for fl in flight]
    out = pl.pallas_call(
        body, name=name, in_specs=[HBM] * (2 * nw) + [SEM] * (2 * nw) + [ANY],
        out_specs=[HBM] * (2 * nw),
        out_shape=[pltpu.HBM(s.shape, BF16) for s in shards] + [pltpu.HBM(f.shape, BF16) for f in fulls],
        input_output_aliases={i: i for i in range(2 * nw)},
        compiler_params=pltpu.CompilerParams(has_side_effects=IN_FLIGHT))(
            *shards, *fulls, *[fl[0] for fl in flight], *[fl[1] for fl in flight], after)
    return out[:nw], out[nw:]


def _gather_finish(ws, shards, fulls, *, name):
    nw = len(ws)
    forward = not ws[0].direct

    def body(*refs):
        src, dst, stage = refs[:nw], refs[3 * nw:4 * nw], refs[4 * nw:5 * nw]
        send_sems, recv_sems, load_sems, store_sems = refs[5 * nw:]
        x, y, c = _mesh_pos()
        me = 2 * x + y
        sibling = (x, y, 1 - c)
        chips = _other_chips(x, y)

        def fwd(i, w, f, half):
            px, py = chips[f]
            landed = w.half_of(w.shard_of(dst[i], 2 * px + py), half)
            return pltpu.make_async_remote_copy(src_ref=landed, dst_ref=landed, send_sem=send_sems.at[3 * i + f],
                                                recv_sem=recv_sems.at[3 * i + f], device_id=sibling,
                                                device_id_type=MESH)

        loads = [pltpu.make_async_copy(src[i], stage[i], load_sems.at[i]) for i in range(nw)]
        for cp in loads:
            cp.start()
        sends = [fwd(i, w, f, c) for i, w in enumerate(ws) for f in range(3)] if forward else []
        for cp in sends:
            cp.start()
        stores = [pltpu.make_async_copy(stage[i], w.shard_of(dst[i], me), store_sems.at[i])
                  for i, w in enumerate(ws)]
        for ld, st in zip(loads, stores):
            ld.wait()
            st.start()
        if forward:
            for i, w in enumerate(ws):
                for f in range(3):
                    fwd(i, w, f, 1 - c).wait_recv()
        for cp in sends:
            cp.wait_send()
        for cp in stores:
            cp.wait()

    out = pl.pallas_call(
        body, name=name, in_specs=[ANY] * (2 * nw), out_specs=[ANY] * (2 * nw),
        out_shape=[jax.ShapeDtypeStruct(s.shape, BF16) for s in shards]
        + [jax.ShapeDtypeStruct(f.shape, BF16) for f in fulls],
        input_output_aliases={i: i for i in range(2 * nw)},
        scratch_shapes=[pltpu.VMEM((w.L, w.ks, w.ns), BF16) for w in ws]
        + [pltpu.SemaphoreType.DMA((3 * nw,)), pltpu.SemaphoreType.DMA((3 * nw,)), pltpu.SemaphoreType.DMA((nw,)),
           pltpu.SemaphoreType.DMA((nw,))],
        compiler_params=_params(has_side_effects=True))(*shards, *fulls)
    return out[nw:]


def _split_copies(name, srcs, lands, n_sems, copies_of, *, flight=None, after=None):
    n = len(srcs)
    starting = flight is None

    def body(*refs):
        src, land = refs[:n], refs[n:2 * n]
        sems = refs[2 * n + 1:4 * n + 1] if starting else refs[2 * n:4 * n]
        for i in range(n):
            for cp in copies_of(i, src[i], land[i], sems[i], sems[n + i]):
                if starting:
                    cp.start()
                else:
                    cp.wait_send()
                    cp.wait_recv()

    thru = [pltpu.HBM(a.shape, a.dtype) for a in list(srcs) + list(lands)]
    if starting:
        out = pl.pallas_call(
            body, name=name, in_specs=[HBM] * (2 * n) + [ANY], out_specs=[SEM] * (2 * n) + [HBM] * (2 * n),
            out_shape=[pltpu.SemaphoreType.DMA((n_sems,))] * (2 * n) + thru,
            input_output_aliases={i: 2 * n + i for i in range(2 * n)},
            compiler_params=pltpu.CompilerParams(has_side_effects=IN_FLIGHT))(
                *[_in_hbm(a) for a in srcs], *[_in_hbm(a) for a in lands], after)
        return [(out[i], out[n + i], out[2 * n + i], out[3 * n + i]) for i in range(n)]
    out = pl.pallas_call(
        body, name=name, in_specs=[HBM] * (2 * n) + [SEM] * (2 * n) + [ANY], out_specs=[HBM] * (2 * n),
        out_shape=thru, input_output_aliases={i: i for i in range(2 * n)},
        compiler_params=pltpu.CompilerParams(has_side_effects=IN_FLIGHT))(
            *srcs, *lands, *[fl[0] for fl in flight], *[fl[1] for fl in flight], after)
    return out[:n], out[n:]


def _sum8(land, vec, me):
    R = vec.shape[0]

    def body(me_ref, land_ref, vec_ref, o_ref):
        acc = jnp.zeros((R, 128), F32)
        for d in range(8):
            acc = acc + jnp.where(me_ref[0] == d, vec_ref[...], land_ref[d])
        o_ref[...] = acc

    grid_spec = pltpu.PrefetchScalarGridSpec(
        num_scalar_prefetch=1, grid=(1,),
        in_specs=[pl.BlockSpec((8, R, 128), lambda i, m: (0, 0, 0)), pl.BlockSpec((R, 128), lambda i, m: (0, 0))],
        out_specs=pl.BlockSpec((R, 128), lambda i, m: (0, 0)))
    return pl.pallas_call(body, name="sum8", grid_spec=grid_spec, out_shape=jax.ShapeDtypeStruct((R, 128), F32),
                          compiler_params=_params(("arbitrary",)))(me.reshape(1), land, vec)


def _swap_copies(i, src, got, send, recv):
    x, y, c = _mesh_pos()
    return [pltpu.make_async_remote_copy(src_ref=src.at[1 - c], dst_ref=got, send_sem=send.at[0], recv_sem=recv.at[0],
                                         device_id=(x, y, 1 - c), device_id_type=MESH)]


def _gather8_copies(i, src, land, send, recv):
    x, y, c = _mesh_pos()
    me = 4 * x + 2 * y + c
    peers = [(x, y, 1 - c)] + [(px, py, pc) for px, py in _other_chips(x, y) for pc in (c, 1 - c)]
    return [pltpu.make_async_remote_copy(src_ref=src, dst_ref=land.at[me], send_sem=send.at[k], recv_sem=recv.at[k],
                                         device_id=peer, device_id_type=MESH) for k, peer in enumerate(peers)]


def _scatter_copy(src, got, send, recv, f, chip, c):
    px, py = chip
    return pltpu.make_async_remote_copy(src_ref=src.at[2 * px + py], dst_ref=got.at[f], send_sem=send.at[f],
                                        recv_sem=recv.at[f], device_id=(px, py, c), device_id_type=MESH)


def _scatter_start(sums, *, name):
    nw = len(sums)

    def body(*refs):
        src, got = refs[:nw], refs[nw:2 * nw]
        send, recv = refs[2 * nw:3 * nw], refs[3 * nw:4 * nw]
        x, y, c = _mesh_pos()
        for i in range(nw):
            for f, chip in enumerate(_other_chips(x, y)):
                _scatter_copy(src[i], got[i], send[i], recv[i], f, chip, c).start()

    lands = [lax.empty((3,) + s.shape[1:], BF16) for s in sums]
    out = pl.pallas_call(
        body, name=name, in_specs=[HBM] * (2 * nw), out_specs=[SEM] * (2 * nw) + [HBM] * (2 * nw),
        out_shape=[pltpu.SemaphoreType.DMA((3,))] * (2 * nw)
        + [pltpu.HBM(s.shape, BF16) for s in sums] + [pltpu.HBM(l.shape, BF16) for l in lands],
        input_output_aliases={i: 2 * nw + i for i in range(2 * nw)},
        compiler_params=pltpu.CompilerParams(has_side_effects=IN_FLIGHT))(
            *[_in_hbm(s) for s in sums], *[_in_hbm(l) for l in lands])
    return [(out[i], out[nw + i], out[2 * nw + i], out[3 * nw + i]) for i in range(nw)]


def _scatter_wait(flight, after):
    nw = len(flight)

    def body(*refs):
        src, got = refs[:nw], refs[nw:2 * nw]
        send, recv = refs[2 * nw:3 * nw], refs[3 * nw:4 * nw]
        x, y, c = _mesh_pos()
        for i in range(nw):
            for f, chip in enumerate(_other_chips(x, y)):
                cp = _scatter_copy(src[i], got[i], send[i], recv[i], f, chip, c)
                cp.wait_send()
                cp.wait_recv()

    sums, lands = [fl[2] for fl in flight], [fl[3] for fl in flight]
    out = pl.pallas_call(
        body, name="scatter_wait", in_specs=[HBM] * (2 * nw) + [SEM] * (2 * nw) + [ANY], out_specs=[HBM] * (2 * nw),
        out_shape=[pltpu.HBM(s.shape, BF16) for s in sums] + [pltpu.HBM(l.shape, BF16) for l in lands],
        input_output_aliases={i: i for i in range(2 * nw)},
        compiler_params=pltpu.CompilerParams(has_side_effects=IN_FLIGHT))(
            *sums, *lands, *[fl[0] for fl in flight], *[fl[1] for fl in flight], after)
    return out[:nw], out[nw:]


def _join_halves(ws, shards):
    nw = len(ws)

    def body(*refs):
        buf = refs[nw:2 * nw]
        send_sems, recv_sems = refs[2 * nw:]
        x, y, c = _mesh_pos()
        sibling = (x, y, 1 - c)

        def copy(i, w, half):
            region = w.half_of(buf[i], half)
            return pltpu.make_async_remote_copy(src_ref=region, dst_ref=region, send_sem=send_sems.at[i],
                                                recv_sem=recv_sems.at[i], device_id=sibling, device_id_type=MESH)

        sends = [copy(i, w, c) for i, w in enumerate(ws)]
        for cp in sends:
            cp.start()
        for i, w in enumerate(ws):
            copy(i, w, 1 - c).wait_recv()
        for cp in sends:
            cp.wait_send()

    return pl.pallas_call(
        body, name="join_halves", in_specs=[ANY] * nw, out_specs=[ANY] * nw,
        out_shape=[jax.ShapeDtypeStruct((w.L, w.ks, w.ns), F32) for w in ws],
        input_output_aliases={i: i for i in range(nw)},
        scratch_shapes=[pltpu.SemaphoreType.DMA((nw,)), pltpu.SemaphoreType.DMA((nw,))],
        compiler_params=_params(has_side_effects=True))(*shards)


def _allreduce_small(vec):
    R = vec.shape[0]

    def body(x_ref, o_ref, buf, send_sems, recv_sems):
        x, y, c = _mesh_pos()
        me, sibling = (x, y, c), (x, y, 1 - c)
        chips = _other_chips(x, y)

        def slot(px, py, pc):
            return buf.at[4 * px + 2 * py + pc]

        def copy(k, block, to, src=None):
            return pltpu.make_async_remote_copy(src_ref=slot(*block) if src is None else src, dst_ref=slot(*block),
                                                send_sem=send_sems.at[k], recv_sem=recv_sems.at[k], device_id=to,
                                                device_id_type=MESH)

        first = [copy(0, me, sibling, src=x_ref)] + [copy(1 + f, me, (*chip, c), src=x_ref)
                                                     for f, chip in enumerate(chips)]
        for cp in first:
            cp.start()
        passed = [copy(4 + f, (*chip, c), sibling) for f, chip in enumerate(chips)]
        for f, chip in enumerate(chips):
            copy(1 + f, (*chip, c), me).wait_recv()
            passed[f].start()
        copy(0, sibling, me).wait_recv()
        for f, chip in enumerate(chips):
            copy(4 + f, (*chip, 1 - c), me).wait_recv()
        for cp in first + passed:
            cp.wait_send()
        slot(*me)[...] = x_ref[...]
        acc = buf[0]
        for d in range(1, 8):
            acc = acc + buf[d]
        o_ref[...] = acc

    return pl.pallas_call(
        body, name="allreduce_small", in_specs=[pl.BlockSpec(memory_space=pltpu.VMEM)],
        out_specs=pl.BlockSpec(memory_space=pltpu.VMEM), out_shape=jax.ShapeDtypeStruct((R, 128), F32),
        scratch_shapes=[pltpu.VMEM((8, R, 128), F32), pltpu.SemaphoreType.DMA((7,)), pltpu.SemaphoreType.DMA((7,))],
        compiler_params=_params())(vec)


def _pack(parts):
    flat = jnp.concatenate([p.reshape(-1).astype(F32) for p in parts])
    n = flat.shape[0]
    pad = (-n) % (64 * 128)
    return jnp.pad(flat, (0, pad)).reshape(-1, 128)


def _unpack(vec, shapes):
    flat = vec.reshape(-1)
    out, off = [], 0
    for s in shapes:
        n = int(np.prod(s))
        out.append(flat[off:off + n].reshape(s))
        off += n
    return out


def kernel(x, a_norm_g, a_w_in, a_v_norm_g, a_w_s, a_b_s, a_w_out, kv_norm_g, w_kv, b_norm_g, b_w_q, b_rel_bias, b_w_o, f_norm_g, f_w_in, f_conv_w, f_conv_b, f_w_down, final_norm_g, loss_target, m_a_norm_g, m_a_w_in, m_a_v_norm_g, m_a_w_s, m_a_b_s, m_a_w_out, m_kv_norm_g, m_w_kv, m_b_norm_g, m_b_w_q, m_b_rel_bias, m_b_w_o, m_f_norm_g, m_f_w_in, m_f_conv_w, m_f_conv_b, m_f_w_down, m_final_norm_g, v_a_norm_g, v_a_w_in, v_a_v_norm_g, v_a_w_s, v_a_b_s, v_a_w_out, v_kv_norm_g, v_w_kv, v_b_norm_g, v_b_w_q, v_b_rel_bias, v_b_w_o, v_f_norm_g, v_f_w_in, v_f_conv_w, v_f_conv_b, v_f_w_down, v_final_norm_g):
    B, S, D = x.shape
    T = B * S
    xi, yi, ci = lax.axis_index("x"), lax.axis_index("y"), lax.axis_index("c")
    j_me = (2 * xi + yi).astype(jnp.int32)
    core = ci.astype(jnp.int32)
    pos = jnp.stack([j_me, core])

    w_shards = {"a_w_in": (a_w_in, False), "a_w_out": (a_w_out, True), "w_kv": (w_kv[None], False),
                "b_w_q": (b_w_q, True), "b_w_o": (b_w_o, True), "f_w_in": (f_w_in, False), "f_w_down": (f_w_down, True)}
    names = list(w_shards)
    ws = [_W(n, w_shards[n][0], w_shards[n][1]) for n in names]
    g_shards = {"a_w_in": (a_w_in, False), "a_w_out": (a_w_out, True),
                "f_w_in0": (f_w_in[0:1], False), "f_w_down0": (f_w_down[0:1], True),
                "w_kv": (w_kv[None], False), "b_w_q": (b_w_q, True), "b_w_o": (b_w_o, True),
                "f_w_in1": (f_w_in[1:2], False), "f_w_down1": (f_w_down[1:2], True)}
    g_names = list(g_shards)
    g_ws = {n: _W(n, *g_shards[n], direct=n in ("w_kv", "b_w_q", "b_w_o", "f_w_in1", "f_w_down1")) for n in g_names}

    Wd = a_w_in.shape[1]
    GW = a_v_norm_g.shape[1] * N_CHIPS
    F2 = f_conv_w.shape[2] * N_CHIPS
    Fh = F2 // 2
    nsd, nsg, nsf = a_norm_g.shape[1], a_v_norm_g.shape[1], f_conv_w.shape[2]
    own = (ci == 0).astype(F32)
    place = lambda sh, width, n: lax.dynamic_update_slice_in_dim(
        jnp.zeros(sh.shape[:-1] + (width,), F32), sh * own, j_me * n, axis=sh.ndim - 1)
    def tied(x, flight):
        x, thru = lax.optimization_barrier((x, flight[0][2]))
        return x, [flight[0][:2] + (thru,) + flight[0][3:]] + flight[1:]

    gathered = _allreduce_small(_pack([place(a_norm_g, Wd, nsd), place(a_v_norm_g, GW, nsg),
                                       place(f_conv_w, F2, nsf)]))
    a_g, a_vg, conv_w = _unpack(gathered, [(1, Wd), (1, GW), (2, 3, F2)])
    first, rest = g_names[:4], g_names[4:]
    flight = dict(zip(first, _gather_start([g_ws[n] for n in first], [g_shards[n][0].astype(BF16) for n in first],
                                           gathered, name="gather_start_first")))
    (fi, fd, kv_w, qw, ow), (flight[first[0]],) = tied((f_w_in, f_w_down, w_kv, b_w_q, b_w_o), [flight[first[0]]])
    late = {"w_kv": kv_w[None], "b_w_q": qw, "b_w_o": ow, "f_w_in1": fi[1:2], "f_w_down1": fd[1:2]}
    flight.update(zip(rest, _gather_start([g_ws[n] for n in rest], [late[n].astype(BF16) for n in rest], kv_w,
                                          name="gather_start_rest")))
    full = {}

    def arrive(group, after, tag):
        gw = [g_ws[n] for n in group]
        sh, fu = _gather_wait(gw, [flight[n] for n in group], after, name=f"gather_wait_{tag}")
        full.update(zip(group, _gather_finish(gw, sh, fu, name=f"gather_finish_{tag}")))
    conv_w2 = conv_w.reshape(2, 3, 2, Fh).transpose(0, 2, 1, 3)
    conv_b2 = f_conv_b.reshape(2, 2, Fh)

    h0 = x.reshape(T, D)
    target = loss_target.reshape(T, D)
    bs_tile = jnp.repeat(a_b_s[0].T, GROUP_DIM, axis=1)
    ws_a = a_w_s[0]
    scale = HEAD_DIM ** -0.5
    HD = b_w_q.shape[2]
    H = HD // HEAD_DIM
    n_rel = b_rel_bias.shape[-1]
    frow, (flight["w_kv"],) = tied(b_rel_bias[0][:, _bias_index()].reshape(H, 1, F_LEN), [flight["w_kv"]])
    bias = _bias_expand(frow)

    def ffn_fwd(h, l, loss=None):
        out = _ffn_fwd(h, full[f"f_w_in{l}"], f_norm_g[l], conv_w2[l], conv_b2[l], full[f"f_w_down{l}"], S,
                       loss=loss, name=f"ffn{l}")
        yff, a, c, n = out[1:5]
        return (out[0] if loss is None else (out[0], out[5], out[6])), (a, c, n, yff)

    arrive(["a_w_in", "a_w_out"], bias, "a")
    h1, zp, out_a, n_a = _mixer_a_fwd(h0, full["a_w_in"], a_g[0], a_vg, ws_a, bs_tile, full["a_w_out"])
    arrive(["f_w_in0"], h1, "f0")
    yff0, a0, c0, n0 = _ffn_fwd(h1, full["f_w_in0"], f_norm_g[0], conv_w2[0], conv_b2[0], None, S, name="ffn0_in")
    arrive(["f_w_down0"], yff0, "fd0")
    h2, saved0 = _mm(yff0, full["f_w_down0"], res=h1, name="ffn0_down"), (a0, c0, n0, yff0)
    arrive(["w_kv", "b_w_q", "b_w_o"], h2, "b")
    arrive(["f_w_in1", "f_w_down1"], h2, "f1")
    q, kv, n_q, n_kv = _qkv_fwd(h2, full["b_w_q"], b_norm_g[0], full["w_kv"], kv_norm_g, scale)
    kv4, q3 = kv.reshape(2, B, S, HD), q.reshape(B, S, HD)
    o = _attn_fwd(q3, kv4, bias, B, S).reshape(T, HD)
    h3 = _mm(o, full["b_w_o"], res=h2, name="attn_out")
    (dh, loss8, dg_final), saved1 = ffn_fwd(h3, 1, loss=(final_norm_g, target))

    units = {}

    in_flight = {}

    def swap_start(group, tag, carry):
        us = [units[n] for n in group]
        lands = [lax.empty(u.shape[1:], BF16) for u in us]
        carry, flight = tied(carry, _split_copies(f"swap_start_{tag}", us, lands, 1, _swap_copies, after=carry))
        return (group, tag, flight), carry

    def reduce_start(swap, after):
        group, tag, flight = swap
        us, got = _split_copies(f"swap_wait_{tag}", [fl[2] for fl in flight], [fl[3] for fl in flight], 1,
                                _swap_copies, flight=flight, after=after)
        sums = [_add_pair(u, g_, core, name=f"pair_{n}") for n, u, g_ in zip(group, us, got)]
        after, flight = tied(after, _scatter_start(sums, name=f"scatter_start_{tag}"))
        in_flight.update(zip(group, flight))
        return after

    def ffn_bwd(dh, h, saved, l, early):
        a, c, n, yff = saved
        units[f"f_w_down{l}"] = _mm_tn(yff, dh, rows_are_shards=True, name=f"ffn{l}_down_dw")
        dh_in = dh
        if early:
            sw, dh_in = swap_start([f"f_w_down{l}"], f"fd{l}", dh)
        dyff = _mm(dh_in, full[f"f_w_down{l}"], trans_w=True, out_dtype=BF16, name=f"ffn{l}_down_dx")
        if early:
            dyff = reduce_start(sw, dyff)
        da, dcw, dcb = _conv_bwd(a, c, dyff, conv_w2[l], S)
        units[f"f_w_in{l}"] = _mm_tn(n, da, split_y=True, name=f"ffn{l}_in_dw")
        sw, da = swap_start([f"f_w_in{l}"] if early else [f"f_w_down{l}", f"f_w_in{l}"], f"f{l}", da)
        dh, dg = _mm(da, full[f"f_w_in{l}"], trans_w=True, split_x=True, bwd=(h, f_norm_g[l], dh),
                     name=f"ffn{l}_in_dx")
        return reduce_start(sw, dh), dg, dcw, dcb

    dh, dg_f1, dcw1, dcb1 = ffn_bwd(dh, h3, saved1, 1, False)
    do = _mm(dh, full["b_w_o"], trans_w=True, out_dtype=BF16, name="attn_out_dx")
    units["b_w_o"] = _mm_tn(o, dh, rows_are_shards=True, name="b_w_o_dw")
    dq, dkv, dbias = _attn_bwd(q3, kv4, bias, do.reshape(B, S, HD), B, S)
    dq, d_rel = lax.optimization_barrier((dq, _bias_reduce(dbias, n_rel)))
    d_rel = d_rel.reshape(1, H, n_rel)
    dq, dkv = dq.reshape(T, HD), dkv.reshape(2, T, HD)
    units["b_w_q"] = _mm_tn(n_q, dq, rows_are_shards=True, name="b_w_q_dw")
    units["w_kv"] = _mm_tn(n_kv, dkv, split_y=True, name="w_kv_dw")
    sw, dkv = swap_start(["b_w_o", "b_w_q", "w_kv"], "b", dkv)
    dh, dg_b, dg_kv = _qkv_dx(dq, full["b_w_q"], b_norm_g[0], dkv, full["w_kv"], kv_norm_g, h2, dh)
    dh = reduce_start(sw, dh)
    dh, dg_f0, dcw0, dcb0 = ffn_bwd(dh, h1, saved0, 0, True)
    units["a_w_out"] = _mm_tn(out_a, dh, rows_are_shards=True, name="a_w_out_dw")
    sw, dh_in = swap_start(["a_w_out"], "ao", dh)
    d_out = _mm(dh_in, full["a_w_out"], trans_w=True, out_dtype=BF16, name="a_out_dx")
    d_out = reduce_start(sw, d_out)
    dzp, dws, dbs, dgv = _gate_bwd(zp, d_out, a_vg, ws_a, bs_tile)
    units["a_w_in"] = _mm_tn(n_a, dzp, name="a_w_in_dw")
    sw, dzp_in = swap_start(["a_w_in"], "ai", dzp)
    dzp_in = reduce_start(sw, dzp_in)
    grad_x, dg_a = _mm(dzp_in, full["a_w_in"], trans_w=True, bwd=(h0, a_g[0], dh), name="a_in_dx")

    to_flat = lambda d: d.transpose(1, 0, 2).reshape(3, F2)
    small = {"a_norm_g": dg_a, "a_v_norm_g": dgv, "a_w_s": dws[None], "a_b_s": dbs[None], "kv_norm_g": dg_kv[0],
             "b_norm_g": dg_b, "b_rel_bias": d_rel, "f_norm_g": jnp.concatenate([dg_f0, dg_f1], axis=0),
             "f_conv_w": jnp.stack([to_flat(dcw0), to_flat(dcw1)]),
             "f_conv_b": jnp.stack([dcb0.reshape(F2), dcb1.reshape(F2)]), "final_norm_g": dg_final[0]}
    snames = list(small)
    small_vec = _pack([small[n] for n in snames] + [loss8[0:1, 0:1]])
    grad_x, small_flight = tied(grad_x, _split_copies("small_start", [small_vec],
                                                      [lax.empty((8,) + small_vec.shape, F32)], 7, _gather8_copies,
                                                      after=grad_x))

    sums, recv = _scatter_wait([in_flight[n] for n in g_names], grad_x)
    sums, recv = dict(zip(g_names, sums)), dict(zip(g_names, recv))
    halves = []
    for n, w in zip(names, ws):
        if w.L == 1:
            halves.append(_sum_chips(w, sums[n], recv[n], pos, name=f"chips_{n}"))
        else:
            first = _sum_chips(w, sums[n + "0"], recv[n + "0"], pos, name=f"chips_{n}0")
            halves.append(_sum_chips(w, sums[n + "1"], recv[n + "1"], pos, layer=1, into=first, name=f"chips_{n}1"))
    g_big = dict(zip(names, _join_halves(ws, halves)))
    g_big["w_kv"] = g_big["w_kv"][0]

    given = dict(a_norm_g=(a_norm_g, m_a_norm_g, v_a_norm_g), a_w_in=(a_w_in, m_a_w_in, v_a_w_in),
                 a_v_norm_g=(a_v_norm_g, m_a_v_norm_g, v_a_v_norm_g), a_w_s=(a_w_s, m_a_w_s, v_a_w_s),
                 a_b_s=(a_b_s, m_a_b_s, v_a_b_s), a_w_out=(a_w_out, m_a_w_out, v_a_w_out),
                 kv_norm_g=(kv_norm_g, m_kv_norm_g, v_kv_norm_g), w_kv=(w_kv, m_w_kv, v_w_kv),
                 b_norm_g=(b_norm_g, m_b_norm_g, v_b_norm_g), b_w_q=(b_w_q, m_b_w_q, v_b_w_q),
                 b_rel_bias=(b_rel_bias, m_b_rel_bias, v_b_rel_bias), b_w_o=(b_w_o, m_b_w_o, v_b_w_o),
                 f_norm_g=(f_norm_g, m_f_norm_g, v_f_norm_g), f_w_in=(f_w_in, m_f_w_in, v_f_w_in),
                 f_conv_w=(f_conv_w, m_f_conv_w, v_f_conv_w), f_conv_b=(f_conv_b, m_f_conv_b, v_f_conv_b),
                 f_w_down=(f_w_down, m_f_w_down, v_f_w_down), final_norm_g=(final_norm_g, m_final_norm_g, v_final_norm_g))
    order = list(given)
    grads, deltas, new_m, new_v = {}, {}, {}, {}
    for n in names:
        w_, m_, v_ = given[n]
        g_ = g_big[n]
        C = w_.shape[-1]
        d2, m2, v2 = _adamw(w_.reshape(-1, C), g_.reshape(-1, C), m_.reshape(-1, C), v_.reshape(-1, C),
                            name=f"adamw_{n}")
        grads[n], deltas[n], new_m[n], new_v[n] = g_.reshape(w_.shape), d2.reshape(w_.shape), m2.reshape(w_.shape), \
            v2.reshape(w_.shape)
    vecs, lands = _split_copies("small_wait", [small_flight[0][2]], [small_flight[0][3]], 7, _gather8_copies,
                                flight=small_flight, after=deltas[names[-1]])
    red = _sum8(lands[0], vecs[0], (4 * xi + 2 * yi + ci).astype(jnp.int32))
    parts = _unpack(red, [small[n].shape for n in snames] + [(1,)])
    g_small = dict(zip(snames, parts[:-1]))
    loss = parts[-1][0]
    g_small["a_norm_g"] = lax.dynamic_slice_in_dim(g_small["a_norm_g"], j_me * nsd, nsd, axis=1)
    g_small["a_v_norm_g"] = lax.dynamic_slice_in_dim(g_small["a_v_norm_g"], j_me * nsg, nsg, axis=1)
    g_small["f_conv_w"] = lax.dynamic_slice_in_dim(g_small["f_conv_w"], j_me * nsf, nsf, axis=2)

    sm = [n for n in order if n not in names]
    d2, m2, v2 = _adamw(_pack([given[n][0] for n in sm]), _pack([g_small[n].reshape(given[n][0].shape) for n in sm]),
                        _pack([given[n][1] for n in sm]), _pack([given[n][2] for n in sm]), name="adamw_small")
    shapes = [given[n][0].shape for n in sm]
    for n, d_, m_, v_ in zip(sm, _unpack(d2, shapes), _unpack(m2, shapes), _unpack(v2, shapes)):
        grads[n], deltas[n], new_m[n], new_v[n] = g_small[n].reshape(given[n][0].shape), d_, m_, v_

    return (loss, grad_x.reshape(B, S, D), *[grads[n] for n in order], *[deltas[n] for n in order],
            *[new_m[n] for n in order], *[new_v[n] for n in order])
```

```python
import functools
import math

import numpy as np
import jax
import jax.numpy as jnp
from jax import lax
from jax.experimental import pallas as pl
from jax.experimental.pallas import tpu as pltpu

F32 = jnp.float32
BF16 = jnp.bfloat16
MESH = pl.DeviceIdType.MESH

EPS = 1e-6
NEG_INF = -1e30
CHUNK = 64
GMLP_BLOCK = 128
GROUP_DIM = 128
HEAD_DIM = 64
LEFT_CHUNKS = 8
PAD = LEFT_CHUNKS * CHUNK
REL_CLIP = 128
Q_BLOCK = 256
K_SPAN = PAD + Q_BLOCK
F_LEN = K_SPAN + Q_BLOCK
HEADS_PER_STEP = 4
N_CHIPS = 4

ADAM_LR = 0.001
ADAM_B1 = 0.9
ADAM_B2 = 0.999
ADAM_EPS = 1e-08
ADAM_WD = 0.01
ADAM_STEP = 10

VMEM_LIMIT = 56 * 1024 * 1024


def _params(sem=None, **kw):
    if sem is not None:
        kw["dimension_semantics"] = sem
    return pltpu.CompilerParams(vmem_limit_bytes=VMEM_LIMIT, **kw)


def _rms(xf):
    r = lax.rsqrt(jnp.mean(xf * xf, axis=-1, keepdims=True) + EPS)
    return xf * r, r


def _gelu(x, with_grad=False):
    c = math.sqrt(2.0 / math.pi)
    x2 = x * x
    t = jnp.tanh(c * x * (1.0 + 0.044715 * x2))
    half = 0.5 * (1.0 + t)
    if not with_grad:
        return x * half
    return x * half, half + 0.5 * x * (1.0 - t * t) * c * (1.0 + 3.0 * 0.044715 * x2)


def _col_tile(n):
    if n <= 1024:
        return n
    for t in (1408, 1024, 512):
        if n % t == 0:
            return t
    raise ValueError(n)


def _row_tile(t, want):
    while t % want:
        want //= 2
    return want


def _loss_epilogue(h, g_ref, t_ref, dh_ref, loss_ref, dg_ref, first):
    @pl.when(first)
    def _():
        loss_ref[...] = jnp.zeros_like(loss_ref)
        dg_ref[...] = jnp.zeros_like(dg_ref)

    n, r = _rms(h)
    g = g_ref[...]
    e = n * g - t_ref[...]
    loss_ref[...] += 0.5 * jnp.sum(jnp.mean(e * e, axis=-1, keepdims=True), axis=0, keepdims=True)
    dy = e * (1.0 / h.shape[-1])
    dg_ref[...] += jnp.sum(dy * n, axis=0, keepdims=True)
    t = dy * g
    dh_ref[...] = r * (t - n * jnp.mean(t * n, axis=-1, keepdims=True))


def _mm(x, w, *, name, trans_w=False, res=None, out_dtype=F32, bwd=None, split_x=False, tm=512):
    T = x.shape[-2]
    K = 2 * x.shape[-1] if split_x else x.shape[-1]
    N = w.shape[-2] if trans_w else w.shape[-1]
    tm = _row_tile(T, 2 * tm if max(K, N) <= 2048 else tm)
    has_res, has_bwd = res is not None, bwd is not None
    dims = (((1,), (1,)), ((), ())) if trans_w else (((1,), (0,)), ((), ()))

    def body(*refs):
        it = iter(refs)
        x_ref, w_ref = next(it), next(it)
        res_ref = next(it) if has_res else None
        if has_bwd:
            h_ref, bg_ref, dh_ref = next(it), next(it), next(it)
        o_ref = next(it)
        if split_x:
            kh = K // 2
            acc = lax.dot_general(x_ref[0].astype(BF16), w_ref[:, :kh] if trans_w else w_ref[:kh, :], dims,
                                  preferred_element_type=F32)
            acc = acc + lax.dot_general(x_ref[1].astype(BF16), w_ref[:, kh:] if trans_w else w_ref[kh:, :], dims,
                                        preferred_element_type=F32)
        else:
            acc = lax.dot_general(x_ref[...].astype(BF16), w_ref[...], dims, preferred_element_type=F32)
        if has_res:
            acc = acc + res_ref[...]
        if has_bwd:
            dg_ref = next(it)
            n, r = _rms(h_ref[...])

            @pl.when(pl.program_id(0) == 0)
            def _():
                dg_ref[...] = jnp.zeros_like(dg_ref)

            dg_ref[...] += jnp.sum(acc * n, axis=0, keepdims=True)
            t = acc * bg_ref[...]
            o_ref[...] = dh_ref[...] + r * (t - n * jnp.mean(t * n, axis=-1, keepdims=True))
        else:
            o_ref[...] = acc.astype(out_dtype)

    row = lambda width: pl.BlockSpec((tm, width), lambda m: (m, 0))
    ins = [x, w]
    in_specs = [pl.BlockSpec((2, tm, K // 2), lambda m: (0, m, 0)) if split_x else row(K),
                pl.BlockSpec((None,) + w.shape[1:], lambda m: (0, 0, 0), pipeline_mode=pl.Buffered(1))]
    if has_res:
        ins.append(res)
        in_specs.append(row(N))
    out_shape = [jax.ShapeDtypeStruct((T, N), F32 if has_bwd else out_dtype)]
    out_specs = [row(N)]
    if has_bwd:
        h, g, dh = bwd
        ins += [h, g.reshape(1, N), dh]
        in_specs += [row(N), pl.BlockSpec((1, N), lambda m: (0, 0)), row(N)]
        out_shape.append(jax.ShapeDtypeStruct((1, N), F32))
        out_specs.append(pl.BlockSpec((1, N), lambda m: (0, 0)))
    out = pl.pallas_call(body, name=name, grid=(T // tm,), in_specs=in_specs, out_specs=out_specs,
                         out_shape=out_shape, compiler_params=_params(("arbitrary",)))(*ins)
    return out if has_bwd else out[0]


def _mm_tn(x, dy, *, name, rows_are_shards=False, split_y=False, tt=1024):
    T, K = x.shape
    N = 2 * dy.shape[-1] if split_y else dy.shape[-1]
    R, C = (K // N_CHIPS, N // 2) if rows_are_shards else (K // 2, N // N_CHIPS)
    nn = 2 if split_y else 1
    tn = N // nn
    per = N_CHIPS // nn
    assert not (rows_are_shards and split_y)
    tt = _row_tile(T, tt)
    nt = T // tt

    def body(x_ref, y_ref, o_ref, acc_ref):
        t = pl.program_id(1)

        @pl.when(t == 0)
        def _():
            acc_ref[...] = jnp.zeros_like(acc_ref)

        acc_ref[...] += lax.dot_general(x_ref[...], y_ref[...].astype(BF16), (((0,), (0,)), ((), ())),
                                        preferred_element_type=F32)

        @pl.when(t == nt - 1)
        def _():
            if rows_are_shards:
                for h in range(2):
                    o_ref[h] = acc_ref[:, h * C:(h + 1) * C].astype(BF16).reshape(N_CHIPS, R, C)
            else:
                for j in range(per):
                    o_ref[:, j] = acc_ref[:, j * C:(j + 1) * C].astype(BF16).reshape(2, R, C)

    if split_y:
        yspec = pl.BlockSpec((None, tt, tn), lambda n, t: (n, t, 0))
    else:
        yspec = pl.BlockSpec((tt, tn), lambda n, t: (t, 0))
    if rows_are_shards:
        out_spec = pl.BlockSpec((2, N_CHIPS, R, C), lambda n, t: (0, 0, 0, 0))
    else:
        out_spec = pl.BlockSpec((2, per, R, C), lambda n, t: (0, n, 0, 0))
    return pl.pallas_call(body, name=name, grid=(nn, nt),
                          in_specs=[pl.BlockSpec((tt, K), lambda n, t: (t, 0)), yspec], out_specs=out_spec,
                          out_shape=jax.ShapeDtypeStruct((2, N_CHIPS, R, C), BF16),
                          scratch_shapes=[pltpu.VMEM((K, tn), F32)],
                          compiler_params=_params(("arbitrary", "arbitrary")))(x, dy)


def _qkv_fwd(h, wq, gq, wkv, gkv, scale, *, tm=512):
    T, D = h.shape
    HD = wq.shape[-1]
    tm = _row_tile(T, tm)

    def body(h_ref, wq_ref, gq_ref, wkv_ref, gkv_ref, q_ref, kv_ref, nq_ref, nkv_ref):
        n = _rms(h_ref[...])[0]
        nq = (n * gq_ref[...]).astype(BF16)
        nkv = (n * gkv_ref[...]).astype(BF16)
        nq_ref[...] = nq
        nkv_ref[...] = nkv
        q_ref[...] = (jnp.dot(nq, wq_ref[...], preferred_element_type=F32) * scale).astype(BF16)
        kv = jnp.dot(nkv, wkv_ref[...], preferred_element_type=F32)
        kv_ref[0] = kv[:, :HD].astype(BF16)
        kv_ref[1] = kv[:, HD:].astype(BF16)

    row = lambda width: pl.BlockSpec((tm, width), lambda i: (i, 0))
    fixed = lambda *shape: pl.BlockSpec(shape, lambda i: (0,) * len(shape))
    weight = lambda n: pl.BlockSpec((None, D, n), lambda i: (0, 0, 0), pipeline_mode=pl.Buffered(1))
    return pl.pallas_call(
        body, name="qkv", grid=(T // tm,),
        in_specs=[row(D), weight(HD), fixed(1, D), weight(2 * HD), fixed(1, D)],
        out_specs=[row(HD), pl.BlockSpec((2, tm, HD), lambda i: (0, i, 0)), row(D), row(D)],
        out_shape=[jax.ShapeDtypeStruct((T, HD), BF16), jax.ShapeDtypeStruct((2, T, HD), BF16),
                   jax.ShapeDtypeStruct((T, D), BF16), jax.ShapeDtypeStruct((T, D), BF16)],
        compiler_params=_params(("arbitrary",)))(h, wq, gq.reshape(1, D), wkv, gkv.reshape(1, D))


def _qkv_dx(dq, wq, gq, dkv, wkv, gkv, h, dh, *, tm=512):
    T, D = h.shape
    HD = wq.shape[-1]
    tm = _row_tile(T, tm)
    nt = (((1,), (1,)), ((), ()))

    def body(dq_ref, wq_ref, gq_ref, dkv_ref, wkv_ref, gkv_ref, h_ref, dh_ref, o_ref, dgq_ref, dgkv_ref):
        @pl.when(pl.program_id(0) == 0)
        def _():
            dgq_ref[...] = jnp.zeros_like(dgq_ref)
            dgkv_ref[...] = jnp.zeros_like(dgkv_ref)

        n, r = _rms(h_ref[...])
        dnq = lax.dot_general(dq_ref[...], wq_ref[...], nt, preferred_element_type=F32)
        dnkv = (lax.dot_general(dkv_ref[0], wkv_ref[:, :HD], nt, preferred_element_type=F32)
                + lax.dot_general(dkv_ref[1], wkv_ref[:, HD:], nt, preferred_element_type=F32))
        dgq_ref[...] += jnp.sum(dnq * n, axis=0, keepdims=True)
        dgkv_ref[...] += jnp.sum(dnkv * n, axis=0, keepdims=True)
        t = dnq * gq_ref[...] + dnkv * gkv_ref[...]
        o_ref[...] = dh_ref[...] + r * (t - n * jnp.mean(t * n, axis=-1, keepdims=True))

    row = lambda width: pl.BlockSpec((tm, width), lambda i: (i, 0))
    fixed = lambda *shape: pl.BlockSpec(shape, lambda i: (0,) * len(shape))
    weight = lambda n: pl.BlockSpec((None, D, n), lambda i: (0, 0, 0), pipeline_mode=pl.Buffered(1))
    return pl.pallas_call(
        body, name="qkv_dx", grid=(T // tm,),
        in_specs=[row(HD), weight(HD), fixed(1, D), pl.BlockSpec((2, tm, HD), lambda i: (0, i, 0)), weight(2 * HD),
                  fixed(1, D), row(D), row(D)],
        out_specs=[row(D), fixed(1, D), fixed(1, D)],
        out_shape=[jax.ShapeDtypeStruct((T, D), F32), jax.ShapeDtypeStruct((1, D), F32),
                   jax.ShapeDtypeStruct((1, D), F32)],
        compiler_params=_params(("arbitrary",)))(dq, wq, gq.reshape(1, D), dkv, wkv, gkv.reshape(1, D), h, dh)


def _chunk_mask():
    i = lax.broadcasted_iota(jnp.int32, (GMLP_BLOCK, GMLP_BLOCK), 0) // CHUNK
    j = lax.broadcasted_iota(jnp.int32, (GMLP_BLOCK, GMLP_BLOCK), 1) // CHUNK
    return i >= j


def _mixer_a_fwd(h, w_in, g, gv, ws, bs_tile, w_out, *, tm=256):
    T, D = h.shape
    W = w_out.shape[-2]
    G = W // GROUP_DIM
    tm = _row_tile(T, tm)

    def body(h_ref, wi_ref, g_ref, gv_ref, ws_ref, bs_ref, wo_ref, o_ref, zp_ref, ga_ref, n_ref):
        nb = (_rms(h_ref[...])[0] * g_ref[...]).astype(BF16)
        n_ref[...] = nb
        zpb = jnp.dot(nb, wi_ref[...], preferred_element_type=F32).astype(BF16)
        zp_ref[...] = zpb
        z = _gelu(zpb.astype(F32))
        u, v = z[:, :W], z[:, W:]
        vn = _rms(v)[0] * gv_ref[...]
        mask = _chunk_mask()
        for gi in range(G):
            cs = slice(gi * GROUP_DIM, (gi + 1) * GROUP_DIM)
            wg = jnp.where(mask, ws_ref[gi], 0.0).astype(BF16)
            for b in range(tm // GMLP_BLOCK):
                rs = slice(b * GMLP_BLOCK, (b + 1) * GMLP_BLOCK)
                s = jnp.dot(wg, vn[rs, cs].astype(BF16), preferred_element_type=F32) + bs_ref[:, cs]
                ga_ref[rs, cs] = (u[rs, cs] * s).astype(BF16)
        o_ref[...] = h_ref[...] + jnp.dot(ga_ref[...], wo_ref[...], preferred_element_type=F32)

    row = lambda width: pl.BlockSpec((tm, width), lambda i: (i, 0))
    fixed = lambda *shape: pl.BlockSpec(shape, lambda i: (0,) * len(shape))
    weight = lambda k, n: pl.BlockSpec((None, k, n), lambda i: (0, 0, 0), pipeline_mode=pl.Buffered(1))
    return pl.pallas_call(
        body, name="mixer_a", grid=(T // tm,),
        in_specs=[row(D), weight(D, 2 * W), fixed(1, D), fixed(1, W), fixed(G, GMLP_BLOCK, GMLP_BLOCK),
                  fixed(GMLP_BLOCK, W), weight(W, D)],
        out_specs=[row(D), row(2 * W), row(W), row(D)],
        out_shape=[jax.ShapeDtypeStruct((T, D), F32), jax.ShapeDtypeStruct((T, 2 * W), BF16),
                   jax.ShapeDtypeStruct((T, W), BF16), jax.ShapeDtypeStruct((T, D), BF16)],
        compiler_params=_params(("arbitrary",)))(h, w_in, g.reshape(1, D), gv, ws, bs_tile, w_out)


def _gate_bwd(zp, d_out, gv, ws, bs_tile, *, tm=256):
    T, W2 = zp.shape
    W = W2 // 2
    G = W // GROUP_DIM
    tm = _row_tile(T, tm)
    nm = T // tm

    def body(zp_ref, do_ref, gv_ref, ws_ref, bs_ref, dzp_ref, dws_ref, dbs_ref, dgv_ref, du_scr, dvn_scr, dsum_scr):
        i = pl.program_id(0)

        @pl.when(i == 0)
        def _():
            dws_ref[...] = jnp.zeros_like(dws_ref)
            dgv_ref[...] = jnp.zeros_like(dgv_ref)
            dsum_scr[...] = jnp.zeros_like(dsum_scr)

        zp = zp_ref[...].astype(F32)
        z, dz = _gelu(zp, with_grad=True)
        u, v = z[:, :W], z[:, W:]
        n, r = _rms(v)
        gv = gv_ref[...]
        vn = n * gv
        d_out = do_ref[...].astype(F32)
        mask = _chunk_mask()
        for g in range(G):
            cs = slice(g * GROUP_DIM, (g + 1) * GROUP_DIM)
            wg = jnp.where(mask, ws_ref[g], 0.0).astype(BF16)
            dw = jnp.zeros((GMLP_BLOCK, GMLP_BLOCK), F32)
            for b in range(tm // GMLP_BLOCK):
                rs = slice(b * GMLP_BLOCK, (b + 1) * GMLP_BLOCK)
                vb = vn[rs, cs].astype(BF16)
                s = jnp.dot(wg, vb, preferred_element_type=F32) + bs_ref[:, cs]
                du_scr[rs, cs] = d_out[rs, cs] * s
                ds = d_out[rs, cs] * u[rs, cs]
                dsb = ds.astype(BF16)
                dvn_scr[rs, cs] = lax.dot_general(wg, dsb, (((0,), (0,)), ((), ())), preferred_element_type=F32)
                dw = dw + lax.dot_general(dsb, vb, (((1,), (1,)), ((), ())), preferred_element_type=F32)
                dsum_scr[:, cs] += ds
            dws_ref[g] += jnp.where(mask, dw, 0.0)
        dvn = dvn_scr[...]
        dgv_ref[...] += jnp.sum(dvn * n, axis=0, keepdims=True)
        t = dvn * gv
        dv = r * (t - n * jnp.mean(t * n, axis=-1, keepdims=True))
        dzp_ref[:, :W] = (du_scr[...] * dz[:, :W]).astype(BF16)
        dzp_ref[:, W:] = (dv * dz[:, W:]).astype(BF16)

        @pl.when(i == nm - 1)
        def _():
            sel = (lax.broadcasted_iota(jnp.int32, (G, W), 1) // GROUP_DIM
                   == lax.broadcasted_iota(jnp.int32, (G, W), 0)).astype(F32)
            dbs_ref[...] = lax.dot_general(sel, dsum_scr[...], (((1,), (1,)), ((), ())),
                                           precision=lax.Precision.HIGHEST, preferred_element_type=F32)

    return pl.pallas_call(
        body, name="gate_bwd", grid=(nm,),
        in_specs=[pl.BlockSpec((tm, W2), lambda i: (i, 0)), pl.BlockSpec((tm, W), lambda i: (i, 0)),
                  pl.BlockSpec((1, W), lambda i: (0, 0)),
                  pl.BlockSpec((G, GMLP_BLOCK, GMLP_BLOCK), lambda i: (0, 0, 0)),
                  pl.BlockSpec((GMLP_BLOCK, W), lambda i: (0, 0))],
        out_specs=[pl.BlockSpec((tm, W2), lambda i: (i, 0)),
                   pl.BlockSpec((G, GMLP_BLOCK, GMLP_BLOCK), lambda i: (0, 0, 0)),
                   pl.BlockSpec((G, GMLP_BLOCK), lambda i: (0, 0)), pl.BlockSpec((1, W), lambda i: (0, 0))],
        out_shape=[jax.ShapeDtypeStruct((T, W2), BF16), jax.ShapeDtypeStruct((G, GMLP_BLOCK, GMLP_BLOCK), F32),
                   jax.ShapeDtypeStruct((G, GMLP_BLOCK), F32), jax.ShapeDtypeStruct((1, W), F32)],
        scratch_shapes=[pltpu.VMEM((tm, W), F32), pltpu.VMEM((tm, W), F32), pltpu.VMEM((GMLP_BLOCK, W), F32)],
        compiler_params=_params(("arbitrary",)))(zp, d_out, gv, ws, bs_tile)


LANES = 128
HALO = 16


def _taps(ext, w, b):
    return w[2:3] * ext[HALO:] + w[1:2] * pltpu.roll(ext, 1, 0)[HALO:] + w[0:1] * pltpu.roll(ext, 2, 0)[HALO:] + b


def _ffn_fwd(h, w, g, cw, cb, wd, S, *, name, loss=None, tm=256):
    T, D = h.shape
    F = w.shape[-1] // 2
    tc = _col_tile(F)
    tm = _row_tile(S, tm)
    has_loss, has_down = loss is not None, wd is not None
    n_in = 5 + has_down + 2 * has_loss

    def body(*refs):
        h_ref, w_ref, g_ref, cw_ref, cb_ref = refs[:5]
        outs, tail = refs[n_in:-1], refs[-1]
        y_ref, a_ref, c_ref, n_ref = outs[has_down:has_down + 4]
        first = (pl.program_id(0) * tm) % S == 0
        nb = (_rms(h_ref[...])[0] * g_ref[...]).astype(BF16)
        n_ref[...] = nb
        for j in range(F // tc):
            cs = slice(j * tc, (j + 1) * tc)
            conv = []
            for s in range(2):
                acc = jnp.dot(nb, w_ref[:, s * F + j * tc:s * F + (j + 1) * tc], preferred_element_type=F32)
                ab = acc.astype(BF16)
                a_ref[s, :, cs] = ab
                af = ab.astype(F32)
                ext = jnp.concatenate([jnp.where(first, 0.0, tail[s, :, cs]), af], axis=0)
                tail[s, :, cs] = af[tm - HALO:, :]
                cv = _taps(ext, cw_ref[s, :, cs], cb_ref[s:s + 1, cs]).astype(BF16)
                c_ref[s, :, cs] = cv
                conv.append(cv.astype(F32))
            up, gate = conv
            y_ref[:, cs] = (gate * jax.nn.sigmoid(gate) * up).astype(BF16)
        if has_down:
            out = h_ref[...] + jnp.dot(y_ref[...], refs[5][...], preferred_element_type=F32)
            if has_loss:
                _loss_epilogue(out, refs[6], refs[7], outs[0], outs[5], outs[6], pl.program_id(0) == 0)
            else:
                outs[0][...] = out

    row = lambda width: pl.BlockSpec((tm, width), lambda i: (i, 0))
    wide = pl.BlockSpec((2, tm, F), lambda i: (0, i, 0))
    fixed = lambda *shape: pl.BlockSpec(shape, lambda i: (0,) * len(shape))
    once = pl.Buffered(1)
    ins = [h, w, g.reshape(1, D), cw, cb]
    in_specs = [row(D), pl.BlockSpec((None, D, 2 * F), lambda i: (0, 0, 0), pipeline_mode=once), fixed(1, D),
                fixed(2, 3, F), fixed(2, F)]
    out_specs = [row(F), wide, wide, row(D)]
    out_shape = [jax.ShapeDtypeStruct((T, F), BF16), jax.ShapeDtypeStruct((2, T, F), BF16),
                 jax.ShapeDtypeStruct((2, T, F), BF16), jax.ShapeDtypeStruct((T, D), BF16)]
    if has_down:
        ins.append(wd)
        in_specs.append(pl.BlockSpec((None, F, D), lambda i: (0, 0, 0), pipeline_mode=once))
        out_specs.insert(0, row(D))
        out_shape.insert(0, jax.ShapeDtypeStruct((T, D), F32))
    if has_loss:
        ins += [loss[0].reshape(1, D), loss[1]]
        in_specs += [fixed(1, D), row(D)]
        out_specs += [fixed(8, 128), fixed(1, D)]
        out_shape += [jax.ShapeDtypeStruct((8, 128), F32), jax.ShapeDtypeStruct((1, D), F32)]
    return pl.pallas_call(body, name=name, grid=(T // tm,), in_specs=in_specs, out_specs=out_specs,
                          out_shape=out_shape, scratch_shapes=[pltpu.VMEM((2, HALO, F), F32)],
                          compiler_params=_params(("arbitrary",)))(*ins)


def _conv_bwd(a, c, dy, cw, S, *, tm=256):
    _, T, F = a.shape
    tc = _col_tile(F)
    tm = _row_tile(S, tm)
    nm = T // tm
    hb = tm // HALO
    TE = tm + HALO
    nxt = lambda j, i: jnp.minimum((i + 1) * hb, T // HALO - 1)

    def body(a_ref, c_ref, nc_ref, dy_ref, ndy_ref, w_ref, da_ref, dw_ref, db_ref):
        i = pl.program_id(1)
        last = ((i + 1) * tm) % S == 0
        keep_n = jnp.where(last, 0.0, 1.0)

        @pl.when(i == 0)
        def _():
            dw_ref[...] = jnp.zeros_like(dw_ref)
            db_ref[...] = jnp.zeros_like(db_ref)

        for j in range(tc // LANES):
            cs = slice(j * LANES, (j + 1) * LANES)
            dyf = jnp.concatenate([dy_ref[:, cs].astype(F32), ndy_ref[:, cs].astype(F32) * keep_n], axis=0)
            up = jnp.concatenate([c_ref[0, :, cs].astype(F32), nc_ref[0, :, cs].astype(F32)], axis=0)
            gate = jnp.concatenate([c_ref[1, :, cs].astype(F32), nc_ref[1, :, cs].astype(F32)], axis=0)
            sg = jax.nn.sigmoid(gate)
            for s, d in ((0, dyf * (gate * sg)), (1, dyf * up * (sg * (1.0 + gate * (1.0 - sg))))):
                a = a_ref[s, :, cs].astype(F32)
                w = w_ref[s, :, cs]
                u1, u2 = pltpu.roll(d, TE - 1, 0), pltpu.roll(d, TE - 2, 0)
                db_ref[s:s + 1, cs] += jnp.sum(d[:tm], axis=0, keepdims=True)
                dw_ref[s, 2:3, cs] += jnp.sum(d[:tm] * a, axis=0, keepdims=True)
                dw_ref[s, 1:2, cs] += jnp.sum(u1[:tm] * a, axis=0, keepdims=True)
                dw_ref[s, 0:1, cs] += jnp.sum(u2[:tm] * a, axis=0, keepdims=True)
                da_ref[s, :, cs] = (w[2:3] * d + w[1:2] * u1 + w[0:1] * u2)[:tm].astype(BF16)

    cur = pl.BlockSpec((2, tm, tc), lambda j, i: (0, i, j))
    return pl.pallas_call(
        body, name="conv_bwd", grid=(F // tc, nm),
        in_specs=[cur, cur, pl.BlockSpec((2, HALO, tc), lambda j, i: (0, nxt(j, i), j)),
                  pl.BlockSpec((tm, tc), lambda j, i: (i, j)), pl.BlockSpec((HALO, tc), lambda j, i: (nxt(j, i), j)),
                  pl.BlockSpec((2, 3, tc), lambda j, i: (0, 0, j))],
        out_specs=[cur, pl.BlockSpec((2, 3, tc), lambda j, i: (0, 0, j)), pl.BlockSpec((2, tc), lambda j, i: (0, j))],
        out_shape=[jax.ShapeDtypeStruct((2, T, F), BF16), jax.ShapeDtypeStruct((2, 3, F), F32),
                   jax.ShapeDtypeStruct((2, F), F32)],
        compiler_params=_params(("arbitrary", "arbitrary")))(a, c, c, dy, dy, cw)


def _bias_index():
    idx = np.arange(F_LEN)
    d = np.where(idx < K_SPAN, idx, idx - F_LEN)
    return np.clip(PAD - d, -REL_CLIP, REL_CLIP) + REL_CLIP


ROW_GROUP = 16


def _roll_rows(x, sign, unit, steps):
    rows = lax.broadcasted_iota(jnp.int32, x.shape, 0)
    step = 1
    while step < steps:
        shift = unit * step if sign > 0 else F_LEN - unit * step
        x = jnp.where((rows & step) != 0, pltpu.roll(x, shift, 1), x)
        step *= 2
    return x


def _bias_expand(frow):
    H = frow.shape[0]
    groups = Q_BLOCK // ROW_GROUP

    def body(f_ref, o_ref):
        coarse = _roll_rows(jnp.broadcast_to(f_ref[...], (groups, F_LEN)), 1, ROW_GROUP, groups)
        x = jnp.concatenate([jnp.broadcast_to(coarse[a:a + 1], (ROW_GROUP, F_LEN)) for a in range(groups)], axis=0)
        x = _roll_rows(x, 1, 1, ROW_GROUP)[:, :K_SPAN]
        qc = lax.broadcasted_iota(jnp.int32, (Q_BLOCK, K_SPAN), 0) // CHUNK * CHUNK
        kj = lax.broadcasted_iota(jnp.int32, (Q_BLOCK, K_SPAN), 1)
        o_ref[...] = jnp.where((kj >= qc) & (kj < qc + PAD + CHUNK), x, NEG_INF)

    return pl.pallas_call(
        body, name="bias_expand", grid=(H,),
        in_specs=[pl.BlockSpec((None, 1, F_LEN), lambda h: (h, 0, 0))],
        out_specs=pl.BlockSpec((None, Q_BLOCK, K_SPAN), lambda h: (h, 0, 0)),
        out_shape=jax.ShapeDtypeStruct((H, Q_BLOCK, K_SPAN), F32), compiler_params=_params(("arbitrary",)))(frow)


def _bias_reduce(dbias, n_rel):
    H = dbias.shape[0]
    onehot = jnp.asarray((_bias_index()[:, None] == np.arange(n_rel)[None, :]).astype(np.float32), dtype=BF16)

    def body(d_ref, oh_ref, o_ref):
        x = jnp.concatenate([d_ref[...], jnp.zeros((Q_BLOCK, F_LEN - K_SPAN), F32)], axis=1)
        fine = _roll_rows(x, -1, 1, ROW_GROUP).reshape(Q_BLOCK // ROW_GROUP, ROW_GROUP, F_LEN)
        coarse = _roll_rows(jnp.sum(fine, axis=1), -1, ROW_GROUP, Q_BLOCK // ROW_GROUP)
        row = jnp.broadcast_to(jnp.sum(coarse, axis=0, keepdims=True), (8, F_LEN))
        acc = jnp.zeros((8, n_rel), F32)
        for _ in range(3):
            piece = row.astype(BF16)
            acc = acc + jnp.dot(piece, oh_ref[...], preferred_element_type=F32)
            row = row - piece.astype(F32)
        o_ref[...] = acc[0:1]

    return pl.pallas_call(
        body, name="bias_reduce", grid=(H,),
        in_specs=[pl.BlockSpec((None, Q_BLOCK, K_SPAN), lambda h: (h, 0, 0)),
                  pl.BlockSpec((F_LEN, n_rel), lambda h: (0, 0))],
        out_specs=pl.BlockSpec((None, 1, n_rel), lambda h: (h, 0, 0)),
        out_shape=jax.ShapeDtypeStruct((H, 1, n_rel), F32), compiler_params=_params(("arbitrary",)))(dbias, onehot)


def _attn_specs(S):
    hw = HEADS_PER_STEP * HEAD_DIM
    qspec = pl.BlockSpec((None, Q_BLOCK, hw), lambda g, b, i: (b, i, g))
    kspec = pl.BlockSpec((None, None, S, hw), lambda g, b, i: (0, b, 0, g))
    vspec = pl.BlockSpec((None, None, S, hw), lambda g, b, i: (1, b, 0, g))
    bspec = pl.BlockSpec((HEADS_PER_STEP, Q_BLOCK, K_SPAN), lambda g, b, i: (g, 0, 0))
    return hw, qspec, kspec, vspec, bspec


def _span_cases(i, fn):
    short = PAD // Q_BLOCK
    for j in range(short):
        pl.when(i == j)(functools.partial(fn, PAD - j * Q_BLOCK))
    pl.when(i >= short)(functools.partial(fn, 0))


def _key_start(i, off):
    return 0 if off else pl.multiple_of(i * Q_BLOCK - PAD, Q_BLOCK)


def _attn_exp(q_ref, k_ref, b_ref, h, k0, off):
    hs = slice(h * HEAD_DIM, (h + 1) * HEAD_DIM)
    kh = k_ref[pl.ds(k0, K_SPAN - off), hs]
    s = lax.dot_general(q_ref[:, hs], kh, (((1,), (1,)), ((), ())), preferred_element_type=F32) + b_ref[h, :, off:]
    p = jnp.exp(s - jnp.max(s, axis=-1, keepdims=True))
    return p, 1.0 / jnp.sum(p, axis=-1, keepdims=True), kh


def _attn_fwd(q, kv, bias, B, S):
    HD = q.shape[-1]
    hw, qspec, kspec, vspec, bspec = _attn_specs(S)

    def body(q_ref, k_ref, v_ref, b_ref, o_ref):
        i = pl.program_id(2)

        def block(off):
            k0 = _key_start(i, off)
            outs = []
            for h in range(HEADS_PER_STEP):
                hs = slice(h * HEAD_DIM, (h + 1) * HEAD_DIM)
                p, inv, _ = _attn_exp(q_ref, k_ref, b_ref, h, k0, off)
                outs.append(jnp.dot(p.astype(BF16), v_ref[pl.ds(k0, K_SPAN - off), hs],
                                    preferred_element_type=F32) * inv)
            o_ref[...] = jnp.concatenate(outs, axis=1).astype(BF16)

        _span_cases(i, block)

    return pl.pallas_call(
        body, name="attn_fwd", grid=(HD // hw, B, S // Q_BLOCK), in_specs=[qspec, kspec, vspec, bspec],
        out_specs=qspec, out_shape=jax.ShapeDtypeStruct((B, S, HD), BF16),
        compiler_params=_params(("arbitrary", "arbitrary", "arbitrary")))(q, kv, kv, bias)


def _attn_bwd(q, kv, bias, do, B, S):
    HD = q.shape[-1]
    H = HD // HEAD_DIM
    hw, qspec, kspec, vspec, bspec = _attn_specs(S)
    scale = HEAD_DIM ** -0.5
    nq = S // Q_BLOCK

    def body(q_ref, k_ref, v_ref, b_ref, do_ref, dq_ref, dkv_ref, db_ref, dk_acc, dv_acc):
        b, i = pl.program_id(1), pl.program_id(2)

        @pl.when(i == 0)
        def _():
            dk_acc[...] = jnp.zeros_like(dk_acc)
            dv_acc[...] = jnp.zeros_like(dv_acc)

        @pl.when((i == 0) & (b == 0))
        def _():
            db_ref[...] = jnp.zeros_like(db_ref)

        def block(off):
            k0 = _key_start(i, off)
            keys = pl.ds(k0, K_SPAN - off)
            for h in range(HEADS_PER_STEP):
                hs = slice(h * HEAD_DIM, (h + 1) * HEAD_DIM)
                p, inv, kh = _attn_exp(q_ref, k_ref, b_ref, h, k0, off)
                p = p * inv
                doh = do_ref[:, hs]
                dp = lax.dot_general(doh, v_ref[keys, hs], (((1,), (1,)), ((), ())), preferred_element_type=F32)
                ds = p * (dp - jnp.sum(p * dp, axis=-1, keepdims=True))
                db_ref[h, :, off:] += ds
                dsb = ds.astype(BF16)
                dq_ref[:, hs] = (jnp.dot(dsb, kh, preferred_element_type=F32) * scale).astype(BF16)
                dk_acc[hs, keys] += lax.dot_general(q_ref[:, hs], dsb, (((0,), (0,)), ((), ())),
                                                     preferred_element_type=F32)
                dv_acc[hs, keys] += lax.dot_general(doh, p.astype(BF16), (((0,), (0,)), ((), ())),
                                                     preferred_element_type=F32)

        _span_cases(i, block)

        @pl.when(i == nq - 1)
        def _():
            dkv_ref[0] = dk_acc[...].T.astype(BF16)
            dkv_ref[1] = dv_acc[...].T.astype(BF16)

    return pl.pallas_call(
        body, name="attn_bwd", grid=(HD // hw, B, nq), in_specs=[qspec, kspec, vspec, bspec, qspec],
        out_specs=[qspec, pl.BlockSpec((2, None, S, hw), lambda g, b, i: (0, b, 0, g)), bspec],
        out_shape=[jax.ShapeDtypeStruct((B, S, HD), BF16), jax.ShapeDtypeStruct((2, B, S, HD), BF16),
                   jax.ShapeDtypeStruct((H, Q_BLOCK, K_SPAN), F32)],
        scratch_shapes=[pltpu.VMEM((hw, S), F32), pltpu.VMEM((hw, S), F32)],
        compiler_params=_params(("arbitrary", "arbitrary", "arbitrary")))(q, kv, kv, bias, do)


def _sub_rows(R):
    for cand in (256, 352, 128, 64, 8):
        if R % cand == 0 and R > cand:
            return cand
    return R


def _adamw(w, g, m, v, *, name):
    R, C = w.shape
    tr = _sub_rows(R)

    def body(w_ref, g_ref, m_ref, v_ref, d_ref, nm_ref, nv_ref):
        g = g_ref[...]
        m = ADAM_B1 * m_ref[...] + (1.0 - ADAM_B1) * g
        v = ADAM_B2 * v_ref[...] + (1.0 - ADAM_B2) * (g * g)
        m_hat = m / (1.0 - ADAM_B1 ** ADAM_STEP)
        v_hat = v / (1.0 - ADAM_B2 ** ADAM_STEP)
        d_ref[...] = -ADAM_LR * (m_hat / (jnp.sqrt(v_hat) + ADAM_EPS) + ADAM_WD * w_ref[...])
        nm_ref[...] = m
        nv_ref[...] = v

    spec = pl.BlockSpec((tr, C), lambda i: (i, 0))
    return pl.pallas_call(body, name=name, grid=(R // tr,), in_specs=[spec] * 4, out_specs=[spec] * 3,
                          out_shape=[jax.ShapeDtypeStruct((R, C), F32)] * 3,
                          compiler_params=_params(("arbitrary",)))(w, g, m, v)


def _add_pair(units, got, core, *, name):
    n4, R, C = got.shape
    rows = n4 * R
    tr = 512 if rows % 512 == 0 else R

    def body(c_ref, u_ref, got_ref, o_ref):
        o_ref[...] = (u_ref[...].astype(F32) + got_ref[...].astype(F32)).astype(BF16)

    spec = pl.BlockSpec((tr, C), lambda i, c: (i, 0))
    grid_spec = pltpu.PrefetchScalarGridSpec(
        num_scalar_prefetch=1, grid=(rows // tr,),
        in_specs=[pl.BlockSpec((None, tr, C), lambda i, c: (c[0], i, 0)), spec], out_specs=spec)
    out = pl.pallas_call(body, name=name, grid_spec=grid_spec, out_shape=jax.ShapeDtypeStruct((rows, C), BF16),
                         compiler_params=_params(("arbitrary",)))(core.reshape(1), units.reshape(2, rows, C),
                                                                   got.reshape(rows, C))
    return out.reshape(n4, R, C)


def _sum_chips(w, own, got, pos, *, name, layer=0, into=None):
    _, R, C = own.shape
    tr = _sub_rows(R)
    nr = R // tr

    def body(p_ref, own_ref, got_ref, *rest):
        o_ref = rest[-1]
        o_ref[...] = (own_ref[...].astype(F32) + got_ref[0].astype(F32) + got_ref[1].astype(F32)
                      + got_ref[2].astype(F32))

    if w.row_sharded:
        out_map = lambda i, p: (layer, i, p[1])
    else:
        out_map = lambda i, p: (layer, p[1] * nr + i, 0)
    ins = [pos, own, got]
    in_specs = [pl.BlockSpec((None, tr, C), lambda i, p: (p[0], i, 0)),
                pl.BlockSpec((3, tr, C), lambda i, p: (0, i, 0))]
    alias = {}
    if into is not None:
        ins.append(into)
        in_specs.append(ANY)
        alias = {3: 0}
    grid_spec = pltpu.PrefetchScalarGridSpec(num_scalar_prefetch=1, grid=(nr,), in_specs=in_specs,
                                             out_specs=pl.BlockSpec((None, tr, C), out_map))
    return pl.pallas_call(body, name=name, grid_spec=grid_spec, input_output_aliases=alias,
                          out_shape=jax.ShapeDtypeStruct((w.L, w.ks, w.ns), F32),
                          compiler_params=_params(("arbitrary",)))(*ins)


def _mesh_pos():
    return lax.axis_index("x"), lax.axis_index("y"), lax.axis_index("c")


def _other_chips(x, y):
    return [(1 - x, y), (x, 1 - y), (1 - x, 1 - y)]


ANY = pl.BlockSpec(memory_space=pl.ANY)


class _W:
    def __init__(self, name, shard, row_sharded, direct=False):
        self.name = name
        self.direct = direct
        self.L, ks, ns = shard.shape
        self.row_sharded = row_sharded
        self.K, self.N = (ks * N_CHIPS, ns) if row_sharded else (ks, ns * N_CHIPS)
        self.ks, self.ns = ks, ns

    def shard_of(self, full, j):
        if self.row_sharded:
            return full.at[:, pl.ds(j * self.ks, self.ks), :]
        return full.at[:, :, pl.ds(j * self.ns, self.ns)]

    def half_of(self, shard, c):
        if self.row_sharded:
            return shard.at[:, :, pl.ds(c * (self.ns // 2), self.ns // 2)]
        return shard.at[:, pl.ds(c * (self.ks // 2), self.ks // 2), :]


HBM = pl.BlockSpec(memory_space=pltpu.HBM)
SEM = pl.BlockSpec(memory_space=pltpu.SEMAPHORE)
IN_FLIGHT = pltpu.SideEffectType.DATAFLOW_SIDE_EFFECTING


def _in_hbm(a):
    return pltpu.with_memory_space_constraint(a, pltpu.HBM)


def _gather_start(ws, shards, after, *, name):
    nw = len(ws)

    def body(*refs):
        src, dst = refs[:nw], refs[nw:2 * nw]
        send, recv = refs[2 * nw + 1:3 * nw + 1], refs[3 * nw + 1:4 * nw + 1]
        x, y, c = _mesh_pos()
        me = 2 * x + y
        for i, w in enumerate(ws):
            for f, (px, py) in enumerate(_other_chips(x, y)):
                for e in range(2 if w.direct else 1):
                    k = 2 * f + e
                    pltpu.make_async_remote_copy(
                        src_ref=w.half_of(src[i], c), dst_ref=w.half_of(w.shard_of(dst[i], me), c),
                        send_sem=send[i].at[k], recv_sem=recv[i].at[k], device_id=(px, py, c if e == 0 else 1 - c),
                        device_id_type=MESH).start()

    fulls = [lax.empty((w.L, w.K, w.N), BF16) for w in ws]
    out = pl.pallas_call(
        body, name=name, in_specs=[HBM] * (2 * nw) + [ANY],
        out_specs=[SEM] * (2 * nw) + [HBM] * (2 * nw),
        out_shape=[pltpu.SemaphoreType.DMA((6,))] * (2 * nw)
        + [pltpu.HBM(s.shape, BF16) for s in shards] + [pltpu.HBM(f.shape, BF16) for f in fulls],
        input_output_aliases={i: 2 * nw + i for i in range(2 * nw)},
        compiler_params=pltpu.CompilerParams(has_side_effects=IN_FLIGHT))(
            *[_in_hbm(s) for s in shards], *[_in_hbm(f) for f in fulls], after)
    return [(out[i], out[nw + i], out[2 * nw + i], out[3 * nw + i]) for i in range(nw)]


def _gather_wait(ws, flight, after, *, name):
    nw = len(ws)

    def body(*refs):
        src, dst = refs[:nw], refs[nw:2 * nw]
        send, recv = refs[2 * nw:3 * nw], refs[3 * nw:4 * nw]
        x, y, c = _mesh_pos()
        for i, w in enumerate(ws):
            for f, (px, py) in enumerate(_other_chips(x, y)):
                for e in range(2 if w.direct else 1):
                    k = 2 * f + e
                    landed = w.half_of(w.shard_of(dst[i], 2 * px + py), c if e == 0 else 1 - c)
                    cp = pltpu.make_async_remote_copy(
                        src_ref=w.half_of(src[i], c), dst_ref=landed, send_sem=send[i].at[k], recv_sem=recv[i].at[k],
                        device_id=(px, py, c), device_id_type=MESH)
                    cp.wait_send()
                    cp.wait_recv()

    shards, fulls = [fl[2] for fl in flight], [fl[3] for fl in flight]
    out = pl.pallas_call(
        body, name=name, in_specs=[HBM] * (2 * nw) + [SEM] * (2 * nw) + [ANY],
        out_specs=[HBM] * (2 * nw),
        out_shape=[pltpu.HBM(s.shape, BF16) for s in shards] + [pltpu.HBM(f.shape, BF16) for f in fulls],
        input_output_aliases={i: i for i in range(2 * nw)},
        compiler_params=pltpu.CompilerParams(has_side_effects=IN_FLIGHT))(
            *shards, *fulls, *[fl[0] for fl in flight], *[fl[1] for fl in flight], after)
    return out[:nw], out[nw:]


def _gather_finish(ws, shards, fulls, *, name):
    nw = len(ws)
    forward = not ws[0].direct

    def body(*refs):
        src, dst, stage = refs[:nw], refs[3 * nw:4 * nw], refs[4 * nw:5 * nw]
        send_sems, recv_sems, load_sems, store_sems = refs[5 * nw:]
        x, y, c = _mesh_pos()
        me = 2 * x + y
        sibling = (x, y, 1 - c)
        chips = _other_chips(x, y)

        def fwd(i, w, f, half):
            px, py = chips[f]
            landed = w.half_of(w.shard_of(dst[i], 2 * px + py), half)
            return pltpu.make_async_remote_copy(src_ref=landed, dst_ref=landed, send_sem=send_sems.at[3 * i + f],
                                                recv_sem=recv_sems.at[3 * i + f], device_id=sibling,
                                                device_id_type=MESH)

        loads = [pltpu.make_async_copy(src[i], stage[i], load_sems.at[i]) for i in range(nw)]
        for cp in loads:
            cp.start()
        sends = [fwd(i, w, f, c) for i, w in enumerate(ws) for f in range(3)] if forward else []
        for cp in sends:
            cp.start()
        stores = [pltpu.make_async_copy(stage[i], w.shard_of(dst[i], me), store_sems.at[i])
                  for i, w in enumerate(ws)]
        for ld, st in zip(loads, stores):
            ld.wait()
            st.start()
        if forward:
            for i, w in enumerate(ws):
                for f in range(3):
                    fwd(i, w, f, 1 - c).wait_recv()
        for cp in sends:
            cp.wait_send()
        for cp in stores:
            cp.wait()

    out = pl.pallas_call(
        body, name=name, in_specs=[ANY] * (2 * nw), out_specs=[ANY] * (2 * nw),
        out_shape=[jax.ShapeDtypeStruct(s.shape, BF16) for s in shards]
        + [jax.ShapeDtypeStruct(f.shape, BF16) for f in fulls],
        input_output_aliases={i: i for i in range(2 * nw)},
        scratch_shapes=[pltpu.VMEM((w.L, w.ks, w.ns), BF16) for w in ws]
        + [pltpu.SemaphoreType.DMA((3 * nw,)), pltpu.SemaphoreType.DMA((3 * nw,)), pltpu.SemaphoreType.DMA((nw,)),
           pltpu.SemaphoreType.DMA((nw,))],
        compiler_params=_params(has_side_effects=True))(*shards, *fulls)
    return out[nw:]


def _split_copies(name, srcs, lands, n_sems, copies_of, *, flight=None, after=None):
    n = len(srcs)
    starting = flight is None

    def body(*refs):
        src, land = refs[:n], refs[n:2 * n]
        sems = refs[2 * n + 1:4 * n + 1] if starting else refs[2 * n:4 * n]
        for i in range(n):
            for cp in copies_of(i, src[i], land[i], sems[i], sems[n + i]):
                if starting:
                    cp.start()
                else:
                    cp.wait_send()
                    cp.wait_recv()

    thru = [pltpu.HBM(a.shape, a.dtype) for a in list(srcs) + list(lands)]
    if starting:
        out = pl.pallas_call(
            body, name=name, in_specs=[HBM] * (2 * n) + [ANY], out_specs=[SEM] * (2 * n) + [HBM] * (2 * n),
            out_shape=[pltpu.SemaphoreType.DMA((n_sems,))] * (2 * n) + thru,
            input_output_aliases={i: 2 * n + i for i in range(2 * n)},
            compiler_params=pltpu.CompilerParams(has_side_effects=IN_FLIGHT))(
                *[_in_hbm(a) for a in srcs], *[_in_hbm(a) for a in lands], after)
        return [(out[i], out[n + i], out[2 * n + i], out[3 * n + i]) for i in range(n)]
    out = pl.pallas_call(
        body, name=name, in_specs=[HBM] * (2 * n) + [SEM] * (2 * n) + [ANY], out_specs=[HBM] * (2 * n),
        out_shape=thru, input_output_aliases={i: i for i in range(2 * n)},
        compiler_params=pltpu.CompilerParams(has_side_effects=IN_FLIGHT))(
            *srcs, *lands, *[fl[0] for fl in flight], *[fl[1] for fl in flight], after)
    return out[:n], out[n:]


def _sum8(land, vec, me):
    R = vec.shape[0]

    def body(me_ref, land_ref, vec_ref, o_ref):
        acc = jnp.zeros((R, 128), F32)
        for d in range(8):
            acc = acc + jnp.where(me_ref[0] == d, vec_ref[...], land_ref[d])
        o_ref[...] = acc

    grid_spec = pltpu.PrefetchScalarGridSpec(
        num_scalar_prefetch=1, grid=(1,),
        in_specs=[pl.BlockSpec((8, R, 128), lambda i, m: (0, 0, 0)), pl.BlockSpec((R, 128), lambda i, m: (0, 0))],
        out_specs=pl.BlockSpec((R, 128), lambda i, m: (0, 0)))
    return pl.pallas_call(body, name="sum8", grid_spec=grid_spec, out_shape=jax.ShapeDtypeStruct((R, 128), F32),
                          compiler_params=_params(("arbitrary",)))(me.reshape(1), land, vec)


def _swap_copies(i, src, got, send, recv):
    x, y, c = _mesh_pos()
    return [pltpu.make_async_remote_copy(src_ref=src.at[1 - c], dst_ref=got, send_sem=send.at[0], recv_sem=recv.at[0],
                                         device_id=(x, y, 1 - c), device_id_type=MESH)]


def _gather8_copies(i, src, land, send, recv):
    x, y, c = _mesh_pos()
    me = 4 * x + 2 * y + c
    peers = [(x, y, 1 - c)] + [(px, py, pc) for px, py in _other_chips(x, y) for pc in (c, 1 - c)]
    return [pltpu.make_async_remote_copy(src_ref=src, dst_ref=land.at[me], send_sem=send.at[k], recv_sem=recv.at[k],
                                         device_id=peer, device_id_type=MESH) for k, peer in enumerate(peers)]


def _scatter_copy(src, got, send, recv, f, chip, c):
    px, py = chip
    return pltpu.make_async_remote_copy(src_ref=src.at[2 * px + py], dst_ref=got.at[f], send_sem=send.at[f],
                                        recv_sem=recv.at[f], device_id=(px, py, c), device_id_type=MESH)


def _scatter_start(sums, *, name):
    nw = len(sums)

    def body(*refs):
        src, got = refs[:nw], refs[nw:2 * nw]
        send, recv = refs[2 * nw:3 * nw], refs[3 * nw:4 * nw]
        x, y, c = _mesh_pos()
        for i in range(nw):
            for f, chip in enumerate(_other_chips(x, y)):
                _scatter_copy(src[i], got[i], send[i], recv[i], f, chip, c).start()

    lands = [lax.empty((3,) + s.shape[1:], BF16) for s in sums]
    out = pl.pallas_call(
        body, name=name, in_specs=[HBM] * (2 * nw), out_specs=[SEM] * (2 * nw) + [HBM] * (2 * nw),
        out_shape=[pltpu.SemaphoreType.DMA((3,))] * (2 * nw)
        + [pltpu.HBM(s.shape, BF16) for s in sums] + [pltpu.HBM(l.shape, BF16) for l in lands],
        input_output_aliases={i: 2 * nw + i for i in range(2 * nw)},
        compiler_params=pltpu.CompilerParams(has_side_effects=IN_FLIGHT))(
            *[_in_hbm(s) for s in sums], *[_in_hbm(l) for l in lands])
    return [(out[i], out[nw + i], out[2 * nw + i], out[3 * nw + i]) for i in range(nw)]


def _scatter_wait(flight, after):
    nw = len(flight)

    def body(*refs):
        src, got = refs[:nw], refs[nw:2 * nw]
        send, recv = refs[2 * nw:3 * nw], refs[3 * nw:4 * nw]
        x, y, c = _mesh_pos()
        for i in range(nw):
            for f, chip in enumerate(_other_chips(x, y)):
                cp = _scatter_copy(src[i], got[i], send[i], recv[i], f, chip, c)
                cp.wait_send()
                cp.wait_recv()

    sums, lands = [fl[2] for fl in flight], [fl[3] for fl in flight]
    out = pl.pallas_call(
        body, name="scatter_wait", in_specs=[HBM] * (2 * nw) + [SEM] * (2 * nw) + [ANY], out_specs=[HBM] * (2 * nw),
        out_shape=[pltpu.HBM(s.shape, BF16) for s in sums] + [pltpu.HBM(l.shape, BF16) for l in lands],
        input_output_aliases={i: i for i in range(2 * nw)},
        compiler_params=pltpu.CompilerParams(has_side_effects=IN_FLIGHT))(
            *sums, *lands, *[fl[0] for fl in flight], *[fl[1] for fl in flight], after)
    return out[:nw], out[nw:]


def _join_halves(ws, shards):
    nw = len(ws)

    def body(*refs):
        buf = refs[nw:2 * nw]
        send_sems, recv_sems = refs[2 * nw:]
        x, y, c = _mesh_pos()
        sibling = (x, y, 1 - c)

        def copy(i, w, half):
            region = w.half_of(buf[i], half)
            return pltpu.make_async_remote_copy(src_ref=region, dst_ref=region, send_sem=send_sems.at[i],
                                                recv_sem=recv_sems.at[i], device_id=sibling, device_id_type=MESH)

        sends = [copy(i, w, c) for i, w in enumerate(ws)]
        for cp in sends:
            cp.start()
        for i, w in enumerate(ws):
            copy(i, w, 1 - c).wait_recv()
        for cp in sends:
            cp.wait_send()

    return pl.pallas_call(
        body, name="join_halves", in_specs=[ANY] * nw, out_specs=[ANY] * nw,
        out_shape=[jax.ShapeDtypeStruct((w.L, w.ks, w.ns), F32) for w in ws],
        input_output_aliases={i: i for i in range(nw)},
        scratch_shapes=[pltpu.SemaphoreType.DMA((nw,)), pltpu.SemaphoreType.DMA((nw,))],
        compiler_params=_params(has_side_effects=True))(*shards)


def _allreduce_small(vec):
    R = vec.shape[0]

    def body(x_ref, o_ref, buf, send_sems, recv_sems):
        x, y, c = _mesh_pos()
        me, sibling = (x, y, c), (x, y, 1 - c)
        chips = _other_chips(x, y)

        def slot(px, py, pc):
            return buf.at[4 * px + 2 * py + pc]

        def copy(k, block, to, src=None):
            return pltpu.make_async_remote_copy(src_ref=slot(*block) if src is None else src, dst_ref=slot(*block),
                                                send_sem=send_sems.at[k], recv_sem=recv_sems.at[k], device_id=to,
                                                device_id_type=MESH)

        first = [copy(0, me, sibling, src=x_ref)] + [copy(1 + f, me, (*chip, c), src=x_ref)
                                                     for f, chip in enumerate(chips)]
        for cp in first:
            cp.start()
        passed = [copy(4 + f, (*chip, c), sibling) for f, chip in enumerate(chips)]
        for f, chip in enumerate(chips):
            copy(1 + f, (*chip, c), me).wait_recv()
            passed[f].start()
        copy(0, sibling, me).wait_recv()
        for f, chip in enumerate(chips):
            copy(4 + f, (*chip, 1 - c), me).wait_recv()
        for cp in first + passed:
            cp.wait_send()
        slot(*me)[...] = x_ref[...]
        acc = buf[0]
        for d in range(1, 8):
            acc = acc + buf[d]
        o_ref[...] = acc

    return pl.pallas_call(
        body, name="allreduce_small", in_specs=[pl.BlockSpec(memory_space=pltpu.VMEM)],
        out_specs=pl.BlockSpec(memory_space=pltpu.VMEM), out_shape=jax.ShapeDtypeStruct((R, 128), F32),
        scratch_shapes=[pltpu.VMEM((8, R, 128), F32), pltpu.SemaphoreType.DMA((7,)), pltpu.SemaphoreType.DMA((7,))],
        compiler_params=_params())(vec)


def _pack(parts):
    flat = jnp.concatenate([p.reshape(-1).astype(F32) for p in parts])
    n = flat.shape[0]
    pad = (-n) % (64 * 128)
    return jnp.pad(flat, (0, pad)).reshape(-1, 128)


def _unpack(vec, shapes):
    flat = vec.reshape(-1)
    out, off = [], 0
    for s in shapes:
        n = int(np.prod(s))
        out.append(flat[off:off + n].reshape(s))
        off += n
    return out


def kernel(x, a_norm_g, a_w_in, a_v_norm_g, a_w_s, a_b_s, a_w_out, kv_norm_g, w_kv, b_norm_g, b_w_q, b_rel_bias, b_w_o, f_norm_g, f_w_in, f_conv_w, f_conv_b, f_w_down, final_norm_g, loss_target, m_a_norm_g, m_a_w_in, m_a_v_norm_g, m_a_w_s, m_a_b_s, m_a_w_out, m_kv_norm_g, m_w_kv, m_b_norm_g, m_b_w_q, m_b_rel_bias, m_b_w_o, m_f_norm_g, m_f_w_in, m_f_conv_w, m_f_conv_b, m_f_w_down, m_final_norm_g, v_a_norm_g, v_a_w_in, v_a_v_norm_g, v_a_w_s, v_a_b_s, v_a_w_out, v_kv_norm_g, v_w_kv, v_b_norm_g, v_b_w_q, v_b_rel_bias, v_b_w_o, v_f_norm_g, v_f_w_in, v_f_conv_w, v_f_conv_b, v_f_w_down, v_final_norm_g):
    B, S, D = x.shape
    T = B * S
    xi, yi, ci = lax.axis_index("x"), lax.axis_index("y"), lax.axis_index("c")
    j_me = (2 * xi + yi).astype(jnp.int32)
    core = ci.astype(jnp.int32)
    pos = jnp.stack([j_me, core])

    w_shards = {"a_w_in": (a_w_in, False), "a_w_out": (a_w_out, True), "w_kv": (w_kv[None], False),
                "b_w_q": (b_w_q, True), "b_w_o": (b_w_o, True), "f_w_in": (f_w_in, False), "f_w_down": (f_w_down, True)}
    names = list(w_shards)
    ws = [_W(n, w_shards[n][0], w_shards[n][1]) for n in names]
    g_shards = {"a_w_in": (a_w_in, False), "a_w_out": (a_w_out, True),
                "f_w_in0": (f_w_in[0:1], False), "f_w_down0": (f_w_down[0:1], True),
                "w_kv": (w_kv[None], False), "b_w_q": (b_w_q, True), "b_w_o": (b_w_o, True),
                "f_w_in1": (f_w_in[1:2], False), "f_w_down1": (f_w_down[1:2], True)}
    g_names = list(g_shards)
    g_ws = {n: _W(n, *g_shards[n], direct=n not in ("a_w_in", "a_w_out", "f_w_in0")) for n in g_names}

    Wd = a_w_in.shape[1]
    GW = a_v_norm_g.shape[1] * N_CHIPS
    F2 = f_conv_w.shape[2] * N_CHIPS
    Fh = F2 // 2
    nsd, nsg, nsf = a_norm_g.shape[1], a_v_norm_g.shape[1], f_conv_w.shape[2]
    own = (ci == 0).astype(F32)
    place = lambda sh, width, n: lax.dynamic_update_slice_in_dim(
        jnp.zeros(sh.shape[:-1] + (width,), F32), sh * own, j_me * n, axis=sh.ndim - 1)
    def tied(x, flight):
        x, thru = lax.optimization_barrier((x, flight[0][2]))
        return x, [flight[0][:2] + (thru,) + flight[0][3:]] + flight[1:]

    gathered = _allreduce_small(_pack([place(a_norm_g, Wd, nsd), place(a_v_norm_g, GW, nsg),
                                       place(f_conv_w, F2, nsf)]))
    a_g, a_vg, conv_w = _unpack(gathered, [(1, Wd), (1, GW), (2, 3, F2)])
    first, rest = g_names[:4], g_names[4:]
    flight = dict(zip(first, _gather_start([g_ws[n] for n in first], [g_shards[n][0].astype(BF16) for n in first],
                                           gathered, name="gather_start_first")))
    (fi, fd, kv_w, qw, ow), (flight[first[0]],) = tied((f_w_in, f_w_down, w_kv, b_w_q, b_w_o), [flight[first[0]]])
    late = {"w_kv": kv_w[None], "b_w_q": qw, "b_w_o": ow, "f_w_in1": fi[1:2], "f_w_down1": fd[1:2]}
    flight.update(zip(rest, _gather_start([g_ws[n] for n in rest], [late[n].astype(BF16) for n in rest], kv_w,
                                          name="gather_start_rest")))
    full = {}

    def arrive(group, after, tag):
        gw = [g_ws[n] for n in group]
        sh, fu = _gather_wait(gw, [flight[n] for n in group], after, name=f"gather_wait_{tag}")
        full.update(zip(group, _gather_finish(gw, sh, fu, name=f"gather_finish_{tag}")))
    conv_w2 = conv_w.reshape(2, 3, 2, Fh).transpose(0, 2, 1, 3)
    conv_b2 = f_conv_b.reshape(2, 2, Fh)

    h0 = x.reshape(T, D)
    target = loss_target.reshape(T, D)
    bs_tile = jnp.repeat(a_b_s[0].T, GROUP_DIM, axis=1)
    ws_a = a_w_s[0]
    scale = HEAD_DIM ** -0.5
    HD = b_w_q.shape[2]
    H = HD // HEAD_DIM
    n_rel = b_rel_bias.shape[-1]
    frow, (flight["w_kv"],) = tied(b_rel_bias[0][:, _bias_index()].reshape(H, 1, F_LEN), [flight["w_kv"]])
    bias = _bias_expand(frow)

    def ffn_fwd(h, l, loss=None):
        out = _ffn_fwd(h, full[f"f_w_in{l}"], f_norm_g[l], conv_w2[l], conv_b2[l], full[f"f_w_down{l}"], S,
                       loss=loss, name=f"ffn{l}")
        yff, a, c, n = out[1:5]
        return (out[0] if loss is None else (out[0], out[5], out[6])), (a, c, n, yff)

    arrive(["a_w_in", "a_w_out"], bias, "a")
    h1, zp, out_a, n_a = _mixer_a_fwd(h0, full["a_w_in"], a_g[0], a_vg, ws_a, bs_tile, full["a_w_out"])
    arrive(["f_w_in0"], h1, "f0")
    yff0, a0, c0, n0 = _ffn_fwd(h1, full["f_w_in0"], f_norm_g[0], conv_w2[0], conv_b2[0], None, S, name="ffn0_in")
    arrive(["f_w_down0"], yff0, "fd0")
    h2, saved0 = _mm(yff0, full["f_w_down0"], res=h1, name="ffn0_down"), (a0, c0, n0, yff0)
    arrive(["w_kv", "b_w_q", "b_w_o"], h2, "b")
    arrive(["f_w_in1", "f_w_down1"], h2, "f1")
    q, kv, n_q, n_kv = _qkv_fwd(h2, full["b_w_q"], b_norm_g[0], full["w_kv"], kv_norm_g, scale)
    kv4, q3 = kv.reshape(2, B, S, HD), q.reshape(B, S, HD)
    o = _attn_fwd(q3, kv4, bias, B, S).reshape(T, HD)
    h3 = _mm(o, full["b_w_o"], res=h2, name="attn_out")
    (dh, loss8, dg_final), saved1 = ffn_fwd(h3, 1, loss=(final_norm_g, target))

    units = {}

    in_flight = {}

    def swap_start(group, tag, carry):
        us = [units[n] for n in group]
        lands = [lax.empty(u.shape[1:], BF16) for u in us]
        carry, flight = tied(carry, _split_copies(f"swap_start_{tag}", us, lands, 1, _swap_copies, after=carry))
        return (group, tag, flight), carry

    def reduce_start(swap, after):
        group, tag, flight = swap
        us, got = _split_copies(f"swap_wait_{tag}", [fl[2] for fl in flight], [fl[3] for fl in flight], 1,
                                _swap_copies, flight=flight, after=after)
        sums = [_add_pair(u, g_, core, name=f"pair_{n}") for n, u, g_ in zip(group, us, got)]
        after, flight = tied(after, _scatter_start(sums, name=f"scatter_start_{tag}"))
        in_flight.update(zip(group, flight))
        return after

    def ffn_bwd(dh, h, saved, l, early):
        a, c, n, yff = saved
        units[f"f_w_down{l}"] = _mm_tn(yff, dh, rows_are_shards=True, name=f"ffn{l}_down_dw")
        dh_in = dh
        if early:
            sw, dh_in = swap_start([f"f_w_down{l}"], f"fd{l}", dh)
        dyff = _mm(dh_in, full[f"f_w_down{l}"], trans_w=True, out_dtype=BF16, name=f"ffn{l}_down_dx")
        if early:
            dyff = reduce_start(sw, dyff)
        da, dcw, dcb = _conv_bwd(a, c, dyff, conv_w2[l], S)
        units[f"f_w_in{l}"] = _mm_tn(n, da, split_y=True, name=f"ffn{l}_in_dw")
        sw, da = swap_start([f"f_w_in{l}"] if early else [f"f_w_down{l}", f"f_w_in{l}"], f"f{l}", da)
        dh, dg = _mm(da, full[f"f_w_in{l}"], trans_w=True, split_x=True, bwd=(h, f_norm_g[l], dh),
                     name=f"ffn{l}_in_dx")
        return reduce_start(sw, dh), dg, dcw, dcb

    dh, dg_f1, dcw1, dcb1 = ffn_bwd(dh, h3, saved1, 1, False)
    do = _mm(dh, full["b_w_o"], trans_w=True, out_dtype=BF16, name="attn_out_dx")
    units["b_w_o"] = _mm_tn(o, dh, rows_are_shards=True, name="b_w_o_dw")
    dq, dkv, dbias = _attn_bwd(q3, kv4, bias, do.reshape(B, S, HD), B, S)
    dq, d_rel = lax.optimization_barrier((dq, _bias_reduce(dbias, n_rel)))
    d_rel = d_rel.reshape(1, H, n_rel)
    dq, dkv = dq.reshape(T, HD), dkv.reshape(2, T, HD)
    units["b_w_q"] = _mm_tn(n_q, dq, rows_are_shards=True, name="b_w_q_dw")
    units["w_kv"] = _mm_tn(n_kv, dkv, split_y=True, name="w_kv_dw")
    sw, dkv = swap_start(["b_w_o", "b_w_q", "w_kv"], "b", dkv)
    dh, dg_b, dg_kv = _qkv_dx(dq, full["b_w_q"], b_norm_g[0], dkv, full["w_kv"], kv_norm_g, h2, dh)
    dh = reduce_start(sw, dh)
    dh, dg_f0, dcw0, dcb0 = ffn_bwd(dh, h1, saved0, 0, True)
    units["a_w_out"] = _mm_tn(out_a, dh, rows_are_shards=True, name="a_w_out_dw")
    sw, dh_in = swap_start(["a_w_out"], "ao", dh)
    d_out = _mm(dh_in, full["a_w_out"], trans_w=True, out_dtype=BF16, name="a_out_dx")
    d_out = reduce_start(sw, d_out)
    dzp, dws, dbs, dgv = _gate_bwd(zp, d_out, a_vg, ws_a, bs_tile)
    units["a_w_in"] = _mm_tn(n_a, dzp, name="a_w_in_dw")
    sw, dzp_in = swap_start(["a_w_in"], "ai", dzp)
    dzp_in = reduce_start(sw, dzp_in)
    grad_x, dg_a = _mm(dzp_in, full["a_w_in"], trans_w=True, bwd=(h0, a_g[0], dh), name="a_in_dx")

    to_flat = lambda d: d.transpose(1, 0, 2).reshape(3, F2)
    small = {"a_norm_g": dg_a, "a_v_norm_g": dgv, "a_w_s": dws[None], "a_b_s": dbs[None], "kv_norm_g": dg_kv[0],
             "b_norm_g": dg_b, "b_rel_bias": d_rel, "f_norm_g": jnp.concatenate([dg_f0, dg_f1], axis=0),
             "f_conv_w": jnp.stack([to_flat(dcw0), to_flat(dcw1)]),
             "f_conv_b": jnp.stack([dcb0.reshape(F2), dcb1.reshape(F2)]), "final_norm_g": dg_final[0]}
    snames = list(small)
    small_vec = _pack([small[n] for n in snames] + [loss8[0:1, 0:1]])
    grad_x, small_flight = tied(grad_x, _split_copies("small_start", [small_vec],
                                                      [lax.empty((8,) + small_vec.shape, F32)], 7, _gather8_copies,
                                                      after=grad_x))

    sums, recv = _scatter_wait([in_flight[n] for n in g_names], grad_x)
    sums, recv = dict(zip(g_names, sums)), dict(zip(g_names, recv))
    halves = []
    for n, w in zip(names, ws):
        if w.L == 1:
            halves.append(_sum_chips(w, sums[n], recv[n], pos, name=f"chips_{n}"))
        else:
            first = _sum_chips(w, sums[n + "0"], recv[n + "0"], pos, name=f"chips_{n}0")
            halves.append(_sum_chips(w, sums[n + "1"], recv[n + "1"], pos, layer=1, into=first, name=f"chips_{n}1"))
    g_big = dict(zip(names, _join_halves(ws, halves)))
    g_big["w_kv"] = g_big["w_kv"][0]

    given = dict(a_norm_g=(a_norm_g, m_a_norm_g, v_a_norm_g), a_w_in=(a_w_in, m_a_w_in, v_a_w_in),
                 a_v_norm_g=(a_v_norm_g, m_a_v_norm_g, v_a_v_norm_g), a_w_s=(a_w_s, m_a_w_s, v_a_w_s),
                 a_b_s=(a_b_s, m_a_b_s, v_a_b_s), a_w_out=(a_w_out, m_a_w_out, v_a_w_out),
                 kv_norm_g=(kv_norm_g, m_kv_norm_g, v_kv_norm_g), w_kv=(w_kv, m_w_kv, v_w_kv),
                 b_norm_g=(b_norm_g, m_b_norm_g, v_b_norm_g), b_w_q=(b_w_q, m_b_w_q, v_b_w_q),
                 b_rel_bias=(b_rel_bias, m_b_rel_bias, v_b_rel_bias), b_w_o=(b_w_o, m_b_w_o, v_b_w_o),
                 f_norm_g=(f_norm_g, m_f_norm_g, v_f_norm_g), f_w_in=(f_w_in, m_f_w_in, v_f_w_in),
                 f_conv_w=(f_conv_w, m_f_conv_w, v_f_conv_w), f_conv_b=(f_conv_b, m_f_conv_b, v_f_conv_b),
                 f_w_down=(f_w_down, m_f_w_down, v_f_w_down), final_norm_g=(final_norm_g, m_final_norm_g, v_final_norm_g))
    order = list(given)
    grads, deltas, new_m, new_v = {}, {}, {}, {}
    for n in names:
        w_, m_, v_ = given[n]
        g_ = g_big[n]
        C = w_.shape[-1]
        d2, m2, v2 = _adamw(w_.reshape(-1, C), g_.reshape(-1, C), m_.reshape(-1, C), v_.reshape(-1, C),
                            name=f"adamw_{n}")
        grads[n], deltas[n], new_m[n], new_v[n] = g_.reshape(w_.shape), d2.reshape(w_.shape), m2.reshape(w_.shape), \
            v2.reshape(w_.shape)
    vecs, lands = _split_copies("small_wait", [small_flight[0][2]], [small_flight[0][3]], 7, _gather8_copies,
                                flight=small_flight, after=deltas[names[-1]])
    red = _sum8(lands[0], vecs[0], (4 * xi + 2 * yi + ci).astype(jnp.int32))
    parts = _unpack(red, [small[n].shape for n in snames] + [(1,)])
    g_small = dict(zip(snames, parts[:-1]))
    loss = parts[-1][0]
    g_small["a_norm_g"] = lax.dynamic_slice_in_dim(g_small["a_norm_g"], j_me * nsd, nsd, axis=1)
    g_small["a_v_norm_g"] = lax.dynamic_slice_in_dim(g_small["a_v_norm_g"], j_me * nsg, nsg, axis=1)
    g_small["f_conv_w"] = lax.dynamic_slice_in_dim(g_small["f_conv_w"], j_me * nsf, nsf, axis=2)

    sm = [n for n in order if n not in names]
    d2, m2, v2 = _adamw(_pack([given[n][0] for n in sm]), _pack([g_small[n].reshape(given[n][0].shape) for n in sm]),
                        _pack([given[n][1] for n in sm]), _pack([given[n][2] for n in sm]), name="adamw_small")
    shapes = [given[n][0].shape for n in sm]
    for n, d_, m_, v_ in zip(sm, _unpack(d2, shapes), _unpack(m2, shapes), _unpack(v2, shapes)):
        grads[n], deltas[n], new_m[n], new_v[n] = g_small[n].reshape(given[n][0].shape), d_, m_, v_

    return (loss, grad_x.reshape(B, S, D), *[grads[n] for n in order], *[deltas[n] for n in order],
            *[new_m[n] for n in order], *[new_v[n] for n in order])
```

```python
import functools
import math

import numpy as np
import jax
import jax.numpy as jnp
from jax import lax
from jax.experimental import pallas as pl
from jax.experimental.pallas import tpu as pltpu

F32 = jnp.float32
BF16 = jnp.bfloat16
MESH = pl.DeviceIdType.MESH

EPS = 1e-6
NEG_INF = -1e30
CHUNK = 64
GMLP_BLOCK = 128
GROUP_DIM = 128
HEAD_DIM = 64
LEFT_CHUNKS = 8
PAD = LEFT_CHUNKS * CHUNK
REL_CLIP = 128
Q_BLOCK = 256
K_SPAN = PAD + Q_BLOCK
F_LEN = K_SPAN + Q_BLOCK
HEADS_PER_STEP = 4
N_CHIPS = 4

ADAM_LR = 0.001
ADAM_B1 = 0.9
ADAM_B2 = 0.999
ADAM_EPS = 1e-08
ADAM_WD = 0.01
ADAM_STEP = 10

VMEM_LIMIT = 56 * 1024 * 1024


def _params(sem=None, **kw):
    if sem is not None:
        kw["dimension_semantics"] = sem
    return pltpu.CompilerParams(vmem_limit_bytes=VMEM_LIMIT, **kw)


def _rms(xf):
    r = lax.rsqrt(jnp.mean(xf * xf, axis=-1, keepdims=True) + EPS)
    return xf * r, r


def _gelu(x, with_grad=False):
    c = math.sqrt(2.0 / math.pi)
    x2 = x * x
    t = jnp.tanh(c * x * (1.0 + 0.044715 * x2))
    half = 0.5 * (1.0 + t)
    if not with_grad:
        return x * half
    return x * half, half + 0.5 * x * (1.0 - t * t) * c * (1.0 + 3.0 * 0.044715 * x2)


def _col_tile(n):
    if n <= 1024:
        return n
    for t in (1408, 1024, 512):
        if n % t == 0:
            return t
    raise ValueError(n)


def _row_tile(t, want):
    while t % want:
        want //= 2
    return want


def _loss_epilogue(h, g_ref, t_ref, dh_ref, loss_ref, dg_ref, first):
    @pl.when(first)
    def _():
        loss_ref[...] = jnp.zeros_like(loss_ref)
        dg_ref[...] = jnp.zeros_like(dg_ref)

    n, r = _rms(h)
    g = g_ref[...]
    e = n * g - t_ref[...]
    loss_ref[...] += 0.5 * jnp.sum(jnp.mean(e * e, axis=-1, keepdims=True), axis=0, keepdims=True)
    dy = e * (1.0 / h.shape[-1])
    dg_ref[...] += jnp.sum(dy * n, axis=0, keepdims=True)
    t = dy * g
    dh_ref[...] = r * (t - n * jnp.mean(t * n, axis=-1, keepdims=True))


def _mm(x, w, *, name, trans_w=False, res=None, out_dtype=F32, bwd=None, split_x=False, tm=512):
    T = x.shape[-2]
    K = 2 * x.shape[-1] if split_x else x.shape[-1]
    N = w.shape[-2] if trans_w else w.shape[-1]
    tm = _row_tile(T, 2 * tm if max(K, N) <= 2048 else tm)
    has_res, has_bwd = res is not None, bwd is not None
    dims = (((1,), (1,)), ((), ())) if trans_w else (((1,), (0,)), ((), ()))

    def body(*refs):
        it = iter(refs)
        x_ref, w_ref = next(it), next(it)
        res_ref = next(it) if has_res else None
        if has_bwd:
            h_ref, bg_ref, dh_ref = next(it), next(it), next(it)
        o_ref = next(it)
        if split_x:
            kh = K // 2
            acc = lax.dot_general(x_ref[0].astype(BF16), w_ref[:, :kh] if trans_w else w_ref[:kh, :], dims,
                                  preferred_element_type=F32)
            acc = acc + lax.dot_general(x_ref[1].astype(BF16), w_ref[:, kh:] if trans_w else w_ref[kh:, :], dims,
                                        preferred_element_type=F32)
        else:
            acc = lax.dot_general(x_ref[...].astype(BF16), w_ref[...], dims, preferred_element_type=F32)
        if has_res:
            acc = acc + res_ref[...]
        if has_bwd:
            dg_ref = next(it)
            n, r = _rms(h_ref[...])

            @pl.when(pl.program_id(0) == 0)
            def _():
                dg_ref[...] = jnp.zeros_like(dg_ref)

            dg_ref[...] += jnp.sum(acc * n, axis=0, keepdims=True)
            t = acc * bg_ref[...]
            o_ref[...] = dh_ref[...] + r * (t - n * jnp.mean(t * n, axis=-1, keepdims=True))
        else:
            o_ref[...] = acc.astype(out_dtype)

    row = lambda width: pl.BlockSpec((tm, width), lambda m: (m, 0))
    ins = [x, w]
    in_specs = [pl.BlockSpec((2, tm, K // 2), lambda m: (0, m, 0)) if split_x else row(K),
                pl.BlockSpec((None,) + w.shape[1:], lambda m: (0, 0, 0), pipeline_mode=pl.Buffered(1))]
    if has_res:
        ins.append(res)
        in_specs.append(row(N))
    out_shape = [jax.ShapeDtypeStruct((T, N), F32 if has_bwd else out_dtype)]
    out_specs = [row(N)]
    if has_bwd:
        h, g, dh = bwd
        ins += [h, g.reshape(1, N), dh]
        in_specs += [row(N), pl.BlockSpec((1, N), lambda m: (0, 0)), row(N)]
        out_shape.append(jax.ShapeDtypeStruct((1, N), F32))
        out_specs.append(pl.BlockSpec((1, N), lambda m: (0, 0)))
    out = pl.pallas_call(body, name=name, grid=(T // tm,), in_specs=in_specs, out_specs=out_specs,
                         out_shape=out_shape, compiler_params=_params(("arbitrary",)))(*ins)
    return out if has_bwd else out[0]


def _mm_tn(x, dy, *, name, rows_are_shards=False, split_y=False, tt=1024):
    T, K = x.shape
    N = 2 * dy.shape[-1] if split_y else dy.shape[-1]
    R, C = (K // N_CHIPS, N // 2) if rows_are_shards else (K // 2, N // N_CHIPS)
    nn = 2 if split_y else 1
    tn = N // nn
    per = N_CHIPS // nn
    assert not (rows_are_shards and split_y)
    tt = _row_tile(T, tt)
    nt = T // tt

    def body(x_ref, y_ref, o_ref, acc_ref):
        t = pl.program_id(1)

        @pl.when(t == 0)
        def _():
            acc_ref[...] = jnp.zeros_like(acc_ref)

        acc_ref[...] += lax.dot_general(x_ref[...], y_ref[...].astype(BF16), (((0,), (0,)), ((), ())),
                                        preferred_element_type=F32)

        @pl.when(t == nt - 1)
        def _():
            if rows_are_shards:
                for h in range(2):
                    o_ref[h] = acc_ref[:, h * C:(h + 1) * C].astype(BF16).reshape(N_CHIPS, R, C)
            else:
                for j in range(per):
                    o_ref[:, j] = acc_ref[:, j * C:(j + 1) * C].astype(BF16).reshape(2, R, C)

    if split_y:
        yspec = pl.BlockSpec((None, tt, tn), lambda n, t: (n, t, 0))
    else:
        yspec = pl.BlockSpec((tt, tn), lambda n, t: (t, 0))
    if rows_are_shards:
        out_spec = pl.BlockSpec((2, N_CHIPS, R, C), lambda n, t: (0, 0, 0, 0))
    else:
        out_spec = pl.BlockSpec((2, per, R, C), lambda n, t: (0, n, 0, 0))
    return pl.pallas_call(body, name=name, grid=(nn, nt),
                          in_specs=[pl.BlockSpec((tt, K), lambda n, t: (t, 0)), yspec], out_specs=out_spec,
                          out_shape=jax.ShapeDtypeStruct((2, N_CHIPS, R, C), BF16),
                          scratch_shapes=[pltpu.VMEM((K, tn), F32)],
                          compiler_params=_params(("arbitrary", "arbitrary")))(x, dy)


def _qkv_fwd(h, wq, gq, wkv, gkv, scale, *, tm=512):
    T, D = h.shape
    HD = wq.shape[-1]
    tm = _row_tile(T, tm)

    def body(h_ref, wq_ref, gq_ref, wkv_ref, gkv_ref, q_ref, kv_ref, nq_ref, nkv_ref):
        n = _rms(h_ref[...])[0]
        nq = (n * gq_ref[...]).astype(BF16)
        nkv = (n * gkv_ref[...]).astype(BF16)
        nq_ref[...] = nq
        nkv_ref[...] = nkv
        q_ref[...] = (jnp.dot(nq, wq_ref[...], preferred_element_type=F32) * scale).astype(BF16)
        kv = jnp.dot(nkv, wkv_ref[...], preferred_element_type=F32)
        kv_ref[0] = kv[:, :HD].astype(BF16)
        kv_ref[1] = kv[:, HD:].astype(BF16)

    row = lambda width: pl.BlockSpec((tm, width), lambda i: (i, 0))
    fixed = lambda *shape: pl.BlockSpec(shape, lambda i: (0,) * len(shape))
    weight = lambda n: pl.BlockSpec((None, D, n), lambda i: (0, 0, 0), pipeline_mode=pl.Buffered(1))
    return pl.pallas_call(
        body, name="qkv", grid=(T // tm,),
        in_specs=[row(D), weight(HD), fixed(1, D), weight(2 * HD), fixed(1, D)],
        out_specs=[row(HD), pl.BlockSpec((2, tm, HD), lambda i: (0, i, 0)), row(D), row(D)],
        out_shape=[jax.ShapeDtypeStruct((T, HD), BF16), jax.ShapeDtypeStruct((2, T, HD), BF16),
                   jax.ShapeDtypeStruct((T, D), BF16), jax.ShapeDtypeStruct((T, D), BF16)],
        compiler_params=_params(("arbitrary",)))(h, wq, gq.reshape(1, D), wkv, gkv.reshape(1, D))


def _qkv_dx(dq, wq, gq, dkv, wkv, gkv, h, dh, *, tm=512):
    T, D = h.shape
    HD = wq.shape[-1]
    tm = _row_tile(T, tm)
    nt = (((1,), (1,)), ((), ()))

    def body(dq_ref, wq_ref, gq_ref, dkv_ref, wkv_ref, gkv_ref, h_ref, dh_ref, o_ref, dgq_ref, dgkv_ref):
        @pl.when(pl.program_id(0) == 0)
        def _():
            dgq_ref[...] = jnp.zeros_like(dgq_ref)
            dgkv_ref[...] = jnp.zeros_like(dgkv_ref)

        n, r = _rms(h_ref[...])
        dnq = lax.dot_general(dq_ref[...], wq_ref[...], nt, preferred_element_type=F32)
        dnkv = (lax.dot_general(dkv_ref[0], wkv_ref[:, :HD], nt, preferred_element_type=F32)
                + lax.dot_general(dkv_ref[1], wkv_ref[:, HD:], nt, preferred_element_type=F32))
        dgq_ref[...] += jnp.sum(dnq * n, axis=0, keepdims=True)
        dgkv_ref[...] += jnp.sum(dnkv * n, axis=0, keepdims=True)
        t = dnq * gq_ref[...] + dnkv * gkv_ref[...]
        o_ref[...] = dh_ref[...] + r * (t - n * jnp.mean(t * n, axis=-1, keepdims=True))

    row = lambda width: pl.BlockSpec((tm, width), lambda i: (i, 0))
    fixed = lambda *shape: pl.BlockSpec(shape, lambda i: (0,) * len(shape))
    weight = lambda n: pl.BlockSpec((None, D, n), lambda i: (0, 0, 0), pipeline_mode=pl.Buffered(1))
    return pl.pallas_call(
        body, name="qkv_dx", grid=(T // tm,),
        in_specs=[row(HD), weight(HD), fixed(1, D), pl.BlockSpec((2, tm, HD), lambda i: (0, i, 0)), weight(2 * HD),
                  fixed(1, D), row(D), row(D)],
        out_specs=[row(D), fixed(1, D), fixed(1, D)],
        out_shape=[jax.ShapeDtypeStruct((T, D), F32), jax.ShapeDtypeStruct((1, D), F32),
                   jax.ShapeDtypeStruct((1, D), F32)],
        compiler_params=_params(("arbitrary",)))(dq, wq, gq.reshape(1, D), dkv, wkv, gkv.reshape(1, D), h, dh)


def _chunk_mask():
    i = lax.broadcasted_iota(jnp.int32, (GMLP_BLOCK, GMLP_BLOCK), 0) // CHUNK
    j = lax.broadcasted_iota(jnp.int32, (GMLP_BLOCK, GMLP_BLOCK), 1) // CHUNK
    return i >= j


def _mixer_a_fwd(h, w_in, g, gv, ws, bs_tile, w_out, *, tm=256):
    T, D = h.shape
    W = w_out.shape[-2]
    G = W // GROUP_DIM
    tm = _row_tile(T, tm)

    def body(h_ref, wi_ref, g_ref, gv_ref, ws_ref, bs_ref, wo_ref, o_ref, zp_ref, ga_ref, n_ref):
        nb = (_rms(h_ref[...])[0] * g_ref[...]).astype(BF16)
        n_ref[...] = nb
        zpb = jnp.dot(nb, wi_ref[...], preferred_element_type=F32).astype(BF16)
        zp_ref[...] = zpb
        z = _gelu(zpb.astype(F32))
        u, v = z[:, :W], z[:, W:]
        vn = _rms(v)[0] * gv_ref[...]
        mask = _chunk_mask()
        for gi in range(G):
            cs = slice(gi * GROUP_DIM, (gi + 1) * GROUP_DIM)
            wg = jnp.where(mask, ws_ref[gi], 0.0).astype(BF16)
            for b in range(tm // GMLP_BLOCK):
                rs = slice(b * GMLP_BLOCK, (b + 1) * GMLP_BLOCK)
                s = jnp.dot(wg, vn[rs, cs].astype(BF16), preferred_element_type=F32) + bs_ref[:, cs]
                ga_ref[rs, cs] = (u[rs, cs] * s).astype(BF16)
        o_ref[...] = h_ref[...] + jnp.dot(ga_ref[...], wo_ref[...], preferred_element_type=F32)

    row = lambda width: pl.BlockSpec((tm, width), lambda i: (i, 0))
    fixed = lambda *shape: pl.BlockSpec(shape, lambda i: (0,) * len(shape))
    weight = lambda k, n: pl.BlockSpec((None, k, n), lambda i: (0, 0, 0), pipeline_mode=pl.Buffered(1))
    return pl.pallas_call(
        body, name="mixer_a", grid=(T // tm,),
        in_specs=[row(D), weight(D, 2 * W), fixed(1, D), fixed(1, W), fixed(G, GMLP_BLOCK, GMLP_BLOCK),
                  fixed(GMLP_BLOCK, W), weight(W, D)],
        out_specs=[row(D), row(2 * W), row(W), row(D)],
        out_shape=[jax.ShapeDtypeStruct((T, D), F32), jax.ShapeDtypeStruct((T, 2 * W), BF16),
                   jax.ShapeDtypeStruct((T, W), BF16), jax.ShapeDtypeStruct((T, D), BF16)],
        compiler_params=_params(("arbitrary",)))(h, w_in, g.reshape(1, D), gv, ws, bs_tile, w_out)


def _gate_bwd(zp, d_out, gv, ws, bs_tile, *, tm=256):
    T, W2 = zp.shape
    W = W2 // 2
    G = W // GROUP_DIM
    tm = _row_tile(T, tm)
    nm = T // tm

    def body(zp_ref, do_ref, gv_ref, ws_ref, bs_ref, dzp_ref, dws_ref, dbs_ref, dgv_ref, du_scr, dvn_scr, dsum_scr):
        i = pl.program_id(0)

        @pl.when(i == 0)
        def _():
            dws_ref[...] = jnp.zeros_like(dws_ref)
            dgv_ref[...] = jnp.zeros_like(dgv_ref)
            dsum_scr[...] = jnp.zeros_like(dsum_scr)

        zp = zp_ref[...].astype(F32)
        z, dz = _gelu(zp, with_grad=True)
        u, v = z[:, :W], z[:, W:]
        n, r = _rms(v)
        gv = gv_ref[...]
        vn = n * gv
        d_out = do_ref[...].astype(F32)
        mask = _chunk_mask()
        for g in range(G):
            cs = slice(g * GROUP_DIM, (g + 1) * GROUP_DIM)
            wg = jnp.where(mask, ws_ref[g], 0.0).astype(BF16)
            dw = jnp.zeros((GMLP_BLOCK, GMLP_BLOCK), F32)
            for b in range(tm // GMLP_BLOCK):
                rs = slice(b * GMLP_BLOCK, (b + 1) * GMLP_BLOCK)
                vb = vn[rs, cs].astype(BF16)
                s = jnp.dot(wg, vb, preferred_element_type=F32) + bs_ref[:, cs]
                du_scr[rs, cs] = d_out[rs, cs] * s
                ds = d_out[rs, cs] * u[rs, cs]
                dsb = ds.astype(BF16)
                dvn_scr[rs, cs] = lax.dot_general(wg, dsb, (((0,), (0,)), ((), ())), preferred_element_type=F32)
                dw = dw + lax.dot_general(dsb, vb, (((1,), (1,)), ((), ())), preferred_element_type=F32)
                dsum_scr[:, cs] += ds
            dws_ref[g] += jnp.where(mask, dw, 0.0)
        dvn = dvn_scr[...]
        dgv_ref[...] += jnp.sum(dvn * n, axis=0, keepdims=True)
        t = dvn * gv
        dv = r * (t - n * jnp.mean(t * n, axis=-1, keepdims=True))
        dzp_ref[:, :W] = (du_scr[...] * dz[:, :W]).astype(BF16)
        dzp_ref[:, W:] = (dv * dz[:, W:]).astype(BF16)

        @pl.when(i == nm - 1)
        def _():
            sel = (lax.broadcasted_iota(jnp.int32, (G, W), 1) // GROUP_DIM
                   == lax.broadcasted_iota(jnp.int32, (G, W), 0)).astype(F32)
            dbs_ref[...] = lax.dot_general(sel, dsum_scr[...], (((1,), (1,)), ((), ())),
                                           precision=lax.Precision.HIGHEST, preferred_element_type=F32)

    return pl.pallas_call(
        body, name="gate_bwd", grid=(nm,),
        in_specs=[pl.BlockSpec((tm, W2), lambda i: (i, 0)), pl.BlockSpec((tm, W), lambda i: (i, 0)),
                  pl.BlockSpec((1, W), lambda i: (0, 0)),
                  pl.BlockSpec((G, GMLP_BLOCK, GMLP_BLOCK), lambda i: (0, 0, 0)),
                  pl.BlockSpec((GMLP_BLOCK, W), lambda i: (0, 0))],
        out_specs=[pl.BlockSpec((tm, W2), lambda i: (i, 0)),
                   pl.BlockSpec((G, GMLP_BLOCK, GMLP_BLOCK), lambda i: (0, 0, 0)),
                   pl.BlockSpec((G, GMLP_BLOCK), lambda i: (0, 0)), pl.BlockSpec((1, W), lambda i: (0, 0))],
        out_shape=[jax.ShapeDtypeStruct((T, W2), BF16), jax.ShapeDtypeStruct((G, GMLP_BLOCK, GMLP_BLOCK), F32),
                   jax.ShapeDtypeStruct((G, GMLP_BLOCK), F32), jax.ShapeDtypeStruct((1, W), F32)],
        scratch_shapes=[pltpu.VMEM((tm, W), F32), pltpu.VMEM((tm, W), F32), pltpu.VMEM((GMLP_BLOCK, W), F32)],
        compiler_params=_params(("arbitrary",)))(zp, d_out, gv, ws, bs_tile)


LANES = 128
HALO = 16


def _taps(ext, w, b):
    return w[2:3] * ext[HALO:] + w[1:2] * pltpu.roll(ext, 1, 0)[HALO:] + w[0:1] * pltpu.roll(ext, 2, 0)[HALO:] + b


def _ffn_fwd(h, w, g, cw, cb, wd, S, *, name, loss=None, tm=256):
    T, D = h.shape
    F = w.shape[-1] // 2
    tc = _col_tile(F)
    tm = _row_tile(S, tm)
    has_loss, has_down = loss is not None, wd is not None
    n_in = 5 + has_down + 2 * has_loss

    def body(*refs):
        h_ref, w_ref, g_ref, cw_ref, cb_ref = refs[:5]
        outs, tail = refs[n_in:-1], refs[-1]
        y_ref, a_ref, c_ref, n_ref = outs[has_down:has_down + 4]
        first = (pl.program_id(0) * tm) % S == 0
        nb = (_rms(h_ref[...])[0] * g_ref[...]).astype(BF16)
        n_ref[...] = nb
        for j in range(F // tc):
            cs = slice(j * tc, (j + 1) * tc)
            conv = []
            for s in range(2):
                acc = jnp.dot(nb, w_ref[:, s * F + j * tc:s * F + (j + 1) * tc], preferred_element_type=F32)
                ab = acc.astype(BF16)
                a_ref[s, :, cs] = ab
                af = ab.astype(F32)
                ext = jnp.concatenate([jnp.where(first, 0.0, tail[s, :, cs]), af], axis=0)
                tail[s, :, cs] = af[tm - HALO:, :]
                cv = _taps(ext, cw_ref[s, :, cs], cb_ref[s:s + 1, cs]).astype(BF16)
                c_ref[s, :, cs] = cv
                conv.append(cv.astype(F32))
            up, gate = conv
            y_ref[:, cs] = (gate * jax.nn.sigmoid(gate) * up).astype(BF16)
        if has_down:
            out = h_ref[...] + jnp.dot(y_ref[...], refs[5][...], preferred_element_type=F32)
            if has_loss:
                _loss_epilogue(out, refs[6], refs[7], outs[0], outs[5], outs[6], pl.program_id(0) == 0)
            else:
                outs[0][...] = out

    row = lambda width: pl.BlockSpec((tm, width), lambda i: (i, 0))
    wide = pl.BlockSpec((2, tm, F), lambda i: (0, i, 0))
    fixed = lambda *shape: pl.BlockSpec(shape, lambda i: (0,) * len(shape))
    once = pl.Buffered(1)
    ins = [h, w, g.reshape(1, D), cw, cb]
    in_specs = [row(D), pl.BlockSpec((None, D, 2 * F), lambda i: (0, 0, 0), pipeline_mode=once), fixed(1, D),
                fixed(2, 3, F), fixed(2, F)]
    out_specs = [row(F), wide, wide, row(D)]
    out_shape = [jax.ShapeDtypeStruct((T, F), BF16), jax.ShapeDtypeStruct((2, T, F), BF16),
                 jax.ShapeDtypeStruct((2, T, F), BF16), jax.ShapeDtypeStruct((T, D), BF16)]
    if has_down:
        ins.append(wd)
        in_specs.append(pl.BlockSpec((None, F, D), lambda i: (0, 0, 0), pipeline_mode=once))
        out_specs.insert(0, row(D))
        out_shape.insert(0, jax.ShapeDtypeStruct((T, D), F32))
    if has_loss:
        ins += [loss[0].reshape(1, D), loss[1]]
        in_specs += [fixed(1, D), row(D)]
        out_specs += [fixed(8, 128), fixed(1, D)]
        out_shape += [jax.ShapeDtypeStruct((8, 128), F32), jax.ShapeDtypeStruct((1, D), F32)]
    return pl.pallas_call(body, name=name, grid=(T // tm,), in_specs=in_specs, out_specs=out_specs,
                          out_shape=out_shape, scratch_shapes=[pltpu.VMEM((2, HALO, F), F32)],
                          compiler_params=_params(("arbitrary",)))(*ins)


def _conv_bwd(a, c, dy, cw, S, *, tm=256):
    _, T, F = a.shape
    tc = _col_tile(F)
    tm = _row_tile(S, tm)
    nm = T // tm
    hb = tm // HALO
    TE = tm + HALO
    nxt = lambda j, i: jnp.minimum((i + 1) * hb, T // HALO - 1)

    def body(a_ref, c_ref, nc_ref, dy_ref, ndy_ref, w_ref, da_ref, dw_ref, db_ref):
        i = pl.program_id(1)
        last = ((i + 1) * tm) % S == 0
        keep_n = jnp.where(last, 0.0, 1.0)

        @pl.when(i == 0)
        def _():
            dw_ref[...] = jnp.zeros_like(dw_ref)
            db_ref[...] = jnp.zeros_like(db_ref)

        for j in range(tc // LANES):
            cs = slice(j * LANES, (j + 1) * LANES)
            dyf = jnp.concatenate([dy_ref[:, cs].astype(F32), ndy_ref[:, cs].astype(F32) * keep_n], axis=0)
            up = jnp.concatenate([c_ref[0, :, cs].astype(F32), nc_ref[0, :, cs].astype(F32)], axis=0)
            gate = jnp.concatenate([c_ref[1, :, cs].astype(F32), nc_ref[1, :, cs].astype(F32)], axis=0)
            sg = jax.nn.sigmoid(gate)
            for s, d in ((0, dyf * (gate * sg)), (1, dyf * up * (sg * (1.0 + gate * (1.0 - sg))))):
                a = a_ref[s, :, cs].astype(F32)
                w = w_ref[s, :, cs]
                u1, u2 = pltpu.roll(d, TE - 1, 0), pltpu.roll(d, TE - 2, 0)
                db_ref[s:s + 1, cs] += jnp.sum(d[:tm], axis=0, keepdims=True)
                dw_ref[s, 2:3, cs] += jnp.sum(d[:tm] * a, axis=0, keepdims=True)
                dw_ref[s, 1:2, cs] += jnp.sum(u1[:tm] * a, axis=0, keepdims=True)
                dw_ref[s, 0:1, cs] += jnp.sum(u2[:tm] * a, axis=0, keepdims=True)
                da_ref[s, :, cs] = (w[2:3] * d + w[1:2] * u1 + w[0:1] * u2)[:tm].astype(BF16)

    cur = pl.BlockSpec((2, tm, tc), lambda j, i: (0, i, j))
    return pl.pallas_call(
        body, name="conv_bwd", grid=(F // tc, nm),
        in_specs=[cur, cur, pl.BlockSpec((2, HALO, tc), lambda j, i: (0, nxt(j, i), j)),
                  pl.BlockSpec((tm, tc), lambda j, i: (i, j)), pl.BlockSpec((HALO, tc), lambda j, i: (nxt(j, i), j)),
                  pl.BlockSpec((2, 3, tc), lambda j, i: (0, 0, j))],
        out_specs=[cur, pl.BlockSpec((2, 3, tc), lambda j, i: (0, 0, j)), pl.BlockSpec((2, tc), lambda j, i: (0, j))],
        out_shape=[jax.ShapeDtypeStruct((2, T, F), BF16), jax.ShapeDtypeStruct((2, 3, F), F32),
                   jax.ShapeDtypeStruct((2, F), F32)],
        compiler_params=_params(("arbitrary", "arbitrary")))(a, c, c, dy, dy, cw)


def _bias_index():
    idx = np.arange(F_LEN)
    d = np.where(idx < K_SPAN, idx, idx - F_LEN)
    return np.clip(PAD - d, -REL_CLIP, REL_CLIP) + REL_CLIP


ROW_GROUP = 16


def _roll_rows(x, sign, unit, steps):
    rows = lax.broadcasted_iota(jnp.int32, x.shape, 0)
    step = 1
    while step < steps:
        shift = unit * step if sign > 0 else F_LEN - unit * step
        x = jnp.where((rows & step) != 0, pltpu.roll(x, shift, 1), x)
        step *= 2
    return x


def _bias_expand(frow):
    H = frow.shape[0]
    groups = Q_BLOCK // ROW_GROUP

    def body(f_ref, o_ref):
        coarse = _roll_rows(jnp.broadcast_to(f_ref[...], (groups, F_LEN)), 1, ROW_GROUP, groups)
        x = jnp.concatenate([jnp.broadcast_to(coarse[a:a + 1], (ROW_GROUP, F_LEN)) for a in range(groups)], axis=0)
        x = _roll_rows(x, 1, 1, ROW_GROUP)[:, :K_SPAN]
        qc = lax.broadcasted_iota(jnp.int32, (Q_BLOCK, K_SPAN), 0) // CHUNK * CHUNK
        kj = lax.broadcasted_iota(jnp.int32, (Q_BLOCK, K_SPAN), 1)
        o_ref[...] = jnp.where((kj >= qc) & (kj < qc + PAD + CHUNK), x, NEG_INF)

    return pl.pallas_call(
        body, name="bias_expand", grid=(H,),
        in_specs=[pl.BlockSpec((None, 1, F_LEN), lambda h: (h, 0, 0))],
        out_specs=pl.BlockSpec((None, Q_BLOCK, K_SPAN), lambda h: (h, 0, 0)),
        out_shape=jax.ShapeDtypeStruct((H, Q_BLOCK, K_SPAN), F32), compiler_params=_params(("arbitrary",)))(frow)


def _bias_reduce(dbias, n_rel):
    H = dbias.shape[0]
    onehot = jnp.asarray((_bias_index()[:, None] == np.arange(n_rel)[None, :]).astype(np.float32), dtype=BF16)

    def body(d_ref, oh_ref, o_ref):
        x = jnp.concatenate([d_ref[...], jnp.zeros((Q_BLOCK, F_LEN - K_SPAN), F32)], axis=1)
        fine = _roll_rows(x, -1, 1, ROW_GROUP).reshape(Q_BLOCK // ROW_GROUP, ROW_GROUP, F_LEN)
        coarse = _roll_rows(jnp.sum(fine, axis=1), -1, ROW_GROUP, Q_BLOCK // ROW_GROUP)
        row = jnp.broadcast_to(jnp.sum(coarse, axis=0, keepdims=True), (8, F_LEN))
        acc = jnp.zeros((8, n_rel), F32)
        for _ in range(3):
            piece = row.astype(BF16)
            acc = acc + jnp.dot(piece, oh_ref[...], preferred_element_type=F32)
            row = row - piece.astype(F32)
        o_ref[...] = acc[0:1]

    return pl.pallas_call(
        body, name="bias_reduce", grid=(H,),
        in_specs=[pl.BlockSpec((None, Q_BLOCK, K_SPAN), lambda h: (h, 0, 0)),
                  pl.BlockSpec((F_LEN, n_rel), lambda h: (0, 0))],
        out_specs=pl.BlockSpec((None, 1, n_rel), lambda h: (h, 0, 0)),
        out_shape=jax.ShapeDtypeStruct((H, 1, n_rel), F32), compiler_params=_params(("arbitrary",)))(dbias, onehot)


def _attn_specs(S, heads=HEADS_PER_STEP):
    hw = heads * HEAD_DIM
    qspec = pl.BlockSpec((None, Q_BLOCK, hw), lambda g, b, i: (b, i, g))
    kspec = pl.BlockSpec((None, None, S, hw), lambda g, b, i: (0, b, 0, g))
    vspec = pl.BlockSpec((None, None, S, hw), lambda g, b, i: (1, b, 0, g))
    bspec = pl.BlockSpec((heads, Q_BLOCK, K_SPAN), lambda g, b, i: (g, 0, 0))
    return hw, qspec, kspec, vspec, bspec


def _span_cases(i, fn):
    short = PAD // Q_BLOCK
    for j in range(short):
        pl.when(i == j)(functools.partial(fn, PAD - j * Q_BLOCK))
    pl.when(i >= short)(functools.partial(fn, 0))


def _key_start(i, off):
    return 0 if off else pl.multiple_of(i * Q_BLOCK - PAD, Q_BLOCK)


def _attn_exp(q_ref, k_ref, b_ref, h, k0, off):
    hs = slice(h * HEAD_DIM, (h + 1) * HEAD_DIM)
    kh = k_ref[pl.ds(k0, K_SPAN - off), hs]
    s = lax.dot_general(q_ref[:, hs], kh, (((1,), (1,)), ((), ())), preferred_element_type=F32) + b_ref[h, :, off:]
    p = jnp.exp(s - jnp.max(s, axis=-1, keepdims=True))
    return p, 1.0 / jnp.sum(p, axis=-1, keepdims=True), kh


def _attn_fwd(q, kv, bias, B, S):
    HD = q.shape[-1]
    heads = min(2 * HEADS_PER_STEP, HD // HEAD_DIM)
    hw, qspec, kspec, vspec, bspec = _attn_specs(S, heads)

    def body(q_ref, k_ref, v_ref, b_ref, o_ref):
        i = pl.program_id(2)

        def block(off):
            k0 = _key_start(i, off)
            outs = []
            for h in range(heads):
                hs = slice(h * HEAD_DIM, (h + 1) * HEAD_DIM)
                p, inv, _ = _attn_exp(q_ref, k_ref, b_ref, h, k0, off)
                outs.append(jnp.dot(p.astype(BF16), v_ref[pl.ds(k0, K_SPAN - off), hs],
                                    preferred_element_type=F32) * inv)
            o_ref[...] = jnp.concatenate(outs, axis=1).astype(BF16)

        _span_cases(i, block)

    return pl.pallas_call(
        body, name="attn_fwd", grid=(HD // hw, B, S // Q_BLOCK), in_specs=[qspec, kspec, vspec, bspec],
        out_specs=qspec, out_shape=jax.ShapeDtypeStruct((B, S, HD), BF16),
        compiler_params=_params(("arbitrary", "arbitrary", "arbitrary")))(q, kv, kv, bias)


def _attn_bwd(q, kv, bias, do, B, S):
    HD = q.shape[-1]
    H = HD // HEAD_DIM
    hw, qspec, kspec, vspec, bspec = _attn_specs(S)
    scale = HEAD_DIM ** -0.5
    nq = S // Q_BLOCK

    def body(q_ref, k_ref, v_ref, b_ref, do_ref, dq_ref, dkv_ref, db_ref, dk_acc, dv_acc):
        b, i = pl.program_id(1), pl.program_id(2)

        @pl.when(i == 0)
        def _():
            dk_acc[...] = jnp.zeros_like(dk_acc)
            dv_acc[...] = jnp.zeros_like(dv_acc)

        @pl.when((i == 0) & (b == 0))
        def _():
            db_ref[...] = jnp.zeros_like(db_ref)

        def block(off):
            k0 = _key_start(i, off)
            keys = pl.ds(k0, K_SPAN - off)
            for h in range(HEADS_PER_STEP):
                hs = slice(h * HEAD_DIM, (h + 1) * HEAD_DIM)
                p, inv, kh = _attn_exp(q_ref, k_ref, b_ref, h, k0, off)
                p = p * inv
                doh = do_ref[:, hs]
                dp = lax.dot_general(doh, v_ref[keys, hs], (((1,), (1,)), ((), ())), preferred_element_type=F32)
                ds = p * (dp - jnp.sum(p * dp, axis=-1, keepdims=True))
                db_ref[h, :, off:] += ds
                dsb = ds.astype(BF16)
                dq_ref[:, hs] = (jnp.dot(dsb, kh, preferred_element_type=F32) * scale).astype(BF16)
                dk_acc[hs, keys] += lax.dot_general(q_ref[:, hs], dsb, (((0,), (0,)), ((), ())),
                                                     preferred_element_type=F32)
                dv_acc[hs, keys] += lax.dot_general(doh, p.astype(BF16), (((0,), (0,)), ((), ())),
                                                     preferred_element_type=F32)

        _span_cases(i, block)

        @pl.when(i == nq - 1)
        def _():
            dkv_ref[0] = dk_acc[...].T.astype(BF16)
            dkv_ref[1] = dv_acc[...].T.astype(BF16)

    return pl.pallas_call(
        body, name="attn_bwd", grid=(HD // hw, B, nq), in_specs=[qspec, kspec, vspec, bspec, qspec],
        out_specs=[qspec, pl.BlockSpec((2, None, S, hw), lambda g, b, i: (0, b, 0, g)), bspec],
        out_shape=[jax.ShapeDtypeStruct((B, S, HD), BF16), jax.ShapeDtypeStruct((2, B, S, HD), BF16),
                   jax.ShapeDtypeStruct((H, Q_BLOCK, K_SPAN), F32)],
        scratch_shapes=[pltpu.VMEM((hw, S), F32), pltpu.VMEM((hw, S), F32)],
        compiler_params=_params(("arbitrary", "arbitrary", "arbitrary")))(q, kv, kv, bias, do)


def _sub_rows(R):
    for cand in (256, 352, 128, 64, 8):
        if R % cand == 0 and R > cand:
            return cand
    return R


def _adamw(w, g, m, v, *, name):
    R, C = w.shape
    tr = _sub_rows(R)

    def body(w_ref, g_ref, m_ref, v_ref, d_ref, nm_ref, nv_ref):
        g = g_ref[...]
        m = ADAM_B1 * m_ref[...] + (1.0 - ADAM_B1) * g
        v = ADAM_B2 * v_ref[...] + (1.0 - ADAM_B2) * (g * g)
        m_hat = m / (1.0 - ADAM_B1 ** ADAM_STEP)
        v_hat = v / (1.0 - ADAM_B2 ** ADAM_STEP)
        d_ref[...] = -ADAM_LR * (m_hat / (jnp.sqrt(v_hat) + ADAM_EPS) + ADAM_WD * w_ref[...])
        nm_ref[...] = m
        nv_ref[...] = v

    spec = pl.BlockSpec((tr, C), lambda i: (i, 0))
    return pl.pallas_call(body, name=name, grid=(R // tr,), in_specs=[spec] * 4, out_specs=[spec] * 3,
                          out_shape=[jax.ShapeDtypeStruct((R, C), F32)] * 3,
                          compiler_params=_params(("arbitrary",)))(w, g, m, v)


def _add_pair(units, got, core, *, name):
    n4, R, C = got.shape
    rows = n4 * R
    tr = 512 if rows % 512 == 0 else R

    def body(c_ref, u_ref, got_ref, o_ref):
        o_ref[...] = (u_ref[...].astype(F32) + got_ref[...].astype(F32)).astype(BF16)

    spec = pl.BlockSpec((tr, C), lambda i, c: (i, 0))
    grid_spec = pltpu.PrefetchScalarGridSpec(
        num_scalar_prefetch=1, grid=(rows // tr,),
        in_specs=[pl.BlockSpec((None, tr, C), lambda i, c: (c[0], i, 0)), spec], out_specs=spec)
    out = pl.pallas_call(body, name=name, grid_spec=grid_spec, out_shape=jax.ShapeDtypeStruct((rows, C), BF16),
                         compiler_params=_params(("arbitrary",)))(core.reshape(1), units.reshape(2, rows, C),
                                                                   got.reshape(rows, C))
    return out.reshape(n4, R, C)


def _sum_chips(w, own, got, pos, *, name, layer=0, into=None):
    _, R, C = own.shape
    tr = _sub_rows(R)
    nr = R // tr

    def body(p_ref, own_ref, got_ref, *rest):
        o_ref = rest[-1]
        o_ref[...] = (own_ref[...].astype(F32) + got_ref[0].astype(F32) + got_ref[1].astype(F32)
                      + got_ref[2].astype(F32))

    if w.row_sharded:
        out_map = lambda i, p: (layer, i, p[1])
    else:
        out_map = lambda i, p: (layer, p[1] * nr + i, 0)
    ins = [pos, own, got]
    in_specs = [pl.BlockSpec((None, tr, C), lambda i, p: (p[0], i, 0)),
                pl.BlockSpec((3, tr, C), lambda i, p: (0, i, 0))]
    alias = {}
    if into is not None:
        ins.append(into)
        in_specs.append(ANY)
        alias = {3: 0}
    grid_spec = pltpu.PrefetchScalarGridSpec(num_scalar_prefetch=1, grid=(nr,), in_specs=in_specs,
                                             out_specs=pl.BlockSpec((None, tr, C), out_map))
    return pl.pallas_call(body, name=name, grid_spec=grid_spec, input_output_aliases=alias,
                          out_shape=jax.ShapeDtypeStruct((w.L, w.ks, w.ns), F32),
                          compiler_params=_params(("arbitrary",)))(*ins)


def _mesh_pos():
    return lax.axis_index("x"), lax.axis_index("y"), lax.axis_index("c")


def _other_chips(x, y):
    return [(1 - x, y), (x, 1 - y), (1 - x, 1 - y)]


ANY = pl.BlockSpec(memory_space=pl.ANY)


class _W:
    def __init__(self, name, shard, row_sharded, direct=False):
        self.name = name
        self.direct = direct
        self.L, ks, ns = shard.shape
        self.row_sharded = row_sharded
        self.K, self.N = (ks * N_CHIPS, ns) if row_sharded else (ks, ns * N_CHIPS)
        self.ks, self.ns = ks, ns

    def shard_of(self, full, j):
        if self.row_sharded:
            return full.at[:, pl.ds(j * self.ks, self.ks), :]
        return full.at[:, :, pl.ds(j * self.ns, self.ns)]

    def half_of(self, shard, c):
        if self.row_sharded:
            return shard.at[:, :, pl.ds(c * (self.ns // 2), self.ns // 2)]
        return shard.at[:, pl.ds(c * (self.ks // 2), self.ks // 2), :]


HBM = pl.BlockSpec(memory_space=pltpu.HBM)
SEM = pl.BlockSpec(memory_space=pltpu.SEMAPHORE)
IN_FLIGHT = pltpu.SideEffectType.DATAFLOW_SIDE_EFFECTING


def _in_hbm(a):
    return pltpu.with_memory_space_constraint(a, pltpu.HBM)


def _gather_start(ws, shards, after, *, name):
    nw = len(ws)

    def body(*refs):
        src, dst = refs[:nw], refs[nw:2 * nw]
        send, recv = refs[2 * nw + 1:3 * nw + 1], refs[3 * nw + 1:4 * nw + 1]
        x, y, c = _mesh_pos()
        me = 2 * x + y
        for i, w in enumerate(ws):
            for f, (px, py) in enumerate(_other_chips(x, y)):
                for e in range(2 if w.direct else 1):
                    k = 2 * f + e
                    pltpu.make_async_remote_copy(
                        src_ref=w.half_of(src[i], c), dst_ref=w.half_of(w.shard_of(dst[i], me), c),
                        send_sem=send[i].at[k], recv_sem=recv[i].at[k], device_id=(px, py, c if e == 0 else 1 - c),
                        device_id_type=MESH).start()

    fulls = [lax.empty((w.L, w.K, w.N), BF16) for w in ws]
    out = pl.pallas_call(
        body, name=name, in_specs=[HBM] * (2 * nw) + [ANY],
        out_specs=[SEM] * (2 * nw) + [HBM] * (2 * nw),
        out_shape=[pltpu.SemaphoreType.DMA((6,))] * (2 * nw)
        + [pltpu.HBM(s.shape, BF16) for s in shards] + [pltpu.HBM(f.shape, BF16) for f in fulls],
        input_output_aliases={i: 2 * nw + i for i in range(2 * nw)},
        compiler_params=pltpu.CompilerParams(has_side_effects=IN_FLIGHT))(
            *[_in_hbm(s) for s in shards], *[_in_hbm(f) for f in fulls], after)
    return [(out[i], out[nw + i], out[2 * nw + i], out[3 * nw + i]) for i in range(nw)]


def _gather_wait(ws, flight, after, *, name):
    nw = len(ws)

    def body(*refs):
        src, dst = refs[:nw], refs[nw:2 * nw]
        send, recv = refs[2 * nw:3 * nw], refs[3 * nw:4 * nw]
        x, y, c = _mesh_pos()
        for i, w in enumerate(ws):
            for f, (px, py) in enumerate(_other_chips(x, y)):
                for e in range(2 if w.direct else 1):
                    k = 2 * f + e
                    landed = w.half_of(w.shard_of(dst[i], 2 * px + py), c if e == 0 else 1 - c)
                    cp = pltpu.make_async_remote_copy(
                        src_ref=w.half_of(src[i], c), dst_ref=landed, send_sem=send[i].at[k], recv_sem=recv[i].at[k],
                        device_id=(px, py, c), device_id_type=MESH)
                    cp.wait_send()
                    cp.wait_recv()

    shards, fulls = [fl[2] for fl in flight], [fl[3] for fl in flight]
    out = pl.pallas_call(
        body, name=name, in_specs=[HBM] * (2 * nw) + [SEM] * (2 * nw) + [ANY],
        out_specs=[HBM] * (2 * nw),
        out_shape=[pltpu.HBM(s.shape, BF16) for s in shards] + [pltpu.HBM(f.shape, BF16) for f in fulls],
        input_output_aliases={i: i for i in range(2 * nw)},
        compiler_params=pltpu.CompilerParams(has_side_effects=IN_FLIGHT))(
            *shards, *fulls, *[fl[0] for fl in flight], *[fl[1] for fl in flight], after)
    return out[:nw], out[nw:]


def _gather_finish(ws, shards, fulls, *, name):
    nw = len(ws)
    forward = not ws[0].direct

    def body(*refs):
        src, dst, stage = refs[:nw], refs[3 * nw:4 * nw], refs[4 * nw:5 * nw]
        send_sems, recv_sems, load_sems, store_sems = refs[5 * nw:]
        x, y, c = _mesh_pos()
        me = 2 * x + y
        sibling = (x, y, 1 - c)
        chips = _other_chips(x, y)

        def fwd(i, w, f, half):
            px, py = chips[f]
            landed = w.half_of(w.shard_of(dst[i], 2 * px + py), half)
            return pltpu.make_async_remote_copy(src_ref=landed, dst_ref=landed, send_sem=send_sems.at[3 * i + f],
                                                recv_sem=recv_sems.at[3 * i + f], device_id=sibling,
                                                device_id_type=MESH)

        loads = [pltpu.make_async_copy(src[i], stage[i], load_sems.at[i]) for i in range(nw)]
        for cp in loads:
            cp.start()
        sends = [fwd(i, w, f, c) for i, w in enumerate(ws) for f in range(3)] if forward else []
        for cp in sends:
            cp.start()
        stores = [pltpu.make_async_copy(stage[i], w.shard_of(dst[i], me), store_sems.at[i])
                  for i, w in enumerate(ws)]
        for ld, st in zip(loads, stores):
            ld.wait()
            st.start()
        if forward:
            for i, w in enumerate(ws):
                for f in range(3):
                    fwd(i, w, f, 1 - c).wait_recv()
        for cp in sends:
            cp.wait_send()
        for cp in stores:
            cp.wait()

    out = pl.pallas_call(
        body, name=name, in_specs=[ANY] * (2 * nw), out_specs=[ANY] * (2 * nw),
        out_shape=[jax.ShapeDtypeStruct(s.shape, BF16) for s in shards]
        + [jax.ShapeDtypeStruct(f.shape, BF16) for f in fulls],
        input_output_aliases={i: i for i in range(2 * nw)},
        scratch_shapes=[pltpu.VMEM((w.L, w.ks, w.ns), BF16) for w in ws]
        + [pltpu.SemaphoreType.DMA((3 * nw,)), pltpu.SemaphoreType.DMA((3 * nw,)), pltpu.SemaphoreType.DMA((nw,)),
           pltpu.SemaphoreType.DMA((nw,))],
        compiler_params=_params(has_side_effects=True))(*shards, *fulls)
    return out[nw:]


def _split_copies(name, srcs, lands, n_sems, copies_of, *, flight=None, after=None):
    n = len(srcs)
    starting = flight is None

    def body(*refs):
        src, land = refs[:n], refs[n:2 * n]
        sems = refs[2 * n + 1:4 * n + 1] if starting else refs[2 * n:4 * n]
        for i in range(n):
            for cp in copies_of(i, src[i], land[i], sems[i], sems[n + i]):
                if starting:
                    cp.start()
                else:
                    cp.wait_send()
                    cp.wait_recv()

    thru = [pltpu.HBM(a.shape, a.dtype) for a in list(srcs) + list(lands)]
    if starting:
        out = pl.pallas_call(
            body, name=name, in_specs=[HBM] * (2 * n) + [ANY], out_specs=[SEM] * (2 * n) + [HBM] * (2 * n),
            out_shape=[pltpu.SemaphoreType.DMA((n_sems,))] * (2 * n) + thru,
            input_output_aliases={i: 2 * n + i for i in range(2 * n)},
            compiler_params=pltpu.CompilerParams(has_side_effects=IN_FLIGHT))(
                *[_in_hbm(a) for a in srcs], *[_in_hbm(a) for a in lands], after)
        return [(out[i], out[n + i], out[2 * n + i], out[3 * n + i]) for i in range(n)]
    out = pl.pallas_call(
        body, name=name, in_specs=[HBM] * (2 * n) + [SEM] * (2 * n) + [ANY], out_specs=[HBM] * (2 * n),
        out_shape=thru, input_output_aliases={i: i for i in range(2 * n)},
        compiler_params=pltpu.CompilerParams(has_side_effects=IN_FLIGHT))(
            *srcs, *lands, *[fl[0] for fl in flight], *[fl[1] for fl in flight], after)
    return out[:n], out[n:]


def _sum8(land, vec, me):
    R = vec.shape[0]

    def body(me_ref, land_ref, vec_ref, o_ref):
        acc = jnp.zeros((R, 128), F32)
        for d in range(8):
            acc = acc + jnp.where(me_ref[0] == d, vec_ref[...], land_ref[d])
        o_ref[...] = acc

    grid_spec = pltpu.PrefetchScalarGridSpec(
        num_scalar_prefetch=1, grid=(1,),
        in_specs=[pl.BlockSpec((8, R, 128), lambda i, m: (0, 0, 0)), pl.BlockSpec((R, 128), lambda i, m: (0, 0))],
        out_specs=pl.BlockSpec((R, 128), lambda i, m: (0, 0)))
    return pl.pallas_call(body, name="sum8", grid_spec=grid_spec, out_shape=jax.ShapeDtypeStruct((R, 128), F32),
                          compiler_params=_params(("arbitrary",)))(me.reshape(1), land, vec)


def _swap_copies(i, src, got, send, recv):
    x, y, c = _mesh_pos()
    return [pltpu.make_async_remote_copy(src_ref=src.at[1 - c], dst_ref=got, send_sem=send.at[0], recv_sem=recv.at[0],
                                         device_id=(x, y, 1 - c), device_id_type=MESH)]


def _gather8_copies(i, src, land, send, recv):
    x, y, c = _mesh_pos()
    me = 4 * x + 2 * y + c
    peers = [(x, y, 1 - c)] + [(px, py, pc) for px, py in _other_chips(x, y) for pc in (c, 1 - c)]
    return [pltpu.make_async_remote_copy(src_ref=src, dst_ref=land.at[me], send_sem=send.at[k], recv_sem=recv.at[k],
                                         device_id=peer, device_id_type=MESH) for k, peer in enumerate(peers)]


def _scatter_copy(src, got, send, recv, f, chip, c):
    px, py = chip
    return pltpu.make_async_remote_copy(src_ref=src.at[2 * px + py], dst_ref=got.at[f], send_sem=send.at[f],
                                        recv_sem=recv.at[f], device_id=(px, py, c), device_id_type=MESH)


def _scatter_start(sums, *, name):
    nw = len(sums)

    def body(*refs):
        src, got = refs[:nw], refs[nw:2 * nw]
        send, recv = refs[2 * nw:3 * nw], refs[3 * nw:4 * nw]
        x, y, c = _mesh_pos()
        for i in range(nw):
            for f, chip in enumerate(_other_chips(x, y)):
                _scatter_copy(src[i], got[i], send[i], recv[i], f, chip, c).start()

    lands = [lax.empty((3,) + s.shape[1:], BF16) for s in sums]
    out = pl.pallas_call(
        body, name=name, in_specs=[HBM] * (2 * nw), out_specs=[SEM] * (2 * nw) + [HBM] * (2 * nw),
        out_shape=[pltpu.SemaphoreType.DMA((3,))] * (2 * nw)
        + [pltpu.HBM(s.shape, BF16) for s in sums] + [pltpu.HBM(l.shape, BF16) for l in lands],
        input_output_aliases={i: 2 * nw + i for i in range(2 * nw)},
        compiler_params=pltpu.CompilerParams(has_side_effects=IN_FLIGHT))(
            *[_in_hbm(s) for s in sums], *[_in_hbm(l) for l in lands])
    return [(out[i], out[nw + i], out[2 * nw + i], out[3 * nw + i]) for i in range(nw)]


def _scatter_wait(flight, after):
    nw = len(flight)

    def body(*refs):
        src, got = refs[:nw], refs[nw:2 * nw]
        send, recv = refs[2 * nw:3 * nw], refs[3 * nw:4 * nw]
        x, y, c = _mesh_pos()
        for i in range(nw):
            for f, chip in enumerate(_other_chips(x, y)):
                cp = _scatter_copy(src[i], got[i], send[i], recv[i], f, chip, c)
                cp.wait_send()
                cp.wait_recv()

    sums, lands = [fl[2] for fl in flight], [fl[3] for fl in flight]
    out = pl.pallas_call(
        body, name="scatter_wait", in_specs=[HBM] * (2 * nw) + [SEM] * (2 * nw) + [ANY], out_specs=[HBM] * (2 * nw),
        out_shape=[pltpu.HBM(s.shape, BF16) for s in sums] + [pltpu.HBM(l.shape, BF16) for l in lands],
        input_output_aliases={i: i for i in range(2 * nw)},
        compiler_params=pltpu.CompilerParams(has_side_effects=IN_FLIGHT))(
            *sums, *lands, *[fl[0] for fl in flight], *[fl[1] for fl in flight], after)
    return out[:nw], out[nw:]


def _join_halves(ws, shards):
    nw = len(ws)

    def body(*refs):
        buf = refs[nw:2 * nw]
        send_sems, recv_sems = refs[2 * nw:]
        x, y, c = _mesh_pos()
        sibling = (x, y, 1 - c)

        def copy(i, w, half):
            region = w.half_of(buf[i], half)
            return pltpu.make_async_remote_copy(src_ref=region, dst_ref=region, send_sem=send_sems.at[i],
                                                recv_sem=recv_sems.at[i], device_id=sibling, device_id_type=MESH)

        sends = [copy(i, w, c) for i, w in enumerate(ws)]
        for cp in sends:
            cp.start()
        for i, w in enumerate(ws):
            copy(i, w, 1 - c).wait_recv()
        for cp in sends:
            cp.wait_send()

    return pl.pallas_call(
        body, name="join_halves", in_specs=[ANY] * nw, out_specs=[ANY] * nw,
        out_shape=[jax.ShapeDtypeStruct((w.L, w.ks, w.ns), F32) for w in ws],
        input_output_aliases={i: i for i in range(nw)},
        scratch_shapes=[pltpu.SemaphoreType.DMA((nw,)), pltpu.SemaphoreType.DMA((nw,))],
        compiler_params=_params(has_side_effects=True))(*shards)


def _allreduce_small(vec):
    R = vec.shape[0]

    def body(x_ref, o_ref, buf, send_sems, recv_sems):
        x, y, c = _mesh_pos()
        me, sibling = (x, y, c), (x, y, 1 - c)
        chips = _other_chips(x, y)

        def slot(px, py, pc):
            return buf.at[4 * px + 2 * py + pc]

        def copy(k, block, to, src=None):
            return pltpu.make_async_remote_copy(src_ref=slot(*block) if src is None else src, dst_ref=slot(*block),
                                                send_sem=send_sems.at[k], recv_sem=recv_sems.at[k], device_id=to,
                                                device_id_type=MESH)

        first = [copy(0, me, sibling, src=x_ref)] + [copy(1 + f, me, (*chip, c), src=x_ref)
                                                     for f, chip in enumerate(chips)]
        for cp in first:
            cp.start()
        passed = [copy(4 + f, (*chip, c), sibling) for f, chip in enumerate(chips)]
        for f, chip in enumerate(chips):
            copy(1 + f, (*chip, c), me).wait_recv()
            passed[f].start()
        copy(0, sibling, me).wait_recv()
        for f, chip in enumerate(chips):
            copy(4 + f, (*chip, 1 - c), me).wait_recv()
        for cp in first + passed:
            cp.wait_send()
        slot(*me)[...] = x_ref[...]
        acc = buf[0]
        for d in range(1, 8):
            acc = acc + buf[d]
        o_ref[...] = acc

    return pl.pallas_call(
        body, name="allreduce_small", in_specs=[pl.BlockSpec(memory_space=pltpu.VMEM)],
        out_specs=pl.BlockSpec(memory_space=pltpu.VMEM), out_shape=jax.ShapeDtypeStruct((R, 128), F32),
        scratch_shapes=[pltpu.VMEM((8, R, 128), F32), pltpu.SemaphoreType.DMA((7,)), pltpu.SemaphoreType.DMA((7,))],
        compiler_params=_params())(vec)


def _pack(parts):
    flat = jnp.concatenate([p.reshape(-1).astype(F32) for p in parts])
    n = flat.shape[0]
    pad = (-n) % (64 * 128)
    return jnp.pad(flat, (0, pad)).reshape(-1, 128)


def _unpack(vec, shapes):
    flat = vec.reshape(-1)
    out, off = [], 0
    for s in shapes:
        n = int(np.prod(s))
        out.append(flat[off:off + n].reshape(s))
        off += n
    return out


def kernel(x, a_norm_g, a_w_in, a_v_norm_g, a_w_s, a_b_s, a_w_out, kv_norm_g, w_kv, b_norm_g, b_w_q, b_rel_bias, b_w_o, f_norm_g, f_w_in, f_conv_w, f_conv_b, f_w_down, final_norm_g, loss_target, m_a_norm_g, m_a_w_in, m_a_v_norm_g, m_a_w_s, m_a_b_s, m_a_w_out, m_kv_norm_g, m_w_kv, m_b_norm_g, m_b_w_q, m_b_rel_bias, m_b_w_o, m_f_norm_g, m_f_w_in, m_f_conv_w, m_f_conv_b, m_f_w_down, m_final_norm_g, v_a_norm_g, v_a_w_in, v_a_v_norm_g, v_a_w_s, v_a_b_s, v_a_w_out, v_kv_norm_g, v_w_kv, v_b_norm_g, v_b_w_q, v_b_rel_bias, v_b_w_o, v_f_norm_g, v_f_w_in, v_f_conv_w, v_f_conv_b, v_f_w_down, v_final_norm_g):
    B, S, D = x.shape
    T = B * S
    xi, yi, ci = lax.axis_index("x"), lax.axis_index("y"), lax.axis_index("c")
    j_me = (2 * xi + yi).astype(jnp.int32)
    core = ci.astype(jnp.int32)
    pos = jnp.stack([j_me, core])

    w_shards = {"a_w_in": (a_w_in, False), "a_w_out": (a_w_out, True), "w_kv": (w_kv[None], False),
                "b_w_q": (b_w_q, True), "b_w_o": (b_w_o, True), "f_w_in": (f_w_in, False), "f_w_down": (f_w_down, True)}
    names = list(w_shards)
    ws = [_W(n, w_shards[n][0], w_shards[n][1]) for n in names]
    g_shards = {"a_w_in": (a_w_in, False), "a_w_out": (a_w_out, True),
                "f_w_in0": (f_w_in[0:1], False), "f_w_down0": (f_w_down[0:1], True),
                "w_kv": (w_kv[None], False), "b_w_q": (b_w_q, True), "b_w_o": (b_w_o, True),
                "f_w_in1": (f_w_in[1:2], False), "f_w_down1": (f_w_down[1:2], True)}
    g_names = list(g_shards)
    g_ws = {n: _W(n, *g_shards[n], direct=n not in ("a_w_in", "a_w_out", "f_w_in0")) for n in g_names}

    Wd = a_w_in.shape[1]
    GW = a_v_norm_g.shape[1] * N_CHIPS
    F2 = f_conv_w.shape[2] * N_CHIPS
    Fh = F2 // 2
    nsd, nsg, nsf = a_norm_g.shape[1], a_v_norm_g.shape[1], f_conv_w.shape[2]
    own = (ci == 0).astype(F32)
    place = lambda sh, width, n: lax.dynamic_update_slice_in_dim(
        jnp.zeros(sh.shape[:-1] + (width,), F32), sh * own, j_me * n, axis=sh.ndim - 1)
    def tied(x, flight):
        x, thru = lax.optimization_barrier((x, flight[0][2]))
        return x, [flight[0][:2] + (thru,) + flight[0][3:]] + flight[1:]

    gathered = _allreduce_small(_pack([place(a_norm_g, Wd, nsd), place(a_v_norm_g, GW, nsg),
                                       place(f_conv_w, F2, nsf)]))
    a_g, a_vg, conv_w = _unpack(gathered, [(1, Wd), (1, GW), (2, 3, F2)])
    first, rest = g_names[:4], g_names[4:]
    flight = dict(zip(first, _gather_start([g_ws[n] for n in first], [g_shards[n][0].astype(BF16) for n in first],
                                           gathered, name="gather_start_first")))
    (fi, fd, kv_w, qw, ow), (flight[first[0]],) = tied((f_w_in, f_w_down, w_kv, b_w_q, b_w_o), [flight[first[0]]])
    late = {"w_kv": kv_w[None], "b_w_q": qw, "b_w_o": ow, "f_w_in1": fi[1:2], "f_w_down1": fd[1:2]}
    flight.update(zip(rest, _gather_start([g_ws[n] for n in rest], [late[n].astype(BF16) for n in rest], kv_w,
                                          name="gather_start_rest")))
    full = {}

    def arrive(group, after, tag):
        gw = [g_ws[n] for n in group]
        sh, fu = _gather_wait(gw, [flight[n] for n in group], after, name=f"gather_wait_{tag}")
        full.update(zip(group, _gather_finish(gw, sh, fu, name=f"gather_finish_{tag}")))
    conv_w2 = conv_w.reshape(2, 3, 2, Fh).transpose(0, 2, 1, 3)
    conv_b2 = f_conv_b.reshape(2, 2, Fh)

    h0 = x.reshape(T, D)
    target = loss_target.reshape(T, D)
    bs_tile = jnp.repeat(a_b_s[0].T, GROUP_DIM, axis=1)
    ws_a = a_w_s[0]
    scale = HEAD_DIM ** -0.5
    HD = b_w_q.shape[2]
    H = HD // HEAD_DIM
    n_rel = b_rel_bias.shape[-1]
    frow, (flight["w_kv"],) = tied(b_rel_bias[0][:, _bias_index()].reshape(H, 1, F_LEN), [flight["w_kv"]])
    bias = _bias_expand(frow)

    def ffn_fwd(h, l, loss=None):
        out = _ffn_fwd(h, full[f"f_w_in{l}"], f_norm_g[l], conv_w2[l], conv_b2[l], full[f"f_w_down{l}"], S,
                       loss=loss, name=f"ffn{l}")
        yff, a, c, n = out[1:5]
        return (out[0] if loss is None else (out[0], out[5], out[6])), (a, c, n, yff)

    arrive(["a_w_in", "a_w_out"], bias, "a")
    h1, zp, out_a, n_a = _mixer_a_fwd(h0, full["a_w_in"], a_g[0], a_vg, ws_a, bs_tile, full["a_w_out"])
    arrive(["f_w_in0"], h1, "f0")
    yff0, a0, c0, n0 = _ffn_fwd(h1, full["f_w_in0"], f_norm_g[0], conv_w2[0], conv_b2[0], None, S, name="ffn0_in")
    arrive(["f_w_down0"], yff0, "fd0")
    h2, saved0 = _mm(yff0, full["f_w_down0"], res=h1, name="ffn0_down"), (a0, c0, n0, yff0)
    arrive(["w_kv", "b_w_q", "b_w_o"], h2, "b")
    arrive(["f_w_in1", "f_w_down1"], h2, "f1")
    q, kv, n_q, n_kv = _qkv_fwd(h2, full["b_w_q"], b_norm_g[0], full["w_kv"], kv_norm_g, scale)
    kv4, q3 = kv.reshape(2, B, S, HD), q.reshape(B, S, HD)
    o = _attn_fwd(q3, kv4, bias, B, S).reshape(T, HD)
    h3 = _mm(o, full["b_w_o"], res=h2, name="attn_out")
    (dh, loss8, dg_final), saved1 = ffn_fwd(h3, 1, loss=(final_norm_g, target))

    units = {}

    in_flight = {}

    def swap_start(group, tag, carry):
        us = [units[n] for n in group]
        lands = [lax.empty(u.shape[1:], BF16) for u in us]
        carry, flight = tied(carry, _split_copies(f"swap_start_{tag}", us, lands, 1, _swap_copies, after=carry))
        return (group, tag, flight), carry

    def reduce_start(swap, after):
        group, tag, flight = swap
        us, got = _split_copies(f"swap_wait_{tag}", [fl[2] for fl in flight], [fl[3] for fl in flight], 1,
                                _swap_copies, flight=flight, after=after)
        sums = [_add_pair(u, g_, core, name=f"pair_{n}") for n, u, g_ in zip(group, us, got)]
        after, flight = tied(after, _scatter_start(sums, name=f"scatter_start_{tag}"))
        in_flight.update(zip(group, flight))
        return after

    def ffn_bwd(dh, h, saved, l, early):
        a, c, n, yff = saved
        units[f"f_w_down{l}"] = _mm_tn(yff, dh, rows_are_shards=True, name=f"ffn{l}_down_dw")
        dh_in = dh
        if early:
            sw, dh_in = swap_start([f"f_w_down{l}"], f"fd{l}", dh)
        dyff = _mm(dh_in, full[f"f_w_down{l}"], trans_w=True, out_dtype=BF16, name=f"ffn{l}_down_dx")
        if early:
            dyff = reduce_start(sw, dyff)
        da, dcw, dcb = _conv_bwd(a, c, dyff, conv_w2[l], S)
        units[f"f_w_in{l}"] = _mm_tn(n, da, split_y=True, name=f"ffn{l}_in_dw")
        sw, da = swap_start([f"f_w_in{l}"] if early else [f"f_w_down{l}", f"f_w_in{l}"], f"f{l}", da)
        dh, dg = _mm(da, full[f"f_w_in{l}"], trans_w=True, split_x=True, bwd=(h, f_norm_g[l], dh),
                     name=f"ffn{l}_in_dx")
        return reduce_start(sw, dh), dg, dcw, dcb

    dh, dg_f1, dcw1, dcb1 = ffn_bwd(dh, h3, saved1, 1, False)
    do = _mm(dh, full["b_w_o"], trans_w=True, out_dtype=BF16, name="attn_out_dx")
    units["b_w_o"] = _mm_tn(o, dh, rows_are_shards=True, name="b_w_o_dw")
    dq, dkv, dbias = _attn_bwd(q3, kv4, bias, do.reshape(B, S, HD), B, S)
    dq, d_rel = lax.optimization_barrier((dq, _bias_reduce(dbias, n_rel)))
    d_rel = d_rel.reshape(1, H, n_rel)
    dq, dkv = dq.reshape(T, HD), dkv.reshape(2, T, HD)
    units["b_w_q"] = _mm_tn(n_q, dq, rows_are_shards=True, name="b_w_q_dw")
    units["w_kv"] = _mm_tn(n_kv, dkv, split_y=True, name="w_kv_dw")
    sw, dkv = swap_start(["b_w_o", "b_w_q", "w_kv"], "b", dkv)
    dh, dg_b, dg_kv = _qkv_dx(dq, full["b_w_q"], b_norm_g[0], dkv, full["w_kv"], kv_norm_g, h2, dh)
    dh = reduce_start(sw, dh)
    dh, dg_f0, dcw0, dcb0 = ffn_bwd(dh, h1, saved0, 0, True)
    units["a_w_out"] = _mm_tn(out_a, dh, rows_are_shards=True, name="a_w_out_dw")
    sw, dh_in = swap_start(["a_w_out"], "ao", dh)
    d_out = _mm(dh_in, full["a_w_out"], trans_w=True, out_dtype=BF16, name="a_out_dx")
    d_out = reduce_start(sw, d_out)
    dzp, dws, dbs, dgv = _gate_bwd(zp, d_out, a_vg, ws_a, bs_tile)
    units["a_w_in"] = _mm_tn(n_a, dzp, name="a_w_in_dw")
    sw, dzp_in = swap_start(["a_w_in"], "ai", dzp)
    dzp_in = reduce_start(sw, dzp_in)
    grad_x, dg_a = _mm(dzp_in, full["a_w_in"], trans_w=True, bwd=(h0, a_g[0], dh), name="a_in_dx")

    to_flat = lambda d: d.transpose(1, 0, 2).reshape(3, F2)
    small = {"a_norm_g": dg_a, "a_v_norm_g": dgv, "a_w_s": dws[None], "a_b_s": dbs[None], "kv_norm_g": dg_kv[0],
             "b_norm_g": dg_b, "b_rel_bias": d_rel, "f_norm_g": jnp.concatenate([dg_f0, dg_f1], axis=0),
             "f_conv_w": jnp.stack([to_flat(dcw0), to_flat(dcw1)]),
             "f_conv_b": jnp.stack([dcb0.reshape(F2), dcb1.reshape(F2)]), "final_norm_g": dg_final[0]}
    snames = list(small)
    small_vec = _pack([small[n] for n in snames] + [loss8[0:1, 0:1]])
    grad_x, small_flight = tied(grad_x, _split_copies("small_start", [small_vec],
                                                      [lax.empty((8,) + small_vec.shape, F32)], 7, _gather8_copies,
                                                      after=grad_x))

    sums, recv = _scatter_wait([in_flight[n] for n in g_names], grad_x)
    sums, recv = dict(zip(g_names, sums)), dict(zip(g_names, recv))
    halves = []
    for n, w in zip(names, ws):
        if w.L == 1:
            halves.append(_sum_chips(w, sums[n], recv[n], pos, name=f"chips_{n}"))
        else:
            first = _sum_chips(w, sums[n + "0"], recv[n + "0"], pos, name=f"chips_{n}0")
            halves.append(_sum_chips(w, sums[n + "1"], recv[n + "1"], pos, layer=1, into=first, name=f"chips_{n}1"))
    g_big = dict(zip(names, _join_halves(ws, halves)))
    g_big["w_kv"] = g_big["w_kv"][0]

    given = dict(a_norm_g=(a_norm_g, m_a_norm_g, v_a_norm_g), a_w_in=(a_w_in, m_a_w_in, v_a_w_in),
                 a_v_norm_g=(a_v_norm_g, m_a_v_norm_g, v_a_v_norm_g), a_w_s=(a_w_s, m_a_w_s, v_a_w_s),
                 a_b_s=(a_b_s, m_a_b_s, v_a_b_s), a_w_out=(a_w_out, m_a_w_out, v_a_w_out),
                 kv_norm_g=(kv_norm_g, m_kv_norm_g, v_kv_norm_g), w_kv=(w_kv, m_w_kv, v_w_kv),
                 b_norm_g=(b_norm_g, m_b_norm_g, v_b_norm_g), b_w_q=(b_w_q, m_b_w_q, v_b_w_q),
                 b_rel_bias=(b_rel_bias, m_b_rel_bias, v_b_rel_bias), b_w_o=(b_w_o, m_b_w_o, v_b_w_o),
                 f_norm_g=(f_norm_g, m_f_norm_g, v_f_norm_g), f_w_in=(f_w_in, m_f_w_in, v_f_w_in),
                 f_conv_w=(f_conv_w, m_f_conv_w, v_f_conv_w), f_conv_b=(f_conv_b, m_f_conv_b, v_f_conv_b),
                 f_w_down=(f_w_down, m_f_w_down, v_f_w_down), final_norm_g=(final_norm_g, m_final_norm_g, v_final_norm_g))
    order = list(given)
    grads, deltas, new_m, new_v = {}, {}, {}, {}
    for n in names:
        w_, m_, v_ = given[n]
        g_ = g_big[n]
        C = w_.shape[-1]
        d2, m2, v2 = _adamw(w_.reshape(-1, C), g_.reshape(-1, C), m_.reshape(-1, C), v_.reshape(-1, C),
                            name=f"adamw_{n}")
        grads[n], deltas[n], new_m[n], new_v[n] = g_.reshape(w_.shape), d2.reshape(w_.shape), m2.reshape(w_.shape), \
            v2.reshape(w_.shape)
    vecs, lands = _split_copies("small_wait", [small_flight[0][2]], [small_flight[0][3]], 7, _gather8_copies,
                                flight=small_flight, after=deltas[names[-1]])
    red = _sum8(lands[0], vecs[0], (4 * xi + 2 * yi + ci).astype(jnp.int32))
    parts = _unpack(red, [small[n].shape for n in snames] + [(1,)])
    g_small = dict(zip(snames, parts[:-1]))
    loss = parts[-1][0]
    g_small["a_norm_g"] = lax.dynamic_slice_in_dim(g_small["a_norm_g"], j_me * nsd, nsd, axis=1)
    g_small["a_v_norm_g"] = lax.dynamic_slice_in_dim(g_small["a_v_norm_g"], j_me * nsg, nsg, axis=1)
    g_small["f_conv_w"] = lax.dynamic_slice_in_dim(g_small["f_conv_w"], j_me * nsf, nsf, axis=2)

    sm = [n for n in order if n not in names]
    d2, m2, v2 = _adamw(_pack([given[n][0] for n in sm]), _pack([g_small[n].reshape(given[n][0].shape) for n in sm]),
                        _pack([given[n][1] for n in sm]), _pack([given[n][2] for n in sm]), name="adamw_small")
    shapes = [given[n][0].shape for n in sm]
    for n, d_, m_, v_ in zip(sm, _unpack(d2, shapes), _unpack(m2, shapes), _unpack(v2, shapes)):
        grads[n], deltas[n], new_m[n], new_v[n] = g_small[n].reshape(given[n][0].shape), d_, m_, v_

    return (loss, grad_x.reshape(B, S, D), *[grads[n] for n in order], *[deltas[n] for n in order],
            *[new_m[n] for n in order], *[new_v[n] for n in order])
```

```python
import functools
import math

import numpy as np
import jax
import jax.numpy as jnp
from jax import lax
from jax.experimental import pallas as pl
from jax.experimental.pallas import tpu as pltpu

F32 = jnp.float32
BF16 = jnp.bfloat16
MESH = pl.DeviceIdType.MESH

EPS = 1e-6
NEG_INF = -1e30
CHUNK = 64
GMLP_BLOCK = 128
GROUP_DIM = 128
HEAD_DIM = 64
LEFT_CHUNKS = 8
PAD = LEFT_CHUNKS * CHUNK
REL_CLIP = 128
Q_BLOCK = 256
K_SPAN = PAD + Q_BLOCK
F_LEN = K_SPAN + Q_BLOCK
HEADS_PER_STEP = 4
N_CHIPS = 4

ADAM_LR = 0.001
ADAM_B1 = 0.9
ADAM_B2 = 0.999
ADAM_EPS = 1e-08
ADAM_WD = 0.01
ADAM_STEP = 10

VMEM_LIMIT = 56 * 1024 * 1024


def _params(sem=None, **kw):
    if sem is not None:
        kw["dimension_semantics"] = sem
    return pltpu.CompilerParams(vmem_limit_bytes=VMEM_LIMIT, **kw)


def _rms(xf):
    r = lax.rsqrt(jnp.mean(xf * xf, axis=-1, keepdims=True) + EPS)
    return xf * r, r


def _gelu(x, with_grad=False):
    c = math.sqrt(2.0 / math.pi)
    x2 = x * x
    t = jnp.tanh(c * x * (1.0 + 0.044715 * x2))
    half = 0.5 * (1.0 + t)
    if not with_grad:
        return x * half
    return x * half, half + 0.5 * x * (1.0 - t * t) * c * (1.0 + 3.0 * 0.044715 * x2)


def _col_tile(n):
    if n <= 1024:
        return n
    for t in (1408, 1024, 512):
        if n % t == 0:
            return t
    raise ValueError(n)


def _row_tile(t, want):
    while t % want:
        want //= 2
    return want


def _loss_epilogue(h, g_ref, t_ref, dh_ref, loss_ref, dg_ref, first):
    @pl.when(first)
    def _():
        loss_ref[...] = jnp.zeros_like(loss_ref)
        dg_ref[...] = jnp.zeros_like(dg_ref)

    n, r = _rms(h)
    g = g_ref[...]
    e = n * g - t_ref[...]
    loss_ref[...] += 0.5 * jnp.sum(jnp.mean(e * e, axis=-1, keepdims=True), axis=0, keepdims=True)
    dy = e * (1.0 / h.shape[-1])
    dg_ref[...] += jnp.sum(dy * n, axis=0, keepdims=True)
    t = dy * g
    dh_ref[...] = r * (t - n * jnp.mean(t * n, axis=-1, keepdims=True))


def _mm(x, w, *, name, trans_w=False, res=None, out_dtype=F32, bwd=None, split_x=False, tm=512):
    T = x.shape[-2]
    K = 2 * x.shape[-1] if split_x else x.shape[-1]
    N = w.shape[-2] if trans_w else w.shape[-1]
    tm = _row_tile(T, 2 * tm if max(K, N) <= 2048 else tm)
    has_res, has_bwd = res is not None, bwd is not None
    dims = (((1,), (1,)), ((), ())) if trans_w else (((1,), (0,)), ((), ()))

    def body(*refs):
        it = iter(refs)
        x_ref, w_ref = next(it), next(it)
        res_ref = next(it) if has_res else None
        if has_bwd:
            h_ref, bg_ref, dh_ref = next(it), next(it), next(it)
        o_ref = next(it)
        if split_x:
            kh = K // 2
            acc = lax.dot_general(x_ref[0].astype(BF16), w_ref[:, :kh] if trans_w else w_ref[:kh, :], dims,
                                  preferred_element_type=F32)
            acc = acc + lax.dot_general(x_ref[1].astype(BF16), w_ref[:, kh:] if trans_w else w_ref[kh:, :], dims,
                                        preferred_element_type=F32)
        else:
            acc = lax.dot_general(x_ref[...].astype(BF16), w_ref[...], dims, preferred_element_type=F32)
        if has_res:
            acc = acc + res_ref[...]
        if has_bwd:
            dg_ref = next(it)
            n, r = _rms(h_ref[...])

            @pl.when(pl.program_id(0) == 0)
            def _():
                dg_ref[...] = jnp.zeros_like(dg_ref)

            dg_ref[...] += jnp.sum(acc * n, axis=0, keepdims=True)
            t = acc * bg_ref[...]
            o_ref[...] = dh_ref[...] + r * (t - n * jnp.mean(t * n, axis=-1, keepdims=True))
        else:
            o_ref[...] = acc.astype(out_dtype)

    row = lambda width: pl.BlockSpec((tm, width), lambda m: (m, 0))
    ins = [x, w]
    in_specs = [pl.BlockSpec((2, tm, K // 2), lambda m: (0, m, 0)) if split_x else row(K),
                pl.BlockSpec((None,) + w.shape[1:], lambda m: (0, 0, 0), pipeline_mode=pl.Buffered(1))]
    if has_res:
        ins.append(res)
        in_specs.append(row(N))
    out_shape = [jax.ShapeDtypeStruct((T, N), F32 if has_bwd else out_dtype)]
    out_specs = [row(N)]
    if has_bwd:
        h, g, dh = bwd
        ins += [h, g.reshape(1, N), dh]
        in_specs += [row(N), pl.BlockSpec((1, N), lambda m: (0, 0)), row(N)]
        out_shape.append(jax.ShapeDtypeStruct((1, N), F32))
        out_specs.append(pl.BlockSpec((1, N), lambda m: (0, 0)))
    out = pl.pallas_call(body, name=name, grid=(T // tm,), in_specs=in_specs, out_specs=out_specs,
                         out_shape=out_shape, compiler_params=_params(("arbitrary",)))(*ins)
    return out if has_bwd else out[0]


def _mm_tn(x, dy, *, name, rows_are_shards=False, split_y=False, tt=1024):
    T, K = x.shape
    N = 2 * dy.shape[-1] if split_y else dy.shape[-1]
    R, C = (K // N_CHIPS, N // 2) if rows_are_shards else (K // 2, N // N_CHIPS)
    nn = 2 if split_y else 1
    tn = N // nn
    per = N_CHIPS // nn
    assert not (rows_are_shards and split_y)
    tt = _row_tile(T, tt)
    nt = T // tt

    def body(x_ref, y_ref, o_ref, acc_ref):
        t = pl.program_id(1)

        @pl.when(t == 0)
        def _():
            acc_ref[...] = jnp.zeros_like(acc_ref)

        acc_ref[...] += lax.dot_general(x_ref[...], y_ref[...].astype(BF16), (((0,), (0,)), ((), ())),
                                        preferred_element_type=F32)

        @pl.when(t == nt - 1)
        def _():
            if rows_are_shards:
                for h in range(2):
                    o_ref[h] = acc_ref[:, h * C:(h + 1) * C].astype(BF16).reshape(N_CHIPS, R, C)
            else:
                for j in range(per):
                    o_ref[:, j] = acc_ref[:, j * C:(j + 1) * C].astype(BF16).reshape(2, R, C)

    if split_y:
        yspec = pl.BlockSpec((None, tt, tn), lambda n, t: (n, t, 0))
    else:
        yspec = pl.BlockSpec((tt, tn), lambda n, t: (t, 0))
    if rows_are_shards:
        out_spec = pl.BlockSpec((2, N_CHIPS, R, C), lambda n, t: (0, 0, 0, 0))
    else:
        out_spec = pl.BlockSpec((2, per, R, C), lambda n, t: (0, n, 0, 0))
    return pl.pallas_call(body, name=name, grid=(nn, nt),
                          in_specs=[pl.BlockSpec((tt, K), lambda n, t: (t, 0)), yspec], out_specs=out_spec,
                          out_shape=jax.ShapeDtypeStruct((2, N_CHIPS, R, C), BF16),
                          scratch_shapes=[pltpu.VMEM((K, tn), F32)],
                          compiler_params=_params(("arbitrary", "arbitrary")))(x, dy)


def _qkv_fwd(h, wq, gq, wkv, gkv, scale, *, tm=512):
    T, D = h.shape
    HD = wq.shape[-1]
    tm = _row_tile(T, tm)

    def body(h_ref, wq_ref, gq_ref, wkv_ref, gkv_ref, q_ref, kv_ref, nq_ref, nkv_ref):
        n = _rms(h_ref[...])[0]
        nq = (n * gq_ref[...]).astype(BF16)
        nkv = (n * gkv_ref[...]).astype(BF16)
        nq_ref[...] = nq
        nkv_ref[...] = nkv
        q_ref[...] = (jnp.dot(nq, wq_ref[...], preferred_element_type=F32) * scale).astype(BF16)
        kv = jnp.dot(nkv, wkv_ref[...], preferred_element_type=F32)
        kv_ref[0] = kv[:, :HD].astype(BF16)
        kv_ref[1] = kv[:, HD:].astype(BF16)

    row = lambda width: pl.BlockSpec((tm, width), lambda i: (i, 0))
    fixed = lambda *shape: pl.BlockSpec(shape, lambda i: (0,) * len(shape))
    weight = lambda n: pl.BlockSpec((None, D, n), lambda i: (0, 0, 0), pipeline_mode=pl.Buffered(1))
    return pl.pallas_call(
        body, name="qkv", grid=(T // tm,),
        in_specs=[row(D), weight(HD), fixed(1, D), weight(2 * HD), fixed(1, D)],
        out_specs=[row(HD), pl.BlockSpec((2, tm, HD), lambda i: (0, i, 0)), row(D), row(D)],
        out_shape=[jax.ShapeDtypeStruct((T, HD), BF16), jax.ShapeDtypeStruct((2, T, HD), BF16),
                   jax.ShapeDtypeStruct((T, D), BF16), jax.ShapeDtypeStruct((T, D), BF16)],
        compiler_params=_params(("arbitrary",)))(h, wq, gq.reshape(1, D), wkv, gkv.reshape(1, D))


def _qkv_dx(dq, wq, gq, dkv, wkv, gkv, h, dh, *, tm=512):
    T, D = h.shape
    HD = wq.shape[-1]
    tm = _row_tile(T, tm)
    nt = (((1,), (1,)), ((), ()))

    def body(dq_ref, wq_ref, gq_ref, dkv_ref, wkv_ref, gkv_ref, h_ref, dh_ref, o_ref, dgq_ref, dgkv_ref):
        @pl.when(pl.program_id(0) == 0)
        def _():
            dgq_ref[...] = jnp.zeros_like(dgq_ref)
            dgkv_ref[...] = jnp.zeros_like(dgkv_ref)

        n, r = _rms(h_ref[...])
        dnq = lax.dot_general(dq_ref[...], wq_ref[...], nt, preferred_element_type=F32)
        dnkv = (lax.dot_general(dkv_ref[0], wkv_ref[:, :HD], nt, preferred_element_type=F32)
                + lax.dot_general(dkv_ref[1], wkv_ref[:, HD:], nt, preferred_element_type=F32))
        dgq_ref[...] += jnp.sum(dnq * n, axis=0, keepdims=True)
        dgkv_ref[...] += jnp.sum(dnkv * n, axis=0, keepdims=True)
        t = dnq * gq_ref[...] + dnkv * gkv_ref[...]
        o_ref[...] = dh_ref[...] + r * (t - n * jnp.mean(t * n, axis=-1, keepdims=True))

    row = lambda width: pl.BlockSpec((tm, width), lambda i: (i, 0))
    fixed = lambda *shape: pl.BlockSpec(shape, lambda i: (0,) * len(shape))
    weight = lambda n: pl.BlockSpec((None, D, n), lambda i: (0, 0, 0), pipeline_mode=pl.Buffered(1))
    return pl.pallas_call(
        body, name="qkv_dx", grid=(T // tm,),
        in_specs=[row(HD), weight(HD), fixed(1, D), pl.BlockSpec((2, tm, HD), lambda i: (0, i, 0)), weight(2 * HD),
                  fixed(1, D), row(D), row(D)],
        out_specs=[row(D), fixed(1, D), fixed(1, D)],
        out_shape=[jax.ShapeDtypeStruct((T, D), F32), jax.ShapeDtypeStruct((1, D), F32),
                   jax.ShapeDtypeStruct((1, D), F32)],
        compiler_params=_params(("arbitrary",)))(dq, wq, gq.reshape(1, D), dkv, wkv, gkv.reshape(1, D), h, dh)


def _chunk_mask():
    i = lax.broadcasted_iota(jnp.int32, (GMLP_BLOCK, GMLP_BLOCK), 0) // CHUNK
    j = lax.broadcasted_iota(jnp.int32, (GMLP_BLOCK, GMLP_BLOCK), 1) // CHUNK
    return i >= j


def _mixer_a_fwd(h, w_in, g, gv, ws, bs_tile, w_out, *, tm=256):
    T, D = h.shape
    W = w_out.shape[-2]
    G = W // GROUP_DIM
    tm = _row_tile(T, tm)

    def body(h_ref, wi_ref, g_ref, gv_ref, ws_ref, bs_ref, wo_ref, o_ref, zp_ref, ga_ref, n_ref):
        nb = (_rms(h_ref[...])[0] * g_ref[...]).astype(BF16)
        n_ref[...] = nb
        zpb = jnp.dot(nb, wi_ref[...], preferred_element_type=F32).astype(BF16)
        zp_ref[...] = zpb
        z = _gelu(zpb.astype(F32))
        u, v = z[:, :W], z[:, W:]
        vn = _rms(v)[0] * gv_ref[...]
        mask = _chunk_mask()
        for gi in range(G):
            cs = slice(gi * GROUP_DIM, (gi + 1) * GROUP_DIM)
            wg = jnp.where(mask, ws_ref[gi], 0.0).astype(BF16)
            for b in range(tm // GMLP_BLOCK):
                rs = slice(b * GMLP_BLOCK, (b + 1) * GMLP_BLOCK)
                s = jnp.dot(wg, vn[rs, cs].astype(BF16), preferred_element_type=F32) + bs_ref[:, cs]
                ga_ref[rs, cs] = (u[rs, cs] * s).astype(BF16)
        o_ref[...] = h_ref[...] + jnp.dot(ga_ref[...], wo_ref[...], preferred_element_type=F32)

    row = lambda width: pl.BlockSpec((tm, width), lambda i: (i, 0))
    fixed = lambda *shape: pl.BlockSpec(shape, lambda i: (0,) * len(shape))
    weight = lambda k, n: pl.BlockSpec((None, k, n), lambda i: (0, 0, 0), pipeline_mode=pl.Buffered(1))
    return pl.pallas_call(
        body, name="mixer_a", grid=(T // tm,),
        in_specs=[row(D), weight(D, 2 * W), fixed(1, D), fixed(1, W), fixed(G, GMLP_BLOCK, GMLP_BLOCK),
                  fixed(GMLP_BLOCK, W), weight(W, D)],
        out_specs=[row(D), row(2 * W), row(W), row(D)],
        out_shape=[jax.ShapeDtypeStruct((T, D), F32), jax.ShapeDtypeStruct((T, 2 * W), BF16),
                   jax.ShapeDtypeStruct((T, W), BF16), jax.ShapeDtypeStruct((T, D), BF16)],
        compiler_params=_params(("arbitrary",)))(h, w_in, g.reshape(1, D), gv, ws, bs_tile, w_out)


def _gate_bwd(zp, d_out, gv, ws, bs_tile, *, tm=256):
    T, W2 = zp.shape
    W = W2 // 2
    G = W // GROUP_DIM
    tm = _row_tile(T, tm)
    nm = T // tm

    def body(zp_ref, do_ref, gv_ref, ws_ref, bs_ref, dzp_ref, dws_ref, dbs_ref, dgv_ref, du_scr, dvn_scr, dsum_scr):
        i = pl.program_id(0)

        @pl.when(i == 0)
        def _():
            dws_ref[...] = jnp.zeros_like(dws_ref)
            dgv_ref[...] = jnp.zeros_like(dgv_ref)
            dsum_scr[...] = jnp.zeros_like(dsum_scr)

        zp = zp_ref[...].astype(F32)
        z, dz = _gelu(zp, with_grad=True)
        u, v = z[:, :W], z[:, W:]
        n, r = _rms(v)
        gv = gv_ref[...]
        vn = n * gv
        d_out = do_ref[...].astype(F32)
        mask = _chunk_mask()
        for g in range(G):
            cs = slice(g * GROUP_DIM, (g + 1) * GROUP_DIM)
            wg = jnp.where(mask, ws_ref[g], 0.0).astype(BF16)
            dw = jnp.zeros((GMLP_BLOCK, GMLP_BLOCK), F32)
            for b in range(tm // GMLP_BLOCK):
                rs = slice(b * GMLP_BLOCK, (b + 1) * GMLP_BLOCK)
                vb = vn[rs, cs].astype(BF16)
                s = jnp.dot(wg, vb, preferred_element_type=F32) + bs_ref[:, cs]
                du_scr[rs, cs] = d_out[rs, cs] * s
                ds = d_out[rs, cs] * u[rs, cs]
                dsb = ds.astype(BF16)
                dvn_scr[rs, cs] = lax.dot_general(wg, dsb, (((0,), (0,)), ((), ())), preferred_element_type=F32)
                dw = dw + lax.dot_general(dsb, vb, (((1,), (1,)), ((), ())), preferred_element_type=F32)
                dsum_scr[:, cs] += ds
            dws_ref[g] += jnp.where(mask, dw, 0.0)
        dvn = dvn_scr[...]
        dgv_ref[...] += jnp.sum(dvn * n, axis=0, keepdims=True)
        t = dvn * gv
        dv = r * (t - n * jnp.mean(t * n, axis=-1, keepdims=True))
        dzp_ref[:, :W] = (du_scr[...] * dz[:, :W]).astype(BF16)
        dzp_ref[:, W:] = (dv * dz[:, W:]).astype(BF16)

        @pl.when(i == nm - 1)
        def _():
            sel = (lax.broadcasted_iota(jnp.int32, (G, W), 1) // GROUP_DIM
                   == lax.broadcasted_iota(jnp.int32, (G, W), 0)).astype(F32)
            dbs_ref[...] = lax.dot_general(sel, dsum_scr[...], (((1,), (1,)), ((), ())),
                                           precision=lax.Precision.HIGHEST, preferred_element_type=F32)

    return pl.pallas_call(
        body, name="gate_bwd", grid=(nm,),
        in_specs=[pl.BlockSpec((tm, W2), lambda i: (i, 0)), pl.BlockSpec((tm, W), lambda i: (i, 0)),
                  pl.BlockSpec((1, W), lambda i: (0, 0)),
                  pl.BlockSpec((G, GMLP_BLOCK, GMLP_BLOCK), lambda i: (0, 0, 0)),
                  pl.BlockSpec((GMLP_BLOCK, W), lambda i: (0, 0))],
        out_specs=[pl.BlockSpec((tm, W2), lambda i: (i, 0)),
                   pl.BlockSpec((G, GMLP_BLOCK, GMLP_BLOCK), lambda i: (0, 0, 0)),
                   pl.BlockSpec((G, GMLP_BLOCK), lambda i: (0, 0)), pl.BlockSpec((1, W), lambda i: (0, 0))],
        out_shape=[jax.ShapeDtypeStruct((T, W2), BF16), jax.ShapeDtypeStruct((G, GMLP_BLOCK, GMLP_BLOCK), F32),
                   jax.ShapeDtypeStruct((G, GMLP_BLOCK), F32), jax.ShapeDtypeStruct((1, W), F32)],
        scratch_shapes=[pltpu.VMEM((tm, W), F32), pltpu.VMEM((tm, W), F32), pltpu.VMEM((GMLP_BLOCK, W), F32)],
        compiler_params=_params(("arbitrary",)))(zp, d_out, gv, ws, bs_tile)


LANES = 128
HALO = 16


def _taps(ext, w, b):
    return w[2:3] * ext[HALO:] + w[1:2] * pltpu.roll(ext, 1, 0)[HALO:] + w[0:1] * pltpu.roll(ext, 2, 0)[HALO:] + b


def _ffn_fwd(h, w, g, cw, cb, wd, S, *, name, loss=None, tm=256):
    T, D = h.shape
    F = w.shape[-1] // 2
    tc = _col_tile(F)
    tm = _row_tile(S, tm)
    has_loss, has_down = loss is not None, wd is not None
    n_in = 5 + has_down + 2 * has_loss

    def body(*refs):
        h_ref, w_ref, g_ref, cw_ref, cb_ref = refs[:5]
        outs, tail = refs[n_in:-1], refs[-1]
        y_ref, a_ref, c_ref, n_ref = outs[has_down:has_down + 4]
        first = (pl.program_id(0) * tm) % S == 0
        nb = (_rms(h_ref[...])[0] * g_ref[...]).astype(BF16)
        n_ref[...] = nb
        for j in range(F // tc):
            cs = slice(j * tc, (j + 1) * tc)
            conv = []
            for s in range(2):
                acc = jnp.dot(nb, w_ref[:, s * F + j * tc:s * F + (j + 1) * tc], preferred_element_type=F32)
                ab = acc.astype(BF16)
                a_ref[s, :, cs] = ab
                af = ab.astype(F32)
                ext = jnp.concatenate([jnp.where(first, 0.0, tail[s, :, cs]), af], axis=0)
                tail[s, :, cs] = af[tm - HALO:, :]
                cv = _taps(ext, cw_ref[s, :, cs], cb_ref[s:s + 1, cs]).astype(BF16)
                c_ref[s, :, cs] = cv
                conv.append(cv.astype(F32))
            up, gate = conv
            y_ref[:, cs] = (gate * jax.nn.sigmoid(gate) * up).astype(BF16)
        if has_down:
            out = h_ref[...] + jnp.dot(y_ref[...], refs[5][...], preferred_element_type=F32)
            if has_loss:
                _loss_epilogue(out, refs[6], refs[7], outs[0], outs[5], outs[6], pl.program_id(0) == 0)
            else:
                outs[0][...] = out

    row = lambda width: pl.BlockSpec((tm, width), lambda i: (i, 0))
    wide = pl.BlockSpec((2, tm, F), lambda i: (0, i, 0))
    fixed = lambda *shape: pl.BlockSpec(shape, lambda i: (0,) * len(shape))
    once = pl.Buffered(1)
    ins = [h, w, g.reshape(1, D), cw, cb]
    in_specs = [row(D), pl.BlockSpec((None, D, 2 * F), lambda i: (0, 0, 0), pipeline_mode=once), fixed(1, D),
                fixed(2, 3, F), fixed(2, F)]
    out_specs = [row(F), wide, wide, row(D)]
    out_shape = [jax.ShapeDtypeStruct((T, F), BF16), jax.ShapeDtypeStruct((2, T, F), BF16),
                 jax.ShapeDtypeStruct((2, T, F), BF16), jax.ShapeDtypeStruct((T, D), BF16)]
    if has_down:
        ins.append(wd)
        in_specs.append(pl.BlockSpec((None, F, D), lambda i: (0, 0, 0), pipeline_mode=once))
        out_specs.insert(0, row(D))
        out_shape.insert(0, jax.ShapeDtypeStruct((T, D), F32))
    if has_loss:
        ins += [loss[0].reshape(1, D), loss[1]]
        in_specs += [fixed(1, D), row(D)]
        out_specs += [fixed(8, 128), fixed(1, D)]
        out_shape += [jax.ShapeDtypeStruct((8, 128), F32), jax.ShapeDtypeStruct((1, D), F32)]
    return pl.pallas_call(body, name=name, grid=(T // tm,), in_specs=in_specs, out_specs=out_specs,
                          out_shape=out_shape, scratch_shapes=[pltpu.VMEM((2, HALO, F), F32)],
                          compiler_params=_params(("arbitrary",)))(*ins)


def _conv_bwd(a, c, dy, cw, S, *, tm=256):
    _, T, F = a.shape
    tc = _col_tile(F)
    tm = _row_tile(S, tm)
    nm = T // tm
    hb = tm // HALO
    TE = tm + HALO
    nxt = lambda j, i: jnp.minimum((i + 1) * hb, T // HALO - 1)

    def body(a_ref, c_ref, nc_ref, dy_ref, ndy_ref, w_ref, da_ref, dw_ref, db_ref):
        i = pl.program_id(1)
        last = ((i + 1) * tm) % S == 0
        keep_n = jnp.where(last, 0.0, 1.0)

        @pl.when(i == 0)
        def _():
            dw_ref[...] = jnp.zeros_like(dw_ref)
            db_ref[...] = jnp.zeros_like(db_ref)

        for j in range(tc // LANES):
            cs = slice(j * LANES, (j + 1) * LANES)
            dyf = jnp.concatenate([dy_ref[:, cs].astype(F32), ndy_ref[:, cs].astype(F32) * keep_n], axis=0)
            up = jnp.concatenate([c_ref[0, :, cs].astype(F32), nc_ref[0, :, cs].astype(F32)], axis=0)
            gate = jnp.concatenate([c_ref[1, :, cs].astype(F32), nc_ref[1, :, cs].astype(F32)], axis=0)
            sg = jax.nn.sigmoid(gate)
            for s, d in ((0, dyf * (gate * sg)), (1, dyf * up * (sg * (1.0 + gate * (1.0 - sg))))):
                a = a_ref[s, :, cs].astype(F32)
                w = w_ref[s, :, cs]
                u1, u2 = pltpu.roll(d, TE - 1, 0), pltpu.roll(d, TE - 2, 0)
                db_ref[s:s + 1, cs] += jnp.sum(d[:tm], axis=0, keepdims=True)
                dw_ref[s, 2:3, cs] += jnp.sum(d[:tm] * a, axis=0, keepdims=True)
                dw_ref[s, 1:2, cs] += jnp.sum(u1[:tm] * a, axis=0, keepdims=True)
                dw_ref[s, 0:1, cs] += jnp.sum(u2[:tm] * a, axis=0, keepdims=True)
                da_ref[s, :, cs] = (w[2:3] * d + w[1:2] * u1 + w[0:1] * u2)[:tm].astype(BF16)

    cur = pl.BlockSpec((2, tm, tc), lambda j, i: (0, i, j))
    return pl.pallas_call(
        body, name="conv_bwd", grid=(F // tc, nm),
        in_specs=[cur, cur, pl.BlockSpec((2, HALO, tc), lambda j, i: (0, nxt(j, i), j)),
                  pl.BlockSpec((tm, tc), lambda j, i: (i, j)), pl.BlockSpec((HALO, tc), lambda j, i: (nxt(j, i), j)),
                  pl.BlockSpec((2, 3, tc), lambda j, i: (0, 0, j))],
        out_specs=[cur, pl.BlockSpec((2, 3, tc), lambda j, i: (0, 0, j)), pl.BlockSpec((2, tc), lambda j, i: (0, j))],
        out_shape=[jax.ShapeDtypeStruct((2, T, F), BF16), jax.ShapeDtypeStruct((2, 3, F), F32),
                   jax.ShapeDtypeStruct((2, F), F32)],
        compiler_params=_params(("arbitrary", "arbitrary")))(a, c, c, dy, dy, cw)


def _bias_index():
    idx = np.arange(F_LEN)
    d = np.where(idx < K_SPAN, idx, idx - F_LEN)
    return np.clip(PAD - d, -REL_CLIP, REL_CLIP) + REL_CLIP


ROW_GROUP = 16


def _roll_rows(x, sign, unit, steps):
    rows = lax.broadcasted_iota(jnp.int32, x.shape, 0)
    step = 1
    while step < steps:
        shift = unit * step if sign > 0 else F_LEN - unit * step
        x = jnp.where((rows & step) != 0, pltpu.roll(x, shift, 1), x)
        step *= 2
    return x


def _bias_expand(frow):
    H = frow.shape[0]
    groups = Q_BLOCK // ROW_GROUP

    def body(f_ref, o_ref):
        coarse = _roll_rows(jnp.broadcast_to(f_ref[...], (groups, F_LEN)), 1, ROW_GROUP, groups)
        x = jnp.concatenate([jnp.broadcast_to(coarse[a:a + 1], (ROW_GROUP, F_LEN)) for a in range(groups)], axis=0)
        x = _roll_rows(x, 1, 1, ROW_GROUP)[:, :K_SPAN]
        qc = lax.broadcasted_iota(jnp.int32, (Q_BLOCK, K_SPAN), 0) // CHUNK * CHUNK
        kj = lax.broadcasted_iota(jnp.int32, (Q_BLOCK, K_SPAN), 1)
        o_ref[...] = jnp.where((kj >= qc) & (kj < qc + PAD + CHUNK), x, NEG_INF)

    return pl.pallas_call(
        body, name="bias_expand", grid=(H,),
        in_specs=[pl.BlockSpec((None, 1, F_LEN), lambda h: (h, 0, 0))],
        out_specs=pl.BlockSpec((None, Q_BLOCK, K_SPAN), lambda h: (h, 0, 0)),
        out_shape=jax.ShapeDtypeStruct((H, Q_BLOCK, K_SPAN), F32), compiler_params=_params(("arbitrary",)))(frow)


def _bias_reduce(dbias, n_rel):
    H = dbias.shape[0]
    onehot = jnp.asarray((_bias_index()[:, None] == np.arange(n_rel)[None, :]).astype(np.float32), dtype=BF16)

    def body(d_ref, oh_ref, o_ref):
        x = jnp.concatenate([d_ref[...], jnp.zeros((Q_BLOCK, F_LEN - K_SPAN), F32)], axis=1)
        fine = _roll_rows(x, -1, 1, ROW_GROUP).reshape(Q_BLOCK // ROW_GROUP, ROW_GROUP, F_LEN)
        coarse = _roll_rows(jnp.sum(fine, axis=1), -1, ROW_GROUP, Q_BLOCK // ROW_GROUP)
        row = jnp.broadcast_to(jnp.sum(coarse, axis=0, keepdims=True), (8, F_LEN))
        acc = jnp.zeros((8, n_rel), F32)
        for _ in range(3):
            piece = row.astype(BF16)
            acc = acc + jnp.dot(piece, oh_ref[...], preferred_element_type=F32)
            row = row - piece.astype(F32)
        o_ref[...] = acc[0:1]

    return pl.pallas_call(
        body, name="bias_reduce", grid=(H,),
        in_specs=[pl.BlockSpec((None, Q_BLOCK, K_SPAN), lambda h: (h, 0, 0)),
                  pl.BlockSpec((F_LEN, n_rel), lambda h: (0, 0))],
        out_specs=pl.BlockSpec((None, 1, n_rel), lambda h: (h, 0, 0)),
        out_shape=jax.ShapeDtypeStruct((H, 1, n_rel), F32), compiler_params=_params(("arbitrary",)))(dbias, onehot)


def _attn_specs(S, heads=HEADS_PER_STEP):
    hw = heads * HEAD_DIM
    qspec = pl.BlockSpec((None, Q_BLOCK, hw), lambda g, b, i: (b, i, g))
    kspec = pl.BlockSpec((None, None, S, hw), lambda g, b, i: (0, b, 0, g))
    vspec = pl.BlockSpec((None, None, S, hw), lambda g, b, i: (1, b, 0, g))
    bspec = pl.BlockSpec((heads, Q_BLOCK, K_SPAN), lambda g, b, i: (g, 0, 0))
    return hw, qspec, kspec, vspec, bspec


def _span_cases(i, fn):
    short = PAD // Q_BLOCK
    for j in range(short):
        pl.when(i == j)(functools.partial(fn, PAD - j * Q_BLOCK))
    pl.when(i >= short)(functools.partial(fn, 0))


def _key_start(i, off):
    return 0 if off else pl.multiple_of(i * Q_BLOCK - PAD, Q_BLOCK)


def _attn_exp(q_ref, k_ref, b_ref, h, k0, off):
    hs = slice(h * HEAD_DIM, (h + 1) * HEAD_DIM)
    kh = k_ref[pl.ds(k0, K_SPAN - off), hs]
    s = lax.dot_general(q_ref[:, hs], kh, (((1,), (1,)), ((), ())), preferred_element_type=F32) + b_ref[h, :, off:]
    p = jnp.exp(s - jnp.max(s, axis=-1, keepdims=True))
    return p, 1.0 / jnp.sum(p, axis=-1, keepdims=True), kh


def _attn_fwd(q, kv, bias, B, S):
    HD = q.shape[-1]
    heads = min(2 * HEADS_PER_STEP, HD // HEAD_DIM)
    hw, qspec, kspec, vspec, bspec = _attn_specs(S, heads)

    def body(q_ref, k_ref, v_ref, b_ref, o_ref):
        i = pl.program_id(2)

        def block(off):
            k0 = _key_start(i, off)
            outs = []
            for h in range(heads):
                hs = slice(h * HEAD_DIM, (h + 1) * HEAD_DIM)
                p, inv, _ = _attn_exp(q_ref, k_ref, b_ref, h, k0, off)
                outs.append(jnp.dot(p.astype(BF16), v_ref[pl.ds(k0, K_SPAN - off), hs],
                                    preferred_element_type=F32) * inv)
            o_ref[...] = jnp.concatenate(outs, axis=1).astype(BF16)

        _span_cases(i, block)

    return pl.pallas_call(
        body, name="attn_fwd", grid=(HD // hw, B, S // Q_BLOCK), in_specs=[qspec, kspec, vspec, bspec],
        out_specs=qspec, out_shape=jax.ShapeDtypeStruct((B, S, HD), BF16),
        compiler_params=_params(("arbitrary", "arbitrary", "arbitrary")))(q, kv, kv, bias)


def _attn_bwd(q, kv, bias, do, B, S):
    HD = q.shape[-1]
    H = HD // HEAD_DIM
    heads = min(2 * HEADS_PER_STEP, H)
    hw, qspec, kspec, vspec, bspec = _attn_specs(S, heads)
    scale = HEAD_DIM ** -0.5
    nq = S // Q_BLOCK

    def body(q_ref, k_ref, v_ref, b_ref, do_ref, dq_ref, dkv_ref, db_ref, dk_acc, dv_acc):
        b, i = pl.program_id(1), pl.program_id(2)

        @pl.when(i == 0)
        def _():
            dk_acc[...] = jnp.zeros_like(dk_acc)
            dv_acc[...] = jnp.zeros_like(dv_acc)

        @pl.when((i == 0) & (b == 0))
        def _():
            db_ref[...] = jnp.zeros_like(db_ref)

        def block(off):
            k0 = _key_start(i, off)
            keys = pl.ds(k0, K_SPAN - off)
            for h in range(heads):
                hs = slice(h * HEAD_DIM, (h + 1) * HEAD_DIM)
                p, inv, kh = _attn_exp(q_ref, k_ref, b_ref, h, k0, off)
                p = p * inv
                doh = do_ref[:, hs]
                dp = lax.dot_general(doh, v_ref[keys, hs], (((1,), (1,)), ((), ())), preferred_element_type=F32)
                ds = p * (dp - jnp.sum(p * dp, axis=-1, keepdims=True))
                db_ref[h, :, off:] += ds
                dsb = ds.astype(BF16)
                dq_ref[:, hs] = (jnp.dot(dsb, kh, preferred_element_type=F32) * scale).astype(BF16)
                dk_acc[hs, keys] += lax.dot_general(q_ref[:, hs], dsb, (((0,), (0,)), ((), ())),
                                                     preferred_element_type=F32)
                dv_acc[hs, keys] += lax.dot_general(doh, p.astype(BF16), (((0,), (0,)), ((), ())),
                                                     preferred_element_type=F32)

        _span_cases(i, block)

        @pl.when(i == nq - 1)
        def _():
            dkv_ref[0] = dk_acc[...].T.astype(BF16)
            dkv_ref[1] = dv_acc[...].T.astype(BF16)

    return pl.pallas_call(
        body, name="attn_bwd", grid=(HD // hw, B, nq), in_specs=[qspec, kspec, vspec, bspec, qspec],
        out_specs=[qspec, pl.BlockSpec((2, None, S, hw), lambda g, b, i: (0, b, 0, g)), bspec],
        out_shape=[jax.ShapeDtypeStruct((B, S, HD), BF16), jax.ShapeDtypeStruct((2, B, S, HD), BF16),
                   jax.ShapeDtypeStruct((H, Q_BLOCK, K_SPAN), F32)],
        scratch_shapes=[pltpu.VMEM((hw, S), F32), pltpu.VMEM((hw, S), F32)],
        compiler_params=_params(("arbitrary", "arbitrary", "arbitrary")))(q, kv, kv, bias, do)


def _sub_rows(R):
    for cand in (256, 352, 128, 64, 8):
        if R % cand == 0 and R > cand:
            return cand
    return R


def _adamw(w, g, m, v, *, name):
    R, C = w.shape
    tr = _sub_rows(R)

    def body(w_ref, g_ref, m_ref, v_ref, d_ref, nm_ref, nv_ref):
        g = g_ref[...]
        m = ADAM_B1 * m_ref[...] + (1.0 - ADAM_B1) * g
        v = ADAM_B2 * v_ref[...] + (1.0 - ADAM_B2) * (g * g)
        m_hat = m / (1.0 - ADAM_B1 ** ADAM_STEP)
        v_hat = v / (1.0 - ADAM_B2 ** ADAM_STEP)
        d_ref[...] = -ADAM_LR * (m_hat / (jnp.sqrt(v_hat) + ADAM_EPS) + ADAM_WD * w_ref[...])
        nm_ref[...] = m
        nv_ref[...] = v

    spec = pl.BlockSpec((tr, C), lambda i: (i, 0))
    return pl.pallas_call(body, name=name, grid=(R // tr,), in_specs=[spec] * 4, out_specs=[spec] * 3,
                          out_shape=[jax.ShapeDtypeStruct((R, C), F32)] * 3,
                          compiler_params=_params(("arbitrary",)))(w, g, m, v)


def _add_pair(units, got, core, *, name):
    n4, R, C = got.shape
    rows = n4 * R
    tr = 512 if rows % 512 == 0 else R

    def body(c_ref, u_ref, got_ref, o_ref):
        o_ref[...] = (u_ref[...].astype(F32) + got_ref[...].astype(F32)).astype(BF16)

    spec = pl.BlockSpec((tr, C), lambda i, c: (i, 0))
    grid_spec = pltpu.PrefetchScalarGridSpec(
        num_scalar_prefetch=1, grid=(rows // tr,),
        in_specs=[pl.BlockSpec((None, tr, C), lambda i, c: (c[0], i, 0)), spec], out_specs=spec)
    out = pl.pallas_call(body, name=name, grid_spec=grid_spec, out_shape=jax.ShapeDtypeStruct((rows, C), BF16),
                         compiler_params=_params(("arbitrary",)))(core.reshape(1), units.reshape(2, rows, C),
                                                                   got.reshape(rows, C))
    return out.reshape(n4, R, C)


def _sum_chips(w, own, got, pos, *, name, layer=0, into=None):
    _, R, C = own.shape
    tr = _sub_rows(R)
    nr = R // tr

    def body(p_ref, own_ref, got_ref, *rest):
        o_ref = rest[-1]
        o_ref[...] = (own_ref[...].astype(F32) + got_ref[0].astype(F32) + got_ref[1].astype(F32)
                      + got_ref[2].astype(F32))

    if w.row_sharded:
        out_map = lambda i, p: (layer, i, p[1])
    else:
        out_map = lambda i, p: (layer, p[1] * nr + i, 0)
    ins = [pos, own, got]
    in_specs = [pl.BlockSpec((None, tr, C), lambda i, p: (p[0], i, 0)),
                pl.BlockSpec((3, tr, C), lambda i, p: (0, i, 0))]
    alias = {}
    if into is not None:
        ins.append(into)
        in_specs.append(ANY)
        alias = {3: 0}
    grid_spec = pltpu.PrefetchScalarGridSpec(num_scalar_prefetch=1, grid=(nr,), in_specs=in_specs,
                                             out_specs=pl.BlockSpec((None, tr, C), out_map))
    return pl.pallas_call(body, name=name, grid_spec=grid_spec, input_output_aliases=alias,
                          out_shape=jax.ShapeDtypeStruct((w.L, w.ks, w.ns), F32),
                          compiler_params=_params(("arbitrary",)))(*ins)


def _mesh_pos():
    return lax.axis_index("x"), lax.axis_index("y"), lax.axis_index("c")


def _other_chips(x, y):
    return [(1 - x, y), (x, 1 - y), (1 - x, 1 - y)]


ANY = pl.BlockSpec(memory_space=pl.ANY)


class _W:
    def __init__(self, name, shard, row_sharded, direct=False):
        self.name = name
        self.direct = direct
        self.L, ks, ns = shard.shape
        self.row_sharded = row_sharded
        self.K, self.N = (ks * N_CHIPS, ns) if row_sharded else (ks, ns * N_CHIPS)
        self.ks, self.ns = ks, ns

    def shard_of(self, full, j):
        if self.row_sharded:
            return full.at[:, pl.ds(j * self.ks, self.ks), :]
        return full.at[:, :, pl.ds(j * self.ns, self.ns)]

    def half_of(self, shard, c):
        if self.row_sharded:
            return shard.at[:, :, pl.ds(c * (self.ns // 2), self.ns // 2)]
        return shard.at[:, pl.ds(c * (self.ks // 2), self.ks // 2), :]


HBM = pl.BlockSpec(memory_space=pltpu.HBM)
SEM = pl.BlockSpec(memory_space=pltpu.SEMAPHORE)
IN_FLIGHT = pltpu.SideEffectType.DATAFLOW_SIDE_EFFECTING


def _in_hbm(a):
    return pltpu.with_memory_space_constraint(a, pltpu.HBM)


def _gather_start(ws, shards, after, *, name):
    nw = len(ws)

    def body(*refs):
        src, dst = refs[:nw], refs[nw:2 * nw]
        send, recv = refs[2 * nw + 1:3 * nw + 1], refs[3 * nw + 1:4 * nw + 1]
        x, y, c = _mesh_pos()
        me = 2 * x + y
        for i, w in enumerate(ws):
            for f, (px, py) in enumerate(_other_chips(x, y)):
                for e in range(2 if w.direct else 1):
                    k = 2 * f + e
                    pltpu.make_async_remote_copy(
                        src_ref=w.half_of(src[i], c), dst_ref=w.half_of(w.shard_of(dst[i], me), c),
                        send_sem=send[i].at[k], recv_sem=recv[i].at[k], device_id=(px, py, c if e == 0 else 1 - c),
                        device_id_type=MESH).start()

    fulls = [lax.empty((w.L, w.K, w.N), BF16) for w in ws]
    out = pl.pallas_call(
        body, name=name, in_specs=[HBM] * (2 * nw) + [ANY],
        out_specs=[SEM] * (2 * nw) + [HBM] * (2 * nw),
        out_shape=[pltpu.SemaphoreType.DMA((6,))] * (2 * nw)
        + [pltpu.HBM(s.shape, BF16) for s in shards] + [pltpu.HBM(f.shape, BF16) for f in fulls],
        input_output_aliases={i: 2 * nw + i for i in range(2 * nw)},
        compiler_params=pltpu.CompilerParams(has_side_effects=IN_FLIGHT))(
            *[_in_hbm(s) for s in shards], *[_in_hbm(f) for f in fulls], after)
    return [(out[i], out[nw + i], out[2 * nw + i], out[3 * nw + i]) for i in range(nw)]


def _gather_wait(ws, flight, after, *, name):
    nw = len(ws)

    def body(*refs):
        src, dst = refs[:nw], refs[nw:2 * nw]
        send, recv = refs[2 * nw:3 * nw], refs[3 * nw:4 * nw]
        x, y, c = _mesh_pos()
        for i, w in enumerate(ws):
            for f, (px, py) in enumerate(_other_chips(x, y)):
                for e in range(2 if w.direct else 1):
                    k = 2 * f + e
                    landed = w.half_of(w.shard_of(dst[i], 2 * px + py), c if e == 0 else 1 - c)
                    cp = pltpu.make_async_remote_copy(
                        src_ref=w.half_of(src[i], c), dst_ref=landed, send_sem=send[i].at[k], recv_sem=recv[i].at[k],
                        device_id=(px, py, c), device_id_type=MESH)
                    cp.wait_send()
                    cp.wait_recv()

    shards, fulls = [fl[2] for fl in flight], [fl[3] for fl in flight]
    out = pl.pallas_call(
        body, name=name, in_specs=[HBM] * (2 * nw) + [SEM] * (2 * nw) + [ANY],
        out_specs=[HBM] * (2 * nw),
        out_shape=[pltpu.HBM(s.shape, BF16) for s in shards] + [pltpu.HBM(f.shape, BF16) for f in fulls],
        input_output_aliases={i: i for i in range(2 * nw)},
        compiler_params=pltpu.CompilerParams(has_side_effects=IN_FLIGHT))(
            *shards, *fulls, *[fl[0] for fl in flight], *[fl[1] for fl in flight], after)
    return out[:nw], out[nw:]


def _gather_finish(ws, shards, fulls, *, name):
    nw = len(ws)
    forward = not ws[0].direct

    def body(*refs):
        src, dst, stage = refs[:nw], refs[3 * nw:4 * nw], refs[4 * nw:5 * nw]
        send_sems, recv_sems, load_sems, store_sems = refs[5 * nw:]
        x, y, c = _mesh_pos()
        me = 2 * x + y
        sibling = (x, y, 1 - c)
        chips = _other_chips(x, y)

        def fwd(i, w, f, half):
            px, py = chips[f]
            landed = w.half_of(w.shard_of(dst[i], 2 * px + py), half)
            return pltpu.make_async_remote_copy(src_ref=landed, dst_ref=landed, send_sem=send_sems.at[3 * i + f],
                                                recv_sem=recv_sems.at[3 * i + f], device_id=sibling,
                                                device_id_type=MESH)

        loads = [pltpu.make_async_copy(src[i], stage[i], load_sems.at[i]) for i in range(nw)]
        for cp in loads:
            cp.start()
        sends = [fwd(i, w, f, c) for i, w in enumerate(ws) for f in range(3)] if forward else []
        for cp in sends:
            cp.start()
        stores = [pltpu.make_async_copy(stage[i], w.shard_of(dst[i], me), store_sems.at[i])
                  for i, w in enumerate(ws)]
        for ld, st in zip(loads, stores):
            ld.wait()
            st.start()
        if forward:
            for i, w in enumerate(ws):
                for f in range(3):
                    fwd(i, w, f, 1 - c).wait_recv()
        for cp in sends:
            cp.wait_send()
        for cp in stores:
            cp.wait()

    out = pl.pallas_call(
        body, name=name, in_specs=[ANY] * (2 * nw), out_specs=[ANY] * (2 * nw),
        out_shape=[jax.ShapeDtypeStruct(s.shape, BF16) for s in shards]
        + [jax.ShapeDtypeStruct(f.shape, BF16) for f in fulls],
        input_output_aliases={i: i for i in range(2 * nw)},
        scratch_shapes=[pltpu.VMEM((w.L, w.ks, w.ns), BF16) for w in ws]
        + [pltpu.SemaphoreType.DMA((3 * nw,)), pltpu.SemaphoreType.DMA((3 * nw,)), pltpu.SemaphoreType.DMA((nw,)),
           pltpu.SemaphoreType.DMA((nw,))],
        compiler_params=_params(has_side_effects=True))(*shards, *fulls)
    return out[nw:]


def _split_copies(name, srcs, lands, n_sems, copies_of, *, flight=None, after=None):
    n = len(srcs)
    starting = flight is None

    def body(*refs):
        src, land = refs[:n], refs[n:2 * n]
        sems = refs[2 * n + 1:4 * n + 1] if starting else refs[2 * n:4 * n]
        for i in range(n):
            for cp in copies_of(i, src[i], land[i], sems[i], sems[n + i]):
                if starting:
                    cp.start()
                else:
                    cp.wait_send()
                    cp.wait_recv()

    thru = [pltpu.HBM(a.shape, a.dtype) for a in list(srcs) + list(lands)]
    if starting:
        out = pl.pallas_call(
            body, name=name, in_specs=[HBM] * (2 * n) + [ANY], out_specs=[SEM] * (2 * n) + [HBM] * (2 * n),
            out_shape=[pltpu.SemaphoreType.DMA((n_sems,))] * (2 * n) + thru,
            input_output_aliases={i: 2 * n + i for i in range(2 * n)},
            compiler_params=pltpu.CompilerParams(has_side_effects=IN_FLIGHT))(
                *[_in_hbm(a) for a in srcs], *[_in_hbm(a) for a in lands], after)
        return [(out[i], out[n + i], out[2 * n + i], out[3 * n + i]) for i in range(n)]
    out = pl.pallas_call(
        body, name=name, in_specs=[HBM] * (2 * n) + [SEM] * (2 * n) + [ANY], out_specs=[HBM] * (2 * n),
        out_shape=thru, input_output_aliases={i: i for i in range(2 * n)},
        compiler_params=pltpu.CompilerParams(has_side_effects=IN_FLIGHT))(
            *srcs, *lands, *[fl[0] for fl in flight], *[fl[1] for fl in flight], after)
    return out[:n], out[n:]


def _sum8(land, vec, me):
    R = vec.shape[0]

    def body(me_ref, land_ref, vec_ref, o_ref):
        acc = jnp.zeros((R, 128), F32)
        for d in range(8):
            acc = acc + jnp.where(me_ref[0] == d, vec_ref[...], land_ref[d])
        o_ref[...] = acc

    grid_spec = pltpu.PrefetchScalarGridSpec(
        num_scalar_prefetch=1, grid=(1,),
        in_specs=[pl.BlockSpec((8, R, 128), lambda i, m: (0, 0, 0)), pl.BlockSpec((R, 128), lambda i, m: (0, 0))],
        out_specs=pl.BlockSpec((R, 128), lambda i, m: (0, 0)))
    return pl.pallas_call(body, name="sum8", grid_spec=grid_spec, out_shape=jax.ShapeDtypeStruct((R, 128), F32),
                          compiler_params=_params(("arbitrary",)))(me.reshape(1), land, vec)


def _swap_copies(i, src, got, send, recv):
    x, y, c = _mesh_pos()
    return [pltpu.make_async_remote_copy(src_ref=src.at[1 - c], dst_ref=got, send_sem=send.at[0], recv_sem=recv.at[0],
                                         device_id=(x, y, 1 - c), device_id_type=MESH)]


def _gather8_copies(i, src, land, send, recv):
    x, y, c = _mesh_pos()
    me = 4 * x + 2 * y + c
    peers = [(x, y, 1 - c)] + [(px, py, pc) for px, py in _other_chips(x, y) for pc in (c, 1 - c)]
    return [pltpu.make_async_remote_copy(src_ref=src, dst_ref=land.at[me], send_sem=send.at[k], recv_sem=recv.at[k],
                                         device_id=peer, device_id_type=MESH) for k, peer in enumerate(peers)]


def _scatter_copy(src, got, send, recv, f, chip, c):
    px, py = chip
    return pltpu.make_async_remote_copy(src_ref=src.at[2 * px + py], dst_ref=got.at[f], send_sem=send.at[f],
                                        recv_sem=recv.at[f], device_id=(px, py, c), device_id_type=MESH)


def _scatter_start(sums, *, name):
    nw = len(sums)

    def body(*refs):
        src, got = refs[:nw], refs[nw:2 * nw]
        send, recv = refs[2 * nw:3 * nw], refs[3 * nw:4 * nw]
        x, y, c = _mesh_pos()
        for i in range(nw):
            for f, chip in enumerate(_other_chips(x, y)):
                _scatter_copy(src[i], got[i], send[i], recv[i], f, chip, c).start()

    lands = [lax.empty((3,) + s.shape[1:], BF16) for s in sums]
    out = pl.pallas_call(
        body, name=name, in_specs=[HBM] * (2 * nw), out_specs=[SEM] * (2 * nw) + [HBM] * (2 * nw),
        out_shape=[pltpu.SemaphoreType.DMA((3,))] * (2 * nw)
        + [pltpu.HBM(s.shape, BF16) for s in sums] + [pltpu.HBM(l.shape, BF16) for l in lands],
        input_output_aliases={i: 2 * nw + i for i in range(2 * nw)},
        compiler_params=pltpu.CompilerParams(has_side_effects=IN_FLIGHT))(
            *[_in_hbm(s) for s in sums], *[_in_hbm(l) for l in lands])
    return [(out[i], out[nw + i], out[2 * nw + i], out[3 * nw + i]) for i in range(nw)]


def _scatter_wait(flight, after):
    nw = len(flight)

    def body(*refs):
        src, got = refs[:nw], refs[nw:2 * nw]
        send, recv = refs[2 * nw:3 * nw], refs[3 * nw:4 * nw]
        x, y, c = _mesh_pos()
        for i in range(nw):
            for f, chip in enumerate(_other_chips(x, y)):
                cp = _scatter_copy(src[i], got[i], send[i], recv[i], f, chip, c)
                cp.wait_send()
                cp.wait_recv()

    sums, lands = [fl[2] for fl in flight], [fl[3] for fl in flight]
    out = pl.pallas_call(
        body, name="scatter_wait", in_specs=[HBM] * (2 * nw) + [SEM] * (2 * nw) + [ANY], out_specs=[HBM] * (2 * nw),
        out_shape=[pltpu.HBM(s.shape, BF16) for s in sums] + [pltpu.HBM(l.shape, BF16) for l in lands],
        input_output_aliases={i: i for i in range(2 * nw)},
        compiler_params=pltpu.CompilerParams(has_side_effects=IN_FLIGHT))(
            *sums, *lands, *[fl[0] for fl in flight], *[fl[1] for fl in flight], after)
    return out[:nw], out[nw:]


def _join_halves(ws, shards):
    nw = len(ws)

    def body(*refs):
        buf = refs[nw:2 * nw]
        send_sems, recv_sems = refs[2 * nw:]
        x, y, c = _mesh_pos()
        sibling = (x, y, 1 - c)

        def copy(i, w, half):
            region = w.half_of(buf[i], half)
            return pltpu.make_async_remote_copy(src_ref=region, dst_ref=region, send_sem=send_sems.at[i],
                                                recv_sem=recv_sems.at[i], device_id=sibling, device_id_type=MESH)

        sends = [copy(i, w, c) for i, w in enumerate(ws)]
        for cp in sends:
            cp.start()
        for i, w in enumerate(ws):
            copy(i, w, 1 - c).wait_recv()
        for cp in sends:
            cp.wait_send()

    return pl.pallas_call(
        body, name="join_halves", in_specs=[ANY] * nw, out_specs=[ANY] * nw,
        out_shape=[jax.ShapeDtypeStruct((w.L, w.ks, w.ns), F32) for w in ws],
        input_output_aliases={i: i for i in range(nw)},
        scratch_shapes=[pltpu.SemaphoreType.DMA((nw,)), pltpu.SemaphoreType.DMA((nw,))],
        compiler_params=_params(has_side_effects=True))(*shards)


def _allreduce_small(vec):
    R = vec.shape[0]

    def body(x_ref, o_ref, buf, send_sems, recv_sems):
        x, y, c = _mesh_pos()
        me, sibling = (x, y, c), (x, y, 1 - c)
        chips = _other_chips(x, y)

        def slot(px, py, pc):
            return buf.at[4 * px + 2 * py + pc]

        def copy(k, block, to, src=None):
            return pltpu.make_async_remote_copy(src_ref=slot(*block) if src is None else src, dst_ref=slot(*block),
                                                send_sem=send_sems.at[k], recv_sem=recv_sems.at[k], device_id=to,
                                                device_id_type=MESH)

        first = [copy(0, me, sibling, src=x_ref)] + [copy(1 + f, me, (*chip, c), src=x_ref)
                                                     for f, chip in enumerate(chips)]
        for cp in first:
            cp.start()
        passed = [copy(4 + f, (*chip, c), sibling) for f, chip in enumerate(chips)]
        for f, chip in enumerate(chips):
            copy(1 + f, (*chip, c), me).wait_recv()
            passed[f].start()
        copy(0, sibling, me).wait_recv()
        for f, chip in enumerate(chips):
            copy(4 + f, (*chip, 1 - c), me).wait_recv()
        for cp in first + passed:
            cp.wait_send()
        slot(*me)[...] = x_ref[...]
        acc = buf[0]
        for d in range(1, 8):
            acc = acc + buf[d]
        o_ref[...] = acc

    return pl.pallas_call(
        body, name="allreduce_small", in_specs=[pl.BlockSpec(memory_space=pltpu.VMEM)],
        out_specs=pl.BlockSpec(memory_space=pltpu.VMEM), out_shape=jax.ShapeDtypeStruct((R, 128), F32),
        scratch_shapes=[pltpu.VMEM((8, R, 128), F32), pltpu.SemaphoreType.DMA((7,)), pltpu.SemaphoreType.DMA((7,))],
        compiler_params=_params())(vec)


def _pack(parts):
    flat = jnp.concatenate([p.reshape(-1).astype(F32) for p in parts])
    n = flat.shape[0]
    pad = (-n) % (64 * 128)
    return jnp.pad(flat, (0, pad)).reshape(-1, 128)


def _unpack(vec, shapes):
    flat = vec.reshape(-1)
    out, off = [], 0
    for s in shapes:
        n = int(np.prod(s))
        out.append(flat[off:off + n].reshape(s))
        off += n
    return out


def kernel(x, a_norm_g, a_w_in, a_v_norm_g, a_w_s, a_b_s, a_w_out, kv_norm_g, w_kv, b_norm_g, b_w_q, b_rel_bias, b_w_o, f_norm_g, f_w_in, f_conv_w, f_conv_b, f_w_down, final_norm_g, loss_target, m_a_norm_g, m_a_w_in, m_a_v_norm_g, m_a_w_s, m_a_b_s, m_a_w_out, m_kv_norm_g, m_w_kv, m_b_norm_g, m_b_w_q, m_b_rel_bias, m_b_w_o, m_f_norm_g, m_f_w_in, m_f_conv_w, m_f_conv_b, m_f_w_down, m_final_norm_g, v_a_norm_g, v_a_w_in, v_a_v_norm_g, v_a_w_s, v_a_b_s, v_a_w_out, v_kv_norm_g, v_w_kv, v_b_norm_g, v_b_w_q, v_b_rel_bias, v_b_w_o, v_f_norm_g, v_f_w_in, v_f_conv_w, v_f_conv_b, v_f_w_down, v_final_norm_g):
    B, S, D = x.shape
    T = B * S
    xi, yi, ci = lax.axis_index("x"), lax.axis_index("y"), lax.axis_index("c")
    j_me = (2 * xi + yi).astype(jnp.int32)
    core = ci.astype(jnp.int32)
    pos = jnp.stack([j_me, core])

    w_shards = {"a_w_in": (a_w_in, False), "a_w_out": (a_w_out, True), "w_kv": (w_kv[None], False),
                "b_w_q": (b_w_q, True), "b_w_o": (b_w_o, True), "f_w_in": (f_w_in, False), "f_w_down": (f_w_down, True)}
    names = list(w_shards)
    ws = [_W(n, w_shards[n][0], w_shards[n][1]) for n in names]
    g_shards = {"a_w_in": (a_w_in, False), "a_w_out": (a_w_out, True),
                "f_w_in0": (f_w_in[0:1], False), "f_w_down0": (f_w_down[0:1], True),
                "w_kv": (w_kv[None], False), "b_w_q": (b_w_q, True), "b_w_o": (b_w_o, True),
                "f_w_in1": (f_w_in[1:2], False), "f_w_down1": (f_w_down[1:2], True)}
    g_names = list(g_shards)
    g_ws = {n: _W(n, *g_shards[n], direct=n not in ("a_w_in", "a_w_out", "f_w_in0")) for n in g_names}

    Wd = a_w_in.shape[1]
    GW = a_v_norm_g.shape[1] * N_CHIPS
    F2 = f_conv_w.shape[2] * N_CHIPS
    Fh = F2 // 2
    nsd, nsg, nsf = a_norm_g.shape[1], a_v_norm_g.shape[1], f_conv_w.shape[2]
    own = (ci == 0).astype(F32)
    place = lambda sh, width, n: lax.dynamic_update_slice_in_dim(
        jnp.zeros(sh.shape[:-1] + (width,), F32), sh * own, j_me * n, axis=sh.ndim - 1)
    def tied(x, flight):
        x, thru = lax.optimization_barrier((x, flight[0][2]))
        return x, [flight[0][:2] + (thru,) + flight[0][3:]] + flight[1:]

    gathered = _allreduce_small(_pack([place(a_norm_g, Wd, nsd), place(a_v_norm_g, GW, nsg),
                                       place(f_conv_w, F2, nsf)]))
    a_g, a_vg, conv_w = _unpack(gathered, [(1, Wd), (1, GW), (2, 3, F2)])
    first, rest = g_names[:4], g_names[4:]
    flight = dict(zip(first, _gather_start([g_ws[n] for n in first], [g_shards[n][0].astype(BF16) for n in first],
                                           gathered, name="gather_start_first")))
    (fi, fd, kv_w, qw, ow), (flight[first[0]],) = tied((f_w_in, f_w_down, w_kv, b_w_q, b_w_o), [flight[first[0]]])
    late = {"w_kv": kv_w[None], "b_w_q": qw, "b_w_o": ow, "f_w_in1": fi[1:2], "f_w_down1": fd[1:2]}
    flight.update(zip(rest, _gather_start([g_ws[n] for n in rest], [late[n].astype(BF16) for n in rest], kv_w,
                                          name="gather_start_rest")))
    full = {}

    def arrive(group, after, tag):
        gw = [g_ws[n] for n in group]
        sh, fu = _gather_wait(gw, [flight[n] for n in group], after, name=f"gather_wait_{tag}")
        full.update(zip(group, _gather_finish(gw, sh, fu, name=f"gather_finish_{tag}")))
    conv_w2 = conv_w.reshape(2, 3, 2, Fh).transpose(0, 2, 1, 3)
    conv_b2 = f_conv_b.reshape(2, 2, Fh)

    h0 = x.reshape(T, D)
    target = loss_target.reshape(T, D)
    bs_tile = jnp.repeat(a_b_s[0].T, GROUP_DIM, axis=1)
    ws_a = a_w_s[0]
    scale = HEAD_DIM ** -0.5
    HD = b_w_q.shape[2]
    H = HD // HEAD_DIM
    n_rel = b_rel_bias.shape[-1]
    frow, (flight["w_kv"],) = tied(b_rel_bias[0][:, _bias_index()].reshape(H, 1, F_LEN), [flight["w_kv"]])
    bias = _bias_expand(frow)

    def ffn_fwd(h, l, loss=None):
        out = _ffn_fwd(h, full[f"f_w_in{l}"], f_norm_g[l], conv_w2[l], conv_b2[l], full[f"f_w_down{l}"], S,
                       loss=loss, name=f"ffn{l}")
        yff, a, c, n = out[1:5]
        return (out[0] if loss is None else (out[0], out[5], out[6])), (a, c, n, yff)

    arrive(["a_w_in", "a_w_out"], bias, "a")
    h1, zp, out_a, n_a = _mixer_a_fwd(h0, full["a_w_in"], a_g[0], a_vg, ws_a, bs_tile, full["a_w_out"])
    arrive(["f_w_in0"], h1, "f0")
    yff0, a0, c0, n0 = _ffn_fwd(h1, full["f_w_in0"], f_norm_g[0], conv_w2[0], conv_b2[0], None, S, name="ffn0_in")
    arrive(["f_w_down0"], yff0, "fd0")
    h2, saved0 = _mm(yff0, full["f_w_down0"], res=h1, name="ffn0_down"), (a0, c0, n0, yff0)
    arrive(["w_kv", "b_w_q", "b_w_o"], h2, "b")
    arrive(["f_w_in1", "f_w_down1"], h2, "f1")
    q, kv, n_q, n_kv = _qkv_fwd(h2, full["b_w_q"], b_norm_g[0], full["w_kv"], kv_norm_g, scale)
    kv4, q3 = kv.reshape(2, B, S, HD), q.reshape(B, S, HD)
    o = _attn_fwd(q3, kv4, bias, B, S).reshape(T, HD)
    h3 = _mm(o, full["b_w_o"], res=h2, name="attn_out")
    (dh, loss8, dg_final), saved1 = ffn_fwd(h3, 1, loss=(final_norm_g, target))

    units = {}

    in_flight = {}

    def swap_start(group, tag, carry):
        us = [units[n] for n in group]
        lands = [lax.empty(u.shape[1:], BF16) for u in us]
        carry, flight = tied(carry, _split_copies(f"swap_start_{tag}", us, lands, 1, _swap_copies, after=carry))
        return (group, tag, flight), carry

    def reduce_start(swap, after):
        group, tag, flight = swap
        us, got = _split_copies(f"swap_wait_{tag}", [fl[2] for fl in flight], [fl[3] for fl in flight], 1,
                                _swap_copies, flight=flight, after=after)
        sums = [_add_pair(u, g_, core, name=f"pair_{n}") for n, u, g_ in zip(group, us, got)]
        after, flight = tied(after, _scatter_start(sums, name=f"scatter_start_{tag}"))
        in_flight.update(zip(group, flight))
        return after

    def ffn_bwd(dh, h, saved, l, early):
        a, c, n, yff = saved
        units[f"f_w_down{l}"] = _mm_tn(yff, dh, rows_are_shards=True, name=f"ffn{l}_down_dw")
        dh_in = dh
        if early:
            sw, dh_in = swap_start([f"f_w_down{l}"], f"fd{l}", dh)
        dyff = _mm(dh_in, full[f"f_w_down{l}"], trans_w=True, out_dtype=BF16, name=f"ffn{l}_down_dx")
        if early:
            dyff = reduce_start(sw, dyff)
        da, dcw, dcb = _conv_bwd(a, c, dyff, conv_w2[l], S)
        units[f"f_w_in{l}"] = _mm_tn(n, da, split_y=True, name=f"ffn{l}_in_dw")
        sw, da = swap_start([f"f_w_in{l}"] if early else [f"f_w_down{l}", f"f_w_in{l}"], f"f{l}", da)
        dh, dg = _mm(da, full[f"f_w_in{l}"], trans_w=True, split_x=True, bwd=(h, f_norm_g[l], dh),
                     name=f"ffn{l}_in_dx")
        return reduce_start(sw, dh), dg, dcw, dcb

    dh, dg_f1, dcw1, dcb1 = ffn_bwd(dh, h3, saved1, 1, False)
    do = _mm(dh, full["b_w_o"], trans_w=True, out_dtype=BF16, name="attn_out_dx")
    units["b_w_o"] = _mm_tn(o, dh, rows_are_shards=True, name="b_w_o_dw")
    dq, dkv, dbias = _attn_bwd(q3, kv4, bias, do.reshape(B, S, HD), B, S)
    dq, d_rel = lax.optimization_barrier((dq, _bias_reduce(dbias, n_rel)))
    d_rel = d_rel.reshape(1, H, n_rel)
    dq, dkv = dq.reshape(T, HD), dkv.reshape(2, T, HD)
    units["b_w_q"] = _mm_tn(n_q, dq, rows_are_shards=True, name="b_w_q_dw")
    units["w_kv"] = _mm_tn(n_kv, dkv, split_y=True, name="w_kv_dw")
    sw, dkv = swap_start(["b_w_o", "b_w_q", "w_kv"], "b", dkv)
    dh, dg_b, dg_kv = _qkv_dx(dq, full["b_w_q"], b_norm_g[0], dkv, full["w_kv"], kv_norm_g, h2, dh)
    dh = reduce_start(sw, dh)
    dh, dg_f0, dcw0, dcb0 = ffn_bwd(dh, h1, saved0, 0, True)
    units["a_w_out"] = _mm_tn(out_a, dh, rows_are_shards=True, name="a_w_out_dw")
    sw, dh_in = swap_start(["a_w_out"], "ao", dh)
    d_out = _mm(dh_in, full["a_w_out"], trans_w=True, out_dtype=BF16, name="a_out_dx")
    d_out = reduce_start(sw, d_out)
    dzp, dws, dbs, dgv = _gate_bwd(zp, d_out, a_vg, ws_a, bs_tile)
    units["a_w_in"] = _mm_tn(n_a, dzp, name="a_w_in_dw")
    sw, dzp_in = swap_start(["a_w_in"], "ai", dzp)
    dzp_in = reduce_start(sw, dzp_in)
    grad_x, dg_a = _mm(dzp_in, full["a_w_in"], trans_w=True, bwd=(h0, a_g[0], dh), name="a_in_dx")

    to_flat = lambda d: d.transpose(1, 0, 2).reshape(3, F2)
    small = {"a_norm_g": dg_a, "a_v_norm_g": dgv, "a_w_s": dws[None], "a_b_s": dbs[None], "kv_norm_g": dg_kv[0],
             "b_norm_g": dg_b, "b_rel_bias": d_rel, "f_norm_g": jnp.concatenate([dg_f0, dg_f1], axis=0),
             "f_conv_w": jnp.stack([to_flat(dcw0), to_flat(dcw1)]),
             "f_conv_b": jnp.stack([dcb0.reshape(F2), dcb1.reshape(F2)]), "final_norm_g": dg_final[0]}
    snames = list(small)
    small_vec = _pack([small[n] for n in snames] + [loss8[0:1, 0:1]])
    grad_x, small_flight = tied(grad_x, _split_copies("small_start", [small_vec],
                                                      [lax.empty((8,) + small_vec.shape, F32)], 7, _gather8_copies,
                                                      after=grad_x))

    sums, recv = _scatter_wait([in_flight[n] for n in g_names], grad_x)
    sums, recv = dict(zip(g_names, sums)), dict(zip(g_names, recv))
    halves = []
    for n, w in zip(names, ws):
        if w.L == 1:
            halves.append(_sum_chips(w, sums[n], recv[n], pos, name=f"chips_{n}"))
        else:
            first = _sum_chips(w, sums[n + "0"], recv[n + "0"], pos, name=f"chips_{n}0")
            halves.append(_sum_chips(w, sums[n + "1"], recv[n + "1"], pos, layer=1, into=first, name=f"chips_{n}1"))
    g_big = dict(zip(names, _join_halves(ws, halves)))
    g_big["w_kv"] = g_big["w_kv"][0]

    given = dict(a_norm_g=(a_norm_g, m_a_norm_g, v_a_norm_g), a_w_in=(a_w_in, m_a_w_in, v_a_w_in),
                 a_v_norm_g=(a_v_norm_g, m_a_v_norm_g, v_a_v_norm_g), a_w_s=(a_w_s, m_a_w_s, v_a_w_s),
                 a_b_s=(a_b_s, m_a_b_s, v_a_b_s), a_w_out=(a_w_out, m_a_w_out, v_a_w_out),
                 kv_norm_g=(kv_norm_g, m_kv_norm_g, v_kv_norm_g), w_kv=(w_kv, m_w_kv, v_w_kv),
                 b_norm_g=(b_norm_g, m_b_norm_g, v_b_norm_g), b_w_q=(b_w_q, m_b_w_q, v_b_w_q),
                 b_rel_bias=(b_rel_bias, m_b_rel_bias, v_b_rel_bias), b_w_o=(b_w_o, m_b_w_o, v_b_w_o),
                 f_norm_g=(f_norm_g, m_f_norm_g, v_f_norm_g), f_w_in=(f_w_in, m_f_w_in, v_f_w_in),
                 f_conv_w=(f_conv_w, m_f_conv_w, v_f_conv_w), f_conv_b=(f_conv_b, m_f_conv_b, v_f_conv_b),
                 f_w_down=(f_w_down, m_f_w_down, v_f_w_down), final_norm_g=(final_norm_g, m_final_norm_g, v_final_norm_g))
    order = list(given)
    grads, deltas, new_m, new_v = {}, {}, {}, {}
    for n in names:
        w_, m_, v_ = given[n]
        g_ = g_big[n]
        C = w_.shape[-1]
        d2, m2, v2 = _adamw(w_.reshape(-1, C), g_.reshape(-1, C), m_.reshape(-1, C), v_.reshape(-1, C),
                            name=f"adamw_{n}")
        grads[n], deltas[n], new_m[n], new_v[n] = g_.reshape(w_.shape), d2.reshape(w_.shape), m2.reshape(w_.shape), \
            v2.reshape(w_.shape)
    vecs, lands = _split_copies("small_wait", [small_flight[0][2]], [small_flight[0][3]], 7, _gather8_copies,
                                flight=small_flight, after=deltas[names[-1]])
    red = _sum8(lands[0], vecs[0], (4 * xi + 2 * yi + ci).astype(jnp.int32))
    parts = _unpack(red, [small[n].shape for n in snames] + [(1,)])
    g_small = dict(zip(snames, parts[:-1]))
    loss = parts[-1][0]
    g_small["a_norm_g"] = lax.dynamic_slice_in_dim(g_small["a_norm_g"], j_me * nsd, nsd, axis=1)
    g_small["a_v_norm_g"] = lax.dynamic_slice_in_dim(g_small["a_v_norm_g"], j_me * nsg, nsg, axis=1)
    g_small["f_conv_w"] = lax.dynamic_slice_in_dim(g_small["f_conv_w"], j_me * nsf, nsf, axis=2)

    sm = [n for n in order if n not in names]
    d2, m2, v2 = _adamw(_pack([given[n][0] for n in sm]), _pack([g_small[n].reshape(given[n][0].shape) for n in sm]),
                        _pack([given[n][1] for n in sm]), _pack([given[n][2] for n in sm]), name="adamw_small")
    shapes = [given[n][0].shape for n in sm]
    for n, d_, m_, v_ in zip(sm, _unpack(d2, shapes), _unpack(m2, shapes), _unpack(v2, shapes)):
        grads[n], deltas[n], new_m[n], new_v[n] = g_small[n].reshape(given[n][0].shape), d_, m_, v_

    return (loss, grad_x.reshape(B, S, D), *[grads[n] for n in order], *[deltas[n] for n in order],
            *[new_m[n] for n in order], *[new_v[n] for n in order])
```

```python
import functools
import math

import numpy as np
import jax
import jax.numpy as jnp
from jax import lax
from jax.experimental import pallas as pl
from jax.experimental.pallas import tpu as pltpu

F32 = jnp.float32
BF16 = jnp.bfloat16
MESH = pl.DeviceIdType.MESH

EPS = 1e-6
NEG_INF = -1e30
CHUNK = 64
GMLP_BLOCK = 128
GROUP_DIM = 128
HEAD_DIM = 64
LEFT_CHUNKS = 8
PAD = LEFT_CHUNKS * CHUNK
REL_CLIP = 128
Q_BLOCK = 256
K_SPAN = PAD + Q_BLOCK
F_LEN = K_SPAN + Q_BLOCK
HEADS_PER_STEP = 4
N_CHIPS = 4

ADAM_LR = 0.001
ADAM_B1 = 0.9
ADAM_B2 = 0.999
ADAM_EPS = 1e-08
ADAM_WD = 0.01
ADAM_STEP = 10

VMEM_LIMIT = 56 * 1024 * 1024


def _params(sem=None, **kw):
    if sem is not None:
        kw["dimension_semantics"] = sem
    return pltpu.CompilerParams(vmem_limit_bytes=VMEM_LIMIT, **kw)


def _rms(xf):
    r = lax.rsqrt(jnp.mean(xf * xf, axis=-1, keepdims=True) + EPS)
    return xf * r, r


def _gelu(x, with_grad=False):
    c = math.sqrt(2.0 / math.pi)
    x2 = x * x
    t = jnp.tanh(c * x * (1.0 + 0.044715 * x2))
    half = 0.5 * (1.0 + t)
    if not with_grad:
        return x * half
    return x * half, half + 0.5 * x * (1.0 - t * t) * c * (1.0 + 3.0 * 0.044715 * x2)


def _col_tile(n):
    if n <= 1024:
        return n
    for t in (1408, 1024, 512):
        if n % t == 0:
            return t
    raise ValueError(n)


def _row_tile(t, want):
    while t % want:
        want //= 2
    return want


def _loss_epilogue(h, g_ref, t_ref, dh_ref, loss_ref, dg_ref, first):
    @pl.when(first)
    def _():
        loss_ref[...] = jnp.zeros_like(loss_ref)
        dg_ref[...] = jnp.zeros_like(dg_ref)

    n, r = _rms(h)
    g = g_ref[...]
    e = n * g - t_ref[...]
    loss_ref[...] += 0.5 * jnp.sum(jnp.mean(e * e, axis=-1, keepdims=True), axis=0, keepdims=True)
    dy = e * (1.0 / h.shape[-1])
    dg_ref[...] += jnp.sum(dy * n, axis=0, keepdims=True)
    t = dy * g
    dh_ref[...] = r * (t - n * jnp.mean(t * n, axis=-1, keepdims=True))


def _mm(x, w, *, name, trans_w=False, res=None, out_dtype=F32, bwd=None, split_x=False, tm=512):
    T = x.shape[-2]
    K = 2 * x.shape[-1] if split_x else x.shape[-1]
    N = w.shape[-2] if trans_w else w.shape[-1]
    tm = _row_tile(T, 2 * tm if max(K, N) <= 2048 else tm)
    has_res, has_bwd = res is not None, bwd is not None
    dims = (((1,), (1,)), ((), ())) if trans_w else (((1,), (0,)), ((), ()))

    def body(*refs):
        it = iter(refs)
        x_ref, w_ref = next(it), next(it)
        res_ref = next(it) if has_res else None
        if has_bwd:
            h_ref, bg_ref, dh_ref = next(it), next(it), next(it)
        o_ref = next(it)
        if split_x:
            kh = K // 2
            acc = lax.dot_general(x_ref[0].astype(BF16), w_ref[:, :kh] if trans_w else w_ref[:kh, :], dims,
                                  preferred_element_type=F32)
            acc = acc + lax.dot_general(x_ref[1].astype(BF16), w_ref[:, kh:] if trans_w else w_ref[kh:, :], dims,
                                        preferred_element_type=F32)
        else:
            acc = lax.dot_general(x_ref[...].astype(BF16), w_ref[...], dims, preferred_element_type=F32)
        if has_res:
            acc = acc + res_ref[...]
        if has_bwd:
            dg_ref = next(it)
            n, r = _rms(h_ref[...])

            @pl.when(pl.program_id(0) == 0)
            def _():
                dg_ref[...] = jnp.zeros_like(dg_ref)

            dg_ref[...] += jnp.sum(acc * n, axis=0, keepdims=True)
            t = acc * bg_ref[...]
            o_ref[...] = dh_ref[...] + r * (t - n * jnp.mean(t * n, axis=-1, keepdims=True))
        else:
            o_ref[...] = acc.astype(out_dtype)

    row = lambda width: pl.BlockSpec((tm, width), lambda m: (m, 0))
    ins = [x, w]
    in_specs = [pl.BlockSpec((2, tm, K // 2), lambda m: (0, m, 0)) if split_x else row(K),
                pl.BlockSpec((None,) + w.shape[1:], lambda m: (0, 0, 0), pipeline_mode=pl.Buffered(1))]
    if has_res:
        ins.append(res)
        in_specs.append(row(N))
    out_shape = [jax.ShapeDtypeStruct((T, N), F32 if has_bwd else out_dtype)]
    out_specs = [row(N)]
    if has_bwd:
        h, g, dh = bwd
        ins += [h, g.reshape(1, N), dh]
        in_specs += [row(N), pl.BlockSpec((1, N), lambda m: (0, 0)), row(N)]
        out_shape.append(jax.ShapeDtypeStruct((1, N), F32))
        out_specs.append(pl.BlockSpec((1, N), lambda m: (0, 0)))
    out = pl.pallas_call(body, name=name, grid=(T // tm,), in_specs=in_specs, out_specs=out_specs,
                         out_shape=out_shape, compiler_params=_params(("arbitrary",)))(*ins)
    return out if has_bwd else out[0]


def _mm_tn(x, dy, *, name, rows_are_shards=False, split_y=False, tt=1024):
    T, K = x.shape
    N = 2 * dy.shape[-1] if split_y else dy.shape[-1]
    R, C = (K // N_CHIPS, N // 2) if rows_are_shards else (K // 2, N // N_CHIPS)
    nn = 2 if split_y else 1
    tn = N // nn
    per = N_CHIPS // nn
    assert not (rows_are_shards and split_y)
    tt = _row_tile(T, tt)
    nt = T // tt
    resident = nn > 1 and T * K * 2 <= 8 * 2**20

    def body(x_ref, y_ref, o_ref, acc_ref):
        t = pl.program_id(1)

        @pl.when(t == 0)
        def _():
            acc_ref[...] = jnp.zeros_like(acc_ref)

        xs = x_ref[pl.ds(pl.multiple_of(t * tt, tt), tt), :] if resident else x_ref[...]
        acc_ref[...] += lax.dot_general(xs, y_ref[...].astype(BF16), (((0,), (0,)), ((), ())),
                                        preferred_element_type=F32)

        @pl.when(t == nt - 1)
        def _():
            if rows_are_shards:
                for h in range(2):
                    o_ref[h] = acc_ref[:, h * C:(h + 1) * C].astype(BF16).reshape(N_CHIPS, R, C)
            else:
                for j in range(per):
                    o_ref[:, j] = acc_ref[:, j * C:(j + 1) * C].astype(BF16).reshape(2, R, C)

    if split_y:
        yspec = pl.BlockSpec((None, tt, tn), lambda n, t: (n, t, 0))
    else:
        yspec = pl.BlockSpec((tt, tn), lambda n, t: (t, 0))
    if rows_are_shards:
        out_spec = pl.BlockSpec((2, N_CHIPS, R, C), lambda n, t: (0, 0, 0, 0))
    else:
        out_spec = pl.BlockSpec((2, per, R, C), lambda n, t: (0, n, 0, 0))
    if resident:
        xspec = pl.BlockSpec((T, K), lambda n, t: (0, 0), pipeline_mode=pl.Buffered(1))
    else:
        xspec = pl.BlockSpec((tt, K), lambda n, t: (t, 0))
    return pl.pallas_call(body, name=name, grid=(nn, nt),
                          in_specs=[xspec, yspec], out_specs=out_spec,
                          out_shape=jax.ShapeDtypeStruct((2, N_CHIPS, R, C), BF16),
                          scratch_shapes=[pltpu.VMEM((K, tn), F32)],
                          compiler_params=_params(("arbitrary", "arbitrary")))(x, dy)


def _qkv_fwd(h, wq, gq, wkv, gkv, scale, *, tm=512):
    T, D = h.shape
    HD = wq.shape[-1]
    tm = _row_tile(T, tm)

    def body(h_ref, wq_ref, gq_ref, wkv_ref, gkv_ref, q_ref, kv_ref, nq_ref, nkv_ref):
        n = _rms(h_ref[...])[0]
        nq = (n * gq_ref[...]).astype(BF16)
        nkv = (n * gkv_ref[...]).astype(BF16)
        nq_ref[...] = nq
        nkv_ref[...] = nkv
        q_ref[...] = (jnp.dot(nq, wq_ref[...], preferred_element_type=F32) * scale).astype(BF16)
        kv = jnp.dot(nkv, wkv_ref[...], preferred_element_type=F32)
        kv_ref[0] = kv[:, :HD].astype(BF16)
        kv_ref[1] = kv[:, HD:].astype(BF16)

    row = lambda width: pl.BlockSpec((tm, width), lambda i: (i, 0))
    fixed = lambda *shape: pl.BlockSpec(shape, lambda i: (0,) * len(shape))
    weight = lambda n: pl.BlockSpec((None, D, n), lambda i: (0, 0, 0), pipeline_mode=pl.Buffered(1))
    return pl.pallas_call(
        body, name="qkv", grid=(T // tm,),
        in_specs=[row(D), weight(HD), fixed(1, D), weight(2 * HD), fixed(1, D)],
        out_specs=[row(HD), pl.BlockSpec((2, tm, HD), lambda i: (0, i, 0)), row(D), row(D)],
        out_shape=[jax.ShapeDtypeStruct((T, HD), BF16), jax.ShapeDtypeStruct((2, T, HD), BF16),
                   jax.ShapeDtypeStruct((T, D), BF16), jax.ShapeDtypeStruct((T, D), BF16)],
        compiler_params=_params(("arbitrary",)))(h, wq, gq.reshape(1, D), wkv, gkv.reshape(1, D))


def _qkv_dx(dq, wq, gq, dkv, wkv, gkv, h, dh, *, tm=512):
    T, D = h.shape
    HD = wq.shape[-1]
    tm = _row_tile(T, tm)
    nt = (((1,), (1,)), ((), ()))

    def body(dq_ref, wq_ref, gq_ref, dkv_ref, wkv_ref, gkv_ref, h_ref, dh_ref, o_ref, dgq_ref, dgkv_ref):
        @pl.when(pl.program_id(0) == 0)
        def _():
            dgq_ref[...] = jnp.zeros_like(dgq_ref)
            dgkv_ref[...] = jnp.zeros_like(dgkv_ref)

        n, r = _rms(h_ref[...])
        dnq = lax.dot_general(dq_ref[...], wq_ref[...], nt, preferred_element_type=F32)
        dnkv = (lax.dot_general(dkv_ref[0], wkv_ref[:, :HD], nt, preferred_element_type=F32)
                + lax.dot_general(dkv_ref[1], wkv_ref[:, HD:], nt, preferred_element_type=F32))
        dgq_ref[...] += jnp.sum(dnq * n, axis=0, keepdims=True)
        dgkv_ref[...] += jnp.sum(dnkv * n, axis=0, keepdims=True)
        t = dnq * gq_ref[...] + dnkv * gkv_ref[...]
        o_ref[...] = dh_ref[...] + r * (t - n * jnp.mean(t * n, axis=-1, keepdims=True))

    row = lambda width: pl.BlockSpec((tm, width), lambda i: (i, 0))
    fixed = lambda *shape: pl.BlockSpec(shape, lambda i: (0,) * len(shape))
    weight = lambda n: pl.BlockSpec((None, D, n), lambda i: (0, 0, 0), pipeline_mode=pl.Buffered(1))
    return pl.pallas_call(
        body, name="qkv_dx", grid=(T // tm,),
        in_specs=[row(HD), weight(HD), fixed(1, D), pl.BlockSpec((2, tm, HD), lambda i: (0, i, 0)), weight(2 * HD),
                  fixed(1, D), row(D), row(D)],
        out_specs=[row(D), fixed(1, D), fixed(1, D)],
        out_shape=[jax.ShapeDtypeStruct((T, D), F32), jax.ShapeDtypeStruct((1, D), F32),
                   jax.ShapeDtypeStruct((1, D), F32)],
        compiler_params=_params(("arbitrary",)))(dq, wq, gq.reshape(1, D), dkv, wkv, gkv.reshape(1, D), h, dh)


def _chunk_mask():
    i = lax.broadcasted_iota(jnp.int32, (GMLP_BLOCK, GMLP_BLOCK), 0) // CHUNK
    j = lax.broadcasted_iota(jnp.int32, (GMLP_BLOCK, GMLP_BLOCK), 1) // CHUNK
    return i >= j


def _mixer_a_fwd(h, w_in, g, gv, ws, bs_tile, w_out, *, tm=256):
    T, D = h.shape
    W = w_out.shape[-2]
    G = W // GROUP_DIM
    tm = _row_tile(T, tm)

    def body(h_ref, wi_ref, g_ref, gv_ref, ws_ref, bs_ref, wo_ref, o_ref, zp_ref, ga_ref, n_ref):
        nb = (_rms(h_ref[...])[0] * g_ref[...]).astype(BF16)
        n_ref[...] = nb
        zpb = jnp.dot(nb, wi_ref[...], preferred_element_type=F32).astype(BF16)
        zp_ref[...] = zpb
        z = _gelu(zpb.astype(F32))
        u, v = z[:, :W], z[:, W:]
        vn = _rms(v)[0] * gv_ref[...]
        mask = _chunk_mask()
        for gi in range(G):
            cs = slice(gi * GROUP_DIM, (gi + 1) * GROUP_DIM)
            wg = jnp.where(mask, ws_ref[gi], 0.0).astype(BF16)
            for b in range(tm // GMLP_BLOCK):
                rs = slice(b * GMLP_BLOCK, (b + 1) * GMLP_BLOCK)
                s = jnp.dot(wg, vn[rs, cs].astype(BF16), preferred_element_type=F32) + bs_ref[:, cs]
                ga_ref[rs, cs] = (u[rs, cs] * s).astype(BF16)
        o_ref[...] = h_ref[...] + jnp.dot(ga_ref[...], wo_ref[...], preferred_element_type=F32)

    row = lambda width: pl.BlockSpec((tm, width), lambda i: (i, 0))
    fixed = lambda *shape: pl.BlockSpec(shape, lambda i: (0,) * len(shape))
    weight = lambda k, n: pl.BlockSpec((None, k, n), lambda i: (0, 0, 0), pipeline_mode=pl.Buffered(1))
    return pl.pallas_call(
        body, name="mixer_a", grid=(T // tm,),
        in_specs=[row(D), weight(D, 2 * W), fixed(1, D), fixed(1, W), fixed(G, GMLP_BLOCK, GMLP_BLOCK),
                  fixed(GMLP_BLOCK, W), weight(W, D)],
        out_specs=[row(D), row(2 * W), row(W), row(D)],
        out_shape=[jax.ShapeDtypeStruct((T, D), F32), jax.ShapeDtypeStruct((T, 2 * W), BF16),
                   jax.ShapeDtypeStruct((T, W), BF16), jax.ShapeDtypeStruct((T, D), BF16)],
        compiler_params=_params(("arbitrary",)))(h, w_in, g.reshape(1, D), gv, ws, bs_tile, w_out)


def _gate_bwd(zp, d_out, gv, ws, bs_tile, *, tm=256):
    T, W2 = zp.shape
    W = W2 // 2
    G = W // GROUP_DIM
    tm = _row_tile(T, tm)
    nm = T // tm

    def body(zp_ref, do_ref, gv_ref, ws_ref, bs_ref, dzp_ref, dws_ref, dbs_ref, dgv_ref, du_scr, dvn_scr, dsum_scr):
        i = pl.program_id(0)

        @pl.when(i == 0)
        def _():
            dws_ref[...] = jnp.zeros_like(dws_ref)
            dgv_ref[...] = jnp.zeros_like(dgv_ref)
            dsum_scr[...] = jnp.zeros_like(dsum_scr)

        zp = zp_ref[...].astype(F32)
        z, dz = _gelu(zp, with_grad=True)
        u, v = z[:, :W], z[:, W:]
        n, r = _rms(v)
        gv = gv_ref[...]
        vn = n * gv
        d_out = do_ref[...].astype(F32)
        mask = _chunk_mask()
        for g in range(G):
            cs = slice(g * GROUP_DIM, (g + 1) * GROUP_DIM)
            wg = jnp.where(mask, ws_ref[g], 0.0).astype(BF16)
            dw = jnp.zeros((GMLP_BLOCK, GMLP_BLOCK), F32)
            for b in range(tm // GMLP_BLOCK):
                rs = slice(b * GMLP_BLOCK, (b + 1) * GMLP_BLOCK)
                vb = vn[rs, cs].astype(BF16)
                s = jnp.dot(wg, vb, preferred_element_type=F32) + bs_ref[:, cs]
                du_scr[rs, cs] = d_out[rs, cs] * s
                ds = d_out[rs, cs] * u[rs, cs]
                dsb = ds.astype(BF16)
                dvn_scr[rs, cs] = lax.dot_general(wg, dsb, (((0,), (0,)), ((), ())), preferred_element_type=F32)
                dw = dw + lax.dot_general(dsb, vb, (((1,), (1,)), ((), ())), preferred_element_type=F32)
                dsum_scr[:, cs] += ds
            dws_ref[g] += jnp.where(mask, dw, 0.0)
        dvn = dvn_scr[...]
        dgv_ref[...] += jnp.sum(dvn * n, axis=0, keepdims=True)
        t = dvn * gv
        dv = r * (t - n * jnp.mean(t * n, axis=-1, keepdims=True))
        dzp_ref[:, :W] = (du_scr[...] * dz[:, :W]).astype(BF16)
        dzp_ref[:, W:] = (dv * dz[:, W:]).astype(BF16)

        @pl.when(i == nm - 1)
        def _():
            sel = (lax.broadcasted_iota(jnp.int32, (G, W), 1) // GROUP_DIM
                   == lax.broadcasted_iota(jnp.int32, (G, W), 0)).astype(F32)
            dbs_ref[...] = lax.dot_general(sel, dsum_scr[...], (((1,), (1,)), ((), ())),
                                           precision=lax.Precision.HIGHEST, preferred_element_type=F32)

    return pl.pallas_call(
        body, name="gate_bwd", grid=(nm,),
        in_specs=[pl.BlockSpec((tm, W2), lambda i: (i, 0)), pl.BlockSpec((tm, W), lambda i: (i, 0)),
                  pl.BlockSpec((1, W), lambda i: (0, 0)),
                  pl.BlockSpec((G, GMLP_BLOCK, GMLP_BLOCK), lambda i: (0, 0, 0)),
                  pl.BlockSpec((GMLP_BLOCK, W), lambda i: (0, 0))],
        out_specs=[pl.BlockSpec((tm, W2), lambda i: (i, 0)),
                   pl.BlockSpec((G, GMLP_BLOCK, GMLP_BLOCK), lambda i: (0, 0, 0)),
                   pl.BlockSpec((G, GMLP_BLOCK), lambda i: (0, 0)), pl.BlockSpec((1, W), lambda i: (0, 0))],
        out_shape=[jax.ShapeDtypeStruct((T, W2), BF16), jax.ShapeDtypeStruct((G, GMLP_BLOCK, GMLP_BLOCK), F32),
                   jax.ShapeDtypeStruct((G, GMLP_BLOCK), F32), jax.ShapeDtypeStruct((1, W), F32)],
        scratch_shapes=[pltpu.VMEM((tm, W), F32), pltpu.VMEM((tm, W), F32), pltpu.VMEM((GMLP_BLOCK, W), F32)],
        compiler_params=_params(("arbitrary",)))(zp, d_out, gv, ws, bs_tile)


LANES = 128
HALO = 16


def _taps(ext, w, b):
    return w[2:3] * ext[HALO:] + w[1:2] * pltpu.roll(ext, 1, 0)[HALO:] + w[0:1] * pltpu.roll(ext, 2, 0)[HALO:] + b


def _ffn_fwd(h, w, g, cw, cb, wd, S, *, name, loss=None, tm=256):
    T, D = h.shape
    F = w.shape[-1] // 2
    tc = _col_tile(F)
    tm = _row_tile(S, tm)
    has_loss, has_down = loss is not None, wd is not None
    n_in = 5 + has_down + 2 * has_loss

    def body(*refs):
        h_ref, w_ref, g_ref, cw_ref, cb_ref = refs[:5]
        outs, tail = refs[n_in:-1], refs[-1]
        y_ref, a_ref, c_ref, n_ref = outs[has_down:has_down + 4]
        first = (pl.program_id(0) * tm) % S == 0
        nb = (_rms(h_ref[...])[0] * g_ref[...]).astype(BF16)
        n_ref[...] = nb
        for j in range(F // tc):
            cs = slice(j * tc, (j + 1) * tc)
            conv = []
            for s in range(2):
                acc = jnp.dot(nb, w_ref[:, s * F + j * tc:s * F + (j + 1) * tc], preferred_element_type=F32)
                ab = acc.astype(BF16)
                a_ref[s, :, cs] = ab
                af = ab.astype(F32)
                ext = jnp.concatenate([jnp.where(first, 0.0, tail[s, :, cs]), af], axis=0)
                tail[s, :, cs] = af[tm - HALO:, :]
                cv = _taps(ext, cw_ref[s, :, cs], cb_ref[s:s + 1, cs]).astype(BF16)
                c_ref[s, :, cs] = cv
                conv.append(cv.astype(F32))
            up, gate = conv
            y_ref[:, cs] = (gate * jax.nn.sigmoid(gate) * up).astype(BF16)
        if has_down:
            out = h_ref[...] + jnp.dot(y_ref[...], refs[5][...], preferred_element_type=F32)
            if has_loss:
                _loss_epilogue(out, refs[6], refs[7], outs[0], outs[5], outs[6], pl.program_id(0) == 0)
            else:
                outs[0][...] = out

    row = lambda width: pl.BlockSpec((tm, width), lambda i: (i, 0))
    wide = pl.BlockSpec((2, tm, F), lambda i: (0, i, 0))
    fixed = lambda *shape: pl.BlockSpec(shape, lambda i: (0,) * len(shape))
    once = pl.Buffered(1)
    ins = [h, w, g.reshape(1, D), cw, cb]
    in_specs = [row(D), pl.BlockSpec((None, D, 2 * F), lambda i: (0, 0, 0), pipeline_mode=once), fixed(1, D),
                fixed(2, 3, F), fixed(2, F)]
    out_specs = [row(F), wide, wide, row(D)]
    out_shape = [jax.ShapeDtypeStruct((T, F), BF16), jax.ShapeDtypeStruct((2, T, F), BF16),
                 jax.ShapeDtypeStruct((2, T, F), BF16), jax.ShapeDtypeStruct((T, D), BF16)]
    if has_down:
        ins.append(wd)
        in_specs.append(pl.BlockSpec((None, F, D), lambda i: (0, 0, 0), pipeline_mode=once))
        out_specs.insert(0, row(D))
        out_shape.insert(0, jax.ShapeDtypeStruct((T, D), F32))
    if has_loss:
        ins += [loss[0].reshape(1, D), loss[1]]
        in_specs += [fixed(1, D), row(D)]
        out_specs += [fixed(8, 128), fixed(1, D)]
        out_shape += [jax.ShapeDtypeStruct((8, 128), F32), jax.ShapeDtypeStruct((1, D), F32)]
    return pl.pallas_call(body, name=name, grid=(T // tm,), in_specs=in_specs, out_specs=out_specs,
                          out_shape=out_shape, scratch_shapes=[pltpu.VMEM((2, HALO, F), F32)],
                          compiler_params=_params(("arbitrary",)))(*ins)


def _conv_bwd(a, c, dy, cw, S, *, tm=256):
    _, T, F = a.shape
    tc = _col_tile(F)
    tm = _row_tile(S, tm)
    nm = T // tm
    hb = tm // HALO
    TE = tm + HALO
    nxt = lambda j, i: jnp.minimum((i + 1) * hb, T // HALO - 1)

    def body(a_ref, c_ref, nc_ref, dy_ref, ndy_ref, w_ref, da_ref, dw_ref, db_ref):
        i = pl.program_id(1)
        last = ((i + 1) * tm) % S == 0
        keep_n = jnp.where(last, 0.0, 1.0)

        @pl.when(i == 0)
        def _():
            dw_ref[...] = jnp.zeros_like(dw_ref)
            db_ref[...] = jnp.zeros_like(db_ref)

        for j in range(tc // LANES):
            cs = slice(j * LANES, (j + 1) * LANES)
            dyf = jnp.concatenate([dy_ref[:, cs].astype(F32), ndy_ref[:, cs].astype(F32) * keep_n], axis=0)
            up = jnp.concatenate([c_ref[0, :, cs].astype(F32), nc_ref[0, :, cs].astype(F32)], axis=0)
            gate = jnp.concatenate([c_ref[1, :, cs].astype(F32), nc_ref[1, :, cs].astype(F32)], axis=0)
            sg = jax.nn.sigmoid(gate)
            for s, d in ((0, dyf * (gate * sg)), (1, dyf * up * (sg * (1.0 + gate * (1.0 - sg))))):
                a = a_ref[s, :, cs].astype(F32)
                w = w_ref[s, :, cs]
                u1, u2 = pltpu.roll(d, TE - 1, 0), pltpu.roll(d, TE - 2, 0)
                db_ref[s:s + 1, cs] += jnp.sum(d[:tm], axis=0, keepdims=True)
                dw_ref[s, 2:3, cs] += jnp.sum(d[:tm] * a, axis=0, keepdims=True)
                dw_ref[s, 1:2, cs] += jnp.sum(u1[:tm] * a, axis=0, keepdims=True)
                dw_ref[s, 0:1, cs] += jnp.sum(u2[:tm] * a, axis=0, keepdims=True)
                da_ref[s, :, cs] = (w[2:3] * d + w[1:2] * u1 + w[0:1] * u2)[:tm].astype(BF16)

    cur = pl.BlockSpec((2, tm, tc), lambda j, i: (0, i, j))
    return pl.pallas_call(
        body, name="conv_bwd", grid=(F // tc, nm),
        in_specs=[cur, cur, pl.BlockSpec((2, HALO, tc), lambda j, i: (0, nxt(j, i), j)),
                  pl.BlockSpec((tm, tc), lambda j, i: (i, j)), pl.BlockSpec((HALO, tc), lambda j, i: (nxt(j, i), j)),
                  pl.BlockSpec((2, 3, tc), lambda j, i: (0, 0, j))],
        out_specs=[cur, pl.BlockSpec((2, 3, tc), lambda j, i: (0, 0, j)), pl.BlockSpec((2, tc), lambda j, i: (0, j))],
        out_shape=[jax.ShapeDtypeStruct((2, T, F), BF16), jax.ShapeDtypeStruct((2, 3, F), F32),
                   jax.ShapeDtypeStruct((2, F), F32)],
        compiler_params=_params(("arbitrary", "arbitrary")))(a, c, c, dy, dy, cw)


def _bias_index():
    idx = np.arange(F_LEN)
    d = np.where(idx < K_SPAN, idx, idx - F_LEN)
    return np.clip(PAD - d, -REL_CLIP, REL_CLIP) + REL_CLIP


ROW_GROUP = 16


def _roll_rows(x, sign, unit, steps):
    rows = lax.broadcasted_iota(jnp.int32, x.shape, 0)
    step = 1
    while step < steps:
        shift = unit * step if sign > 0 else F_LEN - unit * step
        x = jnp.where((rows & step) != 0, pltpu.roll(x, shift, 1), x)
        step *= 2
    return x


def _bias_expand(frow):
    H = frow.shape[0]
    groups = Q_BLOCK // ROW_GROUP

    def body(f_ref, o_ref):
        coarse = _roll_rows(jnp.broadcast_to(f_ref[...], (groups, F_LEN)), 1, ROW_GROUP, groups)
        x = jnp.concatenate([jnp.broadcast_to(coarse[a:a + 1], (ROW_GROUP, F_LEN)) for a in range(groups)], axis=0)
        x = _roll_rows(x, 1, 1, ROW_GROUP)[:, :K_SPAN]
        qc = lax.broadcasted_iota(jnp.int32, (Q_BLOCK, K_SPAN), 0) // CHUNK * CHUNK
        kj = lax.broadcasted_iota(jnp.int32, (Q_BLOCK, K_SPAN), 1)
        o_ref[...] = jnp.where((kj >= qc) & (kj < qc + PAD + CHUNK), x, NEG_INF)

    return pl.pallas_call(
        body, name="bias_expand", grid=(H,),
        in_specs=[pl.BlockSpec((None, 1, F_LEN), lambda h: (h, 0, 0))],
        out_specs=pl.BlockSpec((None, Q_BLOCK, K_SPAN), lambda h: (h, 0, 0)),
        out_shape=jax.ShapeDtypeStruct((H, Q_BLOCK, K_SPAN), F32), compiler_params=_params(("arbitrary",)))(frow)


def _bias_reduce(dbias, n_rel):
    H = dbias.shape[0]
    onehot = jnp.asarray((_bias_index()[:, None] == np.arange(n_rel)[None, :]).astype(np.float32), dtype=BF16)

    def body(d_ref, oh_ref, o_ref):
        x = jnp.concatenate([d_ref[...], jnp.zeros((Q_BLOCK, F_LEN - K_SPAN), F32)], axis=1)
        fine = _roll_rows(x, -1, 1, ROW_GROUP).reshape(Q_BLOCK // ROW_GROUP, ROW_GROUP, F_LEN)
        coarse = _roll_rows(jnp.sum(fine, axis=1), -1, ROW_GROUP, Q_BLOCK // ROW_GROUP)
        row = jnp.broadcast_to(jnp.sum(coarse, axis=0, keepdims=True), (8, F_LEN))
        acc = jnp.zeros((8, n_rel), F32)
        for _ in range(3):
            piece = row.astype(BF16)
            acc = acc + jnp.dot(piece, oh_ref[...], preferred_element_type=F32)
            row = row - piece.astype(F32)
        o_ref[...] = acc[0:1]

    return pl.pallas_call(
        body, name="bias_reduce", grid=(H,),
        in_specs=[pl.BlockSpec((None, Q_BLOCK, K_SPAN), lambda h: (h, 0, 0)),
                  pl.BlockSpec((F_LEN, n_rel), lambda h: (0, 0))],
        out_specs=pl.BlockSpec((None, 1, n_rel), lambda h: (h, 0, 0)),
        out_shape=jax.ShapeDtypeStruct((H, 1, n_rel), F32), compiler_params=_params(("arbitrary",)))(dbias, onehot)


def _attn_specs(S, heads=HEADS_PER_STEP):
    hw = heads * HEAD_DIM
    qspec = pl.BlockSpec((None, Q_BLOCK, hw), lambda g, b, i: (b, i, g))
    kspec = pl.BlockSpec((None, None, S, hw), lambda g, b, i: (0, b, 0, g))
    vspec = pl.BlockSpec((None, None, S, hw), lambda g, b, i: (1, b, 0, g))
    bspec = pl.BlockSpec((heads, Q_BLOCK, K_SPAN), lambda g, b, i: (g, 0, 0))
    return hw, qspec, kspec, vspec, bspec


def _span_cases(i, fn):
    short = PAD // Q_BLOCK
    for j in range(short):
        pl.when(i == j)(functools.partial(fn, PAD - j * Q_BLOCK))
    pl.when(i >= short)(functools.partial(fn, 0))


def _key_start(i, off):
    return 0 if off else pl.multiple_of(i * Q_BLOCK - PAD, Q_BLOCK)


def _attn_exp(q_ref, k_ref, b_ref, h, k0, off):
    hs = slice(h * HEAD_DIM, (h + 1) * HEAD_DIM)
    kh = k_ref[pl.ds(k0, K_SPAN - off), hs]
    s = lax.dot_general(q_ref[:, hs], kh, (((1,), (1,)), ((), ())), preferred_element_type=F32) + b_ref[h, :, off:]
    p = jnp.exp(s - jnp.max(s, axis=-1, keepdims=True))
    return p, 1.0 / jnp.sum(p, axis=-1, keepdims=True), kh


def _attn_fwd(q, kv, bias, B, S):
    HD = q.shape[-1]
    heads = min(2 * HEADS_PER_STEP, HD // HEAD_DIM)
    hw, qspec, kspec, vspec, bspec = _attn_specs(S, heads)

    def body(q_ref, k_ref, v_ref, b_ref, o_ref):
        i = pl.program_id(2)

        def block(off):
            k0 = _key_start(i, off)
            outs = []
            for h in range(heads):
                hs = slice(h * HEAD_DIM, (h + 1) * HEAD_DIM)
                p, inv, _ = _attn_exp(q_ref, k_ref, b_ref, h, k0, off)
                outs.append(jnp.dot(p.astype(BF16), v_ref[pl.ds(k0, K_SPAN - off), hs],
                                    preferred_element_type=F32) * inv)
            o_ref[...] = jnp.concatenate(outs, axis=1).astype(BF16)

        _span_cases(i, block)

    return pl.pallas_call(
        body, name="attn_fwd", grid=(HD // hw, B, S // Q_BLOCK), in_specs=[qspec, kspec, vspec, bspec],
        out_specs=qspec, out_shape=jax.ShapeDtypeStruct((B, S, HD), BF16),
        compiler_params=_params(("arbitrary", "arbitrary", "arbitrary")))(q, kv, kv, bias)


def _attn_bwd(q, kv, bias, do, B, S):
    HD = q.shape[-1]
    H = HD // HEAD_DIM
    heads = min(2 * HEADS_PER_STEP, H)
    hw, qspec, kspec, vspec, bspec = _attn_specs(S, heads)
    scale = HEAD_DIM ** -0.5
    nq = S // Q_BLOCK

    def body(q_ref, k_ref, v_ref, b_ref, do_ref, dq_ref, dkv_ref, db_ref, dk_acc, dv_acc):
        b, i = pl.program_id(1), pl.program_id(2)

        @pl.when(i == 0)
        def _():
            dk_acc[...] = jnp.zeros_like(dk_acc)
            dv_acc[...] = jnp.zeros_like(dv_acc)

        @pl.when((i == 0) & (b == 0))
        def _():
            db_ref[...] = jnp.zeros_like(db_ref)

        def block(off):
            k0 = _key_start(i, off)
            keys = pl.ds(k0, K_SPAN - off)
            for h in range(heads):
                hs = slice(h * HEAD_DIM, (h + 1) * HEAD_DIM)
                p, inv, kh = _attn_exp(q_ref, k_ref, b_ref, h, k0, off)
                p = p * inv
                doh = do_ref[:, hs]
                dp = lax.dot_general(doh, v_ref[keys, hs], (((1,), (1,)), ((), ())), preferred_element_type=F32)
                ds = p * (dp - jnp.sum(p * dp, axis=-1, keepdims=True))
                db_ref[h, :, off:] += ds
                dsb = ds.astype(BF16)
                dq_ref[:, hs] = (jnp.dot(dsb, kh, preferred_element_type=F32) * scale).astype(BF16)
                dk_acc[hs, keys] += lax.dot_general(q_ref[:, hs], dsb, (((0,), (0,)), ((), ())),
                                                     preferred_element_type=F32)
                dv_acc[hs, keys] += lax.dot_general(doh, p.astype(BF16), (((0,), (0,)), ((), ())),
                                                     preferred_element_type=F32)

        _span_cases(i, block)

        @pl.when(i == nq - 1)
        def _():
            dkv_ref[0] = dk_acc[...].T.astype(BF16)
            dkv_ref[1] = dv_acc[...].T.astype(BF16)

    return pl.pallas_call(
        body, name="attn_bwd", grid=(HD // hw, B, nq), in_specs=[qspec, kspec, vspec, bspec, qspec],
        out_specs=[qspec, pl.BlockSpec((2, None, S, hw), lambda g, b, i: (0, b, 0, g)), bspec],
        out_shape=[jax.ShapeDtypeStruct((B, S, HD), BF16), jax.ShapeDtypeStruct((2, B, S, HD), BF16),
                   jax.ShapeDtypeStruct((H, Q_BLOCK, K_SPAN), F32)],
        scratch_shapes=[pltpu.VMEM((hw, S), F32), pltpu.VMEM((hw, S), F32)],
        compiler_params=_params(("arbitrary", "arbitrary", "arbitrary")))(q, kv, kv, bias, do)


def _sub_rows(R):
    for cand in (256, 352, 128, 64, 8):
        if R % cand == 0 and R > cand:
            return cand
    return R


def _adamw(w, g, m, v, *, name):
    R, C = w.shape
    tr = _sub_rows(R)

    def body(w_ref, g_ref, m_ref, v_ref, d_ref, nm_ref, nv_ref):
        g = g_ref[...]
        m = ADAM_B1 * m_ref[...] + (1.0 - ADAM_B1) * g
        v = ADAM_B2 * v_ref[...] + (1.0 - ADAM_B2) * (g * g)
        m_hat = m / (1.0 - ADAM_B1 ** ADAM_STEP)
        v_hat = v / (1.0 - ADAM_B2 ** ADAM_STEP)
        d_ref[...] = -ADAM_LR * (m_hat / (jnp.sqrt(v_hat) + ADAM_EPS) + ADAM_WD * w_ref[...])
        nm_ref[...] = m
        nv_ref[...] = v

    spec = pl.BlockSpec((tr, C), lambda i: (i, 0))
    return pl.pallas_call(body, name=name, grid=(R // tr,), in_specs=[spec] * 4, out_specs=[spec] * 3,
                          out_shape=[jax.ShapeDtypeStruct((R, C), F32)] * 3,
                          compiler_params=_params(("arbitrary",)))(w, g, m, v)


def _add_pair(units, got, core, *, name):
    n4, R, C = got.shape
    rows = n4 * R
    tr = 512 if rows % 512 == 0 else R

    def body(c_ref, u_ref, got_ref, o_ref):
        o_ref[...] = (u_ref[...].astype(F32) + got_ref[...].astype(F32)).astype(BF16)

    spec = pl.BlockSpec((tr, C), lambda i, c: (i, 0))
    grid_spec = pltpu.PrefetchScalarGridSpec(
        num_scalar_prefetch=1, grid=(rows // tr,),
        in_specs=[pl.BlockSpec((None, tr, C), lambda i, c: (c[0], i, 0)), spec], out_specs=spec)
    out = pl.pallas_call(body, name=name, grid_spec=grid_spec, out_shape=jax.ShapeDtypeStruct((rows, C), BF16),
                         compiler_params=_params(("arbitrary",)))(core.reshape(1), units.reshape(2, rows, C),
                                                                   got.reshape(rows, C))
    return out.reshape(n4, R, C)


def _sum_chips(w, own, got, pos, *, name, layer=0, into=None):
    _, R, C = own.shape
    tr = _sub_rows(R)
    nr = R // tr

    def body(p_ref, own_ref, got_ref, *rest):
        o_ref = rest[-1]
        o_ref[...] = (own_ref[...].astype(F32) + got_ref[0].astype(F32) + got_ref[1].astype(F32)
                      + got_ref[2].astype(F32))

    if w.row_sharded:
        out_map = lambda i, p: (layer, i, p[1])
    else:
        out_map = lambda i, p: (layer, p[1] * nr + i, 0)
    ins = [pos, own, got]
    in_specs = [pl.BlockSpec((None, tr, C), lambda i, p: (p[0], i, 0)),
                pl.BlockSpec((3, tr, C), lambda i, p: (0, i, 0))]
    alias = {}
    if into is not None:
        ins.append(into)
        in_specs.append(ANY)
        alias = {3: 0}
    grid_spec = pltpu.PrefetchScalarGridSpec(num_scalar_prefetch=1, grid=(nr,), in_specs=in_specs,
                                             out_specs=pl.BlockSpec((None, tr, C), out_map))
    return pl.pallas_call(body, name=name, grid_spec=grid_spec, input_output_aliases=alias,
                          out_shape=jax.ShapeDtypeStruct((w.L, w.ks, w.ns), F32),
                          compiler_params=_params(("arbitrary",)))(*ins)


def _mesh_pos():
    return lax.axis_index("x"), lax.axis_index("y"), lax.axis_index("c")


def _other_chips(x, y):
    return [(1 - x, y), (x, 1 - y), (1 - x, 1 - y)]


ANY = pl.BlockSpec(memory_space=pl.ANY)


class _W:
    def __init__(self, name, shard, row_sharded, direct=False):
        self.name = name
        self.direct = direct
        self.L, ks, ns = shard.shape
        self.row_sharded = row_sharded
        self.K, self.N = (ks * N_CHIPS, ns) if row_sharded else (ks, ns * N_CHIPS)
        self.ks, self.ns = ks, ns

    def shard_of(self, full, j):
        if self.row_sharded:
            return full.at[:, pl.ds(j * self.ks, self.ks), :]
        return full.at[:, :, pl.ds(j * self.ns, self.ns)]

    def half_of(self, shard, c):
        if self.row_sharded:
            return shard.at[:, :, pl.ds(c * (self.ns // 2), self.ns // 2)]
        return shard.at[:, pl.ds(c * (self.ks // 2), self.ks // 2), :]


HBM = pl.BlockSpec(memory_space=pltpu.HBM)
SEM = pl.BlockSpec(memory_space=pltpu.SEMAPHORE)
IN_FLIGHT = pltpu.SideEffectType.DATAFLOW_SIDE_EFFECTING


def _in_hbm(a):
    return pltpu.with_memory_space_constraint(a, pltpu.HBM)


def _gather_start(ws, shards, after, *, name):
    nw = len(ws)

    def body(*refs):
        src, dst = refs[:nw], refs[nw:2 * nw]
        send, recv = refs[2 * nw + 1:3 * nw + 1], refs[3 * nw + 1:4 * nw + 1]
        x, y, c = _mesh_pos()
        me = 2 * x + y
        for i, w in enumerate(ws):
            for f, (px, py) in enumerate(_other_chips(x, y)):
                for e in range(2 if w.direct else 1):
                    k = 2 * f + e
                    pltpu.make_async_remote_copy(
                        src_ref=w.half_of(src[i], c), dst_ref=w.half_of(w.shard_of(dst[i], me), c),
                        send_sem=send[i].at[k], recv_sem=recv[i].at[k], device_id=(px, py, c if e == 0 else 1 - c),
                        device_id_type=MESH).start()

    fulls = [lax.empty((w.L, w.K, w.N), BF16) for w in ws]
    out = pl.pallas_call(
        body, name=name, in_specs=[HBM] * (2 * nw) + [ANY],
        out_specs=[SEM] * (2 * nw) + [HBM] * (2 * nw),
        out_shape=[pltpu.SemaphoreType.DMA((6,))] * (2 * nw)
        + [pltpu.HBM(s.shape, BF16) for s in shards] + [pltpu.HBM(f.shape, BF16) for f in fulls],
        input_output_aliases={i: 2 * nw + i for i in range(2 * nw)},
        compiler_params=pltpu.CompilerParams(has_side_effects=IN_FLIGHT))(
            *[_in_hbm(s) for s in shards], *[_in_hbm(f) for f in fulls], after)
    return [(out[i], out[nw + i], out[2 * nw + i], out[3 * nw + i]) for i in range(nw)]


def _gather_wait(ws, flight, after, *, name):
    nw = len(ws)

    def body(*refs):
        src, dst = refs[:nw], refs[nw:2 * nw]
        send, recv = refs[2 * nw:3 * nw], refs[3 * nw:4 * nw]
        x, y, c = _mesh_pos()
        for i, w in enumerate(ws):
            for f, (px, py) in enumerate(_other_chips(x, y)):
                for e in range(2 if w.direct else 1):
                    k = 2 * f + e
                    landed = w.half_of(w.shard_of(dst[i], 2 * px + py), c if e == 0 else 1 - c)
                    cp = pltpu.make_async_remote_copy(
                        src_ref=w.half_of(src[i], c), dst_ref=landed, send_sem=send[i].at[k], recv_sem=recv[i].at[k],
                        device_id=(px, py, c), device_id_type=MESH)
                    cp.wait_send()
                    cp.wait_recv()

    shards, fulls = [fl[2] for fl in flight], [fl[3] for fl in flight]
    out = pl.pallas_call(
        body, name=name, in_specs=[HBM] * (2 * nw) + [SEM] * (2 * nw) + [ANY],
        out_specs=[HBM] * (2 * nw),
        out_shape=[pltpu.HBM(s.shape, BF16) for s in shards] + [pltpu.HBM(f.shape, BF16) for f in fulls],
        input_output_aliases={i: i for i in range(2 * nw)},
        compiler_params=pltpu.CompilerParams(has_side_effects=IN_FLIGHT))(
            *shards, *fulls, *[fl[0] for fl in flight], *[fl[1] for fl in flight], after)
    return out[:nw], out[nw:]


def _gather_finish(ws, shards, fulls, *, name):
    nw = len(ws)
    forward = not ws[0].direct

    def body(*refs):
        src, dst, stage = refs[:nw], refs[3 * nw:4 * nw], refs[4 * nw:5 * nw]
        send_sems, recv_sems, load_sems, store_sems = refs[5 * nw:]
        x, y, c = _mesh_pos()
        me = 2 * x + y
        sibling = (x, y, 1 - c)
        chips = _other_chips(x, y)

        def fwd(i, w, f, half):
            px, py = chips[f]
            landed = w.half_of(w.shard_of(dst[i], 2 * px + py), half)
            return pltpu.make_async_remote_copy(src_ref=landed, dst_ref=landed, send_sem=send_sems.at[3 * i + f],
                                                recv_sem=recv_sems.at[3 * i + f], device_id=sibling,
                                                device_id_type=MESH)

        loads = [pltpu.make_async_copy(src[i], stage[i], load_sems.at[i]) for i in range(nw)]
        for cp in loads:
            cp.start()
        sends = [fwd(i, w, f, c) for i, w in enumerate(ws) for f in range(3)] if forward else []
        for cp in sends:
            cp.start()
        stores = [pltpu.make_async_copy(stage[i], w.shard_of(dst[i], me), store_sems.at[i])
                  for i, w in enumerate(ws)]
        for ld, st in zip(loads, stores):
            ld.wait()
            st.start()
        if forward:
            for i, w in enumerate(ws):
                for f in range(3):
                    fwd(i, w, f, 1 - c).wait_recv()
        for cp in sends:
            cp.wait_send()
        for cp in stores:
            cp.wait()

    out = pl.pallas_call(
        body, name=name, in_specs=[ANY] * (2 * nw), out_specs=[ANY] * (2 * nw),
        out_shape=[jax.ShapeDtypeStruct(s.shape, BF16) for s in shards]
        + [jax.ShapeDtypeStruct(f.shape, BF16) for f in fulls],
        input_output_aliases={i: i for i in range(2 * nw)},
        scratch_shapes=[pltpu.VMEM((w.L, w.ks, w.ns), BF16) for w in ws]
        + [pltpu.SemaphoreType.DMA((3 * nw,)), pltpu.SemaphoreType.DMA((3 * nw,)), pltpu.SemaphoreType.DMA((nw,)),
           pltpu.SemaphoreType.DMA((nw,))],
        compiler_params=_params(has_side_effects=True))(*shards, *fulls)
    return out[nw:]


def _split_copies(name, srcs, lands, n_sems, copies_of, *, flight=None, after=None):
    n = len(srcs)
    starting = flight is None

    def body(*refs):
        src, land = refs[:n], refs[n:2 * n]
        sems = refs[2 * n + 1:4 * n + 1] if starting else refs[2 * n:4 * n]
        for i in range(n):
            for cp in copies_of(i, src[i], land[i], sems[i], sems[n + i]):
                if starting:
                    cp.start()
                else:
                    cp.wait_send()
                    cp.wait_recv()

    thru = [pltpu.HBM(a.shape, a.dtype) for a in list(srcs) + list(lands)]
    if starting:
        out = pl.pallas_call(
            body, name=name, in_specs=[HBM] * (2 * n) + [ANY], out_specs=[SEM] * (2 * n) + [HBM] * (2 * n),
            out_shape=[pltpu.SemaphoreType.DMA((n_sems,))] * (2 * n) + thru,
            input_output_aliases={i: 2 * n + i for i in range(2 * n)},
            compiler_params=pltpu.CompilerParams(has_side_effects=IN_FLIGHT))(
                *[_in_hbm(a) for a in srcs], *[_in_hbm(a) for a in lands], after)
        return [(out[i], out[n + i], out[2 * n + i], out[3 * n + i]) for i in range(n)]
    out = pl.pallas_call(
        body, name=name, in_specs=[HBM] * (2 * n) + [SEM] * (2 * n) + [ANY], out_specs=[HBM] * (2 * n),
        out_shape=thru, input_output_aliases={i: i for i in range(2 * n)},
        compiler_params=pltpu.CompilerParams(has_side_effects=IN_FLIGHT))(
            *srcs, *lands, *[fl[0] for fl in flight], *[fl[1] for fl in flight], after)
    return out[:n], out[n:]


def _sum8(land, vec, me):
    R = vec.shape[0]

    def body(me_ref, land_ref, vec_ref, o_ref):
        acc = jnp.zeros((R, 128), F32)
        for d in range(8):
            acc = acc + jnp.where(me_ref[0] == d, vec_ref[...], land_ref[d])
        o_ref[...] = acc

    grid_spec = pltpu.PrefetchScalarGridSpec(
        num_scalar_prefetch=1, grid=(1,),
        in_specs=[pl.BlockSpec((8, R, 128), lambda i, m: (0, 0, 0)), pl.BlockSpec((R, 128), lambda i, m: (0, 0))],
        out_specs=pl.BlockSpec((R, 128), lambda i, m: (0, 0)))
    return pl.pallas_call(body, name="sum8", grid_spec=grid_spec, out_shape=jax.ShapeDtypeStruct((R, 128), F32),
                          compiler_params=_params(("arbitrary",)))(me.reshape(1), land, vec)


def _swap_copies(i, src, got, send, recv):
    x, y, c = _mesh_pos()
    return [pltpu.make_async_remote_copy(src_ref=src.at[1 - c], dst_ref=got, send_sem=send.at[0], recv_sem=recv.at[0],
                                         device_id=(x, y, 1 - c), device_id_type=MESH)]


def _gather8_copies(i, src, land, send, recv):
    x, y, c = _mesh_pos()
    me = 4 * x + 2 * y + c
    peers = [(x, y, 1 - c)] + [(px, py, pc) for px, py in _other_chips(x, y) for pc in (c, 1 - c)]
    return [pltpu.make_async_remote_copy(src_ref=src, dst_ref=land.at[me], send_sem=send.at[k], recv_sem=recv.at[k],
                                         device_id=peer, device_id_type=MESH) for k, peer in enumerate(peers)]


def _scatter_copy(src, got, send, recv, f, chip, c):
    px, py = chip
    return pltpu.make_async_remote_copy(src_ref=src.at[2 * px + py], dst_ref=got.at[f], send_sem=send.at[f],
                                        recv_sem=recv.at[f], device_id=(px, py, c), device_id_type=MESH)


def _scatter_start(sums, *, name):
    nw = len(sums)

    def body(*refs):
        src, got = refs[:nw], refs[nw:2 * nw]
        send, recv = refs[2 * nw:3 * nw], refs[3 * nw:4 * nw]
        x, y, c = _mesh_pos()
        for i in range(nw):
            for f, chip in enumerate(_other_chips(x, y)):
                _scatter_copy(src[i], got[i], send[i], recv[i], f, chip, c).start()

    lands = [lax.empty((3,) + s.shape[1:], BF16) for s in sums]
    out = pl.pallas_call(
        body, name=name, in_specs=[HBM] * (2 * nw), out_specs=[SEM] * (2 * nw) + [HBM] * (2 * nw),
        out_shape=[pltpu.SemaphoreType.DMA((3,))] * (2 * nw)
        + [pltpu.HBM(s.shape, BF16) for s in sums] + [pltpu.HBM(l.shape, BF16) for l in lands],
        input_output_aliases={i: 2 * nw + i for i in range(2 * nw)},
        compiler_params=pltpu.CompilerParams(has_side_effects=IN_FLIGHT))(
            *[_in_hbm(s) for s in sums], *[_in_hbm(l) for l in lands])
    return [(out[i], out[nw + i], out[2 * nw + i], out[3 * nw + i]) for i in range(nw)]


def _scatter_wait(flight, after):
    nw = len(flight)

    def body(*refs):
        src, got = refs[:nw], refs[nw:2 * nw]
        send, recv = refs[2 * nw:3 * nw], refs[3 * nw:4 * nw]
        x, y, c = _mesh_pos()
        for i in range(nw):
            for f, chip in enumerate(_other_chips(x, y)):
                cp = _scatter_copy(src[i], got[i], send[i], recv[i], f, chip, c)
                cp.wait_send()
                cp.wait_recv()

    sums, lands = [fl[2] for fl in flight], [fl[3] for fl in flight]
    out = pl.pallas_call(
        body, name="scatter_wait", in_specs=[HBM] * (2 * nw) + [SEM] * (2 * nw) + [ANY], out_specs=[HBM] * (2 * nw),
        out_shape=[pltpu.HBM(s.shape, BF16) for s in sums] + [pltpu.HBM(l.shape, BF16) for l in lands],
        input_output_aliases={i: i for i in range(2 * nw)},
        compiler_params=pltpu.CompilerParams(has_side_effects=IN_FLIGHT))(
            *sums, *lands, *[fl[0] for fl in flight], *[fl[1] for fl in flight], after)
    return out[:nw], out[nw:]


def _join_halves(ws, shards):
    nw = len(ws)

    def body(*refs):
        buf = refs[nw:2 * nw]
        send_sems, recv_sems = refs[2 * nw:]
        x, y, c = _mesh_pos()
        sibling = (x, y, 1 - c)

        def copy(i, w, half):
            region = w.half_of(buf[i], half)
            return pltpu.make_async_remote_copy(src_ref=region, dst_ref=region, send_sem=send_sems.at[i],
                                                recv_sem=recv_sems.at[i], device_id=sibling, device_id_type=MESH)

        sends = [copy(i, w, c) for i, w in enumerate(ws)]
        for cp in sends:
            cp.start()
        for i, w in enumerate(ws):
            copy(i, w, 1 - c).wait_recv()
        for cp in sends:
            cp.wait_send()

    return pl.pallas_call(
        body, name="join_halves", in_specs=[ANY] * nw, out_specs=[ANY] * nw,
        out_shape=[jax.ShapeDtypeStruct((w.L, w.ks, w.ns), F32) for w in ws],
        input_output_aliases={i: i for i in range(nw)},
        scratch_shapes=[pltpu.SemaphoreType.DMA((nw,)), pltpu.SemaphoreType.DMA((nw,))],
        compiler_params=_params(has_side_effects=True))(*shards)


def _allreduce_small(vec):
    R = vec.shape[0]

    def body(x_ref, o_ref, buf, send_sems, recv_sems):
        x, y, c = _mesh_pos()
        me, sibling = (x, y, c), (x, y, 1 - c)
        chips = _other_chips(x, y)

        def slot(px, py, pc):
            return buf.at[4 * px + 2 * py + pc]

        def copy(k, block, to, src=None):
            return pltpu.make_async_remote_copy(src_ref=slot(*block) if src is None else src, dst_ref=slot(*block),
                                                send_sem=send_sems.at[k], recv_sem=recv_sems.at[k], device_id=to,
                                                device_id_type=MESH)

        first = [copy(0, me, sibling, src=x_ref)] + [copy(1 + f, me, (*chip, c), src=x_ref)
                                                     for f, chip in enumerate(chips)]
        for cp in first:
            cp.start()
        passed = [copy(4 + f, (*chip, c), sibling) for f, chip in enumerate(chips)]
        for f, chip in enumerate(chips):
            copy(1 + f, (*chip, c), me).wait_recv()
            passed[f].start()
        copy(0, sibling, me).wait_recv()
        for f, chip in enumerate(chips):
            copy(4 + f, (*chip, 1 - c), me).wait_recv()
        for cp in first + passed:
            cp.wait_send()
        slot(*me)[...] = x_ref[...]
        acc = buf[0]
        for d in range(1, 8):
            acc = acc + buf[d]
        o_ref[...] = acc

    return pl.pallas_call(
        body, name="allreduce_small", in_specs=[pl.BlockSpec(memory_space=pltpu.VMEM)],
        out_specs=pl.BlockSpec(memory_space=pltpu.VMEM), out_shape=jax.ShapeDtypeStruct((R, 128), F32),
        scratch_shapes=[pltpu.VMEM((8, R, 128), F32), pltpu.SemaphoreType.DMA((7,)), pltpu.SemaphoreType.DMA((7,))],
        compiler_params=_params())(vec)


def _pack(parts):
    flat = jnp.concatenate([p.reshape(-1).astype(F32) for p in parts])
    n = flat.shape[0]
    pad = (-n) % (64 * 128)
    return jnp.pad(flat, (0, pad)).reshape(-1, 128)


def _unpack(vec, shapes):
    flat = vec.reshape(-1)
    out, off = [], 0
    for s in shapes:
        n = int(np.prod(s))
        out.append(flat[off:off + n].reshape(s))
        off += n
    return out


def kernel(x, a_norm_g, a_w_in, a_v_norm_g, a_w_s, a_b_s, a_w_out, kv_norm_g, w_kv, b_norm_g, b_w_q, b_rel_bias, b_w_o, f_norm_g, f_w_in, f_conv_w, f_conv_b, f_w_down, final_norm_g, loss_target, m_a_norm_g, m_a_w_in, m_a_v_norm_g, m_a_w_s, m_a_b_s, m_a_w_out, m_kv_norm_g, m_w_kv, m_b_norm_g, m_b_w_q, m_b_rel_bias, m_b_w_o, m_f_norm_g, m_f_w_in, m_f_conv_w, m_f_conv_b, m_f_w_down, m_final_norm_g, v_a_norm_g, v_a_w_in, v_a_v_norm_g, v_a_w_s, v_a_b_s, v_a_w_out, v_kv_norm_g, v_w_kv, v_b_norm_g, v_b_w_q, v_b_rel_bias, v_b_w_o, v_f_norm_g, v_f_w_in, v_f_conv_w, v_f_conv_b, v_f_w_down, v_final_norm_g):
    B, S, D = x.shape
    T = B * S
    xi, yi, ci = lax.axis_index("x"), lax.axis_index("y"), lax.axis_index("c")
    j_me = (2 * xi + yi).astype(jnp.int32)
    core = ci.astype(jnp.int32)
    pos = jnp.stack([j_me, core])

    w_shards = {"a_w_in": (a_w_in, False), "a_w_out": (a_w_out, True), "w_kv": (w_kv[None], False),
                "b_w_q": (b_w_q, True), "b_w_o": (b_w_o, True), "f_w_in": (f_w_in, False), "f_w_down": (f_w_down, True)}
    names = list(w_shards)
    ws = [_W(n, w_shards[n][0], w_shards[n][1]) for n in names]
    g_shards = {"a_w_in": (a_w_in, False), "a_w_out": (a_w_out, True),
                "f_w_in0": (f_w_in[0:1], False), "f_w_down0": (f_w_down[0:1], True),
                "w_kv": (w_kv[None], False), "b_w_q": (b_w_q, True), "b_w_o": (b_w_o, True),
                "f_w_in1": (f_w_in[1:2], False), "f_w_down1": (f_w_down[1:2], True)}
    g_names = list(g_shards)
    g_ws = {n: _W(n, *g_shards[n], direct=n not in ("a_w_in", "a_w_out", "f_w_in0")) for n in g_names}

    Wd = a_w_in.shape[1]
    GW = a_v_norm_g.shape[1] * N_CHIPS
    F2 = f_conv_w.shape[2] * N_CHIPS
    Fh = F2 // 2
    nsd, nsg, nsf = a_norm_g.shape[1], a_v_norm_g.shape[1], f_conv_w.shape[2]
    own = (ci == 0).astype(F32)
    place = lambda sh, width, n: lax.dynamic_update_slice_in_dim(
        jnp.zeros(sh.shape[:-1] + (width,), F32), sh * own, j_me * n, axis=sh.ndim - 1)
    def tied(x, flight):
        x, thru = lax.optimization_barrier((x, flight[0][2]))
        return x, [flight[0][:2] + (thru,) + flight[0][3:]] + flight[1:]

    gathered = _allreduce_small(_pack([place(a_norm_g, Wd, nsd), place(a_v_norm_g, GW, nsg),
                                       place(f_conv_w, F2, nsf)]))
    a_g, a_vg, conv_w = _unpack(gathered, [(1, Wd), (1, GW), (2, 3, F2)])
    first, rest = g_names[:4], g_names[4:]
    flight = dict(zip(first, _gather_start([g_ws[n] for n in first], [g_shards[n][0].astype(BF16) for n in first],
                                           gathered, name="gather_start_first")))
    (fi, fd, kv_w, qw, ow), (flight[first[0]],) = tied((f_w_in, f_w_down, w_kv, b_w_q, b_w_o), [flight[first[0]]])
    late = {"w_kv": kv_w[None], "b_w_q": qw, "b_w_o": ow, "f_w_in1": fi[1:2], "f_w_down1": fd[1:2]}
    flight.update(zip(rest, _gather_start([g_ws[n] for n in rest], [late[n].astype(BF16) for n in rest], kv_w,
                                          name="gather_start_rest")))
    full = {}

    def arrive(group, after, tag):
        gw = [g_ws[n] for n in group]
        sh, fu = _gather_wait(gw, [flight[n] for n in group], after, name=f"gather_wait_{tag}")
        full.update(zip(group, _gather_finish(gw, sh, fu, name=f"gather_finish_{tag}")))
    conv_w2 = conv_w.reshape(2, 3, 2, Fh).transpose(0, 2, 1, 3)
    conv_b2 = f_conv_b.reshape(2, 2, Fh)

    h0 = x.reshape(T, D)
    target = loss_target.reshape(T, D)
    bs_tile = jnp.repeat(a_b_s[0].T, GROUP_DIM, axis=1)
    ws_a = a_w_s[0]
    scale = HEAD_DIM ** -0.5
    HD = b_w_q.shape[2]
    H = HD // HEAD_DIM
    n_rel = b_rel_bias.shape[-1]
    frow, (flight["w_kv"],) = tied(b_rel_bias[0][:, _bias_index()].reshape(H, 1, F_LEN), [flight["w_kv"]])
    bias = _bias_expand(frow)

    def ffn_fwd(h, l, loss=None):
        out = _ffn_fwd(h, full[f"f_w_in{l}"], f_norm_g[l], conv_w2[l], conv_b2[l], full[f"f_w_down{l}"], S,
                       loss=loss, name=f"ffn{l}")
        yff, a, c, n = out[1:5]
        return (out[0] if loss is None else (out[0], out[5], out[6])), (a, c, n, yff)

    arrive(["a_w_in", "a_w_out"], bias, "a")
    h1, zp, out_a, n_a = _mixer_a_fwd(h0, full["a_w_in"], a_g[0], a_vg, ws_a, bs_tile, full["a_w_out"])
    arrive(["f_w_in0"], h1, "f0")
    yff0, a0, c0, n0 = _ffn_fwd(h1, full["f_w_in0"], f_norm_g[0], conv_w2[0], conv_b2[0], None, S, name="ffn0_in")
    arrive(["f_w_down0"], yff0, "fd0")
    h2, saved0 = _mm(yff0, full["f_w_down0"], res=h1, name="ffn0_down"), (a0, c0, n0, yff0)
    arrive(["w_kv", "b_w_q", "b_w_o"], h2, "b")
    arrive(["f_w_in1", "f_w_down1"], h2, "f1")
    q, kv, n_q, n_kv = _qkv_fwd(h2, full["b_w_q"], b_norm_g[0], full["w_kv"], kv_norm_g, scale)
    kv4, q3 = kv.reshape(2, B, S, HD), q.reshape(B, S, HD)
    o = _attn_fwd(q3, kv4, bias, B, S).reshape(T, HD)
    h3 = _mm(o, full["b_w_o"], res=h2, name="attn_out")
    (dh, loss8, dg_final), saved1 = ffn_fwd(h3, 1, loss=(final_norm_g, target))

    units = {}

    in_flight = {}

    def swap_start(group, tag, carry):
        us = [units[n] for n in group]
        lands = [lax.empty(u.shape[1:], BF16) for u in us]
        carry, flight = tied(carry, _split_copies(f"swap_start_{tag}", us, lands, 1, _swap_copies, after=carry))
        return (group, tag, flight), carry

    def reduce_start(swap, after):
        group, tag, flight = swap
        us, got = _split_copies(f"swap_wait_{tag}", [fl[2] for fl in flight], [fl[3] for fl in flight], 1,
                                _swap_copies, flight=flight, after=after)
        sums = [_add_pair(u, g_, core, name=f"pair_{n}") for n, u, g_ in zip(group, us, got)]
        after, flight = tied(after, _scatter_start(sums, name=f"scatter_start_{tag}"))
        in_flight.update(zip(group, flight))
        return after

    def ffn_bwd(dh, h, saved, l, early):
        a, c, n, yff = saved
        units[f"f_w_down{l}"] = _mm_tn(yff, dh, rows_are_shards=True, name=f"ffn{l}_down_dw")
        dh_in = dh
        if early:
            sw, dh_in = swap_start([f"f_w_down{l}"], f"fd{l}", dh)
        dyff = _mm(dh_in, full[f"f_w_down{l}"], trans_w=True, out_dtype=BF16, name=f"ffn{l}_down_dx")
        if early:
            dyff = reduce_start(sw, dyff)
        da, dcw, dcb = _conv_bwd(a, c, dyff, conv_w2[l], S)
        units[f"f_w_in{l}"] = _mm_tn(n, da, split_y=True, name=f"ffn{l}_in_dw")
        sw, da = swap_start([f"f_w_in{l}"] if early else [f"f_w_down{l}", f"f_w_in{l}"], f"f{l}", da)
        dh, dg = _mm(da, full[f"f_w_in{l}"], trans_w=True, split_x=True, bwd=(h, f_norm_g[l], dh),
                     name=f"ffn{l}_in_dx")
        return reduce_start(sw, dh), dg, dcw, dcb

    dh, dg_f1, dcw1, dcb1 = ffn_bwd(dh, h3, saved1, 1, False)
    do = _mm(dh, full["b_w_o"], trans_w=True, out_dtype=BF16, name="attn_out_dx")
    units["b_w_o"] = _mm_tn(o, dh, rows_are_shards=True, name="b_w_o_dw")
    dq, dkv, dbias = _attn_bwd(q3, kv4, bias, do.reshape(B, S, HD), B, S)
    dq, d_rel = lax.optimization_barrier((dq, _bias_reduce(dbias, n_rel)))
    d_rel = d_rel.reshape(1, H, n_rel)
    dq, dkv = dq.reshape(T, HD), dkv.reshape(2, T, HD)
    units["b_w_q"] = _mm_tn(n_q, dq, rows_are_shards=True, name="b_w_q_dw")
    units["w_kv"] = _mm_tn(n_kv, dkv, split_y=True, name="w_kv_dw")
    sw, dkv = swap_start(["b_w_o", "b_w_q", "w_kv"], "b", dkv)
    dh, dg_b, dg_kv = _qkv_dx(dq, full["b_w_q"], b_norm_g[0], dkv, full["w_kv"], kv_norm_g, h2, dh)
    dh = reduce_start(sw, dh)
    dh, dg_f0, dcw0, dcb0 = ffn_bwd(dh, h1, saved0, 0, True)
    units["a_w_out"] = _mm_tn(out_a, dh, rows_are_shards=True, name="a_w_out_dw")
    sw, dh_in = swap_start(["a_w_out"], "ao", dh)
    d_out = _mm(dh_in, full["a_w_out"], trans_w=True, out_dtype=BF16, name="a_out_dx")
    d_out = reduce_start(sw, d_out)
    dzp, dws, dbs, dgv = _gate_bwd(zp, d_out, a_vg, ws_a, bs_tile)
    units["a_w_in"] = _mm_tn(n_a, dzp, name="a_w_in_dw")
    sw, dzp_in = swap_start(["a_w_in"], "ai", dzp)
    dzp_in = reduce_start(sw, dzp_in)
    grad_x, dg_a = _mm(dzp_in, full["a_w_in"], trans_w=True, bwd=(h0, a_g[0], dh), name="a_in_dx")

    to_flat = lambda d: d.transpose(1, 0, 2).reshape(3, F2)
    small = {"a_norm_g": dg_a, "a_v_norm_g": dgv, "a_w_s": dws[None], "a_b_s": dbs[None], "kv_norm_g": dg_kv[0],
             "b_norm_g": dg_b, "b_rel_bias": d_rel, "f_norm_g": jnp.concatenate([dg_f0, dg_f1], axis=0),
             "f_conv_w": jnp.stack([to_flat(dcw0), to_flat(dcw1)]),
             "f_conv_b": jnp.stack([dcb0.reshape(F2), dcb1.reshape(F2)]), "final_norm_g": dg_final[0]}
    snames = list(small)
    small_vec = _pack([small[n] for n in snames] + [loss8[0:1, 0:1]])
    grad_x, small_flight = tied(grad_x, _split_copies("small_start", [small_vec],
                                                      [lax.empty((8,) + small_vec.shape, F32)], 7, _gather8_copies,
                                                      after=grad_x))

    sums, recv = _scatter_wait([in_flight[n] for n in g_names], grad_x)
    sums, recv = dict(zip(g_names, sums)), dict(zip(g_names, recv))
    halves = []
    for n, w in zip(names, ws):
        if w.L == 1:
            halves.append(_sum_chips(w, sums[n], recv[n], pos, name=f"chips_{n}"))
        else:
            first = _sum_chips(w, sums[n + "0"], recv[n + "0"], pos, name=f"chips_{n}0")
            halves.append(_sum_chips(w, sums[n + "1"], recv[n + "1"], pos, layer=1, into=first, name=f"chips_{n}1"))
    g_big = dict(zip(names, _join_halves(ws, halves)))
    g_big["w_kv"] = g_big["w_kv"][0]

    given = dict(a_norm_g=(a_norm_g, m_a_norm_g, v_a_norm_g), a_w_in=(a_w_in, m_a_w_in, v_a_w_in),
                 a_v_norm_g=(a_v_norm_g, m_a_v_norm_g, v_a_v_norm_g), a_w_s=(a_w_s, m_a_w_s, v_a_w_s),
                 a_b_s=(a_b_s, m_a_b_s, v_a_b_s), a_w_out=(a_w_out, m_a_w_out, v_a_w_out),
                 kv_norm_g=(kv_norm_g, m_kv_norm_g, v_kv_norm_g), w_kv=(w_kv, m_w_kv, v_w_kv),
                 b_norm_g=(b_norm_g, m_b_norm_g, v_b_norm_g), b_w_q=(b_w_q, m_b_w_q, v_b_w_q),
                 b_rel_bias=(b_rel_bias, m_b_rel_bias, v_b_rel_bias), b_w_o=(b_w_o, m_b_w_o, v_b_w_o),
                 f_norm_g=(f_norm_g, m_f_norm_g, v_f_norm_g), f_w_in=(f_w_in, m_f_w_in, v_f_w_in),
                 f_conv_w=(f_conv_w, m_f_conv_w, v_f_conv_w), f_conv_b=(f_conv_b, m_f_conv_b, v_f_conv_b),
                 f_w_down=(f_w_down, m_f_w_down, v_f_w_down), final_norm_g=(final_norm_g, m_final_norm_g, v_final_norm_g))
    order = list(given)
    grads, deltas, new_m, new_v = {}, {}, {}, {}
    for n in names:
        w_, m_, v_ = given[n]
        g_ = g_big[n]
        C = w_.shape[-1]
        d2, m2, v2 = _adamw(w_.reshape(-1, C), g_.reshape(-1, C), m_.reshape(-1, C), v_.reshape(-1, C),
                            name=f"adamw_{n}")
        grads[n], deltas[n], new_m[n], new_v[n] = g_.reshape(w_.shape), d2.reshape(w_.shape), m2.reshape(w_.shape), \
            v2.reshape(w_.shape)
    vecs, lands = _split_copies("small_wait", [small_flight[0][2]], [small_flight[0][3]], 7, _gather8_copies,
                                flight=small_flight, after=deltas[names[-1]])
    red = _sum8(lands[0], vecs[0], (4 * xi + 2 * yi + ci).astype(jnp.int32))
    parts = _unpack(red, [small[n].shape for n in snames] + [(1,)])
    g_small = dict(zip(snames, parts[:-1]))
    loss = parts[-1][0]
    g_small["a_norm_g"] = lax.dynamic_slice_in_dim(g_small["a_norm_g"], j_me * nsd, nsd, axis=1)
    g_small["a_v_norm_g"] = lax.dynamic_slice_in_dim(g_small["a_v_norm_g"], j_me * nsg, nsg, axis=1)
    g_small["f_conv_w"] = lax.dynamic_slice_in_dim(g_small["f_conv_w"], j_me * nsf, nsf, axis=2)

    sm = [n for n in order if n not in names]
    d2, m2, v2 = _adamw(_pack([given[n][0] for n in sm]), _pack([g_small[n].reshape(given[n][0].shape) for n in sm]),
                        _pack([given[n][1] for n in sm]), _pack([given[n][2] for n in sm]), name="adamw_small")
    shapes = [given[n][0].shape for n in sm]
    for n, d_, m_, v_ in zip(sm, _unpack(d2, shapes), _unpack(m2, shapes), _unpack(v2, shapes)):
        grads[n], deltas[n], new_m[n], new_v[n] = g_small[n].reshape(given[n][0].shape), d_, m_, v_

    return (loss, grad_x.reshape(B, S, D), *[grads[n] for n in order], *[deltas[n] for n in order],
            *[new_m[n] for n in order], *[new_v[n] for n in order])
```
